```python
import math
import jax, jax.numpy as jnp
from jax import lax
import numpy as np

D_MODEL = 1024
BATCH = 8
SEQ = 4096
DEPTH = 2

BRANCH_WIDTH = 512
N_BRANCH = 3
S5_GROUP = 16
S5_GROUPS = BRANCH_WIDTH // S5_GROUP
S5_STATE = 64
S5_DT_MIN = 1e-3
S5_DT_MAX = 1e-1
S5_EIG_MAX = -1e-4
HG_HEADS = 4
HG_DK = 128
HG_DV = BRANCH_WIDTH // HG_HEADS
HG_KEY_WIDTH = HG_HEADS * HG_DK
HG_CHUNK = 64
RG_BLOCKS = 8
RG_BLOCK = BRANCH_WIDTH // RG_BLOCKS
RG_C = 8.0
CONV_WIDTH = 4
D_FF = 2816
EPS = 1e-6
IN_SPLIT_SIZES = (BRANCH_WIDTH, HG_KEY_WIDTH, HG_KEY_WIDTH, BRANCH_WIDTH, BRANCH_WIDTH, BRANCH_WIDTH, BRANCH_WIDTH)
IN_TOTAL = BRANCH_WIDTH + 2 * HG_KEY_WIDTH + 2 * BRANCH_WIDTH + 2 * BRANCH_WIDTH + N_BRANCH * D_MODEL

kernel_name = 'hybrid_s5_hgrn2_rglru_macaron'


def rms_norm(x, w):
    xf = x.astype(jnp.float32)
    y = xf * lax.rsqrt(jnp.mean(xf * xf, axis=-1, keepdims=True) + EPS)
    return (y * w.astype(jnp.float32)).astype(x.dtype)


def swiglu(h, w_gate, w_up, w_down):
    return (jax.nn.silu(h @ w_gate) * (h @ w_up)) @ w_down


def _complex_affine_combine(e1, e2):
    a1r, a1i, b1r, b1i = e1
    a2r, a2i, b2r, b2i = e2
    return (a2r * a1r - a2i * a1i,
            a2r * a1i + a2i * a1r,
            a2r * b1r - a2i * b1i + b2r,
            a2r * b1i + a2i * b1r + b2i)


def _real_affine_combine(e1, e2):
    a1, b1 = e1
    a2, b2 = e2
    return (a2 * a1, a2 * b1 + b2)


def s5_mixer(u, lam_re, lam_im, log_dt, b_re, b_im, c_re, c_im, d_skip, glu_w, glu_b):
    f32 = jnp.float32
    bsz, seq, _ = u.shape
    uf = u.astype(f32)
    ug = uf.reshape(bsz, seq, S5_GROUPS, S5_GROUP)
    lr = jnp.minimum(lam_re.astype(f32), S5_EIG_MAX)
    li = lam_im.astype(f32)
    dt = jnp.exp(log_dt.astype(f32))[:, None]
    mag = jnp.exp(lr * dt)
    ar = mag * jnp.cos(li * dt)
    ai = mag * jnp.sin(li * dt)
    den = lr * lr + li * li
    fr = ((ar - 1.0) * lr + ai * li) / den
    fi = (ai * lr - (ar - 1.0) * li) / den
    br, bi = b_re.astype(f32), b_im.astype(f32)
    bbr = fr[..., None] * br - fi[..., None] * bi
    bbi = fr[..., None] * bi + fi[..., None] * br
    bu_r = jnp.einsum('blgc,gpc->blgp', ug, bbr)
    bu_i = jnp.einsum('blgc,gpc->blgp', ug, bbi)
    a_r = jnp.broadcast_to(ar, bu_r.shape)
    a_i = jnp.broadcast_to(ai, bu_i.shape)
    _, _, xr, xi = lax.associative_scan(_complex_affine_combine, (a_r, a_i, bu_r, bu_i), axis=1)
    y = (jnp.einsum('blgp,gcp->blgc', xr, c_re.astype(f32))
         - jnp.einsum('blgp,gcp->blgc', xi, c_im.astype(f32)))
    y = y.reshape(bsz, seq, BRANCH_WIDTH) + d_skip.astype(f32) * uf
    z = jax.nn.gelu(y)
    out = z * jax.nn.sigmoid(z @ glu_w.astype(f32) + glu_b.astype(f32))
    return out.astype(u.dtype)


def hgrn2_mixer(q, z_f, v, g, lb, norm_w):
    f32 = jnp.float32
    bsz, seq, _ = q.shape
    n_chunks = seq // HG_CHUNK
    lb = lb.astype(f32).reshape(HG_HEADS, HG_DK)
    qh = jax.nn.silu(q.astype(f32)).reshape(bsz, seq, HG_HEADS, HG_DK)
    zf = z_f.astype(f32).reshape(bsz, seq, HG_HEADS, HG_DK)
    log_f = jnp.log(lb + (1.0 - lb) * jax.nn.sigmoid(zf))
    kh = (1.0 - lb) * jax.nn.sigmoid(-zf)
    vh = v.astype(f32).reshape(bsz, seq, HG_HEADS, HG_DV)

    def to_chunks(t):
        return t.reshape(bsz, n_chunks, HG_CHUNK, HG_HEADS, t.shape[-1]).transpose(1, 0, 3, 2, 4)

    causal = jnp.tril(jnp.ones((HG_CHUNK, HG_CHUNK), dtype=bool))[:, :, None]

    def chunk_step(state, inp):
        qc, kc, vc, lfc = inp
        b = jnp.cumsum(lfc, axis=2)
        o_inter = jnp.einsum('bhcd,bhde->bhce', qc * jnp.exp(b), state)
        diff = b[:, :, :, None, :] - b[:, :, None, :, :]
        decay = jnp.where(causal, jnp.exp(jnp.where(causal, diff, 0.0)), 0.0)
        scores = jnp.einsum('bhtd,bhtsd,bhsd->bhts', qc, decay, kc)
        o_intra = jnp.einsum('bhts,bhse->bhte', scores, vc)
        b_last = b[:, :, -1:, :]
        new_state = (jnp.exp(b_last[:, :, 0, :])[..., None] * state
                     + jnp.einsum('bhsd,bhse->bhde', kc * jnp.exp(b_last - b), vc))
        return new_state, o_inter + o_intra

    s0 = jnp.zeros((bsz, HG_HEADS, HG_DK, HG_DV), f32)
    _, o = lax.scan(chunk_step, s0, (to_chunks(qh), to_chunks(kh), to_chunks(vh), to_chunks(log_f)))
    o = o.transpose(1, 0, 3, 2, 4).reshape(bsz, seq, HG_HEADS, HG_DV)
    o = o * lax.rsqrt(jnp.mean(o * o, axis=-1, keepdims=True) + EPS)
    o = o * norm_w.astype(f32).reshape(HG_HEADS, HG_DV)
    out = o.reshape(bsz, seq, BRANCH_WIDTH) * jax.nn.silu(g.astype(f32))
    return out.astype(q.dtype)


def rglru_mixer(xb, gate, conv_w, conv_b, wa, ba, wx, bx, lam):
    f32 = jnp.float32
    bsz, seq, _ = xb.shape
    xc = lax.conv_general_dilated(
        xb, conv_w[:, None, :], window_strides=(1,), padding=[(CONV_WIDTH - 1, 0)],
        dimension_numbers=('NWC', 'WIO', 'NWC'), feature_group_count=BRANCH_WIDTH) + conv_b
    xcf = xc.astype(f32)
    xblk = xcf.reshape(bsz, seq, RG_BLOCKS, RG_BLOCK)
    r = jax.nn.sigmoid(jnp.einsum('blhi,hij->blhj', xblk, wa.astype(f32)).reshape(bsz, seq, BRANCH_WIDTH) + ba.astype(f32))
    i = jax.nn.sigmoid(jnp.einsum('blhi,hij->blhj', xblk, wx.astype(f32)).reshape(bsz, seq, BRANCH_WIDTH) + bx.astype(f32))
    log_a = -RG_C * jax.nn.softplus(-lam.astype(f32)) * r
    a = jnp.exp(log_a)
    b = jnp.sqrt(-jnp.expm1(2.0 * log_a)) * (i * xcf)
    _, hseq = lax.associative_scan(_real_affine_combine, (a, b), axis=1)
    return (hseq * jax.nn.gelu(gate.astype(f32))).astype(xb.dtype)


def hybrid_mixer(h, w_in, branch_proj, w_out,
                 s5_lambda_re, s5_lambda_im, s5_log_dt, s5_b_re, s5_b_im, s5_c_re, s5_c_im,
                 s5_d, s5_glu_w, s5_glu_b, hg_lb, hg_norm_w,
                 rg_conv_w, rg_conv_b, rg_wa, rg_ba, rg_wx, rg_bx, rg_lambda):
    bsz, seq, _ = h.shape
    proj = h @ w_in
    points, acc = [], 0
    for s in IN_SPLIT_SIZES:
        acc += s
        points.append(acc)
    u_a, q_b, f_b, v_b, g_b, x_c, gate_c, gate_merge = jnp.split(proj, points, axis=-1)
    y_a = s5_mixer(u_a, s5_lambda_re, s5_lambda_im, s5_log_dt, s5_b_re, s5_b_im,
                   s5_c_re, s5_c_im, s5_d, s5_glu_w, s5_glu_b)
    y_b = hgrn2_mixer(q_b, f_b, v_b, g_b, hg_lb, hg_norm_w)
    y_c = rglru_mixer(x_c, gate_c, rg_conv_w, rg_conv_b, rg_wa, rg_ba, rg_wx, rg_bx, rg_lambda)
    branches = jnp.stack([y_a, y_b, y_c], axis=2)
    up = jnp.einsum('blnw,nwd->blnd', branches, branch_proj)
    gates = jax.nn.sigmoid(gate_merge.astype(jnp.float32)).reshape(bsz, seq, N_BRANCH, D_MODEL)
    merged = jnp.sum(gates * up.astype(jnp.float32), axis=2).astype(h.dtype)
    return merged @ w_out


def _fwd_setup_inputs(seed: int = 0) -> dict:
    key = jax.random.key(seed)
    ks = jax.random.split(key, 32)
    f32 = jnp.float32

    def nrm(k, shape, scale):
        return jax.random.normal(k, shape, f32) * scale

    x = nrm(ks[0], (BATCH, SEQ, D_MODEL), 1.0)
    norm_w = 1.0 + nrm(ks[1], (DEPTH, 3, D_MODEL), 0.02)
    final_norm_w = 1.0 + nrm(ks[2], (D_MODEL,), 0.02)
    ffn_gate = nrm(ks[3], (DEPTH, 2, D_MODEL, D_FF), D_MODEL ** -0.5)
    ffn_up = nrm(ks[4], (DEPTH, 2, D_MODEL, D_FF), D_MODEL ** -0.5)
    ffn_down = nrm(ks[5], (DEPTH, 2, D_FF, D_MODEL), D_FF ** -0.5)
    w_in = nrm(ks[6], (DEPTH, D_MODEL, IN_TOTAL), D_MODEL ** -0.5)
    branch_proj = nrm(ks[7], (DEPTH, N_BRANCH, BRANCH_WIDTH, D_MODEL), BRANCH_WIDTH ** -0.5)
    w_out = nrm(ks[8], (DEPTH, D_MODEL, D_MODEL), D_MODEL ** -0.5)
    s5_lambda_re = -0.5 + nrm(ks[9], (DEPTH, S5_GROUPS, S5_STATE), 0.01)
    s5_lambda_im = (math.pi * jnp.arange(S5_STATE, dtype=f32)) + nrm(ks[10], (DEPTH, S5_GROUPS, S5_STATE), 0.01)
    s5_log_dt = jax.random.uniform(ks[11], (DEPTH, S5_GROUPS), f32, math.log(S5_DT_MIN), math.log(S5_DT_MAX))
    s5_b_re = nrm(ks[12], (DEPTH, S5_GROUPS, S5_STATE, S5_GROUP), (2.0 * S5_GROUP) ** -0.5)
    s5_b_im = nrm(ks[13], (DEPTH, S5_GROUPS, S5_STATE, S5_GROUP), (2.0 * S5_GROUP) ** -0.5)
    s5_c_re = nrm(ks[14], (DEPTH, S5_GROUPS, S5_GROUP, S5_STATE), S5_STATE ** -0.5)
    s5_c_im = nrm(ks[15], (DEPTH, S5_GROUPS, S5_GROUP, S5_STATE), S5_STATE ** -0.5)
    s5_d = nrm(ks[16], (DEPTH, BRANCH_WIDTH), 1.0)
    s5_glu_w = nrm(ks[17], (DEPTH, BRANCH_WIDTH, BRANCH_WIDTH), BRANCH_WIDTH ** -0.5)
    s5_glu_b = nrm(ks[18], (DEPTH, BRANCH_WIDTH), 0.01)
    hg_lb_logits = 1.0 + nrm(ks[19], (DEPTH, HG_KEY_WIDTH), 0.1)
    hg_norm_w = 1.0 + nrm(ks[20], (DEPTH, BRANCH_WIDTH), 0.02)
    rg_conv_w = nrm(ks[21], (DEPTH, CONV_WIDTH, BRANCH_WIDTH), CONV_WIDTH ** -0.5)
    rg_conv_b = nrm(ks[22], (DEPTH, BRANCH_WIDTH), 0.01)
    rg_wa = nrm(ks[23], (DEPTH, RG_BLOCKS, RG_BLOCK, RG_BLOCK), RG_BLOCK ** -0.5)
    rg_ba = nrm(ks[24], (DEPTH, BRANCH_WIDTH), 0.01)
    rg_wx = nrm(ks[25], (DEPTH, RG_BLOCKS, RG_BLOCK, RG_BLOCK), RG_BLOCK ** -0.5)
    rg_bx = nrm(ks[26], (DEPTH, BRANCH_WIDTH), 0.01)
    a_c = jax.random.uniform(ks[27], (DEPTH, BRANCH_WIDTH), f32, 0.9, 0.999)
    s = a_c ** (1.0 / RG_C)
    rg_lambda = jnp.log(s) - jnp.log1p(-s)
    return {'x': x, 'norm_w': norm_w, 'final_norm_w': final_norm_w,
            'ffn_gate': ffn_gate, 'ffn_up': ffn_up, 'ffn_down': ffn_down,
            'w_in': w_in, 'branch_proj': branch_proj, 'w_out': w_out,
            's5_lambda_re': s5_lambda_re, 's5_lambda_im': s5_lambda_im, 's5_log_dt': s5_log_dt,
            's5_b_re': s5_b_re, 's5_b_im': s5_b_im, 's5_c_re': s5_c_re, 's5_c_im': s5_c_im,
            's5_d': s5_d, 's5_glu_w': s5_glu_w, 's5_glu_b': s5_glu_b,
            'hg_lb_logits': hg_lb_logits, 'hg_norm_w': hg_norm_w,
            'rg_conv_w': rg_conv_w, 'rg_conv_b': rg_conv_b, 'rg_wa': rg_wa, 'rg_ba': rg_ba,
            'rg_wx': rg_wx, 'rg_bx': rg_bx, 'rg_lambda': rg_lambda}


def _fwd_reference(x, norm_w, final_norm_w, ffn_gate, ffn_up, ffn_down, w_in, branch_proj, w_out,
              s5_lambda_re, s5_lambda_im, s5_log_dt, s5_b_re, s5_b_im, s5_c_re, s5_c_im,
              s5_d, s5_glu_w, s5_glu_b, hg_lb_logits, hg_norm_w,
              rg_conv_w, rg_conv_b, rg_wa, rg_ba, rg_wx, rg_bx, rg_lambda):
    p = jax.nn.softmax(hg_lb_logits.astype(jnp.float32), axis=0)
    lower_bounds = jnp.cumsum(p, axis=0) - p[0]
    for l in range(DEPTH):
        h = rms_norm(x, norm_w[l, 0])
        x = x + 0.5 * swiglu(h, ffn_gate[l, 0], ffn_up[l, 0], ffn_down[l, 0])
        h = rms_norm(x, norm_w[l, 1])
        x = x + hybrid_mixer(h, w_in[l], branch_proj[l], w_out[l],
                             s5_lambda_re[l], s5_lambda_im[l], s5_log_dt[l], s5_b_re[l], s5_b_im[l],
                             s5_c_re[l], s5_c_im[l], s5_d[l], s5_glu_w[l], s5_glu_b[l],
                             lower_bounds[l], hg_norm_w[l],
                             rg_conv_w[l], rg_conv_b[l], rg_wa[l], rg_ba[l], rg_wx[l], rg_bx[l], rg_lambda[l])
        h = rms_norm(x, norm_w[l, 2])
        x = x + 0.5 * swiglu(h, ffn_gate[l, 1], ffn_up[l, 1], ffn_down[l, 1])
    return rms_norm(x, final_norm_w)


import jax as _jax
import jax.numpy as _jnp

TWIN_FORMAT = 'train_step'
FWD_PARAMS = ['x', 'norm_w', 'final_norm_w', 'ffn_gate', 'ffn_up', 'ffn_down', 'w_in', 'branch_proj', 'w_out', 's5_lambda_re', 's5_lambda_im', 's5_log_dt', 's5_b_re', 's5_b_im', 's5_c_re', 's5_c_im', 's5_d', 's5_glu_w', 's5_glu_b', 'hg_lb_logits', 'hg_norm_w', 'rg_conv_w', 'rg_conv_b', 'rg_wa', 'rg_ba', 'rg_wx', 'rg_bx', 'rg_lambda']
TWIN_WEIGHTS = ['norm_w', 'final_norm_w', 'ffn_gate', 'ffn_up', 'ffn_down', 'w_in', 'branch_proj', 'w_out', 's5_lambda_re', 's5_lambda_im', 's5_log_dt', 's5_b_re', 's5_b_im', 's5_c_re', 's5_c_im', 's5_d', 's5_glu_w', 's5_glu_b', 'hg_lb_logits', 'hg_norm_w', 'rg_conv_w', 'rg_conv_b', 'rg_wa', 'rg_ba', 'rg_wx', 'rg_bx', 'rg_lambda']
TWIN_DIFF_INPUT = 'x'
TWIN_INPUTS = ['x', 'norm_w', 'final_norm_w', 'ffn_gate', 'ffn_up', 'ffn_down', 'w_in', 'branch_proj', 'w_out', 's5_lambda_re', 's5_lambda_im', 's5_log_dt', 's5_b_re', 's5_b_im', 's5_c_re', 's5_c_im', 's5_d', 's5_glu_w', 's5_glu_b', 'hg_lb_logits', 'hg_norm_w', 'rg_conv_w', 'rg_conv_b', 'rg_wa', 'rg_ba', 'rg_wx', 'rg_bx', 'rg_lambda', 'loss_target', 'm_norm_w', 'm_final_norm_w', 'm_ffn_gate', 'm_ffn_up', 'm_ffn_down', 'm_w_in', 'm_branch_proj', 'm_w_out', 'm_s5_lambda_re', 'm_s5_lambda_im', 'm_s5_log_dt', 'm_s5_b_re', 'm_s5_b_im', 'm_s5_c_re', 'm_s5_c_im', 'm_s5_d', 'm_s5_glu_w', 'm_s5_glu_b', 'm_hg_lb_logits', 'm_hg_norm_w', 'm_rg_conv_w', 'm_rg_conv_b', 'm_rg_wa', 'm_rg_ba', 'm_rg_wx', 'm_rg_bx', 'm_rg_lambda', 'v_norm_w', 'v_final_norm_w', 'v_ffn_gate', 'v_ffn_up', 'v_ffn_down', 'v_w_in', 'v_branch_proj', 'v_w_out', 'v_s5_lambda_re', 'v_s5_lambda_im', 'v_s5_log_dt', 'v_s5_b_re', 'v_s5_b_im', 'v_s5_c_re', 'v_s5_c_im', 'v_s5_d', 'v_s5_glu_w', 'v_s5_glu_b', 'v_hg_lb_logits', 'v_hg_norm_w', 'v_rg_conv_w', 'v_rg_conv_b', 'v_rg_wa', 'v_rg_ba', 'v_rg_wx', 'v_rg_bx', 'v_rg_lambda']
TWIN_OUTPUTS = ['loss', 'grad_x', 'grad_norm_w', 'grad_final_norm_w', 'grad_ffn_gate', 'grad_ffn_up', 'grad_ffn_down', 'grad_w_in', 'grad_branch_proj', 'grad_w_out', 'grad_s5_lambda_re', 'grad_s5_lambda_im', 'grad_s5_log_dt', 'grad_s5_b_re', 'grad_s5_b_im', 'grad_s5_c_re', 'grad_s5_c_im', 'grad_s5_d', 'grad_s5_glu_w', 'grad_s5_glu_b', 'grad_hg_lb_logits', 'grad_hg_norm_w', 'grad_rg_conv_w', 'grad_rg_conv_b', 'grad_rg_wa', 'grad_rg_ba', 'grad_rg_wx', 'grad_rg_bx', 'grad_rg_lambda', 'delta_norm_w', 'delta_final_norm_w', 'delta_ffn_gate', 'delta_ffn_up', 'delta_ffn_down', 'delta_w_in', 'delta_branch_proj', 'delta_w_out', 'delta_s5_lambda_re', 'delta_s5_lambda_im', 'delta_s5_log_dt', 'delta_s5_b_re', 'delta_s5_b_im', 'delta_s5_c_re', 'delta_s5_c_im', 'delta_s5_d', 'delta_s5_glu_w', 'delta_s5_glu_b', 'delta_hg_lb_logits', 'delta_hg_norm_w', 'delta_rg_conv_w', 'delta_rg_conv_b', 'delta_rg_wa', 'delta_rg_ba', 'delta_rg_wx', 'delta_rg_bx', 'delta_rg_lambda', 'new_m_norm_w', 'new_m_final_norm_w', 'new_m_ffn_gate', 'new_m_ffn_up', 'new_m_ffn_down', 'new_m_w_in', 'new_m_branch_proj', 'new_m_w_out', 'new_m_s5_lambda_re', 'new_m_s5_lambda_im', 'new_m_s5_log_dt', 'new_m_s5_b_re', 'new_m_s5_b_im', 'new_m_s5_c_re', 'new_m_s5_c_im', 'new_m_s5_d', 'new_m_s5_glu_w', 'new_m_s5_glu_b', 'new_m_hg_lb_logits', 'new_m_hg_norm_w', 'new_m_rg_conv_w', 'new_m_rg_conv_b', 'new_m_rg_wa', 'new_m_rg_ba', 'new_m_rg_wx', 'new_m_rg_bx', 'new_m_rg_lambda', 'new_v_norm_w', 'new_v_final_norm_w', 'new_v_ffn_gate', 'new_v_ffn_up', 'new_v_ffn_down', 'new_v_w_in', 'new_v_branch_proj', 'new_v_w_out', 'new_v_s5_lambda_re', 'new_v_s5_lambda_im', 'new_v_s5_log_dt', 'new_v_s5_b_re', 'new_v_s5_b_im', 'new_v_s5_c_re', 'new_v_s5_c_im', 'new_v_s5_d', 'new_v_s5_glu_w', 'new_v_s5_glu_b', 'new_v_hg_lb_logits', 'new_v_hg_norm_w', 'new_v_rg_conv_w', 'new_v_rg_conv_b', 'new_v_rg_wa', 'new_v_rg_ba', 'new_v_rg_wx', 'new_v_rg_bx', 'new_v_rg_lambda']
TWIN_LEAF_KINDS = {'loss': 'loss', 'grad_x': 'grad_x', 'grad_norm_w': 'grad_w', 'grad_final_norm_w': 'grad_w', 'grad_ffn_gate': 'grad_w', 'grad_ffn_up': 'grad_w', 'grad_ffn_down': 'grad_w', 'grad_w_in': 'grad_w', 'grad_branch_proj': 'grad_w', 'grad_w_out': 'grad_w', 'grad_s5_lambda_re': 'grad_w', 'grad_s5_lambda_im': 'grad_w', 'grad_s5_log_dt': 'grad_w', 'grad_s5_b_re': 'grad_w', 'grad_s5_b_im': 'grad_w', 'grad_s5_c_re': 'grad_w', 'grad_s5_c_im': 'grad_w', 'grad_s5_d': 'grad_w', 'grad_s5_glu_w': 'grad_w', 'grad_s5_glu_b': 'grad_w', 'grad_hg_lb_logits': 'grad_w', 'grad_hg_norm_w': 'grad_w', 'grad_rg_conv_w': 'grad_w', 'grad_rg_conv_b': 'grad_w', 'grad_rg_wa': 'grad_w', 'grad_rg_ba': 'grad_w', 'grad_rg_wx': 'grad_w', 'grad_rg_bx': 'grad_w', 'grad_rg_lambda': 'grad_w', 'delta_norm_w': 'delta_w', 'delta_final_norm_w': 'delta_w', 'delta_ffn_gate': 'delta_w', 'delta_ffn_up': 'delta_w', 'delta_ffn_down': 'delta_w', 'delta_w_in': 'delta_w', 'delta_branch_proj': 'delta_w', 'delta_w_out': 'delta_w', 'delta_s5_lambda_re': 'delta_w', 'delta_s5_lambda_im': 'delta_w', 'delta_s5_log_dt': 'delta_w', 'delta_s5_b_re': 'delta_w', 'delta_s5_b_im': 'delta_w', 'delta_s5_c_re': 'delta_w', 'delta_s5_c_im': 'delta_w', 'delta_s5_d': 'delta_w', 'delta_s5_glu_w': 'delta_w', 'delta_s5_glu_b': 'delta_w', 'delta_hg_lb_logits': 'delta_w', 'delta_hg_norm_w': 'delta_w', 'delta_rg_conv_w': 'delta_w', 'delta_rg_conv_b': 'delta_w', 'delta_rg_wa': 'delta_w', 'delta_rg_ba': 'delta_w', 'delta_rg_wx': 'delta_w', 'delta_rg_bx': 'delta_w', 'delta_rg_lambda': 'delta_w', 'new_m_norm_w': 'new_m', 'new_m_final_norm_w': 'new_m', 'new_m_ffn_gate': 'new_m', 'new_m_ffn_up': 'new_m', 'new_m_ffn_down': 'new_m', 'new_m_w_in': 'new_m', 'new_m_branch_proj': 'new_m', 'new_m_w_out': 'new_m', 'new_m_s5_lambda_re': 'new_m', 'new_m_s5_lambda_im': 'new_m', 'new_m_s5_log_dt': 'new_m', 'new_m_s5_b_re': 'new_m', 'new_m_s5_b_im': 'new_m', 'new_m_s5_c_re': 'new_m', 'new_m_s5_c_im': 'new_m', 'new_m_s5_d': 'new_m', 'new_m_s5_glu_w': 'new_m', 'new_m_s5_glu_b': 'new_m', 'new_m_hg_lb_logits': 'new_m', 'new_m_hg_norm_w': 'new_m', 'new_m_rg_conv_w': 'new_m', 'new_m_rg_conv_b': 'new_m', 'new_m_rg_wa': 'new_m', 'new_m_rg_ba': 'new_m', 'new_m_rg_wx': 'new_m', 'new_m_rg_bx': 'new_m', 'new_m_rg_lambda': 'new_m', 'new_v_norm_w': 'new_v', 'new_v_final_norm_w': 'new_v', 'new_v_ffn_gate': 'new_v', 'new_v_ffn_up': 'new_v', 'new_v_ffn_down': 'new_v', 'new_v_w_in': 'new_v', 'new_v_branch_proj': 'new_v', 'new_v_w_out': 'new_v', 'new_v_s5_lambda_re': 'new_v', 'new_v_s5_lambda_im': 'new_v', 'new_v_s5_log_dt': 'new_v', 'new_v_s5_b_re': 'new_v', 'new_v_s5_b_im': 'new_v', 'new_v_s5_c_re': 'new_v', 'new_v_s5_c_im': 'new_v', 'new_v_s5_d': 'new_v', 'new_v_s5_glu_w': 'new_v', 'new_v_s5_glu_b': 'new_v', 'new_v_hg_lb_logits': 'new_v', 'new_v_hg_norm_w': 'new_v', 'new_v_rg_conv_w': 'new_v', 'new_v_rg_conv_b': 'new_v', 'new_v_rg_wa': 'new_v', 'new_v_rg_ba': 'new_v', 'new_v_rg_wx': 'new_v', 'new_v_rg_bx': 'new_v', 'new_v_rg_lambda': 'new_v'}


def _forward(args):
    return _fwd_reference(*[args[k] for k in FWD_PARAMS])


def _output_shape():
    def fwd():
        inp = _fwd_setup_inputs(0)
        return _fwd_reference(*[inp[k] for k in FWD_PARAMS])
    out = _jax.eval_shape(fwd)
    return out.shape, out.dtype

N_MICROBATCH = 1
ADAM_LR = 0.001
ADAM_B1 = 0.9
ADAM_B2 = 0.999
ADAM_EPS = 1e-08
ADAM_WD = 0.01
ADAM_STEP = 10
PER_EXAMPLE_BATCH_AXIS = {'x': 0, 'loss_target': 0}
SHARED_INPUTS = []
_WEIGHT_DTYPES = {'norm_w': _jnp.float32, 'final_norm_w': _jnp.float32, 'ffn_gate': _jnp.float32, 'ffn_up': _jnp.float32, 'ffn_down': _jnp.float32, 'w_in': _jnp.float32, 'branch_proj': _jnp.float32, 'w_out': _jnp.float32, 's5_lambda_re': _jnp.float32, 's5_lambda_im': _jnp.float32, 's5_log_dt': _jnp.float32, 's5_b_re': _jnp.float32, 's5_b_im': _jnp.float32, 's5_c_re': _jnp.float32, 's5_c_im': _jnp.float32, 's5_d': _jnp.float32, 's5_glu_w': _jnp.float32, 's5_glu_b': _jnp.float32, 'hg_lb_logits': _jnp.float32, 'hg_norm_w': _jnp.float32, 'rg_conv_w': _jnp.float32, 'rg_conv_b': _jnp.float32, 'rg_wa': _jnp.float32, 'rg_ba': _jnp.float32, 'rg_wx': _jnp.float32, 'rg_bx': _jnp.float32, 'rg_lambda': _jnp.float32}
MOMENT_SCALE = {'norm_w': 8.832047e-02, 'final_norm_w': 3.199016e+01, 'ffn_gate': 3.178521e-02, 'ffn_up': 3.079240e-02, 'ffn_down': 5.101199e-02, 'w_in': 4.329908e-02, 'branch_proj': 4.763497e-02, 'w_out': 8.258257e-02, 's5_lambda_re': 3.511479e-03, 's5_lambda_im': 3.544431e-03, 's5_log_dt': 2.552440e+00, 's5_b_re': 2.096447e-03, 's5_b_im': 2.103364e-03, 's5_c_re': 2.964384e-03, 's5_c_im': 3.008267e-03, 's5_d': 6.476645e-02, 's5_glu_w': 1.356921e-02, 's5_glu_b': 2.504627e-02, 'hg_lb_logits': 5.909353e-03, 'hg_norm_w': 7.522063e-02, 'rg_conv_w': 6.908784e-02, 'rg_conv_b': 7.514595e-01, 'rg_wa': 2.866623e-02, 'rg_ba': 2.392350e-02, 'rg_wx': 5.279752e-02, 'rg_bx': 2.975932e-02, 'rg_lambda': 4.306764e-02}


def _to_microbatches(a, axis):
    t = _jnp.moveaxis(a, axis, 0)
    t = t.reshape((N_MICROBATCH, t.shape[0] // N_MICROBATCH) + t.shape[1:])
    return _jnp.moveaxis(t, 1, axis + 1)


def setup_inputs(seed: int = 0) -> dict:
    inp = _fwd_setup_inputs(seed)
    key = _jax.random.fold_in(_jax.random.key(seed), 7919)
    shape, _ = _output_shape()
    out = dict(inp)
    out["loss_target"] = _jax.random.normal(_jax.random.fold_in(key, 0), shape, _jnp.float32)
    for i, name in enumerate(TWIN_WEIGHTS):
        w = inp[name].astype(_jnp.float32)
        if MOMENT_SCALE is None:
            s = _jnp.sqrt(_jnp.mean(_jnp.square(w)) + 1e-30)
        else:
            s = MOMENT_SCALE[name]
        km, kv = _jax.random.split(_jax.random.fold_in(key, i + 1))
        out[name] = w
        out["m_" + name] = s * _jax.random.normal(km, w.shape, _jnp.float32)
        out["v_" + name] = (s * s) * _jax.random.uniform(kv, w.shape, _jnp.float32, 0.5, 1.5)
    if N_MICROBATCH > 1:
        for name, axis in PER_EXAMPLE_BATCH_AXIS.items():
            out[name] = _to_microbatches(out[name], axis)
    return {'x': out['x'], 'norm_w': out['norm_w'], 'final_norm_w': out['final_norm_w'], 'ffn_gate': out['ffn_gate'], 'ffn_up': out['ffn_up'], 'ffn_down': out['ffn_down'], 'w_in': out['w_in'], 'branch_proj': out['branch_proj'], 'w_out': out['w_out'], 's5_lambda_re': out['s5_lambda_re'], 's5_lambda_im': out['s5_lambda_im'], 's5_log_dt': out['s5_log_dt'], 's5_b_re': out['s5_b_re'], 's5_b_im': out['s5_b_im'], 's5_c_re': out['s5_c_re'], 's5_c_im': out['s5_c_im'], 's5_d': out['s5_d'], 's5_glu_w': out['s5_glu_w'], 's5_glu_b': out['s5_glu_b'], 'hg_lb_logits': out['hg_lb_logits'], 'hg_norm_w': out['hg_norm_w'], 'rg_conv_w': out['rg_conv_w'], 'rg_conv_b': out['rg_conv_b'], 'rg_wa': out['rg_wa'], 'rg_ba': out['rg_ba'], 'rg_wx': out['rg_wx'], 'rg_bx': out['rg_bx'], 'rg_lambda': out['rg_lambda'], 'loss_target': out['loss_target'], 'm_norm_w': out['m_norm_w'], 'm_final_norm_w': out['m_final_norm_w'], 'm_ffn_gate': out['m_ffn_gate'], 'm_ffn_up': out['m_ffn_up'], 'm_ffn_down': out['m_ffn_down'], 'm_w_in': out['m_w_in'], 'm_branch_proj': out['m_branch_proj'], 'm_w_out': out['m_w_out'], 'm_s5_lambda_re': out['m_s5_lambda_re'], 'm_s5_lambda_im': out['m_s5_lambda_im'], 'm_s5_log_dt': out['m_s5_log_dt'], 'm_s5_b_re': out['m_s5_b_re'], 'm_s5_b_im': out['m_s5_b_im'], 'm_s5_c_re': out['m_s5_c_re'], 'm_s5_c_im': out['m_s5_c_im'], 'm_s5_d': out['m_s5_d'], 'm_s5_glu_w': out['m_s5_glu_w'], 'm_s5_glu_b': out['m_s5_glu_b'], 'm_hg_lb_logits': out['m_hg_lb_logits'], 'm_hg_norm_w': out['m_hg_norm_w'], 'm_rg_conv_w': out['m_rg_conv_w'], 'm_rg_conv_b': out['m_rg_conv_b'], 'm_rg_wa': out['m_rg_wa'], 'm_rg_ba': out['m_rg_ba'], 'm_rg_wx': out['m_rg_wx'], 'm_rg_bx': out['m_rg_bx'], 'm_rg_lambda': out['m_rg_lambda'], 'v_norm_w': out['v_norm_w'], 'v_final_norm_w': out['v_final_norm_w'], 'v_ffn_gate': out['v_ffn_gate'], 'v_ffn_up': out['v_ffn_up'], 'v_ffn_down': out['v_ffn_down'], 'v_w_in': out['v_w_in'], 'v_branch_proj': out['v_branch_proj'], 'v_w_out': out['v_w_out'], 'v_s5_lambda_re': out['v_s5_lambda_re'], 'v_s5_lambda_im': out['v_s5_lambda_im'], 'v_s5_log_dt': out['v_s5_log_dt'], 'v_s5_b_re': out['v_s5_b_re'], 'v_s5_b_im': out['v_s5_b_im'], 'v_s5_c_re': out['v_s5_c_re'], 'v_s5_c_im': out['v_s5_c_im'], 'v_s5_d': out['v_s5_d'], 'v_s5_glu_w': out['v_s5_glu_w'], 'v_s5_glu_b': out['v_s5_glu_b'], 'v_hg_lb_logits': out['v_hg_lb_logits'], 'v_hg_norm_w': out['v_hg_norm_w'], 'v_rg_conv_w': out['v_rg_conv_w'], 'v_rg_conv_b': out['v_rg_conv_b'], 'v_rg_wa': out['v_rg_wa'], 'v_rg_ba': out['v_rg_ba'], 'v_rg_wx': out['v_rg_wx'], 'v_rg_bx': out['v_rg_bx'], 'v_rg_lambda': out['v_rg_lambda']}


def _loss(weights, diff, rest, loss_target):
    with _jax.named_scope("forward"):
        args = {**rest, TWIN_DIFF_INPUT: diff, **{k: w.astype(_WEIGHT_DTYPES[k]) for k, w in weights.items()}}
        y = _forward(args)
    with _jax.named_scope("loss_head"):
        err = _jnp.square(y.astype(_jnp.float32) - loss_target)
        return 0.5 * _jnp.sum(_jnp.mean(err, axis=-1)) if err.ndim else 0.5 * err


def _adamw(w, g, m, v):
    m = ADAM_B1 * m + (1.0 - ADAM_B1) * g
    v = ADAM_B2 * v + (1.0 - ADAM_B2) * _jnp.square(g)
    m_hat = m / (1.0 - ADAM_B1 ** ADAM_STEP)
    v_hat = v / (1.0 - ADAM_B2 ** ADAM_STEP)
    delta = -ADAM_LR * (m_hat / (_jnp.sqrt(v_hat) + ADAM_EPS) + ADAM_WD * w)
    return delta, m, v


def reference(x, norm_w, final_norm_w, ffn_gate, ffn_up, ffn_down, w_in, branch_proj, w_out, s5_lambda_re, s5_lambda_im, s5_log_dt, s5_b_re, s5_b_im, s5_c_re, s5_c_im, s5_d, s5_glu_w, s5_glu_b, hg_lb_logits, hg_norm_w, rg_conv_w, rg_conv_b, rg_wa, rg_ba, rg_wx, rg_bx, rg_lambda, loss_target, m_norm_w, m_final_norm_w, m_ffn_gate, m_ffn_up, m_ffn_down, m_w_in, m_branch_proj, m_w_out, m_s5_lambda_re, m_s5_lambda_im, m_s5_log_dt, m_s5_b_re, m_s5_b_im, m_s5_c_re, m_s5_c_im, m_s5_d, m_s5_glu_w, m_s5_glu_b, m_hg_lb_logits, m_hg_norm_w, m_rg_conv_w, m_rg_conv_b, m_rg_wa, m_rg_ba, m_rg_wx, m_rg_bx, m_rg_lambda, v_norm_w, v_final_norm_w, v_ffn_gate, v_ffn_up, v_ffn_down, v_w_in, v_branch_proj, v_w_out, v_s5_lambda_re, v_s5_lambda_im, v_s5_log_dt, v_s5_b_re, v_s5_b_im, v_s5_c_re, v_s5_c_im, v_s5_d, v_s5_glu_w, v_s5_glu_b, v_hg_lb_logits, v_hg_norm_w, v_rg_conv_w, v_rg_conv_b, v_rg_wa, v_rg_ba, v_rg_wx, v_rg_bx, v_rg_lambda):
    given = dict(x=x, norm_w=norm_w, final_norm_w=final_norm_w, ffn_gate=ffn_gate, ffn_up=ffn_up, ffn_down=ffn_down, w_in=w_in, branch_proj=branch_proj, w_out=w_out, s5_lambda_re=s5_lambda_re, s5_lambda_im=s5_lambda_im, s5_log_dt=s5_log_dt, s5_b_re=s5_b_re, s5_b_im=s5_b_im, s5_c_re=s5_c_re, s5_c_im=s5_c_im, s5_d=s5_d, s5_glu_w=s5_glu_w, s5_glu_b=s5_glu_b, hg_lb_logits=hg_lb_logits, hg_norm_w=hg_norm_w, rg_conv_w=rg_conv_w, rg_conv_b=rg_conv_b, rg_wa=rg_wa, rg_ba=rg_ba, rg_wx=rg_wx, rg_bx=rg_bx, rg_lambda=rg_lambda, loss_target=loss_target, m_norm_w=m_norm_w, m_final_norm_w=m_final_norm_w, m_ffn_gate=m_ffn_gate, m_ffn_up=m_ffn_up, m_ffn_down=m_ffn_down, m_w_in=m_w_in, m_branch_proj=m_branch_proj, m_w_out=m_w_out, m_s5_lambda_re=m_s5_lambda_re, m_s5_lambda_im=m_s5_lambda_im, m_s5_log_dt=m_s5_log_dt, m_s5_b_re=m_s5_b_re, m_s5_b_im=m_s5_b_im, m_s5_c_re=m_s5_c_re, m_s5_c_im=m_s5_c_im, m_s5_d=m_s5_d, m_s5_glu_w=m_s5_glu_w, m_s5_glu_b=m_s5_glu_b, m_hg_lb_logits=m_hg_lb_logits, m_hg_norm_w=m_hg_norm_w, m_rg_conv_w=m_rg_conv_w, m_rg_conv_b=m_rg_conv_b, m_rg_wa=m_rg_wa, m_rg_ba=m_rg_ba, m_rg_wx=m_rg_wx, m_rg_bx=m_rg_bx, m_rg_lambda=m_rg_lambda, v_norm_w=v_norm_w, v_final_norm_w=v_final_norm_w, v_ffn_gate=v_ffn_gate, v_ffn_up=v_ffn_up, v_ffn_down=v_ffn_down, v_w_in=v_w_in, v_branch_proj=v_branch_proj, v_w_out=v_w_out, v_s5_lambda_re=v_s5_lambda_re, v_s5_lambda_im=v_s5_lambda_im, v_s5_log_dt=v_s5_log_dt, v_s5_b_re=v_s5_b_re, v_s5_b_im=v_s5_b_im, v_s5_c_re=v_s5_c_re, v_s5_c_im=v_s5_c_im, v_s5_d=v_s5_d, v_s5_glu_w=v_s5_glu_w, v_s5_glu_b=v_s5_glu_b, v_hg_lb_logits=v_hg_lb_logits, v_hg_norm_w=v_hg_norm_w, v_rg_conv_w=v_rg_conv_w, v_rg_conv_b=v_rg_conv_b, v_rg_wa=v_rg_wa, v_rg_ba=v_rg_ba, v_rg_wx=v_rg_wx, v_rg_bx=v_rg_bx, v_rg_lambda=v_rg_lambda)
    weights = {n: given[n] for n in TWIN_WEIGHTS}
    shared = {n: given[n] for n in SHARED_INPUTS}
    per_example = {n: given[n] for n in ['x']}
    grad_fn = _jax.value_and_grad(_loss, argnums=(0, 1))

    def one_microbatch(ex, loss_target):
        ex = dict(ex)
        diff = ex.pop(TWIN_DIFF_INPUT)
        return grad_fn(weights, diff, {**shared, **ex}, loss_target)

    if N_MICROBATCH == 1:
        loss, (grad_w, grad_x) = one_microbatch(per_example, given["loss_target"])
    else:
        def body(carry, xs):
            loss_sum, grad_sum = carry
            l_k, (gw_k, gx_k) = one_microbatch(xs[0], xs[1])
            with _jax.named_scope("update"):
                return (loss_sum + l_k, _jax.tree.map(_jnp.add, grad_sum, gw_k)), gx_k

        init = (_jnp.zeros((), _jnp.float32), _jax.tree.map(_jnp.zeros_like, weights))
        (loss, grad_w), grad_x = _jax.lax.scan(body, init, (per_example, given["loss_target"]))
    with _jax.named_scope("update"):
        delta_w, new_m, new_v = {}, {}, {}
        for n in TWIN_WEIGHTS:
            delta_w[n], new_m[n], new_v[n] = _adamw(weights[n], grad_w[n], given["m_" + n], given["v_" + n])
    return (loss, grad_x, *[grad_w[n] for n in TWIN_WEIGHTS], *[delta_w[n] for n in TWIN_WEIGHTS],
            *[new_m[n] for n in TWIN_WEIGHTS], *[new_v[n] for n in TWIN_WEIGHTS])
```

```python
import functools
import math
from typing import NamedTuple

import jax
import jax.numpy as jnp
from jax import lax
from jax.experimental import pallas as pl
from jax.experimental.pallas import tpu as pltpu

F32 = jnp.float32
MMT = jnp.bfloat16
HI = lax.Precision.HIGHEST

D_MODEL = 1024
BW = 512
S5_GROUP, S5_GROUPS, S5_STATE = 16, 32, 64
S5_N = S5_GROUPS * S5_STATE
HG_HEADS, HG_D = 4, 128
HG_CHUNK = 64
RG_BLOCKS, RG_BLOCK = 8, 64
RG_C = 8.0
CONV_W = 4
D_FF = 2816
EPS = 1e-6
IN_TOTAL = 6656
NSH = 4
NSEG = 8
LANE = 128
VMEM_LIMIT = 56 * 1024 * 1024

ADAM_LR, ADAM_B1, ADAM_B2, ADAM_EPS, ADAM_WD, ADAM_STEP = 0.001, 0.9, 0.999, 1e-08, 0.01, 10

MESH = pl.DeviceIdType.MESH


class WP(NamedTuple):
    w: jax.Array
    p: jax.Array


def _dg(a, b, ca, cb):
    return lax.dot_general(a, b, (((ca,), (cb,)), ((), ())), preferred_element_type=F32)


@jax.custom_vjp
def _mmw(a, w, p):
    return _dg(a.astype(MMT), w, 1, 0)


def _mmw_fwd(a, w, p):
    return _mmw(a, w, p), (a, w)


def _mmw_bwd(res, g):
    a, w = res
    gb = g.astype(MMT)
    return _dg(gb, w, 1, 1), jnp.zeros_like(w), _dg(a.astype(MMT), gb, 0, 0)


_mmw.defvjp(_mmw_fwd, _mmw_bwd)


def mm(a, w):
    if isinstance(w, WP):
        return _mmw(a, w.w, w.p)
    return _dg(a.astype(MMT), w, 1, 0)


@jax.custom_vjp
def mma_nn(a, b):
    return _dg(a.astype(MMT), b.astype(MMT), 1, 0)


def _nn_f(a, b):
    return mma_nn(a, b), (a, b)


def _nn_b(res, g):
    a, b = res
    gb = g.astype(MMT)
    return _dg(gb, b.astype(MMT), 1, 1), _dg(a.astype(MMT), gb, 0, 0)


mma_nn.defvjp(_nn_f, _nn_b)


@jax.custom_vjp
def mma_nt(a, b):
    return _dg(a.astype(MMT), b.astype(MMT), 1, 1)


def _nt_f(a, b):
    return mma_nt(a, b), (a, b)


def _nt_b(res, g):
    a, b = res
    gb = g.astype(MMT)
    return _dg(gb, b.astype(MMT), 1, 0), _dg(gb, a.astype(MMT), 0, 0)


mma_nt.defvjp(_nt_f, _nt_b)


@jax.custom_vjp
def mma_tn(a, b):
    return _dg(a.astype(MMT), b.astype(MMT), 0, 0)


def _tn_f(a, b):
    return mma_tn(a, b), (a, b)


def _tn_b(res, g):
    a, b = res
    gb = g.astype(MMT)
    return _dg(b.astype(MMT), gb, 1, 1), _dg(a.astype(MMT), gb, 1, 0)


mma_tn.defvjp(_tn_f, _tn_b)


def mm_exact(m, x):
    return jnp.dot(m, x, precision=HI, preferred_element_type=F32)


def _rms(x, w):
    return x * lax.rsqrt(jnp.mean(x * x, axis=-1, keepdims=True) + EPS) * w


def _expm1(x):
    series = x * (1.0 + x * (1.0 / 2) * (1.0 + x * (1.0 / 3) * (1.0 + x * (1.0 / 4) * (1.0 + x * (1.0 / 5) * (1.0 + x * (1.0 / 6))))))
    return jnp.where(jnp.abs(x) < 0.1, series, jnp.exp(x) - 1.0)


def _bspec(block, fn, order):
    if order == "is":
        return pl.BlockSpec(block, lambda i, s: fn(s, i))
    return pl.BlockSpec(block, lambda s, i: fn(s, i))


def tile_fwd(fn, name, n_i, n_s, ins, outs):
    n_in = len(ins)

    def body(*refs):
        s = pl.program_id(1)
        res = fn(*[r[...] for r in refs[:n_in]], s)
        for o_ref, val, spec in zip(refs[n_in:], res, outs):
            if spec[4] and n_s > 1:
                @pl.when(s == 0)
                def _(o_ref=o_ref, val=val):
                    o_ref[...] = val.astype(o_ref.dtype)

                @pl.when(s != 0)
                def _(o_ref=o_ref, val=val):
                    o_ref[...] += val.astype(o_ref.dtype)
            else:
                o_ref[...] = val.astype(o_ref.dtype)

    return pl.pallas_call(
        body, grid=(n_i, n_s), name=name,
        in_specs=[_bspec(b, f, "is") for _, b, f in ins],
        out_specs=[_bspec(b, f, "is") for _, _, b, f, _ in outs],
        out_shape=[jax.ShapeDtypeStruct(sh, dt) for sh, dt, _, _, _ in outs],
        compiler_params=pltpu.CompilerParams(vmem_limit_bytes=VMEM_LIMIT,
                                             dimension_semantics=("arbitrary", "arbitrary")),
    )(*[a for a, _, _ in ins])


def tile_bwd(fn, name, n_i, n_s, ins, cts, gouts):
    n_in, n_ct = len(ins), len(cts)
    kinds = [k for _, _, _, k in ins]
    d_pos = [j for j, k in enumerate(kinds) if k != "c"]

    def body(*refs):
        s, i = pl.program_id(0), pl.program_id(1)
        vals = [r[...] for r in refs[:n_in]]
        ctv = tuple(r[...] for r in refs[n_in:n_in + n_ct])
        g_refs = refs[n_in + n_ct:]

        def g(*dv):
            args = list(vals)
            for j, v in zip(d_pos, dv):
                args[j] = WP(vals[j], v) if kinds[j] == "w" else v
            return tuple(fn(*args))

        dv0 = [jnp.zeros(vals[j].shape, F32) if kinds[j] == "w" else vals[j] for j in d_pos]
        _, vjp = jax.vjp(g, *dv0)
        grads = vjp(ctv)
        for g_ref, gv, spec in zip(g_refs, grads, gouts):
            mode = spec[3]
            if mode == "write":
                g_ref[...] = gv.astype(g_ref.dtype)
            else:
                first = (i == 0) if mode == "acc_i" else jnp.logical_and(i == 0, s == 0)

                @pl.when(first)
                def _(g_ref=g_ref, gv=gv):
                    g_ref[...] = gv.astype(g_ref.dtype)

                @pl.when(jnp.logical_not(first))
                def _(g_ref=g_ref, gv=gv):
                    g_ref[...] += gv.astype(g_ref.dtype)

    return pl.pallas_call(
        body, grid=(n_s, n_i), name=name,
        in_specs=[_bspec(b, f, "si") for _, b, f, _ in ins] + [_bspec(b, f, "si") for _, b, f in cts],
        out_specs=[_bspec(b, f, "si") for _, b, f, _ in gouts],
        out_shape=[jax.ShapeDtypeStruct(sh, F32) for sh, _, _, _ in gouts],
        compiler_params=pltpu.CompilerParams(vmem_limit_bytes=VMEM_LIMIT,
                                             dimension_semantics=("arbitrary", "arbitrary")),
    )(*[a for a, _, _, _ in ins], *[a for a, _, _ in cts])


def _row_tile(rows, width, itemsize=4, budget=2 * 1024 * 1024):
    best = 8
    for t in range(8, rows + 1, 8):
        if rows % t == 0 and t * width * itemsize <= budget:
            best = t
    return best


def add_n(name, terms, shape):
    rows, cols = shape
    tr = _row_tile(rows, cols)

    def body(*refs):
        acc = refs[0][...]
        for r in refs[1:-1]:
            acc = acc + r[...]
        refs[-1][...] = acc

    specs = []
    for _, lead in terms:
        specs.append(pl.BlockSpec((None,) * len(lead) + (tr, cols), functools.partial(lambda i, lead: (*lead, i, 0), lead=lead)))
    return pl.pallas_call(
        body, grid=(rows // tr,), name=name, in_specs=specs,
        out_specs=pl.BlockSpec((tr, cols), lambda i: (i, 0)),
        out_shape=jax.ShapeDtypeStruct((rows, cols), F32),
    )(*[a for a, _ in terms])


def ffn_core(x, nw, wg, wu, wd):
    h = _rms(x, nw)
    return (0.5 * mm(jax.nn.silu(mm(h, wg)) * mm(h, wu), wd),)


def pre_core(x, nw, win):
    return (mm(_rms(x, nw), win),)


def _split_lanes(y):
    return jnp.stack([y[:, k * LANE:(k + 1) * LANE] for k in range(y.shape[1] // LANE)], axis=0)


def _join_lanes(y3):
    return jnp.concatenate([y3[k] for k in range(y3.shape[0])], axis=1)


def s5_pre_core(u, bmat):
    bu = mm(u, bmat)
    return _split_lanes(bu[:, :S5_N]), _split_lanes(bu[:, S5_N:])


def mid_core(xr, xi, u, o, g, hs, gc, hmat, cmat, d, gluw, glub, hgw):
    xs = jnp.concatenate([_join_lanes(xr), _join_lanes(xi)], axis=1)
    y = mm(xs, cmat) + d * u
    z = jax.nn.gelu(y)
    ya = z * jax.nn.sigmoid(mm(z, gluw) + glub)
    ms = mm_exact(o * o, hmat)
    yb = o * lax.rsqrt(ms + EPS) * hgw * jax.nn.silu(g)
    yc = hs * jax.nn.gelu(gc)
    return ya, yb, yc


def merge_core(ya, yb, yc, g0, g1, g2, p0, p1, p2, wout):
    m = (jax.nn.sigmoid(g0) * mm(ya, p0) + jax.nn.sigmoid(g1) * mm(yb, p1) + jax.nn.sigmoid(g2) * mm(yc, p2))
    return (mm(m, wout),)


def gates_core(xc, wa, ba, wx, bx, lam):
    r = jax.nn.sigmoid(mm(xc, wa) + ba)
    i = jax.nn.sigmoid(mm(xc, wx) + bx)
    log_a = -RG_C * jax.nn.softplus(-lam) * r
    a = jnp.exp(log_a)
    b = jnp.sqrt(-_expm1(2.0 * log_a)) * (i * xc)
    return a, b


def _seg_rows(ref, k, j, n):
    if k is None:
        return ref[pl.ds(j, NSEG, stride=n), :]
    return ref[k, pl.ds(j, NSEG, stride=n), :]


def _seg_store(ref, k, j, n, val):
    if k is None:
        ref[pl.ds(j, NSEG, stride=n), :] = val
    else:
        ref[k, pl.ds(j, NSEG, stride=n), :] = val


def _seg_carries(er, ei, pr, pi, reverse):
    rows = lax.broadcasted_iota(jnp.int32, er.shape, 0)
    cr = jnp.zeros_like(er)
    ci = None if ei is None else jnp.zeros_like(er)
    order = range(NSEG - 2, -1, -1) if reverse else range(1, NSEG)
    shift = NSEG - 1 if reverse else 1
    for s in order:
        if ei is None:
            tr = er + pr * cr
            cr = jnp.where(rows == s, pltpu.roll(tr, shift, 0), cr)
        else:
            tr = er + pr * cr - pi * ci
            ti = ei + pr * ci + pi * cr
            cr = jnp.where(rows == s, pltpu.roll(tr, shift, 0), cr)
            ci = jnp.where(rows == s, pltpu.roll(ti, shift, 0), ci)
    return cr, ci


S5_K = 2


def s5_scan_fwd(bur, bui, ar, ai, L):
    n = L // NSEG
    nb = S5_N // LANE
    K = S5_K

    def body(br_ref, bi_ref, ar_ref, ai_ref, xr_ref, xi_ref):
        zero = jnp.zeros((NSEG, LANE), F32)
        A = [(jnp.broadcast_to(ar_ref[k], (NSEG, LANE)), jnp.broadcast_to(ai_ref[k], (NSEG, LANE))) for k in range(K)]

        def p1(j, st):
            new = []
            for k in range(K):
                sr, si, pr, pi = st[k]
                a_r, a_i = A[k]
                nr = a_r * sr - a_i * si + _seg_rows(br_ref, k, j, n)
                ni = a_r * si + a_i * sr + _seg_rows(bi_ref, k, j, n)
                _seg_store(xr_ref, k, j, n, nr)
                _seg_store(xi_ref, k, j, n, ni)
                new.append((nr, ni, a_r * pr - a_i * pi, a_r * pi + a_i * pr))
            return tuple(new)

        st = lax.fori_loop(0, n, p1, tuple((zero, zero, zero + 1.0, zero) for _ in range(K)))
        C = [_seg_carries(st[k][0], st[k][1], st[k][2], st[k][3], False) for k in range(K)]

        def p2(j, st):
            new = []
            for k in range(K):
                pr, pi = st[k]
                a_r, a_i = A[k]
                pr, pi = a_r * pr - a_i * pi, a_r * pi + a_i * pr
                cr, ci = C[k]
                _seg_store(xr_ref, k, j, n, _seg_rows(xr_ref, k, j, n) + pr * cr - pi * ci)
                _seg_store(xi_ref, k, j, n, _seg_rows(xi_ref, k, j, n) + pr * ci + pi * cr)
                new.append((pr, pi))
            return tuple(new)

        lax.fori_loop(0, n, p2, tuple((zero + 1.0, zero) for _ in range(K)))

    blk = pl.BlockSpec((K, L, LANE), lambda g: (g, 0, 0))
    ablk = pl.BlockSpec((K, 1, LANE), lambda g: (g, 0, 0))
    return pl.pallas_call(
        body, grid=(nb // K,), name="s5_scan_fwd",
        in_specs=[blk, blk, ablk, ablk], out_specs=[blk, blk],
        out_shape=[jax.ShapeDtypeStruct((nb, L, LANE), F32)] * 2,
        compiler_params=pltpu.CompilerParams(vmem_limit_bytes=VMEM_LIMIT),
    )(bur, bui, ar, ai)


def s5_scan_bwd(dxr, dxi, xr, xi, ar, ai, L):
    n = L // NSEG
    nb = S5_N // LANE
    K = S5_K

    def body(dr_ref, di_ref, xr_ref, xi_ref, ar_ref, ai_ref, gr_ref, gi_ref, dar_ref, dai_ref):
        zero = jnp.zeros((NSEG, LANE), F32)
        rows = lax.broadcasted_iota(jnp.int32, (NSEG, LANE), 0)
        A = [(jnp.broadcast_to(ar_ref[k], (NSEG, LANE)), -jnp.broadcast_to(ai_ref[k], (NSEG, LANE))) for k in range(K)]

        def p1(jj, st):
            j = n - 1 - jj
            new = []
            for k in range(K):
                sr, si, pr, pi = st[k]
                a_r, a_i = A[k]
                nr = a_r * sr - a_i * si + _seg_rows(dr_ref, k, j, n)
                ni = a_r * si + a_i * sr + _seg_rows(di_ref, k, j, n)
                _seg_store(gr_ref, k, j, n, nr)
                _seg_store(gi_ref, k, j, n, ni)
                new.append((nr, ni, a_r * pr - a_i * pi, a_r * pi + a_i * pr))
            return tuple(new)

        st = lax.fori_loop(0, n, p1, tuple((zero, zero, zero + 1.0, zero) for _ in range(K)))
        C = [_seg_carries(st[k][0], st[k][1], st[k][2], st[k][3], True) for k in range(K)]
        xb = [(jnp.where(rows == 0, 0.0, pltpu.roll(_seg_rows(xr_ref, k, n - 1, n), 1, 0)),
               jnp.where(rows == 0, 0.0, pltpu.roll(_seg_rows(xi_ref, k, n - 1, n), 1, 0))) for k in range(K)]

        def p2(jj, st):
            j = n - 1 - jj
            jp = jnp.maximum(j - 1, 0)
            new = []
            for k in range(K):
                pr, pi, acr, aci = st[k]
                a_r, a_i = A[k]
                pr, pi = a_r * pr - a_i * pi, a_r * pi + a_i * pr
                cr, ci = C[k]
                g_r = _seg_rows(gr_ref, k, j, n) + pr * cr - pi * ci
                g_i = _seg_rows(gi_ref, k, j, n) + pr * ci + pi * cr
                _seg_store(gr_ref, k, j, n, g_r)
                _seg_store(gi_ref, k, j, n, g_i)
                xpr = jnp.where(j == 0, xb[k][0], _seg_rows(xr_ref, k, jp, n))
                xpi = jnp.where(j == 0, xb[k][1], _seg_rows(xi_ref, k, jp, n))
                new.append((pr, pi, acr + g_r * xpr + g_i * xpi, aci + g_i * xpr - g_r * xpi))
            return tuple(new)

        st = lax.fori_loop(0, n, p2, tuple((zero + 1.0, zero, zero, zero) for _ in range(K)))
        for k in range(K):
            dar_ref[k] = jnp.sum(st[k][2], axis=0, keepdims=True)
            dai_ref[k] = jnp.sum(st[k][3], axis=0, keepdims=True)

    blk = pl.BlockSpec((K, L, LANE), lambda g: (g, 0, 0))
    ablk = pl.BlockSpec((K, 1, LANE), lambda g: (g, 0, 0))
    return pl.pallas_call(
        body, grid=(nb // K,), name="s5_scan_bwd",
        in_specs=[blk, blk, blk, blk, ablk, ablk], out_specs=[blk, blk, ablk, ablk],
        out_shape=[jax.ShapeDtypeStruct((nb, L, LANE), F32)] * 2 + [jax.ShapeDtypeStruct((nb, 1, LANE), F32)] * 2,
        compiler_params=pltpu.CompilerParams(vmem_limit_bytes=VMEM_LIMIT),
    )(dxr, dxi, xr, xi, ar, ai)


def rg_scan_fwd(a, b, L):
    n = L // NSEG

    def body(a_ref, b_ref, h_ref):
        zero = jnp.zeros((NSEG, LANE), F32)

        def p1(j, st):
            h, p = st
            aj = _seg_rows(a_ref, None, j, n)
            h = aj * h + _seg_rows(b_ref, None, j, n)
            _seg_store(h_ref, None, j, n, h)
            return h, aj * p

        e, pe = lax.fori_loop(0, n, p1, (zero, zero + 1.0))
        c, _ = _seg_carries(e, None, pe, None, False)

        def p2(j, p):
            p = _seg_rows(a_ref, None, j, n) * p
            _seg_store(h_ref, None, j, n, _seg_rows(h_ref, None, j, n) + p * c)
            return p

        lax.fori_loop(0, n, p2, zero + 1.0)

    blk = pl.BlockSpec((L, LANE), lambda g: (0, g))
    return pl.pallas_call(
        body, grid=(BW // LANE,), name="rg_scan_fwd", in_specs=[blk, blk], out_specs=blk,
        out_shape=jax.ShapeDtypeStruct((L, BW), F32),
        compiler_params=pltpu.CompilerParams(vmem_limit_bytes=VMEM_LIMIT),
    )(a, b)


def rg_scan_bwd(a, h, dh, L):
    n = L // NSEG

    def body(a_ref, h_ref, dh_ref, da_ref, db_ref):
        zero = jnp.zeros((NSEG, LANE), F32)
        rows = lax.broadcasted_iota(jnp.int32, (NSEG, LANE), 0)
        a_edge = jnp.where(rows == NSEG - 1, 0.0, pltpu.roll(_seg_rows(a_ref, None, 0, n), NSEG - 1, 0))
        h_edge = jnp.where(rows == 0, 0.0, pltpu.roll(_seg_rows(h_ref, None, n - 1, n), 1, 0))

        def mult(j):
            return jnp.where(j == n - 1, a_edge, _seg_rows(a_ref, None, jnp.minimum(j + 1, n - 1), n))

        def p1(jj, st):
            j = n - 1 - jj
            g, p = st
            m = mult(j)
            g = m * g + _seg_rows(dh_ref, None, j, n)
            _seg_store(db_ref, None, j, n, g)
            return g, m * p

        e, pe = lax.fori_loop(0, n, p1, (zero, zero + 1.0))
        c, _ = _seg_carries(e, None, pe, None, True)

        def p2(jj, p):
            j = n - 1 - jj
            p = mult(j) * p
            g = _seg_rows(db_ref, None, j, n) + p * c
            _seg_store(db_ref, None, j, n, g)
            hp = jnp.where(j == 0, h_edge, _seg_rows(h_ref, None, jnp.maximum(j - 1, 0), n))
            _seg_store(da_ref, None, j, n, g * hp)
            return p

        lax.fori_loop(0, n, p2, zero + 1.0)

    blk = pl.BlockSpec((L, LANE), lambda g: (0, g))
    return pl.pallas_call(
        body, grid=(BW // LANE,), name="rg_scan_bwd", in_specs=[blk, blk, blk], out_specs=[blk, blk],
        out_shape=[jax.ShapeDtypeStruct((L, BW), F32)] * 2,
        compiler_params=pltpu.CompilerParams(vmem_limit_bytes=VMEM_LIMIT),
    )(a, h, dh)


def _hg_consts(C):
    t = lax.broadcasted_iota(jnp.int32, (C, C), 0)
    s = lax.broadcasted_iota(jnp.int32, (C, C), 1)
    tril = (s <= t).astype(F32)
    diag = (s == t).astype(F32)
    levels = []
    k = 1
    while (1 << k) <= C:
        m = 1 << (k - 1)
        same = (t >> k) == (s >> k)
        t_right = ((t >> (k - 1)) & 1) == 1
        s_left = ((s >> (k - 1)) & 1) == 0
        mask = jnp.logical_and(same, jnp.logical_and(t_right, s_left)).astype(F32)
        bnd = ((t >> k) << k) + (m - 1)
        levels.append((mask, (s <= bnd).astype(F32)))
        k += 1
    return tril, diag, levels


def hg_chunk(st, q, z, v, lb):
    C = q.shape[0]
    tril, diag, levels = _hg_consts(C)
    sig = jax.nn.sigmoid(z)
    lf = jnp.log(lb + (1.0 - lb) * sig)
    k = (1.0 - lb) * jax.nn.sigmoid(-z)
    qh = jax.nn.silu(q)
    b = mm_exact(tril, lf)
    blast = jnp.sum(lf, axis=0, keepdims=True)
    qe = qh * jnp.exp(b)
    kd = k * jnp.exp(blast - b)
    scaled = []
    for _, sel in levels:
        ref = mm_exact(sel, lf)
        scaled.append((qh * jnp.exp(jnp.minimum(b - ref, 0.0)), k * jnp.exp(jnp.minimum(ref - b, 0.0))))
    outs, news = [], []
    for h in range(HG_HEADS):
        sl = slice(h * HG_D, (h + 1) * HG_D)
        st_h = st[h * HG_D:(h + 1) * HG_D, :]
        sc = diag * mma_nt(qh[:, sl], k[:, sl])
        for (mask, _), (qt, kt) in zip(levels, scaled):
            sc = sc + mask * mma_nt(qt[:, sl], kt[:, sl])
        outs.append(mma_nt(qe[:, sl], st_h) + mma_nn(sc, v[:, sl]))
        news.append(st_h * jnp.exp(blast[:, sl]) + mma_tn(v[:, sl], kd[:, sl]))
    return jnp.concatenate(news, axis=0), jnp.concatenate(outs, axis=1)


def hg_fwd(proj, lb, L):
    C = HG_CHUNK
    nc = L // C

    def body(q_ref, z_ref, v_ref, lb_ref, o_ref, sst_ref, st_ref):
        @pl.when(pl.program_id(0) == 0)
        def _():
            st_ref[...] = jnp.zeros_like(st_ref)

        st = st_ref[...]
        sst_ref[...] = st
        new, o = hg_chunk(st, q_ref[...], z_ref[...], v_ref[...], lb_ref[...])
        st_ref[...] = new
        o_ref[...] = o

    col = lambda cb: pl.BlockSpec((C, BW), functools.partial(lambda c, cb: (c, cb), cb=cb))
    return pl.pallas_call(
        body, grid=(nc,), name="hg_fwd",
        in_specs=[col(1), col(2), col(3), pl.BlockSpec((1, BW), lambda c: (0, 0))],
        out_specs=[pl.BlockSpec((C, BW), lambda c: (c, 0)), pl.BlockSpec((None, BW, HG_D), lambda c: (c, 0, 0))],
        out_shape=[jax.ShapeDtypeStruct((L, BW), F32), jax.ShapeDtypeStruct((nc, BW, HG_D), F32)],
        scratch_shapes=[pltpu.VMEM((BW, HG_D), F32)],
        compiler_params=pltpu.CompilerParams(vmem_limit_bytes=VMEM_LIMIT, dimension_semantics=("arbitrary",)),
    )(proj, proj, proj, lb)


def hg_bwd(proj, lb, sst, do, L):
    C = HG_CHUNK
    nc = L // C

    def body(q_ref, z_ref, v_ref, lb_ref, sst_ref, do_ref, dq_ref, dz_ref, dv_ref, dlb_ref, dst_ref):
        @pl.when(pl.program_id(0) == 0)
        def _():
            dst_ref[...] = jnp.zeros_like(dst_ref)
            dlb_ref[...] = jnp.zeros_like(dlb_ref)

        _, vjp = jax.vjp(hg_chunk, sst_ref[...], q_ref[...], z_ref[...], v_ref[...], lb_ref[...])
        dst, dq, dz, dv, dlb = vjp((dst_ref[...], do_ref[...]))
        dst_ref[...] = dst
        dq_ref[...] = dq
        dz_ref[...] = dz
        dv_ref[...] = dv
        dlb_ref[...] += dlb

    col = lambda cb: pl.BlockSpec((C, BW), functools.partial(lambda c, cb: (nc - 1 - c, cb), cb=cb))
    rev = pl.BlockSpec((C, BW), lambda c: (nc - 1 - c, 0))
    return pl.pallas_call(
        body, grid=(nc,), name="hg_bwd",
        in_specs=[col(1), col(2), col(3), pl.BlockSpec((1, BW), lambda c: (0, 0)),
                  pl.BlockSpec((None, BW, HG_D), lambda c: (nc - 1 - c, 0, 0)), rev],
        out_specs=[rev, rev, rev, pl.BlockSpec((1, BW), lambda c: (0, 0))],
        out_shape=[jax.ShapeDtypeStruct((L, BW), F32)] * 3 + [jax.ShapeDtypeStruct((1, BW), F32)],
        scratch_shapes=[pltpu.VMEM((BW, HG_D), F32)],
        compiler_params=pltpu.CompilerParams(vmem_limit_bytes=VMEM_LIMIT, dimension_semantics=("arbitrary",)),
    )(proj, proj, proj, lb, sst, do)


def _shift_down(x, d, rows):
    return x if d == 0 else jnp.where(rows < d, 0.0, pltpu.roll(x, d, 0))


def _shift_up(x, d, rows, L):
    return x if d == 0 else jnp.where(rows >= L - d, 0.0, pltpu.roll(x, L - d, 0))


def conv_fwd(proj, w, b, L):
    def body(x_ref, w_ref, b_ref, o_ref):
        x = x_ref[...]
        rows = lax.broadcasted_iota(jnp.int32, x.shape, 0)
        acc = jnp.broadcast_to(b_ref[...], x.shape)
        for k in range(CONV_W):
            acc = acc + w_ref[pl.ds(k, 1), :] * _shift_down(x, CONV_W - 1 - k, rows)
        o_ref[...] = acc

    nl = BW // LANE
    return pl.pallas_call(
        body, grid=(nl,), name="conv_fwd",
        in_specs=[pl.BlockSpec((L, LANE), lambda g: (0, 5 * nl + g)), pl.BlockSpec((CONV_W, LANE), lambda g: (0, g)),
                  pl.BlockSpec((1, LANE), lambda g: (0, g))],
        out_specs=pl.BlockSpec((L, LANE), lambda g: (0, g)),
        out_shape=jax.ShapeDtypeStruct((L, BW), F32),
        compiler_params=pltpu.CompilerParams(vmem_limit_bytes=VMEM_LIMIT),
    )(proj, w, b)


def conv_bwd(proj, w, dxc, L):
    def body(x_ref, w_ref, d_ref, dx_ref, dw_ref, db_ref):
        x, d = x_ref[...], d_ref[...]
        rows = lax.broadcasted_iota(jnp.int32, x.shape, 0)
        acc = jnp.zeros_like(x)
        for k in range(CONV_W):
            acc = acc + w_ref[pl.ds(k, 1), :] * _shift_up(d, CONV_W - 1 - k, rows, L)
            dw_ref[pl.ds(k, 1), :] = jnp.sum(d * _shift_down(x, CONV_W - 1 - k, rows), axis=0, keepdims=True)
        dx_ref[...] = acc
        db_ref[...] = jnp.sum(d, axis=0, keepdims=True)

    nl = BW // LANE
    blk = pl.BlockSpec((L, LANE), lambda g: (0, g))
    return pl.pallas_call(
        body, grid=(nl,), name="conv_bwd",
        in_specs=[pl.BlockSpec((L, LANE), lambda g: (0, 5 * nl + g)), pl.BlockSpec((CONV_W, LANE), lambda g: (0, g)), blk],
        out_specs=[blk, pl.BlockSpec((CONV_W, LANE), lambda g: (0, g)), pl.BlockSpec((1, LANE), lambda g: (0, g))],
        out_shape=[jax.ShapeDtypeStruct((L, BW), F32), jax.ShapeDtypeStruct((CONV_W, BW), F32),
                   jax.ShapeDtypeStruct((1, BW), F32)],
        compiler_params=pltpu.CompilerParams(vmem_limit_bytes=VMEM_LIMIT),
    )(proj, w, dxc)


def loss_fwd_bwd(x, fw, target, L, tm):
    def fn(x, fw, t):
        err = jnp.square(_rms(x, fw) - t)
        return jnp.sum(0.5 * jnp.mean(err, axis=-1, keepdims=True), axis=0, keepdims=True)

    def body(x_ref, fw_ref, t_ref, l_ref, dx_ref, dfw_ref):
        i = pl.program_id(0)
        t = t_ref[...]
        val, vjp = jax.vjp(lambda x, fw: fn(x, fw, t), x_ref[...], fw_ref[...])
        dx, dfw = vjp(jnp.ones((1, 1), F32))
        dx_ref[...] = dx

        @pl.when(i == 0)
        def _():
            l_ref[...] = jnp.zeros_like(l_ref)
            dfw_ref[...] = jnp.zeros_like(dfw_ref)

        l_ref[...] += jnp.broadcast_to(val, l_ref.shape)
        dfw_ref[...] += dfw

    row = pl.BlockSpec((tm, D_MODEL), lambda i: (i, 0))
    vec = pl.BlockSpec((1, D_MODEL), lambda i: (0, 0))
    return pl.pallas_call(
        body, grid=(L // tm,), name="loss_fwd_bwd", in_specs=[row, vec, row],
        out_specs=[pl.BlockSpec((1, LANE), lambda i: (0, 0)), row, vec],
        out_shape=[jax.ShapeDtypeStruct((1, LANE), F32), jax.ShapeDtypeStruct((L, D_MODEL), F32),
                   jax.ShapeDtypeStruct((1, D_MODEL), F32)],
        compiler_params=pltpu.CompilerParams(vmem_limit_bytes=VMEM_LIMIT, dimension_semantics=("arbitrary",)),
    )(x, fw, target)


def adamw(w, g, m, v):
    rows, cols = w.shape
    tr = _row_tile(rows, cols, budget=1024 * 1024)
    c1 = 1.0 - ADAM_B1 ** ADAM_STEP
    c2 = 1.0 - ADAM_B2 ** ADAM_STEP

    def body(w_ref, g_ref, m_ref, v_ref, d_ref, nm_ref, nv_ref):
        g = g_ref[...]
        nm = ADAM_B1 * m_ref[...] + (1.0 - ADAM_B1) * g
        nv = ADAM_B2 * v_ref[...] + (1.0 - ADAM_B2) * jnp.square(g)
        d_ref[...] = -ADAM_LR * ((nm / c1) / (jnp.sqrt(nv / c2) + ADAM_EPS) + ADAM_WD * w_ref[...])
        nm_ref[...] = nm
        nv_ref[...] = nv

    blk = pl.BlockSpec((tr, cols), lambda i: (i, 0))
    return pl.pallas_call(
        body, grid=(rows // tr,), name="adamw", in_specs=[blk] * 4, out_specs=[blk] * 3,
        out_shape=[jax.ShapeDtypeStruct((rows, cols), F32)] * 3,
    )(w, g, m, v)


def s5_prep(lam_re, lam_im, log_dt, b_re, b_im, c_re, c_im):
    lr = jnp.minimum(lam_re, -1e-4)
    li = lam_im
    dt = jnp.exp(log_dt)[:, None]
    mag = jnp.exp(lr * dt)
    ar = mag * jnp.cos(li * dt)
    ai = mag * jnp.sin(li * dt)
    den = lr * lr + li * li
    fr = ((ar - 1.0) * lr + ai * li) / den
    fi = (ai * lr - (ar - 1.0) * li) / den
    bbr = fr[..., None] * b_re - fi[..., None] * b_im
    bbi = fr[..., None] * b_im + fi[..., None] * b_re
    eye = jnp.eye(S5_GROUPS, dtype=F32)
    emb_b = lambda bb: jnp.einsum("gpc,gh->gchp", bb, eye).reshape(BW, S5_N)
    emb_c = lambda cc: jnp.einsum("gcp,gh->gphc", cc, eye).reshape(S5_N, BW)
    bmat = jnp.concatenate([emb_b(bbr), emb_b(bbi)], axis=1)
    cmat = jnp.concatenate([emb_c(c_re), -emb_c(c_im)], axis=0)
    nb = S5_N // LANE
    return ar.reshape(nb, 1, LANE), ai.reshape(nb, 1, LANE), bmat, cmat


def rg_prep(w):
    return jnp.einsum("hij,hk->hikj", w, jnp.eye(RG_BLOCKS, dtype=F32)).reshape(BW, BW)


def hg_prep(logits):
    p = jax.nn.softmax(logits, axis=0)
    return jnp.cumsum(p, axis=0) - p[0]


def _head_mean_matrix():
    r = jnp.arange(BW) // HG_D
    return (r[:, None] == r[None, :]).astype(F32) / HG_D


def _const(*idx):
    return lambda s, i: idx


def _rows(cb=0):
    return lambda s, i: (i, cb)


def _sum_parts(name, first, parts, shape):
    return add_n(name, [(first, ())] + [(parts, (s,)) for s in range(NSH)], shape)


def ffn_fwd(name, x, W, l, j, k, L, tm):
    D = D_MODEL
    fn = lambda x, nw, wg, wu, wd, s: (jnp.where(s == 0, x, 0.0) + ffn_core(x, nw, wg, wu, wd)[0],)
    ins = [(x, (tm, D), _rows()),
           (W["nw"], (None, None, 1, D), _const(l, k, 0, 0)),
           (W["wg"], (None, None, None, D, D_FF // NSH), lambda s, i: (l, s, j, 0, 0)),
           (W["wu"], (None, None, None, D, D_FF // NSH), lambda s, i: (l, s, j, 0, 0)),
           (W["wd"], (None, None, None, D_FF // NSH, D), lambda s, i: (l, s, j, 0, 0))]
    return tile_fwd(fn, name, L // tm, NSH, ins, [((L, D), F32, (tm, D), _rows(), True)])[0]


def ffn_bwd(name, x, dy, W, l, j, k, L, tm):
    D, F = D_MODEL, D_FF // NSH
    ins = [(x, (tm, D), _rows(), "r"),
           (W["nw"], (None, None, 1, D), _const(l, k, 0, 0), "p"),
           (W["wg"], (None, None, None, D, F), lambda s, i: (l, s, j, 0, 0), "w"),
           (W["wu"], (None, None, None, D, F), lambda s, i: (l, s, j, 0, 0), "w"),
           (W["wd"], (None, None, None, F, D), lambda s, i: (l, s, j, 0, 0), "w")]
    gouts = [((NSH, L, D), (None, tm, D), lambda s, i: (s, i, 0), "write"),
             ((1, D), (1, D), _const(0, 0), "acc_all"),
             ((NSH, D, F), (None, D, F), lambda s, i: (s, 0, 0), "acc_i"),
             ((NSH, D, F), (None, D, F), lambda s, i: (s, 0, 0), "acc_i"),
             ((NSH, F, D), (None, F, D), lambda s, i: (s, 0, 0), "acc_i")]
    part, dnw, dwg, dwu, dwd = tile_bwd(ffn_core, name, L // tm, NSH, ins, [(dy, (tm, D), _rows())], gouts)
    return _sum_parts(name + "_dx", dy, part, (L, D)), dnw, dwg, dwu, dwd


def layer_fwd(l, x0, W, P, L, tm):
    D = D_MODEL
    n_i = L // tm
    x1 = ffn_fwd(f"ffn_fwd_{l}0", x0, W, l, 0, 0, L, tm)
    proj = tile_fwd(
        lambda x, nw, win, s: pre_core(x, nw, win), f"pre_fwd_{l}", n_i, NSH,
        [(x1, (tm, D), _rows()), (W["nw"], (None, None, 1, D), _const(l, 1, 0, 0)),
         (W["win"], (None, None, D, IN_TOTAL // NSH), lambda s, i: (l, s, 0, 0))],
        [((L, IN_TOTAL), F32, (tm, IN_TOTAL // NSH), lambda s, i: (i, s), False)])[0]
    nb = S5_N // LANE
    blk3 = lambda s, i: (0, i, 0)
    bur, bui = tile_fwd(
        lambda u, bmat, s: s5_pre_core(u, bmat), f"s5pre_fwd_{l}", n_i, 1,
        [(proj, (tm, BW), _rows(0)), (P["bmat"][l], (BW, 2 * S5_N), _const(0, 0))],
        [((nb, L, LANE), F32, (nb, tm, LANE), blk3, False)] * 2)
    xr, xi = s5_scan_fwd(bur, bui, P["ar"][l], P["ai"][l], L)
    o, sst = hg_fwd(proj, P["lb"][l], L)
    xc = conv_fwd(proj, W["convw"][l], P["convb"][l], L)
    vec = (None, 1, BW)
    a, b = tile_fwd(
        lambda xc, wa, ba, wx, bx, lam, s: gates_core(xc, wa, ba, wx, bx, lam), f"gates_fwd_{l}", n_i, 1,
        [(xc, (tm, BW), _rows()), (P["wa"][l], (BW, BW), _const(0, 0)), (P["ba"], vec, _const(l, 0, 0)),
         (P["wx"][l], (BW, BW), _const(0, 0)), (P["bx"], vec, _const(l, 0, 0)), (P["lam"], vec, _const(l, 0, 0))],
        [((L, BW), F32, (tm, BW), _rows(), False)] * 2)
    hs = rg_scan_fwd(a, b, L)
    tmm = min(tm, 128)
    ya, yb, yc = tile_fwd(
        lambda *a: mid_core(*a[:-1]), f"mid_fwd_{l}", L // tmm, 1,
        [(xr, (nb, tmm, LANE), blk3), (xi, (nb, tmm, LANE), blk3), (proj, (tmm, BW), _rows(0)), (o, (tmm, BW), _rows()),
         (proj, (tmm, BW), _rows(4)), (hs, (tmm, BW), _rows()), (proj, (tmm, BW), _rows(6)),
         (P["hmat"], (BW, BW), _const(0, 0)), (P["cmat"][l], (2 * S5_N, BW), _const(0, 0)), (P["d"], vec, _const(l, 0, 0)),
         (W["gluw"], (None, BW, BW), _const(l, 0, 0)), (P["glub"], vec, _const(l, 0, 0)), (P["hgw"], vec, _const(l, 0, 0))],
        [((L, BW), F32, (tmm, BW), _rows(), False)] * 3)
    dq = D // NSH
    gm = lambda n: (proj, (tm, dq), functools.partial(lambda s, i, n: (i, 14 + 4 * n + s), n=n))
    pw = lambda n: (W["p"], (None, None, None, BW, dq), functools.partial(lambda s, i, n: (l, s, n, 0, 0), n=n))
    x2 = tile_fwd(
        lambda x, ya, yb, yc, g0, g1, g2, p0, p1, p2, wout, s:
            (jnp.where(s == 0, x, 0.0) + merge_core(ya, yb, yc, g0, g1, g2, p0, p1, p2, wout)[0],),
        f"merge_fwd_{l}", n_i, NSH,
        [(x1, (tm, D), _rows()), (ya, (tm, BW), _rows()), (yb, (tm, BW), _rows()), (yc, (tm, BW), _rows()),
         gm(0), gm(1), gm(2), pw(0), pw(1), pw(2), (W["wout"], (None, None, dq, D), lambda s, i: (l, s, 0, 0))],
        [((L, D), F32, (tm, D), _rows(), True)])[0]
    x3 = ffn_fwd(f"ffn_fwd_{l}1", x2, W, l, 1, 2, L, tm)
    saved = dict(x0=x0, x1=x1, x2=x2, proj=proj, xr=xr, xi=xi, o=o, sst=sst, xc=xc, a=a, hs=hs, ya=ya, yb=yb, yc=yc)
    return x3, saved


def layer_bwd(l, dx3, sv, W, P, L, tm):
    D = D_MODEL
    n_i = L // tm
    nb = S5_N // LANE
    dq = D // NSH
    vec = (None, 1, BW)
    vout = ((1, BW), (1, BW), _const(0, 0), "acc_all")
    blk3 = lambda s, i: (0, i, 0)
    big, small = {}, {}
    proj = sv["proj"]

    dx2, dnw2, dwg, dwu, dwd = ffn_bwd(f"ffn_bwd_{l}1", sv["x2"], dx3, W, l, 1, 2, L, tm)
    big[("ffn_gate", (l, 1))], big[("ffn_up", (l, 1))], big[("ffn_down", (l, 1))] = dwg, dwu, dwd

    gm = lambda n: (proj, (tm, dq), functools.partial(lambda s, i, n: (i, 14 + 4 * n + s), n=n), "r")
    pw = lambda n: (W["p"], (None, None, None, BW, dq), functools.partial(lambda s, i, n: (l, s, n, 0, 0), n=n), "w")
    ypart = ((NSH, L, BW), (None, tm, BW), lambda s, i: (s, i, 0), "write")
    gpart = ((L, D), (tm, dq), lambda s, i: (i, s), "write")
    ppart = ((NSH, BW, dq), (None, BW, dq), lambda s, i: (s, 0, 0), "acc_i")
    res = tile_bwd(
        merge_core, f"merge_bwd_{l}", n_i, NSH,
        [(sv["ya"], (tm, BW), _rows(), "r"), (sv["yb"], (tm, BW), _rows(), "r"), (sv["yc"], (tm, BW), _rows(), "r"),
         gm(0), gm(1), gm(2), pw(0), pw(1), pw(2), (W["wout"], (None, None, dq, D), lambda s, i: (l, s, 0, 0), "w")],
        [(dx2, (tm, D), _rows())],
        [ypart, ypart, ypart, gpart, gpart, gpart, ppart, ppart, ppart,
         ((NSH, dq, D), (None, dq, D), lambda s, i: (s, 0, 0), "acc_i")])
    pya, pyb, pyc, dg0, dg1, dg2, dp0, dp1, dp2, dwout = res
    for n, dp in enumerate((dp0, dp1, dp2)):
        big[("branch_proj", (l, n))] = dp
    big[("w_out", (l,))] = dwout
    sum4 = lambda name, part: add_n(name, [(part, (s,)) for s in range(NSH)], (L, BW))
    dya, dyb, dyc = sum4(f"dya_{l}", pya), sum4(f"dyb_{l}", pyb), sum4(f"dyc_{l}", pyc)

    tmm = min(tm, 128)
    rw = ((L, BW), (tmm, BW), _rows(), "write")
    xw = ((nb, L, LANE), (nb, tmm, LANE), blk3, "write")
    res = tile_bwd(
        mid_core, f"mid_bwd_{l}", L // tmm, 1,
        [(sv["xr"], (nb, tmm, LANE), blk3, "r"), (sv["xi"], (nb, tmm, LANE), blk3, "r"), (proj, (tmm, BW), _rows(0), "r"),
         (sv["o"], (tmm, BW), _rows(), "r"), (proj, (tmm, BW), _rows(4), "r"), (sv["hs"], (tmm, BW), _rows(), "r"),
         (proj, (tmm, BW), _rows(6), "r"), (P["hmat"], (BW, BW), _const(0, 0), "c"),
         (P["cmat"][l], (2 * S5_N, BW), _const(0, 0), "w"), (P["d"], vec, _const(l, 0, 0), "p"),
         (W["gluw"], (None, BW, BW), _const(l, 0, 0), "w"), (P["glub"], vec, _const(l, 0, 0), "p"),
         (P["hgw"], vec, _const(l, 0, 0), "p")],
        [(dya, (tmm, BW), _rows()), (dyb, (tmm, BW), _rows()), (dyc, (tmm, BW), _rows())],
        [xw, xw, rw, rw, rw, rw, rw, ((2 * S5_N, BW), (2 * S5_N, BW), _const(0, 0), "acc_all"), vout,
         ((BW, BW), (BW, BW), _const(0, 0), "acc_all"), vout, vout])
    dxr, dxi, du_skip, do, dg_b, dhs, dgate_c, dcmat, dd, dgluw, dglub, dhgw = res
    small["s5_d"], small["s5_glu_b"], small["hg_norm_w"] = dd[0], dglub[0], dhgw[0]
    big[("s5_glu_w", (l,))] = dgluw.reshape(NSH, BW // NSH, BW)

    da, db = rg_scan_bwd(sv["a"], sv["hs"], dhs, L)
    wmat = ((BW, BW), (BW, BW), _const(0, 0), "acc_all")
    res = tile_bwd(
        gates_core, f"gates_bwd_{l}", n_i, 1,
        [(sv["xc"], (tm, BW), _rows(), "r"), (P["wa"][l], (BW, BW), _const(0, 0), "w"), (P["ba"], vec, _const(l, 0, 0), "p"),
         (P["wx"][l], (BW, BW), _const(0, 0), "w"), (P["bx"], vec, _const(l, 0, 0), "p"), (P["lam"], vec, _const(l, 0, 0), "p")],
        [(da, (tm, BW), _rows()), (db, (tm, BW), _rows())],
        [((L, BW), (tm, BW), _rows(), "write"), wmat, vout, wmat, vout, vout])
    dxc, dwa, dba, dwx, dbx, dlam = res
    small["rg_ba"], small["rg_bx"], small["rg_lambda"] = dba[0], dbx[0], dlam[0]
    dx_c, dconvw, dconvb = conv_bwd(proj, W["convw"][l], dxc, L)
    small["rg_conv_w"], small["rg_conv_b"] = dconvw, dconvb[0]

    dq_b, dz_b, dv_b, dlb = hg_bwd(proj, P["lb"][l], sv["sst"], do, L)

    gr, gi, dar, dai = s5_scan_bwd(dxr, dxi, sv["xr"], sv["xi"], P["ar"][l], P["ai"][l], L)
    du_pre, dbmat = tile_bwd(
        s5_pre_core, f"s5pre_bwd_{l}", n_i, 1,
        [(proj, (tm, BW), _rows(0), "r"), (P["bmat"][l], (BW, 2 * S5_N), _const(0, 0), "w")],
        [(gr, (nb, tm, LANE), blk3), (gi, (nb, tm, LANE), blk3)],
        [((L, BW), (tm, BW), _rows(), "write"), ((BW, 2 * S5_N), (BW, 2 * S5_N), _const(0, 0), "acc_all")])
    du_a = add_n(f"du_a_{l}", [(du_skip, ()), (du_pre, ())], (L, BW))
    prep_ct = dict(dar=dar, dai=dai, dbmat=dbmat, dcmat=dcmat, dwa=dwa, dwx=dwx, dlb=dlb)

    dproj = jnp.concatenate([du_a, dq_b, dz_b, dv_b, dg_b, dx_c, dgate_c, dg0, dg1, dg2], axis=1)
    part, dnw1, dwin = tile_bwd(
        pre_core, f"pre_bwd_{l}", n_i, NSH,
        [(sv["x1"], (tm, D), _rows(), "r"), (W["nw"], (None, None, 1, D), _const(l, 1, 0, 0), "p"),
         (W["win"], (None, None, D, IN_TOTAL // NSH), lambda s, i: (l, s, 0, 0), "w")],
        [(dproj, (tm, IN_TOTAL // NSH), lambda s, i: (i, s))],
        [((NSH, L, D), (None, tm, D), lambda s, i: (s, i, 0), "write"), ((1, D), (1, D), _const(0, 0), "acc_all"),
         ((NSH, D, IN_TOTAL // NSH), (None, D, IN_TOTAL // NSH), lambda s, i: (s, 0, 0), "acc_i")])
    big[("w_in", (l,))] = dwin
    dx1 = _sum_parts(f"pre_bwd_{l}_dx", dx2, part, (L, D))

    dx0, dnw0, dwg, dwu, dwd = ffn_bwd(f"ffn_bwd_{l}0", sv["x0"], dx1, W, l, 0, 0, L, tm)
    big[("ffn_gate", (l, 0))], big[("ffn_up", (l, 0))], big[("ffn_down", (l, 0))] = dwg, dwu, dwd
    small["norm_w"] = jnp.concatenate([dnw0, dnw1, dnw2], axis=0)
    return dx0, big, small, prep_ct


SMALL_RAW = ("s5_lambda_re", "s5_lambda_im", "s5_log_dt", "s5_b_re", "s5_b_im", "s5_c_re", "s5_c_im", "s5_d", "s5_glu_b",
             "hg_lb_logits", "hg_norm_w", "rg_conv_b", "rg_wa", "rg_ba", "rg_wx", "rg_bx", "rg_lambda", "final_norm_w")
DEPTH = 2


def local_step(x, target, W, raw):
    L = x.shape[0]
    tm = min(256, L)
    col = lambda v: v.reshape(DEPTH, 1, BW)
    s5_out, s5_vjp = [], []
    for l in range(DEPTH):
        out, vjp = jax.vjp(s5_prep, *[raw[k][l] for k in SMALL_RAW[:7]])
        s5_out.append(out)
        s5_vjp.append(vjp)
    (wa, wx), rg_vjp = jax.vjp(lambda a, b: (jax.vmap(rg_prep)(a), jax.vmap(rg_prep)(b)), raw["rg_wa"], raw["rg_wx"])
    lb, hg_vjp = jax.vjp(hg_prep, raw["hg_lb_logits"])
    P = dict(
        ar=[o[0] for o in s5_out], ai=[o[1] for o in s5_out],
        bmat=[o[2].astype(MMT) for o in s5_out], cmat=[o[3].astype(MMT) for o in s5_out],
        lb=[lb[l].reshape(1, BW) for l in range(DEPTH)], convb=[raw["rg_conv_b"][l].reshape(1, BW) for l in range(DEPTH)],
        wa=[wa[l].astype(MMT) for l in range(DEPTH)], wx=[wx[l].astype(MMT) for l in range(DEPTH)],
        ba=col(raw["rg_ba"]), bx=col(raw["rg_bx"]), lam=col(raw["rg_lambda"]), d=col(raw["s5_d"]),
        glub=col(raw["s5_glu_b"]), hgw=col(raw["hg_norm_w"]), hmat=_head_mean_matrix())

    saved = []
    h = x
    for l in range(DEPTH):
        h, sv = layer_fwd(l, h, W, P, L, tm)
        saved.append(sv)
    loss, dh, dfw = loss_fwd_bwd(h, raw["final_norm_w"].reshape(1, D_MODEL), target, L, tm)

    big, per_layer, prep_cts = {}, [None] * DEPTH, [None] * DEPTH
    for l in reversed(range(DEPTH)):
        dh, bg, sm, pc = layer_bwd(l, dh, saved[l], W, P, L, tm)
        big.update(bg)
        per_layer[l], prep_cts[l] = sm, pc

    small = {k: jnp.stack([per_layer[l][k] for l in range(DEPTH)]) for k in per_layer[0]}
    s5_g = [s5_vjp[l]((prep_cts[l]["dar"], prep_cts[l]["dai"], prep_cts[l]["dbmat"], prep_cts[l]["dcmat"])) for l in range(DEPTH)]
    for j, k in enumerate(SMALL_RAW[:7]):
        small[k] = jnp.stack([s5_g[l][j] for l in range(DEPTH)])
    small["rg_wa"], small["rg_wx"] = rg_vjp((jnp.stack([prep_cts[l]["dwa"] for l in range(DEPTH)]),
                                             jnp.stack([prep_cts[l]["dwx"] for l in range(DEPTH)])))
    (small["hg_lb_logits"],) = hg_vjp(jnp.concatenate([prep_cts[l]["dlb"] for l in range(DEPTH)], axis=0))
    small["final_norm_w"] = dfw[0]
    return loss, dh, big, small


ANY = pl.BlockSpec(memory_space=pl.ANY)


def _place():
    x, y, c = lax.axis_index("x"), lax.axis_index("y"), lax.axis_index("c")
    chips = [(1 - x, y), (x, 1 - y), (1 - x, 1 - y)]
    return x, y, c, chips


def _remote(src, dst, send, recv, k, to):
    return pltpu.make_async_remote_copy(src_ref=src, dst_ref=dst, send_sem=send.at[k], recv_sem=recv.at[k],
                                        device_id=to, device_id_type=MESH)


def _comm_call(body, name, ins, out_shapes, n_sem, n_loc):
    return pl.pallas_call(
        body, name=name, in_specs=[ANY] * len(ins), out_specs=[ANY] * len(out_shapes), out_shape=out_shapes,
        scratch_shapes=[pltpu.SemaphoreType.DMA((n_sem,)), pltpu.SemaphoreType.DMA((n_sem,)),
                        pltpu.SemaphoreType.DMA((max(n_loc, 1),))],
    )(*ins)


def gather_shards(name, shards):
    n = len(shards)

    def body(*refs):
        ins, outs = refs[:n], refs[n:2 * n]
        send, recv, loc = refs[2 * n:]
        x, y, c, chips = _place()
        me = 2 * x + y
        sib = (x, y, 1 - c)
        started = []
        for w in range(n):
            cp = pltpu.make_async_copy(ins[w], outs[w].at[:, me], loc.at[w])
            cp.start()
            started.append(cp)
        sends = []
        for w in range(n):
            for j, (cx, cy) in enumerate(chips):
                cp = _remote(ins[w].at[c], outs[w].at[c, me], send, recv, 6 * w + j, (cx, cy, c))
                cp.start()
                sends.append(cp)
        for w in range(n):
            for j, (cx, cy) in enumerate(chips):
                theirs = outs[w].at[c, 2 * cx + cy]
                _remote(ins[w].at[c], theirs, send, recv, 6 * w + j, (cx, cy, c)).wait_recv()
                cp = _remote(theirs, theirs, send, recv, 6 * w + 3 + j, sib)
                cp.start()
                sends.append(cp)
        for w in range(n):
            for j, (cx, cy) in enumerate(chips):
                dst = outs[w].at[1 - c, 2 * cx + cy]
                _remote(dst, dst, send, recv, 6 * w + 3 + j, sib).wait_recv()
        for cp in sends:
            cp.wait_send()
        for cp in started:
            cp.wait()

    shapes = [jax.ShapeDtypeStruct((2, NSH) + s.shape[1:], s.dtype) for s in shards]
    return _comm_call(body, name, shards, shapes, 6 * n, n)


def exchange_halves(name, grads):
    n = len(grads)

    def body(*refs):
        ins, outs = refs[:n], refs[n:2 * n]
        send, recv, _ = refs[2 * n:]
        x, y, c, _chips = _place()
        cps = []
        for w in range(n):
            h = grads[w].shape[1] // 2
            cp = _remote(ins[w].at[:, pl.ds((1 - c) * h, h)], outs[w], send, recv, w, (x, y, 1 - c))
            cp.start()
            cps.append(cp)
        for cp in cps:
            cp.wait()

    shapes = [jax.ShapeDtypeStruct((NSH, g.shape[1] // 2, g.shape[2]), g.dtype) for g in grads]
    return _comm_call(body, name, grads, shapes, n, 0)


def scatter_to_chips(name, halves):
    n = len(halves)

    def body(*refs):
        ins, outs = refs[:n], refs[n:2 * n]
        send, recv, _ = refs[2 * n:]
        x, y, c, chips = _place()
        cps = []
        for w in range(n):
            for j, (cx, cy) in enumerate(chips):
                cp = _remote(ins[w].at[2 * cx + cy], outs[w].at[j], send, recv, 3 * w + j, (cx, cy, c))
                cp.start()
                cps.append(cp)
        for cp in cps:
            cp.wait()

    shapes = [jax.ShapeDtypeStruct((3,) + h.shape[1:], h.dtype) for h in halves]
    return _comm_call(body, name, halves, shapes, 3 * n, 0)


def share_reduced(name, pieces, dests, out_shapes):
    n, m = len(pieces), len(out_shapes)

    def body(*refs):
        ins, outs = refs[:n], refs[n:n + m]
        send, recv, loc = refs[n + m:]
        x, y, c, _chips = _place()
        cps = []
        for k in range(n):
            h = pieces[k].shape[0]
            o, lead = dests[k]
            dst = outs[o].at[(*lead, pl.ds(c * h, h))]
            lc = pltpu.make_async_copy(ins[k], dst, loc.at[k])
            lc.start()
            cp = _remote(ins[k], dst, send, recv, k, (x, y, 1 - c))
            cp.start()
            cps += [lc, cp]
        for cp in cps:
            cp.wait()

    return _comm_call(body, name, pieces, out_shapes, n, n)


def add_own_half(name, g, ra, c):
    _, h, cols = ra.shape
    tr = _row_tile(h, cols)
    nt = h // tr

    def body(c_ref, g_ref, r_ref, o_ref):
        o_ref[...] = g_ref[...] + r_ref[...]

    blk = (None, tr, cols)
    return pl.pallas_call(
        body, name=name,
        grid_spec=pltpu.PrefetchScalarGridSpec(
            num_scalar_prefetch=1, grid=(NSH, nt),
            in_specs=[pl.BlockSpec(blk, lambda s, i, c_ref: (s, c_ref[0] * nt + i, 0)), pl.BlockSpec(blk, lambda s, i, c_ref: (s, i, 0))],
            out_specs=pl.BlockSpec(blk, lambda s, i, c_ref: (s, i, 0))),
        out_shape=jax.ShapeDtypeStruct(ra.shape, F32),
    )(c.reshape(1), g, ra)


def add_chips(name, hb, rb, me):
    _, h, cols = hb.shape
    tr = _row_tile(h, cols)

    def body(me_ref, h_ref, r0, r1, r2, o_ref):
        o_ref[...] = ((h_ref[...] + r0[...]) + r1[...]) + r2[...]

    blk = (None, tr, cols)
    rspec = lambda j: pl.BlockSpec(blk, functools.partial(lambda i, me_ref, j: (j, i, 0), j=j))
    return pl.pallas_call(
        body, name=name,
        grid_spec=pltpu.PrefetchScalarGridSpec(
            num_scalar_prefetch=1, grid=(h // tr,),
            in_specs=[pl.BlockSpec(blk, lambda i, me_ref: (me_ref[0], i, 0)), rspec(0), rspec(1), rspec(2)],
            out_specs=pl.BlockSpec((tr, cols), lambda i, me_ref: (i, 0))),
        out_shape=jax.ShapeDtypeStruct((h, cols), F32),
    )(me.reshape(1), hb, rb, rb, rb)


WEIGHTS = ("norm_w", "final_norm_w", "ffn_gate", "ffn_up", "ffn_down", "w_in", "branch_proj", "w_out", "s5_lambda_re",
           "s5_lambda_im", "s5_log_dt", "s5_b_re", "s5_b_im", "s5_c_re", "s5_c_im", "s5_d", "s5_glu_w", "s5_glu_b",
           "hg_lb_logits", "hg_norm_w", "rg_conv_w", "rg_conv_b", "rg_wa", "rg_ba", "rg_wx", "rg_bx", "rg_lambda")
BIG = ("ffn_gate", "ffn_up", "ffn_down", "w_in", "branch_proj", "w_out", "s5_glu_w")
SHARDED_SMALL = ("norm_w", "rg_conv_w")
SMALL = SMALL_RAW + SHARDED_SMALL


def _pack(arrays, multiple):
    flat = jnp.concatenate([a.reshape(-1) for a in arrays])
    total = flat.shape[0]
    padded = -(-total // multiple) * multiple
    return jnp.pad(flat, (0, padded - total))


def _unpack(flat, like):
    out, at = [], 0
    for a in like:
        size = math.prod(a.shape)
        out.append(flat[at:at + size].reshape(a.shape))
        at += size
    return out


def _step(x, target, w, m, v):
    mx, my, mc = lax.axis_index("x"), lax.axis_index("y"), lax.axis_index("c")
    me = (2 * mx + my).astype(jnp.int32)
    mc = mc.astype(jnp.int32)

    gathered = gather_shards("gather_weights", [w[n].astype(MMT) for n in BIG] + [w[n] for n in SHARDED_SMALL])
    W = dict(wg=gathered[0], wu=gathered[1], wd=gathered[2], win=gathered[3], p=gathered[4], wout=gathered[5],
             gluw=gathered[6].reshape(DEPTH, BW, BW),
             nw=gathered[7].transpose(0, 2, 1, 3).reshape(DEPTH, 3, 1, D_MODEL),
             convw=gathered[8].transpose(0, 2, 1, 3).reshape(DEPTH, CONV_W, BW))
    loss, dx, big, small = local_step(x[0], target[0], W, {k: w[k] for k in SMALL_RAW})

    keys = sorted(big)
    small_flat = _pack([small[n] for n in SMALL], NSH * 16 * LANE)
    grads = [big[k] for k in keys] + [small_flat.reshape(NSH, -1, LANE)]
    from_sibling = exchange_halves("reduce_cores", grads)
    halves = [add_own_half(f"sum_cores_{i}", g, r, mc) for i, (g, r) in enumerate(zip(grads, from_sibling))]
    from_chips = scatter_to_chips("reduce_chips", halves)
    reduced = [add_chips(f"sum_chips_{i}", h, r, me) for i, (h, r) in enumerate(zip(halves, from_chips))]
    dests = [(BIG.index(name), lead) for name, lead in keys] + [(len(BIG), ())]
    rq = grads[-1].shape[1]
    shapes = [jax.ShapeDtypeStruct(w[n].shape, F32) for n in BIG] + [jax.ShapeDtypeStruct((rq, LANE), F32)]
    shared = share_reduced("reduce_share", reduced, dests, shapes)
    g = dict(zip(BIG, shared[:len(BIG)]))
    (all_small,) = gather_shards("gather_small", [shared[-1].reshape(2, rq // 2, LANE)])
    full_small = _unpack(all_small.transpose(1, 0, 2, 3).reshape(-1), [small[n] for n in SMALL])
    g.update(zip(SMALL, full_small))
    g["norm_w"] = lax.dynamic_slice_in_dim(g["norm_w"], me * (D_MODEL // NSH), D_MODEL // NSH, axis=2)
    g["rg_conv_w"] = lax.dynamic_slice_in_dim(g["rg_conv_w"], me * (BW // NSH), BW // NSH, axis=2)

    delta, new_m, new_v = {}, {}, {}
    for n in BIG:
        as2d = lambda a: a.reshape(-1, a.shape[-1])
        d, nm, nv = adamw(as2d(w[n]), as2d(g[n]), as2d(m[n]), as2d(v[n]))
        delta[n], new_m[n], new_v[n] = d.reshape(w[n].shape), nm.reshape(w[n].shape), nv.reshape(w[n].shape)
    packed = [_pack([src[n] for n in SMALL], 8 * LANE).reshape(-1, LANE) for src in (w, g, m, v)]
    for dst, flat in zip((delta, new_m, new_v), adamw(*packed)):
        dst.update(zip(SMALL, _unpack(flat.reshape(-1), [w[n] for n in SMALL])))

    total = lax.psum(loss[0, 0], ("x", "y", "c"))
    return (total, dx[None], *[g[n] for n in WEIGHTS], *[delta[n] for n in WEIGHTS],
            *[new_m[n] for n in WEIGHTS], *[new_v[n] for n in WEIGHTS])


def kernel(x, norm_w, final_norm_w, ffn_gate, ffn_up, ffn_down, w_in, branch_proj, w_out, s5_lambda_re, s5_lambda_im, s5_log_dt, s5_b_re, s5_b_im, s5_c_re, s5_c_im, s5_d, s5_glu_w, s5_glu_b, hg_lb_logits, hg_norm_w, rg_conv_w, rg_conv_b, rg_wa, rg_ba, rg_wx, rg_bx, rg_lambda, loss_target, m_norm_w, m_final_norm_w, m_ffn_gate, m_ffn_up, m_ffn_down, m_w_in, m_branch_proj, m_w_out, m_s5_lambda_re, m_s5_lambda_im, m_s5_log_dt, m_s5_b_re, m_s5_b_im, m_s5_c_re, m_s5_c_im, m_s5_d, m_s5_glu_w, m_s5_glu_b, m_hg_lb_logits, m_hg_norm_w, m_rg_conv_w, m_rg_conv_b, m_rg_wa, m_rg_ba, m_rg_wx, m_rg_bx, m_rg_lambda, v_norm_w, v_final_norm_w, v_ffn_gate, v_ffn_up, v_ffn_down, v_w_in, v_branch_proj, v_w_out, v_s5_lambda_re, v_s5_lambda_im, v_s5_log_dt, v_s5_b_re, v_s5_b_im, v_s5_c_re, v_s5_c_im, v_s5_d, v_s5_glu_w, v_s5_glu_b, v_hg_lb_logits, v_hg_norm_w, v_rg_conv_w, v_rg_conv_b, v_rg_wa, v_rg_ba, v_rg_wx, v_rg_bx, v_rg_lambda):
    ws = (norm_w, final_norm_w, ffn_gate, ffn_up, ffn_down, w_in, branch_proj, w_out, s5_lambda_re, s5_lambda_im, s5_log_dt, s5_b_re, s5_b_im, s5_c_re, s5_c_im, s5_d, s5_glu_w, s5_glu_b, hg_lb_logits, hg_norm_w, rg_conv_w, rg_conv_b, rg_wa, rg_ba, rg_wx, rg_bx, rg_lambda)
    ms = (m_norm_w, m_final_norm_w, m_ffn_gate, m_ffn_up, m_ffn_down, m_w_in, m_branch_proj, m_w_out, m_s5_lambda_re, m_s5_lambda_im, m_s5_log_dt, m_s5_b_re, m_s5_b_im, m_s5_c_re, m_s5_c_im, m_s5_d, m_s5_glu_w, m_s5_glu_b, m_hg_lb_logits, m_hg_norm_w, m_rg_conv_w, m_rg_conv_b, m_rg_wa, m_rg_ba, m_rg_wx, m_rg_bx, m_rg_lambda)
    vs = (v_norm_w, v_final_norm_w, v_ffn_gate, v_ffn_up, v_ffn_down, v_w_in, v_branch_proj, v_w_out, v_s5_lambda_re, v_s5_lambda_im, v_s5_log_dt, v_s5_b_re, v_s5_b_im, v_s5_c_re, v_s5_c_im, v_s5_d, v_s5_glu_w, v_s5_glu_b, v_hg_lb_logits, v_hg_norm_w, v_rg_conv_w, v_rg_conv_b, v_rg_wa, v_rg_ba, v_rg_wx, v_rg_bx, v_rg_lambda)
    return _step(x, loss_target, dict(zip(WEIGHTS, ws)), dict(zip(WEIGHTS, ms)), dict(zip(WEIGHTS, vs)))
```

```python
import functools
import math
from typing import NamedTuple

import jax
import jax.numpy as jnp
from jax import lax
from jax.experimental import pallas as pl
from jax.experimental.pallas import tpu as pltpu

F32 = jnp.float32
MMT = jnp.bfloat16
HI = lax.Precision.HIGHEST

D_MODEL = 1024
BW = 512
S5_GROUP, S5_GROUPS, S5_STATE = 16, 32, 64
S5_N = S5_GROUPS * S5_STATE
HG_HEADS, HG_D = 4, 128
HG_CHUNK = 64
RG_BLOCKS, RG_BLOCK = 8, 64
RG_C = 8.0
CONV_W = 4
D_FF = 2816
EPS = 1e-6
IN_TOTAL = 6656
NSH = 4
NSEG = 8
LANE = 128
VMEM_LIMIT = 56 * 1024 * 1024

ADAM_LR, ADAM_B1, ADAM_B2, ADAM_EPS, ADAM_WD, ADAM_STEP = 0.001, 0.9, 0.999, 1e-08, 0.01, 10

MESH = pl.DeviceIdType.MESH


class WP(NamedTuple):
    w: jax.Array
    p: jax.Array


def _dg(a, b, ca, cb):
    return lax.dot_general(a, b, (((ca,), (cb,)), ((), ())), preferred_element_type=F32)


@jax.custom_vjp
def _mmw(a, w, p):
    return _dg(a.astype(MMT), w, 1, 0)


def _mmw_fwd(a, w, p):
    return _mmw(a, w, p), (a, w)


def _mmw_bwd(res, g):
    a, w = res
    gb = g.astype(MMT)
    return _dg(gb, w, 1, 1), jnp.zeros_like(w), _dg(a.astype(MMT), gb, 0, 0)


_mmw.defvjp(_mmw_fwd, _mmw_bwd)


def mm(a, w):
    if isinstance(w, WP):
        return _mmw(a, w.w, w.p)
    return _dg(a.astype(MMT), w, 1, 0)


@jax.custom_vjp
def mma_nn(a, b):
    return _dg(a.astype(MMT), b.astype(MMT), 1, 0)


def _nn_f(a, b):
    return mma_nn(a, b), (a, b)


def _nn_b(res, g):
    a, b = res
    gb = g.astype(MMT)
    return _dg(gb, b.astype(MMT), 1, 1), _dg(a.astype(MMT), gb, 0, 0)


mma_nn.defvjp(_nn_f, _nn_b)


@jax.custom_vjp
def mma_nt(a, b):
    return _dg(a.astype(MMT), b.astype(MMT), 1, 1)


def _nt_f(a, b):
    return mma_nt(a, b), (a, b)


def _nt_b(res, g):
    a, b = res
    gb = g.astype(MMT)
    return _dg(gb, b.astype(MMT), 1, 0), _dg(gb, a.astype(MMT), 0, 0)


mma_nt.defvjp(_nt_f, _nt_b)


@jax.custom_vjp
def mma_tn(a, b):
    return _dg(a.astype(MMT), b.astype(MMT), 0, 0)


def _tn_f(a, b):
    return mma_tn(a, b), (a, b)


def _tn_b(res, g):
    a, b = res
    gb = g.astype(MMT)
    return _dg(b.astype(MMT), gb, 1, 1), _dg(a.astype(MMT), gb, 1, 0)


mma_tn.defvjp(_tn_f, _tn_b)


def mm_exact(m, x):
    return jnp.dot(m, x, precision=HI, preferred_element_type=F32)


def _rms(x, w):
    return x * lax.rsqrt(jnp.mean(x * x, axis=-1, keepdims=True) + EPS) * w


def _expm1(x):
    series = x * (1.0 + x * (1.0 / 2) * (1.0 + x * (1.0 / 3) * (1.0 + x * (1.0 / 4) * (1.0 + x * (1.0 / 5) * (1.0 + x * (1.0 / 6))))))
    return jnp.where(jnp.abs(x) < 0.1, series, jnp.exp(x) - 1.0)


def _bspec(block, fn, order):
    if order == "is":
        return pl.BlockSpec(block, lambda i, s: fn(s, i))
    return pl.BlockSpec(block, lambda s, i: fn(s, i))


def tile_fwd(fn, name, n_i, n_s, ins, outs):
    n_in = len(ins)

    def body(*refs):
        s = pl.program_id(1)
        res = fn(*[r[...] for r in refs[:n_in]], s)
        for o_ref, val, spec in zip(refs[n_in:], res, outs):
            if spec[4] and n_s > 1:
                @pl.when(s == 0)
                def _(o_ref=o_ref, val=val):
                    o_ref[...] = val.astype(o_ref.dtype)

                @pl.when(s != 0)
                def _(o_ref=o_ref, val=val):
                    o_ref[...] += val.astype(o_ref.dtype)
            else:
                o_ref[...] = val.astype(o_ref.dtype)

    return pl.pallas_call(
        body, grid=(n_i, n_s), name=name,
        in_specs=[_bspec(b, f, "is") for _, b, f in ins],
        out_specs=[_bspec(b, f, "is") for _, _, b, f, _ in outs],
        out_shape=[jax.ShapeDtypeStruct(sh, dt) for sh, dt, _, _, _ in outs],
        compiler_params=pltpu.CompilerParams(vmem_limit_bytes=VMEM_LIMIT,
                                             dimension_semantics=("arbitrary", "arbitrary")),
    )(*[a for a, _, _ in ins])


def tile_bwd(fn, name, n_i, n_s, ins, cts, gouts):
    n_in, n_ct = len(ins), len(cts)
    kinds = [k for _, _, _, k in ins]
    d_pos = [j for j, k in enumerate(kinds) if k != "c"]
    shared = [(gi, spec[4]) for gi, spec in enumerate(gouts) if len(spec) == 5 and spec[4] is not None]
    n_sh = len(shared)

    def body(*refs):
        s, i = pl.program_id(0), pl.program_id(1)
        vals = [r[...] for r in refs[:n_in]]
        ctv = tuple(r[...] for r in refs[n_in:n_in + n_ct])
        g_refs = refs[n_in + n_ct + n_sh:]

        def g(*dv):
            args = list(vals)
            for j, v in zip(d_pos, dv):
                args[j] = WP(vals[j], v) if kinds[j] == "w" else v
            return tuple(fn(*args))

        dv0 = [jnp.zeros(vals[j].shape, F32) if kinds[j] == "w" else vals[j] for j in d_pos]
        _, vjp = jax.vjp(g, *dv0)
        grads = vjp(ctv)
        for g_ref, gv, spec in zip(g_refs, grads, gouts):
            mode = spec[3]
            if mode == "write":
                g_ref[...] = gv.astype(g_ref.dtype)
            else:
                first = (i == 0) if mode == "acc_i" else jnp.logical_and(i == 0, s == 0)

                @pl.when(first)
                def _(g_ref=g_ref, gv=gv):
                    g_ref[...] = gv.astype(g_ref.dtype)

                @pl.when(jnp.logical_not(first))
                def _(g_ref=g_ref, gv=gv):
                    g_ref[...] += gv.astype(g_ref.dtype)

    return pl.pallas_call(
        body, grid=(n_s, n_i), name=name,
        in_specs=([_bspec(b, f, "si") for _, b, f, _ in ins] + [_bspec(b, f, "si") for _, b, f in cts]
                  + [pl.BlockSpec(memory_space=pl.ANY)] * n_sh),
        out_specs=[_bspec(spec[1], spec[2], "si") for spec in gouts],
        out_shape=[jax.ShapeDtypeStruct(spec[0], F32) for spec in gouts],
        input_output_aliases={n_in + n_ct + k: gi for k, (gi, _) in enumerate(shared)},
        compiler_params=pltpu.CompilerParams(vmem_limit_bytes=VMEM_LIMIT,
                                             dimension_semantics=("arbitrary", "arbitrary")),
    )(*[a for a, _, _, _ in ins], *[a for a, _, _ in cts], *[buf for _, buf in shared])


def _row_tile(rows, width, itemsize=4, budget=2 * 1024 * 1024, mult=8):
    best = mult
    for t in range(mult, rows + 1, mult):
        if rows % t == 0 and t * width * itemsize <= budget:
            best = t
    return best


def add_n(name, terms, shape):
    rows, cols = shape
    tr = _row_tile(rows, cols)

    def body(*refs):
        acc = refs[0][...]
        for r in refs[1:-1]:
            acc = acc + r[...]
        refs[-1][...] = acc

    specs = []
    for _, lead in terms:
        specs.append(pl.BlockSpec((None,) * len(lead) + (tr, cols), functools.partial(lambda i, lead: (*lead, i, 0), lead=lead)))
    return pl.pallas_call(
        body, grid=(rows // tr,), name=name, in_specs=specs,
        out_specs=pl.BlockSpec((tr, cols), lambda i: (i, 0)),
        out_shape=jax.ShapeDtypeStruct((rows, cols), F32),
    )(*[a for a, _ in terms])


def ffn_core(x, nw, wg, wu, wd):
    h = _rms(x, nw)
    return (0.5 * mm(jax.nn.silu(mm(h, wg)) * mm(h, wu), wd),)


def pre_core(x, nw, win):
    return (mm(_rms(x, nw), win),)


def _split_lanes(y):
    return jnp.stack([y[:, k * LANE:(k + 1) * LANE] for k in range(y.shape[1] // LANE)], axis=0)


def _join_lanes(y3):
    return jnp.concatenate([y3[k] for k in range(y3.shape[0])], axis=1)


def s5_pre_core(u, bmat):
    bu = mm(u, bmat)
    return _split_lanes(bu[:, :S5_N]), _split_lanes(bu[:, S5_N:])


def mid_core(xr, xi, u, o, g, hs, gc, hmat, cmat, d, gluw, glub, hgw):
    xs = jnp.concatenate([_join_lanes(xr), _join_lanes(xi)], axis=1)
    y = mm(xs, cmat) + d * u
    z = jax.nn.gelu(y)
    ya = z * jax.nn.sigmoid(mm(z, gluw) + glub)
    ms = mm_exact(o * o, hmat)
    yb = o * lax.rsqrt(ms + EPS) * hgw * jax.nn.silu(g)
    yc = hs * jax.nn.gelu(gc)
    return ya, yb, yc


def _sub(w, n):
    return WP(w.w[n], w.p[n]) if isinstance(w, WP) else w[n]


def merge_core(ya, yb, yc, g0, g1, g2, p, wout):
    m = (jax.nn.sigmoid(g0) * mm(ya, _sub(p, 0)) + jax.nn.sigmoid(g1) * mm(yb, _sub(p, 1))
         + jax.nn.sigmoid(g2) * mm(yc, _sub(p, 2)))
    return (mm(m, wout),)


def gates_core(xc, wa, ba, wx, bx, lam):
    r = jax.nn.sigmoid(mm(xc, wa) + ba)
    i = jax.nn.sigmoid(mm(xc, wx) + bx)
    log_a = -RG_C * jax.nn.softplus(-lam) * r
    a = jnp.exp(log_a)
    b = jnp.sqrt(-_expm1(2.0 * log_a)) * (i * xc)
    return a, b


def _seg_rows(ref, k, j, n):
    rows = pl.ds(pl.multiple_of(j * NSEG, NSEG), NSEG)
    if k is None:
        return ref[rows, :]
    return ref[k, rows, :]


def _seg_store(ref, k, j, n, val):
    rows = pl.ds(pl.multiple_of(j * NSEG, NSEG), NSEG)
    if k is None:
        ref[rows, :] = val
    else:
        ref[k, rows, :] = val


def _seg_carries(er, ei, pr, pi, reverse):
    rows = lax.broadcasted_iota(jnp.int32, er.shape, 0)
    cr = jnp.zeros_like(er)
    ci = None if ei is None else jnp.zeros_like(er)
    order = range(NSEG - 2, -1, -1) if reverse else range(1, NSEG)
    shift = NSEG - 1 if reverse else 1
    for s in order:
        if ei is None:
            tr = er + pr * cr
            cr = jnp.where(rows == s, pltpu.roll(tr, shift, 0), cr)
        else:
            tr = er + pr * cr - pi * ci
            ti = ei + pr * ci + pi * cr
            cr = jnp.where(rows == s, pltpu.roll(tr, shift, 0), cr)
            ci = jnp.where(rows == s, pltpu.roll(ti, shift, 0), ci)
    return cr, ci


S5_K = 2


def s5_scan_fwd(bur, bui, ar, ai, L):
    n = L // NSEG
    nb = S5_N // LANE
    K = S5_K

    def body(br_ref, bi_ref, ar_ref, ai_ref, xr_ref, xi_ref):
        zero = jnp.zeros((NSEG, LANE), F32)
        A = [(jnp.broadcast_to(ar_ref[k], (NSEG, LANE)), jnp.broadcast_to(ai_ref[k], (NSEG, LANE))) for k in range(K)]

        def p1(j, st):
            new = []
            for k in range(K):
                sr, si, pr, pi = st[k]
                a_r, a_i = A[k]
                nr = a_r * sr - a_i * si + _seg_rows(br_ref, k, j, n)
                ni = a_r * si + a_i * sr + _seg_rows(bi_ref, k, j, n)
                _seg_store(xr_ref, k, j, n, nr)
                _seg_store(xi_ref, k, j, n, ni)
                new.append((nr, ni, a_r * pr - a_i * pi, a_r * pi + a_i * pr))
            return tuple(new)

        st = lax.fori_loop(0, n, p1, tuple((zero, zero, zero + 1.0, zero) for _ in range(K)))
        C = [_seg_carries(st[k][0], st[k][1], st[k][2], st[k][3], False) for k in range(K)]

        def p2(j, st):
            new = []
            for k in range(K):
                pr, pi = st[k]
                a_r, a_i = A[k]
                pr, pi = a_r * pr - a_i * pi, a_r * pi + a_i * pr
                cr, ci = C[k]
                _seg_store(xr_ref, k, j, n, _seg_rows(xr_ref, k, j, n) + pr * cr - pi * ci)
                _seg_store(xi_ref, k, j, n, _seg_rows(xi_ref, k, j, n) + pr * ci + pi * cr)
                new.append((pr, pi))
            return tuple(new)

        lax.fori_loop(0, n, p2, tuple((zero + 1.0, zero) for _ in range(K)))

    blk = pl.BlockSpec((K, L, LANE), lambda g: (g, 0, 0))
    ablk = pl.BlockSpec((K, 1, LANE), lambda g: (g, 0, 0))
    return pl.pallas_call(
        body, grid=(nb // K,), name="s5_scan_fwd",
        in_specs=[blk, blk, ablk, ablk], out_specs=[blk, blk],
        out_shape=[jax.ShapeDtypeStruct((nb, L, LANE), F32)] * 2,
        compiler_params=pltpu.CompilerParams(vmem_limit_bytes=VMEM_LIMIT),
    )(bur, bui, ar, ai)


def s5_scan_bwd(dxr, dxi, xr, xi, ar, ai, L):
    n = L // NSEG
    nb = S5_N // LANE
    K = S5_K

    def body(dr_ref, di_ref, xr_ref, xi_ref, ar_ref, ai_ref, gr_ref, gi_ref, dar_ref, dai_ref):
        zero = jnp.zeros((NSEG, LANE), F32)
        rows = lax.broadcasted_iota(jnp.int32, (NSEG, LANE), 0)
        A = [(jnp.broadcast_to(ar_ref[k], (NSEG, LANE)), -jnp.broadcast_to(ai_ref[k], (NSEG, LANE))) for k in range(K)]

        def p1(jj, st):
            j = n - 1 - jj
            new = []
            for k in range(K):
                sr, si, pr, pi = st[k]
                a_r, a_i = A[k]
                nr = a_r * sr - a_i * si + _seg_rows(dr_ref, k, j, n)
                ni = a_r * si + a_i * sr + _seg_rows(di_ref, k, j, n)
                _seg_store(gr_ref, k, j, n, nr)
                _seg_store(gi_ref, k, j, n, ni)
                new.append((nr, ni, a_r * pr - a_i * pi, a_r * pi + a_i * pr))
            return tuple(new)

        st = lax.fori_loop(0, n, p1, tuple((zero, zero, zero + 1.0, zero) for _ in range(K)))
        C = [_seg_carries(st[k][0], st[k][1], st[k][2], st[k][3], True) for k in range(K)]
        xb = [(jnp.where(rows == 0, 0.0, pltpu.roll(_seg_rows(xr_ref, k, n - 1, n), 1, 0)),
               jnp.where(rows == 0, 0.0, pltpu.roll(_seg_rows(xi_ref, k, n - 1, n), 1, 0))) for k in range(K)]

        def p2(jj, st):
            j = n - 1 - jj
            jp = jnp.maximum(j - 1, 0)
            new = []
            for k in range(K):
                pr, pi, acr, aci = st[k]
                a_r, a_i = A[k]
                pr, pi = a_r * pr - a_i * pi, a_r * pi + a_i * pr
                cr, ci = C[k]
                g_r = _seg_rows(gr_ref, k, j, n) + pr * cr - pi * ci
                g_i = _seg_rows(gi_ref, k, j, n) + pr * ci + pi * cr
                _seg_store(gr_ref, k, j, n, g_r)
                _seg_store(gi_ref, k, j, n, g_i)
                xpr = jnp.where(j == 0, xb[k][0], _seg_rows(xr_ref, k, jp, n))
                xpi = jnp.where(j == 0, xb[k][1], _seg_rows(xi_ref, k, jp, n))
                new.append((pr, pi, acr + g_r * xpr + g_i * xpi, aci + g_i * xpr - g_r * xpi))
            return tuple(new)

        st = lax.fori_loop(0, n, p2, tuple((zero + 1.0, zero, zero, zero) for _ in range(K)))
        for k in range(K):
            dar_ref[k] = jnp.sum(st[k][2], axis=0, keepdims=True)
            dai_ref[k] = jnp.sum(st[k][3], axis=0, keepdims=True)

    blk = pl.BlockSpec((K, L, LANE), lambda g: (g, 0, 0))
    ablk = pl.BlockSpec((K, 1, LANE), lambda g: (g, 0, 0))
    return pl.pallas_call(
        body, grid=(nb // K,), name="s5_scan_bwd",
        in_specs=[blk, blk, blk, blk, ablk, ablk], out_specs=[blk, blk, ablk, ablk],
        out_shape=[jax.ShapeDtypeStruct((nb, L, LANE), F32)] * 2 + [jax.ShapeDtypeStruct((nb, 1, LANE), F32)] * 2,
        compiler_params=pltpu.CompilerParams(vmem_limit_bytes=VMEM_LIMIT),
    )(dxr, dxi, xr, xi, ar, ai)


def rg_scan_fwd(a, b, L):
    n = L // NSEG

    def body(a_ref, b_ref, h_ref):
        zero = jnp.zeros((NSEG, LANE), F32)

        def p1(j, st):
            h, p = st
            aj = _seg_rows(a_ref, None, j, n)
            h = aj * h + _seg_rows(b_ref, None, j, n)
            _seg_store(h_ref, None, j, n, h)
            return h, aj * p

        e, pe = lax.fori_loop(0, n, p1, (zero, zero + 1.0))
        c, _ = _seg_carries(e, None, pe, None, False)

        def p2(j, p):
            p = _seg_rows(a_ref, None, j, n) * p
            _seg_store(h_ref, None, j, n, _seg_rows(h_ref, None, j, n) + p * c)
            return p

        lax.fori_loop(0, n, p2, zero + 1.0)

    blk = pl.BlockSpec((L, LANE), lambda g: (0, g))
    return pl.pallas_call(
        body, grid=(BW // LANE,), name="rg_scan_fwd", in_specs=[blk, blk], out_specs=blk,
        out_shape=jax.ShapeDtypeStruct((L, BW), F32),
        compiler_params=pltpu.CompilerParams(vmem_limit_bytes=VMEM_LIMIT),
    )(a, b)


def rg_scan_bwd(a, h, dh, L):
    n = L // NSEG

    def body(a_ref, h_ref, dh_ref, da_ref, db_ref):
        zero = jnp.zeros((NSEG, LANE), F32)
        rows = lax.broadcasted_iota(jnp.int32, (NSEG, LANE), 0)
        a_edge = jnp.where(rows == NSEG - 1, 0.0, pltpu.roll(_seg_rows(a_ref, None, 0, n), NSEG - 1, 0))
        h_edge = jnp.where(rows == 0, 0.0, pltpu.roll(_seg_rows(h_ref, None, n - 1, n), 1, 0))

        def mult(j):
            return jnp.where(j == n - 1, a_edge, _seg_rows(a_ref, None, jnp.minimum(j + 1, n - 1), n))

        def p1(jj, st):
            j = n - 1 - jj
            g, p = st
            m = mult(j)
            g = m * g + _seg_rows(dh_ref, None, j, n)
            _seg_store(db_ref, None, j, n, g)
            return g, m * p

        e, pe = lax.fori_loop(0, n, p1, (zero, zero + 1.0))
        c, _ = _seg_carries(e, None, pe, None, True)

        def p2(jj, p):
            j = n - 1 - jj
            p = mult(j) * p
            g = _seg_rows(db_ref, None, j, n) + p * c
            _seg_store(db_ref, None, j, n, g)
            hp = jnp.where(j == 0, h_edge, _seg_rows(h_ref, None, jnp.maximum(j - 1, 0), n))
            _seg_store(da_ref, None, j, n, g * hp)
            return p

        lax.fori_loop(0, n, p2, zero + 1.0)

    blk = pl.BlockSpec((L, LANE), lambda g: (0, g))
    return pl.pallas_call(
        body, grid=(BW // LANE,), name="rg_scan_bwd", in_specs=[blk, blk, blk], out_specs=[blk, blk],
        out_shape=[jax.ShapeDtypeStruct((L, BW), F32)] * 2,
        compiler_params=pltpu.CompilerParams(vmem_limit_bytes=VMEM_LIMIT),
    )(a, h, dh)


def _hg_consts(C):
    t = lax.broadcasted_iota(jnp.int32, (C, C), 0)
    s = lax.broadcasted_iota(jnp.int32, (C, C), 1)
    tril = (s <= t).astype(F32)
    diag = (s == t).astype(F32)
    levels = []
    k = 1
    while (1 << k) <= C:
        m = 1 << (k - 1)
        same = (t >> k) == (s >> k)
        t_right = ((t >> (k - 1)) & 1) == 1
        s_left = ((s >> (k - 1)) & 1) == 0
        mask = jnp.logical_and(same, jnp.logical_and(t_right, s_left)).astype(F32)
        bnd = ((t >> k) << k) + (m - 1)
        levels.append((mask, (s <= bnd).astype(F32)))
        k += 1
    return tril, diag, levels


def hg_chunk(st, q, z, v, lb):
    C = q.shape[0]
    tril, diag, levels = _hg_consts(C)
    sig = jax.nn.sigmoid(z)
    lf = jnp.log(lb + (1.0 - lb) * sig)
    k = (1.0 - lb) * jax.nn.sigmoid(-z)
    qh = jax.nn.silu(q)
    b = mm_exact(tril, lf)
    blast = jnp.sum(lf, axis=0, keepdims=True)
    qe = qh * jnp.exp(b)
    kd = k * jnp.exp(blast - b)
    scaled = []
    for _, sel in levels:
        ref = mm_exact(sel, lf)
        scaled.append((qh * jnp.exp(jnp.minimum(b - ref, 0.0)), k * jnp.exp(jnp.minimum(ref - b, 0.0))))
    outs, news = [], []
    for h in range(HG_HEADS):
        sl = slice(h * HG_D, (h + 1) * HG_D)
        st_h = st[h * HG_D:(h + 1) * HG_D, :]
        sc = diag * mma_nt(qh[:, sl], k[:, sl])
        for (mask, _), (qt, kt) in zip(levels, scaled):
            sc = sc + mask * mma_nt(qt[:, sl], kt[:, sl])
        outs.append(mma_nt(qe[:, sl], st_h) + mma_nn(sc, v[:, sl]))
        news.append(st_h * jnp.exp(blast[:, sl]) + mma_tn(v[:, sl], kd[:, sl]))
    return jnp.concatenate(news, axis=0), jnp.concatenate(outs, axis=1)


def hg_fwd(qzv, lb, L):
    C = HG_CHUNK
    nc = L // C

    def body(q_ref, z_ref, v_ref, lb_ref, o_ref, sst_ref, st_ref):
        @pl.when(pl.program_id(0) == 0)
        def _():
            st_ref[...] = jnp.zeros_like(st_ref)

        st = st_ref[...]
        sst_ref[...] = st
        new, o = hg_chunk(st, q_ref[...], z_ref[...], v_ref[...], lb_ref[...])
        st_ref[...] = new
        o_ref[...] = o

    col = lambda cb: pl.BlockSpec((C, BW), functools.partial(lambda c, cb: (c, cb), cb=cb))
    return pl.pallas_call(
        body, grid=(nc,), name="hg_fwd",
        in_specs=[col(0), col(1), col(2), pl.BlockSpec((1, BW), lambda c: (0, 0))],
        out_specs=[pl.BlockSpec((C, BW), lambda c: (c, 0)), pl.BlockSpec((None, BW, HG_D), lambda c: (c, 0, 0))],
        out_shape=[jax.ShapeDtypeStruct((L, BW), F32), jax.ShapeDtypeStruct((nc, BW, HG_D), F32)],
        scratch_shapes=[pltpu.VMEM((BW, HG_D), F32)],
        compiler_params=pltpu.CompilerParams(vmem_limit_bytes=VMEM_LIMIT, dimension_semantics=("arbitrary",)),
    )(qzv, qzv, qzv, lb)


def hg_bwd(qzv, lb, sst, do, L):
    C = HG_CHUNK
    nc = L // C

    def body(q_ref, z_ref, v_ref, lb_ref, sst_ref, do_ref, dq_ref, dz_ref, dv_ref, dlb_ref, dst_ref):
        @pl.when(pl.program_id(0) == 0)
        def _():
            dst_ref[...] = jnp.zeros_like(dst_ref)
            dlb_ref[...] = jnp.zeros_like(dlb_ref)

        _, vjp = jax.vjp(hg_chunk, sst_ref[...], q_ref[...], z_ref[...], v_ref[...], lb_ref[...])
        dst, dq, dz, dv, dlb = vjp((dst_ref[...], do_ref[...]))
        dst_ref[...] = dst
        dq_ref[...] = dq
        dz_ref[...] = dz
        dv_ref[...] = dv
        dlb_ref[...] += dlb

    col = lambda cb: pl.BlockSpec((C, BW), functools.partial(lambda c, cb: (nc - 1 - c, cb), cb=cb))
    rev = pl.BlockSpec((C, BW), lambda c: (nc - 1 - c, 0))
    return pl.pallas_call(
        body, grid=(nc,), name="hg_bwd",
        in_specs=[col(0), col(1), col(2), pl.BlockSpec((1, BW), lambda c: (0, 0)),
                  pl.BlockSpec((None, BW, HG_D), lambda c: (nc - 1 - c, 0, 0)), rev],
        out_specs=[rev, rev, rev, pl.BlockSpec((1, BW), lambda c: (0, 0))],
        out_shape=[jax.ShapeDtypeStruct((L, BW), F32)] * 3 + [jax.ShapeDtypeStruct((1, BW), F32)],
        scratch_shapes=[pltpu.VMEM((BW, HG_D), F32)],
        compiler_params=pltpu.CompilerParams(vmem_limit_bytes=VMEM_LIMIT, dimension_semantics=("arbitrary",)),
    )(qzv, qzv, qzv, lb, sst, do)


def _shift_down(x, d, rows, L):
    if d == 0:
        return x
    wrapped = jnp.where((rows & (NSEG - 1)) == 0, 0.0, pltpu.roll(x, NSEG * d + 1, 0))
    return jnp.where(rows < NSEG * d, wrapped, pltpu.roll(x, NSEG * d, 0))


def _shift_up(x, d, rows, L):
    if d == 0:
        return x
    wrapped = jnp.where((rows & (NSEG - 1)) == NSEG - 1, 0.0, pltpu.roll(x, L - (NSEG * d + 1), 0))
    return jnp.where(rows >= L - NSEG * d, wrapped, pltpu.roll(x, L - NSEG * d, 0))


def conv_fwd(proj, w, b, L):
    def body(x_ref, w_ref, b_ref, o_ref):
        x = x_ref[...]
        rows = lax.broadcasted_iota(jnp.int32, x.shape, 0)
        acc = jnp.broadcast_to(b_ref[...], x.shape)
        for k in range(CONV_W):
            acc = acc + w_ref[pl.ds(k, 1), :] * _shift_down(x, CONV_W - 1 - k, rows, L)
        o_ref[...] = acc

    nl = BW // LANE
    return pl.pallas_call(
        body, grid=(nl,), name="conv_fwd",
        in_specs=[pl.BlockSpec((L, LANE), lambda g: (0, 5 * nl + g)), pl.BlockSpec((CONV_W, LANE), lambda g: (0, g)),
                  pl.BlockSpec((1, LANE), lambda g: (0, g))],
        out_specs=pl.BlockSpec((L, LANE), lambda g: (0, g)),
        out_shape=jax.ShapeDtypeStruct((L, BW), F32),
        compiler_params=pltpu.CompilerParams(vmem_limit_bytes=VMEM_LIMIT),
    )(proj, w, b)


def conv_bwd(proj, w, dxc, L):
    def body(x_ref, w_ref, d_ref, dx_ref, dw_ref, db_ref):
        x, d = x_ref[...], d_ref[...]
        rows = lax.broadcasted_iota(jnp.int32, x.shape, 0)
        acc = jnp.zeros_like(x)
        for k in range(CONV_W):
            acc = acc + w_ref[pl.ds(k, 1), :] * _shift_up(d, CONV_W - 1 - k, rows, L)
            dw_ref[pl.ds(k, 1), :] = jnp.sum(d * _shift_down(x, CONV_W - 1 - k, rows, L), axis=0, keepdims=True)
        dx_ref[...] = acc
        db_ref[...] = jnp.sum(d, axis=0, keepdims=True)

    nl = BW // LANE
    blk = pl.BlockSpec((L, LANE), lambda g: (0, g))
    return pl.pallas_call(
        body, grid=(nl,), name="conv_bwd",
        in_specs=[pl.BlockSpec((L, LANE), lambda g: (0, 5 * nl + g)), pl.BlockSpec((CONV_W, LANE), lambda g: (0, g)), blk],
        out_specs=[blk, pl.BlockSpec((CONV_W, LANE), lambda g: (0, g)), pl.BlockSpec((1, LANE), lambda g: (0, g))],
        out_shape=[jax.ShapeDtypeStruct((L, BW), F32), jax.ShapeDtypeStruct((CONV_W, BW), F32),
                   jax.ShapeDtypeStruct((1, BW), F32)],
        compiler_params=pltpu.CompilerParams(vmem_limit_bytes=VMEM_LIMIT),
    )(proj, w, dxc)


def loss_fwd_bwd(x, fw, target, L, tm):
    def fn(x, fw, t):
        err = jnp.square(_rms(x, fw) - t)
        return jnp.sum(0.5 * jnp.mean(err, axis=-1, keepdims=True), axis=0, keepdims=True)

    def body(x_ref, fw_ref, t_ref, l_ref, dx_ref, dfw_ref):
        i = pl.program_id(0)
        t = t_ref[...]
        val, vjp = jax.vjp(lambda x, fw: fn(x, fw, t), x_ref[...], fw_ref[...])
        dx, dfw = vjp(jnp.ones((1, 1), F32))
        dx_ref[...] = dx

        @pl.when(i == 0)
        def _():
            l_ref[...] = jnp.zeros_like(l_ref)
            dfw_ref[...] = jnp.zeros_like(dfw_ref)

        l_ref[...] += jnp.broadcast_to(val, l_ref.shape)
        dfw_ref[...] += dfw

    row = pl.BlockSpec((tm, D_MODEL), lambda i: (i, 0))
    vec = pl.BlockSpec((1, D_MODEL), lambda i: (0, 0))
    return pl.pallas_call(
        body, grid=(L // tm,), name="loss_fwd_bwd", in_specs=[row, vec, row],
        out_specs=[pl.BlockSpec((1, LANE), lambda i: (0, 0)), row, vec],
        out_shape=[jax.ShapeDtypeStruct((1, LANE), F32), jax.ShapeDtypeStruct((L, D_MODEL), F32),
                   jax.ShapeDtypeStruct((1, D_MODEL), F32)],
        compiler_params=pltpu.CompilerParams(vmem_limit_bytes=VMEM_LIMIT, dimension_semantics=("arbitrary",)),
    )(x, fw, target)


def adamw(w, g, m, v):
    rows, cols = w.shape
    tr = _row_tile(rows, cols, budget=1024 * 1024)
    c1 = 1.0 - ADAM_B1 ** ADAM_STEP
    c2 = 1.0 - ADAM_B2 ** ADAM_STEP

    def body(w_ref, g_ref, m_ref, v_ref, d_ref, nm_ref, nv_ref):
        g = g_ref[...]
        nm = ADAM_B1 * m_ref[...] + (1.0 - ADAM_B1) * g
        nv = ADAM_B2 * v_ref[...] + (1.0 - ADAM_B2) * jnp.square(g)
        d_ref[...] = -ADAM_LR * ((nm / c1) / (jnp.sqrt(nv / c2) + ADAM_EPS) + ADAM_WD * w_ref[...])
        nm_ref[...] = nm
        nv_ref[...] = nv

    blk = pl.BlockSpec((tr, cols), lambda i: (i, 0))
    return pl.pallas_call(
        body, grid=(rows // tr,), name="adamw", in_specs=[blk] * 4, out_specs=[blk] * 3,
        out_shape=[jax.ShapeDtypeStruct((rows, cols), F32)] * 3,
    )(w, g, m, v)


def s5_prep(lam_re, lam_im, log_dt, b_re, b_im, c_re, c_im):
    lr = jnp.minimum(lam_re, -1e-4)
    li = lam_im
    dt = jnp.exp(log_dt)[:, None]
    mag = jnp.exp(lr * dt)
    ar = mag * jnp.cos(li * dt)
    ai = mag * jnp.sin(li * dt)
    den = lr * lr + li * li
    fr = ((ar - 1.0) * lr + ai * li) / den
    fi = (ai * lr - (ar - 1.0) * li) / den
    bbr = fr[..., None] * b_re - fi[..., None] * b_im
    bbi = fr[..., None] * b_im + fi[..., None] * b_re
    eye = jnp.eye(S5_GROUPS, dtype=F32)
    emb_b = lambda bb: jnp.einsum("gpc,gh->gchp", bb, eye).reshape(BW, S5_N)
    emb_c = lambda cc: jnp.einsum("gcp,gh->gphc", cc, eye).reshape(S5_N, BW)
    bmat = jnp.concatenate([emb_b(bbr), emb_b(bbi)], axis=1)
    cmat = jnp.concatenate([emb_c(c_re), -emb_c(c_im)], axis=0)
    nb = S5_N // LANE
    return ar.reshape(nb, 1, LANE), ai.reshape(nb, 1, LANE), bmat, cmat


def rg_prep(w):
    return jnp.einsum("hij,hk->hikj", w, jnp.eye(RG_BLOCKS, dtype=F32)).reshape(BW, BW)


def hg_prep(logits):
    p = jax.nn.softmax(logits, axis=0)
    return jnp.cumsum(p, axis=0) - p[0]


def _head_mean_matrix():
    r = jnp.arange(BW) // HG_D
    return (r[:, None] == r[None, :]).astype(F32) / HG_D


def _to_segment_order(a):
    L = a.shape[0]
    return a.reshape(NSEG, L // NSEG, -1).transpose(1, 0, 2).reshape(a.shape)


def _to_time_order(a):
    L = a.shape[0]
    return a.reshape(L // NSEG, NSEG, -1).transpose(1, 0, 2).reshape(a.shape)


def _const(*idx):
    return lambda s, i: idx


def _rows(cb=0):
    return lambda s, i: (i, cb)


def _sum_parts(name, first, parts, shape):
    return add_n(name, [(first, ())] + [(parts, (s,)) for s in range(NSH)], shape)


def ffn_fwd(name, x, W, l, j, k, L, tm):
    D = D_MODEL
    fn = lambda x, nw, wg, wu, wd, s: (jnp.where(s == 0, x, 0.0) + ffn_core(x, nw, wg, wu, wd)[0],)
    ins = [(x, (tm, D), _rows()),
           (W["nw"], (None, None, 1, D), _const(l, k, 0, 0)),
           (W["wg"], (None, None, None, D, D_FF // NSH), lambda s, i: (l, s, j, 0, 0)),
           (W["wu"], (None, None, None, D, D_FF // NSH), lambda s, i: (l, s, j, 0, 0)),
           (W["wd"], (None, None, None, D_FF // NSH, D), lambda s, i: (l, s, j, 0, 0))]
    return tile_fwd(fn, name, L // tm, NSH, ins, [((L, D), F32, (tm, D), _rows(), True)])[0]


def ffn_bwd(name, x, dy, W, bufs, l, j, k, L, tm):
    D, F = D_MODEL, D_FF // NSH
    stk = lambda key, r, c: ((2 * DEPTH, NSH, r, c), (None, None, r, c), lambda s, i: (2 * l + j, s, 0, 0), "acc_i",
                             bufs.get(key))
    ins = [(x, (tm, D), _rows(), "r"),
           (W["nw"], (None, None, 1, D), _const(l, k, 0, 0), "p"),
           (W["wg"], (None, None, None, D, F), lambda s, i: (l, s, j, 0, 0), "w"),
           (W["wu"], (None, None, None, D, F), lambda s, i: (l, s, j, 0, 0), "w"),
           (W["wd"], (None, None, None, F, D), lambda s, i: (l, s, j, 0, 0), "w")]
    gouts = [((NSH, L, D), (None, tm, D), lambda s, i: (s, i, 0), "write"),
             ((1, D), (1, D), _const(0, 0), "acc_all"),
             stk("ffn_gate", D, F), stk("ffn_up", D, F), stk("ffn_down", F, D)]
    part, dnw, bufs["ffn_gate"], bufs["ffn_up"], bufs["ffn_down"] = tile_bwd(
        ffn_core, name, L // tm, NSH, ins, [(dy, (tm, D), _rows())], gouts)
    return _sum_parts(name + "_dx", dy, part, (L, D)), dnw


def layer_fwd(l, x0, W, P, L, tm):
    D = D_MODEL
    n_i = L // tm
    x1 = ffn_fwd(f"ffn_fwd_{l}0", x0, W, l, 0, 0, L, tm)
    proj = tile_fwd(
        lambda x, nw, win, s: pre_core(x, nw, win), f"pre_fwd_{l}", n_i, NSH,
        [(x1, (tm, D), _rows()), (W["nw"], (None, None, 1, D), _const(l, 1, 0, 0)),
         (W["win"], (None, None, D, IN_TOTAL // NSH), lambda s, i: (l, s, 0, 0))],
        [((L, IN_TOTAL), F32, (tm, IN_TOTAL // NSH), lambda s, i: (i, s), False)])[0]
    nb = S5_N // LANE
    blk3 = lambda s, i: (0, i, 0)
    bur, bui = tile_fwd(
        lambda u, bmat, s: s5_pre_core(u, bmat), f"s5pre_fwd_{l}", n_i, 1,
        [(proj, (tm, BW), _rows(0)), (P["bmat"][l], (BW, 2 * S5_N), _const(0, 0))],
        [((nb, L, LANE), F32, (nb, tm, LANE), blk3, False)] * 2)
    xr, xi = s5_scan_fwd(bur, bui, P["ar"][l], P["ai"][l], L)
    qzv = _to_time_order(proj[:, BW:4 * BW])
    o_t, sst = hg_fwd(qzv, P["lb"][l], L)
    o = _to_segment_order(o_t)
    xc = conv_fwd(proj, W["convw"][l], P["convb"][l], L)
    vec = (None, 1, BW)
    a, b = tile_fwd(
        lambda xc, wa, ba, wx, bx, lam, s: gates_core(xc, wa, ba, wx, bx, lam), f"gates_fwd_{l}", n_i, 1,
        [(xc, (tm, BW), _rows()), (P["wa"][l], (BW, BW), _const(0, 0)), (P["ba"], vec, _const(l, 0, 0)),
         (P["wx"][l], (BW, BW), _const(0, 0)), (P["bx"], vec, _const(l, 0, 0)), (P["lam"], vec, _const(l, 0, 0))],
        [((L, BW), F32, (tm, BW), _rows(), False)] * 2)
    hs = rg_scan_fwd(a, b, L)
    tmm = min(tm, 128)
    ya, yb, yc = tile_fwd(
        lambda *a: mid_core(*a[:-1]), f"mid_fwd_{l}", L // tmm, 1,
        [(xr, (nb, tmm, LANE), blk3), (xi, (nb, tmm, LANE), blk3), (proj, (tmm, BW), _rows(0)), (o, (tmm, BW), _rows()),
         (proj, (tmm, BW), _rows(4)), (hs, (tmm, BW), _rows()), (proj, (tmm, BW), _rows(6)),
         (P["hmat"], (BW, BW), _const(0, 0)), (P["cmat"][l], (2 * S5_N, BW), _const(0, 0)), (P["d"], vec, _const(l, 0, 0)),
         (W["gluw"], (None, BW, BW), _const(l, 0, 0)), (P["glub"], vec, _const(l, 0, 0)), (P["hgw"], vec, _const(l, 0, 0))],
        [((L, BW), F32, (tmm, BW), _rows(), False)] * 3)
    dq = D // NSH
    gm = lambda n: (proj, (tm, dq), functools.partial(lambda s, i, n: (i, 14 + 4 * n + s), n=n))
    pw = (W["p"], (None, None, 3, BW, dq), lambda s, i: (l, s, 0, 0, 0))
    x2 = tile_fwd(
        lambda x, ya, yb, yc, g0, g1, g2, p, wout, s:
            (jnp.where(s == 0, x, 0.0) + merge_core(ya, yb, yc, g0, g1, g2, p, wout)[0],),
        f"merge_fwd_{l}", n_i, NSH,
        [(x1, (tm, D), _rows()), (ya, (tm, BW), _rows()), (yb, (tm, BW), _rows()), (yc, (tm, BW), _rows()),
         gm(0), gm(1), gm(2), pw, (W["wout"], (None, None, dq, D), lambda s, i: (l, s, 0, 0))],
        [((L, D), F32, (tm, D), _rows(), True)])[0]
    x3 = ffn_fwd(f"ffn_fwd_{l}1", x2, W, l, 1, 2, L, tm)
    saved = dict(x0=x0, x1=x1, x2=x2, proj=proj, xr=xr, xi=xi, o=o, sst=sst, xc=xc, a=a, hs=hs, ya=ya, yb=yb, yc=yc,
                 qzv=qzv)
    return x3, saved


def layer_bwd(l, dx3, sv, W, P, bufs, L, tm):
    D = D_MODEL
    n_i = L // tm
    nb = S5_N // LANE
    dq = D // NSH
    vec = (None, 1, BW)
    vout = ((1, BW), (1, BW), _const(0, 0), "acc_all")
    blk3 = lambda s, i: (0, i, 0)
    small = {}
    proj = sv["proj"]

    dx2, dnw2 = ffn_bwd(f"ffn_bwd_{l}1", sv["x2"], dx3, W, bufs, l, 1, 2, L, tm)

    gm = lambda n: (proj, (tm, dq), functools.partial(lambda s, i, n: (i, 14 + 4 * n + s), n=n), "r")
    ypart = ((NSH, L, BW), (None, tm, BW), lambda s, i: (s, i, 0), "write")
    gpart = ((L, D), (tm, dq), lambda s, i: (i, s), "write")
    res = tile_bwd(
        merge_core, f"merge_bwd_{l}", n_i, NSH,
        [(sv["ya"], (tm, BW), _rows(), "r"), (sv["yb"], (tm, BW), _rows(), "r"), (sv["yc"], (tm, BW), _rows(), "r"),
         gm(0), gm(1), gm(2), (W["p"], (None, None, 3, BW, dq), lambda s, i: (l, s, 0, 0, 0), "w"),
         (W["wout"], (None, None, dq, D), lambda s, i: (l, s, 0, 0), "w")],
        [(dx2, (tm, D), _rows())],
        [ypart, ypart, ypart, gpart, gpart, gpart,
         ((DEPTH, NSH, 3, BW, dq), (None, None, 3, BW, dq), lambda s, i: (l, s, 0, 0, 0), "acc_i", bufs.get("branch_proj")),
         ((DEPTH, NSH, dq, D), (None, None, dq, D), lambda s, i: (l, s, 0, 0), "acc_i", bufs.get("w_out"))])
    pya, pyb, pyc, dg0, dg1, dg2, bufs["branch_proj"], bufs["w_out"] = res
    sum4 = lambda name, part: add_n(name, [(part, (s,)) for s in range(NSH)], (L, BW))
    dya, dyb, dyc = sum4(f"dya_{l}", pya), sum4(f"dyb_{l}", pyb), sum4(f"dyc_{l}", pyc)

    tmm = min(tm, 128)
    rw = ((L, BW), (tmm, BW), _rows(), "write")
    xw = ((nb, L, LANE), (nb, tmm, LANE), blk3, "write")
    res = tile_bwd(
        mid_core, f"mid_bwd_{l}", L // tmm, 1,
        [(sv["xr"], (nb, tmm, LANE), blk3, "r"), (sv["xi"], (nb, tmm, LANE), blk3, "r"), (proj, (tmm, BW), _rows(0), "r"),
         (sv["o"], (tmm, BW), _rows(), "r"), (proj, (tmm, BW), _rows(4), "r"), (sv["hs"], (tmm, BW), _rows(), "r"),
         (proj, (tmm, BW), _rows(6), "r"), (P["hmat"], (BW, BW), _const(0, 0), "c"),
         (P["cmat"][l], (2 * S5_N, BW), _const(0, 0), "w"), (P["d"], vec, _const(l, 0, 0), "p"),
         (W["gluw"], (None, BW, BW), _const(l, 0, 0), "w"), (P["glub"], vec, _const(l, 0, 0), "p"),
         (P["hgw"], vec, _const(l, 0, 0), "p")],
        [(dya, (tmm, BW), _rows()), (dyb, (tmm, BW), _rows()), (dyc, (tmm, BW), _rows())],
        [xw, xw, rw, rw, rw, rw, rw, ((2 * S5_N, BW), (2 * S5_N, BW), _const(0, 0), "acc_all"), vout,
         ((DEPTH, BW, BW), (None, BW, BW), _const(l, 0, 0), "acc_all", bufs.get("s5_glu_w")), vout, vout])
    dxr, dxi, du_skip, do, dg_b, dhs, dgate_c, dcmat, dd, bufs["s5_glu_w"], dglub, dhgw = res
    small["s5_d"], small["s5_glu_b"], small["hg_norm_w"] = dd[0], dglub[0], dhgw[0]

    da, db = rg_scan_bwd(sv["a"], sv["hs"], dhs, L)
    wmat = ((BW, BW), (BW, BW), _const(0, 0), "acc_all")
    res = tile_bwd(
        gates_core, f"gates_bwd_{l}", n_i, 1,
        [(sv["xc"], (tm, BW), _rows(), "r"), (P["wa"][l], (BW, BW), _const(0, 0), "w"), (P["ba"], vec, _const(l, 0, 0), "p"),
         (P["wx"][l], (BW, BW), _const(0, 0), "w"), (P["bx"], vec, _const(l, 0, 0), "p"), (P["lam"], vec, _const(l, 0, 0), "p")],
        [(da, (tm, BW), _rows()), (db, (tm, BW), _rows())],
        [((L, BW), (tm, BW), _rows(), "write"), wmat, vout, wmat, vout, vout])
    dxc, dwa, dba, dwx, dbx, dlam = res
    small["rg_ba"], small["rg_bx"], small["rg_lambda"] = dba[0], dbx[0], dlam[0]
    dx_c, dconvw, dconvb = conv_bwd(proj, W["convw"][l], dxc, L)
    small["rg_conv_w"], small["rg_conv_b"] = dconvw, dconvb[0]

    dq_b, dz_b, dv_b, dlb = hg_bwd(sv["qzv"], P["lb"][l], sv["sst"], _to_time_order(do), L)
    dqzv = _to_segment_order(jnp.concatenate([dq_b, dz_b, dv_b], axis=1))

    gr, gi, dar, dai = s5_scan_bwd(dxr, dxi, sv["xr"], sv["xi"], P["ar"][l], P["ai"][l], L)
    du_pre, dbmat = tile_bwd(
        s5_pre_core, f"s5pre_bwd_{l}", n_i, 1,
        [(proj, (tm, BW), _rows(0), "r"), (P["bmat"][l], (BW, 2 * S5_N), _const(0, 0), "w")],
        [(gr, (nb, tm, LANE), blk3), (gi, (nb, tm, LANE), blk3)],
        [((L, BW), (tm, BW), _rows(), "write"), ((BW, 2 * S5_N), (BW, 2 * S5_N), _const(0, 0), "acc_all")])
    du_a = add_n(f"du_a_{l}", [(du_skip, ()), (du_pre, ())], (L, BW))
    prep_ct = dict(dar=dar, dai=dai, dbmat=dbmat, dcmat=dcmat, dwa=dwa, dwx=dwx, dlb=dlb)

    dproj = jnp.concatenate([du_a, dqzv, dg_b, dx_c, dgate_c, dg0, dg1, dg2], axis=1)
    part, dnw1, bufs["w_in"] = tile_bwd(
        pre_core, f"pre_bwd_{l}", n_i, NSH,
        [(sv["x1"], (tm, D), _rows(), "r"), (W["nw"], (None, None, 1, D), _const(l, 1, 0, 0), "p"),
         (W["win"], (None, None, D, IN_TOTAL // NSH), lambda s, i: (l, s, 0, 0), "w")],
        [(dproj, (tm, IN_TOTAL // NSH), lambda s, i: (i, s))],
        [((NSH, L, D), (None, tm, D), lambda s, i: (s, i, 0), "write"), ((1, D), (1, D), _const(0, 0), "acc_all"),
         ((DEPTH, NSH, D, IN_TOTAL // NSH), (None, None, D, IN_TOTAL // NSH), lambda s, i: (l, s, 0, 0), "acc_i",
          bufs.get("w_in"))])
    dx1 = _sum_parts(f"pre_bwd_{l}_dx", dx2, part, (L, D))

    dx0, dnw0 = ffn_bwd(f"ffn_bwd_{l}0", sv["x0"], dx1, W, bufs, l, 0, 0, L, tm)
    small["norm_w"] = jnp.concatenate([dnw0, dnw1, dnw2], axis=0)
    return dx0, small, prep_ct


SMALL_RAW = ("s5_lambda_re", "s5_lambda_im", "s5_log_dt", "s5_b_re", "s5_b_im", "s5_c_re", "s5_c_im", "s5_d", "s5_glu_b",
             "hg_lb_logits", "hg_norm_w", "rg_conv_b", "rg_wa", "rg_ba", "rg_wx", "rg_bx", "rg_lambda", "final_norm_w")
DEPTH = 2


def local_step(x, target, W, raw):
    L = x.shape[0]
    tm = min(256, L)
    col = lambda v: v.reshape(DEPTH, 1, BW)
    s5_out, s5_vjp = [], []
    for l in range(DEPTH):
        out, vjp = jax.vjp(s5_prep, *[raw[k][l] for k in SMALL_RAW[:7]])
        s5_out.append(out)
        s5_vjp.append(vjp)
    (wa, wx), rg_vjp = jax.vjp(lambda a, b: (jax.vmap(rg_prep)(a), jax.vmap(rg_prep)(b)), raw["rg_wa"], raw["rg_wx"])
    lb, hg_vjp = jax.vjp(hg_prep, raw["hg_lb_logits"])
    P = dict(
        ar=[o[0] for o in s5_out], ai=[o[1] for o in s5_out],
        bmat=[o[2].astype(MMT) for o in s5_out], cmat=[o[3].astype(MMT) for o in s5_out],
        lb=[lb[l].reshape(1, BW) for l in range(DEPTH)], convb=[raw["rg_conv_b"][l].reshape(1, BW) for l in range(DEPTH)],
        wa=[wa[l].astype(MMT) for l in range(DEPTH)], wx=[wx[l].astype(MMT) for l in range(DEPTH)],
        ba=col(raw["rg_ba"]), bx=col(raw["rg_bx"]), lam=col(raw["rg_lambda"]), d=col(raw["s5_d"]),
        glub=col(raw["s5_glu_b"]), hgw=col(raw["hg_norm_w"]), hmat=_head_mean_matrix())

    saved = []
    h = _to_segment_order(x)
    for l in range(DEPTH):
        h, sv = layer_fwd(l, h, W, P, L, tm)
        saved.append(sv)
    loss, dh, dfw = loss_fwd_bwd(h, raw["final_norm_w"].reshape(1, D_MODEL), _to_segment_order(target), L, tm)

    big, per_layer, prep_cts = {}, [None] * DEPTH, [None] * DEPTH
    for l in reversed(range(DEPTH)):
        dh, sm, pc = layer_bwd(l, dh, saved[l], W, P, big, L, tm)
        per_layer[l], prep_cts[l] = sm, pc
    dh = _to_time_order(dh)

    small = {k: jnp.stack([per_layer[l][k] for l in range(DEPTH)]) for k in per_layer[0]}
    s5_g = [s5_vjp[l]((prep_cts[l]["dar"], prep_cts[l]["dai"], prep_cts[l]["dbmat"], prep_cts[l]["dcmat"])) for l in range(DEPTH)]
    for j, k in enumerate(SMALL_RAW[:7]):
        small[k] = jnp.stack([s5_g[l][j] for l in range(DEPTH)])
    small["rg_wa"], small["rg_wx"] = rg_vjp((jnp.stack([prep_cts[l]["dwa"] for l in range(DEPTH)]),
                                             jnp.stack([prep_cts[l]["dwx"] for l in range(DEPTH)])))
    (small["hg_lb_logits"],) = hg_vjp(jnp.concatenate([prep_cts[l]["dlb"] for l in range(DEPTH)], axis=0))
    small["final_norm_w"] = dfw[0]
    return loss, dh, big, small


ANY = pl.BlockSpec(memory_space=pl.ANY)


def _place():
    x, y, c = lax.axis_index("x"), lax.axis_index("y"), lax.axis_index("c")
    chips = [(1 - x, y), (x, 1 - y), (1 - x, 1 - y)]
    return x, y, c, chips


def _remote(src, dst, send, recv, k, to):
    return pltpu.make_async_remote_copy(src_ref=src, dst_ref=dst, send_sem=send.at[k], recv_sem=recv.at[k],
                                        device_id=to, device_id_type=MESH)


def _comm_call(body, name, ins, out_shapes, n_sem, n_loc):
    return pl.pallas_call(
        body, name=name, in_specs=[ANY] * len(ins), out_specs=[ANY] * len(out_shapes), out_shape=out_shapes,
        scratch_shapes=[pltpu.SemaphoreType.DMA((n_sem,)), pltpu.SemaphoreType.DMA((n_sem,)),
                        pltpu.SemaphoreType.DMA((max(n_loc, 1),))],
    )(*ins)


def gather_shards(name, shards):
    n = len(shards)
    per = 8

    def body(*refs):
        ins, outs = refs[:n], refs[n:2 * n]
        send, recv, _ = refs[2 * n:]
        x, y, c, chips = _place()
        me = 2 * x + y
        sib = (x, y, 1 - c)
        sends = []
        for w in range(n):
            for j, (cx, cy) in enumerate(chips):
                cp = _remote(ins[w].at[c], outs[w].at[c, me], send, recv, per * w + j, (cx, cy, c))
                cp.start()
                sends.append(cp)
        for w in range(n):
            for l in range(2):
                cp = _remote(ins[w].at[l], outs[w].at[l, me], send, recv, per * w + 6 + l, sib)
                cp.start()
                sends.append(cp)
        for w in range(n):
            for j, (cx, cy) in enumerate(chips):
                theirs = outs[w].at[c, 2 * cx + cy]
                _remote(ins[w].at[c], theirs, send, recv, per * w + j, (cx, cy, c)).wait_recv()
                cp = _remote(theirs, theirs, send, recv, per * w + 3 + j, sib)
                cp.start()
                sends.append(cp)
        for w in range(n):
            for j, (cx, cy) in enumerate(chips):
                dst = outs[w].at[1 - c, 2 * cx + cy]
                _remote(dst, dst, send, recv, per * w + 3 + j, sib).wait_recv()
            for l in range(2):
                dst = outs[w].at[l, me]
                _remote(dst, dst, send, recv, per * w + 6 + l, sib).wait_recv()
        for cp in sends:
            cp.wait_send()

    shapes = [jax.ShapeDtypeStruct((2, NSH) + s.shape[1:], s.dtype) for s in shards]
    return _comm_call(body, name, shards, shapes, per * n, 0)


def exchange_halves(name, grads):
    n = len(grads)

    def body(*refs):
        ins, outs = refs[:n], refs[n:2 * n]
        send, recv, _ = refs[2 * n:]
        x, y, c, _chips = _place()
        cps = []
        for w in range(n):
            h = grads[w].shape[2] // 2
            cp = _remote(ins[w].at[:, :, pl.ds((1 - c) * h, h)], outs[w], send, recv, w, (x, y, 1 - c))
            cp.start()
            cps.append(cp)
        for cp in cps:
            cp.wait()

    shapes = [jax.ShapeDtypeStruct(g.shape[:2] + (g.shape[2] // 2, g.shape[3]), g.dtype) for g in grads]
    return _comm_call(body, name, grads, shapes, n, 0)


def scatter_to_chips(name, halves):
    n = len(halves)

    def body(*refs):
        ins, outs = refs[:n], refs[n:2 * n]
        send, recv, _ = refs[2 * n:]
        x, y, c, chips = _place()
        cps = []
        for w in range(n):
            for j, (cx, cy) in enumerate(chips):
                cp = _remote(ins[w].at[:, 2 * cx + cy], outs[w].at[j], send, recv, 3 * w + j, (cx, cy, c))
                cp.start()
                cps.append(cp)
        for cp in cps:
            cp.wait()

    shapes = [jax.ShapeDtypeStruct((3, h.shape[0]) + h.shape[2:], h.dtype) for h in halves]
    return _comm_call(body, name, halves, shapes, 3 * n, 0)


def share_halves(name, pieces):
    n = len(pieces)

    def body(*refs):
        ins, outs = refs[:n], refs[n:2 * n]
        send, recv, _ = refs[2 * n:]
        x, y, c, _chips = _place()
        cps = []
        for w in range(n):
            cp = _remote(ins[w], outs[w], send, recv, w, (x, y, 1 - c))
            cp.start()
            cps.append(cp)
        for cp in cps:
            cp.wait()

    return _comm_call(body, name, pieces, [jax.ShapeDtypeStruct(p.shape, p.dtype) for p in pieces], n, 0)


def add_own_half(name, g, ra, c, wire):
    nblk, h, cols = ra.shape
    tr = _row_tile(h, cols, mult=16)
    nt = h // tr

    def body(c_ref, g_ref, r_ref, o_ref):
        o_ref[...] = (g_ref[...] + r_ref[...]).astype(o_ref.dtype)

    blk = (None, tr, cols)
    return pl.pallas_call(
        body, name=name,
        grid_spec=pltpu.PrefetchScalarGridSpec(
            num_scalar_prefetch=1, grid=(nblk, nt),
            in_specs=[pl.BlockSpec(blk, lambda s, i, c_ref: (s, c_ref[0] * nt + i, 0)), pl.BlockSpec(blk, lambda s, i, c_ref: (s, i, 0))],
            out_specs=pl.BlockSpec(blk, lambda s, i, c_ref: (s, i, 0))),
        out_shape=jax.ShapeDtypeStruct(ra.shape, wire),
    )(c.reshape(1), g, ra)


def add_chips(name, hb, rb, me):
    npc, _, h, cols = hb.shape
    tr = _row_tile(h, cols, mult=16)

    def body(me_ref, h_ref, r0, r1, r2, o_ref):
        f = lambda r: r[...].astype(F32)
        o_ref[...] = ((f(h_ref) + f(r0)) + f(r1)) + f(r2)

    rspec = lambda j: pl.BlockSpec((None, None, tr, cols), functools.partial(lambda p, i, me_ref, j: (j, p, i, 0), j=j))
    return pl.pallas_call(
        body, name=name,
        grid_spec=pltpu.PrefetchScalarGridSpec(
            num_scalar_prefetch=1, grid=(npc, h // tr),
            in_specs=[pl.BlockSpec((None, None, tr, cols), lambda p, i, me_ref: (p, me_ref[0], i, 0)), rspec(0), rspec(1), rspec(2)],
            out_specs=pl.BlockSpec((None, tr, cols), lambda p, i, me_ref: (p, i, 0))),
        out_shape=jax.ShapeDtypeStruct((npc, h, cols), F32),
    )(me.reshape(1), hb, rb, rb, rb)


def adamw_halves(name, w, m, v, own, other, c):
    npc, rows, cols = w.shape
    h = rows // 2
    tr = _row_tile(h, cols, budget=1024 * 1024)
    nt = h // tr
    c1 = 1.0 - ADAM_B1 ** ADAM_STEP
    c2 = 1.0 - ADAM_B2 ** ADAM_STEP

    def body(c_ref, w_ref, m_ref, v_ref, own_ref, oth_ref, g_ref, d_ref, nm_ref, nv_ref):
        g = jnp.where(pl.program_id(1) == c_ref[0], own_ref[...], oth_ref[...])
        nm = ADAM_B1 * m_ref[...] + (1.0 - ADAM_B1) * g
        nv = ADAM_B2 * v_ref[...] + (1.0 - ADAM_B2) * jnp.square(g)
        g_ref[...] = g
        d_ref[...] = -ADAM_LR * ((nm / c1) / (jnp.sqrt(nv / c2) + ADAM_EPS) + ADAM_WD * w_ref[...])
        nm_ref[...] = nm
        nv_ref[...] = nv

    full = pl.BlockSpec((None, tr, cols), lambda p, hh, i, c_ref: (p, hh * nt + i, 0))
    half = pl.BlockSpec((None, tr, cols), lambda p, hh, i, c_ref: (p, i, 0))
    return pl.pallas_call(
        body, name=name,
        grid_spec=pltpu.PrefetchScalarGridSpec(
            num_scalar_prefetch=1, grid=(npc, 2, nt),
            in_specs=[full, full, full, half, half], out_specs=[full] * 4),
        out_shape=[jax.ShapeDtypeStruct(w.shape, F32)] * 4,
    )(c.reshape(1), w, m, v, own, other)


WEIGHTS = ("norm_w", "final_norm_w", "ffn_gate", "ffn_up", "ffn_down", "w_in", "branch_proj", "w_out", "s5_lambda_re",
           "s5_lambda_im", "s5_log_dt", "s5_b_re", "s5_b_im", "s5_c_re", "s5_c_im", "s5_d", "s5_glu_w", "s5_glu_b",
           "hg_lb_logits", "hg_norm_w", "rg_conv_w", "rg_conv_b", "rg_wa", "rg_ba", "rg_wx", "rg_bx", "rg_lambda")
BIG = ("ffn_gate", "ffn_up", "ffn_down", "w_in", "branch_proj", "w_out", "s5_glu_w")
SHARDED_SMALL = ("norm_w", "rg_conv_w")
SMALL = SMALL_RAW + SHARDED_SMALL


def _pack(arrays, multiple):
    flat = jnp.concatenate([a.reshape(-1) for a in arrays])
    total = flat.shape[0]
    padded = -(-total // multiple) * multiple
    return jnp.pad(flat, (0, padded - total))


def _unpack(flat, like):
    out, at = [], 0
    for a in like:
        size = math.prod(a.shape)
        out.append(flat[at:at + size].reshape(a.shape))
        at += size
    return out


def _step(x, target, w, m, v):
    mx, my, mc = lax.axis_index("x"), lax.axis_index("y"), lax.axis_index("c")
    me = (2 * mx + my).astype(jnp.int32)
    mc = mc.astype(jnp.int32)

    gathered = gather_shards("gather_weights", [w[n].astype(MMT) for n in BIG] + [w[n] for n in SHARDED_SMALL])
    W = dict(wg=gathered[0], wu=gathered[1], wd=gathered[2], win=gathered[3], p=gathered[4], wout=gathered[5],
             gluw=gathered[6].reshape(DEPTH, BW, BW),
             nw=gathered[7].transpose(0, 2, 1, 3).reshape(DEPTH, 3, 1, D_MODEL),
             convw=gathered[8].transpose(0, 2, 1, 3).reshape(DEPTH, CONV_W, BW))
    loss, dx, big, small = local_step(x[0], target[0], W, {k: w[k] for k in SMALL_RAW})

    small_flat = _pack([small[n] for n in SMALL], NSH * 32 * LANE)
    grads = [big[n].reshape(big[n].shape[0], NSH, -1, big[n].shape[-1]) for n in BIG] + [small_flat.reshape(1, NSH, -1, LANE)]
    from_sibling = exchange_halves("reduce_cores", grads)
    merge = lambda a: a.reshape((-1,) + a.shape[2:])
    wire = [jnp.bfloat16] * len(BIG) + [F32]
    halves = [add_own_half(f"sum_cores_{i}", merge(g), merge(r), mc, wire[i]).reshape(r.shape)
              for i, (g, r) in enumerate(zip(grads, from_sibling))]
    from_chips = scatter_to_chips("reduce_chips", halves)
    own = [add_chips(f"sum_chips_{i}", h, r, me) for i, (h, r) in enumerate(zip(halves, from_chips))]
    other = share_halves("reduce_share", own)

    g, delta, new_m, new_v = {}, {}, {}, {}
    for i, n in enumerate(BIG):
        view = lambda a: a.reshape(own[i].shape[0], -1, own[i].shape[2])
        res = adamw_halves(f"adamw_{n}", view(w[n]), view(m[n]), view(v[n]), own[i], other[i], mc)
        g[n], delta[n], new_m[n], new_v[n] = [a.reshape(w[n].shape) for a in res]

    piece = jnp.stack([jnp.where(mc == 0, own[-1][0], other[-1][0]), jnp.where(mc == 0, other[-1][0], own[-1][0])])
    (all_small,) = gather_shards("gather_small", [piece])
    full_small = _unpack(all_small.transpose(1, 0, 2, 3).reshape(-1), [small[n] for n in SMALL])
    g.update(zip(SMALL, full_small))
    g["norm_w"] = lax.dynamic_slice_in_dim(g["norm_w"], me * (D_MODEL // NSH), D_MODEL // NSH, axis=2)
    g["rg_conv_w"] = lax.dynamic_slice_in_dim(g["rg_conv_w"], me * (BW // NSH), BW // NSH, axis=2)

    packed = [_pack([src[n] for n in SMALL], 8 * LANE).reshape(-1, LANE) for src in (w, g, m, v)]
    for dst, flat in zip((delta, new_m, new_v), adamw(*packed)):
        dst.update(zip(SMALL, _unpack(flat.reshape(-1), [w[n] for n in SMALL])))

    total = lax.psum(loss[0, 0], ("x", "y", "c"))
    return (total, dx[None], *[g[n] for n in WEIGHTS], *[delta[n] for n in WEIGHTS],
            *[new_m[n] for n in WEIGHTS], *[new_v[n] for n in WEIGHTS])


def kernel(x, norm_w, final_norm_w, ffn_gate, ffn_up, ffn_down, w_in, branch_proj, w_out, s5_lambda_re, s5_lambda_im, s5_log_dt, s5_b_re, s5_b_im, s5_c_re, s5_c_im, s5_d, s5_glu_w, s5_glu_b, hg_lb_logits, hg_norm_w, rg_conv_w, rg_conv_b, rg_wa, rg_ba, rg_wx, rg_bx, rg_lambda, loss_target, m_norm_w, m_final_norm_w, m_ffn_gate, m_ffn_up, m_ffn_down, m_w_in, m_branch_proj, m_w_out, m_s5_lambda_re, m_s5_lambda_im, m_s5_log_dt, m_s5_b_re, m_s5_b_im, m_s5_c_re, m_s5_c_im, m_s5_d, m_s5_glu_w, m_s5_glu_b, m_hg_lb_logits, m_hg_norm_w, m_rg_conv_w, m_rg_conv_b, m_rg_wa, m_rg_ba, m_rg_wx, m_rg_bx, m_rg_lambda, v_norm_w, v_final_norm_w, v_ffn_gate, v_ffn_up, v_ffn_down, v_w_in, v_branch_proj, v_w_out, v_s5_lambda_re, v_s5_lambda_im, v_s5_log_dt, v_s5_b_re, v_s5_b_im, v_s5_c_re, v_s5_c_im, v_s5_d, v_s5_glu_w, v_s5_glu_b, v_hg_lb_logits, v_hg_norm_w, v_rg_conv_w, v_rg_conv_b, v_rg_wa, v_rg_ba, v_rg_wx, v_rg_bx, v_rg_lambda):
    ws = (norm_w, final_norm_w, ffn_gate, ffn_up, ffn_down, w_in, branch_proj, w_out, s5_lambda_re, s5_lambda_im, s5_log_dt, s5_b_re, s5_b_im, s5_c_re, s5_c_im, s5_d, s5_glu_w, s5_glu_b, hg_lb_logits, hg_norm_w, rg_conv_w, rg_conv_b, rg_wa, rg_ba, rg_wx, rg_bx, rg_lambda)
    ms = (m_norm_w, m_final_norm_w, m_ffn_gate, m_ffn_up, m_ffn_down, m_w_in, m_branch_proj, m_w_out, m_s5_lambda_re, m_s5_lambda_im, m_s5_log_dt, m_s5_b_re, m_s5_b_im, m_s5_c_re, m_s5_c_im, m_s5_d, m_s5_glu_w, m_s5_glu_b, m_hg_lb_logits, m_hg_norm_w, m_rg_conv_w, m_rg_conv_b, m_rg_wa, m_rg_ba, m_rg_wx, m_rg_bx, m_rg_lambda)
    vs = (v_norm_w, v_final_norm_w, v_ffn_gate, v_ffn_up, v_ffn_down, v_w_in, v_branch_proj, v_w_out, v_s5_lambda_re, v_s5_lambda_im, v_s5_log_dt, v_s5_b_re, v_s5_b_im, v_s5_c_re, v_s5_c_im, v_s5_d, v_s5_glu_w, v_s5_glu_b, v_hg_lb_logits, v_hg_norm_w, v_rg_conv_w, v_rg_conv_b, v_rg_wa, v_rg_ba, v_rg_wx, v_rg_bx, v_rg_lambda)
    return _step(x, loss_target, dict(zip(WEIGHTS, ws)), dict(zip(WEIGHTS, ms)), dict(zip(WEIGHTS, vs)))
```

```python
import functools
import math
from typing import NamedTuple

import jax
import jax.numpy as jnp
from jax import lax
from jax.experimental import pallas as pl
from jax.experimental.pallas import tpu as pltpu

F32 = jnp.float32
MMT = jnp.bfloat16
HI = lax.Precision.HIGHEST

D_MODEL = 1024
BW = 512
S5_GROUP, S5_GROUPS, S5_STATE = 16, 32, 64
S5_N = S5_GROUPS * S5_STATE
HG_HEADS, HG_D = 4, 128
HG_CHUNK = 64
RG_BLOCKS, RG_BLOCK = 8, 64
RG_C = 8.0
CONV_W = 4
D_FF = 2816
EPS = 1e-6
IN_TOTAL = 6656
NSH = 4
NSEG = 8
LANE = 128
VMEM_LIMIT = 56 * 1024 * 1024

ADAM_LR, ADAM_B1, ADAM_B2, ADAM_EPS, ADAM_WD, ADAM_STEP = 0.001, 0.9, 0.999, 1e-08, 0.01, 10

MESH = pl.DeviceIdType.MESH


class WP(NamedTuple):
    w: jax.Array
    p: jax.Array


def _dg(a, b, ca, cb):
    return lax.dot_general(a, b, (((ca,), (cb,)), ((), ())), preferred_element_type=F32)


@jax.custom_vjp
def _mmw(a, w, p):
    return _dg(a.astype(MMT), w, 1, 0)


def _mmw_fwd(a, w, p):
    return _mmw(a, w, p), (a, w)


def _mmw_bwd(res, g):
    a, w = res
    gb = g.astype(MMT)
    return _dg(gb, w, 1, 1), jnp.zeros_like(w), _dg(a.astype(MMT), gb, 0, 0)


_mmw.defvjp(_mmw_fwd, _mmw_bwd)


def mm(a, w):
    if isinstance(w, WP):
        return _mmw(a, w.w, w.p)
    return _dg(a.astype(MMT), w, 1, 0)


@jax.custom_vjp
def mma_nn(a, b):
    return _dg(a.astype(MMT), b.astype(MMT), 1, 0)


def _nn_f(a, b):
    return mma_nn(a, b), (a, b)


def _nn_b(res, g):
    a, b = res
    gb = g.astype(MMT)
    return _dg(gb, b.astype(MMT), 1, 1), _dg(a.astype(MMT), gb, 0, 0)


mma_nn.defvjp(_nn_f, _nn_b)


@jax.custom_vjp
def mma_nt(a, b):
    return _dg(a.astype(MMT), b.astype(MMT), 1, 1)


def _nt_f(a, b):
    return mma_nt(a, b), (a, b)


def _nt_b(res, g):
    a, b = res
    gb = g.astype(MMT)
    return _dg(gb, b.astype(MMT), 1, 0), _dg(gb, a.astype(MMT), 0, 0)


mma_nt.defvjp(_nt_f, _nt_b)


@jax.custom_vjp
def mma_tn(a, b):
    return _dg(a.astype(MMT), b.astype(MMT), 0, 0)


def _tn_f(a, b):
    return mma_tn(a, b), (a, b)


def _tn_b(res, g):
    a, b = res
    gb = g.astype(MMT)
    return _dg(b.astype(MMT), gb, 1, 1), _dg(a.astype(MMT), gb, 1, 0)


mma_tn.defvjp(_tn_f, _tn_b)


def mm_exact(m, x):
    return jnp.dot(m, x, precision=HI, preferred_element_type=F32)


def _rms(x, w):
    return x * lax.rsqrt(jnp.mean(x * x, axis=-1, keepdims=True) + EPS) * w


def _expm1(x):
    series = x * (1.0 + x * (1.0 / 2) * (1.0 + x * (1.0 / 3) * (1.0 + x * (1.0 / 4) * (1.0 + x * (1.0 / 5) * (1.0 + x * (1.0 / 6))))))
    return jnp.where(jnp.abs(x) < 0.1, series, jnp.exp(x) - 1.0)


def _bspec(block, fn, order):
    if order == "is":
        return pl.BlockSpec(block, lambda i, s: fn(s, i))
    return pl.BlockSpec(block, lambda s, i: fn(s, i))


def tile_fwd(fn, name, n_i, n_s, ins, outs, s_outer=False):
    n_in = len(ins)
    order = "si" if s_outer else "is"
    assert not (s_outer and any(o[4] for o in outs))

    def body(*refs):
        s = pl.program_id(0 if s_outer else 1)
        res = fn(*[r[...] for r in refs[:n_in]], s)
        for o_ref, val, spec in zip(refs[n_in:], res, outs):
            if spec[4] and n_s > 1:
                @pl.when(s == 0)
                def _(o_ref=o_ref, val=val):
                    o_ref[...] = val.astype(o_ref.dtype)

                @pl.when(s != 0)
                def _(o_ref=o_ref, val=val):
                    o_ref[...] += val.astype(o_ref.dtype)
            else:
                o_ref[...] = val.astype(o_ref.dtype)

    return pl.pallas_call(
        body, grid=(n_s, n_i) if s_outer else (n_i, n_s), name=name,
        in_specs=[_bspec(b, f, order) for _, b, f in ins],
        out_specs=[_bspec(b, f, order) for _, _, b, f, _ in outs],
        out_shape=[jax.ShapeDtypeStruct(sh, dt) for sh, dt, _, _, _ in outs],
        compiler_params=pltpu.CompilerParams(vmem_limit_bytes=VMEM_LIMIT,
                                             dimension_semantics=("arbitrary", "arbitrary")),
    )(*[a for a, _, _ in ins])


def tile_bwd(fn, name, n_i, n_s, ins, cts, gouts):
    n_in, n_ct = len(ins), len(cts)
    kinds = [k for _, _, _, k in ins]
    d_pos = [j for j, k in enumerate(kinds) if k != "c"]
    shared = [(gi, spec[4]) for gi, spec in enumerate(gouts) if len(spec) == 5 and spec[4] is not None]
    n_sh = len(shared)

    def body(*refs):
        s, i = pl.program_id(0), pl.program_id(1)
        vals = [r[...] for r in refs[:n_in]]
        ctv = tuple(r[...] for r in refs[n_in:n_in + n_ct])
        g_refs = refs[n_in + n_ct + n_sh:]

        def g(*dv):
            args = list(vals)
            for j, v in zip(d_pos, dv):
                args[j] = WP(vals[j], v) if kinds[j] == "w" else v
            return tuple(fn(*args))

        dv0 = [jnp.zeros(vals[j].shape, F32) if kinds[j] == "w" else vals[j] for j in d_pos]
        _, vjp = jax.vjp(g, *dv0)
        grads = vjp(ctv)
        for g_ref, gv, spec in zip(g_refs, grads, gouts):
            mode = spec[3]
            if mode == "write":
                g_ref[...] = gv.astype(g_ref.dtype)
            else:
                first = (i == 0) if mode == "acc_i" else jnp.logical_and(i == 0, s == 0)

                @pl.when(first)
                def _(g_ref=g_ref, gv=gv):
                    g_ref[...] = gv.astype(g_ref.dtype)

                @pl.when(jnp.logical_not(first))
                def _(g_ref=g_ref, gv=gv):
                    g_ref[...] += gv.astype(g_ref.dtype)

    return pl.pallas_call(
        body, grid=(n_s, n_i), name=name,
        in_specs=([_bspec(b, f, "si") for _, b, f, _ in ins] + [_bspec(b, f, "si") for _, b, f in cts]
                  + [pl.BlockSpec(memory_space=pl.ANY)] * n_sh),
        out_specs=[_bspec(spec[1], spec[2], "si") for spec in gouts],
        out_shape=[jax.ShapeDtypeStruct(spec[0], F32) for spec in gouts],
        input_output_aliases={n_in + n_ct + k: gi for k, (gi, _) in enumerate(shared)},
        compiler_params=pltpu.CompilerParams(vmem_limit_bytes=VMEM_LIMIT,
                                             dimension_semantics=("arbitrary", "arbitrary")),
    )(*[a for a, _, _, _ in ins], *[a for a, _, _ in cts], *[buf for _, buf in shared])


def _row_tile(rows, width, itemsize=4, budget=2 * 1024 * 1024, mult=8):
    best = mult
    for t in range(mult, rows + 1, mult):
        if rows % t == 0 and t * width * itemsize <= budget:
            best = t
    return best


def add_n(name, terms, shape):
    rows, cols = shape
    tr = _row_tile(rows, cols)

    def body(*refs):
        acc = refs[0][...]
        for r in refs[1:-1]:
            acc = acc + r[...]
        refs[-1][...] = acc

    specs = []
    for _, lead in terms:
        specs.append(pl.BlockSpec((None,) * len(lead) + (tr, cols), functools.partial(lambda i, lead: (*lead, i, 0), lead=lead)))
    return pl.pallas_call(
        body, grid=(rows // tr,), name=name, in_specs=specs,
        out_specs=pl.BlockSpec((tr, cols), lambda i: (i, 0)),
        out_shape=jax.ShapeDtypeStruct((rows, cols), F32),
    )(*[a for a, _ in terms])


def ffn_core(x, nw, wg, wu, wd):
    h = _rms(x, nw)
    return (0.5 * mm(jax.nn.silu(mm(h, wg)) * mm(h, wu), wd),)


def pre_core(x, nw, win):
    return (mm(_rms(x, nw), win),)


def _split_lanes(y):
    return jnp.stack([y[:, k * LANE:(k + 1) * LANE] for k in range(y.shape[1] // LANE)], axis=0)


def _join_lanes(y3):
    return jnp.concatenate([y3[k] for k in range(y3.shape[0])], axis=1)


def s5_pre_core(u, bmat):
    bu = mm(u, bmat)
    return _split_lanes(bu[:, :S5_N]), _split_lanes(bu[:, S5_N:])


def mid_core(xr, xi, u, o, g, hs, gc, hmat, cmat, d, gluw, glub, hgw):
    xs = jnp.concatenate([_join_lanes(xr), _join_lanes(xi)], axis=1)
    y = mm(xs, cmat) + d * u
    z = jax.nn.gelu(y)
    ya = z * jax.nn.sigmoid(mm(z, gluw) + glub)
    ms = mm_exact(o * o, hmat)
    yb = o * lax.rsqrt(ms + EPS) * hgw * jax.nn.silu(g)
    yc = hs * jax.nn.gelu(gc)
    return ya, yb, yc


def _sub(w, n):
    return WP(w.w[n], w.p[n]) if isinstance(w, WP) else w[n]


def merge_core(ya, yb, yc, g0, g1, g2, g3, g4, g5, p, wout):
    gate = lambda a, b: jax.nn.sigmoid(jnp.concatenate([a, b], axis=1))
    m = gate(g0, g1) * mm(ya, _sub(p, 0)) + gate(g2, g3) * mm(yb, _sub(p, 1)) + gate(g4, g5) * mm(yc, _sub(p, 2))
    return (mm(m, wout),)


def gates_core(xc, wa, ba, wx, bx, lam):
    r = jax.nn.sigmoid(mm(xc, wa) + ba)
    i = jax.nn.sigmoid(mm(xc, wx) + bx)
    log_a = -RG_C * jax.nn.softplus(-lam) * r
    a = jnp.exp(log_a)
    b = jnp.sqrt(-_expm1(2.0 * log_a)) * (i * xc)
    return a, b


def _seg_rows(ref, k, j, n):
    rows = pl.ds(pl.multiple_of(j * NSEG, NSEG), NSEG)
    if k is None:
        return ref[rows, :]
    return ref[k, rows, :]


def _seg_store(ref, k, j, n, val):
    rows = pl.ds(pl.multiple_of(j * NSEG, NSEG), NSEG)
    if k is None:
        ref[rows, :] = val
    else:
        ref[k, rows, :] = val


def _seg_carries(er, ei, pr, pi, reverse):
    rows = lax.broadcasted_iota(jnp.int32, er.shape, 0)
    cr = jnp.zeros_like(er)
    ci = None if ei is None else jnp.zeros_like(er)
    order = range(NSEG - 2, -1, -1) if reverse else range(1, NSEG)
    shift = NSEG - 1 if reverse else 1
    for s in order:
        if ei is None:
            tr = er + pr * cr
            cr = jnp.where(rows == s, pltpu.roll(tr, shift, 0), cr)
        else:
            tr = er + pr * cr - pi * ci
            ti = ei + pr * ci + pi * cr
            cr = jnp.where(rows == s, pltpu.roll(tr, shift, 0), cr)
            ci = jnp.where(rows == s, pltpu.roll(ti, shift, 0), ci)
    return cr, ci


S5_K = 2


def s5_scan_fwd(bur, bui, ar, ai, L):
    n = L // NSEG
    nb = S5_N // LANE
    K = S5_K

    def body(br_ref, bi_ref, ar_ref, ai_ref, xr_ref, xi_ref):
        zero = jnp.zeros((NSEG, LANE), F32)
        A = [(jnp.broadcast_to(ar_ref[k], (NSEG, LANE)), jnp.broadcast_to(ai_ref[k], (NSEG, LANE))) for k in range(K)]

        def p1(j, st):
            new = []
            for k in range(K):
                sr, si, pr, pi = st[k]
                a_r, a_i = A[k]
                nr = a_r * sr - a_i * si + _seg_rows(br_ref, k, j, n)
                ni = a_r * si + a_i * sr + _seg_rows(bi_ref, k, j, n)
                _seg_store(xr_ref, k, j, n, nr)
                _seg_store(xi_ref, k, j, n, ni)
                new.append((nr, ni, a_r * pr - a_i * pi, a_r * pi + a_i * pr))
            return tuple(new)

        st = lax.fori_loop(0, n, p1, tuple((zero, zero, zero + 1.0, zero) for _ in range(K)))
        C = [_seg_carries(st[k][0], st[k][1], st[k][2], st[k][3], False) for k in range(K)]

        def p2(j, st):
            new = []
            for k in range(K):
                pr, pi = st[k]
                a_r, a_i = A[k]
                pr, pi = a_r * pr - a_i * pi, a_r * pi + a_i * pr
                cr, ci = C[k]
                _seg_store(xr_ref, k, j, n, _seg_rows(xr_ref, k, j, n) + pr * cr - pi * ci)
                _seg_store(xi_ref, k, j, n, _seg_rows(xi_ref, k, j, n) + pr * ci + pi * cr)
                new.append((pr, pi))
            return tuple(new)

        lax.fori_loop(0, n, p2, tuple((zero + 1.0, zero) for _ in range(K)))

    blk = pl.BlockSpec((K, L, LANE), lambda g: (g, 0, 0))
    ablk = pl.BlockSpec((K, 1, LANE), lambda g: (g, 0, 0))
    return pl.pallas_call(
        body, grid=(nb // K,), name="s5_scan_fwd",
        in_specs=[blk, blk, ablk, ablk], out_specs=[blk, blk],
        out_shape=[jax.ShapeDtypeStruct((nb, L, LANE), F32)] * 2,
        compiler_params=pltpu.CompilerParams(vmem_limit_bytes=VMEM_LIMIT),
    )(bur, bui, ar, ai)


def s5_scan_bwd(dxr, dxi, xr, xi, ar, ai, L):
    n = L // NSEG
    nb = S5_N // LANE
    K = S5_K

    def body(dr_ref, di_ref, xr_ref, xi_ref, ar_ref, ai_ref, gr_ref, gi_ref, dar_ref, dai_ref):
        zero = jnp.zeros((NSEG, LANE), F32)
        rows = lax.broadcasted_iota(jnp.int32, (NSEG, LANE), 0)
        A = [(jnp.broadcast_to(ar_ref[k], (NSEG, LANE)), -jnp.broadcast_to(ai_ref[k], (NSEG, LANE))) for k in range(K)]

        def p1(jj, st):
            j = n - 1 - jj
            new = []
            for k in range(K):
                sr, si, pr, pi = st[k]
                a_r, a_i = A[k]
                nr = a_r * sr - a_i * si + _seg_rows(dr_ref, k, j, n)
                ni = a_r * si + a_i * sr + _seg_rows(di_ref, k, j, n)
                _seg_store(gr_ref, k, j, n, nr)
                _seg_store(gi_ref, k, j, n, ni)
                new.append((nr, ni, a_r * pr - a_i * pi, a_r * pi + a_i * pr))
            return tuple(new)

        st = lax.fori_loop(0, n, p1, tuple((zero, zero, zero + 1.0, zero) for _ in range(K)))
        C = [_seg_carries(st[k][0], st[k][1], st[k][2], st[k][3], True) for k in range(K)]
        xb = [(jnp.where(rows == 0, 0.0, pltpu.roll(_seg_rows(xr_ref, k, n - 1, n), 1, 0)),
               jnp.where(rows == 0, 0.0, pltpu.roll(_seg_rows(xi_ref, k, n - 1, n), 1, 0))) for k in range(K)]

        def p2(jj, st):
            j = n - 1 - jj
            jp = jnp.maximum(j - 1, 0)
            new = []
            for k in range(K):
                pr, pi, acr, aci = st[k]
                a_r, a_i = A[k]
                pr, pi = a_r * pr - a_i * pi, a_r * pi + a_i * pr
                cr, ci = C[k]
                g_r = _seg_rows(gr_ref, k, j, n) + pr * cr - pi * ci
                g_i = _seg_rows(gi_ref, k, j, n) + pr * ci + pi * cr
                _seg_store(gr_ref, k, j, n, g_r)
                _seg_store(gi_ref, k, j, n, g_i)
                xpr = jnp.where(j == 0, xb[k][0], _seg_rows(xr_ref, k, jp, n))
                xpi = jnp.where(j == 0, xb[k][1], _seg_rows(xi_ref, k, jp, n))
                new.append((pr, pi, acr + g_r * xpr + g_i * xpi, aci + g_i * xpr - g_r * xpi))
            return tuple(new)

        st = lax.fori_loop(0, n, p2, tuple((zero + 1.0, zero, zero, zero) for _ in range(K)))
        for k in range(K):
            dar_ref[k] = jnp.sum(st[k][2], axis=0, keepdims=True)
            dai_ref[k] = jnp.sum(st[k][3], axis=0, keepdims=True)

    blk = pl.BlockSpec((K, L, LANE), lambda g: (g, 0, 0))
    ablk = pl.BlockSpec((K, 1, LANE), lambda g: (g, 0, 0))
    return pl.pallas_call(
        body, grid=(nb // K,), name="s5_scan_bwd",
        in_specs=[blk, blk, blk, blk, ablk, ablk], out_specs=[blk, blk, ablk, ablk],
        out_shape=[jax.ShapeDtypeStruct((nb, L, LANE), F32)] * 2 + [jax.ShapeDtypeStruct((nb, 1, LANE), F32)] * 2,
        compiler_params=pltpu.CompilerParams(vmem_limit_bytes=VMEM_LIMIT),
    )(dxr, dxi, xr, xi, ar, ai)


def rg_scan_fwd(a, b, L):
    n = L // NSEG

    def body(a_ref, b_ref, h_ref):
        zero = jnp.zeros((NSEG, LANE), F32)

        def p1(j, st):
            h, p = st
            aj = _seg_rows(a_ref, None, j, n)
            h = aj * h + _seg_rows(b_ref, None, j, n)
            _seg_store(h_ref, None, j, n, h)
            return h, aj * p

        e, pe = lax.fori_loop(0, n, p1, (zero, zero + 1.0))
        c, _ = _seg_carries(e, None, pe, None, False)

        def p2(j, p):
            p = _seg_rows(a_ref, None, j, n) * p
            _seg_store(h_ref, None, j, n, _seg_rows(h_ref, None, j, n) + p * c)
            return p

        lax.fori_loop(0, n, p2, zero + 1.0)

    blk = pl.BlockSpec((L, LANE), lambda g: (0, g))
    return pl.pallas_call(
        body, grid=(BW // LANE,), name="rg_scan_fwd", in_specs=[blk, blk], out_specs=blk,
        out_shape=jax.ShapeDtypeStruct((L, BW), F32),
        compiler_params=pltpu.CompilerParams(vmem_limit_bytes=VMEM_LIMIT),
    )(a, b)


def rg_scan_bwd(a, h, dh, L):
    n = L // NSEG

    def body(a_ref, h_ref, dh_ref, da_ref, db_ref):
        zero = jnp.zeros((NSEG, LANE), F32)
        rows = lax.broadcasted_iota(jnp.int32, (NSEG, LANE), 0)
        a_edge = jnp.where(rows == NSEG - 1, 0.0, pltpu.roll(_seg_rows(a_ref, None, 0, n), NSEG - 1, 0))
        h_edge = jnp.where(rows == 0, 0.0, pltpu.roll(_seg_rows(h_ref, None, n - 1, n), 1, 0))

        def mult(j):
            return jnp.where(j == n - 1, a_edge, _seg_rows(a_ref, None, jnp.minimum(j + 1, n - 1), n))

        def p1(jj, st):
            j = n - 1 - jj
            g, p = st
            m = mult(j)
            g = m * g + _seg_rows(dh_ref, None, j, n)
            _seg_store(db_ref, None, j, n, g)
            return g, m * p

        e, pe = lax.fori_loop(0, n, p1, (zero, zero + 1.0))
        c, _ = _seg_carries(e, None, pe, None, True)

        def p2(jj, p):
            j = n - 1 - jj
            p = mult(j) * p
            g = _seg_rows(db_ref, None, j, n) + p * c
            _seg_store(db_ref, None, j, n, g)
            hp = jnp.where(j == 0, h_edge, _seg_rows(h_ref, None, jnp.maximum(j - 1, 0), n))
            _seg_store(da_ref, None, j, n, g * hp)
            return p

        lax.fori_loop(0, n, p2, zero + 1.0)

    blk = pl.BlockSpec((L, LANE), lambda g: (0, g))
    return pl.pallas_call(
        body, grid=(BW // LANE,), name="rg_scan_bwd", in_specs=[blk, blk, blk], out_specs=[blk, blk],
        out_shape=[jax.ShapeDtypeStruct((L, BW), F32)] * 2,
        compiler_params=pltpu.CompilerParams(vmem_limit_bytes=VMEM_LIMIT),
    )(a, h, dh)


def _hg_consts(C):
    t = lax.broadcasted_iota(jnp.int32, (C, C), 0)
    s = lax.broadcasted_iota(jnp.int32, (C, C), 1)
    tril = (s <= t).astype(F32)
    diag = (s == t).astype(F32)
    levels = []
    k = 1
    while (1 << k) <= C:
        m = 1 << (k - 1)
        same = (t >> k) == (s >> k)
        t_right = ((t >> (k - 1)) & 1) == 1
        s_left = ((s >> (k - 1)) & 1) == 0
        mask = jnp.logical_and(same, jnp.logical_and(t_right, s_left)).astype(F32)
        bnd = ((t >> k) << k) + (m - 1)
        levels.append((mask, (s <= bnd).astype(F32)))
        k += 1
    return tril, diag, levels


def hg_chunk(st, q, z, v, lb):
    C = q.shape[0]
    tril, diag, levels = _hg_consts(C)
    sig = jax.nn.sigmoid(z)
    lf = jnp.log(lb + (1.0 - lb) * sig)
    k = (1.0 - lb) * jax.nn.sigmoid(-z)
    qh = jax.nn.silu(q)
    b = mm_exact(tril, lf)
    blast = jnp.sum(lf, axis=0, keepdims=True)
    qe = qh * jnp.exp(b)
    kd = k * jnp.exp(blast - b)
    scaled = []
    for _, sel in levels:
        ref = mm_exact(sel, lf)
        scaled.append((qh * jnp.exp(jnp.minimum(b - ref, 0.0)), k * jnp.exp(jnp.minimum(ref - b, 0.0))))
    outs, news = [], []
    for h in range(HG_HEADS):
        sl = slice(h * HG_D, (h + 1) * HG_D)
        st_h = st[h * HG_D:(h + 1) * HG_D, :]
        sc = diag * mma_nt(qh[:, sl], k[:, sl])
        for (mask, _), (qt, kt) in zip(levels, scaled):
            sc = sc + mask * mma_nt(qt[:, sl], kt[:, sl])
        outs.append(mma_nt(qe[:, sl], st_h) + mma_nn(sc, v[:, sl]))
        news.append(st_h * jnp.exp(blast[:, sl]) + mma_tn(v[:, sl], kd[:, sl]))
    return jnp.concatenate(news, axis=0), jnp.concatenate(outs, axis=1)


def hg_fwd(qzv, lb, L):
    C = HG_CHUNK
    nc = L // C

    def body(q_ref, z_ref, v_ref, lb_ref, o_ref, sst_ref, st_ref):
        @pl.when(pl.program_id(0) == 0)
        def _():
            st_ref[...] = jnp.zeros_like(st_ref)

        st = st_ref[...]
        sst_ref[...] = st
        new, o = hg_chunk(st, q_ref[...], z_ref[...], v_ref[...], lb_ref[...])
        st_ref[...] = new
        o_ref[...] = o

    col = lambda cb: pl.BlockSpec((C, BW), functools.partial(lambda c, cb: (c, cb), cb=cb))
    return pl.pallas_call(
        body, grid=(nc,), name="hg_fwd",
        in_specs=[col(0), col(1), col(2), pl.BlockSpec((1, BW), lambda c: (0, 0))],
        out_specs=[pl.BlockSpec((C, BW), lambda c: (c, 0)), pl.BlockSpec((None, BW, HG_D), lambda c: (c, 0, 0))],
        out_shape=[jax.ShapeDtypeStruct((L, BW), F32), jax.ShapeDtypeStruct((nc, BW, HG_D), F32)],
        scratch_shapes=[pltpu.VMEM((BW, HG_D), F32)],
        compiler_params=pltpu.CompilerParams(vmem_limit_bytes=VMEM_LIMIT, dimension_semantics=("arbitrary",)),
    )(qzv, qzv, qzv, lb)


def hg_bwd(qzv, lb, sst, do, L):
    C = HG_CHUNK
    nc = L // C

    def body(q_ref, z_ref, v_ref, lb_ref, sst_ref, do_ref, dq_ref, dz_ref, dv_ref, dlb_ref, dst_ref):
        @pl.when(pl.program_id(0) == 0)
        def _():
            dst_ref[...] = jnp.zeros_like(dst_ref)
            dlb_ref[...] = jnp.zeros_like(dlb_ref)

        _, vjp = jax.vjp(hg_chunk, sst_ref[...], q_ref[...], z_ref[...], v_ref[...], lb_ref[...])
        dst, dq, dz, dv, dlb = vjp((dst_ref[...], do_ref[...]))
        dst_ref[...] = dst
        dq_ref[...] = dq
        dz_ref[...] = dz
        dv_ref[...] = dv
        dlb_ref[...] += dlb

    col = lambda cb: pl.BlockSpec((C, BW), functools.partial(lambda c, cb: (nc - 1 - c, cb), cb=cb))
    rev = pl.BlockSpec((C, BW), lambda c: (nc - 1 - c, 0))
    return pl.pallas_call(
        body, grid=(nc,), name="hg_bwd",
        in_specs=[col(0), col(1), col(2), pl.BlockSpec((1, BW), lambda c: (0, 0)),
                  pl.BlockSpec((None, BW, HG_D), lambda c: (nc - 1 - c, 0, 0)), rev],
        out_specs=[rev, rev, rev, pl.BlockSpec((1, BW), lambda c: (0, 0))],
        out_shape=[jax.ShapeDtypeStruct((L, BW), F32)] * 3 + [jax.ShapeDtypeStruct((1, BW), F32)],
        scratch_shapes=[pltpu.VMEM((BW, HG_D), F32)],
        compiler_params=pltpu.CompilerParams(vmem_limit_bytes=VMEM_LIMIT, dimension_semantics=("arbitrary",)),
    )(qzv, qzv, qzv, lb, sst, do)


def _shift_down(x, d, rows, L):
    if d == 0:
        return x
    wrapped = jnp.where((rows & (NSEG - 1)) == 0, 0.0, pltpu.roll(x, NSEG * d + 1, 0))
    return jnp.where(rows < NSEG * d, wrapped, pltpu.roll(x, NSEG * d, 0))


def _shift_up(x, d, rows, L):
    if d == 0:
        return x
    wrapped = jnp.where((rows & (NSEG - 1)) == NSEG - 1, 0.0, pltpu.roll(x, L - (NSEG * d + 1), 0))
    return jnp.where(rows >= L - NSEG * d, wrapped, pltpu.roll(x, L - NSEG * d, 0))


def conv_fwd(proj, w, b, L):
    def body(x_ref, w_ref, b_ref, o_ref):
        x = x_ref[...]
        rows = lax.broadcasted_iota(jnp.int32, x.shape, 0)
        acc = jnp.broadcast_to(b_ref[...], x.shape)
        for k in range(CONV_W):
            acc = acc + w_ref[pl.ds(k, 1), :] * _shift_down(x, CONV_W - 1 - k, rows, L)
        o_ref[...] = acc

    nl = BW // LANE
    return pl.pallas_call(
        body, grid=(nl,), name="conv_fwd",
        in_specs=[pl.BlockSpec((L, LANE), lambda g: (0, 5 * nl + g)), pl.BlockSpec((CONV_W, LANE), lambda g: (0, g)),
                  pl.BlockSpec((1, LANE), lambda g: (0, g))],
        out_specs=pl.BlockSpec((L, LANE), lambda g: (0, g)),
        out_shape=jax.ShapeDtypeStruct((L, BW), F32),
        compiler_params=pltpu.CompilerParams(vmem_limit_bytes=VMEM_LIMIT),
    )(proj, w, b)


def conv_bwd(proj, w, dxc, L):
    def body(x_ref, w_ref, d_ref, dx_ref, dw_ref, db_ref):
        x, d = x_ref[...], d_ref[...]
        rows = lax.broadcasted_iota(jnp.int32, x.shape, 0)
        acc = jnp.zeros_like(x)
        for k in range(CONV_W):
            acc = acc + w_ref[pl.ds(k, 1), :] * _shift_up(d, CONV_W - 1 - k, rows, L)
            dw_ref[pl.ds(k, 1), :] = jnp.sum(d * _shift_down(x, CONV_W - 1 - k, rows, L), axis=0, keepdims=True)
        dx_ref[...] = acc
        db_ref[...] = jnp.sum(d, axis=0, keepdims=True)

    nl = BW // LANE
    blk = pl.BlockSpec((L, LANE), lambda g: (0, g))
    return pl.pallas_call(
        body, grid=(nl,), name="conv_bwd",
        in_specs=[pl.BlockSpec((L, LANE), lambda g: (0, 5 * nl + g)), pl.BlockSpec((CONV_W, LANE), lambda g: (0, g)), blk],
        out_specs=[blk, pl.BlockSpec((CONV_W, LANE), lambda g: (0, g)), pl.BlockSpec((1, LANE), lambda g: (0, g))],
        out_shape=[jax.ShapeDtypeStruct((L, BW), F32), jax.ShapeDtypeStruct((CONV_W, BW), F32),
                   jax.ShapeDtypeStruct((1, BW), F32)],
        compiler_params=pltpu.CompilerParams(vmem_limit_bytes=VMEM_LIMIT),
    )(proj, w, dxc)


def loss_fwd_bwd(x, fw, target, L, tm):
    def fn(x, fw, t):
        err = jnp.square(_rms(x, fw) - t)
        return jnp.sum(0.5 * jnp.mean(err, axis=-1, keepdims=True), axis=0, keepdims=True)

    def body(x_ref, fw_ref, t_ref, l_ref, dx_ref, dfw_ref):
        i = pl.program_id(0)
        t = t_ref[...]
        val, vjp = jax.vjp(lambda x, fw: fn(x, fw, t), x_ref[...], fw_ref[...])
        dx, dfw = vjp(jnp.ones((1, 1), F32))
        dx_ref[...] = dx

        @pl.when(i == 0)
        def _():
            l_ref[...] = jnp.zeros_like(l_ref)
            dfw_ref[...] = jnp.zeros_like(dfw_ref)

        l_ref[...] += jnp.broadcast_to(val, l_ref.shape)
        dfw_ref[...] += dfw

    row = pl.BlockSpec((tm, D_MODEL), lambda i: (i, 0))
    vec = pl.BlockSpec((1, D_MODEL), lambda i: (0, 0))
    return pl.pallas_call(
        body, grid=(L // tm,), name="loss_fwd_bwd", in_specs=[row, vec, row],
        out_specs=[pl.BlockSpec((1, LANE), lambda i: (0, 0)), row, vec],
        out_shape=[jax.ShapeDtypeStruct((1, LANE), F32), jax.ShapeDtypeStruct((L, D_MODEL), F32),
                   jax.ShapeDtypeStruct((1, D_MODEL), F32)],
        compiler_params=pltpu.CompilerParams(vmem_limit_bytes=VMEM_LIMIT, dimension_semantics=("arbitrary",)),
    )(x, fw, target)


def adamw(w, g, m, v):
    rows, cols = w.shape
    tr = _row_tile(rows, cols, budget=1024 * 1024)
    c1 = 1.0 - ADAM_B1 ** ADAM_STEP
    c2 = 1.0 - ADAM_B2 ** ADAM_STEP

    def body(w_ref, g_ref, m_ref, v_ref, d_ref, nm_ref, nv_ref):
        g = g_ref[...]
        nm = ADAM_B1 * m_ref[...] + (1.0 - ADAM_B1) * g
        nv = ADAM_B2 * v_ref[...] + (1.0 - ADAM_B2) * jnp.square(g)
        d_ref[...] = -ADAM_LR * ((nm / c1) / (jnp.sqrt(nv / c2) + ADAM_EPS) + ADAM_WD * w_ref[...])
        nm_ref[...] = nm
        nv_ref[...] = nv

    blk = pl.BlockSpec((tr, cols), lambda i: (i, 0))
    return pl.pallas_call(
        body, grid=(rows // tr,), name="adamw", in_specs=[blk] * 4, out_specs=[blk] * 3,
        out_shape=[jax.ShapeDtypeStruct((rows, cols), F32)] * 3,
    )(w, g, m, v)


def s5_prep(lam_re, lam_im, log_dt, b_re, b_im, c_re, c_im):
    lr = jnp.minimum(lam_re, -1e-4)
    li = lam_im
    dt = jnp.exp(log_dt)[:, None]
    mag = jnp.exp(lr * dt)
    ar = mag * jnp.cos(li * dt)
    ai = mag * jnp.sin(li * dt)
    den = lr * lr + li * li
    fr = ((ar - 1.0) * lr + ai * li) / den
    fi = (ai * lr - (ar - 1.0) * li) / den
    bbr = fr[..., None] * b_re - fi[..., None] * b_im
    bbi = fr[..., None] * b_im + fi[..., None] * b_re
    emb_b = lambda bb: _block_diag(bb.transpose(0, 2, 1).reshape(BW, S5_STATE), S5_GROUPS)
    emb_c = lambda cc: _block_diag(cc.transpose(0, 2, 1).reshape(S5_N, S5_GROUP), S5_GROUPS)
    bmat = jnp.concatenate([emb_b(bbr), emb_b(bbi)], axis=1)
    cmat = jnp.concatenate([emb_c(c_re), -emb_c(c_im)], axis=0)
    nb = S5_N // LANE
    return ar.reshape(nb, 1, LANE), ai.reshape(nb, 1, LANE), bmat, cmat


def _block_diag(stacked, groups):
    rows, c = stacked.shape
    r = rows // groups
    row_g = jnp.arange(rows)[:, None] // r
    col_g = jnp.arange(groups * c)[None, :] // c
    return jnp.where(row_g == col_g, jnp.tile(stacked, (1, groups)), 0.0)


def rg_prep(w):
    return _block_diag(w.reshape(BW, RG_BLOCK), RG_BLOCKS)


def hg_prep(logits):
    p = jax.nn.softmax(logits, axis=0)
    return jnp.cumsum(p, axis=0) - p[0]


def _head_mean_matrix():
    r = jnp.arange(BW) // HG_D
    return (r[:, None] == r[None, :]).astype(F32) / HG_D


def _to_segment_order(a):
    L = a.shape[0]
    return a.reshape(NSEG, L // NSEG, -1).transpose(1, 0, 2).reshape(a.shape)


def _to_time_order(a):
    L = a.shape[0]
    return a.reshape(L // NSEG, NSEG, -1).transpose(1, 0, 2).reshape(a.shape)


def _const(*idx):
    return lambda s, i: idx


def _rows(cb=0):
    return lambda s, i: (i, cb)


def _sum_parts(name, first, parts, shape):
    return add_n(name, [(first, ())] + [(parts, (s,)) for s in range(NSH)], shape)


def ffn_fwd(name, x, W, l, j, k, L, tm):
    D = D_MODEL
    def fn(x, nw, wg, wu, wd, s):
        h = _rms(x, nw)
        y = x
        for sh in range(NSH):
            y = y + 0.5 * mm(jax.nn.silu(mm(h, wg[sh])) * mm(h, wu[sh]), wd[sh])
        return (y,)

    ins = [(x, (tm, D), _rows()),
           (W["nw"], (None, None, 1, D), _const(l, k, 0, 0)),
           (W["wg"], (None, NSH, None, D, D_FF // NSH), _const(l, 0, j, 0, 0)),
           (W["wu"], (None, NSH, None, D, D_FF // NSH), _const(l, 0, j, 0, 0)),
           (W["wd"], (None, NSH, None, D_FF // NSH, D), _const(l, 0, j, 0, 0))]
    return tile_fwd(fn, name, L // tm, 1, ins, [((L, D), F32, (tm, D), _rows(), False)])[0]


def ffn_bwd(name, x, dy, W, bufs, l, j, k, L, tm):
    D, F = D_MODEL, D_FF // NSH
    stk = lambda key, r, c: ((2 * DEPTH, NSH, r, c), (None, None, r, c), lambda s, i: (2 * l + j, s, 0, 0), "acc_i",
                             bufs.get(key))
    ins = [(x, (tm, D), _rows(), "r"),
           (W["nw"], (None, None, 1, D), _const(l, k, 0, 0), "p"),
           (W["wg"], (None, None, None, D, F), lambda s, i: (l, s, j, 0, 0), "w"),
           (W["wu"], (None, None, None, D, F), lambda s, i: (l, s, j, 0, 0), "w"),
           (W["wd"], (None, None, None, F, D), lambda s, i: (l, s, j, 0, 0), "w")]
    gouts = [((NSH, L, D), (None, tm, D), lambda s, i: (s, i, 0), "write"),
             ((1, D), (1, D), _const(0, 0), "acc_all"),
             stk("ffn_gate", D, F), stk("ffn_up", D, F), stk("ffn_down", F, D)]
    part, dnw, bufs["ffn_gate"], bufs["ffn_up"], bufs["ffn_down"] = tile_bwd(
        ffn_core, name, L // tm, NSH, ins, [(dy, (tm, D), _rows())], gouts)
    return _sum_parts(name + "_dx", dy, part, (L, D)), dnw


def layer_fwd(l, x0, W, P, L, tm):
    D = D_MODEL
    n_i = L // tm
    x1 = ffn_fwd(f"ffn_fwd_{l}0", x0, W, l, 0, 0, L, tm)
    proj = tile_fwd(
        lambda x, nw, win, s: pre_core(x, nw, win), f"pre_fwd_{l}", n_i, NSH,
        [(x1, (tm, D), _rows()), (W["nw"], (None, None, 1, D), _const(l, 1, 0, 0)),
         (W["win"], (None, None, D, IN_TOTAL // NSH), lambda s, i: (l, s, 0, 0))],
        [((L, IN_TOTAL), F32, (tm, IN_TOTAL // NSH), lambda s, i: (i, s), False)], s_outer=True)[0]
    nb = S5_N // LANE
    blk3 = lambda s, i: (0, i, 0)
    bur, bui = tile_fwd(
        lambda u, bmat, s: s5_pre_core(u, bmat), f"s5pre_fwd_{l}", n_i, 1,
        [(proj, (tm, BW), _rows(0)), (P["bmat"][l], (BW, 2 * S5_N), _const(0, 0))],
        [((nb, L, LANE), F32, (nb, tm, LANE), blk3, False)] * 2)
    xr, xi = s5_scan_fwd(bur, bui, P["ar"][l], P["ai"][l], L)
    qzv = _to_time_order(proj[:, BW:4 * BW])
    o_t, sst = hg_fwd(qzv, P["lb"][l], L)
    o = _to_segment_order(o_t)
    xc = conv_fwd(proj, W["convw"][l], P["convb"][l], L)
    vec = (None, 1, BW)
    a, b = tile_fwd(
        lambda xc, wa, ba, wx, bx, lam, s: gates_core(xc, wa, ba, wx, bx, lam), f"gates_fwd_{l}", n_i, 1,
        [(xc, (tm, BW), _rows()), (P["wa"][l], (BW, BW), _const(0, 0)), (P["ba"], vec, _const(l, 0, 0)),
         (P["wx"][l], (BW, BW), _const(0, 0)), (P["bx"], vec, _const(l, 0, 0)), (P["lam"], vec, _const(l, 0, 0))],
        [((L, BW), F32, (tm, BW), _rows(), False)] * 2)
    hs = rg_scan_fwd(a, b, L)
    tmm = tm
    ya, yb, yc = tile_fwd(
        lambda *a: mid_core(*a[:-1]), f"mid_fwd_{l}", L // tmm, 1,
        [(xr, (nb, tmm, LANE), blk3), (xi, (nb, tmm, LANE), blk3), (proj, (tmm, BW), _rows(0)), (o, (tmm, BW), _rows()),
         (proj, (tmm, BW), _rows(4)), (hs, (tmm, BW), _rows()), (proj, (tmm, BW), _rows(6)),
         (P["hmat"], (BW, BW), _const(0, 0)), (P["cmat"][l], (2 * S5_N, BW), _const(0, 0)), (P["d"], vec, _const(l, 0, 0)),
         (W["gluw"], (None, BW, BW), _const(l, 0, 0)), (P["glub"], vec, _const(l, 0, 0)), (P["hgw"], vec, _const(l, 0, 0))],
        [((L, BW), F32, (tmm, BW), _rows(), False)] * 3)
    x2 = tile_fwd(
        lambda x, *rest: (x + merge_core(*rest[:-1])[0],), f"merge_fwd_{l}", n_i, 1,
        [(x1, (tm, D), _rows()), (ya, (tm, BW), _rows()), (yb, (tm, BW), _rows()), (yc, (tm, BW), _rows())]
        + [(proj, (tm, BW), _rows(7 + k)) for k in range(6)]
        + [(W["pfull"], (None, 3, BW, D), _const(l, 0, 0, 0)), (W["woutfull"], (None, D, D), _const(l, 0, 0))],
        [((L, D), F32, (tm, D), _rows(), False)])[0]
    x3 = ffn_fwd(f"ffn_fwd_{l}1", x2, W, l, 1, 2, L, tm)
    saved = dict(x0=x0, x1=x1, x2=x2, proj=proj, xr=xr, xi=xi, o=o, sst=sst, xc=xc, a=a, hs=hs, ya=ya, yb=yb, yc=yc,
                 qzv=qzv)
    return x3, saved


def layer_bwd(l, dx3, sv, W, P, bufs, L, tm):
    D = D_MODEL
    n_i = L // tm
    nb = S5_N // LANE
    dq = D // NSH
    vec = (None, 1, BW)
    vout = ((1, BW), (1, BW), _const(0, 0), "acc_all")
    blk3 = lambda s, i: (0, i, 0)
    small = {}
    proj = sv["proj"]

    dx2, dnw2 = ffn_bwd(f"ffn_bwd_{l}1", sv["x2"], dx3, W, bufs, l, 1, 2, L, tm)

    rw256 = ((L, BW), (tm, BW), _rows(), "write")
    res = tile_bwd(
        merge_core, f"merge_bwd_{l}", n_i, 1,
        [(sv["ya"], (tm, BW), _rows(), "r"), (sv["yb"], (tm, BW), _rows(), "r"), (sv["yc"], (tm, BW), _rows(), "r")]
        + [(proj, (tm, BW), _rows(7 + k), "r") for k in range(6)]
        + [(W["pfull"], (None, 3, BW, D), _const(l, 0, 0, 0), "w"), (W["woutfull"], (None, D, D), _const(l, 0, 0), "w")],
        [(dx2, (tm, D), _rows())],
        [rw256] * 9
        + [((DEPTH, 3, BW, D), (None, 3, BW, D), _const(l, 0, 0, 0), "acc_all", bufs.get("branch_proj")),
           ((DEPTH, D, D), (None, D, D), _const(l, 0, 0), "acc_all", bufs.get("w_out"))])
    dya, dyb, dyc = res[:3]
    dgm = res[3:9]
    bufs["branch_proj"], bufs["w_out"] = res[9:]

    tmm = min(tm, 128)
    rw = ((L, BW), (tmm, BW), _rows(), "write")
    xw = ((nb, L, LANE), (nb, tmm, LANE), blk3, "write")
    res = tile_bwd(
        mid_core, f"mid_bwd_{l}", L // tmm, 1,
        [(sv["xr"], (nb, tmm, LANE), blk3, "r"), (sv["xi"], (nb, tmm, LANE), blk3, "r"), (proj, (tmm, BW), _rows(0), "r"),
         (sv["o"], (tmm, BW), _rows(), "r"), (proj, (tmm, BW), _rows(4), "r"), (sv["hs"], (tmm, BW), _rows(), "r"),
         (proj, (tmm, BW), _rows(6), "r"), (P["hmat"], (BW, BW), _const(0, 0), "c"),
         (P["cmat"][l], (2 * S5_N, BW), _const(0, 0), "w"), (P["d"], vec, _const(l, 0, 0), "p"),
         (W["gluw"], (None, BW, BW), _const(l, 0, 0), "w"), (P["glub"], vec, _const(l, 0, 0), "p"),
         (P["hgw"], vec, _const(l, 0, 0), "p")],
        [(dya, (tmm, BW), _rows()), (dyb, (tmm, BW), _rows()), (dyc, (tmm, BW), _rows())],
        [xw, xw, rw, rw, rw, rw, rw, ((2 * S5_N, BW), (2 * S5_N, BW), _const(0, 0), "acc_all"), vout,
         ((DEPTH, BW, BW), (None, BW, BW), _const(l, 0, 0), "acc_all", bufs.get("s5_glu_w")), vout, vout])
    dxr, dxi, du_skip, do, dg_b, dhs, dgate_c, dcmat, dd, bufs["s5_glu_w"], dglub, dhgw = res
    small["s5_d"], small["s5_glu_b"], small["hg_norm_w"] = dd[0], dglub[0], dhgw[0]

    da, db = rg_scan_bwd(sv["a"], sv["hs"], dhs, L)
    wmat = ((BW, BW), (BW, BW), _const(0, 0), "acc_all")
    res = tile_bwd(
        gates_core, f"gates_bwd_{l}", n_i, 1,
        [(sv["xc"], (tm, BW), _rows(), "r"), (P["wa"][l], (BW, BW), _const(0, 0), "w"), (P["ba"], vec, _const(l, 0, 0), "p"),
         (P["wx"][l], (BW, BW), _const(0, 0), "w"), (P["bx"], vec, _const(l, 0, 0), "p"), (P["lam"], vec, _const(l, 0, 0), "p")],
        [(da, (tm, BW), _rows()), (db, (tm, BW), _rows())],
        [((L, BW), (tm, BW), _rows(), "write"), wmat, vout, wmat, vout, vout])
    dxc, dwa, dba, dwx, dbx, dlam = res
    small["rg_ba"], small["rg_bx"], small["rg_lambda"] = dba[0], dbx[0], dlam[0]
    dx_c, dconvw, dconvb = conv_bwd(proj, W["convw"][l], dxc, L)
    small["rg_conv_w"], small["rg_conv_b"] = dconvw, dconvb[0]

    dq_b, dz_b, dv_b, dlb = hg_bwd(sv["qzv"], P["lb"][l], sv["sst"], _to_time_order(do), L)
    dqzv = _to_segment_order(jnp.concatenate([dq_b, dz_b, dv_b], axis=1))

    gr, gi, dar, dai = s5_scan_bwd(dxr, dxi, sv["xr"], sv["xi"], P["ar"][l], P["ai"][l], L)
    du_pre, dbmat = tile_bwd(
        s5_pre_core, f"s5pre_bwd_{l}", n_i, 1,
        [(proj, (tm, BW), _rows(0), "r"), (P["bmat"][l], (BW, 2 * S5_N), _const(0, 0), "w")],
        [(gr, (nb, tm, LANE), blk3), (gi, (nb, tm, LANE), blk3)],
        [((L, BW), (tm, BW), _rows(), "write"), ((BW, 2 * S5_N), (BW, 2 * S5_N), _const(0, 0), "acc_all")])
    du_a = add_n(f"du_a_{l}", [(du_skip, ()), (du_pre, ())], (L, BW))
    prep_ct = dict(dar=dar, dai=dai, dbmat=dbmat, dcmat=dcmat, dwa=dwa, dwx=dwx, dlb=dlb)

    dproj = jnp.concatenate([du_a, dqzv, dg_b, dx_c, dgate_c, *dgm], axis=1)
    part, dnw1, bufs["w_in"] = tile_bwd(
        pre_core, f"pre_bwd_{l}", n_i, NSH,
        [(sv["x1"], (tm, D), _rows(), "r"), (W["nw"], (None, None, 1, D), _const(l, 1, 0, 0), "p"),
         (W["win"], (None, None, D, IN_TOTAL // NSH), lambda s, i: (l, s, 0, 0), "w")],
        [(dproj, (tm, IN_TOTAL // NSH), lambda s, i: (i, s))],
        [((NSH, L, D), (None, tm, D), lambda s, i: (s, i, 0), "write"), ((1, D), (1, D), _const(0, 0), "acc_all"),
         ((DEPTH, NSH, D, IN_TOTAL // NSH), (None, None, D, IN_TOTAL // NSH), lambda s, i: (l, s, 0, 0), "acc_i",
          bufs.get("w_in"))])
    dx1 = _sum_parts(f"pre_bwd_{l}_dx", dx2, part, (L, D))

    dx0, dnw0 = ffn_bwd(f"ffn_bwd_{l}0", sv["x0"], dx1, W, bufs, l, 0, 0, L, tm)
    small["norm_w"] = jnp.concatenate([dnw0, dnw1, dnw2], axis=0)
    return dx0, small, prep_ct


SMALL_RAW = ("s5_lambda_re", "s5_lambda_im", "s5_log_dt", "s5_b_re", "s5_b_im", "s5_c_re", "s5_c_im", "s5_d", "s5_glu_b",
             "hg_lb_logits", "hg_norm_w", "rg_conv_b", "rg_wa", "rg_ba", "rg_wx", "rg_bx", "rg_lambda", "final_norm_w")
DEPTH = 2


def local_step(x, target, W, raw):
    L = x.shape[0]
    tm = min(256, L)
    col = lambda v: v.reshape(DEPTH, 1, BW)
    s5_out, s5_vjp = [], []
    for l in range(DEPTH):
        out, vjp = jax.vjp(s5_prep, *[raw[k][l] for k in SMALL_RAW[:7]])
        s5_out.append(out)
        s5_vjp.append(vjp)
    (wa, wx), rg_vjp = jax.vjp(lambda a, b: (jax.vmap(rg_prep)(a), jax.vmap(rg_prep)(b)), raw["rg_wa"], raw["rg_wx"])
    lb, hg_vjp = jax.vjp(hg_prep, raw["hg_lb_logits"])
    P = dict(
        ar=[o[0] for o in s5_out], ai=[o[1] for o in s5_out],
        bmat=[o[2].astype(MMT) for o in s5_out], cmat=[o[3].astype(MMT) for o in s5_out],
        lb=[lb[l].reshape(1, BW) for l in range(DEPTH)], convb=[raw["rg_conv_b"][l].reshape(1, BW) for l in range(DEPTH)],
        wa=[wa[l].astype(MMT) for l in range(DEPTH)], wx=[wx[l].astype(MMT) for l in range(DEPTH)],
        ba=col(raw["rg_ba"]), bx=col(raw["rg_bx"]), lam=col(raw["rg_lambda"]), d=col(raw["s5_d"]),
        glub=col(raw["s5_glu_b"]), hgw=col(raw["hg_norm_w"]), hmat=_head_mean_matrix())

    saved = []
    h = _to_segment_order(x)
    for l in range(DEPTH):
        h, sv = layer_fwd(l, h, W, P, L, tm)
        saved.append(sv)
    loss, dh, dfw = loss_fwd_bwd(h, raw["final_norm_w"].reshape(1, D_MODEL), _to_segment_order(target), L, tm)

    big, per_layer, prep_cts = {}, [None] * DEPTH, [None] * DEPTH
    for l in reversed(range(DEPTH)):
        dh, sm, pc = layer_bwd(l, dh, saved[l], W, P, big, L, tm)
        per_layer[l], prep_cts[l] = sm, pc
    dh = _to_time_order(dh)

    small = {k: jnp.stack([per_layer[l][k] for l in range(DEPTH)]) for k in per_layer[0]}
    s5_g = [s5_vjp[l]((prep_cts[l]["dar"], prep_cts[l]["dai"], prep_cts[l]["dbmat"], prep_cts[l]["dcmat"])) for l in range(DEPTH)]
    for j, k in enumerate(SMALL_RAW[:7]):
        small[k] = jnp.stack([s5_g[l][j] for l in range(DEPTH)])
    small["rg_wa"], small["rg_wx"] = rg_vjp((jnp.stack([prep_cts[l]["dwa"] for l in range(DEPTH)]),
                                             jnp.stack([prep_cts[l]["dwx"] for l in range(DEPTH)])))
    (small["hg_lb_logits"],) = hg_vjp(jnp.concatenate([prep_cts[l]["dlb"] for l in range(DEPTH)], axis=0))
    small["final_norm_w"] = dfw[0]
    return loss, dh, big, small


ANY = pl.BlockSpec(memory_space=pl.ANY)


def _place():
    x, y, c = lax.axis_index("x"), lax.axis_index("y"), lax.axis_index("c")
    chips = [(1 - x, y), (x, 1 - y), (1 - x, 1 - y)]
    return x, y, c, chips


def _remote(src, dst, send, recv, k, to):
    return pltpu.make_async_remote_copy(src_ref=src, dst_ref=dst, send_sem=send.at[k], recv_sem=recv.at[k],
                                        device_id=to, device_id_type=MESH)


def _comm_call(body, name, ins, out_shapes, n_sem, n_loc):
    return pl.pallas_call(
        body, name=name, in_specs=[ANY] * len(ins), out_specs=[ANY] * len(out_shapes), out_shape=out_shapes,
        scratch_shapes=[pltpu.SemaphoreType.DMA((n_sem,)), pltpu.SemaphoreType.DMA((n_sem,)),
                        pltpu.SemaphoreType.DMA((max(n_loc, 1),))],
    )(*ins)


def gather_shards(name, shards):
    n = len(shards)
    per = 8

    def body(*refs):
        ins, outs = refs[:n], refs[n:2 * n]
        send, recv, _ = refs[2 * n:]
        x, y, c, chips = _place()
        me = 2 * x + y
        sib = (x, y, 1 - c)
        sends = []
        for w in range(n):
            for j, (cx, cy) in enumerate(chips):
                cp = _remote(ins[w].at[c], outs[w].at[c, me], send, recv, per * w + j, (cx, cy, c))
                cp.start()
                sends.append(cp)
        for w in range(n):
            for l in range(2):
                cp = _remote(ins[w].at[l], outs[w].at[l, me], send, recv, per * w + 6 + l, sib)
                cp.start()
                sends.append(cp)
        for w in range(n):
            for j, (cx, cy) in enumerate(chips):
                theirs = outs[w].at[c, 2 * cx + cy]
                _remote(ins[w].at[c], theirs, send, recv, per * w + j, (cx, cy, c)).wait_recv()
                cp = _remote(theirs, theirs, send, recv, per * w + 3 + j, sib)
                cp.start()
                sends.append(cp)
        for w in range(n):
            for j, (cx, cy) in enumerate(chips):
                dst = outs[w].at[1 - c, 2 * cx + cy]
                _remote(dst, dst, send, recv, per * w + 3 + j, sib).wait_recv()
            for l in range(2):
                dst = outs[w].at[l, me]
                _remote(dst, dst, send, recv, per * w + 6 + l, sib).wait_recv()
        for cp in sends:
            cp.wait_send()

    shapes = [jax.ShapeDtypeStruct((2, NSH) + s.shape[1:], s.dtype) for s in shards]
    return _comm_call(body, name, shards, shapes, per * n, 0)


def exchange_halves(name, grads):
    n = len(grads)

    def body(*refs):
        ins, outs = refs[:n], refs[n:2 * n]
        send, recv, _ = refs[2 * n:]
        x, y, c, _chips = _place()
        cps = []
        for w in range(n):
            h = grads[w].shape[2] // 2
            cp = _remote(ins[w].at[:, :, pl.ds((1 - c) * h, h)], outs[w], send, recv, w, (x, y, 1 - c))
            cp.start()
            cps.append(cp)
        for cp in cps:
            cp.wait()

    shapes = [jax.ShapeDtypeStruct(g.shape[:2] + (g.shape[2] // 2, g.shape[3]), g.dtype) for g in grads]
    return _comm_call(body, name, grads, shapes, n, 0)


def scatter_to_chips(name, halves):
    n = len(halves)

    def body(*refs):
        ins, outs = refs[:n], refs[n:2 * n]
        send, recv, _ = refs[2 * n:]
        x, y, c, chips = _place()
        cps = []
        for w in range(n):
            for j, (cx, cy) in enumerate(chips):
                cp = _remote(ins[w].at[:, 2 * cx + cy], outs[w].at[j], send, recv, 3 * w + j, (cx, cy, c))
                cp.start()
                cps.append(cp)
        for cp in cps:
            cp.wait()

    shapes = [jax.ShapeDtypeStruct((3, h.shape[0]) + h.shape[2:], h.dtype) for h in halves]
    return _comm_call(body, name, halves, shapes, 3 * n, 0)


def share_halves(name, pieces):
    n = len(pieces)

    def body(*refs):
        ins, outs = refs[:n], refs[n:2 * n]
        send, recv, _ = refs[2 * n:]
        x, y, c, _chips = _place()
        cps = []
        for w in range(n):
            cp = _remote(ins[w], outs[w], send, recv, w, (x, y, 1 - c))
            cp.start()
            cps.append(cp)
        for cp in cps:
            cp.wait()

    return _comm_call(body, name, pieces, [jax.ShapeDtypeStruct(p.shape, p.dtype) for p in pieces], n, 0)


def add_own_half(name, g, ra, c, wire):
    nblk, h, cols = ra.shape
    tr = _row_tile(h, cols, mult=16)
    nt = h // tr

    def body(c_ref, g_ref, r_ref, o_ref):
        o_ref[...] = (g_ref[...] + r_ref[...]).astype(o_ref.dtype)

    blk = (None, tr, cols)
    return pl.pallas_call(
        body, name=name,
        grid_spec=pltpu.PrefetchScalarGridSpec(
            num_scalar_prefetch=1, grid=(nblk, nt),
            in_specs=[pl.BlockSpec(blk, lambda s, i, c_ref: (s, c_ref[0] * nt + i, 0)), pl.BlockSpec(blk, lambda s, i, c_ref: (s, i, 0))],
            out_specs=pl.BlockSpec(blk, lambda s, i, c_ref: (s, i, 0))),
        out_shape=jax.ShapeDtypeStruct(ra.shape, wire),
    )(c.reshape(1), g, ra)


def add_chips(name, hb, rb, me):
    npc, _, h, cols = hb.shape
    tr = _row_tile(h, cols, mult=16)

    def body(me_ref, h_ref, r0, r1, r2, o_ref):
        f = lambda r: r[...].astype(F32)
        o_ref[...] = ((f(h_ref) + f(r0)) + f(r1)) + f(r2)

    rspec = lambda j: pl.BlockSpec((None, None, tr, cols), functools.partial(lambda p, i, me_ref, j: (j, p, i, 0), j=j))
    return pl.pallas_call(
        body, name=name,
        grid_spec=pltpu.PrefetchScalarGridSpec(
            num_scalar_prefetch=1, grid=(npc, h // tr),
            in_specs=[pl.BlockSpec((None, None, tr, cols), lambda p, i, me_ref: (p, me_ref[0], i, 0)), rspec(0), rspec(1), rspec(2)],
            out_specs=pl.BlockSpec((None, tr, cols), lambda p, i, me_ref: (p, i, 0))),
        out_shape=jax.ShapeDtypeStruct((npc, h, cols), F32),
    )(me.reshape(1), hb, rb, rb, rb)


def adamw_halves(name, w, m, v, own, other, c):
    npc, rows, cols = w.shape
    h = rows // 2
    tr = _row_tile(h, cols, budget=1024 * 1024)
    nt = h // tr
    c1 = 1.0 - ADAM_B1 ** ADAM_STEP
    c2 = 1.0 - ADAM_B2 ** ADAM_STEP

    def body(c_ref, w_ref, m_ref, v_ref, own_ref, oth_ref, g_ref, d_ref, nm_ref, nv_ref):
        g = jnp.where(pl.program_id(1) == c_ref[0], own_ref[...], oth_ref[...])
        nm = ADAM_B1 * m_ref[...] + (1.0 - ADAM_B1) * g
        nv = ADAM_B2 * v_ref[...] + (1.0 - ADAM_B2) * jnp.square(g)
        g_ref[...] = g
        d_ref[...] = -ADAM_LR * ((nm / c1) / (jnp.sqrt(nv / c2) + ADAM_EPS) + ADAM_WD * w_ref[...])
        nm_ref[...] = nm
        nv_ref[...] = nv

    full = pl.BlockSpec((None, tr, cols), lambda p, hh, i, c_ref: (p, hh * nt + i, 0))
    half = pl.BlockSpec((None, tr, cols), lambda p, hh, i, c_ref: (p, i, 0))
    return pl.pallas_call(
        body, name=name,
        grid_spec=pltpu.PrefetchScalarGridSpec(
            num_scalar_prefetch=1, grid=(npc, 2, nt),
            in_specs=[full, full, full, half, half], out_specs=[full] * 4),
        out_shape=[jax.ShapeDtypeStruct(w.shape, F32)] * 4,
    )(c.reshape(1), w, m, v, own, other)


WEIGHTS = ("norm_w", "final_norm_w", "ffn_gate", "ffn_up", "ffn_down", "w_in", "branch_proj", "w_out", "s5_lambda_re",
           "s5_lambda_im", "s5_log_dt", "s5_b_re", "s5_b_im", "s5_c_re", "s5_c_im", "s5_d", "s5_glu_w", "s5_glu_b",
           "hg_lb_logits", "hg_norm_w", "rg_conv_w", "rg_conv_b", "rg_wa", "rg_ba", "rg_wx", "rg_bx", "rg_lambda")
BIG = ("ffn_gate", "ffn_up", "ffn_down", "w_in", "branch_proj", "w_out", "s5_glu_w")
SHARDED_SMALL = ("norm_w", "rg_conv_w")
SMALL = SMALL_RAW + SHARDED_SMALL


def _pack(arrays, multiple):
    flat = jnp.concatenate([a.reshape(-1) for a in arrays])
    total = flat.shape[0]
    padded = -(-total // multiple) * multiple
    return jnp.pad(flat, (0, padded - total))


def _unpack(flat, like):
    out, at = [], 0
    for a in like:
        size = math.prod(a.shape)
        out.append(flat[at:at + size].reshape(a.shape))
        at += size
    return out


def _step(x, target, w, m, v):
    mx, my, mc = lax.axis_index("x"), lax.axis_index("y"), lax.axis_index("c")
    me = (2 * mx + my).astype(jnp.int32)
    mc = mc.astype(jnp.int32)

    gathered = gather_shards("gather_weights", [w[n].astype(MMT) for n in BIG] + [w[n] for n in SHARDED_SMALL])
    W = dict(wg=gathered[0], wu=gathered[1], wd=gathered[2], win=gathered[3],
             pfull=gathered[4].transpose(0, 2, 3, 1, 4).reshape(DEPTH, 3, BW, D_MODEL),
             woutfull=gathered[5].reshape(DEPTH, D_MODEL, D_MODEL),
             gluw=gathered[6].reshape(DEPTH, BW, BW),
             nw=gathered[7].transpose(0, 2, 1, 3).reshape(DEPTH, 3, 1, D_MODEL),
             convw=gathered[8].transpose(0, 2, 1, 3).reshape(DEPTH, CONV_W, BW))
    loss, dx, big, small = local_step(x[0], target[0], W, {k: w[k] for k in SMALL_RAW})

    small_flat = _pack([small[n] for n in SMALL], NSH * 32 * LANE)
    dq = D_MODEL // NSH
    big["branch_proj"] = big["branch_proj"].reshape(DEPTH, 3, BW, NSH, dq).transpose(0, 3, 1, 2, 4).reshape(DEPTH, NSH, 3 * BW, dq)
    grads =[big[n].reshape(big[n].shape[0], NSH, -1, big[n].shape[-1]) for n in BIG] + [small_flat.reshape(1, NSH, -1, LANE)]
    from_sibling = exchange_halves("reduce_cores", grads)
    merge = lambda a: a.reshape((-1,) + a.shape[2:])
    wire = [jnp.bfloat16] * len(BIG) + [F32]
    halves = [add_own_half(f"sum_cores_{i}", merge(g), merge(r), mc, wire[i]).reshape(r.shape)
              for i, (g, r) in enumerate(zip(grads, from_sibling))]
    from_chips = scatter_to_chips("reduce_chips", halves)
    own = [add_chips(f"sum_chips_{i}", h, r, me) for i, (h, r) in enumerate(zip(halves, from_chips))]
    other = share_halves("reduce_share", own)

    g, delta, new_m, new_v = {}, {}, {}, {}
    for i, n in enumerate(BIG):
        view = lambda a: a.reshape(own[i].shape[0], -1, own[i].shape[2])
        res = adamw_halves(f"adamw_{n}", view(w[n]), view(m[n]), view(v[n]), own[i], other[i], mc)
        g[n], delta[n], new_m[n], new_v[n] = [a.reshape(w[n].shape) for a in res]

    piece = jnp.stack([jnp.where(mc == 0, own[-1][0], other[-1][0]), jnp.where(mc == 0, other[-1][0], own[-1][0])])
    (all_small,) = gather_shards("gather_small", [piece])
    full_small = _unpack(all_small.transpose(1, 0, 2, 3).reshape(-1), [small[n] for n in SMALL])
    g.update(zip(SMALL, full_small))
    g["norm_w"] = lax.dynamic_slice_in_dim(g["norm_w"], me * (D_MODEL // NSH), D_MODEL // NSH, axis=2)
    g["rg_conv_w"] = lax.dynamic_slice_in_dim(g["rg_conv_w"], me * (BW // NSH), BW // NSH, axis=2)

    packed = [_pack([src[n] for n in SMALL], 8 * LANE).reshape(-1, LANE) for src in (w, g, m, v)]
    for dst, flat in zip((delta, new_m, new_v), adamw(*packed)):
        dst.update(zip(SMALL, _unpack(flat.reshape(-1), [w[n] for n in SMALL])))

    total = lax.psum(loss[0, 0], ("x", "y", "c"))
    return (total, dx[None], *[g[n] for n in WEIGHTS], *[delta[n] for n in WEIGHTS],
            *[new_m[n] for n in WEIGHTS], *[new_v[n] for n in WEIGHTS])


def kernel(x, norm_w, final_norm_w, ffn_gate, ffn_up, ffn_down, w_in, branch_proj, w_out, s5_lambda_re, s5_lambda_im, s5_log_dt, s5_b_re, s5_b_im, s5_c_re, s5_c_im, s5_d, s5_glu_w, s5_glu_b, hg_lb_logits, hg_norm_w, rg_conv_w, rg_conv_b, rg_wa, rg_ba, rg_wx, rg_bx, rg_lambda, loss_target, m_norm_w, m_final_norm_w, m_ffn_gate, m_ffn_up, m_ffn_down, m_w_in, m_branch_proj, m_w_out, m_s5_lambda_re, m_s5_lambda_im, m_s5_log_dt, m_s5_b_re, m_s5_b_im, m_s5_c_re, m_s5_c_im, m_s5_d, m_s5_glu_w, m_s5_glu_b, m_hg_lb_logits, m_hg_norm_w, m_rg_conv_w, m_rg_conv_b, m_rg_wa, m_rg_ba, m_rg_wx, m_rg_bx, m_rg_lambda, v_norm_w, v_final_norm_w, v_ffn_gate, v_ffn_up, v_ffn_down, v_w_in, v_branch_proj, v_w_out, v_s5_lambda_re, v_s5_lambda_im, v_s5_log_dt, v_s5_b_re, v_s5_b_im, v_s5_c_re, v_s5_c_im, v_s5_d, v_s5_glu_w, v_s5_glu_b, v_hg_lb_logits, v_hg_norm_w, v_rg_conv_w, v_rg_conv_b, v_rg_wa, v_rg_ba, v_rg_wx, v_rg_bx, v_rg_lambda):
    ws = (norm_w, final_norm_w, ffn_gate, ffn_up, ffn_down, w_in, branch_proj, w_out, s5_lambda_re, s5_lambda_im, s5_log_dt, s5_b_re, s5_b_im, s5_c_re, s5_c_im, s5_d, s5_glu_w, s5_glu_b, hg_lb_logits, hg_norm_w, rg_conv_w, rg_conv_b, rg_wa, rg_ba, rg_wx, rg_bx, rg_lambda)
    ms = (m_norm_w, m_final_norm_w, m_ffn_gate, m_ffn_up, m_ffn_down, m_w_in, m_branch_proj, m_w_out, m_s5_lambda_re, m_s5_lambda_im, m_s5_log_dt, m_s5_b_re, m_s5_b_im, m_s5_c_re, m_s5_c_im, m_s5_d, m_s5_glu_w, m_s5_glu_b, m_hg_lb_logits, m_hg_norm_w, m_rg_conv_w, m_rg_conv_b, m_rg_wa, m_rg_ba, m_rg_wx, m_rg_bx, m_rg_lambda)
    vs = (v_norm_w, v_final_norm_w, v_ffn_gate, v_ffn_up, v_ffn_down, v_w_in, v_branch_proj, v_w_out, v_s5_lambda_re, v_s5_lambda_im, v_s5_log_dt, v_s5_b_re, v_s5_b_im, v_s5_c_re, v_s5_c_im, v_s5_d, v_s5_glu_w, v_s5_glu_b, v_hg_lb_logits, v_hg_norm_w, v_rg_conv_w, v_rg_conv_b, v_rg_wa, v_rg_ba, v_rg_wx, v_rg_bx, v_rg_lambda)
    return _step(x, loss_target, dict(zip(WEIGHTS, ws)), dict(zip(WEIGHTS, ms)), dict(zip(WEIGHTS, vs)))
```

```python
import functools
import math
from typing import NamedTuple

import jax
import jax.numpy as jnp
from jax import lax
from jax.experimental import pallas as pl
from jax.experimental.pallas import tpu as pltpu

F32 = jnp.float32
MMT = jnp.bfloat16
HI = lax.Precision.HIGHEST

D_MODEL = 1024
BW = 512
S5_GROUP, S5_GROUPS, S5_STATE = 16, 32, 64
S5_N = S5_GROUPS * S5_STATE
HG_HEADS, HG_D = 4, 128
HG_CHUNK = 64
RG_BLOCKS, RG_BLOCK = 8, 64
RG_C = 8.0
CONV_W = 4
D_FF = 2816
EPS = 1e-6
IN_TOTAL = 6656
NSH = 4
NSEG = 8
LANE = 128
VMEM_LIMIT = 56 * 1024 * 1024

ADAM_LR, ADAM_B1, ADAM_B2, ADAM_EPS, ADAM_WD, ADAM_STEP = 0.001, 0.9, 0.999, 1e-08, 0.01, 10

MESH = pl.DeviceIdType.MESH


class WP(NamedTuple):
    w: jax.Array
    p: jax.Array


def _dg(a, b, ca, cb):
    return lax.dot_general(a, b, (((ca,), (cb,)), ((), ())), preferred_element_type=F32)


@jax.custom_vjp
def _mmw(a, w, p):
    return _dg(a.astype(MMT), w, 1, 0)


def _mmw_fwd(a, w, p):
    return _mmw(a, w, p), (a, w)


def _mmw_bwd(res, g):
    a, w = res
    gb = g.astype(MMT)
    return _dg(gb, w, 1, 1), jnp.zeros_like(w), _dg(a.astype(MMT), gb, 0, 0)


_mmw.defvjp(_mmw_fwd, _mmw_bwd)


def mm(a, w):
    if isinstance(w, WP):
        return _mmw(a, w.w, w.p)
    return _dg(a.astype(MMT), w, 1, 0)


@jax.custom_vjp
def mma_nn(a, b):
    return _dg(a.astype(MMT), b.astype(MMT), 1, 0)


def _nn_f(a, b):
    return mma_nn(a, b), (a, b)


def _nn_b(res, g):
    a, b = res
    gb = g.astype(MMT)
    return _dg(gb, b.astype(MMT), 1, 1), _dg(a.astype(MMT), gb, 0, 0)


mma_nn.defvjp(_nn_f, _nn_b)


@jax.custom_vjp
def mma_nt(a, b):
    return _dg(a.astype(MMT), b.astype(MMT), 1, 1)


def _nt_f(a, b):
    return mma_nt(a, b), (a, b)


def _nt_b(res, g):
    a, b = res
    gb = g.astype(MMT)
    return _dg(gb, b.astype(MMT), 1, 0), _dg(gb, a.astype(MMT), 0, 0)


mma_nt.defvjp(_nt_f, _nt_b)


@jax.custom_vjp
def mma_tn(a, b):
    return _dg(a.astype(MMT), b.astype(MMT), 0, 0)


def _tn_f(a, b):
    return mma_tn(a, b), (a, b)


def _tn_b(res, g):
    a, b = res
    gb = g.astype(MMT)
    return _dg(b.astype(MMT), gb, 1, 1), _dg(a.astype(MMT), gb, 1, 0)


mma_tn.defvjp(_tn_f, _tn_b)


def mm_exact(m, x):
    return jnp.dot(m, x, precision=HI, preferred_element_type=F32)


def _rms(x, w):
    return x * lax.rsqrt(jnp.mean(x * x, axis=-1, keepdims=True) + EPS) * w


def _expm1(x):
    series = x * (1.0 + x * (1.0 / 2) * (1.0 + x * (1.0 / 3) * (1.0 + x * (1.0 / 4) * (1.0 + x * (1.0 / 5) * (1.0 + x * (1.0 / 6))))))
    return jnp.where(jnp.abs(x) < 0.1, series, jnp.exp(x) - 1.0)


def _bspec(block, fn, order):
    if order == "is":
        return pl.BlockSpec(block, lambda i, s: fn(s, i))
    return pl.BlockSpec(block, lambda s, i: fn(s, i))


def tile_fwd(fn, name, n_i, n_s, ins, outs, s_outer=False):
    n_in = len(ins)
    order = "si" if s_outer else "is"
    assert not (s_outer and any(o[4] for o in outs))

    def body(*refs):
        s = pl.program_id(0 if s_outer else 1)
        res = fn(*[r[...] for r in refs[:n_in]], s)
        for o_ref, val, spec in zip(refs[n_in:], res, outs):
            if spec[4] and n_s > 1:
                @pl.when(s == 0)
                def _(o_ref=o_ref, val=val):
                    o_ref[...] = val.astype(o_ref.dtype)

                @pl.when(s != 0)
                def _(o_ref=o_ref, val=val):
                    o_ref[...] += val.astype(o_ref.dtype)
            else:
                o_ref[...] = val.astype(o_ref.dtype)

    return pl.pallas_call(
        body, grid=(n_s, n_i) if s_outer else (n_i, n_s), name=name,
        in_specs=[_bspec(b, f, order) for _, b, f in ins],
        out_specs=[_bspec(b, f, order) for _, _, b, f, _ in outs],
        out_shape=[jax.ShapeDtypeStruct(sh, dt) for sh, dt, _, _, _ in outs],
        compiler_params=pltpu.CompilerParams(vmem_limit_bytes=VMEM_LIMIT,
                                             dimension_semantics=("arbitrary", "arbitrary")),
    )(*[a for a, _, _ in ins])


def tile_bwd(fn, name, n_i, n_s, ins, cts, gouts):
    n_in, n_ct = len(ins), len(cts)
    kinds = [k for _, _, _, k in ins]
    d_pos = [j for j, k in enumerate(kinds) if k != "c"]
    shared = [(gi, spec[4]) for gi, spec in enumerate(gouts) if len(spec) == 5 and spec[4] is not None]
    n_sh = len(shared)

    def body(*refs):
        s, i = pl.program_id(0), pl.program_id(1)
        vals = [r[...] for r in refs[:n_in]]
        ctv = tuple(r[...] for r in refs[n_in:n_in + n_ct])
        g_refs = refs[n_in + n_ct + n_sh:]

        def g(*dv):
            args = list(vals)
            for j, v in zip(d_pos, dv):
                args[j] = WP(vals[j], v) if kinds[j] == "w" else v
            return tuple(fn(*args))

        dv0 = [jnp.zeros(vals[j].shape, F32) if kinds[j] == "w" else vals[j] for j in d_pos]
        _, vjp = jax.vjp(g, *dv0)
        grads = vjp(ctv)
        for g_ref, gv, spec in zip(g_refs, grads, gouts):
            mode = spec[3]
            if mode == "write":
                g_ref[...] = gv.astype(g_ref.dtype)
            else:
                first = (i == 0) if mode == "acc_i" else jnp.logical_and(i == 0, s == 0)

                @pl.when(first)
                def _(g_ref=g_ref, gv=gv):
                    g_ref[...] = gv.astype(g_ref.dtype)

                @pl.when(jnp.logical_not(first))
                def _(g_ref=g_ref, gv=gv):
                    g_ref[...] += gv.astype(g_ref.dtype)

    return pl.pallas_call(
        body, grid=(n_s, n_i), name=name,
        in_specs=([_bspec(b, f, "si") for _, b, f, _ in ins] + [_bspec(b, f, "si") for _, b, f in cts]
                  + [pl.BlockSpec(memory_space=pl.ANY)] * n_sh),
        out_specs=[_bspec(spec[1], spec[2], "si") for spec in gouts],
        out_shape=[jax.ShapeDtypeStruct(spec[0], F32) for spec in gouts],
        input_output_aliases={n_in + n_ct + k: gi for k, (gi, _) in enumerate(shared)},
        compiler_params=pltpu.CompilerParams(vmem_limit_bytes=VMEM_LIMIT,
                                             dimension_semantics=("arbitrary", "arbitrary")),
    )(*[a for a, _, _, _ in ins], *[a for a, _, _ in cts], *[buf for _, buf in shared])


def _row_tile(rows, width, itemsize=4, budget=2 * 1024 * 1024, mult=8):
    best = mult
    for t in range(mult, rows + 1, mult):
        if rows % t == 0 and t * width * itemsize <= budget:
            best = t
    return best


def add_n(name, terms, shape):
    rows, cols = shape
    tr = _row_tile(rows, cols)

    def body(*refs):
        acc = refs[0][...]
        for r in refs[1:-1]:
            acc = acc + r[...]
        refs[-1][...] = acc

    specs = []
    for _, lead in terms:
        specs.append(pl.BlockSpec((None,) * len(lead) + (tr, cols), functools.partial(lambda i, lead: (*lead, i, 0), lead=lead)))
    return pl.pallas_call(
        body, grid=(rows // tr,), name=name, in_specs=specs,
        out_specs=pl.BlockSpec((tr, cols), lambda i: (i, 0)),
        out_shape=jax.ShapeDtypeStruct((rows, cols), F32),
    )(*[a for a, _ in terms])


def ffn_core(x, nw, wg, wu, wd):
    h = _rms(x, nw)
    return (0.5 * mm(jax.nn.silu(mm(h, wg)) * mm(h, wu), wd),)


def pre_core(x, nw, win):
    return (mm(_rms(x, nw), win),)


def _split_lanes(y):
    return jnp.stack([y[:, k * LANE:(k + 1) * LANE] for k in range(y.shape[1] // LANE)], axis=0)


def _join_lanes(y3):
    return jnp.concatenate([y3[k] for k in range(y3.shape[0])], axis=1)


def s5_pre_core(u, bmat):
    bu = mm(u, bmat)
    return _split_lanes(bu[:, :S5_N]), _split_lanes(bu[:, S5_N:])


def mid_core(xr, xi, u, o, g, hs, gc, hmat, cmat, d, gluw, glub, hgw):
    xs = jnp.concatenate([_join_lanes(xr), _join_lanes(xi)], axis=1)
    y = mm(xs, cmat) + d * u
    z = jax.nn.gelu(y)
    ya = z * jax.nn.sigmoid(mm(z, gluw) + glub)
    ms = mm_exact(o * o, hmat)
    yb = o * lax.rsqrt(ms + EPS) * hgw * jax.nn.silu(g)
    yc = hs * jax.nn.gelu(gc)
    return ya, yb, yc


def _sub(w, n):
    return WP(w.w[n], w.p[n]) if isinstance(w, WP) else w[n]


def merge_core(ya, yb, yc, g0, g1, g2, g3, g4, g5, p, wout):
    gate = lambda a, b: jax.nn.sigmoid(jnp.concatenate([a, b], axis=1))
    m = gate(g0, g1) * mm(ya, _sub(p, 0)) + gate(g2, g3) * mm(yb, _sub(p, 1)) + gate(g4, g5) * mm(yc, _sub(p, 2))
    return (mm(m, wout),)


def gates_core(xc, wa, ba, wx, bx, lam):
    r = jax.nn.sigmoid(mm(xc, wa) + ba)
    i = jax.nn.sigmoid(mm(xc, wx) + bx)
    log_a = -RG_C * jax.nn.softplus(-lam) * r
    a = jnp.exp(log_a)
    b = jnp.sqrt(-_expm1(2.0 * log_a)) * (i * xc)
    return a, b


def _seg_rows(ref, k, j, n):
    rows = pl.ds(pl.multiple_of(j * NSEG, NSEG), NSEG)
    if k is None:
        return ref[rows, :]
    return ref[k, rows, :]


def _seg_store(ref, k, j, n, val):
    rows = pl.ds(pl.multiple_of(j * NSEG, NSEG), NSEG)
    if k is None:
        ref[rows, :] = val
    else:
        ref[k, rows, :] = val


def _seg_carries(er, ei, pr, pi, reverse):
    rows = lax.broadcasted_iota(jnp.int32, er.shape, 0)
    cr = jnp.zeros_like(er)
    ci = None if ei is None else jnp.zeros_like(er)
    order = range(NSEG - 2, -1, -1) if reverse else range(1, NSEG)
    shift = NSEG - 1 if reverse else 1
    for s in order:
        if ei is None:
            tr = er + pr * cr
            cr = jnp.where(rows == s, pltpu.roll(tr, shift, 0), cr)
        else:
            tr = er + pr * cr - pi * ci
            ti = ei + pr * ci + pi * cr
            cr = jnp.where(rows == s, pltpu.roll(tr, shift, 0), cr)
            ci = jnp.where(rows == s, pltpu.roll(ti, shift, 0), ci)
    return cr, ci


S5_K = 2


def s5_scan_fwd(bur, bui, ar, ai, L):
    n = L // NSEG
    nb = S5_N // LANE
    K = S5_K

    def body(br_ref, bi_ref, ar_ref, ai_ref, xr_ref, xi_ref):
        zero = jnp.zeros((NSEG, LANE), F32)
        A = [(jnp.broadcast_to(ar_ref[k], (NSEG, LANE)), jnp.broadcast_to(ai_ref[k], (NSEG, LANE))) for k in range(K)]

        def p1(j, st):
            new = []
            for k in range(K):
                sr, si, pr, pi = st[k]
                a_r, a_i = A[k]
                nr = a_r * sr - a_i * si + _seg_rows(br_ref, k, j, n)
                ni = a_r * si + a_i * sr + _seg_rows(bi_ref, k, j, n)
                _seg_store(xr_ref, k, j, n, nr)
                _seg_store(xi_ref, k, j, n, ni)
                new.append((nr, ni, a_r * pr - a_i * pi, a_r * pi + a_i * pr))
            return tuple(new)

        st = lax.fori_loop(0, n, p1, tuple((zero, zero, zero + 1.0, zero) for _ in range(K)))
        C = [_seg_carries(st[k][0], st[k][1], st[k][2], st[k][3], False) for k in range(K)]

        def p2(j, st):
            new = []
            for k in range(K):
                pr, pi = st[k]
                a_r, a_i = A[k]
                pr, pi = a_r * pr - a_i * pi, a_r * pi + a_i * pr
                cr, ci = C[k]
                _seg_store(xr_ref, k, j, n, _seg_rows(xr_ref, k, j, n) + pr * cr - pi * ci)
                _seg_store(xi_ref, k, j, n, _seg_rows(xi_ref, k, j, n) + pr * ci + pi * cr)
                new.append((pr, pi))
            return tuple(new)

        lax.fori_loop(0, n, p2, tuple((zero + 1.0, zero) for _ in range(K)))

    blk = pl.BlockSpec((K, L, LANE), lambda g: (g, 0, 0))
    ablk = pl.BlockSpec((K, 1, LANE), lambda g: (g, 0, 0))
    return pl.pallas_call(
        body, grid=(nb // K,), name="s5_scan_fwd",
        in_specs=[blk, blk, ablk, ablk], out_specs=[blk, blk],
        out_shape=[jax.ShapeDtypeStruct((nb, L, LANE), F32)] * 2,
        compiler_params=pltpu.CompilerParams(vmem_limit_bytes=VMEM_LIMIT),
    )(bur, bui, ar, ai)


def s5_scan_bwd(dxr, dxi, xr, xi, ar, ai, L):
    n = L // NSEG
    nb = S5_N // LANE
    K = S5_K

    def body(dr_ref, di_ref, xr_ref, xi_ref, ar_ref, ai_ref, gr_ref, gi_ref, dar_ref, dai_ref):
        zero = jnp.zeros((NSEG, LANE), F32)
        rows = lax.broadcasted_iota(jnp.int32, (NSEG, LANE), 0)
        A = [(jnp.broadcast_to(ar_ref[k], (NSEG, LANE)), -jnp.broadcast_to(ai_ref[k], (NSEG, LANE))) for k in range(K)]

        def p1(jj, st):
            j = n - 1 - jj
            new = []
            for k in range(K):
                sr, si, pr, pi = st[k]
                a_r, a_i = A[k]
                nr = a_r * sr - a_i * si + _seg_rows(dr_ref, k, j, n)
                ni = a_r * si + a_i * sr + _seg_rows(di_ref, k, j, n)
                _seg_store(gr_ref, k, j, n, nr)
                _seg_store(gi_ref, k, j, n, ni)
                new.append((nr, ni, a_r * pr - a_i * pi, a_r * pi + a_i * pr))
            return tuple(new)

        st = lax.fori_loop(0, n, p1, tuple((zero, zero, zero + 1.0, zero) for _ in range(K)))
        C = [_seg_carries(st[k][0], st[k][1], st[k][2], st[k][3], True) for k in range(K)]
        xb = [(jnp.where(rows == 0, 0.0, pltpu.roll(_seg_rows(xr_ref, k, n - 1, n), 1, 0)),
               jnp.where(rows == 0, 0.0, pltpu.roll(_seg_rows(xi_ref, k, n - 1, n), 1, 0))) for k in range(K)]

        def p2(jj, st):
            j = n - 1 - jj
            jp = jnp.maximum(j - 1, 0)
            new = []
            for k in range(K):
                pr, pi, acr, aci = st[k]
                a_r, a_i = A[k]
                pr, pi = a_r * pr - a_i * pi, a_r * pi + a_i * pr
                cr, ci = C[k]
                g_r = _seg_rows(gr_ref, k, j, n) + pr * cr - pi * ci
                g_i = _seg_rows(gi_ref, k, j, n) + pr * ci + pi * cr
                _seg_store(gr_ref, k, j, n, g_r)
                _seg_store(gi_ref, k, j, n, g_i)
                xpr = jnp.where(j == 0, xb[k][0], _seg_rows(xr_ref, k, jp, n))
                xpi = jnp.where(j == 0, xb[k][1], _seg_rows(xi_ref, k, jp, n))
                new.append((pr, pi, acr + g_r * xpr + g_i * xpi, aci + g_i * xpr - g_r * xpi))
            return tuple(new)

        st = lax.fori_loop(0, n, p2, tuple((zero + 1.0, zero, zero, zero) for _ in range(K)))
        for k in range(K):
            dar_ref[k] = jnp.sum(st[k][2], axis=0, keepdims=True)
            dai_ref[k] = jnp.sum(st[k][3], axis=0, keepdims=True)

    blk = pl.BlockSpec((K, L, LANE), lambda g: (g, 0, 0))
    ablk = pl.BlockSpec((K, 1, LANE), lambda g: (g, 0, 0))
    return pl.pallas_call(
        body, grid=(nb // K,), name="s5_scan_bwd",
        in_specs=[blk, blk, blk, blk, ablk, ablk], out_specs=[blk, blk, ablk, ablk],
        out_shape=[jax.ShapeDtypeStruct((nb, L, LANE), F32)] * 2 + [jax.ShapeDtypeStruct((nb, 1, LANE), F32)] * 2,
        compiler_params=pltpu.CompilerParams(vmem_limit_bytes=VMEM_LIMIT),
    )(dxr, dxi, xr, xi, ar, ai)


def rg_scan_fwd(a, b, L):
    n = L // NSEG

    def body(a_ref, b_ref, h_ref):
        zero = jnp.zeros((NSEG, LANE), F32)

        def p1(j, st):
            h, p = st
            aj = _seg_rows(a_ref, None, j, n)
            h = aj * h + _seg_rows(b_ref, None, j, n)
            _seg_store(h_ref, None, j, n, h)
            return h, aj * p

        e, pe = lax.fori_loop(0, n, p1, (zero, zero + 1.0))
        c, _ = _seg_carries(e, None, pe, None, False)

        def p2(j, p):
            p = _seg_rows(a_ref, None, j, n) * p
            _seg_store(h_ref, None, j, n, _seg_rows(h_ref, None, j, n) + p * c)
            return p

        lax.fori_loop(0, n, p2, zero + 1.0)

    blk = pl.BlockSpec((L, LANE), lambda g: (0, g))
    return pl.pallas_call(
        body, grid=(BW // LANE,), name="rg_scan_fwd", in_specs=[blk, blk], out_specs=blk,
        out_shape=jax.ShapeDtypeStruct((L, BW), F32),
        compiler_params=pltpu.CompilerParams(vmem_limit_bytes=VMEM_LIMIT),
    )(a, b)


def rg_scan_bwd(a, h, dh, L):
    n = L // NSEG

    def body(a_ref, h_ref, dh_ref, da_ref, db_ref):
        zero = jnp.zeros((NSEG, LANE), F32)
        rows = lax.broadcasted_iota(jnp.int32, (NSEG, LANE), 0)
        a_edge = jnp.where(rows == NSEG - 1, 0.0, pltpu.roll(_seg_rows(a_ref, None, 0, n), NSEG - 1, 0))
        h_edge = jnp.where(rows == 0, 0.0, pltpu.roll(_seg_rows(h_ref, None, n - 1, n), 1, 0))

        def mult(j):
            return jnp.where(j == n - 1, a_edge, _seg_rows(a_ref, None, jnp.minimum(j + 1, n - 1), n))

        def p1(jj, st):
            j = n - 1 - jj
            g, p = st
            m = mult(j)
            g = m * g + _seg_rows(dh_ref, None, j, n)
            _seg_store(db_ref, None, j, n, g)
            return g, m * p

        e, pe = lax.fori_loop(0, n, p1, (zero, zero + 1.0))
        c, _ = _seg_carries(e, None, pe, None, True)

        def p2(jj, p):
            j = n - 1 - jj
            p = mult(j) * p
            g = _seg_rows(db_ref, None, j, n) + p * c
            _seg_store(db_ref, None, j, n, g)
            hp = jnp.where(j == 0, h_edge, _seg_rows(h_ref, None, jnp.maximum(j - 1, 0), n))
            _seg_store(da_ref, None, j, n, g * hp)
            return p

        lax.fori_loop(0, n, p2, zero + 1.0)

    blk = pl.BlockSpec((L, LANE), lambda g: (0, g))
    return pl.pallas_call(
        body, grid=(BW // LANE,), name="rg_scan_bwd", in_specs=[blk, blk, blk], out_specs=[blk, blk],
        out_shape=[jax.ShapeDtypeStruct((L, BW), F32)] * 2,
        compiler_params=pltpu.CompilerParams(vmem_limit_bytes=VMEM_LIMIT),
    )(a, h, dh)


def _hg_consts(C):
    t = lax.broadcasted_iota(jnp.int32, (C, C), 0)
    s = lax.broadcasted_iota(jnp.int32, (C, C), 1)
    tril = (s <= t).astype(F32)
    diag = (s == t).astype(F32)
    levels = []
    k = 1
    while (1 << k) <= C:
        m = 1 << (k - 1)
        same = (t >> k) == (s >> k)
        t_right = ((t >> (k - 1)) & 1) == 1
        s_left = ((s >> (k - 1)) & 1) == 0
        mask = jnp.logical_and(same, jnp.logical_and(t_right, s_left)).astype(F32)
        bnd = ((t >> k) << k) + (m - 1)
        levels.append((mask, (s <= bnd).astype(F32)))
        k += 1
    return tril, diag, levels


def hg_chunk(st, q, z, v, lb):
    C = q.shape[0]
    tril, diag, levels = _hg_consts(C)
    sig = jax.nn.sigmoid(z)
    lf = jnp.log(lb + (1.0 - lb) * sig)
    k = (1.0 - lb) * jax.nn.sigmoid(-z)
    qh = jax.nn.silu(q)
    b = mm_exact(tril, lf)
    blast = jnp.sum(lf, axis=0, keepdims=True)
    qe = qh * jnp.exp(b)
    kd = k * jnp.exp(blast - b)
    scaled = []
    for _, sel in levels:
        ref = mm_exact(sel, lf)
        scaled.append((qh * jnp.exp(jnp.minimum(b - ref, 0.0)), k * jnp.exp(jnp.minimum(ref - b, 0.0))))
    outs, news = [], []
    for h in range(HG_HEADS):
        sl = slice(h * HG_D, (h + 1) * HG_D)
        st_h = st[h * HG_D:(h + 1) * HG_D, :]
        sc = diag * mma_nt(qh[:, sl], k[:, sl])
        for (mask, _), (qt, kt) in zip(levels, scaled):
            sc = sc + mask * mma_nt(qt[:, sl], kt[:, sl])
        outs.append(mma_nt(qe[:, sl], st_h) + mma_nn(sc, v[:, sl]))
        news.append(st_h * jnp.exp(blast[:, sl]) + mma_tn(v[:, sl], kd[:, sl]))
    return jnp.concatenate(news, axis=0), jnp.concatenate(outs, axis=1)


def hg_fwd(qzv, lb, L):
    C = HG_CHUNK
    nc = L // C

    def body(q_ref, z_ref, v_ref, lb_ref, o_ref, sst_ref, st_ref):
        @pl.when(pl.program_id(0) == 0)
        def _():
            st_ref[...] = jnp.zeros_like(st_ref)

        st = st_ref[...]
        sst_ref[...] = st
        new, o = hg_chunk(st, q_ref[...], z_ref[...], v_ref[...], lb_ref[...])
        st_ref[...] = new
        o_ref[...] = o

    col = lambda cb: pl.BlockSpec((C, BW), functools.partial(lambda c, cb: (c, cb), cb=cb))
    return pl.pallas_call(
        body, grid=(nc,), name="hg_fwd",
        in_specs=[col(0), col(1), col(2), pl.BlockSpec((1, BW), lambda c: (0, 0))],
        out_specs=[pl.BlockSpec((C, BW), lambda c: (c, 0)), pl.BlockSpec((None, BW, HG_D), lambda c: (c, 0, 0))],
        out_shape=[jax.ShapeDtypeStruct((L, BW), F32), jax.ShapeDtypeStruct((nc, BW, HG_D), F32)],
        scratch_shapes=[pltpu.VMEM((BW, HG_D), F32)],
        compiler_params=pltpu.CompilerParams(vmem_limit_bytes=VMEM_LIMIT, dimension_semantics=("arbitrary",)),
    )(qzv, qzv, qzv, lb)


def hg_bwd(qzv, lb, sst, do, L):
    C = HG_CHUNK
    nc = L // C

    def body(q_ref, z_ref, v_ref, lb_ref, sst_ref, do_ref, dq_ref, dz_ref, dv_ref, dlb_ref, dst_ref):
        @pl.when(pl.program_id(0) == 0)
        def _():
            dst_ref[...] = jnp.zeros_like(dst_ref)
            dlb_ref[...] = jnp.zeros_like(dlb_ref)

        _, vjp = jax.vjp(hg_chunk, sst_ref[...], q_ref[...], z_ref[...], v_ref[...], lb_ref[...])
        dst, dq, dz, dv, dlb = vjp((dst_ref[...], do_ref[...]))
        dst_ref[...] = dst
        dq_ref[...] = dq
        dz_ref[...] = dz
        dv_ref[...] = dv
        dlb_ref[...] += dlb

    col = lambda cb: pl.BlockSpec((C, BW), functools.partial(lambda c, cb: (nc - 1 - c, cb), cb=cb))
    rev = pl.BlockSpec((C, BW), lambda c: (nc - 1 - c, 0))
    return pl.pallas_call(
        body, grid=(nc,), name="hg_bwd",
        in_specs=[col(0), col(1), col(2), pl.BlockSpec((1, BW), lambda c: (0, 0)),
                  pl.BlockSpec((None, BW, HG_D), lambda c: (nc - 1 - c, 0, 0)), rev],
        out_specs=[rev, rev, rev, pl.BlockSpec((1, BW), lambda c: (0, 0))],
        out_shape=[jax.ShapeDtypeStruct((L, BW), F32)] * 3 + [jax.ShapeDtypeStruct((1, BW), F32)],
        scratch_shapes=[pltpu.VMEM((BW, HG_D), F32)],
        compiler_params=pltpu.CompilerParams(vmem_limit_bytes=VMEM_LIMIT, dimension_semantics=("arbitrary",)),
    )(qzv, qzv, qzv, lb, sst, do)


def _shift_down(x, d, rows, L):
    if d == 0:
        return x
    wrapped = jnp.where((rows & (NSEG - 1)) == 0, 0.0, pltpu.roll(x, NSEG * d + 1, 0))
    return jnp.where(rows < NSEG * d, wrapped, pltpu.roll(x, NSEG * d, 0))


def _shift_up(x, d, rows, L):
    if d == 0:
        return x
    wrapped = jnp.where((rows & (NSEG - 1)) == NSEG - 1, 0.0, pltpu.roll(x, L - (NSEG * d + 1), 0))
    return jnp.where(rows >= L - NSEG * d, wrapped, pltpu.roll(x, L - NSEG * d, 0))


def conv_fwd(proj, w, b, L):
    def body(x_ref, w_ref, b_ref, o_ref):
        x = x_ref[...]
        rows = lax.broadcasted_iota(jnp.int32, x.shape, 0)
        acc = jnp.broadcast_to(b_ref[...], x.shape)
        for k in range(CONV_W):
            acc = acc + w_ref[pl.ds(k, 1), :] * _shift_down(x, CONV_W - 1 - k, rows, L)
        o_ref[...] = acc

    nl = BW // LANE
    return pl.pallas_call(
        body, grid=(nl,), name="conv_fwd",
        in_specs=[pl.BlockSpec((L, LANE), lambda g: (0, 5 * nl + g)), pl.BlockSpec((CONV_W, LANE), lambda g: (0, g)),
                  pl.BlockSpec((1, LANE), lambda g: (0, g))],
        out_specs=pl.BlockSpec((L, LANE), lambda g: (0, g)),
        out_shape=jax.ShapeDtypeStruct((L, BW), F32),
        compiler_params=pltpu.CompilerParams(vmem_limit_bytes=VMEM_LIMIT),
    )(proj, w, b)


def conv_bwd(proj, w, dxc, L):
    def body(x_ref, w_ref, d_ref, dx_ref, dw_ref, db_ref):
        x, d = x_ref[...], d_ref[...]
        rows = lax.broadcasted_iota(jnp.int32, x.shape, 0)
        acc = jnp.zeros_like(x)
        for k in range(CONV_W):
            acc = acc + w_ref[pl.ds(k, 1), :] * _shift_up(d, CONV_W - 1 - k, rows, L)
            dw_ref[pl.ds(k, 1), :] = jnp.sum(d * _shift_down(x, CONV_W - 1 - k, rows, L), axis=0, keepdims=True)
        dx_ref[...] = acc
        db_ref[...] = jnp.sum(d, axis=0, keepdims=True)

    nl = BW // LANE
    blk = pl.BlockSpec((L, LANE), lambda g: (0, g))
    return pl.pallas_call(
        body, grid=(nl,), name="conv_bwd",
        in_specs=[pl.BlockSpec((L, LANE), lambda g: (0, 5 * nl + g)), pl.BlockSpec((CONV_W, LANE), lambda g: (0, g)), blk],
        out_specs=[blk, pl.BlockSpec((CONV_W, LANE), lambda g: (0, g)), pl.BlockSpec((1, LANE), lambda g: (0, g))],
        out_shape=[jax.ShapeDtypeStruct((L, BW), F32), jax.ShapeDtypeStruct((CONV_W, BW), F32),
                   jax.ShapeDtypeStruct((1, BW), F32)],
        compiler_params=pltpu.CompilerParams(vmem_limit_bytes=VMEM_LIMIT),
    )(proj, w, dxc)


def loss_fwd_bwd(x, fw, target, L, tm):
    def fn(x, fw, t):
        err = jnp.square(_rms(x, fw) - t)
        return jnp.sum(0.5 * jnp.mean(err, axis=-1, keepdims=True), axis=0, keepdims=True)

    def body(x_ref, fw_ref, t_ref, l_ref, dx_ref, dfw_ref):
        i = pl.program_id(0)
        t = t_ref[...]
        val, vjp = jax.vjp(lambda x, fw: fn(x, fw, t), x_ref[...], fw_ref[...])
        dx, dfw = vjp(jnp.ones((1, 1), F32))
        dx_ref[...] = dx

        @pl.when(i == 0)
        def _():
            l_ref[...] = jnp.zeros_like(l_ref)
            dfw_ref[...] = jnp.zeros_like(dfw_ref)

        l_ref[...] += jnp.broadcast_to(val, l_ref.shape)
        dfw_ref[...] += dfw

    row = pl.BlockSpec((tm, D_MODEL), lambda i: (i, 0))
    vec = pl.BlockSpec((1, D_MODEL), lambda i: (0, 0))
    return pl.pallas_call(
        body, grid=(L // tm,), name="loss_fwd_bwd", in_specs=[row, vec, row],
        out_specs=[pl.BlockSpec((1, LANE), lambda i: (0, 0)), row, vec],
        out_shape=[jax.ShapeDtypeStruct((1, LANE), F32), jax.ShapeDtypeStruct((L, D_MODEL), F32),
                   jax.ShapeDtypeStruct((1, D_MODEL), F32)],
        compiler_params=pltpu.CompilerParams(vmem_limit_bytes=VMEM_LIMIT, dimension_semantics=("arbitrary",)),
    )(x, fw, target)


def adamw(w, g, m, v):
    rows, cols = w.shape
    tr = _row_tile(rows, cols, budget=1024 * 1024)
    c1 = 1.0 - ADAM_B1 ** ADAM_STEP
    c2 = 1.0 - ADAM_B2 ** ADAM_STEP

    def body(w_ref, g_ref, m_ref, v_ref, d_ref, nm_ref, nv_ref):
        g = g_ref[...]
        nm = ADAM_B1 * m_ref[...] + (1.0 - ADAM_B1) * g
        nv = ADAM_B2 * v_ref[...] + (1.0 - ADAM_B2) * jnp.square(g)
        d_ref[...] = -ADAM_LR * ((nm / c1) / (jnp.sqrt(nv / c2) + ADAM_EPS) + ADAM_WD * w_ref[...])
        nm_ref[...] = nm
        nv_ref[...] = nv

    blk = pl.BlockSpec((tr, cols), lambda i: (i, 0))
    return pl.pallas_call(
        body, grid=(rows // tr,), name="adamw", in_specs=[blk] * 4, out_specs=[blk] * 3,
        out_shape=[jax.ShapeDtypeStruct((rows, cols), F32)] * 3,
    )(w, g, m, v)


def s5_prep(lam_re, lam_im, log_dt, b_re, b_im, c_re, c_im):
    lr = jnp.minimum(lam_re, -1e-4)
    li = lam_im
    dt = jnp.exp(log_dt)[:, None]
    mag = jnp.exp(lr * dt)
    ar = mag * jnp.cos(li * dt)
    ai = mag * jnp.sin(li * dt)
    den = lr * lr + li * li
    fr = ((ar - 1.0) * lr + ai * li) / den
    fi = (ai * lr - (ar - 1.0) * li) / den
    bbr = fr[..., None] * b_re - fi[..., None] * b_im
    bbi = fr[..., None] * b_im + fi[..., None] * b_re
    emb_b = lambda bb: _block_diag(bb.transpose(0, 2, 1).reshape(BW, S5_STATE), S5_GROUPS)
    emb_c = lambda cc: _block_diag(cc.transpose(0, 2, 1).reshape(S5_N, S5_GROUP), S5_GROUPS)
    bmat = jnp.concatenate([emb_b(bbr), emb_b(bbi)], axis=1)
    cmat = jnp.concatenate([emb_c(c_re), -emb_c(c_im)], axis=0)
    nb = S5_N // LANE
    return ar.reshape(nb, 1, LANE), ai.reshape(nb, 1, LANE), bmat, cmat


def _block_diag(stacked, groups):
    rows, c = stacked.shape
    r = rows // groups
    row_g = jnp.arange(rows)[:, None] // r
    col_g = jnp.arange(groups * c)[None, :] // c
    return jnp.where(row_g == col_g, jnp.tile(stacked, (1, groups)), 0.0)


def rg_prep(w):
    return _block_diag(w.reshape(BW, RG_BLOCK), RG_BLOCKS)


def hg_prep(logits):
    p = jax.nn.softmax(logits, axis=0)
    return jnp.cumsum(p, axis=0) - p[0]


def _head_mean_matrix():
    r = jnp.arange(BW) // HG_D
    return (r[:, None] == r[None, :]).astype(F32) / HG_D


def _to_segment_order(a):
    L = a.shape[0]
    return a.reshape(NSEG, L // NSEG, -1).transpose(1, 0, 2).reshape(a.shape)


def _to_time_order(a):
    L = a.shape[0]
    return a.reshape(L // NSEG, NSEG, -1).transpose(1, 0, 2).reshape(a.shape)


def _const(*idx):
    return lambda s, i: idx


def _rows(cb=0):
    return lambda s, i: (i, cb)


def _sum_parts(name, first, parts, shape):
    return add_n(name, [(first, ())] + [(parts, (s,)) for s in range(NSH)], shape)


def ffn_fwd(name, x, W, l, j, k, L, tm):
    D = D_MODEL
    def fn(x, nw, wg, wu, wd, s):
        h = _rms(x, nw)
        y = x
        for sh in range(NSH):
            y = y + 0.5 * mm(jax.nn.silu(mm(h, wg[sh])) * mm(h, wu[sh]), wd[sh])
        return (y,)

    ins = [(x, (tm, D), _rows()),
           (W["nw"], (None, None, 1, D), _const(l, k, 0, 0)),
           (W["wg"], (None, NSH, None, D, D_FF // NSH), _const(l, 0, j, 0, 0)),
           (W["wu"], (None, NSH, None, D, D_FF // NSH), _const(l, 0, j, 0, 0)),
           (W["wd"], (None, NSH, None, D_FF // NSH, D), _const(l, 0, j, 0, 0))]
    return tile_fwd(fn, name, L // tm, 1, ins, [((L, D), F32, (tm, D), _rows(), False)])[0]


def ffn_bwd(name, x, dy, W, bufs, l, j, k, L, tm):
    D, F = D_MODEL, D_FF // NSH
    stk = lambda key, r, c: ((2 * DEPTH, NSH, r, c), (None, None, r, c), lambda s, i: (2 * l + j, s, 0, 0), "acc_i",
                             bufs.get(key))
    ins = [(x, (tm, D), _rows(), "r"),
           (W["nw"], (None, None, 1, D), _const(l, k, 0, 0), "p"),
           (W["wg"], (None, None, None, D, F), lambda s, i: (l, s, j, 0, 0), "w"),
           (W["wu"], (None, None, None, D, F), lambda s, i: (l, s, j, 0, 0), "w"),
           (W["wd"], (None, None, None, F, D), lambda s, i: (l, s, j, 0, 0), "w")]
    gouts = [((NSH, L, D), (None, tm, D), lambda s, i: (s, i, 0), "write"),
             ((1, D), (1, D), _const(0, 0), "acc_all"),
             stk("ffn_gate", D, F), stk("ffn_up", D, F), stk("ffn_down", F, D)]
    part, dnw, bufs["ffn_gate"], bufs["ffn_up"], bufs["ffn_down"] = tile_bwd(
        ffn_core, name, L // tm, NSH, ins, [(dy, (tm, D), _rows())], gouts)
    return _sum_parts(name + "_dx", dy, part, (L, D)), dnw


def layer_fwd(l, x0, W, P, L, tm):
    D = D_MODEL
    n_i = L // tm
    x1 = ffn_fwd(f"ffn_fwd_{l}0", x0, W, l, 0, 0, L, tm)
    proj = tile_fwd(
        lambda x, nw, win, s: pre_core(x, nw, win), f"pre_fwd_{l}", n_i, NSH,
        [(x1, (tm, D), _rows()), (W["nw"], (None, None, 1, D), _const(l, 1, 0, 0)),
         (W["win"], (None, None, D, IN_TOTAL // NSH), lambda s, i: (l, s, 0, 0))],
        [((L, IN_TOTAL), F32, (tm, IN_TOTAL // NSH), lambda s, i: (i, s), False)], s_outer=True)[0]
    nb = S5_N // LANE
    blk3 = lambda s, i: (0, i, 0)
    bur, bui = tile_fwd(
        lambda u, bmat, s: s5_pre_core(u, bmat), f"s5pre_fwd_{l}", n_i, 1,
        [(proj, (tm, BW), _rows(0)), (P["bmat"], (None, BW, 2 * S5_N), _const(l, 0, 0))],
        [((nb, L, LANE), F32, (nb, tm, LANE), blk3, False)] * 2)
    xr, xi = s5_scan_fwd(bur, bui, P["ar"][l], P["ai"][l], L)
    qzv = _to_time_order(proj[:, BW:4 * BW])
    o_t, sst = hg_fwd(qzv, P["lb"][l], L)
    o = _to_segment_order(o_t)
    xc = conv_fwd(proj, W["convw"][l], P["convb"][l], L)
    vec = (None, 1, BW)
    a, b = tile_fwd(
        lambda xc, wa, ba, wx, bx, lam, s: gates_core(xc, wa, ba, wx, bx, lam), f"gates_fwd_{l}", n_i, 1,
        [(xc, (tm, BW), _rows()), (P["wa"], (None, BW, BW), _const(l, 0, 0)), (P["ba"], vec, _const(l, 0, 0)),
         (P["wx"], (None, BW, BW), _const(l, 0, 0)), (P["bx"], vec, _const(l, 0, 0)), (P["lam"], vec, _const(l, 0, 0))],
        [((L, BW), F32, (tm, BW), _rows(), False)] * 2)
    hs = rg_scan_fwd(a, b, L)
    tmm = tm
    ya, yb, yc = tile_fwd(
        lambda *a: mid_core(*a[:-1]), f"mid_fwd_{l}", L // tmm, 1,
        [(xr, (nb, tmm, LANE), blk3), (xi, (nb, tmm, LANE), blk3), (proj, (tmm, BW), _rows(0)), (o, (tmm, BW), _rows()),
         (proj, (tmm, BW), _rows(4)), (hs, (tmm, BW), _rows()), (proj, (tmm, BW), _rows(6)),
         (P["hmat"], (BW, BW), _const(0, 0)), (P["cmat"], (None, 2 * S5_N, BW), _const(l, 0, 0)), (P["d"], vec, _const(l, 0, 0)),
         (W["gluw"], (None, BW, BW), _const(l, 0, 0)), (P["glub"], vec, _const(l, 0, 0)), (P["hgw"], vec, _const(l, 0, 0))],
        [((L, BW), F32, (tmm, BW), _rows(), False)] * 3)
    x2 = tile_fwd(
        lambda x, *rest: (x + merge_core(*rest[:-1])[0],), f"merge_fwd_{l}", n_i, 1,
        [(x1, (tm, D), _rows()), (ya, (tm, BW), _rows()), (yb, (tm, BW), _rows()), (yc, (tm, BW), _rows())]
        + [(proj, (tm, BW), _rows(7 + k)) for k in range(6)]
        + [(W["pfull"], (None, 3, BW, D), _const(l, 0, 0, 0)), (W["woutfull"], (None, D, D), _const(l, 0, 0))],
        [((L, D), F32, (tm, D), _rows(), False)])[0]
    x3 = ffn_fwd(f"ffn_fwd_{l}1", x2, W, l, 1, 2, L, tm)
    saved = dict(x0=x0, x1=x1, x2=x2, proj=proj, xr=xr, xi=xi, o=o, sst=sst, xc=xc, a=a, hs=hs, ya=ya, yb=yb, yc=yc,
                 qzv=qzv)
    return x3, saved


def layer_bwd(l, dx3, sv, W, P, bufs, L, tm):
    D = D_MODEL
    n_i = L // tm
    nb = S5_N // LANE
    dq = D // NSH
    vec = (None, 1, BW)
    vout = ((1, BW), (1, BW), _const(0, 0), "acc_all")
    blk3 = lambda s, i: (0, i, 0)
    small = {}
    proj = sv["proj"]

    dx2, dnw2 = ffn_bwd(f"ffn_bwd_{l}1", sv["x2"], dx3, W, bufs, l, 1, 2, L, tm)

    rw256 = ((L, BW), (tm, BW), _rows(), "write")
    res = tile_bwd(
        merge_core, f"merge_bwd_{l}", n_i, 1,
        [(sv["ya"], (tm, BW), _rows(), "r"), (sv["yb"], (tm, BW), _rows(), "r"), (sv["yc"], (tm, BW), _rows(), "r")]
        + [(proj, (tm, BW), _rows(7 + k), "r") for k in range(6)]
        + [(W["pfull"], (None, 3, BW, D), _const(l, 0, 0, 0), "w"), (W["woutfull"], (None, D, D), _const(l, 0, 0), "w")],
        [(dx2, (tm, D), _rows())],
        [rw256] * 9
        + [((DEPTH, 3, BW, D), (None, 3, BW, D), _const(l, 0, 0, 0), "acc_all", bufs.get("branch_proj")),
           ((DEPTH, D, D), (None, D, D), _const(l, 0, 0), "acc_all", bufs.get("w_out"))])
    dya, dyb, dyc = res[:3]
    dgm = res[3:9]
    bufs["branch_proj"], bufs["w_out"] = res[9:]

    tmm = min(tm, 128)
    rw = ((L, BW), (tmm, BW), _rows(), "write")
    xw = ((nb, L, LANE), (nb, tmm, LANE), blk3, "write")
    res = tile_bwd(
        mid_core, f"mid_bwd_{l}", L // tmm, 1,
        [(sv["xr"], (nb, tmm, LANE), blk3, "r"), (sv["xi"], (nb, tmm, LANE), blk3, "r"), (proj, (tmm, BW), _rows(0), "r"),
         (sv["o"], (tmm, BW), _rows(), "r"), (proj, (tmm, BW), _rows(4), "r"), (sv["hs"], (tmm, BW), _rows(), "r"),
         (proj, (tmm, BW), _rows(6), "r"), (P["hmat"], (BW, BW), _const(0, 0), "c"),
         (P["cmat"], (None, 2 * S5_N, BW), _const(l, 0, 0), "w"), (P["d"], vec, _const(l, 0, 0), "p"),
         (W["gluw"], (None, BW, BW), _const(l, 0, 0), "w"), (P["glub"], vec, _const(l, 0, 0), "p"),
         (P["hgw"], vec, _const(l, 0, 0), "p")],
        [(dya, (tmm, BW), _rows()), (dyb, (tmm, BW), _rows()), (dyc, (tmm, BW), _rows())],
        [xw, xw, rw, rw, rw, rw, rw,
         ((DEPTH, 2 * S5_N, BW), (None, 2 * S5_N, BW), _const(l, 0, 0), "acc_all", bufs.get("cmat")), vout,
         ((DEPTH, BW, BW), (None, BW, BW), _const(l, 0, 0), "acc_all", bufs.get("s5_glu_w")), vout, vout])
    dxr, dxi, du_skip, do, dg_b, dhs, dgate_c, bufs["cmat"], dd, bufs["s5_glu_w"], dglub, dhgw = res
    small["s5_d"], small["s5_glu_b"], small["hg_norm_w"] = dd[0], dglub[0], dhgw[0]

    da, db = rg_scan_bwd(sv["a"], sv["hs"], dhs, L)
    wmat = lambda key: ((DEPTH, BW, BW), (None, BW, BW), _const(l, 0, 0), "acc_all", bufs.get(key))
    res = tile_bwd(
        gates_core, f"gates_bwd_{l}", n_i, 1,
        [(sv["xc"], (tm, BW), _rows(), "r"), (P["wa"], (None, BW, BW), _const(l, 0, 0), "w"), (P["ba"], vec, _const(l, 0, 0), "p"),
         (P["wx"], (None, BW, BW), _const(l, 0, 0), "w"), (P["bx"], vec, _const(l, 0, 0), "p"), (P["lam"], vec, _const(l, 0, 0), "p")],
        [(da, (tm, BW), _rows()), (db, (tm, BW), _rows())],
        [((L, BW), (tm, BW), _rows(), "write"), wmat("wa"), vout, wmat("wx"), vout, vout])
    dxc, bufs["wa"], dba, bufs["wx"], dbx, dlam = res
    small["rg_ba"], small["rg_bx"], small["rg_lambda"] = dba[0], dbx[0], dlam[0]
    dx_c, dconvw, dconvb = conv_bwd(proj, W["convw"][l], dxc, L)
    small["rg_conv_w"], small["rg_conv_b"] = dconvw, dconvb[0]

    dq_b, dz_b, dv_b, dlb = hg_bwd(sv["qzv"], P["lb"][l], sv["sst"], _to_time_order(do), L)
    dqzv = _to_segment_order(jnp.concatenate([dq_b, dz_b, dv_b], axis=1))

    gr, gi, dar, dai = s5_scan_bwd(dxr, dxi, sv["xr"], sv["xi"], P["ar"][l], P["ai"][l], L)
    du_pre, bufs["bmat"] = tile_bwd(
        s5_pre_core, f"s5pre_bwd_{l}", n_i, 1,
        [(proj, (tm, BW), _rows(0), "r"), (P["bmat"], (None, BW, 2 * S5_N), _const(l, 0, 0), "w")],
        [(gr, (nb, tm, LANE), blk3), (gi, (nb, tm, LANE), blk3)],
        [((L, BW), (tm, BW), _rows(), "write"),
         ((DEPTH, BW, 2 * S5_N), (None, BW, 2 * S5_N), _const(l, 0, 0), "acc_all", bufs.get("bmat"))])
    du_a = add_n(f"du_a_{l}", [(du_skip, ()), (du_pre, ())], (L, BW))
    prep_ct = dict(dar=dar, dai=dai, dlb=dlb)

    dproj = jnp.concatenate([du_a, dqzv, dg_b, dx_c, dgate_c, *dgm], axis=1)
    part, dnw1, bufs["w_in"] = tile_bwd(
        pre_core, f"pre_bwd_{l}", n_i, NSH,
        [(sv["x1"], (tm, D), _rows(), "r"), (W["nw"], (None, None, 1, D), _const(l, 1, 0, 0), "p"),
         (W["win"], (None, None, D, IN_TOTAL // NSH), lambda s, i: (l, s, 0, 0), "w")],
        [(dproj, (tm, IN_TOTAL // NSH), lambda s, i: (i, s))],
        [((NSH, L, D), (None, tm, D), lambda s, i: (s, i, 0), "write"), ((1, D), (1, D), _const(0, 0), "acc_all"),
         ((DEPTH, NSH, D, IN_TOTAL // NSH), (None, None, D, IN_TOTAL // NSH), lambda s, i: (l, s, 0, 0), "acc_i",
          bufs.get("w_in"))])
    dx1 = _sum_parts(f"pre_bwd_{l}_dx", dx2, part, (L, D))

    dx0, dnw0 = ffn_bwd(f"ffn_bwd_{l}0", sv["x0"], dx1, W, bufs, l, 0, 0, L, tm)
    small["norm_w"] = jnp.concatenate([dnw0, dnw1, dnw2], axis=0)
    return dx0, small, prep_ct


SMALL_RAW = ("s5_lambda_re", "s5_lambda_im", "s5_log_dt", "s5_b_re", "s5_b_im", "s5_c_re", "s5_c_im", "s5_d", "s5_glu_b",
             "hg_lb_logits", "hg_norm_w", "rg_conv_b", "rg_wa", "rg_ba", "rg_wx", "rg_bx", "rg_lambda", "final_norm_w")
DEPTH = 2


def local_step(x, target, W, raw):
    L = x.shape[0]
    tm = min(256, L)
    col = lambda v: v.reshape(DEPTH, 1, BW)
    (ar, ai, bmat, cmat), s5_vjp = jax.vjp(jax.vmap(s5_prep), *[raw[k] for k in SMALL_RAW[:7]])
    (wa, wx), rg_vjp = jax.vjp(lambda a, b: (jax.vmap(rg_prep)(a), jax.vmap(rg_prep)(b)), raw["rg_wa"], raw["rg_wx"])
    lb, hg_vjp = jax.vjp(hg_prep, raw["hg_lb_logits"])
    P = dict(
        ar=[ar[l] for l in range(DEPTH)], ai=[ai[l] for l in range(DEPTH)],
        bmat=bmat.astype(MMT), cmat=cmat.astype(MMT), wa=wa.astype(MMT), wx=wx.astype(MMT),
        lb=[lb[l].reshape(1, BW) for l in range(DEPTH)], convb=[raw["rg_conv_b"][l].reshape(1, BW) for l in range(DEPTH)],
        ba=col(raw["rg_ba"]), bx=col(raw["rg_bx"]), lam=col(raw["rg_lambda"]), d=col(raw["s5_d"]),
        glub=col(raw["s5_glu_b"]), hgw=col(raw["hg_norm_w"]), hmat=_head_mean_matrix())

    saved = []
    h = _to_segment_order(x)
    for l in range(DEPTH):
        h, sv = layer_fwd(l, h, W, P, L, tm)
        saved.append(sv)
    loss, dh, dfw = loss_fwd_bwd(h, raw["final_norm_w"].reshape(1, D_MODEL), _to_segment_order(target), L, tm)

    big, per_layer, prep_cts = {}, [None] * DEPTH, [None] * DEPTH
    for l in reversed(range(DEPTH)):
        dh, sm, pc = layer_bwd(l, dh, saved[l], W, P, big, L, tm)
        per_layer[l], prep_cts[l] = sm, pc
    dh = _to_time_order(dh)

    small = {k: jnp.stack([per_layer[l][k] for l in range(DEPTH)]) for k in per_layer[0]}
    both = lambda k: jnp.stack([prep_cts[l][k] for l in range(DEPTH)])
    s5_g = s5_vjp((both("dar"), both("dai"), big.pop("bmat"), big.pop("cmat")))
    small.update(zip(SMALL_RAW[:7], s5_g))
    small["rg_wa"], small["rg_wx"] = rg_vjp((big.pop("wa"), big.pop("wx")))
    (small["hg_lb_logits"],) = hg_vjp(jnp.concatenate([prep_cts[l]["dlb"] for l in range(DEPTH)], axis=0))
    small["final_norm_w"] = dfw[0]
    return loss, dh, big, small


ANY = pl.BlockSpec(memory_space=pl.ANY)


def _place():
    x, y, c = lax.axis_index("x"), lax.axis_index("y"), lax.axis_index("c")
    chips = [(1 - x, y), (x, 1 - y), (1 - x, 1 - y)]
    return x, y, c, chips


def _remote(src, dst, send, recv, k, to):
    return pltpu.make_async_remote_copy(src_ref=src, dst_ref=dst, send_sem=send.at[k], recv_sem=recv.at[k],
                                        device_id=to, device_id_type=MESH)


def _comm_call(body, name, ins, out_shapes, n_sem, n_loc):
    return pl.pallas_call(
        body, name=name, in_specs=[ANY] * len(ins), out_specs=[ANY] * len(out_shapes), out_shape=out_shapes,
        scratch_shapes=[pltpu.SemaphoreType.DMA((n_sem,)), pltpu.SemaphoreType.DMA((n_sem,)),
                        pltpu.SemaphoreType.DMA((max(n_loc, 1),))],
    )(*ins)


def gather_shards(name, shards):
    n = len(shards)
    per = 8

    def body(*refs):
        ins, outs = refs[:n], refs[n:2 * n]
        send, recv, _ = refs[2 * n:]
        x, y, c, chips = _place()
        me = 2 * x + y
        sib = (x, y, 1 - c)
        sends = []
        for w in range(n):
            for j, (cx, cy) in enumerate(chips):
                cp = _remote(ins[w].at[c], outs[w].at[c, me], send, recv, per * w + j, (cx, cy, c))
                cp.start()
                sends.append(cp)
        for w in range(n):
            for l in range(2):
                cp = _remote(ins[w].at[l], outs[w].at[l, me], send, recv, per * w + 6 + l, sib)
                cp.start()
                sends.append(cp)
        for w in range(n):
            for j, (cx, cy) in enumerate(chips):
                theirs = outs[w].at[c, 2 * cx + cy]
                _remote(ins[w].at[c], theirs, send, recv, per * w + j, (cx, cy, c)).wait_recv()
                cp = _remote(theirs, theirs, send, recv, per * w + 3 + j, sib)
                cp.start()
                sends.append(cp)
        for w in range(n):
            for j, (cx, cy) in enumerate(chips):
                dst = outs[w].at[1 - c, 2 * cx + cy]
                _remote(dst, dst, send, recv, per * w + 3 + j, sib).wait_recv()
            for l in range(2):
                dst = outs[w].at[l, me]
                _remote(dst, dst, send, recv, per * w + 6 + l, sib).wait_recv()
        for cp in sends:
            cp.wait_send()

    shapes = [jax.ShapeDtypeStruct((2, NSH) + s.shape[1:], s.dtype) for s in shards]
    return _comm_call(body, name, shards, shapes, per * n, 0)


def exchange_halves(name, grads):
    n = len(grads)

    def body(*refs):
        ins, outs = refs[:n], refs[n:2 * n]
        send, recv, _ = refs[2 * n:]
        x, y, c, _chips = _place()
        cps = []
        for w in range(n):
            h = grads[w].shape[2] // 2
            cp = _remote(ins[w].at[:, :, pl.ds((1 - c) * h, h)], outs[w], send, recv, w, (x, y, 1 - c))
            cp.start()
            cps.append(cp)
        for cp in cps:
            cp.wait()

    shapes = [jax.ShapeDtypeStruct(g.shape[:2] + (g.shape[2] // 2, g.shape[3]), g.dtype) for g in grads]
    return _comm_call(body, name, grads, shapes, n, 0)


def scatter_to_chips(name, halves):
    n = len(halves)

    def body(*refs):
        ins, outs = refs[:n], refs[n:2 * n]
        send, recv, _ = refs[2 * n:]
        x, y, c, chips = _place()
        cps = []
        for w in range(n):
            for j, (cx, cy) in enumerate(chips):
                cp = _remote(ins[w].at[:, 2 * cx + cy], outs[w].at[j], send, recv, 3 * w + j, (cx, cy, c))
                cp.start()
                cps.append(cp)
        for cp in cps:
            cp.wait()

    shapes = [jax.ShapeDtypeStruct((3, h.shape[0]) + h.shape[2:], h.dtype) for h in halves]
    return _comm_call(body, name, halves, shapes, 3 * n, 0)


def share_halves(name, pieces):
    n = len(pieces)

    def body(*refs):
        ins, outs = refs[:n], refs[n:2 * n]
        send, recv, _ = refs[2 * n:]
        x, y, c, _chips = _place()
        cps = []
        for w in range(n):
            cp = _remote(ins[w], outs[w], send, recv, w, (x, y, 1 - c))
            cp.start()
            cps.append(cp)
        for cp in cps:
            cp.wait()

    return _comm_call(body, name, pieces, [jax.ShapeDtypeStruct(p.shape, p.dtype) for p in pieces], n, 0)


def add_own_half(name, g, ra, c, wire):
    nblk, h, cols = ra.shape
    tr = _row_tile(h, cols, mult=16)
    nt = h // tr

    def body(c_ref, g_ref, r_ref, o_ref):
        o_ref[...] = (g_ref[...] + r_ref[...]).astype(o_ref.dtype)

    blk = (None, tr, cols)
    return pl.pallas_call(
        body, name=name,
        grid_spec=pltpu.PrefetchScalarGridSpec(
            num_scalar_prefetch=1, grid=(nblk, nt),
            in_specs=[pl.BlockSpec(blk, lambda s, i, c_ref: (s, c_ref[0] * nt + i, 0)), pl.BlockSpec(blk, lambda s, i, c_ref: (s, i, 0))],
            out_specs=pl.BlockSpec(blk, lambda s, i, c_ref: (s, i, 0))),
        out_shape=jax.ShapeDtypeStruct(ra.shape, wire),
    )(c.reshape(1), g, ra)


def add_chips(name, hb, rb, me):
    npc, _, h, cols = hb.shape
    tr = _row_tile(h, cols, mult=16)

    def body(me_ref, h_ref, r0, r1, r2, o_ref):
        f = lambda r: r[...].astype(F32)
        o_ref[...] = ((f(h_ref) + f(r0)) + f(r1)) + f(r2)

    rspec = lambda j: pl.BlockSpec((None, None, tr, cols), functools.partial(lambda p, i, me_ref, j: (j, p, i, 0), j=j))
    return pl.pallas_call(
        body, name=name,
        grid_spec=pltpu.PrefetchScalarGridSpec(
            num_scalar_prefetch=1, grid=(npc, h // tr),
            in_specs=[pl.BlockSpec((None, None, tr, cols), lambda p, i, me_ref: (p, me_ref[0], i, 0)), rspec(0), rspec(1), rspec(2)],
            out_specs=pl.BlockSpec((None, tr, cols), lambda p, i, me_ref: (p, i, 0))),
        out_shape=jax.ShapeDtypeStruct((npc, h, cols), F32),
    )(me.reshape(1), hb, rb, rb, rb)


def adamw_halves(name, w, m, v, own, other, c):
    npc, rows, cols = w.shape
    h = rows // 2
    tr = _row_tile(h, cols, budget=1024 * 1024)
    nt = h // tr
    c1 = 1.0 - ADAM_B1 ** ADAM_STEP
    c2 = 1.0 - ADAM_B2 ** ADAM_STEP

    def body(c_ref, w_ref, m_ref, v_ref, own_ref, oth_ref, g_ref, d_ref, nm_ref, nv_ref):
        g = jnp.where(pl.program_id(1) == c_ref[0], own_ref[...], oth_ref[...])
        nm = ADAM_B1 * m_ref[...] + (1.0 - ADAM_B1) * g
        nv = ADAM_B2 * v_ref[...] + (1.0 - ADAM_B2) * jnp.square(g)
        g_ref[...] = g
        d_ref[...] = -ADAM_LR * ((nm / c1) / (jnp.sqrt(nv / c2) + ADAM_EPS) + ADAM_WD * w_ref[...])
        nm_ref[...] = nm
        nv_ref[...] = nv

    full = pl.BlockSpec((None, tr, cols), lambda p, hh, i, c_ref: (p, hh * nt + i, 0))
    half = pl.BlockSpec((None, tr, cols), lambda p, hh, i, c_ref: (p, i, 0))
    return pl.pallas_call(
        body, name=name,
        grid_spec=pltpu.PrefetchScalarGridSpec(
            num_scalar_prefetch=1, grid=(npc, 2, nt),
            in_specs=[full, full, full, half, half], out_specs=[full] * 4),
        out_shape=[jax.ShapeDtypeStruct(w.shape, F32)] * 4,
    )(c.reshape(1), w, m, v, own, other)


WEIGHTS = ("norm_w", "final_norm_w", "ffn_gate", "ffn_up", "ffn_down", "w_in", "branch_proj", "w_out", "s5_lambda_re",
           "s5_lambda_im", "s5_log_dt", "s5_b_re", "s5_b_im", "s5_c_re", "s5_c_im", "s5_d", "s5_glu_w", "s5_glu_b",
           "hg_lb_logits", "hg_norm_w", "rg_conv_w", "rg_conv_b", "rg_wa", "rg_ba", "rg_wx", "rg_bx", "rg_lambda")
BIG = ("ffn_gate", "ffn_up", "ffn_down", "w_in", "branch_proj", "w_out", "s5_glu_w")
SHARDED_SMALL = ("norm_w", "rg_conv_w")
SMALL = SMALL_RAW + SHARDED_SMALL


def _view2d(shape):
    return (1, shape[0]) if len(shape) == 1 else (math.prod(shape[:-1]), shape[-1])


def _small_layout(shapes, row_multiple):
    layout, at = [], 0
    for shape in shapes:
        r, c = _view2d(shape)
        rp = -(-r // 8) * 8
        layout.append((at, r, c, rp))
        at += rp * max(1, c // LANE)
    return layout, -(-at // row_multiple) * row_multiple


def pack_small(name, arrays, row_multiple):
    layout, rows = _small_layout([a.shape for a in arrays], row_multiple)

    def body(*refs):
        out = refs[-1]
        out[...] = jnp.zeros_like(out)
        for ref, (r0, r, c, rp) in zip(refs[:-1], layout):
            if c <= LANE:
                out[r0:r0 + r, 0:c] = ref[...]
            else:
                for q in range(c // LANE):
                    out[r0 + q * rp:r0 + q * rp + r, :] = ref[:, q * LANE:(q + 1) * LANE]

    return pl.pallas_call(
        body, name=name, out_shape=jax.ShapeDtypeStruct((rows, LANE), F32),
        compiler_params=pltpu.CompilerParams(vmem_limit_bytes=VMEM_LIMIT),
    )(*[a.reshape(_view2d(a.shape)) for a in arrays])


def unpack_small(name, packed, shapes):
    layout, _ = _small_layout(shapes, 8)

    def body(p_ref, *outs):
        for ref, (r0, r, c, rp) in zip(outs, layout):
            if c <= LANE:
                ref[...] = p_ref[r0:r0 + r, 0:c]
            else:
                for q in range(c // LANE):
                    ref[:, q * LANE:(q + 1) * LANE] = p_ref[r0 + q * rp:r0 + q * rp + r, :]

    res = pl.pallas_call(
        body, name=name, out_shape=[jax.ShapeDtypeStruct(_view2d(s), F32) for s in shapes],
        compiler_params=pltpu.CompilerParams(vmem_limit_bytes=VMEM_LIMIT),
    )(packed)
    return [a.reshape(s) for a, s in zip(res, shapes)]


def _step(x, target, w, m, v):
    mx, my, mc = lax.axis_index("x"), lax.axis_index("y"), lax.axis_index("c")
    me = (2 * mx + my).astype(jnp.int32)
    mc = mc.astype(jnp.int32)

    gathered = gather_shards("gather_weights", [w[n].astype(MMT) for n in BIG] + [w[n] for n in SHARDED_SMALL])
    W = dict(wg=gathered[0], wu=gathered[1], wd=gathered[2], win=gathered[3],
             pfull=gathered[4].transpose(0, 2, 3, 1, 4).reshape(DEPTH, 3, BW, D_MODEL),
             woutfull=gathered[5].reshape(DEPTH, D_MODEL, D_MODEL),
             gluw=gathered[6].reshape(DEPTH, BW, BW),
             nw=gathered[7].transpose(0, 2, 1, 3).reshape(DEPTH, 3, 1, D_MODEL),
             convw=gathered[8].transpose(0, 2, 1, 3).reshape(DEPTH, CONV_W, BW))
    loss, dx, big, small = local_step(x[0], target[0], W, {k: w[k] for k in SMALL_RAW})

    small_packed = pack_small("pack_small_grads", [small[n] for n in SMALL], NSH * 32)
    dq = D_MODEL // NSH
    big["branch_proj"] = big["branch_proj"].reshape(DEPTH, 3, BW, NSH, dq).transpose(0, 3, 1, 2, 4).reshape(DEPTH, NSH, 3 * BW, dq)
    grads = [big[n].reshape(big[n].shape[0], NSH, -1, big[n].shape[-1]) for n in BIG] + [small_packed.reshape(1, NSH, -1, LANE)]
    from_sibling = exchange_halves("reduce_cores", grads)
    merge = lambda a: a.reshape((-1,) + a.shape[2:])
    wire = [jnp.bfloat16] * len(BIG) + [F32]
    halves = [add_own_half(f"sum_cores_{i}", merge(g), merge(r), mc, wire[i]).reshape(r.shape)
              for i, (g, r) in enumerate(zip(grads, from_sibling))]
    from_chips = scatter_to_chips("reduce_chips", halves)
    own = [add_chips(f"sum_chips_{i}", h, r, me) for i, (h, r) in enumerate(zip(halves, from_chips))]
    other = share_halves("reduce_share", own)

    g, delta, new_m, new_v = {}, {}, {}, {}
    for i, n in enumerate(BIG):
        view = lambda a: a.reshape(own[i].shape[0], -1, own[i].shape[2])
        res = adamw_halves(f"adamw_{n}", view(w[n]), view(m[n]), view(v[n]), own[i], other[i], mc)
        g[n], delta[n], new_m[n], new_v[n] = [a.reshape(w[n].shape) for a in res]

    piece = jnp.stack([jnp.where(mc == 0, own[-1][0], other[-1][0]), jnp.where(mc == 0, other[-1][0], own[-1][0])])
    (all_small,) = gather_shards("gather_small", [piece])
    full_small = unpack_small("unpack_small_grads", all_small.transpose(1, 0, 2, 3).reshape(-1, LANE),
                              [small[n].shape for n in SMALL])
    g.update(zip(SMALL, full_small))
    g["norm_w"] = lax.dynamic_slice_in_dim(g["norm_w"], me * (D_MODEL // NSH), D_MODEL // NSH, axis=2)
    g["rg_conv_w"] = lax.dynamic_slice_in_dim(g["rg_conv_w"], me * (BW // NSH), BW // NSH, axis=2)

    packed = [pack_small(f"pack_small_{tag}", [src[n] for n in SMALL], 8)
              for tag, src in (("w", w), ("g", g), ("m", m), ("v", v))]
    for tag, dst, flat in zip(("delta", "m", "v"), (delta, new_m, new_v), adamw(*packed)):
        dst.update(zip(SMALL, unpack_small(f"unpack_small_{tag}", flat, [w[n].shape for n in SMALL])))

    total = lax.psum(loss[0, 0], ("x", "y", "c"))
    return (total, dx[None], *[g[n] for n in WEIGHTS], *[delta[n] for n in WEIGHTS],
            *[new_m[n] for n in WEIGHTS], *[new_v[n] for n in WEIGHTS])


def kernel(x, norm_w, final_norm_w, ffn_gate, ffn_up, ffn_down, w_in, branch_proj, w_out, s5_lambda_re, s5_lambda_im, s5_log_dt, s5_b_re, s5_b_im, s5_c_re, s5_c_im, s5_d, s5_glu_w, s5_glu_b, hg_lb_logits, hg_norm_w, rg_conv_w, rg_conv_b, rg_wa, rg_ba, rg_wx, rg_bx, rg_lambda, loss_target, m_norm_w, m_final_norm_w, m_ffn_gate, m_ffn_up, m_ffn_down, m_w_in, m_branch_proj, m_w_out, m_s5_lambda_re, m_s5_lambda_im, m_s5_log_dt, m_s5_b_re, m_s5_b_im, m_s5_c_re, m_s5_c_im, m_s5_d, m_s5_glu_w, m_s5_glu_b, m_hg_lb_logits, m_hg_norm_w, m_rg_conv_w, m_rg_conv_b, m_rg_wa, m_rg_ba, m_rg_wx, m_rg_bx, m_rg_lambda, v_norm_w, v_final_norm_w, v_ffn_gate, v_ffn_up, v_ffn_down, v_w_in, v_branch_proj, v_w_out, v_s5_lambda_re, v_s5_lambda_im, v_s5_log_dt, v_s5_b_re, v_s5_b_im, v_s5_c_re, v_s5_c_im, v_s5_d, v_s5_glu_w, v_s5_glu_b, v_hg_lb_logits, v_hg_norm_w, v_rg_conv_w, v_rg_conv_b, v_rg_wa, v_rg_ba, v_rg_wx, v_rg_bx, v_rg_lambda):
    ws = (norm_w, final_norm_w, ffn_gate, ffn_up, ffn_down, w_in, branch_proj, w_out, s5_lambda_re, s5_lambda_im, s5_log_dt, s5_b_re, s5_b_im, s5_c_re, s5_c_im, s5_d, s5_glu_w, s5_glu_b, hg_lb_logits, hg_norm_w, rg_conv_w, rg_conv_b, rg_wa, rg_ba, rg_wx, rg_bx, rg_lambda)
    ms = (m_norm_w, m_final_norm_w, m_ffn_gate, m_ffn_up, m_ffn_down, m_w_in, m_branch_proj, m_w_out, m_s5_lambda_re, m_s5_lambda_im, m_s5_log_dt, m_s5_b_re, m_s5_b_im, m_s5_c_re, m_s5_c_im, m_s5_d, m_s5_glu_w, m_s5_glu_b, m_hg_lb_logits, m_hg_norm_w, m_rg_conv_w, m_rg_conv_b, m_rg_wa, m_rg_ba, m_rg_wx, m_rg_bx, m_rg_lambda)
    vs = (v_norm_w, v_final_norm_w, v_ffn_gate, v_ffn_up, v_ffn_down, v_w_in, v_branch_proj, v_w_out, v_s5_lambda_re, v_s5_lambda_im, v_s5_log_dt, v_s5_b_re, v_s5_b_im, v_s5_c_re, v_s5_c_im, v_s5_d, v_s5_glu_w, v_s5_glu_b, v_hg_lb_logits, v_hg_norm_w, v_rg_conv_w, v_rg_conv_b, v_rg_wa, v_rg_ba, v_rg_wx, v_rg_bx, v_rg_lambda)
    return _step(x, loss_target, dict(zip(WEIGHTS, ws)), dict(zip(WEIGHTS, ms)), dict(zip(WEIGHTS, vs)))
```

```python
import functools
import math
from typing import NamedTuple

import jax
import jax.numpy as jnp
from jax import lax
from jax.experimental import pallas as pl
from jax.experimental.pallas import tpu as pltpu

F32 = jnp.float32
MMT = jnp.bfloat16
HI = lax.Precision.HIGHEST

D_MODEL = 1024
BW = 512
S5_GROUP, S5_GROUPS, S5_STATE = 16, 32, 64
S5_N = S5_GROUPS * S5_STATE
HG_HEADS, HG_D = 4, 128
HG_CHUNK = 128
RG_BLOCKS, RG_BLOCK = 8, 64
RG_C = 8.0
CONV_W = 4
D_FF = 2816
EPS = 1e-6
IN_TOTAL = 6656
NSH = 4
NSEG = 8
LANE = 128
VMEM_LIMIT = 56 * 1024 * 1024

ADAM_LR, ADAM_B1, ADAM_B2, ADAM_EPS, ADAM_WD, ADAM_STEP = 0.001, 0.9, 0.999, 1e-08, 0.01, 10

MESH = pl.DeviceIdType.MESH


class WP(NamedTuple):
    w: jax.Array
    p: jax.Array


def _dg(a, b, ca, cb):
    return lax.dot_general(a, b, (((ca,), (cb,)), ((), ())), preferred_element_type=F32)


@jax.custom_vjp
def _mmw(a, w, p):
    return _dg(a.astype(MMT), w, 1, 0)


def _mmw_fwd(a, w, p):
    return _mmw(a, w, p), (a, w)


def _mmw_bwd(res, g):
    a, w = res
    gb = g.astype(MMT)
    return _dg(gb, w, 1, 1), jnp.zeros_like(w), _dg(a.astype(MMT), gb, 0, 0)


_mmw.defvjp(_mmw_fwd, _mmw_bwd)


def mm(a, w):
    if isinstance(w, WP):
        return _mmw(a, w.w, w.p)
    return _dg(a.astype(MMT), w, 1, 0)


@jax.custom_vjp
def mma_nn(a, b):
    return _dg(a.astype(MMT), b.astype(MMT), 1, 0)


def _nn_f(a, b):
    return mma_nn(a, b), (a, b)


def _nn_b(res, g):
    a, b = res
    gb = g.astype(MMT)
    return _dg(gb, b.astype(MMT), 1, 1), _dg(a.astype(MMT), gb, 0, 0)


mma_nn.defvjp(_nn_f, _nn_b)


@jax.custom_vjp
def mma_nt(a, b):
    return _dg(a.astype(MMT), b.astype(MMT), 1, 1)


def _nt_f(a, b):
    return mma_nt(a, b), (a, b)


def _nt_b(res, g):
    a, b = res
    gb = g.astype(MMT)
    return _dg(gb, b.astype(MMT), 1, 0), _dg(gb, a.astype(MMT), 0, 0)


mma_nt.defvjp(_nt_f, _nt_b)


@jax.custom_vjp
def mma_tn(a, b):
    return _dg(a.astype(MMT), b.astype(MMT), 0, 0)


def _tn_f(a, b):
    return mma_tn(a, b), (a, b)


def _tn_b(res, g):
    a, b = res
    gb = g.astype(MMT)
    return _dg(b.astype(MMT), gb, 1, 1), _dg(a.astype(MMT), gb, 1, 0)


mma_tn.defvjp(_tn_f, _tn_b)


def mm_exact(m, x):
    return jnp.dot(m, x, precision=HI, preferred_element_type=F32)


def _rms(x, w):
    return x * lax.rsqrt(jnp.mean(x * x, axis=-1, keepdims=True) + EPS) * w


def _expm1(x):
    series = x * (1.0 + x * (1.0 / 2) * (1.0 + x * (1.0 / 3) * (1.0 + x * (1.0 / 4) * (1.0 + x * (1.0 / 5) * (1.0 + x * (1.0 / 6))))))
    return jnp.where(jnp.abs(x) < 0.1, series, jnp.exp(x) - 1.0)


def _bspec(block, fn, order):
    if order == "is":
        return pl.BlockSpec(block, lambda i, s: fn(s, i))
    return pl.BlockSpec(block, lambda s, i: fn(s, i))


def tile_fwd(fn, name, n_i, n_s, ins, outs, s_outer=False):
    n_in = len(ins)
    order = "si" if s_outer else "is"
    assert not (s_outer and any(o[4] for o in outs))

    def body(*refs):
        s = pl.program_id(0 if s_outer else 1)
        res = fn(*[r[...] for r in refs[:n_in]], s)
        for o_ref, val, spec in zip(refs[n_in:], res, outs):
            if spec[4] and n_s > 1:
                @pl.when(s == 0)
                def _(o_ref=o_ref, val=val):
                    o_ref[...] = val.astype(o_ref.dtype)

                @pl.when(s != 0)
                def _(o_ref=o_ref, val=val):
                    o_ref[...] += val.astype(o_ref.dtype)
            else:
                o_ref[...] = val.astype(o_ref.dtype)

    return pl.pallas_call(
        body, grid=(n_s, n_i) if s_outer else (n_i, n_s), name=name,
        in_specs=[_bspec(b, f, order) for _, b, f in ins],
        out_specs=[_bspec(b, f, order) for _, _, b, f, _ in outs],
        out_shape=[jax.ShapeDtypeStruct(sh, dt) for sh, dt, _, _, _ in outs],
        compiler_params=pltpu.CompilerParams(vmem_limit_bytes=VMEM_LIMIT,
                                             dimension_semantics=("arbitrary", "arbitrary")),
    )(*[a for a, _, _ in ins])


def tile_bwd(fn, name, n_i, n_s, ins, cts, gouts):
    groups = [c if isinstance(c, list) else [c] for c in cts]
    cts = [blk for grp in groups for blk in grp]
    n_in, n_ct = len(ins), len(cts)
    kinds = [k for _, _, _, k in ins]
    d_pos = [j for j, k in enumerate(kinds) if k != "c"]
    shared = [(gi, spec[4]) for gi, spec in enumerate(gouts) if len(spec) == 5 and spec[4] is not None]
    n_sh = len(shared)

    def body(*refs):
        s, i = pl.program_id(0), pl.program_id(1)
        vals = [r[...] for r in refs[:n_in]]
        ct_refs, ctv = list(refs[n_in:n_in + n_ct]), []
        for grp in groups:
            parts = [ct_refs.pop(0)[...] for _ in grp]
            ctv.append(parts[0] if len(parts) == 1 else jnp.concatenate(parts, axis=1))
        ctv = tuple(ctv)
        g_refs = refs[n_in + n_ct + n_sh:]

        def g(*dv):
            args = list(vals)
            for j, v in zip(d_pos, dv):
                args[j] = WP(vals[j], v) if kinds[j] == "w" else v
            return tuple(fn(*args))

        dv0 = [jnp.zeros(vals[j].shape, F32) if kinds[j] == "w" else vals[j] for j in d_pos]
        _, vjp = jax.vjp(g, *dv0)
        grads = vjp(ctv)
        for g_ref, gv, spec in zip(g_refs, grads, gouts):
            mode = spec[3]
            if mode == "write":
                g_ref[...] = gv.astype(g_ref.dtype)
            else:
                first = (i == 0) if mode == "acc_i" else jnp.logical_and(i == 0, s == 0)

                @pl.when(first)
                def _(g_ref=g_ref, gv=gv):
                    g_ref[...] = gv.astype(g_ref.dtype)

                @pl.when(jnp.logical_not(first))
                def _(g_ref=g_ref, gv=gv):
                    g_ref[...] += gv.astype(g_ref.dtype)

    return pl.pallas_call(
        body, grid=(n_s, n_i), name=name,
        in_specs=([_bspec(b, f, "si") for _, b, f, _ in ins] + [_bspec(b, f, "si") for _, b, f in cts]
                  + [pl.BlockSpec(memory_space=pl.ANY)] * n_sh),
        out_specs=[_bspec(spec[1], spec[2], "si") for spec in gouts],
        out_shape=[jax.ShapeDtypeStruct(spec[0], F32) for spec in gouts],
        input_output_aliases={n_in + n_ct + k: gi for k, (gi, _) in enumerate(shared)},
        compiler_params=pltpu.CompilerParams(vmem_limit_bytes=VMEM_LIMIT,
                                             dimension_semantics=("arbitrary", "arbitrary")),
    )(*[a for a, _, _, _ in ins], *[a for a, _, _ in cts], *[buf for _, buf in shared])


def _row_tile(rows, width, itemsize=4, budget=2 * 1024 * 1024, mult=8):
    best = mult
    for t in range(mult, rows + 1, mult):
        if rows % t == 0 and t * width * itemsize <= budget:
            best = t
    return best


def add_n(name, terms, shape):
    rows, cols = shape
    tr = _row_tile(rows, cols)

    def body(*refs):
        acc = refs[0][...]
        for r in refs[1:-1]:
            acc = acc + r[...]
        refs[-1][...] = acc

    specs = []
    for _, lead in terms:
        specs.append(pl.BlockSpec((None,) * len(lead) + (tr, cols), functools.partial(lambda i, lead: (*lead, i, 0), lead=lead)))
    return pl.pallas_call(
        body, grid=(rows // tr,), name=name, in_specs=specs,
        out_specs=pl.BlockSpec((tr, cols), lambda i: (i, 0)),
        out_shape=jax.ShapeDtypeStruct((rows, cols), F32),
    )(*[a for a, _ in terms])


def ffn_core(x, nw, wg, wu, wd):
    h = _rms(x, nw)
    return (0.5 * mm(jax.nn.silu(mm(h, wg)) * mm(h, wu), wd),)


def pre_core(x, nw, win):
    return (mm(_rms(x, nw), win),)


def _split_lanes(y):
    return jnp.stack([y[:, k * LANE:(k + 1) * LANE] for k in range(y.shape[1] // LANE)], axis=0)


def _join_lanes(y3):
    return jnp.concatenate([y3[k] for k in range(y3.shape[0])], axis=1)


def s5_pre_core(u, bmat):
    bu = mm(u, bmat)
    return _split_lanes(bu[:, :S5_N]), _split_lanes(bu[:, S5_N:])


def mid_core(xr, xi, u, o, g, hs, gc, hmat, cmat, d, gluw, glub, hgw):
    xs = jnp.concatenate([_join_lanes(xr), _join_lanes(xi)], axis=1)
    y = mm(xs, cmat) + d * u
    z = jax.nn.gelu(y)
    ya = z * jax.nn.sigmoid(mm(z, gluw) + glub)
    ms = mm_exact(o * o, hmat)
    yb = o * lax.rsqrt(ms + EPS) * hgw * jax.nn.silu(g)
    yc = hs * jax.nn.gelu(gc)
    return ya, yb, yc


def _sub(w, n):
    return WP(w.w[n], w.p[n]) if isinstance(w, WP) else w[n]


def merge_core(ya, yb, yc, g0, g1, g2, g3, g4, g5, p, wout):
    gate = lambda a, b: jax.nn.sigmoid(jnp.concatenate([a, b], axis=1))
    m = gate(g0, g1) * mm(ya, _sub(p, 0)) + gate(g2, g3) * mm(yb, _sub(p, 1)) + gate(g4, g5) * mm(yc, _sub(p, 2))
    return (mm(m, wout),)


def gates_core(xc, wa, ba, wx, bx, lam):
    r = jax.nn.sigmoid(mm(xc, wa) + ba)
    i = jax.nn.sigmoid(mm(xc, wx) + bx)
    log_a = -RG_C * jax.nn.softplus(-lam) * r
    a = jnp.exp(log_a)
    b = jnp.sqrt(-_expm1(2.0 * log_a)) * (i * xc)
    return a, b


def _seg_rows(ref, k, j, n):
    rows = pl.ds(pl.multiple_of(j * NSEG, NSEG), NSEG)
    if k is None:
        return ref[rows, :]
    return ref[k, rows, :]


def _seg_store(ref, k, j, n, val):
    rows = pl.ds(pl.multiple_of(j * NSEG, NSEG), NSEG)
    if k is None:
        ref[rows, :] = val
    else:
        ref[k, rows, :] = val


def _seg_carries(er, ei, pr, pi, reverse):
    rows = lax.broadcasted_iota(jnp.int32, er.shape, 0)
    cr = jnp.zeros_like(er)
    ci = None if ei is None else jnp.zeros_like(er)
    order = range(NSEG - 2, -1, -1) if reverse else range(1, NSEG)
    shift = NSEG - 1 if reverse else 1
    for s in order:
        if ei is None:
            tr = er + pr * cr
            cr = jnp.where(rows == s, pltpu.roll(tr, shift, 0), cr)
        else:
            tr = er + pr * cr - pi * ci
            ti = ei + pr * ci + pi * cr
            cr = jnp.where(rows == s, pltpu.roll(tr, shift, 0), cr)
            ci = jnp.where(rows == s, pltpu.roll(ti, shift, 0), ci)
    return cr, ci


S5_K = 2


def s5_scan_fwd(bur, bui, ar, ai, L):
    n = L // NSEG
    nb = S5_N // LANE
    K = S5_K

    def body(br_ref, bi_ref, ar_ref, ai_ref, xr_ref, xi_ref):
        zero = jnp.zeros((NSEG, LANE), F32)
        A = [(jnp.broadcast_to(ar_ref[k], (NSEG, LANE)), jnp.broadcast_to(ai_ref[k], (NSEG, LANE))) for k in range(K)]

        def p1(j, st):
            new = []
            for k in range(K):
                sr, si, pr, pi = st[k]
                a_r, a_i = A[k]
                nr = a_r * sr - a_i * si + _seg_rows(br_ref, k, j, n)
                ni = a_r * si + a_i * sr + _seg_rows(bi_ref, k, j, n)
                _seg_store(xr_ref, k, j, n, nr)
                _seg_store(xi_ref, k, j, n, ni)
                new.append((nr, ni, a_r * pr - a_i * pi, a_r * pi + a_i * pr))
            return tuple(new)

        st = lax.fori_loop(0, n, p1, tuple((zero, zero, zero + 1.0, zero) for _ in range(K)))
        C = [_seg_carries(st[k][0], st[k][1], st[k][2], st[k][3], False) for k in range(K)]

        def p2(j, st):
            new = []
            for k in range(K):
                pr, pi = st[k]
                a_r, a_i = A[k]
                pr, pi = a_r * pr - a_i * pi, a_r * pi + a_i * pr
                cr, ci = C[k]
                _seg_store(xr_ref, k, j, n, _seg_rows(xr_ref, k, j, n) + pr * cr - pi * ci)
                _seg_store(xi_ref, k, j, n, _seg_rows(xi_ref, k, j, n) + pr * ci + pi * cr)
                new.append((pr, pi))
            return tuple(new)

        lax.fori_loop(0, n, p2, tuple((zero + 1.0, zero) for _ in range(K)))

    blk = pl.BlockSpec((K, L, LANE), lambda g: (g, 0, 0))
    ablk = pl.BlockSpec((K, 1, LANE), lambda g: (g, 0, 0))
    return pl.pallas_call(
        body, grid=(nb // K,), name="s5_scan_fwd",
        in_specs=[blk, blk, ablk, ablk], out_specs=[blk, blk],
        out_shape=[jax.ShapeDtypeStruct((nb, L, LANE), F32)] * 2,
        compiler_params=pltpu.CompilerParams(vmem_limit_bytes=VMEM_LIMIT),
    )(bur, bui, ar, ai)


def s5_scan_bwd(dxr, dxi, xr, xi, ar, ai, L):
    n = L // NSEG
    nb = S5_N // LANE
    K = S5_K

    def body(dr_ref, di_ref, xr_ref, xi_ref, ar_ref, ai_ref, gr_ref, gi_ref, dar_ref, dai_ref):
        zero = jnp.zeros((NSEG, LANE), F32)
        rows = lax.broadcasted_iota(jnp.int32, (NSEG, LANE), 0)
        A = [(jnp.broadcast_to(ar_ref[k], (NSEG, LANE)), -jnp.broadcast_to(ai_ref[k], (NSEG, LANE))) for k in range(K)]

        def p1(jj, st):
            j = n - 1 - jj
            new = []
            for k in range(K):
                sr, si, pr, pi = st[k]
                a_r, a_i = A[k]
                nr = a_r * sr - a_i * si + _seg_rows(dr_ref, k, j, n)
                ni = a_r * si + a_i * sr + _seg_rows(di_ref, k, j, n)
                _seg_store(gr_ref, k, j, n, nr)
                _seg_store(gi_ref, k, j, n, ni)
                new.append((nr, ni, a_r * pr - a_i * pi, a_r * pi + a_i * pr))
            return tuple(new)

        st = lax.fori_loop(0, n, p1, tuple((zero, zero, zero + 1.0, zero) for _ in range(K)))
        C = [_seg_carries(st[k][0], st[k][1], st[k][2], st[k][3], True) for k in range(K)]
        xb = [(jnp.where(rows == 0, 0.0, pltpu.roll(_seg_rows(xr_ref, k, n - 1, n), 1, 0)),
               jnp.where(rows == 0, 0.0, pltpu.roll(_seg_rows(xi_ref, k, n - 1, n), 1, 0))) for k in range(K)]

        def p2(jj, st):
            j = n - 1 - jj
            jp = jnp.maximum(j - 1, 0)
            new = []
            for k in range(K):
                pr, pi, acr, aci = st[k]
                a_r, a_i = A[k]
                pr, pi = a_r * pr - a_i * pi, a_r * pi + a_i * pr
                cr, ci = C[k]
                g_r = _seg_rows(gr_ref, k, j, n) + pr * cr - pi * ci
                g_i = _seg_rows(gi_ref, k, j, n) + pr * ci + pi * cr
                _seg_store(gr_ref, k, j, n, g_r)
                _seg_store(gi_ref, k, j, n, g_i)
                xpr = jnp.where(j == 0, xb[k][0], _seg_rows(xr_ref, k, jp, n))
                xpi = jnp.where(j == 0, xb[k][1], _seg_rows(xi_ref, k, jp, n))
                new.append((pr, pi, acr + g_r * xpr + g_i * xpi, aci + g_i * xpr - g_r * xpi))
            return tuple(new)

        st = lax.fori_loop(0, n, p2, tuple((zero + 1.0, zero, zero, zero) for _ in range(K)))
        for k in range(K):
            dar_ref[k] = jnp.sum(st[k][2], axis=0, keepdims=True)
            dai_ref[k] = jnp.sum(st[k][3], axis=0, keepdims=True)

    blk = pl.BlockSpec((K, L, LANE), lambda g: (g, 0, 0))
    ablk = pl.BlockSpec((K, 1, LANE), lambda g: (g, 0, 0))
    return pl.pallas_call(
        body, grid=(nb // K,), name="s5_scan_bwd",
        in_specs=[blk, blk, blk, blk, ablk, ablk], out_specs=[blk, blk, ablk, ablk],
        out_shape=[jax.ShapeDtypeStruct((nb, L, LANE), F32)] * 2 + [jax.ShapeDtypeStruct((nb, 1, LANE), F32)] * 2,
        compiler_params=pltpu.CompilerParams(vmem_limit_bytes=VMEM_LIMIT),
    )(dxr, dxi, xr, xi, ar, ai)


def rg_scan_fwd(a, b, L):
    n = L // NSEG

    def body(a_ref, b_ref, h_ref):
        zero = jnp.zeros((NSEG, LANE), F32)

        def p1(j, st):
            h, p = st
            aj = _seg_rows(a_ref, None, j, n)
            h = aj * h + _seg_rows(b_ref, None, j, n)
            _seg_store(h_ref, None, j, n, h)
            return h, aj * p

        e, pe = lax.fori_loop(0, n, p1, (zero, zero + 1.0))
        c, _ = _seg_carries(e, None, pe, None, False)

        def p2(j, p):
            p = _seg_rows(a_ref, None, j, n) * p
            _seg_store(h_ref, None, j, n, _seg_rows(h_ref, None, j, n) + p * c)
            return p

        lax.fori_loop(0, n, p2, zero + 1.0)

    blk = pl.BlockSpec((L, LANE), lambda g: (0, g))
    return pl.pallas_call(
        body, grid=(BW // LANE,), name="rg_scan_fwd", in_specs=[blk, blk], out_specs=blk,
        out_shape=jax.ShapeDtypeStruct((L, BW), F32),
        compiler_params=pltpu.CompilerParams(vmem_limit_bytes=VMEM_LIMIT),
    )(a, b)


def rg_scan_bwd(a, h, dh, L):
    n = L // NSEG

    def body(a_ref, h_ref, dh_ref, da_ref, db_ref):
        zero = jnp.zeros((NSEG, LANE), F32)
        rows = lax.broadcasted_iota(jnp.int32, (NSEG, LANE), 0)
        a_edge = jnp.where(rows == NSEG - 1, 0.0, pltpu.roll(_seg_rows(a_ref, None, 0, n), NSEG - 1, 0))
        h_edge = jnp.where(rows == 0, 0.0, pltpu.roll(_seg_rows(h_ref, None, n - 1, n), 1, 0))

        def mult(j):
            return jnp.where(j == n - 1, a_edge, _seg_rows(a_ref, None, jnp.minimum(j + 1, n - 1), n))

        def p1(jj, st):
            j = n - 1 - jj
            g, p = st
            m = mult(j)
            g = m * g + _seg_rows(dh_ref, None, j, n)
            _seg_store(db_ref, None, j, n, g)
            return g, m * p

        e, pe = lax.fori_loop(0, n, p1, (zero, zero + 1.0))
        c, _ = _seg_carries(e, None, pe, None, True)

        def p2(jj, p):
            j = n - 1 - jj
            p = mult(j) * p
            g = _seg_rows(db_ref, None, j, n) + p * c
            _seg_store(db_ref, None, j, n, g)
            hp = jnp.where(j == 0, h_edge, _seg_rows(h_ref, None, jnp.maximum(j - 1, 0), n))
            _seg_store(da_ref, None, j, n, g * hp)
            return p

        lax.fori_loop(0, n, p2, zero + 1.0)

    blk = pl.BlockSpec((L, LANE), lambda g: (0, g))
    return pl.pallas_call(
        body, grid=(BW // LANE,), name="rg_scan_bwd", in_specs=[blk, blk, blk], out_specs=[blk, blk],
        out_shape=[jax.ShapeDtypeStruct((L, BW), F32)] * 2,
        compiler_params=pltpu.CompilerParams(vmem_limit_bytes=VMEM_LIMIT),
    )(a, h, dh)


def _hg_consts(C):
    t = lax.broadcasted_iota(jnp.int32, (C, C), 0)
    s = lax.broadcasted_iota(jnp.int32, (C, C), 1)
    tril = (s <= t).astype(F32)
    diag = (s == t).astype(F32)
    levels = []
    k = 1
    while (1 << k) <= C:
        m = 1 << (k - 1)
        same = (t >> k) == (s >> k)
        t_right = ((t >> (k - 1)) & 1) == 1
        s_left = ((s >> (k - 1)) & 1) == 0
        mask = jnp.logical_and(same, jnp.logical_and(t_right, s_left)).astype(F32)
        bnd = ((t >> k) << k) + (m - 1)
        levels.append((mask, (s <= bnd).astype(F32)))
        k += 1
    return tril, diag, levels


def hg_chunk(st, q, z, v, lb):
    C = q.shape[0]
    tril, diag, levels = _hg_consts(C)
    sig = jax.nn.sigmoid(z)
    lf = jnp.log(lb + (1.0 - lb) * sig)
    k = (1.0 - lb) * jax.nn.sigmoid(-z)
    qh = jax.nn.silu(q)
    b = mm_exact(tril, lf)
    blast = jnp.sum(lf, axis=0, keepdims=True)
    qe = qh * jnp.exp(b)
    kd = k * jnp.exp(blast - b)
    scaled = []
    for _, sel in levels:
        ref = mm_exact(sel, lf)
        scaled.append((qh * jnp.exp(jnp.minimum(b - ref, 0.0)), k * jnp.exp(jnp.minimum(ref - b, 0.0))))
    outs, news = [], []
    for h in range(HG_HEADS):
        sl = slice(h * HG_D, (h + 1) * HG_D)
        st_h = st[h * HG_D:(h + 1) * HG_D, :]
        sc = diag * mma_nt(qh[:, sl], k[:, sl])
        for (mask, _), (qt, kt) in zip(levels, scaled):
            sc = sc + mask * mma_nt(qt[:, sl], kt[:, sl])
        outs.append(mma_nt(qe[:, sl], st_h) + mma_nn(sc, v[:, sl]))
        news.append(st_h * jnp.exp(blast[:, sl]) + mma_tn(v[:, sl], kd[:, sl]))
    return jnp.concatenate(news, axis=0), jnp.concatenate(outs, axis=1)


def hg_fwd(qzv, lb, L):
    C = HG_CHUNK
    nc = L // C

    def body(q_ref, z_ref, v_ref, lb_ref, o_ref, sst_ref, st_ref):
        @pl.when(pl.program_id(0) == 0)
        def _():
            st_ref[...] = jnp.zeros_like(st_ref)

        st = st_ref[...]
        sst_ref[...] = st
        new, o = hg_chunk(st, q_ref[...], z_ref[...], v_ref[...], lb_ref[...])
        st_ref[...] = new
        o_ref[...] = o

    col = lambda cb: pl.BlockSpec((C, BW), functools.partial(lambda c, cb: (c, cb), cb=cb))
    return pl.pallas_call(
        body, grid=(nc,), name="hg_fwd",
        in_specs=[col(0), col(1), col(2), pl.BlockSpec((1, BW), lambda c: (0, 0))],
        out_specs=[pl.BlockSpec((C, BW), lambda c: (c, 0)), pl.BlockSpec((None, BW, HG_D), lambda c: (c, 0, 0))],
        out_shape=[jax.ShapeDtypeStruct((L, BW), F32), jax.ShapeDtypeStruct((nc, BW, HG_D), F32)],
        scratch_shapes=[pltpu.VMEM((BW, HG_D), F32)],
        compiler_params=pltpu.CompilerParams(vmem_limit_bytes=VMEM_LIMIT, dimension_semantics=("arbitrary",)),
    )(qzv, qzv, qzv, lb)


def hg_bwd(qzv, lb, sst, do, L):
    C = HG_CHUNK
    nc = L // C

    def body(q_ref, z_ref, v_ref, lb_ref, sst_ref, do_ref, dq_ref, dz_ref, dv_ref, dlb_ref, dst_ref):
        @pl.when(pl.program_id(0) == 0)
        def _():
            dst_ref[...] = jnp.zeros_like(dst_ref)
            dlb_ref[...] = jnp.zeros_like(dlb_ref)

        _, vjp = jax.vjp(hg_chunk, sst_ref[...], q_ref[...], z_ref[...], v_ref[...], lb_ref[...])
        dst, dq, dz, dv, dlb = vjp((dst_ref[...], do_ref[...]))
        dst_ref[...] = dst
        dq_ref[...] = dq
        dz_ref[...] = dz
        dv_ref[...] = dv
        dlb_ref[...] += dlb

    col = lambda cb: pl.BlockSpec((C, BW), functools.partial(lambda c, cb: (nc - 1 - c, cb), cb=cb))
    rev = pl.BlockSpec((C, BW), lambda c: (nc - 1 - c, 0))
    return pl.pallas_call(
        body, grid=(nc,), name="hg_bwd",
        in_specs=[col(0), col(1), col(2), pl.BlockSpec((1, BW), lambda c: (0, 0)),
                  pl.BlockSpec((None, BW, HG_D), lambda c: (nc - 1 - c, 0, 0)), rev],
        out_specs=[rev, rev, rev, pl.BlockSpec((1, BW), lambda c: (0, 0))],
        out_shape=[jax.ShapeDtypeStruct((L, BW), F32)] * 3 + [jax.ShapeDtypeStruct((1, BW), F32)],
        scratch_shapes=[pltpu.VMEM((BW, HG_D), F32)],
        compiler_params=pltpu.CompilerParams(vmem_limit_bytes=VMEM_LIMIT, dimension_semantics=("arbitrary",)),
    )(qzv, qzv, qzv, lb, sst, do)


def _shift_down(x, d, rows, L):
    if d == 0:
        return x
    wrapped = jnp.where((rows & (NSEG - 1)) == 0, 0.0, pltpu.roll(x, NSEG * d + 1, 0))
    return jnp.where(rows < NSEG * d, wrapped, pltpu.roll(x, NSEG * d, 0))


def _shift_up(x, d, rows, L):
    if d == 0:
        return x
    wrapped = jnp.where((rows & (NSEG - 1)) == NSEG - 1, 0.0, pltpu.roll(x, L - (NSEG * d + 1), 0))
    return jnp.where(rows >= L - NSEG * d, wrapped, pltpu.roll(x, L - NSEG * d, 0))


def conv_fwd(proj, w, b, L):
    def body(x_ref, w_ref, b_ref, o_ref):
        x = x_ref[...]
        rows = lax.broadcasted_iota(jnp.int32, x.shape, 0)
        acc = jnp.broadcast_to(b_ref[...], x.shape)
        for k in range(CONV_W):
            acc = acc + w_ref[pl.ds(k, 1), :] * _shift_down(x, CONV_W - 1 - k, rows, L)
        o_ref[...] = acc

    nl = BW // LANE
    return pl.pallas_call(
        body, grid=(nl,), name="conv_fwd",
        in_specs=[pl.BlockSpec((L, LANE), lambda g: (0, 5 * nl + g)), pl.BlockSpec((CONV_W, LANE), lambda g: (0, g)),
                  pl.BlockSpec((1, LANE), lambda g: (0, g))],
        out_specs=pl.BlockSpec((L, LANE), lambda g: (0, g)),
        out_shape=jax.ShapeDtypeStruct((L, BW), F32),
        compiler_params=pltpu.CompilerParams(vmem_limit_bytes=VMEM_LIMIT),
    )(proj, w, b)


def conv_bwd(proj, w, dxc, L):
    def body(x_ref, w_ref, d_ref, dx_ref, dw_ref, db_ref):
        x, d = x_ref[...], d_ref[...]
        rows = lax.broadcasted_iota(jnp.int32, x.shape, 0)
        acc = jnp.zeros_like(x)
        for k in range(CONV_W):
            acc = acc + w_ref[pl.ds(k, 1), :] * _shift_up(d, CONV_W - 1 - k, rows, L)
            dw_ref[pl.ds(k, 1), :] = jnp.sum(d * _shift_down(x, CONV_W - 1 - k, rows, L), axis=0, keepdims=True)
        dx_ref[...] = acc
        db_ref[...] = jnp.sum(d, axis=0, keepdims=True)

    nl = BW // LANE
    blk = pl.BlockSpec((L, LANE), lambda g: (0, g))
    return pl.pallas_call(
        body, grid=(nl,), name="conv_bwd",
        in_specs=[pl.BlockSpec((L, LANE), lambda g: (0, 5 * nl + g)), pl.BlockSpec((CONV_W, LANE), lambda g: (0, g)), blk],
        out_specs=[blk, pl.BlockSpec((CONV_W, LANE), lambda g: (0, g)), pl.BlockSpec((1, LANE), lambda g: (0, g))],
        out_shape=[jax.ShapeDtypeStruct((L, BW), F32), jax.ShapeDtypeStruct((CONV_W, BW), F32),
                   jax.ShapeDtypeStruct((1, BW), F32)],
        compiler_params=pltpu.CompilerParams(vmem_limit_bytes=VMEM_LIMIT),
    )(proj, w, dxc)


def loss_fwd_bwd(x, fw, target, L, tm):
    def fn(x, fw, t):
        err = jnp.square(_rms(x, fw) - t)
        return jnp.sum(0.5 * jnp.mean(err, axis=-1, keepdims=True), axis=0, keepdims=True)

    def body(x_ref, fw_ref, t_ref, l_ref, dx_ref, dfw_ref):
        i = pl.program_id(0)
        t = t_ref[...]
        val, vjp = jax.vjp(lambda x, fw: fn(x, fw, t), x_ref[...], fw_ref[...])
        dx, dfw = vjp(jnp.ones((1, 1), F32))
        dx_ref[...] = dx

        @pl.when(i == 0)
        def _():
            l_ref[...] = jnp.zeros_like(l_ref)
            dfw_ref[...] = jnp.zeros_like(dfw_ref)

        l_ref[...] += jnp.broadcast_to(val, l_ref.shape)
        dfw_ref[...] += dfw

    row = pl.BlockSpec((tm, D_MODEL), lambda i: (i, 0))
    vec = pl.BlockSpec((1, D_MODEL), lambda i: (0, 0))
    return pl.pallas_call(
        body, grid=(L // tm,), name="loss_fwd_bwd", in_specs=[row, vec, row],
        out_specs=[pl.BlockSpec((1, LANE), lambda i: (0, 0)), row, vec],
        out_shape=[jax.ShapeDtypeStruct((1, LANE), F32), jax.ShapeDtypeStruct((L, D_MODEL), F32),
                   jax.ShapeDtypeStruct((1, D_MODEL), F32)],
        compiler_params=pltpu.CompilerParams(vmem_limit_bytes=VMEM_LIMIT, dimension_semantics=("arbitrary",)),
    )(x, fw, target)


def adamw(w, g, m, v):
    rows, cols = w.shape
    tr = _row_tile(rows, cols, budget=1024 * 1024)
    c1 = 1.0 - ADAM_B1 ** ADAM_STEP
    c2 = 1.0 - ADAM_B2 ** ADAM_STEP

    def body(w_ref, g_ref, m_ref, v_ref, d_ref, nm_ref, nv_ref):
        g = g_ref[...]
        nm = ADAM_B1 * m_ref[...] + (1.0 - ADAM_B1) * g
        nv = ADAM_B2 * v_ref[...] + (1.0 - ADAM_B2) * jnp.square(g)
        d_ref[...] = -ADAM_LR * ((nm / c1) / (jnp.sqrt(nv / c2) + ADAM_EPS) + ADAM_WD * w_ref[...])
        nm_ref[...] = nm
        nv_ref[...] = nv

    blk = pl.BlockSpec((tr, cols), lambda i: (i, 0))
    return pl.pallas_call(
        body, grid=(rows // tr,), name="adamw", in_specs=[blk] * 4, out_specs=[blk] * 3,
        out_shape=[jax.ShapeDtypeStruct((rows, cols), F32)] * 3,
    )(w, g, m, v)


def s5_prep(lam_re, lam_im, log_dt, b_re, b_im, c_re, c_im):
    lr = jnp.minimum(lam_re, -1e-4)
    li = lam_im
    dt = jnp.exp(log_dt)[:, None]
    mag = jnp.exp(lr * dt)
    ar = mag * jnp.cos(li * dt)
    ai = mag * jnp.sin(li * dt)
    den = lr * lr + li * li
    fr = ((ar - 1.0) * lr + ai * li) / den
    fi = (ai * lr - (ar - 1.0) * li) / den
    bbr = fr[..., None] * b_re - fi[..., None] * b_im
    bbi = fr[..., None] * b_im + fi[..., None] * b_re
    emb_b = lambda bb: _block_diag(bb.transpose(0, 2, 1).reshape(BW, S5_STATE), S5_GROUPS)
    emb_c = lambda cc: _block_diag(cc.transpose(0, 2, 1).reshape(S5_N, S5_GROUP), S5_GROUPS)
    bmat = jnp.concatenate([emb_b(bbr), emb_b(bbi)], axis=1)
    cmat = jnp.concatenate([emb_c(c_re), -emb_c(c_im)], axis=0)
    nb = S5_N // LANE
    return ar.reshape(nb, 1, LANE), ai.reshape(nb, 1, LANE), bmat, cmat


def _block_diag(stacked, groups):
    rows, c = stacked.shape
    r = rows // groups
    row_g = jnp.arange(rows)[:, None] // r
    col_g = jnp.arange(groups * c)[None, :] // c
    return jnp.where(row_g == col_g, jnp.tile(stacked, (1, groups)), 0.0)


def rg_prep(w):
    return _block_diag(w.reshape(BW, RG_BLOCK), RG_BLOCKS)


def hg_prep(logits):
    p = jax.nn.softmax(logits, axis=0)
    return jnp.cumsum(p, axis=0) - p[0]


def _head_mean_matrix():
    r = jnp.arange(BW) // HG_D
    return (r[:, None] == r[None, :]).astype(F32) / HG_D


def _to_segment_order(a):
    L = a.shape[0]
    return a.reshape(NSEG, L // NSEG, -1).transpose(1, 0, 2).reshape(a.shape)


def _to_time_order(a):
    L = a.shape[0]
    return a.reshape(L // NSEG, NSEG, -1).transpose(1, 0, 2).reshape(a.shape)


def _const(*idx):
    return lambda s, i: idx


def _rows(cb=0):
    return lambda s, i: (i, cb)


def _sum_parts(name, first, parts, shape):
    return add_n(name, [(first, ())] + [(parts, (s,)) for s in range(NSH)], shape)


def _ffn_weight_specs(l, j):
    F = D_FF // NSH
    one = pl.Buffered(1)
    return [pl.BlockSpec((None, NSH, None, D_MODEL, F), lambda i: (l, 0, j, 0, 0), pipeline_mode=one),
            pl.BlockSpec((None, NSH, None, D_MODEL, F), lambda i: (l, 0, j, 0, 0), pipeline_mode=one),
            pl.BlockSpec((None, NSH, None, F, D_MODEL), lambda i: (l, 0, j, 0, 0), pipeline_mode=one)]


def ffn_fwd(name, x, W, l, j, k, L, tm):
    D, F = D_MODEL, D_FF // NSH

    def body(x_ref, nw_ref, wg_ref, wu_ref, wd_ref, y_ref, g_ref, u_ref):
        x = x_ref[...]
        h = _rms(x, nw_ref[...]).astype(MMT)
        y = x
        for s in range(NSH):
            g = _dg(h, wg_ref[s], 1, 0)
            u = _dg(h, wu_ref[s], 1, 0)
            g_ref[s] = g.astype(g_ref.dtype)
            u_ref[s] = u.astype(u_ref.dtype)
            y = y + 0.5 * _dg((jax.nn.silu(g) * u).astype(MMT), wd_ref[s], 1, 0)
        y_ref[...] = y

    row = pl.BlockSpec((tm, D), lambda i: (i, 0))
    act = pl.BlockSpec((NSH, tm, F), lambda i: (0, i, 0))
    return pl.pallas_call(
        body, grid=(L // tm,), name=name,
        in_specs=[row, pl.BlockSpec((None, None, 1, D), lambda i: (l, k, 0, 0))] + _ffn_weight_specs(l, j),
        out_specs=[row, act, act],
        out_shape=[jax.ShapeDtypeStruct((L, D), F32), jax.ShapeDtypeStruct((NSH, L, F), MMT),
                   jax.ShapeDtypeStruct((NSH, L, F), MMT)],
        compiler_params=pltpu.CompilerParams(vmem_limit_bytes=VMEM_LIMIT, dimension_semantics=("arbitrary",)),
    )(x, W["nw"], W["wg"], W["wu"], W["wd"])


def ffn_bwd(name, x, g, u, dy, W, bufs, l, j, k, L, tm):
    D, F = D_MODEL, D_FF // NSH
    tma = min(tm, 128)

    def body_x(x_ref, nw_ref, dy_ref, g_ref, u_ref, wg_ref, wu_ref, wd_ref, dx_ref, dnw_ref, dg_ref, du_ref):
        x, nw, dy = x_ref[...], nw_ref[...], dy_ref[...]
        half_dy = (0.5 * dy).astype(MMT)
        dh = jnp.zeros(x.shape, F32)
        for s in range(NSH):
            gs, us = g_ref[s].astype(F32), u_ref[s].astype(F32)
            sig = jax.nn.sigmoid(gs)
            da = _dg(half_dy, wd_ref[s], 1, 1)
            du = (da * (gs * sig)).astype(MMT)
            dg = (da * us * (sig * (1.0 + gs * (1.0 - sig)))).astype(MMT)
            dg_ref[s] = dg
            du_ref[s] = du
            dh = dh + _dg(dg, wg_ref[s], 1, 1) + _dg(du, wu_ref[s], 1, 1)
        r = lax.rsqrt(jnp.mean(x * x, axis=-1, keepdims=True) + EPS)
        xhat = x * r
        dxh = dh * nw
        dx_ref[...] = dy + r * (dxh - xhat * jnp.mean(dxh * xhat, axis=-1, keepdims=True))

        @pl.when(pl.program_id(0) == 0)
        def _():
            dnw_ref[...] = jnp.zeros_like(dnw_ref)

        dnw_ref[...] += jnp.sum(dh * xhat, axis=0, keepdims=True)

    row = pl.BlockSpec((tma, D), lambda i: (i, 0))
    act = pl.BlockSpec((NSH, tma, F), lambda i: (0, i, 0))
    vec = pl.BlockSpec((None, None, 1, D), lambda i: (l, k, 0, 0))
    dx, dnw, dg, du = pl.pallas_call(
        body_x, grid=(L // tma,), name=name + "_x",
        in_specs=[row, vec, row, act, act] + _ffn_weight_specs(l, j),
        out_specs=[row, pl.BlockSpec((1, D), lambda i: (0, 0)), act, act],
        out_shape=[jax.ShapeDtypeStruct((L, D), F32), jax.ShapeDtypeStruct((1, D), F32),
                   jax.ShapeDtypeStruct((NSH, L, F), MMT), jax.ShapeDtypeStruct((NSH, L, F), MMT)],
        compiler_params=pltpu.CompilerParams(vmem_limit_bytes=VMEM_LIMIT, dimension_semantics=("arbitrary",)),
    )(x, W["nw"], dy, g, u, W["wg"], W["wu"], W["wd"])

    def body_w(x_ref, nw_ref, dy_ref, g_ref, u_ref, dg_ref, du_ref, *rest):
        dwg_ref, dwu_ref, dwd_ref = rest[-3:]
        h = _rms(x_ref[...], nw_ref[...]).astype(MMT)
        gs, us = g_ref[...].astype(F32), u_ref[...].astype(F32)
        a = (jax.nn.silu(gs) * us).astype(MMT)
        grads = (_dg(h, dg_ref[...], 0, 0), _dg(h, du_ref[...], 0, 0), _dg(a, (0.5 * dy_ref[...]).astype(MMT), 0, 0))
        first = pl.program_id(1) == 0
        for ref, val in zip((dwg_ref, dwu_ref, dwd_ref), grads):
            @pl.when(first)
            def _(ref=ref, val=val):
                ref[...] = val

            @pl.when(jnp.logical_not(first))
            def _(ref=ref, val=val):
                ref[...] += val

    keys = ("ffn_gate", "ffn_up", "ffn_down")
    given = [bufs[key] for key in keys if bufs.get(key) is not None]
    row2 = pl.BlockSpec((tm, D), lambda s, i: (i, 0))
    act2 = pl.BlockSpec((None, tm, F), lambda s, i: (s, i, 0))
    stk = lambda r, c: pl.BlockSpec((None, None, r, c), lambda s, i: (2 * l + j, s, 0, 0))
    bufs["ffn_gate"], bufs["ffn_up"], bufs["ffn_down"] = pl.pallas_call(
        body_w, grid=(NSH, L // tm), name=name + "_w",
        in_specs=[row2, pl.BlockSpec((None, None, 1, D), lambda s, i: (l, k, 0, 0)), row2, act2, act2, act2, act2]
        + [pl.BlockSpec(memory_space=pl.ANY)] * len(given),
        out_specs=[stk(D, F), stk(D, F), stk(F, D)],
        out_shape=[jax.ShapeDtypeStruct((2 * DEPTH, NSH, D, F), F32)] * 2 + [jax.ShapeDtypeStruct((2 * DEPTH, NSH, F, D), F32)],
        input_output_aliases={7 + n: n for n in range(len(given))},
        compiler_params=pltpu.CompilerParams(vmem_limit_bytes=VMEM_LIMIT, dimension_semantics=("arbitrary", "arbitrary")),
    )(x, W["nw"], dy, g, u, dg, du, *given)
    return dx, dnw


def layer_fwd(l, x0, W, P, L, tm):
    D = D_MODEL
    n_i = L // tm
    x1, g0, u0 = ffn_fwd(f"ffn_fwd_{l}0", x0, W, l, 0, 0, L, tm)
    proj = tile_fwd(
        lambda x, nw, win, s: pre_core(x, nw, win), f"pre_fwd_{l}", n_i, NSH,
        [(x1, (tm, D), _rows()), (W["nw"], (None, None, 1, D), _const(l, 1, 0, 0)),
         (W["win"], (None, None, D, IN_TOTAL // NSH), lambda s, i: (l, s, 0, 0))],
        [((L, IN_TOTAL), F32, (tm, IN_TOTAL // NSH), lambda s, i: (i, s), False)], s_outer=True)[0]
    nb = S5_N // LANE
    blk3 = lambda s, i: (0, i, 0)
    bur, bui = tile_fwd(
        lambda u, bmat, s: s5_pre_core(u, bmat), f"s5pre_fwd_{l}", n_i, 1,
        [(proj, (tm, BW), _rows(0)), (P["bmat"], (None, BW, 2 * S5_N), _const(l, 0, 0))],
        [((nb, L, LANE), F32, (nb, tm, LANE), blk3, False)] * 2)
    xr, xi = s5_scan_fwd(bur, bui, P["ar"][l], P["ai"][l], L)
    qzv = _to_time_order(proj[:, BW:4 * BW])
    o_t, sst = hg_fwd(qzv, P["lb"][l], L)
    o = _to_segment_order(o_t)
    xc = conv_fwd(proj, W["convw"][l], P["convb"][l], L)
    vec = (None, 1, BW)
    a, b = tile_fwd(
        lambda xc, wa, ba, wx, bx, lam, s: gates_core(xc, wa, ba, wx, bx, lam), f"gates_fwd_{l}", n_i, 1,
        [(xc, (tm, BW), _rows()), (P["wa"], (None, BW, BW), _const(l, 0, 0)), (P["ba"], vec, _const(l, 0, 0)),
         (P["wx"], (None, BW, BW), _const(l, 0, 0)), (P["bx"], vec, _const(l, 0, 0)), (P["lam"], vec, _const(l, 0, 0))],
        [((L, BW), F32, (tm, BW), _rows(), False)] * 2)
    hs = rg_scan_fwd(a, b, L)
    tmm = tm
    ya, yb, yc = tile_fwd(
        lambda *a: mid_core(*a[:-1]), f"mid_fwd_{l}", L // tmm, 1,
        [(xr, (nb, tmm, LANE), blk3), (xi, (nb, tmm, LANE), blk3), (proj, (tmm, BW), _rows(0)), (o, (tmm, BW), _rows()),
         (proj, (tmm, BW), _rows(4)), (hs, (tmm, BW), _rows()), (proj, (tmm, BW), _rows(6)),
         (P["hmat"], (BW, BW), _const(0, 0)), (P["cmat"], (None, 2 * S5_N, BW), _const(l, 0, 0)), (P["d"], vec, _const(l, 0, 0)),
         (W["gluw"], (None, BW, BW), _const(l, 0, 0)), (P["glub"], vec, _const(l, 0, 0)), (P["hgw"], vec, _const(l, 0, 0))],
        [((L, BW), F32, (tmm, BW), _rows(), False)] * 3)
    x2 = tile_fwd(
        lambda x, *rest: (x + merge_core(*rest[:-1])[0],), f"merge_fwd_{l}", n_i, 1,
        [(x1, (tm, D), _rows()), (ya, (tm, BW), _rows()), (yb, (tm, BW), _rows()), (yc, (tm, BW), _rows())]
        + [(proj, (tm, BW), _rows(7 + k)) for k in range(6)]
        + [(W["pfull"], (None, 3, BW, D), _const(l, 0, 0, 0)), (W["woutfull"], (None, D, D), _const(l, 0, 0))],
        [((L, D), F32, (tm, D), _rows(), False)])[0]
    x3, g1, u1 = ffn_fwd(f"ffn_fwd_{l}1", x2, W, l, 1, 2, L, tm)
    saved = dict(x0=x0, x1=x1, x2=x2, proj=proj, xr=xr, xi=xi, o=o, sst=sst, xc=xc, a=a, hs=hs, ya=ya, yb=yb, yc=yc,
                 qzv=qzv, g0=g0, u0=u0, g1=g1, u1=u1)
    return x3, saved


def layer_bwd(l, dx3, sv, W, P, bufs, L, tm):
    D = D_MODEL
    n_i = L // tm
    nb = S5_N // LANE
    dq = D // NSH
    vec = (None, 1, BW)
    vout = ((1, BW), (1, BW), _const(0, 0), "acc_all")
    blk3 = lambda s, i: (0, i, 0)
    small = {}
    proj = sv["proj"]

    dx2, dnw2 = ffn_bwd(f"ffn_bwd_{l}1", sv["x2"], sv["g1"], sv["u1"], dx3, W, bufs, l, 1, 2, L, tm)

    rw256 = ((L, BW), (tm, BW), _rows(), "write")
    res = tile_bwd(
        merge_core, f"merge_bwd_{l}", n_i, 1,
        [(sv["ya"], (tm, BW), _rows(), "r"), (sv["yb"], (tm, BW), _rows(), "r"), (sv["yc"], (tm, BW), _rows(), "r")]
        + [(proj, (tm, BW), _rows(7 + k), "r") for k in range(6)]
        + [(W["pfull"], (None, 3, BW, D), _const(l, 0, 0, 0), "w"), (W["woutfull"], (None, D, D), _const(l, 0, 0), "w")],
        [(dx2, (tm, D), _rows())],
        [rw256] * 9
        + [((DEPTH, 3, BW, D), (None, 3, BW, D), _const(l, 0, 0, 0), "acc_all", bufs.get("branch_proj")),
           ((DEPTH, D, D), (None, D, D), _const(l, 0, 0), "acc_all", bufs.get("w_out"))])
    dya, dyb, dyc = res[:3]
    dgm = res[3:9]
    bufs["branch_proj"], bufs["w_out"] = res[9:]

    tmm = min(tm, 128)
    rw = ((L, BW), (tmm, BW), _rows(), "write")
    xw = ((nb, L, LANE), (nb, tmm, LANE), blk3, "write")
    res = tile_bwd(
        mid_core, f"mid_bwd_{l}", L // tmm, 1,
        [(sv["xr"], (nb, tmm, LANE), blk3, "r"), (sv["xi"], (nb, tmm, LANE), blk3, "r"), (proj, (tmm, BW), _rows(0), "r"),
         (sv["o"], (tmm, BW), _rows(), "r"), (proj, (tmm, BW), _rows(4), "r"), (sv["hs"], (tmm, BW), _rows(), "r"),
         (proj, (tmm, BW), _rows(6), "r"), (P["hmat"], (BW, BW), _const(0, 0), "c"),
         (P["cmat"], (None, 2 * S5_N, BW), _const(l, 0, 0), "w"), (P["d"], vec, _const(l, 0, 0), "p"),
         (W["gluw"], (None, BW, BW), _const(l, 0, 0), "w"), (P["glub"], vec, _const(l, 0, 0), "p"),
         (P["hgw"], vec, _const(l, 0, 0), "p")],
        [(dya, (tmm, BW), _rows()), (dyb, (tmm, BW), _rows()), (dyc, (tmm, BW), _rows())],
        [xw, xw, rw, rw, rw, rw, rw,
         ((DEPTH, 2 * S5_N, BW), (None, 2 * S5_N, BW), _const(l, 0, 0), "acc_all", bufs.get("cmat")), vout,
         ((DEPTH, BW, BW), (None, BW, BW), _const(l, 0, 0), "acc_all", bufs.get("s5_glu_w")), vout, vout])
    dxr, dxi, du_skip, do, dg_b, dhs, dgate_c, bufs["cmat"], dd, bufs["s5_glu_w"], dglub, dhgw = res
    small["s5_d"], small["s5_glu_b"], small["hg_norm_w"] = dd[0], dglub[0], dhgw[0]

    da, db = rg_scan_bwd(sv["a"], sv["hs"], dhs, L)
    wmat = lambda key: ((DEPTH, BW, BW), (None, BW, BW), _const(l, 0, 0), "acc_all", bufs.get(key))
    res = tile_bwd(
        gates_core, f"gates_bwd_{l}", n_i, 1,
        [(sv["xc"], (tm, BW), _rows(), "r"), (P["wa"], (None, BW, BW), _const(l, 0, 0), "w"), (P["ba"], vec, _const(l, 0, 0), "p"),
         (P["wx"], (None, BW, BW), _const(l, 0, 0), "w"), (P["bx"], vec, _const(l, 0, 0), "p"), (P["lam"], vec, _const(l, 0, 0), "p")],
        [(da, (tm, BW), _rows()), (db, (tm, BW), _rows())],
        [((L, BW), (tm, BW), _rows(), "write"), wmat("wa"), vout, wmat("wx"), vout, vout])
    dxc, bufs["wa"], dba, bufs["wx"], dbx, dlam = res
    small["rg_ba"], small["rg_bx"], small["rg_lambda"] = dba[0], dbx[0], dlam[0]
    dx_c, dconvw, dconvb = conv_bwd(proj, W["convw"][l], dxc, L)
    small["rg_conv_w"], small["rg_conv_b"] = dconvw, dconvb[0]

    dq_b, dz_b, dv_b, dlb = hg_bwd(sv["qzv"], P["lb"][l], sv["sst"], _to_time_order(do), L)
    dq_b, dz_b, dv_b = [_to_segment_order(a) for a in (dq_b, dz_b, dv_b)]

    gr, gi, dar, dai = s5_scan_bwd(dxr, dxi, sv["xr"], sv["xi"], P["ar"][l], P["ai"][l], L)
    du_pre, bufs["bmat"] = tile_bwd(
        s5_pre_core, f"s5pre_bwd_{l}", n_i, 1,
        [(proj, (tm, BW), _rows(0), "r"), (P["bmat"], (None, BW, 2 * S5_N), _const(l, 0, 0), "w")],
        [(gr, (nb, tm, LANE), blk3), (gi, (nb, tm, LANE), blk3)],
        [((L, BW), (tm, BW), _rows(), "write"),
         ((DEPTH, BW, 2 * S5_N), (None, BW, 2 * S5_N), _const(l, 0, 0), "acc_all", bufs.get("bmat"))])
    du_a = add_n(f"du_a_{l}", [(du_skip, ()), (du_pre, ())], (L, BW))
    prep_ct = dict(dar=dar, dai=dai, dlb=dlb)

    pieces = [du_a, dq_b, dz_b, dv_b, dg_b, dx_c, dgate_c, *dgm]
    per_piece, per_shard = BW // LANE, IN_TOTAL // NSH // LANE
    part, dnw1 = None, []
    for s in range(NSH):
        groups = [(pieces[g // per_piece], (tm, LANE), _rows(g % per_piece))
                  for g in range(s * per_shard, (s + 1) * per_shard)]
        part, dnw_s, bufs["w_in"] = tile_bwd(
            pre_core, f"pre_bwd_{l}{s}", n_i, 1,
            [(sv["x1"], (tm, D), _rows(), "r"), (W["nw"], (None, None, 1, D), _const(l, 1, 0, 0), "p"),
             (W["win"], (None, None, D, IN_TOTAL // NSH), _const(l, s, 0, 0), "w")],
            [groups],
            [((NSH, L, D), (None, tm, D), functools.partial(lambda _s, i, s: (s, i, 0), s=s), "write", part),
             ((1, D), (1, D), _const(0, 0), "acc_all"),
             ((DEPTH, NSH, D, IN_TOTAL // NSH), (None, None, D, IN_TOTAL // NSH), _const(l, s, 0, 0), "acc_all",
              bufs.get("w_in"))])
        dnw1.append(dnw_s)
    dnw1 = (dnw1[0] + dnw1[1]) + (dnw1[2] + dnw1[3])
    dx1 = _sum_parts(f"pre_bwd_{l}_dx", dx2, part, (L, D))

    dx0, dnw0 = ffn_bwd(f"ffn_bwd_{l}0", sv["x0"], sv["g0"], sv["u0"], dx1, W, bufs, l, 0, 0, L, tm)
    small["norm_w"] = jnp.concatenate([dnw0, dnw1, dnw2], axis=0)
    return dx0, small, prep_ct


SMALL_RAW = ("s5_lambda_re", "s5_lambda_im", "s5_log_dt", "s5_b_re", "s5_b_im", "s5_c_re", "s5_c_im", "s5_d", "s5_glu_b",
             "hg_lb_logits", "hg_norm_w", "rg_conv_b", "rg_wa", "rg_ba", "rg_wx", "rg_bx", "rg_lambda", "final_norm_w")
DEPTH = 2


def local_step(x, target, W, raw):
    L = x.shape[0]
    tm = min(256, L)
    col = lambda v: v.reshape(DEPTH, 1, BW)
    (ar, ai, bmat, cmat), s5_vjp = jax.vjp(jax.vmap(s5_prep), *[raw[k] for k in SMALL_RAW[:7]])
    (wa, wx), rg_vjp = jax.vjp(lambda a, b: (jax.vmap(rg_prep)(a), jax.vmap(rg_prep)(b)), raw["rg_wa"], raw["rg_wx"])
    lb, hg_vjp = jax.vjp(hg_prep, raw["hg_lb_logits"])
    P = dict(
        ar=[ar[l] for l in range(DEPTH)], ai=[ai[l] for l in range(DEPTH)],
        bmat=bmat.astype(MMT), cmat=cmat.astype(MMT), wa=wa.astype(MMT), wx=wx.astype(MMT),
        lb=[lb[l].reshape(1, BW) for l in range(DEPTH)], convb=[raw["rg_conv_b"][l].reshape(1, BW) for l in range(DEPTH)],
        ba=col(raw["rg_ba"]), bx=col(raw["rg_bx"]), lam=col(raw["rg_lambda"]), d=col(raw["s5_d"]),
        glub=col(raw["s5_glu_b"]), hgw=col(raw["hg_norm_w"]), hmat=_head_mean_matrix())

    saved = []
    h = _to_segment_order(x)
    for l in range(DEPTH):
        h, sv = layer_fwd(l, h, W, P, L, tm)
        saved.append(sv)
    loss, dh, dfw = loss_fwd_bwd(h, raw["final_norm_w"].reshape(1, D_MODEL), _to_segment_order(target), L, tm)

    big, per_layer, prep_cts = {}, [None] * DEPTH, [None] * DEPTH
    for l in reversed(range(DEPTH)):
        dh, sm, pc = layer_bwd(l, dh, saved[l], W, P, big, L, tm)
        per_layer[l], prep_cts[l] = sm, pc
    dh = _to_time_order(dh)

    small = {k: jnp.stack([per_layer[l][k] for l in range(DEPTH)]) for k in per_layer[0]}
    both = lambda k: jnp.stack([prep_cts[l][k] for l in range(DEPTH)])
    s5_g = s5_vjp((both("dar"), both("dai"), big.pop("bmat"), big.pop("cmat")))
    small.update(zip(SMALL_RAW[:7], s5_g))
    small["rg_wa"], small["rg_wx"] = rg_vjp((big.pop("wa"), big.pop("wx")))
    (small["hg_lb_logits"],) = hg_vjp(jnp.concatenate([prep_cts[l]["dlb"] for l in range(DEPTH)], axis=0))
    small["final_norm_w"] = dfw[0]
    return loss, dh, big, small


ANY = pl.BlockSpec(memory_space=pl.ANY)


def _place():
    x, y, c = lax.axis_index("x"), lax.axis_index("y"), lax.axis_index("c")
    chips = [(1 - x, y), (x, 1 - y), (1 - x, 1 - y)]
    return x, y, c, chips


def _remote(src, dst, send, recv, k, to):
    return pltpu.make_async_remote_copy(src_ref=src, dst_ref=dst, send_sem=send.at[k], recv_sem=recv.at[k],
                                        device_id=to, device_id_type=MESH)


def _comm_call(body, name, ins, out_shapes, n_sem, n_loc):
    return pl.pallas_call(
        body, name=name, in_specs=[ANY] * len(ins), out_specs=[ANY] * len(out_shapes), out_shape=out_shapes,
        scratch_shapes=[pltpu.SemaphoreType.DMA((n_sem,)), pltpu.SemaphoreType.DMA((n_sem,)),
                        pltpu.SemaphoreType.DMA((max(n_loc, 1),))],
    )(*ins)


def gather_shards(name, shards):
    n = len(shards)
    per = 8

    def body(*refs):
        ins, outs = refs[:n], refs[n:2 * n]
        send, recv, _ = refs[2 * n:]
        x, y, c, chips = _place()
        me = 2 * x + y
        sib = (x, y, 1 - c)
        sends = []
        for w in range(n):
            for j, (cx, cy) in enumerate(chips):
                cp = _remote(ins[w].at[c], outs[w].at[c, me], send, recv, per * w + j, (cx, cy, c))
                cp.start()
                sends.append(cp)
        for w in range(n):
            for l in range(2):
                cp = _remote(ins[w].at[l], outs[w].at[l, me], send, recv, per * w + 6 + l, sib)
                cp.start()
                sends.append(cp)
        for w in range(n):
            for j, (cx, cy) in enumerate(chips):
                theirs = outs[w].at[c, 2 * cx + cy]
                _remote(ins[w].at[c], theirs, send, recv, per * w + j, (cx, cy, c)).wait_recv()
                cp = _remote(theirs, theirs, send, recv, per * w + 3 + j, sib)
                cp.start()
                sends.append(cp)
        for w in range(n):
            for j, (cx, cy) in enumerate(chips):
                dst = outs[w].at[1 - c, 2 * cx + cy]
                _remote(dst, dst, send, recv, per * w + 3 + j, sib).wait_recv()
            for l in range(2):
                dst = outs[w].at[l, me]
                _remote(dst, dst, send, recv, per * w + 6 + l, sib).wait_recv()
        for cp in sends:
            cp.wait_send()

    shapes = [jax.ShapeDtypeStruct((2, NSH) + s.shape[1:], s.dtype) for s in shards]
    return _comm_call(body, name, shards, shapes, per * n, 0)


def exchange_halves(name, grads):
    n = len(grads)

    def body(*refs):
        ins, outs = refs[:n], refs[n:2 * n]
        send, recv, _ = refs[2 * n:]
        x, y, c, _chips = _place()
        cps = []
        for w in range(n):
            h = grads[w].shape[2] // 2
            cp = _remote(ins[w].at[:, :, pl.ds((1 - c) * h, h)], outs[w], send, recv, w, (x, y, 1 - c))
            cp.start()
            cps.append(cp)
        for cp in cps:
            cp.wait()

    shapes = [jax.ShapeDtypeStruct(g.shape[:2] + (g.shape[2] // 2, g.shape[3]), g.dtype) for g in grads]
    return _comm_call(body, name, grads, shapes, n, 0)


def scatter_to_chips(name, halves):
    n = len(halves)

    def body(*refs):
        ins, outs = refs[:n], refs[n:2 * n]
        send, recv, _ = refs[2 * n:]
        x, y, c, chips = _place()
        cps = []
        for w in range(n):
            for j, (cx, cy) in enumerate(chips):
                cp = _remote(ins[w].at[:, 2 * cx + cy], outs[w].at[j], send, recv, 3 * w + j, (cx, cy, c))
                cp.start()
                cps.append(cp)
        for cp in cps:
            cp.wait()

    shapes = [jax.ShapeDtypeStruct((3, h.shape[0]) + h.shape[2:], h.dtype) for h in halves]
    return _comm_call(body, name, halves, shapes, 3 * n, 0)


def share_halves(name, pieces):
    n = len(pieces)

    def body(*refs):
        ins, outs = refs[:n], refs[n:2 * n]
        send, recv, _ = refs[2 * n:]
        x, y, c, _chips = _place()
        cps = []
        for w in range(n):
            cp = _remote(ins[w], outs[w], send, recv, w, (x, y, 1 - c))
            cp.start()
            cps.append(cp)
        for cp in cps:
            cp.wait()

    return _comm_call(body, name, pieces, [jax.ShapeDtypeStruct(p.shape, p.dtype) for p in pieces], n, 0)


def add_own_half(name, g, ra, c, wire):
    nblk, h, cols = ra.shape
    tr = _row_tile(h, cols, mult=16)
    nt = h // tr

    def body(c_ref, g_ref, r_ref, o_ref):
        o_ref[...] = (g_ref[...] + r_ref[...]).astype(o_ref.dtype)

    blk = (None, tr, cols)
    return pl.pallas_call(
        body, name=name,
        grid_spec=pltpu.PrefetchScalarGridSpec(
            num_scalar_prefetch=1, grid=(nblk, nt),
            in_specs=[pl.BlockSpec(blk, lambda s, i, c_ref: (s, c_ref[0] * nt + i, 0)), pl.BlockSpec(blk, lambda s, i, c_ref: (s, i, 0))],
            out_specs=pl.BlockSpec(blk, lambda s, i, c_ref: (s, i, 0))),
        out_shape=jax.ShapeDtypeStruct(ra.shape, wire),
    )(c.reshape(1), g, ra)


def add_chips(name, hb, rb, me):
    npc, _, h, cols = hb.shape
    tr = _row_tile(h, cols, mult=16)

    def body(me_ref, h_ref, r0, r1, r2, o_ref):
        f = lambda r: r[...].astype(F32)
        o_ref[...] = ((f(h_ref) + f(r0)) + f(r1)) + f(r2)

    rspec = lambda j: pl.BlockSpec((None, None, tr, cols), functools.partial(lambda p, i, me_ref, j: (j, p, i, 0), j=j))
    return pl.pallas_call(
        body, name=name,
        grid_spec=pltpu.PrefetchScalarGridSpec(
            num_scalar_prefetch=1, grid=(npc, h // tr),
            in_specs=[pl.BlockSpec((None, None, tr, cols), lambda p, i, me_ref: (p, me_ref[0], i, 0)), rspec(0), rspec(1), rspec(2)],
            out_specs=pl.BlockSpec((None, tr, cols), lambda p, i, me_ref: (p, i, 0))),
        out_shape=jax.ShapeDtypeStruct((npc, h, cols), F32),
    )(me.reshape(1), hb, rb, rb, rb)


def adamw_halves(name, w, m, v, own, other, c):
    npc, rows, cols = w.shape
    h = rows // 2
    tr = _row_tile(h, cols, budget=1024 * 1024)
    nt = h // tr
    c1 = 1.0 - ADAM_B1 ** ADAM_STEP
    c2 = 1.0 - ADAM_B2 ** ADAM_STEP

    def body(c_ref, w_ref, m_ref, v_ref, own_ref, oth_ref, g_ref, d_ref, nm_ref, nv_ref):
        g = jnp.where(pl.program_id(1) == c_ref[0], own_ref[...], oth_ref[...])
        nm = ADAM_B1 * m_ref[...] + (1.0 - ADAM_B1) * g
        nv = ADAM_B2 * v_ref[...] + (1.0 - ADAM_B2) * jnp.square(g)
        g_ref[...] = g
        d_ref[...] = -ADAM_LR * ((nm / c1) / (jnp.sqrt(nv / c2) + ADAM_EPS) + ADAM_WD * w_ref[...])
        nm_ref[...] = nm
        nv_ref[...] = nv

    full = pl.BlockSpec((None, tr, cols), lambda p, hh, i, c_ref: (p, hh * nt + i, 0))
    half = pl.BlockSpec((None, tr, cols), lambda p, hh, i, c_ref: (p, i, 0))
    return pl.pallas_call(
        body, name=name,
        grid_spec=pltpu.PrefetchScalarGridSpec(
            num_scalar_prefetch=1, grid=(npc, 2, nt),
            in_specs=[full, full, full, half, half], out_specs=[full] * 4),
        out_shape=[jax.ShapeDtypeStruct(w.shape, F32)] * 4,
    )(c.reshape(1), w, m, v, own, other)


WEIGHTS = ("norm_w", "final_norm_w", "ffn_gate", "ffn_up", "ffn_down", "w_in", "branch_proj", "w_out", "s5_lambda_re",
           "s5_lambda_im", "s5_log_dt", "s5_b_re", "s5_b_im", "s5_c_re", "s5_c_im", "s5_d", "s5_glu_w", "s5_glu_b",
           "hg_lb_logits", "hg_norm_w", "rg_conv_w", "rg_conv_b", "rg_wa", "rg_ba", "rg_wx", "rg_bx", "rg_lambda")
BIG = ("ffn_gate", "ffn_up", "ffn_down", "w_in", "branch_proj", "w_out", "s5_glu_w")
SHARDED_SMALL = ("norm_w", "rg_conv_w")
SMALL = SMALL_RAW + SHARDED_SMALL


def _view2d(shape):
    return (1, shape[0]) if len(shape) == 1 else (math.prod(shape[:-1]), shape[-1])


def _small_layout(shapes, row_multiple):
    layout, at = [], 0
    for shape in shapes:
        r, c = _view2d(shape)
        rp = -(-r // 8) * 8
        layout.append((at, r, c, rp))
        at += rp * max(1, c // LANE)
    return layout, -(-at // row_multiple) * row_multiple


def pack_small(name, arrays, row_multiple):
    layout, rows = _small_layout([a.shape for a in arrays], row_multiple)

    def body(*refs):
        out = refs[-1]
        out[...] = jnp.zeros_like(out)
        for ref, (r0, r, c, rp) in zip(refs[:-1], layout):
            if c <= LANE:
                out[r0:r0 + r, 0:c] = ref[...]
            else:
                for q in range(c // LANE):
                    out[r0 + q * rp:r0 + q * rp + r, :] = ref[:, q * LANE:(q + 1) * LANE]

    return pl.pallas_call(
        body, name=name, out_shape=jax.ShapeDtypeStruct((rows, LANE), F32),
        compiler_params=pltpu.CompilerParams(vmem_limit_bytes=VMEM_LIMIT),
    )(*[a.reshape(_view2d(a.shape)) for a in arrays])


def unpack_small(name, packed, shapes):
    layout, _ = _small_layout(shapes, 8)

    def body(p_ref, *outs):
        for ref, (r0, r, c, rp) in zip(outs, layout):
            if c <= LANE:
                ref[...] = p_ref[r0:r0 + r, 0:c]
            else:
                for q in range(c // LANE):
                    ref[:, q * LANE:(q + 1) * LANE] = p_ref[r0 + q * rp:r0 + q * rp + r, :]

    res = pl.pallas_call(
        body, name=name, out_shape=[jax.ShapeDtypeStruct(_view2d(s), F32) for s in shapes],
        compiler_params=pltpu.CompilerParams(vmem_limit_bytes=VMEM_LIMIT),
    )(packed)
    return [a.reshape(s) for a, s in zip(res, shapes)]


def _step(x, target, w, m, v):
    mx, my, mc = lax.axis_index("x"), lax.axis_index("y"), lax.axis_index("c")
    me = (2 * mx + my).astype(jnp.int32)
    mc = mc.astype(jnp.int32)

    gathered = gather_shards("gather_weights", [w[n].astype(MMT) for n in BIG] + [w[n] for n in SHARDED_SMALL])
    W = dict(wg=gathered[0], wu=gathered[1], wd=gathered[2], win=gathered[3],
             pfull=gathered[4].transpose(0, 2, 3, 1, 4).reshape(DEPTH, 3, BW, D_MODEL),
             woutfull=gathered[5].reshape(DEPTH, D_MODEL, D_MODEL),
             gluw=gathered[6].reshape(DEPTH, BW, BW),
             nw=gathered[7].transpose(0, 2, 1, 3).reshape(DEPTH, 3, 1, D_MODEL),
             convw=gathered[8].transpose(0, 2, 1, 3).reshape(DEPTH, CONV_W, BW))
    loss, dx, big, small = local_step(x[0], target[0], W, {k: w[k] for k in SMALL_RAW})

    small_packed = pack_small("pack_small_grads", [small[n] for n in SMALL], NSH * 32)
    dq = D_MODEL // NSH
    big["branch_proj"] = big["branch_proj"].reshape(DEPTH, 3, BW, NSH, dq).transpose(0, 3, 1, 2, 4).reshape(DEPTH, NSH, 3 * BW, dq)
    grads = [big[n].reshape(big[n].shape[0], NSH, -1, big[n].shape[-1]) for n in BIG] + [small_packed.reshape(1, NSH, -1, LANE)]
    from_sibling = exchange_halves("reduce_cores", grads)
    merge = lambda a: a.reshape((-1,) + a.shape[2:])
    wire = [jnp.bfloat16] * len(BIG) + [F32]
    halves = [add_own_half(f"sum_cores_{i}", merge(g), merge(r), mc, wire[i]).reshape(r.shape)
              for i, (g, r) in enumerate(zip(grads, from_sibling))]
    from_chips = scatter_to_chips("reduce_chips", halves)
    own = [add_chips(f"sum_chips_{i}", h, r, me) for i, (h, r) in enumerate(zip(halves, from_chips))]
    other = share_halves("reduce_share", own)

    g, delta, new_m, new_v = {}, {}, {}, {}
    for i, n in enumerate(BIG):
        view = lambda a: a.reshape(own[i].shape[0], -1, own[i].shape[2])
        res = adamw_halves(f"adamw_{n}", view(w[n]), view(m[n]), view(v[n]), own[i], other[i], mc)
        g[n], delta[n], new_m[n], new_v[n] = [a.reshape(w[n].shape) for a in res]

    piece = jnp.stack([jnp.where(mc == 0, own[-1][0], other[-1][0]), jnp.where(mc == 0, other[-1][0], own[-1][0])])
    (all_small,) = gather_shards("gather_small", [piece])
    full_small = unpack_small("unpack_small_grads", all_small.transpose(1, 0, 2, 3).reshape(-1, LANE),
                              [small[n].shape for n in SMALL])
    g.update(zip(SMALL, full_small))
    g["norm_w"] = lax.dynamic_slice_in_dim(g["norm_w"], me * (D_MODEL // NSH), D_MODEL // NSH, axis=2)
    g["rg_conv_w"] = lax.dynamic_slice_in_dim(g["rg_conv_w"], me * (BW // NSH), BW // NSH, axis=2)

    packed = [pack_small(f"pack_small_{tag}", [src[n] for n in SMALL], 8)
              for tag, src in (("w", w), ("g", g), ("m", m), ("v", v))]
    for tag, dst, flat in zip(("delta", "m", "v"), (delta, new_m, new_v), adamw(*packed)):
        dst.update(zip(SMALL, unpack_small(f"unpack_small_{tag}", flat, [w[n].shape for n in SMALL])))

    total = lax.psum(loss[0, 0], ("x", "y", "c"))
    return (total, dx[None], *[g[n] for n in WEIGHTS], *[delta[n] for n in WEIGHTS],
            *[new_m[n] for n in WEIGHTS], *[new_v[n] for n in WEIGHTS])


def kernel(x, norm_w, final_norm_w, ffn_gate, ffn_up, ffn_down, w_in, branch_proj, w_out, s5_lambda_re, s5_lambda_im, s5_log_dt, s5_b_re, s5_b_im, s5_c_re, s5_c_im, s5_d, s5_glu_w, s5_glu_b, hg_lb_logits, hg_norm_w, rg_conv_w, rg_conv_b, rg_wa, rg_ba, rg_wx, rg_bx, rg_lambda, loss_target, m_norm_w, m_final_norm_w, m_ffn_gate, m_ffn_up, m_ffn_down, m_w_in, m_branch_proj, m_w_out, m_s5_lambda_re, m_s5_lambda_im, m_s5_log_dt, m_s5_b_re, m_s5_b_im, m_s5_c_re, m_s5_c_im, m_s5_d, m_s5_glu_w, m_s5_glu_b, m_hg_lb_logits, m_hg_norm_w, m_rg_conv_w, m_rg_conv_b, m_rg_wa, m_rg_ba, m_rg_wx, m_rg_bx, m_rg_lambda, v_norm_w, v_final_norm_w, v_ffn_gate, v_ffn_up, v_ffn_down, v_w_in, v_branch_proj, v_w_out, v_s5_lambda_re, v_s5_lambda_im, v_s5_log_dt, v_s5_b_re, v_s5_b_im, v_s5_c_re, v_s5_c_im, v_s5_d, v_s5_glu_w, v_s5_glu_b, v_hg_lb_logits, v_hg_norm_w, v_rg_conv_w, v_rg_conv_b, v_rg_wa, v_rg_ba, v_rg_wx, v_rg_bx, v_rg_lambda):
    ws = (norm_w, final_norm_w, ffn_gate, ffn_up, ffn_down, w_in, branch_proj, w_out, s5_lambda_re, s5_lambda_im, s5_log_dt, s5_b_re, s5_b_im, s5_c_re, s5_c_im, s5_d, s5_glu_w, s5_glu_b, hg_lb_logits, hg_norm_w, rg_conv_w, rg_conv_b, rg_wa, rg_ba, rg_wx, rg_bx, rg_lambda)
    ms = (m_norm_w, m_final_norm_w, m_ffn_gate, m_ffn_up, m_ffn_down, m_w_in, m_branch_proj, m_w_out, m_s5_lambda_re, m_s5_lambda_im, m_s5_log_dt, m_s5_b_re, m_s5_b_im, m_s5_c_re, m_s5_c_im, m_s5_d, m_s5_glu_w, m_s5_glu_b, m_hg_lb_logits, m_hg_norm_w, m_rg_conv_w, m_rg_conv_b, m_rg_wa, m_rg_ba, m_rg_wx, m_rg_bx, m_rg_lambda)
    vs = (v_norm_w, v_final_norm_w, v_ffn_gate, v_ffn_up, v_ffn_down, v_w_in, v_branch_proj, v_w_out, v_s5_lambda_re, v_s5_lambda_im, v_s5_log_dt, v_s5_b_re, v_s5_b_im, v_s5_c_re, v_s5_c_im, v_s5_d, v_s5_glu_w, v_s5_glu_b, v_hg_lb_logits, v_hg_norm_w, v_rg_conv_w, v_rg_conv_b, v_rg_wa, v_rg_ba, v_rg_wx, v_rg_bx, v_rg_lambda)
    return _step(x, loss_target, dict(zip(WEIGHTS, ws)), dict(zip(WEIGHTS, ms)), dict(zip(WEIGHTS, vs)))
```

```python
import functools
import math
from typing import NamedTuple

import jax
import jax.numpy as jnp
from jax import lax
from jax.experimental import pallas as pl
from jax.experimental.pallas import tpu as pltpu

F32 = jnp.float32
MMT = jnp.bfloat16
HI = lax.Precision.HIGHEST

D_MODEL = 1024
BW = 512
S5_GROUP, S5_GROUPS, S5_STATE = 16, 32, 64
S5_N = S5_GROUPS * S5_STATE
HG_HEADS, HG_D = 4, 128
HG_CHUNK = 128
RG_BLOCKS, RG_BLOCK = 8, 64
RG_C = 8.0
CONV_W = 4
D_FF = 2816
EPS = 1e-6
IN_TOTAL = 6656
NSH = 4
NSEG = 8
LANE = 128
VMEM_LIMIT = 56 * 1024 * 1024

ADAM_LR, ADAM_B1, ADAM_B2, ADAM_EPS, ADAM_WD, ADAM_STEP = 0.001, 0.9, 0.999, 1e-08, 0.01, 10

MESH = pl.DeviceIdType.MESH


class WP(NamedTuple):
    w: jax.Array
    p: jax.Array


def _dg(a, b, ca, cb):
    return lax.dot_general(a, b, (((ca,), (cb,)), ((), ())), preferred_element_type=F32)


@jax.custom_vjp
def _mmw(a, w, p):
    return _dg(a.astype(MMT), w, 1, 0)


def _mmw_fwd(a, w, p):
    return _mmw(a, w, p), (a, w)


def _mmw_bwd(res, g):
    a, w = res
    gb = g.astype(MMT)
    return _dg(gb, w, 1, 1), jnp.zeros_like(w), _dg(a.astype(MMT), gb, 0, 0)


_mmw.defvjp(_mmw_fwd, _mmw_bwd)


def mm(a, w):
    if isinstance(w, WP):
        return _mmw(a, w.w, w.p)
    return _dg(a.astype(MMT), w, 1, 0)


@jax.custom_vjp
def mma_nn(a, b):
    return _dg(a.astype(MMT), b.astype(MMT), 1, 0)


def _nn_f(a, b):
    return mma_nn(a, b), (a, b)


def _nn_b(res, g):
    a, b = res
    gb = g.astype(MMT)
    return _dg(gb, b.astype(MMT), 1, 1), _dg(a.astype(MMT), gb, 0, 0)


mma_nn.defvjp(_nn_f, _nn_b)


@jax.custom_vjp
def mma_nt(a, b):
    return _dg(a.astype(MMT), b.astype(MMT), 1, 1)


def _nt_f(a, b):
    return mma_nt(a, b), (a, b)


def _nt_b(res, g):
    a, b = res
    gb = g.astype(MMT)
    return _dg(gb, b.astype(MMT), 1, 0), _dg(gb, a.astype(MMT), 0, 0)


mma_nt.defvjp(_nt_f, _nt_b)


@jax.custom_vjp
def mma_tn(a, b):
    return _dg(a.astype(MMT), b.astype(MMT), 0, 0)


def _tn_f(a, b):
    return mma_tn(a, b), (a, b)


def _tn_b(res, g):
    a, b = res
    gb = g.astype(MMT)
    return _dg(b.astype(MMT), gb, 1, 1), _dg(a.astype(MMT), gb, 1, 0)


mma_tn.defvjp(_tn_f, _tn_b)


def mm_exact(m, x):
    return jnp.dot(m, x, precision=HI, preferred_element_type=F32)


def _rms(x, w):
    return x * lax.rsqrt(jnp.mean(x * x, axis=-1, keepdims=True) + EPS) * w


def _expm1(x):
    series = x * (1.0 + x * (1.0 / 2) * (1.0 + x * (1.0 / 3) * (1.0 + x * (1.0 / 4) * (1.0 + x * (1.0 / 5) * (1.0 + x * (1.0 / 6))))))
    return jnp.where(jnp.abs(x) < 0.1, series, jnp.exp(x) - 1.0)


def _bspec(block, fn, order):
    if order == "is":
        return pl.BlockSpec(block, lambda i, s: fn(s, i))
    return pl.BlockSpec(block, lambda s, i: fn(s, i))


def tile_fwd(fn, name, n_i, n_s, ins, outs, s_outer=False):
    n_in = len(ins)
    order = "si" if s_outer else "is"
    assert not (s_outer and any(o[4] for o in outs))

    def body(*refs):
        s = pl.program_id(0 if s_outer else 1)
        res = fn(*[r[...] for r in refs[:n_in]], s)
        for o_ref, val, spec in zip(refs[n_in:], res, outs):
            if spec[4] and n_s > 1:
                @pl.when(s == 0)
                def _(o_ref=o_ref, val=val):
                    o_ref[...] = val.astype(o_ref.dtype)

                @pl.when(s != 0)
                def _(o_ref=o_ref, val=val):
                    o_ref[...] += val.astype(o_ref.dtype)
            else:
                o_ref[...] = val.astype(o_ref.dtype)

    return pl.pallas_call(
        body, grid=(n_s, n_i) if s_outer else (n_i, n_s), name=name,
        in_specs=[_bspec(b, f, order) for _, b, f in ins],
        out_specs=[_bspec(b, f, order) for _, _, b, f, _ in outs],
        out_shape=[jax.ShapeDtypeStruct(sh, dt) for sh, dt, _, _, _ in outs],
        compiler_params=pltpu.CompilerParams(vmem_limit_bytes=VMEM_LIMIT,
                                             dimension_semantics=("arbitrary", "arbitrary")),
    )(*[a for a, _, _ in ins])


def tile_bwd(fn, name, n_i, n_s, ins, cts, gouts):
    groups = [c if isinstance(c, list) else [c] for c in cts]
    cts = [blk for grp in groups for blk in grp]
    n_in, n_ct = len(ins), len(cts)
    kinds = [k for _, _, _, k in ins]
    d_pos = [j for j, k in enumerate(kinds) if k != "c"]
    shared = [(gi, spec[4]) for gi, spec in enumerate(gouts) if len(spec) == 5 and spec[4] is not None]
    n_sh = len(shared)

    def body(*refs):
        s, i = pl.program_id(0), pl.program_id(1)
        vals = [r[...] for r in refs[:n_in]]
        ct_refs, ctv = list(refs[n_in:n_in + n_ct]), []
        for grp in groups:
            parts = [ct_refs.pop(0)[...] for _ in grp]
            ctv.append(parts[0] if len(parts) == 1 else jnp.concatenate(parts, axis=1))
        ctv = tuple(ctv)
        g_refs = refs[n_in + n_ct + n_sh:]

        def g(*dv):
            args = list(vals)
            for j, v in zip(d_pos, dv):
                args[j] = WP(vals[j], v) if kinds[j] == "w" else v
            return tuple(fn(*args))

        dv0 = [jnp.zeros(vals[j].shape, F32) if kinds[j] == "w" else vals[j] for j in d_pos]
        _, vjp = jax.vjp(g, *dv0)
        grads = vjp(ctv)
        for g_ref, gv, spec in zip(g_refs, grads, gouts):
            mode = spec[3]
            if mode == "write":
                g_ref[...] = gv.astype(g_ref.dtype)
            else:
                first = (i == 0) if mode == "acc_i" else jnp.logical_and(i == 0, s == 0)

                @pl.when(first)
                def _(g_ref=g_ref, gv=gv):
                    g_ref[...] = gv.astype(g_ref.dtype)

                @pl.when(jnp.logical_not(first))
                def _(g_ref=g_ref, gv=gv):
                    g_ref[...] += gv.astype(g_ref.dtype)

    return pl.pallas_call(
        body, grid=(n_s, n_i), name=name,
        in_specs=([_bspec(b, f, "si") for _, b, f, _ in ins] + [_bspec(b, f, "si") for _, b, f in cts]
                  + [pl.BlockSpec(memory_space=pl.ANY)] * n_sh),
        out_specs=[_bspec(spec[1], spec[2], "si") for spec in gouts],
        out_shape=[jax.ShapeDtypeStruct(spec[0], F32) for spec in gouts],
        input_output_aliases={n_in + n_ct + k: gi for k, (gi, _) in enumerate(shared)},
        compiler_params=pltpu.CompilerParams(vmem_limit_bytes=VMEM_LIMIT,
                                             dimension_semantics=("arbitrary", "arbitrary")),
    )(*[a for a, _, _, _ in ins], *[a for a, _, _ in cts], *[buf for _, buf in shared])


def _row_tile(rows, width, itemsize=4, budget=2 * 1024 * 1024, mult=8):
    best = mult
    for t in range(mult, rows + 1, mult):
        if rows % t == 0 and t * width * itemsize <= budget:
            best = t
    return best


def add_n(name, terms, shape):
    rows, cols = shape
    tr = _row_tile(rows, cols)

    def body(*refs):
        acc = refs[0][...]
        for r in refs[1:-1]:
            acc = acc + r[...]
        refs[-1][...] = acc

    specs = []
    for _, lead in terms:
        specs.append(pl.BlockSpec((None,) * len(lead) + (tr, cols), functools.partial(lambda i, lead: (*lead, i, 0), lead=lead)))
    return pl.pallas_call(
        body, grid=(rows // tr,), name=name, in_specs=specs,
        out_specs=pl.BlockSpec((tr, cols), lambda i: (i, 0)),
        out_shape=jax.ShapeDtypeStruct((rows, cols), F32),
    )(*[a for a, _ in terms])


def ffn_core(x, nw, wg, wu, wd):
    h = _rms(x, nw)
    return (0.5 * mm(jax.nn.silu(mm(h, wg)) * mm(h, wu), wd),)


def pre_core(x, nw, win):
    return (mm(_rms(x, nw), win),)


def _split_lanes(y):
    return jnp.stack([y[:, k * LANE:(k + 1) * LANE] for k in range(y.shape[1] // LANE)], axis=0)


def _join_lanes(y3):
    return jnp.concatenate([y3[k] for k in range(y3.shape[0])], axis=1)


def s5_pre_core(u, bmat):
    bu = mm(u, bmat)
    return _split_lanes(bu[:, :S5_N]), _split_lanes(bu[:, S5_N:])


def mid_core(xr, xi, u, o, g, hs, gc, hmat, cmat, d, gluw, glub, hgw):
    xs = jnp.concatenate([_join_lanes(xr), _join_lanes(xi)], axis=1)
    y = mm(xs, cmat) + d * u
    z = jax.nn.gelu(y)
    ya = z * jax.nn.sigmoid(mm(z, gluw) + glub)
    ms = mm_exact(o * o, hmat)
    yb = o * lax.rsqrt(ms + EPS) * hgw * jax.nn.silu(g)
    yc = hs * jax.nn.gelu(gc)
    return ya, yb, yc


def _sub(w, n):
    return WP(w.w[n], w.p[n]) if isinstance(w, WP) else w[n]


def merge_core(ya, yb, yc, g0, g1, g2, g3, g4, g5, p, wout):
    gate = lambda a, b: jax.nn.sigmoid(jnp.concatenate([a, b], axis=1))
    m = gate(g0, g1) * mm(ya, _sub(p, 0)) + gate(g2, g3) * mm(yb, _sub(p, 1)) + gate(g4, g5) * mm(yc, _sub(p, 2))
    return (mm(m, wout),)


def gates_core(xc, wa, ba, wx, bx, lam):
    r = jax.nn.sigmoid(mm(xc, wa) + ba)
    i = jax.nn.sigmoid(mm(xc, wx) + bx)
    log_a = -RG_C * jax.nn.softplus(-lam) * r
    a = jnp.exp(log_a)
    b = jnp.sqrt(-_expm1(2.0 * log_a)) * (i * xc)
    return a, b


def _seg_rows(ref, k, j, n):
    rows = pl.ds(pl.multiple_of(j * NSEG, NSEG), NSEG)
    if k is None:
        return ref[rows, :]
    return ref[k, rows, :]


def _seg_store(ref, k, j, n, val):
    rows = pl.ds(pl.multiple_of(j * NSEG, NSEG), NSEG)
    if k is None:
        ref[rows, :] = val
    else:
        ref[k, rows, :] = val


def _seg_carries(er, ei, pr, pi, reverse):
    rows = lax.broadcasted_iota(jnp.int32, er.shape, 0)
    cr = jnp.zeros_like(er)
    ci = None if ei is None else jnp.zeros_like(er)
    order = range(NSEG - 2, -1, -1) if reverse else range(1, NSEG)
    shift = NSEG - 1 if reverse else 1
    for s in order:
        if ei is None:
            tr = er + pr * cr
            cr = jnp.where(rows == s, pltpu.roll(tr, shift, 0), cr)
        else:
            tr = er + pr * cr - pi * ci
            ti = ei + pr * ci + pi * cr
            cr = jnp.where(rows == s, pltpu.roll(tr, shift, 0), cr)
            ci = jnp.where(rows == s, pltpu.roll(ti, shift, 0), ci)
    return cr, ci


S5_K = 2


def s5_scan_fwd(bur, bui, ar, ai, L):
    n = L // NSEG
    nb = S5_N // LANE
    K = S5_K

    def body(br_ref, bi_ref, ar_ref, ai_ref, xr_ref, xi_ref):
        zero = jnp.zeros((NSEG, LANE), F32)
        A = [(jnp.broadcast_to(ar_ref[k], (NSEG, LANE)), jnp.broadcast_to(ai_ref[k], (NSEG, LANE))) for k in range(K)]

        def p1(j, st):
            new = []
            for k in range(K):
                sr, si, pr, pi = st[k]
                a_r, a_i = A[k]
                nr = a_r * sr - a_i * si + _seg_rows(br_ref, k, j, n)
                ni = a_r * si + a_i * sr + _seg_rows(bi_ref, k, j, n)
                _seg_store(xr_ref, k, j, n, nr)
                _seg_store(xi_ref, k, j, n, ni)
                new.append((nr, ni, a_r * pr - a_i * pi, a_r * pi + a_i * pr))
            return tuple(new)

        st = lax.fori_loop(0, n, p1, tuple((zero, zero, zero + 1.0, zero) for _ in range(K)))
        C = [_seg_carries(st[k][0], st[k][1], st[k][2], st[k][3], False) for k in range(K)]

        def p2(j, st):
            new = []
            for k in range(K):
                pr, pi = st[k]
                a_r, a_i = A[k]
                pr, pi = a_r * pr - a_i * pi, a_r * pi + a_i * pr
                cr, ci = C[k]
                _seg_store(xr_ref, k, j, n, _seg_rows(xr_ref, k, j, n) + pr * cr - pi * ci)
                _seg_store(xi_ref, k, j, n, _seg_rows(xi_ref, k, j, n) + pr * ci + pi * cr)
                new.append((pr, pi))
            return tuple(new)

        lax.fori_loop(0, n, p2, tuple((zero + 1.0, zero) for _ in range(K)))

    blk = pl.BlockSpec((K, L, LANE), lambda g: (g, 0, 0))
    ablk = pl.BlockSpec((K, 1, LANE), lambda g: (g, 0, 0))
    return pl.pallas_call(
        body, grid=(nb // K,), name="s5_scan_fwd",
        in_specs=[blk, blk, ablk, ablk], out_specs=[blk, blk],
        out_shape=[jax.ShapeDtypeStruct((nb, L, LANE), F32)] * 2,
        compiler_params=pltpu.CompilerParams(vmem_limit_bytes=VMEM_LIMIT),
    )(bur, bui, ar, ai)


def s5_scan_bwd(dxr, dxi, xr, xi, ar, ai, L):
    n = L // NSEG
    nb = S5_N // LANE
    K = S5_K

    def body(dr_ref, di_ref, xr_ref, xi_ref, ar_ref, ai_ref, gr_ref, gi_ref, dar_ref, dai_ref):
        zero = jnp.zeros((NSEG, LANE), F32)
        rows = lax.broadcasted_iota(jnp.int32, (NSEG, LANE), 0)
        A = [(jnp.broadcast_to(ar_ref[k], (NSEG, LANE)), -jnp.broadcast_to(ai_ref[k], (NSEG, LANE))) for k in range(K)]

        def p1(jj, st):
            j = n - 1 - jj
            new = []
            for k in range(K):
                sr, si, pr, pi = st[k]
                a_r, a_i = A[k]
                nr = a_r * sr - a_i * si + _seg_rows(dr_ref, k, j, n)
                ni = a_r * si + a_i * sr + _seg_rows(di_ref, k, j, n)
                _seg_store(gr_ref, k, j, n, nr)
                _seg_store(gi_ref, k, j, n, ni)
                new.append((nr, ni, a_r * pr - a_i * pi, a_r * pi + a_i * pr))
            return tuple(new)

        st = lax.fori_loop(0, n, p1, tuple((zero, zero, zero + 1.0, zero) for _ in range(K)))
        C = [_seg_carries(st[k][0], st[k][1], st[k][2], st[k][3], True) for k in range(K)]
        xb = [(jnp.where(rows == 0, 0.0, pltpu.roll(_seg_rows(xr_ref, k, n - 1, n), 1, 0)),
               jnp.where(rows == 0, 0.0, pltpu.roll(_seg_rows(xi_ref, k, n - 1, n), 1, 0))) for k in range(K)]

        def p2(jj, st):
            j = n - 1 - jj
            jp = jnp.maximum(j - 1, 0)
            new = []
            for k in range(K):
                pr, pi, acr, aci = st[k]
                a_r, a_i = A[k]
                pr, pi = a_r * pr - a_i * pi, a_r * pi + a_i * pr
                cr, ci = C[k]
                g_r = _seg_rows(gr_ref, k, j, n) + pr * cr - pi * ci
                g_i = _seg_rows(gi_ref, k, j, n) + pr * ci + pi * cr
                _seg_store(gr_ref, k, j, n, g_r)
                _seg_store(gi_ref, k, j, n, g_i)
                xpr = jnp.where(j == 0, xb[k][0], _seg_rows(xr_ref, k, jp, n))
                xpi = jnp.where(j == 0, xb[k][1], _seg_rows(xi_ref, k, jp, n))
                new.append((pr, pi, acr + g_r * xpr + g_i * xpi, aci + g_i * xpr - g_r * xpi))
            return tuple(new)

        st = lax.fori_loop(0, n, p2, tuple((zero + 1.0, zero, zero, zero) for _ in range(K)))
        for k in range(K):
            dar_ref[k] = jnp.sum(st[k][2], axis=0, keepdims=True)
            dai_ref[k] = jnp.sum(st[k][3], axis=0, keepdims=True)

    blk = pl.BlockSpec((K, L, LANE), lambda g: (g, 0, 0))
    ablk = pl.BlockSpec((K, 1, LANE), lambda g: (g, 0, 0))
    return pl.pallas_call(
        body, grid=(nb // K,), name="s5_scan_bwd",
        in_specs=[blk, blk, blk, blk, ablk, ablk], out_specs=[blk, blk, ablk, ablk],
        out_shape=[jax.ShapeDtypeStruct((nb, L, LANE), F32)] * 2 + [jax.ShapeDtypeStruct((nb, 1, LANE), F32)] * 2,
        compiler_params=pltpu.CompilerParams(vmem_limit_bytes=VMEM_LIMIT),
    )(dxr, dxi, xr, xi, ar, ai)


def rg_scan_fwd(a, b, L):
    n = L // NSEG

    def body(a_ref, b_ref, h_ref):
        zero = jnp.zeros((NSEG, LANE), F32)

        def p1(j, st):
            h, p = st
            aj = _seg_rows(a_ref, None, j, n)
            h = aj * h + _seg_rows(b_ref, None, j, n)
            _seg_store(h_ref, None, j, n, h)
            return h, aj * p

        e, pe = lax.fori_loop(0, n, p1, (zero, zero + 1.0))
        c, _ = _seg_carries(e, None, pe, None, False)

        def p2(j, p):
            p = _seg_rows(a_ref, None, j, n) * p
            _seg_store(h_ref, None, j, n, _seg_rows(h_ref, None, j, n) + p * c)
            return p

        lax.fori_loop(0, n, p2, zero + 1.0)

    blk = pl.BlockSpec((L, LANE), lambda g: (0, g))
    return pl.pallas_call(
        body, grid=(BW // LANE,), name="rg_scan_fwd", in_specs=[blk, blk], out_specs=blk,
        out_shape=jax.ShapeDtypeStruct((L, BW), F32),
        compiler_params=pltpu.CompilerParams(vmem_limit_bytes=VMEM_LIMIT),
    )(a, b)


def rg_scan_bwd(a, h, dh, L):
    n = L // NSEG

    def body(a_ref, h_ref, dh_ref, da_ref, db_ref):
        zero = jnp.zeros((NSEG, LANE), F32)
        rows = lax.broadcasted_iota(jnp.int32, (NSEG, LANE), 0)
        a_edge = jnp.where(rows == NSEG - 1, 0.0, pltpu.roll(_seg_rows(a_ref, None, 0, n), NSEG - 1, 0))
        h_edge = jnp.where(rows == 0, 0.0, pltpu.roll(_seg_rows(h_ref, None, n - 1, n), 1, 0))

        def mult(j):
            return jnp.where(j == n - 1, a_edge, _seg_rows(a_ref, None, jnp.minimum(j + 1, n - 1), n))

        def p1(jj, st):
            j = n - 1 - jj
            g, p = st
            m = mult(j)
            g = m * g + _seg_rows(dh_ref, None, j, n)
            _seg_store(db_ref, None, j, n, g)
            return g, m * p

        e, pe = lax.fori_loop(0, n, p1, (zero, zero + 1.0))
        c, _ = _seg_carries(e, None, pe, None, True)

        def p2(jj, p):
            j = n - 1 - jj
            p = mult(j) * p
            g = _seg_rows(db_ref, None, j, n) + p * c
            _seg_store(db_ref, None, j, n, g)
            hp = jnp.where(j == 0, h_edge, _seg_rows(h_ref, None, jnp.maximum(j - 1, 0), n))
            _seg_store(da_ref, None, j, n, g * hp)
            return p

        lax.fori_loop(0, n, p2, zero + 1.0)

    blk = pl.BlockSpec((L, LANE), lambda g: (0, g))
    return pl.pallas_call(
        body, grid=(BW // LANE,), name="rg_scan_bwd", in_specs=[blk, blk, blk], out_specs=[blk, blk],
        out_shape=[jax.ShapeDtypeStruct((L, BW), F32)] * 2,
        compiler_params=pltpu.CompilerParams(vmem_limit_bytes=VMEM_LIMIT),
    )(a, h, dh)


def _hg_consts(C):
    t = lax.broadcasted_iota(jnp.int32, (C, C), 0)
    s = lax.broadcasted_iota(jnp.int32, (C, C), 1)
    tril = (s <= t).astype(F32)
    diag = (s == t).astype(F32)
    levels = []
    k = 1
    while (1 << k) <= C:
        m = 1 << (k - 1)
        same = (t >> k) == (s >> k)
        t_right = ((t >> (k - 1)) & 1) == 1
        s_left = ((s >> (k - 1)) & 1) == 0
        mask = jnp.logical_and(same, jnp.logical_and(t_right, s_left)).astype(F32)
        bnd = ((t >> k) << k) + (m - 1)
        levels.append((mask, (s <= bnd).astype(F32)))
        k += 1
    return tril, diag, levels


def hg_chunk(st, q, z, v, lb):
    C = q.shape[0]
    tril, diag, levels = _hg_consts(C)
    sig = jax.nn.sigmoid(z)
    lf = jnp.log(lb + (1.0 - lb) * sig)
    k = (1.0 - lb) * jax.nn.sigmoid(-z)
    qh = jax.nn.silu(q)
    b = mm_exact(tril, lf)
    blast = jnp.sum(lf, axis=0, keepdims=True)
    qe = qh * jnp.exp(b)
    kd = k * jnp.exp(blast - b)
    scaled = []
    for _, sel in levels:
        ref = mm_exact(sel, lf)
        scaled.append((qh * jnp.exp(jnp.minimum(b - ref, 0.0)), k * jnp.exp(jnp.minimum(ref - b, 0.0))))
    outs, news = [], []
    for h in range(HG_HEADS):
        sl = slice(h * HG_D, (h + 1) * HG_D)
        st_h = st[h * HG_D:(h + 1) * HG_D, :]
        sc = diag * mma_nt(qh[:, sl], k[:, sl])
        for (mask, _), (qt, kt) in zip(levels, scaled):
            sc = sc + mask * mma_nt(qt[:, sl], kt[:, sl])
        outs.append(mma_nt(qe[:, sl], st_h) + mma_nn(sc, v[:, sl]))
        news.append(st_h * jnp.exp(blast[:, sl]) + mma_tn(v[:, sl], kd[:, sl]))
    return jnp.concatenate(news, axis=0), jnp.concatenate(outs, axis=1)


def hg_fwd(qzv, lb, L):
    C = HG_CHUNK
    nc = L // C

    def body(q_ref, z_ref, v_ref, lb_ref, o_ref, sst_ref, st_ref):
        @pl.when(pl.program_id(0) == 0)
        def _():
            st_ref[...] = jnp.zeros_like(st_ref)

        st = st_ref[...]
        sst_ref[...] = st
        new, o = hg_chunk(st, q_ref[...], z_ref[...], v_ref[...], lb_ref[...])
        st_ref[...] = new
        o_ref[...] = o

    col = lambda cb: pl.BlockSpec((C, BW), functools.partial(lambda c, cb: (c, cb), cb=cb))
    return pl.pallas_call(
        body, grid=(nc,), name="hg_fwd",
        in_specs=[col(0), col(1), col(2), pl.BlockSpec((1, BW), lambda c: (0, 0))],
        out_specs=[pl.BlockSpec((C, BW), lambda c: (c, 0)), pl.BlockSpec((None, BW, HG_D), lambda c: (c, 0, 0))],
        out_shape=[jax.ShapeDtypeStruct((L, BW), F32), jax.ShapeDtypeStruct((nc, BW, HG_D), F32)],
        scratch_shapes=[pltpu.VMEM((BW, HG_D), F32)],
        compiler_params=pltpu.CompilerParams(vmem_limit_bytes=VMEM_LIMIT, dimension_semantics=("arbitrary",)),
    )(qzv, qzv, qzv, lb)


def hg_bwd(qzv, lb, sst, do, L):
    C = HG_CHUNK
    nc = L // C

    def body(q_ref, z_ref, v_ref, lb_ref, sst_ref, do_ref, dq_ref, dz_ref, dv_ref, dlb_ref, dst_ref):
        @pl.when(pl.program_id(0) == 0)
        def _():
            dst_ref[...] = jnp.zeros_like(dst_ref)
            dlb_ref[...] = jnp.zeros_like(dlb_ref)

        _, vjp = jax.vjp(hg_chunk, sst_ref[...], q_ref[...], z_ref[...], v_ref[...], lb_ref[...])
        dst, dq, dz, dv, dlb = vjp((dst_ref[...], do_ref[...]))
        dst_ref[...] = dst
        dq_ref[...] = dq
        dz_ref[...] = dz
        dv_ref[...] = dv
        dlb_ref[...] += dlb

    col = lambda cb: pl.BlockSpec((C, BW), functools.partial(lambda c, cb: (nc - 1 - c, cb), cb=cb))
    rev = pl.BlockSpec((C, BW), lambda c: (nc - 1 - c, 0))
    return pl.pallas_call(
        body, grid=(nc,), name="hg_bwd",
        in_specs=[col(0), col(1), col(2), pl.BlockSpec((1, BW), lambda c: (0, 0)),
                  pl.BlockSpec((None, BW, HG_D), lambda c: (nc - 1 - c, 0, 0)), rev],
        out_specs=[rev, rev, rev, pl.BlockSpec((1, BW), lambda c: (0, 0))],
        out_shape=[jax.ShapeDtypeStruct((L, BW), F32)] * 3 + [jax.ShapeDtypeStruct((1, BW), F32)],
        scratch_shapes=[pltpu.VMEM((BW, HG_D), F32)],
        compiler_params=pltpu.CompilerParams(vmem_limit_bytes=VMEM_LIMIT, dimension_semantics=("arbitrary",)),
    )(qzv, qzv, qzv, lb, sst, do)


def _shift_down(x, d, rows, L):
    if d == 0:
        return x
    wrapped = jnp.where((rows & (NSEG - 1)) == 0, 0.0, pltpu.roll(x, NSEG * d + 1, 0))
    return jnp.where(rows < NSEG * d, wrapped, pltpu.roll(x, NSEG * d, 0))


def _shift_up(x, d, rows, L):
    if d == 0:
        return x
    wrapped = jnp.where((rows & (NSEG - 1)) == NSEG - 1, 0.0, pltpu.roll(x, L - (NSEG * d + 1), 0))
    return jnp.where(rows >= L - NSEG * d, wrapped, pltpu.roll(x, L - NSEG * d, 0))


def conv_fwd(proj, w, b, L):
    def body(x_ref, w_ref, b_ref, o_ref):
        x = x_ref[...]
        rows = lax.broadcasted_iota(jnp.int32, x.shape, 0)
        acc = jnp.broadcast_to(b_ref[...], x.shape)
        for k in range(CONV_W):
            acc = acc + w_ref[pl.ds(k, 1), :] * _shift_down(x, CONV_W - 1 - k, rows, L)
        o_ref[...] = acc

    nl = BW // LANE
    return pl.pallas_call(
        body, grid=(nl,), name="conv_fwd",
        in_specs=[pl.BlockSpec((L, LANE), lambda g: (0, 5 * nl + g)), pl.BlockSpec((CONV_W, LANE), lambda g: (0, g)),
                  pl.BlockSpec((1, LANE), lambda g: (0, g))],
        out_specs=pl.BlockSpec((L, LANE), lambda g: (0, g)),
        out_shape=jax.ShapeDtypeStruct((L, BW), F32),
        compiler_params=pltpu.CompilerParams(vmem_limit_bytes=VMEM_LIMIT),
    )(proj, w, b)


def conv_bwd(proj, w, dxc, L):
    def body(x_ref, w_ref, d_ref, dx_ref, dw_ref, db_ref):
        x, d = x_ref[...], d_ref[...]
        rows = lax.broadcasted_iota(jnp.int32, x.shape, 0)
        acc = jnp.zeros_like(x)
        for k in range(CONV_W):
            acc = acc + w_ref[pl.ds(k, 1), :] * _shift_up(d, CONV_W - 1 - k, rows, L)
            dw_ref[pl.ds(k, 1), :] = jnp.sum(d * _shift_down(x, CONV_W - 1 - k, rows, L), axis=0, keepdims=True)
        dx_ref[...] = acc
        db_ref[...] = jnp.sum(d, axis=0, keepdims=True)

    nl = BW // LANE
    blk = pl.BlockSpec((L, LANE), lambda g: (0, g))
    return pl.pallas_call(
        body, grid=(nl,), name="conv_bwd",
        in_specs=[pl.BlockSpec((L, LANE), lambda g: (0, 5 * nl + g)), pl.BlockSpec((CONV_W, LANE), lambda g: (0, g)), blk],
        out_specs=[blk, pl.BlockSpec((CONV_W, LANE), lambda g: (0, g)), pl.BlockSpec((1, LANE), lambda g: (0, g))],
        out_shape=[jax.ShapeDtypeStruct((L, BW), F32), jax.ShapeDtypeStruct((CONV_W, BW), F32),
                   jax.ShapeDtypeStruct((1, BW), F32)],
        compiler_params=pltpu.CompilerParams(vmem_limit_bytes=VMEM_LIMIT),
    )(proj, w, dxc)


def loss_fwd_bwd(x, fw, target, L, tm):
    def fn(x, fw, t):
        err = jnp.square(_rms(x, fw) - t)
        return jnp.sum(0.5 * jnp.mean(err, axis=-1, keepdims=True), axis=0, keepdims=True)

    def body(x_ref, fw_ref, t_ref, l_ref, dx_ref, dfw_ref):
        i = pl.program_id(0)
        t = t_ref[...]
        val, vjp = jax.vjp(lambda x, fw: fn(x, fw, t), x_ref[...], fw_ref[...])
        dx, dfw = vjp(jnp.ones((1, 1), F32))
        dx_ref[...] = dx

        @pl.when(i == 0)
        def _():
            l_ref[...] = jnp.zeros_like(l_ref)
            dfw_ref[...] = jnp.zeros_like(dfw_ref)

        l_ref[...] += jnp.broadcast_to(val, l_ref.shape)
        dfw_ref[...] += dfw

    row = pl.BlockSpec((tm, D_MODEL), lambda i: (i, 0))
    vec = pl.BlockSpec((1, D_MODEL), lambda i: (0, 0))
    return pl.pallas_call(
        body, grid=(L // tm,), name="loss_fwd_bwd", in_specs=[row, vec, row],
        out_specs=[pl.BlockSpec((1, LANE), lambda i: (0, 0)), row, vec],
        out_shape=[jax.ShapeDtypeStruct((1, LANE), F32), jax.ShapeDtypeStruct((L, D_MODEL), F32),
                   jax.ShapeDtypeStruct((1, D_MODEL), F32)],
        compiler_params=pltpu.CompilerParams(vmem_limit_bytes=VMEM_LIMIT, dimension_semantics=("arbitrary",)),
    )(x, fw, target)


def adamw(w, g, m, v):
    rows, cols = w.shape
    tr = _row_tile(rows, cols, budget=1024 * 1024)
    c1 = 1.0 - ADAM_B1 ** ADAM_STEP
    c2 = 1.0 - ADAM_B2 ** ADAM_STEP

    def body(w_ref, g_ref, m_ref, v_ref, d_ref, nm_ref, nv_ref):
        g = g_ref[...]
        nm = ADAM_B1 * m_ref[...] + (1.0 - ADAM_B1) * g
        nv = ADAM_B2 * v_ref[...] + (1.0 - ADAM_B2) * jnp.square(g)
        d_ref[...] = -ADAM_LR * ((nm / c1) / (jnp.sqrt(nv / c2) + ADAM_EPS) + ADAM_WD * w_ref[...])
        nm_ref[...] = nm
        nv_ref[...] = nv

    blk = pl.BlockSpec((tr, cols), lambda i: (i, 0))
    return pl.pallas_call(
        body, grid=(rows // tr,), name="adamw", in_specs=[blk] * 4, out_specs=[blk] * 3,
        out_shape=[jax.ShapeDtypeStruct((rows, cols), F32)] * 3,
    )(w, g, m, v)


def s5_prep(lam_re, lam_im, log_dt, b_re, b_im, c_re, c_im):
    lr = jnp.minimum(lam_re, -1e-4)
    li = lam_im
    dt = jnp.exp(log_dt)[:, None]
    mag = jnp.exp(lr * dt)
    ar = mag * jnp.cos(li * dt)
    ai = mag * jnp.sin(li * dt)
    den = lr * lr + li * li
    fr = ((ar - 1.0) * lr + ai * li) / den
    fi = (ai * lr - (ar - 1.0) * li) / den
    bbr = fr[..., None] * b_re - fi[..., None] * b_im
    bbi = fr[..., None] * b_im + fi[..., None] * b_re
    emb_b = lambda bb: _block_diag(bb.transpose(0, 2, 1).reshape(BW, S5_STATE), S5_GROUPS)
    emb_c = lambda cc: _block_diag(cc.transpose(0, 2, 1).reshape(S5_N, S5_GROUP), S5_GROUPS)
    bmat = jnp.concatenate([emb_b(bbr), emb_b(bbi)], axis=1)
    cmat = jnp.concatenate([emb_c(c_re), -emb_c(c_im)], axis=0)
    nb = S5_N // LANE
    return ar.reshape(nb, 1, LANE), ai.reshape(nb, 1, LANE), bmat, cmat


def _block_diag(stacked, groups):
    rows, c = stacked.shape
    r = rows // groups
    row_g = jnp.arange(rows)[:, None] // r
    col_g = jnp.arange(groups * c)[None, :] // c
    return jnp.where(row_g == col_g, jnp.tile(stacked, (1, groups)), 0.0)


def rg_prep(w):
    return _block_diag(w.reshape(BW, RG_BLOCK), RG_BLOCKS)


def hg_prep(logits):
    p = jax.nn.softmax(logits, axis=0)
    return jnp.cumsum(p, axis=0) - p[0]


def _head_mean_matrix():
    r = jnp.arange(BW) // HG_D
    return (r[:, None] == r[None, :]).astype(F32) / HG_D


def _to_segment_order(a):
    L = a.shape[0]
    return a.reshape(NSEG, L // NSEG, -1).transpose(1, 0, 2).reshape(a.shape)


def _to_time_order(a):
    L = a.shape[0]
    return a.reshape(L // NSEG, NSEG, -1).transpose(1, 0, 2).reshape(a.shape)


def _const(*idx):
    return lambda s, i: idx


def _rows(cb=0):
    return lambda s, i: (i, cb)


def _sum_parts(name, first, parts, shape):
    return add_n(name, [(first, ())] + [(parts, (s,)) for s in range(NSH)], shape)


def _ffn_weight_specs(l, j):
    F = D_FF // NSH
    one = pl.Buffered(1)
    return [pl.BlockSpec((None, NSH, None, D_MODEL, F), lambda i: (l, 0, j, 0, 0), pipeline_mode=one),
            pl.BlockSpec((None, NSH, None, D_MODEL, F), lambda i: (l, 0, j, 0, 0), pipeline_mode=one),
            pl.BlockSpec((None, NSH, None, F, D_MODEL), lambda i: (l, 0, j, 0, 0), pipeline_mode=one)]


def ffn_fwd(name, x, W, l, j, k, L, tm):
    D, F = D_MODEL, D_FF // NSH

    def body(x_ref, nw_ref, wg_ref, wu_ref, wd_ref, y_ref, g_ref, u_ref):
        x = x_ref[...]
        h = _rms(x, nw_ref[...]).astype(MMT)
        y = x
        for s in range(NSH):
            g = _dg(h, wg_ref[s], 1, 0)
            u = _dg(h, wu_ref[s], 1, 0)
            g_ref[s] = g.astype(g_ref.dtype)
            u_ref[s] = u.astype(u_ref.dtype)
            y = y + 0.5 * _dg((jax.nn.silu(g) * u).astype(MMT), wd_ref[s], 1, 0)
        y_ref[...] = y

    row = pl.BlockSpec((tm, D), lambda i: (i, 0))
    act = pl.BlockSpec((NSH, tm, F), lambda i: (0, i, 0))
    return pl.pallas_call(
        body, grid=(L // tm,), name=name,
        in_specs=[row, pl.BlockSpec((None, None, 1, D), lambda i: (l, k, 0, 0))] + _ffn_weight_specs(l, j),
        out_specs=[row, act, act],
        out_shape=[jax.ShapeDtypeStruct((L, D), F32), jax.ShapeDtypeStruct((NSH, L, F), MMT),
                   jax.ShapeDtypeStruct((NSH, L, F), MMT)],
        compiler_params=pltpu.CompilerParams(vmem_limit_bytes=VMEM_LIMIT, dimension_semantics=("arbitrary",)),
    )(x, W["nw"], W["wg"], W["wu"], W["wd"])


def ffn_bwd(name, x, g, u, dy, W, bufs, l, j, k, L, tm):
    D, F = D_MODEL, D_FF // NSH

    def body(x_ref, nw_ref, dy_ref, g_ref, u_ref, wg_ref, wu_ref, wd_ref, *rest):
        part_ref, dnw_ref, dwg_ref, dwu_ref, dwd_ref = rest[-5:]
        s, i = pl.program_id(0), pl.program_id(1)
        x, nw = x_ref[...], nw_ref[...]
        r = lax.rsqrt(jnp.mean(x * x, axis=-1, keepdims=True) + EPS)
        xhat = x * r
        h = (xhat * nw).astype(MMT)
        half_dy = (0.5 * dy_ref[...]).astype(MMT)
        gs, us = g_ref[...].astype(F32), u_ref[...].astype(F32)
        sig = jax.nn.sigmoid(gs)
        act = gs * sig
        da = _dg(half_dy, wd_ref[...], 1, 1)
        du = (da * act).astype(MMT)
        dg = (da * us * (sig * (1.0 + gs * (1.0 - sig)))).astype(MMT)
        dh = _dg(dg, wg_ref[...], 1, 1) + _dg(du, wu_ref[...], 1, 1)
        dxh = dh * nw
        part_ref[...] = r * (dxh - xhat * jnp.mean(dxh * xhat, axis=-1, keepdims=True))
        grads = (_dg(h, dg, 0, 0), _dg(h, du, 0, 0), _dg((act * us).astype(MMT), half_dy, 0, 0))
        dnw = jnp.sum(dh * xhat, axis=0, keepdims=True)
        first = jnp.logical_and(s == 0, i == 0)
        for ref, val, start in zip((dwg_ref, dwu_ref, dwd_ref, dnw_ref), grads + (dnw,), (i == 0, i == 0, i == 0, first)):
            @pl.when(start)
            def _(ref=ref, val=val):
                ref[...] = val

            @pl.when(jnp.logical_not(start))
            def _(ref=ref, val=val):
                ref[...] += val

    keys = ("ffn_gate", "ffn_up", "ffn_down")
    given = [bufs[key] for key in keys if bufs.get(key) is not None]
    row = pl.BlockSpec((tm, D), lambda s, i: (i, 0))
    act = pl.BlockSpec((None, tm, F), lambda s, i: (s, i, 0))
    wsp = lambda r, c: pl.BlockSpec((None, None, None, r, c), lambda s, i: (l, s, j, 0, 0))
    stk = lambda r, c: pl.BlockSpec((None, None, r, c), lambda s, i: (2 * l + j, s, 0, 0))
    part, dnw, bufs["ffn_gate"], bufs["ffn_up"], bufs["ffn_down"] = pl.pallas_call(
        body, grid=(NSH, L // tm), name=name,
        in_specs=[row, pl.BlockSpec((None, None, 1, D), lambda s, i: (l, k, 0, 0)), row, act, act,
                  wsp(D, F), wsp(D, F), wsp(F, D)] + [pl.BlockSpec(memory_space=pl.ANY)] * len(given),
        out_specs=[pl.BlockSpec((None, tm, D), lambda s, i: (s, i, 0)), pl.BlockSpec((1, D), lambda s, i: (0, 0)),
                   stk(D, F), stk(D, F), stk(F, D)],
        out_shape=[jax.ShapeDtypeStruct((NSH, L, D), F32), jax.ShapeDtypeStruct((1, D), F32)]
        + [jax.ShapeDtypeStruct((2 * DEPTH, NSH, D, F), F32)] * 2 + [jax.ShapeDtypeStruct((2 * DEPTH, NSH, F, D), F32)],
        input_output_aliases={8 + n: 2 + n for n in range(len(given))},
        compiler_params=pltpu.CompilerParams(vmem_limit_bytes=VMEM_LIMIT, dimension_semantics=("arbitrary", "arbitrary")),
    )(x, W["nw"], dy, g, u, W["wg"], W["wu"], W["wd"], *given)
    return _sum_parts(name + "_dx", dy, part, (L, D)), dnw


def layer_fwd(l, x0, W, P, L, tm):
    D = D_MODEL
    n_i = L // tm
    x1, g0, u0 = ffn_fwd(f"ffn_fwd_{l}0", x0, W, l, 0, 0, L, tm)
    proj = tile_fwd(
        lambda x, nw, win, s: pre_core(x, nw, win), f"pre_fwd_{l}", n_i, NSH,
        [(x1, (tm, D), _rows()), (W["nw"], (None, None, 1, D), _const(l, 1, 0, 0)),
         (W["win"], (None, None, D, IN_TOTAL // NSH), lambda s, i: (l, s, 0, 0))],
        [((L, IN_TOTAL), F32, (tm, IN_TOTAL // NSH), lambda s, i: (i, s), False)], s_outer=True)[0]
    nb = S5_N // LANE
    blk3 = lambda s, i: (0, i, 0)
    bur, bui = tile_fwd(
        lambda u, bmat, s: s5_pre_core(u, bmat), f"s5pre_fwd_{l}", n_i, 1,
        [(proj, (tm, BW), _rows(0)), (P["bmat"], (None, BW, 2 * S5_N), _const(l, 0, 0))],
        [((nb, L, LANE), F32, (nb, tm, LANE), blk3, False)] * 2)
    xr, xi = s5_scan_fwd(bur, bui, P["ar"][l], P["ai"][l], L)
    qzv = _to_time_order(proj[:, BW:4 * BW])
    o_t, sst = hg_fwd(qzv, P["lb"][l], L)
    o = _to_segment_order(o_t)
    xc = conv_fwd(proj, W["convw"][l], P["convb"][l], L)
    vec = (None, 1, BW)
    a, b = tile_fwd(
        lambda xc, wa, ba, wx, bx, lam, s: gates_core(xc, wa, ba, wx, bx, lam), f"gates_fwd_{l}", n_i, 1,
        [(xc, (tm, BW), _rows()), (P["wa"], (None, BW, BW), _const(l, 0, 0)), (P["ba"], vec, _const(l, 0, 0)),
         (P["wx"], (None, BW, BW), _const(l, 0, 0)), (P["bx"], vec, _const(l, 0, 0)), (P["lam"], vec, _const(l, 0, 0))],
        [((L, BW), F32, (tm, BW), _rows(), False)] * 2)
    hs = rg_scan_fwd(a, b, L)
    tmm = tm
    ya, yb, yc = tile_fwd(
        lambda *a: mid_core(*a[:-1]), f"mid_fwd_{l}", L // tmm, 1,
        [(xr, (nb, tmm, LANE), blk3), (xi, (nb, tmm, LANE), blk3), (proj, (tmm, BW), _rows(0)), (o, (tmm, BW), _rows()),
         (proj, (tmm, BW), _rows(4)), (hs, (tmm, BW), _rows()), (proj, (tmm, BW), _rows(6)),
         (P["hmat"], (BW, BW), _const(0, 0)), (P["cmat"], (None, 2 * S5_N, BW), _const(l, 0, 0)), (P["d"], vec, _const(l, 0, 0)),
         (W["gluw"], (None, BW, BW), _const(l, 0, 0)), (P["glub"], vec, _const(l, 0, 0)), (P["hgw"], vec, _const(l, 0, 0))],
        [((L, BW), F32, (tmm, BW), _rows(), False)] * 3)
    x2 = tile_fwd(
        lambda x, *rest: (x + merge_core(*rest[:-1])[0],), f"merge_fwd_{l}", n_i, 1,
        [(x1, (tm, D), _rows()), (ya, (tm, BW), _rows()), (yb, (tm, BW), _rows()), (yc, (tm, BW), _rows())]
        + [(proj, (tm, BW), _rows(7 + k)) for k in range(6)]
        + [(W["pfull"], (None, 3, BW, D), _const(l, 0, 0, 0)), (W["woutfull"], (None, D, D), _const(l, 0, 0))],
        [((L, D), F32, (tm, D), _rows(), False)])[0]
    x3, g1, u1 = ffn_fwd(f"ffn_fwd_{l}1", x2, W, l, 1, 2, L, tm)
    saved = dict(x0=x0, x1=x1, x2=x2, proj=proj, xr=xr, xi=xi, o=o, sst=sst, xc=xc, a=a, hs=hs, ya=ya, yb=yb, yc=yc,
                 qzv=qzv, g0=g0, u0=u0, g1=g1, u1=u1)
    return x3, saved


def layer_bwd(l, dx3, sv, W, P, bufs, L, tm):
    D = D_MODEL
    n_i = L // tm
    nb = S5_N // LANE
    dq = D // NSH
    vec = (None, 1, BW)
    vout = ((1, BW), (1, BW), _const(0, 0), "acc_all")
    blk3 = lambda s, i: (0, i, 0)
    small = {}
    proj = sv["proj"]

    dx2, dnw2 = ffn_bwd(f"ffn_bwd_{l}1", sv["x2"], sv["g1"], sv["u1"], dx3, W, bufs, l, 1, 2, L, tm)

    rw256 = ((L, BW), (tm, BW), _rows(), "write")
    res = tile_bwd(
        merge_core, f"merge_bwd_{l}", n_i, 1,
        [(sv["ya"], (tm, BW), _rows(), "r"), (sv["yb"], (tm, BW), _rows(), "r"), (sv["yc"], (tm, BW), _rows(), "r")]
        + [(proj, (tm, BW), _rows(7 + k), "r") for k in range(6)]
        + [(W["pfull"], (None, 3, BW, D), _const(l, 0, 0, 0), "w"), (W["woutfull"], (None, D, D), _const(l, 0, 0), "w")],
        [(dx2, (tm, D), _rows())],
        [rw256] * 9
        + [((DEPTH, 3, BW, D), (None, 3, BW, D), _const(l, 0, 0, 0), "acc_all", bufs.get("branch_proj")),
           ((DEPTH, D, D), (None, D, D), _const(l, 0, 0), "acc_all", bufs.get("w_out"))])
    dya, dyb, dyc = res[:3]
    dgm = res[3:9]
    bufs["branch_proj"], bufs["w_out"] = res[9:]

    tmm = min(tm, 128)
    rw = ((L, BW), (tmm, BW), _rows(), "write")
    xw = ((nb, L, LANE), (nb, tmm, LANE), blk3, "write")
    res = tile_bwd(
        mid_core, f"mid_bwd_{l}", L // tmm, 1,
        [(sv["xr"], (nb, tmm, LANE), blk3, "r"), (sv["xi"], (nb, tmm, LANE), blk3, "r"), (proj, (tmm, BW), _rows(0), "r"),
         (sv["o"], (tmm, BW), _rows(), "r"), (proj, (tmm, BW), _rows(4), "r"), (sv["hs"], (tmm, BW), _rows(), "r"),
         (proj, (tmm, BW), _rows(6), "r"), (P["hmat"], (BW, BW), _const(0, 0), "c"),
         (P["cmat"], (None, 2 * S5_N, BW), _const(l, 0, 0), "w"), (P["d"], vec, _const(l, 0, 0), "p"),
         (W["gluw"], (None, BW, BW), _const(l, 0, 0), "w"), (P["glub"], vec, _const(l, 0, 0), "p"),
         (P["hgw"], vec, _const(l, 0, 0), "p")],
        [(dya, (tmm, BW), _rows()), (dyb, (tmm, BW), _rows()), (dyc, (tmm, BW), _rows())],
        [xw, xw, rw, rw, rw, rw, rw,
         ((DEPTH, 2 * S5_N, BW), (None, 2 * S5_N, BW), _const(l, 0, 0), "acc_all", bufs.get("cmat")), vout,
         ((DEPTH, BW, BW), (None, BW, BW), _const(l, 0, 0), "acc_all", bufs.get("s5_glu_w")), vout, vout])
    dxr, dxi, du_skip, do, dg_b, dhs, dgate_c, bufs["cmat"], dd, bufs["s5_glu_w"], dglub, dhgw = res
    small["s5_d"], small["s5_glu_b"], small["hg_norm_w"] = dd[0], dglub[0], dhgw[0]

    da, db = rg_scan_bwd(sv["a"], sv["hs"], dhs, L)
    wmat = lambda key: ((DEPTH, BW, BW), (None, BW, BW), _const(l, 0, 0), "acc_all", bufs.get(key))
    res = tile_bwd(
        gates_core, f"gates_bwd_{l}", n_i, 1,
        [(sv["xc"], (tm, BW), _rows(), "r"), (P["wa"], (None, BW, BW), _const(l, 0, 0), "w"), (P["ba"], vec, _const(l, 0, 0), "p"),
         (P["wx"], (None, BW, BW), _const(l, 0, 0), "w"), (P["bx"], vec, _const(l, 0, 0), "p"), (P["lam"], vec, _const(l, 0, 0), "p")],
        [(da, (tm, BW), _rows()), (db, (tm, BW), _rows())],
        [((L, BW), (tm, BW), _rows(), "write"), wmat("wa"), vout, wmat("wx"), vout, vout])
    dxc, bufs["wa"], dba, bufs["wx"], dbx, dlam = res
    small["rg_ba"], small["rg_bx"], small["rg_lambda"] = dba[0], dbx[0], dlam[0]
    dx_c, dconvw, dconvb = conv_bwd(proj, W["convw"][l], dxc, L)
    small["rg_conv_w"], small["rg_conv_b"] = dconvw, dconvb[0]

    dq_b, dz_b, dv_b, dlb = hg_bwd(sv["qzv"], P["lb"][l], sv["sst"], _to_time_order(do), L)
    dq_b, dz_b, dv_b = [_to_segment_order(a) for a in (dq_b, dz_b, dv_b)]

    gr, gi, dar, dai = s5_scan_bwd(dxr, dxi, sv["xr"], sv["xi"], P["ar"][l], P["ai"][l], L)
    du_pre, bufs["bmat"] = tile_bwd(
        s5_pre_core, f"s5pre_bwd_{l}", n_i, 1,
        [(proj, (tm, BW), _rows(0), "r"), (P["bmat"], (None, BW, 2 * S5_N), _const(l, 0, 0), "w")],
        [(gr, (nb, tm, LANE), blk3), (gi, (nb, tm, LANE), blk3)],
        [((L, BW), (tm, BW), _rows(), "write"),
         ((DEPTH, BW, 2 * S5_N), (None, BW, 2 * S5_N), _const(l, 0, 0), "acc_all", bufs.get("bmat"))])
    du_a = add_n(f"du_a_{l}", [(du_skip, ()), (du_pre, ())], (L, BW))
    prep_ct = dict(dar=dar, dai=dai, dlb=dlb)

    pieces = [du_a, dq_b, dz_b, dv_b, dg_b, dx_c, dgate_c, *dgm]
    per_piece, per_shard = BW // LANE, IN_TOTAL // NSH // LANE
    part, dnw1 = None, []
    for s in range(NSH):
        groups = [(pieces[g // per_piece], (tm, LANE), _rows(g % per_piece))
                  for g in range(s * per_shard, (s + 1) * per_shard)]
        part, dnw_s, bufs["w_in"] = tile_bwd(
            pre_core, f"pre_bwd_{l}{s}", n_i, 1,
            [(sv["x1"], (tm, D), _rows(), "r"), (W["nw"], (None, None, 1, D), _const(l, 1, 0, 0), "p"),
             (W["win"], (None, None, D, IN_TOTAL // NSH), _const(l, s, 0, 0), "w")],
            [groups],
            [((NSH, L, D), (None, tm, D), functools.partial(lambda _s, i, s: (s, i, 0), s=s), "write", part),
             ((1, D), (1, D), _const(0, 0), "acc_all"),
             ((DEPTH, NSH, D, IN_TOTAL // NSH), (None, None, D, IN_TOTAL // NSH), _const(l, s, 0, 0), "acc_all",
              bufs.get("w_in"))])
        dnw1.append(dnw_s)
    dnw1 = (dnw1[0] + dnw1[1]) + (dnw1[2] + dnw1[3])
    dx1 = _sum_parts(f"pre_bwd_{l}_dx", dx2, part, (L, D))

    dx0, dnw0 = ffn_bwd(f"ffn_bwd_{l}0", sv["x0"], sv["g0"], sv["u0"], dx1, W, bufs, l, 0, 0, L, tm)
    small["norm_w"] = jnp.concatenate([dnw0, dnw1, dnw2], axis=0)
    return dx0, small, prep_ct


SMALL_RAW = ("s5_lambda_re", "s5_lambda_im", "s5_log_dt", "s5_b_re", "s5_b_im", "s5_c_re", "s5_c_im", "s5_d", "s5_glu_b",
             "hg_lb_logits", "hg_norm_w", "rg_conv_b", "rg_wa", "rg_ba", "rg_wx", "rg_bx", "rg_lambda", "final_norm_w")
DEPTH = 2


def local_step(x, target, W, raw):
    L = x.shape[0]
    tm = min(256, L)
    col = lambda v: v.reshape(DEPTH, 1, BW)
    (ar, ai, bmat, cmat), s5_vjp = jax.vjp(jax.vmap(s5_prep), *[raw[k] for k in SMALL_RAW[:7]])
    (wa, wx), rg_vjp = jax.vjp(lambda a, b: (jax.vmap(rg_prep)(a), jax.vmap(rg_prep)(b)), raw["rg_wa"], raw["rg_wx"])
    lb, hg_vjp = jax.vjp(hg_prep, raw["hg_lb_logits"])
    P = dict(
        ar=[ar[l] for l in range(DEPTH)], ai=[ai[l] for l in range(DEPTH)],
        bmat=bmat.astype(MMT), cmat=cmat.astype(MMT), wa=wa.astype(MMT), wx=wx.astype(MMT),
        lb=[lb[l].reshape(1, BW) for l in range(DEPTH)], convb=[raw["rg_conv_b"][l].reshape(1, BW) for l in range(DEPTH)],
        ba=col(raw["rg_ba"]), bx=col(raw["rg_bx"]), lam=col(raw["rg_lambda"]), d=col(raw["s5_d"]),
        glub=col(raw["s5_glu_b"]), hgw=col(raw["hg_norm_w"]), hmat=_head_mean_matrix())

    saved = []
    h = _to_segment_order(x)
    for l in range(DEPTH):
        h, sv = layer_fwd(l, h, W, P, L, tm)
        saved.append(sv)
    loss, dh, dfw = loss_fwd_bwd(h, raw["final_norm_w"].reshape(1, D_MODEL), _to_segment_order(target), L, tm)

    big, per_layer, prep_cts = {}, [None] * DEPTH, [None] * DEPTH
    for l in reversed(range(DEPTH)):
        dh, sm, pc = layer_bwd(l, dh, saved[l], W, P, big, L, tm)
        per_layer[l], prep_cts[l] = sm, pc
    dh = _to_time_order(dh)

    small = {k: jnp.stack([per_layer[l][k] for l in range(DEPTH)]) for k in per_layer[0]}
    both = lambda k: jnp.stack([prep_cts[l][k] for l in range(DEPTH)])
    s5_g = s5_vjp((both("dar"), both("dai"), big.pop("bmat"), big.pop("cmat")))
    small.update(zip(SMALL_RAW[:7], s5_g))
    small["rg_wa"], small["rg_wx"] = rg_vjp((big.pop("wa"), big.pop("wx")))
    (small["hg_lb_logits"],) = hg_vjp(jnp.concatenate([prep_cts[l]["dlb"] for l in range(DEPTH)], axis=0))
    small["final_norm_w"] = dfw[0]
    return loss, dh, big, small


ANY = pl.BlockSpec(memory_space=pl.ANY)


def _place():
    x, y, c = lax.axis_index("x"), lax.axis_index("y"), lax.axis_index("c")
    chips = [(1 - x, y), (x, 1 - y), (1 - x, 1 - y)]
    return x, y, c, chips


def _remote(src, dst, send, recv, k, to):
    return pltpu.make_async_remote_copy(src_ref=src, dst_ref=dst, send_sem=send.at[k], recv_sem=recv.at[k],
                                        device_id=to, device_id_type=MESH)


def _comm_call(body, name, ins, out_shapes, n_sem, n_loc):
    return pl.pallas_call(
        body, name=name, in_specs=[ANY] * len(ins), out_specs=[ANY] * len(out_shapes), out_shape=out_shapes,
        scratch_shapes=[pltpu.SemaphoreType.DMA((n_sem,)), pltpu.SemaphoreType.DMA((n_sem,)),
                        pltpu.SemaphoreType.DMA((max(n_loc, 1),))],
    )(*ins)


def gather_shards(name, shards):
    n = len(shards)
    per = 8

    def body(*refs):
        ins, outs = refs[:n], refs[n:2 * n]
        send, recv, _ = refs[2 * n:]
        x, y, c, chips = _place()
        me = 2 * x + y
        sib = (x, y, 1 - c)
        sends = []
        for w in range(n):
            for j, (cx, cy) in enumerate(chips):
                cp = _remote(ins[w].at[c], outs[w].at[c, me], send, recv, per * w + j, (cx, cy, c))
                cp.start()
                sends.append(cp)
        for w in range(n):
            for l in range(2):
                cp = _remote(ins[w].at[l], outs[w].at[l, me], send, recv, per * w + 6 + l, sib)
                cp.start()
                sends.append(cp)
        for w in range(n):
            for j, (cx, cy) in enumerate(chips):
                theirs = outs[w].at[c, 2 * cx + cy]
                _remote(ins[w].at[c], theirs, send, recv, per * w + j, (cx, cy, c)).wait_recv()
                cp = _remote(theirs, theirs, send, recv, per * w + 3 + j, sib)
                cp.start()
                sends.append(cp)
        for w in range(n):
            for j, (cx, cy) in enumerate(chips):
                dst = outs[w].at[1 - c, 2 * cx + cy]
                _remote(dst, dst, send, recv, per * w + 3 + j, sib).wait_recv()
            for l in range(2):
                dst = outs[w].at[l, me]
                _remote(dst, dst, send, recv, per * w + 6 + l, sib).wait_recv()
        for cp in sends:
            cp.wait_send()

    shapes = [jax.ShapeDtypeStruct((2, NSH) + s.shape[1:], s.dtype) for s in shards]
    return _comm_call(body, name, shards, shapes, per * n, 0)


def exchange_halves(name, grads):
    n = len(grads)

    def body(*refs):
        ins, outs = refs[:n], refs[n:2 * n]
        send, recv, _ = refs[2 * n:]
        x, y, c, _chips = _place()
        cps = []
        for w in range(n):
            h = grads[w].shape[2] // 2
            cp = _remote(ins[w].at[:, :, pl.ds((1 - c) * h, h)], outs[w], send, recv, w, (x, y, 1 - c))
            cp.start()
            cps.append(cp)
        for cp in cps:
            cp.wait()

    shapes = [jax.ShapeDtypeStruct(g.shape[:2] + (g.shape[2] // 2, g.shape[3]), g.dtype) for g in grads]
    return _comm_call(body, name, grads, shapes, n, 0)


def scatter_to_chips(name, halves):
    n = len(halves)

    def body(*refs):
        ins, outs = refs[:n], refs[n:2 * n]
        send, recv, _ = refs[2 * n:]
        x, y, c, chips = _place()
        cps = []
        for w in range(n):
            for j, (cx, cy) in enumerate(chips):
                cp = _remote(ins[w].at[:, 2 * cx + cy], outs[w].at[j], send, recv, 3 * w + j, (cx, cy, c))
                cp.start()
                cps.append(cp)
        for cp in cps:
            cp.wait()

    shapes = [jax.ShapeDtypeStruct((3, h.shape[0]) + h.shape[2:], h.dtype) for h in halves]
    return _comm_call(body, name, halves, shapes, 3 * n, 0)


def share_halves(name, pieces):
    n = len(pieces)

    def body(*refs):
        ins, outs = refs[:n], refs[n:2 * n]
        send, recv, _ = refs[2 * n:]
        x, y, c, _chips = _place()
        cps = []
        for w in range(n):
            cp = _remote(ins[w], outs[w], send, recv, w, (x, y, 1 - c))
            cp.start()
            cps.append(cp)
        for cp in cps:
            cp.wait()

    return _comm_call(body, name, pieces, [jax.ShapeDtypeStruct(p.shape, p.dtype) for p in pieces], n, 0)


def add_own_half(name, g, ra, c, wire):
    nblk, h, cols = ra.shape
    tr = _row_tile(h, cols, mult=16)
    nt = h // tr

    def body(c_ref, g_ref, r_ref, o_ref):
        o_ref[...] = (g_ref[...] + r_ref[...]).astype(o_ref.dtype)

    blk = (None, tr, cols)
    return pl.pallas_call(
        body, name=name,
        grid_spec=pltpu.PrefetchScalarGridSpec(
            num_scalar_prefetch=1, grid=(nblk, nt),
            in_specs=[pl.BlockSpec(blk, lambda s, i, c_ref: (s, c_ref[0] * nt + i, 0)), pl.BlockSpec(blk, lambda s, i, c_ref: (s, i, 0))],
            out_specs=pl.BlockSpec(blk, lambda s, i, c_ref: (s, i, 0))),
        out_shape=jax.ShapeDtypeStruct(ra.shape, wire),
    )(c.reshape(1), g, ra)


def add_chips(name, hb, rb, me):
    npc, _, h, cols = hb.shape
    tr = _row_tile(h, cols, mult=16)

    def body(me_ref, h_ref, r0, r1, r2, o_ref):
        f = lambda r: r[...].astype(F32)
        o_ref[...] = ((f(h_ref) + f(r0)) + f(r1)) + f(r2)

    rspec = lambda j: pl.BlockSpec((None, None, tr, cols), functools.partial(lambda p, i, me_ref, j: (j, p, i, 0), j=j))
    return pl.pallas_call(
        body, name=name,
        grid_spec=pltpu.PrefetchScalarGridSpec(
            num_scalar_prefetch=1, grid=(npc, h // tr),
            in_specs=[pl.BlockSpec((None, None, tr, cols), lambda p, i, me_ref: (p, me_ref[0], i, 0)), rspec(0), rspec(1), rspec(2)],
            out_specs=pl.BlockSpec((None, tr, cols), lambda p, i, me_ref: (p, i, 0))),
        out_shape=jax.ShapeDtypeStruct((npc, h, cols), F32),
    )(me.reshape(1), hb, rb, rb, rb)


def adamw_halves(name, w, m, v, own, other, c):
    npc, rows, cols = w.shape
    h = rows // 2
    tr = _row_tile(h, cols, budget=1024 * 1024)
    nt = h // tr
    c1 = 1.0 - ADAM_B1 ** ADAM_STEP
    c2 = 1.0 - ADAM_B2 ** ADAM_STEP

    def body(c_ref, w_ref, m_ref, v_ref, own_ref, oth_ref, g_ref, d_ref, nm_ref, nv_ref):
        g = jnp.where(pl.program_id(1) == c_ref[0], own_ref[...], oth_ref[...])
        nm = ADAM_B1 * m_ref[...] + (1.0 - ADAM_B1) * g
        nv = ADAM_B2 * v_ref[...] + (1.0 - ADAM_B2) * jnp.square(g)
        g_ref[...] = g
        d_ref[...] = -ADAM_LR * ((nm / c1) / (jnp.sqrt(nv / c2) + ADAM_EPS) + ADAM_WD * w_ref[...])
        nm_ref[...] = nm
        nv_ref[...] = nv

    full = pl.BlockSpec((None, tr, cols), lambda p, hh, i, c_ref: (p, hh * nt + i, 0))
    half = pl.BlockSpec((None, tr, cols), lambda p, hh, i, c_ref: (p, i, 0))
    return pl.pallas_call(
        body, name=name,
        grid_spec=pltpu.PrefetchScalarGridSpec(
            num_scalar_prefetch=1, grid=(npc, 2, nt),
            in_specs=[full, full, full, half, half], out_specs=[full] * 4),
        out_shape=[jax.ShapeDtypeStruct(w.shape, F32)] * 4,
    )(c.reshape(1), w, m, v, own, other)


WEIGHTS = ("norm_w", "final_norm_w", "ffn_gate", "ffn_up", "ffn_down", "w_in", "branch_proj", "w_out", "s5_lambda_re",
           "s5_lambda_im", "s5_log_dt", "s5_b_re", "s5_b_im", "s5_c_re", "s5_c_im", "s5_d", "s5_glu_w", "s5_glu_b",
           "hg_lb_logits", "hg_norm_w", "rg_conv_w", "rg_conv_b", "rg_wa", "rg_ba", "rg_wx", "rg_bx", "rg_lambda")
BIG = ("ffn_gate", "ffn_up", "ffn_down", "w_in", "branch_proj", "w_out", "s5_glu_w")
SHARDED_SMALL = ("norm_w", "rg_conv_w")
SMALL = SMALL_RAW + SHARDED_SMALL


def _view2d(shape):
    return (1, shape[0]) if len(shape) == 1 else (math.prod(shape[:-1]), shape[-1])


def _small_layout(shapes, row_multiple):
    layout, at = [], 0
    for shape in shapes:
        r, c = _view2d(shape)
        rp = -(-r // 8) * 8
        layout.append((at, r, c, rp))
        at += rp * max(1, c // LANE)
    return layout, -(-at // row_multiple) * row_multiple


def pack_small(name, arrays, row_multiple):
    layout, rows = _small_layout([a.shape for a in arrays], row_multiple)

    def body(*refs):
        out = refs[-1]
        out[...] = jnp.zeros_like(out)
        for ref, (r0, r, c, rp) in zip(refs[:-1], layout):
            if c <= LANE:
                out[r0:r0 + r, 0:c] = ref[...]
            else:
                for q in range(c // LANE):
                    out[r0 + q * rp:r0 + q * rp + r, :] = ref[:, q * LANE:(q + 1) * LANE]

    return pl.pallas_call(
        body, name=name, out_shape=jax.ShapeDtypeStruct((rows, LANE), F32),
        compiler_params=pltpu.CompilerParams(vmem_limit_bytes=VMEM_LIMIT),
    )(*[a.reshape(_view2d(a.shape)) for a in arrays])


def unpack_small(name, packed, shapes):
    layout, _ = _small_layout(shapes, 8)

    def body(p_ref, *outs):
        for ref, (r0, r, c, rp) in zip(outs, layout):
            if c <= LANE:
                ref[...] = p_ref[r0:r0 + r, 0:c]
            else:
                for q in range(c // LANE):
                    ref[:, q * LANE:(q + 1) * LANE] = p_ref[r0 + q * rp:r0 + q * rp + r, :]

    res = pl.pallas_call(
        body, name=name, out_shape=[jax.ShapeDtypeStruct(_view2d(s), F32) for s in shapes],
        compiler_params=pltpu.CompilerParams(vmem_limit_bytes=VMEM_LIMIT),
    )(packed)
    return [a.reshape(s) for a, s in zip(res, shapes)]


def _step(x, target, w, m, v):
    mx, my, mc = lax.axis_index("x"), lax.axis_index("y"), lax.axis_index("c")
    me = (2 * mx + my).astype(jnp.int32)
    mc = mc.astype(jnp.int32)

    gathered = gather_shards("gather_weights", [w[n].astype(MMT) for n in BIG] + [w[n] for n in SHARDED_SMALL])
    W = dict(wg=gathered[0], wu=gathered[1], wd=gathered[2], win=gathered[3],
             pfull=gathered[4].transpose(0, 2, 3, 1, 4).reshape(DEPTH, 3, BW, D_MODEL),
             woutfull=gathered[5].reshape(DEPTH, D_MODEL, D_MODEL),
             gluw=gathered[6].reshape(DEPTH, BW, BW),
             nw=gathered[7].transpose(0, 2, 1, 3).reshape(DEPTH, 3, 1, D_MODEL),
             convw=gathered[8].transpose(0, 2, 1, 3).reshape(DEPTH, CONV_W, BW))
    loss, dx, big, small = local_step(x[0], target[0], W, {k: w[k] for k in SMALL_RAW})

    small_packed = pack_small("pack_small_grads", [small[n] for n in SMALL], NSH * 32)
    dq = D_MODEL // NSH
    big["branch_proj"] = big["branch_proj"].reshape(DEPTH, 3, BW, NSH, dq).transpose(0, 3, 1, 2, 4).reshape(DEPTH, NSH, 3 * BW, dq)
    grads = [big[n].reshape(big[n].shape[0], NSH, -1, big[n].shape[-1]) for n in BIG] + [small_packed.reshape(1, NSH, -1, LANE)]
    from_sibling = exchange_halves("reduce_cores", grads)
    merge = lambda a: a.reshape((-1,) + a.shape[2:])
    wire = [jnp.bfloat16] * len(BIG) + [F32]
    halves = [add_own_half(f"sum_cores_{i}", merge(g), merge(r), mc, wire[i]).reshape(r.shape)
              for i, (g, r) in enumerate(zip(grads, from_sibling))]
    from_chips = scatter_to_chips("reduce_chips", halves)
    own = [add_chips(f"sum_chips_{i}", h, r, me) for i, (h, r) in enumerate(zip(halves, from_chips))]
    other = share_halves("reduce_share", own)

    g, delta, new_m, new_v = {}, {}, {}, {}
    for i, n in enumerate(BIG):
        view = lambda a: a.reshape(own[i].shape[0], -1, own[i].shape[2])
        res = adamw_halves(f"adamw_{n}", view(w[n]), view(m[n]), view(v[n]), own[i], other[i], mc)
        g[n], delta[n], new_m[n], new_v[n] = [a.reshape(w[n].shape) for a in res]

    piece = jnp.stack([jnp.where(mc == 0, own[-1][0], other[-1][0]), jnp.where(mc == 0, other[-1][0], own[-1][0])])
    (all_small,) = gather_shards("gather_small", [piece])
    full_small = unpack_small("unpack_small_grads", all_small.transpose(1, 0, 2, 3).reshape(-1, LANE),
                              [small[n].shape for n in SMALL])
    g.update(zip(SMALL, full_small))
    g["norm_w"] = lax.dynamic_slice_in_dim(g["norm_w"], me * (D_MODEL // NSH), D_MODEL // NSH, axis=2)
    g["rg_conv_w"] = lax.dynamic_slice_in_dim(g["rg_conv_w"], me * (BW // NSH), BW // NSH, axis=2)

    packed = [pack_small(f"pack_small_{tag}", [src[n] for n in SMALL], 8)
              for tag, src in (("w", w), ("g", g), ("m", m), ("v", v))]
    for tag, dst, flat in zip(("delta", "m", "v"), (delta, new_m, new_v), adamw(*packed)):
        dst.update(zip(SMALL, unpack_small(f"unpack_small_{tag}", flat, [w[n].shape for n in SMALL])))

    total = lax.psum(loss[0, 0], ("x", "y", "c"))
    return (total, dx[None], *[g[n] for n in WEIGHTS], *[delta[n] for n in WEIGHTS],
            *[new_m[n] for n in WEIGHTS], *[new_v[n] for n in WEIGHTS])


def kernel(x, norm_w, final_norm_w, ffn_gate, ffn_up, ffn_down, w_in, branch_proj, w_out, s5_lambda_re, s5_lambda_im, s5_log_dt, s5_b_re, s5_b_im, s5_c_re, s5_c_im, s5_d, s5_glu_w, s5_glu_b, hg_lb_logits, hg_norm_w, rg_conv_w, rg_conv_b, rg_wa, rg_ba, rg_wx, rg_bx, rg_lambda, loss_target, m_norm_w, m_final_norm_w, m_ffn_gate, m_ffn_up, m_ffn_down, m_w_in, m_branch_proj, m_w_out, m_s5_lambda_re, m_s5_lambda_im, m_s5_log_dt, m_s5_b_re, m_s5_b_im, m_s5_c_re, m_s5_c_im, m_s5_d, m_s5_glu_w, m_s5_glu_b, m_hg_lb_logits, m_hg_norm_w, m_rg_conv_w, m_rg_conv_b, m_rg_wa, m_rg_ba, m_rg_wx, m_rg_bx, m_rg_lambda, v_norm_w, v_final_norm_w, v_ffn_gate, v_ffn_up, v_ffn_down, v_w_in, v_branch_proj, v_w_out, v_s5_lambda_re, v_s5_lambda_im, v_s5_log_dt, v_s5_b_re, v_s5_b_im, v_s5_c_re, v_s5_c_im, v_s5_d, v_s5_glu_w, v_s5_glu_b, v_hg_lb_logits, v_hg_norm_w, v_rg_conv_w, v_rg_conv_b, v_rg_wa, v_rg_ba, v_rg_wx, v_rg_bx, v_rg_lambda):
    ws = (norm_w, final_norm_w, ffn_gate, ffn_up, ffn_down, w_in, branch_proj, w_out, s5_lambda_re, s5_lambda_im, s5_log_dt, s5_b_re, s5_b_im, s5_c_re, s5_c_im, s5_d, s5_glu_w, s5_glu_b, hg_lb_logits, hg_norm_w, rg_conv_w, rg_conv_b, rg_wa, rg_ba, rg_wx, rg_bx, rg_lambda)
    ms = (m_norm_w, m_final_norm_w, m_ffn_gate, m_ffn_up, m_ffn_down, m_w_in, m_branch_proj, m_w_out, m_s5_lambda_re, m_s5_lambda_im, m_s5_log_dt, m_s5_b_re, m_s5_b_im, m_s5_c_re, m_s5_c_im, m_s5_d, m_s5_glu_w, m_s5_glu_b, m_hg_lb_logits, m_hg_norm_w, m_rg_conv_w, m_rg_conv_b, m_rg_wa, m_rg_ba, m_rg_wx, m_rg_bx, m_rg_lambda)
    vs = (v_norm_w, v_final_norm_w, v_ffn_gate, v_ffn_up, v_ffn_down, v_w_in, v_branch_proj, v_w_out, v_s5_lambda_re, v_s5_lambda_im, v_s5_log_dt, v_s5_b_re, v_s5_b_im, v_s5_c_re, v_s5_c_im, v_s5_d, v_s5_glu_w, v_s5_glu_b, v_hg_lb_logits, v_hg_norm_w, v_rg_conv_w, v_rg_conv_b, v_rg_wa, v_rg_ba, v_rg_wx, v_rg_bx, v_rg_lambda)
    return _step(x, loss_target, dict(zip(WEIGHTS, ws)), dict(zip(WEIGHTS, ms)), dict(zip(WEIGHTS, vs)))
```

```python
import functools
import math
from typing import NamedTuple

import jax
import jax.numpy as jnp
from jax import lax
from jax.experimental import pallas as pl
from jax.experimental.pallas import tpu as pltpu

F32 = jnp.float32
MMT = jnp.bfloat16
HI = lax.Precision.HIGHEST

D_MODEL = 1024
BW = 512
S5_GROUP, S5_GROUPS, S5_STATE = 16, 32, 64
S5_N = S5_GROUPS * S5_STATE
HG_HEADS, HG_D = 4, 128
HG_CHUNK = 128
RG_BLOCKS, RG_BLOCK = 8, 64
RG_C = 8.0
CONV_W = 4
D_FF = 2816
EPS = 1e-6
IN_TOTAL = 6656
NSH = 4
NSEG = 8
LANE = 128
VMEM_LIMIT = 56 * 1024 * 1024

ADAM_LR, ADAM_B1, ADAM_B2, ADAM_EPS, ADAM_WD, ADAM_STEP = 0.001, 0.9, 0.999, 1e-08, 0.01, 10

MESH = pl.DeviceIdType.MESH


class WP(NamedTuple):
    w: jax.Array
    p: jax.Array


def _dg(a, b, ca, cb):
    return lax.dot_general(a, b, (((ca,), (cb,)), ((), ())), preferred_element_type=F32)


@jax.custom_vjp
def _mmw(a, w, p):
    return _dg(a.astype(MMT), w, 1, 0)


def _mmw_fwd(a, w, p):
    return _mmw(a, w, p), (a, w)


def _mmw_bwd(res, g):
    a, w = res
    gb = g.astype(MMT)
    return _dg(gb, w, 1, 1), jnp.zeros_like(w), _dg(a.astype(MMT), gb, 0, 0)


_mmw.defvjp(_mmw_fwd, _mmw_bwd)


def mm(a, w):
    if isinstance(w, WP):
        return _mmw(a, w.w, w.p)
    return _dg(a.astype(MMT), w, 1, 0)


@jax.custom_vjp
def mma_nn(a, b):
    return _dg(a.astype(MMT), b.astype(MMT), 1, 0)


def _nn_f(a, b):
    return mma_nn(a, b), (a, b)


def _nn_b(res, g):
    a, b = res
    gb = g.astype(MMT)
    return _dg(gb, b.astype(MMT), 1, 1), _dg(a.astype(MMT), gb, 0, 0)


mma_nn.defvjp(_nn_f, _nn_b)


@jax.custom_vjp
def mma_nt(a, b):
    return _dg(a.astype(MMT), b.astype(MMT), 1, 1)


def _nt_f(a, b):
    return mma_nt(a, b), (a, b)


def _nt_b(res, g):
    a, b = res
    gb = g.astype(MMT)
    return _dg(gb, b.astype(MMT), 1, 0), _dg(gb, a.astype(MMT), 0, 0)


mma_nt.defvjp(_nt_f, _nt_b)


@jax.custom_vjp
def mma_tn(a, b):
    return _dg(a.astype(MMT), b.astype(MMT), 0, 0)


def _tn_f(a, b):
    return mma_tn(a, b), (a, b)


def _tn_b(res, g):
    a, b = res
    gb = g.astype(MMT)
    return _dg(b.astype(MMT), gb, 1, 1), _dg(a.astype(MMT), gb, 1, 0)


mma_tn.defvjp(_tn_f, _tn_b)


def mm_exact(m, x):
    return jnp.dot(m, x, precision=HI, preferred_element_type=F32)


def _rms(x, w):
    return x * lax.rsqrt(jnp.mean(x * x, axis=-1, keepdims=True) + EPS) * w


def _expm1(x):
    series = x * (1.0 + x * (1.0 / 2) * (1.0 + x * (1.0 / 3) * (1.0 + x * (1.0 / 4) * (1.0 + x * (1.0 / 5) * (1.0 + x * (1.0 / 6))))))
    return jnp.where(jnp.abs(x) < 0.1, series, jnp.exp(x) - 1.0)


def _bspec(block, fn, order):
    if order == "is":
        return pl.BlockSpec(block, lambda i, s: fn(s, i))
    return pl.BlockSpec(block, lambda s, i: fn(s, i))


def tile_fwd(fn, name, n_i, n_s, ins, outs, s_outer=False):
    n_in = len(ins)
    order = "si" if s_outer else "is"
    assert not (s_outer and any(o[4] for o in outs))

    def body(*refs):
        s = pl.program_id(0 if s_outer else 1)
        res = fn(*[r[...] for r in refs[:n_in]], s)
        for o_ref, val, spec in zip(refs[n_in:], res, outs):
            if spec[4] and n_s > 1:
                @pl.when(s == 0)
                def _(o_ref=o_ref, val=val):
                    o_ref[...] = val.astype(o_ref.dtype)

                @pl.when(s != 0)
                def _(o_ref=o_ref, val=val):
                    o_ref[...] += val.astype(o_ref.dtype)
            else:
                o_ref[...] = val.astype(o_ref.dtype)

    return pl.pallas_call(
        body, grid=(n_s, n_i) if s_outer else (n_i, n_s), name=name,
        in_specs=[_bspec(b, f, order) for _, b, f in ins],
        out_specs=[_bspec(b, f, order) for _, _, b, f, _ in outs],
        out_shape=[jax.ShapeDtypeStruct(sh, dt) for sh, dt, _, _, _ in outs],
        compiler_params=pltpu.CompilerParams(vmem_limit_bytes=VMEM_LIMIT,
                                             dimension_semantics=("arbitrary", "arbitrary")),
    )(*[a for a, _, _ in ins])


def tile_bwd(fn, name, n_i, n_s, ins, cts, gouts):
    groups = [c if isinstance(c, list) else [c] for c in cts]
    cts = [blk for grp in groups for blk in grp]
    n_in, n_ct = len(ins), len(cts)
    kinds = [k for _, _, _, k in ins]
    d_pos = [j for j, k in enumerate(kinds) if k != "c"]
    shared = [(gi, spec[4]) for gi, spec in enumerate(gouts) if len(spec) == 5 and spec[4] is not None]
    n_sh = len(shared)

    def body(*refs):
        s, i = pl.program_id(0), pl.program_id(1)
        vals = [r[...] for r in refs[:n_in]]
        ct_refs, ctv = list(refs[n_in:n_in + n_ct]), []
        for grp in groups:
            parts = [ct_refs.pop(0)[...] for _ in grp]
            ctv.append(parts[0] if len(parts) == 1 else jnp.concatenate(parts, axis=1))
        ctv = tuple(ctv)
        g_refs = refs[n_in + n_ct + n_sh:]

        def g(*dv):
            args = list(vals)
            for j, v in zip(d_pos, dv):
                args[j] = WP(vals[j], v) if kinds[j] == "w" else v
            return tuple(fn(*args))

        dv0 = [jnp.zeros(vals[j].shape, F32) if kinds[j] == "w" else vals[j] for j in d_pos]
        _, vjp = jax.vjp(g, *dv0)
        grads = vjp(ctv)
        for g_ref, gv, spec in zip(g_refs, grads, gouts):
            mode = spec[3]
            if mode == "write":
                g_ref[...] = gv.astype(g_ref.dtype)
            else:
                first = (i == 0) if mode == "acc_i" else jnp.logical_and(i == 0, s == 0)

                @pl.when(first)
                def _(g_ref=g_ref, gv=gv):
                    g_ref[...] = gv.astype(g_ref.dtype)

                @pl.when(jnp.logical_not(first))
                def _(g_ref=g_ref, gv=gv):
                    g_ref[...] += gv.astype(g_ref.dtype)

    return pl.pallas_call(
        body, grid=(n_s, n_i), name=name,
        in_specs=([_bspec(b, f, "si") for _, b, f, _ in ins] + [_bspec(b, f, "si") for _, b, f in cts]
                  + [pl.BlockSpec(memory_space=pl.ANY)] * n_sh),
        out_specs=[_bspec(spec[1], spec[2], "si") for spec in gouts],
        out_shape=[jax.ShapeDtypeStruct(spec[0], F32) for spec in gouts],
        input_output_aliases={n_in + n_ct + k: gi for k, (gi, _) in enumerate(shared)},
        compiler_params=pltpu.CompilerParams(vmem_limit_bytes=VMEM_LIMIT,
                                             dimension_semantics=("arbitrary", "arbitrary")),
    )(*[a for a, _, _, _ in ins], *[a for a, _, _ in cts], *[buf for _, buf in shared])


def _row_tile(rows, width, itemsize=4, budget=2 * 1024 * 1024, mult=8):
    best = mult
    for t in range(mult, rows + 1, mult):
        if rows % t == 0 and t * width * itemsize <= budget:
            best = t
    return best


def add_n(name, terms, shape):
    rows, cols = shape
    tr = _row_tile(rows, cols)

    def body(*refs):
        acc = refs[0][...]
        for r in refs[1:-1]:
            acc = acc + r[...]
        refs[-1][...] = acc

    specs = []
    for _, lead in terms:
        specs.append(pl.BlockSpec((None,) * len(lead) + (tr, cols), functools.partial(lambda i, lead: (*lead, i, 0), lead=lead)))
    return pl.pallas_call(
        body, grid=(rows // tr,), name=name, in_specs=specs,
        out_specs=pl.BlockSpec((tr, cols), lambda i: (i, 0)),
        out_shape=jax.ShapeDtypeStruct((rows, cols), F32),
    )(*[a for a, _ in terms])


def ffn_core(x, nw, wg, wu, wd):
    h = _rms(x, nw)
    return (0.5 * mm(jax.nn.silu(mm(h, wg)) * mm(h, wu), wd),)


def pre_core(x, nw, win):
    return (mm(_rms(x, nw), win),)


def _split_lanes(y):
    return jnp.stack([y[:, k * LANE:(k + 1) * LANE] for k in range(y.shape[1] // LANE)], axis=0)


def _join_lanes(y3):
    return jnp.concatenate([y3[k] for k in range(y3.shape[0])], axis=1)


def s5_pre_core(u, bmat):
    bu = mm(u, bmat)
    return _split_lanes(bu[:, :S5_N]), _split_lanes(bu[:, S5_N:])


def mid_core(xr, xi, u, o, g, hs, gc, hmat, cmat, d, gluw, glub, hgw):
    xs = jnp.concatenate([_join_lanes(xr), _join_lanes(xi)], axis=1)
    y = mm(xs, cmat) + d * u
    z = jax.nn.gelu(y)
    ya = z * jax.nn.sigmoid(mm(z, gluw) + glub)
    ms = mm_exact(o * o, hmat)
    yb = o * lax.rsqrt(ms + EPS) * hgw * jax.nn.silu(g)
    yc = hs * jax.nn.gelu(gc)
    return ya, yb, yc


def _sub(w, n):
    return WP(w.w[n], w.p[n]) if isinstance(w, WP) else w[n]


def merge_core(ya, yb, yc, g0, g1, g2, g3, g4, g5, p, wout):
    gate = lambda a, b: jax.nn.sigmoid(jnp.concatenate([a, b], axis=1))
    m = gate(g0, g1) * mm(ya, _sub(p, 0)) + gate(g2, g3) * mm(yb, _sub(p, 1)) + gate(g4, g5) * mm(yc, _sub(p, 2))
    return (mm(m, wout),)


def gates_core(xc, wa, ba, wx, bx, lam):
    r = jax.nn.sigmoid(mm(xc, wa) + ba)
    i = jax.nn.sigmoid(mm(xc, wx) + bx)
    log_a = -RG_C * jax.nn.softplus(-lam) * r
    a = jnp.exp(log_a)
    b = jnp.sqrt(-_expm1(2.0 * log_a)) * (i * xc)
    return a, b


def _seg_rows(ref, k, j, n):
    rows = pl.ds(pl.multiple_of(j * NSEG, NSEG), NSEG)
    if k is None:
        return ref[rows, :]
    return ref[k, rows, :]


def _seg_store(ref, k, j, n, val):
    rows = pl.ds(pl.multiple_of(j * NSEG, NSEG), NSEG)
    if k is None:
        ref[rows, :] = val
    else:
        ref[k, rows, :] = val


def _seg_carries(er, ei, pr, pi, reverse):
    rows = lax.broadcasted_iota(jnp.int32, er.shape, 0)
    cr = jnp.zeros_like(er)
    ci = None if ei is None else jnp.zeros_like(er)
    order = range(NSEG - 2, -1, -1) if reverse else range(1, NSEG)
    shift = NSEG - 1 if reverse else 1
    for s in order:
        if ei is None:
            tr = er + pr * cr
            cr = jnp.where(rows == s, pltpu.roll(tr, shift, 0), cr)
        else:
            tr = er + pr * cr - pi * ci
            ti = ei + pr * ci + pi * cr
            cr = jnp.where(rows == s, pltpu.roll(tr, shift, 0), cr)
            ci = jnp.where(rows == s, pltpu.roll(ti, shift, 0), ci)
    return cr, ci


S5_K = 2


def s5_scan_fwd(bur, bui, ar, ai, L):
    n = L // NSEG
    nb = S5_N // LANE
    K = S5_K

    def body(br_ref, bi_ref, ar_ref, ai_ref, xr_ref, xi_ref):
        zero = jnp.zeros((NSEG, LANE), F32)
        A = [(jnp.broadcast_to(ar_ref[k], (NSEG, LANE)), jnp.broadcast_to(ai_ref[k], (NSEG, LANE))) for k in range(K)]

        def p1(j, st):
            new = []
            for k in range(K):
                sr, si, pr, pi = st[k]
                a_r, a_i = A[k]
                nr = a_r * sr - a_i * si + _seg_rows(br_ref, k, j, n)
                ni = a_r * si + a_i * sr + _seg_rows(bi_ref, k, j, n)
                _seg_store(xr_ref, k, j, n, nr)
                _seg_store(xi_ref, k, j, n, ni)
                new.append((nr, ni, a_r * pr - a_i * pi, a_r * pi + a_i * pr))
            return tuple(new)

        st = lax.fori_loop(0, n, p1, tuple((zero, zero, zero + 1.0, zero) for _ in range(K)))
        C = [_seg_carries(st[k][0], st[k][1], st[k][2], st[k][3], False) for k in range(K)]

        def p2(j, st):
            new = []
            for k in range(K):
                pr, pi = st[k]
                a_r, a_i = A[k]
                pr, pi = a_r * pr - a_i * pi, a_r * pi + a_i * pr
                cr, ci = C[k]
                _seg_store(xr_ref, k, j, n, _seg_rows(xr_ref, k, j, n) + pr * cr - pi * ci)
                _seg_store(xi_ref, k, j, n, _seg_rows(xi_ref, k, j, n) + pr * ci + pi * cr)
                new.append((pr, pi))
            return tuple(new)

        lax.fori_loop(0, n, p2, tuple((zero + 1.0, zero) for _ in range(K)))

    blk = pl.BlockSpec((K, L, LANE), lambda g: (g, 0, 0))
    ablk = pl.BlockSpec((K, 1, LANE), lambda g: (g, 0, 0))
    return pl.pallas_call(
        body, grid=(nb // K,), name="s5_scan_fwd",
        in_specs=[blk, blk, ablk, ablk], out_specs=[blk, blk],
        out_shape=[jax.ShapeDtypeStruct((nb, L, LANE), F32)] * 2,
        compiler_params=pltpu.CompilerParams(vmem_limit_bytes=VMEM_LIMIT),
    )(bur, bui, ar, ai)


def s5_scan_bwd(dxr, dxi, xr, xi, ar, ai, L):
    n = L // NSEG
    nb = S5_N // LANE
    K = S5_K

    def body(dr_ref, di_ref, xr_ref, xi_ref, ar_ref, ai_ref, gr_ref, gi_ref, dar_ref, dai_ref):
        zero = jnp.zeros((NSEG, LANE), F32)
        rows = lax.broadcasted_iota(jnp.int32, (NSEG, LANE), 0)
        A = [(jnp.broadcast_to(ar_ref[k], (NSEG, LANE)), -jnp.broadcast_to(ai_ref[k], (NSEG, LANE))) for k in range(K)]

        def p1(jj, st):
            j = n - 1 - jj
            new = []
            for k in range(K):
                sr, si, pr, pi = st[k]
                a_r, a_i = A[k]
                nr = a_r * sr - a_i * si + _seg_rows(dr_ref, k, j, n)
                ni = a_r * si + a_i * sr + _seg_rows(di_ref, k, j, n)
                _seg_store(gr_ref, k, j, n, nr)
                _seg_store(gi_ref, k, j, n, ni)
                new.append((nr, ni, a_r * pr - a_i * pi, a_r * pi + a_i * pr))
            return tuple(new)

        st = lax.fori_loop(0, n, p1, tuple((zero, zero, zero + 1.0, zero) for _ in range(K)))
        C = [_seg_carries(st[k][0], st[k][1], st[k][2], st[k][3], True) for k in range(K)]
        xb = [(jnp.where(rows == 0, 0.0, pltpu.roll(_seg_rows(xr_ref, k, n - 1, n), 1, 0)),
               jnp.where(rows == 0, 0.0, pltpu.roll(_seg_rows(xi_ref, k, n - 1, n), 1, 0))) for k in range(K)]

        def p2(jj, st):
            j = n - 1 - jj
            jp = jnp.maximum(j - 1, 0)
            new = []
            for k in range(K):
                pr, pi, acr, aci = st[k]
                a_r, a_i = A[k]
                pr, pi = a_r * pr - a_i * pi, a_r * pi + a_i * pr
                cr, ci = C[k]
                g_r = _seg_rows(gr_ref, k, j, n) + pr * cr - pi * ci
                g_i = _seg_rows(gi_ref, k, j, n) + pr * ci + pi * cr
                _seg_store(gr_ref, k, j, n, g_r)
                _seg_store(gi_ref, k, j, n, g_i)
                xpr = jnp.where(j == 0, xb[k][0], _seg_rows(xr_ref, k, jp, n))
                xpi = jnp.where(j == 0, xb[k][1], _seg_rows(xi_ref, k, jp, n))
                new.append((pr, pi, acr + g_r * xpr + g_i * xpi, aci + g_i * xpr - g_r * xpi))
            return tuple(new)

        st = lax.fori_loop(0, n, p2, tuple((zero + 1.0, zero, zero, zero) for _ in range(K)))
        for k in range(K):
            dar_ref[k] = jnp.sum(st[k][2], axis=0, keepdims=True)
            dai_ref[k] = jnp.sum(st[k][3], axis=0, keepdims=True)

    blk = pl.BlockSpec((K, L, LANE), lambda g: (g, 0, 0))
    ablk = pl.BlockSpec((K, 1, LANE), lambda g: (g, 0, 0))
    return pl.pallas_call(
        body, grid=(nb // K,), name="s5_scan_bwd",
        in_specs=[blk, blk, blk, blk, ablk, ablk], out_specs=[blk, blk, ablk, ablk],
        out_shape=[jax.ShapeDtypeStruct((nb, L, LANE), F32)] * 2 + [jax.ShapeDtypeStruct((nb, 1, LANE), F32)] * 2,
        compiler_params=pltpu.CompilerParams(vmem_limit_bytes=VMEM_LIMIT),
    )(dxr, dxi, xr, xi, ar, ai)


def rg_scan_fwd(a, b, L):
    n = L // NSEG

    def body(a_ref, b_ref, h_ref):
        zero = jnp.zeros((NSEG, LANE), F32)

        def p1(j, st):
            h, p = st
            aj = _seg_rows(a_ref, None, j, n)
            h = aj * h + _seg_rows(b_ref, None, j, n)
            _seg_store(h_ref, None, j, n, h)
            return h, aj * p

        e, pe = lax.fori_loop(0, n, p1, (zero, zero + 1.0))
        c, _ = _seg_carries(e, None, pe, None, False)

        def p2(j, p):
            p = _seg_rows(a_ref, None, j, n) * p
            _seg_store(h_ref, None, j, n, _seg_rows(h_ref, None, j, n) + p * c)
            return p

        lax.fori_loop(0, n, p2, zero + 1.0)

    blk = pl.BlockSpec((L, LANE), lambda g: (0, g))
    return pl.pallas_call(
        body, grid=(BW // LANE,), name="rg_scan_fwd", in_specs=[blk, blk], out_specs=blk,
        out_shape=jax.ShapeDtypeStruct((L, BW), F32),
        compiler_params=pltpu.CompilerParams(vmem_limit_bytes=VMEM_LIMIT),
    )(a, b)


def rg_scan_bwd(a, h, dh, L):
    n = L // NSEG

    def body(a_ref, h_ref, dh_ref, da_ref, db_ref):
        zero = jnp.zeros((NSEG, LANE), F32)
        rows = lax.broadcasted_iota(jnp.int32, (NSEG, LANE), 0)
        a_edge = jnp.where(rows == NSEG - 1, 0.0, pltpu.roll(_seg_rows(a_ref, None, 0, n), NSEG - 1, 0))
        h_edge = jnp.where(rows == 0, 0.0, pltpu.roll(_seg_rows(h_ref, None, n - 1, n), 1, 0))

        def mult(j):
            return jnp.where(j == n - 1, a_edge, _seg_rows(a_ref, None, jnp.minimum(j + 1, n - 1), n))

        def p1(jj, st):
            j = n - 1 - jj
            g, p = st
            m = mult(j)
            g = m * g + _seg_rows(dh_ref, None, j, n)
            _seg_store(db_ref, None, j, n, g)
            return g, m * p

        e, pe = lax.fori_loop(0, n, p1, (zero, zero + 1.0))
        c, _ = _seg_carries(e, None, pe, None, True)

        def p2(jj, p):
            j = n - 1 - jj
            p = mult(j) * p
            g = _seg_rows(db_ref, None, j, n) + p * c
            _seg_store(db_ref, None, j, n, g)
            hp = jnp.where(j == 0, h_edge, _seg_rows(h_ref, None, jnp.maximum(j - 1, 0), n))
            _seg_store(da_ref, None, j, n, g * hp)
            return p

        lax.fori_loop(0, n, p2, zero + 1.0)

    blk = pl.BlockSpec((L, LANE), lambda g: (0, g))
    return pl.pallas_call(
        body, grid=(BW // LANE,), name="rg_scan_bwd", in_specs=[blk, blk, blk], out_specs=[blk, blk],
        out_shape=[jax.ShapeDtypeStruct((L, BW), F32)] * 2,
        compiler_params=pltpu.CompilerParams(vmem_limit_bytes=VMEM_LIMIT),
    )(a, h, dh)


def _hg_consts(C):
    t = lax.broadcasted_iota(jnp.int32, (C, C), 0)
    s = lax.broadcasted_iota(jnp.int32, (C, C), 1)
    tril = (s <= t).astype(F32)
    diag = (s == t).astype(F32)
    levels = []
    k = 1
    while (1 << k) <= C:
        m = 1 << (k - 1)
        same = (t >> k) == (s >> k)
        t_right = ((t >> (k - 1)) & 1) == 1
        s_left = ((s >> (k - 1)) & 1) == 0
        mask = jnp.logical_and(same, jnp.logical_and(t_right, s_left)).astype(F32)
        bnd = ((t >> k) << k) + (m - 1)
        levels.append((mask, (s <= bnd).astype(F32)))
        k += 1
    return tril, diag, levels


def hg_chunk(st, q, z, v, lb):
    C = q.shape[0]
    tril, diag, levels = _hg_consts(C)
    sig = jax.nn.sigmoid(z)
    lf = jnp.log(lb + (1.0 - lb) * sig)
    k = (1.0 - lb) * jax.nn.sigmoid(-z)
    qh = jax.nn.silu(q)
    b = mm_exact(tril, lf)
    blast = jnp.sum(lf, axis=0, keepdims=True)
    qe = qh * jnp.exp(b)
    kd = k * jnp.exp(blast - b)
    scaled = []
    for _, sel in levels:
        ref = mm_exact(sel, lf)
        scaled.append((qh * jnp.exp(jnp.minimum(b - ref, 0.0)), k * jnp.exp(jnp.minimum(ref - b, 0.0))))
    outs, news = [], []
    for h in range(HG_HEADS):
        sl = slice(h * HG_D, (h + 1) * HG_D)
        st_h = st[h * HG_D:(h + 1) * HG_D, :]
        sc = diag * mma_nt(qh[:, sl], k[:, sl])
        for (mask, _), (qt, kt) in zip(levels, scaled):
            sc = sc + mask * mma_nt(qt[:, sl], kt[:, sl])
        outs.append(mma_nt(qe[:, sl], st_h) + mma_nn(sc, v[:, sl]))
        news.append(st_h * jnp.exp(blast[:, sl]) + mma_tn(v[:, sl], kd[:, sl]))
    return jnp.concatenate(news, axis=0), jnp.concatenate(outs, axis=1)


def hg_fwd(qzv, lb, L):
    C = HG_CHUNK
    nc = L // C

    def body(q_ref, z_ref, v_ref, lb_ref, o_ref, sst_ref, st_ref):
        @pl.when(pl.program_id(0) == 0)
        def _():
            st_ref[...] = jnp.zeros_like(st_ref)

        st = st_ref[...]
        sst_ref[...] = st
        new, o = hg_chunk(st, q_ref[...], z_ref[...], v_ref[...], lb_ref[...])
        st_ref[...] = new
        o_ref[...] = o

    col = lambda cb: pl.BlockSpec((C, BW), functools.partial(lambda c, cb: (c, cb), cb=cb))
    return pl.pallas_call(
        body, grid=(nc,), name="hg_fwd",
        in_specs=[col(0), col(1), col(2), pl.BlockSpec((1, BW), lambda c: (0, 0))],
        out_specs=[pl.BlockSpec((C, BW), lambda c: (c, 0)), pl.BlockSpec((None, BW, HG_D), lambda c: (c, 0, 0))],
        out_shape=[jax.ShapeDtypeStruct((L, BW), F32), jax.ShapeDtypeStruct((nc, BW, HG_D), F32)],
        scratch_shapes=[pltpu.VMEM((BW, HG_D), F32)],
        compiler_params=pltpu.CompilerParams(vmem_limit_bytes=VMEM_LIMIT, dimension_semantics=("arbitrary",)),
    )(qzv, qzv, qzv, lb)


def hg_bwd(qzv, lb, sst, do, L):
    C = HG_CHUNK
    nc = L // C

    def body(q_ref, z_ref, v_ref, lb_ref, sst_ref, do_ref, dq_ref, dz_ref, dv_ref, dlb_ref, dst_ref):
        @pl.when(pl.program_id(0) == 0)
        def _():
            dst_ref[...] = jnp.zeros_like(dst_ref)
            dlb_ref[...] = jnp.zeros_like(dlb_ref)

        _, vjp = jax.vjp(hg_chunk, sst_ref[...], q_ref[...], z_ref[...], v_ref[...], lb_ref[...])
        dst, dq, dz, dv, dlb = vjp((dst_ref[...], do_ref[...]))
        dst_ref[...] = dst
        dq_ref[...] = dq
        dz_ref[...] = dz
        dv_ref[...] = dv
        dlb_ref[...] += dlb

    col = lambda cb: pl.BlockSpec((C, BW), functools.partial(lambda c, cb: (nc - 1 - c, cb), cb=cb))
    rev = pl.BlockSpec((C, BW), lambda c: (nc - 1 - c, 0))
    return pl.pallas_call(
        body, grid=(nc,), name="hg_bwd",
        in_specs=[col(0), col(1), col(2), pl.BlockSpec((1, BW), lambda c: (0, 0)),
                  pl.BlockSpec((None, BW, HG_D), lambda c: (nc - 1 - c, 0, 0)), rev],
        out_specs=[rev, rev, rev, pl.BlockSpec((1, BW), lambda c: (0, 0))],
        out_shape=[jax.ShapeDtypeStruct((L, BW), F32)] * 3 + [jax.ShapeDtypeStruct((1, BW), F32)],
        scratch_shapes=[pltpu.VMEM((BW, HG_D), F32)],
        compiler_params=pltpu.CompilerParams(vmem_limit_bytes=VMEM_LIMIT, dimension_semantics=("arbitrary",)),
    )(qzv, qzv, qzv, lb, sst, do)


def _shift_down(x, d, rows, L):
    if d == 0:
        return x
    wrapped = jnp.where((rows & (NSEG - 1)) == 0, 0.0, pltpu.roll(x, NSEG * d + 1, 0))
    return jnp.where(rows < NSEG * d, wrapped, pltpu.roll(x, NSEG * d, 0))


def _shift_up(x, d, rows, L):
    if d == 0:
        return x
    wrapped = jnp.where((rows & (NSEG - 1)) == NSEG - 1, 0.0, pltpu.roll(x, L - (NSEG * d + 1), 0))
    return jnp.where(rows >= L - NSEG * d, wrapped, pltpu.roll(x, L - NSEG * d, 0))


def conv_fwd(proj, w, b, L):
    def body(x_ref, w_ref, b_ref, o_ref):
        x = x_ref[...]
        rows = lax.broadcasted_iota(jnp.int32, x.shape, 0)
        acc = jnp.broadcast_to(b_ref[...], x.shape)
        for k in range(CONV_W):
            acc = acc + w_ref[pl.ds(k, 1), :] * _shift_down(x, CONV_W - 1 - k, rows, L)
        o_ref[...] = acc

    nl = BW // LANE
    return pl.pallas_call(
        body, grid=(nl,), name="conv_fwd",
        in_specs=[pl.BlockSpec((L, LANE), lambda g: (0, 5 * nl + g)), pl.BlockSpec((CONV_W, LANE), lambda g: (0, g)),
                  pl.BlockSpec((1, LANE), lambda g: (0, g))],
        out_specs=pl.BlockSpec((L, LANE), lambda g: (0, g)),
        out_shape=jax.ShapeDtypeStruct((L, BW), F32),
        compiler_params=pltpu.CompilerParams(vmem_limit_bytes=VMEM_LIMIT),
    )(proj, w, b)


def conv_bwd(proj, w, dxc, L):
    def body(x_ref, w_ref, d_ref, dx_ref, dw_ref, db_ref):
        x, d = x_ref[...], d_ref[...]
        rows = lax.broadcasted_iota(jnp.int32, x.shape, 0)
        acc = jnp.zeros_like(x)
        for k in range(CONV_W):
            acc = acc + w_ref[pl.ds(k, 1), :] * _shift_up(d, CONV_W - 1 - k, rows, L)
            dw_ref[pl.ds(k, 1), :] = jnp.sum(d * _shift_down(x, CONV_W - 1 - k, rows, L), axis=0, keepdims=True)
        dx_ref[...] = acc
        db_ref[...] = jnp.sum(d, axis=0, keepdims=True)

    nl = BW // LANE
    blk = pl.BlockSpec((L, LANE), lambda g: (0, g))
    return pl.pallas_call(
        body, grid=(nl,), name="conv_bwd",
        in_specs=[pl.BlockSpec((L, LANE), lambda g: (0, 5 * nl + g)), pl.BlockSpec((CONV_W, LANE), lambda g: (0, g)), blk],
        out_specs=[blk, pl.BlockSpec((CONV_W, LANE), lambda g: (0, g)), pl.BlockSpec((1, LANE), lambda g: (0, g))],
        out_shape=[jax.ShapeDtypeStruct((L, BW), F32), jax.ShapeDtypeStruct((CONV_W, BW), F32),
                   jax.ShapeDtypeStruct((1, BW), F32)],
        compiler_params=pltpu.CompilerParams(vmem_limit_bytes=VMEM_LIMIT),
    )(proj, w, dxc)


def loss_fwd_bwd(x, fw, target, L, tm):
    def fn(x, fw, t):
        err = jnp.square(_rms(x, fw) - t)
        return jnp.sum(0.5 * jnp.mean(err, axis=-1, keepdims=True), axis=0, keepdims=True)

    def body(x_ref, fw_ref, t_ref, l_ref, dx_ref, dfw_ref):
        i = pl.program_id(0)
        t = t_ref[...]
        val, vjp = jax.vjp(lambda x, fw: fn(x, fw, t), x_ref[...], fw_ref[...])
        dx, dfw = vjp(jnp.ones((1, 1), F32))
        dx_ref[...] = dx

        @pl.when(i == 0)
        def _():
            l_ref[...] = jnp.zeros_like(l_ref)
            dfw_ref[...] = jnp.zeros_like(dfw_ref)

        l_ref[...] += jnp.broadcast_to(val, l_ref.shape)
        dfw_ref[...] += dfw

    row = pl.BlockSpec((tm, D_MODEL), lambda i: (i, 0))
    vec = pl.BlockSpec((1, D_MODEL), lambda i: (0, 0))
    return pl.pallas_call(
        body, grid=(L // tm,), name="loss_fwd_bwd", in_specs=[row, vec, row],
        out_specs=[pl.BlockSpec((1, LANE), lambda i: (0, 0)), row, vec],
        out_shape=[jax.ShapeDtypeStruct((1, LANE), F32), jax.ShapeDtypeStruct((L, D_MODEL), F32),
                   jax.ShapeDtypeStruct((1, D_MODEL), F32)],
        compiler_params=pltpu.CompilerParams(vmem_limit_bytes=VMEM_LIMIT, dimension_semantics=("arbitrary",)),
    )(x, fw, target)


def adamw(w, g, m, v):
    rows, cols = w.shape
    tr = _row_tile(rows, cols, budget=1024 * 1024)
    c1 = 1.0 - ADAM_B1 ** ADAM_STEP
    c2 = 1.0 - ADAM_B2 ** ADAM_STEP

    def body(w_ref, g_ref, m_ref, v_ref, d_ref, nm_ref, nv_ref):
        g = g_ref[...]
        nm = ADAM_B1 * m_ref[...] + (1.0 - ADAM_B1) * g
        nv = ADAM_B2 * v_ref[...] + (1.0 - ADAM_B2) * jnp.square(g)
        d_ref[...] = -ADAM_LR * ((nm / c1) / (jnp.sqrt(nv / c2) + ADAM_EPS) + ADAM_WD * w_ref[...])
        nm_ref[...] = nm
        nv_ref[...] = nv

    blk = pl.BlockSpec((tr, cols), lambda i: (i, 0))
    return pl.pallas_call(
        body, grid=(rows // tr,), name="adamw", in_specs=[blk] * 4, out_specs=[blk] * 3,
        out_shape=[jax.ShapeDtypeStruct((rows, cols), F32)] * 3,
    )(w, g, m, v)


def s5_prep(lam_re, lam_im, log_dt, b_re, b_im, c_re, c_im):
    lr = jnp.minimum(lam_re, -1e-4)
    li = lam_im
    dt = jnp.exp(log_dt)[:, None]
    mag = jnp.exp(lr * dt)
    ar = mag * jnp.cos(li * dt)
    ai = mag * jnp.sin(li * dt)
    den = lr * lr + li * li
    fr = ((ar - 1.0) * lr + ai * li) / den
    fi = (ai * lr - (ar - 1.0) * li) / den
    bbr = fr[..., None] * b_re - fi[..., None] * b_im
    bbi = fr[..., None] * b_im + fi[..., None] * b_re
    emb_b = lambda bb: _block_diag(bb.transpose(0, 2, 1).reshape(BW, S5_STATE), S5_GROUPS)
    emb_c = lambda cc: _block_diag(cc.transpose(0, 2, 1).reshape(S5_N, S5_GROUP), S5_GROUPS)
    bmat = jnp.concatenate([emb_b(bbr), emb_b(bbi)], axis=1)
    cmat = jnp.concatenate([emb_c(c_re), -emb_c(c_im)], axis=0)
    nb = S5_N // LANE
    return ar.reshape(nb, 1, LANE), ai.reshape(nb, 1, LANE), bmat, cmat


def _block_diag(stacked, groups):
    rows, c = stacked.shape
    r = rows // groups
    row_g = jnp.arange(rows)[:, None] // r
    col_g = jnp.arange(groups * c)[None, :] // c
    return jnp.where(row_g == col_g, jnp.tile(stacked, (1, groups)), 0.0)


def rg_prep(w):
    return _block_diag(w.reshape(BW, RG_BLOCK), RG_BLOCKS)


def hg_prep(logits):
    p = jax.nn.softmax(logits, axis=0)
    return jnp.cumsum(p, axis=0) - p[0]


def _head_mean_matrix():
    r = jnp.arange(BW) // HG_D
    return (r[:, None] == r[None, :]).astype(F32) / HG_D


def _to_segment_order(a):
    L = a.shape[0]
    return a.reshape(NSEG, L // NSEG, -1).transpose(1, 0, 2).reshape(a.shape)


def _to_time_order(a):
    L = a.shape[0]
    return a.reshape(L // NSEG, NSEG, -1).transpose(1, 0, 2).reshape(a.shape)


def _const(*idx):
    return lambda s, i: idx


def _rows(cb=0):
    return lambda s, i: (i, cb)


def _sum_parts(name, first, parts, shape):
    return add_n(name, [(first, ())] + [(parts, (s,)) for s in range(NSH)], shape)


def _ffn_weight_specs(l, j):
    F = D_FF // NSH
    one = pl.Buffered(1)
    return [pl.BlockSpec((None, NSH, D_MODEL, F), lambda i: (j, 0, 0, 0), pipeline_mode=one),
            pl.BlockSpec((None, NSH, D_MODEL, F), lambda i: (j, 0, 0, 0), pipeline_mode=one),
            pl.BlockSpec((None, NSH, F, D_MODEL), lambda i: (j, 0, 0, 0), pipeline_mode=one)]


def ffn_fwd(name, x, W, l, j, k, L, tm):
    D, F = D_MODEL, D_FF // NSH

    def body(x_ref, nw_ref, wg_ref, wu_ref, wd_ref, y_ref, g_ref, u_ref):
        x = x_ref[...]
        h = _rms(x, nw_ref[...]).astype(MMT)
        y = x
        for s in range(NSH):
            g = _dg(h, wg_ref[s], 1, 0)
            u = _dg(h, wu_ref[s], 1, 0)
            g_ref[s] = g.astype(g_ref.dtype)
            u_ref[s] = u.astype(u_ref.dtype)
            y = y + 0.5 * _dg((jax.nn.silu(g) * u).astype(MMT), wd_ref[s], 1, 0)
        y_ref[...] = y

    row = pl.BlockSpec((tm, D), lambda i: (i, 0))
    act = pl.BlockSpec((NSH, tm, F), lambda i: (0, i, 0))
    return pl.pallas_call(
        body, grid=(L // tm,), name=name,
        in_specs=[row, pl.BlockSpec((None, None, 1, D), lambda i: (l, k, 0, 0))] + _ffn_weight_specs(l, j),
        out_specs=[row, act, act],
        out_shape=[jax.ShapeDtypeStruct((L, D), F32), jax.ShapeDtypeStruct((NSH, L, F), MMT),
                   jax.ShapeDtypeStruct((NSH, L, F), MMT)],
        compiler_params=pltpu.CompilerParams(vmem_limit_bytes=VMEM_LIMIT, dimension_semantics=("arbitrary",)),
    )(x, W["nw"], W["L"][l]["wg"], W["L"][l]["wu"], W["L"][l]["wd"])


def ffn_bwd(name, x, g, u, dy, W, bufs, l, j, k, L, tm):
    D, F = D_MODEL, D_FF // NSH

    def body(x_ref, nw_ref, dy_ref, g_ref, u_ref, wg_ref, wu_ref, wd_ref, *rest):
        part_ref, dnw_ref, dwg_ref, dwu_ref, dwd_ref = rest[-5:]
        s, i = pl.program_id(0), pl.program_id(1)
        x, nw = x_ref[...], nw_ref[...]
        r = lax.rsqrt(jnp.mean(x * x, axis=-1, keepdims=True) + EPS)
        xhat = x * r
        h = (xhat * nw).astype(MMT)
        half_dy = (0.5 * dy_ref[...]).astype(MMT)
        gs, us = g_ref[...].astype(F32), u_ref[...].astype(F32)
        sig = jax.nn.sigmoid(gs)
        act = gs * sig
        da = _dg(half_dy, wd_ref[...], 1, 1)
        du = (da * act).astype(MMT)
        dg = (da * us * (sig * (1.0 + gs * (1.0 - sig)))).astype(MMT)
        dh = _dg(dg, wg_ref[...], 1, 1) + _dg(du, wu_ref[...], 1, 1)
        dxh = dh * nw
        part_ref[...] = r * (dxh - xhat * jnp.mean(dxh * xhat, axis=-1, keepdims=True))
        grads = (_dg(h, dg, 0, 0), _dg(h, du, 0, 0), _dg((act * us).astype(MMT), half_dy, 0, 0))
        dnw = jnp.sum(dh * xhat, axis=0, keepdims=True)
        first = jnp.logical_and(s == 0, i == 0)
        for ref, val, start in zip((dwg_ref, dwu_ref, dwd_ref, dnw_ref), grads + (dnw,), (i == 0, i == 0, i == 0, first)):
            @pl.when(start)
            def _(ref=ref, val=val):
                ref[...] = val

            @pl.when(jnp.logical_not(start))
            def _(ref=ref, val=val):
                ref[...] += val

    keys = ("ffn_gate", "ffn_up", "ffn_down")
    given = [bufs[key] for key in keys if bufs.get(key) is not None]
    row = pl.BlockSpec((tm, D), lambda s, i: (i, 0))
    act = pl.BlockSpec((None, tm, F), lambda s, i: (s, i, 0))
    wsp = lambda r, c: pl.BlockSpec((None, None, r, c), lambda s, i: (j, s, 0, 0))
    stk = lambda r, c: pl.BlockSpec((None, None, r, c), lambda s, i: (2 * l + j, s, 0, 0))
    part, dnw, bufs["ffn_gate"], bufs["ffn_up"], bufs["ffn_down"] = pl.pallas_call(
        body, grid=(NSH, L // tm), name=name,
        in_specs=[row, pl.BlockSpec((None, None, 1, D), lambda s, i: (l, k, 0, 0)), row, act, act,
                  wsp(D, F), wsp(D, F), wsp(F, D)] + [pl.BlockSpec(memory_space=pl.ANY)] * len(given),
        out_specs=[pl.BlockSpec((None, tm, D), lambda s, i: (s, i, 0)), pl.BlockSpec((1, D), lambda s, i: (0, 0)),
                   stk(D, F), stk(D, F), stk(F, D)],
        out_shape=[jax.ShapeDtypeStruct((NSH, L, D), F32), jax.ShapeDtypeStruct((1, D), F32)]
        + [jax.ShapeDtypeStruct((2 * DEPTH, NSH, D, F), F32)] * 2 + [jax.ShapeDtypeStruct((2 * DEPTH, NSH, F, D), F32)],
        input_output_aliases={8 + n: 2 + n for n in range(len(given))},
        compiler_params=pltpu.CompilerParams(vmem_limit_bytes=VMEM_LIMIT, dimension_semantics=("arbitrary", "arbitrary")),
    )(x, W["nw"], dy, g, u, W["L"][l]["wg"], W["L"][l]["wu"], W["L"][l]["wd"], *given)
    return _sum_parts(name + "_dx", dy, part, (L, D)), dnw


def layer_fwd(l, x0, W, P, L, tm):
    D = D_MODEL
    n_i = L // tm
    x1, g0, u0 = ffn_fwd(f"ffn_fwd_{l}0", x0, W, l, 0, 0, L, tm)
    proj = tile_fwd(
        lambda x, nw, win, s: pre_core(x, nw, win), f"pre_fwd_{l}", n_i, NSH,
        [(x1, (tm, D), _rows()), (W["nw"], (None, None, 1, D), _const(l, 1, 0, 0)),
         (W["L"][l]["win"], (None, D, IN_TOTAL // NSH), lambda s, i: (s, 0, 0))],
        [((L, IN_TOTAL), F32, (tm, IN_TOTAL // NSH), lambda s, i: (i, s), False)], s_outer=True)[0]
    nb = S5_N // LANE
    blk3 = lambda s, i: (0, i, 0)
    bur, bui = tile_fwd(
        lambda u, bmat, s: s5_pre_core(u, bmat), f"s5pre_fwd_{l}", n_i, 1,
        [(proj, (tm, BW), _rows(0)), (P["bmat"], (None, BW, 2 * S5_N), _const(l, 0, 0))],
        [((nb, L, LANE), F32, (nb, tm, LANE), blk3, False)] * 2)
    xr, xi = s5_scan_fwd(bur, bui, P["ar"][l], P["ai"][l], L)
    qzv = _to_time_order(proj[:, BW:4 * BW])
    o_t, sst = hg_fwd(qzv, P["lb"][l], L)
    o = _to_segment_order(o_t)
    xc = conv_fwd(proj, W["convw"][l], P["convb"][l], L)
    vec = (None, 1, BW)
    a, b = tile_fwd(
        lambda xc, wa, ba, wx, bx, lam, s: gates_core(xc, wa, ba, wx, bx, lam), f"gates_fwd_{l}", n_i, 1,
        [(xc, (tm, BW), _rows()), (P["wa"], (None, BW, BW), _const(l, 0, 0)), (P["ba"], vec, _const(l, 0, 0)),
         (P["wx"], (None, BW, BW), _const(l, 0, 0)), (P["bx"], vec, _const(l, 0, 0)), (P["lam"], vec, _const(l, 0, 0))],
        [((L, BW), F32, (tm, BW), _rows(), False)] * 2)
    hs = rg_scan_fwd(a, b, L)
    tmm = tm
    ya, yb, yc = tile_fwd(
        lambda *a: mid_core(*a[:-1]), f"mid_fwd_{l}", L // tmm, 1,
        [(xr, (nb, tmm, LANE), blk3), (xi, (nb, tmm, LANE), blk3), (proj, (tmm, BW), _rows(0)), (o, (tmm, BW), _rows()),
         (proj, (tmm, BW), _rows(4)), (hs, (tmm, BW), _rows()), (proj, (tmm, BW), _rows(6)),
         (P["hmat"], (BW, BW), _const(0, 0)), (P["cmat"], (None, 2 * S5_N, BW), _const(l, 0, 0)), (P["d"], vec, _const(l, 0, 0)),
         (W["L"][l]["gluw"], (BW, BW), _const(0, 0)), (P["glub"], vec, _const(l, 0, 0)), (P["hgw"], vec, _const(l, 0, 0))],
        [((L, BW), F32, (tmm, BW), _rows(), False)] * 3)
    x2 = tile_fwd(
        lambda x, *rest: (x + merge_core(*rest[:-1])[0],), f"merge_fwd_{l}", n_i, 1,
        [(x1, (tm, D), _rows()), (ya, (tm, BW), _rows()), (yb, (tm, BW), _rows()), (yc, (tm, BW), _rows())]
        + [(proj, (tm, BW), _rows(7 + k)) for k in range(6)]
        + [(W["L"][l]["pfull"], (3, BW, D), _const(0, 0, 0)), (W["L"][l]["woutfull"], (D, D), _const(0, 0))],
        [((L, D), F32, (tm, D), _rows(), False)])[0]
    x3, g1, u1 = ffn_fwd(f"ffn_fwd_{l}1", x2, W, l, 1, 2, L, tm)
    saved = dict(x0=x0, x1=x1, x2=x2, proj=proj, xr=xr, xi=xi, o=o, sst=sst, xc=xc, a=a, hs=hs, ya=ya, yb=yb, yc=yc,
                 qzv=qzv, g0=g0, u0=u0, g1=g1, u1=u1)
    return x3, saved


def layer_bwd(l, dx3, sv, W, P, bufs, L, tm):
    D = D_MODEL
    n_i = L // tm
    nb = S5_N // LANE
    dq = D // NSH
    vec = (None, 1, BW)
    vout = ((1, BW), (1, BW), _const(0, 0), "acc_all")
    blk3 = lambda s, i: (0, i, 0)
    small = {}
    proj = sv["proj"]

    dx2, dnw2 = ffn_bwd(f"ffn_bwd_{l}1", sv["x2"], sv["g1"], sv["u1"], dx3, W, bufs, l, 1, 2, L, tm)

    rw256 = ((L, BW), (tm, BW), _rows(), "write")
    res = tile_bwd(
        merge_core, f"merge_bwd_{l}", n_i, 1,
        [(sv["ya"], (tm, BW), _rows(), "r"), (sv["yb"], (tm, BW), _rows(), "r"), (sv["yc"], (tm, BW), _rows(), "r")]
        + [(proj, (tm, BW), _rows(7 + k), "r") for k in range(6)]
        + [(W["L"][l]["pfull"], (3, BW, D), _const(0, 0, 0), "w"), (W["L"][l]["woutfull"], (D, D), _const(0, 0), "w")],
        [(dx2, (tm, D), _rows())],
        [rw256] * 9
        + [((DEPTH, 3, BW, D), (None, 3, BW, D), _const(l, 0, 0, 0), "acc_all", bufs.get("branch_proj")),
           ((DEPTH, D, D), (None, D, D), _const(l, 0, 0), "acc_all", bufs.get("w_out"))])
    dya, dyb, dyc = res[:3]
    dgm = res[3:9]
    bufs["branch_proj"], bufs["w_out"] = res[9:]

    tmm = min(tm, 128)
    rw = ((L, BW), (tmm, BW), _rows(), "write")
    xw = ((nb, L, LANE), (nb, tmm, LANE), blk3, "write")
    res = tile_bwd(
        mid_core, f"mid_bwd_{l}", L // tmm, 1,
        [(sv["xr"], (nb, tmm, LANE), blk3, "r"), (sv["xi"], (nb, tmm, LANE), blk3, "r"), (proj, (tmm, BW), _rows(0), "r"),
         (sv["o"], (tmm, BW), _rows(), "r"), (proj, (tmm, BW), _rows(4), "r"), (sv["hs"], (tmm, BW), _rows(), "r"),
         (proj, (tmm, BW), _rows(6), "r"), (P["hmat"], (BW, BW), _const(0, 0), "c"),
         (P["cmat"], (None, 2 * S5_N, BW), _const(l, 0, 0), "w"), (P["d"], vec, _const(l, 0, 0), "p"),
         (W["L"][l]["gluw"], (BW, BW), _const(0, 0), "w"), (P["glub"], vec, _const(l, 0, 0), "p"),
         (P["hgw"], vec, _const(l, 0, 0), "p")],
        [(dya, (tmm, BW), _rows()), (dyb, (tmm, BW), _rows()), (dyc, (tmm, BW), _rows())],
        [xw, xw, rw, rw, rw, rw, rw,
         ((DEPTH, 2 * S5_N, BW), (None, 2 * S5_N, BW), _const(l, 0, 0), "acc_all", bufs.get("cmat")), vout,
         ((DEPTH, BW, BW), (None, BW, BW), _const(l, 0, 0), "acc_all", bufs.get("s5_glu_w")), vout, vout])
    dxr, dxi, du_skip, do, dg_b, dhs, dgate_c, bufs["cmat"], dd, bufs["s5_glu_w"], dglub, dhgw = res
    small["s5_d"], small["s5_glu_b"], small["hg_norm_w"] = dd[0], dglub[0], dhgw[0]

    da, db = rg_scan_bwd(sv["a"], sv["hs"], dhs, L)
    wmat = lambda key: ((DEPTH, BW, BW), (None, BW, BW), _const(l, 0, 0), "acc_all", bufs.get(key))
    res = tile_bwd(
        gates_core, f"gates_bwd_{l}", n_i, 1,
        [(sv["xc"], (tm, BW), _rows(), "r"), (P["wa"], (None, BW, BW), _const(l, 0, 0), "w"), (P["ba"], vec, _const(l, 0, 0), "p"),
         (P["wx"], (None, BW, BW), _const(l, 0, 0), "w"), (P["bx"], vec, _const(l, 0, 0), "p"), (P["lam"], vec, _const(l, 0, 0), "p")],
        [(da, (tm, BW), _rows()), (db, (tm, BW), _rows())],
        [((L, BW), (tm, BW), _rows(), "write"), wmat("wa"), vout, wmat("wx"), vout, vout])
    dxc, bufs["wa"], dba, bufs["wx"], dbx, dlam = res
    small["rg_ba"], small["rg_bx"], small["rg_lambda"] = dba[0], dbx[0], dlam[0]
    dx_c, dconvw, dconvb = conv_bwd(proj, W["convw"][l], dxc, L)
    small["rg_conv_w"], small["rg_conv_b"] = dconvw, dconvb[0]

    dq_b, dz_b, dv_b, dlb = hg_bwd(sv["qzv"], P["lb"][l], sv["sst"], _to_time_order(do), L)
    dq_b, dz_b, dv_b = [_to_segment_order(a) for a in (dq_b, dz_b, dv_b)]

    gr, gi, dar, dai = s5_scan_bwd(dxr, dxi, sv["xr"], sv["xi"], P["ar"][l], P["ai"][l], L)
    du_pre, bufs["bmat"] = tile_bwd(
        s5_pre_core, f"s5pre_bwd_{l}", n_i, 1,
        [(proj, (tm, BW), _rows(0), "r"), (P["bmat"], (None, BW, 2 * S5_N), _const(l, 0, 0), "w")],
        [(gr, (nb, tm, LANE), blk3), (gi, (nb, tm, LANE), blk3)],
        [((L, BW), (tm, BW), _rows(), "write"),
         ((DEPTH, BW, 2 * S5_N), (None, BW, 2 * S5_N), _const(l, 0, 0), "acc_all", bufs.get("bmat"))])
    du_a = add_n(f"du_a_{l}", [(du_skip, ()), (du_pre, ())], (L, BW))
    prep_ct = dict(dar=dar, dai=dai, dlb=dlb)

    pieces = [du_a, dq_b, dz_b, dv_b, dg_b, dx_c, dgate_c, *dgm]
    per_piece, per_shard = BW // LANE, IN_TOTAL // NSH // LANE
    part, dnw1 = None, []
    for s in range(NSH):
        groups = [(pieces[g // per_piece], (tm, LANE), _rows(g % per_piece))
                  for g in range(s * per_shard, (s + 1) * per_shard)]
        part, dnw_s, bufs["w_in"] = tile_bwd(
            pre_core, f"pre_bwd_{l}{s}", n_i, 1,
            [(sv["x1"], (tm, D), _rows(), "r"), (W["nw"], (None, None, 1, D), _const(l, 1, 0, 0), "p"),
             (W["L"][l]["win"], (None, D, IN_TOTAL // NSH), _const(s, 0, 0), "w")],
            [groups],
            [((NSH, L, D), (None, tm, D), functools.partial(lambda _s, i, s: (s, i, 0), s=s), "write", part),
             ((1, D), (1, D), _const(0, 0), "acc_all"),
             ((DEPTH, NSH, D, IN_TOTAL // NSH), (None, None, D, IN_TOTAL // NSH), _const(l, s, 0, 0), "acc_all",
              bufs.get("w_in"))])
        dnw1.append(dnw_s)
    dnw1 = (dnw1[0] + dnw1[1]) + (dnw1[2] + dnw1[3])
    dx1 = _sum_parts(f"pre_bwd_{l}_dx", dx2, part, (L, D))

    dx0, dnw0 = ffn_bwd(f"ffn_bwd_{l}0", sv["x0"], sv["g0"], sv["u0"], dx1, W, bufs, l, 0, 0, L, tm)
    small["norm_w"] = jnp.concatenate([dnw0, dnw1, dnw2], axis=0)
    return dx0, small, prep_ct


SMALL_RAW = ("s5_lambda_re", "s5_lambda_im", "s5_log_dt", "s5_b_re", "s5_b_im", "s5_c_re", "s5_c_im", "s5_d", "s5_glu_b",
             "hg_lb_logits", "hg_norm_w", "rg_conv_b", "rg_wa", "rg_ba", "rg_wx", "rg_bx", "rg_lambda", "final_norm_w")
DEPTH = 2


def local_step(x, target, W, raw, layer_weights=None, layer_grads=None):
    L = x.shape[0]
    tm = min(256, L)
    col = lambda v: v.reshape(DEPTH, 1, BW)
    (ar, ai, bmat, cmat), s5_vjp = jax.vjp(jax.vmap(s5_prep), *[raw[k] for k in SMALL_RAW[:7]])
    (wa, wx), rg_vjp = jax.vjp(lambda a, b: (jax.vmap(rg_prep)(a), jax.vmap(rg_prep)(b)), raw["rg_wa"], raw["rg_wx"])
    lb, hg_vjp = jax.vjp(hg_prep, raw["hg_lb_logits"])
    P = dict(
        ar=[ar[l] for l in range(DEPTH)], ai=[ai[l] for l in range(DEPTH)],
        bmat=bmat.astype(MMT), cmat=cmat.astype(MMT), wa=wa.astype(MMT), wx=wx.astype(MMT),
        lb=[lb[l].reshape(1, BW) for l in range(DEPTH)], convb=[raw["rg_conv_b"][l].reshape(1, BW) for l in range(DEPTH)],
        ba=col(raw["rg_ba"]), bx=col(raw["rg_bx"]), lam=col(raw["rg_lambda"]), d=col(raw["s5_d"]),
        glub=col(raw["s5_glu_b"]), hgw=col(raw["hg_norm_w"]), hmat=_head_mean_matrix())

    saved = []
    h = _to_segment_order(x)
    for l in range(DEPTH):
        if layer_weights is not None:
            W["L"][l], h = layer_weights(l, h)
        h, sv = layer_fwd(l, h, W, P, L, tm)
        saved.append(sv)
    loss, dh, dfw = loss_fwd_bwd(h, raw["final_norm_w"].reshape(1, D_MODEL), _to_segment_order(target), L, tm)

    big, per_layer, prep_cts = {}, [None] * DEPTH, [None] * DEPTH
    for l in reversed(range(DEPTH)):
        dh, sm, pc = layer_bwd(l, dh, saved[l], W, P, big, L, tm)
        per_layer[l], prep_cts[l] = sm, pc
        if layer_grads is not None:
            dh = layer_grads(l, big, dh)
    dh = _to_time_order(dh)

    small = {k: jnp.stack([per_layer[l][k] for l in range(DEPTH)]) for k in per_layer[0]}
    both = lambda k: jnp.stack([prep_cts[l][k] for l in range(DEPTH)])
    s5_g = s5_vjp((both("dar"), both("dai"), big.pop("bmat"), big.pop("cmat")))
    small.update(zip(SMALL_RAW[:7], s5_g))
    small["rg_wa"], small["rg_wx"] = rg_vjp((big.pop("wa"), big.pop("wx")))
    (small["hg_lb_logits"],) = hg_vjp(jnp.concatenate([prep_cts[l]["dlb"] for l in range(DEPTH)], axis=0))
    small["final_norm_w"] = dfw[0]
    return loss, dh, big, small


ANY = pl.BlockSpec(memory_space=pl.ANY)


def _place():
    x, y, c = lax.axis_index("x"), lax.axis_index("y"), lax.axis_index("c")
    chips = [(1 - x, y), (x, 1 - y), (1 - x, 1 - y)]
    return x, y, c, chips


def _remote(src, dst, send, recv, k, to):
    return pltpu.make_async_remote_copy(src_ref=src, dst_ref=dst, send_sem=send.at[k], recv_sem=recv.at[k],
                                        device_id=to, device_id_type=MESH)


def _comm_call(body, name, ins, out_shapes, n_sem, n_loc):
    return pl.pallas_call(
        body, name=name, in_specs=[ANY] * len(ins), out_specs=[ANY] * len(out_shapes), out_shape=out_shapes,
        scratch_shapes=[pltpu.SemaphoreType.DMA((n_sem,)), pltpu.SemaphoreType.DMA((n_sem,)),
                        pltpu.SemaphoreType.DMA((max(n_loc, 1),))],
    )(*ins)


def gather_shards(name, shards):
    n = len(shards)
    per = 8

    def body(*refs):
        ins, outs = refs[:n], refs[n:2 * n]
        send, recv, _ = refs[2 * n:]
        x, y, c, chips = _place()
        me = 2 * x + y
        sib = (x, y, 1 - c)
        sends = []
        for w in range(n):
            for j, (cx, cy) in enumerate(chips):
                cp = _remote(ins[w].at[c], outs[w].at[c, me], send, recv, per * w + j, (cx, cy, c))
                cp.start()
                sends.append(cp)
        for w in range(n):
            for l in range(2):
                cp = _remote(ins[w].at[l], outs[w].at[l, me], send, recv, per * w + 6 + l, sib)
                cp.start()
                sends.append(cp)
        for w in range(n):
            for j, (cx, cy) in enumerate(chips):
                theirs = outs[w].at[c, 2 * cx + cy]
                _remote(ins[w].at[c], theirs, send, recv, per * w + j, (cx, cy, c)).wait_recv()
                cp = _remote(theirs, theirs, send, recv, per * w + 3 + j, sib)
                cp.start()
                sends.append(cp)
        for w in range(n):
            for j, (cx, cy) in enumerate(chips):
                dst = outs[w].at[1 - c, 2 * cx + cy]
                _remote(dst, dst, send, recv, per * w + 3 + j, sib).wait_recv()
            for l in range(2):
                dst = outs[w].at[l, me]
                _remote(dst, dst, send, recv, per * w + 6 + l, sib).wait_recv()
        for cp in sends:
            cp.wait_send()

    shapes = [jax.ShapeDtypeStruct((2, NSH) + s.shape[1:], s.dtype) for s in shards]
    return _comm_call(body, name, shards, shapes, per * n, 0)


def exchange_halves(name, grads, ranges):
    n = len(grads)

    def body(*refs):
        ins, outs = refs[:n], refs[n:2 * n]
        send, recv, _ = refs[2 * n:]
        x, y, c, _chips = _place()
        cps = []
        for w in range(n):
            h = grads[w].shape[2] // 2
            p0, np_ = ranges[w]
            cp = _remote(ins[w].at[pl.ds(p0, np_), :, pl.ds((1 - c) * h, h)], outs[w], send, recv, w, (x, y, 1 - c))
            cp.start()
            cps.append(cp)
        for cp in cps:
            cp.wait()

    shapes = [jax.ShapeDtypeStruct((r[1], NSH, g.shape[2] // 2, g.shape[3]), g.dtype) for g, r in zip(grads, ranges)]
    return _comm_call(body, name, grads, shapes, n, 0)


def scatter_to_chips(name, halves):
    n = len(halves)

    def body(*refs):
        ins, outs = refs[:n], refs[n:2 * n]
        send, recv, _ = refs[2 * n:]
        x, y, c, chips = _place()
        cps = []
        for w in range(n):
            for j, (cx, cy) in enumerate(chips):
                cp = _remote(ins[w].at[:, 2 * cx + cy], outs[w].at[j], send, recv, 3 * w + j, (cx, cy, c))
                cp.start()
                cps.append(cp)
        for cp in cps:
            cp.wait()

    shapes = [jax.ShapeDtypeStruct((3, h.shape[0]) + h.shape[2:], h.dtype) for h in halves]
    return _comm_call(body, name, halves, shapes, 3 * n, 0)


def share_halves(name, pieces):
    n = len(pieces)

    def body(*refs):
        ins, outs = refs[:n], refs[n:2 * n]
        send, recv, _ = refs[2 * n:]
        x, y, c, _chips = _place()
        cps = []
        for w in range(n):
            cp = _remote(ins[w], outs[w], send, recv, w, (x, y, 1 - c))
            cp.start()
            cps.append(cp)
        for cp in cps:
            cp.wait()

    return _comm_call(body, name, pieces, [jax.ShapeDtypeStruct(p.shape, p.dtype) for p in pieces], n, 0)


def add_own_half(name, g, ra, c, wire, b0):
    nblk, h, cols = ra.shape
    tr = _row_tile(h, cols, mult=16)
    nt = h // tr

    def body(c_ref, g_ref, r_ref, o_ref):
        o_ref[...] = (g_ref[...] + r_ref[...]).astype(o_ref.dtype)

    blk = (None, tr, cols)
    return pl.pallas_call(
        body, name=name,
        grid_spec=pltpu.PrefetchScalarGridSpec(
            num_scalar_prefetch=1, grid=(nblk, nt),
            in_specs=[pl.BlockSpec(blk, lambda s, i, c_ref: (b0 + s, c_ref[0] * nt + i, 0)), pl.BlockSpec(blk, lambda s, i, c_ref: (s, i, 0))],
            out_specs=pl.BlockSpec(blk, lambda s, i, c_ref: (s, i, 0))),
        out_shape=jax.ShapeDtypeStruct(ra.shape, wire),
    )(c.reshape(1), g, ra)


def add_chips(name, hb, rb, me):
    npc, _, h, cols = hb.shape
    tr = _row_tile(h, cols, mult=16)

    def body(me_ref, h_ref, r0, r1, r2, o_ref):
        f = lambda r: r[...].astype(F32)
        o_ref[...] = ((f(h_ref) + f(r0)) + f(r1)) + f(r2)

    rspec = lambda j: pl.BlockSpec((None, None, tr, cols), functools.partial(lambda p, i, me_ref, j: (j, p, i, 0), j=j))
    return pl.pallas_call(
        body, name=name,
        grid_spec=pltpu.PrefetchScalarGridSpec(
            num_scalar_prefetch=1, grid=(npc, h // tr),
            in_specs=[pl.BlockSpec((None, None, tr, cols), lambda p, i, me_ref: (p, me_ref[0], i, 0)), rspec(0), rspec(1), rspec(2)],
            out_specs=pl.BlockSpec((None, tr, cols), lambda p, i, me_ref: (p, i, 0))),
        out_shape=jax.ShapeDtypeStruct((npc, h, cols), F32),
    )(me.reshape(1), hb, rb, rb, rb)


def adamw_halves(name, w, m, v, own, other, c):
    npc, rows, cols = w.shape
    h = rows // 2
    tr = _row_tile(h, cols, budget=1024 * 1024)
    nt = h // tr
    c1 = 1.0 - ADAM_B1 ** ADAM_STEP
    c2 = 1.0 - ADAM_B2 ** ADAM_STEP

    def body(c_ref, w_ref, m_ref, v_ref, own_ref, oth_ref, g_ref, d_ref, nm_ref, nv_ref):
        g = jnp.where(pl.program_id(1) == c_ref[0], own_ref[...], oth_ref[...])
        nm = ADAM_B1 * m_ref[...] + (1.0 - ADAM_B1) * g
        nv = ADAM_B2 * v_ref[...] + (1.0 - ADAM_B2) * jnp.square(g)
        g_ref[...] = g
        d_ref[...] = -ADAM_LR * ((nm / c1) / (jnp.sqrt(nv / c2) + ADAM_EPS) + ADAM_WD * w_ref[...])
        nm_ref[...] = nm
        nv_ref[...] = nv

    full = pl.BlockSpec((None, tr, cols), lambda p, hh, i, c_ref: (p, hh * nt + i, 0))
    half = pl.BlockSpec((None, tr, cols), lambda p, hh, i, c_ref: (p, i, 0))
    return pl.pallas_call(
        body, name=name,
        grid_spec=pltpu.PrefetchScalarGridSpec(
            num_scalar_prefetch=1, grid=(npc, 2, nt),
            in_specs=[full, full, full, half, half], out_specs=[full] * 4),
        out_shape=[jax.ShapeDtypeStruct(w.shape, F32)] * 4,
    )(c.reshape(1), w, m, v, own, other)


WEIGHTS = ("norm_w", "final_norm_w", "ffn_gate", "ffn_up", "ffn_down", "w_in", "branch_proj", "w_out", "s5_lambda_re",
           "s5_lambda_im", "s5_log_dt", "s5_b_re", "s5_b_im", "s5_c_re", "s5_c_im", "s5_d", "s5_glu_w", "s5_glu_b",
           "hg_lb_logits", "hg_norm_w", "rg_conv_w", "rg_conv_b", "rg_wa", "rg_ba", "rg_wx", "rg_bx", "rg_lambda")
BIG = ("ffn_gate", "ffn_up", "ffn_down", "w_in", "branch_proj", "w_out", "s5_glu_w")
SHARDED_SMALL = ("norm_w", "rg_conv_w")
SMALL = SMALL_RAW + SHARDED_SMALL


def _view2d(shape):
    return (1, shape[0]) if len(shape) == 1 else (math.prod(shape[:-1]), shape[-1])


def _small_layout(shapes, row_multiple):
    layout, at = [], 0
    for shape in shapes:
        r, c = _view2d(shape)
        rp = -(-r // 8) * 8
        layout.append((at, r, c, rp))
        at += rp * max(1, c // LANE)
    return layout, -(-at // row_multiple) * row_multiple


def pack_small(name, arrays, row_multiple):
    layout, rows = _small_layout([a.shape for a in arrays], row_multiple)

    def body(*refs):
        out = refs[-1]
        out[...] = jnp.zeros_like(out)
        for ref, (r0, r, c, rp) in zip(refs[:-1], layout):
            if c <= LANE:
                out[r0:r0 + r, 0:c] = ref[...]
            else:
                for q in range(c // LANE):
                    out[r0 + q * rp:r0 + q * rp + r, :] = ref[:, q * LANE:(q + 1) * LANE]

    return pl.pallas_call(
        body, name=name, out_shape=jax.ShapeDtypeStruct((rows, LANE), F32),
        compiler_params=pltpu.CompilerParams(vmem_limit_bytes=VMEM_LIMIT),
    )(*[a.reshape(_view2d(a.shape)) for a in arrays])


def unpack_small(name, packed, shapes):
    layout, _ = _small_layout(shapes, 8)

    def body(p_ref, *outs):
        for ref, (r0, r, c, rp) in zip(outs, layout):
            if c <= LANE:
                ref[...] = p_ref[r0:r0 + r, 0:c]
            else:
                for q in range(c // LANE):
                    ref[:, q * LANE:(q + 1) * LANE] = p_ref[r0 + q * rp:r0 + q * rp + r, :]

    res = pl.pallas_call(
        body, name=name, out_shape=[jax.ShapeDtypeStruct(_view2d(s), F32) for s in shapes],
        compiler_params=pltpu.CompilerParams(vmem_limit_bytes=VMEM_LIMIT),
    )(packed)
    return [a.reshape(s) for a, s in zip(res, shapes)]


HBM = pl.BlockSpec(memory_space=pltpu.HBM)
SEM = pl.BlockSpec(memory_space=pltpu.SEMAPHORE)
EFFECT = pltpu.SideEffectType.DATAFLOW_SIDE_EFFECTING


def split_start(name, srcs, land_shapes, plan, n_send, n_recv):
    ns, nl = len(srcs), len(land_shapes)

    def body(*refs):
        ins, lands = refs[:ns], refs[ns:ns + nl]
        send, recv = refs[ns + nl], refs[ns + nl + 1]
        for src, dst, ks, kr, dev in plan(ins, lands):
            pltpu.make_async_remote_copy(src_ref=src, dst_ref=dst, send_sem=send.at[ks], recv_sem=recv.at[kr],
                                         device_id=dev, device_id_type=MESH).start()
        refs[-1][...] = jnp.zeros_like(refs[-1])

    hbm = lambda a: pltpu.with_memory_space_constraint(a, pltpu.HBM)
    lands = [lax.empty(s.shape, s.dtype) for s in land_shapes]
    out = pl.pallas_call(
        body, name=name,
        out_shape=(pltpu.SemaphoreType.DMA((n_send,)), pltpu.SemaphoreType.DMA((n_recv,)),
                   *[pltpu.HBM(a.shape, a.dtype) for a in srcs], *[pltpu.HBM(s.shape, s.dtype) for s in land_shapes],
                   jax.ShapeDtypeStruct((8, LANE), F32)),
        in_specs=[HBM] * (ns + nl), out_specs=(SEM, SEM, *[HBM] * (ns + nl), pl.BlockSpec(memory_space=pltpu.VMEM)),
        input_output_aliases={k: 2 + k for k in range(ns + nl)},
        compiler_params=pltpu.CompilerParams(has_side_effects=EFFECT),
    )(*[hbm(a) for a in srcs], *[hbm(a) for a in lands])
    return out[:-1], out[-1]


def split_wait(name, handles, n_src, waits, after):
    send, recv, *bufs = handles
    nb = len(bufs)

    def body(*refs):
        ins, lands = refs[:n_src], refs[n_src:nb]
        send_sem, recv_sem = refs[nb], refs[nb + 1]
        x, y, c, _chips = _place()
        sends, recvs = waits(ins, lands)
        for src, k in sends:
            pltpu.make_async_remote_copy(src_ref=src, dst_ref=src, send_sem=send_sem.at[k], recv_sem=recv_sem.at[0],
                                         device_id=(x, y, 1 - c), device_id_type=MESH).wait_send()
        for dst, k in recvs:
            pltpu.make_async_remote_copy(src_ref=dst, dst_ref=dst, send_sem=send_sem.at[0], recv_sem=recv_sem.at[k],
                                         device_id=(x, y, 1 - c), device_id_type=MESH).wait_recv()

    out = pl.pallas_call(
        body, name=name, out_shape=tuple(pltpu.HBM(a.shape, a.dtype) for a in bufs),
        in_specs=[HBM] * nb + [SEM, SEM, ANY], out_specs=tuple([HBM] * nb),
        input_output_aliases={k: k for k in range(nb)},
        compiler_params=pltpu.CompilerParams(has_side_effects=EFFECT),
    )(*bufs, send, recv, after)
    return list(out[:n_src]), list(out[n_src:])


def gather_plan(n):
    def plan(ins, lands):
        x, y, c, chips = _place()
        me = 2 * x + y
        copies = []
        for w in range(n):
            for j, (cx, cy) in enumerate(chips):
                for t in range(2):
                    copies.append((ins[w].at[c], lands[w].at[c, me], 8 * w + 2 * j + t, 8 * w + 2 * j + c, (cx, cy, t)))
            for half in range(2):
                copies.append((ins[w].at[half], lands[w].at[half, me], 8 * w + 6 + half, 8 * w + 6 + half, (x, y, 1 - c)))
        return copies

    def waits(ins, lands):
        x, y, c, chips = _place()
        me = 2 * x + y
        sends, recvs = [], []
        for w in range(n):
            for j, (cx, cy) in enumerate(chips):
                for t in range(2):
                    sends.append((ins[w].at[c], 8 * w + 2 * j + t))
                    recvs.append((lands[w].at[t, 2 * cx + cy], 8 * w + 2 * j + t))
            for half in range(2):
                sends.append((ins[w].at[half], 8 * w + 6 + half))
                recvs.append((lands[w].at[half, me], 8 * w + 6 + half))
        return sends, recvs

    return plan, waits


def scatter_plan(n):
    def plan(ins, lands):
        x, y, c, chips = _place()
        return [(ins[w].at[:, 2 * cx + cy], lands[w].at[j], 3 * w + j, 3 * w + j, (cx, cy, c))
                for w in range(n) for j, (cx, cy) in enumerate(chips)]

    def waits(ins, lands):
        x, y, c, chips = _place()
        sends = [(ins[w].at[:, 2 * cx + cy], 3 * w + j) for w in range(n) for j, (cx, cy) in enumerate(chips)]
        recvs = [(lands[w].at[j], 3 * w + j) for w in range(n) for j in range(3)]
        return sends, recvs

    return plan, waits


def _layer_shards(w, l):
    return [w["ffn_gate"][l].astype(MMT), w["ffn_up"][l].astype(MMT), w["ffn_down"][l].astype(MMT),
            w["w_in"][l].reshape(2, D_MODEL // 2, -1).astype(MMT),
            w["branch_proj"][l].reshape(2, 3 * BW // 2, -1).astype(MMT),
            w["w_out"][l].reshape(2, -1, D_MODEL).astype(MMT),
            w["s5_glu_w"][l].reshape(2, -1, BW).astype(MMT)]


def _layer_weights(g):
    rows = lambda a: a.transpose(1, 0, 2, 3).reshape(NSH, -1, a.shape[-1])
    p = rows(g[4]).reshape(NSH, 3, BW, -1).transpose(1, 2, 0, 3).reshape(3, BW, D_MODEL)
    return dict(wg=g[0], wu=g[1], wd=g[2], win=rows(g[3]), pfull=p,
                woutfull=rows(g[5]).reshape(D_MODEL, D_MODEL), gluw=rows(g[6]).reshape(BW, BW))


def _grad_views(big, l):
    dq = D_MODEL // NSH
    bp = big["branch_proj"][l].reshape(3, BW, NSH, dq).transpose(2, 0, 1, 3).reshape(1, NSH, 3 * BW, dq)
    four = lambda a: a.reshape(a.shape[0], NSH, -1, a.shape[-1])
    return [(four(big["ffn_gate"]), 2 * l, 2), (four(big["ffn_up"]), 2 * l, 2), (four(big["ffn_down"]), 2 * l, 2),
            (four(big["w_in"]), l, 1), (bp, 0, 1), (four(big["w_out"]), l, 1), (four(big["s5_glu_w"]), l, 1)]


def _reduce_to_halves(tag, views, c, wire):
    from_sibling = exchange_halves(f"reduce_cores_{tag}", [a for a, _, _ in views], [(p0, n) for _, p0, n in views])
    merge = lambda a: a.reshape((-1,) + a.shape[2:])
    return [add_own_half(f"sum_cores_{tag}_{i}", merge(a), merge(r), c, wire[i], NSH * p0).reshape(r.shape)
            for i, ((a, p0, _), r) in enumerate(zip(views, from_sibling))]


def _step(x, target, w, m, v):
    mx, my, mc = lax.axis_index("x"), lax.axis_index("y"), lax.axis_index("c")
    me = (2 * mx + my).astype(jnp.int32)
    mc = mc.astype(jnp.int32)

    norms = gather_shards("gather_norms", [w[n] for n in SHARDED_SMALL])
    W = dict(nw=norms[0].transpose(0, 2, 1, 3).reshape(DEPTH, 3, 1, D_MODEL),
             convw=norms[1].transpose(0, 2, 1, 3).reshape(DEPTH, CONV_W, BW), L=[None] * DEPTH)
    wire = [jnp.bfloat16] * len(BIG) + [F32]
    state = {}

    n_big = len(BIG)
    g_plan, g_waits = gather_plan(n_big)
    s_plan, s_waits = scatter_plan(n_big)

    def layer_weights(l, h):
        if l == 0:
            now = _layer_weights(gather_shards("gather_weights_0", _layer_shards(w, 0)))
            nxt = _layer_shards(w, 1)
            shapes = [jax.ShapeDtypeStruct((2, NSH) + a.shape[1:], a.dtype) for a in nxt]
            state["gather"], token = split_start("gather_weights_1_start", nxt, shapes, g_plan, 8 * n_big, 8 * n_big)
            W["nw"] = W["nw"] + token[0, 0]
            return now, h
        return _layer_weights(split_wait("gather_weights_1_wait", state["gather"], n_big, g_waits, h)[1]), h

    def layer_grads(l, big, dh):
        if l == DEPTH - 1:
            halves = _reduce_to_halves(str(l), _grad_views(big, l), mc, wire)
            shapes = [jax.ShapeDtypeStruct((3, a.shape[0]) + a.shape[2:], a.dtype) for a in halves]
            state["scatter"], token = split_start("reduce_chips_1_start", halves, shapes, s_plan, 3 * n_big, 3 * n_big)
            W["nw"] = W["nw"] + token[0, 0]
        return dh

    loss, dx, big, small = local_step(x[0], target[0], W, {k: w[k] for k in SMALL_RAW}, layer_weights, layer_grads)

    small_packed = pack_small("pack_small_grads", [small[n] for n in SMALL], NSH * 32)
    halves = _reduce_to_halves("0", _grad_views(big, 0) + [(small_packed.reshape(1, NSH, -1, LANE), 0, 1)], mc, wire)
    from_chips = scatter_to_chips("reduce_chips_0", halves)
    own0 = [add_chips(f"sum_chips_0_{i}", h, r, me) for i, (h, r) in enumerate(zip(halves, from_chips))]
    sent, landed = split_wait("reduce_chips_1_wait", state["scatter"], n_big, s_waits, dx)
    own1 = [add_chips(f"sum_chips_1_{i}", h, r, me) for i, (h, r) in enumerate(zip(sent, landed))]
    own = [jnp.concatenate([a, b], axis=0) for a, b in zip(own0, own1)] + [own0[-1]]
    other = share_halves("reduce_share", own)

    g, delta, new_m, new_v = {}, {}, {}, {}
    for i, n in enumerate(BIG):
        view = lambda a: a.reshape(own[i].shape[0], -1, own[i].shape[2])
        res = adamw_halves(f"adamw_{n}", view(w[n]), view(m[n]), view(v[n]), own[i], other[i], mc)
        g[n], delta[n], new_m[n], new_v[n] = [a.reshape(w[n].shape) for a in res]

    piece = jnp.stack([jnp.where(mc == 0, own[-1][0], other[-1][0]), jnp.where(mc == 0, other[-1][0], own[-1][0])])
    (all_small,) = gather_shards("gather_small", [piece])
    full_small = unpack_small("unpack_small_grads", all_small.transpose(1, 0, 2, 3).reshape(-1, LANE),
                              [small[n].shape for n in SMALL])
    g.update(zip(SMALL, full_small))
    g["norm_w"] = lax.dynamic_slice_in_dim(g["norm_w"], me * (D_MODEL // NSH), D_MODEL // NSH, axis=2)
    g["rg_conv_w"] = lax.dynamic_slice_in_dim(g["rg_conv_w"], me * (BW // NSH), BW // NSH, axis=2)

    packed = [pack_small(f"pack_small_{tag}", [src[n] for n in SMALL], 8)
              for tag, src in (("w", w), ("g", g), ("m", m), ("v", v))]
    for tag, dst, flat in zip(("delta", "m", "v"), (delta, new_m, new_v), adamw(*packed)):
        dst.update(zip(SMALL, unpack_small(f"unpack_small_{tag}", flat, [w[n].shape for n in SMALL])))

    total = lax.psum(loss[0, 0], ("x", "y", "c"))
    return (total, dx[None], *[g[n] for n in WEIGHTS], *[delta[n] for n in WEIGHTS],
            *[new_m[n] for n in WEIGHTS], *[new_v[n] for n in WEIGHTS])


def kernel(x, norm_w, final_norm_w, ffn_gate, ffn_up, ffn_down, w_in, branch_proj, w_out, s5_lambda_re, s5_lambda_im, s5_log_dt, s5_b_re, s5_b_im, s5_c_re, s5_c_im, s5_d, s5_glu_w, s5_glu_b, hg_lb_logits, hg_norm_w, rg_conv_w, rg_conv_b, rg_wa, rg_ba, rg_wx, rg_bx, rg_lambda, loss_target, m_norm_w, m_final_norm_w, m_ffn_gate, m_ffn_up, m_ffn_down, m_w_in, m_branch_proj, m_w_out, m_s5_lambda_re, m_s5_lambda_im, m_s5_log_dt, m_s5_b_re, m_s5_b_im, m_s5_c_re, m_s5_c_im, m_s5_d, m_s5_glu_w, m_s5_glu_b, m_hg_lb_logits, m_hg_norm_w, m_rg_conv_w, m_rg_conv_b, m_rg_wa, m_rg_ba, m_rg_wx, m_rg_bx, m_rg_lambda, v_norm_w, v_final_norm_w, v_ffn_gate, v_ffn_up, v_ffn_down, v_w_in, v_branch_proj, v_w_out, v_s5_lambda_re, v_s5_lambda_im, v_s5_log_dt, v_s5_b_re, v_s5_b_im, v_s5_c_re, v_s5_c_im, v_s5_d, v_s5_glu_w, v_s5_glu_b, v_hg_lb_logits, v_hg_norm_w, v_rg_conv_w, v_rg_conv_b, v_rg_wa, v_rg_ba, v_rg_wx, v_rg_bx, v_rg_lambda):
    ws = (norm_w, final_norm_w, ffn_gate, ffn_up, ffn_down, w_in, branch_proj, w_out, s5_lambda_re, s5_lambda_im, s5_log_dt, s5_b_re, s5_b_im, s5_c_re, s5_c_im, s5_d, s5_glu_w, s5_glu_b, hg_lb_logits, hg_norm_w, rg_conv_w, rg_conv_b, rg_wa, rg_ba, rg_wx, rg_bx, rg_lambda)
    ms = (m_norm_w, m_final_norm_w, m_ffn_gate, m_ffn_up, m_ffn_down, m_w_in, m_branch_proj, m_w_out, m_s5_lambda_re, m_s5_lambda_im, m_s5_log_dt, m_s5_b_re, m_s5_b_im, m_s5_c_re, m_s5_c_im, m_s5_d, m_s5_glu_w, m_s5_glu_b, m_hg_lb_logits, m_hg_norm_w, m_rg_conv_w, m_rg_conv_b, m_rg_wa, m_rg_ba, m_rg_wx, m_rg_bx, m_rg_lambda)
    vs = (v_norm_w, v_final_norm_w, v_ffn_gate, v_ffn_up, v_ffn_down, v_w_in, v_branch_proj, v_w_out, v_s5_lambda_re, v_s5_lambda_im, v_s5_log_dt, v_s5_b_re, v_s5_b_im, v_s5_c_re, v_s5_c_im, v_s5_d, v_s5_glu_w, v_s5_glu_b, v_hg_lb_logits, v_hg_norm_w, v_rg_conv_w, v_rg_conv_b, v_rg_wa, v_rg_ba, v_rg_wx, v_rg_bx, v_rg_lambda)
    return _step(x, loss_target, dict(zip(WEIGHTS, ws)), dict(zip(WEIGHTS, ms)), dict(zip(WEIGHTS, vs)))
```

```python
import functools
import math
from typing import NamedTuple

import jax
import jax.numpy as jnp
from jax import lax
from jax.experimental import pallas as pl
from jax.experimental.pallas import tpu as pltpu

F32 = jnp.float32
MMT = jnp.bfloat16
HI = lax.Precision.HIGHEST

D_MODEL = 1024
BW = 512
S5_GROUP, S5_GROUPS, S5_STATE = 16, 32, 64
S5_N = S5_GROUPS * S5_STATE
HG_HEADS, HG_D = 4, 128
HG_CHUNK = 128
RG_BLOCKS, RG_BLOCK = 8, 64
RG_C = 8.0
CONV_W = 4
D_FF = 2816
EPS = 1e-6
IN_TOTAL = 6656
NSH = 4
NSEG = 8
LANE = 128
VMEM_LIMIT = 56 * 1024 * 1024

ADAM_LR, ADAM_B1, ADAM_B2, ADAM_EPS, ADAM_WD, ADAM_STEP = 0.001, 0.9, 0.999, 1e-08, 0.01, 10

MESH = pl.DeviceIdType.MESH


class WP(NamedTuple):
    w: jax.Array
    p: jax.Array


def _dg(a, b, ca, cb):
    return lax.dot_general(a, b, (((ca,), (cb,)), ((), ())), preferred_element_type=F32)


@jax.custom_vjp
def _mmw(a, w, p):
    return _dg(a.astype(MMT), w, 1, 0)


def _mmw_fwd(a, w, p):
    return _mmw(a, w, p), (a, w)


def _mmw_bwd(res, g):
    a, w = res
    gb = g.astype(MMT)
    return _dg(gb, w, 1, 1), jnp.zeros_like(w), _dg(a.astype(MMT), gb, 0, 0)


_mmw.defvjp(_mmw_fwd, _mmw_bwd)


def mm(a, w):
    if isinstance(w, WP):
        return _mmw(a, w.w, w.p)
    return _dg(a.astype(MMT), w, 1, 0)


@jax.custom_vjp
def mma_nn(a, b):
    return _dg(a.astype(MMT), b.astype(MMT), 1, 0)


def _nn_f(a, b):
    return mma_nn(a, b), (a, b)


def _nn_b(res, g):
    a, b = res
    gb = g.astype(MMT)
    return _dg(gb, b.astype(MMT), 1, 1), _dg(a.astype(MMT), gb, 0, 0)


mma_nn.defvjp(_nn_f, _nn_b)


@jax.custom_vjp
def mma_nt(a, b):
    return _dg(a.astype(MMT), b.astype(MMT), 1, 1)


def _nt_f(a, b):
    return mma_nt(a, b), (a, b)


def _nt_b(res, g):
    a, b = res
    gb = g.astype(MMT)
    return _dg(gb, b.astype(MMT), 1, 0), _dg(gb, a.astype(MMT), 0, 0)


mma_nt.defvjp(_nt_f, _nt_b)


@jax.custom_vjp
def mma_tn(a, b):
    return _dg(a.astype(MMT), b.astype(MMT), 0, 0)


def _tn_f(a, b):
    return mma_tn(a, b), (a, b)


def _tn_b(res, g):
    a, b = res
    gb = g.astype(MMT)
    return _dg(b.astype(MMT), gb, 1, 1), _dg(a.astype(MMT), gb, 1, 0)


mma_tn.defvjp(_tn_f, _tn_b)


def mm_exact(m, x):
    return jnp.dot(m, x, precision=HI, preferred_element_type=F32)


def _rms(x, w):
    return x * lax.rsqrt(jnp.mean(x * x, axis=-1, keepdims=True) + EPS) * w


def _expm1(x):
    series = x * (1.0 + x * (1.0 / 2) * (1.0 + x * (1.0 / 3) * (1.0 + x * (1.0 / 4) * (1.0 + x * (1.0 / 5) * (1.0 + x * (1.0 / 6))))))
    return jnp.where(jnp.abs(x) < 0.1, series, jnp.exp(x) - 1.0)


def _bspec(block, fn, order):
    if order == "is":
        return pl.BlockSpec(block, lambda i, s: fn(s, i))
    return pl.BlockSpec(block, lambda s, i: fn(s, i))


def tile_fwd(fn, name, n_i, n_s, ins, outs, s_outer=False):
    n_in = len(ins)
    order = "si" if s_outer else "is"
    assert not (s_outer and any(o[4] for o in outs))

    def body(*refs):
        s = pl.program_id(0 if s_outer else 1)
        res = fn(*[r[...] for r in refs[:n_in]], s)
        for o_ref, val, spec in zip(refs[n_in:], res, outs):
            if spec[4] and n_s > 1:
                @pl.when(s == 0)
                def _(o_ref=o_ref, val=val):
                    o_ref[...] = val.astype(o_ref.dtype)

                @pl.when(s != 0)
                def _(o_ref=o_ref, val=val):
                    o_ref[...] += val.astype(o_ref.dtype)
            else:
                o_ref[...] = val.astype(o_ref.dtype)

    return pl.pallas_call(
        body, grid=(n_s, n_i) if s_outer else (n_i, n_s), name=name,
        in_specs=[_bspec(b, f, order) for _, b, f in ins],
        out_specs=[_bspec(b, f, order) for _, _, b, f, _ in outs],
        out_shape=[jax.ShapeDtypeStruct(sh, dt) for sh, dt, _, _, _ in outs],
        compiler_params=pltpu.CompilerParams(vmem_limit_bytes=VMEM_LIMIT,
                                             dimension_semantics=("arbitrary", "arbitrary")),
    )(*[a for a, _, _ in ins])


def tile_bwd(fn, name, n_i, n_s, ins, cts, gouts):
    groups = [c if isinstance(c, list) else [c] for c in cts]
    cts = [blk for grp in groups for blk in grp]
    n_in, n_ct = len(ins), len(cts)
    kinds = [k for _, _, _, k in ins]
    d_pos = [j for j, k in enumerate(kinds) if k != "c"]
    shared = [(gi, spec[4]) for gi, spec in enumerate(gouts) if len(spec) == 5 and spec[4] is not None]
    n_sh = len(shared)

    def body(*refs):
        s, i = pl.program_id(0), pl.program_id(1)
        vals = [r[...] for r in refs[:n_in]]
        ct_refs, ctv = list(refs[n_in:n_in + n_ct]), []
        for grp in groups:
            parts = [ct_refs.pop(0)[...] for _ in grp]
            ctv.append(parts[0] if len(parts) == 1 else jnp.concatenate(parts, axis=1))
        ctv = tuple(ctv)
        g_refs = refs[n_in + n_ct + n_sh:]

        def g(*dv):
            args = list(vals)
            for j, v in zip(d_pos, dv):
                args[j] = WP(vals[j], v) if kinds[j] == "w" else v
            return tuple(fn(*args))

        dv0 = [jnp.zeros(vals[j].shape, F32) if kinds[j] == "w" else vals[j] for j in d_pos]
        _, vjp = jax.vjp(g, *dv0)
        grads = vjp(ctv)
        for g_ref, gv, spec in zip(g_refs, grads, gouts):
            mode = spec[3]
            if mode == "write":
                g_ref[...] = gv.astype(g_ref.dtype)
            else:
                first = (i == 0) if mode == "acc_i" else jnp.logical_and(i == 0, s == 0)

                @pl.when(first)
                def _(g_ref=g_ref, gv=gv):
                    g_ref[...] = gv.astype(g_ref.dtype)

                @pl.when(jnp.logical_not(first))
                def _(g_ref=g_ref, gv=gv):
                    g_ref[...] += gv.astype(g_ref.dtype)

    return pl.pallas_call(
        body, grid=(n_s, n_i), name=name,
        in_specs=([_bspec(b, f, "si") for _, b, f, _ in ins] + [_bspec(b, f, "si") for _, b, f in cts]
                  + [pl.BlockSpec(memory_space=pl.ANY)] * n_sh),
        out_specs=[_bspec(spec[1], spec[2], "si") for spec in gouts],
        out_shape=[jax.ShapeDtypeStruct(spec[0], F32) for spec in gouts],
        input_output_aliases={n_in + n_ct + k: gi for k, (gi, _) in enumerate(shared)},
        compiler_params=pltpu.CompilerParams(vmem_limit_bytes=VMEM_LIMIT,
                                             dimension_semantics=("arbitrary", "arbitrary")),
    )(*[a for a, _, _, _ in ins], *[a for a, _, _ in cts], *[buf for _, buf in shared])


def _row_tile(rows, width, itemsize=4, budget=2 * 1024 * 1024, mult=8):
    best = mult
    for t in range(mult, rows + 1, mult):
        if rows % t == 0 and t * width * itemsize <= budget:
            best = t
    return best


def add_n(name, terms, shape):
    rows, cols = shape
    tr = _row_tile(rows, cols)

    def body(*refs):
        acc = refs[0][...]
        for r in refs[1:-1]:
            acc = acc + r[...]
        refs[-1][...] = acc

    specs = []
    for _, lead in terms:
        specs.append(pl.BlockSpec((None,) * len(lead) + (tr, cols), functools.partial(lambda i, lead: (*lead, i, 0), lead=lead)))
    return pl.pallas_call(
        body, grid=(rows // tr,), name=name, in_specs=specs,
        out_specs=pl.BlockSpec((tr, cols), lambda i: (i, 0)),
        out_shape=jax.ShapeDtypeStruct((rows, cols), F32),
    )(*[a for a, _ in terms])


def ffn_core(x, nw, wg, wu, wd):
    h = _rms(x, nw)
    return (0.5 * mm(jax.nn.silu(mm(h, wg)) * mm(h, wu), wd),)


def pre_core(x, nw, win):
    return (mm(_rms(x, nw), win),)


def _split_lanes(y):
    return jnp.stack([y[:, k * LANE:(k + 1) * LANE] for k in range(y.shape[1] // LANE)], axis=0)


def _join_lanes(y3):
    return jnp.concatenate([y3[k] for k in range(y3.shape[0])], axis=1)


def s5_pre_core(u, bmat):
    bu = mm(u, bmat)
    return _split_lanes(bu[:, :S5_N]), _split_lanes(bu[:, S5_N:])


def mid_core(xr, xi, u, o, g, hs, gc, hmat, cmat, d, gluw, glub, hgw):
    xs = jnp.concatenate([_join_lanes(xr), _join_lanes(xi)], axis=1)
    y = mm(xs, cmat) + d * u
    z = jax.nn.gelu(y)
    ya = z * jax.nn.sigmoid(mm(z, gluw) + glub)
    ms = mm_exact(o * o, hmat)
    yb = o * lax.rsqrt(ms + EPS) * hgw * jax.nn.silu(g)
    yc = hs * jax.nn.gelu(gc)
    return ya, yb, yc


def _sub(w, n):
    return WP(w.w[n], w.p[n]) if isinstance(w, WP) else w[n]


def merge_core(ya, yb, yc, g0, g1, g2, g3, g4, g5, p, wout):
    gate = lambda a, b: jax.nn.sigmoid(jnp.concatenate([a, b], axis=1))
    m = gate(g0, g1) * mm(ya, _sub(p, 0)) + gate(g2, g3) * mm(yb, _sub(p, 1)) + gate(g4, g5) * mm(yc, _sub(p, 2))
    return (mm(m, wout),)


def gates_core(xc, wa, ba, wx, bx, lam):
    r = jax.nn.sigmoid(mm(xc, wa) + ba)
    i = jax.nn.sigmoid(mm(xc, wx) + bx)
    log_a = -RG_C * jax.nn.softplus(-lam) * r
    a = jnp.exp(log_a)
    b = jnp.sqrt(-_expm1(2.0 * log_a)) * (i * xc)
    return a, b


def _seg_rows(ref, k, j, n):
    rows = pl.ds(pl.multiple_of(j * NSEG, NSEG), NSEG)
    if k is None:
        return ref[rows, :]
    return ref[k, rows, :]


def _seg_store(ref, k, j, n, val):
    rows = pl.ds(pl.multiple_of(j * NSEG, NSEG), NSEG)
    if k is None:
        ref[rows, :] = val
    else:
        ref[k, rows, :] = val


def _seg_carries(er, ei, pr, pi, reverse):
    rows = lax.broadcasted_iota(jnp.int32, er.shape, 0)
    cr = jnp.zeros_like(er)
    ci = None if ei is None else jnp.zeros_like(er)
    order = range(NSEG - 2, -1, -1) if reverse else range(1, NSEG)
    shift = NSEG - 1 if reverse else 1
    for s in order:
        if ei is None:
            tr = er + pr * cr
            cr = jnp.where(rows == s, pltpu.roll(tr, shift, 0), cr)
        else:
            tr = er + pr * cr - pi * ci
            ti = ei + pr * ci + pi * cr
            cr = jnp.where(rows == s, pltpu.roll(tr, shift, 0), cr)
            ci = jnp.where(rows == s, pltpu.roll(ti, shift, 0), ci)
    return cr, ci


S5_K = 2


def s5_scan_fwd(bur, bui, ar, ai, L):
    n = L // NSEG
    nb = S5_N // LANE
    K = S5_K

    def body(br_ref, bi_ref, ar_ref, ai_ref, xr_ref, xi_ref):
        zero = jnp.zeros((NSEG, LANE), F32)
        A = [(jnp.broadcast_to(ar_ref[k], (NSEG, LANE)), jnp.broadcast_to(ai_ref[k], (NSEG, LANE))) for k in range(K)]

        def p1(j, st):
            new = []
            for k in range(K):
                sr, si, pr, pi = st[k]
                a_r, a_i = A[k]
                nr = a_r * sr - a_i * si + _seg_rows(br_ref, k, j, n)
                ni = a_r * si + a_i * sr + _seg_rows(bi_ref, k, j, n)
                _seg_store(xr_ref, k, j, n, nr)
                _seg_store(xi_ref, k, j, n, ni)
                new.append((nr, ni, a_r * pr - a_i * pi, a_r * pi + a_i * pr))
            return tuple(new)

        st = lax.fori_loop(0, n, p1, tuple((zero, zero, zero + 1.0, zero) for _ in range(K)))
        C = [_seg_carries(st[k][0], st[k][1], st[k][2], st[k][3], False) for k in range(K)]

        def p2(j, st):
            new = []
            for k in range(K):
                pr, pi = st[k]
                a_r, a_i = A[k]
                pr, pi = a_r * pr - a_i * pi, a_r * pi + a_i * pr
                cr, ci = C[k]
                _seg_store(xr_ref, k, j, n, _seg_rows(xr_ref, k, j, n) + pr * cr - pi * ci)
                _seg_store(xi_ref, k, j, n, _seg_rows(xi_ref, k, j, n) + pr * ci + pi * cr)
                new.append((pr, pi))
            return tuple(new)

        lax.fori_loop(0, n, p2, tuple((zero + 1.0, zero) for _ in range(K)))

    blk = pl.BlockSpec((K, L, LANE), lambda g: (g, 0, 0))
    ablk = pl.BlockSpec((K, 1, LANE), lambda g: (g, 0, 0))
    return pl.pallas_call(
        body, grid=(nb // K,), name="s5_scan_fwd",
        in_specs=[blk, blk, ablk, ablk], out_specs=[blk, blk],
        out_shape=[jax.ShapeDtypeStruct((nb, L, LANE), F32)] * 2,
        compiler_params=pltpu.CompilerParams(vmem_limit_bytes=VMEM_LIMIT),
    )(bur, bui, ar, ai)


def s5_scan_bwd(dxr, dxi, xr, xi, ar, ai, L):
    n = L // NSEG
    nb = S5_N // LANE
    K = S5_K

    def body(dr_ref, di_ref, xr_ref, xi_ref, ar_ref, ai_ref, gr_ref, gi_ref, dar_ref, dai_ref):
        zero = jnp.zeros((NSEG, LANE), F32)
        rows = lax.broadcasted_iota(jnp.int32, (NSEG, LANE), 0)
        A = [(jnp.broadcast_to(ar_ref[k], (NSEG, LANE)), -jnp.broadcast_to(ai_ref[k], (NSEG, LANE))) for k in range(K)]

        def p1(jj, st):
            j = n - 1 - jj
            new = []
            for k in range(K):
                sr, si, pr, pi = st[k]
                a_r, a_i = A[k]
                nr = a_r * sr - a_i * si + _seg_rows(dr_ref, k, j, n)
                ni = a_r * si + a_i * sr + _seg_rows(di_ref, k, j, n)
                _seg_store(gr_ref, k, j, n, nr)
                _seg_store(gi_ref, k, j, n, ni)
                new.append((nr, ni, a_r * pr - a_i * pi, a_r * pi + a_i * pr))
            return tuple(new)

        st = lax.fori_loop(0, n, p1, tuple((zero, zero, zero + 1.0, zero) for _ in range(K)))
        C = [_seg_carries(st[k][0], st[k][1], st[k][2], st[k][3], True) for k in range(K)]
        xb = [(jnp.where(rows == 0, 0.0, pltpu.roll(_seg_rows(xr_ref, k, n - 1, n), 1, 0)),
               jnp.where(rows == 0, 0.0, pltpu.roll(_seg_rows(xi_ref, k, n - 1, n), 1, 0))) for k in range(K)]

        def p2(jj, st):
            j = n - 1 - jj
            jp = jnp.maximum(j - 1, 0)
            new = []
            for k in range(K):
                pr, pi, acr, aci = st[k]
                a_r, a_i = A[k]
                pr, pi = a_r * pr - a_i * pi, a_r * pi + a_i * pr
                cr, ci = C[k]
                g_r = _seg_rows(gr_ref, k, j, n) + pr * cr - pi * ci
                g_i = _seg_rows(gi_ref, k, j, n) + pr * ci + pi * cr
                _seg_store(gr_ref, k, j, n, g_r)
                _seg_store(gi_ref, k, j, n, g_i)
                xpr = jnp.where(j == 0, xb[k][0], _seg_rows(xr_ref, k, jp, n))
                xpi = jnp.where(j == 0, xb[k][1], _seg_rows(xi_ref, k, jp, n))
                new.append((pr, pi, acr + g_r * xpr + g_i * xpi, aci + g_i * xpr - g_r * xpi))
            return tuple(new)

        st = lax.fori_loop(0, n, p2, tuple((zero + 1.0, zero, zero, zero) for _ in range(K)))
        for k in range(K):
            dar_ref[k] = jnp.sum(st[k][2], axis=0, keepdims=True)
            dai_ref[k] = jnp.sum(st[k][3], axis=0, keepdims=True)

    blk = pl.BlockSpec((K, L, LANE), lambda g: (g, 0, 0))
    ablk = pl.BlockSpec((K, 1, LANE), lambda g: (g, 0, 0))
    return pl.pallas_call(
        body, grid=(nb // K,), name="s5_scan_bwd",
        in_specs=[blk, blk, blk, blk, ablk, ablk], out_specs=[blk, blk, ablk, ablk],
        out_shape=[jax.ShapeDtypeStruct((nb, L, LANE), F32)] * 2 + [jax.ShapeDtypeStruct((nb, 1, LANE), F32)] * 2,
        compiler_params=pltpu.CompilerParams(vmem_limit_bytes=VMEM_LIMIT),
    )(dxr, dxi, xr, xi, ar, ai)


def rg_scan_fwd(a, b, L):
    n = L // NSEG

    def body(a_ref, b_ref, h_ref):
        zero = jnp.zeros((NSEG, LANE), F32)

        def p1(j, st):
            h, p = st
            aj = _seg_rows(a_ref, None, j, n)
            h = aj * h + _seg_rows(b_ref, None, j, n)
            _seg_store(h_ref, None, j, n, h)
            return h, aj * p

        e, pe = lax.fori_loop(0, n, p1, (zero, zero + 1.0))
        c, _ = _seg_carries(e, None, pe, None, False)

        def p2(j, p):
            p = _seg_rows(a_ref, None, j, n) * p
            _seg_store(h_ref, None, j, n, _seg_rows(h_ref, None, j, n) + p * c)
            return p

        lax.fori_loop(0, n, p2, zero + 1.0)

    blk = pl.BlockSpec((L, LANE), lambda g: (0, g))
    return pl.pallas_call(
        body, grid=(BW // LANE,), name="rg_scan_fwd", in_specs=[blk, blk], out_specs=blk,
        out_shape=jax.ShapeDtypeStruct((L, BW), F32),
        compiler_params=pltpu.CompilerParams(vmem_limit_bytes=VMEM_LIMIT),
    )(a, b)


def rg_scan_bwd(a, h, dh, L):
    n = L // NSEG

    def body(a_ref, h_ref, dh_ref, da_ref, db_ref):
        zero = jnp.zeros((NSEG, LANE), F32)
        rows = lax.broadcasted_iota(jnp.int32, (NSEG, LANE), 0)
        a_edge = jnp.where(rows == NSEG - 1, 0.0, pltpu.roll(_seg_rows(a_ref, None, 0, n), NSEG - 1, 0))
        h_edge = jnp.where(rows == 0, 0.0, pltpu.roll(_seg_rows(h_ref, None, n - 1, n), 1, 0))

        def mult(j):
            return jnp.where(j == n - 1, a_edge, _seg_rows(a_ref, None, jnp.minimum(j + 1, n - 1), n))

        def p1(jj, st):
            j = n - 1 - jj
            g, p = st
            m = mult(j)
            g = m * g + _seg_rows(dh_ref, None, j, n)
            _seg_store(db_ref, None, j, n, g)
            return g, m * p

        e, pe = lax.fori_loop(0, n, p1, (zero, zero + 1.0))
        c, _ = _seg_carries(e, None, pe, None, True)

        def p2(jj, p):
            j = n - 1 - jj
            p = mult(j) * p
            g = _seg_rows(db_ref, None, j, n) + p * c
            _seg_store(db_ref, None, j, n, g)
            hp = jnp.where(j == 0, h_edge, _seg_rows(h_ref, None, jnp.maximum(j - 1, 0), n))
            _seg_store(da_ref, None, j, n, g * hp)
            return p

        lax.fori_loop(0, n, p2, zero + 1.0)

    blk = pl.BlockSpec((L, LANE), lambda g: (0, g))
    return pl.pallas_call(
        body, grid=(BW // LANE,), name="rg_scan_bwd", in_specs=[blk, blk, blk], out_specs=[blk, blk],
        out_shape=[jax.ShapeDtypeStruct((L, BW), F32)] * 2,
        compiler_params=pltpu.CompilerParams(vmem_limit_bytes=VMEM_LIMIT),
    )(a, h, dh)


def _hg_consts(C):
    t = lax.broadcasted_iota(jnp.int32, (C, C), 0)
    s = lax.broadcasted_iota(jnp.int32, (C, C), 1)
    tril = (s <= t).astype(F32)
    diag = (s == t).astype(F32)
    levels = []
    k = 1
    while (1 << k) <= C:
        m = 1 << (k - 1)
        same = (t >> k) == (s >> k)
        t_right = ((t >> (k - 1)) & 1) == 1
        s_left = ((s >> (k - 1)) & 1) == 0
        mask = jnp.logical_and(same, jnp.logical_and(t_right, s_left)).astype(F32)
        bnd = ((t >> k) << k) + (m - 1)
        levels.append((mask, (s <= bnd).astype(F32)))
        k += 1
    return tril, diag, levels


def hg_chunk(st, q, z, v, lb):
    C = q.shape[0]
    tril, diag, levels = _hg_consts(C)
    sig = jax.nn.sigmoid(z)
    lf = jnp.log(lb + (1.0 - lb) * sig)
    k = (1.0 - lb) * jax.nn.sigmoid(-z)
    qh = jax.nn.silu(q)
    b = mm_exact(tril, lf)
    blast = jnp.sum(lf, axis=0, keepdims=True)
    qe = qh * jnp.exp(b)
    kd = k * jnp.exp(blast - b)
    scaled = []
    for _, sel in levels:
        ref = mm_exact(sel, lf)
        scaled.append((qh * jnp.exp(jnp.minimum(b - ref, 0.0)), k * jnp.exp(jnp.minimum(ref - b, 0.0))))
    outs, news = [], []
    for h in range(HG_HEADS):
        sl = slice(h * HG_D, (h + 1) * HG_D)
        st_h = st[h * HG_D:(h + 1) * HG_D, :]
        sc = diag * mma_nt(qh[:, sl], k[:, sl])
        for (mask, _), (qt, kt) in zip(levels, scaled):
            sc = sc + mask * mma_nt(qt[:, sl], kt[:, sl])
        outs.append(mma_nt(qe[:, sl], st_h) + mma_nn(sc, v[:, sl]))
        news.append(st_h * jnp.exp(blast[:, sl]) + mma_tn(v[:, sl], kd[:, sl]))
    return jnp.concatenate(news, axis=0), jnp.concatenate(outs, axis=1)


def hg_fwd(qzv, lb, L):
    C = HG_CHUNK
    nc = L // C

    def body(q_ref, z_ref, v_ref, lb_ref, o_ref, sst_ref, st_ref):
        @pl.when(pl.program_id(0) == 0)
        def _():
            st_ref[...] = jnp.zeros_like(st_ref)

        st = st_ref[...]
        sst_ref[...] = st
        new, o = hg_chunk(st, q_ref[...], z_ref[...], v_ref[...], lb_ref[...])
        st_ref[...] = new
        o_ref[...] = o

    col = lambda cb: pl.BlockSpec((C, BW), functools.partial(lambda c, cb: (c, cb), cb=cb))
    return pl.pallas_call(
        body, grid=(nc,), name="hg_fwd",
        in_specs=[col(0), col(1), col(2), pl.BlockSpec((1, BW), lambda c: (0, 0))],
        out_specs=[pl.BlockSpec((C, BW), lambda c: (c, 0)), pl.BlockSpec((None, BW, HG_D), lambda c: (c, 0, 0))],
        out_shape=[jax.ShapeDtypeStruct((L, BW), F32), jax.ShapeDtypeStruct((nc, BW, HG_D), F32)],
        scratch_shapes=[pltpu.VMEM((BW, HG_D), F32)],
        compiler_params=pltpu.CompilerParams(vmem_limit_bytes=VMEM_LIMIT, dimension_semantics=("arbitrary",)),
    )(qzv, qzv, qzv, lb)


def hg_bwd(qzv, lb, sst, do, L):
    C = HG_CHUNK
    nc = L // C

    def body(q_ref, z_ref, v_ref, lb_ref, sst_ref, do_ref, dq_ref, dz_ref, dv_ref, dlb_ref, dst_ref):
        @pl.when(pl.program_id(0) == 0)
        def _():
            dst_ref[...] = jnp.zeros_like(dst_ref)
            dlb_ref[...] = jnp.zeros_like(dlb_ref)

        _, vjp = jax.vjp(hg_chunk, sst_ref[...], q_ref[...], z_ref[...], v_ref[...], lb_ref[...])
        dst, dq, dz, dv, dlb = vjp((dst_ref[...], do_ref[...]))
        dst_ref[...] = dst
        dq_ref[...] = dq
        dz_ref[...] = dz
        dv_ref[...] = dv
        dlb_ref[...] += dlb

    col = lambda cb: pl.BlockSpec((C, BW), functools.partial(lambda c, cb: (nc - 1 - c, cb), cb=cb))
    rev = pl.BlockSpec((C, BW), lambda c: (nc - 1 - c, 0))
    return pl.pallas_call(
        body, grid=(nc,), name="hg_bwd",
        in_specs=[col(0), col(1), col(2), pl.BlockSpec((1, BW), lambda c: (0, 0)),
                  pl.BlockSpec((None, BW, HG_D), lambda c: (nc - 1 - c, 0, 0)), rev],
        out_specs=[rev, rev, rev, pl.BlockSpec((1, BW), lambda c: (0, 0))],
        out_shape=[jax.ShapeDtypeStruct((L, BW), F32)] * 3 + [jax.ShapeDtypeStruct((1, BW), F32)],
        scratch_shapes=[pltpu.VMEM((BW, HG_D), F32)],
        compiler_params=pltpu.CompilerParams(vmem_limit_bytes=VMEM_LIMIT, dimension_semantics=("arbitrary",)),
    )(qzv, qzv, qzv, lb, sst, do)


def _shift_down(x, d, rows, L):
    if d == 0:
        return x
    wrapped = jnp.where((rows & (NSEG - 1)) == 0, 0.0, pltpu.roll(x, NSEG * d + 1, 0))
    return jnp.where(rows < NSEG * d, wrapped, pltpu.roll(x, NSEG * d, 0))


def _shift_up(x, d, rows, L):
    if d == 0:
        return x
    wrapped = jnp.where((rows & (NSEG - 1)) == NSEG - 1, 0.0, pltpu.roll(x, L - (NSEG * d + 1), 0))
    return jnp.where(rows >= L - NSEG * d, wrapped, pltpu.roll(x, L - NSEG * d, 0))


def conv_fwd(proj, w, b, L):
    def body(x_ref, w_ref, b_ref, o_ref):
        x = x_ref[...]
        rows = lax.broadcasted_iota(jnp.int32, x.shape, 0)
        acc = jnp.broadcast_to(b_ref[...], x.shape)
        for k in range(CONV_W):
            acc = acc + w_ref[pl.ds(k, 1), :] * _shift_down(x, CONV_W - 1 - k, rows, L)
        o_ref[...] = acc

    nl = BW // LANE
    return pl.pallas_call(
        body, grid=(nl,), name="conv_fwd",
        in_specs=[pl.BlockSpec((L, LANE), lambda g: (0, 5 * nl + g)), pl.BlockSpec((CONV_W, LANE), lambda g: (0, g)),
                  pl.BlockSpec((1, LANE), lambda g: (0, g))],
        out_specs=pl.BlockSpec((L, LANE), lambda g: (0, g)),
        out_shape=jax.ShapeDtypeStruct((L, BW), F32),
        compiler_params=pltpu.CompilerParams(vmem_limit_bytes=VMEM_LIMIT),
    )(proj, w, b)


def conv_bwd(proj, w, dxc, L):
    def body(x_ref, w_ref, d_ref, dx_ref, dw_ref, db_ref):
        x, d = x_ref[...], d_ref[...]
        rows = lax.broadcasted_iota(jnp.int32, x.shape, 0)
        acc = jnp.zeros_like(x)
        for k in range(CONV_W):
            acc = acc + w_ref[pl.ds(k, 1), :] * _shift_up(d, CONV_W - 1 - k, rows, L)
            dw_ref[pl.ds(k, 1), :] = jnp.sum(d * _shift_down(x, CONV_W - 1 - k, rows, L), axis=0, keepdims=True)
        dx_ref[...] = acc
        db_ref[...] = jnp.sum(d, axis=0, keepdims=True)

    nl = BW // LANE
    blk = pl.BlockSpec((L, LANE), lambda g: (0, g))
    return pl.pallas_call(
        body, grid=(nl,), name="conv_bwd",
        in_specs=[pl.BlockSpec((L, LANE), lambda g: (0, 5 * nl + g)), pl.BlockSpec((CONV_W, LANE), lambda g: (0, g)), blk],
        out_specs=[blk, pl.BlockSpec((CONV_W, LANE), lambda g: (0, g)), pl.BlockSpec((1, LANE), lambda g: (0, g))],
        out_shape=[jax.ShapeDtypeStruct((L, BW), F32), jax.ShapeDtypeStruct((CONV_W, BW), F32),
                   jax.ShapeDtypeStruct((1, BW), F32)],
        compiler_params=pltpu.CompilerParams(vmem_limit_bytes=VMEM_LIMIT),
    )(proj, w, dxc)


def loss_fwd_bwd(x, fw, target, L, tm):
    def fn(x, fw, t):
        err = jnp.square(_rms(x, fw) - t)
        return jnp.sum(0.5 * jnp.mean(err, axis=-1, keepdims=True), axis=0, keepdims=True)

    def body(x_ref, fw_ref, t_ref, l_ref, dx_ref, dfw_ref):
        i = pl.program_id(0)
        t = t_ref[...]
        val, vjp = jax.vjp(lambda x, fw: fn(x, fw, t), x_ref[...], fw_ref[...])
        dx, dfw = vjp(jnp.ones((1, 1), F32))
        dx_ref[...] = dx

        @pl.when(i == 0)
        def _():
            l_ref[...] = jnp.zeros_like(l_ref)
            dfw_ref[...] = jnp.zeros_like(dfw_ref)

        l_ref[...] += jnp.broadcast_to(val, l_ref.shape)
        dfw_ref[...] += dfw

    row = pl.BlockSpec((tm, D_MODEL), lambda i: (i, 0))
    vec = pl.BlockSpec((1, D_MODEL), lambda i: (0, 0))
    return pl.pallas_call(
        body, grid=(L // tm,), name="loss_fwd_bwd", in_specs=[row, vec, row],
        out_specs=[pl.BlockSpec((1, LANE), lambda i: (0, 0)), row, vec],
        out_shape=[jax.ShapeDtypeStruct((1, LANE), F32), jax.ShapeDtypeStruct((L, D_MODEL), F32),
                   jax.ShapeDtypeStruct((1, D_MODEL), F32)],
        compiler_params=pltpu.CompilerParams(vmem_limit_bytes=VMEM_LIMIT, dimension_semantics=("arbitrary",)),
    )(x, fw, target)


def adamw(w, g, m, v):
    rows, cols = w.shape
    tr = _row_tile(rows, cols, budget=1024 * 1024)
    c1 = 1.0 - ADAM_B1 ** ADAM_STEP
    c2 = 1.0 - ADAM_B2 ** ADAM_STEP

    def body(w_ref, g_ref, m_ref, v_ref, d_ref, nm_ref, nv_ref):
        g = g_ref[...]
        nm = ADAM_B1 * m_ref[...] + (1.0 - ADAM_B1) * g
        nv = ADAM_B2 * v_ref[...] + (1.0 - ADAM_B2) * jnp.square(g)
        d_ref[...] = -ADAM_LR * ((nm / c1) / (jnp.sqrt(nv / c2) + ADAM_EPS) + ADAM_WD * w_ref[...])
        nm_ref[...] = nm
        nv_ref[...] = nv

    blk = pl.BlockSpec((tr, cols), lambda i: (i, 0))
    return pl.pallas_call(
        body, grid=(rows // tr,), name="adamw", in_specs=[blk] * 4, out_specs=[blk] * 3,
        out_shape=[jax.ShapeDtypeStruct((rows, cols), F32)] * 3,
    )(w, g, m, v)


def s5_prep(lam_re, lam_im, log_dt, b_re, b_im, c_re, c_im):
    lr = jnp.minimum(lam_re, -1e-4)
    li = lam_im
    dt = jnp.exp(log_dt)[:, None]
    mag = jnp.exp(lr * dt)
    ar = mag * jnp.cos(li * dt)
    ai = mag * jnp.sin(li * dt)
    den = lr * lr + li * li
    fr = ((ar - 1.0) * lr + ai * li) / den
    fi = (ai * lr - (ar - 1.0) * li) / den
    bbr = fr[..., None] * b_re - fi[..., None] * b_im
    bbi = fr[..., None] * b_im + fi[..., None] * b_re
    emb_b = lambda bb: _block_diag(bb.transpose(0, 2, 1).reshape(BW, S5_STATE), S5_GROUPS)
    emb_c = lambda cc: _block_diag(cc.transpose(0, 2, 1).reshape(S5_N, S5_GROUP), S5_GROUPS)
    bmat = jnp.concatenate([emb_b(bbr), emb_b(bbi)], axis=1)
    cmat = jnp.concatenate([emb_c(c_re), -emb_c(c_im)], axis=0)
    nb = S5_N // LANE
    return ar.reshape(nb, 1, LANE), ai.reshape(nb, 1, LANE), bmat, cmat


def _block_diag(stacked, groups):
    rows, c = stacked.shape
    r = rows // groups
    row_g = jnp.arange(rows)[:, None] // r
    col_g = jnp.arange(groups * c)[None, :] // c
    return jnp.where(row_g == col_g, jnp.tile(stacked, (1, groups)), 0.0)


def rg_prep(w):
    return _block_diag(w.reshape(BW, RG_BLOCK), RG_BLOCKS)


def hg_prep(logits):
    p = jax.nn.softmax(logits, axis=0)
    return jnp.cumsum(p, axis=0) - p[0]


def _head_mean_matrix():
    r = jnp.arange(BW) // HG_D
    return (r[:, None] == r[None, :]).astype(F32) / HG_D


def _to_segment_order(a):
    L = a.shape[0]
    return a.reshape(NSEG, L // NSEG, -1).transpose(1, 0, 2).reshape(a.shape)


def _to_time_order(a):
    L = a.shape[0]
    return a.reshape(L // NSEG, NSEG, -1).transpose(1, 0, 2).reshape(a.shape)


def _const(*idx):
    return lambda s, i: idx


def _rows(cb=0):
    return lambda s, i: (i, cb)


def _sum_parts(name, first, parts, shape):
    return add_n(name, [(first, ())] + [(parts, (s,)) for s in range(NSH)], shape)


def _ffn_weight_specs(l, j):
    F = D_FF // NSH
    one = pl.Buffered(1)
    return [pl.BlockSpec((None, NSH, D_MODEL, F), lambda i: (j, 0, 0, 0), pipeline_mode=one),
            pl.BlockSpec((None, NSH, D_MODEL, F), lambda i: (j, 0, 0, 0), pipeline_mode=one),
            pl.BlockSpec((None, NSH, F, D_MODEL), lambda i: (j, 0, 0, 0), pipeline_mode=one)]


def ffn_fwd(name, x, W, l, j, k, L, tm):
    D, F = D_MODEL, D_FF // NSH

    def body(x_ref, nw_ref, wg_ref, wu_ref, wd_ref, y_ref, g_ref, u_ref):
        x = x_ref[...]
        h = _rms(x, nw_ref[...]).astype(MMT)
        y = x
        for s in range(NSH):
            g = _dg(h, wg_ref[s], 1, 0)
            u = _dg(h, wu_ref[s], 1, 0)
            g_ref[s] = g.astype(g_ref.dtype)
            u_ref[s] = u.astype(u_ref.dtype)
            y = y + 0.5 * _dg((jax.nn.silu(g) * u).astype(MMT), wd_ref[s], 1, 0)
        y_ref[...] = y

    row = pl.BlockSpec((tm, D), lambda i: (i, 0))
    act = pl.BlockSpec((NSH, tm, F), lambda i: (0, i, 0))
    return pl.pallas_call(
        body, grid=(L // tm,), name=name,
        in_specs=[row, pl.BlockSpec((None, None, 1, D), lambda i: (l, k, 0, 0))] + _ffn_weight_specs(l, j),
        out_specs=[row, act, act],
        out_shape=[jax.ShapeDtypeStruct((L, D), F32), jax.ShapeDtypeStruct((NSH, L, F), MMT),
                   jax.ShapeDtypeStruct((NSH, L, F), MMT)],
        compiler_params=pltpu.CompilerParams(vmem_limit_bytes=VMEM_LIMIT, dimension_semantics=("arbitrary",)),
    )(x, W["nw"], W["L"][l]["wg"], W["L"][l]["wu"], W["L"][l]["wd"])


def ffn_bwd(name, x, g, u, dy, W, bufs, l, j, k, L, tm):
    D, F = D_MODEL, D_FF // NSH

    def body(x_ref, nw_ref, dy_ref, g_ref, u_ref, wg_ref, wu_ref, wd_ref, *rest):
        part_ref, dnw_ref, dwg_ref, dwu_ref, dwd_ref = rest[-5:]
        s, i = pl.program_id(0), pl.program_id(1)
        x, nw = x_ref[...], nw_ref[...]
        r = lax.rsqrt(jnp.mean(x * x, axis=-1, keepdims=True) + EPS)
        xhat = x * r
        h = (xhat * nw).astype(MMT)
        half_dy = (0.5 * dy_ref[...]).astype(MMT)
        gs, us = g_ref[...].astype(F32), u_ref[...].astype(F32)
        sig = jax.nn.sigmoid(gs)
        act = gs * sig
        da = _dg(half_dy, wd_ref[...], 1, 1)
        du = (da * act).astype(MMT)
        dg = (da * us * (sig * (1.0 + gs * (1.0 - sig)))).astype(MMT)
        dh = _dg(dg, wg_ref[...], 1, 1) + _dg(du, wu_ref[...], 1, 1)
        dxh = dh * nw
        part_ref[...] = r * (dxh - xhat * jnp.mean(dxh * xhat, axis=-1, keepdims=True))
        grads = (_dg(h, dg, 0, 0), _dg(h, du, 0, 0), _dg((act * us).astype(MMT), half_dy, 0, 0))
        dnw = jnp.sum(dh * xhat, axis=0, keepdims=True)
        first = jnp.logical_and(s == 0, i == 0)
        for ref, val, start in zip((dwg_ref, dwu_ref, dwd_ref, dnw_ref), grads + (dnw,), (i == 0, i == 0, i == 0, first)):
            @pl.when(start)
            def _(ref=ref, val=val):
                ref[...] = val

            @pl.when(jnp.logical_not(start))
            def _(ref=ref, val=val):
                ref[...] += val

    keys = ("ffn_gate", "ffn_up", "ffn_down")
    given = [bufs[key] for key in keys if bufs.get(key) is not None]
    row = pl.BlockSpec((tm, D), lambda s, i: (i, 0))
    act = pl.BlockSpec((None, tm, F), lambda s, i: (s, i, 0))
    wsp = lambda r, c: pl.BlockSpec((None, None, r, c), lambda s, i: (j, s, 0, 0))
    stk = lambda r, c: pl.BlockSpec((None, None, r, c), lambda s, i: (2 * l + j, s, 0, 0))
    part, dnw, bufs["ffn_gate"], bufs["ffn_up"], bufs["ffn_down"] = pl.pallas_call(
        body, grid=(NSH, L // tm), name=name,
        in_specs=[row, pl.BlockSpec((None, None, 1, D), lambda s, i: (l, k, 0, 0)), row, act, act,
                  wsp(D, F), wsp(D, F), wsp(F, D)] + [pl.BlockSpec(memory_space=pl.ANY)] * len(given),
        out_specs=[pl.BlockSpec((None, tm, D), lambda s, i: (s, i, 0)), pl.BlockSpec((1, D), lambda s, i: (0, 0)),
                   stk(D, F), stk(D, F), stk(F, D)],
        out_shape=[jax.ShapeDtypeStruct((NSH, L, D), F32), jax.ShapeDtypeStruct((1, D), F32)]
        + [jax.ShapeDtypeStruct((2 * DEPTH, NSH, D, F), F32)] * 2 + [jax.ShapeDtypeStruct((2 * DEPTH, NSH, F, D), F32)],
        input_output_aliases={8 + n: 2 + n for n in range(len(given))},
        compiler_params=pltpu.CompilerParams(vmem_limit_bytes=VMEM_LIMIT, dimension_semantics=("arbitrary", "arbitrary")),
    )(x, W["nw"], dy, g, u, W["L"][l]["wg"], W["L"][l]["wu"], W["L"][l]["wd"], *given)
    return _sum_parts(name + "_dx", dy, part, (L, D)), dnw


def layer_fwd(l, x0, W, P, L, tm):
    D = D_MODEL
    n_i = L // tm
    x1, g0, u0 = ffn_fwd(f"ffn_fwd_{l}0", x0, W, l, 0, 0, L, tm)
    proj = tile_fwd(
        lambda x, nw, win, s: pre_core(x, nw, win), f"pre_fwd_{l}", n_i, NSH,
        [(x1, (tm, D), _rows()), (W["nw"], (None, None, 1, D), _const(l, 1, 0, 0)),
         (W["L"][l]["win"], (None, D, IN_TOTAL // NSH), lambda s, i: (s, 0, 0))],
        [((L, IN_TOTAL), F32, (tm, IN_TOTAL // NSH), lambda s, i: (i, s), False)], s_outer=True)[0]
    nb = S5_N // LANE
    blk3 = lambda s, i: (0, i, 0)
    bur, bui = tile_fwd(
        lambda u, bmat, s: s5_pre_core(u, bmat), f"s5pre_fwd_{l}", n_i, 1,
        [(proj, (tm, BW), _rows(0)), (P["bmat"], (None, BW, 2 * S5_N), _const(l, 0, 0))],
        [((nb, L, LANE), F32, (nb, tm, LANE), blk3, False)] * 2)
    xr, xi = s5_scan_fwd(bur, bui, P["ar"][l], P["ai"][l], L)
    qzv = _to_time_order(proj[:, BW:4 * BW])
    o_t, sst = hg_fwd(qzv, P["lb"][l], L)
    o = _to_segment_order(o_t)
    xc = conv_fwd(proj, W["convw"][l], P["convb"][l], L)
    vec = (None, 1, BW)
    a, b = tile_fwd(
        lambda xc, wa, ba, wx, bx, lam, s: gates_core(xc, wa, ba, wx, bx, lam), f"gates_fwd_{l}", n_i, 1,
        [(xc, (tm, BW), _rows()), (P["wa"], (None, BW, BW), _const(l, 0, 0)), (P["ba"], vec, _const(l, 0, 0)),
         (P["wx"], (None, BW, BW), _const(l, 0, 0)), (P["bx"], vec, _const(l, 0, 0)), (P["lam"], vec, _const(l, 0, 0))],
        [((L, BW), F32, (tm, BW), _rows(), False)] * 2)
    hs = rg_scan_fwd(a, b, L)
    tmm = tm
    ya, yb, yc = tile_fwd(
        lambda *a: mid_core(*a[:-1]), f"mid_fwd_{l}", L // tmm, 1,
        [(xr, (nb, tmm, LANE), blk3), (xi, (nb, tmm, LANE), blk3), (proj, (tmm, BW), _rows(0)), (o, (tmm, BW), _rows()),
         (proj, (tmm, BW), _rows(4)), (hs, (tmm, BW), _rows()), (proj, (tmm, BW), _rows(6)),
         (P["hmat"], (BW, BW), _const(0, 0)), (P["cmat"], (None, 2 * S5_N, BW), _const(l, 0, 0)), (P["d"], vec, _const(l, 0, 0)),
         (W["L"][l]["gluw"], (BW, BW), _const(0, 0)), (P["glub"], vec, _const(l, 0, 0)), (P["hgw"], vec, _const(l, 0, 0))],
        [((L, BW), F32, (tmm, BW), _rows(), False)] * 3)
    x2 = tile_fwd(
        lambda x, *rest: (x + merge_core(*rest[:-1])[0],), f"merge_fwd_{l}", n_i, 1,
        [(x1, (tm, D), _rows()), (ya, (tm, BW), _rows()), (yb, (tm, BW), _rows()), (yc, (tm, BW), _rows())]
        + [(proj, (tm, BW), _rows(7 + k)) for k in range(6)]
        + [(W["L"][l]["pfull"], (3, BW, D), _const(0, 0, 0)), (W["L"][l]["woutfull"], (D, D), _const(0, 0))],
        [((L, D), F32, (tm, D), _rows(), False)])[0]
    x3, g1, u1 = ffn_fwd(f"ffn_fwd_{l}1", x2, W, l, 1, 2, L, tm)
    saved = dict(x0=x0, x1=x1, x2=x2, proj=proj, xr=xr, xi=xi, o=o, sst=sst, xc=xc, a=a, hs=hs, ya=ya, yb=yb, yc=yc,
                 qzv=qzv, g0=g0, u0=u0, g1=g1, u1=u1)
    return x3, saved


def layer_bwd(l, dx3, sv, W, P, bufs, L, tm, ready=lambda l, group: None):
    D = D_MODEL
    n_i = L // tm
    nb = S5_N // LANE
    dq = D // NSH
    vec = (None, 1, BW)
    vout = ((1, BW), (1, BW), _const(0, 0), "acc_all")
    blk3 = lambda s, i: (0, i, 0)
    small = {}
    proj = sv["proj"]

    dx2, dnw2 = ffn_bwd(f"ffn_bwd_{l}1", sv["x2"], sv["g1"], sv["u1"], dx3, W, bufs, l, 1, 2, L, tm)
    ready(l, "ffn1")

    rw256 = ((L, BW), (tm, BW), _rows(), "write")
    res = tile_bwd(
        merge_core, f"merge_bwd_{l}", n_i, 1,
        [(sv["ya"], (tm, BW), _rows(), "r"), (sv["yb"], (tm, BW), _rows(), "r"), (sv["yc"], (tm, BW), _rows(), "r")]
        + [(proj, (tm, BW), _rows(7 + k), "r") for k in range(6)]
        + [(W["L"][l]["pfull"], (3, BW, D), _const(0, 0, 0), "w"), (W["L"][l]["woutfull"], (D, D), _const(0, 0), "w")],
        [(dx2, (tm, D), _rows())],
        [rw256] * 9
        + [((DEPTH, 3, BW, D), (None, 3, BW, D), _const(l, 0, 0, 0), "acc_all", bufs.get("branch_proj")),
           ((DEPTH, D, D), (None, D, D), _const(l, 0, 0), "acc_all", bufs.get("w_out"))])
    dya, dyb, dyc = res[:3]
    dgm = res[3:9]
    bufs["branch_proj"], bufs["w_out"] = res[9:]
    ready(l, "merge")

    tmm = min(tm, 128)
    rw = ((L, BW), (tmm, BW), _rows(), "write")
    xw = ((nb, L, LANE), (nb, tmm, LANE), blk3, "write")
    res = tile_bwd(
        mid_core, f"mid_bwd_{l}", L // tmm, 1,
        [(sv["xr"], (nb, tmm, LANE), blk3, "r"), (sv["xi"], (nb, tmm, LANE), blk3, "r"), (proj, (tmm, BW), _rows(0), "r"),
         (sv["o"], (tmm, BW), _rows(), "r"), (proj, (tmm, BW), _rows(4), "r"), (sv["hs"], (tmm, BW), _rows(), "r"),
         (proj, (tmm, BW), _rows(6), "r"), (P["hmat"], (BW, BW), _const(0, 0), "c"),
         (P["cmat"], (None, 2 * S5_N, BW), _const(l, 0, 0), "w"), (P["d"], vec, _const(l, 0, 0), "p"),
         (W["L"][l]["gluw"], (BW, BW), _const(0, 0), "w"), (P["glub"], vec, _const(l, 0, 0), "p"),
         (P["hgw"], vec, _const(l, 0, 0), "p")],
        [(dya, (tmm, BW), _rows()), (dyb, (tmm, BW), _rows()), (dyc, (tmm, BW), _rows())],
        [xw, xw, rw, rw, rw, rw, rw,
         ((DEPTH, 2 * S5_N, BW), (None, 2 * S5_N, BW), _const(l, 0, 0), "acc_all", bufs.get("cmat")), vout,
         ((DEPTH, BW, BW), (None, BW, BW), _const(l, 0, 0), "acc_all", bufs.get("s5_glu_w")), vout, vout])
    dxr, dxi, du_skip, do, dg_b, dhs, dgate_c, bufs["cmat"], dd, bufs["s5_glu_w"], dglub, dhgw = res
    small["s5_d"], small["s5_glu_b"], small["hg_norm_w"] = dd[0], dglub[0], dhgw[0]
    ready(l, "mid")

    da, db = rg_scan_bwd(sv["a"], sv["hs"], dhs, L)
    wmat = lambda key: ((DEPTH, BW, BW), (None, BW, BW), _const(l, 0, 0), "acc_all", bufs.get(key))
    res = tile_bwd(
        gates_core, f"gates_bwd_{l}", n_i, 1,
        [(sv["xc"], (tm, BW), _rows(), "r"), (P["wa"], (None, BW, BW), _const(l, 0, 0), "w"), (P["ba"], vec, _const(l, 0, 0), "p"),
         (P["wx"], (None, BW, BW), _const(l, 0, 0), "w"), (P["bx"], vec, _const(l, 0, 0), "p"), (P["lam"], vec, _const(l, 0, 0), "p")],
        [(da, (tm, BW), _rows()), (db, (tm, BW), _rows())],
        [((L, BW), (tm, BW), _rows(), "write"), wmat("wa"), vout, wmat("wx"), vout, vout])
    dxc, bufs["wa"], dba, bufs["wx"], dbx, dlam = res
    small["rg_ba"], small["rg_bx"], small["rg_lambda"] = dba[0], dbx[0], dlam[0]
    dx_c, dconvw, dconvb = conv_bwd(proj, W["convw"][l], dxc, L)
    small["rg_conv_w"], small["rg_conv_b"] = dconvw, dconvb[0]

    dq_b, dz_b, dv_b, dlb = hg_bwd(sv["qzv"], P["lb"][l], sv["sst"], _to_time_order(do), L)
    dq_b, dz_b, dv_b = [_to_segment_order(a) for a in (dq_b, dz_b, dv_b)]

    gr, gi, dar, dai = s5_scan_bwd(dxr, dxi, sv["xr"], sv["xi"], P["ar"][l], P["ai"][l], L)
    du_pre, bufs["bmat"] = tile_bwd(
        s5_pre_core, f"s5pre_bwd_{l}", n_i, 1,
        [(proj, (tm, BW), _rows(0), "r"), (P["bmat"], (None, BW, 2 * S5_N), _const(l, 0, 0), "w")],
        [(gr, (nb, tm, LANE), blk3), (gi, (nb, tm, LANE), blk3)],
        [((L, BW), (tm, BW), _rows(), "write"),
         ((DEPTH, BW, 2 * S5_N), (None, BW, 2 * S5_N), _const(l, 0, 0), "acc_all", bufs.get("bmat"))])
    du_a = add_n(f"du_a_{l}", [(du_skip, ()), (du_pre, ())], (L, BW))
    prep_ct = dict(dar=dar, dai=dai, dlb=dlb)

    pieces = [du_a, dq_b, dz_b, dv_b, dg_b, dx_c, dgate_c, *dgm]
    per_piece, per_shard = BW // LANE, IN_TOTAL // NSH // LANE
    part, dnw1 = None, []
    for s in range(NSH):
        groups = [(pieces[g // per_piece], (tm, LANE), _rows(g % per_piece))
                  for g in range(s * per_shard, (s + 1) * per_shard)]
        part, dnw_s, bufs["w_in"] = tile_bwd(
            pre_core, f"pre_bwd_{l}{s}", n_i, 1,
            [(sv["x1"], (tm, D), _rows(), "r"), (W["nw"], (None, None, 1, D), _const(l, 1, 0, 0), "p"),
             (W["L"][l]["win"], (None, D, IN_TOTAL // NSH), _const(s, 0, 0), "w")],
            [groups],
            [((NSH, L, D), (None, tm, D), functools.partial(lambda _s, i, s: (s, i, 0), s=s), "write", part),
             ((1, D), (1, D), _const(0, 0), "acc_all"),
             ((DEPTH, NSH, D, IN_TOTAL // NSH), (None, None, D, IN_TOTAL // NSH), _const(l, s, 0, 0), "acc_all",
              bufs.get("w_in"))])
        dnw1.append(dnw_s)
    dnw1 = (dnw1[0] + dnw1[1]) + (dnw1[2] + dnw1[3])
    dx1 = _sum_parts(f"pre_bwd_{l}_dx", dx2, part, (L, D))
    ready(l, "pre")

    dx0, dnw0 = ffn_bwd(f"ffn_bwd_{l}0", sv["x0"], sv["g0"], sv["u0"], dx1, W, bufs, l, 0, 0, L, tm)
    ready(l, "ffn0")
    small["norm_w"] = jnp.concatenate([dnw0, dnw1, dnw2], axis=0)
    return dx0, small, prep_ct


SMALL_RAW = ("s5_lambda_re", "s5_lambda_im", "s5_log_dt", "s5_b_re", "s5_b_im", "s5_c_re", "s5_c_im", "s5_d", "s5_glu_b",
             "hg_lb_logits", "hg_norm_w", "rg_conv_b", "rg_wa", "rg_ba", "rg_wx", "rg_bx", "rg_lambda", "final_norm_w")
DEPTH = 2


def local_step(x, target, W, raw, layer_weights=None, layer_grads=None):
    L = x.shape[0]
    tm = min(256, L)
    col = lambda v: v.reshape(DEPTH, 1, BW)
    (ar, ai, bmat, cmat), s5_vjp = jax.vjp(jax.vmap(s5_prep), *[raw[k] for k in SMALL_RAW[:7]])
    (wa, wx), rg_vjp = jax.vjp(lambda a, b: (jax.vmap(rg_prep)(a), jax.vmap(rg_prep)(b)), raw["rg_wa"], raw["rg_wx"])
    lb, hg_vjp = jax.vjp(hg_prep, raw["hg_lb_logits"])
    P = dict(
        ar=[ar[l] for l in range(DEPTH)], ai=[ai[l] for l in range(DEPTH)],
        bmat=bmat.astype(MMT), cmat=cmat.astype(MMT), wa=wa.astype(MMT), wx=wx.astype(MMT),
        lb=[lb[l].reshape(1, BW) for l in range(DEPTH)], convb=[raw["rg_conv_b"][l].reshape(1, BW) for l in range(DEPTH)],
        ba=col(raw["rg_ba"]), bx=col(raw["rg_bx"]), lam=col(raw["rg_lambda"]), d=col(raw["s5_d"]),
        glub=col(raw["s5_glu_b"]), hgw=col(raw["hg_norm_w"]), hmat=_head_mean_matrix())

    saved = []
    h = _to_segment_order(x)
    for l in range(DEPTH):
        if layer_weights is not None:
            W["L"][l], h = layer_weights(l, h)
        h, sv = layer_fwd(l, h, W, P, L, tm)
        saved.append(sv)
    loss, dh, dfw = loss_fwd_bwd(h, raw["final_norm_w"].reshape(1, D_MODEL), _to_segment_order(target), L, tm)

    big, per_layer, prep_cts = {}, [None] * DEPTH, [None] * DEPTH
    ready = (lambda l, group: None) if layer_grads is None else (lambda l, group: layer_grads(l, group, big))
    for l in reversed(range(DEPTH)):
        dh, sm, pc = layer_bwd(l, dh, saved[l], W, P, big, L, tm, ready)
        per_layer[l], prep_cts[l] = sm, pc
    dh = _to_time_order(dh)

    small = {k: jnp.stack([per_layer[l][k] for l in range(DEPTH)]) for k in per_layer[0]}
    both = lambda k: jnp.stack([prep_cts[l][k] for l in range(DEPTH)])
    s5_g = s5_vjp((both("dar"), both("dai"), big.pop("bmat"), big.pop("cmat")))
    small.update(zip(SMALL_RAW[:7], s5_g))
    small["rg_wa"], small["rg_wx"] = rg_vjp((big.pop("wa"), big.pop("wx")))
    (small["hg_lb_logits"],) = hg_vjp(jnp.concatenate([prep_cts[l]["dlb"] for l in range(DEPTH)], axis=0))
    small["final_norm_w"] = dfw[0]
    return loss, dh, big, small


ANY = pl.BlockSpec(memory_space=pl.ANY)


def _place():
    x, y, c = lax.axis_index("x"), lax.axis_index("y"), lax.axis_index("c")
    chips = [(1 - x, y), (x, 1 - y), (1 - x, 1 - y)]
    return x, y, c, chips


def _remote(src, dst, send, recv, k, to):
    return pltpu.make_async_remote_copy(src_ref=src, dst_ref=dst, send_sem=send.at[k], recv_sem=recv.at[k],
                                        device_id=to, device_id_type=MESH)


def _comm_call(body, name, ins, out_shapes, n_sem, n_loc):
    return pl.pallas_call(
        body, name=name, in_specs=[ANY] * len(ins), out_specs=[ANY] * len(out_shapes), out_shape=out_shapes,
        scratch_shapes=[pltpu.SemaphoreType.DMA((n_sem,)), pltpu.SemaphoreType.DMA((n_sem,)),
                        pltpu.SemaphoreType.DMA((max(n_loc, 1),))],
    )(*ins)


def gather_shards(name, shards):
    n = len(shards)
    per = 8

    def body(*refs):
        ins, outs = refs[:n], refs[n:2 * n]
        send, recv, _ = refs[2 * n:]
        x, y, c, chips = _place()
        me = 2 * x + y
        sib = (x, y, 1 - c)
        sends = []
        for w in range(n):
            for j, (cx, cy) in enumerate(chips):
                cp = _remote(ins[w].at[c], outs[w].at[c, me], send, recv, per * w + j, (cx, cy, c))
                cp.start()
                sends.append(cp)
        for w in range(n):
            for l in range(2):
                cp = _remote(ins[w].at[l], outs[w].at[l, me], send, recv, per * w + 6 + l, sib)
                cp.start()
                sends.append(cp)
        for w in range(n):
            for j, (cx, cy) in enumerate(chips):
                theirs = outs[w].at[c, 2 * cx + cy]
                _remote(ins[w].at[c], theirs, send, recv, per * w + j, (cx, cy, c)).wait_recv()
                cp = _remote(theirs, theirs, send, recv, per * w + 3 + j, sib)
                cp.start()
                sends.append(cp)
        for w in range(n):
            for j, (cx, cy) in enumerate(chips):
                dst = outs[w].at[1 - c, 2 * cx + cy]
                _remote(dst, dst, send, recv, per * w + 3 + j, sib).wait_recv()
            for l in range(2):
                dst = outs[w].at[l, me]
                _remote(dst, dst, send, recv, per * w + 6 + l, sib).wait_recv()
        for cp in sends:
            cp.wait_send()

    shapes = [jax.ShapeDtypeStruct((2, NSH) + s.shape[1:], s.dtype) for s in shards]
    return _comm_call(body, name, shards, shapes, per * n, 0)


def exchange_halves(name, grads, ranges):
    n = len(grads)

    def body(*refs):
        ins, outs = refs[:n], refs[n:2 * n]
        send, recv, _ = refs[2 * n:]
        x, y, c, _chips = _place()
        cps = []
        for w in range(n):
            h = grads[w].shape[2] // 2
            p0, np_ = ranges[w]
            cp = _remote(ins[w].at[pl.ds(p0, np_), :, pl.ds((1 - c) * h, h)], outs[w], send, recv, w, (x, y, 1 - c))
            cp.start()
            cps.append(cp)
        for cp in cps:
            cp.wait()

    shapes = [jax.ShapeDtypeStruct((r[1], NSH, g.shape[2] // 2, g.shape[3]), g.dtype) for g, r in zip(grads, ranges)]
    return _comm_call(body, name, grads, shapes, n, 0)


def scatter_to_chips(name, halves):
    n = len(halves)

    def body(*refs):
        ins, outs = refs[:n], refs[n:2 * n]
        send, recv, _ = refs[2 * n:]
        x, y, c, chips = _place()
        cps = []
        for w in range(n):
            for j, (cx, cy) in enumerate(chips):
                cp = _remote(ins[w].at[:, 2 * cx + cy], outs[w].at[j], send, recv, 3 * w + j, (cx, cy, c))
                cp.start()
                cps.append(cp)
        for cp in cps:
            cp.wait()

    shapes = [jax.ShapeDtypeStruct((3, h.shape[0]) + h.shape[2:], h.dtype) for h in halves]
    return _comm_call(body, name, halves, shapes, 3 * n, 0)


def share_halves(name, pieces):
    n = len(pieces)

    def body(*refs):
        ins, outs = refs[:n], refs[n:2 * n]
        send, recv, _ = refs[2 * n:]
        x, y, c, _chips = _place()
        cps = []
        for w in range(n):
            cp = _remote(ins[w], outs[w], send, recv, w, (x, y, 1 - c))
            cp.start()
            cps.append(cp)
        for cp in cps:
            cp.wait()

    return _comm_call(body, name, pieces, [jax.ShapeDtypeStruct(p.shape, p.dtype) for p in pieces], n, 0)


def add_own_half(name, g, ra, c, wire, b0):
    nblk, h, cols = ra.shape
    tr = _row_tile(h, cols, mult=16)
    nt = h // tr

    def body(c_ref, g_ref, r_ref, o_ref):
        o_ref[...] = (g_ref[...] + r_ref[...]).astype(o_ref.dtype)

    blk = (None, tr, cols)
    return pl.pallas_call(
        body, name=name,
        grid_spec=pltpu.PrefetchScalarGridSpec(
            num_scalar_prefetch=1, grid=(nblk, nt),
            in_specs=[pl.BlockSpec(blk, lambda s, i, c_ref: (b0 + s, c_ref[0] * nt + i, 0)), pl.BlockSpec(blk, lambda s, i, c_ref: (s, i, 0))],
            out_specs=pl.BlockSpec(blk, lambda s, i, c_ref: (s, i, 0))),
        out_shape=jax.ShapeDtypeStruct(ra.shape, wire),
    )(c.reshape(1), g, ra)


def add_chips(name, hb, rb, me):
    npc, _, h, cols = hb.shape
    tr = _row_tile(h, cols, mult=16)

    def body(me_ref, h_ref, r0, r1, r2, o_ref):
        f = lambda r: r[...].astype(F32)
        o_ref[...] = ((f(h_ref) + f(r0)) + f(r1)) + f(r2)

    rspec = lambda j: pl.BlockSpec((None, None, tr, cols), functools.partial(lambda p, i, me_ref, j: (j, p, i, 0), j=j))
    return pl.pallas_call(
        body, name=name,
        grid_spec=pltpu.PrefetchScalarGridSpec(
            num_scalar_prefetch=1, grid=(npc, h // tr),
            in_specs=[pl.BlockSpec((None, None, tr, cols), lambda p, i, me_ref: (p, me_ref[0], i, 0)), rspec(0), rspec(1), rspec(2)],
            out_specs=pl.BlockSpec((None, tr, cols), lambda p, i, me_ref: (p, i, 0))),
        out_shape=jax.ShapeDtypeStruct((npc, h, cols), F32),
    )(me.reshape(1), hb, rb, rb, rb)


def adamw_halves(name, w, m, v, own, other, c):
    npc, rows, cols = w.shape
    h = rows // 2
    tr = _row_tile(h, cols, budget=1024 * 1024)
    nt = h // tr
    c1 = 1.0 - ADAM_B1 ** ADAM_STEP
    c2 = 1.0 - ADAM_B2 ** ADAM_STEP

    def body(c_ref, w_ref, m_ref, v_ref, own_ref, oth_ref, g_ref, d_ref, nm_ref, nv_ref):
        g = jnp.where(pl.program_id(1) == c_ref[0], own_ref[...], oth_ref[...])
        nm = ADAM_B1 * m_ref[...] + (1.0 - ADAM_B1) * g
        nv = ADAM_B2 * v_ref[...] + (1.0 - ADAM_B2) * jnp.square(g)
        g_ref[...] = g
        d_ref[...] = -ADAM_LR * ((nm / c1) / (jnp.sqrt(nv / c2) + ADAM_EPS) + ADAM_WD * w_ref[...])
        nm_ref[...] = nm
        nv_ref[...] = nv

    full = pl.BlockSpec((None, tr, cols), lambda p, hh, i, c_ref: (p, hh * nt + i, 0))
    half = pl.BlockSpec((None, tr, cols), lambda p, hh, i, c_ref: (p, i, 0))
    return pl.pallas_call(
        body, name=name,
        grid_spec=pltpu.PrefetchScalarGridSpec(
            num_scalar_prefetch=1, grid=(npc, 2, nt),
            in_specs=[full, full, full, half, half], out_specs=[full] * 4),
        out_shape=[jax.ShapeDtypeStruct(w.shape, F32)] * 4,
    )(c.reshape(1), w, m, v, own, other)


WEIGHTS = ("norm_w", "final_norm_w", "ffn_gate", "ffn_up", "ffn_down", "w_in", "branch_proj", "w_out", "s5_lambda_re",
           "s5_lambda_im", "s5_log_dt", "s5_b_re", "s5_b_im", "s5_c_re", "s5_c_im", "s5_d", "s5_glu_w", "s5_glu_b",
           "hg_lb_logits", "hg_norm_w", "rg_conv_w", "rg_conv_b", "rg_wa", "rg_ba", "rg_wx", "rg_bx", "rg_lambda")
BIG = ("ffn_gate", "ffn_up", "ffn_down", "w_in", "branch_proj", "w_out", "s5_glu_w")
SHARDED_SMALL = ("norm_w", "rg_conv_w")
SMALL = SMALL_RAW + SHARDED_SMALL


def _view2d(shape):
    return (1, shape[0]) if len(shape) == 1 else (math.prod(shape[:-1]), shape[-1])


def _small_layout(shapes, row_multiple):
    layout, at = [], 0
    for shape in shapes:
        r, c = _view2d(shape)
        rp = -(-r // 8) * 8
        layout.append((at, r, c, rp))
        at += rp * max(1, c // LANE)
    return layout, -(-at // row_multiple) * row_multiple


def pack_small(name, arrays, row_multiple):
    layout, rows = _small_layout([a.shape for a in arrays], row_multiple)

    def body(*refs):
        out = refs[-1]
        out[...] = jnp.zeros_like(out)
        for ref, (r0, r, c, rp) in zip(refs[:-1], layout):
            if c <= LANE:
                out[r0:r0 + r, 0:c] = ref[...]
            else:
                for q in range(c // LANE):
                    out[r0 + q * rp:r0 + q * rp + r, :] = ref[:, q * LANE:(q + 1) * LANE]

    return pl.pallas_call(
        body, name=name, out_shape=jax.ShapeDtypeStruct((rows, LANE), F32),
        compiler_params=pltpu.CompilerParams(vmem_limit_bytes=VMEM_LIMIT),
    )(*[a.reshape(_view2d(a.shape)) for a in arrays])


def unpack_small(name, packed, shapes):
    layout, _ = _small_layout(shapes, 8)

    def body(p_ref, *outs):
        for ref, (r0, r, c, rp) in zip(outs, layout):
            if c <= LANE:
                ref[...] = p_ref[r0:r0 + r, 0:c]
            else:
                for q in range(c // LANE):
                    ref[:, q * LANE:(q + 1) * LANE] = p_ref[r0 + q * rp:r0 + q * rp + r, :]

    res = pl.pallas_call(
        body, name=name, out_shape=[jax.ShapeDtypeStruct(_view2d(s), F32) for s in shapes],
        compiler_params=pltpu.CompilerParams(vmem_limit_bytes=VMEM_LIMIT),
    )(packed)
    return [a.reshape(s) for a, s in zip(res, shapes)]


HBM = pl.BlockSpec(memory_space=pltpu.HBM)
SEM = pl.BlockSpec(memory_space=pltpu.SEMAPHORE)
EFFECT = pltpu.SideEffectType.DATAFLOW_SIDE_EFFECTING


def split_start(name, srcs, land_shapes, plan, n_send, n_recv):
    ns, nl = len(srcs), len(land_shapes)

    def body(*refs):
        ins, lands = refs[:ns], refs[ns:ns + nl]
        send, recv = refs[ns + nl], refs[ns + nl + 1]
        for src, dst, ks, kr, dev in plan(ins, lands):
            pltpu.make_async_remote_copy(src_ref=src, dst_ref=dst, send_sem=send.at[ks], recv_sem=recv.at[kr],
                                         device_id=dev, device_id_type=MESH).start()
        refs[-1][...] = jnp.zeros_like(refs[-1])

    hbm = lambda a: pltpu.with_memory_space_constraint(a, pltpu.HBM)
    lands = [lax.empty(s.shape, s.dtype) for s in land_shapes]
    out = pl.pallas_call(
        body, name=name,
        out_shape=(pltpu.SemaphoreType.DMA((n_send,)), pltpu.SemaphoreType.DMA((n_recv,)),
                   *[pltpu.HBM(a.shape, a.dtype) for a in srcs], *[pltpu.HBM(s.shape, s.dtype) for s in land_shapes],
                   jax.ShapeDtypeStruct((8, LANE), F32)),
        in_specs=[HBM] * (ns + nl), out_specs=(SEM, SEM, *[HBM] * (ns + nl), pl.BlockSpec(memory_space=pltpu.VMEM)),
        input_output_aliases={k: 2 + k for k in range(ns + nl)},
        compiler_params=pltpu.CompilerParams(has_side_effects=EFFECT),
    )(*[hbm(a) for a in srcs], *[hbm(a) for a in lands])
    return out[:-1], out[-1]


def split_wait(name, handles, n_src, waits, after):
    send, recv, *bufs = handles
    nb = len(bufs)

    def body(*refs):
        ins, lands = refs[:n_src], refs[n_src:nb]
        send_sem, recv_sem = refs[nb], refs[nb + 1]
        x, y, c, _chips = _place()
        sends, recvs = waits(ins, lands)
        for src, k in sends:
            pltpu.make_async_remote_copy(src_ref=src, dst_ref=src, send_sem=send_sem.at[k], recv_sem=recv_sem.at[0],
                                         device_id=(x, y, 1 - c), device_id_type=MESH).wait_send()
        for dst, k in recvs:
            pltpu.make_async_remote_copy(src_ref=dst, dst_ref=dst, send_sem=send_sem.at[0], recv_sem=recv_sem.at[k],
                                         device_id=(x, y, 1 - c), device_id_type=MESH).wait_recv()

    out = pl.pallas_call(
        body, name=name, out_shape=tuple(pltpu.HBM(a.shape, a.dtype) for a in bufs),
        in_specs=[HBM] * nb + [SEM, SEM, ANY], out_specs=tuple([HBM] * nb),
        input_output_aliases={k: k for k in range(nb)},
        compiler_params=pltpu.CompilerParams(has_side_effects=EFFECT),
    )(*bufs, send, recv, after)
    return list(out[:n_src]), list(out[n_src:])


def gather_plan(n):
    def plan(ins, lands):
        x, y, c, chips = _place()
        me = 2 * x + y
        copies = []
        for w in range(n):
            for j, (cx, cy) in enumerate(chips):
                for t in range(2):
                    copies.append((ins[w].at[c], lands[w].at[c, me], 8 * w + 2 * j + t, 8 * w + 2 * j + c, (cx, cy, t)))
            for half in range(2):
                copies.append((ins[w].at[half], lands[w].at[half, me], 8 * w + 6 + half, 8 * w + 6 + half, (x, y, 1 - c)))
        return copies

    def waits(ins, lands):
        x, y, c, chips = _place()
        me = 2 * x + y
        sends, recvs = [], []
        for w in range(n):
            for j, (cx, cy) in enumerate(chips):
                for t in range(2):
                    sends.append((ins[w].at[c], 8 * w + 2 * j + t))
                    recvs.append((lands[w].at[t, 2 * cx + cy], 8 * w + 2 * j + t))
            for half in range(2):
                sends.append((ins[w].at[half], 8 * w + 6 + half))
                recvs.append((lands[w].at[half, me], 8 * w + 6 + half))
        return sends, recvs

    return plan, waits


def scatter_plan(n):
    def plan(ins, lands):
        x, y, c, chips = _place()
        return [(ins[w].at[:, 2 * cx + cy], lands[w].at[j], 3 * w + j, 3 * w + j, (cx, cy, c))
                for w in range(n) for j, (cx, cy) in enumerate(chips)]

    def waits(ins, lands):
        x, y, c, chips = _place()
        sends = [(ins[w].at[:, 2 * cx + cy], 3 * w + j) for w in range(n) for j, (cx, cy) in enumerate(chips)]
        recvs = [(lands[w].at[j], 3 * w + j) for w in range(n) for j in range(3)]
        return sends, recvs

    return plan, waits


def _layer_shards(w, l):
    return [w["ffn_gate"][l].astype(MMT), w["ffn_up"][l].astype(MMT), w["ffn_down"][l].astype(MMT),
            w["w_in"][l].reshape(2, D_MODEL // 2, -1).astype(MMT),
            w["branch_proj"][l].reshape(2, 3 * BW // 2, -1).astype(MMT),
            w["w_out"][l].reshape(2, -1, D_MODEL).astype(MMT),
            w["s5_glu_w"][l].reshape(2, -1, BW).astype(MMT)]


def _layer_weights(g):
    rows = lambda a: a.transpose(1, 0, 2, 3).reshape(NSH, -1, a.shape[-1])
    p = rows(g[4]).reshape(NSH, 3, BW, -1).transpose(1, 2, 0, 3).reshape(3, BW, D_MODEL)
    return dict(wg=g[0], wu=g[1], wd=g[2], win=rows(g[3]), pfull=p,
                woutfull=rows(g[5]).reshape(D_MODEL, D_MODEL), gluw=rows(g[6]).reshape(BW, BW))


GROUPS = {"ffn1": ("ffn_gate", "ffn_up", "ffn_down"), "merge": ("branch_proj", "w_out"), "mid": ("s5_glu_w",),
          "pre": ("w_in",), "ffn0": ("ffn_gate", "ffn_up", "ffn_down")}


def _grad_views(big, l, group):
    four = lambda a: a.reshape(a.shape[0], NSH, -1, a.shape[-1])
    views = []
    for name in GROUPS[group]:
        if name == "branch_proj":
            dq = D_MODEL // NSH
            bp = big[name][l].reshape(3, BW, NSH, dq).transpose(2, 0, 1, 3).reshape(1, NSH, 3 * BW, dq)
            views.append((name, bp, 0))
        elif name.startswith("ffn"):
            views.append((name, four(big[name]), 2 * l + (1 if group == "ffn1" else 0)))
        else:
            views.append((name, four(big[name]), l))
    return views


def _reduce_to_halves(tag, views, c, wire):
    from_sibling = exchange_halves(f"reduce_cores_{tag}", [a for _, a, _ in views], [(p0, 1) for _, _, p0 in views])
    merge = lambda a: a.reshape((-1,) + a.shape[2:])
    return [add_own_half(f"sum_cores_{tag}_{i}", merge(a), merge(r), c, wire[i], NSH * p0).reshape(r.shape)
            for i, ((_, a, p0), r) in enumerate(zip(views, from_sibling))]


def _step(x, target, w, m, v):
    mx, my, mc = lax.axis_index("x"), lax.axis_index("y"), lax.axis_index("c")
    me = (2 * mx + my).astype(jnp.int32)
    mc = mc.astype(jnp.int32)

    W = dict(L=[None] * DEPTH)
    state = {"pending": []}
    n_big = len(BIG)
    g_plan, g_waits = gather_plan(n_big)

    def layer_weights(l, h):
        if l == 0:
            got = gather_shards("gather_weights_0", _layer_shards(w, 0) + [w[n] for n in SHARDED_SMALL])
            nxt = _layer_shards(w, 1)
            got, nxt = lax.optimization_barrier((got, nxt))
            shapes = [jax.ShapeDtypeStruct((2, NSH) + a.shape[1:], a.dtype) for a in nxt]
            state["gather"], token = split_start("gather_weights_1_start", nxt, shapes, g_plan, 8 * n_big, 8 * n_big)
            W["nw"] = got[n_big].transpose(0, 2, 1, 3).reshape(DEPTH, 3, 1, D_MODEL) + token[0, 0]
            W["convw"] = got[n_big + 1].transpose(0, 2, 1, 3).reshape(DEPTH, CONV_W, BW)
            return _layer_weights(got[:n_big]), h
        return _layer_weights(split_wait("gather_weights_1_wait", state["gather"], n_big, g_waits, h)[1]), h

    def layer_grads(l, group, big):
        if (l, group) == (0, "ffn0"):
            return
        views = _grad_views(big, l, group)
        tag = f"{l}_{group}"
        halves = _reduce_to_halves(tag, views, mc, [jnp.bfloat16] * len(views))
        shapes = [jax.ShapeDtypeStruct((3, a.shape[0]) + a.shape[2:], a.dtype) for a in halves]
        plan, waits = scatter_plan(len(halves))
        handles, token = split_start(f"reduce_chips_{tag}_start", halves, shapes, plan, 3 * len(halves), 3 * len(halves))
        W["nw"] = W["nw"] + token[0, 0]
        state["pending"].append((tag, [name for name, _, _ in views], l, group, handles, waits))

    loss, dx, big, small = local_step(x[0], target[0], W, {k: w[k] for k in SMALL_RAW}, layer_weights, layer_grads)

    pieces = {n: {} for n in BIG}
    block_of = lambda name, l, group: (2 * l + (group == "ffn1")) if name.startswith("ffn") else l
    views = _grad_views(big, 0, "ffn0")
    small_packed = pack_small("pack_small_grads", [small[n] for n in SMALL], NSH * 32)
    halves = _reduce_to_halves("0_ffn0", views + [("small", small_packed.reshape(1, NSH, -1, LANE), 0)], mc,
                               [jnp.bfloat16] * len(views) + [F32])
    from_chips = scatter_to_chips("reduce_chips_0_ffn0", halves)
    last = [add_chips(f"sum_chips_0_ffn0_{i}", h, r, me) for i, (h, r) in enumerate(zip(halves, from_chips))]
    for (name, _, _), piece in zip(views, last):
        pieces[name][block_of(name, 0, "ffn0")] = piece
    after = dx
    for tag, names, l, group, handles, waits in state["pending"]:
        sent, landed = split_wait(f"reduce_chips_{tag}_wait", handles, len(names), waits, after)
        for i, (name, h, r) in enumerate(zip(names, sent, landed)):
            pieces[name][block_of(name, l, group)] = add_chips(f"sum_chips_{tag}_{i}", h, r, me)
    own = [jnp.concatenate([pieces[n][b] for b in sorted(pieces[n])], axis=0) for n in BIG] + [last[-1]]
    other = share_halves("reduce_share", own)

    g, delta, new_m, new_v = {}, {}, {}, {}
    for i, n in enumerate(BIG):
        view = lambda a: a.reshape(own[i].shape[0], -1, own[i].shape[2])
        res = adamw_halves(f"adamw_{n}", view(w[n]), view(m[n]), view(v[n]), own[i], other[i], mc)
        g[n], delta[n], new_m[n], new_v[n] = [a.reshape(w[n].shape) for a in res]

    piece = jnp.stack([jnp.where(mc == 0, own[-1][0], other[-1][0]), jnp.where(mc == 0, other[-1][0], own[-1][0])])
    (all_small,) = gather_shards("gather_small", [piece])
    full_small = unpack_small("unpack_small_grads", all_small.transpose(1, 0, 2, 3).reshape(-1, LANE),
                              [small[n].shape for n in SMALL])
    g.update(zip(SMALL, full_small))
    g["norm_w"] = lax.dynamic_slice_in_dim(g["norm_w"], me * (D_MODEL // NSH), D_MODEL // NSH, axis=2)
    g["rg_conv_w"] = lax.dynamic_slice_in_dim(g["rg_conv_w"], me * (BW // NSH), BW // NSH, axis=2)

    packed = [pack_small(f"pack_small_{tag}", [src[n] for n in SMALL], 8)
              for tag, src in (("w", w), ("g", g), ("m", m), ("v", v))]
    for tag, dst, flat in zip(("delta", "m", "v"), (delta, new_m, new_v), adamw(*packed)):
        dst.update(zip(SMALL, unpack_small(f"unpack_small_{tag}", flat, [w[n].shape for n in SMALL])))

    total = lax.psum(loss[0, 0], ("x", "y", "c"))
    return (total, dx[None], *[g[n] for n in WEIGHTS], *[delta[n] for n in WEIGHTS],
            *[new_m[n] for n in WEIGHTS], *[new_v[n] for n in WEIGHTS])


def kernel(x, norm_w, final_norm_w, ffn_gate, ffn_up, ffn_down, w_in, branch_proj, w_out, s5_lambda_re, s5_lambda_im, s5_log_dt, s5_b_re, s5_b_im, s5_c_re, s5_c_im, s5_d, s5_glu_w, s5_glu_b, hg_lb_logits, hg_norm_w, rg_conv_w, rg_conv_b, rg_wa, rg_ba, rg_wx, rg_bx, rg_lambda, loss_target, m_norm_w, m_final_norm_w, m_ffn_gate, m_ffn_up, m_ffn_down, m_w_in, m_branch_proj, m_w_out, m_s5_lambda_re, m_s5_lambda_im, m_s5_log_dt, m_s5_b_re, m_s5_b_im, m_s5_c_re, m_s5_c_im, m_s5_d, m_s5_glu_w, m_s5_glu_b, m_hg_lb_logits, m_hg_norm_w, m_rg_conv_w, m_rg_conv_b, m_rg_wa, m_rg_ba, m_rg_wx, m_rg_bx, m_rg_lambda, v_norm_w, v_final_norm_w, v_ffn_gate, v_ffn_up, v_ffn_down, v_w_in, v_branch_proj, v_w_out, v_s5_lambda_re, v_s5_lambda_im, v_s5_log_dt, v_s5_b_re, v_s5_b_im, v_s5_c_re, v_s5_c_im, v_s5_d, v_s5_glu_w, v_s5_glu_b, v_hg_lb_logits, v_hg_norm_w, v_rg_conv_w, v_rg_conv_b, v_rg_wa, v_rg_ba, v_rg_wx, v_rg_bx, v_rg_lambda):
    ws = (norm_w, final_norm_w, ffn_gate, ffn_up, ffn_down, w_in, branch_proj, w_out, s5_lambda_re, s5_lambda_im, s5_log_dt, s5_b_re, s5_b_im, s5_c_re, s5_c_im, s5_d, s5_glu_w, s5_glu_b, hg_lb_logits, hg_norm_w, rg_conv_w, rg_conv_b, rg_wa, rg_ba, rg_wx, rg_bx, rg_lambda)
    ms = (m_norm_w, m_final_norm_w, m_ffn_gate, m_ffn_up, m_ffn_down, m_w_in, m_branch_proj, m_w_out, m_s5_lambda_re, m_s5_lambda_im, m_s5_log_dt, m_s5_b_re, m_s5_b_im, m_s5_c_re, m_s5_c_im, m_s5_d, m_s5_glu_w, m_s5_glu_b, m_hg_lb_logits, m_hg_norm_w, m_rg_conv_w, m_rg_conv_b, m_rg_wa, m_rg_ba, m_rg_wx, m_rg_bx, m_rg_lambda)
    vs = (v_norm_w, v_final_norm_w, v_ffn_gate, v_ffn_up, v_ffn_down, v_w_in, v_branch_proj, v_w_out, v_s5_lambda_re, v_s5_lambda_im, v_s5_log_dt, v_s5_b_re, v_s5_b_im, v_s5_c_re, v_s5_c_im, v_s5_d, v_s5_glu_w, v_s5_glu_b, v_hg_lb_logits, v_hg_norm_w, v_rg_conv_w, v_rg_conv_b, v_rg_wa, v_rg_ba, v_rg_wx, v_rg_bx, v_rg_lambda)
    return _step(x, loss_target, dict(zip(WEIGHTS, ws)), dict(zip(WEIGHTS, ms)), dict(zip(WEIGHTS, vs)))
```

```python
import functools
import math
from typing import NamedTuple

import jax
import jax.numpy as jnp
from jax import lax
from jax.experimental import pallas as pl
from jax.experimental.pallas import tpu as pltpu

F32 = jnp.float32
MMT = jnp.bfloat16
HI = lax.Precision.HIGHEST

D_MODEL = 1024
BW = 512
S5_GROUP, S5_GROUPS, S5_STATE = 16, 32, 64
S5_N = S5_GROUPS * S5_STATE
HG_HEADS, HG_D = 4, 128
HG_CHUNK = 128
RG_BLOCKS, RG_BLOCK = 8, 64
RG_C = 8.0
CONV_W = 4
D_FF = 2816
EPS = 1e-6
IN_TOTAL = 6656
NSH = 4
NSEG = 8
LANE = 128
VMEM_LIMIT = 56 * 1024 * 1024
TM_WGRAD = 512

ADAM_LR, ADAM_B1, ADAM_B2, ADAM_EPS, ADAM_WD, ADAM_STEP = 0.001, 0.9, 0.999, 1e-08, 0.01, 10

MESH = pl.DeviceIdType.MESH


class WP(NamedTuple):
    w: jax.Array
    p: jax.Array


def _dg(a, b, ca, cb):
    return lax.dot_general(a, b, (((ca,), (cb,)), ((), ())), preferred_element_type=F32)


@jax.custom_vjp
def _mmw(a, w, p):
    return _dg(a.astype(MMT), w, 1, 0)


def _mmw_fwd(a, w, p):
    return _mmw(a, w, p), (a, w)


def _mmw_bwd(res, g):
    a, w = res
    gb = g.astype(MMT)
    return _dg(gb, w, 1, 1), jnp.zeros_like(w), _dg(a.astype(MMT), gb, 0, 0)


_mmw.defvjp(_mmw_fwd, _mmw_bwd)


def mm(a, w):
    if isinstance(w, WP):
        return _mmw(a, w.w, w.p)
    return _dg(a.astype(MMT), w, 1, 0)


@jax.custom_vjp
def mma_nn(a, b):
    return _dg(a.astype(MMT), b.astype(MMT), 1, 0)


def _nn_f(a, b):
    return mma_nn(a, b), (a, b)


def _nn_b(res, g):
    a, b = res
    gb = g.astype(MMT)
    return _dg(gb, b.astype(MMT), 1, 1), _dg(a.astype(MMT), gb, 0, 0)


mma_nn.defvjp(_nn_f, _nn_b)


@jax.custom_vjp
def mma_nt(a, b):
    return _dg(a.astype(MMT), b.astype(MMT), 1, 1)


def _nt_f(a, b):
    return mma_nt(a, b), (a, b)


def _nt_b(res, g):
    a, b = res
    gb = g.astype(MMT)
    return _dg(gb, b.astype(MMT), 1, 0), _dg(gb, a.astype(MMT), 0, 0)


mma_nt.defvjp(_nt_f, _nt_b)


@jax.custom_vjp
def mma_tn(a, b):
    return _dg(a.astype(MMT), b.astype(MMT), 0, 0)


def _tn_f(a, b):
    return mma_tn(a, b), (a, b)


def _tn_b(res, g):
    a, b = res
    gb = g.astype(MMT)
    return _dg(b.astype(MMT), gb, 1, 1), _dg(a.astype(MMT), gb, 1, 0)


mma_tn.defvjp(_tn_f, _tn_b)


def mm_exact(m, x):
    return jnp.dot(m, x, precision=HI, preferred_element_type=F32)


def _rms(x, w):
    return x * lax.rsqrt(jnp.mean(x * x, axis=-1, keepdims=True) + EPS) * w


def _expm1(x):
    series = x * (1.0 + x * (1.0 / 2) * (1.0 + x * (1.0 / 3) * (1.0 + x * (1.0 / 4) * (1.0 + x * (1.0 / 5) * (1.0 + x * (1.0 / 6))))))
    return jnp.where(jnp.abs(x) < 0.1, series, jnp.exp(x) - 1.0)


def _bspec(block, fn, order):
    if order == "is":
        return pl.BlockSpec(block, lambda i, s: fn(s, i))
    return pl.BlockSpec(block, lambda s, i: fn(s, i))


def tile_fwd(fn, name, n_i, n_s, ins, outs, s_outer=False):
    n_in = len(ins)
    order = "si" if s_outer else "is"
    assert not (s_outer and any(o[4] for o in outs))

    def body(*refs):
        s = pl.program_id(0 if s_outer else 1)
        res = fn(*[r[...] for r in refs[:n_in]], s)
        for o_ref, val, spec in zip(refs[n_in:], res, outs):
            if spec[4] and n_s > 1:
                @pl.when(s == 0)
                def _(o_ref=o_ref, val=val):
                    o_ref[...] = val.astype(o_ref.dtype)

                @pl.when(s != 0)
                def _(o_ref=o_ref, val=val):
                    o_ref[...] += val.astype(o_ref.dtype)
            else:
                o_ref[...] = val.astype(o_ref.dtype)

    return pl.pallas_call(
        body, grid=(n_s, n_i) if s_outer else (n_i, n_s), name=name,
        in_specs=[_bspec(b, f, order) for _, b, f in ins],
        out_specs=[_bspec(b, f, order) for _, _, b, f, _ in outs],
        out_shape=[jax.ShapeDtypeStruct(sh, dt) for sh, dt, _, _, _ in outs],
        compiler_params=pltpu.CompilerParams(vmem_limit_bytes=VMEM_LIMIT,
                                             dimension_semantics=("arbitrary", "arbitrary")),
    )(*[a for a, _, _ in ins])


def tile_bwd(fn, name, n_i, n_s, ins, cts, gouts):
    groups = [c if isinstance(c, list) else [c] for c in cts]
    cts = [blk for grp in groups for blk in grp]
    n_in, n_ct = len(ins), len(cts)
    kinds = [k for _, _, _, k in ins]
    d_pos = [j for j, k in enumerate(kinds) if k != "c"]
    shared = [(gi, spec[4]) for gi, spec in enumerate(gouts) if len(spec) == 5 and spec[4] is not None]
    n_sh = len(shared)

    def body(*refs):
        s, i = pl.program_id(0), pl.program_id(1)
        vals = [r[...] for r in refs[:n_in]]
        ct_refs, ctv = list(refs[n_in:n_in + n_ct]), []
        for grp in groups:
            parts = [ct_refs.pop(0)[...] for _ in grp]
            ctv.append(parts[0] if len(parts) == 1 else jnp.concatenate(parts, axis=1))
        ctv = tuple(ctv)
        g_refs = refs[n_in + n_ct + n_sh:]

        def g(*dv):
            args = list(vals)
            for j, v in zip(d_pos, dv):
                args[j] = WP(vals[j], v) if kinds[j] == "w" else v
            return tuple(fn(*args))

        dv0 = [jnp.zeros(vals[j].shape, F32) if kinds[j] == "w" else vals[j] for j in d_pos]
        _, vjp = jax.vjp(g, *dv0)
        grads = vjp(ctv)
        for g_ref, gv, spec in zip(g_refs, grads, gouts):
            mode = spec[3]
            if mode == "write":
                g_ref[...] = gv.astype(g_ref.dtype)
            else:
                first = (i == 0) if mode == "acc_i" else jnp.logical_and(i == 0, s == 0)

                @pl.when(first)
                def _(g_ref=g_ref, gv=gv):
                    g_ref[...] = gv.astype(g_ref.dtype)

                @pl.when(jnp.logical_not(first))
                def _(g_ref=g_ref, gv=gv):
                    g_ref[...] += gv.astype(g_ref.dtype)

    return pl.pallas_call(
        body, grid=(n_s, n_i), name=name,
        in_specs=([_bspec(b, f, "si") for _, b, f, _ in ins] + [_bspec(b, f, "si") for _, b, f in cts]
                  + [pl.BlockSpec(memory_space=pl.ANY)] * n_sh),
        out_specs=[_bspec(spec[1], spec[2], "si") for spec in gouts],
        out_shape=[jax.ShapeDtypeStruct(spec[0], F32) for spec in gouts],
        input_output_aliases={n_in + n_ct + k: gi for k, (gi, _) in enumerate(shared)},
        compiler_params=pltpu.CompilerParams(vmem_limit_bytes=VMEM_LIMIT,
                                             dimension_semantics=("arbitrary", "arbitrary")),
    )(*[a for a, _, _, _ in ins], *[a for a, _, _ in cts], *[buf for _, buf in shared])


def _row_tile(rows, width, itemsize=4, budget=2 * 1024 * 1024, mult=8):
    best = mult
    for t in range(mult, rows + 1, mult):
        if rows % t == 0 and t * width * itemsize <= budget:
            best = t
    return best


def add_n(name, terms, shape):
    rows, cols = shape
    tr = _row_tile(rows, cols)

    def body(*refs):
        acc = refs[0][...]
        for r in refs[1:-1]:
            acc = acc + r[...]
        refs[-1][...] = acc

    specs = []
    for _, lead in terms:
        specs.append(pl.BlockSpec((None,) * len(lead) + (tr, cols), functools.partial(lambda i, lead: (*lead, i, 0), lead=lead)))
    return pl.pallas_call(
        body, grid=(rows // tr,), name=name, in_specs=specs,
        out_specs=pl.BlockSpec((tr, cols), lambda i: (i, 0)),
        out_shape=jax.ShapeDtypeStruct((rows, cols), F32),
    )(*[a for a, _ in terms])


def ffn_core(x, nw, wg, wu, wd):
    h = _rms(x, nw)
    return (0.5 * mm(jax.nn.silu(mm(h, wg)) * mm(h, wu), wd),)


def pre_core(x, nw, win):
    return (mm(_rms(x, nw), win),)


def _split_lanes(y):
    return jnp.stack([y[:, k * LANE:(k + 1) * LANE] for k in range(y.shape[1] // LANE)], axis=0)


def _join_lanes(y3):
    return jnp.concatenate([y3[k] for k in range(y3.shape[0])], axis=1)


def s5_pre_core(u, bmat):
    bu = mm(u, bmat)
    return _split_lanes(bu[:, :S5_N]), _split_lanes(bu[:, S5_N:])


def mid_core(xr, xi, u, o, g, hs, gc, hmat, cmat, d, gluw, glub, hgw):
    xs = jnp.concatenate([_join_lanes(xr), _join_lanes(xi)], axis=1)
    y = mm(xs, cmat) + d * u
    z = jax.nn.gelu(y)
    ya = z * jax.nn.sigmoid(mm(z, gluw) + glub)
    ms = mm_exact(o * o, hmat)
    yb = o * lax.rsqrt(ms + EPS) * hgw * jax.nn.silu(g)
    yc = hs * jax.nn.gelu(gc)
    return ya, yb, yc


def _sub(w, n):
    return WP(w.w[n], w.p[n]) if isinstance(w, WP) else w[n]


def merge_core(ya, yb, yc, g0, g1, g2, g3, g4, g5, p, wout):
    gate = lambda a, b: jax.nn.sigmoid(jnp.concatenate([a, b], axis=1))
    m = gate(g0, g1) * mm(ya, _sub(p, 0)) + gate(g2, g3) * mm(yb, _sub(p, 1)) + gate(g4, g5) * mm(yc, _sub(p, 2))
    return (mm(m, wout),)


def gates_core(xc, wa, ba, wx, bx, lam):
    r = jax.nn.sigmoid(mm(xc, wa) + ba)
    i = jax.nn.sigmoid(mm(xc, wx) + bx)
    log_a = -RG_C * jax.nn.softplus(-lam) * r
    a = jnp.exp(log_a)
    b = jnp.sqrt(-_expm1(2.0 * log_a)) * (i * xc)
    return a, b


def _seg_rows(ref, k, j, n):
    rows = pl.ds(pl.multiple_of(j * NSEG, NSEG), NSEG)
    if k is None:
        return ref[rows, :]
    return ref[k, rows, :]


def _seg_store(ref, k, j, n, val):
    rows = pl.ds(pl.multiple_of(j * NSEG, NSEG), NSEG)
    if k is None:
        ref[rows, :] = val
    else:
        ref[k, rows, :] = val


def _seg_carries(er, ei, pr, pi, reverse):
    rows = lax.broadcasted_iota(jnp.int32, er.shape, 0)
    cr = jnp.zeros_like(er)
    ci = None if ei is None else jnp.zeros_like(er)
    order = range(NSEG - 2, -1, -1) if reverse else range(1, NSEG)
    shift = NSEG - 1 if reverse else 1
    for s in order:
        if ei is None:
            tr = er + pr * cr
            cr = jnp.where(rows == s, pltpu.roll(tr, shift, 0), cr)
        else:
            tr = er + pr * cr - pi * ci
            ti = ei + pr * ci + pi * cr
            cr = jnp.where(rows == s, pltpu.roll(tr, shift, 0), cr)
            ci = jnp.where(rows == s, pltpu.roll(ti, shift, 0), ci)
    return cr, ci


S5_K = 2


def s5_scan_fwd(bur, bui, ar, ai, L):
    n = L // NSEG
    nb = S5_N // LANE
    K = S5_K

    def body(br_ref, bi_ref, ar_ref, ai_ref, xr_ref, xi_ref):
        zero = jnp.zeros((NSEG, LANE), F32)
        A = [(jnp.broadcast_to(ar_ref[k], (NSEG, LANE)), jnp.broadcast_to(ai_ref[k], (NSEG, LANE))) for k in range(K)]

        def p1(j, st):
            new = []
            for k in range(K):
                sr, si, pr, pi = st[k]
                a_r, a_i = A[k]
                nr = a_r * sr - a_i * si + _seg_rows(br_ref, k, j, n)
                ni = a_r * si + a_i * sr + _seg_rows(bi_ref, k, j, n)
                _seg_store(xr_ref, k, j, n, nr)
                _seg_store(xi_ref, k, j, n, ni)
                new.append((nr, ni, a_r * pr - a_i * pi, a_r * pi + a_i * pr))
            return tuple(new)

        st = lax.fori_loop(0, n, p1, tuple((zero, zero, zero + 1.0, zero) for _ in range(K)))
        C = [_seg_carries(st[k][0], st[k][1], st[k][2], st[k][3], False) for k in range(K)]

        def p2(j, st):
            new = []
            for k in range(K):
                pr, pi = st[k]
                a_r, a_i = A[k]
                pr, pi = a_r * pr - a_i * pi, a_r * pi + a_i * pr
                cr, ci = C[k]
                _seg_store(xr_ref, k, j, n, _seg_rows(xr_ref, k, j, n) + pr * cr - pi * ci)
                _seg_store(xi_ref, k, j, n, _seg_rows(xi_ref, k, j, n) + pr * ci + pi * cr)
                new.append((pr, pi))
            return tuple(new)

        lax.fori_loop(0, n, p2, tuple((zero + 1.0, zero) for _ in range(K)))

    blk = pl.BlockSpec((K, L, LANE), lambda g: (g, 0, 0))
    ablk = pl.BlockSpec((K, 1, LANE), lambda g: (g, 0, 0))
    return pl.pallas_call(
        body, grid=(nb // K,), name="s5_scan_fwd",
        in_specs=[blk, blk, ablk, ablk], out_specs=[blk, blk],
        out_shape=[jax.ShapeDtypeStruct((nb, L, LANE), F32)] * 2,
        compiler_params=pltpu.CompilerParams(vmem_limit_bytes=VMEM_LIMIT),
    )(bur, bui, ar, ai)


def s5_scan_bwd(dxr, dxi, xr, xi, ar, ai, L):
    n = L // NSEG
    nb = S5_N // LANE
    K = S5_K

    def body(dr_ref, di_ref, xr_ref, xi_ref, ar_ref, ai_ref, gr_ref, gi_ref, dar_ref, dai_ref):
        zero = jnp.zeros((NSEG, LANE), F32)
        rows = lax.broadcasted_iota(jnp.int32, (NSEG, LANE), 0)
        A = [(jnp.broadcast_to(ar_ref[k], (NSEG, LANE)), -jnp.broadcast_to(ai_ref[k], (NSEG, LANE))) for k in range(K)]

        def p1(jj, st):
            j = n - 1 - jj
            new = []
            for k in range(K):
                sr, si, pr, pi = st[k]
                a_r, a_i = A[k]
                nr = a_r * sr - a_i * si + _seg_rows(dr_ref, k, j, n)
                ni = a_r * si + a_i * sr + _seg_rows(di_ref, k, j, n)
                _seg_store(gr_ref, k, j, n, nr)
                _seg_store(gi_ref, k, j, n, ni)
                new.append((nr, ni, a_r * pr - a_i * pi, a_r * pi + a_i * pr))
            return tuple(new)

        st = lax.fori_loop(0, n, p1, tuple((zero, zero, zero + 1.0, zero) for _ in range(K)))
        C = [_seg_carries(st[k][0], st[k][1], st[k][2], st[k][3], True) for k in range(K)]
        xb = [(jnp.where(rows == 0, 0.0, pltpu.roll(_seg_rows(xr_ref, k, n - 1, n), 1, 0)),
               jnp.where(rows == 0, 0.0, pltpu.roll(_seg_rows(xi_ref, k, n - 1, n), 1, 0))) for k in range(K)]

        def p2(jj, st):
            j = n - 1 - jj
            jp = jnp.maximum(j - 1, 0)
            new = []
            for k in range(K):
                pr, pi, acr, aci = st[k]
                a_r, a_i = A[k]
                pr, pi = a_r * pr - a_i * pi, a_r * pi + a_i * pr
                cr, ci = C[k]
                g_r = _seg_rows(gr_ref, k, j, n) + pr * cr - pi * ci
                g_i = _seg_rows(gi_ref, k, j, n) + pr * ci + pi * cr
                _seg_store(gr_ref, k, j, n, g_r)
                _seg_store(gi_ref, k, j, n, g_i)
                xpr = jnp.where(j == 0, xb[k][0], _seg_rows(xr_ref, k, jp, n))
                xpi = jnp.where(j == 0, xb[k][1], _seg_rows(xi_ref, k, jp, n))
                new.append((pr, pi, acr + g_r * xpr + g_i * xpi, aci + g_i * xpr - g_r * xpi))
            return tuple(new)

        st = lax.fori_loop(0, n, p2, tuple((zero + 1.0, zero, zero, zero) for _ in range(K)))
        for k in range(K):
            dar_ref[k] = jnp.sum(st[k][2], axis=0, keepdims=True)
            dai_ref[k] = jnp.sum(st[k][3], axis=0, keepdims=True)

    blk = pl.BlockSpec((K, L, LANE), lambda g: (g, 0, 0))
    ablk = pl.BlockSpec((K, 1, LANE), lambda g: (g, 0, 0))
    return pl.pallas_call(
        body, grid=(nb // K,), name="s5_scan_bwd",
        in_specs=[blk, blk, blk, blk, ablk, ablk], out_specs=[blk, blk, ablk, ablk],
        out_shape=[jax.ShapeDtypeStruct((nb, L, LANE), F32)] * 2 + [jax.ShapeDtypeStruct((nb, 1, LANE), F32)] * 2,
        compiler_params=pltpu.CompilerParams(vmem_limit_bytes=VMEM_LIMIT),
    )(dxr, dxi, xr, xi, ar, ai)


def rg_scan_fwd(a, b, L):
    n = L // NSEG

    def body(a_ref, b_ref, h_ref):
        zero = jnp.zeros((NSEG, LANE), F32)

        def p1(j, st):
            h, p = st
            aj = _seg_rows(a_ref, None, j, n)
            h = aj * h + _seg_rows(b_ref, None, j, n)
            _seg_store(h_ref, None, j, n, h)
            return h, aj * p

        e, pe = lax.fori_loop(0, n, p1, (zero, zero + 1.0))
        c, _ = _seg_carries(e, None, pe, None, False)

        def p2(j, p):
            p = _seg_rows(a_ref, None, j, n) * p
            _seg_store(h_ref, None, j, n, _seg_rows(h_ref, None, j, n) + p * c)
            return p

        lax.fori_loop(0, n, p2, zero + 1.0)

    blk = pl.BlockSpec((L, LANE), lambda g: (0, g))
    return pl.pallas_call(
        body, grid=(BW // LANE,), name="rg_scan_fwd", in_specs=[blk, blk], out_specs=blk,
        out_shape=jax.ShapeDtypeStruct((L, BW), F32),
        compiler_params=pltpu.CompilerParams(vmem_limit_bytes=VMEM_LIMIT),
    )(a, b)


def rg_scan_bwd(a, h, dh, L):
    n = L // NSEG

    def body(a_ref, h_ref, dh_ref, da_ref, db_ref):
        zero = jnp.zeros((NSEG, LANE), F32)
        rows = lax.broadcasted_iota(jnp.int32, (NSEG, LANE), 0)
        a_edge = jnp.where(rows == NSEG - 1, 0.0, pltpu.roll(_seg_rows(a_ref, None, 0, n), NSEG - 1, 0))
        h_edge = jnp.where(rows == 0, 0.0, pltpu.roll(_seg_rows(h_ref, None, n - 1, n), 1, 0))

        def mult(j):
            return jnp.where(j == n - 1, a_edge, _seg_rows(a_ref, None, jnp.minimum(j + 1, n - 1), n))

        def p1(jj, st):
            j = n - 1 - jj
            g, p = st
            m = mult(j)
            g = m * g + _seg_rows(dh_ref, None, j, n)
            _seg_store(db_ref, None, j, n, g)
            return g, m * p

        e, pe = lax.fori_loop(0, n, p1, (zero, zero + 1.0))
        c, _ = _seg_carries(e, None, pe, None, True)

        def p2(jj, p):
            j = n - 1 - jj
            p = mult(j) * p
            g = _seg_rows(db_ref, None, j, n) + p * c
            _seg_store(db_ref, None, j, n, g)
            hp = jnp.where(j == 0, h_edge, _seg_rows(h_ref, None, jnp.maximum(j - 1, 0), n))
            _seg_store(da_ref, None, j, n, g * hp)
            return p

        lax.fori_loop(0, n, p2, zero + 1.0)

    blk = pl.BlockSpec((L, LANE), lambda g: (0, g))
    return pl.pallas_call(
        body, grid=(BW // LANE,), name="rg_scan_bwd", in_specs=[blk, blk, blk], out_specs=[blk, blk],
        out_shape=[jax.ShapeDtypeStruct((L, BW), F32)] * 2,
        compiler_params=pltpu.CompilerParams(vmem_limit_bytes=VMEM_LIMIT),
    )(a, h, dh)


def _hg_consts(C):
    t = lax.broadcasted_iota(jnp.int32, (C, C), 0)
    s = lax.broadcasted_iota(jnp.int32, (C, C), 1)
    tril = (s <= t).astype(F32)
    diag = (s == t).astype(F32)
    levels = []
    k = 1
    while (1 << k) <= C:
        m = 1 << (k - 1)
        same = (t >> k) == (s >> k)
        t_right = ((t >> (k - 1)) & 1) == 1
        s_left = ((s >> (k - 1)) & 1) == 0
        mask = jnp.logical_and(same, jnp.logical_and(t_right, s_left)).astype(F32)
        bnd = ((t >> k) << k) + (m - 1)
        levels.append((mask, (s <= bnd).astype(F32)))
        k += 1
    return tril, diag, levels


def hg_chunk(st, q, z, v, lb):
    C = q.shape[0]
    tril, diag, levels = _hg_consts(C)
    sig = jax.nn.sigmoid(z)
    lf = jnp.log(lb + (1.0 - lb) * sig)
    k = (1.0 - lb) * jax.nn.sigmoid(-z)
    qh = jax.nn.silu(q)
    b = mm_exact(tril, lf)
    blast = jnp.sum(lf, axis=0, keepdims=True)
    qe = qh * jnp.exp(b)
    kd = k * jnp.exp(blast - b)
    scaled = []
    for _, sel in levels:
        ref = mm_exact(sel, lf)
        scaled.append((qh * jnp.exp(jnp.minimum(b - ref, 0.0)), k * jnp.exp(jnp.minimum(ref - b, 0.0))))
    outs, news = [], []
    for h in range(HG_HEADS):
        sl = slice(h * HG_D, (h + 1) * HG_D)
        st_h = st[h * HG_D:(h + 1) * HG_D, :]
        sc = diag * mma_nt(qh[:, sl], k[:, sl])
        for (mask, _), (qt, kt) in zip(levels, scaled):
            sc = sc + mask * mma_nt(qt[:, sl], kt[:, sl])
        outs.append(mma_nt(qe[:, sl], st_h) + mma_nn(sc, v[:, sl]))
        news.append(st_h * jnp.exp(blast[:, sl]) + mma_tn(v[:, sl], kd[:, sl]))
    return jnp.concatenate(news, axis=0), jnp.concatenate(outs, axis=1)


def hg_fwd(qzv, lb, L):
    C = HG_CHUNK
    nc = L // C

    def body(q_ref, z_ref, v_ref, lb_ref, o_ref, sst_ref, st_ref):
        @pl.when(pl.program_id(0) == 0)
        def _():
            st_ref[...] = jnp.zeros_like(st_ref)

        st = st_ref[...]
        sst_ref[...] = st
        new, o = hg_chunk(st, q_ref[...], z_ref[...], v_ref[...], lb_ref[...])
        st_ref[...] = new
        o_ref[...] = o

    col = lambda cb: pl.BlockSpec((C, BW), functools.partial(lambda c, cb: (c, cb), cb=cb))
    return pl.pallas_call(
        body, grid=(nc,), name="hg_fwd",
        in_specs=[col(0), col(1), col(2), pl.BlockSpec((1, BW), lambda c: (0, 0))],
        out_specs=[pl.BlockSpec((C, BW), lambda c: (c, 0)), pl.BlockSpec((None, BW, HG_D), lambda c: (c, 0, 0))],
        out_shape=[jax.ShapeDtypeStruct((L, BW), F32), jax.ShapeDtypeStruct((nc, BW, HG_D), F32)],
        scratch_shapes=[pltpu.VMEM((BW, HG_D), F32)],
        compiler_params=pltpu.CompilerParams(vmem_limit_bytes=VMEM_LIMIT, dimension_semantics=("arbitrary",)),
    )(qzv, qzv, qzv, lb)


def hg_bwd(qzv, lb, sst, do, L):
    C = HG_CHUNK
    nc = L // C

    def body(q_ref, z_ref, v_ref, lb_ref, sst_ref, do_ref, dq_ref, dz_ref, dv_ref, dlb_ref, dst_ref):
        @pl.when(pl.program_id(0) == 0)
        def _():
            dst_ref[...] = jnp.zeros_like(dst_ref)
            dlb_ref[...] = jnp.zeros_like(dlb_ref)

        _, vjp = jax.vjp(hg_chunk, sst_ref[...], q_ref[...], z_ref[...], v_ref[...], lb_ref[...])
        dst, dq, dz, dv, dlb = vjp((dst_ref[...], do_ref[...]))
        dst_ref[...] = dst
        dq_ref[...] = dq
        dz_ref[...] = dz
        dv_ref[...] = dv
        dlb_ref[...] += dlb

    col = lambda cb: pl.BlockSpec((C, BW), functools.partial(lambda c, cb: (nc - 1 - c, cb), cb=cb))
    rev = pl.BlockSpec((C, BW), lambda c: (nc - 1 - c, 0))
    return pl.pallas_call(
        body, grid=(nc,), name="hg_bwd",
        in_specs=[col(0), col(1), col(2), pl.BlockSpec((1, BW), lambda c: (0, 0)),
                  pl.BlockSpec((None, BW, HG_D), lambda c: (nc - 1 - c, 0, 0)), rev],
        out_specs=[rev, rev, rev, pl.BlockSpec((1, BW), lambda c: (0, 0))],
        out_shape=[jax.ShapeDtypeStruct((L, BW), F32)] * 3 + [jax.ShapeDtypeStruct((1, BW), F32)],
        scratch_shapes=[pltpu.VMEM((BW, HG_D), F32)],
        compiler_params=pltpu.CompilerParams(vmem_limit_bytes=VMEM_LIMIT, dimension_semantics=("arbitrary",)),
    )(qzv, qzv, qzv, lb, sst, do)


def _shift_down(x, d, rows, L):
    if d == 0:
        return x
    wrapped = jnp.where((rows & (NSEG - 1)) == 0, 0.0, pltpu.roll(x, NSEG * d + 1, 0))
    return jnp.where(rows < NSEG * d, wrapped, pltpu.roll(x, NSEG * d, 0))


def _shift_up(x, d, rows, L):
    if d == 0:
        return x
    wrapped = jnp.where((rows & (NSEG - 1)) == NSEG - 1, 0.0, pltpu.roll(x, L - (NSEG * d + 1), 0))
    return jnp.where(rows >= L - NSEG * d, wrapped, pltpu.roll(x, L - NSEG * d, 0))


def conv_fwd(proj, w, b, L):
    def body(x_ref, w_ref, b_ref, o_ref):
        x = x_ref[...]
        rows = lax.broadcasted_iota(jnp.int32, x.shape, 0)
        acc = jnp.broadcast_to(b_ref[...], x.shape)
        for k in range(CONV_W):
            acc = acc + w_ref[pl.ds(k, 1), :] * _shift_down(x, CONV_W - 1 - k, rows, L)
        o_ref[...] = acc

    nl = BW // LANE
    return pl.pallas_call(
        body, grid=(nl,), name="conv_fwd",
        in_specs=[pl.BlockSpec((L, LANE), lambda g: (0, 5 * nl + g)), pl.BlockSpec((CONV_W, LANE), lambda g: (0, g)),
                  pl.BlockSpec((1, LANE), lambda g: (0, g))],
        out_specs=pl.BlockSpec((L, LANE), lambda g: (0, g)),
        out_shape=jax.ShapeDtypeStruct((L, BW), F32),
        compiler_params=pltpu.CompilerParams(vmem_limit_bytes=VMEM_LIMIT),
    )(proj, w, b)


def conv_bwd(proj, w, dxc, L):
    def body(x_ref, w_ref, d_ref, dx_ref, dw_ref, db_ref):
        x, d = x_ref[...], d_ref[...]
        rows = lax.broadcasted_iota(jnp.int32, x.shape, 0)
        acc = jnp.zeros_like(x)
        for k in range(CONV_W):
            acc = acc + w_ref[pl.ds(k, 1), :] * _shift_up(d, CONV_W - 1 - k, rows, L)
            dw_ref[pl.ds(k, 1), :] = jnp.sum(d * _shift_down(x, CONV_W - 1 - k, rows, L), axis=0, keepdims=True)
        dx_ref[...] = acc
        db_ref[...] = jnp.sum(d, axis=0, keepdims=True)

    nl = BW // LANE
    blk = pl.BlockSpec((L, LANE), lambda g: (0, g))
    return pl.pallas_call(
        body, grid=(nl,), name="conv_bwd",
        in_specs=[pl.BlockSpec((L, LANE), lambda g: (0, 5 * nl + g)), pl.BlockSpec((CONV_W, LANE), lambda g: (0, g)), blk],
        out_specs=[blk, pl.BlockSpec((CONV_W, LANE), lambda g: (0, g)), pl.BlockSpec((1, LANE), lambda g: (0, g))],
        out_shape=[jax.ShapeDtypeStruct((L, BW), F32), jax.ShapeDtypeStruct((CONV_W, BW), F32),
                   jax.ShapeDtypeStruct((1, BW), F32)],
        compiler_params=pltpu.CompilerParams(vmem_limit_bytes=VMEM_LIMIT),
    )(proj, w, dxc)


def loss_fwd_bwd(x, fw, target, L, tm):
    def fn(x, fw, t):
        err = jnp.square(_rms(x, fw) - t)
        return jnp.sum(0.5 * jnp.mean(err, axis=-1, keepdims=True), axis=0, keepdims=True)

    def body(x_ref, fw_ref, t_ref, l_ref, dx_ref, dfw_ref):
        i = pl.program_id(0)
        t = t_ref[...]
        val, vjp = jax.vjp(lambda x, fw: fn(x, fw, t), x_ref[...], fw_ref[...])
        dx, dfw = vjp(jnp.ones((1, 1), F32))
        dx_ref[...] = dx

        @pl.when(i == 0)
        def _():
            l_ref[...] = jnp.zeros_like(l_ref)
            dfw_ref[...] = jnp.zeros_like(dfw_ref)

        l_ref[...] += jnp.broadcast_to(val, l_ref.shape)
        dfw_ref[...] += dfw

    row = pl.BlockSpec((tm, D_MODEL), lambda i: (i, 0))
    vec = pl.BlockSpec((1, D_MODEL), lambda i: (0, 0))
    return pl.pallas_call(
        body, grid=(L // tm,), name="loss_fwd_bwd", in_specs=[row, vec, row],
        out_specs=[pl.BlockSpec((1, LANE), lambda i: (0, 0)), row, vec],
        out_shape=[jax.ShapeDtypeStruct((1, LANE), F32), jax.ShapeDtypeStruct((L, D_MODEL), F32),
                   jax.ShapeDtypeStruct((1, D_MODEL), F32)],
        compiler_params=pltpu.CompilerParams(vmem_limit_bytes=VMEM_LIMIT, dimension_semantics=("arbitrary",)),
    )(x, fw, target)


def adamw(w, g, m, v):
    rows, cols = w.shape
    tr = _row_tile(rows, cols, budget=1024 * 1024)
    c1 = 1.0 - ADAM_B1 ** ADAM_STEP
    c2 = 1.0 - ADAM_B2 ** ADAM_STEP

    def body(w_ref, g_ref, m_ref, v_ref, d_ref, nm_ref, nv_ref):
        g = g_ref[...]
        nm = ADAM_B1 * m_ref[...] + (1.0 - ADAM_B1) * g
        nv = ADAM_B2 * v_ref[...] + (1.0 - ADAM_B2) * jnp.square(g)
        d_ref[...] = -ADAM_LR * ((nm / c1) / (jnp.sqrt(nv / c2) + ADAM_EPS) + ADAM_WD * w_ref[...])
        nm_ref[...] = nm
        nv_ref[...] = nv

    blk = pl.BlockSpec((tr, cols), lambda i: (i, 0))
    return pl.pallas_call(
        body, grid=(rows // tr,), name="adamw", in_specs=[blk] * 4, out_specs=[blk] * 3,
        out_shape=[jax.ShapeDtypeStruct((rows, cols), F32)] * 3,
    )(w, g, m, v)


def s5_prep(lam_re, lam_im, log_dt, b_re, b_im, c_re, c_im):
    lr = jnp.minimum(lam_re, -1e-4)
    li = lam_im
    dt = jnp.exp(log_dt)[:, None]
    mag = jnp.exp(lr * dt)
    ar = mag * jnp.cos(li * dt)
    ai = mag * jnp.sin(li * dt)
    den = lr * lr + li * li
    fr = ((ar - 1.0) * lr + ai * li) / den
    fi = (ai * lr - (ar - 1.0) * li) / den
    bbr = fr[..., None] * b_re - fi[..., None] * b_im
    bbi = fr[..., None] * b_im + fi[..., None] * b_re
    emb_b = lambda bb: _block_diag(bb.transpose(0, 2, 1).reshape(BW, S5_STATE), S5_GROUPS)
    emb_c = lambda cc: _block_diag(cc.transpose(0, 2, 1).reshape(S5_N, S5_GROUP), S5_GROUPS)
    bmat = jnp.concatenate([emb_b(bbr), emb_b(bbi)], axis=1)
    cmat = jnp.concatenate([emb_c(c_re), -emb_c(c_im)], axis=0)
    nb = S5_N // LANE
    return ar.reshape(nb, 1, LANE), ai.reshape(nb, 1, LANE), bmat, cmat


def _block_diag(stacked, groups):
    rows, c = stacked.shape
    r = rows // groups
    row_g = jnp.arange(rows)[:, None] // r
    col_g = jnp.arange(groups * c)[None, :] // c
    return jnp.where(row_g == col_g, jnp.tile(stacked, (1, groups)), 0.0)


def rg_prep(w):
    return _block_diag(w.reshape(BW, RG_BLOCK), RG_BLOCKS)


def hg_prep(logits):
    p = jax.nn.softmax(logits, axis=0)
    return jnp.cumsum(p, axis=0) - p[0]


def _head_mean_matrix():
    r = jnp.arange(BW) // HG_D
    return (r[:, None] == r[None, :]).astype(F32) / HG_D


def _to_segment_order(a):
    L = a.shape[0]
    return a.reshape(NSEG, L // NSEG, -1).transpose(1, 0, 2).reshape(a.shape)


def _to_time_order(a):
    L = a.shape[0]
    return a.reshape(L // NSEG, NSEG, -1).transpose(1, 0, 2).reshape(a.shape)


def _const(*idx):
    return lambda s, i: idx


def _rows(cb=0):
    return lambda s, i: (i, cb)


def _sum_parts(name, first, parts, shape):
    return add_n(name, [(first, ())] + [(parts, (s,)) for s in range(NSH)], shape)


def _ffn_weight_specs(l, j):
    F = D_FF // NSH
    one = pl.Buffered(1)
    return [pl.BlockSpec((None, NSH, D_MODEL, F), lambda i: (j, 0, 0, 0), pipeline_mode=one),
            pl.BlockSpec((None, NSH, D_MODEL, F), lambda i: (j, 0, 0, 0), pipeline_mode=one),
            pl.BlockSpec((None, NSH, F, D_MODEL), lambda i: (j, 0, 0, 0), pipeline_mode=one)]


def ffn_fwd(name, x, W, l, j, k, L, tm):
    D, F = D_MODEL, D_FF // NSH

    def body(x_ref, nw_ref, wg_ref, wu_ref, wd_ref, y_ref, g_ref, u_ref):
        x = x_ref[...]
        h = _rms(x, nw_ref[...]).astype(MMT)
        y = x
        for s in range(NSH):
            g = _dg(h, wg_ref[s], 1, 0)
            u = _dg(h, wu_ref[s], 1, 0)
            g_ref[s] = g.astype(g_ref.dtype)
            u_ref[s] = u.astype(u_ref.dtype)
            y = y + 0.5 * _dg((jax.nn.silu(g) * u).astype(MMT), wd_ref[s], 1, 0)
        y_ref[...] = y

    row = pl.BlockSpec((tm, D), lambda i: (i, 0))
    act = pl.BlockSpec((NSH, tm, F), lambda i: (0, i, 0))
    return pl.pallas_call(
        body, grid=(L // tm,), name=name,
        in_specs=[row, pl.BlockSpec((None, None, 1, D), lambda i: (l, k, 0, 0))] + _ffn_weight_specs(l, j),
        out_specs=[row, act, act],
        out_shape=[jax.ShapeDtypeStruct((L, D), F32), jax.ShapeDtypeStruct((NSH, L, F), MMT),
                   jax.ShapeDtypeStruct((NSH, L, F), MMT)],
        compiler_params=pltpu.CompilerParams(vmem_limit_bytes=VMEM_LIMIT, dimension_semantics=("arbitrary",)),
    )(x, W["nw"], W["L"][l]["wg"], W["L"][l]["wu"], W["L"][l]["wd"])


def ffn_bwd(name, x, g, u, dy, W, bufs, l, j, k, L, tm):
    D, F = D_MODEL, D_FF // NSH
    tm = min(TM_WGRAD, L)

    def body(x_ref, nw_ref, dy_ref, g_ref, u_ref, wg_ref, wu_ref, wd_ref, *rest):
        part_ref, dnw_ref, dwg_ref, dwu_ref, dwd_ref = rest[-5:]
        s, i = pl.program_id(0), pl.program_id(1)
        x, nw = x_ref[...], nw_ref[...]
        r = lax.rsqrt(jnp.mean(x * x, axis=-1, keepdims=True) + EPS)
        xhat = x * r
        h = (xhat * nw).astype(MMT)
        half_dy = (0.5 * dy_ref[...]).astype(MMT)
        gs, us = g_ref[...].astype(F32), u_ref[...].astype(F32)
        sig = jax.nn.sigmoid(gs)
        act = gs * sig
        da = _dg(half_dy, wd_ref[...], 1, 1)
        du = (da * act).astype(MMT)
        dg = (da * us * (sig * (1.0 + gs * (1.0 - sig)))).astype(MMT)
        dh = _dg(dg, wg_ref[...], 1, 1) + _dg(du, wu_ref[...], 1, 1)
        dxh = dh * nw
        part_ref[...] = r * (dxh - xhat * jnp.mean(dxh * xhat, axis=-1, keepdims=True))
        grads = (_dg(h, dg, 0, 0), _dg(h, du, 0, 0), _dg((act * us).astype(MMT), half_dy, 0, 0))
        dnw = jnp.sum(dh * xhat, axis=0, keepdims=True)
        first = jnp.logical_and(s == 0, i == 0)
        for ref, val, start in zip((dwg_ref, dwu_ref, dwd_ref, dnw_ref), grads + (dnw,), (i == 0, i == 0, i == 0, first)):
            @pl.when(start)
            def _(ref=ref, val=val):
                ref[...] = val

            @pl.when(jnp.logical_not(start))
            def _(ref=ref, val=val):
                ref[...] += val

    keys = ("ffn_gate", "ffn_up", "ffn_down")
    given = [bufs[key] for key in keys if bufs.get(key) is not None]
    row = pl.BlockSpec((tm, D), lambda s, i: (i, 0))
    act = pl.BlockSpec((None, tm, F), lambda s, i: (s, i, 0))
    wsp = lambda r, c: pl.BlockSpec((None, None, r, c), lambda s, i: (j, s, 0, 0))
    stk = lambda r, c: pl.BlockSpec((None, None, r, c), lambda s, i: (2 * l + j, s, 0, 0))
    part, dnw, bufs["ffn_gate"], bufs["ffn_up"], bufs["ffn_down"] = pl.pallas_call(
        body, grid=(NSH, L // tm), name=name,
        in_specs=[row, pl.BlockSpec((None, None, 1, D), lambda s, i: (l, k, 0, 0)), row, act, act,
                  wsp(D, F), wsp(D, F), wsp(F, D)] + [pl.BlockSpec(memory_space=pl.ANY)] * len(given),
        out_specs=[pl.BlockSpec((None, tm, D), lambda s, i: (s, i, 0)), pl.BlockSpec((1, D), lambda s, i: (0, 0)),
                   stk(D, F), stk(D, F), stk(F, D)],
        out_shape=[jax.ShapeDtypeStruct((NSH, L, D), F32), jax.ShapeDtypeStruct((1, D), F32)]
        + [jax.ShapeDtypeStruct((2 * DEPTH, NSH, D, F), F32)] * 2 + [jax.ShapeDtypeStruct((2 * DEPTH, NSH, F, D), F32)],
        input_output_aliases={8 + n: 2 + n for n in range(len(given))},
        compiler_params=pltpu.CompilerParams(vmem_limit_bytes=VMEM_LIMIT, dimension_semantics=("arbitrary", "arbitrary")),
    )(x, W["nw"], dy, g, u, W["L"][l]["wg"], W["L"][l]["wu"], W["L"][l]["wd"], *given)
    return _sum_parts(name + "_dx", dy, part, (L, D)), dnw


def layer_fwd(l, x0, W, P, L, tm):
    D = D_MODEL
    n_i = L // tm
    x1, g0, u0 = ffn_fwd(f"ffn_fwd_{l}0", x0, W, l, 0, 0, L, tm)
    proj = tile_fwd(
        lambda x, nw, win, s: pre_core(x, nw, win), f"pre_fwd_{l}", n_i, NSH,
        [(x1, (tm, D), _rows()), (W["nw"], (None, None, 1, D), _const(l, 1, 0, 0)),
         (W["L"][l]["win"], (None, D, IN_TOTAL // NSH), lambda s, i: (s, 0, 0))],
        [((L, IN_TOTAL), F32, (tm, IN_TOTAL // NSH), lambda s, i: (i, s), False)], s_outer=True)[0]
    nb = S5_N // LANE
    blk3 = lambda s, i: (0, i, 0)
    bur, bui = tile_fwd(
        lambda u, bmat, s: s5_pre_core(u, bmat), f"s5pre_fwd_{l}", n_i, 1,
        [(proj, (tm, BW), _rows(0)), (P["bmat"], (None, BW, 2 * S5_N), _const(l, 0, 0))],
        [((nb, L, LANE), F32, (nb, tm, LANE), blk3, False)] * 2)
    xr, xi = s5_scan_fwd(bur, bui, P["ar"][l], P["ai"][l], L)
    qzv = _to_time_order(proj[:, BW:4 * BW])
    o_t, sst = hg_fwd(qzv, P["lb"][l], L)
    o = _to_segment_order(o_t)
    xc = conv_fwd(proj, W["convw"][l], P["convb"][l], L)
    vec = (None, 1, BW)
    a, b = tile_fwd(
        lambda xc, wa, ba, wx, bx, lam, s: gates_core(xc, wa, ba, wx, bx, lam), f"gates_fwd_{l}", n_i, 1,
        [(xc, (tm, BW), _rows()), (P["wa"], (None, BW, BW), _const(l, 0, 0)), (P["ba"], vec, _const(l, 0, 0)),
         (P["wx"], (None, BW, BW), _const(l, 0, 0)), (P["bx"], vec, _const(l, 0, 0)), (P["lam"], vec, _const(l, 0, 0))],
        [((L, BW), F32, (tm, BW), _rows(), False)] * 2)
    hs = rg_scan_fwd(a, b, L)
    tmm = tm
    ya, yb, yc = tile_fwd(
        lambda *a: mid_core(*a[:-1]), f"mid_fwd_{l}", L // tmm, 1,
        [(xr, (nb, tmm, LANE), blk3), (xi, (nb, tmm, LANE), blk3), (proj, (tmm, BW), _rows(0)), (o, (tmm, BW), _rows()),
         (proj, (tmm, BW), _rows(4)), (hs, (tmm, BW), _rows()), (proj, (tmm, BW), _rows(6)),
         (P["hmat"], (BW, BW), _const(0, 0)), (P["cmat"], (None, 2 * S5_N, BW), _const(l, 0, 0)), (P["d"], vec, _const(l, 0, 0)),
         (W["L"][l]["gluw"], (BW, BW), _const(0, 0)), (P["glub"], vec, _const(l, 0, 0)), (P["hgw"], vec, _const(l, 0, 0))],
        [((L, BW), F32, (tmm, BW), _rows(), False)] * 3)
    x2 = tile_fwd(
        lambda x, *rest: (x + merge_core(*rest[:-1])[0],), f"merge_fwd_{l}", n_i, 1,
        [(x1, (tm, D), _rows()), (ya, (tm, BW), _rows()), (yb, (tm, BW), _rows()), (yc, (tm, BW), _rows())]
        + [(proj, (tm, BW), _rows(7 + k)) for k in range(6)]
        + [(W["L"][l]["pfull"], (3, BW, D), _const(0, 0, 0)), (W["L"][l]["woutfull"], (D, D), _const(0, 0))],
        [((L, D), F32, (tm, D), _rows(), False)])[0]
    x3, g1, u1 = ffn_fwd(f"ffn_fwd_{l}1", x2, W, l, 1, 2, L, tm)
    saved = dict(x0=x0, x1=x1, x2=x2, proj=proj, xr=xr, xi=xi, o=o, sst=sst, xc=xc, a=a, hs=hs, ya=ya, yb=yb, yc=yc,
                 qzv=qzv, g0=g0, u0=u0, g1=g1, u1=u1)
    return x3, saved


def layer_bwd(l, dx3, sv, W, P, bufs, L, tm, ready=lambda l, group: None):
    D = D_MODEL
    n_i = L // tm
    nb = S5_N // LANE
    dq = D // NSH
    vec = (None, 1, BW)
    vout = ((1, BW), (1, BW), _const(0, 0), "acc_all")
    blk3 = lambda s, i: (0, i, 0)
    small = {}
    proj = sv["proj"]

    dx2, dnw2 = ffn_bwd(f"ffn_bwd_{l}1", sv["x2"], sv["g1"], sv["u1"], dx3, W, bufs, l, 1, 2, L, tm)
    ready(l, "ffn1")

    rw256 = ((L, BW), (tm, BW), _rows(), "write")
    res = tile_bwd(
        merge_core, f"merge_bwd_{l}", n_i, 1,
        [(sv["ya"], (tm, BW), _rows(), "r"), (sv["yb"], (tm, BW), _rows(), "r"), (sv["yc"], (tm, BW), _rows(), "r")]
        + [(proj, (tm, BW), _rows(7 + k), "r") for k in range(6)]
        + [(W["L"][l]["pfull"], (3, BW, D), _const(0, 0, 0), "w"), (W["L"][l]["woutfull"], (D, D), _const(0, 0), "w")],
        [(dx2, (tm, D), _rows())],
        [rw256] * 9
        + [((DEPTH, 3, BW, D), (None, 3, BW, D), _const(l, 0, 0, 0), "acc_all", bufs.get("branch_proj")),
           ((DEPTH, D, D), (None, D, D), _const(l, 0, 0), "acc_all", bufs.get("w_out"))])
    dya, dyb, dyc = res[:3]
    dgm = res[3:9]
    bufs["branch_proj"], bufs["w_out"] = res[9:]
    ready(l, "merge")

    tmm = tm
    rw = ((L, BW), (tmm, BW), _rows(), "write")
    xw = ((nb, L, LANE), (nb, tmm, LANE), blk3, "write")
    res = tile_bwd(
        mid_core, f"mid_bwd_{l}", L // tmm, 1,
        [(sv["xr"], (nb, tmm, LANE), blk3, "r"), (sv["xi"], (nb, tmm, LANE), blk3, "r"), (proj, (tmm, BW), _rows(0), "r"),
         (sv["o"], (tmm, BW), _rows(), "r"), (proj, (tmm, BW), _rows(4), "r"), (sv["hs"], (tmm, BW), _rows(), "r"),
         (proj, (tmm, BW), _rows(6), "r"), (P["hmat"], (BW, BW), _const(0, 0), "c"),
         (P["cmat"], (None, 2 * S5_N, BW), _const(l, 0, 0), "w"), (P["d"], vec, _const(l, 0, 0), "p"),
         (W["L"][l]["gluw"], (BW, BW), _const(0, 0), "w"), (P["glub"], vec, _const(l, 0, 0), "p"),
         (P["hgw"], vec, _const(l, 0, 0), "p")],
        [(dya, (tmm, BW), _rows()), (dyb, (tmm, BW), _rows()), (dyc, (tmm, BW), _rows())],
        [xw, xw, rw, rw, rw, rw, rw,
         ((DEPTH, 2 * S5_N, BW), (None, 2 * S5_N, BW), _const(l, 0, 0), "acc_all", bufs.get("cmat")), vout,
         ((DEPTH, BW, BW), (None, BW, BW), _const(l, 0, 0), "acc_all", bufs.get("s5_glu_w")), vout, vout])
    dxr, dxi, du_skip, do, dg_b, dhs, dgate_c, bufs["cmat"], dd, bufs["s5_glu_w"], dglub, dhgw = res
    small["s5_d"], small["s5_glu_b"], small["hg_norm_w"] = dd[0], dglub[0], dhgw[0]
    ready(l, "mid")

    da, db = rg_scan_bwd(sv["a"], sv["hs"], dhs, L)
    wmat = lambda key: ((DEPTH, BW, BW), (None, BW, BW), _const(l, 0, 0), "acc_all", bufs.get(key))
    res = tile_bwd(
        gates_core, f"gates_bwd_{l}", n_i, 1,
        [(sv["xc"], (tm, BW), _rows(), "r"), (P["wa"], (None, BW, BW), _const(l, 0, 0), "w"), (P["ba"], vec, _const(l, 0, 0), "p"),
         (P["wx"], (None, BW, BW), _const(l, 0, 0), "w"), (P["bx"], vec, _const(l, 0, 0), "p"), (P["lam"], vec, _const(l, 0, 0), "p")],
        [(da, (tm, BW), _rows()), (db, (tm, BW), _rows())],
        [((L, BW), (tm, BW), _rows(), "write"), wmat("wa"), vout, wmat("wx"), vout, vout])
    dxc, bufs["wa"], dba, bufs["wx"], dbx, dlam = res
    small["rg_ba"], small["rg_bx"], small["rg_lambda"] = dba[0], dbx[0], dlam[0]
    dx_c, dconvw, dconvb = conv_bwd(proj, W["convw"][l], dxc, L)
    small["rg_conv_w"], small["rg_conv_b"] = dconvw, dconvb[0]

    dq_b, dz_b, dv_b, dlb = hg_bwd(sv["qzv"], P["lb"][l], sv["sst"], _to_time_order(do), L)
    dq_b, dz_b, dv_b = [_to_segment_order(a) for a in (dq_b, dz_b, dv_b)]

    gr, gi, dar, dai = s5_scan_bwd(dxr, dxi, sv["xr"], sv["xi"], P["ar"][l], P["ai"][l], L)
    du_pre, bufs["bmat"] = tile_bwd(
        s5_pre_core, f"s5pre_bwd_{l}", n_i, 1,
        [(proj, (tm, BW), _rows(0), "r"), (P["bmat"], (None, BW, 2 * S5_N), _const(l, 0, 0), "w")],
        [(gr, (nb, tm, LANE), blk3), (gi, (nb, tm, LANE), blk3)],
        [((L, BW), (tm, BW), _rows(), "write"),
         ((DEPTH, BW, 2 * S5_N), (None, BW, 2 * S5_N), _const(l, 0, 0), "acc_all", bufs.get("bmat"))])
    du_a = add_n(f"du_a_{l}", [(du_skip, ()), (du_pre, ())], (L, BW))
    prep_ct = dict(dar=dar, dai=dai, dlb=dlb)

    pieces = [du_a, dq_b, dz_b, dv_b, dg_b, dx_c, dgate_c, *dgm]
    per_piece, per_shard = BW // LANE, IN_TOTAL // NSH // LANE
    part, dnw1 = None, []
    tmw = min(TM_WGRAD, L)
    for s in range(NSH):
        groups = [(pieces[g // per_piece], (tmw, LANE), _rows(g % per_piece))
                  for g in range(s * per_shard, (s + 1) * per_shard)]
        part, dnw_s, bufs["w_in"] = tile_bwd(
            pre_core, f"pre_bwd_{l}{s}", L // tmw, 1,
            [(sv["x1"], (tmw, D), _rows(), "r"), (W["nw"], (None, None, 1, D), _const(l, 1, 0, 0), "p"),
             (W["L"][l]["win"], (None, D, IN_TOTAL // NSH), _const(s, 0, 0), "w")],
            [groups],
            [((NSH, L, D), (None, tmw, D), functools.partial(lambda _s, i, s: (s, i, 0), s=s), "write", part),
             ((1, D), (1, D), _const(0, 0), "acc_all"),
             ((DEPTH, NSH, D, IN_TOTAL // NSH), (None, None, D, IN_TOTAL // NSH), _const(l, s, 0, 0), "acc_all",
              bufs.get("w_in"))])
        dnw1.append(dnw_s)
    dnw1 = (dnw1[0] + dnw1[1]) + (dnw1[2] + dnw1[3])
    dx1 = _sum_parts(f"pre_bwd_{l}_dx", dx2, part, (L, D))
    ready(l, "pre")

    dx0, dnw0 = ffn_bwd(f"ffn_bwd_{l}0", sv["x0"], sv["g0"], sv["u0"], dx1, W, bufs, l, 0, 0, L, tm)
    ready(l, "ffn0")
    small["norm_w"] = jnp.concatenate([dnw0, dnw1, dnw2], axis=0)
    return dx0, small, prep_ct


SMALL_RAW = ("s5_lambda_re", "s5_lambda_im", "s5_log_dt", "s5_b_re", "s5_b_im", "s5_c_re", "s5_c_im", "s5_d", "s5_glu_b",
             "hg_lb_logits", "hg_norm_w", "rg_conv_b", "rg_wa", "rg_ba", "rg_wx", "rg_bx", "rg_lambda", "final_norm_w")
DEPTH = 2


def local_step(x, target, W, raw, layer_weights=None, layer_grads=None):
    L = x.shape[0]
    tm = min(256, L)
    col = lambda v: v.reshape(DEPTH, 1, BW)
    (ar, ai, bmat, cmat), s5_vjp = jax.vjp(jax.vmap(s5_prep), *[raw[k] for k in SMALL_RAW[:7]])
    (wa, wx), rg_vjp = jax.vjp(lambda a, b: (jax.vmap(rg_prep)(a), jax.vmap(rg_prep)(b)), raw["rg_wa"], raw["rg_wx"])
    lb, hg_vjp = jax.vjp(hg_prep, raw["hg_lb_logits"])
    P = dict(
        ar=[ar[l] for l in range(DEPTH)], ai=[ai[l] for l in range(DEPTH)],
        bmat=bmat.astype(MMT), cmat=cmat.astype(MMT), wa=wa.astype(MMT), wx=wx.astype(MMT),
        lb=[lb[l].reshape(1, BW) for l in range(DEPTH)], convb=[raw["rg_conv_b"][l].reshape(1, BW) for l in range(DEPTH)],
        ba=col(raw["rg_ba"]), bx=col(raw["rg_bx"]), lam=col(raw["rg_lambda"]), d=col(raw["s5_d"]),
        glub=col(raw["s5_glu_b"]), hgw=col(raw["hg_norm_w"]), hmat=_head_mean_matrix())

    saved = []
    h = _to_segment_order(x)
    for l in range(DEPTH):
        if layer_weights is not None:
            W["L"][l], h = layer_weights(l, h)
        h, sv = layer_fwd(l, h, W, P, L, tm)
        saved.append(sv)
    loss, dh, dfw = loss_fwd_bwd(h, raw["final_norm_w"].reshape(1, D_MODEL), _to_segment_order(target), L, tm)

    big, per_layer, prep_cts = {}, [None] * DEPTH, [None] * DEPTH
    ready = (lambda l, group: None) if layer_grads is None else (lambda l, group: layer_grads(l, group, big))
    for l in reversed(range(DEPTH)):
        dh, sm, pc = layer_bwd(l, dh, saved[l], W, P, big, L, tm, ready)
        per_layer[l], prep_cts[l] = sm, pc
    dh = _to_time_order(dh)

    small = {k: jnp.stack([per_layer[l][k] for l in range(DEPTH)]) for k in per_layer[0]}
    both = lambda k: jnp.stack([prep_cts[l][k] for l in range(DEPTH)])
    s5_g = s5_vjp((both("dar"), both("dai"), big.pop("bmat"), big.pop("cmat")))
    small.update(zip(SMALL_RAW[:7], s5_g))
    small["rg_wa"], small["rg_wx"] = rg_vjp((big.pop("wa"), big.pop("wx")))
    (small["hg_lb_logits"],) = hg_vjp(jnp.concatenate([prep_cts[l]["dlb"] for l in range(DEPTH)], axis=0))
    small["final_norm_w"] = dfw[0]
    return loss, dh, big, small


ANY = pl.BlockSpec(memory_space=pl.ANY)


def _place():
    x, y, c = lax.axis_index("x"), lax.axis_index("y"), lax.axis_index("c")
    chips = [(1 - x, y), (x, 1 - y), (1 - x, 1 - y)]
    return x, y, c, chips


def _remote(src, dst, send, recv, k, to):
    return pltpu.make_async_remote_copy(src_ref=src, dst_ref=dst, send_sem=send.at[k], recv_sem=recv.at[k],
                                        device_id=to, device_id_type=MESH)


def _comm_call(body, name, ins, out_shapes, n_sem, n_loc):
    return pl.pallas_call(
        body, name=name, in_specs=[ANY] * len(ins), out_specs=[ANY] * len(out_shapes), out_shape=out_shapes,
        scratch_shapes=[pltpu.SemaphoreType.DMA((n_sem,)), pltpu.SemaphoreType.DMA((n_sem,)),
                        pltpu.SemaphoreType.DMA((max(n_loc, 1),))],
    )(*ins)


def gather_shards(name, shards):
    n = len(shards)
    per = 8

    def body(*refs):
        ins, outs = refs[:n], refs[n:2 * n]
        send, recv, _ = refs[2 * n:]
        x, y, c, chips = _place()
        me = 2 * x + y
        sib = (x, y, 1 - c)
        sends = []
        for w in range(n):
            for j, (cx, cy) in enumerate(chips):
                cp = _remote(ins[w].at[c], outs[w].at[c, me], send, recv, per * w + j, (cx, cy, c))
                cp.start()
                sends.append(cp)
        for w in range(n):
            for l in range(2):
                cp = _remote(ins[w].at[l], outs[w].at[l, me], send, recv, per * w + 6 + l, sib)
                cp.start()
                sends.append(cp)
        for w in range(n):
            for j, (cx, cy) in enumerate(chips):
                theirs = outs[w].at[c, 2 * cx + cy]
                _remote(ins[w].at[c], theirs, send, recv, per * w + j, (cx, cy, c)).wait_recv()
                cp = _remote(theirs, theirs, send, recv, per * w + 3 + j, sib)
                cp.start()
                sends.append(cp)
        for w in range(n):
            for j, (cx, cy) in enumerate(chips):
                dst = outs[w].at[1 - c, 2 * cx + cy]
                _remote(dst, dst, send, recv, per * w + 3 + j, sib).wait_recv()
            for l in range(2):
                dst = outs[w].at[l, me]
                _remote(dst, dst, send, recv, per * w + 6 + l, sib).wait_recv()
        for cp in sends:
            cp.wait_send()

    shapes = [jax.ShapeDtypeStruct((2, NSH) + s.shape[1:], s.dtype) for s in shards]
    return _comm_call(body, name, shards, shapes, per * n, 0)


def exchange_halves(name, grads, ranges):
    n = len(grads)

    def body(*refs):
        ins, outs = refs[:n], refs[n:2 * n]
        send, recv, _ = refs[2 * n:]
        x, y, c, _chips = _place()
        cps = []
        for w in range(n):
            h = grads[w].shape[2] // 2
            p0, np_ = ranges[w]
            cp = _remote(ins[w].at[pl.ds(p0, np_), :, pl.ds((1 - c) * h, h)], outs[w], send, recv, w, (x, y, 1 - c))
            cp.start()
            cps.append(cp)
        for cp in cps:
            cp.wait()

    shapes = [jax.ShapeDtypeStruct((r[1], NSH, g.shape[2] // 2, g.shape[3]), g.dtype) for g, r in zip(grads, ranges)]
    return _comm_call(body, name, grads, shapes, n, 0)


def scatter_to_chips(name, halves):
    n = len(halves)

    def body(*refs):
        ins, outs = refs[:n], refs[n:2 * n]
        send, recv, _ = refs[2 * n:]
        x, y, c, chips = _place()
        cps = []
        for w in range(n):
            for j, (cx, cy) in enumerate(chips):
                cp = _remote(ins[w].at[:, 2 * cx + cy], outs[w].at[j], send, recv, 3 * w + j, (cx, cy, c))
                cp.start()
                cps.append(cp)
        for cp in cps:
            cp.wait()

    shapes = [jax.ShapeDtypeStruct((3, h.shape[0]) + h.shape[2:], h.dtype) for h in halves]
    return _comm_call(body, name, halves, shapes, 3 * n, 0)


def share_halves(name, pieces):
    n = len(pieces)

    def body(*refs):
        ins, outs = refs[:n], refs[n:2 * n]
        send, recv, _ = refs[2 * n:]
        x, y, c, _chips = _place()
        cps = []
        for w in range(n):
            cp = _remote(ins[w], outs[w], send, recv, w, (x, y, 1 - c))
            cp.start()
            cps.append(cp)
        for cp in cps:
            cp.wait()

    return _comm_call(body, name, pieces, [jax.ShapeDtypeStruct(p.shape, p.dtype) for p in pieces], n, 0)


def add_own_half(name, g, ra, c, wire, b0):
    nblk, h, cols = ra.shape
    tr = _row_tile(h, cols, mult=16)
    nt = h // tr

    def body(c_ref, g_ref, r_ref, o_ref):
        o_ref[...] = (g_ref[...] + r_ref[...]).astype(o_ref.dtype)

    blk = (None, tr, cols)
    return pl.pallas_call(
        body, name=name,
        grid_spec=pltpu.PrefetchScalarGridSpec(
            num_scalar_prefetch=1, grid=(nblk, nt),
            in_specs=[pl.BlockSpec(blk, lambda s, i, c_ref: (b0 + s, c_ref[0] * nt + i, 0)), pl.BlockSpec(blk, lambda s, i, c_ref: (s, i, 0))],
            out_specs=pl.BlockSpec(blk, lambda s, i, c_ref: (s, i, 0))),
        out_shape=jax.ShapeDtypeStruct(ra.shape, wire),
    )(c.reshape(1), g, ra)


def add_chips(name, hb, rb, me):
    npc, _, h, cols = hb.shape
    tr = _row_tile(h, cols, mult=16)

    def body(me_ref, h_ref, r0, r1, r2, o_ref):
        f = lambda r: r[...].astype(F32)
        o_ref[...] = ((f(h_ref) + f(r0)) + f(r1)) + f(r2)

    rspec = lambda j: pl.BlockSpec((None, None, tr, cols), functools.partial(lambda p, i, me_ref, j: (j, p, i, 0), j=j))
    return pl.pallas_call(
        body, name=name,
        grid_spec=pltpu.PrefetchScalarGridSpec(
            num_scalar_prefetch=1, grid=(npc, h // tr),
            in_specs=[pl.BlockSpec((None, None, tr, cols), lambda p, i, me_ref: (p, me_ref[0], i, 0)), rspec(0), rspec(1), rspec(2)],
            out_specs=pl.BlockSpec((None, tr, cols), lambda p, i, me_ref: (p, i, 0))),
        out_shape=jax.ShapeDtypeStruct((npc, h, cols), F32),
    )(me.reshape(1), hb, rb, rb, rb)


def adamw_halves(name, w, m, v, own, other, c):
    npc, rows, cols = w.shape
    h = rows // 2
    tr = _row_tile(h, cols, budget=1024 * 1024)
    nt = h // tr
    c1 = 1.0 - ADAM_B1 ** ADAM_STEP
    c2 = 1.0 - ADAM_B2 ** ADAM_STEP

    def body(c_ref, w_ref, m_ref, v_ref, own_ref, oth_ref, g_ref, d_ref, nm_ref, nv_ref):
        g = jnp.where(pl.program_id(1) == c_ref[0], own_ref[...], oth_ref[...])
        nm = ADAM_B1 * m_ref[...] + (1.0 - ADAM_B1) * g
        nv = ADAM_B2 * v_ref[...] + (1.0 - ADAM_B2) * jnp.square(g)
        g_ref[...] = g
        d_ref[...] = -ADAM_LR * ((nm / c1) / (jnp.sqrt(nv / c2) + ADAM_EPS) + ADAM_WD * w_ref[...])
        nm_ref[...] = nm
        nv_ref[...] = nv

    full = pl.BlockSpec((None, tr, cols), lambda p, hh, i, c_ref: (p, hh * nt + i, 0))
    half = pl.BlockSpec((None, tr, cols), lambda p, hh, i, c_ref: (p, i, 0))
    return pl.pallas_call(
        body, name=name,
        grid_spec=pltpu.PrefetchScalarGridSpec(
            num_scalar_prefetch=1, grid=(npc, 2, nt),
            in_specs=[full, full, full, half, half], out_specs=[full] * 4),
        out_shape=[jax.ShapeDtypeStruct(w.shape, F32)] * 4,
    )(c.reshape(1), w, m, v, own, other)


WEIGHTS = ("norm_w", "final_norm_w", "ffn_gate", "ffn_up", "ffn_down", "w_in", "branch_proj", "w_out", "s5_lambda_re",
           "s5_lambda_im", "s5_log_dt", "s5_b_re", "s5_b_im", "s5_c_re", "s5_c_im", "s5_d", "s5_glu_w", "s5_glu_b",
           "hg_lb_logits", "hg_norm_w", "rg_conv_w", "rg_conv_b", "rg_wa", "rg_ba", "rg_wx", "rg_bx", "rg_lambda")
BIG = ("ffn_gate", "ffn_up", "ffn_down", "w_in", "branch_proj", "w_out", "s5_glu_w")
SHARDED_SMALL = ("norm_w", "rg_conv_w")
SMALL = SMALL_RAW + SHARDED_SMALL


def _view2d(shape):
    return (1, shape[0]) if len(shape) == 1 else (math.prod(shape[:-1]), shape[-1])


def _small_layout(shapes, row_multiple):
    layout, at = [], 0
    for shape in shapes:
        r, c = _view2d(shape)
        rp = -(-r // 8) * 8
        layout.append((at, r, c, rp))
        at += rp * max(1, c // LANE)
    return layout, -(-at // row_multiple) * row_multiple


def pack_small(name, arrays, row_multiple):
    layout, rows = _small_layout([a.shape for a in arrays], row_multiple)

    def body(*refs):
        out = refs[-1]
        out[...] = jnp.zeros_like(out)
        for ref, (r0, r, c, rp) in zip(refs[:-1], layout):
            if c <= LANE:
                out[r0:r0 + r, 0:c] = ref[...]
            else:
                for q in range(c // LANE):
                    out[r0 + q * rp:r0 + q * rp + r, :] = ref[:, q * LANE:(q + 1) * LANE]

    return pl.pallas_call(
        body, name=name, out_shape=jax.ShapeDtypeStruct((rows, LANE), F32),
        compiler_params=pltpu.CompilerParams(vmem_limit_bytes=VMEM_LIMIT),
    )(*[a.reshape(_view2d(a.shape)) for a in arrays])


def unpack_small(name, packed, shapes):
    layout, _ = _small_layout(shapes, 8)

    def body(p_ref, *outs):
        for ref, (r0, r, c, rp) in zip(outs, layout):
            if c <= LANE:
                ref[...] = p_ref[r0:r0 + r, 0:c]
            else:
                for q in range(c // LANE):
                    ref[:, q * LANE:(q + 1) * LANE] = p_ref[r0 + q * rp:r0 + q * rp + r, :]

    res = pl.pallas_call(
        body, name=name, out_shape=[jax.ShapeDtypeStruct(_view2d(s), F32) for s in shapes],
        compiler_params=pltpu.CompilerParams(vmem_limit_bytes=VMEM_LIMIT),
    )(packed)
    return [a.reshape(s) for a, s in zip(res, shapes)]


HBM = pl.BlockSpec(memory_space=pltpu.HBM)
SEM = pl.BlockSpec(memory_space=pltpu.SEMAPHORE)
EFFECT = pltpu.SideEffectType.DATAFLOW_SIDE_EFFECTING


def split_start(name, srcs, land_shapes, plan, n_send, n_recv):
    ns, nl = len(srcs), len(land_shapes)

    def body(*refs):
        ins, lands = refs[:ns], refs[ns:ns + nl]
        send, recv = refs[ns + nl], refs[ns + nl + 1]
        for src, dst, ks, kr, dev in plan(ins, lands):
            pltpu.make_async_remote_copy(src_ref=src, dst_ref=dst, send_sem=send.at[ks], recv_sem=recv.at[kr],
                                         device_id=dev, device_id_type=MESH).start()
        refs[-1][...] = jnp.zeros_like(refs[-1])

    hbm = lambda a: pltpu.with_memory_space_constraint(a, pltpu.HBM)
    lands = [lax.empty(s.shape, s.dtype) for s in land_shapes]
    out = pl.pallas_call(
        body, name=name,
        out_shape=(pltpu.SemaphoreType.DMA((n_send,)), pltpu.SemaphoreType.DMA((n_recv,)),
                   *[pltpu.HBM(a.shape, a.dtype) for a in srcs], *[pltpu.HBM(s.shape, s.dtype) for s in land_shapes],
                   jax.ShapeDtypeStruct((8, LANE), F32)),
        in_specs=[HBM] * (ns + nl), out_specs=(SEM, SEM, *[HBM] * (ns + nl), pl.BlockSpec(memory_space=pltpu.VMEM)),
        input_output_aliases={k: 2 + k for k in range(ns + nl)},
        compiler_params=pltpu.CompilerParams(has_side_effects=EFFECT),
    )(*[hbm(a) for a in srcs], *[hbm(a) for a in lands])
    return out[:-1], out[-1]


def split_wait(name, handles, n_src, waits, after):
    send, recv, *bufs = handles
    nb = len(bufs)

    def body(*refs):
        ins, lands = refs[:n_src], refs[n_src:nb]
        send_sem, recv_sem = refs[nb], refs[nb + 1]
        x, y, c, _chips = _place()
        sends, recvs = waits(ins, lands)
        for src, k in sends:
            pltpu.make_async_remote_copy(src_ref=src, dst_ref=src, send_sem=send_sem.at[k], recv_sem=recv_sem.at[0],
                                         device_id=(x, y, 1 - c), device_id_type=MESH).wait_send()
        for dst, k in recvs:
            pltpu.make_async_remote_copy(src_ref=dst, dst_ref=dst, send_sem=send_sem.at[0], recv_sem=recv_sem.at[k],
                                         device_id=(x, y, 1 - c), device_id_type=MESH).wait_recv()

    out = pl.pallas_call(
        body, name=name, out_shape=tuple(pltpu.HBM(a.shape, a.dtype) for a in bufs),
        in_specs=[HBM] * nb + [SEM, SEM, ANY], out_specs=tuple([HBM] * nb),
        input_output_aliases={k: k for k in range(nb)},
        compiler_params=pltpu.CompilerParams(has_side_effects=EFFECT),
    )(*bufs, send, recv, after)
    return list(out[:n_src]), list(out[n_src:])


def gather_plan(n):
    def plan(ins, lands):
        x, y, c, chips = _place()
        me = 2 * x + y
        copies = []
        for w in range(n):
            for j, (cx, cy) in enumerate(chips):
                for t in range(2):
                    copies.append((ins[w].at[c], lands[w].at[c, me], 8 * w + 2 * j + t, 8 * w + 2 * j + c, (cx, cy, t)))
            for half in range(2):
                copies.append((ins[w].at[half], lands[w].at[half, me], 8 * w + 6 + half, 8 * w + 6 + half, (x, y, 1 - c)))
        return copies

    def waits(ins, lands):
        x, y, c, chips = _place()
        me = 2 * x + y
        sends, recvs = [], []
        for w in range(n):
            for j, (cx, cy) in enumerate(chips):
                for t in range(2):
                    sends.append((ins[w].at[c], 8 * w + 2 * j + t))
                    recvs.append((lands[w].at[t, 2 * cx + cy], 8 * w + 2 * j + t))
            for half in range(2):
                sends.append((ins[w].at[half], 8 * w + 6 + half))
                recvs.append((lands[w].at[half, me], 8 * w + 6 + half))
        return sends, recvs

    return plan, waits


def scatter_plan(n):
    def plan(ins, lands):
        x, y, c, chips = _place()
        return [(ins[w].at[:, 2 * cx + cy], lands[w].at[j], 3 * w + j, 3 * w + j, (cx, cy, c))
                for w in range(n) for j, (cx, cy) in enumerate(chips)]

    def waits(ins, lands):
        x, y, c, chips = _place()
        sends = [(ins[w].at[:, 2 * cx + cy], 3 * w + j) for w in range(n) for j, (cx, cy) in enumerate(chips)]
        recvs = [(lands[w].at[j], 3 * w + j) for w in range(n) for j in range(3)]
        return sends, recvs

    return plan, waits


def _layer_shards(w, l):
    return [w["ffn_gate"][l].astype(MMT), w["ffn_up"][l].astype(MMT), w["ffn_down"][l].astype(MMT),
            w["w_in"][l].reshape(2, D_MODEL // 2, -1).astype(MMT),
            w["branch_proj"][l].reshape(2, 3 * BW // 2, -1).astype(MMT),
            w["w_out"][l].reshape(2, -1, D_MODEL).astype(MMT),
            w["s5_glu_w"][l].reshape(2, -1, BW).astype(MMT)]


def _layer_weights(g):
    rows = lambda a: a.transpose(1, 0, 2, 3).reshape(NSH, -1, a.shape[-1])
    p = rows(g[4]).reshape(NSH, 3, BW, -1).transpose(1, 2, 0, 3).reshape(3, BW, D_MODEL)
    return dict(wg=g[0], wu=g[1], wd=g[2], win=rows(g[3]), pfull=p,
                woutfull=rows(g[5]).reshape(D_MODEL, D_MODEL), gluw=rows(g[6]).reshape(BW, BW))


GROUPS = {"ffn1": ("ffn_gate", "ffn_up", "ffn_down"), "merge": ("branch_proj", "w_out"), "mid": ("s5_glu_w",),
          "pre": ("w_in",), "ffn0": ("ffn_gate", "ffn_up", "ffn_down")}


def _grad_views(big, l, group):
    four = lambda a: a.reshape(a.shape[0], NSH, -1, a.shape[-1])
    views = []
    for name in GROUPS[group]:
        if name == "branch_proj":
            dq = D_MODEL // NSH
            bp = big[name][l].reshape(3, BW, NSH, dq).transpose(2, 0, 1, 3).reshape(1, NSH, 3 * BW, dq)
            views.append((name, bp, 0))
        elif name.startswith("ffn"):
            views.append((name, four(big[name]), 2 * l + (1 if group == "ffn1" else 0)))
        else:
            views.append((name, four(big[name]), l))
    return views


def _reduce_to_halves(tag, views, c, wire):
    from_sibling = exchange_halves(f"reduce_cores_{tag}", [a for _, a, _ in views], [(p0, 1) for _, _, p0 in views])
    merge = lambda a: a.reshape((-1,) + a.shape[2:])
    return [add_own_half(f"sum_cores_{tag}_{i}", merge(a), merge(r), c, wire[i], NSH * p0).reshape(r.shape)
            for i, ((_, a, p0), r) in enumerate(zip(views, from_sibling))]


def _step(x, target, w, m, v):
    mx, my, mc = lax.axis_index("x"), lax.axis_index("y"), lax.axis_index("c")
    me = (2 * mx + my).astype(jnp.int32)
    mc = mc.astype(jnp.int32)

    W = dict(L=[None] * DEPTH)
    state = {"pending": []}
    n_big = len(BIG)
    g_plan, g_waits = gather_plan(n_big)

    def layer_weights(l, h):
        if l == 0:
            got = gather_shards("gather_weights_0", _layer_shards(w, 0) + [w[n] for n in SHARDED_SMALL])
            nxt = _layer_shards(w, 1)
            got, nxt = lax.optimization_barrier((got, nxt))
            shapes = [jax.ShapeDtypeStruct((2, NSH) + a.shape[1:], a.dtype) for a in nxt]
            state["gather"], token = split_start("gather_weights_1_start", nxt, shapes, g_plan, 8 * n_big, 8 * n_big)
            W["nw"] = got[n_big].transpose(0, 2, 1, 3).reshape(DEPTH, 3, 1, D_MODEL) + token[0, 0]
            W["convw"] = got[n_big + 1].transpose(0, 2, 1, 3).reshape(DEPTH, CONV_W, BW)
            return _layer_weights(got[:n_big]), h
        return _layer_weights(split_wait("gather_weights_1_wait", state["gather"], n_big, g_waits, h)[1]), h

    def layer_grads(l, group, big):
        if (l, group) == (0, "ffn0"):
            return
        views = _grad_views(big, l, group)
        tag = f"{l}_{group}"
        halves = _reduce_to_halves(tag, views, mc, [jnp.bfloat16] * len(views))
        shapes = [jax.ShapeDtypeStruct((3, a.shape[0]) + a.shape[2:], a.dtype) for a in halves]
        plan, waits = scatter_plan(len(halves))
        handles, token = split_start(f"reduce_chips_{tag}_start", halves, shapes, plan, 3 * len(halves), 3 * len(halves))
        W["nw"] = W["nw"] + token[0, 0]
        state["pending"].append((tag, [name for name, _, _ in views], l, group, handles, waits))

    loss, dx, big, small = local_step(x[0], target[0], W, {k: w[k] for k in SMALL_RAW}, layer_weights, layer_grads)

    pieces = {n: {} for n in BIG}
    block_of = lambda name, l, group: (2 * l + (group == "ffn1")) if name.startswith("ffn") else l
    views = _grad_views(big, 0, "ffn0")
    small_packed = pack_small("pack_small_grads", [small[n] for n in SMALL], NSH * 32)
    halves = _reduce_to_halves("0_ffn0", views + [("small", small_packed.reshape(1, NSH, -1, LANE), 0)], mc,
                               [jnp.bfloat16] * len(views) + [F32])
    from_chips = scatter_to_chips("reduce_chips_0_ffn0", halves)
    last = [add_chips(f"sum_chips_0_ffn0_{i}", h, r, me) for i, (h, r) in enumerate(zip(halves, from_chips))]
    for (name, _, _), piece in zip(views, last):
        pieces[name][block_of(name, 0, "ffn0")] = piece
    after = dx
    for tag, names, l, group, handles, waits in state["pending"]:
        sent, landed = split_wait(f"reduce_chips_{tag}_wait", handles, len(names), waits, after)
        for i, (name, h, r) in enumerate(zip(names, sent, landed)):
            pieces[name][block_of(name, l, group)] = add_chips(f"sum_chips_{tag}_{i}", h, r, me)
    own = [jnp.concatenate([pieces[n][b] for b in sorted(pieces[n])], axis=0) for n in BIG] + [last[-1]]
    other = share_halves("reduce_share", own)

    g, delta, new_m, new_v = {}, {}, {}, {}
    for i, n in enumerate(BIG):
        view = lambda a: a.reshape(own[i].shape[0], -1, own[i].shape[2])
        res = adamw_halves(f"adamw_{n}", view(w[n]), view(m[n]), view(v[n]), own[i], other[i], mc)
        g[n], delta[n], new_m[n], new_v[n] = [a.reshape(w[n].shape) for a in res]

    piece = jnp.stack([jnp.where(mc == 0, own[-1][0], other[-1][0]), jnp.where(mc == 0, other[-1][0], own[-1][0])])
    (all_small,) = gather_shards("gather_small", [piece])
    full_small = unpack_small("unpack_small_grads", all_small.transpose(1, 0, 2, 3).reshape(-1, LANE),
                              [small[n].shape for n in SMALL])
    g.update(zip(SMALL, full_small))
    g["norm_w"] = lax.dynamic_slice_in_dim(g["norm_w"], me * (D_MODEL // NSH), D_MODEL // NSH, axis=2)
    g["rg_conv_w"] = lax.dynamic_slice_in_dim(g["rg_conv_w"], me * (BW // NSH), BW // NSH, axis=2)

    packed = [pack_small(f"pack_small_{tag}", [src[n] for n in SMALL], 8)
              for tag, src in (("w", w), ("g", g), ("m", m), ("v", v))]
    for tag, dst, flat in zip(("delta", "m", "v"), (delta, new_m, new_v), adamw(*packed)):
        dst.update(zip(SMALL, unpack_small(f"unpack_small_{tag}", flat, [w[n].shape for n in SMALL])))

    total = lax.psum(loss[0, 0], ("x", "y", "c"))
    return (total, dx[None], *[g[n] for n in WEIGHTS], *[delta[n] for n in WEIGHTS],
            *[new_m[n] for n in WEIGHTS], *[new_v[n] for n in WEIGHTS])


def kernel(x, norm_w, final_norm_w, ffn_gate, ffn_up, ffn_down, w_in, branch_proj, w_out, s5_lambda_re, s5_lambda_im, s5_log_dt, s5_b_re, s5_b_im, s5_c_re, s5_c_im, s5_d, s5_glu_w, s5_glu_b, hg_lb_logits, hg_norm_w, rg_conv_w, rg_conv_b, rg_wa, rg_ba, rg_wx, rg_bx, rg_lambda, loss_target, m_norm_w, m_final_norm_w, m_ffn_gate, m_ffn_up, m_ffn_down, m_w_in, m_branch_proj, m_w_out, m_s5_lambda_re, m_s5_lambda_im, m_s5_log_dt, m_s5_b_re, m_s5_b_im, m_s5_c_re, m_s5_c_im, m_s5_d, m_s5_glu_w, m_s5_glu_b, m_hg_lb_logits, m_hg_norm_w, m_rg_conv_w, m_rg_conv_b, m_rg_wa, m_rg_ba, m_rg_wx, m_rg_bx, m_rg_lambda, v_norm_w, v_final_norm_w, v_ffn_gate, v_ffn_up, v_ffn_down, v_w_in, v_branch_proj, v_w_out, v_s5_lambda_re, v_s5_lambda_im, v_s5_log_dt, v_s5_b_re, v_s5_b_im, v_s5_c_re, v_s5_c_im, v_s5_d, v_s5_glu_w, v_s5_glu_b, v_hg_lb_logits, v_hg_norm_w, v_rg_conv_w, v_rg_conv_b, v_rg_wa, v_rg_ba, v_rg_wx, v_rg_bx, v_rg_lambda):
    ws = (norm_w, final_norm_w, ffn_gate, ffn_up, ffn_down, w_in, branch_proj, w_out, s5_lambda_re, s5_lambda_im, s5_log_dt, s5_b_re, s5_b_im, s5_c_re, s5_c_im, s5_d, s5_glu_w, s5_glu_b, hg_lb_logits, hg_norm_w, rg_conv_w, rg_conv_b, rg_wa, rg_ba, rg_wx, rg_bx, rg_lambda)
    ms = (m_norm_w, m_final_norm_w, m_ffn_gate, m_ffn_up, m_ffn_down, m_w_in, m_branch_proj, m_w_out, m_s5_lambda_re, m_s5_lambda_im, m_s5_log_dt, m_s5_b_re, m_s5_b_im, m_s5_c_re, m_s5_c_im, m_s5_d, m_s5_glu_w, m_s5_glu_b, m_hg_lb_logits, m_hg_norm_w, m_rg_conv_w, m_rg_conv_b, m_rg_wa, m_rg_ba, m_rg_wx, m_rg_bx, m_rg_lambda)
    vs = (v_norm_w, v_final_norm_w, v_ffn_gate, v_ffn_up, v_ffn_down, v_w_in, v_branch_proj, v_w_out, v_s5_lambda_re, v_s5_lambda_im, v_s5_log_dt, v_s5_b_re, v_s5_b_im, v_s5_c_re, v_s5_c_im, v_s5_d, v_s5_glu_w, v_s5_glu_b, v_hg_lb_logits, v_hg_norm_w, v_rg_conv_w, v_rg_conv_b, v_rg_wa, v_rg_ba, v_rg_wx, v_rg_bx, v_rg_lambda)
    return _step(x, loss_target, dict(zip(WEIGHTS, ws)), dict(zip(WEIGHTS, ms)), dict(zip(WEIGHTS, vs)))
```

```python
import functools
import math
from typing import NamedTuple

import jax
import jax.numpy as jnp
from jax import lax
from jax.experimental import pallas as pl
from jax.experimental.pallas import tpu as pltpu

F32 = jnp.float32
MMT = jnp.bfloat16
HI = lax.Precision.HIGHEST

D_MODEL = 1024
BW = 512
S5_GROUP, S5_GROUPS, S5_STATE = 16, 32, 64
S5_N = S5_GROUPS * S5_STATE
HG_HEADS, HG_D = 4, 128
HG_CHUNK = 128
RG_BLOCKS, RG_BLOCK = 8, 64
RG_C = 8.0
CONV_W = 4
D_FF = 2816
EPS = 1e-6
IN_TOTAL = 6656
NSH = 4
NSEG = 8
LANE = 128
VMEM_LIMIT = 56 * 1024 * 1024
TM_FWD = 512
TM_WGRAD = 512

ADAM_LR, ADAM_B1, ADAM_B2, ADAM_EPS, ADAM_WD, ADAM_STEP = 0.001, 0.9, 0.999, 1e-08, 0.01, 10

MESH = pl.DeviceIdType.MESH


class WP(NamedTuple):
    w: jax.Array
    p: jax.Array


def _dg(a, b, ca, cb):
    return lax.dot_general(a, b, (((ca,), (cb,)), ((), ())), preferred_element_type=F32)


@jax.custom_vjp
def _mmw(a, w, p):
    return _dg(a.astype(MMT), w, 1, 0)


def _mmw_fwd(a, w, p):
    return _mmw(a, w, p), (a, w)


def _mmw_bwd(res, g):
    a, w = res
    gb = g.astype(MMT)
    return _dg(gb, w, 1, 1), jnp.zeros_like(w), _dg(a.astype(MMT), gb, 0, 0)


_mmw.defvjp(_mmw_fwd, _mmw_bwd)


def mm(a, w):
    if isinstance(w, WP):
        return _mmw(a, w.w, w.p)
    return _dg(a.astype(MMT), w, 1, 0)


@jax.custom_vjp
def mma_nn(a, b):
    return _dg(a.astype(MMT), b.astype(MMT), 1, 0)


def _nn_f(a, b):
    return mma_nn(a, b), (a, b)


def _nn_b(res, g):
    a, b = res
    gb = g.astype(MMT)
    return _dg(gb, b.astype(MMT), 1, 1), _dg(a.astype(MMT), gb, 0, 0)


mma_nn.defvjp(_nn_f, _nn_b)


@jax.custom_vjp
def mma_nt(a, b):
    return _dg(a.astype(MMT), b.astype(MMT), 1, 1)


def _nt_f(a, b):
    return mma_nt(a, b), (a, b)


def _nt_b(res, g):
    a, b = res
    gb = g.astype(MMT)
    return _dg(gb, b.astype(MMT), 1, 0), _dg(gb, a.astype(MMT), 0, 0)


mma_nt.defvjp(_nt_f, _nt_b)


@jax.custom_vjp
def mma_tn(a, b):
    return _dg(a.astype(MMT), b.astype(MMT), 0, 0)


def _tn_f(a, b):
    return mma_tn(a, b), (a, b)


def _tn_b(res, g):
    a, b = res
    gb = g.astype(MMT)
    return _dg(b.astype(MMT), gb, 1, 1), _dg(a.astype(MMT), gb, 1, 0)


mma_tn.defvjp(_tn_f, _tn_b)


def mm_exact(m, x):
    return jnp.dot(m, x, precision=HI, preferred_element_type=F32)


def _rms(x, w):
    return x * lax.rsqrt(jnp.mean(x * x, axis=-1, keepdims=True) + EPS) * w


def _expm1(x):
    series = x * (1.0 + x * (1.0 / 2) * (1.0 + x * (1.0 / 3) * (1.0 + x * (1.0 / 4) * (1.0 + x * (1.0 / 5) * (1.0 + x * (1.0 / 6))))))
    return jnp.where(jnp.abs(x) < 0.1, series, jnp.exp(x) - 1.0)


def _bspec(block, fn, order):
    if order == "is":
        return pl.BlockSpec(block, lambda i, s: fn(s, i))
    return pl.BlockSpec(block, lambda s, i: fn(s, i))


def tile_fwd(fn, name, n_i, n_s, ins, outs, s_outer=False):
    n_in = len(ins)
    order = "si" if s_outer else "is"
    assert not (s_outer and any(o[4] for o in outs))

    def body(*refs):
        s = pl.program_id(0 if s_outer else 1)
        res = fn(*[r[...] for r in refs[:n_in]], s)
        for o_ref, val, spec in zip(refs[n_in:], res, outs):
            if spec[4] and n_s > 1:
                @pl.when(s == 0)
                def _(o_ref=o_ref, val=val):
                    o_ref[...] = val.astype(o_ref.dtype)

                @pl.when(s != 0)
                def _(o_ref=o_ref, val=val):
                    o_ref[...] += val.astype(o_ref.dtype)
            else:
                o_ref[...] = val.astype(o_ref.dtype)

    return pl.pallas_call(
        body, grid=(n_s, n_i) if s_outer else (n_i, n_s), name=name,
        in_specs=[_bspec(b, f, order) for _, b, f in ins],
        out_specs=[_bspec(b, f, order) for _, _, b, f, _ in outs],
        out_shape=[jax.ShapeDtypeStruct(sh, dt) for sh, dt, _, _, _ in outs],
        compiler_params=pltpu.CompilerParams(vmem_limit_bytes=VMEM_LIMIT,
                                             dimension_semantics=("arbitrary", "arbitrary")),
    )(*[a for a, _, _ in ins])


def tile_bwd(fn, name, n_i, n_s, ins, cts, gouts):
    groups = [c if isinstance(c, list) else [c] for c in cts]
    cts = [blk for grp in groups for blk in grp]
    n_in, n_ct = len(ins), len(cts)
    kinds = [k for _, _, _, k in ins]
    d_pos = [j for j, k in enumerate(kinds) if k != "c"]
    shared = [(gi, spec[4]) for gi, spec in enumerate(gouts) if len(spec) == 5 and spec[4] is not None]
    n_sh = len(shared)

    def body(*refs):
        s, i = pl.program_id(0), pl.program_id(1)
        vals = [r[...] for r in refs[:n_in]]
        ct_refs, ctv = list(refs[n_in:n_in + n_ct]), []
        for grp in groups:
            parts = [ct_refs.pop(0)[...] for _ in grp]
            ctv.append(parts[0] if len(parts) == 1 else jnp.concatenate(parts, axis=1))
        ctv = tuple(ctv)
        g_refs = refs[n_in + n_ct + n_sh:]

        def g(*dv):
            args = list(vals)
            for j, v in zip(d_pos, dv):
                args[j] = WP(vals[j], v) if kinds[j] == "w" else v
            return tuple(fn(*args))

        dv0 = [jnp.zeros(vals[j].shape, F32) if kinds[j] == "w" else vals[j] for j in d_pos]
        _, vjp = jax.vjp(g, *dv0)
        grads = vjp(ctv)
        for g_ref, gv, spec in zip(g_refs, grads, gouts):
            mode = spec[3]
            if mode == "write":
                g_ref[...] = gv.astype(g_ref.dtype)
            else:
                first = (i == 0) if mode == "acc_i" else jnp.logical_and(i == 0, s == 0)

                @pl.when(first)
                def _(g_ref=g_ref, gv=gv):
                    g_ref[...] = gv.astype(g_ref.dtype)

                @pl.when(jnp.logical_not(first))
                def _(g_ref=g_ref, gv=gv):
                    g_ref[...] += gv.astype(g_ref.dtype)

    return pl.pallas_call(
        body, grid=(n_s, n_i), name=name,
        in_specs=([_bspec(b, f, "si") for _, b, f, _ in ins] + [_bspec(b, f, "si") for _, b, f in cts]
                  + [pl.BlockSpec(memory_space=pl.ANY)] * n_sh),
        out_specs=[_bspec(spec[1], spec[2], "si") for spec in gouts],
        out_shape=[jax.ShapeDtypeStruct(spec[0], F32) for spec in gouts],
        input_output_aliases={n_in + n_ct + k: gi for k, (gi, _) in enumerate(shared)},
        compiler_params=pltpu.CompilerParams(vmem_limit_bytes=VMEM_LIMIT,
                                             dimension_semantics=("arbitrary", "arbitrary")),
    )(*[a for a, _, _, _ in ins], *[a for a, _, _ in cts], *[buf for _, buf in shared])


def _row_tile(rows, width, itemsize=4, budget=2 * 1024 * 1024, mult=8):
    best = mult
    for t in range(mult, rows + 1, mult):
        if rows % t == 0 and t * width * itemsize <= budget:
            best = t
    return best


def add_n(name, terms, shape):
    rows, cols = shape
    tr = _row_tile(rows, cols)

    def body(*refs):
        acc = refs[0][...]
        for r in refs[1:-1]:
            acc = acc + r[...]
        refs[-1][...] = acc

    specs = []
    for _, lead in terms:
        specs.append(pl.BlockSpec((None,) * len(lead) + (tr, cols), functools.partial(lambda i, lead: (*lead, i, 0), lead=lead)))
    return pl.pallas_call(
        body, grid=(rows // tr,), name=name, in_specs=specs,
        out_specs=pl.BlockSpec((tr, cols), lambda i: (i, 0)),
        out_shape=jax.ShapeDtypeStruct((rows, cols), F32),
    )(*[a for a, _ in terms])


def ffn_core(x, nw, wg, wu, wd):
    h = _rms(x, nw)
    return (0.5 * mm(jax.nn.silu(mm(h, wg)) * mm(h, wu), wd),)


def pre_core(x, nw, win):
    return (mm(_rms(x, nw), win),)


def _split_lanes(y):
    return jnp.stack([y[:, k * LANE:(k + 1) * LANE] for k in range(y.shape[1] // LANE)], axis=0)


def _join_lanes(y3):
    return jnp.concatenate([y3[k] for k in range(y3.shape[0])], axis=1)


def s5_pre_core(u, bmat):
    bu = mm(u, bmat)
    return _split_lanes(bu[:, :S5_N]), _split_lanes(bu[:, S5_N:])


def mid_core(xr, xi, u, o, g, hs, gc, hmat, cmat, d, gluw, glub, hgw):
    xs = jnp.concatenate([_join_lanes(xr), _join_lanes(xi)], axis=1)
    y = mm(xs, cmat) + d * u
    z = jax.nn.gelu(y)
    ya = z * jax.nn.sigmoid(mm(z, gluw) + glub)
    ms = mm_exact(o * o, hmat)
    yb = o * lax.rsqrt(ms + EPS) * hgw * jax.nn.silu(g)
    yc = hs * jax.nn.gelu(gc)
    return ya, yb, yc


def _sub(w, n):
    return WP(w.w[n], w.p[n]) if isinstance(w, WP) else w[n]


def merge_core(ya, yb, yc, g0, g1, g2, g3, g4, g5, p, wout):
    gate = lambda a, b: jax.nn.sigmoid(jnp.concatenate([a, b], axis=1))
    m = gate(g0, g1) * mm(ya, _sub(p, 0)) + gate(g2, g3) * mm(yb, _sub(p, 1)) + gate(g4, g5) * mm(yc, _sub(p, 2))
    return (mm(m, wout),)


def gates_core(xc, wa, ba, wx, bx, lam):
    r = jax.nn.sigmoid(mm(xc, wa) + ba)
    i = jax.nn.sigmoid(mm(xc, wx) + bx)
    log_a = -RG_C * jax.nn.softplus(-lam) * r
    a = jnp.exp(log_a)
    b = jnp.sqrt(-_expm1(2.0 * log_a)) * (i * xc)
    return a, b


def _seg_rows(ref, k, j, n):
    rows = pl.ds(pl.multiple_of(j * NSEG, NSEG), NSEG)
    if k is None:
        return ref[rows, :]
    return ref[k, rows, :]


def _seg_store(ref, k, j, n, val):
    rows = pl.ds(pl.multiple_of(j * NSEG, NSEG), NSEG)
    if k is None:
        ref[rows, :] = val
    else:
        ref[k, rows, :] = val


def _seg_carries(er, ei, pr, pi, reverse):
    rows = lax.broadcasted_iota(jnp.int32, er.shape, 0)
    cr = jnp.zeros_like(er)
    ci = None if ei is None else jnp.zeros_like(er)
    order = range(NSEG - 2, -1, -1) if reverse else range(1, NSEG)
    shift = NSEG - 1 if reverse else 1
    for s in order:
        if ei is None:
            tr = er + pr * cr
            cr = jnp.where(rows == s, pltpu.roll(tr, shift, 0), cr)
        else:
            tr = er + pr * cr - pi * ci
            ti = ei + pr * ci + pi * cr
            cr = jnp.where(rows == s, pltpu.roll(tr, shift, 0), cr)
            ci = jnp.where(rows == s, pltpu.roll(ti, shift, 0), ci)
    return cr, ci


S5_K = 2


def s5_scan_fwd(bur, bui, ar, ai, L):
    n = L // NSEG
    nb = S5_N // LANE
    K = S5_K

    def body(br_ref, bi_ref, ar_ref, ai_ref, xr_ref, xi_ref):
        zero = jnp.zeros((NSEG, LANE), F32)
        A = [(jnp.broadcast_to(ar_ref[k], (NSEG, LANE)), jnp.broadcast_to(ai_ref[k], (NSEG, LANE))) for k in range(K)]

        def p1(j, st):
            new = []
            for k in range(K):
                sr, si, pr, pi = st[k]
                a_r, a_i = A[k]
                nr = a_r * sr - a_i * si + _seg_rows(br_ref, k, j, n)
                ni = a_r * si + a_i * sr + _seg_rows(bi_ref, k, j, n)
                _seg_store(xr_ref, k, j, n, nr)
                _seg_store(xi_ref, k, j, n, ni)
                new.append((nr, ni, a_r * pr - a_i * pi, a_r * pi + a_i * pr))
            return tuple(new)

        st = lax.fori_loop(0, n, p1, tuple((zero, zero, zero + 1.0, zero) for _ in range(K)))
        C = [_seg_carries(st[k][0], st[k][1], st[k][2], st[k][3], False) for k in range(K)]

        def p2(j, st):
            new = []
            for k in range(K):
                pr, pi = st[k]
                a_r, a_i = A[k]
                pr, pi = a_r * pr - a_i * pi, a_r * pi + a_i * pr
                cr, ci = C[k]
                _seg_store(xr_ref, k, j, n, _seg_rows(xr_ref, k, j, n) + pr * cr - pi * ci)
                _seg_store(xi_ref, k, j, n, _seg_rows(xi_ref, k, j, n) + pr * ci + pi * cr)
                new.append((pr, pi))
            return tuple(new)

        lax.fori_loop(0, n, p2, tuple((zero + 1.0, zero) for _ in range(K)))

    blk = pl.BlockSpec((K, L, LANE), lambda g: (g, 0, 0))
    ablk = pl.BlockSpec((K, 1, LANE), lambda g: (g, 0, 0))
    return pl.pallas_call(
        body, grid=(nb // K,), name="s5_scan_fwd",
        in_specs=[blk, blk, ablk, ablk], out_specs=[blk, blk],
        out_shape=[jax.ShapeDtypeStruct((nb, L, LANE), F32)] * 2,
        compiler_params=pltpu.CompilerParams(vmem_limit_bytes=VMEM_LIMIT),
    )(bur, bui, ar, ai)


def s5_scan_bwd(dxr, dxi, xr, xi, ar, ai, L):
    n = L // NSEG
    nb = S5_N // LANE
    K = S5_K

    def body(dr_ref, di_ref, xr_ref, xi_ref, ar_ref, ai_ref, gr_ref, gi_ref, dar_ref, dai_ref):
        zero = jnp.zeros((NSEG, LANE), F32)
        rows = lax.broadcasted_iota(jnp.int32, (NSEG, LANE), 0)
        A = [(jnp.broadcast_to(ar_ref[k], (NSEG, LANE)), -jnp.broadcast_to(ai_ref[k], (NSEG, LANE))) for k in range(K)]

        def p1(jj, st):
            j = n - 1 - jj
            new = []
            for k in range(K):
                sr, si, pr, pi = st[k]
                a_r, a_i = A[k]
                nr = a_r * sr - a_i * si + _seg_rows(dr_ref, k, j, n)
                ni = a_r * si + a_i * sr + _seg_rows(di_ref, k, j, n)
                _seg_store(gr_ref, k, j, n, nr)
                _seg_store(gi_ref, k, j, n, ni)
                new.append((nr, ni, a_r * pr - a_i * pi, a_r * pi + a_i * pr))
            return tuple(new)

        st = lax.fori_loop(0, n, p1, tuple((zero, zero, zero + 1.0, zero) for _ in range(K)))
        C = [_seg_carries(st[k][0], st[k][1], st[k][2], st[k][3], True) for k in range(K)]
        xb = [(jnp.where(rows == 0, 0.0, pltpu.roll(_seg_rows(xr_ref, k, n - 1, n), 1, 0)),
               jnp.where(rows == 0, 0.0, pltpu.roll(_seg_rows(xi_ref, k, n - 1, n), 1, 0))) for k in range(K)]

        def p2(jj, st):
            j = n - 1 - jj
            jp = jnp.maximum(j - 1, 0)
            new = []
            for k in range(K):
                pr, pi, acr, aci = st[k]
                a_r, a_i = A[k]
                pr, pi = a_r * pr - a_i * pi, a_r * pi + a_i * pr
                cr, ci = C[k]
                g_r = _seg_rows(gr_ref, k, j, n) + pr * cr - pi * ci
                g_i = _seg_rows(gi_ref, k, j, n) + pr * ci + pi * cr
                _seg_store(gr_ref, k, j, n, g_r)
                _seg_store(gi_ref, k, j, n, g_i)
                xpr = jnp.where(j == 0, xb[k][0], _seg_rows(xr_ref, k, jp, n))
                xpi = jnp.where(j == 0, xb[k][1], _seg_rows(xi_ref, k, jp, n))
                new.append((pr, pi, acr + g_r * xpr + g_i * xpi, aci + g_i * xpr - g_r * xpi))
            return tuple(new)

        st = lax.fori_loop(0, n, p2, tuple((zero + 1.0, zero, zero, zero) for _ in range(K)))
        for k in range(K):
            dar_ref[k] = jnp.sum(st[k][2], axis=0, keepdims=True)
            dai_ref[k] = jnp.sum(st[k][3], axis=0, keepdims=True)

    blk = pl.BlockSpec((K, L, LANE), lambda g: (g, 0, 0))
    ablk = pl.BlockSpec((K, 1, LANE), lambda g: (g, 0, 0))
    return pl.pallas_call(
        body, grid=(nb // K,), name="s5_scan_bwd",
        in_specs=[blk, blk, blk, blk, ablk, ablk], out_specs=[blk, blk, ablk, ablk],
        out_shape=[jax.ShapeDtypeStruct((nb, L, LANE), F32)] * 2 + [jax.ShapeDtypeStruct((nb, 1, LANE), F32)] * 2,
        compiler_params=pltpu.CompilerParams(vmem_limit_bytes=VMEM_LIMIT),
    )(dxr, dxi, xr, xi, ar, ai)


def rg_scan_fwd(a, b, L):
    n = L // NSEG

    def body(a_ref, b_ref, h_ref):
        zero = jnp.zeros((NSEG, LANE), F32)

        def p1(j, st):
            h, p = st
            aj = _seg_rows(a_ref, None, j, n)
            h = aj * h + _seg_rows(b_ref, None, j, n)
            _seg_store(h_ref, None, j, n, h)
            return h, aj * p

        e, pe = lax.fori_loop(0, n, p1, (zero, zero + 1.0))
        c, _ = _seg_carries(e, None, pe, None, False)

        def p2(j, p):
            p = _seg_rows(a_ref, None, j, n) * p
            _seg_store(h_ref, None, j, n, _seg_rows(h_ref, None, j, n) + p * c)
            return p

        lax.fori_loop(0, n, p2, zero + 1.0)

    blk = pl.BlockSpec((L, LANE), lambda g: (0, g))
    return pl.pallas_call(
        body, grid=(BW // LANE,), name="rg_scan_fwd", in_specs=[blk, blk], out_specs=blk,
        out_shape=jax.ShapeDtypeStruct((L, BW), F32),
        compiler_params=pltpu.CompilerParams(vmem_limit_bytes=VMEM_LIMIT),
    )(a, b)


def rg_scan_bwd(a, h, dh, L):
    n = L // NSEG

    def body(a_ref, h_ref, dh_ref, da_ref, db_ref):
        zero = jnp.zeros((NSEG, LANE), F32)
        rows = lax.broadcasted_iota(jnp.int32, (NSEG, LANE), 0)
        a_edge = jnp.where(rows == NSEG - 1, 0.0, pltpu.roll(_seg_rows(a_ref, None, 0, n), NSEG - 1, 0))
        h_edge = jnp.where(rows == 0, 0.0, pltpu.roll(_seg_rows(h_ref, None, n - 1, n), 1, 0))

        def mult(j):
            return jnp.where(j == n - 1, a_edge, _seg_rows(a_ref, None, jnp.minimum(j + 1, n - 1), n))

        def p1(jj, st):
            j = n - 1 - jj
            g, p = st
            m = mult(j)
            g = m * g + _seg_rows(dh_ref, None, j, n)
            _seg_store(db_ref, None, j, n, g)
            return g, m * p

        e, pe = lax.fori_loop(0, n, p1, (zero, zero + 1.0))
        c, _ = _seg_carries(e, None, pe, None, True)

        def p2(jj, p):
            j = n - 1 - jj
            p = mult(j) * p
            g = _seg_rows(db_ref, None, j, n) + p * c
            _seg_store(db_ref, None, j, n, g)
            hp = jnp.where(j == 0, h_edge, _seg_rows(h_ref, None, jnp.maximum(j - 1, 0), n))
            _seg_store(da_ref, None, j, n, g * hp)
            return p

        lax.fori_loop(0, n, p2, zero + 1.0)

    blk = pl.BlockSpec((L, LANE), lambda g: (0, g))
    return pl.pallas_call(
        body, grid=(BW // LANE,), name="rg_scan_bwd", in_specs=[blk, blk, blk], out_specs=[blk, blk],
        out_shape=[jax.ShapeDtypeStruct((L, BW), F32)] * 2,
        compiler_params=pltpu.CompilerParams(vmem_limit_bytes=VMEM_LIMIT),
    )(a, h, dh)


def _hg_consts(C):
    t = lax.broadcasted_iota(jnp.int32, (C, C), 0)
    s = lax.broadcasted_iota(jnp.int32, (C, C), 1)
    tril = (s <= t).astype(F32)
    diag = (s == t).astype(F32)
    levels = []
    k = 1
    while (1 << k) <= C:
        m = 1 << (k - 1)
        same = (t >> k) == (s >> k)
        t_right = ((t >> (k - 1)) & 1) == 1
        s_left = ((s >> (k - 1)) & 1) == 0
        mask = jnp.logical_and(same, jnp.logical_and(t_right, s_left)).astype(F32)
        bnd = ((t >> k) << k) + (m - 1)
        levels.append((mask, (s <= bnd).astype(F32)))
        k += 1
    return tril, diag, levels


def hg_chunk(st, q, z, v, lb):
    C = q.shape[0]
    tril, diag, levels = _hg_consts(C)
    sig = jax.nn.sigmoid(z)
    lf = jnp.log(lb + (1.0 - lb) * sig)
    k = (1.0 - lb) * jax.nn.sigmoid(-z)
    qh = jax.nn.silu(q)
    b = mm_exact(tril, lf)
    blast = jnp.sum(lf, axis=0, keepdims=True)
    qe = qh * jnp.exp(b)
    kd = k * jnp.exp(blast - b)
    scaled = []
    for _, sel in levels:
        ref = mm_exact(sel, lf)
        scaled.append((qh * jnp.exp(jnp.minimum(b - ref, 0.0)), k * jnp.exp(jnp.minimum(ref - b, 0.0))))
    outs, news = [], []
    for h in range(HG_HEADS):
        sl = slice(h * HG_D, (h + 1) * HG_D)
        st_h = st[h * HG_D:(h + 1) * HG_D, :]
        sc = diag * mma_nt(qh[:, sl], k[:, sl])
        for (mask, _), (qt, kt) in zip(levels, scaled):
            sc = sc + mask * mma_nt(qt[:, sl], kt[:, sl])
        outs.append(mma_nt(qe[:, sl], st_h) + mma_nn(sc, v[:, sl]))
        news.append(st_h * jnp.exp(blast[:, sl]) + mma_tn(v[:, sl], kd[:, sl]))
    return jnp.concatenate(news, axis=0), jnp.concatenate(outs, axis=1)


def hg_fwd(qzv, lb, L):
    C = HG_CHUNK
    nc = L // C

    def body(q_ref, z_ref, v_ref, lb_ref, o_ref, sst_ref, st_ref):
        @pl.when(pl.program_id(0) == 0)
        def _():
            st_ref[...] = jnp.zeros_like(st_ref)

        st = st_ref[...]
        sst_ref[...] = st
        new, o = hg_chunk(st, q_ref[...], z_ref[...], v_ref[...], lb_ref[...])
        st_ref[...] = new
        o_ref[...] = o

    col = lambda cb: pl.BlockSpec((C, BW), functools.partial(lambda c, cb: (c, cb), cb=cb))
    return pl.pallas_call(
        body, grid=(nc,), name="hg_fwd",
        in_specs=[col(0), col(1), col(2), pl.BlockSpec((1, BW), lambda c: (0, 0))],
        out_specs=[pl.BlockSpec((C, BW), lambda c: (c, 0)), pl.BlockSpec((None, BW, HG_D), lambda c: (c, 0, 0))],
        out_shape=[jax.ShapeDtypeStruct((L, BW), F32), jax.ShapeDtypeStruct((nc, BW, HG_D), F32)],
        scratch_shapes=[pltpu.VMEM((BW, HG_D), F32)],
        compiler_params=pltpu.CompilerParams(vmem_limit_bytes=VMEM_LIMIT, dimension_semantics=("arbitrary",)),
    )(qzv, qzv, qzv, lb)


def hg_bwd(qzv, lb, sst, do, L):
    C = HG_CHUNK
    nc = L // C

    def body(q_ref, z_ref, v_ref, lb_ref, sst_ref, do_ref, dq_ref, dz_ref, dv_ref, dlb_ref, dst_ref):
        @pl.when(pl.program_id(0) == 0)
        def _():
            dst_ref[...] = jnp.zeros_like(dst_ref)
            dlb_ref[...] = jnp.zeros_like(dlb_ref)

        _, vjp = jax.vjp(hg_chunk, sst_ref[...], q_ref[...], z_ref[...], v_ref[...], lb_ref[...])
        dst, dq, dz, dv, dlb = vjp((dst_ref[...], do_ref[...]))
        dst_ref[...] = dst
        dq_ref[...] = dq
        dz_ref[...] = dz
        dv_ref[...] = dv
        dlb_ref[...] += dlb

    col = lambda cb: pl.BlockSpec((C, BW), functools.partial(lambda c, cb: (nc - 1 - c, cb), cb=cb))
    rev = pl.BlockSpec((C, BW), lambda c: (nc - 1 - c, 0))
    return pl.pallas_call(
        body, grid=(nc,), name="hg_bwd",
        in_specs=[col(0), col(1), col(2), pl.BlockSpec((1, BW), lambda c: (0, 0)),
                  pl.BlockSpec((None, BW, HG_D), lambda c: (nc - 1 - c, 0, 0)), rev],
        out_specs=[rev, rev, rev, pl.BlockSpec((1, BW), lambda c: (0, 0))],
        out_shape=[jax.ShapeDtypeStruct((L, BW), F32)] * 3 + [jax.ShapeDtypeStruct((1, BW), F32)],
        scratch_shapes=[pltpu.VMEM((BW, HG_D), F32)],
        compiler_params=pltpu.CompilerParams(vmem_limit_bytes=VMEM_LIMIT, dimension_semantics=("arbitrary",)),
    )(qzv, qzv, qzv, lb, sst, do)


def _shift_down(x, d, rows, L):
    if d == 0:
        return x
    wrapped = jnp.where((rows & (NSEG - 1)) == 0, 0.0, pltpu.roll(x, NSEG * d + 1, 0))
    return jnp.where(rows < NSEG * d, wrapped, pltpu.roll(x, NSEG * d, 0))


def _shift_up(x, d, rows, L):
    if d == 0:
        return x
    wrapped = jnp.where((rows & (NSEG - 1)) == NSEG - 1, 0.0, pltpu.roll(x, L - (NSEG * d + 1), 0))
    return jnp.where(rows >= L - NSEG * d, wrapped, pltpu.roll(x, L - NSEG * d, 0))


def conv_fwd(proj, w, b, L):
    def body(x_ref, w_ref, b_ref, o_ref):
        x = x_ref[...]
        rows = lax.broadcasted_iota(jnp.int32, x.shape, 0)
        acc = jnp.broadcast_to(b_ref[...], x.shape)
        for k in range(CONV_W):
            acc = acc + w_ref[pl.ds(k, 1), :] * _shift_down(x, CONV_W - 1 - k, rows, L)
        o_ref[...] = acc

    nl = BW // LANE
    return pl.pallas_call(
        body, grid=(nl,), name="conv_fwd",
        in_specs=[pl.BlockSpec((L, LANE), lambda g: (0, 5 * nl + g)), pl.BlockSpec((CONV_W, LANE), lambda g: (0, g)),
                  pl.BlockSpec((1, LANE), lambda g: (0, g))],
        out_specs=pl.BlockSpec((L, LANE), lambda g: (0, g)),
        out_shape=jax.ShapeDtypeStruct((L, BW), F32),
        compiler_params=pltpu.CompilerParams(vmem_limit_bytes=VMEM_LIMIT),
    )(proj, w, b)


def conv_bwd(proj, w, dxc, L):
    def body(x_ref, w_ref, d_ref, dx_ref, dw_ref, db_ref):
        x, d = x_ref[...], d_ref[...]
        rows = lax.broadcasted_iota(jnp.int32, x.shape, 0)
        acc = jnp.zeros_like(x)
        for k in range(CONV_W):
            acc = acc + w_ref[pl.ds(k, 1), :] * _shift_up(d, CONV_W - 1 - k, rows, L)
            dw_ref[pl.ds(k, 1), :] = jnp.sum(d * _shift_down(x, CONV_W - 1 - k, rows, L), axis=0, keepdims=True)
        dx_ref[...] = acc
        db_ref[...] = jnp.sum(d, axis=0, keepdims=True)

    nl = BW // LANE
    blk = pl.BlockSpec((L, LANE), lambda g: (0, g))
    return pl.pallas_call(
        body, grid=(nl,), name="conv_bwd",
        in_specs=[pl.BlockSpec((L, LANE), lambda g: (0, 5 * nl + g)), pl.BlockSpec((CONV_W, LANE), lambda g: (0, g)), blk],
        out_specs=[blk, pl.BlockSpec((CONV_W, LANE), lambda g: (0, g)), pl.BlockSpec((1, LANE), lambda g: (0, g))],
        out_shape=[jax.ShapeDtypeStruct((L, BW), F32), jax.ShapeDtypeStruct((CONV_W, BW), F32),
                   jax.ShapeDtypeStruct((1, BW), F32)],
        compiler_params=pltpu.CompilerParams(vmem_limit_bytes=VMEM_LIMIT),
    )(proj, w, dxc)


def loss_fwd_bwd(x, fw, target, L, tm):
    def fn(x, fw, t):
        err = jnp.square(_rms(x, fw) - t)
        return jnp.sum(0.5 * jnp.mean(err, axis=-1, keepdims=True), axis=0, keepdims=True)

    def body(x_ref, fw_ref, t_ref, l_ref, dx_ref, dfw_ref):
        i = pl.program_id(0)
        t = t_ref[...]
        val, vjp = jax.vjp(lambda x, fw: fn(x, fw, t), x_ref[...], fw_ref[...])
        dx, dfw = vjp(jnp.ones((1, 1), F32))
        dx_ref[...] = dx

        @pl.when(i == 0)
        def _():
            l_ref[...] = jnp.zeros_like(l_ref)
            dfw_ref[...] = jnp.zeros_like(dfw_ref)

        l_ref[...] += jnp.broadcast_to(val, l_ref.shape)
        dfw_ref[...] += dfw

    row = pl.BlockSpec((tm, D_MODEL), lambda i: (i, 0))
    vec = pl.BlockSpec((1, D_MODEL), lambda i: (0, 0))
    return pl.pallas_call(
        body, grid=(L // tm,), name="loss_fwd_bwd", in_specs=[row, vec, row],
        out_specs=[pl.BlockSpec((1, LANE), lambda i: (0, 0)), row, vec],
        out_shape=[jax.ShapeDtypeStruct((1, LANE), F32), jax.ShapeDtypeStruct((L, D_MODEL), F32),
                   jax.ShapeDtypeStruct((1, D_MODEL), F32)],
        compiler_params=pltpu.CompilerParams(vmem_limit_bytes=VMEM_LIMIT, dimension_semantics=("arbitrary",)),
    )(x, fw, target)


def adamw(w, g, m, v):
    rows, cols = w.shape
    tr = _row_tile(rows, cols, budget=1024 * 1024)
    c1 = 1.0 - ADAM_B1 ** ADAM_STEP
    c2 = 1.0 - ADAM_B2 ** ADAM_STEP

    def body(w_ref, g_ref, m_ref, v_ref, d_ref, nm_ref, nv_ref):
        g = g_ref[...]
        nm = ADAM_B1 * m_ref[...] + (1.0 - ADAM_B1) * g
        nv = ADAM_B2 * v_ref[...] + (1.0 - ADAM_B2) * jnp.square(g)
        d_ref[...] = -ADAM_LR * ((nm / c1) / (jnp.sqrt(nv / c2) + ADAM_EPS) + ADAM_WD * w_ref[...])
        nm_ref[...] = nm
        nv_ref[...] = nv

    blk = pl.BlockSpec((tr, cols), lambda i: (i, 0))
    return pl.pallas_call(
        body, grid=(rows // tr,), name="adamw", in_specs=[blk] * 4, out_specs=[blk] * 3,
        out_shape=[jax.ShapeDtypeStruct((rows, cols), F32)] * 3,
    )(w, g, m, v)


def s5_prep(lam_re, lam_im, log_dt, b_re, b_im, c_re, c_im):
    lr = jnp.minimum(lam_re, -1e-4)
    li = lam_im
    dt = jnp.exp(log_dt)[:, None]
    mag = jnp.exp(lr * dt)
    ar = mag * jnp.cos(li * dt)
    ai = mag * jnp.sin(li * dt)
    den = lr * lr + li * li
    fr = ((ar - 1.0) * lr + ai * li) / den
    fi = (ai * lr - (ar - 1.0) * li) / den
    bbr = fr[..., None] * b_re - fi[..., None] * b_im
    bbi = fr[..., None] * b_im + fi[..., None] * b_re
    emb_b = lambda bb: _block_diag(bb.transpose(0, 2, 1).reshape(BW, S5_STATE), S5_GROUPS)
    emb_c = lambda cc: _block_diag(cc.transpose(0, 2, 1).reshape(S5_N, S5_GROUP), S5_GROUPS)
    bmat = jnp.concatenate([emb_b(bbr), emb_b(bbi)], axis=1)
    cmat = jnp.concatenate([emb_c(c_re), -emb_c(c_im)], axis=0)
    nb = S5_N // LANE
    return ar.reshape(nb, 1, LANE), ai.reshape(nb, 1, LANE), bmat, cmat


def _block_diag(stacked, groups):
    rows, c = stacked.shape
    r = rows // groups
    row_g = jnp.arange(rows)[:, None] // r
    col_g = jnp.arange(groups * c)[None, :] // c
    return jnp.where(row_g == col_g, jnp.tile(stacked, (1, groups)), 0.0)


def rg_prep(w):
    return _block_diag(w.reshape(BW, RG_BLOCK), RG_BLOCKS)


def hg_prep(logits):
    p = jax.nn.softmax(logits, axis=0)
    return jnp.cumsum(p, axis=0) - p[0]


def _head_mean_matrix():
    r = jnp.arange(BW) // HG_D
    return (r[:, None] == r[None, :]).astype(F32) / HG_D


def _to_segment_order(a):
    L = a.shape[0]
    return a.reshape(NSEG, L // NSEG, -1).transpose(1, 0, 2).reshape(a.shape)


def _to_time_order(a):
    L = a.shape[0]
    return a.reshape(L // NSEG, NSEG, -1).transpose(1, 0, 2).reshape(a.shape)


def _const(*idx):
    return lambda s, i: idx


def _rows(cb=0):
    return lambda s, i: (i, cb)


def _sum_parts(name, first, parts, shape):
    return add_n(name, [(first, ())] + [(parts, (s,)) for s in range(NSH)], shape)


def _ffn_weight_specs(l, j):
    F = D_FF // NSH
    one = pl.Buffered(1)
    return [pl.BlockSpec((None, NSH, D_MODEL, F), lambda i: (j, 0, 0, 0), pipeline_mode=one),
            pl.BlockSpec((None, NSH, D_MODEL, F), lambda i: (j, 0, 0, 0), pipeline_mode=one),
            pl.BlockSpec((None, NSH, F, D_MODEL), lambda i: (j, 0, 0, 0), pipeline_mode=one)]


def ffn_fwd(name, x, W, l, j, k, L, tm):
    D, F = D_MODEL, D_FF // NSH

    def body(x_ref, nw_ref, wg_ref, wu_ref, wd_ref, y_ref, g_ref, u_ref):
        x = x_ref[...]
        h = _rms(x, nw_ref[...]).astype(MMT)
        y = x
        for s in range(NSH):
            g = _dg(h, wg_ref[s], 1, 0)
            u = _dg(h, wu_ref[s], 1, 0)
            g_ref[s] = g.astype(g_ref.dtype)
            u_ref[s] = u.astype(u_ref.dtype)
            y = y + 0.5 * _dg((jax.nn.silu(g) * u).astype(MMT), wd_ref[s], 1, 0)
        y_ref[...] = y

    row = pl.BlockSpec((tm, D), lambda i: (i, 0))
    act = pl.BlockSpec((NSH, tm, F), lambda i: (0, i, 0))
    return pl.pallas_call(
        body, grid=(L // tm,), name=name,
        in_specs=[row, pl.BlockSpec((None, None, 1, D), lambda i: (l, k, 0, 0))] + _ffn_weight_specs(l, j),
        out_specs=[row, act, act],
        out_shape=[jax.ShapeDtypeStruct((L, D), F32), jax.ShapeDtypeStruct((NSH, L, F), MMT),
                   jax.ShapeDtypeStruct((NSH, L, F), MMT)],
        compiler_params=pltpu.CompilerParams(vmem_limit_bytes=VMEM_LIMIT, dimension_semantics=("arbitrary",)),
    )(x, W["nw"], W["L"][l]["wg"], W["L"][l]["wu"], W["L"][l]["wd"])


def ffn_bwd(name, x, g, u, dy, W, bufs, l, j, k, L, tm):
    D, F = D_MODEL, D_FF // NSH
    tm = min(TM_WGRAD, L)

    def body(x_ref, nw_ref, dy_ref, g_ref, u_ref, wg_ref, wu_ref, wd_ref, *rest):
        part_ref, dnw_ref, dwg_ref, dwu_ref, dwd_ref = rest[-5:]
        s, i = pl.program_id(0), pl.program_id(1)
        x, nw = x_ref[...], nw_ref[...]
        r = lax.rsqrt(jnp.mean(x * x, axis=-1, keepdims=True) + EPS)
        xhat = x * r
        h = (xhat * nw).astype(MMT)
        half_dy = (0.5 * dy_ref[...]).astype(MMT)
        gs, us = g_ref[...].astype(F32), u_ref[...].astype(F32)
        sig = jax.nn.sigmoid(gs)
        act = gs * sig
        da = _dg(half_dy, wd_ref[...], 1, 1)
        du = (da * act).astype(MMT)
        dg = (da * us * (sig * (1.0 + gs * (1.0 - sig)))).astype(MMT)
        dh = _dg(dg, wg_ref[...], 1, 1) + _dg(du, wu_ref[...], 1, 1)
        dxh = dh * nw
        part_ref[...] = r * (dxh - xhat * jnp.mean(dxh * xhat, axis=-1, keepdims=True))
        grads = (_dg(h, dg, 0, 0), _dg(h, du, 0, 0), _dg((act * us).astype(MMT), half_dy, 0, 0))
        dnw = jnp.sum(dh * xhat, axis=0, keepdims=True)
        first = jnp.logical_and(s == 0, i == 0)
        for ref, val, start in zip((dwg_ref, dwu_ref, dwd_ref, dnw_ref), grads + (dnw,), (i == 0, i == 0, i == 0, first)):
            @pl.when(start)
            def _(ref=ref, val=val):
                ref[...] = val

            @pl.when(jnp.logical_not(start))
            def _(ref=ref, val=val):
                ref[...] += val

    keys = ("ffn_gate", "ffn_up", "ffn_down")
    given = [bufs[key] for key in keys if bufs.get(key) is not None]
    row = pl.BlockSpec((tm, D), lambda s, i: (i, 0))
    act = pl.BlockSpec((None, tm, F), lambda s, i: (s, i, 0))
    wsp = lambda r, c: pl.BlockSpec((None, None, r, c), lambda s, i: (j, s, 0, 0))
    stk = lambda r, c: pl.BlockSpec((None, None, r, c), lambda s, i: (2 * l + j, s, 0, 0))
    part, dnw, bufs["ffn_gate"], bufs["ffn_up"], bufs["ffn_down"] = pl.pallas_call(
        body, grid=(NSH, L // tm), name=name,
        in_specs=[row, pl.BlockSpec((None, None, 1, D), lambda s, i: (l, k, 0, 0)), row, act, act,
                  wsp(D, F), wsp(D, F), wsp(F, D)] + [pl.BlockSpec(memory_space=pl.ANY)] * len(given),
        out_specs=[pl.BlockSpec((None, tm, D), lambda s, i: (s, i, 0)), pl.BlockSpec((1, D), lambda s, i: (0, 0)),
                   stk(D, F), stk(D, F), stk(F, D)],
        out_shape=[jax.ShapeDtypeStruct((NSH, L, D), F32), jax.ShapeDtypeStruct((1, D), F32)]
        + [jax.ShapeDtypeStruct((2 * DEPTH, NSH, D, F), F32)] * 2 + [jax.ShapeDtypeStruct((2 * DEPTH, NSH, F, D), F32)],
        input_output_aliases={8 + n: 2 + n for n in range(len(given))},
        compiler_params=pltpu.CompilerParams(vmem_limit_bytes=VMEM_LIMIT, dimension_semantics=("arbitrary", "arbitrary")),
    )(x, W["nw"], dy, g, u, W["L"][l]["wg"], W["L"][l]["wu"], W["L"][l]["wd"], *given)
    return _sum_parts(name + "_dx", dy, part, (L, D)), dnw


def layer_fwd(l, x0, W, P, L, tm):
    D = D_MODEL
    tmm = tm
    tm = min(TM_FWD, L)
    n_i = L // tm
    x1, g0, u0 = ffn_fwd(f"ffn_fwd_{l}0", x0, W, l, 0, 0, L, tm)
    proj = tile_fwd(
        lambda x, nw, win, s: pre_core(x, nw, win), f"pre_fwd_{l}", n_i, NSH,
        [(x1, (tm, D), _rows()), (W["nw"], (None, None, 1, D), _const(l, 1, 0, 0)),
         (W["L"][l]["win"], (None, D, IN_TOTAL // NSH), lambda s, i: (s, 0, 0))],
        [((L, IN_TOTAL), F32, (tm, IN_TOTAL // NSH), lambda s, i: (i, s), False)], s_outer=True)[0]
    nb = S5_N // LANE
    blk3 = lambda s, i: (0, i, 0)
    bur, bui = tile_fwd(
        lambda u, bmat, s: s5_pre_core(u, bmat), f"s5pre_fwd_{l}", n_i, 1,
        [(proj, (tm, BW), _rows(0)), (P["bmat"], (None, BW, 2 * S5_N), _const(l, 0, 0))],
        [((nb, L, LANE), F32, (nb, tm, LANE), blk3, False)] * 2)
    xr, xi = s5_scan_fwd(bur, bui, P["ar"][l], P["ai"][l], L)
    qzv = _to_time_order(proj[:, BW:4 * BW])
    o_t, sst = hg_fwd(qzv, P["lb"][l], L)
    o = _to_segment_order(o_t)
    xc = conv_fwd(proj, W["convw"][l], P["convb"][l], L)
    vec = (None, 1, BW)
    a, b = tile_fwd(
        lambda xc, wa, ba, wx, bx, lam, s: gates_core(xc, wa, ba, wx, bx, lam), f"gates_fwd_{l}", n_i, 1,
        [(xc, (tm, BW), _rows()), (P["wa"], (None, BW, BW), _const(l, 0, 0)), (P["ba"], vec, _const(l, 0, 0)),
         (P["wx"], (None, BW, BW), _const(l, 0, 0)), (P["bx"], vec, _const(l, 0, 0)), (P["lam"], vec, _const(l, 0, 0))],
        [((L, BW), F32, (tm, BW), _rows(), False)] * 2)
    hs = rg_scan_fwd(a, b, L)
    ya, yb, yc = tile_fwd(
        lambda *a: mid_core(*a[:-1]), f"mid_fwd_{l}", L // tmm, 1,
        [(xr, (nb, tmm, LANE), blk3), (xi, (nb, tmm, LANE), blk3), (proj, (tmm, BW), _rows(0)), (o, (tmm, BW), _rows()),
         (proj, (tmm, BW), _rows(4)), (hs, (tmm, BW), _rows()), (proj, (tmm, BW), _rows(6)),
         (P["hmat"], (BW, BW), _const(0, 0)), (P["cmat"], (None, 2 * S5_N, BW), _const(l, 0, 0)), (P["d"], vec, _const(l, 0, 0)),
         (W["L"][l]["gluw"], (BW, BW), _const(0, 0)), (P["glub"], vec, _const(l, 0, 0)), (P["hgw"], vec, _const(l, 0, 0))],
        [((L, BW), F32, (tmm, BW), _rows(), False)] * 3)
    x2 = tile_fwd(
        lambda x, *rest: (x + merge_core(*rest[:-1])[0],), f"merge_fwd_{l}", n_i, 1,
        [(x1, (tm, D), _rows()), (ya, (tm, BW), _rows()), (yb, (tm, BW), _rows()), (yc, (tm, BW), _rows())]
        + [(proj, (tm, BW), _rows(7 + k)) for k in range(6)]
        + [(W["L"][l]["pfull"], (3, BW, D), _const(0, 0, 0)), (W["L"][l]["woutfull"], (D, D), _const(0, 0))],
        [((L, D), F32, (tm, D), _rows(), False)])[0]
    x3, g1, u1 = ffn_fwd(f"ffn_fwd_{l}1", x2, W, l, 1, 2, L, tm)
    saved = dict(x0=x0, x1=x1, x2=x2, proj=proj, xr=xr, xi=xi, o=o, sst=sst, xc=xc, a=a, hs=hs, ya=ya, yb=yb, yc=yc,
                 qzv=qzv, g0=g0, u0=u0, g1=g1, u1=u1)
    return x3, saved


def layer_bwd(l, dx3, sv, W, P, bufs, L, tm, ready=lambda l, group: None):
    D = D_MODEL
    n_i = L // tm
    nb = S5_N // LANE
    dq = D // NSH
    vec = (None, 1, BW)
    vout = ((1, BW), (1, BW), _const(0, 0), "acc_all")
    blk3 = lambda s, i: (0, i, 0)
    small = {}
    proj = sv["proj"]

    dx2, dnw2 = ffn_bwd(f"ffn_bwd_{l}1", sv["x2"], sv["g1"], sv["u1"], dx3, W, bufs, l, 1, 2, L, tm)
    ready(l, "ffn1")

    rw256 = ((L, BW), (tm, BW), _rows(), "write")
    res = tile_bwd(
        merge_core, f"merge_bwd_{l}", n_i, 1,
        [(sv["ya"], (tm, BW), _rows(), "r"), (sv["yb"], (tm, BW), _rows(), "r"), (sv["yc"], (tm, BW), _rows(), "r")]
        + [(proj, (tm, BW), _rows(7 + k), "r") for k in range(6)]
        + [(W["L"][l]["pfull"], (3, BW, D), _const(0, 0, 0), "w"), (W["L"][l]["woutfull"], (D, D), _const(0, 0), "w")],
        [(dx2, (tm, D), _rows())],
        [rw256] * 9
        + [((DEPTH, 3, BW, D), (None, 3, BW, D), _const(l, 0, 0, 0), "acc_all", bufs.get("branch_proj")),
           ((DEPTH, D, D), (None, D, D), _const(l, 0, 0), "acc_all", bufs.get("w_out"))])
    dya, dyb, dyc = res[:3]
    dgm = res[3:9]
    bufs["branch_proj"], bufs["w_out"] = res[9:]
    ready(l, "merge")

    tmm = tm
    rw = ((L, BW), (tmm, BW), _rows(), "write")
    xw = ((nb, L, LANE), (nb, tmm, LANE), blk3, "write")
    res = tile_bwd(
        mid_core, f"mid_bwd_{l}", L // tmm, 1,
        [(sv["xr"], (nb, tmm, LANE), blk3, "r"), (sv["xi"], (nb, tmm, LANE), blk3, "r"), (proj, (tmm, BW), _rows(0), "r"),
         (sv["o"], (tmm, BW), _rows(), "r"), (proj, (tmm, BW), _rows(4), "r"), (sv["hs"], (tmm, BW), _rows(), "r"),
         (proj, (tmm, BW), _rows(6), "r"), (P["hmat"], (BW, BW), _const(0, 0), "c"),
         (P["cmat"], (None, 2 * S5_N, BW), _const(l, 0, 0), "w"), (P["d"], vec, _const(l, 0, 0), "p"),
         (W["L"][l]["gluw"], (BW, BW), _const(0, 0), "w"), (P["glub"], vec, _const(l, 0, 0), "p"),
         (P["hgw"], vec, _const(l, 0, 0), "p")],
        [(dya, (tmm, BW), _rows()), (dyb, (tmm, BW), _rows()), (dyc, (tmm, BW), _rows())],
        [xw, xw, rw, rw, rw, rw, rw,
         ((DEPTH, 2 * S5_N, BW), (None, 2 * S5_N, BW), _const(l, 0, 0), "acc_all", bufs.get("cmat")), vout,
         ((DEPTH, BW, BW), (None, BW, BW), _const(l, 0, 0), "acc_all", bufs.get("s5_glu_w")), vout, vout])
    dxr, dxi, du_skip, do, dg_b, dhs, dgate_c, bufs["cmat"], dd, bufs["s5_glu_w"], dglub, dhgw = res
    small["s5_d"], small["s5_glu_b"], small["hg_norm_w"] = dd[0], dglub[0], dhgw[0]
    ready(l, "mid")

    da, db = rg_scan_bwd(sv["a"], sv["hs"], dhs, L)
    wmat = lambda key: ((DEPTH, BW, BW), (None, BW, BW), _const(l, 0, 0), "acc_all", bufs.get(key))
    res = tile_bwd(
        gates_core, f"gates_bwd_{l}", n_i, 1,
        [(sv["xc"], (tm, BW), _rows(), "r"), (P["wa"], (None, BW, BW), _const(l, 0, 0), "w"), (P["ba"], vec, _const(l, 0, 0), "p"),
         (P["wx"], (None, BW, BW), _const(l, 0, 0), "w"), (P["bx"], vec, _const(l, 0, 0), "p"), (P["lam"], vec, _const(l, 0, 0), "p")],
        [(da, (tm, BW), _rows()), (db, (tm, BW), _rows())],
        [((L, BW), (tm, BW), _rows(), "write"), wmat("wa"), vout, wmat("wx"), vout, vout])
    dxc, bufs["wa"], dba, bufs["wx"], dbx, dlam = res
    small["rg_ba"], small["rg_bx"], small["rg_lambda"] = dba[0], dbx[0], dlam[0]
    dx_c, dconvw, dconvb = conv_bwd(proj, W["convw"][l], dxc, L)
    small["rg_conv_w"], small["rg_conv_b"] = dconvw, dconvb[0]

    dq_b, dz_b, dv_b, dlb = hg_bwd(sv["qzv"], P["lb"][l], sv["sst"], _to_time_order(do), L)
    dq_b, dz_b, dv_b = [_to_segment_order(a) for a in (dq_b, dz_b, dv_b)]

    gr, gi, dar, dai = s5_scan_bwd(dxr, dxi, sv["xr"], sv["xi"], P["ar"][l], P["ai"][l], L)
    du_pre, bufs["bmat"] = tile_bwd(
        s5_pre_core, f"s5pre_bwd_{l}", n_i, 1,
        [(proj, (tm, BW), _rows(0), "r"), (P["bmat"], (None, BW, 2 * S5_N), _const(l, 0, 0), "w")],
        [(gr, (nb, tm, LANE), blk3), (gi, (nb, tm, LANE), blk3)],
        [((L, BW), (tm, BW), _rows(), "write"),
         ((DEPTH, BW, 2 * S5_N), (None, BW, 2 * S5_N), _const(l, 0, 0), "acc_all", bufs.get("bmat"))])
    du_a = add_n(f"du_a_{l}", [(du_skip, ()), (du_pre, ())], (L, BW))
    prep_ct = dict(dar=dar, dai=dai, dlb=dlb)

    pieces = [du_a, dq_b, dz_b, dv_b, dg_b, dx_c, dgate_c, *dgm]
    per_piece, per_shard = BW // LANE, IN_TOTAL // NSH // LANE
    part, dnw1 = None, []
    tmw = min(TM_WGRAD, L)
    for s in range(NSH):
        groups = [(pieces[g // per_piece], (tmw, LANE), _rows(g % per_piece))
                  for g in range(s * per_shard, (s + 1) * per_shard)]
        part, dnw_s, bufs["w_in"] = tile_bwd(
            pre_core, f"pre_bwd_{l}{s}", L // tmw, 1,
            [(sv["x1"], (tmw, D), _rows(), "r"), (W["nw"], (None, None, 1, D), _const(l, 1, 0, 0), "p"),
             (W["L"][l]["win"], (None, D, IN_TOTAL // NSH), _const(s, 0, 0), "w")],
            [groups],
            [((NSH, L, D), (None, tmw, D), functools.partial(lambda _s, i, s: (s, i, 0), s=s), "write", part),
             ((1, D), (1, D), _const(0, 0), "acc_all"),
             ((DEPTH, NSH, D, IN_TOTAL // NSH), (None, None, D, IN_TOTAL // NSH), _const(l, s, 0, 0), "acc_all",
              bufs.get("w_in"))])
        dnw1.append(dnw_s)
    dnw1 = (dnw1[0] + dnw1[1]) + (dnw1[2] + dnw1[3])
    dx1 = _sum_parts(f"pre_bwd_{l}_dx", dx2, part, (L, D))
    ready(l, "pre")

    dx0, dnw0 = ffn_bwd(f"ffn_bwd_{l}0", sv["x0"], sv["g0"], sv["u0"], dx1, W, bufs, l, 0, 0, L, tm)
    ready(l, "ffn0")
    small["norm_w"] = jnp.concatenate([dnw0, dnw1, dnw2], axis=0)
    return dx0, small, prep_ct


SMALL_RAW = ("s5_lambda_re", "s5_lambda_im", "s5_log_dt", "s5_b_re", "s5_b_im", "s5_c_re", "s5_c_im", "s5_d", "s5_glu_b",
             "hg_lb_logits", "hg_norm_w", "rg_conv_b", "rg_wa", "rg_ba", "rg_wx", "rg_bx", "rg_lambda", "final_norm_w")
DEPTH = 2


def local_step(x, target, W, raw, layer_weights=None, layer_grads=None):
    L = x.shape[0]
    tm = min(256, L)
    col = lambda v: v.reshape(DEPTH, 1, BW)
    (ar, ai, bmat, cmat), s5_vjp = jax.vjp(jax.vmap(s5_prep), *[raw[k] for k in SMALL_RAW[:7]])
    (wa, wx), rg_vjp = jax.vjp(lambda a, b: (jax.vmap(rg_prep)(a), jax.vmap(rg_prep)(b)), raw["rg_wa"], raw["rg_wx"])
    lb, hg_vjp = jax.vjp(hg_prep, raw["hg_lb_logits"])
    P = dict(
        ar=[ar[l] for l in range(DEPTH)], ai=[ai[l] for l in range(DEPTH)],
        bmat=bmat.astype(MMT), cmat=cmat.astype(MMT), wa=wa.astype(MMT), wx=wx.astype(MMT),
        lb=[lb[l].reshape(1, BW) for l in range(DEPTH)], convb=[raw["rg_conv_b"][l].reshape(1, BW) for l in range(DEPTH)],
        ba=col(raw["rg_ba"]), bx=col(raw["rg_bx"]), lam=col(raw["rg_lambda"]), d=col(raw["s5_d"]),
        glub=col(raw["s5_glu_b"]), hgw=col(raw["hg_norm_w"]), hmat=_head_mean_matrix())

    saved = []
    h = _to_segment_order(x)
    for l in range(DEPTH):
        if layer_weights is not None:
            W["L"][l], h = layer_weights(l, h)
        h, sv = layer_fwd(l, h, W, P, L, tm)
        saved.append(sv)
    loss, dh, dfw = loss_fwd_bwd(h, raw["final_norm_w"].reshape(1, D_MODEL), _to_segment_order(target), L, tm)

    big, per_layer, prep_cts = {}, [None] * DEPTH, [None] * DEPTH
    ready = (lambda l, group: None) if layer_grads is None else (lambda l, group: layer_grads(l, group, big))
    for l in reversed(range(DEPTH)):
        dh, sm, pc = layer_bwd(l, dh, saved[l], W, P, big, L, tm, ready)
        per_layer[l], prep_cts[l] = sm, pc
    dh = _to_time_order(dh)

    small = {k: jnp.stack([per_layer[l][k] for l in range(DEPTH)]) for k in per_layer[0]}
    both = lambda k: jnp.stack([prep_cts[l][k] for l in range(DEPTH)])
    s5_g = s5_vjp((both("dar"), both("dai"), big.pop("bmat"), big.pop("cmat")))
    small.update(zip(SMALL_RAW[:7], s5_g))
    small["rg_wa"], small["rg_wx"] = rg_vjp((big.pop("wa"), big.pop("wx")))
    (small["hg_lb_logits"],) = hg_vjp(jnp.concatenate([prep_cts[l]["dlb"] for l in range(DEPTH)], axis=0))
    small["final_norm_w"] = dfw[0]
    return loss, dh, big, small


ANY = pl.BlockSpec(memory_space=pl.ANY)


def _place():
    x, y, c = lax.axis_index("x"), lax.axis_index("y"), lax.axis_index("c")
    chips = [(1 - x, y), (x, 1 - y), (1 - x, 1 - y)]
    return x, y, c, chips


def _remote(src, dst, send, recv, k, to):
    return pltpu.make_async_remote_copy(src_ref=src, dst_ref=dst, send_sem=send.at[k], recv_sem=recv.at[k],
                                        device_id=to, device_id_type=MESH)


def _comm_call(body, name, ins, out_shapes, n_sem, n_loc):
    return pl.pallas_call(
        body, name=name, in_specs=[ANY] * len(ins), out_specs=[ANY] * len(out_shapes), out_shape=out_shapes,
        scratch_shapes=[pltpu.SemaphoreType.DMA((n_sem,)), pltpu.SemaphoreType.DMA((n_sem,)),
                        pltpu.SemaphoreType.DMA((max(n_loc, 1),))],
    )(*ins)


def gather_shards(name, shards):
    n = len(shards)
    per = 8

    def body(*refs):
        ins, outs = refs[:n], refs[n:2 * n]
        send, recv, _ = refs[2 * n:]
        x, y, c, chips = _place()
        me = 2 * x + y
        sib = (x, y, 1 - c)
        sends = []
        for w in range(n):
            for j, (cx, cy) in enumerate(chips):
                cp = _remote(ins[w].at[c], outs[w].at[c, me], send, recv, per * w + j, (cx, cy, c))
                cp.start()
                sends.append(cp)
        for w in range(n):
            for l in range(2):
                cp = _remote(ins[w].at[l], outs[w].at[l, me], send, recv, per * w + 6 + l, sib)
                cp.start()
                sends.append(cp)
        for w in range(n):
            for j, (cx, cy) in enumerate(chips):
                theirs = outs[w].at[c, 2 * cx + cy]
                _remote(ins[w].at[c], theirs, send, recv, per * w + j, (cx, cy, c)).wait_recv()
                cp = _remote(theirs, theirs, send, recv, per * w + 3 + j, sib)
                cp.start()
                sends.append(cp)
        for w in range(n):
            for j, (cx, cy) in enumerate(chips):
                dst = outs[w].at[1 - c, 2 * cx + cy]
                _remote(dst, dst, send, recv, per * w + 3 + j, sib).wait_recv()
            for l in range(2):
                dst = outs[w].at[l, me]
                _remote(dst, dst, send, recv, per * w + 6 + l, sib).wait_recv()
        for cp in sends:
            cp.wait_send()

    shapes = [jax.ShapeDtypeStruct((2, NSH) + s.shape[1:], s.dtype) for s in shards]
    return _comm_call(body, name, shards, shapes, per * n, 0)


def exchange_halves(name, grads, ranges):
    n = len(grads)

    def body(*refs):
        ins, outs = refs[:n], refs[n:2 * n]
        send, recv, _ = refs[2 * n:]
        x, y, c, _chips = _place()
        cps = []
        for w in range(n):
            h = grads[w].shape[2] // 2
            p0, np_ = ranges[w]
            cp = _remote(ins[w].at[pl.ds(p0, np_), :, pl.ds((1 - c) * h, h)], outs[w], send, recv, w, (x, y, 1 - c))
            cp.start()
            cps.append(cp)
        for cp in cps:
            cp.wait()

    shapes = [jax.ShapeDtypeStruct((r[1], NSH, g.shape[2] // 2, g.shape[3]), g.dtype) for g, r in zip(grads, ranges)]
    return _comm_call(body, name, grads, shapes, n, 0)


def scatter_to_chips(name, halves):
    n = len(halves)

    def body(*refs):
        ins, outs = refs[:n], refs[n:2 * n]
        send, recv, _ = refs[2 * n:]
        x, y, c, chips = _place()
        cps = []
        for w in range(n):
            for j, (cx, cy) in enumerate(chips):
                cp = _remote(ins[w].at[:, 2 * cx + cy], outs[w].at[j], send, recv, 3 * w + j, (cx, cy, c))
                cp.start()
                cps.append(cp)
        for cp in cps:
            cp.wait()

    shapes = [jax.ShapeDtypeStruct((3, h.shape[0]) + h.shape[2:], h.dtype) for h in halves]
    return _comm_call(body, name, halves, shapes, 3 * n, 0)


def share_halves(name, pieces):
    n = len(pieces)

    def body(*refs):
        ins, outs = refs[:n], refs[n:2 * n]
        send, recv, _ = refs[2 * n:]
        x, y, c, _chips = _place()
        cps = []
        for w in range(n):
            cp = _remote(ins[w], outs[w], send, recv, w, (x, y, 1 - c))
            cp.start()
            cps.append(cp)
        for cp in cps:
            cp.wait()

    return _comm_call(body, name, pieces, [jax.ShapeDtypeStruct(p.shape, p.dtype) for p in pieces], n, 0)


def add_own_half(name, g, ra, c, wire, b0):
    nblk, h, cols = ra.shape
    tr = _row_tile(h, cols, mult=16)
    nt = h // tr

    def body(c_ref, g_ref, r_ref, o_ref):
        o_ref[...] = (g_ref[...] + r_ref[...]).astype(o_ref.dtype)

    blk = (None, tr, cols)
    return pl.pallas_call(
        body, name=name,
        grid_spec=pltpu.PrefetchScalarGridSpec(
            num_scalar_prefetch=1, grid=(nblk, nt),
            in_specs=[pl.BlockSpec(blk, lambda s, i, c_ref: (b0 + s, c_ref[0] * nt + i, 0)), pl.BlockSpec(blk, lambda s, i, c_ref: (s, i, 0))],
            out_specs=pl.BlockSpec(blk, lambda s, i, c_ref: (s, i, 0))),
        out_shape=jax.ShapeDtypeStruct(ra.shape, wire),
    )(c.reshape(1), g, ra)


def add_chips(name, hb, rb, me):
    npc, _, h, cols = hb.shape
    tr = _row_tile(h, cols, mult=16)

    def body(me_ref, h_ref, r0, r1, r2, o_ref):
        f = lambda r: r[...].astype(F32)
        o_ref[...] = ((f(h_ref) + f(r0)) + f(r1)) + f(r2)

    rspec = lambda j: pl.BlockSpec((None, None, tr, cols), functools.partial(lambda p, i, me_ref, j: (j, p, i, 0), j=j))
    return pl.pallas_call(
        body, name=name,
        grid_spec=pltpu.PrefetchScalarGridSpec(
            num_scalar_prefetch=1, grid=(npc, h // tr),
            in_specs=[pl.BlockSpec((None, None, tr, cols), lambda p, i, me_ref: (p, me_ref[0], i, 0)), rspec(0), rspec(1), rspec(2)],
            out_specs=pl.BlockSpec((None, tr, cols), lambda p, i, me_ref: (p, i, 0))),
        out_shape=jax.ShapeDtypeStruct((npc, h, cols), F32),
    )(me.reshape(1), hb, rb, rb, rb)


def adamw_halves(name, w, m, v, own, other, c):
    npc, rows, cols = w.shape
    h = rows // 2
    tr = _row_tile(h, cols, budget=1024 * 1024)
    nt = h // tr
    c1 = 1.0 - ADAM_B1 ** ADAM_STEP
    c2 = 1.0 - ADAM_B2 ** ADAM_STEP

    def body(c_ref, w_ref, m_ref, v_ref, own_ref, oth_ref, g_ref, d_ref, nm_ref, nv_ref):
        g = jnp.where(pl.program_id(1) == c_ref[0], own_ref[...], oth_ref[...])
        nm = ADAM_B1 * m_ref[...] + (1.0 - ADAM_B1) * g
        nv = ADAM_B2 * v_ref[...] + (1.0 - ADAM_B2) * jnp.square(g)
        g_ref[...] = g
        d_ref[...] = -ADAM_LR * ((nm / c1) / (jnp.sqrt(nv / c2) + ADAM_EPS) + ADAM_WD * w_ref[...])
        nm_ref[...] = nm
        nv_ref[...] = nv

    full = pl.BlockSpec((None, tr, cols), lambda p, hh, i, c_ref: (p, hh * nt + i, 0))
    half = pl.BlockSpec((None, tr, cols), lambda p, hh, i, c_ref: (p, i, 0))
    return pl.pallas_call(
        body, name=name,
        grid_spec=pltpu.PrefetchScalarGridSpec(
            num_scalar_prefetch=1, grid=(npc, 2, nt),
            in_specs=[full, full, full, half, half], out_specs=[full] * 4),
        out_shape=[jax.ShapeDtypeStruct(w.shape, F32)] * 4,
    )(c.reshape(1), w, m, v, own, other)


WEIGHTS = ("norm_w", "final_norm_w", "ffn_gate", "ffn_up", "ffn_down", "w_in", "branch_proj", "w_out", "s5_lambda_re",
           "s5_lambda_im", "s5_log_dt", "s5_b_re", "s5_b_im", "s5_c_re", "s5_c_im", "s5_d", "s5_glu_w", "s5_glu_b",
           "hg_lb_logits", "hg_norm_w", "rg_conv_w", "rg_conv_b", "rg_wa", "rg_ba", "rg_wx", "rg_bx", "rg_lambda")
BIG = ("ffn_gate", "ffn_up", "ffn_down", "w_in", "branch_proj", "w_out", "s5_glu_w")
SHARDED_SMALL = ("norm_w", "rg_conv_w")
SMALL = SMALL_RAW + SHARDED_SMALL


def _view2d(shape):
    return (1, shape[0]) if len(shape) == 1 else (math.prod(shape[:-1]), shape[-1])


def _small_layout(shapes, row_multiple):
    layout, at = [], 0
    for shape in shapes:
        r, c = _view2d(shape)
        rp = -(-r // 8) * 8
        layout.append((at, r, c, rp))
        at += rp * max(1, c // LANE)
    return layout, -(-at // row_multiple) * row_multiple


def pack_small(name, arrays, row_multiple):
    layout, rows = _small_layout([a.shape for a in arrays], row_multiple)

    def body(*refs):
        out = refs[-1]
        out[...] = jnp.zeros_like(out)
        for ref, (r0, r, c, rp) in zip(refs[:-1], layout):
            if c <= LANE:
                out[r0:r0 + r, 0:c] = ref[...]
            else:
                for q in range(c // LANE):
                    out[r0 + q * rp:r0 + q * rp + r, :] = ref[:, q * LANE:(q + 1) * LANE]

    return pl.pallas_call(
        body, name=name, out_shape=jax.ShapeDtypeStruct((rows, LANE), F32),
        compiler_params=pltpu.CompilerParams(vmem_limit_bytes=VMEM_LIMIT),
    )(*[a.reshape(_view2d(a.shape)) for a in arrays])


def unpack_small(name, packed, shapes):
    layout, _ = _small_layout(shapes, 8)

    def body(p_ref, *outs):
        for ref, (r0, r, c, rp) in zip(outs, layout):
            if c <= LANE:
                ref[...] = p_ref[r0:r0 + r, 0:c]
            else:
                for q in range(c // LANE):
                    ref[:, q * LANE:(q + 1) * LANE] = p_ref[r0 + q * rp:r0 + q * rp + r, :]

    res = pl.pallas_call(
        body, name=name, out_shape=[jax.ShapeDtypeStruct(_view2d(s), F32) for s in shapes],
        compiler_params=pltpu.CompilerParams(vmem_limit_bytes=VMEM_LIMIT),
    )(packed)
    return [a.reshape(s) for a, s in zip(res, shapes)]


HBM = pl.BlockSpec(memory_space=pltpu.HBM)
SEM = pl.BlockSpec(memory_space=pltpu.SEMAPHORE)
EFFECT = pltpu.SideEffectType.DATAFLOW_SIDE_EFFECTING


def split_start(name, srcs, land_shapes, plan, n_send, n_recv):
    ns, nl = len(srcs), len(land_shapes)

    def body(*refs):
        ins, lands = refs[:ns], refs[ns:ns + nl]
        send, recv = refs[ns + nl], refs[ns + nl + 1]
        for src, dst, ks, kr, dev in plan(ins, lands):
            pltpu.make_async_remote_copy(src_ref=src, dst_ref=dst, send_sem=send.at[ks], recv_sem=recv.at[kr],
                                         device_id=dev, device_id_type=MESH).start()
        refs[-1][...] = jnp.zeros_like(refs[-1])

    hbm = lambda a: pltpu.with_memory_space_constraint(a, pltpu.HBM)
    lands = [lax.empty(s.shape, s.dtype) for s in land_shapes]
    out = pl.pallas_call(
        body, name=name,
        out_shape=(pltpu.SemaphoreType.DMA((n_send,)), pltpu.SemaphoreType.DMA((n_recv,)),
                   *[pltpu.HBM(a.shape, a.dtype) for a in srcs], *[pltpu.HBM(s.shape, s.dtype) for s in land_shapes],
                   jax.ShapeDtypeStruct((8, LANE), F32)),
        in_specs=[HBM] * (ns + nl), out_specs=(SEM, SEM, *[HBM] * (ns + nl), pl.BlockSpec(memory_space=pltpu.VMEM)),
        input_output_aliases={k: 2 + k for k in range(ns + nl)},
        compiler_params=pltpu.CompilerParams(has_side_effects=EFFECT),
    )(*[hbm(a) for a in srcs], *[hbm(a) for a in lands])
    return out[:-1], out[-1]


def split_wait(name, handles, n_src, waits, after):
    send, recv, *bufs = handles
    nb = len(bufs)

    def body(*refs):
        ins, lands = refs[:n_src], refs[n_src:nb]
        send_sem, recv_sem = refs[nb], refs[nb + 1]
        x, y, c, _chips = _place()
        sends, recvs = waits(ins, lands)
        for src, k in sends:
            pltpu.make_async_remote_copy(src_ref=src, dst_ref=src, send_sem=send_sem.at[k], recv_sem=recv_sem.at[0],
                                         device_id=(x, y, 1 - c), device_id_type=MESH).wait_send()
        for dst, k in recvs:
            pltpu.make_async_remote_copy(src_ref=dst, dst_ref=dst, send_sem=send_sem.at[0], recv_sem=recv_sem.at[k],
                                         device_id=(x, y, 1 - c), device_id_type=MESH).wait_recv()

    out = pl.pallas_call(
        body, name=name, out_shape=tuple(pltpu.HBM(a.shape, a.dtype) for a in bufs),
        in_specs=[HBM] * nb + [SEM, SEM, ANY], out_specs=tuple([HBM] * nb),
        input_output_aliases={k: k for k in range(nb)},
        compiler_params=pltpu.CompilerParams(has_side_effects=EFFECT),
    )(*bufs, send, recv, after)
    return list(out[:n_src]), list(out[n_src:])


def gather_plan(n):
    def plan(ins, lands):
        x, y, c, chips = _place()
        me = 2 * x + y
        copies = []
        for w in range(n):
            for j, (cx, cy) in enumerate(chips):
                for t in range(2):
                    copies.append((ins[w].at[c], lands[w].at[c, me], 8 * w + 2 * j + t, 8 * w + 2 * j + c, (cx, cy, t)))
            for half in range(2):
                copies.append((ins[w].at[half], lands[w].at[half, me], 8 * w + 6 + half, 8 * w + 6 + half, (x, y, 1 - c)))
        return copies

    def waits(ins, lands):
        x, y, c, chips = _place()
        me = 2 * x + y
        sends, recvs = [], []
        for w in range(n):
            for j, (cx, cy) in enumerate(chips):
                for t in range(2):
                    sends.append((ins[w].at[c], 8 * w + 2 * j + t))
                    recvs.append((lands[w].at[t, 2 * cx + cy], 8 * w + 2 * j + t))
            for half in range(2):
                sends.append((ins[w].at[half], 8 * w + 6 + half))
                recvs.append((lands[w].at[half, me], 8 * w + 6 + half))
        return sends, recvs

    return plan, waits


def scatter_plan(n):
    def plan(ins, lands):
        x, y, c, chips = _place()
        return [(ins[w].at[:, 2 * cx + cy], lands[w].at[j], 3 * w + j, 3 * w + j, (cx, cy, c))
                for w in range(n) for j, (cx, cy) in enumerate(chips)]

    def waits(ins, lands):
        x, y, c, chips = _place()
        sends = [(ins[w].at[:, 2 * cx + cy], 3 * w + j) for w in range(n) for j, (cx, cy) in enumerate(chips)]
        recvs = [(lands[w].at[j], 3 * w + j) for w in range(n) for j in range(3)]
        return sends, recvs

    return plan, waits


def _layer_shards(w, l):
    return [w["ffn_gate"][l].astype(MMT), w["ffn_up"][l].astype(MMT), w["ffn_down"][l].astype(MMT),
            w["w_in"][l].reshape(2, D_MODEL // 2, -1).astype(MMT),
            w["branch_proj"][l].reshape(2, 3 * BW // 2, -1).astype(MMT),
            w["w_out"][l].reshape(2, -1, D_MODEL).astype(MMT),
            w["s5_glu_w"][l].reshape(2, -1, BW).astype(MMT)]


def _layer_weights(g):
    rows = lambda a: a.transpose(1, 0, 2, 3).reshape(NSH, -1, a.shape[-1])
    p = rows(g[4]).reshape(NSH, 3, BW, -1).transpose(1, 2, 0, 3).reshape(3, BW, D_MODEL)
    return dict(wg=g[0], wu=g[1], wd=g[2], win=rows(g[3]), pfull=p,
                woutfull=rows(g[5]).reshape(D_MODEL, D_MODEL), gluw=rows(g[6]).reshape(BW, BW))


GROUPS = {"ffn1": ("ffn_gate", "ffn_up", "ffn_down"), "merge": ("branch_proj", "w_out"), "mid": ("s5_glu_w",),
          "pre": ("w_in",), "ffn0": ("ffn_gate", "ffn_up", "ffn_down")}


def _grad_views(big, l, group):
    four = lambda a: a.reshape(a.shape[0], NSH, -1, a.shape[-1])
    views = []
    for name in GROUPS[group]:
        if name == "branch_proj":
            dq = D_MODEL // NSH
            bp = big[name][l].reshape(3, BW, NSH, dq).transpose(2, 0, 1, 3).reshape(1, NSH, 3 * BW, dq)
            views.append((name, bp, 0))
        elif name.startswith("ffn"):
            views.append((name, four(big[name]), 2 * l + (1 if group == "ffn1" else 0)))
        else:
            views.append((name, four(big[name]), l))
    return views


def _reduce_to_halves(tag, views, c, wire):
    from_sibling = exchange_halves(f"reduce_cores_{tag}", [a for _, a, _ in views], [(p0, 1) for _, _, p0 in views])
    merge = lambda a: a.reshape((-1,) + a.shape[2:])
    return [add_own_half(f"sum_cores_{tag}_{i}", merge(a), merge(r), c, wire[i], NSH * p0).reshape(r.shape)
            for i, ((_, a, p0), r) in enumerate(zip(views, from_sibling))]


def _step(x, target, w, m, v):
    mx, my, mc = lax.axis_index("x"), lax.axis_index("y"), lax.axis_index("c")
    me = (2 * mx + my).astype(jnp.int32)
    mc = mc.astype(jnp.int32)

    W = dict(L=[None] * DEPTH)
    state = {"pending": []}
    n_big = len(BIG)
    g_plan, g_waits = gather_plan(n_big)

    def layer_weights(l, h):
        if l == 0:
            got = gather_shards("gather_weights_0", _layer_shards(w, 0) + [w[n] for n in SHARDED_SMALL])
            nxt = _layer_shards(w, 1)
            got, nxt = lax.optimization_barrier((got, nxt))
            shapes = [jax.ShapeDtypeStruct((2, NSH) + a.shape[1:], a.dtype) for a in nxt]
            state["gather"], token = split_start("gather_weights_1_start", nxt, shapes, g_plan, 8 * n_big, 8 * n_big)
            W["nw"] = got[n_big].transpose(0, 2, 1, 3).reshape(DEPTH, 3, 1, D_MODEL) + token[0, 0]
            W["convw"] = got[n_big + 1].transpose(0, 2, 1, 3).reshape(DEPTH, CONV_W, BW)
            return _layer_weights(got[:n_big]), h
        return _layer_weights(split_wait("gather_weights_1_wait", state["gather"], n_big, g_waits, h)[1]), h

    def layer_grads(l, group, big):
        if (l, group) == (0, "ffn0"):
            return
        views = _grad_views(big, l, group)
        tag = f"{l}_{group}"
        halves = _reduce_to_halves(tag, views, mc, [jnp.bfloat16] * len(views))
        shapes = [jax.ShapeDtypeStruct((3, a.shape[0]) + a.shape[2:], a.dtype) for a in halves]
        plan, waits = scatter_plan(len(halves))
        handles, token = split_start(f"reduce_chips_{tag}_start", halves, shapes, plan, 3 * len(halves), 3 * len(halves))
        W["nw"] = W["nw"] + token[0, 0]
        state["pending"].append((tag, [name for name, _, _ in views], l, group, handles, waits))

    loss, dx, big, small = local_step(x[0], target[0], W, {k: w[k] for k in SMALL_RAW}, layer_weights, layer_grads)

    pieces = {n: {} for n in BIG}
    block_of = lambda name, l, group: (2 * l + (group == "ffn1")) if name.startswith("ffn") else l
    views = _grad_views(big, 0, "ffn0")
    small_packed = pack_small("pack_small_grads", [small[n] for n in SMALL], NSH * 32)
    halves = _reduce_to_halves("0_ffn0", views + [("small", small_packed.reshape(1, NSH, -1, LANE), 0)], mc,
                               [jnp.bfloat16] * len(views) + [F32])
    from_chips = scatter_to_chips("reduce_chips_0_ffn0", halves)
    last = [add_chips(f"sum_chips_0_ffn0_{i}", h, r, me) for i, (h, r) in enumerate(zip(halves, from_chips))]
    for (name, _, _), piece in zip(views, last):
        pieces[name][block_of(name, 0, "ffn0")] = piece
    after = dx
    for tag, names, l, group, handles, waits in state["pending"]:
        sent, landed = split_wait(f"reduce_chips_{tag}_wait", handles, len(names), waits, after)
        for i, (name, h, r) in enumerate(zip(names, sent, landed)):
            pieces[name][block_of(name, l, group)] = add_chips(f"sum_chips_{tag}_{i}", h, r, me)
    own = [jnp.concatenate([pieces[n][b] for b in sorted(pieces[n])], axis=0) for n in BIG] + [last[-1]]
    other = share_halves("reduce_share", own)

    g, delta, new_m, new_v = {}, {}, {}, {}
    for i, n in enumerate(BIG):
        view = lambda a: a.reshape(own[i].shape[0], -1, own[i].shape[2])
        res = adamw_halves(f"adamw_{n}", view(w[n]), view(m[n]), view(v[n]), own[i], other[i], mc)
        g[n], delta[n], new_m[n], new_v[n] = [a.reshape(w[n].shape) for a in res]

    piece = jnp.stack([jnp.where(mc == 0, own[-1][0], other[-1][0]), jnp.where(mc == 0, other[-1][0], own[-1][0])])
    (all_small,) = gather_shards("gather_small", [piece])
    full_small = unpack_small("unpack_small_grads", all_small.transpose(1, 0, 2, 3).reshape(-1, LANE),
                              [small[n].shape for n in SMALL])
    g.update(zip(SMALL, full_small))
    g["norm_w"] = lax.dynamic_slice_in_dim(g["norm_w"], me * (D_MODEL // NSH), D_MODEL // NSH, axis=2)
    g["rg_conv_w"] = lax.dynamic_slice_in_dim(g["rg_conv_w"], me * (BW // NSH), BW // NSH, axis=2)

    packed = [pack_small(f"pack_small_{tag}", [src[n] for n in SMALL], 8)
              for tag, src in (("w", w), ("g", g), ("m", m), ("v", v))]
    for tag, dst, flat in zip(("delta", "m", "v"), (delta, new_m, new_v), adamw(*packed)):
        dst.update(zip(SMALL, unpack_small(f"unpack_small_{tag}", flat, [w[n].shape for n in SMALL])))

    total = lax.psum(loss[0, 0], ("x", "y", "c"))
    return (total, dx[None], *[g[n] for n in WEIGHTS], *[delta[n] for n in WEIGHTS],
            *[new_m[n] for n in WEIGHTS], *[new_v[n] for n in WEIGHTS])


def kernel(x, norm_w, final_norm_w, ffn_gate, ffn_up, ffn_down, w_in, branch_proj, w_out, s5_lambda_re, s5_lambda_im, s5_log_dt, s5_b_re, s5_b_im, s5_c_re, s5_c_im, s5_d, s5_glu_w, s5_glu_b, hg_lb_logits, hg_norm_w, rg_conv_w, rg_conv_b, rg_wa, rg_ba, rg_wx, rg_bx, rg_lambda, loss_target, m_norm_w, m_final_norm_w, m_ffn_gate, m_ffn_up, m_ffn_down, m_w_in, m_branch_proj, m_w_out, m_s5_lambda_re, m_s5_lambda_im, m_s5_log_dt, m_s5_b_re, m_s5_b_im, m_s5_c_re, m_s5_c_im, m_s5_d, m_s5_glu_w, m_s5_glu_b, m_hg_lb_logits, m_hg_norm_w, m_rg_conv_w, m_rg_conv_b, m_rg_wa, m_rg_ba, m_rg_wx, m_rg_bx, m_rg_lambda, v_norm_w, v_final_norm_w, v_ffn_gate, v_ffn_up, v_ffn_down, v_w_in, v_branch_proj, v_w_out, v_s5_lambda_re, v_s5_lambda_im, v_s5_log_dt, v_s5_b_re, v_s5_b_im, v_s5_c_re, v_s5_c_im, v_s5_d, v_s5_glu_w, v_s5_glu_b, v_hg_lb_logits, v_hg_norm_w, v_rg_conv_w, v_rg_conv_b, v_rg_wa, v_rg_ba, v_rg_wx, v_rg_bx, v_rg_lambda):
    ws = (norm_w, final_norm_w, ffn_gate, ffn_up, ffn_down, w_in, branch_proj, w_out, s5_lambda_re, s5_lambda_im, s5_log_dt, s5_b_re, s5_b_im, s5_c_re, s5_c_im, s5_d, s5_glu_w, s5_glu_b, hg_lb_logits, hg_norm_w, rg_conv_w, rg_conv_b, rg_wa, rg_ba, rg_wx, rg_bx, rg_lambda)
    ms = (m_norm_w, m_final_norm_w, m_ffn_gate, m_ffn_up, m_ffn_down, m_w_in, m_branch_proj, m_w_out, m_s5_lambda_re, m_s5_lambda_im, m_s5_log_dt, m_s5_b_re, m_s5_b_im, m_s5_c_re, m_s5_c_im, m_s5_d, m_s5_glu_w, m_s5_glu_b, m_hg_lb_logits, m_hg_norm_w, m_rg_conv_w, m_rg_conv_b, m_rg_wa, m_rg_ba, m_rg_wx, m_rg_bx, m_rg_lambda)
    vs = (v_norm_w, v_final_norm_w, v_ffn_gate, v_ffn_up, v_ffn_down, v_w_in, v_branch_proj, v_w_out, v_s5_lambda_re, v_s5_lambda_im, v_s5_log_dt, v_s5_b_re, v_s5_b_im, v_s5_c_re, v_s5_c_im, v_s5_d, v_s5_glu_w, v_s5_glu_b, v_hg_lb_logits, v_hg_norm_w, v_rg_conv_w, v_rg_conv_b, v_rg_wa, v_rg_ba, v_rg_wx, v_rg_bx, v_rg_lambda)
    return _step(x, loss_target, dict(zip(WEIGHTS, ws)), dict(zip(WEIGHTS, ms)), dict(zip(WEIGHTS, vs)))
```

```python
import functools
import math
from typing import NamedTuple

import jax
import jax.numpy as jnp
from jax import lax
from jax.experimental import pallas as pl
from jax.experimental.pallas import tpu as pltpu

F32 = jnp.float32
MMT = jnp.bfloat16
HI = lax.Precision.HIGHEST

D_MODEL = 1024
BW = 512
S5_GROUP, S5_GROUPS, S5_STATE = 16, 32, 64
S5_N = S5_GROUPS * S5_STATE
HG_HEADS, HG_D = 4, 128
HG_CHUNK = 128
RG_BLOCKS, RG_BLOCK = 8, 64
RG_C = 8.0
CONV_W = 4
D_FF = 2816
EPS = 1e-6
IN_TOTAL = 6656
NSH = 4
NSEG = 8
LANE = 128
VMEM_LIMIT = 56 * 1024 * 1024
TM_FWD = 512
TM_WGRAD = 512

ADAM_LR, ADAM_B1, ADAM_B2, ADAM_EPS, ADAM_WD, ADAM_STEP = 0.001, 0.9, 0.999, 1e-08, 0.01, 10

MESH = pl.DeviceIdType.MESH


class WP(NamedTuple):
    w: jax.Array
    p: jax.Array


def _dg(a, b, ca, cb):
    return lax.dot_general(a, b, (((ca,), (cb,)), ((), ())), preferred_element_type=F32)


@jax.custom_vjp
def _mmw(a, w, p):
    return _dg(a.astype(MMT), w, 1, 0)


def _mmw_fwd(a, w, p):
    return _mmw(a, w, p), (a, w)


def _mmw_bwd(res, g):
    a, w = res
    gb = g.astype(MMT)
    return _dg(gb, w, 1, 1), jnp.zeros_like(w), _dg(a.astype(MMT), gb, 0, 0)


_mmw.defvjp(_mmw_fwd, _mmw_bwd)


def mm(a, w):
    if isinstance(w, WP):
        return _mmw(a, w.w, w.p)
    return _dg(a.astype(MMT), w, 1, 0)


@jax.custom_vjp
def mma_nn(a, b):
    return _dg(a.astype(MMT), b.astype(MMT), 1, 0)


def _nn_f(a, b):
    return mma_nn(a, b), (a, b)


def _nn_b(res, g):
    a, b = res
    gb = g.astype(MMT)
    return _dg(gb, b.astype(MMT), 1, 1), _dg(a.astype(MMT), gb, 0, 0)


mma_nn.defvjp(_nn_f, _nn_b)


@jax.custom_vjp
def mma_nt(a, b):
    return _dg(a.astype(MMT), b.astype(MMT), 1, 1)


def _nt_f(a, b):
    return mma_nt(a, b), (a, b)


def _nt_b(res, g):
    a, b = res
    gb = g.astype(MMT)
    return _dg(gb, b.astype(MMT), 1, 0), _dg(gb, a.astype(MMT), 0, 0)


mma_nt.defvjp(_nt_f, _nt_b)


@jax.custom_vjp
def mma_tn(a, b):
    return _dg(a.astype(MMT), b.astype(MMT), 0, 0)


def _tn_f(a, b):
    return mma_tn(a, b), (a, b)


def _tn_b(res, g):
    a, b = res
    gb = g.astype(MMT)
    return _dg(b.astype(MMT), gb, 1, 1), _dg(a.astype(MMT), gb, 1, 0)


mma_tn.defvjp(_tn_f, _tn_b)


def mm_exact(m, x):
    return jnp.dot(m, x, precision=HI, preferred_element_type=F32)


def _rms(x, w):
    return x * lax.rsqrt(jnp.mean(x * x, axis=-1, keepdims=True) + EPS) * w


def _expm1(x):
    series = x * (1.0 + x * (1.0 / 2) * (1.0 + x * (1.0 / 3) * (1.0 + x * (1.0 / 4) * (1.0 + x * (1.0 / 5) * (1.0 + x * (1.0 / 6))))))
    return jnp.where(jnp.abs(x) < 0.1, series, jnp.exp(x) - 1.0)


def _bspec(block, fn, order):
    if order == "is":
        return pl.BlockSpec(block, lambda i, s: fn(s, i))
    return pl.BlockSpec(block, lambda s, i: fn(s, i))


def tile_fwd(fn, name, n_i, n_s, ins, outs, s_outer=False):
    n_in = len(ins)
    order = "si" if s_outer else "is"
    assert not (s_outer and any(o[4] for o in outs))

    def body(*refs):
        s = pl.program_id(0 if s_outer else 1)
        res = fn(*[r[...] for r in refs[:n_in]], s)
        for o_ref, val, spec in zip(refs[n_in:], res, outs):
            if spec[4] and n_s > 1:
                @pl.when(s == 0)
                def _(o_ref=o_ref, val=val):
                    o_ref[...] = val.astype(o_ref.dtype)

                @pl.when(s != 0)
                def _(o_ref=o_ref, val=val):
                    o_ref[...] += val.astype(o_ref.dtype)
            else:
                o_ref[...] = val.astype(o_ref.dtype)

    return pl.pallas_call(
        body, grid=(n_s, n_i) if s_outer else (n_i, n_s), name=name,
        in_specs=[_bspec(b, f, order) for _, b, f in ins],
        out_specs=[_bspec(b, f, order) for _, _, b, f, _ in outs],
        out_shape=[jax.ShapeDtypeStruct(sh, dt) for sh, dt, _, _, _ in outs],
        compiler_params=pltpu.CompilerParams(vmem_limit_bytes=VMEM_LIMIT,
                                             dimension_semantics=("arbitrary", "arbitrary")),
    )(*[a for a, _, _ in ins])


def tile_bwd(fn, name, n_i, n_s, ins, cts, gouts):
    groups = [c if isinstance(c, list) else [c] for c in cts]
    cts = [blk for grp in groups for blk in grp]
    n_in, n_ct = len(ins), len(cts)
    kinds = [k for _, _, _, k in ins]
    d_pos = [j for j, k in enumerate(kinds) if k != "c"]
    shared = [(gi, spec[4]) for gi, spec in enumerate(gouts) if len(spec) == 5 and spec[4] is not None]
    n_sh = len(shared)

    def body(*refs):
        s, i = pl.program_id(0), pl.program_id(1)
        vals = [r[...] for r in refs[:n_in]]
        ct_refs, ctv = list(refs[n_in:n_in + n_ct]), []
        for grp in groups:
            parts = [ct_refs.pop(0)[...] for _ in grp]
            ctv.append(parts[0] if len(parts) == 1 else jnp.concatenate(parts, axis=1))
        ctv = tuple(ctv)
        g_refs = refs[n_in + n_ct + n_sh:]

        def g(*dv):
            args = list(vals)
            for j, v in zip(d_pos, dv):
                args[j] = WP(vals[j], v) if kinds[j] == "w" else v
            return tuple(fn(*args))

        dv0 = [jnp.zeros(vals[j].shape, F32) if kinds[j] == "w" else vals[j] for j in d_pos]
        _, vjp = jax.vjp(g, *dv0)
        grads = vjp(ctv)
        for g_ref, gv, spec in zip(g_refs, grads, gouts):
            mode = spec[3]
            if mode == "write":
                g_ref[...] = gv.astype(g_ref.dtype)
            else:
                first = (i == 0) if mode == "acc_i" else jnp.logical_and(i == 0, s == 0)

                @pl.when(first)
                def _(g_ref=g_ref, gv=gv):
                    g_ref[...] = gv.astype(g_ref.dtype)

                @pl.when(jnp.logical_not(first))
                def _(g_ref=g_ref, gv=gv):
                    g_ref[...] += gv.astype(g_ref.dtype)

    return pl.pallas_call(
        body, grid=(n_s, n_i), name=name,
        in_specs=([_bspec(b, f, "si") for _, b, f, _ in ins] + [_bspec(b, f, "si") for _, b, f in cts]
                  + [pl.BlockSpec(memory_space=pl.ANY)] * n_sh),
        out_specs=[_bspec(spec[1], spec[2], "si") for spec in gouts],
        out_shape=[jax.ShapeDtypeStruct(spec[0], F32) for spec in gouts],
        input_output_aliases={n_in + n_ct + k: gi for k, (gi, _) in enumerate(shared)},
        compiler_params=pltpu.CompilerParams(vmem_limit_bytes=VMEM_LIMIT,
                                             dimension_semantics=("arbitrary", "arbitrary")),
    )(*[a for a, _, _, _ in ins], *[a for a, _, _ in cts], *[buf for _, buf in shared])


def _row_tile(rows, width, itemsize=4, budget=2 * 1024 * 1024, mult=8):
    best = mult
    for t in range(mult, rows + 1, mult):
        if rows % t == 0 and t * width * itemsize <= budget:
            best = t
    return best


def add_n(name, terms, shape):
    rows, cols = shape
    tr = _row_tile(rows, cols)

    def body(*refs):
        acc = refs[0][...]
        for r in refs[1:-1]:
            acc = acc + r[...]
        refs[-1][...] = acc

    specs = []
    for _, lead in terms:
        specs.append(pl.BlockSpec((None,) * len(lead) + (tr, cols), functools.partial(lambda i, lead: (*lead, i, 0), lead=lead)))
    return pl.pallas_call(
        body, grid=(rows // tr,), name=name, in_specs=specs,
        out_specs=pl.BlockSpec((tr, cols), lambda i: (i, 0)),
        out_shape=jax.ShapeDtypeStruct((rows, cols), F32),
    )(*[a for a, _ in terms])


def ffn_core(x, nw, wg, wu, wd):
    h = _rms(x, nw)
    return (0.5 * mm(jax.nn.silu(mm(h, wg)) * mm(h, wu), wd),)


def pre_core(x, nw, win):
    return (mm(_rms(x, nw), win),)


def _split_lanes(y):
    return jnp.stack([y[:, k * LANE:(k + 1) * LANE] for k in range(y.shape[1] // LANE)], axis=0)


def _join_lanes(y3):
    return jnp.concatenate([y3[k] for k in range(y3.shape[0])], axis=1)


def s5_pre_core(u, bmat):
    bu = mm(u, bmat)
    return _split_lanes(bu[:, :S5_N]), _split_lanes(bu[:, S5_N:])


def mid_core(xr, xi, u, o, g, hs, gc, hmat, cmat, d, gluw, glub, hgw):
    xs = jnp.concatenate([_join_lanes(xr), _join_lanes(xi)], axis=1)
    y = mm(xs, cmat) + d * u
    z = jax.nn.gelu(y)
    ya = z * jax.nn.sigmoid(mm(z, gluw) + glub)
    ms = mm_exact(o * o, hmat)
    yb = o * lax.rsqrt(ms + EPS) * hgw * jax.nn.silu(g)
    yc = hs * jax.nn.gelu(gc)
    return ya, yb, yc


def _sub(w, n):
    return WP(w.w[n], w.p[n]) if isinstance(w, WP) else w[n]


def merge_core(ya, yb, yc, g0, g1, g2, g3, g4, g5, p, wout):
    gate = lambda a, b: jax.nn.sigmoid(jnp.concatenate([a, b], axis=1))
    m = gate(g0, g1) * mm(ya, _sub(p, 0)) + gate(g2, g3) * mm(yb, _sub(p, 1)) + gate(g4, g5) * mm(yc, _sub(p, 2))
    return (mm(m, wout),)


def gates_core(xc, wa, ba, wx, bx, lam):
    r = jax.nn.sigmoid(mm(xc, wa) + ba)
    i = jax.nn.sigmoid(mm(xc, wx) + bx)
    log_a = -RG_C * jax.nn.softplus(-lam) * r
    a = jnp.exp(log_a)
    b = jnp.sqrt(-_expm1(2.0 * log_a)) * (i * xc)
    return a, b


def _seg_rows(ref, k, j, n):
    rows = pl.ds(pl.multiple_of(j * NSEG, NSEG), NSEG)
    if k is None:
        return ref[rows, :]
    return ref[k, rows, :]


def _seg_store(ref, k, j, n, val):
    rows = pl.ds(pl.multiple_of(j * NSEG, NSEG), NSEG)
    if k is None:
        ref[rows, :] = val
    else:
        ref[k, rows, :] = val


def _seg_carries(er, ei, pr, pi, reverse):
    rows = lax.broadcasted_iota(jnp.int32, er.shape, 0)
    cr = jnp.zeros_like(er)
    ci = None if ei is None else jnp.zeros_like(er)
    order = range(NSEG - 2, -1, -1) if reverse else range(1, NSEG)
    shift = NSEG - 1 if reverse else 1
    for s in order:
        if ei is None:
            tr = er + pr * cr
            cr = jnp.where(rows == s, pltpu.roll(tr, shift, 0), cr)
        else:
            tr = er + pr * cr - pi * ci
            ti = ei + pr * ci + pi * cr
            cr = jnp.where(rows == s, pltpu.roll(tr, shift, 0), cr)
            ci = jnp.where(rows == s, pltpu.roll(ti, shift, 0), ci)
    return cr, ci


S5_K = 2


def s5_scan_fwd(bur, bui, ar, ai, L):
    n = L // NSEG
    nb = S5_N // LANE
    K = S5_K

    def body(br_ref, bi_ref, ar_ref, ai_ref, xr_ref, xi_ref):
        zero = jnp.zeros((NSEG, LANE), F32)
        A = [(jnp.broadcast_to(ar_ref[k], (NSEG, LANE)), jnp.broadcast_to(ai_ref[k], (NSEG, LANE))) for k in range(K)]

        def p1(j, st):
            new = []
            for k in range(K):
                sr, si, pr, pi = st[k]
                a_r, a_i = A[k]
                nr = a_r * sr - a_i * si + _seg_rows(br_ref, k, j, n)
                ni = a_r * si + a_i * sr + _seg_rows(bi_ref, k, j, n)
                _seg_store(xr_ref, k, j, n, nr)
                _seg_store(xi_ref, k, j, n, ni)
                new.append((nr, ni, a_r * pr - a_i * pi, a_r * pi + a_i * pr))
            return tuple(new)

        st = lax.fori_loop(0, n, p1, tuple((zero, zero, zero + 1.0, zero) for _ in range(K)))
        C = [_seg_carries(st[k][0], st[k][1], st[k][2], st[k][3], False) for k in range(K)]

        def p2(j, st):
            new = []
            for k in range(K):
                pr, pi = st[k]
                a_r, a_i = A[k]
                pr, pi = a_r * pr - a_i * pi, a_r * pi + a_i * pr
                cr, ci = C[k]
                _seg_store(xr_ref, k, j, n, _seg_rows(xr_ref, k, j, n) + pr * cr - pi * ci)
                _seg_store(xi_ref, k, j, n, _seg_rows(xi_ref, k, j, n) + pr * ci + pi * cr)
                new.append((pr, pi))
            return tuple(new)

        lax.fori_loop(0, n, p2, tuple((zero + 1.0, zero) for _ in range(K)))

    blk = pl.BlockSpec((K, L, LANE), lambda g: (g, 0, 0))
    ablk = pl.BlockSpec((K, 1, LANE), lambda g: (g, 0, 0))
    return pl.pallas_call(
        body, grid=(nb // K,), name="s5_scan_fwd",
        in_specs=[blk, blk, ablk, ablk], out_specs=[blk, blk],
        out_shape=[jax.ShapeDtypeStruct((nb, L, LANE), F32)] * 2,
        compiler_params=pltpu.CompilerParams(vmem_limit_bytes=VMEM_LIMIT),
    )(bur, bui, ar, ai)


def s5_scan_bwd(dxr, dxi, xr, xi, ar, ai, L):
    n = L // NSEG
    nb = S5_N // LANE
    K = S5_K

    def body(dr_ref, di_ref, xr_ref, xi_ref, ar_ref, ai_ref, gr_ref, gi_ref, dar_ref, dai_ref):
        zero = jnp.zeros((NSEG, LANE), F32)
        rows = lax.broadcasted_iota(jnp.int32, (NSEG, LANE), 0)
        A = [(jnp.broadcast_to(ar_ref[k], (NSEG, LANE)), -jnp.broadcast_to(ai_ref[k], (NSEG, LANE))) for k in range(K)]

        def p1(jj, st):
            j = n - 1 - jj
            new = []
            for k in range(K):
                sr, si, pr, pi = st[k]
                a_r, a_i = A[k]
                nr = a_r * sr - a_i * si + _seg_rows(dr_ref, k, j, n)
                ni = a_r * si + a_i * sr + _seg_rows(di_ref, k, j, n)
                _seg_store(gr_ref, k, j, n, nr)
                _seg_store(gi_ref, k, j, n, ni)
                new.append((nr, ni, a_r * pr - a_i * pi, a_r * pi + a_i * pr))
            return tuple(new)

        st = lax.fori_loop(0, n, p1, tuple((zero, zero, zero + 1.0, zero) for _ in range(K)))
        C = [_seg_carries(st[k][0], st[k][1], st[k][2], st[k][3], True) for k in range(K)]
        xb = [(jnp.where(rows == 0, 0.0, pltpu.roll(_seg_rows(xr_ref, k, n - 1, n), 1, 0)),
               jnp.where(rows == 0, 0.0, pltpu.roll(_seg_rows(xi_ref, k, n - 1, n), 1, 0))) for k in range(K)]

        def p2(jj, st):
            j = n - 1 - jj
            jp = jnp.maximum(j - 1, 0)
            new = []
            for k in range(K):
                pr, pi, acr, aci = st[k]
                a_r, a_i = A[k]
                pr, pi = a_r * pr - a_i * pi, a_r * pi + a_i * pr
                cr, ci = C[k]
                g_r = _seg_rows(gr_ref, k, j, n) + pr * cr - pi * ci
                g_i = _seg_rows(gi_ref, k, j, n) + pr * ci + pi * cr
                _seg_store(gr_ref, k, j, n, g_r)
                _seg_store(gi_ref, k, j, n, g_i)
                xpr = jnp.where(j == 0, xb[k][0], _seg_rows(xr_ref, k, jp, n))
                xpi = jnp.where(j == 0, xb[k][1], _seg_rows(xi_ref, k, jp, n))
                new.append((pr, pi, acr + g_r * xpr + g_i * xpi, aci + g_i * xpr - g_r * xpi))
            return tuple(new)

        st = lax.fori_loop(0, n, p2, tuple((zero + 1.0, zero, zero, zero) for _ in range(K)))
        for k in range(K):
            dar_ref[k] = jnp.sum(st[k][2], axis=0, keepdims=True)
            dai_ref[k] = jnp.sum(st[k][3], axis=0, keepdims=True)

    blk = pl.BlockSpec((K, L, LANE), lambda g: (g, 0, 0))
    ablk = pl.BlockSpec((K, 1, LANE), lambda g: (g, 0, 0))
    return pl.pallas_call(
        body, grid=(nb // K,), name="s5_scan_bwd",
        in_specs=[blk, blk, blk, blk, ablk, ablk], out_specs=[blk, blk, ablk, ablk],
        out_shape=[jax.ShapeDtypeStruct((nb, L, LANE), F32)] * 2 + [jax.ShapeDtypeStruct((nb, 1, LANE), F32)] * 2,
        compiler_params=pltpu.CompilerParams(vmem_limit_bytes=VMEM_LIMIT),
    )(dxr, dxi, xr, xi, ar, ai)


def rg_scan_fwd(a, b, L):
    n = L // NSEG

    def body(a_ref, b_ref, h_ref):
        zero = jnp.zeros((NSEG, LANE), F32)

        def p1(j, st):
            h, p = st
            aj = _seg_rows(a_ref, None, j, n)
            h = aj * h + _seg_rows(b_ref, None, j, n)
            _seg_store(h_ref, None, j, n, h)
            return h, aj * p

        e, pe = lax.fori_loop(0, n, p1, (zero, zero + 1.0))
        c, _ = _seg_carries(e, None, pe, None, False)

        def p2(j, p):
            p = _seg_rows(a_ref, None, j, n) * p
            _seg_store(h_ref, None, j, n, _seg_rows(h_ref, None, j, n) + p * c)
            return p

        lax.fori_loop(0, n, p2, zero + 1.0)

    blk = pl.BlockSpec((L, LANE), lambda g: (0, g))
    return pl.pallas_call(
        body, grid=(BW // LANE,), name="rg_scan_fwd", in_specs=[blk, blk], out_specs=blk,
        out_shape=jax.ShapeDtypeStruct((L, BW), F32),
        compiler_params=pltpu.CompilerParams(vmem_limit_bytes=VMEM_LIMIT),
    )(a, b)


def rg_scan_bwd(a, h, dh, L):
    n = L // NSEG

    def body(a_ref, h_ref, dh_ref, da_ref, db_ref):
        zero = jnp.zeros((NSEG, LANE), F32)
        rows = lax.broadcasted_iota(jnp.int32, (NSEG, LANE), 0)
        a_edge = jnp.where(rows == NSEG - 1, 0.0, pltpu.roll(_seg_rows(a_ref, None, 0, n), NSEG - 1, 0))
        h_edge = jnp.where(rows == 0, 0.0, pltpu.roll(_seg_rows(h_ref, None, n - 1, n), 1, 0))

        def mult(j):
            return jnp.where(j == n - 1, a_edge, _seg_rows(a_ref, None, jnp.minimum(j + 1, n - 1), n))

        def p1(jj, st):
            j = n - 1 - jj
            g, p = st
            m = mult(j)
            g = m * g + _seg_rows(dh_ref, None, j, n)
            _seg_store(db_ref, None, j, n, g)
            return g, m * p

        e, pe = lax.fori_loop(0, n, p1, (zero, zero + 1.0))
        c, _ = _seg_carries(e, None, pe, None, True)

        def p2(jj, p):
            j = n - 1 - jj
            p = mult(j) * p
            g = _seg_rows(db_ref, None, j, n) + p * c
            _seg_store(db_ref, None, j, n, g)
            hp = jnp.where(j == 0, h_edge, _seg_rows(h_ref, None, jnp.maximum(j - 1, 0), n))
            _seg_store(da_ref, None, j, n, g * hp)
            return p

        lax.fori_loop(0, n, p2, zero + 1.0)

    blk = pl.BlockSpec((L, LANE), lambda g: (0, g))
    return pl.pallas_call(
        body, grid=(BW // LANE,), name="rg_scan_bwd", in_specs=[blk, blk, blk], out_specs=[blk, blk],
        out_shape=[jax.ShapeDtypeStruct((L, BW), F32)] * 2,
        compiler_params=pltpu.CompilerParams(vmem_limit_bytes=VMEM_LIMIT),
    )(a, h, dh)


def _hg_consts(C):
    t = lax.broadcasted_iota(jnp.int32, (C, C), 0)
    s = lax.broadcasted_iota(jnp.int32, (C, C), 1)
    tril = (s <= t).astype(F32)
    diag = (s == t).astype(F32)
    levels = []
    k = 1
    while (1 << k) <= C:
        m = 1 << (k - 1)
        same = (t >> k) == (s >> k)
        t_right = ((t >> (k - 1)) & 1) == 1
        s_left = ((s >> (k - 1)) & 1) == 0
        mask = jnp.logical_and(same, jnp.logical_and(t_right, s_left)).astype(F32)
        bnd = ((t >> k) << k) + (m - 1)
        levels.append((mask, (s <= bnd).astype(F32)))
        k += 1
    return tril, diag, levels


def hg_chunk(st, q, z, v, lb):
    C = q.shape[0]
    tril, diag, levels = _hg_consts(C)
    sig = jax.nn.sigmoid(z)
    lf = jnp.log(lb + (1.0 - lb) * sig)
    k = (1.0 - lb) * jax.nn.sigmoid(-z)
    qh = jax.nn.silu(q)
    b = mm_exact(tril, lf)
    blast = jnp.sum(lf, axis=0, keepdims=True)
    qe = qh * jnp.exp(b)
    kd = k * jnp.exp(blast - b)
    scaled = []
    for _, sel in levels:
        ref = mm_exact(sel, lf)
        scaled.append((qh * jnp.exp(jnp.minimum(b - ref, 0.0)), k * jnp.exp(jnp.minimum(ref - b, 0.0))))
    outs, news = [], []
    for h in range(HG_HEADS):
        sl = slice(h * HG_D, (h + 1) * HG_D)
        st_h = st[h * HG_D:(h + 1) * HG_D, :]
        sc = diag * mma_nt(qh[:, sl], k[:, sl])
        for (mask, _), (qt, kt) in zip(levels, scaled):
            sc = sc + mask * mma_nt(qt[:, sl], kt[:, sl])
        outs.append(mma_nt(qe[:, sl], st_h) + mma_nn(sc, v[:, sl]))
        news.append(st_h * jnp.exp(blast[:, sl]) + mma_tn(v[:, sl], kd[:, sl]))
    return jnp.concatenate(news, axis=0), jnp.concatenate(outs, axis=1)


def hg_fwd(qzv, lb, L):
    C = HG_CHUNK
    nc = L // C

    def body(q_ref, z_ref, v_ref, lb_ref, o_ref, sst_ref, st_ref):
        @pl.when(pl.program_id(0) == 0)
        def _():
            st_ref[...] = jnp.zeros_like(st_ref)

        st = st_ref[...]
        sst_ref[...] = st
        new, o = hg_chunk(st, q_ref[...], z_ref[...], v_ref[...], lb_ref[...])
        st_ref[...] = new
        o_ref[...] = o

    col = lambda cb: pl.BlockSpec((C, BW), functools.partial(lambda c, cb: (c, cb), cb=cb))
    return pl.pallas_call(
        body, grid=(nc,), name="hg_fwd",
        in_specs=[col(0), col(1), col(2), pl.BlockSpec((1, BW), lambda c: (0, 0))],
        out_specs=[pl.BlockSpec((C, BW), lambda c: (c, 0)), pl.BlockSpec((None, BW, HG_D), lambda c: (c, 0, 0))],
        out_shape=[jax.ShapeDtypeStruct((L, BW), F32), jax.ShapeDtypeStruct((nc, BW, HG_D), F32)],
        scratch_shapes=[pltpu.VMEM((BW, HG_D), F32)],
        compiler_params=pltpu.CompilerParams(vmem_limit_bytes=VMEM_LIMIT, dimension_semantics=("arbitrary",)),
    )(qzv, qzv, qzv, lb)


def hg_bwd(qzv, lb, sst, do, L):
    C = HG_CHUNK
    nc = L // C

    def body(q_ref, z_ref, v_ref, lb_ref, sst_ref, do_ref, dq_ref, dz_ref, dv_ref, dlb_ref, dst_ref):
        @pl.when(pl.program_id(0) == 0)
        def _():
            dst_ref[...] = jnp.zeros_like(dst_ref)
            dlb_ref[...] = jnp.zeros_like(dlb_ref)

        _, vjp = jax.vjp(hg_chunk, sst_ref[...], q_ref[...], z_ref[...], v_ref[...], lb_ref[...])
        dst, dq, dz, dv, dlb = vjp((dst_ref[...], do_ref[...]))
        dst_ref[...] = dst
        dq_ref[...] = dq
        dz_ref[...] = dz
        dv_ref[...] = dv
        dlb_ref[...] += dlb

    col = lambda cb: pl.BlockSpec((C, BW), functools.partial(lambda c, cb: (nc - 1 - c, cb), cb=cb))
    rev = pl.BlockSpec((C, BW), lambda c: (nc - 1 - c, 0))
    return pl.pallas_call(
        body, grid=(nc,), name="hg_bwd",
        in_specs=[col(0), col(1), col(2), pl.BlockSpec((1, BW), lambda c: (0, 0)),
                  pl.BlockSpec((None, BW, HG_D), lambda c: (nc - 1 - c, 0, 0)), rev],
        out_specs=[rev, rev, rev, pl.BlockSpec((1, BW), lambda c: (0, 0))],
        out_shape=[jax.ShapeDtypeStruct((L, BW), F32)] * 3 + [jax.ShapeDtypeStruct((1, BW), F32)],
        scratch_shapes=[pltpu.VMEM((BW, HG_D), F32)],
        compiler_params=pltpu.CompilerParams(vmem_limit_bytes=VMEM_LIMIT, dimension_semantics=("arbitrary",)),
    )(qzv, qzv, qzv, lb, sst, do)


def _shift_down(x, d, rows, L):
    if d == 0:
        return x
    wrapped = jnp.where((rows & (NSEG - 1)) == 0, 0.0, pltpu.roll(x, NSEG * d + 1, 0))
    return jnp.where(rows < NSEG * d, wrapped, pltpu.roll(x, NSEG * d, 0))


def _shift_up(x, d, rows, L):
    if d == 0:
        return x
    wrapped = jnp.where((rows & (NSEG - 1)) == NSEG - 1, 0.0, pltpu.roll(x, L - (NSEG * d + 1), 0))
    return jnp.where(rows >= L - NSEG * d, wrapped, pltpu.roll(x, L - NSEG * d, 0))


def conv_fwd(proj, w, b, L):
    def body(x_ref, w_ref, b_ref, o_ref):
        x = x_ref[...]
        rows = lax.broadcasted_iota(jnp.int32, x.shape, 0)
        acc = jnp.broadcast_to(b_ref[...], x.shape)
        for k in range(CONV_W):
            acc = acc + w_ref[pl.ds(k, 1), :] * _shift_down(x, CONV_W - 1 - k, rows, L)
        o_ref[...] = acc

    nl = BW // LANE
    return pl.pallas_call(
        body, grid=(nl,), name="conv_fwd",
        in_specs=[pl.BlockSpec((L, LANE), lambda g: (0, 5 * nl + g)), pl.BlockSpec((CONV_W, LANE), lambda g: (0, g)),
                  pl.BlockSpec((1, LANE), lambda g: (0, g))],
        out_specs=pl.BlockSpec((L, LANE), lambda g: (0, g)),
        out_shape=jax.ShapeDtypeStruct((L, BW), F32),
        compiler_params=pltpu.CompilerParams(vmem_limit_bytes=VMEM_LIMIT),
    )(proj, w, b)


def conv_bwd(proj, w, dxc, L):
    def body(x_ref, w_ref, d_ref, dx_ref, dw_ref, db_ref):
        x, d = x_ref[...], d_ref[...]
        rows = lax.broadcasted_iota(jnp.int32, x.shape, 0)
        acc = jnp.zeros_like(x)
        for k in range(CONV_W):
            acc = acc + w_ref[pl.ds(k, 1), :] * _shift_up(d, CONV_W - 1 - k, rows, L)
            dw_ref[pl.ds(k, 1), :] = jnp.sum(d * _shift_down(x, CONV_W - 1 - k, rows, L), axis=0, keepdims=True)
        dx_ref[...] = acc
        db_ref[...] = jnp.sum(d, axis=0, keepdims=True)

    nl = BW // LANE
    blk = pl.BlockSpec((L, LANE), lambda g: (0, g))
    return pl.pallas_call(
        body, grid=(nl,), name="conv_bwd",
        in_specs=[pl.BlockSpec((L, LANE), lambda g: (0, 5 * nl + g)), pl.BlockSpec((CONV_W, LANE), lambda g: (0, g)), blk],
        out_specs=[blk, pl.BlockSpec((CONV_W, LANE), lambda g: (0, g)), pl.BlockSpec((1, LANE), lambda g: (0, g))],
        out_shape=[jax.ShapeDtypeStruct((L, BW), F32), jax.ShapeDtypeStruct((CONV_W, BW), F32),
                   jax.ShapeDtypeStruct((1, BW), F32)],
        compiler_params=pltpu.CompilerParams(vmem_limit_bytes=VMEM_LIMIT),
    )(proj, w, dxc)


def loss_fwd_bwd(x, fw, target, L, tm):
    def fn(x, fw, t):
        err = jnp.square(_rms(x, fw) - t)
        return jnp.sum(0.5 * jnp.mean(err, axis=-1, keepdims=True), axis=0, keepdims=True)

    def body(x_ref, fw_ref, t_ref, l_ref, dx_ref, dfw_ref):
        i = pl.program_id(0)
        t = t_ref[...]
        val, vjp = jax.vjp(lambda x, fw: fn(x, fw, t), x_ref[...], fw_ref[...])
        dx, dfw = vjp(jnp.ones((1, 1), F32))
        dx_ref[...] = dx

        @pl.when(i == 0)
        def _():
            l_ref[...] = jnp.zeros_like(l_ref)
            dfw_ref[...] = jnp.zeros_like(dfw_ref)

        l_ref[...] += jnp.broadcast_to(val, l_ref.shape)
        dfw_ref[...] += dfw

    row = pl.BlockSpec((tm, D_MODEL), lambda i: (i, 0))
    vec = pl.BlockSpec((1, D_MODEL), lambda i: (0, 0))
    return pl.pallas_call(
        body, grid=(L // tm,), name="loss_fwd_bwd", in_specs=[row, vec, row],
        out_specs=[pl.BlockSpec((1, LANE), lambda i: (0, 0)), row, vec],
        out_shape=[jax.ShapeDtypeStruct((1, LANE), F32), jax.ShapeDtypeStruct((L, D_MODEL), F32),
                   jax.ShapeDtypeStruct((1, D_MODEL), F32)],
        compiler_params=pltpu.CompilerParams(vmem_limit_bytes=VMEM_LIMIT, dimension_semantics=("arbitrary",)),
    )(x, fw, target)


def adamw(w, g, m, v):
    rows, cols = w.shape
    tr = _row_tile(rows, cols, budget=1024 * 1024)
    c1 = 1.0 - ADAM_B1 ** ADAM_STEP
    c2 = 1.0 - ADAM_B2 ** ADAM_STEP

    def body(w_ref, g_ref, m_ref, v_ref, d_ref, nm_ref, nv_ref):
        g = g_ref[...]
        nm = ADAM_B1 * m_ref[...] + (1.0 - ADAM_B1) * g
        nv = ADAM_B2 * v_ref[...] + (1.0 - ADAM_B2) * jnp.square(g)
        d_ref[...] = -ADAM_LR * ((nm / c1) / (jnp.sqrt(nv / c2) + ADAM_EPS) + ADAM_WD * w_ref[...])
        nm_ref[...] = nm
        nv_ref[...] = nv

    blk = pl.BlockSpec((tr, cols), lambda i: (i, 0))
    return pl.pallas_call(
        body, grid=(rows // tr,), name="adamw", in_specs=[blk] * 4, out_specs=[blk] * 3,
        out_shape=[jax.ShapeDtypeStruct((rows, cols), F32)] * 3,
    )(w, g, m, v)


def s5_prep(lam_re, lam_im, log_dt, b_re, b_im, c_re, c_im):
    lr = jnp.minimum(lam_re, -1e-4)
    li = lam_im
    dt = jnp.exp(log_dt)[:, None]
    mag = jnp.exp(lr * dt)
    ar = mag * jnp.cos(li * dt)
    ai = mag * jnp.sin(li * dt)
    den = lr * lr + li * li
    fr = ((ar - 1.0) * lr + ai * li) / den
    fi = (ai * lr - (ar - 1.0) * li) / den
    bbr = fr[..., None] * b_re - fi[..., None] * b_im
    bbi = fr[..., None] * b_im + fi[..., None] * b_re
    emb_b = lambda bb: _block_diag(bb.transpose(0, 2, 1).reshape(BW, S5_STATE), S5_GROUPS)
    emb_c = lambda cc: _block_diag(cc.transpose(0, 2, 1).reshape(S5_N, S5_GROUP), S5_GROUPS)
    bmat = jnp.concatenate([emb_b(bbr), emb_b(bbi)], axis=1)
    cmat = jnp.concatenate([emb_c(c_re), -emb_c(c_im)], axis=0)
    nb = S5_N // LANE
    return ar.reshape(nb, 1, LANE), ai.reshape(nb, 1, LANE), bmat, cmat


def _block_diag(stacked, groups):
    rows, c = stacked.shape
    r = rows // groups
    row_g = jnp.arange(rows)[:, None] // r
    col_g = jnp.arange(groups * c)[None, :] // c
    return jnp.where(row_g == col_g, jnp.tile(stacked, (1, groups)), 0.0)


def rg_prep(w):
    return _block_diag(w.reshape(BW, RG_BLOCK), RG_BLOCKS)


def hg_prep(logits):
    p = jax.nn.softmax(logits, axis=0)
    return jnp.cumsum(p, axis=0) - p[0]


def _head_mean_matrix():
    r = jnp.arange(BW) // HG_D
    return (r[:, None] == r[None, :]).astype(F32) / HG_D


def _to_segment_order(a):
    L = a.shape[0]
    return a.reshape(NSEG, L // NSEG, -1).transpose(1, 0, 2).reshape(a.shape)


def _to_time_order(a):
    L = a.shape[0]
    return a.reshape(L // NSEG, NSEG, -1).transpose(1, 0, 2).reshape(a.shape)


def _const(*idx):
    return lambda s, i: idx


def _rows(cb=0):
    return lambda s, i: (i, cb)


def _sum_parts(name, first, parts, shape):
    return add_n(name, [(first, ())] + [(parts, (s,)) for s in range(NSH)], shape)


def _ffn_weight_specs(l, j):
    F = D_FF // NSH
    one = pl.Buffered(1)
    return [pl.BlockSpec((None, NSH, D_MODEL, F), lambda i: (j, 0, 0, 0), pipeline_mode=one),
            pl.BlockSpec((None, NSH, D_MODEL, F), lambda i: (j, 0, 0, 0), pipeline_mode=one),
            pl.BlockSpec((None, NSH, F, D_MODEL), lambda i: (j, 0, 0, 0), pipeline_mode=one)]


def ffn_fwd(name, x, W, l, j, k, L, tm):
    D, F = D_MODEL, D_FF // NSH

    def body(x_ref, nw_ref, wg_ref, wu_ref, wd_ref, y_ref, g_ref, u_ref):
        x = x_ref[...]
        h = _rms(x, nw_ref[...]).astype(MMT)
        y = x
        for s in range(NSH):
            g = _dg(h, wg_ref[s], 1, 0)
            u = _dg(h, wu_ref[s], 1, 0)
            g_ref[s] = g.astype(g_ref.dtype)
            u_ref[s] = u.astype(u_ref.dtype)
            y = y + 0.5 * _dg((jax.nn.silu(g) * u).astype(MMT), wd_ref[s], 1, 0)
        y_ref[...] = y

    row = pl.BlockSpec((tm, D), lambda i: (i, 0))
    act = pl.BlockSpec((NSH, tm, F), lambda i: (0, i, 0))
    return pl.pallas_call(
        body, grid=(L // tm,), name=name,
        in_specs=[row, pl.BlockSpec((None, None, 1, D), lambda i: (l, k, 0, 0))] + _ffn_weight_specs(l, j),
        out_specs=[row, act, act],
        out_shape=[jax.ShapeDtypeStruct((L, D), F32), jax.ShapeDtypeStruct((NSH, L, F), MMT),
                   jax.ShapeDtypeStruct((NSH, L, F), MMT)],
        compiler_params=pltpu.CompilerParams(vmem_limit_bytes=VMEM_LIMIT, dimension_semantics=("arbitrary",)),
    )(x, W["nw"], W["L"][l]["wg"], W["L"][l]["wu"], W["L"][l]["wd"])


def ffn_bwd(name, x, g, u, dy, W, bufs, l, j, k, L, tm):
    D, F = D_MODEL, D_FF // NSH
    tm = min(TM_WGRAD, L)

    def body(x_ref, nw_ref, dy_ref, g_ref, u_ref, wg_ref, wu_ref, wd_ref, *rest):
        part_ref, dnw_ref, dwg_ref, dwu_ref, dwd_ref = rest[-5:]
        s, i = pl.program_id(0), pl.program_id(1)
        x, nw = x_ref[...], nw_ref[...]
        r = lax.rsqrt(jnp.mean(x * x, axis=-1, keepdims=True) + EPS)
        xhat = x * r
        h = (xhat * nw).astype(MMT)
        half_dy = (0.5 * dy_ref[...]).astype(MMT)
        gs, us = g_ref[...].astype(F32), u_ref[...].astype(F32)
        sig = jax.nn.sigmoid(gs)
        act = gs * sig
        da = _dg(half_dy, wd_ref[...], 1, 1)
        du = (da * act).astype(MMT)
        dg = (da * us * (sig * (1.0 + gs * (1.0 - sig)))).astype(MMT)
        dh = _dg(dg, wg_ref[...], 1, 1) + _dg(du, wu_ref[...], 1, 1)
        dxh = dh * nw
        part_ref[...] = r * (dxh - xhat * jnp.mean(dxh * xhat, axis=-1, keepdims=True))
        grads = (_dg(h, dg, 0, 0), _dg(h, du, 0, 0), _dg((act * us).astype(MMT), half_dy, 0, 0))
        dnw = jnp.sum(dh * xhat, axis=0, keepdims=True)
        first = jnp.logical_and(s == 0, i == 0)
        for ref, val, start in zip((dwg_ref, dwu_ref, dwd_ref, dnw_ref), grads + (dnw,), (i == 0, i == 0, i == 0, first)):
            @pl.when(start)
            def _(ref=ref, val=val):
                ref[...] = val

            @pl.when(jnp.logical_not(start))
            def _(ref=ref, val=val):
                ref[...] += val

    row = pl.BlockSpec((tm, D), lambda s, i: (i, 0))
    act = pl.BlockSpec((None, tm, F), lambda s, i: (s, i, 0))
    wsp = lambda r, c: pl.BlockSpec((None, None, r, c), lambda s, i: (j, s, 0, 0))
    gsp = lambda r, c: pl.BlockSpec((None, None, r, c), lambda s, i: (0, s, 0, 0))
    part, dnw, bufs[("ffn_gate", l, j)], bufs[("ffn_up", l, j)], bufs[("ffn_down", l, j)] = pl.pallas_call(
        body, grid=(NSH, L // tm), name=name,
        in_specs=[row, pl.BlockSpec((None, None, 1, D), lambda s, i: (l, k, 0, 0)), row, act, act,
                  wsp(D, F), wsp(D, F), wsp(F, D)],
        out_specs=[pl.BlockSpec((None, tm, D), lambda s, i: (s, i, 0)), pl.BlockSpec((1, D), lambda s, i: (0, 0)),
                   gsp(D, F), gsp(D, F), gsp(F, D)],
        out_shape=[jax.ShapeDtypeStruct((NSH, L, D), F32), jax.ShapeDtypeStruct((1, D), F32)]
        + [jax.ShapeDtypeStruct((1, NSH, D, F), F32)] * 2 + [jax.ShapeDtypeStruct((1, NSH, F, D), F32)],
        compiler_params=pltpu.CompilerParams(vmem_limit_bytes=VMEM_LIMIT, dimension_semantics=("arbitrary", "arbitrary")),
    )(x, W["nw"], dy, g, u, W["L"][l]["wg"], W["L"][l]["wu"], W["L"][l]["wd"])
    return _sum_parts(name + "_dx", dy, part, (L, D)), dnw


def layer_fwd(l, x0, W, P, L, tm):
    D = D_MODEL
    tmm = tm
    tm = min(TM_FWD, L)
    n_i = L // tm
    x1, g0, u0 = ffn_fwd(f"ffn_fwd_{l}0", x0, W, l, 0, 0, L, tm)
    proj = tile_fwd(
        lambda x, nw, win, s: pre_core(x, nw, win), f"pre_fwd_{l}", n_i, NSH,
        [(x1, (tm, D), _rows()), (W["nw"], (None, None, 1, D), _const(l, 1, 0, 0)),
         (W["L"][l]["win"], (None, D, IN_TOTAL // NSH), lambda s, i: (s, 0, 0))],
        [((L, IN_TOTAL), F32, (tm, IN_TOTAL // NSH), lambda s, i: (i, s), False)], s_outer=True)[0]
    nb = S5_N // LANE
    blk3 = lambda s, i: (0, i, 0)
    bur, bui = tile_fwd(
        lambda u, bmat, s: s5_pre_core(u, bmat), f"s5pre_fwd_{l}", n_i, 1,
        [(proj, (tm, BW), _rows(0)), (P["bmat"], (None, BW, 2 * S5_N), _const(l, 0, 0))],
        [((nb, L, LANE), F32, (nb, tm, LANE), blk3, False)] * 2)
    xr, xi = s5_scan_fwd(bur, bui, P["ar"][l], P["ai"][l], L)
    qzv = _to_time_order(proj[:, BW:4 * BW])
    o_t, sst = hg_fwd(qzv, P["lb"][l], L)
    o = _to_segment_order(o_t)
    xc = conv_fwd(proj, W["convw"][l], P["convb"][l], L)
    vec = (None, 1, BW)
    a, b = tile_fwd(
        lambda xc, wa, ba, wx, bx, lam, s: gates_core(xc, wa, ba, wx, bx, lam), f"gates_fwd_{l}", n_i, 1,
        [(xc, (tm, BW), _rows()), (P["wa"], (None, BW, BW), _const(l, 0, 0)), (P["ba"], vec, _const(l, 0, 0)),
         (P["wx"], (None, BW, BW), _const(l, 0, 0)), (P["bx"], vec, _const(l, 0, 0)), (P["lam"], vec, _const(l, 0, 0))],
        [((L, BW), F32, (tm, BW), _rows(), False)] * 2)
    hs = rg_scan_fwd(a, b, L)
    ya, yb, yc = tile_fwd(
        lambda *a: mid_core(*a[:-1]), f"mid_fwd_{l}", L // tmm, 1,
        [(xr, (nb, tmm, LANE), blk3), (xi, (nb, tmm, LANE), blk3), (proj, (tmm, BW), _rows(0)), (o, (tmm, BW), _rows()),
         (proj, (tmm, BW), _rows(4)), (hs, (tmm, BW), _rows()), (proj, (tmm, BW), _rows(6)),
         (P["hmat"], (BW, BW), _const(0, 0)), (P["cmat"], (None, 2 * S5_N, BW), _const(l, 0, 0)), (P["d"], vec, _const(l, 0, 0)),
         (W["L"][l]["gluw"], (BW, BW), _const(0, 0)), (P["glub"], vec, _const(l, 0, 0)), (P["hgw"], vec, _const(l, 0, 0))],
        [((L, BW), F32, (tmm, BW), _rows(), False)] * 3)
    x2 = tile_fwd(
        lambda x, *rest: (x + merge_core(*rest[:-1])[0],), f"merge_fwd_{l}", n_i, 1,
        [(x1, (tm, D), _rows()), (ya, (tm, BW), _rows()), (yb, (tm, BW), _rows()), (yc, (tm, BW), _rows())]
        + [(proj, (tm, BW), _rows(7 + k)) for k in range(6)]
        + [(W["L"][l]["pfull"], (3, BW, D), _const(0, 0, 0)), (W["L"][l]["woutfull"], (D, D), _const(0, 0))],
        [((L, D), F32, (tm, D), _rows(), False)])[0]
    x3, g1, u1 = ffn_fwd(f"ffn_fwd_{l}1", x2, W, l, 1, 2, L, tm)
    saved = dict(x0=x0, x1=x1, x2=x2, proj=proj, xr=xr, xi=xi, o=o, sst=sst, xc=xc, a=a, hs=hs, ya=ya, yb=yb, yc=yc,
                 qzv=qzv, g0=g0, u0=u0, g1=g1, u1=u1)
    return x3, saved


def layer_bwd(l, dx3, sv, W, P, bufs, L, tm, ready=lambda l, group: None):
    D = D_MODEL
    n_i = L // tm
    nb = S5_N // LANE
    dq = D // NSH
    vec = (None, 1, BW)
    vout = ((1, BW), (1, BW), _const(0, 0), "acc_all")
    blk3 = lambda s, i: (0, i, 0)
    small = {}
    proj = sv["proj"]

    dx2, dnw2 = ffn_bwd(f"ffn_bwd_{l}1", sv["x2"], sv["g1"], sv["u1"], dx3, W, bufs, l, 1, 2, L, tm)
    ready(l, "ffn1")

    rw256 = ((L, BW), (tm, BW), _rows(), "write")
    res = tile_bwd(
        merge_core, f"merge_bwd_{l}", n_i, 1,
        [(sv["ya"], (tm, BW), _rows(), "r"), (sv["yb"], (tm, BW), _rows(), "r"), (sv["yc"], (tm, BW), _rows(), "r")]
        + [(proj, (tm, BW), _rows(7 + k), "r") for k in range(6)]
        + [(W["L"][l]["pfull"], (3, BW, D), _const(0, 0, 0), "w"), (W["L"][l]["woutfull"], (D, D), _const(0, 0), "w")],
        [(dx2, (tm, D), _rows())],
        [rw256] * 9
        + [((3, BW, D), (3, BW, D), _const(0, 0, 0), "acc_all"), ((D, D), (D, D), _const(0, 0), "acc_all")])
    dya, dyb, dyc = res[:3]
    dgm = res[3:9]
    bufs[("branch_proj", l)], bufs[("w_out", l)] = res[9:]
    ready(l, "merge")

    tmm = tm
    rw = ((L, BW), (tmm, BW), _rows(), "write")
    xw = ((nb, L, LANE), (nb, tmm, LANE), blk3, "write")
    res = tile_bwd(
        mid_core, f"mid_bwd_{l}", L // tmm, 1,
        [(sv["xr"], (nb, tmm, LANE), blk3, "r"), (sv["xi"], (nb, tmm, LANE), blk3, "r"), (proj, (tmm, BW), _rows(0), "r"),
         (sv["o"], (tmm, BW), _rows(), "r"), (proj, (tmm, BW), _rows(4), "r"), (sv["hs"], (tmm, BW), _rows(), "r"),
         (proj, (tmm, BW), _rows(6), "r"), (P["hmat"], (BW, BW), _const(0, 0), "c"),
         (P["cmat"], (None, 2 * S5_N, BW), _const(l, 0, 0), "w"), (P["d"], vec, _const(l, 0, 0), "p"),
         (W["L"][l]["gluw"], (BW, BW), _const(0, 0), "w"), (P["glub"], vec, _const(l, 0, 0), "p"),
         (P["hgw"], vec, _const(l, 0, 0), "p")],
        [(dya, (tmm, BW), _rows()), (dyb, (tmm, BW), _rows()), (dyc, (tmm, BW), _rows())],
        [xw, xw, rw, rw, rw, rw, rw,
         ((DEPTH, 2 * S5_N, BW), (None, 2 * S5_N, BW), _const(l, 0, 0), "acc_all", bufs.get("cmat")), vout,
         ((BW, BW), (BW, BW), _const(0, 0), "acc_all"), vout, vout])
    dxr, dxi, du_skip, do, dg_b, dhs, dgate_c, bufs["cmat"], dd, bufs[("s5_glu_w", l)], dglub, dhgw = res
    small["s5_d"], small["s5_glu_b"], small["hg_norm_w"] = dd[0], dglub[0], dhgw[0]
    ready(l, "mid")

    da, db = rg_scan_bwd(sv["a"], sv["hs"], dhs, L)
    wmat = lambda key: ((DEPTH, BW, BW), (None, BW, BW), _const(l, 0, 0), "acc_all", bufs.get(key))
    res = tile_bwd(
        gates_core, f"gates_bwd_{l}", n_i, 1,
        [(sv["xc"], (tm, BW), _rows(), "r"), (P["wa"], (None, BW, BW), _const(l, 0, 0), "w"), (P["ba"], vec, _const(l, 0, 0), "p"),
         (P["wx"], (None, BW, BW), _const(l, 0, 0), "w"), (P["bx"], vec, _const(l, 0, 0), "p"), (P["lam"], vec, _const(l, 0, 0), "p")],
        [(da, (tm, BW), _rows()), (db, (tm, BW), _rows())],
        [((L, BW), (tm, BW), _rows(), "write"), wmat("wa"), vout, wmat("wx"), vout, vout])
    dxc, bufs["wa"], dba, bufs["wx"], dbx, dlam = res
    small["rg_ba"], small["rg_bx"], small["rg_lambda"] = dba[0], dbx[0], dlam[0]
    dx_c, dconvw, dconvb = conv_bwd(proj, W["convw"][l], dxc, L)
    small["rg_conv_w"], small["rg_conv_b"] = dconvw, dconvb[0]

    dq_b, dz_b, dv_b, dlb = hg_bwd(sv["qzv"], P["lb"][l], sv["sst"], _to_time_order(do), L)
    dq_b, dz_b, dv_b = [_to_segment_order(a) for a in (dq_b, dz_b, dv_b)]

    gr, gi, dar, dai = s5_scan_bwd(dxr, dxi, sv["xr"], sv["xi"], P["ar"][l], P["ai"][l], L)
    du_pre, bufs["bmat"] = tile_bwd(
        s5_pre_core, f"s5pre_bwd_{l}", n_i, 1,
        [(proj, (tm, BW), _rows(0), "r"), (P["bmat"], (None, BW, 2 * S5_N), _const(l, 0, 0), "w")],
        [(gr, (nb, tm, LANE), blk3), (gi, (nb, tm, LANE), blk3)],
        [((L, BW), (tm, BW), _rows(), "write"),
         ((DEPTH, BW, 2 * S5_N), (None, BW, 2 * S5_N), _const(l, 0, 0), "acc_all", bufs.get("bmat"))])
    du_a = add_n(f"du_a_{l}", [(du_skip, ()), (du_pre, ())], (L, BW))
    prep_ct = dict(dar=dar, dai=dai, dlb=dlb)

    pieces = [du_a, dq_b, dz_b, dv_b, dg_b, dx_c, dgate_c, *dgm]
    per_piece, per_shard = BW // LANE, IN_TOTAL // NSH // LANE
    part, dnw1 = None, []
    tmw = min(TM_WGRAD, L)
    for s in range(NSH):
        groups = [(pieces[g // per_piece], (tmw, LANE), _rows(g % per_piece))
                  for g in range(s * per_shard, (s + 1) * per_shard)]
        part, dnw_s, bufs[("w_in", l)] = tile_bwd(
            pre_core, f"pre_bwd_{l}{s}", L // tmw, 1,
            [(sv["x1"], (tmw, D), _rows(), "r"), (W["nw"], (None, None, 1, D), _const(l, 1, 0, 0), "p"),
             (W["L"][l]["win"], (None, D, IN_TOTAL // NSH), _const(s, 0, 0), "w")],
            [groups],
            [((NSH, L, D), (None, tmw, D), functools.partial(lambda _s, i, s: (s, i, 0), s=s), "write", part),
             ((1, D), (1, D), _const(0, 0), "acc_all"),
             ((1, NSH, D, IN_TOTAL // NSH), (None, None, D, IN_TOTAL // NSH), _const(0, s, 0, 0), "acc_all",
              bufs.get(("w_in", l)))])
        dnw1.append(dnw_s)
    dnw1 = (dnw1[0] + dnw1[1]) + (dnw1[2] + dnw1[3])
    dx1 = _sum_parts(f"pre_bwd_{l}_dx", dx2, part, (L, D))
    ready(l, "pre")

    dx0, dnw0 = ffn_bwd(f"ffn_bwd_{l}0", sv["x0"], sv["g0"], sv["u0"], dx1, W, bufs, l, 0, 0, L, tm)
    ready(l, "ffn0")
    small["norm_w"] = jnp.concatenate([dnw0, dnw1, dnw2], axis=0)
    return dx0, small, prep_ct


SMALL_RAW = ("s5_lambda_re", "s5_lambda_im", "s5_log_dt", "s5_b_re", "s5_b_im", "s5_c_re", "s5_c_im", "s5_d", "s5_glu_b",
             "hg_lb_logits", "hg_norm_w", "rg_conv_b", "rg_wa", "rg_ba", "rg_wx", "rg_bx", "rg_lambda", "final_norm_w")
DEPTH = 2


def local_step(x, target, W, raw, layer_weights=None, layer_grads=None):
    L = x.shape[0]
    tm = min(256, L)
    col = lambda v: v.reshape(DEPTH, 1, BW)
    (ar, ai, bmat, cmat), s5_vjp = jax.vjp(jax.vmap(s5_prep), *[raw[k] for k in SMALL_RAW[:7]])
    (wa, wx), rg_vjp = jax.vjp(lambda a, b: (jax.vmap(rg_prep)(a), jax.vmap(rg_prep)(b)), raw["rg_wa"], raw["rg_wx"])
    lb, hg_vjp = jax.vjp(hg_prep, raw["hg_lb_logits"])
    P = dict(
        ar=[ar[l] for l in range(DEPTH)], ai=[ai[l] for l in range(DEPTH)],
        bmat=bmat.astype(MMT), cmat=cmat.astype(MMT), wa=wa.astype(MMT), wx=wx.astype(MMT),
        lb=[lb[l].reshape(1, BW) for l in range(DEPTH)], convb=[raw["rg_conv_b"][l].reshape(1, BW) for l in range(DEPTH)],
        ba=col(raw["rg_ba"]), bx=col(raw["rg_bx"]), lam=col(raw["rg_lambda"]), d=col(raw["s5_d"]),
        glub=col(raw["s5_glu_b"]), hgw=col(raw["hg_norm_w"]), hmat=_head_mean_matrix())

    saved = []
    h = _to_segment_order(x)
    for l in range(DEPTH):
        if layer_weights is not None:
            W["L"][l], h = layer_weights(l, h)
        h, sv = layer_fwd(l, h, W, P, L, tm)
        saved.append(sv)
    loss, dh, dfw = loss_fwd_bwd(h, raw["final_norm_w"].reshape(1, D_MODEL), _to_segment_order(target), L, tm)

    big, per_layer, prep_cts = {}, [None] * DEPTH, [None] * DEPTH
    ready = (lambda l, group: None) if layer_grads is None else (lambda l, group: layer_grads(l, group, big))
    for l in reversed(range(DEPTH)):
        dh, sm, pc = layer_bwd(l, dh, saved[l], W, P, big, L, tm, ready)
        per_layer[l], prep_cts[l] = sm, pc
    dh = _to_time_order(dh)

    small = {k: jnp.stack([per_layer[l][k] for l in range(DEPTH)]) for k in per_layer[0]}
    both = lambda k: jnp.stack([prep_cts[l][k] for l in range(DEPTH)])
    s5_g = s5_vjp((both("dar"), both("dai"), big.pop("bmat"), big.pop("cmat")))
    small.update(zip(SMALL_RAW[:7], s5_g))
    small["rg_wa"], small["rg_wx"] = rg_vjp((big.pop("wa"), big.pop("wx")))
    (small["hg_lb_logits"],) = hg_vjp(jnp.concatenate([prep_cts[l]["dlb"] for l in range(DEPTH)], axis=0))
    small["final_norm_w"] = dfw[0]
    return loss, dh, big, small


ANY = pl.BlockSpec(memory_space=pl.ANY)


def _place():
    x, y, c = lax.axis_index("x"), lax.axis_index("y"), lax.axis_index("c")
    chips = [(1 - x, y), (x, 1 - y), (1 - x, 1 - y)]
    return x, y, c, chips


def _remote(src, dst, send, recv, k, to):
    return pltpu.make_async_remote_copy(src_ref=src, dst_ref=dst, send_sem=send.at[k], recv_sem=recv.at[k],
                                        device_id=to, device_id_type=MESH)


def _comm_call(body, name, ins, out_shapes, n_sem, n_loc):
    return pl.pallas_call(
        body, name=name, in_specs=[ANY] * len(ins), out_specs=[ANY] * len(out_shapes), out_shape=out_shapes,
        scratch_shapes=[pltpu.SemaphoreType.DMA((n_sem,)), pltpu.SemaphoreType.DMA((n_sem,)),
                        pltpu.SemaphoreType.DMA((max(n_loc, 1),))],
    )(*ins)


def gather_shards(name, shards):
    n = len(shards)
    per = 8

    def body(*refs):
        ins, outs = refs[:n], refs[n:2 * n]
        send, recv, _ = refs[2 * n:]
        x, y, c, chips = _place()
        me = 2 * x + y
        sib = (x, y, 1 - c)
        sends = []
        for w in range(n):
            for j, (cx, cy) in enumerate(chips):
                cp = _remote(ins[w].at[c], outs[w].at[c, me], send, recv, per * w + j, (cx, cy, c))
                cp.start()
                sends.append(cp)
        for w in range(n):
            for l in range(2):
                cp = _remote(ins[w].at[l], outs[w].at[l, me], send, recv, per * w + 6 + l, sib)
                cp.start()
                sends.append(cp)
        for w in range(n):
            for j, (cx, cy) in enumerate(chips):
                theirs = outs[w].at[c, 2 * cx + cy]
                _remote(ins[w].at[c], theirs, send, recv, per * w + j, (cx, cy, c)).wait_recv()
                cp = _remote(theirs, theirs, send, recv, per * w + 3 + j, sib)
                cp.start()
                sends.append(cp)
        for w in range(n):
            for j, (cx, cy) in enumerate(chips):
                dst = outs[w].at[1 - c, 2 * cx + cy]
                _remote(dst, dst, send, recv, per * w + 3 + j, sib).wait_recv()
            for l in range(2):
                dst = outs[w].at[l, me]
                _remote(dst, dst, send, recv, per * w + 6 + l, sib).wait_recv()
        for cp in sends:
            cp.wait_send()

    shapes = [jax.ShapeDtypeStruct((2, NSH) + s.shape[1:], s.dtype) for s in shards]
    return _comm_call(body, name, shards, shapes, per * n, 0)


def exchange_halves(name, grads, ranges):
    n = len(grads)

    def body(*refs):
        ins, outs = refs[:n], refs[n:2 * n]
        send, recv, _ = refs[2 * n:]
        x, y, c, _chips = _place()
        cps = []
        for w in range(n):
            h = grads[w].shape[2] // 2
            p0, np_ = ranges[w]
            cp = _remote(ins[w].at[pl.ds(p0, np_), :, pl.ds((1 - c) * h, h)], outs[w], send, recv, w, (x, y, 1 - c))
            cp.start()
            cps.append(cp)
        for cp in cps:
            cp.wait()

    shapes = [jax.ShapeDtypeStruct((r[1], NSH, g.shape[2] // 2, g.shape[3]), g.dtype) for g, r in zip(grads, ranges)]
    return _comm_call(body, name, grads, shapes, n, 0)


def scatter_to_chips(name, halves):
    n = len(halves)

    def body(*refs):
        ins, outs = refs[:n], refs[n:2 * n]
        send, recv, _ = refs[2 * n:]
        x, y, c, chips = _place()
        cps = []
        for w in range(n):
            for j, (cx, cy) in enumerate(chips):
                cp = _remote(ins[w].at[:, 2 * cx + cy], outs[w].at[j], send, recv, 3 * w + j, (cx, cy, c))
                cp.start()
                cps.append(cp)
        for cp in cps:
            cp.wait()

    shapes = [jax.ShapeDtypeStruct((3, h.shape[0]) + h.shape[2:], h.dtype) for h in halves]
    return _comm_call(body, name, halves, shapes, 3 * n, 0)


def share_halves(name, pieces):
    n = len(pieces)

    def body(*refs):
        ins, outs = refs[:n], refs[n:2 * n]
        send, recv, _ = refs[2 * n:]
        x, y, c, _chips = _place()
        cps = []
        for w in range(n):
            cp = _remote(ins[w], outs[w], send, recv, w, (x, y, 1 - c))
            cp.start()
            cps.append(cp)
        for cp in cps:
            cp.wait()

    return _comm_call(body, name, pieces, [jax.ShapeDtypeStruct(p.shape, p.dtype) for p in pieces], n, 0)


def add_own_half(name, g, ra, c, wire, b0):
    nblk, h, cols = ra.shape
    tr = _row_tile(h, cols, mult=16)
    nt = h // tr

    def body(c_ref, g_ref, r_ref, o_ref):
        o_ref[...] = (g_ref[...] + r_ref[...]).astype(o_ref.dtype)

    blk = (None, tr, cols)
    return pl.pallas_call(
        body, name=name,
        grid_spec=pltpu.PrefetchScalarGridSpec(
            num_scalar_prefetch=1, grid=(nblk, nt),
            in_specs=[pl.BlockSpec(blk, lambda s, i, c_ref: (b0 + s, c_ref[0] * nt + i, 0)), pl.BlockSpec(blk, lambda s, i, c_ref: (s, i, 0))],
            out_specs=pl.BlockSpec(blk, lambda s, i, c_ref: (s, i, 0))),
        out_shape=jax.ShapeDtypeStruct(ra.shape, wire),
    )(c.reshape(1), g, ra)


def add_chips(name, hb, rb, me):
    npc, _, h, cols = hb.shape
    tr = _row_tile(h, cols, mult=16)

    def body(me_ref, h_ref, r0, r1, r2, o_ref):
        f = lambda r: r[...].astype(F32)
        o_ref[...] = ((f(h_ref) + f(r0)) + f(r1)) + f(r2)

    rspec = lambda j: pl.BlockSpec((None, None, tr, cols), functools.partial(lambda p, i, me_ref, j: (j, p, i, 0), j=j))
    return pl.pallas_call(
        body, name=name,
        grid_spec=pltpu.PrefetchScalarGridSpec(
            num_scalar_prefetch=1, grid=(npc, h // tr),
            in_specs=[pl.BlockSpec((None, None, tr, cols), lambda p, i, me_ref: (p, me_ref[0], i, 0)), rspec(0), rspec(1), rspec(2)],
            out_specs=pl.BlockSpec((None, tr, cols), lambda p, i, me_ref: (p, i, 0))),
        out_shape=jax.ShapeDtypeStruct((npc, h, cols), F32),
    )(me.reshape(1), hb, rb, rb, rb)


def adamw_halves(name, w, m, v, own, other, c):
    npc, rows, cols = w.shape
    h = rows // 2
    tr = _row_tile(h, cols, budget=1024 * 1024)
    nt = h // tr
    c1 = 1.0 - ADAM_B1 ** ADAM_STEP
    c2 = 1.0 - ADAM_B2 ** ADAM_STEP

    def body(c_ref, w_ref, m_ref, v_ref, own_ref, oth_ref, g_ref, d_ref, nm_ref, nv_ref):
        g = jnp.where(pl.program_id(1) == c_ref[0], own_ref[...], oth_ref[...])
        nm = ADAM_B1 * m_ref[...] + (1.0 - ADAM_B1) * g
        nv = ADAM_B2 * v_ref[...] + (1.0 - ADAM_B2) * jnp.square(g)
        g_ref[...] = g
        d_ref[...] = -ADAM_LR * ((nm / c1) / (jnp.sqrt(nv / c2) + ADAM_EPS) + ADAM_WD * w_ref[...])
        nm_ref[...] = nm
        nv_ref[...] = nv

    full = pl.BlockSpec((None, tr, cols), lambda p, hh, i, c_ref: (p, hh * nt + i, 0))
    half = pl.BlockSpec((None, tr, cols), lambda p, hh, i, c_ref: (p, i, 0))
    return pl.pallas_call(
        body, name=name,
        grid_spec=pltpu.PrefetchScalarGridSpec(
            num_scalar_prefetch=1, grid=(npc, 2, nt),
            in_specs=[full, full, full, half, half], out_specs=[full] * 4),
        out_shape=[jax.ShapeDtypeStruct(w.shape, F32)] * 4,
    )(c.reshape(1), w, m, v, own, other)


WEIGHTS = ("norm_w", "final_norm_w", "ffn_gate", "ffn_up", "ffn_down", "w_in", "branch_proj", "w_out", "s5_lambda_re",
           "s5_lambda_im", "s5_log_dt", "s5_b_re", "s5_b_im", "s5_c_re", "s5_c_im", "s5_d", "s5_glu_w", "s5_glu_b",
           "hg_lb_logits", "hg_norm_w", "rg_conv_w", "rg_conv_b", "rg_wa", "rg_ba", "rg_wx", "rg_bx", "rg_lambda")
BIG = ("ffn_gate", "ffn_up", "ffn_down", "w_in", "branch_proj", "w_out", "s5_glu_w")
SHARDED_SMALL = ("norm_w", "rg_conv_w")
SMALL = SMALL_RAW + SHARDED_SMALL


def _view2d(shape):
    return (1, shape[0]) if len(shape) == 1 else (math.prod(shape[:-1]), shape[-1])


def _small_layout(shapes, row_multiple):
    layout, at = [], 0
    for shape in shapes:
        r, c = _view2d(shape)
        rp = -(-r // 8) * 8
        layout.append((at, r, c, rp))
        at += rp * max(1, c // LANE)
    return layout, -(-at // row_multiple) * row_multiple


def pack_small(name, arrays, row_multiple):
    layout, rows = _small_layout([a.shape for a in arrays], row_multiple)

    def body(*refs):
        out = refs[-1]
        out[...] = jnp.zeros_like(out)
        for ref, (r0, r, c, rp) in zip(refs[:-1], layout):
            if c <= LANE:
                out[r0:r0 + r, 0:c] = ref[...]
            else:
                for q in range(c // LANE):
                    out[r0 + q * rp:r0 + q * rp + r, :] = ref[:, q * LANE:(q + 1) * LANE]

    return pl.pallas_call(
        body, name=name, out_shape=jax.ShapeDtypeStruct((rows, LANE), F32),
        compiler_params=pltpu.CompilerParams(vmem_limit_bytes=VMEM_LIMIT),
    )(*[a.reshape(_view2d(a.shape)) for a in arrays])


def unpack_small(name, packed, shapes):
    layout, _ = _small_layout(shapes, 8)

    def body(p_ref, *outs):
        for ref, (r0, r, c, rp) in zip(outs, layout):
            if c <= LANE:
                ref[...] = p_ref[r0:r0 + r, 0:c]
            else:
                for q in range(c // LANE):
                    ref[:, q * LANE:(q + 1) * LANE] = p_ref[r0 + q * rp:r0 + q * rp + r, :]

    res = pl.pallas_call(
        body, name=name, out_shape=[jax.ShapeDtypeStruct(_view2d(s), F32) for s in shapes],
        compiler_params=pltpu.CompilerParams(vmem_limit_bytes=VMEM_LIMIT),
    )(packed)
    return [a.reshape(s) for a, s in zip(res, shapes)]


HBM = pl.BlockSpec(memory_space=pltpu.HBM)
SEM = pl.BlockSpec(memory_space=pltpu.SEMAPHORE)
EFFECT = pltpu.SideEffectType.DATAFLOW_SIDE_EFFECTING


def split_start(name, srcs, land_shapes, plan, n_send, n_recv):
    ns, nl = len(srcs), len(land_shapes)

    def body(*refs):
        ins, lands = refs[:ns], refs[ns:ns + nl]
        send, recv = refs[ns + nl], refs[ns + nl + 1]
        for src, dst, ks, kr, dev in plan(ins, lands):
            pltpu.make_async_remote_copy(src_ref=src, dst_ref=dst, send_sem=send.at[ks], recv_sem=recv.at[kr],
                                         device_id=dev, device_id_type=MESH).start()
        refs[-1][...] = jnp.zeros_like(refs[-1])

    hbm = lambda a: pltpu.with_memory_space_constraint(a, pltpu.HBM)
    lands = [lax.empty(s.shape, s.dtype) for s in land_shapes]
    out = pl.pallas_call(
        body, name=name,
        out_shape=(pltpu.SemaphoreType.DMA((n_send,)), pltpu.SemaphoreType.DMA((n_recv,)),
                   *[pltpu.HBM(a.shape, a.dtype) for a in srcs], *[pltpu.HBM(s.shape, s.dtype) for s in land_shapes],
                   jax.ShapeDtypeStruct((8, LANE), F32)),
        in_specs=[HBM] * (ns + nl), out_specs=(SEM, SEM, *[HBM] * (ns + nl), pl.BlockSpec(memory_space=pltpu.VMEM)),
        input_output_aliases={k: 2 + k for k in range(ns + nl)},
        compiler_params=pltpu.CompilerParams(has_side_effects=EFFECT),
    )(*[hbm(a) for a in srcs], *[hbm(a) for a in lands])
    return out[:-1], out[-1]


def split_wait(name, handles, n_src, waits, after):
    send, recv, *bufs = handles
    nb = len(bufs)

    def body(*refs):
        ins, lands = refs[:n_src], refs[n_src:nb]
        send_sem, recv_sem = refs[nb], refs[nb + 1]
        x, y, c, _chips = _place()
        sends, recvs = waits(ins, lands)
        for src, k in sends:
            pltpu.make_async_remote_copy(src_ref=src, dst_ref=src, send_sem=send_sem.at[k], recv_sem=recv_sem.at[0],
                                         device_id=(x, y, 1 - c), device_id_type=MESH).wait_send()
        for dst, k in recvs:
            pltpu.make_async_remote_copy(src_ref=dst, dst_ref=dst, send_sem=send_sem.at[0], recv_sem=recv_sem.at[k],
                                         device_id=(x, y, 1 - c), device_id_type=MESH).wait_recv()

    out = pl.pallas_call(
        body, name=name, out_shape=tuple(pltpu.HBM(a.shape, a.dtype) for a in bufs),
        in_specs=[HBM] * nb + [SEM, SEM, ANY], out_specs=tuple([HBM] * nb),
        input_output_aliases={k: k for k in range(nb)},
        compiler_params=pltpu.CompilerParams(has_side_effects=EFFECT),
    )(*bufs, send, recv, after)
    return list(out[:n_src]), list(out[n_src:])


def gather_plan(n):
    def plan(ins, lands):
        x, y, c, chips = _place()
        me = 2 * x + y
        copies = []
        for w in range(n):
            for j, (cx, cy) in enumerate(chips):
                for t in range(2):
                    copies.append((ins[w].at[c], lands[w].at[c, me], 8 * w + 2 * j + t, 8 * w + 2 * j + c, (cx, cy, t)))
            for half in range(2):
                copies.append((ins[w].at[half], lands[w].at[half, me], 8 * w + 6 + half, 8 * w + 6 + half, (x, y, 1 - c)))
        return copies

    def waits(ins, lands):
        x, y, c, chips = _place()
        me = 2 * x + y
        sends, recvs = [], []
        for w in range(n):
            for j, (cx, cy) in enumerate(chips):
                for t in range(2):
                    sends.append((ins[w].at[c], 8 * w + 2 * j + t))
                    recvs.append((lands[w].at[t, 2 * cx + cy], 8 * w + 2 * j + t))
            for half in range(2):
                sends.append((ins[w].at[half], 8 * w + 6 + half))
                recvs.append((lands[w].at[half, me], 8 * w + 6 + half))
        return sends, recvs

    return plan, waits


def scatter_plan(n):
    def plan(ins, lands):
        x, y, c, chips = _place()
        return [(ins[w].at[:, 2 * cx + cy], lands[w].at[j], 3 * w + j, 3 * w + j, (cx, cy, c))
                for w in range(n) for j, (cx, cy) in enumerate(chips)]

    def waits(ins, lands):
        x, y, c, chips = _place()
        sends = [(ins[w].at[:, 2 * cx + cy], 3 * w + j) for w in range(n) for j, (cx, cy) in enumerate(chips)]
        recvs = [(lands[w].at[j], 3 * w + j) for w in range(n) for j in range(3)]
        return sends, recvs

    return plan, waits


def _layer_shards(w, l):
    return [w["ffn_gate"][l].astype(MMT), w["ffn_up"][l].astype(MMT), w["ffn_down"][l].astype(MMT),
            w["w_in"][l].reshape(2, D_MODEL // 2, -1).astype(MMT),
            w["branch_proj"][l].reshape(2, 3 * BW // 2, -1).astype(MMT),
            w["w_out"][l].reshape(2, -1, D_MODEL).astype(MMT),
            w["s5_glu_w"][l].reshape(2, -1, BW).astype(MMT)]


def _layer_weights(g):
    rows = lambda a: a.transpose(1, 0, 2, 3).reshape(NSH, -1, a.shape[-1])
    p = rows(g[4]).reshape(NSH, 3, BW, -1).transpose(1, 2, 0, 3).reshape(3, BW, D_MODEL)
    return dict(wg=g[0], wu=g[1], wd=g[2], win=rows(g[3]), pfull=p,
                woutfull=rows(g[5]).reshape(D_MODEL, D_MODEL), gluw=rows(g[6]).reshape(BW, BW))


GROUPS = {"ffn1": ("ffn_gate", "ffn_up", "ffn_down"), "merge": ("branch_proj", "w_out"), "mid": ("s5_glu_w",),
          "pre": ("w_in",), "ffn0": ("ffn_gate", "ffn_up", "ffn_down")}


def _grad_views(big, l, group):
    views = []
    for name in GROUPS[group]:
        if name == "branch_proj":
            dq = D_MODEL // NSH
            a = big[(name, l)].reshape(3, BW, NSH, dq).transpose(2, 0, 1, 3).reshape(1, NSH, 3 * BW, dq)
        elif name.startswith("ffn"):
            a = big[(name, l, 1 if group == "ffn1" else 0)]
        else:
            a = big[(name, l)]
            a = a.reshape(1, NSH, -1, a.shape[-1])
        views.append((name, a, 0))
    return views


def halves_plan(n):
    def src(ref, c):
        h = ref.shape[2] // 2
        return ref.at[:, :, pl.ds((1 - c) * h, h)]

    def plan(ins, lands):
        x, y, c, _chips = _place()
        return [(src(ins[w], c), lands[w], w, w, (x, y, 1 - c)) for w in range(n)]

    def waits(ins, lands):
        x, y, c, _chips = _place()
        return [(src(ins[w], c), w) for w in range(n)], [(lands[w], w) for w in range(n)]

    return plan, waits


def _reduce_to_halves(tag, views, c, wire):
    from_sibling = exchange_halves(f"reduce_cores_{tag}", [a for _, a, _ in views], [(p0, 1) for _, _, p0 in views])
    merge = lambda a: a.reshape((-1,) + a.shape[2:])
    return [add_own_half(f"sum_cores_{tag}_{i}", merge(a), merge(r), c, wire[i], NSH * p0).reshape(r.shape)
            for i, ((_, a, p0), r) in enumerate(zip(views, from_sibling))]


def _step(x, target, w, m, v):
    mx, my, mc = lax.axis_index("x"), lax.axis_index("y"), lax.axis_index("c")
    me = (2 * mx + my).astype(jnp.int32)
    mc = mc.astype(jnp.int32)

    W = dict(L=[None] * DEPTH)
    state = {"pending": []}
    n_big = len(BIG)
    g_plan, g_waits = gather_plan(n_big)

    def layer_weights(l, h):
        if l == 0:
            got = gather_shards("gather_weights_0", _layer_shards(w, 0) + [w[n] for n in SHARDED_SMALL])
            nxt = _layer_shards(w, 1)
            got, nxt = lax.optimization_barrier((got, nxt))
            shapes = [jax.ShapeDtypeStruct((2, NSH) + a.shape[1:], a.dtype) for a in nxt]
            state["gather"], token = split_start("gather_weights_1_start", nxt, shapes, g_plan, 8 * n_big, 8 * n_big)
            W["nw"] = got[n_big].transpose(0, 2, 1, 3).reshape(DEPTH, 3, 1, D_MODEL) + token[0, 0]
            W["convw"] = got[n_big + 1].transpose(0, 2, 1, 3).reshape(DEPTH, CONV_W, BW)
            return _layer_weights(got[:n_big]), h
        return _layer_weights(split_wait("gather_weights_1_wait", state["gather"], n_big, g_waits, h)[1]), h

    def to_chips(after):
        if "cores" not in state:
            return
        tag, names, l, group, handles, waits = state.pop("cores")
        sent, landed = split_wait(f"reduce_cores_{tag}_wait", handles, len(names), waits, after)
        merge = lambda a: a.reshape((-1,) + a.shape[2:])
        halves = [add_own_half(f"sum_cores_{tag}_{i}", merge(a), merge(r), mc, jnp.bfloat16, 0).reshape(r.shape)
                  for i, (a, r) in enumerate(zip(sent, landed))]
        shapes = [jax.ShapeDtypeStruct((3, a.shape[0]) + a.shape[2:], a.dtype) for a in halves]
        plan, waits = scatter_plan(len(halves))
        handles, token = split_start(f"reduce_chips_{tag}_start", halves, shapes, plan, 3 * len(halves), 3 * len(halves))
        W["nw"] = W["nw"] + token[0, 0]
        state["pending"].append((tag, names, l, group, handles, waits))

    def layer_grads(l, group, big):
        views = _grad_views(big, l, group)
        to_chips(views[0][1])
        if (l, group) == (0, "ffn0"):
            return
        tag = f"{l}_{group}"
        arrays = [a for _, a, _ in views]
        shapes = [jax.ShapeDtypeStruct((1, NSH, a.shape[2] // 2, a.shape[3]), a.dtype) for a in arrays]
        plan, waits = halves_plan(len(arrays))
        handles, token = split_start(f"reduce_cores_{tag}_start", arrays, shapes, plan, len(arrays), len(arrays))
        W["nw"] = W["nw"] + token[0, 0]
        state["cores"] = (tag, [name for name, _, _ in views], l, group, handles, waits)

    loss, dx, big, small = local_step(x[0], target[0], W, {k: w[k] for k in SMALL_RAW}, layer_weights, layer_grads)

    pieces = {n: {} for n in BIG}
    block_of = lambda name, l, group: (2 * l + (group == "ffn1")) if name.startswith("ffn") else l
    views = _grad_views(big, 0, "ffn0")
    small_packed = pack_small("pack_small_grads", [small[n] for n in SMALL], NSH * 32)
    halves = _reduce_to_halves("0_ffn0", views + [("small", small_packed.reshape(1, NSH, -1, LANE), 0)], mc,
                               [jnp.bfloat16] * len(views) + [F32])
    from_chips = scatter_to_chips("reduce_chips_0_ffn0", halves)
    last = [add_chips(f"sum_chips_0_ffn0_{i}", h, r, me) for i, (h, r) in enumerate(zip(halves, from_chips))]
    for (name, _, _), piece in zip(views, last):
        pieces[name][block_of(name, 0, "ffn0")] = piece
    after = dx
    for tag, names, l, group, handles, waits in state["pending"]:
        sent, landed = split_wait(f"reduce_chips_{tag}_wait", handles, len(names), waits, after)
        for i, (name, h, r) in enumerate(zip(names, sent, landed)):
            pieces[name][block_of(name, l, group)] = add_chips(f"sum_chips_{tag}_{i}", h, r, me)
    own = [jnp.concatenate([pieces[n][b] for b in sorted(pieces[n])], axis=0) for n in BIG] + [last[-1]]
    other = share_halves("reduce_share", own)

    g, delta, new_m, new_v = {}, {}, {}, {}
    for i, n in enumerate(BIG):
        view = lambda a: a.reshape(own[i].shape[0], -1, own[i].shape[2])
        res = adamw_halves(f"adamw_{n}", view(w[n]), view(m[n]), view(v[n]), own[i], other[i], mc)
        g[n], delta[n], new_m[n], new_v[n] = [a.reshape(w[n].shape) for a in res]

    piece = jnp.stack([jnp.where(mc == 0, own[-1][0], other[-1][0]), jnp.where(mc == 0, other[-1][0], own[-1][0])])
    (all_small,) = gather_shards("gather_small", [piece])
    full_small = unpack_small("unpack_small_grads", all_small.transpose(1, 0, 2, 3).reshape(-1, LANE),
                              [small[n].shape for n in SMALL])
    g.update(zip(SMALL, full_small))
    g["norm_w"] = lax.dynamic_slice_in_dim(g["norm_w"], me * (D_MODEL // NSH), D_MODEL // NSH, axis=2)
    g["rg_conv_w"] = lax.dynamic_slice_in_dim(g["rg_conv_w"], me * (BW // NSH), BW // NSH, axis=2)

    packed = [pack_small(f"pack_small_{tag}", [src[n] for n in SMALL], 8)
              for tag, src in (("w", w), ("g", g), ("m", m), ("v", v))]
    for tag, dst, flat in zip(("delta", "m", "v"), (delta, new_m, new_v), adamw(*packed)):
        dst.update(zip(SMALL, unpack_small(f"unpack_small_{tag}", flat, [w[n].shape for n in SMALL])))

    total = lax.psum(loss[0, 0], ("x", "y", "c"))
    return (total, dx[None], *[g[n] for n in WEIGHTS], *[delta[n] for n in WEIGHTS],
            *[new_m[n] for n in WEIGHTS], *[new_v[n] for n in WEIGHTS])


def kernel(x, norm_w, final_norm_w, ffn_gate, ffn_up, ffn_down, w_in, branch_proj, w_out, s5_lambda_re, s5_lambda_im, s5_log_dt, s5_b_re, s5_b_im, s5_c_re, s5_c_im, s5_d, s5_glu_w, s5_glu_b, hg_lb_logits, hg_norm_w, rg_conv_w, rg_conv_b, rg_wa, rg_ba, rg_wx, rg_bx, rg_lambda, loss_target, m_norm_w, m_final_norm_w, m_ffn_gate, m_ffn_up, m_ffn_down, m_w_in, m_branch_proj, m_w_out, m_s5_lambda_re, m_s5_lambda_im, m_s5_log_dt, m_s5_b_re, m_s5_b_im, m_s5_c_re, m_s5_c_im, m_s5_d, m_s5_glu_w, m_s5_glu_b, m_hg_lb_logits, m_hg_norm_w, m_rg_conv_w, m_rg_conv_b, m_rg_wa, m_rg_ba, m_rg_wx, m_rg_bx, m_rg_lambda, v_norm_w, v_final_norm_w, v_ffn_gate, v_ffn_up, v_ffn_down, v_w_in, v_branch_proj, v_w_out, v_s5_lambda_re, v_s5_lambda_im, v_s5_log_dt, v_s5_b_re, v_s5_b_im, v_s5_c_re, v_s5_c_im, v_s5_d, v_s5_glu_w, v_s5_glu_b, v_hg_lb_logits, v_hg_norm_w, v_rg_conv_w, v_rg_conv_b, v_rg_wa, v_rg_ba, v_rg_wx, v_rg_bx, v_rg_lambda):
    ws = (norm_w, final_norm_w, ffn_gate, ffn_up, ffn_down, w_in, branch_proj, w_out, s5_lambda_re, s5_lambda_im, s5_log_dt, s5_b_re, s5_b_im, s5_c_re, s5_c_im, s5_d, s5_glu_w, s5_glu_b, hg_lb_logits, hg_norm_w, rg_conv_w, rg_conv_b, rg_wa, rg_ba, rg_wx, rg_bx, rg_lambda)
    ms = (m_norm_w, m_final_norm_w, m_ffn_gate, m_ffn_up, m_ffn_down, m_w_in, m_branch_proj, m_w_out, m_s5_lambda_re, m_s5_lambda_im, m_s5_log_dt, m_s5_b_re, m_s5_b_im, m_s5_c_re, m_s5_c_im, m_s5_d, m_s5_glu_w, m_s5_glu_b, m_hg_lb_logits, m_hg_norm_w, m_rg_conv_w, m_rg_conv_b, m_rg_wa, m_rg_ba, m_rg_wx, m_rg_bx, m_rg_lambda)
    vs = (v_norm_w, v_final_norm_w, v_ffn_gate, v_ffn_up, v_ffn_down, v_w_in, v_branch_proj, v_w_out, v_s5_lambda_re, v_s5_lambda_im, v_s5_log_dt, v_s5_b_re, v_s5_b_im, v_s5_c_re, v_s5_c_im, v_s5_d, v_s5_glu_w, v_s5_glu_b, v_hg_lb_logits, v_hg_norm_w, v_rg_conv_w, v_rg_conv_b, v_rg_wa, v_rg_ba, v_rg_wx, v_rg_bx, v_rg_lambda)
    return _step(x, loss_target, dict(zip(WEIGHTS, ws)), dict(zip(WEIGHTS, ms)), dict(zip(WEIGHTS, vs)))
```

```python
import functools
import math
from typing import NamedTuple

import jax
import jax.numpy as jnp
from jax import lax
from jax.experimental import pallas as pl
from jax.experimental.pallas import tpu as pltpu

F32 = jnp.float32
MMT = jnp.bfloat16
HI = lax.Precision.HIGHEST

D_MODEL = 1024
BW = 512
S5_GROUP, S5_GROUPS, S5_STATE = 16, 32, 64
S5_N = S5_GROUPS * S5_STATE
HG_HEADS, HG_D = 4, 128
HG_CHUNK = 128
RG_BLOCKS, RG_BLOCK = 8, 64
RG_C = 8.0
CONV_W = 4
D_FF = 2816
EPS = 1e-6
IN_TOTAL = 6656
NSH = 4
NSEG = 8
LANE = 128
VMEM_LIMIT = 56 * 1024 * 1024
TM_FWD = 512
TM_WGRAD = 512

ADAM_LR, ADAM_B1, ADAM_B2, ADAM_EPS, ADAM_WD, ADAM_STEP = 0.001, 0.9, 0.999, 1e-08, 0.01, 10

MESH = pl.DeviceIdType.MESH


class WP(NamedTuple):
    w: jax.Array
    p: jax.Array


def _dg(a, b, ca, cb):
    return lax.dot_general(a, b, (((ca,), (cb,)), ((), ())), preferred_element_type=F32)


@jax.custom_vjp
def _mmw(a, w, p):
    return _dg(a.astype(MMT), w, 1, 0)


def _mmw_fwd(a, w, p):
    return _mmw(a, w, p), (a, w)


def _mmw_bwd(res, g):
    a, w = res
    gb = g.astype(MMT)
    return _dg(gb, w, 1, 1), jnp.zeros_like(w), _dg(a.astype(MMT), gb, 0, 0)


_mmw.defvjp(_mmw_fwd, _mmw_bwd)


def mm(a, w):
    if isinstance(w, WP):
        return _mmw(a, w.w, w.p)
    return _dg(a.astype(MMT), w, 1, 0)


@jax.custom_vjp
def mma_nn(a, b):
    return _dg(a.astype(MMT), b.astype(MMT), 1, 0)


def _nn_f(a, b):
    return mma_nn(a, b), (a, b)


def _nn_b(res, g):
    a, b = res
    gb = g.astype(MMT)
    return _dg(gb, b.astype(MMT), 1, 1), _dg(a.astype(MMT), gb, 0, 0)


mma_nn.defvjp(_nn_f, _nn_b)


@jax.custom_vjp
def mma_nt(a, b):
    return _dg(a.astype(MMT), b.astype(MMT), 1, 1)


def _nt_f(a, b):
    return mma_nt(a, b), (a, b)


def _nt_b(res, g):
    a, b = res
    gb = g.astype(MMT)
    return _dg(gb, b.astype(MMT), 1, 0), _dg(gb, a.astype(MMT), 0, 0)


mma_nt.defvjp(_nt_f, _nt_b)


@jax.custom_vjp
def mma_tn(a, b):
    return _dg(a.astype(MMT), b.astype(MMT), 0, 0)


def _tn_f(a, b):
    return mma_tn(a, b), (a, b)


def _tn_b(res, g):
    a, b = res
    gb = g.astype(MMT)
    return _dg(b.astype(MMT), gb, 1, 1), _dg(a.astype(MMT), gb, 1, 0)


mma_tn.defvjp(_tn_f, _tn_b)


def mm_exact(m, x):
    return jnp.dot(m, x, precision=HI, preferred_element_type=F32)


def _rms(x, w):
    return x * lax.rsqrt(jnp.mean(x * x, axis=-1, keepdims=True) + EPS) * w


def _expm1(x):
    series = x * (1.0 + x * (1.0 / 2) * (1.0 + x * (1.0 / 3) * (1.0 + x * (1.0 / 4) * (1.0 + x * (1.0 / 5) * (1.0 + x * (1.0 / 6))))))
    return jnp.where(jnp.abs(x) < 0.1, series, jnp.exp(x) - 1.0)


def _bspec(block, fn, order):
    if order == "is":
        return pl.BlockSpec(block, lambda i, s: fn(s, i))
    return pl.BlockSpec(block, lambda s, i: fn(s, i))


def tile_fwd(fn, name, n_i, n_s, ins, outs, s_outer=False):
    n_in = len(ins)
    order = "si" if s_outer else "is"
    assert not (s_outer and any(o[4] for o in outs))

    def body(*refs):
        s = pl.program_id(0 if s_outer else 1)
        res = fn(*[r[...] for r in refs[:n_in]], s)
        for o_ref, val, spec in zip(refs[n_in:], res, outs):
            if spec[4] and n_s > 1:
                @pl.when(s == 0)
                def _(o_ref=o_ref, val=val):
                    o_ref[...] = val.astype(o_ref.dtype)

                @pl.when(s != 0)
                def _(o_ref=o_ref, val=val):
                    o_ref[...] += val.astype(o_ref.dtype)
            else:
                o_ref[...] = val.astype(o_ref.dtype)

    return pl.pallas_call(
        body, grid=(n_s, n_i) if s_outer else (n_i, n_s), name=name,
        in_specs=[_bspec(b, f, order) for _, b, f in ins],
        out_specs=[_bspec(b, f, order) for _, _, b, f, _ in outs],
        out_shape=[jax.ShapeDtypeStruct(sh, dt) for sh, dt, _, _, _ in outs],
        compiler_params=pltpu.CompilerParams(vmem_limit_bytes=VMEM_LIMIT,
                                             dimension_semantics=("arbitrary", "arbitrary")),
    )(*[a for a, _, _ in ins])


def tile_bwd(fn, name, n_i, n_s, ins, cts, gouts):
    groups = [c if isinstance(c, list) else [c] for c in cts]
    cts = [blk for grp in groups for blk in grp]
    n_in, n_ct = len(ins), len(cts)
    kinds = [k for _, _, _, k in ins]
    d_pos = [j for j, k in enumerate(kinds) if k != "c"]
    shared = [(gi, spec[4]) for gi, spec in enumerate(gouts) if len(spec) == 5 and spec[4] is not None]
    n_sh = len(shared)

    def body(*refs):
        s, i = pl.program_id(0), pl.program_id(1)
        vals = [r[...] for r in refs[:n_in]]
        ct_refs, ctv = list(refs[n_in:n_in + n_ct]), []
        for grp in groups:
            parts = [ct_refs.pop(0)[...] for _ in grp]
            ctv.append(parts[0] if len(parts) == 1 else jnp.concatenate(parts, axis=1))
        ctv = tuple(ctv)
        g_refs = refs[n_in + n_ct + n_sh:]

        def g(*dv):
            args = list(vals)
            for j, v in zip(d_pos, dv):
                args[j] = WP(vals[j], v) if kinds[j] == "w" else v
            return tuple(fn(*args))

        dv0 = [jnp.zeros(vals[j].shape, F32) if kinds[j] == "w" else vals[j] for j in d_pos]
        _, vjp = jax.vjp(g, *dv0)
        grads = vjp(ctv)
        for g_ref, gv, spec in zip(g_refs, grads, gouts):
            mode = spec[3]
            if mode == "write":
                g_ref[...] = gv.astype(g_ref.dtype)
            else:
                first = (i == 0) if mode == "acc_i" else jnp.logical_and(i == 0, s == 0)

                @pl.when(first)
                def _(g_ref=g_ref, gv=gv):
                    g_ref[...] = gv.astype(g_ref.dtype)

                @pl.when(jnp.logical_not(first))
                def _(g_ref=g_ref, gv=gv):
                    g_ref[...] += gv.astype(g_ref.dtype)

    return pl.pallas_call(
        body, grid=(n_s, n_i), name=name,
        in_specs=([_bspec(b, f, "si") for _, b, f, _ in ins] + [_bspec(b, f, "si") for _, b, f in cts]
                  + [pl.BlockSpec(memory_space=pl.ANY)] * n_sh),
        out_specs=[_bspec(spec[1], spec[2], "si") for spec in gouts],
        out_shape=[jax.ShapeDtypeStruct(spec[0], F32) for spec in gouts],
        input_output_aliases={n_in + n_ct + k: gi for k, (gi, _) in enumerate(shared)},
        compiler_params=pltpu.CompilerParams(vmem_limit_bytes=VMEM_LIMIT,
                                             dimension_semantics=("arbitrary", "arbitrary")),
    )(*[a for a, _, _, _ in ins], *[a for a, _, _ in cts], *[buf for _, buf in shared])


def _row_tile(rows, width, itemsize=4, budget=2 * 1024 * 1024, mult=8):
    best = mult
    for t in range(mult, rows + 1, mult):
        if rows % t == 0 and t * width * itemsize <= budget:
            best = t
    return best


def add_n(name, terms, shape):
    rows, cols = shape
    tr = _row_tile(rows, cols)

    def body(*refs):
        acc = refs[0][...]
        for r in refs[1:-1]:
            acc = acc + r[...]
        refs[-1][...] = acc

    specs = []
    for _, lead in terms:
        specs.append(pl.BlockSpec((None,) * len(lead) + (tr, cols), functools.partial(lambda i, lead: (*lead, i, 0), lead=lead)))
    return pl.pallas_call(
        body, grid=(rows // tr,), name=name, in_specs=specs,
        out_specs=pl.BlockSpec((tr, cols), lambda i: (i, 0)),
        out_shape=jax.ShapeDtypeStruct((rows, cols), F32),
    )(*[a for a, _ in terms])


def ffn_core(x, nw, wg, wu, wd):
    h = _rms(x, nw)
    return (0.5 * mm(jax.nn.silu(mm(h, wg)) * mm(h, wu), wd),)


def pre_core(x, nw, win):
    return (mm(_rms(x, nw), win),)


def _split_lanes(y):
    return jnp.stack([y[:, k * LANE:(k + 1) * LANE] for k in range(y.shape[1] // LANE)], axis=0)


def _join_lanes(y3):
    return jnp.concatenate([y3[k] for k in range(y3.shape[0])], axis=1)


def s5_pre_core(u, bmat):
    bu = mm(u, bmat)
    return _split_lanes(bu[:, :S5_N]), _split_lanes(bu[:, S5_N:])


def mid_core(xr, xi, u, o, g, hs, gc, hmat, cmat, d, gluw, glub, hgw):
    xs = jnp.concatenate([_join_lanes(xr), _join_lanes(xi)], axis=1)
    y = mm(xs, cmat) + d * u
    z = jax.nn.gelu(y)
    ya = z * jax.nn.sigmoid(mm(z, gluw) + glub)
    ms = mm_exact(o * o, hmat)
    yb = o * lax.rsqrt(ms + EPS) * hgw * jax.nn.silu(g)
    yc = hs * jax.nn.gelu(gc)
    return ya, yb, yc


def _sub(w, n):
    return WP(w.w[n], w.p[n]) if isinstance(w, WP) else w[n]


def merge_core(ya, yb, yc, g0, g1, g2, g3, g4, g5, p, wout):
    gate = lambda a, b: jax.nn.sigmoid(jnp.concatenate([a, b], axis=1))
    m = gate(g0, g1) * mm(ya, _sub(p, 0)) + gate(g2, g3) * mm(yb, _sub(p, 1)) + gate(g4, g5) * mm(yc, _sub(p, 2))
    return (mm(m, wout),)


def gates_core(xc, wa, ba, wx, bx, lam):
    r = jax.nn.sigmoid(mm(xc, wa) + ba)
    i = jax.nn.sigmoid(mm(xc, wx) + bx)
    log_a = -RG_C * jax.nn.softplus(-lam) * r
    a = jnp.exp(log_a)
    b = jnp.sqrt(-_expm1(2.0 * log_a)) * (i * xc)
    return a, b


def _seg_rows(ref, k, j, n):
    rows = pl.ds(pl.multiple_of(j * NSEG, NSEG), NSEG)
    if k is None:
        return ref[rows, :]
    return ref[k, rows, :]


def _seg_store(ref, k, j, n, val):
    rows = pl.ds(pl.multiple_of(j * NSEG, NSEG), NSEG)
    if k is None:
        ref[rows, :] = val
    else:
        ref[k, rows, :] = val


def _seg_carries(er, ei, pr, pi, reverse):
    rows = lax.broadcasted_iota(jnp.int32, er.shape, 0)
    cr = jnp.zeros_like(er)
    ci = None if ei is None else jnp.zeros_like(er)
    order = range(NSEG - 2, -1, -1) if reverse else range(1, NSEG)
    shift = NSEG - 1 if reverse else 1
    for s in order:
        if ei is None:
            tr = er + pr * cr
            cr = jnp.where(rows == s, pltpu.roll(tr, shift, 0), cr)
        else:
            tr = er + pr * cr - pi * ci
            ti = ei + pr * ci + pi * cr
            cr = jnp.where(rows == s, pltpu.roll(tr, shift, 0), cr)
            ci = jnp.where(rows == s, pltpu.roll(ti, shift, 0), ci)
    return cr, ci


S5_K = 2


def s5_scan_fwd(bur, bui, ar, ai, L):
    n = L // NSEG
    nb = S5_N // LANE
    K = S5_K

    def body(br_ref, bi_ref, ar_ref, ai_ref, xr_ref, xi_ref):
        zero = jnp.zeros((NSEG, LANE), F32)
        A = [(jnp.broadcast_to(ar_ref[k], (NSEG, LANE)), jnp.broadcast_to(ai_ref[k], (NSEG, LANE))) for k in range(K)]

        def p1(j, st):
            new = []
            for k in range(K):
                sr, si, pr, pi = st[k]
                a_r, a_i = A[k]
                nr = a_r * sr - a_i * si + _seg_rows(br_ref, k, j, n)
                ni = a_r * si + a_i * sr + _seg_rows(bi_ref, k, j, n)
                _seg_store(xr_ref, k, j, n, nr)
                _seg_store(xi_ref, k, j, n, ni)
                new.append((nr, ni, a_r * pr - a_i * pi, a_r * pi + a_i * pr))
            return tuple(new)

        st = lax.fori_loop(0, n, p1, tuple((zero, zero, zero + 1.0, zero) for _ in range(K)))
        C = [_seg_carries(st[k][0], st[k][1], st[k][2], st[k][3], False) for k in range(K)]

        def p2(j, st):
            new = []
            for k in range(K):
                pr, pi = st[k]
                a_r, a_i = A[k]
                pr, pi = a_r * pr - a_i * pi, a_r * pi + a_i * pr
                cr, ci = C[k]
                _seg_store(xr_ref, k, j, n, _seg_rows(xr_ref, k, j, n) + pr * cr - pi * ci)
                _seg_store(xi_ref, k, j, n, _seg_rows(xi_ref, k, j, n) + pr * ci + pi * cr)
                new.append((pr, pi))
            return tuple(new)

        lax.fori_loop(0, n, p2, tuple((zero + 1.0, zero) for _ in range(K)))

    blk = pl.BlockSpec((K, L, LANE), lambda g: (g, 0, 0))
    ablk = pl.BlockSpec((K, 1, LANE), lambda g: (g, 0, 0))
    return pl.pallas_call(
        body, grid=(nb // K,), name="s5_scan_fwd",
        in_specs=[blk, blk, ablk, ablk], out_specs=[blk, blk],
        out_shape=[jax.ShapeDtypeStruct((nb, L, LANE), F32)] * 2,
        compiler_params=pltpu.CompilerParams(vmem_limit_bytes=VMEM_LIMIT),
    )(bur, bui, ar, ai)


def s5_scan_bwd(dxr, dxi, xr, xi, ar, ai, L):
    n = L // NSEG
    nb = S5_N // LANE
    K = S5_K

    def body(dr_ref, di_ref, xr_ref, xi_ref, ar_ref, ai_ref, gr_ref, gi_ref, dar_ref, dai_ref):
        zero = jnp.zeros((NSEG, LANE), F32)
        rows = lax.broadcasted_iota(jnp.int32, (NSEG, LANE), 0)
        A = [(jnp.broadcast_to(ar_ref[k], (NSEG, LANE)), -jnp.broadcast_to(ai_ref[k], (NSEG, LANE))) for k in range(K)]

        def p1(jj, st):
            j = n - 1 - jj
            new = []
            for k in range(K):
                sr, si, pr, pi = st[k]
                a_r, a_i = A[k]
                nr = a_r * sr - a_i * si + _seg_rows(dr_ref, k, j, n)
                ni = a_r * si + a_i * sr + _seg_rows(di_ref, k, j, n)
                _seg_store(gr_ref, k, j, n, nr)
                _seg_store(gi_ref, k, j, n, ni)
                new.append((nr, ni, a_r * pr - a_i * pi, a_r * pi + a_i * pr))
            return tuple(new)

        st = lax.fori_loop(0, n, p1, tuple((zero, zero, zero + 1.0, zero) for _ in range(K)))
        C = [_seg_carries(st[k][0], st[k][1], st[k][2], st[k][3], True) for k in range(K)]
        xb = [(jnp.where(rows == 0, 0.0, pltpu.roll(_seg_rows(xr_ref, k, n - 1, n), 1, 0)),
               jnp.where(rows == 0, 0.0, pltpu.roll(_seg_rows(xi_ref, k, n - 1, n), 1, 0))) for k in range(K)]

        def p2(jj, st):
            j = n - 1 - jj
            jp = jnp.maximum(j - 1, 0)
            new = []
            for k in range(K):
                pr, pi, acr, aci = st[k]
                a_r, a_i = A[k]
                pr, pi = a_r * pr - a_i * pi, a_r * pi + a_i * pr
                cr, ci = C[k]
                g_r = _seg_rows(gr_ref, k, j, n) + pr * cr - pi * ci
                g_i = _seg_rows(gi_ref, k, j, n) + pr * ci + pi * cr
                _seg_store(gr_ref, k, j, n, g_r)
                _seg_store(gi_ref, k, j, n, g_i)
                xpr = jnp.where(j == 0, xb[k][0], _seg_rows(xr_ref, k, jp, n))
                xpi = jnp.where(j == 0, xb[k][1], _seg_rows(xi_ref, k, jp, n))
                new.append((pr, pi, acr + g_r * xpr + g_i * xpi, aci + g_i * xpr - g_r * xpi))
            return tuple(new)

        st = lax.fori_loop(0, n, p2, tuple((zero + 1.0, zero, zero, zero) for _ in range(K)))
        for k in range(K):
            dar_ref[k] = jnp.sum(st[k][2], axis=0, keepdims=True)
            dai_ref[k] = jnp.sum(st[k][3], axis=0, keepdims=True)

    blk = pl.BlockSpec((K, L, LANE), lambda g: (g, 0, 0))
    ablk = pl.BlockSpec((K, 1, LANE), lambda g: (g, 0, 0))
    return pl.pallas_call(
        body, grid=(nb // K,), name="s5_scan_bwd",
        in_specs=[blk, blk, blk, blk, ablk, ablk], out_specs=[blk, blk, ablk, ablk],
        out_shape=[jax.ShapeDtypeStruct((nb, L, LANE), F32)] * 2 + [jax.ShapeDtypeStruct((nb, 1, LANE), F32)] * 2,
        compiler_params=pltpu.CompilerParams(vmem_limit_bytes=VMEM_LIMIT),
    )(dxr, dxi, xr, xi, ar, ai)


def rg_scan_fwd(a, b, L):
    n = L // NSEG

    def body(a_ref, b_ref, h_ref):
        zero = jnp.zeros((NSEG, LANE), F32)

        def p1(j, st):
            h, p = st
            aj = _seg_rows(a_ref, None, j, n)
            h = aj * h + _seg_rows(b_ref, None, j, n)
            _seg_store(h_ref, None, j, n, h)
            return h, aj * p

        e, pe = lax.fori_loop(0, n, p1, (zero, zero + 1.0))
        c, _ = _seg_carries(e, None, pe, None, False)

        def p2(j, p):
            p = _seg_rows(a_ref, None, j, n) * p
            _seg_store(h_ref, None, j, n, _seg_rows(h_ref, None, j, n) + p * c)
            return p

        lax.fori_loop(0, n, p2, zero + 1.0)

    blk = pl.BlockSpec((L, LANE), lambda g: (0, g))
    return pl.pallas_call(
        body, grid=(BW // LANE,), name="rg_scan_fwd", in_specs=[blk, blk], out_specs=blk,
        out_shape=jax.ShapeDtypeStruct((L, BW), F32),
        compiler_params=pltpu.CompilerParams(vmem_limit_bytes=VMEM_LIMIT),
    )(a, b)


def rg_scan_bwd(a, h, dh, L):
    n = L // NSEG

    def body(a_ref, h_ref, dh_ref, da_ref, db_ref):
        zero = jnp.zeros((NSEG, LANE), F32)
        rows = lax.broadcasted_iota(jnp.int32, (NSEG, LANE), 0)
        a_edge = jnp.where(rows == NSEG - 1, 0.0, pltpu.roll(_seg_rows(a_ref, None, 0, n), NSEG - 1, 0))
        h_edge = jnp.where(rows == 0, 0.0, pltpu.roll(_seg_rows(h_ref, None, n - 1, n), 1, 0))

        def mult(j):
            return jnp.where(j == n - 1, a_edge, _seg_rows(a_ref, None, jnp.minimum(j + 1, n - 1), n))

        def p1(jj, st):
            j = n - 1 - jj
            g, p = st
            m = mult(j)
            g = m * g + _seg_rows(dh_ref, None, j, n)
            _seg_store(db_ref, None, j, n, g)
            return g, m * p

        e, pe = lax.fori_loop(0, n, p1, (zero, zero + 1.0))
        c, _ = _seg_carries(e, None, pe, None, True)

        def p2(jj, p):
            j = n - 1 - jj
            p = mult(j) * p
            g = _seg_rows(db_ref, None, j, n) + p * c
            _seg_store(db_ref, None, j, n, g)
            hp = jnp.where(j == 0, h_edge, _seg_rows(h_ref, None, jnp.maximum(j - 1, 0), n))
            _seg_store(da_ref, None, j, n, g * hp)
            return p

        lax.fori_loop(0, n, p2, zero + 1.0)

    blk = pl.BlockSpec((L, LANE), lambda g: (0, g))
    return pl.pallas_call(
        body, grid=(BW // LANE,), name="rg_scan_bwd", in_specs=[blk, blk, blk], out_specs=[blk, blk],
        out_shape=[jax.ShapeDtypeStruct((L, BW), F32)] * 2,
        compiler_params=pltpu.CompilerParams(vmem_limit_bytes=VMEM_LIMIT),
    )(a, h, dh)


def _hg_consts(C):
    t = lax.broadcasted_iota(jnp.int32, (C, C), 0)
    s = lax.broadcasted_iota(jnp.int32, (C, C), 1)
    tril = (s <= t).astype(F32)
    diag = (s == t).astype(F32)
    levels = []
    k = 1
    while (1 << k) <= C:
        m = 1 << (k - 1)
        same = (t >> k) == (s >> k)
        t_right = ((t >> (k - 1)) & 1) == 1
        s_left = ((s >> (k - 1)) & 1) == 0
        mask = jnp.logical_and(same, jnp.logical_and(t_right, s_left)).astype(F32)
        bnd = ((t >> k) << k) + (m - 1)
        levels.append((mask, (s <= bnd).astype(F32)))
        k += 1
    return tril, diag, levels


def hg_chunk(st, q, z, v, lb):
    C = q.shape[0]
    tril, diag, levels = _hg_consts(C)
    sig = jax.nn.sigmoid(z)
    lf = jnp.log(lb + (1.0 - lb) * sig)
    k = (1.0 - lb) * jax.nn.sigmoid(-z)
    qh = jax.nn.silu(q)
    b = mm_exact(tril, lf)
    blast = jnp.sum(lf, axis=0, keepdims=True)
    qe = qh * jnp.exp(b)
    kd = k * jnp.exp(blast - b)
    scaled = []
    for _, sel in levels:
        ref = mm_exact(sel, lf)
        scaled.append((qh * jnp.exp(jnp.minimum(b - ref, 0.0)), k * jnp.exp(jnp.minimum(ref - b, 0.0))))
    outs, news = [], []
    for h in range(HG_HEADS):
        sl = slice(h * HG_D, (h + 1) * HG_D)
        st_h = st[h * HG_D:(h + 1) * HG_D, :]
        sc = diag * mma_nt(qh[:, sl], k[:, sl])
        for (mask, _), (qt, kt) in zip(levels, scaled):
            sc = sc + mask * mma_nt(qt[:, sl], kt[:, sl])
        outs.append(mma_nt(qe[:, sl], st_h) + mma_nn(sc, v[:, sl]))
        news.append(st_h * jnp.exp(blast[:, sl]) + mma_tn(v[:, sl], kd[:, sl]))
    return jnp.concatenate(news, axis=0), jnp.concatenate(outs, axis=1)


def hg_fwd(qzv, lb, L):
    C = HG_CHUNK
    nc = L // C

    def body(q_ref, z_ref, v_ref, lb_ref, o_ref, sst_ref, st_ref):
        @pl.when(pl.program_id(0) == 0)
        def _():
            st_ref[...] = jnp.zeros_like(st_ref)

        st = st_ref[...]
        sst_ref[...] = st
        new, o = hg_chunk(st, q_ref[...], z_ref[...], v_ref[...], lb_ref[...])
        st_ref[...] = new
        o_ref[...] = o

    col = lambda cb: pl.BlockSpec((C, BW), functools.partial(lambda c, cb: (c, cb), cb=cb))
    return pl.pallas_call(
        body, grid=(nc,), name="hg_fwd",
        in_specs=[col(0), col(1), col(2), pl.BlockSpec((1, BW), lambda c: (0, 0))],
        out_specs=[pl.BlockSpec((C, BW), lambda c: (c, 0)), pl.BlockSpec((None, BW, HG_D), lambda c: (c, 0, 0))],
        out_shape=[jax.ShapeDtypeStruct((L, BW), F32), jax.ShapeDtypeStruct((nc, BW, HG_D), F32)],
        scratch_shapes=[pltpu.VMEM((BW, HG_D), F32)],
        compiler_params=pltpu.CompilerParams(vmem_limit_bytes=VMEM_LIMIT, dimension_semantics=("arbitrary",)),
    )(qzv, qzv, qzv, lb)


def hg_bwd(qzv, lb, sst, do, L):
    C = HG_CHUNK
    nc = L // C

    def body(q_ref, z_ref, v_ref, lb_ref, sst_ref, do_ref, dq_ref, dz_ref, dv_ref, dlb_ref, dst_ref):
        @pl.when(pl.program_id(0) == 0)
        def _():
            dst_ref[...] = jnp.zeros_like(dst_ref)
            dlb_ref[...] = jnp.zeros_like(dlb_ref)

        _, vjp = jax.vjp(hg_chunk, sst_ref[...], q_ref[...], z_ref[...], v_ref[...], lb_ref[...])
        dst, dq, dz, dv, dlb = vjp((dst_ref[...], do_ref[...]))
        dst_ref[...] = dst
        dq_ref[...] = dq
        dz_ref[...] = dz
        dv_ref[...] = dv
        dlb_ref[...] += dlb

    col = lambda cb: pl.BlockSpec((C, BW), functools.partial(lambda c, cb: (nc - 1 - c, cb), cb=cb))
    rev = pl.BlockSpec((C, BW), lambda c: (nc - 1 - c, 0))
    return pl.pallas_call(
        body, grid=(nc,), name="hg_bwd",
        in_specs=[col(0), col(1), col(2), pl.BlockSpec((1, BW), lambda c: (0, 0)),
                  pl.BlockSpec((None, BW, HG_D), lambda c: (nc - 1 - c, 0, 0)), rev],
        out_specs=[rev, rev, rev, pl.BlockSpec((1, BW), lambda c: (0, 0))],
        out_shape=[jax.ShapeDtypeStruct((L, BW), F32)] * 3 + [jax.ShapeDtypeStruct((1, BW), F32)],
        scratch_shapes=[pltpu.VMEM((BW, HG_D), F32)],
        compiler_params=pltpu.CompilerParams(vmem_limit_bytes=VMEM_LIMIT, dimension_semantics=("arbitrary",)),
    )(qzv, qzv, qzv, lb, sst, do)


def _shift_down(x, d, rows, L):
    if d == 0:
        return x
    wrapped = jnp.where((rows & (NSEG - 1)) == 0, 0.0, pltpu.roll(x, NSEG * d + 1, 0))
    return jnp.where(rows < NSEG * d, wrapped, pltpu.roll(x, NSEG * d, 0))


def _shift_up(x, d, rows, L):
    if d == 0:
        return x
    wrapped = jnp.where((rows & (NSEG - 1)) == NSEG - 1, 0.0, pltpu.roll(x, L - (NSEG * d + 1), 0))
    return jnp.where(rows >= L - NSEG * d, wrapped, pltpu.roll(x, L - NSEG * d, 0))


def conv_fwd(proj, w, b, L):
    def body(x_ref, w_ref, b_ref, o_ref):
        x = x_ref[...]
        rows = lax.broadcasted_iota(jnp.int32, x.shape, 0)
        acc = jnp.broadcast_to(b_ref[...], x.shape)
        for k in range(CONV_W):
            acc = acc + w_ref[pl.ds(k, 1), :] * _shift_down(x, CONV_W - 1 - k, rows, L)
        o_ref[...] = acc

    nl = BW // LANE
    return pl.pallas_call(
        body, grid=(nl,), name="conv_fwd",
        in_specs=[pl.BlockSpec((L, LANE), lambda g: (0, 5 * nl + g)), pl.BlockSpec((CONV_W, LANE), lambda g: (0, g)),
                  pl.BlockSpec((1, LANE), lambda g: (0, g))],
        out_specs=pl.BlockSpec((L, LANE), lambda g: (0, g)),
        out_shape=jax.ShapeDtypeStruct((L, BW), F32),
        compiler_params=pltpu.CompilerParams(vmem_limit_bytes=VMEM_LIMIT),
    )(proj, w, b)


def conv_bwd(proj, w, dxc, L):
    def body(x_ref, w_ref, d_ref, dx_ref, dw_ref, db_ref):
        x, d = x_ref[...], d_ref[...]
        rows = lax.broadcasted_iota(jnp.int32, x.shape, 0)
        acc = jnp.zeros_like(x)
        for k in range(CONV_W):
            acc = acc + w_ref[pl.ds(k, 1), :] * _shift_up(d, CONV_W - 1 - k, rows, L)
            dw_ref[pl.ds(k, 1), :] = jnp.sum(d * _shift_down(x, CONV_W - 1 - k, rows, L), axis=0, keepdims=True)
        dx_ref[...] = acc
        db_ref[...] = jnp.sum(d, axis=0, keepdims=True)

    nl = BW // LANE
    blk = pl.BlockSpec((L, LANE), lambda g: (0, g))
    return pl.pallas_call(
        body, grid=(nl,), name="conv_bwd",
        in_specs=[pl.BlockSpec((L, LANE), lambda g: (0, 5 * nl + g)), pl.BlockSpec((CONV_W, LANE), lambda g: (0, g)), blk],
        out_specs=[blk, pl.BlockSpec((CONV_W, LANE), lambda g: (0, g)), pl.BlockSpec((1, LANE), lambda g: (0, g))],
        out_shape=[jax.ShapeDtypeStruct((L, BW), F32), jax.ShapeDtypeStruct((CONV_W, BW), F32),
                   jax.ShapeDtypeStruct((1, BW), F32)],
        compiler_params=pltpu.CompilerParams(vmem_limit_bytes=VMEM_LIMIT),
    )(proj, w, dxc)


def loss_fwd_bwd(x, fw, target, L, tm):
    def fn(x, fw, t):
        err = jnp.square(_rms(x, fw) - t)
        return jnp.sum(0.5 * jnp.mean(err, axis=-1, keepdims=True), axis=0, keepdims=True)

    def body(x_ref, fw_ref, t_ref, l_ref, dx_ref, dfw_ref):
        i = pl.program_id(0)
        t = t_ref[...]
        val, vjp = jax.vjp(lambda x, fw: fn(x, fw, t), x_ref[...], fw_ref[...])
        dx, dfw = vjp(jnp.ones((1, 1), F32))
        dx_ref[...] = dx

        @pl.when(i == 0)
        def _():
            l_ref[...] = jnp.zeros_like(l_ref)
            dfw_ref[...] = jnp.zeros_like(dfw_ref)

        l_ref[...] += jnp.broadcast_to(val, l_ref.shape)
        dfw_ref[...] += dfw

    row = pl.BlockSpec((tm, D_MODEL), lambda i: (i, 0))
    vec = pl.BlockSpec((1, D_MODEL), lambda i: (0, 0))
    return pl.pallas_call(
        body, grid=(L // tm,), name="loss_fwd_bwd", in_specs=[row, vec, row],
        out_specs=[pl.BlockSpec((1, LANE), lambda i: (0, 0)), row, vec],
        out_shape=[jax.ShapeDtypeStruct((1, LANE), F32), jax.ShapeDtypeStruct((L, D_MODEL), F32),
                   jax.ShapeDtypeStruct((1, D_MODEL), F32)],
        compiler_params=pltpu.CompilerParams(vmem_limit_bytes=VMEM_LIMIT, dimension_semantics=("arbitrary",)),
    )(x, fw, target)


def adamw(w, g, m, v):
    rows, cols = w.shape
    tr = _row_tile(rows, cols, budget=1024 * 1024)
    c1 = 1.0 - ADAM_B1 ** ADAM_STEP
    c2 = 1.0 - ADAM_B2 ** ADAM_STEP

    def body(w_ref, g_ref, m_ref, v_ref, d_ref, nm_ref, nv_ref):
        g = g_ref[...]
        nm = ADAM_B1 * m_ref[...] + (1.0 - ADAM_B1) * g
        nv = ADAM_B2 * v_ref[...] + (1.0 - ADAM_B2) * jnp.square(g)
        d_ref[...] = -ADAM_LR * ((nm / c1) / (jnp.sqrt(nv / c2) + ADAM_EPS) + ADAM_WD * w_ref[...])
        nm_ref[...] = nm
        nv_ref[...] = nv

    blk = pl.BlockSpec((tr, cols), lambda i: (i, 0))
    return pl.pallas_call(
        body, grid=(rows // tr,), name="adamw", in_specs=[blk] * 4, out_specs=[blk] * 3,
        out_shape=[jax.ShapeDtypeStruct((rows, cols), F32)] * 3,
    )(w, g, m, v)


def s5_prep(lam_re, lam_im, log_dt, b_re, b_im, c_re, c_im):
    lr = jnp.minimum(lam_re, -1e-4)
    li = lam_im
    dt = jnp.exp(log_dt)[:, None]
    mag = jnp.exp(lr * dt)
    ar = mag * jnp.cos(li * dt)
    ai = mag * jnp.sin(li * dt)
    den = lr * lr + li * li
    fr = ((ar - 1.0) * lr + ai * li) / den
    fi = (ai * lr - (ar - 1.0) * li) / den
    bbr = fr[..., None] * b_re - fi[..., None] * b_im
    bbi = fr[..., None] * b_im + fi[..., None] * b_re
    emb_b = lambda bb: _block_diag(bb.transpose(0, 2, 1).reshape(BW, S5_STATE), S5_GROUPS)
    emb_c = lambda cc: _block_diag(cc.transpose(0, 2, 1).reshape(S5_N, S5_GROUP), S5_GROUPS)
    bmat = jnp.concatenate([emb_b(bbr), emb_b(bbi)], axis=1)
    cmat = jnp.concatenate([emb_c(c_re), -emb_c(c_im)], axis=0)
    nb = S5_N // LANE
    return ar.reshape(nb, 1, LANE), ai.reshape(nb, 1, LANE), bmat, cmat


def _block_diag(stacked, groups):
    rows, c = stacked.shape
    r = rows // groups
    row_g = jnp.arange(rows)[:, None] // r
    col_g = jnp.arange(groups * c)[None, :] // c
    return jnp.where(row_g == col_g, jnp.tile(stacked, (1, groups)), 0.0)


def rg_prep(w):
    return _block_diag(w.reshape(BW, RG_BLOCK), RG_BLOCKS)


def hg_prep(logits):
    p = jax.nn.softmax(logits, axis=0)
    return jnp.cumsum(p, axis=0) - p[0]


def _head_mean_matrix():
    r = jnp.arange(BW) // HG_D
    return (r[:, None] == r[None, :]).astype(F32) / HG_D


def _to_segment_order(a):
    L = a.shape[0]
    return a.reshape(NSEG, L // NSEG, -1).transpose(1, 0, 2).reshape(a.shape)


def _to_time_order(a):
    L = a.shape[0]
    return a.reshape(L // NSEG, NSEG, -1).transpose(1, 0, 2).reshape(a.shape)


def _const(*idx):
    return lambda s, i: idx


def _rows(cb=0):
    return lambda s, i: (i, cb)


def _sum_parts(name, first, parts, shape):
    return add_n(name, [(first, ())] + [(parts, (s,)) for s in range(NSH)], shape)


def _ffn_weight_specs(l, j):
    F = D_FF // NSH
    one = pl.Buffered(1)
    return [pl.BlockSpec((None, NSH, D_MODEL, F), lambda i: (j, 0, 0, 0), pipeline_mode=one),
            pl.BlockSpec((None, NSH, D_MODEL, F), lambda i: (j, 0, 0, 0), pipeline_mode=one),
            pl.BlockSpec((None, NSH, F, D_MODEL), lambda i: (j, 0, 0, 0), pipeline_mode=one)]


def ffn_fwd(name, x, W, l, j, k, L, tm):
    D, F = D_MODEL, D_FF // NSH

    def body(x_ref, nw_ref, wg_ref, wu_ref, wd_ref, y_ref, g_ref, u_ref):
        x = x_ref[...]
        h = _rms(x, nw_ref[...]).astype(MMT)
        y = x
        for s in range(NSH):
            g = _dg(h, wg_ref[s], 1, 0)
            u = _dg(h, wu_ref[s], 1, 0)
            g_ref[s] = g.astype(g_ref.dtype)
            u_ref[s] = u.astype(u_ref.dtype)
            y = y + 0.5 * _dg((jax.nn.silu(g) * u).astype(MMT), wd_ref[s], 1, 0)
        y_ref[...] = y

    row = pl.BlockSpec((tm, D), lambda i: (i, 0))
    act = pl.BlockSpec((NSH, tm, F), lambda i: (0, i, 0))
    return pl.pallas_call(
        body, grid=(L // tm,), name=name,
        in_specs=[row, pl.BlockSpec((None, None, 1, D), lambda i: (l, k, 0, 0))] + _ffn_weight_specs(l, j),
        out_specs=[row, act, act],
        out_shape=[jax.ShapeDtypeStruct((L, D), F32), jax.ShapeDtypeStruct((NSH, L, F), MMT),
                   jax.ShapeDtypeStruct((NSH, L, F), MMT)],
        compiler_params=pltpu.CompilerParams(vmem_limit_bytes=VMEM_LIMIT, dimension_semantics=("arbitrary",)),
    )(x, W["nw"], W["L"][l]["wg"], W["L"][l]["wu"], W["L"][l]["wd"])


def ffn_bwd(name, x, g, u, dy, W, bufs, l, j, k, L, tm):
    D, F = D_MODEL, D_FF // NSH
    tm = min(TM_WGRAD, L)

    def body(x_ref, nw_ref, dy_ref, g_ref, u_ref, wg_ref, wu_ref, wd_ref, *rest):
        part_ref, dnw_ref, dwg_ref, dwu_ref, dwd_ref = rest[-5:]
        s, i = pl.program_id(0), pl.program_id(1)
        x, nw = x_ref[...], nw_ref[...]
        r = lax.rsqrt(jnp.mean(x * x, axis=-1, keepdims=True) + EPS)
        xhat = x * r
        h = (xhat * nw).astype(MMT)
        half_dy = (0.5 * dy_ref[...]).astype(MMT)
        gs, us = g_ref[...].astype(F32), u_ref[...].astype(F32)
        sig = jax.nn.sigmoid(gs)
        act = gs * sig
        da = _dg(half_dy, wd_ref[...], 1, 1)
        du = (da * act).astype(MMT)
        dg = (da * us * (sig * (1.0 + gs * (1.0 - sig)))).astype(MMT)
        dh = _dg(dg, wg_ref[...], 1, 1) + _dg(du, wu_ref[...], 1, 1)
        dxh = dh * nw
        part_ref[...] = r * (dxh - xhat * jnp.mean(dxh * xhat, axis=-1, keepdims=True))
        grads = (_dg(h, dg, 0, 0), _dg(h, du, 0, 0), _dg((act * us).astype(MMT), half_dy, 0, 0))
        dnw = jnp.sum(dh * xhat, axis=0, keepdims=True)
        first = jnp.logical_and(s == 0, i == 0)
        for ref, val, start in zip((dwg_ref, dwu_ref, dwd_ref, dnw_ref), grads + (dnw,), (i == 0, i == 0, i == 0, first)):
            @pl.when(start)
            def _(ref=ref, val=val):
                ref[...] = val

            @pl.when(jnp.logical_not(start))
            def _(ref=ref, val=val):
                ref[...] += val

    row = pl.BlockSpec((tm, D), lambda s, i: (i, 0))
    act = pl.BlockSpec((None, tm, F), lambda s, i: (s, i, 0))
    wsp = lambda r, c: pl.BlockSpec((None, None, r, c), lambda s, i: (j, s, 0, 0))
    gsp = lambda r, c: pl.BlockSpec((None, None, r, c), lambda s, i: (0, s, 0, 0))
    part, dnw, bufs[("ffn_gate", l, j)], bufs[("ffn_up", l, j)], bufs[("ffn_down", l, j)] = pl.pallas_call(
        body, grid=(NSH, L // tm), name=name,
        in_specs=[row, pl.BlockSpec((None, None, 1, D), lambda s, i: (l, k, 0, 0)), row, act, act,
                  wsp(D, F), wsp(D, F), wsp(F, D)],
        out_specs=[pl.BlockSpec((None, tm, D), lambda s, i: (s, i, 0)), pl.BlockSpec((1, D), lambda s, i: (0, 0)),
                   gsp(D, F), gsp(D, F), gsp(F, D)],
        out_shape=[jax.ShapeDtypeStruct((NSH, L, D), F32), jax.ShapeDtypeStruct((1, D), F32)]
        + [jax.ShapeDtypeStruct((1, NSH, D, F), F32)] * 2 + [jax.ShapeDtypeStruct((1, NSH, F, D), F32)],
        compiler_params=pltpu.CompilerParams(vmem_limit_bytes=VMEM_LIMIT, dimension_semantics=("arbitrary", "arbitrary")),
    )(x, W["nw"], dy, g, u, W["L"][l]["wg"], W["L"][l]["wu"], W["L"][l]["wd"])
    return _sum_parts(name + "_dx", dy, part, (L, D)), dnw


def layer_fwd(l, x0, W, P, L, tm):
    D = D_MODEL
    tmm = tm
    tm = min(TM_FWD, L)
    n_i = L // tm
    x1, g0, u0 = ffn_fwd(f"ffn_fwd_{l}0", x0, W, l, 0, 0, L, tm)
    proj = tile_fwd(
        lambda x, nw, win, s: pre_core(x, nw, win), f"pre_fwd_{l}", n_i, NSH,
        [(x1, (tm, D), _rows()), (W["nw"], (None, None, 1, D), _const(l, 1, 0, 0)),
         (W["L"][l]["win"], (None, D, IN_TOTAL // NSH), lambda s, i: (s, 0, 0))],
        [((L, IN_TOTAL), F32, (tm, IN_TOTAL // NSH), lambda s, i: (i, s), False)], s_outer=True)[0]
    nb = S5_N // LANE
    blk3 = lambda s, i: (0, i, 0)
    bur, bui = tile_fwd(
        lambda u, bmat, s: s5_pre_core(u, bmat), f"s5pre_fwd_{l}", n_i, 1,
        [(proj, (tm, BW), _rows(0)), (P["bmat"], (None, BW, 2 * S5_N), _const(l, 0, 0))],
        [((nb, L, LANE), F32, (nb, tm, LANE), blk3, False)] * 2)
    xr, xi = s5_scan_fwd(bur, bui, P["ar"][l], P["ai"][l], L)
    qzv = _to_time_order(proj[:, BW:4 * BW])
    o_t, sst = hg_fwd(qzv, P["lb"][l], L)
    o = _to_segment_order(o_t)
    xc = conv_fwd(proj, W["convw"][l], P["convb"][l], L)
    vec = (None, 1, BW)
    a, b = tile_fwd(
        lambda xc, wa, ba, wx, bx, lam, s: gates_core(xc, wa, ba, wx, bx, lam), f"gates_fwd_{l}", n_i, 1,
        [(xc, (tm, BW), _rows()), (P["wa"], (None, BW, BW), _const(l, 0, 0)), (P["ba"], vec, _const(l, 0, 0)),
         (P["wx"], (None, BW, BW), _const(l, 0, 0)), (P["bx"], vec, _const(l, 0, 0)), (P["lam"], vec, _const(l, 0, 0))],
        [((L, BW), F32, (tm, BW), _rows(), False)] * 2)
    hs = rg_scan_fwd(a, b, L)
    ya, yb, yc = tile_fwd(
        lambda *a: mid_core(*a[:-1]), f"mid_fwd_{l}", L // tmm, 1,
        [(xr, (nb, tmm, LANE), blk3), (xi, (nb, tmm, LANE), blk3), (proj, (tmm, BW), _rows(0)), (o, (tmm, BW), _rows()),
         (proj, (tmm, BW), _rows(4)), (hs, (tmm, BW), _rows()), (proj, (tmm, BW), _rows(6)),
         (P["hmat"], (BW, BW), _const(0, 0)), (P["cmat"], (None, 2 * S5_N, BW), _const(l, 0, 0)), (P["d"], vec, _const(l, 0, 0)),
         (W["L"][l]["gluw"], (BW, BW), _const(0, 0)), (P["glub"], vec, _const(l, 0, 0)), (P["hgw"], vec, _const(l, 0, 0))],
        [((L, BW), F32, (tmm, BW), _rows(), False)] * 3)
    x2 = tile_fwd(
        lambda x, *rest: (x + merge_core(*rest[:-1])[0],), f"merge_fwd_{l}", n_i, 1,
        [(x1, (tm, D), _rows()), (ya, (tm, BW), _rows()), (yb, (tm, BW), _rows()), (yc, (tm, BW), _rows())]
        + [(proj, (tm, BW), _rows(7 + k)) for k in range(6)]
        + [(W["L"][l]["pfull"], (3, BW, D), _const(0, 0, 0)), (W["L"][l]["woutfull"], (D, D), _const(0, 0))],
        [((L, D), F32, (tm, D), _rows(), False)])[0]
    x3, g1, u1 = ffn_fwd(f"ffn_fwd_{l}1", x2, W, l, 1, 2, L, tm)
    saved = dict(x0=x0, x1=x1, x2=x2, proj=proj, xr=xr, xi=xi, o=o, sst=sst, xc=xc, a=a, hs=hs, ya=ya, yb=yb, yc=yc,
                 qzv=qzv, g0=g0, u0=u0, g1=g1, u1=u1)
    return x3, saved


def layer_bwd(l, dx3, sv, W, P, bufs, L, tm, ready=lambda l, group: None):
    D = D_MODEL
    n_i = L // tm
    nb = S5_N // LANE
    dq = D // NSH
    vec = (None, 1, BW)
    vout = ((1, BW), (1, BW), _const(0, 0), "acc_all")
    blk3 = lambda s, i: (0, i, 0)
    small = {}
    proj = sv["proj"]

    dx2, dnw2 = ffn_bwd(f"ffn_bwd_{l}1", sv["x2"], sv["g1"], sv["u1"], dx3, W, bufs, l, 1, 2, L, tm)
    ready(l, "ffn1")

    rw256 = ((L, BW), (tm, BW), _rows(), "write")
    res = tile_bwd(
        merge_core, f"merge_bwd_{l}", n_i, 1,
        [(sv["ya"], (tm, BW), _rows(), "r"), (sv["yb"], (tm, BW), _rows(), "r"), (sv["yc"], (tm, BW), _rows(), "r")]
        + [(proj, (tm, BW), _rows(7 + k), "r") for k in range(6)]
        + [(W["L"][l]["pfull"], (3, BW, D), _const(0, 0, 0), "w"), (W["L"][l]["woutfull"], (D, D), _const(0, 0), "w")],
        [(dx2, (tm, D), _rows())],
        [rw256] * 9
        + [((3, BW, D), (3, BW, D), _const(0, 0, 0), "acc_all"), ((D, D), (D, D), _const(0, 0), "acc_all")])
    dya, dyb, dyc = res[:3]
    dgm = res[3:9]
    bufs[("branch_proj", l)], bufs[("w_out", l)] = res[9:]
    ready(l, "merge")

    tmm = tm
    rw = ((L, BW), (tmm, BW), _rows(), "write")
    xw = ((nb, L, LANE), (nb, tmm, LANE), blk3, "write")
    res = tile_bwd(
        mid_core, f"mid_bwd_{l}", L // tmm, 1,
        [(sv["xr"], (nb, tmm, LANE), blk3, "r"), (sv["xi"], (nb, tmm, LANE), blk3, "r"), (proj, (tmm, BW), _rows(0), "r"),
         (sv["o"], (tmm, BW), _rows(), "r"), (proj, (tmm, BW), _rows(4), "r"), (sv["hs"], (tmm, BW), _rows(), "r"),
         (proj, (tmm, BW), _rows(6), "r"), (P["hmat"], (BW, BW), _const(0, 0), "c"),
         (P["cmat"], (None, 2 * S5_N, BW), _const(l, 0, 0), "w"), (P["d"], vec, _const(l, 0, 0), "p"),
         (W["L"][l]["gluw"], (BW, BW), _const(0, 0), "w"), (P["glub"], vec, _const(l, 0, 0), "p"),
         (P["hgw"], vec, _const(l, 0, 0), "p")],
        [(dya, (tmm, BW), _rows()), (dyb, (tmm, BW), _rows()), (dyc, (tmm, BW), _rows())],
        [xw, xw, rw, rw, rw, rw, rw,
         ((DEPTH, 2 * S5_N, BW), (None, 2 * S5_N, BW), _const(l, 0, 0), "acc_all", bufs.get("cmat")), vout,
         ((BW, BW), (BW, BW), _const(0, 0), "acc_all"), vout, vout])
    dxr, dxi, du_skip, do, dg_b, dhs, dgate_c, bufs["cmat"], dd, bufs[("s5_glu_w", l)], dglub, dhgw = res
    small["s5_d"], small["s5_glu_b"], small["hg_norm_w"] = dd[0], dglub[0], dhgw[0]
    ready(l, "mid")

    da, db = rg_scan_bwd(sv["a"], sv["hs"], dhs, L)
    wmat = lambda key: ((DEPTH, BW, BW), (None, BW, BW), _const(l, 0, 0), "acc_all", bufs.get(key))
    res = tile_bwd(
        gates_core, f"gates_bwd_{l}", n_i, 1,
        [(sv["xc"], (tm, BW), _rows(), "r"), (P["wa"], (None, BW, BW), _const(l, 0, 0), "w"), (P["ba"], vec, _const(l, 0, 0), "p"),
         (P["wx"], (None, BW, BW), _const(l, 0, 0), "w"), (P["bx"], vec, _const(l, 0, 0), "p"), (P["lam"], vec, _const(l, 0, 0), "p")],
        [(da, (tm, BW), _rows()), (db, (tm, BW), _rows())],
        [((L, BW), (tm, BW), _rows(), "write"), wmat("wa"), vout, wmat("wx"), vout, vout])
    dxc, bufs["wa"], dba, bufs["wx"], dbx, dlam = res
    small["rg_ba"], small["rg_bx"], small["rg_lambda"] = dba[0], dbx[0], dlam[0]
    dx_c, dconvw, dconvb = conv_bwd(proj, W["convw"][l], dxc, L)
    small["rg_conv_w"], small["rg_conv_b"] = dconvw, dconvb[0]

    dq_b, dz_b, dv_b, dlb = hg_bwd(sv["qzv"], P["lb"][l], sv["sst"], _to_time_order(do), L)
    dq_b, dz_b, dv_b = [_to_segment_order(a) for a in (dq_b, dz_b, dv_b)]

    gr, gi, dar, dai = s5_scan_bwd(dxr, dxi, sv["xr"], sv["xi"], P["ar"][l], P["ai"][l], L)
    du_pre, bufs["bmat"] = tile_bwd(
        s5_pre_core, f"s5pre_bwd_{l}", n_i, 1,
        [(proj, (tm, BW), _rows(0), "r"), (P["bmat"], (None, BW, 2 * S5_N), _const(l, 0, 0), "w")],
        [(gr, (nb, tm, LANE), blk3), (gi, (nb, tm, LANE), blk3)],
        [((L, BW), (tm, BW), _rows(), "write"),
         ((DEPTH, BW, 2 * S5_N), (None, BW, 2 * S5_N), _const(l, 0, 0), "acc_all", bufs.get("bmat"))])
    du_a = add_n(f"du_a_{l}", [(du_skip, ()), (du_pre, ())], (L, BW))
    prep_ct = dict(dar=dar, dai=dai, dlb=dlb)

    pieces = [du_a, dq_b, dz_b, dv_b, dg_b, dx_c, dgate_c, *dgm]
    per_piece, per_shard = BW // LANE, IN_TOTAL // NSH // LANE
    part, dnw1 = None, []
    tmw = min(TM_WGRAD, L)
    for s in range(NSH):
        groups = [(pieces[g // per_piece], (tmw, LANE), _rows(g % per_piece))
                  for g in range(s * per_shard, (s + 1) * per_shard)]
        part, dnw_s, bufs[("w_in", l)] = tile_bwd(
            pre_core, f"pre_bwd_{l}{s}", L // tmw, 1,
            [(sv["x1"], (tmw, D), _rows(), "r"), (W["nw"], (None, None, 1, D), _const(l, 1, 0, 0), "p"),
             (W["L"][l]["win"], (None, D, IN_TOTAL // NSH), _const(s, 0, 0), "w")],
            [groups],
            [((NSH, L, D), (None, tmw, D), functools.partial(lambda _s, i, s: (s, i, 0), s=s), "write", part),
             ((1, D), (1, D), _const(0, 0), "acc_all"),
             ((1, NSH, D, IN_TOTAL // NSH), (None, None, D, IN_TOTAL // NSH), _const(0, s, 0, 0), "acc_all",
              bufs.get(("w_in", l)))])
        dnw1.append(dnw_s)
    dnw1 = (dnw1[0] + dnw1[1]) + (dnw1[2] + dnw1[3])
    dx1 = _sum_parts(f"pre_bwd_{l}_dx", dx2, part, (L, D))
    ready(l, "pre")

    dx0, dnw0 = ffn_bwd(f"ffn_bwd_{l}0", sv["x0"], sv["g0"], sv["u0"], dx1, W, bufs, l, 0, 0, L, tm)
    ready(l, "ffn0")
    small["norm_w"] = jnp.concatenate([dnw0, dnw1, dnw2], axis=0)
    return dx0, small, prep_ct


SMALL_RAW = ("s5_lambda_re", "s5_lambda_im", "s5_log_dt", "s5_b_re", "s5_b_im", "s5_c_re", "s5_c_im", "s5_d", "s5_glu_b",
             "hg_lb_logits", "hg_norm_w", "rg_conv_b", "rg_wa", "rg_ba", "rg_wx", "rg_bx", "rg_lambda", "final_norm_w")
DEPTH = 2


def local_step(x, target, W, raw, layer_weights=None, layer_grads=None):
    L = x.shape[0]
    tm = min(256, L)
    col = lambda v: v.reshape(DEPTH, 1, BW)
    (ar, ai, bmat, cmat), s5_vjp = jax.vjp(jax.vmap(s5_prep), *[raw[k] for k in SMALL_RAW[:7]])
    (wa, wx), rg_vjp = jax.vjp(lambda a, b: (jax.vmap(rg_prep)(a), jax.vmap(rg_prep)(b)), raw["rg_wa"], raw["rg_wx"])
    lb, hg_vjp = jax.vjp(hg_prep, raw["hg_lb_logits"])
    P = dict(
        ar=[ar[l] for l in range(DEPTH)], ai=[ai[l] for l in range(DEPTH)],
        bmat=bmat.astype(MMT), cmat=cmat.astype(MMT), wa=wa.astype(MMT), wx=wx.astype(MMT),
        lb=[lb[l].reshape(1, BW) for l in range(DEPTH)], convb=[raw["rg_conv_b"][l].reshape(1, BW) for l in range(DEPTH)],
        ba=col(raw["rg_ba"]), bx=col(raw["rg_bx"]), lam=col(raw["rg_lambda"]), d=col(raw["s5_d"]),
        glub=col(raw["s5_glu_b"]), hgw=col(raw["hg_norm_w"]), hmat=_head_mean_matrix())

    saved = []
    h = _to_segment_order(x)
    for l in range(DEPTH):
        if layer_weights is not None:
            W["L"][l], h = layer_weights(l, h)
        h, sv = layer_fwd(l, h, W, P, L, tm)
        saved.append(sv)
    loss, dh, dfw = loss_fwd_bwd(h, raw["final_norm_w"].reshape(1, D_MODEL), _to_segment_order(target), L, tm)

    big, per_layer, prep_cts = {}, [None] * DEPTH, [None] * DEPTH
    ready = (lambda l, group: None) if layer_grads is None else (lambda l, group: layer_grads(l, group, big))
    for l in reversed(range(DEPTH)):
        dh, sm, pc = layer_bwd(l, dh, saved[l], W, P, big, L, tm, ready)
        per_layer[l], prep_cts[l] = sm, pc
    dh = _to_time_order(dh)

    small = {k: jnp.stack([per_layer[l][k] for l in range(DEPTH)]) for k in per_layer[0]}
    both = lambda k: jnp.stack([prep_cts[l][k] for l in range(DEPTH)])
    s5_g = s5_vjp((both("dar"), both("dai"), big.pop("bmat"), big.pop("cmat")))
    small.update(zip(SMALL_RAW[:7], s5_g))
    small["rg_wa"], small["rg_wx"] = rg_vjp((big.pop("wa"), big.pop("wx")))
    (small["hg_lb_logits"],) = hg_vjp(jnp.concatenate([prep_cts[l]["dlb"] for l in range(DEPTH)], axis=0))
    small["final_norm_w"] = dfw[0]
    return loss, dh, big, small


ANY = pl.BlockSpec(memory_space=pl.ANY)


def _place():
    x, y, c = lax.axis_index("x"), lax.axis_index("y"), lax.axis_index("c")
    chips = [(1 - x, y), (x, 1 - y), (1 - x, 1 - y)]
    return x, y, c, chips


def _remote(src, dst, send, recv, k, to):
    return pltpu.make_async_remote_copy(src_ref=src, dst_ref=dst, send_sem=send.at[k], recv_sem=recv.at[k],
                                        device_id=to, device_id_type=MESH)


def _comm_call(body, name, ins, out_shapes, n_sem, n_loc):
    return pl.pallas_call(
        body, name=name, in_specs=[ANY] * len(ins), out_specs=[ANY] * len(out_shapes), out_shape=out_shapes,
        scratch_shapes=[pltpu.SemaphoreType.DMA((n_sem,)), pltpu.SemaphoreType.DMA((n_sem,)),
                        pltpu.SemaphoreType.DMA((max(n_loc, 1),))],
    )(*ins)


def gather_shards(name, shards):
    n = len(shards)
    per = 8

    def body(*refs):
        ins, outs = refs[:n], refs[n:2 * n]
        send, recv, _ = refs[2 * n:]
        x, y, c, chips = _place()
        me = 2 * x + y
        sib = (x, y, 1 - c)
        sends = []
        for w in range(n):
            for j, (cx, cy) in enumerate(chips):
                cp = _remote(ins[w].at[c], outs[w].at[c, me], send, recv, per * w + j, (cx, cy, c))
                cp.start()
                sends.append(cp)
        for w in range(n):
            for l in range(2):
                cp = _remote(ins[w].at[l], outs[w].at[l, me], send, recv, per * w + 6 + l, sib)
                cp.start()
                sends.append(cp)
        for w in range(n):
            for j, (cx, cy) in enumerate(chips):
                theirs = outs[w].at[c, 2 * cx + cy]
                _remote(ins[w].at[c], theirs, send, recv, per * w + j, (cx, cy, c)).wait_recv()
                cp = _remote(theirs, theirs, send, recv, per * w + 3 + j, sib)
                cp.start()
                sends.append(cp)
        for w in range(n):
            for j, (cx, cy) in enumerate(chips):
                dst = outs[w].at[1 - c, 2 * cx + cy]
                _remote(dst, dst, send, recv, per * w + 3 + j, sib).wait_recv()
            for l in range(2):
                dst = outs[w].at[l, me]
                _remote(dst, dst, send, recv, per * w + 6 + l, sib).wait_recv()
        for cp in sends:
            cp.wait_send()

    shapes = [jax.ShapeDtypeStruct((2, NSH) + s.shape[1:], s.dtype) for s in shards]
    return _comm_call(body, name, shards, shapes, per * n, 0)


def exchange_halves(name, grads, ranges):
    n = len(grads)

    def body(*refs):
        ins, outs = refs[:n], refs[n:2 * n]
        send, recv, _ = refs[2 * n:]
        x, y, c, _chips = _place()
        cps = []
        for w in range(n):
            h = grads[w].shape[2] // 2
            p0, np_ = ranges[w]
            cp = _remote(ins[w].at[pl.ds(p0, np_), :, pl.ds((1 - c) * h, h)], outs[w], send, recv, w, (x, y, 1 - c))
            cp.start()
            cps.append(cp)
        for cp in cps:
            cp.wait()

    shapes = [jax.ShapeDtypeStruct((r[1], NSH, g.shape[2] // 2, g.shape[3]), g.dtype) for g, r in zip(grads, ranges)]
    return _comm_call(body, name, grads, shapes, n, 0)


def scatter_to_chips(name, halves):
    n = len(halves)

    def body(*refs):
        ins, outs = refs[:n], refs[n:2 * n]
        send, recv, _ = refs[2 * n:]
        x, y, c, chips = _place()
        cps = []
        for w in range(n):
            for j, (cx, cy) in enumerate(chips):
                cp = _remote(ins[w].at[:, 2 * cx + cy], outs[w].at[j], send, recv, 3 * w + j, (cx, cy, c))
                cp.start()
                cps.append(cp)
        for cp in cps:
            cp.wait()

    shapes = [jax.ShapeDtypeStruct((3, h.shape[0]) + h.shape[2:], h.dtype) for h in halves]
    return _comm_call(body, name, halves, shapes, 3 * n, 0)


def share_halves(name, pieces):
    n = len(pieces)

    def body(*refs):
        ins, outs = refs[:n], refs[n:2 * n]
        send, recv, _ = refs[2 * n:]
        x, y, c, _chips = _place()
        cps = []
        for w in range(n):
            cp = _remote(ins[w], outs[w], send, recv, w, (x, y, 1 - c))
            cp.start()
            cps.append(cp)
        for cp in cps:
            cp.wait()

    return _comm_call(body, name, pieces, [jax.ShapeDtypeStruct(p.shape, p.dtype) for p in pieces], n, 0)


def add_own_half(name, g, ra, c, wire, b0):
    nblk, h, cols = ra.shape
    tr = _row_tile(h, cols, mult=16)
    nt = h // tr

    def body(c_ref, g_ref, r_ref, o_ref):
        o_ref[...] = (g_ref[...] + r_ref[...]).astype(o_ref.dtype)

    blk = (None, tr, cols)
    return pl.pallas_call(
        body, name=name,
        grid_spec=pltpu.PrefetchScalarGridSpec(
            num_scalar_prefetch=1, grid=(nblk, nt),
            in_specs=[pl.BlockSpec(blk, lambda s, i, c_ref: (b0 + s, c_ref[0] * nt + i, 0)), pl.BlockSpec(blk, lambda s, i, c_ref: (s, i, 0))],
            out_specs=pl.BlockSpec(blk, lambda s, i, c_ref: (s, i, 0))),
        out_shape=jax.ShapeDtypeStruct(ra.shape, wire),
    )(c.reshape(1), g, ra)


def add_chips(name, hb, rb, me):
    npc, _, h, cols = hb.shape
    tr = _row_tile(h, cols, mult=16)

    def body(me_ref, h_ref, r0, r1, r2, o_ref):
        f = lambda r: r[...].astype(F32)
        o_ref[...] = ((f(h_ref) + f(r0)) + f(r1)) + f(r2)

    rspec = lambda j: pl.BlockSpec((None, None, tr, cols), functools.partial(lambda p, i, me_ref, j: (j, p, i, 0), j=j))
    return pl.pallas_call(
        body, name=name,
        grid_spec=pltpu.PrefetchScalarGridSpec(
            num_scalar_prefetch=1, grid=(npc, h // tr),
            in_specs=[pl.BlockSpec((None, None, tr, cols), lambda p, i, me_ref: (p, me_ref[0], i, 0)), rspec(0), rspec(1), rspec(2)],
            out_specs=pl.BlockSpec((None, tr, cols), lambda p, i, me_ref: (p, i, 0))),
        out_shape=jax.ShapeDtypeStruct((npc, h, cols), F32),
    )(me.reshape(1), hb, rb, rb, rb)


def adamw_halves(name, w, m, v, own, other, c):
    npc, rows, cols = w.shape
    h = rows // 2
    tr = _row_tile(h, cols, budget=1024 * 1024)
    nt = h // tr
    c1 = 1.0 - ADAM_B1 ** ADAM_STEP
    c2 = 1.0 - ADAM_B2 ** ADAM_STEP

    def body(c_ref, w_ref, m_ref, v_ref, own_ref, oth_ref, g_ref, d_ref, nm_ref, nv_ref):
        g = jnp.where(pl.program_id(1) == c_ref[0], own_ref[...], oth_ref[...])
        nm = ADAM_B1 * m_ref[...] + (1.0 - ADAM_B1) * g
        nv = ADAM_B2 * v_ref[...] + (1.0 - ADAM_B2) * jnp.square(g)
        g_ref[...] = g
        d_ref[...] = -ADAM_LR * ((nm / c1) / (jnp.sqrt(nv / c2) + ADAM_EPS) + ADAM_WD * w_ref[...])
        nm_ref[...] = nm
        nv_ref[...] = nv

    full = pl.BlockSpec((None, tr, cols), lambda p, hh, i, c_ref: (p, hh * nt + i, 0))
    half = pl.BlockSpec((None, tr, cols), lambda p, hh, i, c_ref: (p, i, 0))
    return pl.pallas_call(
        body, name=name,
        grid_spec=pltpu.PrefetchScalarGridSpec(
            num_scalar_prefetch=1, grid=(npc, 2, nt),
            in_specs=[full, full, full, half, half], out_specs=[full] * 4),
        out_shape=[jax.ShapeDtypeStruct(w.shape, F32)] * 4,
    )(c.reshape(1), w, m, v, own, other)


WEIGHTS = ("norm_w", "final_norm_w", "ffn_gate", "ffn_up", "ffn_down", "w_in", "branch_proj", "w_out", "s5_lambda_re",
           "s5_lambda_im", "s5_log_dt", "s5_b_re", "s5_b_im", "s5_c_re", "s5_c_im", "s5_d", "s5_glu_w", "s5_glu_b",
           "hg_lb_logits", "hg_norm_w", "rg_conv_w", "rg_conv_b", "rg_wa", "rg_ba", "rg_wx", "rg_bx", "rg_lambda")
BIG = ("ffn_gate", "ffn_up", "ffn_down", "w_in", "branch_proj", "w_out", "s5_glu_w")
SHARDED_SMALL = ("norm_w", "rg_conv_w")
SMALL = SMALL_RAW + SHARDED_SMALL


def _view2d(shape):
    return (1, shape[0]) if len(shape) == 1 else (math.prod(shape[:-1]), shape[-1])


def _small_layout(shapes, row_multiple):
    layout, at = [], 0
    for shape in shapes:
        r, c = _view2d(shape)
        rp = -(-r // 8) * 8
        layout.append((at, r, c, rp))
        at += rp * max(1, c // LANE)
    return layout, -(-at // row_multiple) * row_multiple


def pack_small(name, arrays, row_multiple):
    layout, rows = _small_layout([a.shape for a in arrays], row_multiple)

    def body(*refs):
        out = refs[-1]
        out[...] = jnp.zeros_like(out)
        for ref, (r0, r, c, rp) in zip(refs[:-1], layout):
            if c <= LANE:
                out[r0:r0 + r, 0:c] = ref[...]
            else:
                for q in range(c // LANE):
                    out[r0 + q * rp:r0 + q * rp + r, :] = ref[:, q * LANE:(q + 1) * LANE]

    return pl.pallas_call(
        body, name=name, out_shape=jax.ShapeDtypeStruct((rows, LANE), F32),
        compiler_params=pltpu.CompilerParams(vmem_limit_bytes=VMEM_LIMIT),
    )(*[a.reshape(_view2d(a.shape)) for a in arrays])


def unpack_small(name, packed, shapes):
    layout, _ = _small_layout(shapes, 8)

    def body(p_ref, *outs):
        for ref, (r0, r, c, rp) in zip(outs, layout):
            if c <= LANE:
                ref[...] = p_ref[r0:r0 + r, 0:c]
            else:
                for q in range(c // LANE):
                    ref[:, q * LANE:(q + 1) * LANE] = p_ref[r0 + q * rp:r0 + q * rp + r, :]

    res = pl.pallas_call(
        body, name=name, out_shape=[jax.ShapeDtypeStruct(_view2d(s), F32) for s in shapes],
        compiler_params=pltpu.CompilerParams(vmem_limit_bytes=VMEM_LIMIT),
    )(packed)
    return [a.reshape(s) for a, s in zip(res, shapes)]


HBM = pl.BlockSpec(memory_space=pltpu.HBM)
SEM = pl.BlockSpec(memory_space=pltpu.SEMAPHORE)
EFFECT = pltpu.SideEffectType.DATAFLOW_SIDE_EFFECTING


def split_start(name, srcs, land_shapes, plan, n_send, n_recv):
    ns, nl = len(srcs), len(land_shapes)

    def body(*refs):
        ins, lands = refs[:ns], refs[ns:ns + nl]
        send, recv = refs[ns + nl], refs[ns + nl + 1]
        for src, dst, ks, kr, dev in plan(ins, lands):
            pltpu.make_async_remote_copy(src_ref=src, dst_ref=dst, send_sem=send.at[ks], recv_sem=recv.at[kr],
                                         device_id=dev, device_id_type=MESH).start()
        refs[-1][...] = jnp.zeros_like(refs[-1])

    hbm = lambda a: pltpu.with_memory_space_constraint(a, pltpu.HBM)
    lands = [lax.empty(s.shape, s.dtype) for s in land_shapes]
    out = pl.pallas_call(
        body, name=name,
        out_shape=(pltpu.SemaphoreType.DMA((n_send,)), pltpu.SemaphoreType.DMA((n_recv,)),
                   *[pltpu.HBM(a.shape, a.dtype) for a in srcs], *[pltpu.HBM(s.shape, s.dtype) for s in land_shapes],
                   jax.ShapeDtypeStruct((8, LANE), F32)),
        in_specs=[HBM] * (ns + nl), out_specs=(SEM, SEM, *[HBM] * (ns + nl), pl.BlockSpec(memory_space=pltpu.VMEM)),
        input_output_aliases={k: 2 + k for k in range(ns + nl)},
        compiler_params=pltpu.CompilerParams(has_side_effects=EFFECT),
    )(*[hbm(a) for a in srcs], *[hbm(a) for a in lands])
    return out[:-1], out[-1]


def split_wait(name, handles, n_src, waits, after):
    send, recv, *bufs = handles
    nb = len(bufs)

    def body(*refs):
        ins, lands = refs[:n_src], refs[n_src:nb]
        send_sem, recv_sem = refs[nb], refs[nb + 1]
        x, y, c, _chips = _place()
        sends, recvs = waits(ins, lands)
        for src, k in sends:
            pltpu.make_async_remote_copy(src_ref=src, dst_ref=src, send_sem=send_sem.at[k], recv_sem=recv_sem.at[0],
                                         device_id=(x, y, 1 - c), device_id_type=MESH).wait_send()
        for dst, k in recvs:
            pltpu.make_async_remote_copy(src_ref=dst, dst_ref=dst, send_sem=send_sem.at[0], recv_sem=recv_sem.at[k],
                                         device_id=(x, y, 1 - c), device_id_type=MESH).wait_recv()

    out = pl.pallas_call(
        body, name=name, out_shape=tuple(pltpu.HBM(a.shape, a.dtype) for a in bufs),
        in_specs=[HBM] * nb + [SEM, SEM, ANY], out_specs=tuple([HBM] * nb),
        input_output_aliases={k: k for k in range(nb)},
        compiler_params=pltpu.CompilerParams(has_side_effects=EFFECT),
    )(*bufs, send, recv, after)
    return list(out[:n_src]), list(out[n_src:])


def gather_plan(n):
    def plan(ins, lands):
        x, y, c, chips = _place()
        me = 2 * x + y
        copies = []
        for w in range(n):
            for j, (cx, cy) in enumerate(chips):
                for t in range(2):
                    copies.append((ins[w].at[c], lands[w].at[c, me], 8 * w + 2 * j + t, 8 * w + 2 * j + c, (cx, cy, t)))
            for half in range(2):
                copies.append((ins[w].at[half], lands[w].at[half, me], 8 * w + 6 + half, 8 * w + 6 + half, (x, y, 1 - c)))
        return copies

    def waits(ins, lands):
        x, y, c, chips = _place()
        me = 2 * x + y
        sends, recvs = [], []
        for w in range(n):
            for j, (cx, cy) in enumerate(chips):
                for t in range(2):
                    sends.append((ins[w].at[c], 8 * w + 2 * j + t))
                    recvs.append((lands[w].at[t, 2 * cx + cy], 8 * w + 2 * j + t))
            for half in range(2):
                sends.append((ins[w].at[half], 8 * w + 6 + half))
                recvs.append((lands[w].at[half, me], 8 * w + 6 + half))
        return sends, recvs

    return plan, waits


def scatter_plan(n):
    def plan(ins, lands):
        x, y, c, chips = _place()
        return [(ins[w].at[:, 2 * cx + cy], lands[w].at[j], 3 * w + j, 3 * w + j, (cx, cy, c))
                for w in range(n) for j, (cx, cy) in enumerate(chips)]

    def waits(ins, lands):
        x, y, c, chips = _place()
        sends = [(ins[w].at[:, 2 * cx + cy], 3 * w + j) for w in range(n) for j, (cx, cy) in enumerate(chips)]
        recvs = [(lands[w].at[j], 3 * w + j) for w in range(n) for j in range(3)]
        return sends, recvs

    return plan, waits


def _layer_shards(w, l):
    return [w["ffn_gate"][l].astype(MMT), w["ffn_up"][l].astype(MMT), w["ffn_down"][l].astype(MMT),
            w["w_in"][l].reshape(2, D_MODEL // 2, -1).astype(MMT),
            w["branch_proj"][l].reshape(2, 3 * BW // 2, -1).astype(MMT),
            w["w_out"][l].reshape(2, -1, D_MODEL).astype(MMT),
            w["s5_glu_w"][l].reshape(2, -1, BW).astype(MMT)]


def _layer_weights(g):
    rows = lambda a: a.transpose(1, 0, 2, 3).reshape(NSH, -1, a.shape[-1])
    p = rows(g[4]).reshape(NSH, 3, BW, -1).transpose(1, 2, 0, 3).reshape(3, BW, D_MODEL)
    return dict(wg=g[0], wu=g[1], wd=g[2], win=rows(g[3]), pfull=p,
                woutfull=rows(g[5]).reshape(D_MODEL, D_MODEL), gluw=rows(g[6]).reshape(BW, BW))


GROUPS = {"ffn1": ("ffn_gate", "ffn_up", "ffn_down"), "merge": ("branch_proj", "w_out"), "mid": ("s5_glu_w",),
          "pre": ("w_in",), "ffn0": ("ffn_gate", "ffn_up", "ffn_down")}


def _grad_views(big, l, group):
    views = []
    for name in GROUPS[group]:
        if name == "branch_proj":
            dq = D_MODEL // NSH
            a = big[(name, l)].reshape(3, BW, NSH, dq).transpose(2, 0, 1, 3).reshape(1, NSH, 3 * BW, dq)
        elif name.startswith("ffn"):
            a = big[(name, l, 1 if group == "ffn1" else 0)]
        else:
            a = big[(name, l)]
            a = a.reshape(1, NSH, -1, a.shape[-1])
        views.append((name, a, 0))
    return views


def halves_plan(n):
    def src(ref, c):
        h = ref.shape[2] // 2
        return ref.at[:, :, pl.ds((1 - c) * h, h)]

    def plan(ins, lands):
        x, y, c, _chips = _place()
        return [(src(ins[w], c), lands[w], w, w, (x, y, 1 - c)) for w in range(n)]

    def waits(ins, lands):
        x, y, c, _chips = _place()
        return [(src(ins[w], c), w) for w in range(n)], [(lands[w], w) for w in range(n)]

    return plan, waits


def _reduce_to_halves(tag, views, c, wire):
    from_sibling = exchange_halves(f"reduce_cores_{tag}", [a for _, a, _ in views], [(p0, 1) for _, _, p0 in views])
    merge = lambda a: a.reshape((-1,) + a.shape[2:])
    return [add_own_half(f"sum_cores_{tag}_{i}", merge(a), merge(r), c, wire[i], NSH * p0).reshape(r.shape)
            for i, ((_, a, p0), r) in enumerate(zip(views, from_sibling))]


def _step(x, target, w, m, v):
    mx, my, mc = lax.axis_index("x"), lax.axis_index("y"), lax.axis_index("c")
    me = (2 * mx + my).astype(jnp.int32)
    mc = mc.astype(jnp.int32)

    W = dict(L=[None] * DEPTH)
    state = {"pending": []}
    n_big = len(BIG)
    g_plan, g_waits = gather_plan(n_big)

    def layer_weights(l, h):
        if l == 0:
            got = gather_shards("gather_weights_0", _layer_shards(w, 0) + [w[n] for n in SHARDED_SMALL])
            nxt = _layer_shards(w, 1)
            got, nxt = lax.optimization_barrier((got, nxt))
            shapes = [jax.ShapeDtypeStruct((2, NSH) + a.shape[1:], a.dtype) for a in nxt]
            state["gather"], token = split_start("gather_weights_1_start", nxt, shapes, g_plan, 8 * n_big, 8 * n_big)
            W["nw"], _ = lax.optimization_barrier((got[n_big].transpose(0, 2, 1, 3).reshape(DEPTH, 3, 1, D_MODEL), token))
            W["convw"] = got[n_big + 1].transpose(0, 2, 1, 3).reshape(DEPTH, CONV_W, BW)
            return _layer_weights(got[:n_big]), h
        return _layer_weights(split_wait("gather_weights_1_wait", state["gather"], n_big, g_waits, h)[1]), h

    def to_chips(after):
        if "cores" not in state:
            return
        tag, names, l, group, handles, waits = state.pop("cores")
        sent, landed = split_wait(f"reduce_cores_{tag}_wait", handles, len(names), waits, after)
        merge = lambda a: a.reshape((-1,) + a.shape[2:])
        halves = [add_own_half(f"sum_cores_{tag}_{i}", merge(a), merge(r), mc, jnp.bfloat16, 0).reshape(r.shape)
                  for i, (a, r) in enumerate(zip(sent, landed))]
        shapes = [jax.ShapeDtypeStruct((3, a.shape[0]) + a.shape[2:], a.dtype) for a in halves]
        plan, waits = scatter_plan(len(halves))
        handles, token = split_start(f"reduce_chips_{tag}_start", halves, shapes, plan, 3 * len(halves), 3 * len(halves))
        W["nw"], _ = lax.optimization_barrier((W["nw"], token))
        state["pending"].append((tag, names, l, group, handles, waits))

    def layer_grads(l, group, big):
        views = _grad_views(big, l, group)
        to_chips(views[0][1])
        if (l, group) == (0, "ffn0"):
            return
        tag = f"{l}_{group}"
        arrays = [a for _, a, _ in views]
        shapes = [jax.ShapeDtypeStruct((1, NSH, a.shape[2] // 2, a.shape[3]), a.dtype) for a in arrays]
        plan, waits = halves_plan(len(arrays))
        handles, token = split_start(f"reduce_cores_{tag}_start", arrays, shapes, plan, len(arrays), len(arrays))
        W["nw"], _ = lax.optimization_barrier((W["nw"], token))
        state["cores"] = (tag, [name for name, _, _ in views], l, group, handles, waits)

    loss, dx, big, small = local_step(x[0], target[0], W, {k: w[k] for k in SMALL_RAW}, layer_weights, layer_grads)

    pieces = {n: {} for n in BIG}
    block_of = lambda name, l, group: (2 * l + (group == "ffn1")) if name.startswith("ffn") else l
    views = _grad_views(big, 0, "ffn0")
    small_packed = pack_small("pack_small_grads", [small[n] for n in SMALL], NSH * 32)
    halves = _reduce_to_halves("0_ffn0", views + [("small", small_packed.reshape(1, NSH, -1, LANE), 0)], mc,
                               [jnp.bfloat16] * len(views) + [F32])
    shapes = [jax.ShapeDtypeStruct((3, a.shape[0]) + a.shape[2:], a.dtype) for a in halves]
    plan, waits = scatter_plan(len(halves))
    last_handles, _ = split_start("reduce_chips_0_ffn0_start", halves, shapes, plan, 3 * len(halves), 3 * len(halves))
    for tag, names, l, group, handles, waits_k in state["pending"]:
        sent, landed = split_wait(f"reduce_chips_{tag}_wait", handles, len(names), waits_k, dx)
        for i, (name, h, r) in enumerate(zip(names, sent, landed)):
            pieces[name][block_of(name, l, group)] = add_chips(f"sum_chips_{tag}_{i}", h, r, me)

    g, delta, new_m, new_v = {}, {}, {}, {}

    def update(tag, names, extra):
        own = [jnp.concatenate([pieces[n][b] for b in sorted(pieces[n])], axis=0) for n in names] + extra
        other = share_halves(f"reduce_share_{tag}", own)
        for i, n in enumerate(names):
            view = lambda a: a.reshape(own[i].shape[0], -1, own[i].shape[2])
            res = adamw_halves(f"adamw_{n}", view(w[n]), view(m[n]), view(v[n]), own[i], other[i], mc)
            g[n], delta[n], new_m[n], new_v[n] = [a.reshape(w[n].shape) for a in res]
        return own, other

    early = [n for n in BIG if not n.startswith("ffn")]
    update("early", early, [])
    sent, landed = split_wait("reduce_chips_0_ffn0_wait", last_handles, len(halves), waits, new_v[early[0]])
    last = [add_chips(f"sum_chips_0_ffn0_{i}", h, r, me) for i, (h, r) in enumerate(zip(sent, landed))]
    for (name, _, _), piece in zip(views, last):
        pieces[name][block_of(name, 0, "ffn0")] = piece
    own, other = update("last", [n for n in BIG if n.startswith("ffn")], [last[-1]])

    piece = jnp.stack([jnp.where(mc == 0, own[-1][0], other[-1][0]), jnp.where(mc == 0, other[-1][0], own[-1][0])])
    (all_small,) = gather_shards("gather_small", [piece])
    full_small = unpack_small("unpack_small_grads", all_small.transpose(1, 0, 2, 3).reshape(-1, LANE),
                              [small[n].shape for n in SMALL])
    g.update(zip(SMALL, full_small))
    g["norm_w"] = lax.dynamic_slice_in_dim(g["norm_w"], me * (D_MODEL // NSH), D_MODEL // NSH, axis=2)
    g["rg_conv_w"] = lax.dynamic_slice_in_dim(g["rg_conv_w"], me * (BW // NSH), BW // NSH, axis=2)

    packed = [pack_small(f"pack_small_{tag}", [src[n] for n in SMALL], 8)
              for tag, src in (("w", w), ("g", g), ("m", m), ("v", v))]
    for tag, dst, flat in zip(("delta", "m", "v"), (delta, new_m, new_v), adamw(*packed)):
        dst.update(zip(SMALL, unpack_small(f"unpack_small_{tag}", flat, [w[n].shape for n in SMALL])))

    total = lax.psum(loss[0, 0], ("x", "y", "c"))
    return (total, dx[None], *[g[n] for n in WEIGHTS], *[delta[n] for n in WEIGHTS],
            *[new_m[n] for n in WEIGHTS], *[new_v[n] for n in WEIGHTS])


def kernel(x, norm_w, final_norm_w, ffn_gate, ffn_up, ffn_down, w_in, branch_proj, w_out, s5_lambda_re, s5_lambda_im, s5_log_dt, s5_b_re, s5_b_im, s5_c_re, s5_c_im, s5_d, s5_glu_w, s5_glu_b, hg_lb_logits, hg_norm_w, rg_conv_w, rg_conv_b, rg_wa, rg_ba, rg_wx, rg_bx, rg_lambda, loss_target, m_norm_w, m_final_norm_w, m_ffn_gate, m_ffn_up, m_ffn_down, m_w_in, m_branch_proj, m_w_out, m_s5_lambda_re, m_s5_lambda_im, m_s5_log_dt, m_s5_b_re, m_s5_b_im, m_s5_c_re, m_s5_c_im, m_s5_d, m_s5_glu_w, m_s5_glu_b, m_hg_lb_logits, m_hg_norm_w, m_rg_conv_w, m_rg_conv_b, m_rg_wa, m_rg_ba, m_rg_wx, m_rg_bx, m_rg_lambda, v_norm_w, v_final_norm_w, v_ffn_gate, v_ffn_up, v_ffn_down, v_w_in, v_branch_proj, v_w_out, v_s5_lambda_re, v_s5_lambda_im, v_s5_log_dt, v_s5_b_re, v_s5_b_im, v_s5_c_re, v_s5_c_im, v_s5_d, v_s5_glu_w, v_s5_glu_b, v_hg_lb_logits, v_hg_norm_w, v_rg_conv_w, v_rg_conv_b, v_rg_wa, v_rg_ba, v_rg_wx, v_rg_bx, v_rg_lambda):
    ws = (norm_w, final_norm_w, ffn_gate, ffn_up, ffn_down, w_in, branch_proj, w_out, s5_lambda_re, s5_lambda_im, s5_log_dt, s5_b_re, s5_b_im, s5_c_re, s5_c_im, s5_d, s5_glu_w, s5_glu_b, hg_lb_logits, hg_norm_w, rg_conv_w, rg_conv_b, rg_wa, rg_ba, rg_wx, rg_bx, rg_lambda)
    ms = (m_norm_w, m_final_norm_w, m_ffn_gate, m_ffn_up, m_ffn_down, m_w_in, m_branch_proj, m_w_out, m_s5_lambda_re, m_s5_lambda_im, m_s5_log_dt, m_s5_b_re, m_s5_b_im, m_s5_c_re, m_s5_c_im, m_s5_d, m_s5_glu_w, m_s5_glu_b, m_hg_lb_logits, m_hg_norm_w, m_rg_conv_w, m_rg_conv_b, m_rg_wa, m_rg_ba, m_rg_wx, m_rg_bx, m_rg_lambda)
    vs = (v_norm_w, v_final_norm_w, v_ffn_gate, v_ffn_up, v_ffn_down, v_w_in, v_branch_proj, v_w_out, v_s5_lambda_re, v_s5_lambda_im, v_s5_log_dt, v_s5_b_re, v_s5_b_im, v_s5_c_re, v_s5_c_im, v_s5_d, v_s5_glu_w, v_s5_glu_b, v_hg_lb_logits, v_hg_norm_w, v_rg_conv_w, v_rg_conv_b, v_rg_wa, v_rg_ba, v_rg_wx, v_rg_bx, v_rg_lambda)
    return _step(x, loss_target, dict(zip(WEIGHTS, ws)), dict(zip(WEIGHTS, ms)), dict(zip(WEIGHTS, vs)))
```

```python
import functools
import math
from typing import NamedTuple

import jax
import jax.numpy as jnp
from jax import lax
from jax.experimental import pallas as pl
from jax.experimental.pallas import tpu as pltpu

F32 = jnp.float32
MMT = jnp.bfloat16
HI = lax.Precision.HIGHEST

D_MODEL = 1024
BW = 512
S5_GROUP, S5_GROUPS, S5_STATE = 16, 32, 64
S5_N = S5_GROUPS * S5_STATE
HG_HEADS, HG_D = 4, 128
HG_CHUNK = 128
RG_BLOCKS, RG_BLOCK = 8, 64
RG_C = 8.0
CONV_W = 4
D_FF = 2816
EPS = 1e-6
IN_TOTAL = 6656
NSH = 4
NSEG = 8
LANE = 128
VMEM_LIMIT = 56 * 1024 * 1024
TM_FWD = 512
TM_WGRAD = 512

ADAM_LR, ADAM_B1, ADAM_B2, ADAM_EPS, ADAM_WD, ADAM_STEP = 0.001, 0.9, 0.999, 1e-08, 0.01, 10

MESH = pl.DeviceIdType.MESH


class WP(NamedTuple):
    w: jax.Array
    p: jax.Array


def _dg(a, b, ca, cb):
    return lax.dot_general(a, b, (((ca,), (cb,)), ((), ())), preferred_element_type=F32)


@jax.custom_vjp
def _mmw(a, w, p):
    return _dg(a.astype(MMT), w, 1, 0)


def _mmw_fwd(a, w, p):
    return _mmw(a, w, p), (a, w)


def _mmw_bwd(res, g):
    a, w = res
    gb = g.astype(MMT)
    return _dg(gb, w, 1, 1), jnp.zeros_like(w), _dg(a.astype(MMT), gb, 0, 0)


_mmw.defvjp(_mmw_fwd, _mmw_bwd)


def mm(a, w):
    if isinstance(w, WP):
        return _mmw(a, w.w, w.p)
    return _dg(a.astype(MMT), w, 1, 0)


@jax.custom_vjp
def mma_nn(a, b):
    return _dg(a.astype(MMT), b.astype(MMT), 1, 0)


def _nn_f(a, b):
    return mma_nn(a, b), (a, b)


def _nn_b(res, g):
    a, b = res
    gb = g.astype(MMT)
    return _dg(gb, b.astype(MMT), 1, 1), _dg(a.astype(MMT), gb, 0, 0)


mma_nn.defvjp(_nn_f, _nn_b)


@jax.custom_vjp
def mma_nt(a, b):
    return _dg(a.astype(MMT), b.astype(MMT), 1, 1)


def _nt_f(a, b):
    return mma_nt(a, b), (a, b)


def _nt_b(res, g):
    a, b = res
    gb = g.astype(MMT)
    return _dg(gb, b.astype(MMT), 1, 0), _dg(gb, a.astype(MMT), 0, 0)


mma_nt.defvjp(_nt_f, _nt_b)


@jax.custom_vjp
def mma_tn(a, b):
    return _dg(a.astype(MMT), b.astype(MMT), 0, 0)


def _tn_f(a, b):
    return mma_tn(a, b), (a, b)


def _tn_b(res, g):
    a, b = res
    gb = g.astype(MMT)
    return _dg(b.astype(MMT), gb, 1, 1), _dg(a.astype(MMT), gb, 1, 0)


mma_tn.defvjp(_tn_f, _tn_b)


def mm_exact(m, x):
    return jnp.dot(m, x, precision=HI, preferred_element_type=F32)


def _rms(x, w):
    return x * lax.rsqrt(jnp.mean(x * x, axis=-1, keepdims=True) + EPS) * w


def _expm1(x):
    series = x * (1.0 + x * (1.0 / 2) * (1.0 + x * (1.0 / 3) * (1.0 + x * (1.0 / 4) * (1.0 + x * (1.0 / 5) * (1.0 + x * (1.0 / 6))))))
    return jnp.where(jnp.abs(x) < 0.1, series, jnp.exp(x) - 1.0)


def _bspec(block, fn, order):
    if order == "is":
        return pl.BlockSpec(block, lambda i, s: fn(s, i))
    return pl.BlockSpec(block, lambda s, i: fn(s, i))


def tile_fwd(fn, name, n_i, n_s, ins, outs, s_outer=False):
    n_in = len(ins)
    order = "si" if s_outer else "is"
    assert not (s_outer and any(o[4] for o in outs))

    def body(*refs):
        s = pl.program_id(0 if s_outer else 1)
        res = fn(*[r[...] for r in refs[:n_in]], s)
        for o_ref, val, spec in zip(refs[n_in:], res, outs):
            if spec[4] and n_s > 1:
                @pl.when(s == 0)
                def _(o_ref=o_ref, val=val):
                    o_ref[...] = val.astype(o_ref.dtype)

                @pl.when(s != 0)
                def _(o_ref=o_ref, val=val):
                    o_ref[...] += val.astype(o_ref.dtype)
            else:
                o_ref[...] = val.astype(o_ref.dtype)

    return pl.pallas_call(
        body, grid=(n_s, n_i) if s_outer else (n_i, n_s), name=name,
        in_specs=[_bspec(b, f, order) for _, b, f in ins],
        out_specs=[_bspec(b, f, order) for _, _, b, f, _ in outs],
        out_shape=[jax.ShapeDtypeStruct(sh, dt) for sh, dt, _, _, _ in outs],
        compiler_params=pltpu.CompilerParams(vmem_limit_bytes=VMEM_LIMIT,
                                             dimension_semantics=("arbitrary", "arbitrary")),
    )(*[a for a, _, _ in ins])


def tile_bwd(fn, name, n_i, n_s, ins, cts, gouts):
    groups = [c if isinstance(c, list) else [c] for c in cts]
    cts = [blk for grp in groups for blk in grp]
    n_in, n_ct = len(ins), len(cts)
    kinds = [k for _, _, _, k in ins]
    d_pos = [j for j, k in enumerate(kinds) if k != "c"]
    shared = [(gi, spec[4]) for gi, spec in enumerate(gouts) if len(spec) == 5 and spec[4] is not None]
    n_sh = len(shared)

    def body(*refs):
        s, i = pl.program_id(0), pl.program_id(1)
        vals = [r[...] for r in refs[:n_in]]
        ct_refs, ctv = list(refs[n_in:n_in + n_ct]), []
        for grp in groups:
            parts = [ct_refs.pop(0)[...] for _ in grp]
            ctv.append(parts[0] if len(parts) == 1 else jnp.concatenate(parts, axis=1))
        ctv = tuple(ctv)
        g_refs = refs[n_in + n_ct + n_sh:]

        def g(*dv):
            args = list(vals)
            for j, v in zip(d_pos, dv):
                args[j] = WP(vals[j], v) if kinds[j] == "w" else v
            return tuple(fn(*args))

        dv0 = [jnp.zeros(vals[j].shape, F32) if kinds[j] == "w" else vals[j] for j in d_pos]
        _, vjp = jax.vjp(g, *dv0)
        grads = vjp(ctv)
        for g_ref, gv, spec in zip(g_refs, grads, gouts):
            mode = spec[3]
            if mode == "write":
                g_ref[...] = gv.astype(g_ref.dtype)
            else:
                first = (i == 0) if mode == "acc_i" else jnp.logical_and(i == 0, s == 0)

                @pl.when(first)
                def _(g_ref=g_ref, gv=gv):
                    g_ref[...] = gv.astype(g_ref.dtype)

                @pl.when(jnp.logical_not(first))
                def _(g_ref=g_ref, gv=gv):
                    g_ref[...] += gv.astype(g_ref.dtype)

    return pl.pallas_call(
        body, grid=(n_s, n_i), name=name,
        in_specs=([_bspec(b, f, "si") for _, b, f, _ in ins] + [_bspec(b, f, "si") for _, b, f in cts]
                  + [pl.BlockSpec(memory_space=pl.ANY)] * n_sh),
        out_specs=[_bspec(spec[1], spec[2], "si") for spec in gouts],
        out_shape=[jax.ShapeDtypeStruct(spec[0], F32) for spec in gouts],
        input_output_aliases={n_in + n_ct + k: gi for k, (gi, _) in enumerate(shared)},
        compiler_params=pltpu.CompilerParams(vmem_limit_bytes=VMEM_LIMIT,
                                             dimension_semantics=("arbitrary", "arbitrary")),
    )(*[a for a, _, _, _ in ins], *[a for a, _, _ in cts], *[buf for _, buf in shared])


def _row_tile(rows, width, itemsize=4, budget=2 * 1024 * 1024, mult=8):
    best = mult
    for t in range(mult, rows + 1, mult):
        if rows % t == 0 and t * width * itemsize <= budget:
            best = t
    return best


def add_n(name, terms, shape):
    rows, cols = shape
    tr = _row_tile(rows, cols)

    def body(*refs):
        acc = refs[0][...]
        for r in refs[1:-1]:
            acc = acc + r[...]
        refs[-1][...] = acc

    specs = []
    for _, lead in terms:
        specs.append(pl.BlockSpec((None,) * len(lead) + (tr, cols), functools.partial(lambda i, lead: (*lead, i, 0), lead=lead)))
    return pl.pallas_call(
        body, grid=(rows // tr,), name=name, in_specs=specs,
        out_specs=pl.BlockSpec((tr, cols), lambda i: (i, 0)),
        out_shape=jax.ShapeDtypeStruct((rows, cols), F32),
    )(*[a for a, _ in terms])


def ffn_core(x, nw, wg, wu, wd):
    h = _rms(x, nw)
    return (0.5 * mm(jax.nn.silu(mm(h, wg)) * mm(h, wu), wd),)


def pre_core(x, nw, win):
    return (mm(_rms(x, nw), win),)


def _split_lanes(y):
    return jnp.stack([y[:, k * LANE:(k + 1) * LANE] for k in range(y.shape[1] // LANE)], axis=0)


def _join_lanes(y3):
    return jnp.concatenate([y3[k] for k in range(y3.shape[0])], axis=1)


def s5_pre_core(u, bmat):
    bu = mm(u, bmat)
    return _split_lanes(bu[:, :S5_N]), _split_lanes(bu[:, S5_N:])


def mid_core(xr, xi, u, o, g, hs, gc, hmat, cmat, d, gluw, glub, hgw):
    xs = jnp.concatenate([_join_lanes(xr), _join_lanes(xi)], axis=1)
    y = mm(xs, cmat) + d * u
    z = jax.nn.gelu(y)
    ya = z * jax.nn.sigmoid(mm(z, gluw) + glub)
    ms = mm_exact(o * o, hmat)
    yb = o * lax.rsqrt(ms + EPS) * hgw * jax.nn.silu(g)
    yc = hs * jax.nn.gelu(gc)
    return ya, yb, yc


def _sub(w, n):
    return WP(w.w[n], w.p[n]) if isinstance(w, WP) else w[n]


def merge_core(ya, yb, yc, g0, g1, g2, g3, g4, g5, p, wout):
    gate = lambda a, b: jax.nn.sigmoid(jnp.concatenate([a, b], axis=1))
    m = gate(g0, g1) * mm(ya, _sub(p, 0)) + gate(g2, g3) * mm(yb, _sub(p, 1)) + gate(g4, g5) * mm(yc, _sub(p, 2))
    return (mm(m, wout),)


def gates_core(xc, wa, ba, wx, bx, lam):
    r = jax.nn.sigmoid(mm(xc, wa) + ba)
    i = jax.nn.sigmoid(mm(xc, wx) + bx)
    log_a = -RG_C * jax.nn.softplus(-lam) * r
    a = jnp.exp(log_a)
    b = jnp.sqrt(-_expm1(2.0 * log_a)) * (i * xc)
    return a, b


def _seg_rows(ref, k, j, n):
    rows = pl.ds(pl.multiple_of(j * NSEG, NSEG), NSEG)
    if k is None:
        return ref[rows, :]
    return ref[k, rows, :]


def _seg_store(ref, k, j, n, val):
    rows = pl.ds(pl.multiple_of(j * NSEG, NSEG), NSEG)
    if k is None:
        ref[rows, :] = val
    else:
        ref[k, rows, :] = val


def _seg_carries(er, ei, pr, pi, reverse):
    rows = lax.broadcasted_iota(jnp.int32, er.shape, 0)
    cr = jnp.zeros_like(er)
    ci = None if ei is None else jnp.zeros_like(er)
    order = range(NSEG - 2, -1, -1) if reverse else range(1, NSEG)
    shift = NSEG - 1 if reverse else 1
    for s in order:
        if ei is None:
            tr = er + pr * cr
            cr = jnp.where(rows == s, pltpu.roll(tr, shift, 0), cr)
        else:
            tr = er + pr * cr - pi * ci
            ti = ei + pr * ci + pi * cr
            cr = jnp.where(rows == s, pltpu.roll(tr, shift, 0), cr)
            ci = jnp.where(rows == s, pltpu.roll(ti, shift, 0), ci)
    return cr, ci


S5_K = 2


def s5_scan_fwd(bur, bui, ar, ai, L):
    n = L // NSEG
    nb = S5_N // LANE
    K = S5_K

    def body(br_ref, bi_ref, ar_ref, ai_ref, xr_ref, xi_ref):
        zero = jnp.zeros((NSEG, LANE), F32)
        A = [(jnp.broadcast_to(ar_ref[k], (NSEG, LANE)), jnp.broadcast_to(ai_ref[k], (NSEG, LANE))) for k in range(K)]

        def p1(j, st):
            new = []
            for k in range(K):
                sr, si, pr, pi = st[k]
                a_r, a_i = A[k]
                nr = a_r * sr - a_i * si + _seg_rows(br_ref, k, j, n)
                ni = a_r * si + a_i * sr + _seg_rows(bi_ref, k, j, n)
                _seg_store(xr_ref, k, j, n, nr)
                _seg_store(xi_ref, k, j, n, ni)
                new.append((nr, ni, a_r * pr - a_i * pi, a_r * pi + a_i * pr))
            return tuple(new)

        st = lax.fori_loop(0, n, p1, tuple((zero, zero, zero + 1.0, zero) for _ in range(K)))
        C = [_seg_carries(st[k][0], st[k][1], st[k][2], st[k][3], False) for k in range(K)]

        def p2(j, st):
            new = []
            for k in range(K):
                pr, pi = st[k]
                a_r, a_i = A[k]
                pr, pi = a_r * pr - a_i * pi, a_r * pi + a_i * pr
                cr, ci = C[k]
                _seg_store(xr_ref, k, j, n, _seg_rows(xr_ref, k, j, n) + pr * cr - pi * ci)
                _seg_store(xi_ref, k, j, n, _seg_rows(xi_ref, k, j, n) + pr * ci + pi * cr)
                new.append((pr, pi))
            return tuple(new)

        lax.fori_loop(0, n, p2, tuple((zero + 1.0, zero) for _ in range(K)))

    blk = pl.BlockSpec((K, L, LANE), lambda g: (g, 0, 0))
    ablk = pl.BlockSpec((K, 1, LANE), lambda g: (g, 0, 0))
    return pl.pallas_call(
        body, grid=(nb // K,), name="s5_scan_fwd",
        in_specs=[blk, blk, ablk, ablk], out_specs=[blk, blk],
        out_shape=[jax.ShapeDtypeStruct((nb, L, LANE), F32)] * 2,
        compiler_params=pltpu.CompilerParams(vmem_limit_bytes=VMEM_LIMIT),
    )(bur, bui, ar, ai)


def s5_scan_bwd(dxr, dxi, xr, xi, ar, ai, L):
    n = L // NSEG
    nb = S5_N // LANE
    K = S5_K

    def body(dr_ref, di_ref, xr_ref, xi_ref, ar_ref, ai_ref, gr_ref, gi_ref, dar_ref, dai_ref):
        zero = jnp.zeros((NSEG, LANE), F32)
        rows = lax.broadcasted_iota(jnp.int32, (NSEG, LANE), 0)
        A = [(jnp.broadcast_to(ar_ref[k], (NSEG, LANE)), -jnp.broadcast_to(ai_ref[k], (NSEG, LANE))) for k in range(K)]

        def p1(jj, st):
            j = n - 1 - jj
            new = []
            for k in range(K):
                sr, si, pr, pi = st[k]
                a_r, a_i = A[k]
                nr = a_r * sr - a_i * si + _seg_rows(dr_ref, k, j, n)
                ni = a_r * si + a_i * sr + _seg_rows(di_ref, k, j, n)
                _seg_store(gr_ref, k, j, n, nr)
                _seg_store(gi_ref, k, j, n, ni)
                new.append((nr, ni, a_r * pr - a_i * pi, a_r * pi + a_i * pr))
            return tuple(new)

        st = lax.fori_loop(0, n, p1, tuple((zero, zero, zero + 1.0, zero) for _ in range(K)))
        C = [_seg_carries(st[k][0], st[k][1], st[k][2], st[k][3], True) for k in range(K)]
        xb = [(jnp.where(rows == 0, 0.0, pltpu.roll(_seg_rows(xr_ref, k, n - 1, n), 1, 0)),
               jnp.where(rows == 0, 0.0, pltpu.roll(_seg_rows(xi_ref, k, n - 1, n), 1, 0))) for k in range(K)]

        def p2(jj, st):
            j = n - 1 - jj
            jp = jnp.maximum(j - 1, 0)
            new = []
            for k in range(K):
                pr, pi, acr, aci = st[k]
                a_r, a_i = A[k]
                pr, pi = a_r * pr - a_i * pi, a_r * pi + a_i * pr
                cr, ci = C[k]
                g_r = _seg_rows(gr_ref, k, j, n) + pr * cr - pi * ci
                g_i = _seg_rows(gi_ref, k, j, n) + pr * ci + pi * cr
                _seg_store(gr_ref, k, j, n, g_r)
                _seg_store(gi_ref, k, j, n, g_i)
                xpr = jnp.where(j == 0, xb[k][0], _seg_rows(xr_ref, k, jp, n))
                xpi = jnp.where(j == 0, xb[k][1], _seg_rows(xi_ref, k, jp, n))
                new.append((pr, pi, acr + g_r * xpr + g_i * xpi, aci + g_i * xpr - g_r * xpi))
            return tuple(new)

        st = lax.fori_loop(0, n, p2, tuple((zero + 1.0, zero, zero, zero) for _ in range(K)))
        for k in range(K):
            dar_ref[k] = jnp.sum(st[k][2], axis=0, keepdims=True)
            dai_ref[k] = jnp.sum(st[k][3], axis=0, keepdims=True)

    blk = pl.BlockSpec((K, L, LANE), lambda g: (g, 0, 0))
    ablk = pl.BlockSpec((K, 1, LANE), lambda g: (g, 0, 0))
    return pl.pallas_call(
        body, grid=(nb // K,), name="s5_scan_bwd",
        in_specs=[blk, blk, blk, blk, ablk, ablk], out_specs=[blk, blk, ablk, ablk],
        out_shape=[jax.ShapeDtypeStruct((nb, L, LANE), F32)] * 2 + [jax.ShapeDtypeStruct((nb, 1, LANE), F32)] * 2,
        compiler_params=pltpu.CompilerParams(vmem_limit_bytes=VMEM_LIMIT),
    )(dxr, dxi, xr, xi, ar, ai)


def rg_scan_fwd(a, b, L):
    n = L // NSEG

    def body(a_ref, b_ref, h_ref):
        zero = jnp.zeros((NSEG, LANE), F32)

        def p1(j, st):
            h, p = st
            aj = _seg_rows(a_ref, None, j, n)
            h = aj * h + _seg_rows(b_ref, None, j, n)
            _seg_store(h_ref, None, j, n, h)
            return h, aj * p

        e, pe = lax.fori_loop(0, n, p1, (zero, zero + 1.0))
        c, _ = _seg_carries(e, None, pe, None, False)

        def p2(j, p):
            p = _seg_rows(a_ref, None, j, n) * p
            _seg_store(h_ref, None, j, n, _seg_rows(h_ref, None, j, n) + p * c)
            return p

        lax.fori_loop(0, n, p2, zero + 1.0)

    blk = pl.BlockSpec((L, LANE), lambda g: (0, g))
    return pl.pallas_call(
        body, grid=(BW // LANE,), name="rg_scan_fwd", in_specs=[blk, blk], out_specs=blk,
        out_shape=jax.ShapeDtypeStruct((L, BW), F32),
        compiler_params=pltpu.CompilerParams(vmem_limit_bytes=VMEM_LIMIT),
    )(a, b)


def rg_scan_bwd(a, h, dh, L):
    n = L // NSEG

    def body(a_ref, h_ref, dh_ref, da_ref, db_ref):
        zero = jnp.zeros((NSEG, LANE), F32)
        rows = lax.broadcasted_iota(jnp.int32, (NSEG, LANE), 0)
        a_edge = jnp.where(rows == NSEG - 1, 0.0, pltpu.roll(_seg_rows(a_ref, None, 0, n), NSEG - 1, 0))
        h_edge = jnp.where(rows == 0, 0.0, pltpu.roll(_seg_rows(h_ref, None, n - 1, n), 1, 0))

        def mult(j):
            return jnp.where(j == n - 1, a_edge, _seg_rows(a_ref, None, jnp.minimum(j + 1, n - 1), n))

        def p1(jj, st):
            j = n - 1 - jj
            g, p = st
            m = mult(j)
            g = m * g + _seg_rows(dh_ref, None, j, n)
            _seg_store(db_ref, None, j, n, g)
            return g, m * p

        e, pe = lax.fori_loop(0, n, p1, (zero, zero + 1.0))
        c, _ = _seg_carries(e, None, pe, None, True)

        def p2(jj, p):
            j = n - 1 - jj
            p = mult(j) * p
            g = _seg_rows(db_ref, None, j, n) + p * c
            _seg_store(db_ref, None, j, n, g)
            hp = jnp.where(j == 0, h_edge, _seg_rows(h_ref, None, jnp.maximum(j - 1, 0), n))
            _seg_store(da_ref, None, j, n, g * hp)
            return p

        lax.fori_loop(0, n, p2, zero + 1.0)

    blk = pl.BlockSpec((L, LANE), lambda g: (0, g))
    return pl.pallas_call(
        body, grid=(BW // LANE,), name="rg_scan_bwd", in_specs=[blk, blk, blk], out_specs=[blk, blk],
        out_shape=[jax.ShapeDtypeStruct((L, BW), F32)] * 2,
        compiler_params=pltpu.CompilerParams(vmem_limit_bytes=VMEM_LIMIT),
    )(a, h, dh)


def _hg_consts(C):
    t = lax.broadcasted_iota(jnp.int32, (C, C), 0)
    s = lax.broadcasted_iota(jnp.int32, (C, C), 1)
    tril = (s <= t).astype(F32)
    diag = (s == t).astype(F32)
    levels = []
    k = 1
    while (1 << k) <= C:
        m = 1 << (k - 1)
        same = (t >> k) == (s >> k)
        t_right = ((t >> (k - 1)) & 1) == 1
        s_left = ((s >> (k - 1)) & 1) == 0
        mask = jnp.logical_and(same, jnp.logical_and(t_right, s_left)).astype(F32)
        bnd = ((t >> k) << k) + (m - 1)
        levels.append((mask, (s <= bnd).astype(F32)))
        k += 1
    return tril, diag, levels


def hg_chunk(st, q, z, v, lb):
    C = q.shape[0]
    tril, diag, levels = _hg_consts(C)
    sig = jax.nn.sigmoid(z)
    lf = jnp.log(lb + (1.0 - lb) * sig)
    k = (1.0 - lb) * jax.nn.sigmoid(-z)
    qh = jax.nn.silu(q)
    b = mm_exact(tril, lf)
    blast = jnp.sum(lf, axis=0, keepdims=True)
    qe = qh * jnp.exp(b)
    kd = k * jnp.exp(blast - b)
    scaled = []
    for _, sel in levels:
        ref = mm_exact(sel, lf)
        scaled.append((qh * jnp.exp(jnp.minimum(b - ref, 0.0)), k * jnp.exp(jnp.minimum(ref - b, 0.0))))
    outs, news = [], []
    for h in range(HG_HEADS):
        sl = slice(h * HG_D, (h + 1) * HG_D)
        st_h = st[h * HG_D:(h + 1) * HG_D, :]
        sc = diag * mma_nt(qh[:, sl], k[:, sl])
        for (mask, _), (qt, kt) in zip(levels, scaled):
            sc = sc + mask * mma_nt(qt[:, sl], kt[:, sl])
        outs.append(mma_nt(qe[:, sl], st_h) + mma_nn(sc, v[:, sl]))
        news.append(st_h * jnp.exp(blast[:, sl]) + mma_tn(v[:, sl], kd[:, sl]))
    return jnp.concatenate(news, axis=0), jnp.concatenate(outs, axis=1)


def hg_fwd(qzv, lb, L):
    C = HG_CHUNK
    nc = L // C

    def body(q_ref, z_ref, v_ref, lb_ref, o_ref, sst_ref, st_ref):
        @pl.when(pl.program_id(0) == 0)
        def _():
            st_ref[...] = jnp.zeros_like(st_ref)

        st = st_ref[...]
        sst_ref[...] = st
        new, o = hg_chunk(st, q_ref[...], z_ref[...], v_ref[...], lb_ref[...])
        st_ref[...] = new
        o_ref[...] = o

    col = lambda cb: pl.BlockSpec((C, BW), functools.partial(lambda c, cb: (c, cb), cb=cb))
    return pl.pallas_call(
        body, grid=(nc,), name="hg_fwd",
        in_specs=[col(0), col(1), col(2), pl.BlockSpec((1, BW), lambda c: (0, 0))],
        out_specs=[pl.BlockSpec((C, BW), lambda c: (c, 0)), pl.BlockSpec((None, BW, HG_D), lambda c: (c, 0, 0))],
        out_shape=[jax.ShapeDtypeStruct((L, BW), F32), jax.ShapeDtypeStruct((nc, BW, HG_D), F32)],
        scratch_shapes=[pltpu.VMEM((BW, HG_D), F32)],
        compiler_params=pltpu.CompilerParams(vmem_limit_bytes=VMEM_LIMIT, dimension_semantics=("arbitrary",)),
    )(qzv, qzv, qzv, lb)


def hg_bwd(qzv, lb, sst, do, L):
    C = HG_CHUNK
    nc = L // C

    def body(q_ref, z_ref, v_ref, lb_ref, sst_ref, do_ref, dq_ref, dz_ref, dv_ref, dlb_ref, dst_ref):
        @pl.when(pl.program_id(0) == 0)
        def _():
            dst_ref[...] = jnp.zeros_like(dst_ref)
            dlb_ref[...] = jnp.zeros_like(dlb_ref)

        _, vjp = jax.vjp(hg_chunk, sst_ref[...], q_ref[...], z_ref[...], v_ref[...], lb_ref[...])
        dst, dq, dz, dv, dlb = vjp((dst_ref[...], do_ref[...]))
        dst_ref[...] = dst
        dq_ref[...] = dq
        dz_ref[...] = dz
        dv_ref[...] = dv
        dlb_ref[...] += dlb

    col = lambda cb: pl.BlockSpec((C, BW), functools.partial(lambda c, cb: (nc - 1 - c, cb), cb=cb))
    rev = pl.BlockSpec((C, BW), lambda c: (nc - 1 - c, 0))
    return pl.pallas_call(
        body, grid=(nc,), name="hg_bwd",
        in_specs=[col(0), col(1), col(2), pl.BlockSpec((1, BW), lambda c: (0, 0)),
                  pl.BlockSpec((None, BW, HG_D), lambda c: (nc - 1 - c, 0, 0)), rev],
        out_specs=[rev, rev, rev, pl.BlockSpec((1, BW), lambda c: (0, 0))],
        out_shape=[jax.ShapeDtypeStruct((L, BW), F32)] * 3 + [jax.ShapeDtypeStruct((1, BW), F32)],
        scratch_shapes=[pltpu.VMEM((BW, HG_D), F32)],
        compiler_params=pltpu.CompilerParams(vmem_limit_bytes=VMEM_LIMIT, dimension_semantics=("arbitrary",)),
    )(qzv, qzv, qzv, lb, sst, do)


def _shift_down(x, d, rows, L):
    if d == 0:
        return x
    wrapped = jnp.where((rows & (NSEG - 1)) == 0, 0.0, pltpu.roll(x, NSEG * d + 1, 0))
    return jnp.where(rows < NSEG * d, wrapped, pltpu.roll(x, NSEG * d, 0))


def _shift_up(x, d, rows, L):
    if d == 0:
        return x
    wrapped = jnp.where((rows & (NSEG - 1)) == NSEG - 1, 0.0, pltpu.roll(x, L - (NSEG * d + 1), 0))
    return jnp.where(rows >= L - NSEG * d, wrapped, pltpu.roll(x, L - NSEG * d, 0))


def conv_fwd(proj, w, b, L):
    def body(x_ref, w_ref, b_ref, o_ref):
        x = x_ref[...]
        rows = lax.broadcasted_iota(jnp.int32, x.shape, 0)
        acc = jnp.broadcast_to(b_ref[...], x.shape)
        for k in range(CONV_W):
            acc = acc + w_ref[pl.ds(k, 1), :] * _shift_down(x, CONV_W - 1 - k, rows, L)
        o_ref[...] = acc

    nl = BW // LANE
    return pl.pallas_call(
        body, grid=(nl,), name="conv_fwd",
        in_specs=[pl.BlockSpec((L, LANE), lambda g: (0, 5 * nl + g)), pl.BlockSpec((CONV_W, LANE), lambda g: (0, g)),
                  pl.BlockSpec((1, LANE), lambda g: (0, g))],
        out_specs=pl.BlockSpec((L, LANE), lambda g: (0, g)),
        out_shape=jax.ShapeDtypeStruct((L, BW), F32),
        compiler_params=pltpu.CompilerParams(vmem_limit_bytes=VMEM_LIMIT),
    )(proj, w, b)


def conv_bwd(proj, w, dxc, L):
    def body(x_ref, w_ref, d_ref, dx_ref, dw_ref, db_ref):
        x, d = x_ref[...], d_ref[...]
        rows = lax.broadcasted_iota(jnp.int32, x.shape, 0)
        acc = jnp.zeros_like(x)
        for k in range(CONV_W):
            acc = acc + w_ref[pl.ds(k, 1), :] * _shift_up(d, CONV_W - 1 - k, rows, L)
            dw_ref[pl.ds(k, 1), :] = jnp.sum(d * _shift_down(x, CONV_W - 1 - k, rows, L), axis=0, keepdims=True)
        dx_ref[...] = acc
        db_ref[...] = jnp.sum(d, axis=0, keepdims=True)

    nl = BW // LANE
    blk = pl.BlockSpec((L, LANE), lambda g: (0, g))
    return pl.pallas_call(
        body, grid=(nl,), name="conv_bwd",
        in_specs=[pl.BlockSpec((L, LANE), lambda g: (0, 5 * nl + g)), pl.BlockSpec((CONV_W, LANE), lambda g: (0, g)), blk],
        out_specs=[blk, pl.BlockSpec((CONV_W, LANE), lambda g: (0, g)), pl.BlockSpec((1, LANE), lambda g: (0, g))],
        out_shape=[jax.ShapeDtypeStruct((L, BW), F32), jax.ShapeDtypeStruct((CONV_W, BW), F32),
                   jax.ShapeDtypeStruct((1, BW), F32)],
        compiler_params=pltpu.CompilerParams(vmem_limit_bytes=VMEM_LIMIT),
    )(proj, w, dxc)


def loss_fwd_bwd(x, fw, target, L, tm):
    def fn(x, fw, t):
        err = jnp.square(_rms(x, fw) - t)
        return jnp.sum(0.5 * jnp.mean(err, axis=-1, keepdims=True), axis=0, keepdims=True)

    def body(x_ref, fw_ref, t_ref, l_ref, dx_ref, dfw_ref):
        i = pl.program_id(0)
        t = t_ref[...]
        val, vjp = jax.vjp(lambda x, fw: fn(x, fw, t), x_ref[...], fw_ref[...])
        dx, dfw = vjp(jnp.ones((1, 1), F32))
        dx_ref[...] = dx

        @pl.when(i == 0)
        def _():
            l_ref[...] = jnp.zeros_like(l_ref)
            dfw_ref[...] = jnp.zeros_like(dfw_ref)

        l_ref[...] += jnp.broadcast_to(val, l_ref.shape)
        dfw_ref[...] += dfw

    row = pl.BlockSpec((tm, D_MODEL), lambda i: (i, 0))
    vec = pl.BlockSpec((1, D_MODEL), lambda i: (0, 0))
    return pl.pallas_call(
        body, grid=(L // tm,), name="loss_fwd_bwd", in_specs=[row, vec, row],
        out_specs=[pl.BlockSpec((1, LANE), lambda i: (0, 0)), row, vec],
        out_shape=[jax.ShapeDtypeStruct((1, LANE), F32), jax.ShapeDtypeStruct((L, D_MODEL), F32),
                   jax.ShapeDtypeStruct((1, D_MODEL), F32)],
        compiler_params=pltpu.CompilerParams(vmem_limit_bytes=VMEM_LIMIT, dimension_semantics=("arbitrary",)),
    )(x, fw, target)


def adamw(w, g, m, v):
    rows, cols = w.shape
    tr = _row_tile(rows, cols, budget=1024 * 1024)
    c1 = 1.0 - ADAM_B1 ** ADAM_STEP
    c2 = 1.0 - ADAM_B2 ** ADAM_STEP

    def body(w_ref, g_ref, m_ref, v_ref, d_ref, nm_ref, nv_ref):
        g = g_ref[...]
        nm = ADAM_B1 * m_ref[...] + (1.0 - ADAM_B1) * g
        nv = ADAM_B2 * v_ref[...] + (1.0 - ADAM_B2) * jnp.square(g)
        d_ref[...] = -ADAM_LR * ((nm / c1) / (jnp.sqrt(nv / c2) + ADAM_EPS) + ADAM_WD * w_ref[...])
        nm_ref[...] = nm
        nv_ref[...] = nv

    blk = pl.BlockSpec((tr, cols), lambda i: (i, 0))
    return pl.pallas_call(
        body, grid=(rows // tr,), name="adamw", in_specs=[blk] * 4, out_specs=[blk] * 3,
        out_shape=[jax.ShapeDtypeStruct((rows, cols), F32)] * 3,
    )(w, g, m, v)


def s5_prep(lam_re, lam_im, log_dt, b_re, b_im, c_re, c_im):
    lr = jnp.minimum(lam_re, -1e-4)
    li = lam_im
    dt = jnp.exp(log_dt)[:, None]
    mag = jnp.exp(lr * dt)
    ar = mag * jnp.cos(li * dt)
    ai = mag * jnp.sin(li * dt)
    den = lr * lr + li * li
    fr = ((ar - 1.0) * lr + ai * li) / den
    fi = (ai * lr - (ar - 1.0) * li) / den
    bbr = fr[..., None] * b_re - fi[..., None] * b_im
    bbi = fr[..., None] * b_im + fi[..., None] * b_re
    emb_b = lambda bb: _block_diag(bb.transpose(0, 2, 1).reshape(BW, S5_STATE), S5_GROUPS)
    emb_c = lambda cc: _block_diag(cc.transpose(0, 2, 1).reshape(S5_N, S5_GROUP), S5_GROUPS)
    bmat = jnp.concatenate([emb_b(bbr), emb_b(bbi)], axis=1)
    cmat = jnp.concatenate([emb_c(c_re), -emb_c(c_im)], axis=0)
    nb = S5_N // LANE
    return ar.reshape(nb, 1, LANE), ai.reshape(nb, 1, LANE), bmat, cmat


def _block_diag(stacked, groups):
    rows, c = stacked.shape
    r = rows // groups
    row_g = jnp.arange(rows)[:, None] // r
    col_g = jnp.arange(groups * c)[None, :] // c
    return jnp.where(row_g == col_g, jnp.tile(stacked, (1, groups)), 0.0)


def rg_prep(w):
    return _block_diag(w.reshape(BW, RG_BLOCK), RG_BLOCKS)


def hg_prep(logits):
    p = jax.nn.softmax(logits, axis=0)
    return jnp.cumsum(p, axis=0) - p[0]


def _head_mean_matrix():
    r = jnp.arange(BW) // HG_D
    return (r[:, None] == r[None, :]).astype(F32) / HG_D


def _to_segment_order(a):
    L = a.shape[0]
    return a.reshape(NSEG, L // NSEG, -1).transpose(1, 0, 2).reshape(a.shape)


def _to_time_order(a):
    L = a.shape[0]
    return a.reshape(L // NSEG, NSEG, -1).transpose(1, 0, 2).reshape(a.shape)


def _const(*idx):
    return lambda s, i: idx


def _rows(cb=0):
    return lambda s, i: (i, cb)


def _sum_parts(name, first, parts, shape):
    return add_n(name, [(first, ())] + [(parts, (s,)) for s in range(NSH)], shape)


def _ffn_weight_specs(l, j):
    F = D_FF // NSH
    one = pl.Buffered(1)
    return [pl.BlockSpec((None, NSH, D_MODEL, F), lambda i: (j, 0, 0, 0), pipeline_mode=one),
            pl.BlockSpec((None, NSH, D_MODEL, F), lambda i: (j, 0, 0, 0), pipeline_mode=one),
            pl.BlockSpec((None, NSH, F, D_MODEL), lambda i: (j, 0, 0, 0), pipeline_mode=one)]


def ffn_fwd(name, x, W, l, j, k, L, tm):
    D, F = D_MODEL, D_FF // NSH

    def body(x_ref, nw_ref, wg_ref, wu_ref, wd_ref, y_ref, g_ref, u_ref):
        x = x_ref[...]
        h = _rms(x, nw_ref[...]).astype(MMT)
        y = x
        for s in range(NSH):
            g = _dg(h, wg_ref[s], 1, 0)
            u = _dg(h, wu_ref[s], 1, 0)
            g_ref[s] = g.astype(g_ref.dtype)
            u_ref[s] = u.astype(u_ref.dtype)
            y = y + 0.5 * _dg((jax.nn.silu(g) * u).astype(MMT), wd_ref[s], 1, 0)
        y_ref[...] = y

    row = pl.BlockSpec((tm, D), lambda i: (i, 0))
    act = pl.BlockSpec((NSH, tm, F), lambda i: (0, i, 0))
    return pl.pallas_call(
        body, grid=(L // tm,), name=name,
        in_specs=[row, pl.BlockSpec((None, None, 1, D), lambda i: (l, k, 0, 0))] + _ffn_weight_specs(l, j),
        out_specs=[row, act, act],
        out_shape=[jax.ShapeDtypeStruct((L, D), F32), jax.ShapeDtypeStruct((NSH, L, F), MMT),
                   jax.ShapeDtypeStruct((NSH, L, F), MMT)],
        compiler_params=pltpu.CompilerParams(vmem_limit_bytes=VMEM_LIMIT, dimension_semantics=("arbitrary",)),
    )(x, W["nw"], W["L"][l]["wg"], W["L"][l]["wu"], W["L"][l]["wd"])


def ffn_bwd(name, x, g, u, dy, W, bufs, l, j, k, L, tm):
    D, F = D_MODEL, D_FF // NSH
    tm = min(TM_WGRAD, L)

    def body(x_ref, nw_ref, dy_ref, g_ref, u_ref, wg_ref, wu_ref, wd_ref, *rest):
        part_ref, dnw_ref, dwg_ref, dwu_ref, dwd_ref = rest[-5:]
        s, i = pl.program_id(0), pl.program_id(1)
        x, nw = x_ref[...], nw_ref[...]
        r = lax.rsqrt(jnp.mean(x * x, axis=-1, keepdims=True) + EPS)
        xhat = x * r
        h = (xhat * nw).astype(MMT)
        half_dy = (0.5 * dy_ref[...]).astype(MMT)
        gs, us = g_ref[...].astype(F32), u_ref[...].astype(F32)
        sig = jax.nn.sigmoid(gs)
        act = gs * sig
        da = _dg(half_dy, wd_ref[...], 1, 1)
        du = (da * act).astype(MMT)
        dg = (da * us * (sig * (1.0 + gs * (1.0 - sig)))).astype(MMT)
        dh = _dg(dg, wg_ref[...], 1, 1) + _dg(du, wu_ref[...], 1, 1)
        dxh = dh * nw
        part_ref[...] = r * (dxh - xhat * jnp.mean(dxh * xhat, axis=-1, keepdims=True))
        grads = (_dg(h, dg, 0, 0), _dg(h, du, 0, 0), _dg((act * us).astype(MMT), half_dy, 0, 0))
        dnw = jnp.sum(dh * xhat, axis=0, keepdims=True)
        first = jnp.logical_and(s == 0, i == 0)
        for ref, val, start in zip((dwg_ref, dwu_ref, dwd_ref, dnw_ref), grads + (dnw,), (i == 0, i == 0, i == 0, first)):
            @pl.when(start)
            def _(ref=ref, val=val):
                ref[...] = val

            @pl.when(jnp.logical_not(start))
            def _(ref=ref, val=val):
                ref[...] += val

    row = pl.BlockSpec((tm, D), lambda s, i: (i, 0))
    act = pl.BlockSpec((None, tm, F), lambda s, i: (s, i, 0))
    wsp = lambda r, c: pl.BlockSpec((None, None, r, c), lambda s, i: (j, s, 0, 0))
    gsp = lambda r, c: pl.BlockSpec((None, None, r, c), lambda s, i: (0, s, 0, 0))
    part, dnw, bufs[("ffn_gate", l, j)], bufs[("ffn_up", l, j)], bufs[("ffn_down", l, j)] = pl.pallas_call(
        body, grid=(NSH, L // tm), name=name,
        in_specs=[row, pl.BlockSpec((None, None, 1, D), lambda s, i: (l, k, 0, 0)), row, act, act,
                  wsp(D, F), wsp(D, F), wsp(F, D)],
        out_specs=[pl.BlockSpec((None, tm, D), lambda s, i: (s, i, 0)), pl.BlockSpec((1, D), lambda s, i: (0, 0)),
                   gsp(D, F), gsp(D, F), gsp(F, D)],
        out_shape=[jax.ShapeDtypeStruct((NSH, L, D), F32), jax.ShapeDtypeStruct((1, D), F32)]
        + [jax.ShapeDtypeStruct((1, NSH, D, F), F32)] * 2 + [jax.ShapeDtypeStruct((1, NSH, F, D), F32)],
        compiler_params=pltpu.CompilerParams(vmem_limit_bytes=VMEM_LIMIT, dimension_semantics=("arbitrary", "arbitrary")),
    )(x, W["nw"], dy, g, u, W["L"][l]["wg"], W["L"][l]["wu"], W["L"][l]["wd"])
    return _sum_parts(name + "_dx", dy, part, (L, D)), dnw


def layer_fwd(l, x0, W, P, L, tm):
    D = D_MODEL
    tmm = tm
    tm = min(TM_FWD, L)
    n_i = L // tm
    x1, g0, u0 = ffn_fwd(f"ffn_fwd_{l}0", x0, W, l, 0, 0, L, tm)
    proj = tile_fwd(
        lambda x, nw, win, s: pre_core(x, nw, win), f"pre_fwd_{l}", n_i, NSH,
        [(x1, (tm, D), _rows()), (W["nw"], (None, None, 1, D), _const(l, 1, 0, 0)),
         (W["L"][l]["win"], (None, D, IN_TOTAL // NSH), lambda s, i: (s, 0, 0))],
        [((L, IN_TOTAL), F32, (tm, IN_TOTAL // NSH), lambda s, i: (i, s), False)], s_outer=True)[0]
    nb = S5_N // LANE
    blk3 = lambda s, i: (0, i, 0)
    bur, bui = tile_fwd(
        lambda u, bmat, s: s5_pre_core(u, bmat), f"s5pre_fwd_{l}", n_i, 1,
        [(proj, (tm, BW), _rows(0)), (P["bmat"], (None, BW, 2 * S5_N), _const(l, 0, 0))],
        [((nb, L, LANE), F32, (nb, tm, LANE), blk3, False)] * 2)
    xr, xi = s5_scan_fwd(bur, bui, P["ar"][l], P["ai"][l], L)
    qzv = _to_time_order(proj[:, BW:4 * BW])
    o_t, sst = hg_fwd(qzv, P["lb"][l], L)
    o = _to_segment_order(o_t)
    xc = conv_fwd(proj, W["convw"][l], P["convb"][l], L)
    vec = (None, 1, BW)
    a, b = tile_fwd(
        lambda xc, wa, ba, wx, bx, lam, s: gates_core(xc, wa, ba, wx, bx, lam), f"gates_fwd_{l}", n_i, 1,
        [(xc, (tm, BW), _rows()), (P["wa"], (None, BW, BW), _const(l, 0, 0)), (P["ba"], vec, _const(l, 0, 0)),
         (P["wx"], (None, BW, BW), _const(l, 0, 0)), (P["bx"], vec, _const(l, 0, 0)), (P["lam"], vec, _const(l, 0, 0))],
        [((L, BW), F32, (tm, BW), _rows(), False)] * 2)
    hs = rg_scan_fwd(a, b, L)
    ya, yb, yc = tile_fwd(
        lambda *a: mid_core(*a[:-1]), f"mid_fwd_{l}", L // tmm, 1,
        [(xr, (nb, tmm, LANE), blk3), (xi, (nb, tmm, LANE), blk3), (proj, (tmm, BW), _rows(0)), (o, (tmm, BW), _rows()),
         (proj, (tmm, BW), _rows(4)), (hs, (tmm, BW), _rows()), (proj, (tmm, BW), _rows(6)),
         (P["hmat"], (BW, BW), _const(0, 0)), (P["cmat"], (None, 2 * S5_N, BW), _const(l, 0, 0)), (P["d"], vec, _const(l, 0, 0)),
         (W["L"][l]["gluw"], (BW, BW), _const(0, 0)), (P["glub"], vec, _const(l, 0, 0)), (P["hgw"], vec, _const(l, 0, 0))],
        [((L, BW), F32, (tmm, BW), _rows(), False)] * 3)
    x2 = tile_fwd(
        lambda x, *rest: (x + merge_core(*rest[:-1])[0],), f"merge_fwd_{l}", n_i, 1,
        [(x1, (tm, D), _rows()), (ya, (tm, BW), _rows()), (yb, (tm, BW), _rows()), (yc, (tm, BW), _rows())]
        + [(proj, (tm, BW), _rows(7 + k)) for k in range(6)]
        + [(W["L"][l]["pfull"], (3, BW, D), _const(0, 0, 0)), (W["L"][l]["woutfull"], (D, D), _const(0, 0))],
        [((L, D), F32, (tm, D), _rows(), False)])[0]
    x3, g1, u1 = ffn_fwd(f"ffn_fwd_{l}1", x2, W, l, 1, 2, L, tm)
    saved = dict(x0=x0, x1=x1, x2=x2, proj=proj, xr=xr, xi=xi, o=o, sst=sst, xc=xc, a=a, hs=hs, ya=ya, yb=yb, yc=yc,
                 qzv=qzv, g0=g0, u0=u0, g1=g1, u1=u1)
    return x3, saved


def layer_bwd(l, dx3, sv, W, P, bufs, L, tm, ready=lambda l, group: None):
    D = D_MODEL
    n_i = L // tm
    nb = S5_N // LANE
    dq = D // NSH
    vec = (None, 1, BW)
    vout = ((1, BW), (1, BW), _const(0, 0), "acc_all")
    blk3 = lambda s, i: (0, i, 0)
    small = {}
    proj = sv["proj"]

    dx2, dnw2 = ffn_bwd(f"ffn_bwd_{l}1", sv["x2"], sv["g1"], sv["u1"], dx3, W, bufs, l, 1, 2, L, tm)
    ready(l, "ffn1")

    rw256 = ((L, BW), (tm, BW), _rows(), "write")
    res = tile_bwd(
        merge_core, f"merge_bwd_{l}", n_i, 1,
        [(sv["ya"], (tm, BW), _rows(), "r"), (sv["yb"], (tm, BW), _rows(), "r"), (sv["yc"], (tm, BW), _rows(), "r")]
        + [(proj, (tm, BW), _rows(7 + k), "r") for k in range(6)]
        + [(W["L"][l]["pfull"], (3, BW, D), _const(0, 0, 0), "w"), (W["L"][l]["woutfull"], (D, D), _const(0, 0), "w")],
        [(dx2, (tm, D), _rows())],
        [rw256] * 9
        + [((3, BW, D), (3, BW, D), _const(0, 0, 0), "acc_all"), ((D, D), (D, D), _const(0, 0), "acc_all")])
    dya, dyb, dyc = res[:3]
    dgm = res[3:9]
    bufs[("branch_proj", l)], bufs[("w_out", l)] = res[9:]
    ready(l, "merge")

    tmm = tm
    rw = ((L, BW), (tmm, BW), _rows(), "write")
    xw = ((nb, L, LANE), (nb, tmm, LANE), blk3, "write")
    res = tile_bwd(
        mid_core, f"mid_bwd_{l}", L // tmm, 1,
        [(sv["xr"], (nb, tmm, LANE), blk3, "r"), (sv["xi"], (nb, tmm, LANE), blk3, "r"), (proj, (tmm, BW), _rows(0), "r"),
         (sv["o"], (tmm, BW), _rows(), "r"), (proj, (tmm, BW), _rows(4), "r"), (sv["hs"], (tmm, BW), _rows(), "r"),
         (proj, (tmm, BW), _rows(6), "r"), (P["hmat"], (BW, BW), _const(0, 0), "c"),
         (P["cmat"], (None, 2 * S5_N, BW), _const(l, 0, 0), "w"), (P["d"], vec, _const(l, 0, 0), "p"),
         (W["L"][l]["gluw"], (BW, BW), _const(0, 0), "w"), (P["glub"], vec, _const(l, 0, 0), "p"),
         (P["hgw"], vec, _const(l, 0, 0), "p")],
        [(dya, (tmm, BW), _rows()), (dyb, (tmm, BW), _rows()), (dyc, (tmm, BW), _rows())],
        [xw, xw, rw, rw, rw, rw, rw,
         ((DEPTH, 2 * S5_N, BW), (None, 2 * S5_N, BW), _const(l, 0, 0), "acc_all", bufs.get("cmat")), vout,
         ((BW, BW), (BW, BW), _const(0, 0), "acc_all"), vout, vout])
    dxr, dxi, du_skip, do, dg_b, dhs, dgate_c, bufs["cmat"], dd, bufs[("s5_glu_w", l)], dglub, dhgw = res
    small["s5_d"], small["s5_glu_b"], small["hg_norm_w"] = dd[0], dglub[0], dhgw[0]
    ready(l, "mid")

    da, db = rg_scan_bwd(sv["a"], sv["hs"], dhs, L)
    wmat = lambda key: ((DEPTH, BW, BW), (None, BW, BW), _const(l, 0, 0), "acc_all", bufs.get(key))
    res = tile_bwd(
        gates_core, f"gates_bwd_{l}", n_i, 1,
        [(sv["xc"], (tm, BW), _rows(), "r"), (P["wa"], (None, BW, BW), _const(l, 0, 0), "w"), (P["ba"], vec, _const(l, 0, 0), "p"),
         (P["wx"], (None, BW, BW), _const(l, 0, 0), "w"), (P["bx"], vec, _const(l, 0, 0), "p"), (P["lam"], vec, _const(l, 0, 0), "p")],
        [(da, (tm, BW), _rows()), (db, (tm, BW), _rows())],
        [((L, BW), (tm, BW), _rows(), "write"), wmat("wa"), vout, wmat("wx"), vout, vout])
    dxc, bufs["wa"], dba, bufs["wx"], dbx, dlam = res
    small["rg_ba"], small["rg_bx"], small["rg_lambda"] = dba[0], dbx[0], dlam[0]
    dx_c, dconvw, dconvb = conv_bwd(proj, W["convw"][l], dxc, L)
    small["rg_conv_w"], small["rg_conv_b"] = dconvw, dconvb[0]

    dq_b, dz_b, dv_b, dlb = hg_bwd(sv["qzv"], P["lb"][l], sv["sst"], _to_time_order(do), L)
    dq_b, dz_b, dv_b = [_to_segment_order(a) for a in (dq_b, dz_b, dv_b)]

    gr, gi, dar, dai = s5_scan_bwd(dxr, dxi, sv["xr"], sv["xi"], P["ar"][l], P["ai"][l], L)
    du_pre, bufs["bmat"] = tile_bwd(
        s5_pre_core, f"s5pre_bwd_{l}", n_i, 1,
        [(proj, (tm, BW), _rows(0), "r"), (P["bmat"], (None, BW, 2 * S5_N), _const(l, 0, 0), "w")],
        [(gr, (nb, tm, LANE), blk3), (gi, (nb, tm, LANE), blk3)],
        [((L, BW), (tm, BW), _rows(), "write"),
         ((DEPTH, BW, 2 * S5_N), (None, BW, 2 * S5_N), _const(l, 0, 0), "acc_all", bufs.get("bmat"))])
    du_a = add_n(f"du_a_{l}", [(du_skip, ()), (du_pre, ())], (L, BW))
    prep_ct = dict(dar=dar, dai=dai, dlb=dlb)

    pieces = [du_a, dq_b, dz_b, dv_b, dg_b, dx_c, dgate_c, *dgm]
    per_piece, per_shard = BW // LANE, IN_TOTAL // NSH // LANE
    part, dnw1 = None, []
    tmw = min(TM_WGRAD, L)
    for s in range(NSH):
        groups = [(pieces[g // per_piece], (tmw, LANE), _rows(g % per_piece))
                  for g in range(s * per_shard, (s + 1) * per_shard)]
        part, dnw_s, bufs[("w_in", l)] = tile_bwd(
            pre_core, f"pre_bwd_{l}{s}", L // tmw, 1,
            [(sv["x1"], (tmw, D), _rows(), "r"), (W["nw"], (None, None, 1, D), _const(l, 1, 0, 0), "p"),
             (W["L"][l]["win"], (None, D, IN_TOTAL // NSH), _const(s, 0, 0), "w")],
            [groups],
            [((NSH, L, D), (None, tmw, D), functools.partial(lambda _s, i, s: (s, i, 0), s=s), "write", part),
             ((1, D), (1, D), _const(0, 0), "acc_all"),
             ((1, NSH, D, IN_TOTAL // NSH), (None, None, D, IN_TOTAL // NSH), _const(0, s, 0, 0), "acc_all",
              bufs.get(("w_in", l)))])
        dnw1.append(dnw_s)
    dnw1 = (dnw1[0] + dnw1[1]) + (dnw1[2] + dnw1[3])
    dx1 = _sum_parts(f"pre_bwd_{l}_dx", dx2, part, (L, D))
    ready(l, "pre")

    dx0, dnw0 = ffn_bwd(f"ffn_bwd_{l}0", sv["x0"], sv["g0"], sv["u0"], dx1, W, bufs, l, 0, 0, L, tm)
    ready(l, "ffn0")
    small["norm_w"] = jnp.concatenate([dnw0, dnw1, dnw2], axis=0)
    return dx0, small, prep_ct


SMALL_RAW = ("s5_lambda_re", "s5_lambda_im", "s5_log_dt", "s5_b_re", "s5_b_im", "s5_c_re", "s5_c_im", "s5_d", "s5_glu_b",
             "hg_lb_logits", "hg_norm_w", "rg_conv_b", "rg_wa", "rg_ba", "rg_wx", "rg_bx", "rg_lambda", "final_norm_w")
DEPTH = 2


def local_step(x, target, W, raw, layer_weights=None, layer_grads=None):
    L = x.shape[0]
    tm = min(256, L)
    col = lambda v: v.reshape(DEPTH, 1, BW)
    (ar, ai, bmat, cmat), s5_vjp = jax.vjp(jax.vmap(s5_prep), *[raw[k] for k in SMALL_RAW[:7]])
    (wa, wx), rg_vjp = jax.vjp(lambda a, b: (jax.vmap(rg_prep)(a), jax.vmap(rg_prep)(b)), raw["rg_wa"], raw["rg_wx"])
    lb, hg_vjp = jax.vjp(hg_prep, raw["hg_lb_logits"])
    P = dict(
        ar=[ar[l] for l in range(DEPTH)], ai=[ai[l] for l in range(DEPTH)],
        bmat=bmat.astype(MMT), cmat=cmat.astype(MMT), wa=wa.astype(MMT), wx=wx.astype(MMT),
        lb=[lb[l].reshape(1, BW) for l in range(DEPTH)], convb=[raw["rg_conv_b"][l].reshape(1, BW) for l in range(DEPTH)],
        ba=col(raw["rg_ba"]), bx=col(raw["rg_bx"]), lam=col(raw["rg_lambda"]), d=col(raw["s5_d"]),
        glub=col(raw["s5_glu_b"]), hgw=col(raw["hg_norm_w"]), hmat=_head_mean_matrix())

    saved = []
    h = _to_segment_order(x)
    for l in range(DEPTH):
        if layer_weights is not None:
            W["L"][l], h = layer_weights(l, h)
        h, sv = layer_fwd(l, h, W, P, L, tm)
        saved.append(sv)
    loss, dh, dfw = loss_fwd_bwd(h, raw["final_norm_w"].reshape(1, D_MODEL), _to_segment_order(target), L, tm)

    big, per_layer, prep_cts = {}, [None] * DEPTH, [None] * DEPTH
    ready = (lambda l, group: None) if layer_grads is None else (lambda l, group: layer_grads(l, group, big))
    for l in reversed(range(DEPTH)):
        dh, sm, pc = layer_bwd(l, dh, saved[l], W, P, big, L, tm, ready)
        per_layer[l], prep_cts[l] = sm, pc
    dh = _to_time_order(dh)

    small = {k: jnp.stack([per_layer[l][k] for l in range(DEPTH)]) for k in per_layer[0]}
    both = lambda k: jnp.stack([prep_cts[l][k] for l in range(DEPTH)])
    s5_g = s5_vjp((both("dar"), both("dai"), big.pop("bmat"), big.pop("cmat")))
    small.update(zip(SMALL_RAW[:7], s5_g))
    small["rg_wa"], small["rg_wx"] = rg_vjp((big.pop("wa"), big.pop("wx")))
    (small["hg_lb_logits"],) = hg_vjp(jnp.concatenate([prep_cts[l]["dlb"] for l in range(DEPTH)], axis=0))
    small["final_norm_w"] = dfw[0]
    return loss, dh, big, small


ANY = pl.BlockSpec(memory_space=pl.ANY)


def _place():
    x, y, c = lax.axis_index("x"), lax.axis_index("y"), lax.axis_index("c")
    chips = [(1 - x, y), (x, 1 - y), (1 - x, 1 - y)]
    return x, y, c, chips


def _remote(src, dst, send, recv, k, to):
    return pltpu.make_async_remote_copy(src_ref=src, dst_ref=dst, send_sem=send.at[k], recv_sem=recv.at[k],
                                        device_id=to, device_id_type=MESH)


def _comm_call(body, name, ins, out_shapes, n_sem, n_loc):
    return pl.pallas_call(
        body, name=name, in_specs=[ANY] * len(ins), out_specs=[ANY] * len(out_shapes), out_shape=out_shapes,
        scratch_shapes=[pltpu.SemaphoreType.DMA((n_sem,)), pltpu.SemaphoreType.DMA((n_sem,)),
                        pltpu.SemaphoreType.DMA((max(n_loc, 1),))],
    )(*ins)


def gather_shards(name, shards):
    n = len(shards)
    per = 8

    def body(*refs):
        ins, outs = refs[:n], refs[n:2 * n]
        send, recv, _ = refs[2 * n:]
        x, y, c, chips = _place()
        me = 2 * x + y
        sib = (x, y, 1 - c)
        sends = []
        for w in range(n):
            for j, (cx, cy) in enumerate(chips):
                cp = _remote(ins[w].at[c], outs[w].at[c, me], send, recv, per * w + j, (cx, cy, c))
                cp.start()
                sends.append(cp)
        for w in range(n):
            for l in range(2):
                cp = _remote(ins[w].at[l], outs[w].at[l, me], send, recv, per * w + 6 + l, sib)
                cp.start()
                sends.append(cp)
        for w in range(n):
            for j, (cx, cy) in enumerate(chips):
                theirs = outs[w].at[c, 2 * cx + cy]
                _remote(ins[w].at[c], theirs, send, recv, per * w + j, (cx, cy, c)).wait_recv()
                cp = _remote(theirs, theirs, send, recv, per * w + 3 + j, sib)
                cp.start()
                sends.append(cp)
        for w in range(n):
            for j, (cx, cy) in enumerate(chips):
                dst = outs[w].at[1 - c, 2 * cx + cy]
                _remote(dst, dst, send, recv, per * w + 3 + j, sib).wait_recv()
            for l in range(2):
                dst = outs[w].at[l, me]
                _remote(dst, dst, send, recv, per * w + 6 + l, sib).wait_recv()
        for cp in sends:
            cp.wait_send()

    shapes = [jax.ShapeDtypeStruct((2, NSH) + s.shape[1:], s.dtype) for s in shards]
    return _comm_call(body, name, shards, shapes, per * n, 0)


def exchange_halves(name, grads, ranges):
    n = len(grads)

    def body(*refs):
        ins, outs = refs[:n], refs[n:2 * n]
        send, recv, _ = refs[2 * n:]
        x, y, c, _chips = _place()
        cps = []
        for w in range(n):
            h = grads[w].shape[2] // 2
            p0, np_ = ranges[w]
            cp = _remote(ins[w].at[pl.ds(p0, np_), :, pl.ds((1 - c) * h, h)], outs[w], send, recv, w, (x, y, 1 - c))
            cp.start()
            cps.append(cp)
        for cp in cps:
            cp.wait()

    shapes = [jax.ShapeDtypeStruct((r[1], NSH, g.shape[2] // 2, g.shape[3]), g.dtype) for g, r in zip(grads, ranges)]
    return _comm_call(body, name, grads, shapes, n, 0)


def scatter_to_chips(name, halves):
    n = len(halves)

    def body(*refs):
        ins, outs = refs[:n], refs[n:2 * n]
        send, recv, _ = refs[2 * n:]
        x, y, c, chips = _place()
        cps = []
        for w in range(n):
            for j, (cx, cy) in enumerate(chips):
                cp = _remote(ins[w].at[:, 2 * cx + cy], outs[w].at[j], send, recv, 3 * w + j, (cx, cy, c))
                cp.start()
                cps.append(cp)
        for cp in cps:
            cp.wait()

    shapes = [jax.ShapeDtypeStruct((3, h.shape[0]) + h.shape[2:], h.dtype) for h in halves]
    return _comm_call(body, name, halves, shapes, 3 * n, 0)


def share_halves(name, pieces):
    n = len(pieces)

    def body(*refs):
        ins, outs = refs[:n], refs[n:2 * n]
        send, recv, _ = refs[2 * n:]
        x, y, c, _chips = _place()
        cps = []
        for w in range(n):
            cp = _remote(ins[w], outs[w], send, recv, w, (x, y, 1 - c))
            cp.start()
            cps.append(cp)
        for cp in cps:
            cp.wait()

    return _comm_call(body, name, pieces, [jax.ShapeDtypeStruct(p.shape, p.dtype) for p in pieces], n, 0)


def add_own_half(name, g, ra, c, wire, b0):
    nblk, h, cols = ra.shape
    tr = _row_tile(h, cols, mult=16)
    nt = h // tr

    def body(c_ref, g_ref, r_ref, o_ref):
        o_ref[...] = (g_ref[...] + r_ref[...]).astype(o_ref.dtype)

    blk = (None, tr, cols)
    return pl.pallas_call(
        body, name=name,
        grid_spec=pltpu.PrefetchScalarGridSpec(
            num_scalar_prefetch=1, grid=(nblk, nt),
            in_specs=[pl.BlockSpec(blk, lambda s, i, c_ref: (b0 + s, c_ref[0] * nt + i, 0)), pl.BlockSpec(blk, lambda s, i, c_ref: (s, i, 0))],
            out_specs=pl.BlockSpec(blk, lambda s, i, c_ref: (s, i, 0))),
        out_shape=jax.ShapeDtypeStruct(ra.shape, wire),
    )(c.reshape(1), g, ra)


def add_chips(name, hb, rb, me):
    npc, _, h, cols = hb.shape
    tr = _row_tile(h, cols, mult=16)

    def body(me_ref, h_ref, r0, r1, r2, o_ref):
        f = lambda r: r[...].astype(F32)
        o_ref[...] = ((f(h_ref) + f(r0)) + f(r1)) + f(r2)

    rspec = lambda j: pl.BlockSpec((None, None, tr, cols), functools.partial(lambda p, i, me_ref, j: (j, p, i, 0), j=j))
    return pl.pallas_call(
        body, name=name,
        grid_spec=pltpu.PrefetchScalarGridSpec(
            num_scalar_prefetch=1, grid=(npc, h // tr),
            in_specs=[pl.BlockSpec((None, None, tr, cols), lambda p, i, me_ref: (p, me_ref[0], i, 0)), rspec(0), rspec(1), rspec(2)],
            out_specs=pl.BlockSpec((None, tr, cols), lambda p, i, me_ref: (p, i, 0))),
        out_shape=jax.ShapeDtypeStruct((npc, h, cols), F32),
    )(me.reshape(1), hb, rb, rb, rb)


def adamw_halves(name, w, m, v, own, other, c):
    npc, rows, cols = w.shape
    h = rows // 2
    tr = _row_tile(h, cols, budget=1024 * 1024)
    nt = h // tr
    c1 = 1.0 - ADAM_B1 ** ADAM_STEP
    c2 = 1.0 - ADAM_B2 ** ADAM_STEP

    def body(c_ref, w_ref, m_ref, v_ref, own_ref, oth_ref, g_ref, d_ref, nm_ref, nv_ref):
        g = jnp.where(pl.program_id(1) == c_ref[0], own_ref[...], oth_ref[...])
        nm = ADAM_B1 * m_ref[...] + (1.0 - ADAM_B1) * g
        nv = ADAM_B2 * v_ref[...] + (1.0 - ADAM_B2) * jnp.square(g)
        g_ref[...] = g
        d_ref[...] = -ADAM_LR * ((nm / c1) / (jnp.sqrt(nv / c2) + ADAM_EPS) + ADAM_WD * w_ref[...])
        nm_ref[...] = nm
        nv_ref[...] = nv

    full = pl.BlockSpec((None, tr, cols), lambda p, hh, i, c_ref: (p, hh * nt + i, 0))
    half = pl.BlockSpec((None, tr, cols), lambda p, hh, i, c_ref: (p, i, 0))
    return pl.pallas_call(
        body, name=name,
        grid_spec=pltpu.PrefetchScalarGridSpec(
            num_scalar_prefetch=1, grid=(npc, 2, nt),
            in_specs=[full, full, full, half, half], out_specs=[full] * 4),
        out_shape=[jax.ShapeDtypeStruct(w.shape, F32)] * 4,
    )(c.reshape(1), w, m, v, own, other)


WEIGHTS = ("norm_w", "final_norm_w", "ffn_gate", "ffn_up", "ffn_down", "w_in", "branch_proj", "w_out", "s5_lambda_re",
           "s5_lambda_im", "s5_log_dt", "s5_b_re", "s5_b_im", "s5_c_re", "s5_c_im", "s5_d", "s5_glu_w", "s5_glu_b",
           "hg_lb_logits", "hg_norm_w", "rg_conv_w", "rg_conv_b", "rg_wa", "rg_ba", "rg_wx", "rg_bx", "rg_lambda")
BIG = ("ffn_gate", "ffn_up", "ffn_down", "w_in", "branch_proj", "w_out", "s5_glu_w")
SHARDED_SMALL = ("norm_w", "rg_conv_w")
SMALL = SMALL_RAW + SHARDED_SMALL


def _view2d(shape):
    return (1, shape[0]) if len(shape) == 1 else (math.prod(shape[:-1]), shape[-1])


def _small_layout(shapes, row_multiple):
    layout, at = [], 0
    for shape in shapes:
        r, c = _view2d(shape)
        rp = -(-r // 8) * 8
        layout.append((at, r, c, rp))
        at += rp * max(1, c // LANE)
    return layout, -(-at // row_multiple) * row_multiple


def pack_small(name, arrays, row_multiple):
    layout, rows = _small_layout([a.shape for a in arrays], row_multiple)

    def body(*refs):
        out = refs[-1]
        out[...] = jnp.zeros_like(out)
        for ref, (r0, r, c, rp) in zip(refs[:-1], layout):
            if c <= LANE:
                out[r0:r0 + r, 0:c] = ref[...]
            else:
                for q in range(c // LANE):
                    out[r0 + q * rp:r0 + q * rp + r, :] = ref[:, q * LANE:(q + 1) * LANE]

    return pl.pallas_call(
        body, name=name, out_shape=jax.ShapeDtypeStruct((rows, LANE), F32),
        compiler_params=pltpu.CompilerParams(vmem_limit_bytes=VMEM_LIMIT),
    )(*[a.reshape(_view2d(a.shape)) for a in arrays])


def unpack_small(name, packed, shapes):
    layout, _ = _small_layout(shapes, 8)

    def body(p_ref, *outs):
        for ref, (r0, r, c, rp) in zip(outs, layout):
            if c <= LANE:
                ref[...] = p_ref[r0:r0 + r, 0:c]
            else:
                for q in range(c // LANE):
                    ref[:, q * LANE:(q + 1) * LANE] = p_ref[r0 + q * rp:r0 + q * rp + r, :]

    res = pl.pallas_call(
        body, name=name, out_shape=[jax.ShapeDtypeStruct(_view2d(s), F32) for s in shapes],
        compiler_params=pltpu.CompilerParams(vmem_limit_bytes=VMEM_LIMIT),
    )(packed)
    return [a.reshape(s) for a, s in zip(res, shapes)]


HBM = pl.BlockSpec(memory_space=pltpu.HBM)
SEM = pl.BlockSpec(memory_space=pltpu.SEMAPHORE)
EFFECT = pltpu.SideEffectType.DATAFLOW_SIDE_EFFECTING


def split_start(name, srcs, land_shapes, plan, n_send, n_recv):
    ns, nl = len(srcs), len(land_shapes)

    def body(*refs):
        ins, lands = refs[:ns], refs[ns:ns + nl]
        send, recv = refs[ns + nl], refs[ns + nl + 1]
        for src, dst, ks, kr, dev in plan(ins, lands):
            pltpu.make_async_remote_copy(src_ref=src, dst_ref=dst, send_sem=send.at[ks], recv_sem=recv.at[kr],
                                         device_id=dev, device_id_type=MESH).start()
        refs[-1][...] = jnp.zeros_like(refs[-1])

    hbm = lambda a: pltpu.with_memory_space_constraint(a, pltpu.HBM)
    lands = [lax.empty(s.shape, s.dtype) for s in land_shapes]
    out = pl.pallas_call(
        body, name=name,
        out_shape=(pltpu.SemaphoreType.DMA((n_send,)), pltpu.SemaphoreType.DMA((n_recv,)),
                   *[pltpu.HBM(a.shape, a.dtype) for a in srcs], *[pltpu.HBM(s.shape, s.dtype) for s in land_shapes],
                   jax.ShapeDtypeStruct((8, LANE), F32)),
        in_specs=[HBM] * (ns + nl), out_specs=(SEM, SEM, *[HBM] * (ns + nl), pl.BlockSpec(memory_space=pltpu.VMEM)),
        input_output_aliases={k: 2 + k for k in range(ns + nl)},
        compiler_params=pltpu.CompilerParams(has_side_effects=EFFECT),
    )(*[hbm(a) for a in srcs], *[hbm(a) for a in lands])
    return out[:-1], out[-1]


def split_wait(name, handles, n_src, waits, after):
    send, recv, *bufs = handles
    nb = len(bufs)

    def body(*refs):
        ins, lands = refs[:n_src], refs[n_src:nb]
        send_sem, recv_sem = refs[nb], refs[nb + 1]
        x, y, c, _chips = _place()
        sends, recvs = waits(ins, lands)
        for src, k in sends:
            pltpu.make_async_remote_copy(src_ref=src, dst_ref=src, send_sem=send_sem.at[k], recv_sem=recv_sem.at[0],
                                         device_id=(x, y, 1 - c), device_id_type=MESH).wait_send()
        for dst, k in recvs:
            pltpu.make_async_remote_copy(src_ref=dst, dst_ref=dst, send_sem=send_sem.at[0], recv_sem=recv_sem.at[k],
                                         device_id=(x, y, 1 - c), device_id_type=MESH).wait_recv()

    out = pl.pallas_call(
        body, name=name, out_shape=tuple(pltpu.HBM(a.shape, a.dtype) for a in bufs),
        in_specs=[HBM] * nb + [SEM, SEM, ANY], out_specs=tuple([HBM] * nb),
        input_output_aliases={k: k for k in range(nb)},
        compiler_params=pltpu.CompilerParams(has_side_effects=EFFECT),
    )(*bufs, send, recv, after)
    return list(out[:n_src]), list(out[n_src:])


def gather_plan(n):
    def plan(ins, lands):
        x, y, c, chips = _place()
        me = 2 * x + y
        copies = []
        for w in range(n):
            for j, (cx, cy) in enumerate(chips):
                for t in range(2):
                    copies.append((ins[w].at[c], lands[w].at[c, me], 8 * w + 2 * j + t, 8 * w + 2 * j + c, (cx, cy, t)))
            for half in range(2):
                copies.append((ins[w].at[half], lands[w].at[half, me], 8 * w + 6 + half, 8 * w + 6 + half, (x, y, 1 - c)))
        return copies

    def waits(ins, lands):
        x, y, c, chips = _place()
        me = 2 * x + y
        sends, recvs = [], []
        for w in range(n):
            for j, (cx, cy) in enumerate(chips):
                for t in range(2):
                    sends.append((ins[w].at[c], 8 * w + 2 * j + t))
                    recvs.append((lands[w].at[t, 2 * cx + cy], 8 * w + 2 * j + t))
            for half in range(2):
                sends.append((ins[w].at[half], 8 * w + 6 + half))
                recvs.append((lands[w].at[half, me], 8 * w + 6 + half))
        return sends, recvs

    return plan, waits


def scatter_plan(n):
    def plan(ins, lands):
        x, y, c, chips = _place()
        return [(ins[w].at[:, 2 * cx + cy], lands[w].at[j], 3 * w + j, 3 * w + j, (cx, cy, c))
                for w in range(n) for j, (cx, cy) in enumerate(chips)]

    def waits(ins, lands):
        x, y, c, chips = _place()
        sends = [(ins[w].at[:, 2 * cx + cy], 3 * w + j) for w in range(n) for j, (cx, cy) in enumerate(chips)]
        recvs = [(lands[w].at[j], 3 * w + j) for w in range(n) for j in range(3)]
        return sends, recvs

    return plan, waits


def _layer_shards(w, l):
    return [w["ffn_gate"][l].astype(MMT), w["ffn_up"][l].astype(MMT), w["ffn_down"][l].astype(MMT),
            w["w_in"][l].reshape(2, D_MODEL // 2, -1).astype(MMT),
            w["branch_proj"][l].reshape(2, 3 * BW // 2, -1).astype(MMT),
            w["w_out"][l].reshape(2, -1, D_MODEL).astype(MMT),
            w["s5_glu_w"][l].reshape(2, -1, BW).astype(MMT)]


def _layer_weights(g):
    rows = lambda a: a.transpose(1, 0, 2, 3).reshape(NSH, -1, a.shape[-1])
    p = rows(g[4]).reshape(NSH, 3, BW, -1).transpose(1, 2, 0, 3).reshape(3, BW, D_MODEL)
    return dict(wg=g[0], wu=g[1], wd=g[2], win=rows(g[3]), pfull=p,
                woutfull=rows(g[5]).reshape(D_MODEL, D_MODEL), gluw=rows(g[6]).reshape(BW, BW))


GROUPS = {"ffn1": ("ffn_gate", "ffn_up", "ffn_down"), "merge": ("branch_proj", "w_out"), "mid": ("s5_glu_w",),
          "pre": ("w_in",), "ffn0": ("ffn_gate", "ffn_up", "ffn_down")}


def _grad_views(big, l, group):
    views = []
    for name in GROUPS[group]:
        if name == "branch_proj":
            dq = D_MODEL // NSH
            a = big[(name, l)].reshape(3, BW, NSH, dq).transpose(2, 0, 1, 3).reshape(1, NSH, 3 * BW, dq)
        elif name.startswith("ffn"):
            a = big[(name, l, 1 if group == "ffn1" else 0)]
        else:
            a = big[(name, l)]
            a = a.reshape(1, NSH, -1, a.shape[-1])
        views.append((name, a, 0))
    return views


def halves_plan(n):
    def src(ref, c):
        h = ref.shape[2] // 2
        return ref.at[:, :, pl.ds((1 - c) * h, h)]

    def plan(ins, lands):
        x, y, c, _chips = _place()
        return [(src(ins[w], c), lands[w], w, w, (x, y, 1 - c)) for w in range(n)]

    def waits(ins, lands):
        x, y, c, _chips = _place()
        return [(src(ins[w], c), w) for w in range(n)], [(lands[w], w) for w in range(n)]

    return plan, waits


def _reduce_to_halves(tag, views, c, wire):
    from_sibling = exchange_halves(f"reduce_cores_{tag}", [a for _, a, _ in views], [(p0, 1) for _, _, p0 in views])
    merge = lambda a: a.reshape((-1,) + a.shape[2:])
    return [add_own_half(f"sum_cores_{tag}_{i}", merge(a), merge(r), c, wire[i], NSH * p0).reshape(r.shape)
            for i, ((_, a, p0), r) in enumerate(zip(views, from_sibling))]


def _step(x, target, w, m, v):
    mx, my, mc = lax.axis_index("x"), lax.axis_index("y"), lax.axis_index("c")
    me = (2 * mx + my).astype(jnp.int32)
    mc = mc.astype(jnp.int32)

    W = dict(L=[None] * DEPTH)
    state = {"pending": []}
    n_big = len(BIG)
    g_plan, g_waits = gather_plan(n_big)

    def layer_weights(l, h):
        if l == 0:
            got = gather_shards("gather_weights_0", _layer_shards(w, 0) + [w[n] for n in SHARDED_SMALL])
            nxt = _layer_shards(w, 1)
            got, nxt = lax.optimization_barrier((got, nxt))
            shapes = [jax.ShapeDtypeStruct((2, NSH) + a.shape[1:], a.dtype) for a in nxt]
            state["gather"], token = split_start("gather_weights_1_start", nxt, shapes, g_plan, 8 * n_big, 8 * n_big)
            W["nw"] = got[n_big].transpose(0, 2, 1, 3).reshape(DEPTH, 3, 1, D_MODEL) + token[0, 0]
            W["convw"] = got[n_big + 1].transpose(0, 2, 1, 3).reshape(DEPTH, CONV_W, BW)
            return _layer_weights(got[:n_big]), h
        return _layer_weights(split_wait("gather_weights_1_wait", state["gather"], n_big, g_waits, h)[1]), h

    def to_chips(after):
        if "cores" not in state:
            return
        tag, names, l, group, handles, waits = state.pop("cores")
        sent, landed = split_wait(f"reduce_cores_{tag}_wait", handles, len(names), waits, after)
        merge = lambda a: a.reshape((-1,) + a.shape[2:])
        halves = [add_own_half(f"sum_cores_{tag}_{i}", merge(a), merge(r), mc, jnp.bfloat16, 0).reshape(r.shape)
                  for i, (a, r) in enumerate(zip(sent, landed))]
        shapes = [jax.ShapeDtypeStruct((3, a.shape[0]) + a.shape[2:], a.dtype) for a in halves]
        plan, waits = scatter_plan(len(halves))
        handles, token = split_start(f"reduce_chips_{tag}_start", halves, shapes, plan, 3 * len(halves), 3 * len(halves))
        W["nw"] = W["nw"] + token[0, 0]
        state["pending"].append((tag, names, l, group, handles, waits))

    def layer_grads(l, group, big):
        views = _grad_views(big, l, group)
        to_chips(views[0][1])
        if (l, group) == (0, "ffn0"):
            return
        tag = f"{l}_{group}"
        arrays = [a for _, a, _ in views]
        shapes = [jax.ShapeDtypeStruct((1, NSH, a.shape[2] // 2, a.shape[3]), a.dtype) for a in arrays]
        plan, waits = halves_plan(len(arrays))
        handles, token = split_start(f"reduce_cores_{tag}_start", arrays, shapes, plan, len(arrays), len(arrays))
        W["nw"] = W["nw"] + token[0, 0]
        state["cores"] = (tag, [name for name, _, _ in views], l, group, handles, waits)

    loss, dx, big, small = local_step(x[0], target[0], W, {k: w[k] for k in SMALL_RAW}, layer_weights, layer_grads)

    pieces = {n: {} for n in BIG}
    block_of = lambda name, l, group: (2 * l + (group == "ffn1")) if name.startswith("ffn") else l
    views = _grad_views(big, 0, "ffn0")
    small_packed = pack_small("pack_small_grads", [small[n] for n in SMALL], NSH * 32)
    halves = _reduce_to_halves("0_ffn0", views + [("small", small_packed.reshape(1, NSH, -1, LANE), 0)], mc,
                               [jnp.bfloat16] * len(views) + [F32])
    shapes = [jax.ShapeDtypeStruct((3, a.shape[0]) + a.shape[2:], a.dtype) for a in halves]
    plan, waits = scatter_plan(len(halves))
    last_handles, token = split_start("reduce_chips_0_ffn0_start", halves, shapes, plan, 3 * len(halves), 3 * len(halves))
    mc = mc + token[0, 0].astype(jnp.int32)
    for tag, names, l, group, handles, waits_k in state["pending"]:
        sent, landed = split_wait(f"reduce_chips_{tag}_wait", handles, len(names), waits_k, dx)
        for i, (name, h, r) in enumerate(zip(names, sent, landed)):
            pieces[name][block_of(name, l, group)] = add_chips(f"sum_chips_{tag}_{i}", h, r, me)

    g, delta, new_m, new_v = {}, {}, {}, {}

    def update(tag, names, extra):
        own = [jnp.concatenate([pieces[n][b] for b in sorted(pieces[n])], axis=0) for n in names] + extra
        other = share_halves(f"reduce_share_{tag}", own)
        for i, n in enumerate(names):
            view = lambda a: a.reshape(own[i].shape[0], -1, own[i].shape[2])
            res = adamw_halves(f"adamw_{n}", view(w[n]), view(m[n]), view(v[n]), own[i], other[i], mc)
            g[n], delta[n], new_m[n], new_v[n] = [a.reshape(w[n].shape) for a in res]
        return own, other

    early = [n for n in BIG if not n.startswith("ffn")]
    update("early", early, [])
    sent, landed = split_wait("reduce_chips_0_ffn0_wait", last_handles, len(halves), waits, new_v[early[0]])
    last = [add_chips(f"sum_chips_0_ffn0_{i}", h, r, me) for i, (h, r) in enumerate(zip(sent, landed))]
    for (name, _, _), piece in zip(views, last):
        pieces[name][block_of(name, 0, "ffn0")] = piece
    own, other = update("last", [n for n in BIG if n.startswith("ffn")], [last[-1]])

    piece = jnp.stack([jnp.where(mc == 0, own[-1][0], other[-1][0]), jnp.where(mc == 0, other[-1][0], own[-1][0])])
    (all_small,) = gather_shards("gather_small", [piece])
    full_small = unpack_small("unpack_small_grads", all_small.transpose(1, 0, 2, 3).reshape(-1, LANE),
                              [small[n].shape for n in SMALL])
    g.update(zip(SMALL, full_small))
    g["norm_w"] = lax.dynamic_slice_in_dim(g["norm_w"], me * (D_MODEL // NSH), D_MODEL // NSH, axis=2)
    g["rg_conv_w"] = lax.dynamic_slice_in_dim(g["rg_conv_w"], me * (BW // NSH), BW // NSH, axis=2)

    packed = [pack_small(f"pack_small_{tag}", [src[n] for n in SMALL], 8)
              for tag, src in (("w", w), ("g", g), ("m", m), ("v", v))]
    for tag, dst, flat in zip(("delta", "m", "v"), (delta, new_m, new_v), adamw(*packed)):
        dst.update(zip(SMALL, unpack_small(f"unpack_small_{tag}", flat, [w[n].shape for n in SMALL])))

    total = lax.psum(loss[0, 0], ("x", "y", "c"))
    return (total, dx[None], *[g[n] for n in WEIGHTS], *[delta[n] for n in WEIGHTS],
            *[new_m[n] for n in WEIGHTS], *[new_v[n] for n in WEIGHTS])


def kernel(x, norm_w, final_norm_w, ffn_gate, ffn_up, ffn_down, w_in, branch_proj, w_out, s5_lambda_re, s5_lambda_im, s5_log_dt, s5_b_re, s5_b_im, s5_c_re, s5_c_im, s5_d, s5_glu_w, s5_glu_b, hg_lb_logits, hg_norm_w, rg_conv_w, rg_conv_b, rg_wa, rg_ba, rg_wx, rg_bx, rg_lambda, loss_target, m_norm_w, m_final_norm_w, m_ffn_gate, m_ffn_up, m_ffn_down, m_w_in, m_branch_proj, m_w_out, m_s5_lambda_re, m_s5_lambda_im, m_s5_log_dt, m_s5_b_re, m_s5_b_im, m_s5_c_re, m_s5_c_im, m_s5_d, m_s5_glu_w, m_s5_glu_b, m_hg_lb_logits, m_hg_norm_w, m_rg_conv_w, m_rg_conv_b, m_rg_wa, m_rg_ba, m_rg_wx, m_rg_bx, m_rg_lambda, v_norm_w, v_final_norm_w, v_ffn_gate, v_ffn_up, v_ffn_down, v_w_in, v_branch_proj, v_w_out, v_s5_lambda_re, v_s5_lambda_im, v_s5_log_dt, v_s5_b_re, v_s5_b_im, v_s5_c_re, v_s5_c_im, v_s5_d, v_s5_glu_w, v_s5_glu_b, v_hg_lb_logits, v_hg_norm_w, v_rg_conv_w, v_rg_conv_b, v_rg_wa, v_rg_ba, v_rg_wx, v_rg_bx, v_rg_lambda):
    ws = (norm_w, final_norm_w, ffn_gate, ffn_up, ffn_down, w_in, branch_proj, w_out, s5_lambda_re, s5_lambda_im, s5_log_dt, s5_b_re, s5_b_im, s5_c_re, s5_c_im, s5_d, s5_glu_w, s5_glu_b, hg_lb_logits, hg_norm_w, rg_conv_w, rg_conv_b, rg_wa, rg_ba, rg_wx, rg_bx, rg_lambda)
    ms = (m_norm_w, m_final_norm_w, m_ffn_gate, m_ffn_up, m_ffn_down, m_w_in, m_branch_proj, m_w_out, m_s5_lambda_re, m_s5_lambda_im, m_s5_log_dt, m_s5_b_re, m_s5_b_im, m_s5_c_re, m_s5_c_im, m_s5_d, m_s5_glu_w, m_s5_glu_b, m_hg_lb_logits, m_hg_norm_w, m_rg_conv_w, m_rg_conv_b, m_rg_wa, m_rg_ba, m_rg_wx, m_rg_bx, m_rg_lambda)
    vs = (v_norm_w, v_final_norm_w, v_ffn_gate, v_ffn_up, v_ffn_down, v_w_in, v_branch_proj, v_w_out, v_s5_lambda_re, v_s5_lambda_im, v_s5_log_dt, v_s5_b_re, v_s5_b_im, v_s5_c_re, v_s5_c_im, v_s5_d, v_s5_glu_w, v_s5_glu_b, v_hg_lb_logits, v_hg_norm_w, v_rg_conv_w, v_rg_conv_b, v_rg_wa, v_rg_ba, v_rg_wx, v_rg_bx, v_rg_lambda)
    return _step(x, loss_target, dict(zip(WEIGHTS, ws)), dict(zip(WEIGHTS, ms)), dict(zip(WEIGHTS, vs)))
```

```python
import functools
import math
from typing import NamedTuple

import jax
import jax.numpy as jnp
from jax import lax
from jax.experimental import pallas as pl
from jax.experimental.pallas import tpu as pltpu

F32 = jnp.float32
MMT = jnp.bfloat16
HI = lax.Precision.HIGHEST

D_MODEL = 1024
BW = 512
S5_GROUP, S5_GROUPS, S5_STATE = 16, 32, 64
S5_N = S5_GROUPS * S5_STATE
HG_HEADS, HG_D = 4, 128
HG_CHUNK = 128
RG_BLOCKS, RG_BLOCK = 8, 64
RG_C = 8.0
CONV_W = 4
D_FF = 2816
EPS = 1e-6
IN_TOTAL = 6656
NSH = 4
NSEG = 8
LANE = 128
VMEM_LIMIT = 56 * 1024 * 1024
TM_FWD = 512
TM_WGRAD = 512

ADAM_LR, ADAM_B1, ADAM_B2, ADAM_EPS, ADAM_WD, ADAM_STEP = 0.001, 0.9, 0.999, 1e-08, 0.01, 10

MESH = pl.DeviceIdType.MESH


class WP(NamedTuple):
    w: jax.Array
    p: jax.Array


def _dg(a, b, ca, cb):
    return lax.dot_general(a, b, (((ca,), (cb,)), ((), ())), preferred_element_type=F32)


@jax.custom_vjp
def _mmw(a, w, p):
    return _dg(a.astype(MMT), w, 1, 0)


def _mmw_fwd(a, w, p):
    return _mmw(a, w, p), (a, w)


def _mmw_bwd(res, g):
    a, w = res
    gb = g.astype(MMT)
    return _dg(gb, w, 1, 1), jnp.zeros_like(w), _dg(a.astype(MMT), gb, 0, 0)


_mmw.defvjp(_mmw_fwd, _mmw_bwd)


def mm(a, w):
    if isinstance(w, WP):
        return _mmw(a, w.w, w.p)
    return _dg(a.astype(MMT), w, 1, 0)


@jax.custom_vjp
def mma_nn(a, b):
    return _dg(a.astype(MMT), b.astype(MMT), 1, 0)


def _nn_f(a, b):
    return mma_nn(a, b), (a, b)


def _nn_b(res, g):
    a, b = res
    gb = g.astype(MMT)
    return _dg(gb, b.astype(MMT), 1, 1), _dg(a.astype(MMT), gb, 0, 0)


mma_nn.defvjp(_nn_f, _nn_b)


@jax.custom_vjp
def mma_nt(a, b):
    return _dg(a.astype(MMT), b.astype(MMT), 1, 1)


def _nt_f(a, b):
    return mma_nt(a, b), (a, b)


def _nt_b(res, g):
    a, b = res
    gb = g.astype(MMT)
    return _dg(gb, b.astype(MMT), 1, 0), _dg(gb, a.astype(MMT), 0, 0)


mma_nt.defvjp(_nt_f, _nt_b)


@jax.custom_vjp
def mma_tn(a, b):
    return _dg(a.astype(MMT), b.astype(MMT), 0, 0)


def _tn_f(a, b):
    return mma_tn(a, b), (a, b)


def _tn_b(res, g):
    a, b = res
    gb = g.astype(MMT)
    return _dg(b.astype(MMT), gb, 1, 1), _dg(a.astype(MMT), gb, 1, 0)


mma_tn.defvjp(_tn_f, _tn_b)


def mm_exact(m, x):
    return jnp.dot(m, x, precision=HI, preferred_element_type=F32)


def _rms(x, w):
    return x * lax.rsqrt(jnp.mean(x * x, axis=-1, keepdims=True) + EPS) * w


def _expm1(x):
    series = x * (1.0 + x * (1.0 / 2) * (1.0 + x * (1.0 / 3) * (1.0 + x * (1.0 / 4) * (1.0 + x * (1.0 / 5) * (1.0 + x * (1.0 / 6))))))
    return jnp.where(jnp.abs(x) < 0.1, series, jnp.exp(x) - 1.0)


def _bspec(block, fn, order):
    if order == "is":
        return pl.BlockSpec(block, lambda i, s: fn(s, i))
    return pl.BlockSpec(block, lambda s, i: fn(s, i))


def tile_fwd(fn, name, n_i, n_s, ins, outs, s_outer=False):
    n_in = len(ins)
    order = "si" if s_outer else "is"
    assert not (s_outer and any(o[4] for o in outs))

    def body(*refs):
        s = pl.program_id(0 if s_outer else 1)
        res = fn(*[r[...] for r in refs[:n_in]], s)
        for o_ref, val, spec in zip(refs[n_in:], res, outs):
            if spec[4] and n_s > 1:
                @pl.when(s == 0)
                def _(o_ref=o_ref, val=val):
                    o_ref[...] = val.astype(o_ref.dtype)

                @pl.when(s != 0)
                def _(o_ref=o_ref, val=val):
                    o_ref[...] += val.astype(o_ref.dtype)
            else:
                o_ref[...] = val.astype(o_ref.dtype)

    return pl.pallas_call(
        body, grid=(n_s, n_i) if s_outer else (n_i, n_s), name=name,
        in_specs=[_bspec(b, f, order) for _, b, f in ins],
        out_specs=[_bspec(b, f, order) for _, _, b, f, _ in outs],
        out_shape=[jax.ShapeDtypeStruct(sh, dt) for sh, dt, _, _, _ in outs],
        compiler_params=pltpu.CompilerParams(vmem_limit_bytes=VMEM_LIMIT,
                                             dimension_semantics=("arbitrary", "arbitrary")),
    )(*[a for a, _, _ in ins])


def tile_bwd(fn, name, n_i, n_s, ins, cts, gouts):
    groups = [c if isinstance(c, list) else [c] for c in cts]
    cts = [blk for grp in groups for blk in grp]
    n_in, n_ct = len(ins), len(cts)
    kinds = [k for _, _, _, k in ins]
    d_pos = [j for j, k in enumerate(kinds) if k != "c"]
    shared = [(gi, spec[4]) for gi, spec in enumerate(gouts) if len(spec) == 5 and spec[4] is not None]
    n_sh = len(shared)

    def body(*refs):
        s, i = pl.program_id(0), pl.program_id(1)
        vals = [r[...] for r in refs[:n_in]]
        ct_refs, ctv = list(refs[n_in:n_in + n_ct]), []
        for grp in groups:
            parts = [ct_refs.pop(0)[...] for _ in grp]
            ctv.append(parts[0] if len(parts) == 1 else jnp.concatenate(parts, axis=1))
        ctv = tuple(ctv)
        g_refs = refs[n_in + n_ct + n_sh:]

        def g(*dv):
            args = list(vals)
            for j, v in zip(d_pos, dv):
                args[j] = WP(vals[j], v) if kinds[j] == "w" else v
            return tuple(fn(*args))

        dv0 = [jnp.zeros(vals[j].shape, F32) if kinds[j] == "w" else vals[j] for j in d_pos]
        _, vjp = jax.vjp(g, *dv0)
        grads = vjp(ctv)
        for g_ref, gv, spec in zip(g_refs, grads, gouts):
            mode = spec[3]
            if mode == "write":
                g_ref[...] = gv.astype(g_ref.dtype)
            else:
                first = (i == 0) if mode == "acc_i" else jnp.logical_and(i == 0, s == 0)

                @pl.when(first)
                def _(g_ref=g_ref, gv=gv):
                    g_ref[...] = gv.astype(g_ref.dtype)

                @pl.when(jnp.logical_not(first))
                def _(g_ref=g_ref, gv=gv):
                    g_ref[...] += gv.astype(g_ref.dtype)

    return pl.pallas_call(
        body, grid=(n_s, n_i), name=name,
        in_specs=([_bspec(b, f, "si") for _, b, f, _ in ins] + [_bspec(b, f, "si") for _, b, f in cts]
                  + [pl.BlockSpec(memory_space=pl.ANY)] * n_sh),
        out_specs=[_bspec(spec[1], spec[2], "si") for spec in gouts],
        out_shape=[jax.ShapeDtypeStruct(spec[0], F32) for spec in gouts],
        input_output_aliases={n_in + n_ct + k: gi for k, (gi, _) in enumerate(shared)},
        compiler_params=pltpu.CompilerParams(vmem_limit_bytes=VMEM_LIMIT,
                                             dimension_semantics=("arbitrary", "arbitrary")),
    )(*[a for a, _, _, _ in ins], *[a for a, _, _ in cts], *[buf for _, buf in shared])


def _row_tile(rows, width, itemsize=4, budget=2 * 1024 * 1024, mult=8):
    best = mult
    for t in range(mult, rows + 1, mult):
        if rows % t == 0 and t * width * itemsize <= budget:
            best = t
    return best


def add_n(name, terms, shape):
    rows, cols = shape
    tr = _row_tile(rows, cols)

    def body(*refs):
        acc = refs[0][...]
        for r in refs[1:-1]:
            acc = acc + r[...]
        refs[-1][...] = acc

    specs = []
    for _, lead in terms:
        specs.append(pl.BlockSpec((None,) * len(lead) + (tr, cols), functools.partial(lambda i, lead: (*lead, i, 0), lead=lead)))
    return pl.pallas_call(
        body, grid=(rows // tr,), name=name, in_specs=specs,
        out_specs=pl.BlockSpec((tr, cols), lambda i: (i, 0)),
        out_shape=jax.ShapeDtypeStruct((rows, cols), F32),
    )(*[a for a, _ in terms])


def ffn_core(x, nw, wg, wu, wd):
    h = _rms(x, nw)
    return (0.5 * mm(jax.nn.silu(mm(h, wg)) * mm(h, wu), wd),)


def pre_core(x, nw, win):
    return (mm(_rms(x, nw), win),)


def _split_lanes(y):
    return jnp.stack([y[:, k * LANE:(k + 1) * LANE] for k in range(y.shape[1] // LANE)], axis=0)


def _join_lanes(y3):
    return jnp.concatenate([y3[k] for k in range(y3.shape[0])], axis=1)


def s5_pre_core(u, b_re0, b_re1, b_im0, b_im1):
    u0, u1 = u[:, :BW // 2], u[:, BW // 2:]
    re = jnp.concatenate([mm(u0, b_re0), mm(u1, b_re1)], axis=1)
    im = jnp.concatenate([mm(u0, b_im0), mm(u1, b_im1)], axis=1)
    return _split_lanes(re), _split_lanes(im)


def mid_core(xr, xi, u, o, g, hs, gc, hmat, c0, c1, d, gluw, glub, hgw):
    half = xr.shape[0] // 2
    xs0 = jnp.concatenate([_join_lanes(xr[:half]), _join_lanes(xi[:half])], axis=1)
    xs1 = jnp.concatenate([_join_lanes(xr[half:]), _join_lanes(xi[half:])], axis=1)
    y = jnp.concatenate([mm(xs0, c0), mm(xs1, c1)], axis=1) + d * u
    z = jax.nn.gelu(y)
    ya = z * jax.nn.sigmoid(mm(z, gluw) + glub)
    ms = mm_exact(o * o, hmat)
    yb = o * lax.rsqrt(ms + EPS) * hgw * jax.nn.silu(g)
    yc = hs * jax.nn.gelu(gc)
    return ya, yb, yc


def _sub(w, n):
    return WP(w.w[n], w.p[n]) if isinstance(w, WP) else w[n]


def merge_core(ya, yb, yc, g0, g1, g2, g3, g4, g5, p, wout):
    gate = lambda a, b: jax.nn.sigmoid(jnp.concatenate([a, b], axis=1))
    m = gate(g0, g1) * mm(ya, _sub(p, 0)) + gate(g2, g3) * mm(yb, _sub(p, 1)) + gate(g4, g5) * mm(yc, _sub(p, 2))
    return (mm(m, wout),)


def gates_core(xc, wa, ba, wx, bx, lam):
    r = jax.nn.sigmoid(mm(xc, wa) + ba)
    i = jax.nn.sigmoid(mm(xc, wx) + bx)
    log_a = -RG_C * jax.nn.softplus(-lam) * r
    a = jnp.exp(log_a)
    b = jnp.sqrt(-_expm1(2.0 * log_a)) * (i * xc)
    return a, b


def _seg_rows(ref, k, j, n):
    rows = pl.ds(pl.multiple_of(j * NSEG, NSEG), NSEG)
    if k is None:
        return ref[rows, :]
    return ref[k, rows, :]


def _seg_store(ref, k, j, n, val):
    rows = pl.ds(pl.multiple_of(j * NSEG, NSEG), NSEG)
    if k is None:
        ref[rows, :] = val
    else:
        ref[k, rows, :] = val


def _seg_carries(er, ei, pr, pi, reverse):
    rows = lax.broadcasted_iota(jnp.int32, er.shape, 0)
    cr = jnp.zeros_like(er)
    ci = None if ei is None else jnp.zeros_like(er)
    order = range(NSEG - 2, -1, -1) if reverse else range(1, NSEG)
    shift = NSEG - 1 if reverse else 1
    for s in order:
        if ei is None:
            tr = er + pr * cr
            cr = jnp.where(rows == s, pltpu.roll(tr, shift, 0), cr)
        else:
            tr = er + pr * cr - pi * ci
            ti = ei + pr * ci + pi * cr
            cr = jnp.where(rows == s, pltpu.roll(tr, shift, 0), cr)
            ci = jnp.where(rows == s, pltpu.roll(ti, shift, 0), ci)
    return cr, ci


S5_K = 2


def s5_scan_fwd(bur, bui, ar, ai, L):
    n = L // NSEG
    nb = S5_N // LANE
    K = S5_K

    def body(br_ref, bi_ref, ar_ref, ai_ref, xr_ref, xi_ref):
        zero = jnp.zeros((NSEG, LANE), F32)
        A = [(jnp.broadcast_to(ar_ref[k], (NSEG, LANE)), jnp.broadcast_to(ai_ref[k], (NSEG, LANE))) for k in range(K)]

        def p1(j, st):
            new = []
            for k in range(K):
                sr, si, pr, pi = st[k]
                a_r, a_i = A[k]
                nr = a_r * sr - a_i * si + _seg_rows(br_ref, k, j, n)
                ni = a_r * si + a_i * sr + _seg_rows(bi_ref, k, j, n)
                _seg_store(xr_ref, k, j, n, nr)
                _seg_store(xi_ref, k, j, n, ni)
                new.append((nr, ni, a_r * pr - a_i * pi, a_r * pi + a_i * pr))
            return tuple(new)

        st = lax.fori_loop(0, n, p1, tuple((zero, zero, zero + 1.0, zero) for _ in range(K)))
        C = [_seg_carries(st[k][0], st[k][1], st[k][2], st[k][3], False) for k in range(K)]

        def p2(j, st):
            new = []
            for k in range(K):
                pr, pi = st[k]
                a_r, a_i = A[k]
                pr, pi = a_r * pr - a_i * pi, a_r * pi + a_i * pr
                cr, ci = C[k]
                _seg_store(xr_ref, k, j, n, _seg_rows(xr_ref, k, j, n) + pr * cr - pi * ci)
                _seg_store(xi_ref, k, j, n, _seg_rows(xi_ref, k, j, n) + pr * ci + pi * cr)
                new.append((pr, pi))
            return tuple(new)

        lax.fori_loop(0, n, p2, tuple((zero + 1.0, zero) for _ in range(K)))

    blk = pl.BlockSpec((K, L, LANE), lambda g: (g, 0, 0))
    ablk = pl.BlockSpec((K, 1, LANE), lambda g: (g, 0, 0))
    return pl.pallas_call(
        body, grid=(nb // K,), name="s5_scan_fwd",
        in_specs=[blk, blk, ablk, ablk], out_specs=[blk, blk],
        out_shape=[jax.ShapeDtypeStruct((nb, L, LANE), F32)] * 2,
        compiler_params=pltpu.CompilerParams(vmem_limit_bytes=VMEM_LIMIT),
    )(bur, bui, ar, ai)


def s5_scan_bwd(dxr, dxi, xr, xi, ar, ai, L):
    n = L // NSEG
    nb = S5_N // LANE
    K = S5_K

    def body(dr_ref, di_ref, xr_ref, xi_ref, ar_ref, ai_ref, gr_ref, gi_ref, dar_ref, dai_ref):
        zero = jnp.zeros((NSEG, LANE), F32)
        rows = lax.broadcasted_iota(jnp.int32, (NSEG, LANE), 0)
        A = [(jnp.broadcast_to(ar_ref[k], (NSEG, LANE)), -jnp.broadcast_to(ai_ref[k], (NSEG, LANE))) for k in range(K)]

        def p1(jj, st):
            j = n - 1 - jj
            new = []
            for k in range(K):
                sr, si, pr, pi = st[k]
                a_r, a_i = A[k]
                nr = a_r * sr - a_i * si + _seg_rows(dr_ref, k, j, n)
                ni = a_r * si + a_i * sr + _seg_rows(di_ref, k, j, n)
                _seg_store(gr_ref, k, j, n, nr)
                _seg_store(gi_ref, k, j, n, ni)
                new.append((nr, ni, a_r * pr - a_i * pi, a_r * pi + a_i * pr))
            return tuple(new)

        st = lax.fori_loop(0, n, p1, tuple((zero, zero, zero + 1.0, zero) for _ in range(K)))
        C = [_seg_carries(st[k][0], st[k][1], st[k][2], st[k][3], True) for k in range(K)]
        xb = [(jnp.where(rows == 0, 0.0, pltpu.roll(_seg_rows(xr_ref, k, n - 1, n), 1, 0)),
               jnp.where(rows == 0, 0.0, pltpu.roll(_seg_rows(xi_ref, k, n - 1, n), 1, 0))) for k in range(K)]

        def p2(jj, st):
            j = n - 1 - jj
            jp = jnp.maximum(j - 1, 0)
            new = []
            for k in range(K):
                pr, pi, acr, aci = st[k]
                a_r, a_i = A[k]
                pr, pi = a_r * pr - a_i * pi, a_r * pi + a_i * pr
                cr, ci = C[k]
                g_r = _seg_rows(gr_ref, k, j, n) + pr * cr - pi * ci
                g_i = _seg_rows(gi_ref, k, j, n) + pr * ci + pi * cr
                _seg_store(gr_ref, k, j, n, g_r)
                _seg_store(gi_ref, k, j, n, g_i)
                xpr = jnp.where(j == 0, xb[k][0], _seg_rows(xr_ref, k, jp, n))
                xpi = jnp.where(j == 0, xb[k][1], _seg_rows(xi_ref, k, jp, n))
                new.append((pr, pi, acr + g_r * xpr + g_i * xpi, aci + g_i * xpr - g_r * xpi))
            return tuple(new)

        st = lax.fori_loop(0, n, p2, tuple((zero + 1.0, zero, zero, zero) for _ in range(K)))
        for k in range(K):
            dar_ref[k] = jnp.sum(st[k][2], axis=0, keepdims=True)
            dai_ref[k] = jnp.sum(st[k][3], axis=0, keepdims=True)

    blk = pl.BlockSpec((K, L, LANE), lambda g: (g, 0, 0))
    ablk = pl.BlockSpec((K, 1, LANE), lambda g: (g, 0, 0))
    return pl.pallas_call(
        body, grid=(nb // K,), name="s5_scan_bwd",
        in_specs=[blk, blk, blk, blk, ablk, ablk], out_specs=[blk, blk, ablk, ablk],
        out_shape=[jax.ShapeDtypeStruct((nb, L, LANE), F32)] * 2 + [jax.ShapeDtypeStruct((nb, 1, LANE), F32)] * 2,
        compiler_params=pltpu.CompilerParams(vmem_limit_bytes=VMEM_LIMIT),
    )(dxr, dxi, xr, xi, ar, ai)


def rg_scan_fwd(a, b, L):
    n = L // NSEG

    def body(a_ref, b_ref, h_ref):
        zero = jnp.zeros((NSEG, LANE), F32)

        def p1(j, st):
            h, p = st
            aj = _seg_rows(a_ref, None, j, n)
            h = aj * h + _seg_rows(b_ref, None, j, n)
            _seg_store(h_ref, None, j, n, h)
            return h, aj * p

        e, pe = lax.fori_loop(0, n, p1, (zero, zero + 1.0))
        c, _ = _seg_carries(e, None, pe, None, False)

        def p2(j, p):
            p = _seg_rows(a_ref, None, j, n) * p
            _seg_store(h_ref, None, j, n, _seg_rows(h_ref, None, j, n) + p * c)
            return p

        lax.fori_loop(0, n, p2, zero + 1.0)

    blk = pl.BlockSpec((L, LANE), lambda g: (0, g))
    return pl.pallas_call(
        body, grid=(BW // LANE,), name="rg_scan_fwd", in_specs=[blk, blk], out_specs=blk,
        out_shape=jax.ShapeDtypeStruct((L, BW), F32),
        compiler_params=pltpu.CompilerParams(vmem_limit_bytes=VMEM_LIMIT),
    )(a, b)


def rg_scan_bwd(a, h, dh, L):
    n = L // NSEG

    def body(a_ref, h_ref, dh_ref, da_ref, db_ref):
        zero = jnp.zeros((NSEG, LANE), F32)
        rows = lax.broadcasted_iota(jnp.int32, (NSEG, LANE), 0)
        a_edge = jnp.where(rows == NSEG - 1, 0.0, pltpu.roll(_seg_rows(a_ref, None, 0, n), NSEG - 1, 0))
        h_edge = jnp.where(rows == 0, 0.0, pltpu.roll(_seg_rows(h_ref, None, n - 1, n), 1, 0))

        def mult(j):
            return jnp.where(j == n - 1, a_edge, _seg_rows(a_ref, None, jnp.minimum(j + 1, n - 1), n))

        def p1(jj, st):
            j = n - 1 - jj
            g, p = st
            m = mult(j)
            g = m * g + _seg_rows(dh_ref, None, j, n)
            _seg_store(db_ref, None, j, n, g)
            return g, m * p

        e, pe = lax.fori_loop(0, n, p1, (zero, zero + 1.0))
        c, _ = _seg_carries(e, None, pe, None, True)

        def p2(jj, p):
            j = n - 1 - jj
            p = mult(j) * p
            g = _seg_rows(db_ref, None, j, n) + p * c
            _seg_store(db_ref, None, j, n, g)
            hp = jnp.where(j == 0, h_edge, _seg_rows(h_ref, None, jnp.maximum(j - 1, 0), n))
            _seg_store(da_ref, None, j, n, g * hp)
            return p

        lax.fori_loop(0, n, p2, zero + 1.0)

    blk = pl.BlockSpec((L, LANE), lambda g: (0, g))
    return pl.pallas_call(
        body, grid=(BW // LANE,), name="rg_scan_bwd", in_specs=[blk, blk, blk], out_specs=[blk, blk],
        out_shape=[jax.ShapeDtypeStruct((L, BW), F32)] * 2,
        compiler_params=pltpu.CompilerParams(vmem_limit_bytes=VMEM_LIMIT),
    )(a, h, dh)


def _hg_consts(C):
    t = lax.broadcasted_iota(jnp.int32, (C, C), 0)
    s = lax.broadcasted_iota(jnp.int32, (C, C), 1)
    tril = (s <= t).astype(F32)
    diag = (s == t).astype(F32)
    levels = []
    k = 1
    while (1 << k) <= C:
        m = 1 << (k - 1)
        same = (t >> k) == (s >> k)
        t_right = ((t >> (k - 1)) & 1) == 1
        s_left = ((s >> (k - 1)) & 1) == 0
        mask = jnp.logical_and(same, jnp.logical_and(t_right, s_left)).astype(F32)
        bnd = ((t >> k) << k) + (m - 1)
        levels.append((mask, (s <= bnd).astype(F32)))
        k += 1
    return tril, diag, levels


def hg_chunk(st, q, z, v, lb):
    C = q.shape[0]
    tril, diag, levels = _hg_consts(C)
    sig = jax.nn.sigmoid(z)
    lf = jnp.log(lb + (1.0 - lb) * sig)
    k = (1.0 - lb) * jax.nn.sigmoid(-z)
    qh = jax.nn.silu(q)
    b = mm_exact(tril, lf)
    blast = jnp.sum(lf, axis=0, keepdims=True)
    qe = qh * jnp.exp(b)
    kd = k * jnp.exp(blast - b)
    scaled = []
    for _, sel in levels:
        ref = mm_exact(sel, lf)
        scaled.append((qh * jnp.exp(jnp.minimum(b - ref, 0.0)), k * jnp.exp(jnp.minimum(ref - b, 0.0))))
    outs, news = [], []
    for h in range(HG_HEADS):
        sl = slice(h * HG_D, (h + 1) * HG_D)
        st_h = st[h * HG_D:(h + 1) * HG_D, :]
        sc = diag * mma_nt(qh[:, sl], k[:, sl])
        for (mask, _), (qt, kt) in zip(levels, scaled):
            sc = sc + mask * mma_nt(qt[:, sl], kt[:, sl])
        outs.append(mma_nt(qe[:, sl], st_h) + mma_nn(sc, v[:, sl]))
        news.append(st_h * jnp.exp(blast[:, sl]) + mma_tn(v[:, sl], kd[:, sl]))
    return jnp.concatenate(news, axis=0), jnp.concatenate(outs, axis=1)


def hg_fwd(qzv, lb, L):
    C = HG_CHUNK
    nc = L // C

    def body(q_ref, z_ref, v_ref, lb_ref, o_ref, sst_ref, st_ref):
        @pl.when(pl.program_id(0) == 0)
        def _():
            st_ref[...] = jnp.zeros_like(st_ref)

        st = st_ref[...]
        sst_ref[...] = st
        new, o = hg_chunk(st, q_ref[...], z_ref[...], v_ref[...], lb_ref[...])
        st_ref[...] = new
        o_ref[...] = o

    col = lambda cb: pl.BlockSpec((C, BW), functools.partial(lambda c, cb: (c, cb), cb=cb))
    return pl.pallas_call(
        body, grid=(nc,), name="hg_fwd",
        in_specs=[col(0), col(1), col(2), pl.BlockSpec((1, BW), lambda c: (0, 0))],
        out_specs=[pl.BlockSpec((C, BW), lambda c: (c, 0)), pl.BlockSpec((None, BW, HG_D), lambda c: (c, 0, 0))],
        out_shape=[jax.ShapeDtypeStruct((L, BW), F32), jax.ShapeDtypeStruct((nc, BW, HG_D), F32)],
        scratch_shapes=[pltpu.VMEM((BW, HG_D), F32)],
        compiler_params=pltpu.CompilerParams(vmem_limit_bytes=VMEM_LIMIT, dimension_semantics=("arbitrary",)),
    )(qzv, qzv, qzv, lb)


def hg_bwd(qzv, lb, sst, do, L):
    C = HG_CHUNK
    nc = L // C

    def body(q_ref, z_ref, v_ref, lb_ref, sst_ref, do_ref, dq_ref, dz_ref, dv_ref, dlb_ref, dst_ref):
        @pl.when(pl.program_id(0) == 0)
        def _():
            dst_ref[...] = jnp.zeros_like(dst_ref)
            dlb_ref[...] = jnp.zeros_like(dlb_ref)

        _, vjp = jax.vjp(hg_chunk, sst_ref[...], q_ref[...], z_ref[...], v_ref[...], lb_ref[...])
        dst, dq, dz, dv, dlb = vjp((dst_ref[...], do_ref[...]))
        dst_ref[...] = dst
        dq_ref[...] = dq
        dz_ref[...] = dz
        dv_ref[...] = dv
        dlb_ref[...] += dlb

    col = lambda cb: pl.BlockSpec((C, BW), functools.partial(lambda c, cb: (nc - 1 - c, cb), cb=cb))
    rev = pl.BlockSpec((C, BW), lambda c: (nc - 1 - c, 0))
    return pl.pallas_call(
        body, grid=(nc,), name="hg_bwd",
        in_specs=[col(0), col(1), col(2), pl.BlockSpec((1, BW), lambda c: (0, 0)),
                  pl.BlockSpec((None, BW, HG_D), lambda c: (nc - 1 - c, 0, 0)), rev],
        out_specs=[rev, rev, rev, pl.BlockSpec((1, BW), lambda c: (0, 0))],
        out_shape=[jax.ShapeDtypeStruct((L, BW), F32)] * 3 + [jax.ShapeDtypeStruct((1, BW), F32)],
        scratch_shapes=[pltpu.VMEM((BW, HG_D), F32)],
        compiler_params=pltpu.CompilerParams(vmem_limit_bytes=VMEM_LIMIT, dimension_semantics=("arbitrary",)),
    )(qzv, qzv, qzv, lb, sst, do)


def _shift_down(x, d, rows, L):
    if d == 0:
        return x
    wrapped = jnp.where((rows & (NSEG - 1)) == 0, 0.0, pltpu.roll(x, NSEG * d + 1, 0))
    return jnp.where(rows < NSEG * d, wrapped, pltpu.roll(x, NSEG * d, 0))


def _shift_up(x, d, rows, L):
    if d == 0:
        return x
    wrapped = jnp.where((rows & (NSEG - 1)) == NSEG - 1, 0.0, pltpu.roll(x, L - (NSEG * d + 1), 0))
    return jnp.where(rows >= L - NSEG * d, wrapped, pltpu.roll(x, L - NSEG * d, 0))


def conv_fwd(proj, w, b, L):
    def body(x_ref, w_ref, b_ref, o_ref):
        x = x_ref[...]
        rows = lax.broadcasted_iota(jnp.int32, x.shape, 0)
        acc = jnp.broadcast_to(b_ref[...], x.shape)
        for k in range(CONV_W):
            acc = acc + w_ref[pl.ds(k, 1), :] * _shift_down(x, CONV_W - 1 - k, rows, L)
        o_ref[...] = acc

    nl = BW // LANE
    return pl.pallas_call(
        body, grid=(nl,), name="conv_fwd",
        in_specs=[pl.BlockSpec((L, LANE), lambda g: (0, 5 * nl + g)), pl.BlockSpec((CONV_W, LANE), lambda g: (0, g)),
                  pl.BlockSpec((1, LANE), lambda g: (0, g))],
        out_specs=pl.BlockSpec((L, LANE), lambda g: (0, g)),
        out_shape=jax.ShapeDtypeStruct((L, BW), F32),
        compiler_params=pltpu.CompilerParams(vmem_limit_bytes=VMEM_LIMIT),
    )(proj, w, b)


def conv_bwd(proj, w, dxc, L):
    def body(x_ref, w_ref, d_ref, dx_ref, dw_ref, db_ref):
        x, d = x_ref[...], d_ref[...]
        rows = lax.broadcasted_iota(jnp.int32, x.shape, 0)
        acc = jnp.zeros_like(x)
        for k in range(CONV_W):
            acc = acc + w_ref[pl.ds(k, 1), :] * _shift_up(d, CONV_W - 1 - k, rows, L)
            dw_ref[pl.ds(k, 1), :] = jnp.sum(d * _shift_down(x, CONV_W - 1 - k, rows, L), axis=0, keepdims=True)
        dx_ref[...] = acc
        db_ref[...] = jnp.sum(d, axis=0, keepdims=True)

    nl = BW // LANE
    blk = pl.BlockSpec((L, LANE), lambda g: (0, g))
    return pl.pallas_call(
        body, grid=(nl,), name="conv_bwd",
        in_specs=[pl.BlockSpec((L, LANE), lambda g: (0, 5 * nl + g)), pl.BlockSpec((CONV_W, LANE), lambda g: (0, g)), blk],
        out_specs=[blk, pl.BlockSpec((CONV_W, LANE), lambda g: (0, g)), pl.BlockSpec((1, LANE), lambda g: (0, g))],
        out_shape=[jax.ShapeDtypeStruct((L, BW), F32), jax.ShapeDtypeStruct((CONV_W, BW), F32),
                   jax.ShapeDtypeStruct((1, BW), F32)],
        compiler_params=pltpu.CompilerParams(vmem_limit_bytes=VMEM_LIMIT),
    )(proj, w, dxc)


def loss_fwd_bwd(x, fw, target, L, tm):
    def fn(x, fw, t):
        err = jnp.square(_rms(x, fw) - t)
        return jnp.sum(0.5 * jnp.mean(err, axis=-1, keepdims=True), axis=0, keepdims=True)

    def body(x_ref, fw_ref, t_ref, l_ref, dx_ref, dfw_ref):
        i = pl.program_id(0)
        t = t_ref[...]
        val, vjp = jax.vjp(lambda x, fw: fn(x, fw, t), x_ref[...], fw_ref[...])
        dx, dfw = vjp(jnp.ones((1, 1), F32))
        dx_ref[...] = dx

        @pl.when(i == 0)
        def _():
            l_ref[...] = jnp.zeros_like(l_ref)
            dfw_ref[...] = jnp.zeros_like(dfw_ref)

        l_ref[...] += jnp.broadcast_to(val, l_ref.shape)
        dfw_ref[...] += dfw

    row = pl.BlockSpec((tm, D_MODEL), lambda i: (i, 0))
    vec = pl.BlockSpec((1, D_MODEL), lambda i: (0, 0))
    return pl.pallas_call(
        body, grid=(L // tm,), name="loss_fwd_bwd", in_specs=[row, vec, row],
        out_specs=[pl.BlockSpec((1, LANE), lambda i: (0, 0)), row, vec],
        out_shape=[jax.ShapeDtypeStruct((1, LANE), F32), jax.ShapeDtypeStruct((L, D_MODEL), F32),
                   jax.ShapeDtypeStruct((1, D_MODEL), F32)],
        compiler_params=pltpu.CompilerParams(vmem_limit_bytes=VMEM_LIMIT, dimension_semantics=("arbitrary",)),
    )(x, fw, target)


def adamw(w, g, m, v):
    rows, cols = w.shape
    tr = _row_tile(rows, cols, budget=1024 * 1024)
    c1 = 1.0 - ADAM_B1 ** ADAM_STEP
    c2 = 1.0 - ADAM_B2 ** ADAM_STEP

    def body(w_ref, g_ref, m_ref, v_ref, d_ref, nm_ref, nv_ref):
        g = g_ref[...]
        nm = ADAM_B1 * m_ref[...] + (1.0 - ADAM_B1) * g
        nv = ADAM_B2 * v_ref[...] + (1.0 - ADAM_B2) * jnp.square(g)
        d_ref[...] = -ADAM_LR * ((nm / c1) / (jnp.sqrt(nv / c2) + ADAM_EPS) + ADAM_WD * w_ref[...])
        nm_ref[...] = nm
        nv_ref[...] = nv

    blk = pl.BlockSpec((tr, cols), lambda i: (i, 0))
    return pl.pallas_call(
        body, grid=(rows // tr,), name="adamw", in_specs=[blk] * 4, out_specs=[blk] * 3,
        out_shape=[jax.ShapeDtypeStruct((rows, cols), F32)] * 3,
    )(w, g, m, v)


def s5_prep(lam_re, lam_im, log_dt, b_re, b_im, c_re, c_im):
    lr = jnp.minimum(lam_re, -1e-4)
    li = lam_im
    dt = jnp.exp(log_dt)[:, None]
    mag = jnp.exp(lr * dt)
    ar = mag * jnp.cos(li * dt)
    ai = mag * jnp.sin(li * dt)
    den = lr * lr + li * li
    fr = ((ar - 1.0) * lr + ai * li) / den
    fi = (ai * lr - (ar - 1.0) * li) / den
    bbr = fr[..., None] * b_re - fi[..., None] * b_im
    bbi = fr[..., None] * b_im + fi[..., None] * b_re
    hg = S5_GROUPS // 2
    emb_b = lambda bb: _block_diag(bb.transpose(0, 2, 1).reshape(hg * S5_GROUP, S5_STATE), hg)
    emb_c = lambda cc: _block_diag(cc.transpose(0, 2, 1).reshape(hg * S5_STATE, S5_GROUP), hg)
    bsub = jnp.stack([emb_b(bbr[:hg]), emb_b(bbr[hg:]), emb_b(bbi[:hg]), emb_b(bbi[hg:])])
    csub = jnp.stack([jnp.concatenate([emb_c(c_re[:hg]), -emb_c(c_im[:hg])], axis=0),
                      jnp.concatenate([emb_c(c_re[hg:]), -emb_c(c_im[hg:])], axis=0)])
    nb = S5_N // LANE
    return ar.reshape(nb, 1, LANE), ai.reshape(nb, 1, LANE), bsub, csub


def _block_diag(stacked, groups):
    rows, c = stacked.shape
    r = rows // groups
    row_g = jnp.arange(rows)[:, None] // r
    col_g = jnp.arange(groups * c)[None, :] // c
    return jnp.where(row_g == col_g, jnp.tile(stacked, (1, groups)), 0.0)


def rg_prep(w):
    return _block_diag(w.reshape(BW, RG_BLOCK), RG_BLOCKS)


def hg_prep(logits):
    p = jax.nn.softmax(logits, axis=0)
    return jnp.cumsum(p, axis=0) - p[0]


def _head_mean_matrix():
    r = jnp.arange(BW) // HG_D
    return (r[:, None] == r[None, :]).astype(F32) / HG_D


def _to_segment_order(a):
    L = a.shape[0]
    return a.reshape(NSEG, L // NSEG, -1).transpose(1, 0, 2).reshape(a.shape)


def _to_time_order(a):
    L = a.shape[0]
    return a.reshape(L // NSEG, NSEG, -1).transpose(1, 0, 2).reshape(a.shape)


def _const(*idx):
    return lambda s, i: idx


def _rows(cb=0):
    return lambda s, i: (i, cb)


def _sum_parts(name, first, parts, shape):
    return add_n(name, [(first, ())] + [(parts, (s,)) for s in range(NSH)], shape)


def _ffn_weight_specs(l, j):
    F = D_FF // NSH
    one = pl.Buffered(1)
    return [pl.BlockSpec((None, NSH, D_MODEL, F), lambda i: (j, 0, 0, 0), pipeline_mode=one),
            pl.BlockSpec((None, NSH, D_MODEL, F), lambda i: (j, 0, 0, 0), pipeline_mode=one),
            pl.BlockSpec((None, NSH, F, D_MODEL), lambda i: (j, 0, 0, 0), pipeline_mode=one)]


def ffn_fwd(name, x, W, l, j, k, L, tm):
    D, F = D_MODEL, D_FF // NSH

    def body(x_ref, nw_ref, wg_ref, wu_ref, wd_ref, y_ref, g_ref, u_ref):
        x = x_ref[...]
        h = _rms(x, nw_ref[...]).astype(MMT)
        y = x
        for s in range(NSH):
            g = _dg(h, wg_ref[s], 1, 0)
            u = _dg(h, wu_ref[s], 1, 0)
            g_ref[s] = g.astype(g_ref.dtype)
            u_ref[s] = u.astype(u_ref.dtype)
            y = y + 0.5 * _dg((jax.nn.silu(g) * u).astype(MMT), wd_ref[s], 1, 0)
        y_ref[...] = y

    row = pl.BlockSpec((tm, D), lambda i: (i, 0))
    act = pl.BlockSpec((NSH, tm, F), lambda i: (0, i, 0))
    return pl.pallas_call(
        body, grid=(L // tm,), name=name,
        in_specs=[row, pl.BlockSpec((None, None, 1, D), lambda i: (l, k, 0, 0))] + _ffn_weight_specs(l, j),
        out_specs=[row, act, act],
        out_shape=[jax.ShapeDtypeStruct((L, D), F32), jax.ShapeDtypeStruct((NSH, L, F), MMT),
                   jax.ShapeDtypeStruct((NSH, L, F), MMT)],
        compiler_params=pltpu.CompilerParams(vmem_limit_bytes=VMEM_LIMIT, dimension_semantics=("arbitrary",)),
    )(x, W["nw"], W["L"][l]["wg"], W["L"][l]["wu"], W["L"][l]["wd"])


def ffn_bwd(name, x, g, u, dy, W, bufs, l, j, k, L, tm):
    D, F = D_MODEL, D_FF // NSH
    tm = min(TM_WGRAD, L)

    def body(x_ref, nw_ref, dy_ref, g_ref, u_ref, wg_ref, wu_ref, wd_ref, *rest):
        part_ref, dnw_ref, dwg_ref, dwu_ref, dwd_ref = rest[-5:]
        s, i = pl.program_id(0), pl.program_id(1)
        x, nw = x_ref[...], nw_ref[...]
        r = lax.rsqrt(jnp.mean(x * x, axis=-1, keepdims=True) + EPS)
        xhat = x * r
        h = (xhat * nw).astype(MMT)
        half_dy = (0.5 * dy_ref[...]).astype(MMT)
        gs, us = g_ref[...].astype(F32), u_ref[...].astype(F32)
        sig = jax.nn.sigmoid(gs)
        act = gs * sig
        da = _dg(half_dy, wd_ref[...], 1, 1)
        du = (da * act).astype(MMT)
        dg = (da * us * (sig * (1.0 + gs * (1.0 - sig)))).astype(MMT)
        dh = _dg(dg, wg_ref[...], 1, 1) + _dg(du, wu_ref[...], 1, 1)
        dxh = dh * nw
        part_ref[...] = r * (dxh - xhat * jnp.mean(dxh * xhat, axis=-1, keepdims=True))
        grads = (_dg(h, dg, 0, 0), _dg(h, du, 0, 0), _dg((act * us).astype(MMT), half_dy, 0, 0))
        dnw = jnp.sum(dh * xhat, axis=0, keepdims=True)
        first = jnp.logical_and(s == 0, i == 0)
        for ref, val, start in zip((dwg_ref, dwu_ref, dwd_ref, dnw_ref), grads + (dnw,), (i == 0, i == 0, i == 0, first)):
            @pl.when(start)
            def _(ref=ref, val=val):
                ref[...] = val

            @pl.when(jnp.logical_not(start))
            def _(ref=ref, val=val):
                ref[...] += val

    row = pl.BlockSpec((tm, D), lambda s, i: (i, 0))
    act = pl.BlockSpec((None, tm, F), lambda s, i: (s, i, 0))
    wsp = lambda r, c: pl.BlockSpec((None, None, r, c), lambda s, i: (j, s, 0, 0))
    gsp = lambda r, c: pl.BlockSpec((None, None, r, c), lambda s, i: (0, s, 0, 0))
    part, dnw, bufs[("ffn_gate", l, j)], bufs[("ffn_up", l, j)], bufs[("ffn_down", l, j)] = pl.pallas_call(
        body, grid=(NSH, L // tm), name=name,
        in_specs=[row, pl.BlockSpec((None, None, 1, D), lambda s, i: (l, k, 0, 0)), row, act, act,
                  wsp(D, F), wsp(D, F), wsp(F, D)],
        out_specs=[pl.BlockSpec((None, tm, D), lambda s, i: (s, i, 0)), pl.BlockSpec((1, D), lambda s, i: (0, 0)),
                   gsp(D, F), gsp(D, F), gsp(F, D)],
        out_shape=[jax.ShapeDtypeStruct((NSH, L, D), F32), jax.ShapeDtypeStruct((1, D), F32)]
        + [jax.ShapeDtypeStruct((1, NSH, D, F), F32)] * 2 + [jax.ShapeDtypeStruct((1, NSH, F, D), F32)],
        compiler_params=pltpu.CompilerParams(vmem_limit_bytes=VMEM_LIMIT, dimension_semantics=("arbitrary", "arbitrary")),
    )(x, W["nw"], dy, g, u, W["L"][l]["wg"], W["L"][l]["wu"], W["L"][l]["wd"])
    return _sum_parts(name + "_dx", dy, part, (L, D)), dnw


def layer_fwd(l, x0, W, P, L, tm):
    D = D_MODEL
    tmm = tm
    tm = min(TM_FWD, L)
    n_i = L // tm
    x1, g0, u0 = ffn_fwd(f"ffn_fwd_{l}0", x0, W, l, 0, 0, L, tm)
    proj = tile_fwd(
        lambda x, nw, win, s: pre_core(x, nw, win), f"pre_fwd_{l}", n_i, NSH,
        [(x1, (tm, D), _rows()), (W["nw"], (None, None, 1, D), _const(l, 1, 0, 0)),
         (W["L"][l]["win"], (None, D, IN_TOTAL // NSH), lambda s, i: (s, 0, 0))],
        [((L, IN_TOTAL), F32, (tm, IN_TOTAL // NSH), lambda s, i: (i, s), False)], s_outer=True)[0]
    nb = S5_N // LANE
    blk3 = lambda s, i: (0, i, 0)
    bur, bui = tile_fwd(
        lambda *a: s5_pre_core(*a[:-1]), f"s5pre_fwd_{l}", n_i, 1,
        [(proj, (tm, BW), _rows(0))] + [(P["bsub"], (None, None, BW // 2, S5_N // 2), _const(l, q, 0, 0)) for q in range(4)],
        [((nb, L, LANE), F32, (nb, tm, LANE), blk3, False)] * 2)
    xr, xi = s5_scan_fwd(bur, bui, P["ar"][l], P["ai"][l], L)
    qzv = _to_time_order(proj[:, BW:4 * BW])
    o_t, sst = hg_fwd(qzv, P["lb"][l], L)
    o = _to_segment_order(o_t)
    xc = conv_fwd(proj, W["convw"][l], P["convb"][l], L)
    vec = (None, 1, BW)
    a, b = tile_fwd(
        lambda xc, wa, ba, wx, bx, lam, s: gates_core(xc, wa, ba, wx, bx, lam), f"gates_fwd_{l}", n_i, 1,
        [(xc, (tm, BW), _rows()), (P["wa"], (None, BW, BW), _const(l, 0, 0)), (P["ba"], vec, _const(l, 0, 0)),
         (P["wx"], (None, BW, BW), _const(l, 0, 0)), (P["bx"], vec, _const(l, 0, 0)), (P["lam"], vec, _const(l, 0, 0))],
        [((L, BW), F32, (tm, BW), _rows(), False)] * 2)
    hs = rg_scan_fwd(a, b, L)
    ya, yb, yc = tile_fwd(
        lambda *a: mid_core(*a[:-1]), f"mid_fwd_{l}", L // tmm, 1,
        [(xr, (nb, tmm, LANE), blk3), (xi, (nb, tmm, LANE), blk3), (proj, (tmm, BW), _rows(0)), (o, (tmm, BW), _rows()),
         (proj, (tmm, BW), _rows(4)), (hs, (tmm, BW), _rows()), (proj, (tmm, BW), _rows(6)),
         (P["hmat"], (BW, BW), _const(0, 0)), (P["csub"], (None, None, S5_N, BW // 2), _const(l, 0, 0, 0)),
         (P["csub"], (None, None, S5_N, BW // 2), _const(l, 1, 0, 0)), (P["d"], vec, _const(l, 0, 0)),
         (W["L"][l]["gluw"], (BW, BW), _const(0, 0)), (P["glub"], vec, _const(l, 0, 0)), (P["hgw"], vec, _const(l, 0, 0))],
        [((L, BW), F32, (tmm, BW), _rows(), False)] * 3)
    x2 = tile_fwd(
        lambda x, *rest: (x + merge_core(*rest[:-1])[0],), f"merge_fwd_{l}", n_i, 1,
        [(x1, (tm, D), _rows()), (ya, (tm, BW), _rows()), (yb, (tm, BW), _rows()), (yc, (tm, BW), _rows())]
        + [(proj, (tm, BW), _rows(7 + k)) for k in range(6)]
        + [(W["L"][l]["pfull"], (3, BW, D), _const(0, 0, 0)), (W["L"][l]["woutfull"], (D, D), _const(0, 0))],
        [((L, D), F32, (tm, D), _rows(), False)])[0]
    x3, g1, u1 = ffn_fwd(f"ffn_fwd_{l}1", x2, W, l, 1, 2, L, tm)
    saved = dict(x0=x0, x1=x1, x2=x2, proj=proj, xr=xr, xi=xi, o=o, sst=sst, xc=xc, a=a, hs=hs, ya=ya, yb=yb, yc=yc,
                 qzv=qzv, g0=g0, u0=u0, g1=g1, u1=u1)
    return x3, saved


def layer_bwd(l, dx3, sv, W, P, bufs, L, tm, ready=lambda l, group: None):
    D = D_MODEL
    n_i = L // tm
    nb = S5_N // LANE
    dq = D // NSH
    vec = (None, 1, BW)
    vout = ((1, BW), (1, BW), _const(0, 0), "acc_all")
    blk3 = lambda s, i: (0, i, 0)
    small = {}
    proj = sv["proj"]

    dx2, dnw2 = ffn_bwd(f"ffn_bwd_{l}1", sv["x2"], sv["g1"], sv["u1"], dx3, W, bufs, l, 1, 2, L, tm)
    ready(l, "ffn1")

    rw256 = ((L, BW), (tm, BW), _rows(), "write")
    res = tile_bwd(
        merge_core, f"merge_bwd_{l}", n_i, 1,
        [(sv["ya"], (tm, BW), _rows(), "r"), (sv["yb"], (tm, BW), _rows(), "r"), (sv["yc"], (tm, BW), _rows(), "r")]
        + [(proj, (tm, BW), _rows(7 + k), "r") for k in range(6)]
        + [(W["L"][l]["pfull"], (3, BW, D), _const(0, 0, 0), "w"), (W["L"][l]["woutfull"], (D, D), _const(0, 0), "w")],
        [(dx2, (tm, D), _rows())],
        [rw256] * 9
        + [((3, BW, D), (3, BW, D), _const(0, 0, 0), "acc_all"), ((D, D), (D, D), _const(0, 0), "acc_all")])
    dya, dyb, dyc = res[:3]
    dgm = res[3:9]
    bufs[("branch_proj", l)], bufs[("w_out", l)] = res[9:]
    ready(l, "merge")

    tmm = tm
    rw = ((L, BW), (tmm, BW), _rows(), "write")
    xw = ((nb, L, LANE), (nb, tmm, LANE), blk3, "write")
    res = tile_bwd(
        mid_core, f"mid_bwd_{l}", L // tmm, 1,
        [(sv["xr"], (nb, tmm, LANE), blk3, "r"), (sv["xi"], (nb, tmm, LANE), blk3, "r"), (proj, (tmm, BW), _rows(0), "r"),
         (sv["o"], (tmm, BW), _rows(), "r"), (proj, (tmm, BW), _rows(4), "r"), (sv["hs"], (tmm, BW), _rows(), "r"),
         (proj, (tmm, BW), _rows(6), "r"), (P["hmat"], (BW, BW), _const(0, 0), "c"),
         (P["csub"], (None, None, S5_N, BW // 2), _const(l, 0, 0, 0), "w"),
         (P["csub"], (None, None, S5_N, BW // 2), _const(l, 1, 0, 0), "w"), (P["d"], vec, _const(l, 0, 0), "p"),
         (W["L"][l]["gluw"], (BW, BW), _const(0, 0), "w"), (P["glub"], vec, _const(l, 0, 0), "p"),
         (P["hgw"], vec, _const(l, 0, 0), "p")],
        [(dya, (tmm, BW), _rows()), (dyb, (tmm, BW), _rows()), (dyc, (tmm, BW), _rows())],
        [xw, xw, rw, rw, rw, rw, rw,
         ((DEPTH, S5_N, BW // 2), (None, S5_N, BW // 2), _const(l, 0, 0), "acc_all", bufs.get("csub0")),
         ((DEPTH, S5_N, BW // 2), (None, S5_N, BW // 2), _const(l, 0, 0), "acc_all", bufs.get("csub1")), vout,
         ((BW, BW), (BW, BW), _const(0, 0), "acc_all"), vout, vout])
    dxr, dxi, du_skip, do, dg_b, dhs, dgate_c, bufs["csub0"], bufs["csub1"], dd, bufs[("s5_glu_w", l)], dglub, dhgw = res
    small["s5_d"], small["s5_glu_b"], small["hg_norm_w"] = dd[0], dglub[0], dhgw[0]
    ready(l, "mid")

    da, db = rg_scan_bwd(sv["a"], sv["hs"], dhs, L)
    wmat = lambda key: ((DEPTH, BW, BW), (None, BW, BW), _const(l, 0, 0), "acc_all", bufs.get(key))
    res = tile_bwd(
        gates_core, f"gates_bwd_{l}", n_i, 1,
        [(sv["xc"], (tm, BW), _rows(), "r"), (P["wa"], (None, BW, BW), _const(l, 0, 0), "w"), (P["ba"], vec, _const(l, 0, 0), "p"),
         (P["wx"], (None, BW, BW), _const(l, 0, 0), "w"), (P["bx"], vec, _const(l, 0, 0), "p"), (P["lam"], vec, _const(l, 0, 0), "p")],
        [(da, (tm, BW), _rows()), (db, (tm, BW), _rows())],
        [((L, BW), (tm, BW), _rows(), "write"), wmat("wa"), vout, wmat("wx"), vout, vout])
    dxc, bufs["wa"], dba, bufs["wx"], dbx, dlam = res
    small["rg_ba"], small["rg_bx"], small["rg_lambda"] = dba[0], dbx[0], dlam[0]
    dx_c, dconvw, dconvb = conv_bwd(proj, W["convw"][l], dxc, L)
    small["rg_conv_w"], small["rg_conv_b"] = dconvw, dconvb[0]

    dq_b, dz_b, dv_b, dlb = hg_bwd(sv["qzv"], P["lb"][l], sv["sst"], _to_time_order(do), L)
    dq_b, dz_b, dv_b = [_to_segment_order(a) for a in (dq_b, dz_b, dv_b)]

    gr, gi, dar, dai = s5_scan_bwd(dxr, dxi, sv["xr"], sv["xi"], P["ar"][l], P["ai"][l], L)
    bblk = (None, None, BW // 2, S5_N // 2)
    res = tile_bwd(
        s5_pre_core, f"s5pre_bwd_{l}", n_i, 1,
        [(proj, (tm, BW), _rows(0), "r")] + [(P["bsub"], bblk, _const(l, q, 0, 0), "w") for q in range(4)],
        [(gr, (nb, tm, LANE), blk3), (gi, (nb, tm, LANE), blk3)],
        [((L, BW), (tm, BW), _rows(), "write")]
        + [((DEPTH, BW // 2, S5_N // 2), bblk[1:], _const(l, 0, 0), "acc_all", bufs.get(f"bsub{q}")) for q in range(4)])
    du_pre = res[0]
    for q in range(4):
        bufs[f"bsub{q}"] = res[1 + q]
    du_a = add_n(f"du_a_{l}", [(du_skip, ()), (du_pre, ())], (L, BW))
    prep_ct = dict(dar=dar, dai=dai, dlb=dlb)

    pieces = [du_a, dq_b, dz_b, dv_b, dg_b, dx_c, dgate_c, *dgm]
    per_piece, per_shard = BW // LANE, IN_TOTAL // NSH // LANE
    part, dnw1 = None, []
    tmw = min(TM_WGRAD, L)
    for s in range(NSH):
        groups = [(pieces[g // per_piece], (tmw, LANE), _rows(g % per_piece))
                  for g in range(s * per_shard, (s + 1) * per_shard)]
        part, dnw_s, bufs[("w_in", l)] = tile_bwd(
            pre_core, f"pre_bwd_{l}{s}", L // tmw, 1,
            [(sv["x1"], (tmw, D), _rows(), "r"), (W["nw"], (None, None, 1, D), _const(l, 1, 0, 0), "p"),
             (W["L"][l]["win"], (None, D, IN_TOTAL // NSH), _const(s, 0, 0), "w")],
            [groups],
            [((NSH, L, D), (None, tmw, D), functools.partial(lambda _s, i, s: (s, i, 0), s=s), "write", part),
             ((1, D), (1, D), _const(0, 0), "acc_all"),
             ((1, NSH, D, IN_TOTAL // NSH), (None, None, D, IN_TOTAL // NSH), _const(0, s, 0, 0), "acc_all",
              bufs.get(("w_in", l)))])
        dnw1.append(dnw_s)
    dnw1 = (dnw1[0] + dnw1[1]) + (dnw1[2] + dnw1[3])
    dx1 = _sum_parts(f"pre_bwd_{l}_dx", dx2, part, (L, D))
    ready(l, "pre")

    dx0, dnw0 = ffn_bwd(f"ffn_bwd_{l}0", sv["x0"], sv["g0"], sv["u0"], dx1, W, bufs, l, 0, 0, L, tm)
    ready(l, "ffn0")
    small["norm_w"] = jnp.concatenate([dnw0, dnw1, dnw2], axis=0)
    return dx0, small, prep_ct


SMALL_RAW = ("s5_lambda_re", "s5_lambda_im", "s5_log_dt", "s5_b_re", "s5_b_im", "s5_c_re", "s5_c_im", "s5_d", "s5_glu_b",
             "hg_lb_logits", "hg_norm_w", "rg_conv_b", "rg_wa", "rg_ba", "rg_wx", "rg_bx", "rg_lambda", "final_norm_w")
DEPTH = 2


def local_step(x, target, W, raw, layer_weights=None, layer_grads=None):
    L = x.shape[0]
    tm = min(256, L)
    col = lambda v: v.reshape(DEPTH, 1, BW)
    (ar, ai, bsub, csub), s5_vjp = jax.vjp(jax.vmap(s5_prep), *[raw[k] for k in SMALL_RAW[:7]])
    (wa, wx), rg_vjp = jax.vjp(lambda a, b: (jax.vmap(rg_prep)(a), jax.vmap(rg_prep)(b)), raw["rg_wa"], raw["rg_wx"])
    lb, hg_vjp = jax.vjp(hg_prep, raw["hg_lb_logits"])
    P = dict(
        ar=[ar[l] for l in range(DEPTH)], ai=[ai[l] for l in range(DEPTH)],
        bsub=bsub.astype(MMT), csub=csub.astype(MMT), wa=wa.astype(MMT), wx=wx.astype(MMT),
        lb=[lb[l].reshape(1, BW) for l in range(DEPTH)], convb=[raw["rg_conv_b"][l].reshape(1, BW) for l in range(DEPTH)],
        ba=col(raw["rg_ba"]), bx=col(raw["rg_bx"]), lam=col(raw["rg_lambda"]), d=col(raw["s5_d"]),
        glub=col(raw["s5_glu_b"]), hgw=col(raw["hg_norm_w"]), hmat=_head_mean_matrix())

    saved = []
    h = _to_segment_order(x)
    for l in range(DEPTH):
        if layer_weights is not None:
            W["L"][l], h = layer_weights(l, h)
        h, sv = layer_fwd(l, h, W, P, L, tm)
        saved.append(sv)
    loss, dh, dfw = loss_fwd_bwd(h, raw["final_norm_w"].reshape(1, D_MODEL), _to_segment_order(target), L, tm)

    big, per_layer, prep_cts = {}, [None] * DEPTH, [None] * DEPTH
    ready = (lambda l, group: None) if layer_grads is None else (lambda l, group: layer_grads(l, group, big))
    for l in reversed(range(DEPTH)):
        dh, sm, pc = layer_bwd(l, dh, saved[l], W, P, big, L, tm, ready)
        per_layer[l], prep_cts[l] = sm, pc
    dh = _to_time_order(dh)

    small = {k: jnp.stack([per_layer[l][k] for l in range(DEPTH)]) for k in per_layer[0]}
    both = lambda k: jnp.stack([prep_cts[l][k] for l in range(DEPTH)])
    dbsub = jnp.stack([big.pop(f"bsub{q}") for q in range(4)], axis=1)
    dcsub = jnp.stack([big.pop("csub0"), big.pop("csub1")], axis=1)
    s5_g = s5_vjp((both("dar"), both("dai"), dbsub, dcsub))
    small.update(zip(SMALL_RAW[:7], s5_g))
    small["rg_wa"], small["rg_wx"] = rg_vjp((big.pop("wa"), big.pop("wx")))
    (small["hg_lb_logits"],) = hg_vjp(jnp.concatenate([prep_cts[l]["dlb"] for l in range(DEPTH)], axis=0))
    small["final_norm_w"] = dfw[0]
    return loss, dh, big, small


ANY = pl.BlockSpec(memory_space=pl.ANY)


def _place():
    x, y, c = lax.axis_index("x"), lax.axis_index("y"), lax.axis_index("c")
    chips = [(1 - x, y), (x, 1 - y), (1 - x, 1 - y)]
    return x, y, c, chips


def _remote(src, dst, send, recv, k, to):
    return pltpu.make_async_remote_copy(src_ref=src, dst_ref=dst, send_sem=send.at[k], recv_sem=recv.at[k],
                                        device_id=to, device_id_type=MESH)


def _comm_call(body, name, ins, out_shapes, n_sem, n_loc):
    return pl.pallas_call(
        body, name=name, in_specs=[ANY] * len(ins), out_specs=[ANY] * len(out_shapes), out_shape=out_shapes,
        scratch_shapes=[pltpu.SemaphoreType.DMA((n_sem,)), pltpu.SemaphoreType.DMA((n_sem,)),
                        pltpu.SemaphoreType.DMA((max(n_loc, 1),))],
    )(*ins)


def gather_shards(name, shards):
    n = len(shards)
    per = 8

    def body(*refs):
        ins, outs = refs[:n], refs[n:2 * n]
        send, recv, _ = refs[2 * n:]
        x, y, c, chips = _place()
        me = 2 * x + y
        sib = (x, y, 1 - c)
        sends = []
        for w in range(n):
            for j, (cx, cy) in enumerate(chips):
                cp = _remote(ins[w].at[c], outs[w].at[c, me], send, recv, per * w + j, (cx, cy, c))
                cp.start()
                sends.append(cp)
        for w in range(n):
            for l in range(2):
                cp = _remote(ins[w].at[l], outs[w].at[l, me], send, recv, per * w + 6 + l, sib)
                cp.start()
                sends.append(cp)
        for w in range(n):
            for j, (cx, cy) in enumerate(chips):
                theirs = outs[w].at[c, 2 * cx + cy]
                _remote(ins[w].at[c], theirs, send, recv, per * w + j, (cx, cy, c)).wait_recv()
                cp = _remote(theirs, theirs, send, recv, per * w + 3 + j, sib)
                cp.start()
                sends.append(cp)
        for w in range(n):
            for j, (cx, cy) in enumerate(chips):
                dst = outs[w].at[1 - c, 2 * cx + cy]
                _remote(dst, dst, send, recv, per * w + 3 + j, sib).wait_recv()
            for l in range(2):
                dst = outs[w].at[l, me]
                _remote(dst, dst, send, recv, per * w + 6 + l, sib).wait_recv()
        for cp in sends:
            cp.wait_send()

    shapes = [jax.ShapeDtypeStruct((2, NSH) + s.shape[1:], s.dtype) for s in shards]
    return _comm_call(body, name, shards, shapes, per * n, 0)


def exchange_halves(name, grads, ranges):
    n = len(grads)

    def body(*refs):
        ins, outs = refs[:n], refs[n:2 * n]
        send, recv, _ = refs[2 * n:]
        x, y, c, _chips = _place()
        cps = []
        for w in range(n):
            h = grads[w].shape[2] // 2
            p0, np_ = ranges[w]
            cp = _remote(ins[w].at[pl.ds(p0, np_), :, pl.ds((1 - c) * h, h)], outs[w], send, recv, w, (x, y, 1 - c))
            cp.start()
            cps.append(cp)
        for cp in cps:
            cp.wait()

    shapes = [jax.ShapeDtypeStruct((r[1], NSH, g.shape[2] // 2, g.shape[3]), g.dtype) for g, r in zip(grads, ranges)]
    return _comm_call(body, name, grads, shapes, n, 0)


def scatter_to_chips(name, halves):
    n = len(halves)

    def body(*refs):
        ins, outs = refs[:n], refs[n:2 * n]
        send, recv, _ = refs[2 * n:]
        x, y, c, chips = _place()
        cps = []
        for w in range(n):
            for j, (cx, cy) in enumerate(chips):
                cp = _remote(ins[w].at[:, 2 * cx + cy], outs[w].at[j], send, recv, 3 * w + j, (cx, cy, c))
                cp.start()
                cps.append(cp)
        for cp in cps:
            cp.wait()

    shapes = [jax.ShapeDtypeStruct((3, h.shape[0]) + h.shape[2:], h.dtype) for h in halves]
    return _comm_call(body, name, halves, shapes, 3 * n, 0)


def share_halves(name, pieces):
    n = len(pieces)

    def body(*refs):
        ins, outs = refs[:n], refs[n:2 * n]
        send, recv, _ = refs[2 * n:]
        x, y, c, _chips = _place()
        cps = []
        for w in range(n):
            cp = _remote(ins[w], outs[w], send, recv, w, (x, y, 1 - c))
            cp.start()
            cps.append(cp)
        for cp in cps:
            cp.wait()

    return _comm_call(body, name, pieces, [jax.ShapeDtypeStruct(p.shape, p.dtype) for p in pieces], n, 0)


def add_own_half(name, g, ra, c, wire, b0):
    nblk, h, cols = ra.shape
    tr = _row_tile(h, cols, mult=16)
    nt = h // tr

    def body(c_ref, g_ref, r_ref, o_ref):
        o_ref[...] = (g_ref[...] + r_ref[...]).astype(o_ref.dtype)

    blk = (None, tr, cols)
    return pl.pallas_call(
        body, name=name,
        grid_spec=pltpu.PrefetchScalarGridSpec(
            num_scalar_prefetch=1, grid=(nblk, nt),
            in_specs=[pl.BlockSpec(blk, lambda s, i, c_ref: (b0 + s, c_ref[0] * nt + i, 0)), pl.BlockSpec(blk, lambda s, i, c_ref: (s, i, 0))],
            out_specs=pl.BlockSpec(blk, lambda s, i, c_ref: (s, i, 0))),
        out_shape=jax.ShapeDtypeStruct(ra.shape, wire),
    )(c.reshape(1), g, ra)


def add_chips(name, hb, rb, me):
    npc, _, h, cols = hb.shape
    tr = _row_tile(h, cols, mult=16)

    def body(me_ref, h_ref, r0, r1, r2, o_ref):
        f = lambda r: r[...].astype(F32)
        o_ref[...] = ((f(h_ref) + f(r0)) + f(r1)) + f(r2)

    rspec = lambda j: pl.BlockSpec((None, None, tr, cols), functools.partial(lambda p, i, me_ref, j: (j, p, i, 0), j=j))
    return pl.pallas_call(
        body, name=name,
        grid_spec=pltpu.PrefetchScalarGridSpec(
            num_scalar_prefetch=1, grid=(npc, h // tr),
            in_specs=[pl.BlockSpec((None, None, tr, cols), lambda p, i, me_ref: (p, me_ref[0], i, 0)), rspec(0), rspec(1), rspec(2)],
            out_specs=pl.BlockSpec((None, tr, cols), lambda p, i, me_ref: (p, i, 0))),
        out_shape=jax.ShapeDtypeStruct((npc, h, cols), F32),
    )(me.reshape(1), hb, rb, rb, rb)


def adamw_halves(name, w, m, v, own, other, c):
    npc, rows, cols = w.shape
    h = rows // 2
    tr = _row_tile(h, cols, budget=1024 * 1024)
    nt = h // tr
    c1 = 1.0 - ADAM_B1 ** ADAM_STEP
    c2 = 1.0 - ADAM_B2 ** ADAM_STEP

    def body(c_ref, w_ref, m_ref, v_ref, own_ref, oth_ref, g_ref, d_ref, nm_ref, nv_ref):
        g = jnp.where(pl.program_id(1) == c_ref[0], own_ref[...], oth_ref[...])
        nm = ADAM_B1 * m_ref[...] + (1.0 - ADAM_B1) * g
        nv = ADAM_B2 * v_ref[...] + (1.0 - ADAM_B2) * jnp.square(g)
        g_ref[...] = g
        d_ref[...] = -ADAM_LR * ((nm / c1) / (jnp.sqrt(nv / c2) + ADAM_EPS) + ADAM_WD * w_ref[...])
        nm_ref[...] = nm
        nv_ref[...] = nv

    full = pl.BlockSpec((None, tr, cols), lambda p, hh, i, c_ref: (p, hh * nt + i, 0))
    half = pl.BlockSpec((None, tr, cols), lambda p, hh, i, c_ref: (p, i, 0))
    return pl.pallas_call(
        body, name=name,
        grid_spec=pltpu.PrefetchScalarGridSpec(
            num_scalar_prefetch=1, grid=(npc, 2, nt),
            in_specs=[full, full, full, half, half], out_specs=[full] * 4),
        out_shape=[jax.ShapeDtypeStruct(w.shape, F32)] * 4,
    )(c.reshape(1), w, m, v, own, other)


WEIGHTS = ("norm_w", "final_norm_w", "ffn_gate", "ffn_up", "ffn_down", "w_in", "branch_proj", "w_out", "s5_lambda_re",
           "s5_lambda_im", "s5_log_dt", "s5_b_re", "s5_b_im", "s5_c_re", "s5_c_im", "s5_d", "s5_glu_w", "s5_glu_b",
           "hg_lb_logits", "hg_norm_w", "rg_conv_w", "rg_conv_b", "rg_wa", "rg_ba", "rg_wx", "rg_bx", "rg_lambda")
BIG = ("ffn_gate", "ffn_up", "ffn_down", "w_in", "branch_proj", "w_out", "s5_glu_w")
SHARDED_SMALL = ("norm_w", "rg_conv_w")
SMALL = SMALL_RAW + SHARDED_SMALL


def _view2d(shape):
    return (1, shape[0]) if len(shape) == 1 else (math.prod(shape[:-1]), shape[-1])


def _small_layout(shapes, row_multiple):
    layout, at = [], 0
    for shape in shapes:
        r, c = _view2d(shape)
        rp = -(-r // 8) * 8
        layout.append((at, r, c, rp))
        at += rp * max(1, c // LANE)
    return layout, -(-at // row_multiple) * row_multiple


def pack_small(name, arrays, row_multiple):
    layout, rows = _small_layout([a.shape for a in arrays], row_multiple)

    def body(*refs):
        out = refs[-1]
        out[...] = jnp.zeros_like(out)
        for ref, (r0, r, c, rp) in zip(refs[:-1], layout):
            if c <= LANE:
                out[r0:r0 + r, 0:c] = ref[...]
            else:
                for q in range(c // LANE):
                    out[r0 + q * rp:r0 + q * rp + r, :] = ref[:, q * LANE:(q + 1) * LANE]

    return pl.pallas_call(
        body, name=name, out_shape=jax.ShapeDtypeStruct((rows, LANE), F32),
        compiler_params=pltpu.CompilerParams(vmem_limit_bytes=VMEM_LIMIT),
    )(*[a.reshape(_view2d(a.shape)) for a in arrays])


def unpack_small(name, packed, shapes):
    layout, _ = _small_layout(shapes, 8)

    def body(p_ref, *outs):
        for ref, (r0, r, c, rp) in zip(outs, layout):
            if c <= LANE:
                ref[...] = p_ref[r0:r0 + r, 0:c]
            else:
                for q in range(c // LANE):
                    ref[:, q * LANE:(q + 1) * LANE] = p_ref[r0 + q * rp:r0 + q * rp + r, :]

    res = pl.pallas_call(
        body, name=name, out_shape=[jax.ShapeDtypeStruct(_view2d(s), F32) for s in shapes],
        compiler_params=pltpu.CompilerParams(vmem_limit_bytes=VMEM_LIMIT),
    )(packed)
    return [a.reshape(s) for a, s in zip(res, shapes)]


HBM = pl.BlockSpec(memory_space=pltpu.HBM)
SEM = pl.BlockSpec(memory_space=pltpu.SEMAPHORE)
EFFECT = pltpu.SideEffectType.DATAFLOW_SIDE_EFFECTING


def split_start(name, srcs, land_shapes, plan, n_send, n_recv):
    ns, nl = len(srcs), len(land_shapes)

    def body(*refs):
        ins, lands = refs[:ns], refs[ns:ns + nl]
        send, recv = refs[ns + nl], refs[ns + nl + 1]
        for src, dst, ks, kr, dev in plan(ins, lands):
            pltpu.make_async_remote_copy(src_ref=src, dst_ref=dst, send_sem=send.at[ks], recv_sem=recv.at[kr],
                                         device_id=dev, device_id_type=MESH).start()
        refs[-1][...] = jnp.zeros_like(refs[-1])

    hbm = lambda a: pltpu.with_memory_space_constraint(a, pltpu.HBM)
    lands = [lax.empty(s.shape, s.dtype) for s in land_shapes]
    out = pl.pallas_call(
        body, name=name,
        out_shape=(pltpu.SemaphoreType.DMA((n_send,)), pltpu.SemaphoreType.DMA((n_recv,)),
                   *[pltpu.HBM(a.shape, a.dtype) for a in srcs], *[pltpu.HBM(s.shape, s.dtype) for s in land_shapes],
                   jax.ShapeDtypeStruct((8, LANE), F32)),
        in_specs=[HBM] * (ns + nl), out_specs=(SEM, SEM, *[HBM] * (ns + nl), pl.BlockSpec(memory_space=pltpu.VMEM)),
        input_output_aliases={k: 2 + k for k in range(ns + nl)},
        compiler_params=pltpu.CompilerParams(has_side_effects=EFFECT),
    )(*[hbm(a) for a in srcs], *[hbm(a) for a in lands])
    return out[:-1], out[-1]


def split_wait(name, handles, n_src, waits, after):
    send, recv, *bufs = handles
    nb = len(bufs)

    def body(*refs):
        ins, lands = refs[:n_src], refs[n_src:nb]
        send_sem, recv_sem = refs[nb], refs[nb + 1]
        x, y, c, _chips = _place()
        sends, recvs = waits(ins, lands)
        for src, k in sends:
            pltpu.make_async_remote_copy(src_ref=src, dst_ref=src, send_sem=send_sem.at[k], recv_sem=recv_sem.at[0],
                                         device_id=(x, y, 1 - c), device_id_type=MESH).wait_send()
        for dst, k in recvs:
            pltpu.make_async_remote_copy(src_ref=dst, dst_ref=dst, send_sem=send_sem.at[0], recv_sem=recv_sem.at[k],
                                         device_id=(x, y, 1 - c), device_id_type=MESH).wait_recv()

    out = pl.pallas_call(
        body, name=name, out_shape=tuple(pltpu.HBM(a.shape, a.dtype) for a in bufs),
        in_specs=[HBM] * nb + [SEM, SEM, ANY], out_specs=tuple([HBM] * nb),
        input_output_aliases={k: k for k in range(nb)},
        compiler_params=pltpu.CompilerParams(has_side_effects=EFFECT),
    )(*bufs, send, recv, after)
    return list(out[:n_src]), list(out[n_src:])


def gather_plan(n):
    def plan(ins, lands):
        x, y, c, chips = _place()
        me = 2 * x + y
        copies = []
        for w in range(n):
            for j, (cx, cy) in enumerate(chips):
                for t in range(2):
                    copies.append((ins[w].at[c], lands[w].at[c, me], 8 * w + 2 * j + t, 8 * w + 2 * j + c, (cx, cy, t)))
            for half in range(2):
                copies.append((ins[w].at[half], lands[w].at[half, me], 8 * w + 6 + half, 8 * w + 6 + half, (x, y, 1 - c)))
        return copies

    def waits(ins, lands):
        x, y, c, chips = _place()
        me = 2 * x + y
        sends, recvs = [], []
        for w in range(n):
            for j, (cx, cy) in enumerate(chips):
                for t in range(2):
                    sends.append((ins[w].at[c], 8 * w + 2 * j + t))
                    recvs.append((lands[w].at[t, 2 * cx + cy], 8 * w + 2 * j + t))
            for half in range(2):
                sends.append((ins[w].at[half], 8 * w + 6 + half))
                recvs.append((lands[w].at[half, me], 8 * w + 6 + half))
        return sends, recvs

    return plan, waits


def scatter_plan(n):
    def plan(ins, lands):
        x, y, c, chips = _place()
        return [(ins[w].at[:, 2 * cx + cy], lands[w].at[j], 3 * w + j, 3 * w + j, (cx, cy, c))
                for w in range(n) for j, (cx, cy) in enumerate(chips)]

    def waits(ins, lands):
        x, y, c, chips = _place()
        sends = [(ins[w].at[:, 2 * cx + cy], 3 * w + j) for w in range(n) for j, (cx, cy) in enumerate(chips)]
        recvs = [(lands[w].at[j], 3 * w + j) for w in range(n) for j in range(3)]
        return sends, recvs

    return plan, waits


def _layer_shards(w, l):
    return [w["ffn_gate"][l].astype(MMT), w["ffn_up"][l].astype(MMT), w["ffn_down"][l].astype(MMT),
            w["w_in"][l].reshape(2, D_MODEL // 2, -1).astype(MMT),
            w["branch_proj"][l].reshape(2, 3 * BW // 2, -1).astype(MMT),
            w["w_out"][l].reshape(2, -1, D_MODEL).astype(MMT),
            w["s5_glu_w"][l].reshape(2, -1, BW).astype(MMT)]


def _layer_weights(g):
    rows = lambda a: a.transpose(1, 0, 2, 3).reshape(NSH, -1, a.shape[-1])
    p = rows(g[4]).reshape(NSH, 3, BW, -1).transpose(1, 2, 0, 3).reshape(3, BW, D_MODEL)
    return dict(wg=g[0], wu=g[1], wd=g[2], win=rows(g[3]), pfull=p,
                woutfull=rows(g[5]).reshape(D_MODEL, D_MODEL), gluw=rows(g[6]).reshape(BW, BW))


GROUPS = {"ffn1": ("ffn_gate", "ffn_up", "ffn_down"), "merge": ("branch_proj", "w_out"), "mid": ("s5_glu_w",),
          "pre": ("w_in",), "ffn0": ("ffn_gate", "ffn_up", "ffn_down")}


def _grad_views(big, l, group):
    views = []
    for name in GROUPS[group]:
        if name == "branch_proj":
            dq = D_MODEL // NSH
            a = big[(name, l)].reshape(3, BW, NSH, dq).transpose(2, 0, 1, 3).reshape(1, NSH, 3 * BW, dq)
        elif name.startswith("ffn"):
            a = big[(name, l, 1 if group == "ffn1" else 0)]
        else:
            a = big[(name, l)]
            a = a.reshape(1, NSH, -1, a.shape[-1])
        views.append((name, a, 0))
    return views


def halves_plan(n):
    def src(ref, c):
        h = ref.shape[2] // 2
        return ref.at[:, :, pl.ds((1 - c) * h, h)]

    def plan(ins, lands):
        x, y, c, _chips = _place()
        return [(src(ins[w], c), lands[w], w, w, (x, y, 1 - c)) for w in range(n)]

    def waits(ins, lands):
        x, y, c, _chips = _place()
        return [(src(ins[w], c), w) for w in range(n)], [(lands[w], w) for w in range(n)]

    return plan, waits


def _reduce_to_halves(tag, views, c, wire):
    from_sibling = exchange_halves(f"reduce_cores_{tag}", [a for _, a, _ in views], [(p0, 1) for _, _, p0 in views])
    merge = lambda a: a.reshape((-1,) + a.shape[2:])
    return [add_own_half(f"sum_cores_{tag}_{i}", merge(a), merge(r), c, wire[i], NSH * p0).reshape(r.shape)
            for i, ((_, a, p0), r) in enumerate(zip(views, from_sibling))]


def _step(x, target, w, m, v):
    mx, my, mc = lax.axis_index("x"), lax.axis_index("y"), lax.axis_index("c")
    me = (2 * mx + my).astype(jnp.int32)
    mc = mc.astype(jnp.int32)

    W = dict(L=[None] * DEPTH)
    state = {"pending": []}
    n_big = len(BIG)
    g_plan, g_waits = gather_plan(n_big)

    def layer_weights(l, h):
        if l == 0:
            got = gather_shards("gather_weights_0", _layer_shards(w, 0) + [w[n] for n in SHARDED_SMALL])
            nxt = _layer_shards(w, 1)
            got, nxt = lax.optimization_barrier((got, nxt))
            shapes = [jax.ShapeDtypeStruct((2, NSH) + a.shape[1:], a.dtype) for a in nxt]
            state["gather"], token = split_start("gather_weights_1_start", nxt, shapes, g_plan, 8 * n_big, 8 * n_big)
            W["nw"] = got[n_big].transpose(0, 2, 1, 3).reshape(DEPTH, 3, 1, D_MODEL) + token[0, 0]
            W["convw"] = got[n_big + 1].transpose(0, 2, 1, 3).reshape(DEPTH, CONV_W, BW)
            return _layer_weights(got[:n_big]), h
        return _layer_weights(split_wait("gather_weights_1_wait", state["gather"], n_big, g_waits, h)[1]), h

    def to_chips(after):
        if "cores" not in state:
            return
        tag, names, l, group, handles, waits = state.pop("cores")
        sent, landed = split_wait(f"reduce_cores_{tag}_wait", handles, len(names), waits, after)
        merge = lambda a: a.reshape((-1,) + a.shape[2:])
        halves = [add_own_half(f"sum_cores_{tag}_{i}", merge(a), merge(r), mc, jnp.bfloat16, 0).reshape(r.shape)
                  for i, (a, r) in enumerate(zip(sent, landed))]
        shapes = [jax.ShapeDtypeStruct((3, a.shape[0]) + a.shape[2:], a.dtype) for a in halves]
        plan, waits = scatter_plan(len(halves))
        handles, token = split_start(f"reduce_chips_{tag}_start", halves, shapes, plan, 3 * len(halves), 3 * len(halves))
        W["nw"] = W["nw"] + token[0, 0]
        state["pending"].append((tag, names, l, group, handles, waits))

    def layer_grads(l, group, big):
        views = _grad_views(big, l, group)
        to_chips(views[0][1])
        if (l, group) == (0, "ffn0"):
            return
        tag = f"{l}_{group}"
        arrays = [a for _, a, _ in views]
        shapes = [jax.ShapeDtypeStruct((1, NSH, a.shape[2] // 2, a.shape[3]), a.dtype) for a in arrays]
        plan, waits = halves_plan(len(arrays))
        handles, token = split_start(f"reduce_cores_{tag}_start", arrays, shapes, plan, len(arrays), len(arrays))
        W["nw"] = W["nw"] + token[0, 0]
        state["cores"] = (tag, [name for name, _, _ in views], l, group, handles, waits)

    loss, dx, big, small = local_step(x[0], target[0], W, {k: w[k] for k in SMALL_RAW}, layer_weights, layer_grads)

    pieces = {n: {} for n in BIG}
    block_of = lambda name, l, group: (2 * l + (group == "ffn1")) if name.startswith("ffn") else l
    views = _grad_views(big, 0, "ffn0")
    small_packed = pack_small("pack_small_grads", [small[n] for n in SMALL], NSH * 32)
    halves = _reduce_to_halves("0_ffn0", views + [("small", small_packed.reshape(1, NSH, -1, LANE), 0)], mc,
                               [jnp.bfloat16] * len(views) + [F32])
    shapes = [jax.ShapeDtypeStruct((3, a.shape[0]) + a.shape[2:], a.dtype) for a in halves]
    plan, waits = scatter_plan(len(halves))
    last_handles, token = split_start("reduce_chips_0_ffn0_start", halves, shapes, plan, 3 * len(halves), 3 * len(halves))
    mc = mc + token[0, 0].astype(jnp.int32)
    for tag, names, l, group, handles, waits_k in state["pending"]:
        sent, landed = split_wait(f"reduce_chips_{tag}_wait", handles, len(names), waits_k, dx)
        for i, (name, h, r) in enumerate(zip(names, sent, landed)):
            pieces[name][block_of(name, l, group)] = add_chips(f"sum_chips_{tag}_{i}", h, r, me)

    g, delta, new_m, new_v = {}, {}, {}, {}

    def update(tag, names, extra):
        own = [jnp.concatenate([pieces[n][b] for b in sorted(pieces[n])], axis=0) for n in names] + extra
        other = share_halves(f"reduce_share_{tag}", own)
        for i, n in enumerate(names):
            view = lambda a: a.reshape(own[i].shape[0], -1, own[i].shape[2])
            res = adamw_halves(f"adamw_{n}", view(w[n]), view(m[n]), view(v[n]), own[i], other[i], mc)
            g[n], delta[n], new_m[n], new_v[n] = [a.reshape(w[n].shape) for a in res]
        return own, other

    early = [n for n in BIG if not n.startswith("ffn")]
    update("early", early, [])
    sent, landed = split_wait("reduce_chips_0_ffn0_wait", last_handles, len(halves), waits, new_v[early[0]])
    last = [add_chips(f"sum_chips_0_ffn0_{i}", h, r, me) for i, (h, r) in enumerate(zip(sent, landed))]
    for (name, _, _), piece in zip(views, last):
        pieces[name][block_of(name, 0, "ffn0")] = piece
    own, other = update("last", [n for n in BIG if n.startswith("ffn")], [last[-1]])

    piece = jnp.stack([jnp.where(mc == 0, own[-1][0], other[-1][0]), jnp.where(mc == 0, other[-1][0], own[-1][0])])
    (all_small,) = gather_shards("gather_small", [piece])
    full_small = unpack_small("unpack_small_grads", all_small.transpose(1, 0, 2, 3).reshape(-1, LANE),
                              [small[n].shape for n in SMALL])
    g.update(zip(SMALL, full_small))
    g["norm_w"] = lax.dynamic_slice_in_dim(g["norm_w"], me * (D_MODEL // NSH), D_MODEL // NSH, axis=2)
    g["rg_conv_w"] = lax.dynamic_slice_in_dim(g["rg_conv_w"], me * (BW // NSH), BW // NSH, axis=2)

    packed = [pack_small(f"pack_small_{tag}", [src[n] for n in SMALL], 8)
              for tag, src in (("w", w), ("g", g), ("m", m), ("v", v))]
    for tag, dst, flat in zip(("delta", "m", "v"), (delta, new_m, new_v), adamw(*packed)):
        dst.update(zip(SMALL, unpack_small(f"unpack_small_{tag}", flat, [w[n].shape for n in SMALL])))

    total = lax.psum(loss[0, 0], ("x", "y", "c"))
    return (total, dx[None], *[g[n] for n in WEIGHTS], *[delta[n] for n in WEIGHTS],
            *[new_m[n] for n in WEIGHTS], *[new_v[n] for n in WEIGHTS])


def kernel(x, norm_w, final_norm_w, ffn_gate, ffn_up, ffn_down, w_in, branch_proj, w_out, s5_lambda_re, s5_lambda_im, s5_log_dt, s5_b_re, s5_b_im, s5_c_re, s5_c_im, s5_d, s5_glu_w, s5_glu_b, hg_lb_logits, hg_norm_w, rg_conv_w, rg_conv_b, rg_wa, rg_ba, rg_wx, rg_bx, rg_lambda, loss_target, m_norm_w, m_final_norm_w, m_ffn_gate, m_ffn_up, m_ffn_down, m_w_in, m_branch_proj, m_w_out, m_s5_lambda_re, m_s5_lambda_im, m_s5_log_dt, m_s5_b_re, m_s5_b_im, m_s5_c_re, m_s5_c_im, m_s5_d, m_s5_glu_w, m_s5_glu_b, m_hg_lb_logits, m_hg_norm_w, m_rg_conv_w, m_rg_conv_b, m_rg_wa, m_rg_ba, m_rg_wx, m_rg_bx, m_rg_lambda, v_norm_w, v_final_norm_w, v_ffn_gate, v_ffn_up, v_ffn_down, v_w_in, v_branch_proj, v_w_out, v_s5_lambda_re, v_s5_lambda_im, v_s5_log_dt, v_s5_b_re, v_s5_b_im, v_s5_c_re, v_s5_c_im, v_s5_d, v_s5_glu_w, v_s5_glu_b, v_hg_lb_logits, v_hg_norm_w, v_rg_conv_w, v_rg_conv_b, v_rg_wa, v_rg_ba, v_rg_wx, v_rg_bx, v_rg_lambda):
    ws = (norm_w, final_norm_w, ffn_gate, ffn_up, ffn_down, w_in, branch_proj, w_out, s5_lambda_re, s5_lambda_im, s5_log_dt, s5_b_re, s5_b_im, s5_c_re, s5_c_im, s5_d, s5_glu_w, s5_glu_b, hg_lb_logits, hg_norm_w, rg_conv_w, rg_conv_b, rg_wa, rg_ba, rg_wx, rg_bx, rg_lambda)
    ms = (m_norm_w, m_final_norm_w, m_ffn_gate, m_ffn_up, m_ffn_down, m_w_in, m_branch_proj, m_w_out, m_s5_lambda_re, m_s5_lambda_im, m_s5_log_dt, m_s5_b_re, m_s5_b_im, m_s5_c_re, m_s5_c_im, m_s5_d, m_s5_glu_w, m_s5_glu_b, m_hg_lb_logits, m_hg_norm_w, m_rg_conv_w, m_rg_conv_b, m_rg_wa, m_rg_ba, m_rg_wx, m_rg_bx, m_rg_lambda)
    vs = (v_norm_w, v_final_norm_w, v_ffn_gate, v_ffn_up, v_ffn_down, v_w_in, v_branch_proj, v_w_out, v_s5_lambda_re, v_s5_lambda_im, v_s5_log_dt, v_s5_b_re, v_s5_b_im, v_s5_c_re, v_s5_c_im, v_s5_d, v_s5_glu_w, v_s5_glu_b, v_hg_lb_logits, v_hg_norm_w, v_rg_conv_w, v_rg_conv_b, v_rg_wa, v_rg_ba, v_rg_wx, v_rg_bx, v_rg_lambda)
    return _step(x, loss_target, dict(zip(WEIGHTS, ws)), dict(zip(WEIGHTS, ms)), dict(zip(WEIGHTS, vs)))
```

```python
import functools
import math
from typing import NamedTuple

import jax
import jax.numpy as jnp
from jax import lax
from jax.experimental import pallas as pl
from jax.experimental.pallas import tpu as pltpu

F32 = jnp.float32
MMT = jnp.bfloat16
HI = lax.Precision.HIGHEST

D_MODEL = 1024
BW = 512
S5_GROUP, S5_GROUPS, S5_STATE = 16, 32, 64
S5_N = S5_GROUPS * S5_STATE
HG_HEADS, HG_D = 4, 128
HG_CHUNK = 128
RG_BLOCKS, RG_BLOCK = 8, 64
RG_C = 8.0
CONV_W = 4
D_FF = 2816
EPS = 1e-6
IN_TOTAL = 6656
NSH = 4
NSEG = 8
LANE = 128
VMEM_LIMIT = 56 * 1024 * 1024
TM_FWD = 512
TM_WGRAD = 512

ADAM_LR, ADAM_B1, ADAM_B2, ADAM_EPS, ADAM_WD, ADAM_STEP = 0.001, 0.9, 0.999, 1e-08, 0.01, 10

MESH = pl.DeviceIdType.MESH


class WP(NamedTuple):
    w: jax.Array
    p: jax.Array


def _dg(a, b, ca, cb):
    return lax.dot_general(a, b, (((ca,), (cb,)), ((), ())), preferred_element_type=F32)


@jax.custom_vjp
def _mmw(a, w, p):
    return _dg(a.astype(MMT), w, 1, 0)


def _mmw_fwd(a, w, p):
    return _mmw(a, w, p), (a, w)


def _mmw_bwd(res, g):
    a, w = res
    gb = g.astype(MMT)
    return _dg(gb, w, 1, 1), jnp.zeros_like(w), _dg(a.astype(MMT), gb, 0, 0)


_mmw.defvjp(_mmw_fwd, _mmw_bwd)


def mm(a, w):
    if isinstance(w, WP):
        return _mmw(a, w.w, w.p)
    return _dg(a.astype(MMT), w, 1, 0)


@jax.custom_vjp
def mma_nn(a, b):
    return _dg(a.astype(MMT), b.astype(MMT), 1, 0)


def _nn_f(a, b):
    return mma_nn(a, b), (a, b)


def _nn_b(res, g):
    a, b = res
    gb = g.astype(MMT)
    return _dg(gb, b.astype(MMT), 1, 1), _dg(a.astype(MMT), gb, 0, 0)


mma_nn.defvjp(_nn_f, _nn_b)


@jax.custom_vjp
def mma_nt(a, b):
    return _dg(a.astype(MMT), b.astype(MMT), 1, 1)


def _nt_f(a, b):
    return mma_nt(a, b), (a, b)


def _nt_b(res, g):
    a, b = res
    gb = g.astype(MMT)
    return _dg(gb, b.astype(MMT), 1, 0), _dg(gb, a.astype(MMT), 0, 0)


mma_nt.defvjp(_nt_f, _nt_b)


@jax.custom_vjp
def mma_tn(a, b):
    return _dg(a.astype(MMT), b.astype(MMT), 0, 0)


def _tn_f(a, b):
    return mma_tn(a, b), (a, b)


def _tn_b(res, g):
    a, b = res
    gb = g.astype(MMT)
    return _dg(b.astype(MMT), gb, 1, 1), _dg(a.astype(MMT), gb, 1, 0)


mma_tn.defvjp(_tn_f, _tn_b)


def mm_exact(m, x):
    return jnp.dot(m, x, precision=HI, preferred_element_type=F32)


def _rms(x, w):
    return x * lax.rsqrt(jnp.mean(x * x, axis=-1, keepdims=True) + EPS) * w


def _expm1(x):
    series = x * (1.0 + x * (1.0 / 2) * (1.0 + x * (1.0 / 3) * (1.0 + x * (1.0 / 4) * (1.0 + x * (1.0 / 5) * (1.0 + x * (1.0 / 6))))))
    return jnp.where(jnp.abs(x) < 0.1, series, jnp.exp(x) - 1.0)


def _bspec(block, fn, order):
    if order == "is":
        return pl.BlockSpec(block, lambda i, s: fn(s, i))
    return pl.BlockSpec(block, lambda s, i: fn(s, i))


def tile_fwd(fn, name, n_i, n_s, ins, outs, s_outer=False):
    n_in = len(ins)
    order = "si" if s_outer else "is"
    assert not (s_outer and any(o[4] for o in outs))

    def body(*refs):
        s = pl.program_id(0 if s_outer else 1)
        res = fn(*[r[...] for r in refs[:n_in]], s)
        for o_ref, val, spec in zip(refs[n_in:], res, outs):
            if spec[4] and n_s > 1:
                @pl.when(s == 0)
                def _(o_ref=o_ref, val=val):
                    o_ref[...] = val.astype(o_ref.dtype)

                @pl.when(s != 0)
                def _(o_ref=o_ref, val=val):
                    o_ref[...] += val.astype(o_ref.dtype)
            else:
                o_ref[...] = val.astype(o_ref.dtype)

    return pl.pallas_call(
        body, grid=(n_s, n_i) if s_outer else (n_i, n_s), name=name,
        in_specs=[_bspec(b, f, order) for _, b, f in ins],
        out_specs=[_bspec(b, f, order) for _, _, b, f, _ in outs],
        out_shape=[jax.ShapeDtypeStruct(sh, dt) for sh, dt, _, _, _ in outs],
        compiler_params=pltpu.CompilerParams(vmem_limit_bytes=VMEM_LIMIT,
                                             dimension_semantics=("arbitrary", "arbitrary")),
    )(*[a for a, _, _ in ins])


def tile_bwd(fn, name, n_i, n_s, ins, cts, gouts):
    groups = [c if isinstance(c, list) else [c] for c in cts]
    cts = [blk for grp in groups for blk in grp]
    n_in, n_ct = len(ins), len(cts)
    kinds = [k for _, _, _, k in ins]
    d_pos = [j for j, k in enumerate(kinds) if k != "c"]
    shared = [(gi, spec[4]) for gi, spec in enumerate(gouts) if len(spec) == 5 and spec[4] is not None]
    n_sh = len(shared)

    def body(*refs):
        s, i = pl.program_id(0), pl.program_id(1)
        vals = [r[...] for r in refs[:n_in]]
        ct_refs, ctv = list(refs[n_in:n_in + n_ct]), []
        for grp in groups:
            parts = [ct_refs.pop(0)[...] for _ in grp]
            ctv.append(parts[0] if len(parts) == 1 else jnp.concatenate(parts, axis=1))
        ctv = tuple(ctv)
        g_refs = refs[n_in + n_ct + n_sh:]

        def g(*dv):
            args = list(vals)
            for j, v in zip(d_pos, dv):
                args[j] = WP(vals[j], v) if kinds[j] == "w" else v
            return tuple(fn(*args))

        dv0 = [jnp.zeros(vals[j].shape, F32) if kinds[j] == "w" else vals[j] for j in d_pos]
        _, vjp = jax.vjp(g, *dv0)
        grads = vjp(ctv)
        for g_ref, gv, spec in zip(g_refs, grads, gouts):
            mode = spec[3]
            if mode == "write":
                g_ref[...] = gv.astype(g_ref.dtype)
            else:
                first = (i == 0) if mode == "acc_i" else jnp.logical_and(i == 0, s == 0)

                @pl.when(first)
                def _(g_ref=g_ref, gv=gv):
                    g_ref[...] = gv.astype(g_ref.dtype)

                @pl.when(jnp.logical_not(first))
                def _(g_ref=g_ref, gv=gv):
                    g_ref[...] += gv.astype(g_ref.dtype)

    return pl.pallas_call(
        body, grid=(n_s, n_i), name=name,
        in_specs=([_bspec(b, f, "si") for _, b, f, _ in ins] + [_bspec(b, f, "si") for _, b, f in cts]
                  + [pl.BlockSpec(memory_space=pl.ANY)] * n_sh),
        out_specs=[_bspec(spec[1], spec[2], "si") for spec in gouts],
        out_shape=[jax.ShapeDtypeStruct(spec[0], F32) for spec in gouts],
        input_output_aliases={n_in + n_ct + k: gi for k, (gi, _) in enumerate(shared)},
        compiler_params=pltpu.CompilerParams(vmem_limit_bytes=VMEM_LIMIT,
                                             dimension_semantics=("arbitrary", "arbitrary")),
    )(*[a for a, _, _, _ in ins], *[a for a, _, _ in cts], *[buf for _, buf in shared])


def _row_tile(rows, width, itemsize=4, budget=2 * 1024 * 1024, mult=8):
    best = mult
    for t in range(mult, rows + 1, mult):
        if rows % t == 0 and t * width * itemsize <= budget:
            best = t
    return best


def add_n(name, terms, shape):
    rows, cols = shape
    tr = _row_tile(rows, cols)

    def body(*refs):
        acc = refs[0][...]
        for r in refs[1:-1]:
            acc = acc + r[...]
        refs[-1][...] = acc

    specs = []
    for _, lead in terms:
        specs.append(pl.BlockSpec((None,) * len(lead) + (tr, cols), functools.partial(lambda i, lead: (*lead, i, 0), lead=lead)))
    return pl.pallas_call(
        body, grid=(rows // tr,), name=name, in_specs=specs,
        out_specs=pl.BlockSpec((tr, cols), lambda i: (i, 0)),
        out_shape=jax.ShapeDtypeStruct((rows, cols), F32),
    )(*[a for a, _ in terms])


def ffn_core(x, nw, wg, wu, wd):
    h = _rms(x, nw)
    return (0.5 * mm(jax.nn.silu(mm(h, wg)) * mm(h, wu), wd),)


def pre_core(x, nw, win):
    return (mm(_rms(x, nw), win),)


def _split_lanes(y):
    return jnp.stack([y[:, k * LANE:(k + 1) * LANE] for k in range(y.shape[1] // LANE)], axis=0)


def _join_lanes(y3):
    return jnp.concatenate([y3[k] for k in range(y3.shape[0])], axis=1)


def s5_pre_core(u, b_re0, b_re1, b_im0, b_im1):
    u0, u1 = u[:, :BW // 2], u[:, BW // 2:]
    re = jnp.concatenate([mm(u0, b_re0), mm(u1, b_re1)], axis=1)
    im = jnp.concatenate([mm(u0, b_im0), mm(u1, b_im1)], axis=1)
    return _split_lanes(re), _split_lanes(im)


def mid_core(xr, xi, u, o, g, hs, gc, hmat, c0, c1, d, gluw, glub, hgw):
    half = xr.shape[0] // 2
    xs0 = jnp.concatenate([_join_lanes(xr[:half]), _join_lanes(xi[:half])], axis=1)
    xs1 = jnp.concatenate([_join_lanes(xr[half:]), _join_lanes(xi[half:])], axis=1)
    y = jnp.concatenate([mm(xs0, c0), mm(xs1, c1)], axis=1) + d * u
    z = jax.nn.gelu(y)
    ya = z * jax.nn.sigmoid(mm(z, gluw) + glub)
    ms = mm_exact(o * o, hmat)
    yb = o * lax.rsqrt(ms + EPS) * hgw * jax.nn.silu(g)
    yc = hs * jax.nn.gelu(gc)
    return ya, yb, yc


def _sub(w, n):
    return WP(w.w[n], w.p[n]) if isinstance(w, WP) else w[n]


def merge_core(ya, yb, yc, g0, g1, g2, g3, g4, g5, p, wout):
    gate = lambda a, b: jax.nn.sigmoid(jnp.concatenate([a, b], axis=1))
    m = gate(g0, g1) * mm(ya, _sub(p, 0)) + gate(g2, g3) * mm(yb, _sub(p, 1)) + gate(g4, g5) * mm(yc, _sub(p, 2))
    return (mm(m, wout),)


def gates_core(xc, wa, ba, wx, bx, lam):
    r = jax.nn.sigmoid(mm(xc, wa) + ba)
    i = jax.nn.sigmoid(mm(xc, wx) + bx)
    log_a = -RG_C * jax.nn.softplus(-lam) * r
    a = jnp.exp(log_a)
    b = jnp.sqrt(-_expm1(2.0 * log_a)) * (i * xc)
    return a, b


def _seg_rows(ref, k, j, n):
    rows = pl.ds(pl.multiple_of(j * NSEG, NSEG), NSEG)
    if k is None:
        return ref[rows, :]
    return ref[k, rows, :]


def _seg_store(ref, k, j, n, val):
    rows = pl.ds(pl.multiple_of(j * NSEG, NSEG), NSEG)
    if k is None:
        ref[rows, :] = val
    else:
        ref[k, rows, :] = val


def _seg_carries(er, ei, pr, pi, reverse):
    rows = lax.broadcasted_iota(jnp.int32, er.shape, 0)
    cr = jnp.zeros_like(er)
    ci = None if ei is None else jnp.zeros_like(er)
    order = range(NSEG - 2, -1, -1) if reverse else range(1, NSEG)
    shift = NSEG - 1 if reverse else 1
    for s in order:
        if ei is None:
            tr = er + pr * cr
            cr = jnp.where(rows == s, pltpu.roll(tr, shift, 0), cr)
        else:
            tr = er + pr * cr - pi * ci
            ti = ei + pr * ci + pi * cr
            cr = jnp.where(rows == s, pltpu.roll(tr, shift, 0), cr)
            ci = jnp.where(rows == s, pltpu.roll(ti, shift, 0), ci)
    return cr, ci


S5_K = 2


def s5_scan_fwd(bur, bui, ar, ai, L):
    n = L // NSEG
    nb = S5_N // LANE
    K = S5_K

    def body(br_ref, bi_ref, ar_ref, ai_ref, xr_ref, xi_ref):
        zero = jnp.zeros((NSEG, LANE), F32)
        A = [(jnp.broadcast_to(ar_ref[k], (NSEG, LANE)), jnp.broadcast_to(ai_ref[k], (NSEG, LANE))) for k in range(K)]

        def p1(j, st):
            new = []
            for k in range(K):
                sr, si, pr, pi = st[k]
                a_r, a_i = A[k]
                nr = a_r * sr - a_i * si + _seg_rows(br_ref, k, j, n)
                ni = a_r * si + a_i * sr + _seg_rows(bi_ref, k, j, n)
                _seg_store(xr_ref, k, j, n, nr)
                _seg_store(xi_ref, k, j, n, ni)
                new.append((nr, ni, a_r * pr - a_i * pi, a_r * pi + a_i * pr))
            return tuple(new)

        st = lax.fori_loop(0, n, p1, tuple((zero, zero, zero + 1.0, zero) for _ in range(K)))
        C = [_seg_carries(st[k][0], st[k][1], st[k][2], st[k][3], False) for k in range(K)]

        def p2(j, st):
            new = []
            for k in range(K):
                pr, pi = st[k]
                a_r, a_i = A[k]
                pr, pi = a_r * pr - a_i * pi, a_r * pi + a_i * pr
                cr, ci = C[k]
                _seg_store(xr_ref, k, j, n, _seg_rows(xr_ref, k, j, n) + pr * cr - pi * ci)
                _seg_store(xi_ref, k, j, n, _seg_rows(xi_ref, k, j, n) + pr * ci + pi * cr)
                new.append((pr, pi))
            return tuple(new)

        lax.fori_loop(0, n, p2, tuple((zero + 1.0, zero) for _ in range(K)))

    blk = pl.BlockSpec((K, L, LANE), lambda g: (g, 0, 0))
    ablk = pl.BlockSpec((K, 1, LANE), lambda g: (g, 0, 0))
    return pl.pallas_call(
        body, grid=(nb // K,), name="s5_scan_fwd",
        in_specs=[blk, blk, ablk, ablk], out_specs=[blk, blk],
        out_shape=[jax.ShapeDtypeStruct((nb, L, LANE), F32)] * 2,
        compiler_params=pltpu.CompilerParams(vmem_limit_bytes=VMEM_LIMIT),
    )(bur, bui, ar, ai)


def s5_scan_bwd(dxr, dxi, xr, xi, ar, ai, L):
    n = L // NSEG
    nb = S5_N // LANE
    K = S5_K

    def body(dr_ref, di_ref, xr_ref, xi_ref, ar_ref, ai_ref, gr_ref, gi_ref, dar_ref, dai_ref):
        zero = jnp.zeros((NSEG, LANE), F32)
        rows = lax.broadcasted_iota(jnp.int32, (NSEG, LANE), 0)
        A = [(jnp.broadcast_to(ar_ref[k], (NSEG, LANE)), -jnp.broadcast_to(ai_ref[k], (NSEG, LANE))) for k in range(K)]

        def p1(jj, st):
            j = n - 1 - jj
            new = []
            for k in range(K):
                sr, si, pr, pi = st[k]
                a_r, a_i = A[k]
                nr = a_r * sr - a_i * si + _seg_rows(dr_ref, k, j, n)
                ni = a_r * si + a_i * sr + _seg_rows(di_ref, k, j, n)
                _seg_store(gr_ref, k, j, n, nr)
                _seg_store(gi_ref, k, j, n, ni)
                new.append((nr, ni, a_r * pr - a_i * pi, a_r * pi + a_i * pr))
            return tuple(new)

        st = lax.fori_loop(0, n, p1, tuple((zero, zero, zero + 1.0, zero) for _ in range(K)))
        C = [_seg_carries(st[k][0], st[k][1], st[k][2], st[k][3], True) for k in range(K)]
        xb = [(jnp.where(rows == 0, 0.0, pltpu.roll(_seg_rows(xr_ref, k, n - 1, n), 1, 0)),
               jnp.where(rows == 0, 0.0, pltpu.roll(_seg_rows(xi_ref, k, n - 1, n), 1, 0))) for k in range(K)]

        def p2(jj, st):
            j = n - 1 - jj
            jp = jnp.maximum(j - 1, 0)
            new = []
            for k in range(K):
                pr, pi, acr, aci = st[k]
                a_r, a_i = A[k]
                pr, pi = a_r * pr - a_i * pi, a_r * pi + a_i * pr
                cr, ci = C[k]
                g_r = _seg_rows(gr_ref, k, j, n) + pr * cr - pi * ci
                g_i = _seg_rows(gi_ref, k, j, n) + pr * ci + pi * cr
                _seg_store(gr_ref, k, j, n, g_r)
                _seg_store(gi_ref, k, j, n, g_i)
                xpr = jnp.where(j == 0, xb[k][0], _seg_rows(xr_ref, k, jp, n))
                xpi = jnp.where(j == 0, xb[k][1], _seg_rows(xi_ref, k, jp, n))
                new.append((pr, pi, acr + g_r * xpr + g_i * xpi, aci + g_i * xpr - g_r * xpi))
            return tuple(new)

        st = lax.fori_loop(0, n, p2, tuple((zero + 1.0, zero, zero, zero) for _ in range(K)))
        for k in range(K):
            dar_ref[k] = jnp.sum(st[k][2], axis=0, keepdims=True)
            dai_ref[k] = jnp.sum(st[k][3], axis=0, keepdims=True)

    blk = pl.BlockSpec((K, L, LANE), lambda g: (g, 0, 0))
    ablk = pl.BlockSpec((K, 1, LANE), lambda g: (g, 0, 0))
    return pl.pallas_call(
        body, grid=(nb // K,), name="s5_scan_bwd",
        in_specs=[blk, blk, blk, blk, ablk, ablk], out_specs=[blk, blk, ablk, ablk],
        out_shape=[jax.ShapeDtypeStruct((nb, L, LANE), F32)] * 2 + [jax.ShapeDtypeStruct((nb, 1, LANE), F32)] * 2,
        compiler_params=pltpu.CompilerParams(vmem_limit_bytes=VMEM_LIMIT),
    )(dxr, dxi, xr, xi, ar, ai)


def rg_scan_fwd(a, b, L):
    n = L // NSEG

    def body(a_ref, b_ref, h_ref):
        zero = jnp.zeros((NSEG, LANE), F32)

        def p1(j, st):
            h, p = st
            aj = _seg_rows(a_ref, None, j, n)
            h = aj * h + _seg_rows(b_ref, None, j, n)
            _seg_store(h_ref, None, j, n, h)
            return h, aj * p

        e, pe = lax.fori_loop(0, n, p1, (zero, zero + 1.0))
        c, _ = _seg_carries(e, None, pe, None, False)

        def p2(j, p):
            p = _seg_rows(a_ref, None, j, n) * p
            _seg_store(h_ref, None, j, n, _seg_rows(h_ref, None, j, n) + p * c)
            return p

        lax.fori_loop(0, n, p2, zero + 1.0)

    blk = pl.BlockSpec((L, LANE), lambda g: (0, g))
    return pl.pallas_call(
        body, grid=(BW // LANE,), name="rg_scan_fwd", in_specs=[blk, blk], out_specs=blk,
        out_shape=jax.ShapeDtypeStruct((L, BW), F32),
        compiler_params=pltpu.CompilerParams(vmem_limit_bytes=VMEM_LIMIT),
    )(a, b)


def rg_scan_bwd(a, h, dh, L):
    n = L // NSEG

    def body(a_ref, h_ref, dh_ref, da_ref, db_ref):
        zero = jnp.zeros((NSEG, LANE), F32)
        rows = lax.broadcasted_iota(jnp.int32, (NSEG, LANE), 0)
        a_edge = jnp.where(rows == NSEG - 1, 0.0, pltpu.roll(_seg_rows(a_ref, None, 0, n), NSEG - 1, 0))
        h_edge = jnp.where(rows == 0, 0.0, pltpu.roll(_seg_rows(h_ref, None, n - 1, n), 1, 0))

        def mult(j):
            return jnp.where(j == n - 1, a_edge, _seg_rows(a_ref, None, jnp.minimum(j + 1, n - 1), n))

        def p1(jj, st):
            j = n - 1 - jj
            g, p = st
            m = mult(j)
            g = m * g + _seg_rows(dh_ref, None, j, n)
            _seg_store(db_ref, None, j, n, g)
            return g, m * p

        e, pe = lax.fori_loop(0, n, p1, (zero, zero + 1.0))
        c, _ = _seg_carries(e, None, pe, None, True)

        def p2(jj, p):
            j = n - 1 - jj
            p = mult(j) * p
            g = _seg_rows(db_ref, None, j, n) + p * c
            _seg_store(db_ref, None, j, n, g)
            hp = jnp.where(j == 0, h_edge, _seg_rows(h_ref, None, jnp.maximum(j - 1, 0), n))
            _seg_store(da_ref, None, j, n, g * hp)
            return p

        lax.fori_loop(0, n, p2, zero + 1.0)

    blk = pl.BlockSpec((L, LANE), lambda g: (0, g))
    return pl.pallas_call(
        body, grid=(BW // LANE,), name="rg_scan_bwd", in_specs=[blk, blk, blk], out_specs=[blk, blk],
        out_shape=[jax.ShapeDtypeStruct((L, BW), F32)] * 2,
        compiler_params=pltpu.CompilerParams(vmem_limit_bytes=VMEM_LIMIT),
    )(a, h, dh)


def _hg_consts(C):
    t = lax.broadcasted_iota(jnp.int32, (C, C), 0)
    s = lax.broadcasted_iota(jnp.int32, (C, C), 1)
    tril = (s <= t).astype(F32)
    diag = (s == t).astype(F32)
    levels = []
    k = 1
    while (1 << k) <= C:
        m = 1 << (k - 1)
        same = (t >> k) == (s >> k)
        t_right = ((t >> (k - 1)) & 1) == 1
        s_left = ((s >> (k - 1)) & 1) == 0
        mask = jnp.logical_and(same, jnp.logical_and(t_right, s_left)).astype(F32)
        bnd = ((t >> k) << k) + (m - 1)
        levels.append((mask, (s <= bnd).astype(F32)))
        k += 1
    return tril, diag, levels


def hg_chunk(st, q, z, v, lb):
    C = q.shape[0]
    tril, diag, levels = _hg_consts(C)
    sig = jax.nn.sigmoid(z)
    lf = jnp.log(lb + (1.0 - lb) * sig)
    k = (1.0 - lb) * jax.nn.sigmoid(-z)
    qh = jax.nn.silu(q)
    b = mm_exact(tril, lf)
    blast = jnp.sum(lf, axis=0, keepdims=True)
    qe = qh * jnp.exp(b)
    kd = k * jnp.exp(blast - b)
    scaled = []
    for _, sel in levels:
        ref = mm_exact(sel, lf)
        scaled.append((qh * jnp.exp(jnp.minimum(b - ref, 0.0)), k * jnp.exp(jnp.minimum(ref - b, 0.0))))
    outs, news = [], []
    for h in range(HG_HEADS):
        sl = slice(h * HG_D, (h + 1) * HG_D)
        st_h = st[h * HG_D:(h + 1) * HG_D, :]
        sc = diag * mma_nt(qh[:, sl], k[:, sl])
        for (mask, _), (qt, kt) in zip(levels, scaled):
            sc = sc + mask * mma_nt(qt[:, sl], kt[:, sl])
        outs.append(mma_nt(qe[:, sl], st_h) + mma_nn(sc, v[:, sl]))
        news.append(st_h * jnp.exp(blast[:, sl]) + mma_tn(v[:, sl], kd[:, sl]))
    return jnp.concatenate(news, axis=0), jnp.concatenate(outs, axis=1)


def hg_fwd(qzv, lb, L):
    C = HG_CHUNK
    nc = L // C

    def body(q_ref, z_ref, v_ref, lb_ref, o_ref, sst_ref, st_ref):
        @pl.when(pl.program_id(0) == 0)
        def _():
            st_ref[...] = jnp.zeros_like(st_ref)

        st = st_ref[...]
        sst_ref[...] = st
        new, o = hg_chunk(st, q_ref[...], z_ref[...], v_ref[...], lb_ref[...])
        st_ref[...] = new
        o_ref[...] = o

    col = lambda cb: pl.BlockSpec((C, BW), functools.partial(lambda c, cb: (c, cb), cb=cb))
    return pl.pallas_call(
        body, grid=(nc,), name="hg_fwd",
        in_specs=[col(0), col(1), col(2), pl.BlockSpec((1, BW), lambda c: (0, 0))],
        out_specs=[pl.BlockSpec((C, BW), lambda c: (c, 0)), pl.BlockSpec((None, BW, HG_D), lambda c: (c, 0, 0))],
        out_shape=[jax.ShapeDtypeStruct((L, BW), F32), jax.ShapeDtypeStruct((nc, BW, HG_D), F32)],
        scratch_shapes=[pltpu.VMEM((BW, HG_D), F32)],
        compiler_params=pltpu.CompilerParams(vmem_limit_bytes=VMEM_LIMIT, dimension_semantics=("arbitrary",)),
    )(qzv, qzv, qzv, lb)


def hg_bwd(qzv, lb, sst, do, L):
    C = HG_CHUNK
    nc = L // C

    def body(q_ref, z_ref, v_ref, lb_ref, sst_ref, do_ref, dq_ref, dz_ref, dv_ref, dlb_ref, dst_ref):
        @pl.when(pl.program_id(0) == 0)
        def _():
            dst_ref[...] = jnp.zeros_like(dst_ref)
            dlb_ref[...] = jnp.zeros_like(dlb_ref)

        _, vjp = jax.vjp(hg_chunk, sst_ref[...], q_ref[...], z_ref[...], v_ref[...], lb_ref[...])
        dst, dq, dz, dv, dlb = vjp((dst_ref[...], do_ref[...]))
        dst_ref[...] = dst
        dq_ref[...] = dq
        dz_ref[...] = dz
        dv_ref[...] = dv
        dlb_ref[...] += dlb

    col = lambda cb: pl.BlockSpec((C, BW), functools.partial(lambda c, cb: (nc - 1 - c, cb), cb=cb))
    rev = pl.BlockSpec((C, BW), lambda c: (nc - 1 - c, 0))
    return pl.pallas_call(
        body, grid=(nc,), name="hg_bwd",
        in_specs=[col(0), col(1), col(2), pl.BlockSpec((1, BW), lambda c: (0, 0)),
                  pl.BlockSpec((None, BW, HG_D), lambda c: (nc - 1 - c, 0, 0)), rev],
        out_specs=[rev, rev, rev, pl.BlockSpec((1, BW), lambda c: (0, 0))],
        out_shape=[jax.ShapeDtypeStruct((L, BW), F32)] * 3 + [jax.ShapeDtypeStruct((1, BW), F32)],
        scratch_shapes=[pltpu.VMEM((BW, HG_D), F32)],
        compiler_params=pltpu.CompilerParams(vmem_limit_bytes=VMEM_LIMIT, dimension_semantics=("arbitrary",)),
    )(qzv, qzv, qzv, lb, sst, do)


def _shift_down(x, d, rows, L):
    if d == 0:
        return x
    wrapped = jnp.where((rows & (NSEG - 1)) == 0, 0.0, pltpu.roll(x, NSEG * d + 1, 0))
    return jnp.where(rows < NSEG * d, wrapped, pltpu.roll(x, NSEG * d, 0))


def _shift_up(x, d, rows, L):
    if d == 0:
        return x
    wrapped = jnp.where((rows & (NSEG - 1)) == NSEG - 1, 0.0, pltpu.roll(x, L - (NSEG * d + 1), 0))
    return jnp.where(rows >= L - NSEG * d, wrapped, pltpu.roll(x, L - NSEG * d, 0))


def conv_fwd(proj, w, b, L):
    def body(x_ref, w_ref, b_ref, o_ref):
        x = x_ref[...]
        rows = lax.broadcasted_iota(jnp.int32, x.shape, 0)
        acc = jnp.broadcast_to(b_ref[...], x.shape)
        for k in range(CONV_W):
            acc = acc + w_ref[pl.ds(k, 1), :] * _shift_down(x, CONV_W - 1 - k, rows, L)
        o_ref[...] = acc

    nl = BW // LANE
    return pl.pallas_call(
        body, grid=(nl,), name="conv_fwd",
        in_specs=[pl.BlockSpec((L, LANE), lambda g: (0, 5 * nl + g)), pl.BlockSpec((CONV_W, LANE), lambda g: (0, g)),
                  pl.BlockSpec((1, LANE), lambda g: (0, g))],
        out_specs=pl.BlockSpec((L, LANE), lambda g: (0, g)),
        out_shape=jax.ShapeDtypeStruct((L, BW), F32),
        compiler_params=pltpu.CompilerParams(vmem_limit_bytes=VMEM_LIMIT),
    )(proj, w, b)


def conv_bwd(proj, w, dxc, L):
    def body(x_ref, w_ref, d_ref, dx_ref, dw_ref, db_ref):
        x, d = x_ref[...], d_ref[...]
        rows = lax.broadcasted_iota(jnp.int32, x.shape, 0)
        acc = jnp.zeros_like(x)
        for k in range(CONV_W):
            acc = acc + w_ref[pl.ds(k, 1), :] * _shift_up(d, CONV_W - 1 - k, rows, L)
            dw_ref[pl.ds(k, 1), :] = jnp.sum(d * _shift_down(x, CONV_W - 1 - k, rows, L), axis=0, keepdims=True)
        dx_ref[...] = acc
        db_ref[...] = jnp.sum(d, axis=0, keepdims=True)

    nl = BW // LANE
    blk = pl.BlockSpec((L, LANE), lambda g: (0, g))
    return pl.pallas_call(
        body, grid=(nl,), name="conv_bwd",
        in_specs=[pl.BlockSpec((L, LANE), lambda g: (0, 5 * nl + g)), pl.BlockSpec((CONV_W, LANE), lambda g: (0, g)), blk],
        out_specs=[blk, pl.BlockSpec((CONV_W, LANE), lambda g: (0, g)), pl.BlockSpec((1, LANE), lambda g: (0, g))],
        out_shape=[jax.ShapeDtypeStruct((L, BW), F32), jax.ShapeDtypeStruct((CONV_W, BW), F32),
                   jax.ShapeDtypeStruct((1, BW), F32)],
        compiler_params=pltpu.CompilerParams(vmem_limit_bytes=VMEM_LIMIT),
    )(proj, w, dxc)


def loss_fwd_bwd(x, fw, target, L, tm):
    def fn(x, fw, t):
        err = jnp.square(_rms(x, fw) - t)
        return jnp.sum(0.5 * jnp.mean(err, axis=-1, keepdims=True), axis=0, keepdims=True)

    def body(x_ref, fw_ref, t_ref, l_ref, dx_ref, dfw_ref):
        i = pl.program_id(0)
        t = t_ref[...]
        val, vjp = jax.vjp(lambda x, fw: fn(x, fw, t), x_ref[...], fw_ref[...])
        dx, dfw = vjp(jnp.ones((1, 1), F32))
        dx_ref[...] = dx

        @pl.when(i == 0)
        def _():
            l_ref[...] = jnp.zeros_like(l_ref)
            dfw_ref[...] = jnp.zeros_like(dfw_ref)

        l_ref[...] += jnp.broadcast_to(val, l_ref.shape)
        dfw_ref[...] += dfw

    row = pl.BlockSpec((tm, D_MODEL), lambda i: (i, 0))
    vec = pl.BlockSpec((1, D_MODEL), lambda i: (0, 0))
    return pl.pallas_call(
        body, grid=(L // tm,), name="loss_fwd_bwd", in_specs=[row, vec, row],
        out_specs=[pl.BlockSpec((1, LANE), lambda i: (0, 0)), row, vec],
        out_shape=[jax.ShapeDtypeStruct((1, LANE), F32), jax.ShapeDtypeStruct((L, D_MODEL), F32),
                   jax.ShapeDtypeStruct((1, D_MODEL), F32)],
        compiler_params=pltpu.CompilerParams(vmem_limit_bytes=VMEM_LIMIT, dimension_semantics=("arbitrary",)),
    )(x, fw, target)


def adamw(w, g, m, v):
    rows, cols = w.shape
    tr = _row_tile(rows, cols, budget=1024 * 1024)
    c1 = 1.0 - ADAM_B1 ** ADAM_STEP
    c2 = 1.0 - ADAM_B2 ** ADAM_STEP

    def body(w_ref, g_ref, m_ref, v_ref, d_ref, nm_ref, nv_ref):
        g = g_ref[...]
        nm = ADAM_B1 * m_ref[...] + (1.0 - ADAM_B1) * g
        nv = ADAM_B2 * v_ref[...] + (1.0 - ADAM_B2) * jnp.square(g)
        d_ref[...] = -ADAM_LR * ((nm / c1) / (jnp.sqrt(nv / c2) + ADAM_EPS) + ADAM_WD * w_ref[...])
        nm_ref[...] = nm
        nv_ref[...] = nv

    blk = pl.BlockSpec((tr, cols), lambda i: (i, 0))
    return pl.pallas_call(
        body, grid=(rows // tr,), name="adamw", in_specs=[blk] * 4, out_specs=[blk] * 3,
        out_shape=[jax.ShapeDtypeStruct((rows, cols), F32)] * 3,
    )(w, g, m, v)


def s5_prep(lam_re, lam_im, log_dt, b_re, b_im, c_re, c_im):
    lr = jnp.minimum(lam_re, -1e-4)
    li = lam_im
    dt = jnp.exp(log_dt)[:, None]
    mag = jnp.exp(lr * dt)
    ar = mag * jnp.cos(li * dt)
    ai = mag * jnp.sin(li * dt)
    den = lr * lr + li * li
    fr = ((ar - 1.0) * lr + ai * li) / den
    fi = (ai * lr - (ar - 1.0) * li) / den
    bbr = fr[..., None] * b_re - fi[..., None] * b_im
    bbi = fr[..., None] * b_im + fi[..., None] * b_re
    hg = S5_GROUPS // 2
    emb_b = lambda bb: _block_diag(bb.transpose(0, 2, 1).reshape(hg * S5_GROUP, S5_STATE), hg)
    emb_c = lambda cc: _block_diag(cc.transpose(0, 2, 1).reshape(hg * S5_STATE, S5_GROUP), hg)
    bsub = jnp.stack([emb_b(bbr[:hg]), emb_b(bbr[hg:]), emb_b(bbi[:hg]), emb_b(bbi[hg:])])
    csub = jnp.stack([jnp.concatenate([emb_c(c_re[:hg]), -emb_c(c_im[:hg])], axis=0),
                      jnp.concatenate([emb_c(c_re[hg:]), -emb_c(c_im[hg:])], axis=0)])
    nb = S5_N // LANE
    return ar.reshape(nb, 1, LANE), ai.reshape(nb, 1, LANE), bsub, csub


def _block_diag(stacked, groups):
    rows, c = stacked.shape
    r = rows // groups
    row_g = jnp.arange(rows)[:, None] // r
    col_g = jnp.arange(groups * c)[None, :] // c
    return jnp.where(row_g == col_g, jnp.tile(stacked, (1, groups)), 0.0)


def rg_prep(w):
    return _block_diag(w.reshape(BW, RG_BLOCK), RG_BLOCKS)


def hg_prep(logits):
    p = jax.nn.softmax(logits, axis=0)
    return jnp.cumsum(p, axis=0) - p[0]


def _head_mean_matrix():
    r = jnp.arange(BW) // HG_D
    return (r[:, None] == r[None, :]).astype(F32) / HG_D


def _to_segment_order(a):
    L = a.shape[0]
    return a.reshape(NSEG, L // NSEG, -1).transpose(1, 0, 2).reshape(a.shape)


def _to_time_order(a):
    L = a.shape[0]
    return a.reshape(L // NSEG, NSEG, -1).transpose(1, 0, 2).reshape(a.shape)


def _const(*idx):
    return lambda s, i: idx


def _rows(cb=0):
    return lambda s, i: (i, cb)


def _sum_parts(name, first, parts, shape):
    return add_n(name, [(first, ())] + [(parts, (s,)) for s in range(NSH)], shape)


def _ffn_weight_specs(l, j):
    F = D_FF // NSH
    one = pl.Buffered(1)
    return [pl.BlockSpec((None, NSH, D_MODEL, F), lambda i: (j, 0, 0, 0), pipeline_mode=one),
            pl.BlockSpec((None, NSH, D_MODEL, F), lambda i: (j, 0, 0, 0), pipeline_mode=one),
            pl.BlockSpec((None, NSH, F, D_MODEL), lambda i: (j, 0, 0, 0), pipeline_mode=one)]


def ffn_fwd(name, x, W, l, j, k, L, tm):
    D, F = D_MODEL, D_FF // NSH

    def body(x_ref, nw_ref, wg_ref, wu_ref, wd_ref, y_ref, g_ref, u_ref):
        x = x_ref[...]
        h = _rms(x, nw_ref[...]).astype(MMT)
        y = x
        for s in range(NSH):
            g = _dg(h, wg_ref[s], 1, 0)
            u = _dg(h, wu_ref[s], 1, 0)
            g_ref[s] = g.astype(g_ref.dtype)
            u_ref[s] = u.astype(u_ref.dtype)
            y = y + 0.5 * _dg((jax.nn.silu(g) * u).astype(MMT), wd_ref[s], 1, 0)
        y_ref[...] = y

    row = pl.BlockSpec((tm, D), lambda i: (i, 0))
    act = pl.BlockSpec((NSH, tm, F), lambda i: (0, i, 0))
    return pl.pallas_call(
        body, grid=(L // tm,), name=name,
        in_specs=[row, pl.BlockSpec((None, None, 1, D), lambda i: (l, k, 0, 0))] + _ffn_weight_specs(l, j),
        out_specs=[row, act, act],
        out_shape=[jax.ShapeDtypeStruct((L, D), F32), jax.ShapeDtypeStruct((NSH, L, F), MMT),
                   jax.ShapeDtypeStruct((NSH, L, F), MMT)],
        compiler_params=pltpu.CompilerParams(vmem_limit_bytes=VMEM_LIMIT, dimension_semantics=("arbitrary",)),
    )(x, W["nw"], W["L"][l]["wg"], W["L"][l]["wu"], W["L"][l]["wd"])


def ffn_bwd(name, x, g, u, dy, W, bufs, l, j, k, L, tm):
    D, F = D_MODEL, D_FF // NSH
    tm = min(TM_WGRAD, L)

    def body(x_ref, nw_ref, dy_ref, g_ref, u_ref, wg_ref, wu_ref, wd_ref, *rest):
        part_ref, dnw_ref, dwg_ref, dwu_ref, dwd_ref = rest[-5:]
        s, i = pl.program_id(0), pl.program_id(1)
        x, nw = x_ref[...], nw_ref[...]
        r = lax.rsqrt(jnp.mean(x * x, axis=-1, keepdims=True) + EPS)
        xhat = x * r
        h = (xhat * nw).astype(MMT)
        half_dy = (0.5 * dy_ref[...]).astype(MMT)
        gs, us = g_ref[...].astype(F32), u_ref[...].astype(F32)
        sig = jax.nn.sigmoid(gs)
        act = gs * sig
        da = _dg(half_dy, wd_ref[...], 1, 1)
        du = (da * act).astype(MMT)
        dg = (da * us * (sig * (1.0 + gs * (1.0 - sig)))).astype(MMT)
        dh = _dg(dg, wg_ref[...], 1, 1) + _dg(du, wu_ref[...], 1, 1)
        dxh = dh * nw
        part_ref[...] = r * (dxh - xhat * jnp.mean(dxh * xhat, axis=-1, keepdims=True))
        grads = (_dg(h, dg, 0, 0), _dg(h, du, 0, 0), _dg((act * us).astype(MMT), half_dy, 0, 0))
        dnw = jnp.sum(dh * xhat, axis=0, keepdims=True)
        first = jnp.logical_and(s == 0, i == 0)
        for ref, val, start in zip((dwg_ref, dwu_ref, dwd_ref, dnw_ref), grads + (dnw,), (i == 0, i == 0, i == 0, first)):
            @pl.when(start)
            def _(ref=ref, val=val):
                ref[...] = val

            @pl.when(jnp.logical_not(start))
            def _(ref=ref, val=val):
                ref[...] += val

    row = pl.BlockSpec((tm, D), lambda s, i: (i, 0))
    act = pl.BlockSpec((None, tm, F), lambda s, i: (s, i, 0))
    wsp = lambda r, c: pl.BlockSpec((None, None, r, c), lambda s, i: (j, s, 0, 0))
    gsp = lambda r, c: pl.BlockSpec((None, None, r, c), lambda s, i: (0, s, 0, 0))
    part, dnw, bufs[("ffn_gate", l, j)], bufs[("ffn_up", l, j)], bufs[("ffn_down", l, j)] = pl.pallas_call(
        body, grid=(NSH, L // tm), name=name,
        in_specs=[row, pl.BlockSpec((None, None, 1, D), lambda s, i: (l, k, 0, 0)), row, act, act,
                  wsp(D, F), wsp(D, F), wsp(F, D)],
        out_specs=[pl.BlockSpec((None, tm, D), lambda s, i: (s, i, 0)), pl.BlockSpec((1, D), lambda s, i: (0, 0)),
                   gsp(D, F), gsp(D, F), gsp(F, D)],
        out_shape=[jax.ShapeDtypeStruct((NSH, L, D), F32), jax.ShapeDtypeStruct((1, D), F32)]
        + [jax.ShapeDtypeStruct((1, NSH, D, F), F32)] * 2 + [jax.ShapeDtypeStruct((1, NSH, F, D), F32)],
        compiler_params=pltpu.CompilerParams(vmem_limit_bytes=VMEM_LIMIT, dimension_semantics=("arbitrary", "arbitrary")),
    )(x, W["nw"], dy, g, u, W["L"][l]["wg"], W["L"][l]["wu"], W["L"][l]["wd"])
    return _sum_parts(name + "_dx", dy, part, (L, D)), dnw


def layer_fwd(l, x0, W, P, L, tm):
    D = D_MODEL
    tmm = tm
    tm = min(TM_FWD, L)
    n_i = L // tm
    x1, g0, u0 = ffn_fwd(f"ffn_fwd_{l}0", x0, W, l, 0, 0, L, tm)
    proj = tile_fwd(
        lambda x, nw, win, s: pre_core(x, nw, win), f"pre_fwd_{l}", n_i, NSH,
        [(x1, (tm, D), _rows()), (W["nw"], (None, None, 1, D), _const(l, 1, 0, 0)),
         (W["L"][l]["win"], (None, D, IN_TOTAL // NSH), lambda s, i: (s, 0, 0))],
        [((L, IN_TOTAL), F32, (tm, IN_TOTAL // NSH), lambda s, i: (i, s), False)], s_outer=True)[0]
    nb = S5_N // LANE
    blk3 = lambda s, i: (0, i, 0)
    bur, bui = tile_fwd(
        lambda *a: s5_pre_core(*a[:-1]), f"s5pre_fwd_{l}", n_i, 1,
        [(proj, (tm, BW), _rows(0))] + [(P["bsub"], (None, None, BW // 2, S5_N // 2), _const(l, q, 0, 0)) for q in range(4)],
        [((nb, L, LANE), F32, (nb, tm, LANE), blk3, False)] * 2)
    xr, xi = s5_scan_fwd(bur, bui, P["ar"][l], P["ai"][l], L)
    qzv = _to_time_order(proj[:, BW:4 * BW])
    o_t, sst = hg_fwd(qzv, P["lb"][l], L)
    o = _to_segment_order(o_t)
    xc = conv_fwd(proj, W["convw"][l], P["convb"][l], L)
    vec = (None, 1, BW)
    a, b = tile_fwd(
        lambda xc, wa, ba, wx, bx, lam, s: gates_core(xc, wa, ba, wx, bx, lam), f"gates_fwd_{l}", n_i, 1,
        [(xc, (tm, BW), _rows()), (P["wa"], (None, BW, BW), _const(l, 0, 0)), (P["ba"], vec, _const(l, 0, 0)),
         (P["wx"], (None, BW, BW), _const(l, 0, 0)), (P["bx"], vec, _const(l, 0, 0)), (P["lam"], vec, _const(l, 0, 0))],
        [((L, BW), F32, (tm, BW), _rows(), False)] * 2)
    hs = rg_scan_fwd(a, b, L)
    ya, yb, yc = tile_fwd(
        lambda *a: mid_core(*a[:-1]), f"mid_fwd_{l}", L // tmm, 1,
        [(xr, (nb, tmm, LANE), blk3), (xi, (nb, tmm, LANE), blk3), (proj, (tmm, BW), _rows(0)), (o, (tmm, BW), _rows()),
         (proj, (tmm, BW), _rows(4)), (hs, (tmm, BW), _rows()), (proj, (tmm, BW), _rows(6)),
         (P["hmat"], (BW, BW), _const(0, 0)), (P["csub"], (None, None, S5_N, BW // 2), _const(l, 0, 0, 0)),
         (P["csub"], (None, None, S5_N, BW // 2), _const(l, 1, 0, 0)), (P["d"], vec, _const(l, 0, 0)),
         (W["L"][l]["gluw"], (BW, BW), _const(0, 0)), (P["glub"], vec, _const(l, 0, 0)), (P["hgw"], vec, _const(l, 0, 0))],
        [((L, BW), F32, (tmm, BW), _rows(), False)] * 3)
    x2 = tile_fwd(
        lambda x, *rest: (x + merge_core(*rest[:-1])[0],), f"merge_fwd_{l}", n_i, 1,
        [(x1, (tm, D), _rows()), (ya, (tm, BW), _rows()), (yb, (tm, BW), _rows()), (yc, (tm, BW), _rows())]
        + [(proj, (tm, BW), _rows(7 + k)) for k in range(6)]
        + [(W["L"][l]["pfull"], (3, BW, D), _const(0, 0, 0)), (W["L"][l]["woutfull"], (D, D), _const(0, 0))],
        [((L, D), F32, (tm, D), _rows(), False)])[0]
    x3, g1, u1 = ffn_fwd(f"ffn_fwd_{l}1", x2, W, l, 1, 2, L, tm)
    saved = dict(x0=x0, x1=x1, x2=x2, proj=proj, xr=xr, xi=xi, o=o, sst=sst, xc=xc, a=a, hs=hs, ya=ya, yb=yb, yc=yc,
                 qzv=qzv, g0=g0, u0=u0, g1=g1, u1=u1)
    return x3, saved


def layer_bwd(l, dx3, sv, W, P, bufs, L, tm, ready=lambda l, group: None):
    D = D_MODEL
    n_i = L // tm
    nb = S5_N // LANE
    dq = D // NSH
    vec = (None, 1, BW)
    vout = ((1, BW), (1, BW), _const(0, 0), "acc_all")
    blk3 = lambda s, i: (0, i, 0)
    small = {}
    proj = sv["proj"]

    dx2, dnw2 = ffn_bwd(f"ffn_bwd_{l}1", sv["x2"], sv["g1"], sv["u1"], dx3, W, bufs, l, 1, 2, L, tm)
    ready(l, "ffn1")

    rw256 = ((L, BW), (tm, BW), _rows(), "write")
    res = tile_bwd(
        merge_core, f"merge_bwd_{l}", n_i, 1,
        [(sv["ya"], (tm, BW), _rows(), "r"), (sv["yb"], (tm, BW), _rows(), "r"), (sv["yc"], (tm, BW), _rows(), "r")]
        + [(proj, (tm, BW), _rows(7 + k), "r") for k in range(6)]
        + [(W["L"][l]["pfull"], (3, BW, D), _const(0, 0, 0), "w"), (W["L"][l]["woutfull"], (D, D), _const(0, 0), "w")],
        [(dx2, (tm, D), _rows())],
        [rw256] * 9
        + [((3, BW, D), (3, BW, D), _const(0, 0, 0), "acc_all"), ((D, D), (D, D), _const(0, 0), "acc_all")])
    dya, dyb, dyc = res[:3]
    dgm = res[3:9]
    bufs[("branch_proj", l)], bufs[("w_out", l)] = res[9:]
    ready(l, "merge")

    tmm = tm
    rw = ((L, BW), (tmm, BW), _rows(), "write")
    xw = ((nb, L, LANE), (nb, tmm, LANE), blk3, "write")
    res = tile_bwd(
        mid_core, f"mid_bwd_{l}", L // tmm, 1,
        [(sv["xr"], (nb, tmm, LANE), blk3, "r"), (sv["xi"], (nb, tmm, LANE), blk3, "r"), (proj, (tmm, BW), _rows(0), "r"),
         (sv["o"], (tmm, BW), _rows(), "r"), (proj, (tmm, BW), _rows(4), "r"), (sv["hs"], (tmm, BW), _rows(), "r"),
         (proj, (tmm, BW), _rows(6), "r"), (P["hmat"], (BW, BW), _const(0, 0), "c"),
         (P["csub"], (None, None, S5_N, BW // 2), _const(l, 0, 0, 0), "w"),
         (P["csub"], (None, None, S5_N, BW // 2), _const(l, 1, 0, 0), "w"), (P["d"], vec, _const(l, 0, 0), "p"),
         (W["L"][l]["gluw"], (BW, BW), _const(0, 0), "w"), (P["glub"], vec, _const(l, 0, 0), "p"),
         (P["hgw"], vec, _const(l, 0, 0), "p")],
        [(dya, (tmm, BW), _rows()), (dyb, (tmm, BW), _rows()), (dyc, (tmm, BW), _rows())],
        [xw, xw, rw, rw, rw, rw, rw,
         ((DEPTH, S5_N, BW // 2), (None, S5_N, BW // 2), _const(l, 0, 0), "acc_all", bufs.get("csub0")),
         ((DEPTH, S5_N, BW // 2), (None, S5_N, BW // 2), _const(l, 0, 0), "acc_all", bufs.get("csub1")), vout,
         ((BW, BW), (BW, BW), _const(0, 0), "acc_all"), vout, vout])
    dxr, dxi, du_skip, do, dg_b, dhs, dgate_c, bufs["csub0"], bufs["csub1"], dd, bufs[("s5_glu_w", l)], dglub, dhgw = res
    small["s5_d"], small["s5_glu_b"], small["hg_norm_w"] = dd[0], dglub[0], dhgw[0]
    ready(l, "mid")

    da, db = rg_scan_bwd(sv["a"], sv["hs"], dhs, L)
    wmat = lambda key: ((DEPTH, BW, BW), (None, BW, BW), _const(l, 0, 0), "acc_all", bufs.get(key))
    res = tile_bwd(
        gates_core, f"gates_bwd_{l}", n_i, 1,
        [(sv["xc"], (tm, BW), _rows(), "r"), (P["wa"], (None, BW, BW), _const(l, 0, 0), "w"), (P["ba"], vec, _const(l, 0, 0), "p"),
         (P["wx"], (None, BW, BW), _const(l, 0, 0), "w"), (P["bx"], vec, _const(l, 0, 0), "p"), (P["lam"], vec, _const(l, 0, 0), "p")],
        [(da, (tm, BW), _rows()), (db, (tm, BW), _rows())],
        [((L, BW), (tm, BW), _rows(), "write"), wmat("wa"), vout, wmat("wx"), vout, vout])
    dxc, bufs["wa"], dba, bufs["wx"], dbx, dlam = res
    small["rg_ba"], small["rg_bx"], small["rg_lambda"] = dba[0], dbx[0], dlam[0]
    dx_c, dconvw, dconvb = conv_bwd(proj, W["convw"][l], dxc, L)
    small["rg_conv_w"], small["rg_conv_b"] = dconvw, dconvb[0]

    dq_b, dz_b, dv_b, dlb = hg_bwd(sv["qzv"], P["lb"][l], sv["sst"], _to_time_order(do), L)
    dq_b, dz_b, dv_b = [_to_segment_order(a) for a in (dq_b, dz_b, dv_b)]

    gr, gi, dar, dai = s5_scan_bwd(dxr, dxi, sv["xr"], sv["xi"], P["ar"][l], P["ai"][l], L)
    bblk = (None, None, BW // 2, S5_N // 2)
    res = tile_bwd(
        s5_pre_core, f"s5pre_bwd_{l}", n_i, 1,
        [(proj, (tm, BW), _rows(0), "r")] + [(P["bsub"], bblk, _const(l, q, 0, 0), "w") for q in range(4)],
        [(gr, (nb, tm, LANE), blk3), (gi, (nb, tm, LANE), blk3)],
        [((L, BW), (tm, BW), _rows(), "write")]
        + [((DEPTH, BW // 2, S5_N // 2), bblk[1:], _const(l, 0, 0), "acc_all", bufs.get(f"bsub{q}")) for q in range(4)])
    du_pre = res[0]
    for q in range(4):
        bufs[f"bsub{q}"] = res[1 + q]
    du_a = add_n(f"du_a_{l}", [(du_skip, ()), (du_pre, ())], (L, BW))
    prep_ct = dict(dar=dar, dai=dai, dlb=dlb)

    pieces = [du_a, dq_b, dz_b, dv_b, dg_b, dx_c, dgate_c, *dgm]
    per_piece, per_shard = BW // LANE, IN_TOTAL // NSH // LANE
    part, dnw1 = None, []
    tmw = min(TM_WGRAD, L)
    for s in range(NSH):
        groups = [(pieces[g // per_piece], (tmw, LANE), _rows(g % per_piece))
                  for g in range(s * per_shard, (s + 1) * per_shard)]
        part, dnw_s, bufs[("w_in", l)] = tile_bwd(
            pre_core, f"pre_bwd_{l}{s}", L // tmw, 1,
            [(sv["x1"], (tmw, D), _rows(), "r"), (W["nw"], (None, None, 1, D), _const(l, 1, 0, 0), "p"),
             (W["L"][l]["win"], (None, D, IN_TOTAL // NSH), _const(s, 0, 0), "w")],
            [groups],
            [((NSH, L, D), (None, tmw, D), functools.partial(lambda _s, i, s: (s, i, 0), s=s), "write", part),
             ((1, D), (1, D), _const(0, 0), "acc_all"),
             ((1, NSH, D, IN_TOTAL // NSH), (None, None, D, IN_TOTAL // NSH), _const(0, s, 0, 0), "acc_all",
              bufs.get(("w_in", l)))])
        dnw1.append(dnw_s)
    dnw1 = (dnw1[0] + dnw1[1]) + (dnw1[2] + dnw1[3])
    dx1 = _sum_parts(f"pre_bwd_{l}_dx", dx2, part, (L, D))
    ready(l, "pre")

    dx0, dnw0 = ffn_bwd(f"ffn_bwd_{l}0", sv["x0"], sv["g0"], sv["u0"], dx1, W, bufs, l, 0, 0, L, tm)
    ready(l, "ffn0")
    small["norm_w"] = jnp.concatenate([dnw0, dnw1, dnw2], axis=0)
    return dx0, small, prep_ct


SMALL_RAW = ("s5_lambda_re", "s5_lambda_im", "s5_log_dt", "s5_b_re", "s5_b_im", "s5_c_re", "s5_c_im", "s5_d", "s5_glu_b",
             "hg_lb_logits", "hg_norm_w", "rg_conv_b", "rg_wa", "rg_ba", "rg_wx", "rg_bx", "rg_lambda", "final_norm_w")
DEPTH = 2


def local_step(x, target, W, raw, layer_weights=None, layer_grads=None):
    L = x.shape[0]
    tm = min(256, L)
    col = lambda v: v.reshape(DEPTH, 1, BW)
    (ar, ai, bsub, csub), s5_vjp = jax.vjp(jax.vmap(s5_prep), *[raw[k] for k in SMALL_RAW[:7]])
    (wa, wx), rg_vjp = jax.vjp(lambda a, b: (jax.vmap(rg_prep)(a), jax.vmap(rg_prep)(b)), raw["rg_wa"], raw["rg_wx"])
    lb, hg_vjp = jax.vjp(hg_prep, raw["hg_lb_logits"])
    P = dict(
        ar=[ar[l] for l in range(DEPTH)], ai=[ai[l] for l in range(DEPTH)],
        bsub=bsub.astype(MMT), csub=csub.astype(MMT), wa=wa.astype(MMT), wx=wx.astype(MMT),
        lb=[lb[l].reshape(1, BW) for l in range(DEPTH)], convb=[raw["rg_conv_b"][l].reshape(1, BW) for l in range(DEPTH)],
        ba=col(raw["rg_ba"]), bx=col(raw["rg_bx"]), lam=col(raw["rg_lambda"]), d=col(raw["s5_d"]),
        glub=col(raw["s5_glu_b"]), hgw=col(raw["hg_norm_w"]), hmat=_head_mean_matrix())

    saved = []
    h = _to_segment_order(x)
    for l in range(DEPTH):
        if layer_weights is not None:
            W["L"][l], h = layer_weights(l, h)
        h, sv = layer_fwd(l, h, W, P, L, tm)
        saved.append(sv)
    loss, dh, dfw = loss_fwd_bwd(h, raw["final_norm_w"].reshape(1, D_MODEL), _to_segment_order(target), L, tm)

    big, per_layer, prep_cts = {}, [None] * DEPTH, [None] * DEPTH
    ready = (lambda l, group: None) if layer_grads is None else (lambda l, group: layer_grads(l, group, big))
    for l in reversed(range(DEPTH)):
        dh, sm, pc = layer_bwd(l, dh, saved[l], W, P, big, L, tm, ready)
        per_layer[l], prep_cts[l] = sm, pc
    dh = _to_time_order(dh)

    small = {k: jnp.stack([per_layer[l][k] for l in range(DEPTH)]) for k in per_layer[0]}
    both = lambda k: jnp.stack([prep_cts[l][k] for l in range(DEPTH)])
    dbsub = jnp.stack([big.pop(f"bsub{q}") for q in range(4)], axis=1)
    dcsub = jnp.stack([big.pop("csub0"), big.pop("csub1")], axis=1)
    s5_g = s5_vjp((both("dar"), both("dai"), dbsub, dcsub))
    small.update(zip(SMALL_RAW[:7], s5_g))
    small["rg_wa"], small["rg_wx"] = rg_vjp((big.pop("wa"), big.pop("wx")))
    (small["hg_lb_logits"],) = hg_vjp(jnp.concatenate([prep_cts[l]["dlb"] for l in range(DEPTH)], axis=0))
    small["final_norm_w"] = dfw[0]
    return loss, dh, big, small


ANY = pl.BlockSpec(memory_space=pl.ANY)


def _place():
    x, y, c = lax.axis_index("x"), lax.axis_index("y"), lax.axis_index("c")
    chips = [(1 - x, y), (x, 1 - y), (1 - x, 1 - y)]
    return x, y, c, chips


def _remote(src, dst, send, recv, k, to):
    return pltpu.make_async_remote_copy(src_ref=src, dst_ref=dst, send_sem=send.at[k], recv_sem=recv.at[k],
                                        device_id=to, device_id_type=MESH)


def _comm_call(body, name, ins, out_shapes, n_sem, n_loc):
    return pl.pallas_call(
        body, name=name, in_specs=[ANY] * len(ins), out_specs=[ANY] * len(out_shapes), out_shape=out_shapes,
        scratch_shapes=[pltpu.SemaphoreType.DMA((n_sem,)), pltpu.SemaphoreType.DMA((n_sem,)),
                        pltpu.SemaphoreType.DMA((max(n_loc, 1),))],
    )(*ins)


def gather_shards(name, shards):
    n = len(shards)
    per = 8

    def body(*refs):
        ins, outs = refs[:n], refs[n:2 * n]
        send, recv, _ = refs[2 * n:]
        x, y, c, chips = _place()
        me = 2 * x + y
        sib = (x, y, 1 - c)
        sends = []
        for w in range(n):
            for j, (cx, cy) in enumerate(chips):
                cp = _remote(ins[w].at[c], outs[w].at[c, me], send, recv, per * w + j, (cx, cy, c))
                cp.start()
                sends.append(cp)
        for w in range(n):
            for l in range(2):
                cp = _remote(ins[w].at[l], outs[w].at[l, me], send, recv, per * w + 6 + l, sib)
                cp.start()
                sends.append(cp)
        for w in range(n):
            for j, (cx, cy) in enumerate(chips):
                theirs = outs[w].at[c, 2 * cx + cy]
                _remote(ins[w].at[c], theirs, send, recv, per * w + j, (cx, cy, c)).wait_recv()
                cp = _remote(theirs, theirs, send, recv, per * w + 3 + j, sib)
                cp.start()
                sends.append(cp)
        for w in range(n):
            for j, (cx, cy) in enumerate(chips):
                dst = outs[w].at[1 - c, 2 * cx + cy]
                _remote(dst, dst, send, recv, per * w + 3 + j, sib).wait_recv()
            for l in range(2):
                dst = outs[w].at[l, me]
                _remote(dst, dst, send, recv, per * w + 6 + l, sib).wait_recv()
        for cp in sends:
            cp.wait_send()

    shapes = [jax.ShapeDtypeStruct((2, NSH) + s.shape[1:], s.dtype) for s in shards]
    return _comm_call(body, name, shards, shapes, per * n, 0)


def exchange_halves(name, grads, ranges):
    n = len(grads)

    def body(*refs):
        ins, outs = refs[:n], refs[n:2 * n]
        send, recv, _ = refs[2 * n:]
        x, y, c, _chips = _place()
        cps = []
        for w in range(n):
            h = grads[w].shape[2] // 2
            p0, np_ = ranges[w]
            cp = _remote(ins[w].at[pl.ds(p0, np_), :, pl.ds((1 - c) * h, h)], outs[w], send, recv, w, (x, y, 1 - c))
            cp.start()
            cps.append(cp)
        for cp in cps:
            cp.wait()

    shapes = [jax.ShapeDtypeStruct((r[1], NSH, g.shape[2] // 2, g.shape[3]), g.dtype) for g, r in zip(grads, ranges)]
    return _comm_call(body, name, grads, shapes, n, 0)


def scatter_to_chips(name, halves):
    n = len(halves)

    def body(*refs):
        ins, outs = refs[:n], refs[n:2 * n]
        send, recv, _ = refs[2 * n:]
        x, y, c, chips = _place()
        cps = []
        for w in range(n):
            for j, (cx, cy) in enumerate(chips):
                cp = _remote(ins[w].at[:, 2 * cx + cy], outs[w].at[j], send, recv, 3 * w + j, (cx, cy, c))
                cp.start()
                cps.append(cp)
        for cp in cps:
            cp.wait()

    shapes = [jax.ShapeDtypeStruct((3, h.shape[0]) + h.shape[2:], h.dtype) for h in halves]
    return _comm_call(body, name, halves, shapes, 3 * n, 0)


def share_halves(name, pieces):
    n = len(pieces)

    def body(*refs):
        ins, outs = refs[:n], refs[n:2 * n]
        send, recv, _ = refs[2 * n:]
        x, y, c, _chips = _place()
        cps = []
        for w in range(n):
            cp = _remote(ins[w], outs[w], send, recv, w, (x, y, 1 - c))
            cp.start()
            cps.append(cp)
        for cp in cps:
            cp.wait()

    return _comm_call(body, name, pieces, [jax.ShapeDtypeStruct(p.shape, p.dtype) for p in pieces], n, 0)


def add_own_half(name, g, ra, c, wire, b0):
    nblk, h, cols = ra.shape
    tr = _row_tile(h, cols, mult=16)
    nt = h // tr

    def body(c_ref, g_ref, r_ref, o_ref):
        o_ref[...] = (g_ref[...] + r_ref[...]).astype(o_ref.dtype)

    blk = (None, tr, cols)
    return pl.pallas_call(
        body, name=name,
        grid_spec=pltpu.PrefetchScalarGridSpec(
            num_scalar_prefetch=1, grid=(nblk, nt),
            in_specs=[pl.BlockSpec(blk, lambda s, i, c_ref: (b0 + s, c_ref[0] * nt + i, 0)), pl.BlockSpec(blk, lambda s, i, c_ref: (s, i, 0))],
            out_specs=pl.BlockSpec(blk, lambda s, i, c_ref: (s, i, 0))),
        out_shape=jax.ShapeDtypeStruct(ra.shape, wire),
    )(c.reshape(1), g, ra)


def add_chips(name, hb, rb, me):
    npc, _, h, cols = hb.shape
    tr = _row_tile(h, cols, mult=16)

    def body(me_ref, h_ref, r0, r1, r2, o_ref):
        f = lambda r: r[...].astype(F32)
        o_ref[...] = ((f(h_ref) + f(r0)) + f(r1)) + f(r2)

    rspec = lambda j: pl.BlockSpec((None, None, tr, cols), functools.partial(lambda p, i, me_ref, j: (j, p, i, 0), j=j))
    return pl.pallas_call(
        body, name=name,
        grid_spec=pltpu.PrefetchScalarGridSpec(
            num_scalar_prefetch=1, grid=(npc, h // tr),
            in_specs=[pl.BlockSpec((None, None, tr, cols), lambda p, i, me_ref: (p, me_ref[0], i, 0)), rspec(0), rspec(1), rspec(2)],
            out_specs=pl.BlockSpec((None, tr, cols), lambda p, i, me_ref: (p, i, 0))),
        out_shape=jax.ShapeDtypeStruct((npc, h, cols), F32),
    )(me.reshape(1), hb, rb, rb, rb)


def adamw_halves(name, w, m, v, own, other, c):
    npc, rows, cols = w.shape
    h = rows // 2
    tr = _row_tile(h, cols, budget=1024 * 1024)
    nt = h // tr
    c1 = 1.0 - ADAM_B1 ** ADAM_STEP
    c2 = 1.0 - ADAM_B2 ** ADAM_STEP

    def body(c_ref, w_ref, m_ref, v_ref, own_ref, oth_ref, g_ref, d_ref, nm_ref, nv_ref):
        g = jnp.where(pl.program_id(1) == c_ref[0], own_ref[...], oth_ref[...])
        nm = ADAM_B1 * m_ref[...] + (1.0 - ADAM_B1) * g
        nv = ADAM_B2 * v_ref[...] + (1.0 - ADAM_B2) * jnp.square(g)
        g_ref[...] = g
        d_ref[...] = -ADAM_LR * ((nm / c1) / (jnp.sqrt(nv / c2) + ADAM_EPS) + ADAM_WD * w_ref[...])
        nm_ref[...] = nm
        nv_ref[...] = nv

    full = pl.BlockSpec((None, tr, cols), lambda p, hh, i, c_ref: (p, hh * nt + i, 0))
    half = pl.BlockSpec((None, tr, cols), lambda p, hh, i, c_ref: (p, i, 0))
    return pl.pallas_call(
        body, name=name,
        grid_spec=pltpu.PrefetchScalarGridSpec(
            num_scalar_prefetch=1, grid=(npc, 2, nt),
            in_specs=[full, full, full, half, half], out_specs=[full] * 4),
        out_shape=[jax.ShapeDtypeStruct(w.shape, F32)] * 4,
    )(c.reshape(1), w, m, v, own, other)


WEIGHTS = ("norm_w", "final_norm_w", "ffn_gate", "ffn_up", "ffn_down", "w_in", "branch_proj", "w_out", "s5_lambda_re",
           "s5_lambda_im", "s5_log_dt", "s5_b_re", "s5_b_im", "s5_c_re", "s5_c_im", "s5_d", "s5_glu_w", "s5_glu_b",
           "hg_lb_logits", "hg_norm_w", "rg_conv_w", "rg_conv_b", "rg_wa", "rg_ba", "rg_wx", "rg_bx", "rg_lambda")
BIG = ("ffn_gate", "ffn_up", "ffn_down", "w_in", "branch_proj", "w_out", "s5_glu_w")
SHARDED_SMALL = ("norm_w", "rg_conv_w")
SMALL = SMALL_RAW + SHARDED_SMALL


def _view2d(shape):
    return (1, shape[0]) if len(shape) == 1 else (math.prod(shape[:-1]), shape[-1])


def _small_layout(shapes, row_multiple):
    layout, at = [], 0
    for shape in shapes:
        r, c = _view2d(shape)
        rp = -(-r // 8) * 8
        layout.append((at, r, c, rp))
        at += rp * max(1, c // LANE)
    return layout, -(-at // row_multiple) * row_multiple


def pack_small(name, arrays, row_multiple):
    layout, rows = _small_layout([a.shape for a in arrays], row_multiple)

    def body(*refs):
        out = refs[-1]
        out[...] = jnp.zeros_like(out)
        for ref, (r0, r, c, rp) in zip(refs[:-1], layout):
            if c <= LANE:
                out[r0:r0 + r, 0:c] = ref[...]
            else:
                for q in range(c // LANE):
                    out[r0 + q * rp:r0 + q * rp + r, :] = ref[:, q * LANE:(q + 1) * LANE]

    return pl.pallas_call(
        body, name=name, out_shape=jax.ShapeDtypeStruct((rows, LANE), F32),
        compiler_params=pltpu.CompilerParams(vmem_limit_bytes=VMEM_LIMIT),
    )(*[a.reshape(_view2d(a.shape)) for a in arrays])


def unpack_small(name, packed, shapes):
    layout, _ = _small_layout(shapes, 8)

    def body(p_ref, *outs):
        for ref, (r0, r, c, rp) in zip(outs, layout):
            if c <= LANE:
                ref[...] = p_ref[r0:r0 + r, 0:c]
            else:
                for q in range(c // LANE):
                    ref[:, q * LANE:(q + 1) * LANE] = p_ref[r0 + q * rp:r0 + q * rp + r, :]

    res = pl.pallas_call(
        body, name=name, out_shape=[jax.ShapeDtypeStruct(_view2d(s), F32) for s in shapes],
        compiler_params=pltpu.CompilerParams(vmem_limit_bytes=VMEM_LIMIT),
    )(packed)
    return [a.reshape(s) for a, s in zip(res, shapes)]


HBM = pl.BlockSpec(memory_space=pltpu.HBM)
SEM = pl.BlockSpec(memory_space=pltpu.SEMAPHORE)
EFFECT = pltpu.SideEffectType.DATAFLOW_SIDE_EFFECTING


def split_start(name, srcs, land_shapes, plan, n_send, n_recv):
    ns, nl = len(srcs), len(land_shapes)

    def body(*refs):
        ins, lands = refs[:ns], refs[ns:ns + nl]
        send, recv = refs[ns + nl], refs[ns + nl + 1]
        for src, dst, ks, kr, dev in plan(ins, lands):
            pltpu.make_async_remote_copy(src_ref=src, dst_ref=dst, send_sem=send.at[ks], recv_sem=recv.at[kr],
                                         device_id=dev, device_id_type=MESH).start()
        refs[-1][...] = jnp.zeros_like(refs[-1])

    hbm = lambda a: pltpu.with_memory_space_constraint(a, pltpu.HBM)
    lands = [lax.empty(s.shape, s.dtype) for s in land_shapes]
    out = pl.pallas_call(
        body, name=name,
        out_shape=(pltpu.SemaphoreType.DMA((n_send,)), pltpu.SemaphoreType.DMA((n_recv,)),
                   *[pltpu.HBM(a.shape, a.dtype) for a in srcs], *[pltpu.HBM(s.shape, s.dtype) for s in land_shapes],
                   jax.ShapeDtypeStruct((8, LANE), F32)),
        in_specs=[HBM] * (ns + nl), out_specs=(SEM, SEM, *[HBM] * (ns + nl), pl.BlockSpec(memory_space=pltpu.VMEM)),
        input_output_aliases={k: 2 + k for k in range(ns + nl)},
        compiler_params=pltpu.CompilerParams(has_side_effects=EFFECT),
    )(*[hbm(a) for a in srcs], *[hbm(a) for a in lands])
    return out[:-1], out[-1]


def split_wait(name, handles, n_src, waits, after):
    send, recv, *bufs = handles
    nb = len(bufs)

    def body(*refs):
        ins, lands = refs[:n_src], refs[n_src:nb]
        send_sem, recv_sem = refs[nb], refs[nb + 1]
        x, y, c, _chips = _place()
        sends, recvs = waits(ins, lands)
        for src, k in sends:
            pltpu.make_async_remote_copy(src_ref=src, dst_ref=src, send_sem=send_sem.at[k], recv_sem=recv_sem.at[0],
                                         device_id=(x, y, 1 - c), device_id_type=MESH).wait_send()
        for dst, k in recvs:
            pltpu.make_async_remote_copy(src_ref=dst, dst_ref=dst, send_sem=send_sem.at[0], recv_sem=recv_sem.at[k],
                                         device_id=(x, y, 1 - c), device_id_type=MESH).wait_recv()

    out = pl.pallas_call(
        body, name=name, out_shape=tuple(pltpu.HBM(a.shape, a.dtype) for a in bufs),
        in_specs=[HBM] * nb + [SEM, SEM, ANY], out_specs=tuple([HBM] * nb),
        input_output_aliases={k: k for k in range(nb)},
        compiler_params=pltpu.CompilerParams(has_side_effects=EFFECT),
    )(*bufs, send, recv, after)
    return list(out[:n_src]), list(out[n_src:])


def gather_plan(n):
    def plan(ins, lands):
        x, y, c, chips = _place()
        me = 2 * x + y
        copies = []
        for w in range(n):
            for j, (cx, cy) in enumerate(chips):
                for t in range(2):
                    copies.append((ins[w].at[c], lands[w].at[c, me], 8 * w + 2 * j + t, 8 * w + 2 * j + c, (cx, cy, t)))
            for half in range(2):
                copies.append((ins[w].at[half], lands[w].at[half, me], 8 * w + 6 + half, 8 * w + 6 + half, (x, y, 1 - c)))
        return copies

    def waits(ins, lands):
        x, y, c, chips = _place()
        me = 2 * x + y
        sends, recvs = [], []
        for w in range(n):
            for j, (cx, cy) in enumerate(chips):
                for t in range(2):
                    sends.append((ins[w].at[c], 8 * w + 2 * j + t))
                    recvs.append((lands[w].at[t, 2 * cx + cy], 8 * w + 2 * j + t))
            for half in range(2):
                sends.append((ins[w].at[half], 8 * w + 6 + half))
                recvs.append((lands[w].at[half, me], 8 * w + 6 + half))
        return sends, recvs

    return plan, waits


def scatter_plan(n):
    def plan(ins, lands):
        x, y, c, chips = _place()
        return [(ins[w].at[:, 2 * cx + cy], lands[w].at[j], 3 * w + j, 3 * w + j, (cx, cy, c))
                for w in range(n) for j, (cx, cy) in enumerate(chips)]

    def waits(ins, lands):
        x, y, c, chips = _place()
        sends = [(ins[w].at[:, 2 * cx + cy], 3 * w + j) for w in range(n) for j, (cx, cy) in enumerate(chips)]
        recvs = [(lands[w].at[j], 3 * w + j) for w in range(n) for j in range(3)]
        return sends, recvs

    return plan, waits


def _layer_shards(w, l):
    return [w["ffn_gate"][l].astype(MMT), w["ffn_up"][l].astype(MMT), w["ffn_down"][l].astype(MMT),
            w["w_in"][l].reshape(2, D_MODEL // 2, -1).astype(MMT),
            w["branch_proj"][l].reshape(2, 3 * BW // 2, -1).astype(MMT),
            w["w_out"][l].reshape(2, -1, D_MODEL).astype(MMT),
            w["s5_glu_w"][l].reshape(2, -1, BW).astype(MMT)]


def _layer_weights(g):
    rows = lambda a: a.transpose(1, 0, 2, 3).reshape(NSH, -1, a.shape[-1])
    p = rows(g[4]).reshape(NSH, 3, BW, -1).transpose(1, 2, 0, 3).reshape(3, BW, D_MODEL)
    return dict(wg=g[0], wu=g[1], wd=g[2], win=rows(g[3]), pfull=p,
                woutfull=rows(g[5]).reshape(D_MODEL, D_MODEL), gluw=rows(g[6]).reshape(BW, BW))


GROUPS = {"ffn1": ("ffn_gate", "ffn_up", "ffn_down"), "merge": ("branch_proj", "w_out"), "mid": ("s5_glu_w",),
          "pre": ("w_in",), "ffn0": ("ffn_gate", "ffn_up", "ffn_down")}


def _grad_views(big, l, group):
    views = []
    for name in GROUPS[group]:
        if name == "branch_proj":
            dq = D_MODEL // NSH
            a = big[(name, l)].reshape(3, BW, NSH, dq).transpose(2, 0, 1, 3).reshape(1, NSH, 3 * BW, dq)
        elif name.startswith("ffn"):
            a = big[(name, l, 1 if group == "ffn1" else 0)]
        else:
            a = big[(name, l)]
            a = a.reshape(1, NSH, -1, a.shape[-1])
        views.append((name, a, 0))
    return views


def halves_plan(n):
    def src(ref, c):
        h = ref.shape[2] // 2
        return ref.at[:, :, pl.ds((1 - c) * h, h)]

    def plan(ins, lands):
        x, y, c, _chips = _place()
        return [(src(ins[w], c), lands[w], w, w, (x, y, 1 - c)) for w in range(n)]

    def waits(ins, lands):
        x, y, c, _chips = _place()
        return [(src(ins[w], c), w) for w in range(n)], [(lands[w], w) for w in range(n)]

    return plan, waits


def _reduce_to_halves(tag, views, c, wire):
    from_sibling = exchange_halves(f"reduce_cores_{tag}", [a for _, a, _ in views], [(p0, 1) for _, _, p0 in views])
    merge = lambda a: a.reshape((-1,) + a.shape[2:])
    return [add_own_half(f"sum_cores_{tag}_{i}", merge(a), merge(r), c, wire[i], NSH * p0).reshape(r.shape)
            for i, ((_, a, p0), r) in enumerate(zip(views, from_sibling))]


def _step(x, target, w, m, v):
    mx, my, mc = lax.axis_index("x"), lax.axis_index("y"), lax.axis_index("c")
    me = (2 * mx + my).astype(jnp.int32)
    mc = mc.astype(jnp.int32)

    W = dict(L=[None] * DEPTH)
    state = {"pending": []}
    n_big = len(BIG)
    g_plan, g_waits = gather_plan(n_big)

    def layer_weights(l, h):
        if l == 0:
            got = gather_shards("gather_weights_0", _layer_shards(w, 0) + [w[n] for n in SHARDED_SMALL])
            nxt = _layer_shards(w, 1)
            got, nxt = lax.optimization_barrier((got, nxt))
            shapes = [jax.ShapeDtypeStruct((2, NSH) + a.shape[1:], a.dtype) for a in nxt]
            state["gather"], token = split_start("gather_weights_1_start", nxt, shapes, g_plan, 8 * n_big, 8 * n_big)
            W["nw"] = got[n_big].transpose(0, 2, 1, 3).reshape(DEPTH, 3, 1, D_MODEL) + token[0, 0]
            W["convw"] = got[n_big + 1].transpose(0, 2, 1, 3).reshape(DEPTH, CONV_W, BW)
            return _layer_weights(got[:n_big]), h
        return _layer_weights(split_wait("gather_weights_1_wait", state["gather"], n_big, g_waits, h)[1]), h

    def to_chips(after):
        if "cores" not in state:
            return
        tag, names, l, group, handles, waits = state.pop("cores")
        sent, landed = split_wait(f"reduce_cores_{tag}_wait", handles, len(names), waits, after)
        merge = lambda a: a.reshape((-1,) + a.shape[2:])
        halves = [add_own_half(f"sum_cores_{tag}_{i}", merge(a), merge(r), mc, jnp.bfloat16, 0).reshape(r.shape)
                  for i, (a, r) in enumerate(zip(sent, landed))]
        shapes = [jax.ShapeDtypeStruct((3, a.shape[0]) + a.shape[2:], a.dtype) for a in halves]
        plan, waits = scatter_plan(len(halves))
        handles, token = split_start(f"reduce_chips_{tag}_start", halves, shapes, plan, 3 * len(halves), 3 * len(halves))
        W["nw"] = W["nw"] + token[0, 0]
        state["pending"].append((tag, names, l, group, handles, waits))

    def layer_grads(l, group, big):
        views = _grad_views(big, l, group)
        to_chips(views[0][1])
        if (l, group) == (0, "ffn0"):
            return
        tag = f"{l}_{group}"
        if (l, group) == (0, "pre"):
            halves = _reduce_to_halves(tag, views, mc, [jnp.bfloat16] * len(views))
            shapes = [jax.ShapeDtypeStruct((3, a.shape[0]) + a.shape[2:], a.dtype) for a in halves]
            plan, waits = scatter_plan(len(halves))
            handles, token = split_start(f"reduce_chips_{tag}_start", halves, shapes, plan, 3 * len(halves), 3 * len(halves))
            W["nw"] = W["nw"] + token[0, 0]
            state["pending"].append((tag, [name for name, _, _ in views], l, group, handles, waits))
            return
        arrays = [a for _, a, _ in views]
        shapes = [jax.ShapeDtypeStruct((1, NSH, a.shape[2] // 2, a.shape[3]), a.dtype) for a in arrays]
        plan, waits = halves_plan(len(arrays))
        handles, token = split_start(f"reduce_cores_{tag}_start", arrays, shapes, plan, len(arrays), len(arrays))
        W["nw"] = W["nw"] + token[0, 0]
        state["cores"] = (tag, [name for name, _, _ in views], l, group, handles, waits)

    loss, dx, big, small = local_step(x[0], target[0], W, {k: w[k] for k in SMALL_RAW}, layer_weights, layer_grads)

    pieces = {n: {} for n in BIG}
    block_of = lambda name, l, group: (2 * l + (group == "ffn1")) if name.startswith("ffn") else l
    views = _grad_views(big, 0, "ffn0")
    small_packed = pack_small("pack_small_grads", [small[n] for n in SMALL], NSH * 32)
    halves = _reduce_to_halves("0_ffn0", views + [("small", small_packed.reshape(1, NSH, -1, LANE), 0)], mc,
                               [jnp.bfloat16] * len(views) + [F32])
    shapes = [jax.ShapeDtypeStruct((3, a.shape[0]) + a.shape[2:], a.dtype) for a in halves]
    plan, waits = scatter_plan(len(halves))
    last_handles, token = split_start("reduce_chips_0_ffn0_start", halves, shapes, plan, 3 * len(halves), 3 * len(halves))
    mc = mc + token[0, 0].astype(jnp.int32)
    for tag, names, l, group, handles, waits_k in state["pending"]:
        sent, landed = split_wait(f"reduce_chips_{tag}_wait", handles, len(names), waits_k, dx)
        for i, (name, h, r) in enumerate(zip(names, sent, landed)):
            pieces[name][block_of(name, l, group)] = add_chips(f"sum_chips_{tag}_{i}", h, r, me)

    g, delta, new_m, new_v = {}, {}, {}, {}

    def update(tag, names, extra):
        own = [jnp.concatenate([pieces[n][b] for b in sorted(pieces[n])], axis=0) for n in names] + extra
        other = share_halves(f"reduce_share_{tag}", own)
        for i, n in enumerate(names):
            view = lambda a: a.reshape(own[i].shape[0], -1, own[i].shape[2])
            res = adamw_halves(f"adamw_{n}", view(w[n]), view(m[n]), view(v[n]), own[i], other[i], mc)
            g[n], delta[n], new_m[n], new_v[n] = [a.reshape(w[n].shape) for a in res]
        return own, other

    early = [n for n in BIG if not n.startswith("ffn")]
    update("early", early, [])
    sent, landed = split_wait("reduce_chips_0_ffn0_wait", last_handles, len(halves), waits, new_v[early[0]])
    last = [add_chips(f"sum_chips_0_ffn0_{i}", h, r, me) for i, (h, r) in enumerate(zip(sent, landed))]
    for (name, _, _), piece in zip(views, last):
        pieces[name][block_of(name, 0, "ffn0")] = piece
    own, other = update("last", [n for n in BIG if n.startswith("ffn")], [last[-1]])

    piece = jnp.stack([jnp.where(mc == 0, own[-1][0], other[-1][0]), jnp.where(mc == 0, other[-1][0], own[-1][0])])
    (all_small,) = gather_shards("gather_small", [piece])
    full_small = unpack_small("unpack_small_grads", all_small.transpose(1, 0, 2, 3).reshape(-1, LANE),
                              [small[n].shape for n in SMALL])
    g.update(zip(SMALL, full_small))
    g["norm_w"] = lax.dynamic_slice_in_dim(g["norm_w"], me * (D_MODEL // NSH), D_MODEL // NSH, axis=2)
    g["rg_conv_w"] = lax.dynamic_slice_in_dim(g["rg_conv_w"], me * (BW // NSH), BW // NSH, axis=2)

    packed = [pack_small(f"pack_small_{tag}", [src[n] for n in SMALL], 8)
              for tag, src in (("w", w), ("g", g), ("m", m), ("v", v))]
    for tag, dst, flat in zip(("delta", "m", "v"), (delta, new_m, new_v), adamw(*packed)):
        dst.update(zip(SMALL, unpack_small(f"unpack_small_{tag}", flat, [w[n].shape for n in SMALL])))

    total = lax.psum(loss[0, 0], ("x", "y", "c"))
    return (total, dx[None], *[g[n] for n in WEIGHTS], *[delta[n] for n in WEIGHTS],
            *[new_m[n] for n in WEIGHTS], *[new_v[n] for n in WEIGHTS])


def kernel(x, norm_w, final_norm_w, ffn_gate, ffn_up, ffn_down, w_in, branch_proj, w_out, s5_lambda_re, s5_lambda_im, s5_log_dt, s5_b_re, s5_b_im, s5_c_re, s5_c_im, s5_d, s5_glu_w, s5_glu_b, hg_lb_logits, hg_norm_w, rg_conv_w, rg_conv_b, rg_wa, rg_ba, rg_wx, rg_bx, rg_lambda, loss_target, m_norm_w, m_final_norm_w, m_ffn_gate, m_ffn_up, m_ffn_down, m_w_in, m_branch_proj, m_w_out, m_s5_lambda_re, m_s5_lambda_im, m_s5_log_dt, m_s5_b_re, m_s5_b_im, m_s5_c_re, m_s5_c_im, m_s5_d, m_s5_glu_w, m_s5_glu_b, m_hg_lb_logits, m_hg_norm_w, m_rg_conv_w, m_rg_conv_b, m_rg_wa, m_rg_ba, m_rg_wx, m_rg_bx, m_rg_lambda, v_norm_w, v_final_norm_w, v_ffn_gate, v_ffn_up, v_ffn_down, v_w_in, v_branch_proj, v_w_out, v_s5_lambda_re, v_s5_lambda_im, v_s5_log_dt, v_s5_b_re, v_s5_b_im, v_s5_c_re, v_s5_c_im, v_s5_d, v_s5_glu_w, v_s5_glu_b, v_hg_lb_logits, v_hg_norm_w, v_rg_conv_w, v_rg_conv_b, v_rg_wa, v_rg_ba, v_rg_wx, v_rg_bx, v_rg_lambda):
    ws = (norm_w, final_norm_w, ffn_gate, ffn_up, ffn_down, w_in, branch_proj, w_out, s5_lambda_re, s5_lambda_im, s5_log_dt, s5_b_re, s5_b_im, s5_c_re, s5_c_im, s5_d, s5_glu_w, s5_glu_b, hg_lb_logits, hg_norm_w, rg_conv_w, rg_conv_b, rg_wa, rg_ba, rg_wx, rg_bx, rg_lambda)
    ms = (m_norm_w, m_final_norm_w, m_ffn_gate, m_ffn_up, m_ffn_down, m_w_in, m_branch_proj, m_w_out, m_s5_lambda_re, m_s5_lambda_im, m_s5_log_dt, m_s5_b_re, m_s5_b_im, m_s5_c_re, m_s5_c_im, m_s5_d, m_s5_glu_w, m_s5_glu_b, m_hg_lb_logits, m_hg_norm_w, m_rg_conv_w, m_rg_conv_b, m_rg_wa, m_rg_ba, m_rg_wx, m_rg_bx, m_rg_lambda)
    vs = (v_norm_w, v_final_norm_w, v_ffn_gate, v_ffn_up, v_ffn_down, v_w_in, v_branch_proj, v_w_out, v_s5_lambda_re, v_s5_lambda_im, v_s5_log_dt, v_s5_b_re, v_s5_b_im, v_s5_c_re, v_s5_c_im, v_s5_d, v_s5_glu_w, v_s5_glu_b, v_hg_lb_logits, v_hg_norm_w, v_rg_conv_w, v_rg_conv_b, v_rg_wa, v_rg_ba, v_rg_wx, v_rg_bx, v_rg_lambda)
    return _step(x, loss_target, dict(zip(WEIGHTS, ws)), dict(zip(WEIGHTS, ms)), dict(zip(WEIGHTS, vs)))
```

```python
import functools
import math
from typing import NamedTuple

import jax
import jax.numpy as jnp
from jax import lax
from jax.experimental import pallas as pl
from jax.experimental.pallas import tpu as pltpu

F32 = jnp.float32
MMT = jnp.bfloat16
HI = lax.Precision.HIGHEST

D_MODEL = 1024
BW = 512
S5_GROUP, S5_GROUPS, S5_STATE = 16, 32, 64
S5_N = S5_GROUPS * S5_STATE
HG_HEADS, HG_D = 4, 128
HG_CHUNK = 128
RG_BLOCKS, RG_BLOCK = 8, 64
RG_C = 8.0
CONV_W = 4
D_FF = 2816
EPS = 1e-6
IN_TOTAL = 6656
NSH = 4
NSEG = 8
LANE = 128
VMEM_LIMIT = 56 * 1024 * 1024
TM_FWD = 512
TM_WGRAD = 512

ADAM_LR, ADAM_B1, ADAM_B2, ADAM_EPS, ADAM_WD, ADAM_STEP = 0.001, 0.9, 0.999, 1e-08, 0.01, 10

MESH = pl.DeviceIdType.MESH


class WP(NamedTuple):
    w: jax.Array
    p: jax.Array


def _dg(a, b, ca, cb):
    return lax.dot_general(a, b, (((ca,), (cb,)), ((), ())), preferred_element_type=F32)


@jax.custom_vjp
def _mmw(a, w, p):
    return _dg(a.astype(MMT), w, 1, 0)


def _mmw_fwd(a, w, p):
    return _mmw(a, w, p), (a, w)


def _mmw_bwd(res, g):
    a, w = res
    gb = g.astype(MMT)
    return _dg(gb, w, 1, 1), jnp.zeros_like(w), _dg(a.astype(MMT), gb, 0, 0)


_mmw.defvjp(_mmw_fwd, _mmw_bwd)


def mm(a, w):
    if isinstance(w, WP):
        return _mmw(a, w.w, w.p)
    return _dg(a.astype(MMT), w, 1, 0)


@jax.custom_vjp
def mma_nn(a, b):
    return _dg(a.astype(MMT), b.astype(MMT), 1, 0)


def _nn_f(a, b):
    return mma_nn(a, b), (a, b)


def _nn_b(res, g):
    a, b = res
    gb = g.astype(MMT)
    return _dg(gb, b.astype(MMT), 1, 1), _dg(a.astype(MMT), gb, 0, 0)


mma_nn.defvjp(_nn_f, _nn_b)


@jax.custom_vjp
def mma_nt(a, b):
    return _dg(a.astype(MMT), b.astype(MMT), 1, 1)


def _nt_f(a, b):
    return mma_nt(a, b), (a, b)


def _nt_b(res, g):
    a, b = res
    gb = g.astype(MMT)
    return _dg(gb, b.astype(MMT), 1, 0), _dg(gb, a.astype(MMT), 0, 0)


mma_nt.defvjp(_nt_f, _nt_b)


@jax.custom_vjp
def mma_tn(a, b):
    return _dg(a.astype(MMT), b.astype(MMT), 0, 0)


def _tn_f(a, b):
    return mma_tn(a, b), (a, b)


def _tn_b(res, g):
    a, b = res
    gb = g.astype(MMT)
    return _dg(b.astype(MMT), gb, 1, 1), _dg(a.astype(MMT), gb, 1, 0)


mma_tn.defvjp(_tn_f, _tn_b)


def mm_exact(m, x):
    return jnp.dot(m, x, precision=HI, preferred_element_type=F32)


def _rms(x, w):
    return x * lax.rsqrt(jnp.mean(x * x, axis=-1, keepdims=True) + EPS) * w


def _expm1(x):
    series = x * (1.0 + x * (1.0 / 2) * (1.0 + x * (1.0 / 3) * (1.0 + x * (1.0 / 4) * (1.0 + x * (1.0 / 5) * (1.0 + x * (1.0 / 6))))))
    return jnp.where(jnp.abs(x) < 0.1, series, jnp.exp(x) - 1.0)


def _bspec(block, fn, order):
    if order == "is":
        return pl.BlockSpec(block, lambda i, s: fn(s, i))
    return pl.BlockSpec(block, lambda s, i: fn(s, i))


def tile_fwd(fn, name, n_i, n_s, ins, outs, s_outer=False):
    n_in = len(ins)
    order = "si" if s_outer else "is"
    assert not (s_outer and any(o[4] for o in outs))

    def body(*refs):
        s = pl.program_id(0 if s_outer else 1)
        res = fn(*[r[...] for r in refs[:n_in]], s)
        for o_ref, val, spec in zip(refs[n_in:], res, outs):
            if spec[4] and n_s > 1:
                @pl.when(s == 0)
                def _(o_ref=o_ref, val=val):
                    o_ref[...] = val.astype(o_ref.dtype)

                @pl.when(s != 0)
                def _(o_ref=o_ref, val=val):
                    o_ref[...] += val.astype(o_ref.dtype)
            else:
                o_ref[...] = val.astype(o_ref.dtype)

    return pl.pallas_call(
        body, grid=(n_s, n_i) if s_outer else (n_i, n_s), name=name,
        in_specs=[_bspec(b, f, order) for _, b, f in ins],
        out_specs=[_bspec(b, f, order) for _, _, b, f, _ in outs],
        out_shape=[jax.ShapeDtypeStruct(sh, dt) for sh, dt, _, _, _ in outs],
        compiler_params=pltpu.CompilerParams(vmem_limit_bytes=VMEM_LIMIT,
                                             dimension_semantics=("arbitrary", "arbitrary")),
    )(*[a for a, _, _ in ins])


def tile_bwd(fn, name, n_i, n_s, ins, cts, gouts):
    groups = [c if isinstance(c, list) else [c] for c in cts]
    cts = [blk for grp in groups for blk in grp]
    n_in, n_ct = len(ins), len(cts)
    kinds = [k for _, _, _, k in ins]
    d_pos = [j for j, k in enumerate(kinds) if k != "c"]
    shared = [(gi, spec[4]) for gi, spec in enumerate(gouts) if len(spec) == 5 and spec[4] is not None]
    n_sh = len(shared)

    def body(*refs):
        s, i = pl.program_id(0), pl.program_id(1)
        vals = [r[...] for r in refs[:n_in]]
        ct_refs, ctv = list(refs[n_in:n_in + n_ct]), []
        for grp in groups:
            parts = [ct_refs.pop(0)[...] for _ in grp]
            ctv.append(parts[0] if len(parts) == 1 else jnp.concatenate(parts, axis=1))
        ctv = tuple(ctv)
        g_refs = refs[n_in + n_ct + n_sh:]

        def g(*dv):
            args = list(vals)
            for j, v in zip(d_pos, dv):
                args[j] = WP(vals[j], v) if kinds[j] == "w" else v
            return tuple(fn(*args))

        dv0 = [jnp.zeros(vals[j].shape, F32) if kinds[j] == "w" else vals[j] for j in d_pos]
        _, vjp = jax.vjp(g, *dv0)
        grads = vjp(ctv)
        for g_ref, gv, spec in zip(g_refs, grads, gouts):
            mode = spec[3]
            if mode == "write":
                g_ref[...] = gv.astype(g_ref.dtype)
            else:
                first = (i == 0) if mode == "acc_i" else jnp.logical_and(i == 0, s == 0)

                @pl.when(first)
                def _(g_ref=g_ref, gv=gv):
                    g_ref[...] = gv.astype(g_ref.dtype)

                @pl.when(jnp.logical_not(first))
                def _(g_ref=g_ref, gv=gv):
                    g_ref[...] += gv.astype(g_ref.dtype)

    return pl.pallas_call(
        body, grid=(n_s, n_i), name=name,
        in_specs=([_bspec(b, f, "si") for _, b, f, _ in ins] + [_bspec(b, f, "si") for _, b, f in cts]
                  + [pl.BlockSpec(memory_space=pl.ANY)] * n_sh),
        out_specs=[_bspec(spec[1], spec[2], "si") for spec in gouts],
        out_shape=[jax.ShapeDtypeStruct(spec[0], F32) for spec in gouts],
        input_output_aliases={n_in + n_ct + k: gi for k, (gi, _) in enumerate(shared)},
        compiler_params=pltpu.CompilerParams(vmem_limit_bytes=VMEM_LIMIT,
                                             dimension_semantics=("arbitrary", "arbitrary")),
    )(*[a for a, _, _, _ in ins], *[a for a, _, _ in cts], *[buf for _, buf in shared])


def _row_tile(rows, width, itemsize=4, budget=2 * 1024 * 1024, mult=8):
    best = mult
    for t in range(mult, rows + 1, mult):
        if rows % t == 0 and t * width * itemsize <= budget:
            best = t
    return best


def add_n(name, terms, shape):
    rows, cols = shape
    tr = _row_tile(rows, cols)

    def body(*refs):
        acc = refs[0][...]
        for r in refs[1:-1]:
            acc = acc + r[...]
        refs[-1][...] = acc

    specs = []
    for _, lead in terms:
        specs.append(pl.BlockSpec((None,) * len(lead) + (tr, cols), functools.partial(lambda i, lead: (*lead, i, 0), lead=lead)))
    return pl.pallas_call(
        body, grid=(rows // tr,), name=name, in_specs=specs,
        out_specs=pl.BlockSpec((tr, cols), lambda i: (i, 0)),
        out_shape=jax.ShapeDtypeStruct((rows, cols), F32),
    )(*[a for a, _ in terms])


def ffn_core(x, nw, wg, wu, wd):
    h = _rms(x, nw)
    return (0.5 * mm(jax.nn.silu(mm(h, wg)) * mm(h, wu), wd),)


def pre_core(x, nw, win):
    return (mm(_rms(x, nw), win),)


def _split_lanes(y):
    return jnp.stack([y[:, k * LANE:(k + 1) * LANE] for k in range(y.shape[1] // LANE)], axis=0)


def _join_lanes(y3):
    return jnp.concatenate([y3[k] for k in range(y3.shape[0])], axis=1)


def s5_pre_core(u, b_re0, b_re1, b_im0, b_im1):
    u0, u1 = u[:, :BW // 2], u[:, BW // 2:]
    re = jnp.concatenate([mm(u0, b_re0), mm(u1, b_re1)], axis=1)
    im = jnp.concatenate([mm(u0, b_im0), mm(u1, b_im1)], axis=1)
    return _split_lanes(re), _split_lanes(im)


def mid_core(xr, xi, u, o, g, hs, gc, hmat, c0, c1, d, gluw, glub, hgw):
    half = xr.shape[0] // 2
    xs0 = jnp.concatenate([_join_lanes(xr[:half]), _join_lanes(xi[:half])], axis=1)
    xs1 = jnp.concatenate([_join_lanes(xr[half:]), _join_lanes(xi[half:])], axis=1)
    y = jnp.concatenate([mm(xs0, c0), mm(xs1, c1)], axis=1) + d * u
    z = jax.nn.gelu(y)
    ya = z * jax.nn.sigmoid(mm(z, gluw) + glub)
    ms = mm_exact(o * o, hmat)
    yb = o * lax.rsqrt(ms + EPS) * hgw * jax.nn.silu(g)
    yc = hs * jax.nn.gelu(gc)
    return ya, yb, yc


def _sub(w, n):
    return WP(w.w[n], w.p[n]) if isinstance(w, WP) else w[n]


def merge_core(ya, yb, yc, g0, g1, g2, g3, g4, g5, p, wout):
    gate = lambda a, b: jax.nn.sigmoid(jnp.concatenate([a, b], axis=1))
    m = gate(g0, g1) * mm(ya, _sub(p, 0)) + gate(g2, g3) * mm(yb, _sub(p, 1)) + gate(g4, g5) * mm(yc, _sub(p, 2))
    return (mm(m, wout),)


def gates_core(xc, wa, ba, wx, bx, lam):
    r = jax.nn.sigmoid(mm(xc, wa) + ba)
    i = jax.nn.sigmoid(mm(xc, wx) + bx)
    log_a = -RG_C * jax.nn.softplus(-lam) * r
    a = jnp.exp(log_a)
    b = jnp.sqrt(-_expm1(2.0 * log_a)) * (i * xc)
    return a, b


def _seg_rows(ref, k, j, n):
    rows = pl.ds(pl.multiple_of(j * NSEG, NSEG), NSEG)
    if k is None:
        return ref[rows, :]
    return ref[k, rows, :]


def _seg_store(ref, k, j, n, val):
    rows = pl.ds(pl.multiple_of(j * NSEG, NSEG), NSEG)
    if k is None:
        ref[rows, :] = val
    else:
        ref[k, rows, :] = val


def _seg_carries(er, ei, pr, pi, reverse):
    rows = lax.broadcasted_iota(jnp.int32, er.shape, 0)
    cr = jnp.zeros_like(er)
    ci = None if ei is None else jnp.zeros_like(er)
    order = range(NSEG - 2, -1, -1) if reverse else range(1, NSEG)
    shift = NSEG - 1 if reverse else 1
    for s in order:
        if ei is None:
            tr = er + pr * cr
            cr = jnp.where(rows == s, pltpu.roll(tr, shift, 0), cr)
        else:
            tr = er + pr * cr - pi * ci
            ti = ei + pr * ci + pi * cr
            cr = jnp.where(rows == s, pltpu.roll(tr, shift, 0), cr)
            ci = jnp.where(rows == s, pltpu.roll(ti, shift, 0), ci)
    return cr, ci


S5_K = 2


def s5_scan_fwd(bur, bui, ar, ai, L):
    n = L // NSEG
    nb = S5_N // LANE
    K = S5_K

    def body(br_ref, bi_ref, ar_ref, ai_ref, xr_ref, xi_ref):
        zero = jnp.zeros((NSEG, LANE), F32)
        A = [(jnp.broadcast_to(ar_ref[k], (NSEG, LANE)), jnp.broadcast_to(ai_ref[k], (NSEG, LANE))) for k in range(K)]

        def p1(j, st):
            new = []
            for k in range(K):
                sr, si, pr, pi = st[k]
                a_r, a_i = A[k]
                nr = a_r * sr - a_i * si + _seg_rows(br_ref, k, j, n)
                ni = a_r * si + a_i * sr + _seg_rows(bi_ref, k, j, n)
                _seg_store(xr_ref, k, j, n, nr)
                _seg_store(xi_ref, k, j, n, ni)
                new.append((nr, ni, a_r * pr - a_i * pi, a_r * pi + a_i * pr))
            return tuple(new)

        st = lax.fori_loop(0, n, p1, tuple((zero, zero, zero + 1.0, zero) for _ in range(K)))
        C = [_seg_carries(st[k][0], st[k][1], st[k][2], st[k][3], False) for k in range(K)]

        def p2(j, st):
            new = []
            for k in range(K):
                pr, pi = st[k]
                a_r, a_i = A[k]
                pr, pi = a_r * pr - a_i * pi, a_r * pi + a_i * pr
                cr, ci = C[k]
                _seg_store(xr_ref, k, j, n, _seg_rows(xr_ref, k, j, n) + pr * cr - pi * ci)
                _seg_store(xi_ref, k, j, n, _seg_rows(xi_ref, k, j, n) + pr * ci + pi * cr)
                new.append((pr, pi))
            return tuple(new)

        lax.fori_loop(0, n, p2, tuple((zero + 1.0, zero) for _ in range(K)))

    blk = pl.BlockSpec((K, L, LANE), lambda g: (g, 0, 0))
    ablk = pl.BlockSpec((K, 1, LANE), lambda g: (g, 0, 0))
    return pl.pallas_call(
        body, grid=(nb // K,), name="s5_scan_fwd",
        in_specs=[blk, blk, ablk, ablk], out_specs=[blk, blk],
        out_shape=[jax.ShapeDtypeStruct((nb, L, LANE), F32)] * 2,
        compiler_params=pltpu.CompilerParams(vmem_limit_bytes=VMEM_LIMIT),
    )(bur, bui, ar, ai)


def s5_scan_bwd(dxr, dxi, xr, xi, ar, ai, L):
    n = L // NSEG
    nb = S5_N // LANE
    K = S5_K

    def body(dr_ref, di_ref, xr_ref, xi_ref, ar_ref, ai_ref, gr_ref, gi_ref, dar_ref, dai_ref):
        zero = jnp.zeros((NSEG, LANE), F32)
        rows = lax.broadcasted_iota(jnp.int32, (NSEG, LANE), 0)
        A = [(jnp.broadcast_to(ar_ref[k], (NSEG, LANE)), -jnp.broadcast_to(ai_ref[k], (NSEG, LANE))) for k in range(K)]

        def p1(jj, st):
            j = n - 1 - jj
            new = []
            for k in range(K):
                sr, si, pr, pi = st[k]
                a_r, a_i = A[k]
                nr = a_r * sr - a_i * si + _seg_rows(dr_ref, k, j, n)
                ni = a_r * si + a_i * sr + _seg_rows(di_ref, k, j, n)
                _seg_store(gr_ref, k, j, n, nr)
                _seg_store(gi_ref, k, j, n, ni)
                new.append((nr, ni, a_r * pr - a_i * pi, a_r * pi + a_i * pr))
            return tuple(new)

        st = lax.fori_loop(0, n, p1, tuple((zero, zero, zero + 1.0, zero) for _ in range(K)))
        C = [_seg_carries(st[k][0], st[k][1], st[k][2], st[k][3], True) for k in range(K)]
        xb = [(jnp.where(rows == 0, 0.0, pltpu.roll(_seg_rows(xr_ref, k, n - 1, n), 1, 0)),
               jnp.where(rows == 0, 0.0, pltpu.roll(_seg_rows(xi_ref, k, n - 1, n), 1, 0))) for k in range(K)]

        def p2(jj, st):
            j = n - 1 - jj
            jp = jnp.maximum(j - 1, 0)
            new = []
            for k in range(K):
                pr, pi, acr, aci = st[k]
                a_r, a_i = A[k]
                pr, pi = a_r * pr - a_i * pi, a_r * pi + a_i * pr
                cr, ci = C[k]
                g_r = _seg_rows(gr_ref, k, j, n) + pr * cr - pi * ci
                g_i = _seg_rows(gi_ref, k, j, n) + pr * ci + pi * cr
                _seg_store(gr_ref, k, j, n, g_r)
                _seg_store(gi_ref, k, j, n, g_i)
                xpr = jnp.where(j == 0, xb[k][0], _seg_rows(xr_ref, k, jp, n))
                xpi = jnp.where(j == 0, xb[k][1], _seg_rows(xi_ref, k, jp, n))
                new.append((pr, pi, acr + g_r * xpr + g_i * xpi, aci + g_i * xpr - g_r * xpi))
            return tuple(new)

        st = lax.fori_loop(0, n, p2, tuple((zero + 1.0, zero, zero, zero) for _ in range(K)))
        for k in range(K):
            dar_ref[k] = jnp.sum(st[k][2], axis=0, keepdims=True)
            dai_ref[k] = jnp.sum(st[k][3], axis=0, keepdims=True)

    blk = pl.BlockSpec((K, L, LANE), lambda g: (g, 0, 0))
    ablk = pl.BlockSpec((K, 1, LANE), lambda g: (g, 0, 0))
    return pl.pallas_call(
        body, grid=(nb // K,), name="s5_scan_bwd",
        in_specs=[blk, blk, blk, blk, ablk, ablk], out_specs=[blk, blk, ablk, ablk],
        out_shape=[jax.ShapeDtypeStruct((nb, L, LANE), F32)] * 2 + [jax.ShapeDtypeStruct((nb, 1, LANE), F32)] * 2,
        compiler_params=pltpu.CompilerParams(vmem_limit_bytes=VMEM_LIMIT),
    )(dxr, dxi, xr, xi, ar, ai)


def rg_scan_fwd(a, b, L):
    n = L // NSEG

    def body(a_ref, b_ref, h_ref):
        zero = jnp.zeros((NSEG, LANE), F32)

        def p1(j, st):
            h, p = st
            aj = _seg_rows(a_ref, None, j, n)
            h = aj * h + _seg_rows(b_ref, None, j, n)
            _seg_store(h_ref, None, j, n, h)
            return h, aj * p

        e, pe = lax.fori_loop(0, n, p1, (zero, zero + 1.0))
        c, _ = _seg_carries(e, None, pe, None, False)

        def p2(j, p):
            p = _seg_rows(a_ref, None, j, n) * p
            _seg_store(h_ref, None, j, n, _seg_rows(h_ref, None, j, n) + p * c)
            return p

        lax.fori_loop(0, n, p2, zero + 1.0)

    blk = pl.BlockSpec((L, LANE), lambda g: (0, g))
    return pl.pallas_call(
        body, grid=(BW // LANE,), name="rg_scan_fwd", in_specs=[blk, blk], out_specs=blk,
        out_shape=jax.ShapeDtypeStruct((L, BW), F32),
        compiler_params=pltpu.CompilerParams(vmem_limit_bytes=VMEM_LIMIT),
    )(a, b)


def rg_scan_bwd(a, h, dh, L):
    n = L // NSEG

    def body(a_ref, h_ref, dh_ref, da_ref, db_ref):
        zero = jnp.zeros((NSEG, LANE), F32)
        rows = lax.broadcasted_iota(jnp.int32, (NSEG, LANE), 0)
        a_edge = jnp.where(rows == NSEG - 1, 0.0, pltpu.roll(_seg_rows(a_ref, None, 0, n), NSEG - 1, 0))
        h_edge = jnp.where(rows == 0, 0.0, pltpu.roll(_seg_rows(h_ref, None, n - 1, n), 1, 0))

        def mult(j):
            return jnp.where(j == n - 1, a_edge, _seg_rows(a_ref, None, jnp.minimum(j + 1, n - 1), n))

        def p1(jj, st):
            j = n - 1 - jj
            g, p = st
            m = mult(j)
            g = m * g + _seg_rows(dh_ref, None, j, n)
            _seg_store(db_ref, None, j, n, g)
            return g, m * p

        e, pe = lax.fori_loop(0, n, p1, (zero, zero + 1.0))
        c, _ = _seg_carries(e, None, pe, None, True)

        def p2(jj, p):
            j = n - 1 - jj
            p = mult(j) * p
            g = _seg_rows(db_ref, None, j, n) + p * c
            _seg_store(db_ref, None, j, n, g)
            hp = jnp.where(j == 0, h_edge, _seg_rows(h_ref, None, jnp.maximum(j - 1, 0), n))
            _seg_store(da_ref, None, j, n, g * hp)
            return p

        lax.fori_loop(0, n, p2, zero + 1.0)

    blk = pl.BlockSpec((L, LANE), lambda g: (0, g))
    return pl.pallas_call(
        body, grid=(BW // LANE,), name="rg_scan_bwd", in_specs=[blk, blk, blk], out_specs=[blk, blk],
        out_shape=[jax.ShapeDtypeStruct((L, BW), F32)] * 2,
        compiler_params=pltpu.CompilerParams(vmem_limit_bytes=VMEM_LIMIT),
    )(a, h, dh)


def _hg_consts(C):
    t = lax.broadcasted_iota(jnp.int32, (C, C), 0)
    s = lax.broadcasted_iota(jnp.int32, (C, C), 1)
    tril = (s <= t).astype(F32)
    diag = (s == t).astype(F32)
    levels = []
    k = 1
    while (1 << k) <= C:
        m = 1 << (k - 1)
        same = (t >> k) == (s >> k)
        t_right = ((t >> (k - 1)) & 1) == 1
        s_left = ((s >> (k - 1)) & 1) == 0
        mask = jnp.logical_and(same, jnp.logical_and(t_right, s_left)).astype(F32)
        bnd = ((t >> k) << k) + (m - 1)
        levels.append((mask, (s <= bnd).astype(F32)))
        k += 1
    return tril, diag, levels


def hg_chunk(st, q, z, v, lb):
    C = q.shape[0]
    tril, diag, levels = _hg_consts(C)
    sig = jax.nn.sigmoid(z)
    lf = jnp.log(lb + (1.0 - lb) * sig)
    k = (1.0 - lb) * jax.nn.sigmoid(-z)
    qh = jax.nn.silu(q)
    b = mm_exact(tril, lf)
    blast = jnp.sum(lf, axis=0, keepdims=True)
    qe = qh * jnp.exp(b)
    kd = k * jnp.exp(blast - b)
    scaled = []
    for level, (_, sel) in enumerate(levels):
        size = 2 << level
        if size >= NSEG:
            b3 = b.reshape(C // size, size, b.shape[1])
            ref = jnp.broadcast_to(b3[:, size // 2 - 1:size // 2, :], b3.shape).reshape(b.shape)
        else:
            ref = mm_exact(sel, lf)
        scaled.append((qh * jnp.exp(jnp.minimum(b - ref, 0.0)), k * jnp.exp(jnp.minimum(ref - b, 0.0))))
    outs, news = [], []
    for h in range(HG_HEADS):
        sl = slice(h * HG_D, (h + 1) * HG_D)
        st_h = st[h * HG_D:(h + 1) * HG_D, :]
        sc = diag * mma_nt(qh[:, sl], k[:, sl])
        for (mask, _), (qt, kt) in zip(levels, scaled):
            sc = sc + mask * mma_nt(qt[:, sl], kt[:, sl])
        outs.append(mma_nt(qe[:, sl], st_h) + mma_nn(sc, v[:, sl]))
        news.append(st_h * jnp.exp(blast[:, sl]) + mma_tn(v[:, sl], kd[:, sl]))
    return jnp.concatenate(news, axis=0), jnp.concatenate(outs, axis=1)


def hg_fwd(qzv, lb, L):
    C = HG_CHUNK
    nc = L // C

    def body(q_ref, z_ref, v_ref, lb_ref, o_ref, sst_ref, st_ref):
        @pl.when(pl.program_id(0) == 0)
        def _():
            st_ref[...] = jnp.zeros_like(st_ref)

        st = st_ref[...]
        sst_ref[...] = st
        new, o = hg_chunk(st, q_ref[...], z_ref[...], v_ref[...], lb_ref[...])
        st_ref[...] = new
        o_ref[...] = o

    col = lambda cb: pl.BlockSpec((C, BW), functools.partial(lambda c, cb: (c, cb), cb=cb))
    return pl.pallas_call(
        body, grid=(nc,), name="hg_fwd",
        in_specs=[col(0), col(1), col(2), pl.BlockSpec((1, BW), lambda c: (0, 0))],
        out_specs=[pl.BlockSpec((C, BW), lambda c: (c, 0)), pl.BlockSpec((None, BW, HG_D), lambda c: (c, 0, 0))],
        out_shape=[jax.ShapeDtypeStruct((L, BW), F32), jax.ShapeDtypeStruct((nc, BW, HG_D), F32)],
        scratch_shapes=[pltpu.VMEM((BW, HG_D), F32)],
        compiler_params=pltpu.CompilerParams(vmem_limit_bytes=VMEM_LIMIT, dimension_semantics=("arbitrary",)),
    )(qzv, qzv, qzv, lb)


def hg_bwd(qzv, lb, sst, do, L):
    C = HG_CHUNK
    nc = L // C

    def body(q_ref, z_ref, v_ref, lb_ref, sst_ref, do_ref, dq_ref, dz_ref, dv_ref, dlb_ref, dst_ref):
        @pl.when(pl.program_id(0) == 0)
        def _():
            dst_ref[...] = jnp.zeros_like(dst_ref)
            dlb_ref[...] = jnp.zeros_like(dlb_ref)

        _, vjp = jax.vjp(hg_chunk, sst_ref[...], q_ref[...], z_ref[...], v_ref[...], lb_ref[...])
        dst, dq, dz, dv, dlb = vjp((dst_ref[...], do_ref[...]))
        dst_ref[...] = dst
        dq_ref[...] = dq
        dz_ref[...] = dz
        dv_ref[...] = dv
        dlb_ref[...] += dlb

    col = lambda cb: pl.BlockSpec((C, BW), functools.partial(lambda c, cb: (nc - 1 - c, cb), cb=cb))
    rev = pl.BlockSpec((C, BW), lambda c: (nc - 1 - c, 0))
    return pl.pallas_call(
        body, grid=(nc,), name="hg_bwd",
        in_specs=[col(0), col(1), col(2), pl.BlockSpec((1, BW), lambda c: (0, 0)),
                  pl.BlockSpec((None, BW, HG_D), lambda c: (nc - 1 - c, 0, 0)), rev],
        out_specs=[rev, rev, rev, pl.BlockSpec((1, BW), lambda c: (0, 0))],
        out_shape=[jax.ShapeDtypeStruct((L, BW), F32)] * 3 + [jax.ShapeDtypeStruct((1, BW), F32)],
        scratch_shapes=[pltpu.VMEM((BW, HG_D), F32)],
        compiler_params=pltpu.CompilerParams(vmem_limit_bytes=VMEM_LIMIT, dimension_semantics=("arbitrary",)),
    )(qzv, qzv, qzv, lb, sst, do)


def _shift_down(x, d, rows, L):
    if d == 0:
        return x
    wrapped = jnp.where((rows & (NSEG - 1)) == 0, 0.0, pltpu.roll(x, NSEG * d + 1, 0))
    return jnp.where(rows < NSEG * d, wrapped, pltpu.roll(x, NSEG * d, 0))


def _shift_up(x, d, rows, L):
    if d == 0:
        return x
    wrapped = jnp.where((rows & (NSEG - 1)) == NSEG - 1, 0.0, pltpu.roll(x, L - (NSEG * d + 1), 0))
    return jnp.where(rows >= L - NSEG * d, wrapped, pltpu.roll(x, L - NSEG * d, 0))


def conv_fwd(proj, w, b, L):
    def body(x_ref, w_ref, b_ref, o_ref):
        x = x_ref[...]
        rows = lax.broadcasted_iota(jnp.int32, x.shape, 0)
        acc = jnp.broadcast_to(b_ref[...], x.shape)
        for k in range(CONV_W):
            acc = acc + w_ref[pl.ds(k, 1), :] * _shift_down(x, CONV_W - 1 - k, rows, L)
        o_ref[...] = acc

    nl = BW // LANE
    return pl.pallas_call(
        body, grid=(nl,), name="conv_fwd",
        in_specs=[pl.BlockSpec((L, LANE), lambda g: (0, 5 * nl + g)), pl.BlockSpec((CONV_W, LANE), lambda g: (0, g)),
                  pl.BlockSpec((1, LANE), lambda g: (0, g))],
        out_specs=pl.BlockSpec((L, LANE), lambda g: (0, g)),
        out_shape=jax.ShapeDtypeStruct((L, BW), F32),
        compiler_params=pltpu.CompilerParams(vmem_limit_bytes=VMEM_LIMIT),
    )(proj, w, b)


def conv_bwd(proj, w, dxc, L):
    def body(x_ref, w_ref, d_ref, dx_ref, dw_ref, db_ref):
        x, d = x_ref[...], d_ref[...]
        rows = lax.broadcasted_iota(jnp.int32, x.shape, 0)
        acc = jnp.zeros_like(x)
        for k in range(CONV_W):
            acc = acc + w_ref[pl.ds(k, 1), :] * _shift_up(d, CONV_W - 1 - k, rows, L)
            dw_ref[pl.ds(k, 1), :] = jnp.sum(d * _shift_down(x, CONV_W - 1 - k, rows, L), axis=0, keepdims=True)
        dx_ref[...] = acc
        db_ref[...] = jnp.sum(d, axis=0, keepdims=True)

    nl = BW // LANE
    blk = pl.BlockSpec((L, LANE), lambda g: (0, g))
    return pl.pallas_call(
        body, grid=(nl,), name="conv_bwd",
        in_specs=[pl.BlockSpec((L, LANE), lambda g: (0, 5 * nl + g)), pl.BlockSpec((CONV_W, LANE), lambda g: (0, g)), blk],
        out_specs=[blk, pl.BlockSpec((CONV_W, LANE), lambda g: (0, g)), pl.BlockSpec((1, LANE), lambda g: (0, g))],
        out_shape=[jax.ShapeDtypeStruct((L, BW), F32), jax.ShapeDtypeStruct((CONV_W, BW), F32),
                   jax.ShapeDtypeStruct((1, BW), F32)],
        compiler_params=pltpu.CompilerParams(vmem_limit_bytes=VMEM_LIMIT),
    )(proj, w, dxc)


def loss_fwd_bwd(x, fw, target, L, tm):
    def fn(x, fw, t):
        err = jnp.square(_rms(x, fw) - t)
        return jnp.sum(0.5 * jnp.mean(err, axis=-1, keepdims=True), axis=0, keepdims=True)

    def body(x_ref, fw_ref, t_ref, l_ref, dx_ref, dfw_ref):
        i = pl.program_id(0)
        t = t_ref[...]
        val, vjp = jax.vjp(lambda x, fw: fn(x, fw, t), x_ref[...], fw_ref[...])
        dx, dfw = vjp(jnp.ones((1, 1), F32))
        dx_ref[...] = dx

        @pl.when(i == 0)
        def _():
            l_ref[...] = jnp.zeros_like(l_ref)
            dfw_ref[...] = jnp.zeros_like(dfw_ref)

        l_ref[...] += jnp.broadcast_to(val, l_ref.shape)
        dfw_ref[...] += dfw

    row = pl.BlockSpec((tm, D_MODEL), lambda i: (i, 0))
    vec = pl.BlockSpec((1, D_MODEL), lambda i: (0, 0))
    return pl.pallas_call(
        body, grid=(L // tm,), name="loss_fwd_bwd", in_specs=[row, vec, row],
        out_specs=[pl.BlockSpec((1, LANE), lambda i: (0, 0)), row, vec],
        out_shape=[jax.ShapeDtypeStruct((1, LANE), F32), jax.ShapeDtypeStruct((L, D_MODEL), F32),
                   jax.ShapeDtypeStruct((1, D_MODEL), F32)],
        compiler_params=pltpu.CompilerParams(vmem_limit_bytes=VMEM_LIMIT, dimension_semantics=("arbitrary",)),
    )(x, fw, target)


def adamw(w, g, m, v):
    rows, cols = w.shape
    tr = _row_tile(rows, cols, budget=1024 * 1024)
    c1 = 1.0 - ADAM_B1 ** ADAM_STEP
    c2 = 1.0 - ADAM_B2 ** ADAM_STEP

    def body(w_ref, g_ref, m_ref, v_ref, d_ref, nm_ref, nv_ref):
        g = g_ref[...]
        nm = ADAM_B1 * m_ref[...] + (1.0 - ADAM_B1) * g
        nv = ADAM_B2 * v_ref[...] + (1.0 - ADAM_B2) * jnp.square(g)
        d_ref[...] = -ADAM_LR * ((nm / c1) / (jnp.sqrt(nv / c2) + ADAM_EPS) + ADAM_WD * w_ref[...])
        nm_ref[...] = nm
        nv_ref[...] = nv

    blk = pl.BlockSpec((tr, cols), lambda i: (i, 0))
    return pl.pallas_call(
        body, grid=(rows // tr,), name="adamw", in_specs=[blk] * 4, out_specs=[blk] * 3,
        out_shape=[jax.ShapeDtypeStruct((rows, cols), F32)] * 3,
    )(w, g, m, v)


def s5_prep(lam_re, lam_im, log_dt, b_re, b_im, c_re, c_im):
    lr = jnp.minimum(lam_re, -1e-4)
    li = lam_im
    dt = jnp.exp(log_dt)[:, None]
    mag = jnp.exp(lr * dt)
    ar = mag * jnp.cos(li * dt)
    ai = mag * jnp.sin(li * dt)
    den = lr * lr + li * li
    fr = ((ar - 1.0) * lr + ai * li) / den
    fi = (ai * lr - (ar - 1.0) * li) / den
    bbr = fr[..., None] * b_re - fi[..., None] * b_im
    bbi = fr[..., None] * b_im + fi[..., None] * b_re
    hg = S5_GROUPS // 2
    emb_b = lambda bb: _block_diag(bb.transpose(0, 2, 1).reshape(hg * S5_GROUP, S5_STATE), hg)
    emb_c = lambda cc: _block_diag(cc.transpose(0, 2, 1).reshape(hg * S5_STATE, S5_GROUP), hg)
    bsub = jnp.stack([emb_b(bbr[:hg]), emb_b(bbr[hg:]), emb_b(bbi[:hg]), emb_b(bbi[hg:])])
    csub = jnp.stack([jnp.concatenate([emb_c(c_re[:hg]), -emb_c(c_im[:hg])], axis=0),
                      jnp.concatenate([emb_c(c_re[hg:]), -emb_c(c_im[hg:])], axis=0)])
    nb = S5_N // LANE
    return ar.reshape(nb, 1, LANE), ai.reshape(nb, 1, LANE), bsub, csub


def _block_diag(stacked, groups):
    rows, c = stacked.shape
    r = rows // groups
    row_g = jnp.arange(rows)[:, None] // r
    col_g = jnp.arange(groups * c)[None, :] // c
    return jnp.where(row_g == col_g, jnp.tile(stacked, (1, groups)), 0.0)


def rg_prep(w):
    return _block_diag(w.reshape(BW, RG_BLOCK), RG_BLOCKS)


def hg_prep(logits):
    p = jax.nn.softmax(logits, axis=0)
    return jnp.cumsum(p, axis=0) - p[0]


def _head_mean_matrix():
    r = jnp.arange(BW) // HG_D
    return (r[:, None] == r[None, :]).astype(F32) / HG_D


def _to_segment_order(a):
    L = a.shape[0]
    return a.reshape(NSEG, L // NSEG, -1).transpose(1, 0, 2).reshape(a.shape)


def _to_time_order(a):
    L = a.shape[0]
    return a.reshape(L // NSEG, NSEG, -1).transpose(1, 0, 2).reshape(a.shape)


def _const(*idx):
    return lambda s, i: idx


def _rows(cb=0):
    return lambda s, i: (i, cb)


def _sum_parts(name, first, parts, shape):
    return add_n(name, [(first, ())] + [(parts, (s,)) for s in range(NSH)], shape)


def _ffn_weight_specs(l, j):
    F = D_FF // NSH
    one = pl.Buffered(1)
    return [pl.BlockSpec((None, NSH, D_MODEL, F), lambda i: (j, 0, 0, 0), pipeline_mode=one),
            pl.BlockSpec((None, NSH, D_MODEL, F), lambda i: (j, 0, 0, 0), pipeline_mode=one),
            pl.BlockSpec((None, NSH, F, D_MODEL), lambda i: (j, 0, 0, 0), pipeline_mode=one)]


def ffn_fwd(name, x, W, l, j, k, L, tm):
    D, F = D_MODEL, D_FF // NSH

    def body(x_ref, nw_ref, wg_ref, wu_ref, wd_ref, y_ref, g_ref, u_ref):
        x = x_ref[...]
        h = _rms(x, nw_ref[...]).astype(MMT)
        y = x
        for s in range(NSH):
            g = _dg(h, wg_ref[s], 1, 0)
            u = _dg(h, wu_ref[s], 1, 0)
            g_ref[s] = g.astype(g_ref.dtype)
            u_ref[s] = u.astype(u_ref.dtype)
            y = y + 0.5 * _dg((jax.nn.silu(g) * u).astype(MMT), wd_ref[s], 1, 0)
        y_ref[...] = y

    row = pl.BlockSpec((tm, D), lambda i: (i, 0))
    act = pl.BlockSpec((NSH, tm, F), lambda i: (0, i, 0))
    return pl.pallas_call(
        body, grid=(L // tm,), name=name,
        in_specs=[row, pl.BlockSpec((None, None, 1, D), lambda i: (l, k, 0, 0))] + _ffn_weight_specs(l, j),
        out_specs=[row, act, act],
        out_shape=[jax.ShapeDtypeStruct((L, D), F32), jax.ShapeDtypeStruct((NSH, L, F), MMT),
                   jax.ShapeDtypeStruct((NSH, L, F), MMT)],
        compiler_params=pltpu.CompilerParams(vmem_limit_bytes=VMEM_LIMIT, dimension_semantics=("arbitrary",)),
    )(x, W["nw"], W["L"][l]["wg"], W["L"][l]["wu"], W["L"][l]["wd"])


def ffn_bwd(name, x, g, u, dy, W, bufs, l, j, k, L, tm):
    D, F = D_MODEL, D_FF // NSH
    tm = min(TM_WGRAD, L)

    def body(x_ref, nw_ref, dy_ref, g_ref, u_ref, wg_ref, wu_ref, wd_ref, *rest):
        part_ref, dnw_ref, dwg_ref, dwu_ref, dwd_ref = rest[-5:]
        s, i = pl.program_id(0), pl.program_id(1)
        x, nw = x_ref[...], nw_ref[...]
        r = lax.rsqrt(jnp.mean(x * x, axis=-1, keepdims=True) + EPS)
        xhat = x * r
        h = (xhat * nw).astype(MMT)
        half_dy = (0.5 * dy_ref[...]).astype(MMT)
        gs, us = g_ref[...].astype(F32), u_ref[...].astype(F32)
        sig = jax.nn.sigmoid(gs)
        act = gs * sig
        da = _dg(half_dy, wd_ref[...], 1, 1)
        du = (da * act).astype(MMT)
        dg = (da * us * (sig * (1.0 + gs * (1.0 - sig)))).astype(MMT)
        dh = _dg(dg, wg_ref[...], 1, 1) + _dg(du, wu_ref[...], 1, 1)
        dxh = dh * nw
        part_ref[...] = r * (dxh - xhat * jnp.mean(dxh * xhat, axis=-1, keepdims=True))
        grads = (_dg(h, dg, 0, 0), _dg(h, du, 0, 0), _dg((act * us).astype(MMT), half_dy, 0, 0))
        dnw = jnp.sum(dh * xhat, axis=0, keepdims=True)
        first = jnp.logical_and(s == 0, i == 0)
        for ref, val, start in zip((dwg_ref, dwu_ref, dwd_ref, dnw_ref), grads + (dnw,), (i == 0, i == 0, i == 0, first)):
            @pl.when(start)
            def _(ref=ref, val=val):
                ref[...] = val

            @pl.when(jnp.logical_not(start))
            def _(ref=ref, val=val):
                ref[...] += val

    row = pl.BlockSpec((tm, D), lambda s, i: (i, 0))
    act = pl.BlockSpec((None, tm, F), lambda s, i: (s, i, 0))
    wsp = lambda r, c: pl.BlockSpec((None, None, r, c), lambda s, i: (j, s, 0, 0))
    gsp = lambda r, c: pl.BlockSpec((None, None, r, c), lambda s, i: (0, s, 0, 0))
    part, dnw, bufs[("ffn_gate", l, j)], bufs[("ffn_up", l, j)], bufs[("ffn_down", l, j)] = pl.pallas_call(
        body, grid=(NSH, L // tm), name=name,
        in_specs=[row, pl.BlockSpec((None, None, 1, D), lambda s, i: (l, k, 0, 0)), row, act, act,
                  wsp(D, F), wsp(D, F), wsp(F, D)],
        out_specs=[pl.BlockSpec((None, tm, D), lambda s, i: (s, i, 0)), pl.BlockSpec((1, D), lambda s, i: (0, 0)),
                   gsp(D, F), gsp(D, F), gsp(F, D)],
        out_shape=[jax.ShapeDtypeStruct((NSH, L, D), F32), jax.ShapeDtypeStruct((1, D), F32)]
        + [jax.ShapeDtypeStruct((1, NSH, D, F), F32)] * 2 + [jax.ShapeDtypeStruct((1, NSH, F, D), F32)],
        compiler_params=pltpu.CompilerParams(vmem_limit_bytes=VMEM_LIMIT, dimension_semantics=("arbitrary", "arbitrary")),
    )(x, W["nw"], dy, g, u, W["L"][l]["wg"], W["L"][l]["wu"], W["L"][l]["wd"])
    return _sum_parts(name + "_dx", dy, part, (L, D)), dnw


def layer_fwd(l, x0, W, P, L, tm):
    D = D_MODEL
    tmm = tm
    tm = min(TM_FWD, L)
    n_i = L // tm
    x1, g0, u0 = ffn_fwd(f"ffn_fwd_{l}0", x0, W, l, 0, 0, L, tm)
    proj = tile_fwd(
        lambda x, nw, win, s: pre_core(x, nw, win), f"pre_fwd_{l}", n_i, NSH,
        [(x1, (tm, D), _rows()), (W["nw"], (None, None, 1, D), _const(l, 1, 0, 0)),
         (W["L"][l]["win"], (None, D, IN_TOTAL // NSH), lambda s, i: (s, 0, 0))],
        [((L, IN_TOTAL), F32, (tm, IN_TOTAL // NSH), lambda s, i: (i, s), False)], s_outer=True)[0]
    nb = S5_N // LANE
    blk3 = lambda s, i: (0, i, 0)
    bur, bui = tile_fwd(
        lambda *a: s5_pre_core(*a[:-1]), f"s5pre_fwd_{l}", n_i, 1,
        [(proj, (tm, BW), _rows(0))] + [(P["bsub"], (None, None, BW // 2, S5_N // 2), _const(l, q, 0, 0)) for q in range(4)],
        [((nb, L, LANE), F32, (nb, tm, LANE), blk3, False)] * 2)
    xr, xi = s5_scan_fwd(bur, bui, P["ar"][l], P["ai"][l], L)
    qzv = _to_time_order(proj[:, BW:4 * BW])
    o_t, sst = hg_fwd(qzv, P["lb"][l], L)
    o = _to_segment_order(o_t)
    xc = conv_fwd(proj, W["convw"][l], P["convb"][l], L)
    vec = (None, 1, BW)
    a, b = tile_fwd(
        lambda xc, wa, ba, wx, bx, lam, s: gates_core(xc, wa, ba, wx, bx, lam), f"gates_fwd_{l}", n_i, 1,
        [(xc, (tm, BW), _rows()), (P["wa"], (None, BW, BW), _const(l, 0, 0)), (P["ba"], vec, _const(l, 0, 0)),
         (P["wx"], (None, BW, BW), _const(l, 0, 0)), (P["bx"], vec, _const(l, 0, 0)), (P["lam"], vec, _const(l, 0, 0))],
        [((L, BW), F32, (tm, BW), _rows(), False)] * 2)
    hs = rg_scan_fwd(a, b, L)
    ya, yb, yc = tile_fwd(
        lambda *a: mid_core(*a[:-1]), f"mid_fwd_{l}", L // tmm, 1,
        [(xr, (nb, tmm, LANE), blk3), (xi, (nb, tmm, LANE), blk3), (proj, (tmm, BW), _rows(0)), (o, (tmm, BW), _rows()),
         (proj, (tmm, BW), _rows(4)), (hs, (tmm, BW), _rows()), (proj, (tmm, BW), _rows(6)),
         (P["hmat"], (BW, BW), _const(0, 0)), (P["csub"], (None, None, S5_N, BW // 2), _const(l, 0, 0, 0)),
         (P["csub"], (None, None, S5_N, BW // 2), _const(l, 1, 0, 0)), (P["d"], vec, _const(l, 0, 0)),
         (W["L"][l]["gluw"], (BW, BW), _const(0, 0)), (P["glub"], vec, _const(l, 0, 0)), (P["hgw"], vec, _const(l, 0, 0))],
        [((L, BW), F32, (tmm, BW), _rows(), False)] * 3)
    x2 = tile_fwd(
        lambda x, *rest: (x + merge_core(*rest[:-1])[0],), f"merge_fwd_{l}", n_i, 1,
        [(x1, (tm, D), _rows()), (ya, (tm, BW), _rows()), (yb, (tm, BW), _rows()), (yc, (tm, BW), _rows())]
        + [(proj, (tm, BW), _rows(7 + k)) for k in range(6)]
        + [(W["L"][l]["pfull"], (3, BW, D), _const(0, 0, 0)), (W["L"][l]["woutfull"], (D, D), _const(0, 0))],
        [((L, D), F32, (tm, D), _rows(), False)])[0]
    x3, g1, u1 = ffn_fwd(f"ffn_fwd_{l}1", x2, W, l, 1, 2, L, tm)
    saved = dict(x0=x0, x1=x1, x2=x2, proj=proj, xr=xr, xi=xi, o=o, sst=sst, xc=xc, a=a, hs=hs, ya=ya, yb=yb, yc=yc,
                 qzv=qzv, g0=g0, u0=u0, g1=g1, u1=u1)
    return x3, saved


def layer_bwd(l, dx3, sv, W, P, bufs, L, tm, ready=lambda l, group: None):
    D = D_MODEL
    n_i = L // tm
    nb = S5_N // LANE
    dq = D // NSH
    vec = (None, 1, BW)
    vout = ((1, BW), (1, BW), _const(0, 0), "acc_all")
    blk3 = lambda s, i: (0, i, 0)
    small = {}
    proj = sv["proj"]

    dx2, dnw2 = ffn_bwd(f"ffn_bwd_{l}1", sv["x2"], sv["g1"], sv["u1"], dx3, W, bufs, l, 1, 2, L, tm)
    ready(l, "ffn1")

    rw256 = ((L, BW), (tm, BW), _rows(), "write")
    res = tile_bwd(
        merge_core, f"merge_bwd_{l}", n_i, 1,
        [(sv["ya"], (tm, BW), _rows(), "r"), (sv["yb"], (tm, BW), _rows(), "r"), (sv["yc"], (tm, BW), _rows(), "r")]
        + [(proj, (tm, BW), _rows(7 + k), "r") for k in range(6)]
        + [(W["L"][l]["pfull"], (3, BW, D), _const(0, 0, 0), "w"), (W["L"][l]["woutfull"], (D, D), _const(0, 0), "w")],
        [(dx2, (tm, D), _rows())],
        [rw256] * 9
        + [((3, BW, D), (3, BW, D), _const(0, 0, 0), "acc_all"), ((D, D), (D, D), _const(0, 0), "acc_all")])
    dya, dyb, dyc = res[:3]
    dgm = res[3:9]
    bufs[("branch_proj", l)], bufs[("w_out", l)] = res[9:]
    ready(l, "merge")

    tmm = tm
    rw = ((L, BW), (tmm, BW), _rows(), "write")
    xw = ((nb, L, LANE), (nb, tmm, LANE), blk3, "write")
    res = tile_bwd(
        mid_core, f"mid_bwd_{l}", L // tmm, 1,
        [(sv["xr"], (nb, tmm, LANE), blk3, "r"), (sv["xi"], (nb, tmm, LANE), blk3, "r"), (proj, (tmm, BW), _rows(0), "r"),
         (sv["o"], (tmm, BW), _rows(), "r"), (proj, (tmm, BW), _rows(4), "r"), (sv["hs"], (tmm, BW), _rows(), "r"),
         (proj, (tmm, BW), _rows(6), "r"), (P["hmat"], (BW, BW), _const(0, 0), "c"),
         (P["csub"], (None, None, S5_N, BW // 2), _const(l, 0, 0, 0), "w"),
         (P["csub"], (None, None, S5_N, BW // 2), _const(l, 1, 0, 0), "w"), (P["d"], vec, _const(l, 0, 0), "p"),
         (W["L"][l]["gluw"], (BW, BW), _const(0, 0), "w"), (P["glub"], vec, _const(l, 0, 0), "p"),
         (P["hgw"], vec, _const(l, 0, 0), "p")],
        [(dya, (tmm, BW), _rows()), (dyb, (tmm, BW), _rows()), (dyc, (tmm, BW), _rows())],
        [xw, xw, rw, rw, rw, rw, rw,
         ((DEPTH, S5_N, BW // 2), (None, S5_N, BW // 2), _const(l, 0, 0), "acc_all", bufs.get("csub0")),
         ((DEPTH, S5_N, BW // 2), (None, S5_N, BW // 2), _const(l, 0, 0), "acc_all", bufs.get("csub1")), vout,
         ((BW, BW), (BW, BW), _const(0, 0), "acc_all"), vout, vout])
    dxr, dxi, du_skip, do, dg_b, dhs, dgate_c, bufs["csub0"], bufs["csub1"], dd, bufs[("s5_glu_w", l)], dglub, dhgw = res
    small["s5_d"], small["s5_glu_b"], small["hg_norm_w"] = dd[0], dglub[0], dhgw[0]
    ready(l, "mid")

    da, db = rg_scan_bwd(sv["a"], sv["hs"], dhs, L)
    wmat = lambda key: ((DEPTH, BW, BW), (None, BW, BW), _const(l, 0, 0), "acc_all", bufs.get(key))
    res = tile_bwd(
        gates_core, f"gates_bwd_{l}", n_i, 1,
        [(sv["xc"], (tm, BW), _rows(), "r"), (P["wa"], (None, BW, BW), _const(l, 0, 0), "w"), (P["ba"], vec, _const(l, 0, 0), "p"),
         (P["wx"], (None, BW, BW), _const(l, 0, 0), "w"), (P["bx"], vec, _const(l, 0, 0), "p"), (P["lam"], vec, _const(l, 0, 0), "p")],
        [(da, (tm, BW), _rows()), (db, (tm, BW), _rows())],
        [((L, BW), (tm, BW), _rows(), "write"), wmat("wa"), vout, wmat("wx"), vout, vout])
    dxc, bufs["wa"], dba, bufs["wx"], dbx, dlam = res
    small["rg_ba"], small["rg_bx"], small["rg_lambda"] = dba[0], dbx[0], dlam[0]
    dx_c, dconvw, dconvb = conv_bwd(proj, W["convw"][l], dxc, L)
    small["rg_conv_w"], small["rg_conv_b"] = dconvw, dconvb[0]

    dq_b, dz_b, dv_b, dlb = hg_bwd(sv["qzv"], P["lb"][l], sv["sst"], _to_time_order(do), L)
    dq_b, dz_b, dv_b = [_to_segment_order(a) for a in (dq_b, dz_b, dv_b)]

    gr, gi, dar, dai = s5_scan_bwd(dxr, dxi, sv["xr"], sv["xi"], P["ar"][l], P["ai"][l], L)
    bblk = (None, None, BW // 2, S5_N // 2)
    res = tile_bwd(
        s5_pre_core, f"s5pre_bwd_{l}", n_i, 1,
        [(proj, (tm, BW), _rows(0), "r")] + [(P["bsub"], bblk, _const(l, q, 0, 0), "w") for q in range(4)],
        [(gr, (nb, tm, LANE), blk3), (gi, (nb, tm, LANE), blk3)],
        [((L, BW), (tm, BW), _rows(), "write")]
        + [((DEPTH, BW // 2, S5_N // 2), bblk[1:], _const(l, 0, 0), "acc_all", bufs.get(f"bsub{q}")) for q in range(4)])
    du_pre = res[0]
    for q in range(4):
        bufs[f"bsub{q}"] = res[1 + q]
    du_a = add_n(f"du_a_{l}", [(du_skip, ()), (du_pre, ())], (L, BW))
    prep_ct = dict(dar=dar, dai=dai, dlb=dlb)

    pieces = [du_a, dq_b, dz_b, dv_b, dg_b, dx_c, dgate_c, *dgm]
    per_piece, per_shard = BW // LANE, IN_TOTAL // NSH // LANE
    part, dnw1 = None, []
    tmw = min(TM_WGRAD, L)
    for s in range(NSH):
        groups = [(pieces[g // per_piece], (tmw, LANE), _rows(g % per_piece))
                  for g in range(s * per_shard, (s + 1) * per_shard)]
        part, dnw_s, bufs[("w_in", l)] = tile_bwd(
            pre_core, f"pre_bwd_{l}{s}", L // tmw, 1,
            [(sv["x1"], (tmw, D), _rows(), "r"), (W["nw"], (None, None, 1, D), _const(l, 1, 0, 0), "p"),
             (W["L"][l]["win"], (None, D, IN_TOTAL // NSH), _const(s, 0, 0), "w")],
            [groups],
            [((NSH, L, D), (None, tmw, D), functools.partial(lambda _s, i, s: (s, i, 0), s=s), "write", part),
             ((1, D), (1, D), _const(0, 0), "acc_all"),
             ((1, NSH, D, IN_TOTAL // NSH), (None, None, D, IN_TOTAL // NSH), _const(0, s, 0, 0), "acc_all",
              bufs.get(("w_in", l)))])
        dnw1.append(dnw_s)
    dnw1 = (dnw1[0] + dnw1[1]) + (dnw1[2] + dnw1[3])
    dx1 = _sum_parts(f"pre_bwd_{l}_dx", dx2, part, (L, D))
    ready(l, "pre")

    dx0, dnw0 = ffn_bwd(f"ffn_bwd_{l}0", sv["x0"], sv["g0"], sv["u0"], dx1, W, bufs, l, 0, 0, L, tm)
    ready(l, "ffn0")
    small["norm_w"] = jnp.concatenate([dnw0, dnw1, dnw2], axis=0)
    return dx0, small, prep_ct


SMALL_RAW = ("s5_lambda_re", "s5_lambda_im", "s5_log_dt", "s5_b_re", "s5_b_im", "s5_c_re", "s5_c_im", "s5_d", "s5_glu_b",
             "hg_lb_logits", "hg_norm_w", "rg_conv_b", "rg_wa", "rg_ba", "rg_wx", "rg_bx", "rg_lambda", "final_norm_w")
DEPTH = 2


def local_step(x, target, W, raw, layer_weights=None, layer_grads=None):
    L = x.shape[0]
    tm = min(256, L)
    col = lambda v: v.reshape(DEPTH, 1, BW)
    (ar, ai, bsub, csub), s5_vjp = jax.vjp(jax.vmap(s5_prep), *[raw[k] for k in SMALL_RAW[:7]])
    (wa, wx), rg_vjp = jax.vjp(lambda a, b: (jax.vmap(rg_prep)(a), jax.vmap(rg_prep)(b)), raw["rg_wa"], raw["rg_wx"])
    lb, hg_vjp = jax.vjp(hg_prep, raw["hg_lb_logits"])
    P = dict(
        ar=[ar[l] for l in range(DEPTH)], ai=[ai[l] for l in range(DEPTH)],
        bsub=bsub.astype(MMT), csub=csub.astype(MMT), wa=wa.astype(MMT), wx=wx.astype(MMT),
        lb=[lb[l].reshape(1, BW) for l in range(DEPTH)], convb=[raw["rg_conv_b"][l].reshape(1, BW) for l in range(DEPTH)],
        ba=col(raw["rg_ba"]), bx=col(raw["rg_bx"]), lam=col(raw["rg_lambda"]), d=col(raw["s5_d"]),
        glub=col(raw["s5_glu_b"]), hgw=col(raw["hg_norm_w"]), hmat=_head_mean_matrix())

    saved = []
    h = _to_segment_order(x)
    for l in range(DEPTH):
        if layer_weights is not None:
            W["L"][l], h = layer_weights(l, h)
        h, sv = layer_fwd(l, h, W, P, L, tm)
        saved.append(sv)
    loss, dh, dfw = loss_fwd_bwd(h, raw["final_norm_w"].reshape(1, D_MODEL), _to_segment_order(target), L, tm)

    big, per_layer, prep_cts = {}, [None] * DEPTH, [None] * DEPTH
    ready = (lambda l, group: None) if layer_grads is None else (lambda l, group: layer_grads(l, group, big))
    for l in reversed(range(DEPTH)):
        dh, sm, pc = layer_bwd(l, dh, saved[l], W, P, big, L, tm, ready)
        per_layer[l], prep_cts[l] = sm, pc
    dh = _to_time_order(dh)

    small = {k: jnp.stack([per_layer[l][k] for l in range(DEPTH)]) for k in per_layer[0]}
    both = lambda k: jnp.stack([prep_cts[l][k] for l in range(DEPTH)])
    dbsub = jnp.stack([big.pop(f"bsub{q}") for q in range(4)], axis=1)
    dcsub = jnp.stack([big.pop("csub0"), big.pop("csub1")], axis=1)
    s5_g = s5_vjp((both("dar"), both("dai"), dbsub, dcsub))
    small.update(zip(SMALL_RAW[:7], s5_g))
    small["rg_wa"], small["rg_wx"] = rg_vjp((big.pop("wa"), big.pop("wx")))
    (small["hg_lb_logits"],) = hg_vjp(jnp.concatenate([prep_cts[l]["dlb"] for l in range(DEPTH)], axis=0))
    small["final_norm_w"] = dfw[0]
    return loss, dh, big, small


ANY = pl.BlockSpec(memory_space=pl.ANY)


def _place():
    x, y, c = lax.axis_index("x"), lax.axis_index("y"), lax.axis_index("c")
    chips = [(1 - x, y), (x, 1 - y), (1 - x, 1 - y)]
    return x, y, c, chips


def _remote(src, dst, send, recv, k, to):
    return pltpu.make_async_remote_copy(src_ref=src, dst_ref=dst, send_sem=send.at[k], recv_sem=recv.at[k],
                                        device_id=to, device_id_type=MESH)


def _comm_call(body, name, ins, out_shapes, n_sem, n_loc):
    return pl.pallas_call(
        body, name=name, in_specs=[ANY] * len(ins), out_specs=[ANY] * len(out_shapes), out_shape=out_shapes,
        scratch_shapes=[pltpu.SemaphoreType.DMA((n_sem,)), pltpu.SemaphoreType.DMA((n_sem,)),
                        pltpu.SemaphoreType.DMA((max(n_loc, 1),))],
    )(*ins)


def gather_shards(name, shards):
    n = len(shards)
    per = 8

    def body(*refs):
        ins, outs = refs[:n], refs[n:2 * n]
        send, recv, _ = refs[2 * n:]
        x, y, c, chips = _place()
        me = 2 * x + y
        sib = (x, y, 1 - c)
        sends = []
        for w in range(n):
            for j, (cx, cy) in enumerate(chips):
                cp = _remote(ins[w].at[c], outs[w].at[c, me], send, recv, per * w + j, (cx, cy, c))
                cp.start()
                sends.append(cp)
        for w in range(n):
            for l in range(2):
                cp = _remote(ins[w].at[l], outs[w].at[l, me], send, recv, per * w + 6 + l, sib)
                cp.start()
                sends.append(cp)
        for w in range(n):
            for j, (cx, cy) in enumerate(chips):
                theirs = outs[w].at[c, 2 * cx + cy]
                _remote(ins[w].at[c], theirs, send, recv, per * w + j, (cx, cy, c)).wait_recv()
                cp = _remote(theirs, theirs, send, recv, per * w + 3 + j, sib)
                cp.start()
                sends.append(cp)
        for w in range(n):
            for j, (cx, cy) in enumerate(chips):
                dst = outs[w].at[1 - c, 2 * cx + cy]
                _remote(dst, dst, send, recv, per * w + 3 + j, sib).wait_recv()
            for l in range(2):
                dst = outs[w].at[l, me]
                _remote(dst, dst, send, recv, per * w + 6 + l, sib).wait_recv()
        for cp in sends:
            cp.wait_send()

    shapes = [jax.ShapeDtypeStruct((2, NSH) + s.shape[1:], s.dtype) for s in shards]
    return _comm_call(body, name, shards, shapes, per * n, 0)


def exchange_halves(name, grads, ranges):
    n = len(grads)

    def body(*refs):
        ins, outs = refs[:n], refs[n:2 * n]
        send, recv, _ = refs[2 * n:]
        x, y, c, _chips = _place()
        cps = []
        for w in range(n):
            h = grads[w].shape[2] // 2
            p0, np_ = ranges[w]
            cp = _remote(ins[w].at[pl.ds(p0, np_), :, pl.ds((1 - c) * h, h)], outs[w], send, recv, w, (x, y, 1 - c))
            cp.start()
            cps.append(cp)
        for cp in cps:
            cp.wait()

    shapes = [jax.ShapeDtypeStruct((r[1], NSH, g.shape[2] // 2, g.shape[3]), g.dtype) for g, r in zip(grads, ranges)]
    return _comm_call(body, name, grads, shapes, n, 0)


def scatter_to_chips(name, halves):
    n = len(halves)

    def body(*refs):
        ins, outs = refs[:n], refs[n:2 * n]
        send, recv, _ = refs[2 * n:]
        x, y, c, chips = _place()
        cps = []
        for w in range(n):
            for j, (cx, cy) in enumerate(chips):
                cp = _remote(ins[w].at[:, 2 * cx + cy], outs[w].at[j], send, recv, 3 * w + j, (cx, cy, c))
                cp.start()
                cps.append(cp)
        for cp in cps:
            cp.wait()

    shapes = [jax.ShapeDtypeStruct((3, h.shape[0]) + h.shape[2:], h.dtype) for h in halves]
    return _comm_call(body, name, halves, shapes, 3 * n, 0)


def share_halves(name, pieces):
    n = len(pieces)

    def body(*refs):
        ins, outs = refs[:n], refs[n:2 * n]
        send, recv, _ = refs[2 * n:]
        x, y, c, _chips = _place()
        cps = []
        for w in range(n):
            cp = _remote(ins[w], outs[w], send, recv, w, (x, y, 1 - c))
            cp.start()
            cps.append(cp)
        for cp in cps:
            cp.wait()

    return _comm_call(body, name, pieces, [jax.ShapeDtypeStruct(p.shape, p.dtype) for p in pieces], n, 0)


def add_own_half(name, g, ra, c, wire, b0):
    nblk, h, cols = ra.shape
    tr = _row_tile(h, cols, mult=16)
    nt = h // tr

    def body(c_ref, g_ref, r_ref, o_ref):
        o_ref[...] = (g_ref[...] + r_ref[...]).astype(o_ref.dtype)

    blk = (None, tr, cols)
    return pl.pallas_call(
        body, name=name,
        grid_spec=pltpu.PrefetchScalarGridSpec(
            num_scalar_prefetch=1, grid=(nblk, nt),
            in_specs=[pl.BlockSpec(blk, lambda s, i, c_ref: (b0 + s, c_ref[0] * nt + i, 0)), pl.BlockSpec(blk, lambda s, i, c_ref: (s, i, 0))],
            out_specs=pl.BlockSpec(blk, lambda s, i, c_ref: (s, i, 0))),
        out_shape=jax.ShapeDtypeStruct(ra.shape, wire),
    )(c.reshape(1), g, ra)


def add_chips(name, hb, rb, me):
    npc, _, h, cols = hb.shape
    tr = _row_tile(h, cols, mult=16)

    def body(me_ref, h_ref, r0, r1, r2, o_ref):
        f = lambda r: r[...].astype(F32)
        o_ref[...] = ((f(h_ref) + f(r0)) + f(r1)) + f(r2)

    rspec = lambda j: pl.BlockSpec((None, None, tr, cols), functools.partial(lambda p, i, me_ref, j: (j, p, i, 0), j=j))
    return pl.pallas_call(
        body, name=name,
        grid_spec=pltpu.PrefetchScalarGridSpec(
            num_scalar_prefetch=1, grid=(npc, h // tr),
            in_specs=[pl.BlockSpec((None, None, tr, cols), lambda p, i, me_ref: (p, me_ref[0], i, 0)), rspec(0), rspec(1), rspec(2)],
            out_specs=pl.BlockSpec((None, tr, cols), lambda p, i, me_ref: (p, i, 0))),
        out_shape=jax.ShapeDtypeStruct((npc, h, cols), F32),
    )(me.reshape(1), hb, rb, rb, rb)


def adamw_halves(name, w, m, v, own, other, c):
    npc, rows, cols = w.shape
    h = rows // 2
    tr = _row_tile(h, cols, budget=1024 * 1024)
    nt = h // tr
    c1 = 1.0 - ADAM_B1 ** ADAM_STEP
    c2 = 1.0 - ADAM_B2 ** ADAM_STEP

    def body(c_ref, w_ref, m_ref, v_ref, own_ref, oth_ref, g_ref, d_ref, nm_ref, nv_ref):
        g = jnp.where(pl.program_id(1) == c_ref[0], own_ref[...], oth_ref[...])
        nm = ADAM_B1 * m_ref[...] + (1.0 - ADAM_B1) * g
        nv = ADAM_B2 * v_ref[...] + (1.0 - ADAM_B2) * jnp.square(g)
        g_ref[...] = g
        d_ref[...] = -ADAM_LR * ((nm / c1) / (jnp.sqrt(nv / c2) + ADAM_EPS) + ADAM_WD * w_ref[...])
        nm_ref[...] = nm
        nv_ref[...] = nv

    full = pl.BlockSpec((None, tr, cols), lambda p, hh, i, c_ref: (p, hh * nt + i, 0))
    half = pl.BlockSpec((None, tr, cols), lambda p, hh, i, c_ref: (p, i, 0))
    return pl.pallas_call(
        body, name=name,
        grid_spec=pltpu.PrefetchScalarGridSpec(
            num_scalar_prefetch=1, grid=(npc, 2, nt),
            in_specs=[full, full, full, half, half], out_specs=[full] * 4),
        out_shape=[jax.ShapeDtypeStruct(w.shape, F32)] * 4,
    )(c.reshape(1), w, m, v, own, other)


WEIGHTS = ("norm_w", "final_norm_w", "ffn_gate", "ffn_up", "ffn_down", "w_in", "branch_proj", "w_out", "s5_lambda_re",
           "s5_lambda_im", "s5_log_dt", "s5_b_re", "s5_b_im", "s5_c_re", "s5_c_im", "s5_d", "s5_glu_w", "s5_glu_b",
           "hg_lb_logits", "hg_norm_w", "rg_conv_w", "rg_conv_b", "rg_wa", "rg_ba", "rg_wx", "rg_bx", "rg_lambda")
BIG = ("ffn_gate", "ffn_up", "ffn_down", "w_in", "branch_proj", "w_out", "s5_glu_w")
SHARDED_SMALL = ("norm_w", "rg_conv_w")
SMALL = SMALL_RAW + SHARDED_SMALL


def _view2d(shape):
    return (1, shape[0]) if len(shape) == 1 else (math.prod(shape[:-1]), shape[-1])


def _small_layout(shapes, row_multiple):
    layout, at = [], 0
    for shape in shapes:
        r, c = _view2d(shape)
        rp = -(-r // 8) * 8
        layout.append((at, r, c, rp))
        at += rp * max(1, c // LANE)
    return layout, -(-at // row_multiple) * row_multiple


def pack_small(name, arrays, row_multiple):
    layout, rows = _small_layout([a.shape for a in arrays], row_multiple)

    def body(*refs):
        out = refs[-1]
        out[...] = jnp.zeros_like(out)
        for ref, (r0, r, c, rp) in zip(refs[:-1], layout):
            if c <= LANE:
                out[r0:r0 + r, 0:c] = ref[...]
            else:
                for q in range(c // LANE):
                    out[r0 + q * rp:r0 + q * rp + r, :] = ref[:, q * LANE:(q + 1) * LANE]

    return pl.pallas_call(
        body, name=name, out_shape=jax.ShapeDtypeStruct((rows, LANE), F32),
        compiler_params=pltpu.CompilerParams(vmem_limit_bytes=VMEM_LIMIT),
    )(*[a.reshape(_view2d(a.shape)) for a in arrays])


def unpack_small(name, packed, shapes):
    layout, _ = _small_layout(shapes, 8)

    def body(p_ref, *outs):
        for ref, (r0, r, c, rp) in zip(outs, layout):
            if c <= LANE:
                ref[...] = p_ref[r0:r0 + r, 0:c]
            else:
                for q in range(c // LANE):
                    ref[:, q * LANE:(q + 1) * LANE] = p_ref[r0 + q * rp:r0 + q * rp + r, :]

    res = pl.pallas_call(
        body, name=name, out_shape=[jax.ShapeDtypeStruct(_view2d(s), F32) for s in shapes],
        compiler_params=pltpu.CompilerParams(vmem_limit_bytes=VMEM_LIMIT),
    )(packed)
    return [a.reshape(s) for a, s in zip(res, shapes)]


HBM = pl.BlockSpec(memory_space=pltpu.HBM)
SEM = pl.BlockSpec(memory_space=pltpu.SEMAPHORE)
EFFECT = pltpu.SideEffectType.DATAFLOW_SIDE_EFFECTING


def split_start(name, srcs, land_shapes, plan, n_send, n_recv):
    ns, nl = len(srcs), len(land_shapes)

    def body(*refs):
        ins, lands = refs[:ns], refs[ns:ns + nl]
        send, recv = refs[ns + nl], refs[ns + nl + 1]
        for src, dst, ks, kr, dev in plan(ins, lands):
            pltpu.make_async_remote_copy(src_ref=src, dst_ref=dst, send_sem=send.at[ks], recv_sem=recv.at[kr],
                                         device_id=dev, device_id_type=MESH).start()
        refs[-1][...] = jnp.zeros_like(refs[-1])

    hbm = lambda a: pltpu.with_memory_space_constraint(a, pltpu.HBM)
    lands = [lax.empty(s.shape, s.dtype) for s in land_shapes]
    out = pl.pallas_call(
        body, name=name,
        out_shape=(pltpu.SemaphoreType.DMA((n_send,)), pltpu.SemaphoreType.DMA((n_recv,)),
                   *[pltpu.HBM(a.shape, a.dtype) for a in srcs], *[pltpu.HBM(s.shape, s.dtype) for s in land_shapes],
                   jax.ShapeDtypeStruct((8, LANE), F32)),
        in_specs=[HBM] * (ns + nl), out_specs=(SEM, SEM, *[HBM] * (ns + nl), pl.BlockSpec(memory_space=pltpu.VMEM)),
        input_output_aliases={k: 2 + k for k in range(ns + nl)},
        compiler_params=pltpu.CompilerParams(has_side_effects=EFFECT),
    )(*[hbm(a) for a in srcs], *[hbm(a) for a in lands])
    return out[:-1], out[-1]


def split_wait(name, handles, n_src, waits, after):
    send, recv, *bufs = handles
    nb = len(bufs)

    def body(*refs):
        ins, lands = refs[:n_src], refs[n_src:nb]
        send_sem, recv_sem = refs[nb], refs[nb + 1]
        x, y, c, _chips = _place()
        sends, recvs = waits(ins, lands)
        for src, k in sends:
            pltpu.make_async_remote_copy(src_ref=src, dst_ref=src, send_sem=send_sem.at[k], recv_sem=recv_sem.at[0],
                                         device_id=(x, y, 1 - c), device_id_type=MESH).wait_send()
        for dst, k in recvs:
            pltpu.make_async_remote_copy(src_ref=dst, dst_ref=dst, send_sem=send_sem.at[0], recv_sem=recv_sem.at[k],
                                         device_id=(x, y, 1 - c), device_id_type=MESH).wait_recv()

    out = pl.pallas_call(
        body, name=name, out_shape=tuple(pltpu.HBM(a.shape, a.dtype) for a in bufs),
        in_specs=[HBM] * nb + [SEM, SEM, ANY], out_specs=tuple([HBM] * nb),
        input_output_aliases={k: k for k in range(nb)},
        compiler_params=pltpu.CompilerParams(has_side_effects=EFFECT),
    )(*bufs, send, recv, after)
    return list(out[:n_src]), list(out[n_src:])


def gather_plan(n):
    def plan(ins, lands):
        x, y, c, chips = _place()
        me = 2 * x + y
        copies = []
        for w in range(n):
            for j, (cx, cy) in enumerate(chips):
                for t in range(2):
                    copies.append((ins[w].at[c], lands[w].at[c, me], 8 * w + 2 * j + t, 8 * w + 2 * j + c, (cx, cy, t)))
            for half in range(2):
                copies.append((ins[w].at[half], lands[w].at[half, me], 8 * w + 6 + half, 8 * w + 6 + half, (x, y, 1 - c)))
        return copies

    def waits(ins, lands):
        x, y, c, chips = _place()
        me = 2 * x + y
        sends, recvs = [], []
        for w in range(n):
            for j, (cx, cy) in enumerate(chips):
                for t in range(2):
                    sends.append((ins[w].at[c], 8 * w + 2 * j + t))
                    recvs.append((lands[w].at[t, 2 * cx + cy], 8 * w + 2 * j + t))
            for half in range(2):
                sends.append((ins[w].at[half], 8 * w + 6 + half))
                recvs.append((lands[w].at[half, me], 8 * w + 6 + half))
        return sends, recvs

    return plan, waits


def scatter_plan(n):
    def plan(ins, lands):
        x, y, c, chips = _place()
        return [(ins[w].at[:, 2 * cx + cy], lands[w].at[j], 3 * w + j, 3 * w + j, (cx, cy, c))
                for w in range(n) for j, (cx, cy) in enumerate(chips)]

    def waits(ins, lands):
        x, y, c, chips = _place()
        sends = [(ins[w].at[:, 2 * cx + cy], 3 * w + j) for w in range(n) for j, (cx, cy) in enumerate(chips)]
        recvs = [(lands[w].at[j], 3 * w + j) for w in range(n) for j in range(3)]
        return sends, recvs

    return plan, waits


def _layer_shards(w, l):
    return [w["ffn_gate"][l].astype(MMT), w["ffn_up"][l].astype(MMT), w["ffn_down"][l].astype(MMT),
            w["w_in"][l].reshape(2, D_MODEL // 2, -1).astype(MMT),
            w["branch_proj"][l].reshape(2, 3 * BW // 2, -1).astype(MMT),
            w["w_out"][l].reshape(2, -1, D_MODEL).astype(MMT),
            w["s5_glu_w"][l].reshape(2, -1, BW).astype(MMT)]


def _layer_weights(g):
    rows = lambda a: a.transpose(1, 0, 2, 3).reshape(NSH, -1, a.shape[-1])
    p = rows(g[4]).reshape(NSH, 3, BW, -1).transpose(1, 2, 0, 3).reshape(3, BW, D_MODEL)
    return dict(wg=g[0], wu=g[1], wd=g[2], win=rows(g[3]), pfull=p,
                woutfull=rows(g[5]).reshape(D_MODEL, D_MODEL), gluw=rows(g[6]).reshape(BW, BW))


GROUPS = {"ffn1": ("ffn_gate", "ffn_up", "ffn_down"), "merge": ("branch_proj", "w_out"), "mid": ("s5_glu_w",),
          "pre": ("w_in",), "ffn0": ("ffn_gate", "ffn_up", "ffn_down")}


def _grad_views(big, l, group):
    views = []
    for name in GROUPS[group]:
        if name == "branch_proj":
            dq = D_MODEL // NSH
            a = big[(name, l)].reshape(3, BW, NSH, dq).transpose(2, 0, 1, 3).reshape(1, NSH, 3 * BW, dq)
        elif name.startswith("ffn"):
            a = big[(name, l, 1 if group == "ffn1" else 0)]
        else:
            a = big[(name, l)]
            a = a.reshape(1, NSH, -1, a.shape[-1])
        views.append((name, a, 0))
    return views


def halves_plan(n):
    def src(ref, c):
        h = ref.shape[2] // 2
        return ref.at[:, :, pl.ds((1 - c) * h, h)]

    def plan(ins, lands):
        x, y, c, _chips = _place()
        return [(src(ins[w], c), lands[w], w, w, (x, y, 1 - c)) for w in range(n)]

    def waits(ins, lands):
        x, y, c, _chips = _place()
        return [(src(ins[w], c), w) for w in range(n)], [(lands[w], w) for w in range(n)]

    return plan, waits


def _reduce_to_halves(tag, views, c, wire):
    from_sibling = exchange_halves(f"reduce_cores_{tag}", [a for _, a, _ in views], [(p0, 1) for _, _, p0 in views])
    merge = lambda a: a.reshape((-1,) + a.shape[2:])
    return [add_own_half(f"sum_cores_{tag}_{i}", merge(a), merge(r), c, wire[i], NSH * p0).reshape(r.shape)
            for i, ((_, a, p0), r) in enumerate(zip(views, from_sibling))]


def _step(x, target, w, m, v):
    mx, my, mc = lax.axis_index("x"), lax.axis_index("y"), lax.axis_index("c")
    me = (2 * mx + my).astype(jnp.int32)
    mc = mc.astype(jnp.int32)

    W = dict(L=[None] * DEPTH)
    state = {"pending": []}
    n_big = len(BIG)
    g_plan, g_waits = gather_plan(n_big)

    def layer_weights(l, h):
        if l == 0:
            got = gather_shards("gather_weights_0", _layer_shards(w, 0) + [w[n] for n in SHARDED_SMALL])
            nxt = _layer_shards(w, 1)
            got, nxt = lax.optimization_barrier((got, nxt))
            shapes = [jax.ShapeDtypeStruct((2, NSH) + a.shape[1:], a.dtype) for a in nxt]
            state["gather"], token = split_start("gather_weights_1_start", nxt, shapes, g_plan, 8 * n_big, 8 * n_big)
            W["nw"] = got[n_big].transpose(0, 2, 1, 3).reshape(DEPTH, 3, 1, D_MODEL) + token[0, 0]
            W["convw"] = got[n_big + 1].transpose(0, 2, 1, 3).reshape(DEPTH, CONV_W, BW)
            return _layer_weights(got[:n_big]), h
        return _layer_weights(split_wait("gather_weights_1_wait", state["gather"], n_big, g_waits, h)[1]), h

    def to_chips(after):
        if "cores" not in state:
            return
        tag, names, l, group, handles, waits = state.pop("cores")
        sent, landed = split_wait(f"reduce_cores_{tag}_wait", handles, len(names), waits, after)
        merge = lambda a: a.reshape((-1,) + a.shape[2:])
        halves = [add_own_half(f"sum_cores_{tag}_{i}", merge(a), merge(r), mc, jnp.bfloat16, 0).reshape(r.shape)
                  for i, (a, r) in enumerate(zip(sent, landed))]
        shapes = [jax.ShapeDtypeStruct((3, a.shape[0]) + a.shape[2:], a.dtype) for a in halves]
        plan, waits = scatter_plan(len(halves))
        handles, token = split_start(f"reduce_chips_{tag}_start", halves, shapes, plan, 3 * len(halves), 3 * len(halves))
        W["nw"] = W["nw"] + token[0, 0]
        state["pending"].append((tag, names, l, group, handles, waits))

    def layer_grads(l, group, big):
        views = _grad_views(big, l, group)
        to_chips(views[0][1])
        if (l, group) == (0, "ffn0"):
            return
        tag = f"{l}_{group}"
        if (l, group) == (0, "pre"):
            halves = _reduce_to_halves(tag, views, mc, [jnp.bfloat16] * len(views))
            shapes = [jax.ShapeDtypeStruct((3, a.shape[0]) + a.shape[2:], a.dtype) for a in halves]
            plan, waits = scatter_plan(len(halves))
            handles, token = split_start(f"reduce_chips_{tag}_start", halves, shapes, plan, 3 * len(halves), 3 * len(halves))
            W["nw"] = W["nw"] + token[0, 0]
            state["pending"].append((tag, [name for name, _, _ in views], l, group, handles, waits))
            return
        arrays = [a for _, a, _ in views]
        shapes = [jax.ShapeDtypeStruct((1, NSH, a.shape[2] // 2, a.shape[3]), a.dtype) for a in arrays]
        plan, waits = halves_plan(len(arrays))
        handles, token = split_start(f"reduce_cores_{tag}_start", arrays, shapes, plan, len(arrays), len(arrays))
        W["nw"] = W["nw"] + token[0, 0]
        state["cores"] = (tag, [name for name, _, _ in views], l, group, handles, waits)

    loss, dx, big, small = local_step(x[0], target[0], W, {k: w[k] for k in SMALL_RAW}, layer_weights, layer_grads)

    pieces = {n: {} for n in BIG}
    block_of = lambda name, l, group: (2 * l + (group == "ffn1")) if name.startswith("ffn") else l
    views = _grad_views(big, 0, "ffn0")
    small_packed = pack_small("pack_small_grads", [small[n] for n in SMALL], NSH * 32)
    halves = _reduce_to_halves("0_ffn0", views + [("small", small_packed.reshape(1, NSH, -1, LANE), 0)], mc,
                               [jnp.bfloat16] * len(views) + [F32])
    shapes = [jax.ShapeDtypeStruct((3, a.shape[0]) + a.shape[2:], a.dtype) for a in halves]
    plan, waits = scatter_plan(len(halves))
    last_handles, token = split_start("reduce_chips_0_ffn0_start", halves, shapes, plan, 3 * len(halves), 3 * len(halves))
    mc = mc + token[0, 0].astype(jnp.int32)
    for tag, names, l, group, handles, waits_k in state["pending"]:
        sent, landed = split_wait(f"reduce_chips_{tag}_wait", handles, len(names), waits_k, dx)
        for i, (name, h, r) in enumerate(zip(names, sent, landed)):
            pieces[name][block_of(name, l, group)] = add_chips(f"sum_chips_{tag}_{i}", h, r, me)

    g, delta, new_m, new_v = {}, {}, {}, {}

    def update(tag, names, extra):
        own = [jnp.concatenate([pieces[n][b] for b in sorted(pieces[n])], axis=0) for n in names] + extra
        other = share_halves(f"reduce_share_{tag}", own)
        for i, n in enumerate(names):
            view = lambda a: a.reshape(own[i].shape[0], -1, own[i].shape[2])
            res = adamw_halves(f"adamw_{n}", view(w[n]), view(m[n]), view(v[n]), own[i], other[i], mc)
            g[n], delta[n], new_m[n], new_v[n] = [a.reshape(w[n].shape) for a in res]
        return own, other

    early = [n for n in BIG if not n.startswith("ffn")]
    update("early", early, [])
    sent, landed = split_wait("reduce_chips_0_ffn0_wait", last_handles, len(halves), waits, new_v[early[0]])
    last = [add_chips(f"sum_chips_0_ffn0_{i}", h, r, me) for i, (h, r) in enumerate(zip(sent, landed))]
    for (name, _, _), piece in zip(views, last):
        pieces[name][block_of(name, 0, "ffn0")] = piece
    own, other = update("last", [n for n in BIG if n.startswith("ffn")], [last[-1]])

    piece = jnp.stack([jnp.where(mc == 0, own[-1][0], other[-1][0]), jnp.where(mc == 0, other[-1][0], own[-1][0])])
    (all_small,) = gather_shards("gather_small", [piece])
    full_small = unpack_small("unpack_small_grads", all_small.transpose(1, 0, 2, 3).reshape(-1, LANE),
                              [small[n].shape for n in SMALL])
    g.update(zip(SMALL, full_small))
    g["norm_w"] = lax.dynamic_slice_in_dim(g["norm_w"], me * (D_MODEL // NSH), D_MODEL // NSH, axis=2)
    g["rg_conv_w"] = lax.dynamic_slice_in_dim(g["rg_conv_w"], me * (BW // NSH), BW // NSH, axis=2)

    packed = [pack_small(f"pack_small_{tag}", [src[n] for n in SMALL], 8)
              for tag, src in (("w", w), ("g", g), ("m", m), ("v", v))]
    for tag, dst, flat in zip(("delta", "m", "v"), (delta, new_m, new_v), adamw(*packed)):
        dst.update(zip(SMALL, unpack_small(f"unpack_small_{tag}", flat, [w[n].shape for n in SMALL])))

    total = lax.psum(loss[0, 0], ("x", "y", "c"))
    return (total, dx[None], *[g[n] for n in WEIGHTS], *[delta[n] for n in WEIGHTS],
            *[new_m[n] for n in WEIGHTS], *[new_v[n] for n in WEIGHTS])


def kernel(x, norm_w, final_norm_w, ffn_gate, ffn_up, ffn_down, w_in, branch_proj, w_out, s5_lambda_re, s5_lambda_im, s5_log_dt, s5_b_re, s5_b_im, s5_c_re, s5_c_im, s5_d, s5_glu_w, s5_glu_b, hg_lb_logits, hg_norm_w, rg_conv_w, rg_conv_b, rg_wa, rg_ba, rg_wx, rg_bx, rg_lambda, loss_target, m_norm_w, m_final_norm_w, m_ffn_gate, m_ffn_up, m_ffn_down, m_w_in, m_branch_proj, m_w_out, m_s5_lambda_re, m_s5_lambda_im, m_s5_log_dt, m_s5_b_re, m_s5_b_im, m_s5_c_re, m_s5_c_im, m_s5_d, m_s5_glu_w, m_s5_glu_b, m_hg_lb_logits, m_hg_norm_w, m_rg_conv_w, m_rg_conv_b, m_rg_wa, m_rg_ba, m_rg_wx, m_rg_bx, m_rg_lambda, v_norm_w, v_final_norm_w, v_ffn_gate, v_ffn_up, v_ffn_down, v_w_in, v_branch_proj, v_w_out, v_s5_lambda_re, v_s5_lambda_im, v_s5_log_dt, v_s5_b_re, v_s5_b_im, v_s5_c_re, v_s5_c_im, v_s5_d, v_s5_glu_w, v_s5_glu_b, v_hg_lb_logits, v_hg_norm_w, v_rg_conv_w, v_rg_conv_b, v_rg_wa, v_rg_ba, v_rg_wx, v_rg_bx, v_rg_lambda):
    ws = (norm_w, final_norm_w, ffn_gate, ffn_up, ffn_down, w_in, branch_proj, w_out, s5_lambda_re, s5_lambda_im, s5_log_dt, s5_b_re, s5_b_im, s5_c_re, s5_c_im, s5_d, s5_glu_w, s5_glu_b, hg_lb_logits, hg_norm_w, rg_conv_w, rg_conv_b, rg_wa, rg_ba, rg_wx, rg_bx, rg_lambda)
    ms = (m_norm_w, m_final_norm_w, m_ffn_gate, m_ffn_up, m_ffn_down, m_w_in, m_branch_proj, m_w_out, m_s5_lambda_re, m_s5_lambda_im, m_s5_log_dt, m_s5_b_re, m_s5_b_im, m_s5_c_re, m_s5_c_im, m_s5_d, m_s5_glu_w, m_s5_glu_b, m_hg_lb_logits, m_hg_norm_w, m_rg_conv_w, m_rg_conv_b, m_rg_wa, m_rg_ba, m_rg_wx, m_rg_bx, m_rg_lambda)
    vs = (v_norm_w, v_final_norm_w, v_ffn_gate, v_ffn_up, v_ffn_down, v_w_in, v_branch_proj, v_w_out, v_s5_lambda_re, v_s5_lambda_im, v_s5_log_dt, v_s5_b_re, v_s5_b_im, v_s5_c_re, v_s5_c_im, v_s5_d, v_s5_glu_w, v_s5_glu_b, v_hg_lb_logits, v_hg_norm_w, v_rg_conv_w, v_rg_conv_b, v_rg_wa, v_rg_ba, v_rg_wx, v_rg_bx, v_rg_lambda)
    return _step(x, loss_target, dict(zip(WEIGHTS, ws)), dict(zip(WEIGHTS, ms)), dict(zip(WEIGHTS, vs)))
```

```python
import functools
import math
from typing import NamedTuple

import jax
import jax.numpy as jnp
from jax import lax
from jax.experimental import pallas as pl
from jax.experimental.pallas import tpu as pltpu

F32 = jnp.float32
MMT = jnp.bfloat16

D_MODEL = 1024
BW = 512
S5_GROUP, S5_GROUPS, S5_STATE = 16, 32, 64
S5_N = S5_GROUPS * S5_STATE
HG_HEADS, HG_D = 4, 128
HG_CHUNK = 128
RG_BLOCKS, RG_BLOCK = 8, 64
RG_C = 8.0
CONV_W = 4
D_FF = 2816
EPS = 1e-6
IN_TOTAL = 6656
NSH = 4
NSEG = 8
LANE = 128
VMEM_LIMIT = 56 * 1024 * 1024
TM_FWD = 512
TM_WGRAD = 512

ADAM_LR, ADAM_B1, ADAM_B2, ADAM_EPS, ADAM_WD, ADAM_STEP = 0.001, 0.9, 0.999, 1e-08, 0.01, 10

MESH = pl.DeviceIdType.MESH


class WP(NamedTuple):
    w: jax.Array
    p: jax.Array


def _dg(a, b, ca, cb):
    return lax.dot_general(a, b, (((ca,), (cb,)), ((), ())), preferred_element_type=F32)


@jax.custom_vjp
def _mmw(a, w, p):
    return _dg(a.astype(MMT), w, 1, 0)


def _mmw_fwd(a, w, p):
    return _mmw(a, w, p), (a, w)


def _mmw_bwd(res, g):
    a, w = res
    gb = g.astype(MMT)
    return _dg(gb, w, 1, 1), jnp.zeros_like(w), _dg(a.astype(MMT), gb, 0, 0)


_mmw.defvjp(_mmw_fwd, _mmw_bwd)


def mm(a, w):
    if isinstance(w, WP):
        return _mmw(a, w.w, w.p)
    return _dg(a.astype(MMT), w, 1, 0)


@jax.custom_vjp
def mma_nn(a, b):
    return _dg(a.astype(MMT), b.astype(MMT), 1, 0)


def _nn_f(a, b):
    return mma_nn(a, b), (a, b)


def _nn_b(res, g):
    a, b = res
    gb = g.astype(MMT)
    return _dg(gb, b.astype(MMT), 1, 1), _dg(a.astype(MMT), gb, 0, 0)


mma_nn.defvjp(_nn_f, _nn_b)


@jax.custom_vjp
def mma_nt(a, b):
    return _dg(a.astype(MMT), b.astype(MMT), 1, 1)


def _nt_f(a, b):
    return mma_nt(a, b), (a, b)


def _nt_b(res, g):
    a, b = res
    gb = g.astype(MMT)
    return _dg(gb, b.astype(MMT), 1, 0), _dg(gb, a.astype(MMT), 0, 0)


mma_nt.defvjp(_nt_f, _nt_b)


@jax.custom_vjp
def mma_tn(a, b):
    return _dg(a.astype(MMT), b.astype(MMT), 0, 0)


def _tn_f(a, b):
    return mma_tn(a, b), (a, b)


def _tn_b(res, g):
    a, b = res
    gb = g.astype(MMT)
    return _dg(b.astype(MMT), gb, 1, 1), _dg(a.astype(MMT), gb, 1, 0)


mma_tn.defvjp(_tn_f, _tn_b)


def _split3(x):
    hi = x.astype(MMT)
    r = x - hi.astype(F32)
    mid = r.astype(MMT)
    return hi, mid, (r - mid.astype(F32)).astype(MMT)


@jax.custom_vjp
def mm_exact(m, x):
    mb = m.astype(MMT)
    hi, mid, lo = _split3(x)
    return (_dg(mb, hi, 1, 0) + _dg(mb, mid, 1, 0)) + _dg(mb, lo, 1, 0)


def _mm_exact_fwd(m, x):
    return mm_exact(m, x), m


def _mm_exact_bwd(m, g):
    mb = m.astype(MMT)
    hi, mid, lo = _split3(g)
    return jnp.zeros_like(m), (_dg(mb, hi, 0, 0) + _dg(mb, mid, 0, 0)) + _dg(mb, lo, 0, 0)


mm_exact.defvjp(_mm_exact_fwd, _mm_exact_bwd)


@jax.custom_vjp
def mm_exact_r(x, m):
    mb = m.astype(MMT)
    hi, mid, lo = _split3(x)
    return (_dg(hi, mb, 1, 0) + _dg(mid, mb, 1, 0)) + _dg(lo, mb, 1, 0)


def _mm_exact_r_fwd(x, m):
    return mm_exact_r(x, m), m


def _mm_exact_r_bwd(m, g):
    mb = m.astype(MMT)
    hi, mid, lo = _split3(g)
    return (_dg(hi, mb, 1, 1) + _dg(mid, mb, 1, 1)) + _dg(lo, mb, 1, 1), jnp.zeros_like(m)


mm_exact_r.defvjp(_mm_exact_r_fwd, _mm_exact_r_bwd)


def _rms(x, w):
    return x * lax.rsqrt(jnp.mean(x * x, axis=-1, keepdims=True) + EPS) * w


def _expm1(x):
    series = x * (1.0 + x * (1.0 / 2) * (1.0 + x * (1.0 / 3) * (1.0 + x * (1.0 / 4) * (1.0 + x * (1.0 / 5) * (1.0 + x * (1.0 / 6))))))
    return jnp.where(jnp.abs(x) < 0.1, series, jnp.exp(x) - 1.0)


def _bspec(block, fn, order):
    if order == "is":
        return pl.BlockSpec(block, lambda i, s: fn(s, i))
    return pl.BlockSpec(block, lambda s, i: fn(s, i))


def tile_fwd(fn, name, n_i, n_s, ins, outs, s_outer=False):
    n_in = len(ins)
    order = "si" if s_outer else "is"
    assert not (s_outer and any(o[4] for o in outs))

    def body(*refs):
        s = pl.program_id(0 if s_outer else 1)
        res = fn(*[r[...] for r in refs[:n_in]], s)
        for o_ref, val, spec in zip(refs[n_in:], res, outs):
            if spec[4] and n_s > 1:
                @pl.when(s == 0)
                def _(o_ref=o_ref, val=val):
                    o_ref[...] = val.astype(o_ref.dtype)

                @pl.when(s != 0)
                def _(o_ref=o_ref, val=val):
                    o_ref[...] += val.astype(o_ref.dtype)
            else:
                o_ref[...] = val.astype(o_ref.dtype)

    return pl.pallas_call(
        body, grid=(n_s, n_i) if s_outer else (n_i, n_s), name=name,
        in_specs=[_bspec(b, f, order) for _, b, f in ins],
        out_specs=[_bspec(b, f, order) for _, _, b, f, _ in outs],
        out_shape=[jax.ShapeDtypeStruct(sh, dt) for sh, dt, _, _, _ in outs],
        compiler_params=pltpu.CompilerParams(vmem_limit_bytes=VMEM_LIMIT,
                                             dimension_semantics=("arbitrary", "arbitrary")),
    )(*[a for a, _, _ in ins])


def tile_bwd(fn, name, n_i, n_s, ins, cts, gouts):
    groups = [c if isinstance(c, list) else [c] for c in cts]
    cts = [blk for grp in groups for blk in grp]
    n_in, n_ct = len(ins), len(cts)
    kinds = [k for _, _, _, k in ins]
    d_pos = [j for j, k in enumerate(kinds) if k != "c"]
    shared = [(gi, spec[4]) for gi, spec in enumerate(gouts) if len(spec) == 5 and spec[4] is not None]
    n_sh = len(shared)

    def body(*refs):
        s, i = pl.program_id(0), pl.program_id(1)
        vals = [r[...] for r in refs[:n_in]]
        ct_refs, ctv = list(refs[n_in:n_in + n_ct]), []
        for grp in groups:
            parts = [ct_refs.pop(0)[...] for _ in grp]
            ctv.append(parts[0] if len(parts) == 1 else jnp.concatenate(parts, axis=1))
        ctv = tuple(ctv)
        g_refs = refs[n_in + n_ct + n_sh:]

        def g(*dv):
            args = list(vals)
            for j, v in zip(d_pos, dv):
                args[j] = WP(vals[j], v) if kinds[j] == "w" else v
            return tuple(fn(*args))

        dv0 = [jnp.zeros(vals[j].shape, F32) if kinds[j] == "w" else vals[j] for j in d_pos]
        _, vjp = jax.vjp(g, *dv0)
        grads = vjp(ctv)
        for g_ref, gv, spec in zip(g_refs, grads, gouts):
            mode = spec[3]
            if mode == "write":
                g_ref[...] = gv.astype(g_ref.dtype)
            else:
                first = (i == 0) if mode == "acc_i" else jnp.logical_and(i == 0, s == 0)

                @pl.when(first)
                def _(g_ref=g_ref, gv=gv):
                    g_ref[...] = gv.astype(g_ref.dtype)

                @pl.when(jnp.logical_not(first))
                def _(g_ref=g_ref, gv=gv):
                    g_ref[...] += gv.astype(g_ref.dtype)

    return pl.pallas_call(
        body, grid=(n_s, n_i), name=name,
        in_specs=([_bspec(b, f, "si") for _, b, f, _ in ins] + [_bspec(b, f, "si") for _, b, f in cts]
                  + [pl.BlockSpec(memory_space=pl.ANY)] * n_sh),
        out_specs=[_bspec(spec[1], spec[2], "si") for spec in gouts],
        out_shape=[jax.ShapeDtypeStruct(spec[0], F32) for spec in gouts],
        input_output_aliases={n_in + n_ct + k: gi for k, (gi, _) in enumerate(shared)},
        compiler_params=pltpu.CompilerParams(vmem_limit_bytes=VMEM_LIMIT,
                                             dimension_semantics=("arbitrary", "arbitrary")),
    )(*[a for a, _, _, _ in ins], *[a for a, _, _ in cts], *[buf for _, buf in shared])


def _row_tile(rows, width, itemsize=4, budget=2 * 1024 * 1024, mult=8):
    best = mult
    for t in range(mult, rows + 1, mult):
        if rows % t == 0 and t * width * itemsize <= budget:
            best = t
    return best


def add_n(name, terms, shape):
    rows, cols = shape
    tr = _row_tile(rows, cols)

    def body(*refs):
        acc = refs[0][...]
        for r in refs[1:-1]:
            acc = acc + r[...]
        refs[-1][...] = acc

    specs = []
    for _, lead in terms:
        specs.append(pl.BlockSpec((None,) * len(lead) + (tr, cols), functools.partial(lambda i, lead: (*lead, i, 0), lead=lead)))
    return pl.pallas_call(
        body, grid=(rows // tr,), name=name, in_specs=specs,
        out_specs=pl.BlockSpec((tr, cols), lambda i: (i, 0)),
        out_shape=jax.ShapeDtypeStruct((rows, cols), F32),
    )(*[a for a, _ in terms])


def pre_core(x, nw, win):
    return (mm(_rms(x, nw), win),)


def _split_lanes(y):
    return jnp.stack([y[:, k * LANE:(k + 1) * LANE] for k in range(y.shape[1] // LANE)], axis=0)


def _join_lanes(y3):
    return jnp.concatenate([y3[k] for k in range(y3.shape[0])], axis=1)


def s5_pre_core(u, b_re0, b_re1, b_im0, b_im1):
    u0, u1 = u[:, :BW // 2], u[:, BW // 2:]
    re = jnp.concatenate([mm(u0, b_re0), mm(u1, b_re1)], axis=1)
    im = jnp.concatenate([mm(u0, b_im0), mm(u1, b_im1)], axis=1)
    return _split_lanes(re), _split_lanes(im)


def mid_core(xr, xi, u, o, g, hs, gc, hmat, c0, c1, d, gluw, glub, hgw):
    half = xr.shape[0] // 2
    xs0 = jnp.concatenate([_join_lanes(xr[:half]), _join_lanes(xi[:half])], axis=1)
    xs1 = jnp.concatenate([_join_lanes(xr[half:]), _join_lanes(xi[half:])], axis=1)
    y = jnp.concatenate([mm(xs0, c0), mm(xs1, c1)], axis=1) + d * u
    z = jax.nn.gelu(y)
    ya = z * jax.nn.sigmoid(mm(z, gluw) + glub)
    ms = mm_exact_r(o * o, hmat)
    yb = o * lax.rsqrt(ms + EPS) * hgw * jax.nn.silu(g)
    yc = hs * jax.nn.gelu(gc)
    return ya, yb, yc


def _sub(w, n):
    return WP(w.w[n], w.p[n]) if isinstance(w, WP) else w[n]


def merge_core(ya, yb, yc, g0, g1, g2, g3, g4, g5, p, wout):
    gate = lambda a, b: jax.nn.sigmoid(jnp.concatenate([a, b], axis=1))
    m = gate(g0, g1) * mm(ya, _sub(p, 0)) + gate(g2, g3) * mm(yb, _sub(p, 1)) + gate(g4, g5) * mm(yc, _sub(p, 2))
    return (mm(m, wout),)


def gates_core(xc, wa, ba, wx, bx, lam):
    r = jax.nn.sigmoid(mm(xc, wa) + ba)
    i = jax.nn.sigmoid(mm(xc, wx) + bx)
    log_a = -RG_C * jax.nn.softplus(-lam) * r
    a = jnp.exp(log_a)
    b = jnp.sqrt(-_expm1(2.0 * log_a)) * (i * xc)
    return a, b


def _seg_rows(ref, k, j, n):
    rows = pl.ds(pl.multiple_of(j * NSEG, NSEG), NSEG)
    if k is None:
        return ref[rows, :]
    return ref[k, rows, :]


def _seg_store(ref, k, j, n, val):
    rows = pl.ds(pl.multiple_of(j * NSEG, NSEG), NSEG)
    if k is None:
        ref[rows, :] = val
    else:
        ref[k, rows, :] = val


def _seg_carries(er, ei, pr, pi, reverse):
    rows = lax.broadcasted_iota(jnp.int32, er.shape, 0)
    cr = jnp.zeros_like(er)
    ci = None if ei is None else jnp.zeros_like(er)
    order = range(NSEG - 2, -1, -1) if reverse else range(1, NSEG)
    shift = NSEG - 1 if reverse else 1
    for s in order:
        if ei is None:
            tr = er + pr * cr
            cr = jnp.where(rows == s, pltpu.roll(tr, shift, 0), cr)
        else:
            tr = er + pr * cr - pi * ci
            ti = ei + pr * ci + pi * cr
            cr = jnp.where(rows == s, pltpu.roll(tr, shift, 0), cr)
            ci = jnp.where(rows == s, pltpu.roll(ti, shift, 0), ci)
    return cr, ci


S5_K = 2


def s5_scan_fwd(bur, bui, ar, ai, L):
    n = L // NSEG
    nb = S5_N // LANE
    K = S5_K

    def body(br_ref, bi_ref, ar_ref, ai_ref, xr_ref, xi_ref):
        zero = jnp.zeros((NSEG, LANE), F32)
        A = [(jnp.broadcast_to(ar_ref[k], (NSEG, LANE)), jnp.broadcast_to(ai_ref[k], (NSEG, LANE))) for k in range(K)]

        def p1(j, st):
            new = []
            for k in range(K):
                sr, si, pr, pi = st[k]
                a_r, a_i = A[k]
                nr = a_r * sr - a_i * si + _seg_rows(br_ref, k, j, n)
                ni = a_r * si + a_i * sr + _seg_rows(bi_ref, k, j, n)
                _seg_store(xr_ref, k, j, n, nr)
                _seg_store(xi_ref, k, j, n, ni)
                new.append((nr, ni, a_r * pr - a_i * pi, a_r * pi + a_i * pr))
            return tuple(new)

        st = lax.fori_loop(0, n, p1, tuple((zero, zero, zero + 1.0, zero) for _ in range(K)))
        C = [_seg_carries(st[k][0], st[k][1], st[k][2], st[k][3], False) for k in range(K)]

        def p2(j, st):
            new = []
            for k in range(K):
                pr, pi = st[k]
                a_r, a_i = A[k]
                pr, pi = a_r * pr - a_i * pi, a_r * pi + a_i * pr
                cr, ci = C[k]
                _seg_store(xr_ref, k, j, n, _seg_rows(xr_ref, k, j, n) + pr * cr - pi * ci)
                _seg_store(xi_ref, k, j, n, _seg_rows(xi_ref, k, j, n) + pr * ci + pi * cr)
                new.append((pr, pi))
            return tuple(new)

        lax.fori_loop(0, n, p2, tuple((zero + 1.0, zero) for _ in range(K)))

    blk = pl.BlockSpec((K, L, LANE), lambda g: (g, 0, 0))
    ablk = pl.BlockSpec((K, 1, LANE), lambda g: (g, 0, 0))
    return pl.pallas_call(
        body, grid=(nb // K,), name="s5_scan_fwd",
        in_specs=[blk, blk, ablk, ablk], out_specs=[blk, blk],
        out_shape=[jax.ShapeDtypeStruct((nb, L, LANE), F32)] * 2,
        compiler_params=pltpu.CompilerParams(vmem_limit_bytes=VMEM_LIMIT),
    )(bur, bui, ar, ai)


def s5_scan_bwd(dxr, dxi, xr, xi, ar, ai, L):
    n = L // NSEG
    nb = S5_N // LANE
    K = S5_K

    def body(dr_ref, di_ref, xr_ref, xi_ref, ar_ref, ai_ref, gr_ref, gi_ref, dar_ref, dai_ref):
        zero = jnp.zeros((NSEG, LANE), F32)
        rows = lax.broadcasted_iota(jnp.int32, (NSEG, LANE), 0)
        A = [(jnp.broadcast_to(ar_ref[k], (NSEG, LANE)), -jnp.broadcast_to(ai_ref[k], (NSEG, LANE))) for k in range(K)]

        def p1(jj, st):
            j = n - 1 - jj
            new = []
            for k in range(K):
                sr, si, pr, pi = st[k]
                a_r, a_i = A[k]
                nr = a_r * sr - a_i * si + _seg_rows(dr_ref, k, j, n)
                ni = a_r * si + a_i * sr + _seg_rows(di_ref, k, j, n)
                _seg_store(gr_ref, k, j, n, nr)
                _seg_store(gi_ref, k, j, n, ni)
                new.append((nr, ni, a_r * pr - a_i * pi, a_r * pi + a_i * pr))
            return tuple(new)

        st = lax.fori_loop(0, n, p1, tuple((zero, zero, zero + 1.0, zero) for _ in range(K)))
        C = [_seg_carries(st[k][0], st[k][1], st[k][2], st[k][3], True) for k in range(K)]
        xb = [(jnp.where(rows == 0, 0.0, pltpu.roll(_seg_rows(xr_ref, k, n - 1, n), 1, 0)),
               jnp.where(rows == 0, 0.0, pltpu.roll(_seg_rows(xi_ref, k, n - 1, n), 1, 0))) for k in range(K)]

        def p2(jj, st):
            j = n - 1 - jj
            jp = jnp.maximum(j - 1, 0)
            new = []
            for k in range(K):
                pr, pi, acr, aci = st[k]
                a_r, a_i = A[k]
                pr, pi = a_r * pr - a_i * pi, a_r * pi + a_i * pr
                cr, ci = C[k]
                g_r = _seg_rows(gr_ref, k, j, n) + pr * cr - pi * ci
                g_i = _seg_rows(gi_ref, k, j, n) + pr * ci + pi * cr
                _seg_store(gr_ref, k, j, n, g_r)
                _seg_store(gi_ref, k, j, n, g_i)
                xpr = jnp.where(j == 0, xb[k][0], _seg_rows(xr_ref, k, jp, n))
                xpi = jnp.where(j == 0, xb[k][1], _seg_rows(xi_ref, k, jp, n))
                new.append((pr, pi, acr + g_r * xpr + g_i * xpi, aci + g_i * xpr - g_r * xpi))
            return tuple(new)

        st = lax.fori_loop(0, n, p2, tuple((zero + 1.0, zero, zero, zero) for _ in range(K)))
        for k in range(K):
            dar_ref[k] = jnp.sum(st[k][2], axis=0, keepdims=True)
            dai_ref[k] = jnp.sum(st[k][3], axis=0, keepdims=True)

    blk = pl.BlockSpec((K, L, LANE), lambda g: (g, 0, 0))
    ablk = pl.BlockSpec((K, 1, LANE), lambda g: (g, 0, 0))
    return pl.pallas_call(
        body, grid=(nb // K,), name="s5_scan_bwd",
        in_specs=[blk, blk, blk, blk, ablk, ablk], out_specs=[blk, blk, ablk, ablk],
        out_shape=[jax.ShapeDtypeStruct((nb, L, LANE), F32)] * 2 + [jax.ShapeDtypeStruct((nb, 1, LANE), F32)] * 2,
        compiler_params=pltpu.CompilerParams(vmem_limit_bytes=VMEM_LIMIT),
    )(dxr, dxi, xr, xi, ar, ai)


def rg_scan_fwd(a, b, L):
    n = L // NSEG

    def body(a_ref, b_ref, h_ref):
        zero = jnp.zeros((NSEG, LANE), F32)

        def p1(j, st):
            h, p = st
            aj = _seg_rows(a_ref, None, j, n)
            h = aj * h + _seg_rows(b_ref, None, j, n)
            _seg_store(h_ref, None, j, n, h)
            return h, aj * p

        e, pe = lax.fori_loop(0, n, p1, (zero, zero + 1.0))
        c, _ = _seg_carries(e, None, pe, None, False)

        def p2(j, p):
            p = _seg_rows(a_ref, None, j, n) * p
            _seg_store(h_ref, None, j, n, _seg_rows(h_ref, None, j, n) + p * c)
            return p

        lax.fori_loop(0, n, p2, zero + 1.0)

    blk = pl.BlockSpec((L, LANE), lambda g: (0, g))
    return pl.pallas_call(
        body, grid=(BW // LANE,), name="rg_scan_fwd", in_specs=[blk, blk], out_specs=blk,
        out_shape=jax.ShapeDtypeStruct((L, BW), F32),
        compiler_params=pltpu.CompilerParams(vmem_limit_bytes=VMEM_LIMIT),
    )(a, b)


def rg_scan_bwd(a, h, dh, L):
    n = L // NSEG

    def body(a_ref, h_ref, dh_ref, da_ref, db_ref):
        zero = jnp.zeros((NSEG, LANE), F32)
        rows = lax.broadcasted_iota(jnp.int32, (NSEG, LANE), 0)
        a_edge = jnp.where(rows == NSEG - 1, 0.0, pltpu.roll(_seg_rows(a_ref, None, 0, n), NSEG - 1, 0))
        h_edge = jnp.where(rows == 0, 0.0, pltpu.roll(_seg_rows(h_ref, None, n - 1, n), 1, 0))

        def mult(j):
            return jnp.where(j == n - 1, a_edge, _seg_rows(a_ref, None, jnp.minimum(j + 1, n - 1), n))

        def p1(jj, st):
            j = n - 1 - jj
            g, p = st
            m = mult(j)
            g = m * g + _seg_rows(dh_ref, None, j, n)
            _seg_store(db_ref, None, j, n, g)
            return g, m * p

        e, pe = lax.fori_loop(0, n, p1, (zero, zero + 1.0))
        c, _ = _seg_carries(e, None, pe, None, True)

        def p2(jj, p):
            j = n - 1 - jj
            p = mult(j) * p
            g = _seg_rows(db_ref, None, j, n) + p * c
            _seg_store(db_ref, None, j, n, g)
            hp = jnp.where(j == 0, h_edge, _seg_rows(h_ref, None, jnp.maximum(j - 1, 0), n))
            _seg_store(da_ref, None, j, n, g * hp)
            return p

        lax.fori_loop(0, n, p2, zero + 1.0)

    blk = pl.BlockSpec((L, LANE), lambda g: (0, g))
    return pl.pallas_call(
        body, grid=(BW // LANE,), name="rg_scan_bwd", in_specs=[blk, blk, blk], out_specs=[blk, blk],
        out_shape=[jax.ShapeDtypeStruct((L, BW), F32)] * 2,
        compiler_params=pltpu.CompilerParams(vmem_limit_bytes=VMEM_LIMIT),
    )(a, h, dh)


def _hg_consts(C):
    t = lax.broadcasted_iota(jnp.int32, (C, C), 0)
    s = lax.broadcasted_iota(jnp.int32, (C, C), 1)
    tril = (s <= t).astype(F32)
    diag = (s == t).astype(F32)
    levels = []
    k = 1
    while (1 << k) <= C:
        m = 1 << (k - 1)
        same = (t >> k) == (s >> k)
        t_right = ((t >> (k - 1)) & 1) == 1
        s_left = ((s >> (k - 1)) & 1) == 0
        mask = jnp.logical_and(same, jnp.logical_and(t_right, s_left)).astype(F32)
        bnd = ((t >> k) << k) + (m - 1)
        levels.append((mask, (s <= bnd).astype(F32)))
        k += 1
    return tril, diag, levels


def hg_chunk(st, q, z, v, lb):
    C = q.shape[0]
    tril, diag, levels = _hg_consts(C)
    sig = jax.nn.sigmoid(z)
    lf = jnp.log(lb + (1.0 - lb) * sig)
    k = (1.0 - lb) * jax.nn.sigmoid(-z)
    qh = jax.nn.silu(q)
    b = mm_exact(tril, lf)
    blast = jnp.sum(lf, axis=0, keepdims=True)
    qe = qh * jnp.exp(b)
    kd = k * jnp.exp(blast - b)
    scaled = []
    for level, (_, sel) in enumerate(levels):
        size = 2 << level
        if size >= NSEG:
            b3 = b.reshape(C // size, size, b.shape[1])
            ref = jnp.broadcast_to(b3[:, size // 2 - 1:size // 2, :], b3.shape).reshape(b.shape)
        else:
            ref = mm_exact(sel, lf)
        scaled.append((qh * jnp.exp(jnp.minimum(b - ref, 0.0)), k * jnp.exp(jnp.minimum(ref - b, 0.0))))
    outs, news = [], []
    for h in range(HG_HEADS):
        sl = slice(h * HG_D, (h + 1) * HG_D)
        st_h = st[h * HG_D:(h + 1) * HG_D, :]
        sc = diag * mma_nt(qh[:, sl], k[:, sl])
        for (mask, _), (qt, kt) in zip(levels, scaled):
            sc = sc + mask * mma_nt(qt[:, sl], kt[:, sl])
        outs.append(mma_nt(qe[:, sl], st_h) + mma_nn(sc, v[:, sl]))
        news.append(st_h * jnp.exp(blast[:, sl]) + mma_tn(v[:, sl], kd[:, sl]))
    return jnp.concatenate(news, axis=0), jnp.concatenate(outs, axis=1)


def hg_fwd(qzv, lb, L):
    C = HG_CHUNK
    nc = L // C

    def body(q_ref, z_ref, v_ref, lb_ref, o_ref, sst_ref, st_ref):
        @pl.when(pl.program_id(0) == 0)
        def _():
            st_ref[...] = jnp.zeros_like(st_ref)

        st = st_ref[...]
        sst_ref[...] = st
        new, o = hg_chunk(st, q_ref[...], z_ref[...], v_ref[...], lb_ref[...])
        st_ref[...] = new
        o_ref[...] = o

    col = lambda cb: pl.BlockSpec((C, BW), functools.partial(lambda c, cb: (c, cb), cb=cb))
    return pl.pallas_call(
        body, grid=(nc,), name="hg_fwd",
        in_specs=[col(0), col(1), col(2), pl.BlockSpec((1, BW), lambda c: (0, 0))],
        out_specs=[pl.BlockSpec((C, BW), lambda c: (c, 0)), pl.BlockSpec((None, BW, HG_D), lambda c: (c, 0, 0))],
        out_shape=[jax.ShapeDtypeStruct((L, BW), F32), jax.ShapeDtypeStruct((nc, BW, HG_D), F32)],
        scratch_shapes=[pltpu.VMEM((BW, HG_D), F32)],
        compiler_params=pltpu.CompilerParams(vmem_limit_bytes=VMEM_LIMIT, dimension_semantics=("arbitrary",)),
    )(qzv, qzv, qzv, lb)


def hg_bwd(qzv, lb, sst, do, L):
    C = HG_CHUNK
    nc = L // C

    def body(q_ref, z_ref, v_ref, lb_ref, sst_ref, do_ref, dq_ref, dz_ref, dv_ref, dlb_ref, dst_ref):
        @pl.when(pl.program_id(0) == 0)
        def _():
            dst_ref[...] = jnp.zeros_like(dst_ref)
            dlb_ref[...] = jnp.zeros_like(dlb_ref)

        _, vjp = jax.vjp(hg_chunk, sst_ref[...], q_ref[...], z_ref[...], v_ref[...], lb_ref[...])
        dst, dq, dz, dv, dlb = vjp((dst_ref[...], do_ref[...]))
        dst_ref[...] = dst
        dq_ref[...] = dq
        dz_ref[...] = dz
        dv_ref[...] = dv
        dlb_ref[...] += dlb

    col = lambda cb: pl.BlockSpec((C, BW), functools.partial(lambda c, cb: (nc - 1 - c, cb), cb=cb))
    rev = pl.BlockSpec((C, BW), lambda c: (nc - 1 - c, 0))
    return pl.pallas_call(
        body, grid=(nc,), name="hg_bwd",
        in_specs=[col(0), col(1), col(2), pl.BlockSpec((1, BW), lambda c: (0, 0)),
                  pl.BlockSpec((None, BW, HG_D), lambda c: (nc - 1 - c, 0, 0)), rev],
        out_specs=[rev, rev, rev, pl.BlockSpec((1, BW), lambda c: (0, 0))],
        out_shape=[jax.ShapeDtypeStruct((L, BW), F32)] * 3 + [jax.ShapeDtypeStruct((1, BW), F32)],
        scratch_shapes=[pltpu.VMEM((BW, HG_D), F32)],
        compiler_params=pltpu.CompilerParams(vmem_limit_bytes=VMEM_LIMIT, dimension_semantics=("arbitrary",)),
    )(qzv, qzv, qzv, lb, sst, do)


def _shift_down(x, d, rows, L):
    if d == 0:
        return x
    wrapped = jnp.where((rows & (NSEG - 1)) == 0, 0.0, pltpu.roll(x, NSEG * d + 1, 0))
    return jnp.where(rows < NSEG * d, wrapped, pltpu.roll(x, NSEG * d, 0))


def _shift_up(x, d, rows, L):
    if d == 0:
        return x
    wrapped = jnp.where((rows & (NSEG - 1)) == NSEG - 1, 0.0, pltpu.roll(x, L - (NSEG * d + 1), 0))
    return jnp.where(rows >= L - NSEG * d, wrapped, pltpu.roll(x, L - NSEG * d, 0))


def conv_fwd(proj, w, b, L):
    def body(x_ref, w_ref, b_ref, o_ref):
        x = x_ref[...]
        rows = lax.broadcasted_iota(jnp.int32, x.shape, 0)
        acc = jnp.broadcast_to(b_ref[...], x.shape)
        for k in range(CONV_W):
            acc = acc + w_ref[pl.ds(k, 1), :] * _shift_down(x, CONV_W - 1 - k, rows, L)
        o_ref[...] = acc

    nl = BW // LANE
    return pl.pallas_call(
        body, grid=(nl,), name="conv_fwd",
        in_specs=[pl.BlockSpec((L, LANE), lambda g: (0, 5 * nl + g)), pl.BlockSpec((CONV_W, LANE), lambda g: (0, g)),
                  pl.BlockSpec((1, LANE), lambda g: (0, g))],
        out_specs=pl.BlockSpec((L, LANE), lambda g: (0, g)),
        out_shape=jax.ShapeDtypeStruct((L, BW), F32),
        compiler_params=pltpu.CompilerParams(vmem_limit_bytes=VMEM_LIMIT),
    )(proj, w, b)


def conv_bwd(proj, w, dxc, L):
    def body(x_ref, w_ref, d_ref, dx_ref, dw_ref, db_ref):
        x, d = x_ref[...], d_ref[...]
        rows = lax.broadcasted_iota(jnp.int32, x.shape, 0)
        acc = jnp.zeros_like(x)
        for k in range(CONV_W):
            acc = acc + w_ref[pl.ds(k, 1), :] * _shift_up(d, CONV_W - 1 - k, rows, L)
            dw_ref[pl.ds(k, 1), :] = jnp.sum(d * _shift_down(x, CONV_W - 1 - k, rows, L), axis=0, keepdims=True)
        dx_ref[...] = acc
        db_ref[...] = jnp.sum(d, axis=0, keepdims=True)

    nl = BW // LANE
    blk = pl.BlockSpec((L, LANE), lambda g: (0, g))
    return pl.pallas_call(
        body, grid=(nl,), name="conv_bwd",
        in_specs=[pl.BlockSpec((L, LANE), lambda g: (0, 5 * nl + g)), pl.BlockSpec((CONV_W, LANE), lambda g: (0, g)), blk],
        out_specs=[blk, pl.BlockSpec((CONV_W, LANE), lambda g: (0, g)), pl.BlockSpec((1, LANE), lambda g: (0, g))],
        out_shape=[jax.ShapeDtypeStruct((L, BW), F32), jax.ShapeDtypeStruct((CONV_W, BW), F32),
                   jax.ShapeDtypeStruct((1, BW), F32)],
        compiler_params=pltpu.CompilerParams(vmem_limit_bytes=VMEM_LIMIT),
    )(proj, w, dxc)


def loss_fwd_bwd(x, fw, target, L, tm):
    def fn(x, fw, t):
        err = jnp.square(_rms(x, fw) - t)
        return jnp.sum(0.5 * jnp.mean(err, axis=-1, keepdims=True), axis=0, keepdims=True)

    def body(x_ref, fw_ref, t_ref, l_ref, dx_ref, dfw_ref):
        i = pl.program_id(0)
        t = t_ref[...]
        val, vjp = jax.vjp(lambda x, fw: fn(x, fw, t), x_ref[...], fw_ref[...])
        dx, dfw = vjp(jnp.ones((1, 1), F32))
        dx_ref[...] = dx

        @pl.when(i == 0)
        def _():
            l_ref[...] = jnp.zeros_like(l_ref)
            dfw_ref[...] = jnp.zeros_like(dfw_ref)

        l_ref[...] += jnp.broadcast_to(val, l_ref.shape)
        dfw_ref[...] += dfw

    row = pl.BlockSpec((tm, D_MODEL), lambda i: (i, 0))
    vec = pl.BlockSpec((1, D_MODEL), lambda i: (0, 0))
    return pl.pallas_call(
        body, grid=(L // tm,), name="loss_fwd_bwd", in_specs=[row, vec, row],
        out_specs=[pl.BlockSpec((1, LANE), lambda i: (0, 0)), row, vec],
        out_shape=[jax.ShapeDtypeStruct((1, LANE), F32), jax.ShapeDtypeStruct((L, D_MODEL), F32),
                   jax.ShapeDtypeStruct((1, D_MODEL), F32)],
        compiler_params=pltpu.CompilerParams(vmem_limit_bytes=VMEM_LIMIT, dimension_semantics=("arbitrary",)),
    )(x, fw, target)


def adamw(w, g, m, v):
    rows, cols = w.shape
    tr = _row_tile(rows, cols, budget=1024 * 1024)
    c1 = 1.0 - ADAM_B1 ** ADAM_STEP
    c2 = 1.0 - ADAM_B2 ** ADAM_STEP

    def body(w_ref, g_ref, m_ref, v_ref, d_ref, nm_ref, nv_ref):
        g = g_ref[...]
        nm = ADAM_B1 * m_ref[...] + (1.0 - ADAM_B1) * g
        nv = ADAM_B2 * v_ref[...] + (1.0 - ADAM_B2) * jnp.square(g)
        d_ref[...] = -ADAM_LR * ((nm / c1) / (jnp.sqrt(nv / c2) + ADAM_EPS) + ADAM_WD * w_ref[...])
        nm_ref[...] = nm
        nv_ref[...] = nv

    blk = pl.BlockSpec((tr, cols), lambda i: (i, 0))
    return pl.pallas_call(
        body, grid=(rows // tr,), name="adamw", in_specs=[blk] * 4, out_specs=[blk] * 3,
        out_shape=[jax.ShapeDtypeStruct((rows, cols), F32)] * 3,
    )(w, g, m, v)


def s5_prep(lam_re, lam_im, log_dt, b_re, b_im, c_re, c_im):
    lr = jnp.minimum(lam_re, -1e-4)
    li = lam_im
    dt = jnp.exp(log_dt)[:, None]
    mag = jnp.exp(lr * dt)
    ar = mag * jnp.cos(li * dt)
    ai = mag * jnp.sin(li * dt)
    den = lr * lr + li * li
    fr = ((ar - 1.0) * lr + ai * li) / den
    fi = (ai * lr - (ar - 1.0) * li) / den
    bbr = fr[..., None] * b_re - fi[..., None] * b_im
    bbi = fr[..., None] * b_im + fi[..., None] * b_re
    hg = S5_GROUPS // 2
    emb_b = lambda bb: _block_diag(bb.transpose(0, 2, 1).reshape(hg * S5_GROUP, S5_STATE), hg)
    emb_c = lambda cc: _block_diag(cc.transpose(0, 2, 1).reshape(hg * S5_STATE, S5_GROUP), hg)
    bsub = jnp.stack([emb_b(bbr[:hg]), emb_b(bbr[hg:]), emb_b(bbi[:hg]), emb_b(bbi[hg:])])
    csub = jnp.stack([jnp.concatenate([emb_c(c_re[:hg]), -emb_c(c_im[:hg])], axis=0),
                      jnp.concatenate([emb_c(c_re[hg:]), -emb_c(c_im[hg:])], axis=0)])
    nb = S5_N // LANE
    return ar.reshape(nb, 1, LANE), ai.reshape(nb, 1, LANE), bsub, csub


def _block_diag(stacked, groups):
    rows, c = stacked.shape
    r = rows // groups
    row_g = jnp.arange(rows)[:, None] // r
    col_g = jnp.arange(groups * c)[None, :] // c
    return jnp.where(row_g == col_g, jnp.tile(stacked, (1, groups)), 0.0)


def rg_prep(w):
    return _block_diag(w.reshape(BW, RG_BLOCK), RG_BLOCKS)


def hg_prep(logits):
    p = jax.nn.softmax(logits, axis=0)
    return jnp.cumsum(p, axis=0) - p[0]


def _head_mean_matrix():
    r = jnp.arange(BW) // HG_D
    return (r[:, None] == r[None, :]).astype(F32) / HG_D


def _to_segment_order(a):
    L = a.shape[0]
    return a.reshape(NSEG, L // NSEG, -1).transpose(1, 0, 2).reshape(a.shape)


def _to_time_order(a):
    L = a.shape[0]
    return a.reshape(L // NSEG, NSEG, -1).transpose(1, 0, 2).reshape(a.shape)


def _const(*idx):
    return lambda s, i: idx


def _rows(cb=0):
    return lambda s, i: (i, cb)


def _sum_parts(name, first, parts, shape):
    return add_n(name, [(first, ())] + [(parts, (s,)) for s in range(NSH)], shape)


def _ffn_weight_specs(l, j):
    F = D_FF // NSH
    one = pl.Buffered(1)
    return [pl.BlockSpec((None, NSH, D_MODEL, F), lambda i: (j, 0, 0, 0), pipeline_mode=one),
            pl.BlockSpec((None, NSH, D_MODEL, F), lambda i: (j, 0, 0, 0), pipeline_mode=one),
            pl.BlockSpec((None, NSH, F, D_MODEL), lambda i: (j, 0, 0, 0), pipeline_mode=one)]


def ffn_fwd(name, x, W, l, j, k, L, tm):
    D, F = D_MODEL, D_FF // NSH

    def body(x_ref, nw_ref, wg_ref, wu_ref, wd_ref, y_ref, g_ref, u_ref):
        x = x_ref[...]
        h = _rms(x, nw_ref[...]).astype(MMT)
        y = x
        for s in range(NSH):
            g = _dg(h, wg_ref[s], 1, 0)
            u = _dg(h, wu_ref[s], 1, 0)
            g_ref[s] = g.astype(g_ref.dtype)
            u_ref[s] = u.astype(u_ref.dtype)
            y = y + 0.5 * _dg((jax.nn.silu(g) * u).astype(MMT), wd_ref[s], 1, 0)
        y_ref[...] = y

    row = pl.BlockSpec((tm, D), lambda i: (i, 0))
    act = pl.BlockSpec((NSH, tm, F), lambda i: (0, i, 0))
    return pl.pallas_call(
        body, grid=(L // tm,), name=name,
        in_specs=[row, pl.BlockSpec((None, None, 1, D), lambda i: (l, k, 0, 0))] + _ffn_weight_specs(l, j),
        out_specs=[row, act, act],
        out_shape=[jax.ShapeDtypeStruct((L, D), F32), jax.ShapeDtypeStruct((NSH, L, F), MMT),
                   jax.ShapeDtypeStruct((NSH, L, F), MMT)],
        compiler_params=pltpu.CompilerParams(vmem_limit_bytes=VMEM_LIMIT, dimension_semantics=("arbitrary",)),
    )(x, W["nw"], W["L"][l]["wg"], W["L"][l]["wu"], W["L"][l]["wd"])


def ffn_bwd(name, x, g, u, dy, W, bufs, l, j, k, L, tm):
    D, F = D_MODEL, D_FF // NSH
    tm = min(TM_WGRAD, L)

    def body(x_ref, nw_ref, dy_ref, g_ref, u_ref, wg_ref, wu_ref, wd_ref, *rest):
        part_ref, dnw_ref, dwg_ref, dwu_ref, dwd_ref = rest[-5:]
        s, i = pl.program_id(0), pl.program_id(1)
        x, nw = x_ref[...], nw_ref[...]
        r = lax.rsqrt(jnp.mean(x * x, axis=-1, keepdims=True) + EPS)
        xhat = x * r
        h = (xhat * nw).astype(MMT)
        half_dy = (0.5 * dy_ref[...]).astype(MMT)
        gs, us = g_ref[...].astype(F32), u_ref[...].astype(F32)
        sig = jax.nn.sigmoid(gs)
        act = gs * sig
        da = _dg(half_dy, wd_ref[...], 1, 1)
        du = (da * act).astype(MMT)
        dg = (da * us * (sig * (1.0 + gs * (1.0 - sig)))).astype(MMT)
        dh = _dg(dg, wg_ref[...], 1, 1) + _dg(du, wu_ref[...], 1, 1)
        dxh = dh * nw
        part_ref[...] = r * (dxh - xhat * jnp.mean(dxh * xhat, axis=-1, keepdims=True))
        grads = (_dg(h, dg, 0, 0), _dg(h, du, 0, 0), _dg((act * us).astype(MMT), half_dy, 0, 0))
        dnw = jnp.sum(dh * xhat, axis=0, keepdims=True)
        first = jnp.logical_and(s == 0, i == 0)
        for ref, val, start in zip((dwg_ref, dwu_ref, dwd_ref, dnw_ref), grads + (dnw,), (i == 0, i == 0, i == 0, first)):
            @pl.when(start)
            def _(ref=ref, val=val):
                ref[...] = val

            @pl.when(jnp.logical_not(start))
            def _(ref=ref, val=val):
                ref[...] += val

    row = pl.BlockSpec((tm, D), lambda s, i: (i, 0))
    act = pl.BlockSpec((None, tm, F), lambda s, i: (s, i, 0))
    wsp = lambda r, c: pl.BlockSpec((None, None, r, c), lambda s, i: (j, s, 0, 0))
    gsp = lambda r, c: pl.BlockSpec((None, None, r, c), lambda s, i: (0, s, 0, 0))
    part, dnw, bufs[("ffn_gate", l, j)], bufs[("ffn_up", l, j)], bufs[("ffn_down", l, j)] = pl.pallas_call(
        body, grid=(NSH, L // tm), name=name,
        in_specs=[row, pl.BlockSpec((None, None, 1, D), lambda s, i: (l, k, 0, 0)), row, act, act,
                  wsp(D, F), wsp(D, F), wsp(F, D)],
        out_specs=[pl.BlockSpec((None, tm, D), lambda s, i: (s, i, 0)), pl.BlockSpec((1, D), lambda s, i: (0, 0)),
                   gsp(D, F), gsp(D, F), gsp(F, D)],
        out_shape=[jax.ShapeDtypeStruct((NSH, L, D), F32), jax.ShapeDtypeStruct((1, D), F32)]
        + [jax.ShapeDtypeStruct((1, NSH, D, F), F32)] * 2 + [jax.ShapeDtypeStruct((1, NSH, F, D), F32)],
        compiler_params=pltpu.CompilerParams(vmem_limit_bytes=VMEM_LIMIT, dimension_semantics=("arbitrary", "arbitrary")),
    )(x, W["nw"], dy, g, u, W["L"][l]["wg"], W["L"][l]["wu"], W["L"][l]["wd"])
    return _sum_parts(name + "_dx", dy, part, (L, D)), dnw


def layer_fwd(l, x0, W, P, L, tm):
    D = D_MODEL
    tmm = tm
    tm = min(TM_FWD, L)
    n_i = L // tm
    x1, g0, u0 = ffn_fwd(f"ffn_fwd_{l}0", x0, W, l, 0, 0, L, tm)
    proj = tile_fwd(
        lambda x, nw, win, s: pre_core(x, nw, win), f"pre_fwd_{l}", n_i, NSH,
        [(x1, (tm, D), _rows()), (W["nw"], (None, None, 1, D), _const(l, 1, 0, 0)),
         (W["L"][l]["win"], (None, D, IN_TOTAL // NSH), lambda s, i: (s, 0, 0))],
        [((L, IN_TOTAL), F32, (tm, IN_TOTAL // NSH), lambda s, i: (i, s), False)], s_outer=True)[0]
    nb = S5_N // LANE
    blk3 = lambda s, i: (0, i, 0)
    bur, bui = tile_fwd(
        lambda *a: s5_pre_core(*a[:-1]), f"s5pre_fwd_{l}", n_i, 1,
        [(proj, (tm, BW), _rows(0))] + [(P["bsub"], (None, None, BW // 2, S5_N // 2), _const(l, q, 0, 0)) for q in range(4)],
        [((nb, L, LANE), F32, (nb, tm, LANE), blk3, False)] * 2)
    xr, xi = s5_scan_fwd(bur, bui, P["ar"][l], P["ai"][l], L)
    qzv = _to_time_order(proj[:, BW:4 * BW])
    o_t, sst = hg_fwd(qzv, P["lb"][l], L)
    o = _to_segment_order(o_t)
    xc = conv_fwd(proj, W["convw"][l], P["convb"][l], L)
    vec = (None, 1, BW)
    a, b = tile_fwd(
        lambda xc, wa, ba, wx, bx, lam, s: gates_core(xc, wa, ba, wx, bx, lam), f"gates_fwd_{l}", n_i, 1,
        [(xc, (tm, BW), _rows()), (P["wa"], (None, BW, BW), _const(l, 0, 0)), (P["ba"], vec, _const(l, 0, 0)),
         (P["wx"], (None, BW, BW), _const(l, 0, 0)), (P["bx"], vec, _const(l, 0, 0)), (P["lam"], vec, _const(l, 0, 0))],
        [((L, BW), F32, (tm, BW), _rows(), False)] * 2)
    hs = rg_scan_fwd(a, b, L)
    ya, yb, yc = tile_fwd(
        lambda *a: mid_core(*a[:-1]), f"mid_fwd_{l}", L // tmm, 1,
        [(xr, (nb, tmm, LANE), blk3), (xi, (nb, tmm, LANE), blk3), (proj, (tmm, BW), _rows(0)), (o, (tmm, BW), _rows()),
         (proj, (tmm, BW), _rows(4)), (hs, (tmm, BW), _rows()), (proj, (tmm, BW), _rows(6)),
         (P["hmat"], (BW, BW), _const(0, 0)), (P["csub"], (None, None, S5_N, BW // 2), _const(l, 0, 0, 0)),
         (P["csub"], (None, None, S5_N, BW // 2), _const(l, 1, 0, 0)), (P["d"], vec, _const(l, 0, 0)),
         (W["L"][l]["gluw"], (BW, BW), _const(0, 0)), (P["glub"], vec, _const(l, 0, 0)), (P["hgw"], vec, _const(l, 0, 0))],
        [((L, BW), F32, (tmm, BW), _rows(), False)] * 3)
    x2 = tile_fwd(
        lambda x, *rest: (x + merge_core(*rest[:-1])[0],), f"merge_fwd_{l}", n_i, 1,
        [(x1, (tm, D), _rows()), (ya, (tm, BW), _rows()), (yb, (tm, BW), _rows()), (yc, (tm, BW), _rows())]
        + [(proj, (tm, BW), _rows(7 + k)) for k in range(6)]
        + [(W["L"][l]["pfull"], (3, BW, D), _const(0, 0, 0)), (W["L"][l]["woutfull"], (D, D), _const(0, 0))],
        [((L, D), F32, (tm, D), _rows(), False)])[0]
    x3, g1, u1 = ffn_fwd(f"ffn_fwd_{l}1", x2, W, l, 1, 2, L, tm)
    saved = dict(x0=x0, x1=x1, x2=x2, proj=proj, xr=xr, xi=xi, o=o, sst=sst, xc=xc, a=a, hs=hs, ya=ya, yb=yb, yc=yc,
                 qzv=qzv, g0=g0, u0=u0, g1=g1, u1=u1)
    return x3, saved


def layer_bwd(l, dx3, sv, W, P, bufs, L, tm, ready=lambda l, group: None):
    D = D_MODEL
    n_i = L // tm
    nb = S5_N // LANE
    dq = D // NSH
    vec = (None, 1, BW)
    vout = ((1, BW), (1, BW), _const(0, 0), "acc_all")
    blk3 = lambda s, i: (0, i, 0)
    small = {}
    proj = sv["proj"]

    dx2, dnw2 = ffn_bwd(f"ffn_bwd_{l}1", sv["x2"], sv["g1"], sv["u1"], dx3, W, bufs, l, 1, 2, L, tm)
    ready(l, "ffn1")

    rw256 = ((L, BW), (tm, BW), _rows(), "write")
    res = tile_bwd(
        merge_core, f"merge_bwd_{l}", n_i, 1,
        [(sv["ya"], (tm, BW), _rows(), "r"), (sv["yb"], (tm, BW), _rows(), "r"), (sv["yc"], (tm, BW), _rows(), "r")]
        + [(proj, (tm, BW), _rows(7 + k), "r") for k in range(6)]
        + [(W["L"][l]["pfull"], (3, BW, D), _const(0, 0, 0), "w"), (W["L"][l]["woutfull"], (D, D), _const(0, 0), "w")],
        [(dx2, (tm, D), _rows())],
        [rw256] * 9
        + [((3, BW, D), (3, BW, D), _const(0, 0, 0), "acc_all"), ((D, D), (D, D), _const(0, 0), "acc_all")])
    dya, dyb, dyc = res[:3]
    dgm = res[3:9]
    bufs[("branch_proj", l)], bufs[("w_out", l)] = res[9:]
    ready(l, "merge")

    tmm = tm
    rw = ((L, BW), (tmm, BW), _rows(), "write")
    xw = ((nb, L, LANE), (nb, tmm, LANE), blk3, "write")
    res = tile_bwd(
        mid_core, f"mid_bwd_{l}", L // tmm, 1,
        [(sv["xr"], (nb, tmm, LANE), blk3, "r"), (sv["xi"], (nb, tmm, LANE), blk3, "r"), (proj, (tmm, BW), _rows(0), "r"),
         (sv["o"], (tmm, BW), _rows(), "r"), (proj, (tmm, BW), _rows(4), "r"), (sv["hs"], (tmm, BW), _rows(), "r"),
         (proj, (tmm, BW), _rows(6), "r"), (P["hmat"], (BW, BW), _const(0, 0), "c"),
         (P["csub"], (None, None, S5_N, BW // 2), _const(l, 0, 0, 0), "w"),
         (P["csub"], (None, None, S5_N, BW // 2), _const(l, 1, 0, 0), "w"), (P["d"], vec, _const(l, 0, 0), "p"),
         (W["L"][l]["gluw"], (BW, BW), _const(0, 0), "w"), (P["glub"], vec, _const(l, 0, 0), "p"),
         (P["hgw"], vec, _const(l, 0, 0), "p")],
        [(dya, (tmm, BW), _rows()), (dyb, (tmm, BW), _rows()), (dyc, (tmm, BW), _rows())],
        [xw, xw, rw, rw, rw, rw, rw,
         ((DEPTH, S5_N, BW // 2), (None, S5_N, BW // 2), _const(l, 0, 0), "acc_all", bufs.get("csub0")),
         ((DEPTH, S5_N, BW // 2), (None, S5_N, BW // 2), _const(l, 0, 0), "acc_all", bufs.get("csub1")), vout,
         ((BW, BW), (BW, BW), _const(0, 0), "acc_all"), vout, vout])
    dxr, dxi, du_skip, do, dg_b, dhs, dgate_c, bufs["csub0"], bufs["csub1"], dd, bufs[("s5_glu_w", l)], dglub, dhgw = res
    small["s5_d"], small["s5_glu_b"], small["hg_norm_w"] = dd[0], dglub[0], dhgw[0]
    ready(l, "mid")

    da, db = rg_scan_bwd(sv["a"], sv["hs"], dhs, L)
    wmat = lambda key: ((DEPTH, BW, BW), (None, BW, BW), _const(l, 0, 0), "acc_all", bufs.get(key))
    res = tile_bwd(
        gates_core, f"gates_bwd_{l}", n_i, 1,
        [(sv["xc"], (tm, BW), _rows(), "r"), (P["wa"], (None, BW, BW), _const(l, 0, 0), "w"), (P["ba"], vec, _const(l, 0, 0), "p"),
         (P["wx"], (None, BW, BW), _const(l, 0, 0), "w"), (P["bx"], vec, _const(l, 0, 0), "p"), (P["lam"], vec, _const(l, 0, 0), "p")],
        [(da, (tm, BW), _rows()), (db, (tm, BW), _rows())],
        [((L, BW), (tm, BW), _rows(), "write"), wmat("wa"), vout, wmat("wx"), vout, vout])
    dxc, bufs["wa"], dba, bufs["wx"], dbx, dlam = res
    small["rg_ba"], small["rg_bx"], small["rg_lambda"] = dba[0], dbx[0], dlam[0]
    dx_c, dconvw, dconvb = conv_bwd(proj, W["convw"][l], dxc, L)
    small["rg_conv_w"], small["rg_conv_b"] = dconvw, dconvb[0]

    dq_b, dz_b, dv_b, dlb = hg_bwd(sv["qzv"], P["lb"][l], sv["sst"], _to_time_order(do), L)
    dq_b, dz_b, dv_b = [_to_segment_order(a) for a in (dq_b, dz_b, dv_b)]

    gr, gi, dar, dai = s5_scan_bwd(dxr, dxi, sv["xr"], sv["xi"], P["ar"][l], P["ai"][l], L)
    bblk = (None, None, BW // 2, S5_N // 2)
    res = tile_bwd(
        s5_pre_core, f"s5pre_bwd_{l}", n_i, 1,
        [(proj, (tm, BW), _rows(0), "r")] + [(P["bsub"], bblk, _const(l, q, 0, 0), "w") for q in range(4)],
        [(gr, (nb, tm, LANE), blk3), (gi, (nb, tm, LANE), blk3)],
        [((L, BW), (tm, BW), _rows(), "write")]
        + [((DEPTH, BW // 2, S5_N // 2), bblk[1:], _const(l, 0, 0), "acc_all", bufs.get(f"bsub{q}")) for q in range(4)])
    du_pre = res[0]
    for q in range(4):
        bufs[f"bsub{q}"] = res[1 + q]
    du_a = add_n(f"du_a_{l}", [(du_skip, ()), (du_pre, ())], (L, BW))
    prep_ct = dict(dar=dar, dai=dai, dlb=dlb)

    pieces = [du_a, dq_b, dz_b, dv_b, dg_b, dx_c, dgate_c, *dgm]
    per_piece, per_shard = BW // LANE, IN_TOTAL // NSH // LANE
    part, dnw1 = None, []
    tmw = min(TM_WGRAD, L)
    for s in range(NSH):
        groups = [(pieces[g // per_piece], (tmw, LANE), _rows(g % per_piece))
                  for g in range(s * per_shard, (s + 1) * per_shard)]
        part, dnw_s, bufs[("w_in", l)] = tile_bwd(
            pre_core, f"pre_bwd_{l}{s}", L // tmw, 1,
            [(sv["x1"], (tmw, D), _rows(), "r"), (W["nw"], (None, None, 1, D), _const(l, 1, 0, 0), "p"),
             (W["L"][l]["win"], (None, D, IN_TOTAL // NSH), _const(s, 0, 0), "w")],
            [groups],
            [((NSH, L, D), (None, tmw, D), functools.partial(lambda _s, i, s: (s, i, 0), s=s), "write", part),
             ((1, D), (1, D), _const(0, 0), "acc_all"),
             ((1, NSH, D, IN_TOTAL // NSH), (None, None, D, IN_TOTAL // NSH), _const(0, s, 0, 0), "acc_all",
              bufs.get(("w_in", l)))])
        dnw1.append(dnw_s)
    dnw1 = (dnw1[0] + dnw1[1]) + (dnw1[2] + dnw1[3])
    dx1 = _sum_parts(f"pre_bwd_{l}_dx", dx2, part, (L, D))
    ready(l, "pre")

    dx0, dnw0 = ffn_bwd(f"ffn_bwd_{l}0", sv["x0"], sv["g0"], sv["u0"], dx1, W, bufs, l, 0, 0, L, tm)
    ready(l, "ffn0")
    small["norm_w"] = jnp.concatenate([dnw0, dnw1, dnw2], axis=0)
    return dx0, small, prep_ct


SMALL_RAW = ("s5_lambda_re", "s5_lambda_im", "s5_log_dt", "s5_b_re", "s5_b_im", "s5_c_re", "s5_c_im", "s5_d", "s5_glu_b",
             "hg_lb_logits", "hg_norm_w", "rg_conv_b", "rg_wa", "rg_ba", "rg_wx", "rg_bx", "rg_lambda", "final_norm_w")
DEPTH = 2


def local_step(x, target, W, raw, layer_weights=None, layer_grads=None):
    L = x.shape[0]
    tm = min(256, L)
    col = lambda v: v.reshape(DEPTH, 1, BW)
    (ar, ai, bsub, csub), s5_vjp = jax.vjp(jax.vmap(s5_prep), *[raw[k] for k in SMALL_RAW[:7]])
    (wa, wx), rg_vjp = jax.vjp(lambda a, b: (jax.vmap(rg_prep)(a), jax.vmap(rg_prep)(b)), raw["rg_wa"], raw["rg_wx"])
    lb, hg_vjp = jax.vjp(hg_prep, raw["hg_lb_logits"])
    P = dict(
        ar=[ar[l] for l in range(DEPTH)], ai=[ai[l] for l in range(DEPTH)],
        bsub=bsub.astype(MMT), csub=csub.astype(MMT), wa=wa.astype(MMT), wx=wx.astype(MMT),
        lb=[lb[l].reshape(1, BW) for l in range(DEPTH)], convb=[raw["rg_conv_b"][l].reshape(1, BW) for l in range(DEPTH)],
        ba=col(raw["rg_ba"]), bx=col(raw["rg_bx"]), lam=col(raw["rg_lambda"]), d=col(raw["s5_d"]),
        glub=col(raw["s5_glu_b"]), hgw=col(raw["hg_norm_w"]), hmat=_head_mean_matrix())

    saved = []
    h = _to_segment_order(x)
    for l in range(DEPTH):
        if layer_weights is not None:
            W["L"][l], h = layer_weights(l, h)
        h, sv = layer_fwd(l, h, W, P, L, tm)
        saved.append(sv)
    loss, dh, dfw = loss_fwd_bwd(h, raw["final_norm_w"].reshape(1, D_MODEL), _to_segment_order(target), L, tm)

    big, per_layer, prep_cts = {}, [None] * DEPTH, [None] * DEPTH
    ready = (lambda l, group: None) if layer_grads is None else (lambda l, group: layer_grads(l, group, big))
    for l in reversed(range(DEPTH)):
        dh, sm, pc = layer_bwd(l, dh, saved[l], W, P, big, L, tm, ready)
        per_layer[l], prep_cts[l] = sm, pc
    dh = _to_time_order(dh)

    small = {k: jnp.stack([per_layer[l][k] for l in range(DEPTH)]) for k in per_layer[0]}
    both = lambda k: jnp.stack([prep_cts[l][k] for l in range(DEPTH)])
    dbsub = jnp.stack([big.pop(f"bsub{q}") for q in range(4)], axis=1)
    dcsub = jnp.stack([big.pop("csub0"), big.pop("csub1")], axis=1)
    s5_g = s5_vjp((both("dar"), both("dai"), dbsub, dcsub))
    small.update(zip(SMALL_RAW[:7], s5_g))
    small["rg_wa"], small["rg_wx"] = rg_vjp((big.pop("wa"), big.pop("wx")))
    (small["hg_lb_logits"],) = hg_vjp(jnp.concatenate([prep_cts[l]["dlb"] for l in range(DEPTH)], axis=0))
    small["final_norm_w"] = dfw[0]
    return loss, dh, big, small


ANY = pl.BlockSpec(memory_space=pl.ANY)


def _place():
    x, y, c = lax.axis_index("x"), lax.axis_index("y"), lax.axis_index("c")
    chips = [(1 - x, y), (x, 1 - y), (1 - x, 1 - y)]
    return x, y, c, chips


def _remote(src, dst, send, recv, k, to):
    return pltpu.make_async_remote_copy(src_ref=src, dst_ref=dst, send_sem=send.at[k], recv_sem=recv.at[k],
                                        device_id=to, device_id_type=MESH)


def _comm_call(body, name, ins, out_shapes, n_sem, n_loc):
    return pl.pallas_call(
        body, name=name, in_specs=[ANY] * len(ins), out_specs=[ANY] * len(out_shapes), out_shape=out_shapes,
        scratch_shapes=[pltpu.SemaphoreType.DMA((n_sem,)), pltpu.SemaphoreType.DMA((n_sem,)),
                        pltpu.SemaphoreType.DMA((max(n_loc, 1),))],
    )(*ins)


def gather_shards(name, shards):
    n = len(shards)
    per = 8

    def body(*refs):
        ins, outs = refs[:n], refs[n:2 * n]
        send, recv, _ = refs[2 * n:]
        x, y, c, chips = _place()
        me = 2 * x + y
        sib = (x, y, 1 - c)
        sends = []
        for w in range(n):
            for j, (cx, cy) in enumerate(chips):
                cp = _remote(ins[w].at[c], outs[w].at[c, me], send, recv, per * w + j, (cx, cy, c))
                cp.start()
                sends.append(cp)
        for w in range(n):
            for l in range(2):
                cp = _remote(ins[w].at[l], outs[w].at[l, me], send, recv, per * w + 6 + l, sib)
                cp.start()
                sends.append(cp)
        for w in range(n):
            for j, (cx, cy) in enumerate(chips):
                theirs = outs[w].at[c, 2 * cx + cy]
                _remote(ins[w].at[c], theirs, send, recv, per * w + j, (cx, cy, c)).wait_recv()
                cp = _remote(theirs, theirs, send, recv, per * w + 3 + j, sib)
                cp.start()
                sends.append(cp)
        for w in range(n):
            for j, (cx, cy) in enumerate(chips):
                dst = outs[w].at[1 - c, 2 * cx + cy]
                _remote(dst, dst, send, recv, per * w + 3 + j, sib).wait_recv()
            for l in range(2):
                dst = outs[w].at[l, me]
                _remote(dst, dst, send, recv, per * w + 6 + l, sib).wait_recv()
        for cp in sends:
            cp.wait_send()

    shapes = [jax.ShapeDtypeStruct((2, NSH) + s.shape[1:], s.dtype) for s in shards]
    return _comm_call(body, name, shards, shapes, per * n, 0)


def exchange_halves(name, grads, ranges):
    n = len(grads)

    def body(*refs):
        ins, outs = refs[:n], refs[n:2 * n]
        send, recv, _ = refs[2 * n:]
        x, y, c, _chips = _place()
        cps = []
        for w in range(n):
            h = grads[w].shape[2] // 2
            p0, np_ = ranges[w]
            cp = _remote(ins[w].at[pl.ds(p0, np_), :, pl.ds((1 - c) * h, h)], outs[w], send, recv, w, (x, y, 1 - c))
            cp.start()
            cps.append(cp)
        for cp in cps:
            cp.wait()

    shapes = [jax.ShapeDtypeStruct((r[1], NSH, g.shape[2] // 2, g.shape[3]), g.dtype) for g, r in zip(grads, ranges)]
    return _comm_call(body, name, grads, shapes, n, 0)


def share_halves(name, pieces):
    n = len(pieces)

    def body(*refs):
        ins, outs = refs[:n], refs[n:2 * n]
        send, recv, _ = refs[2 * n:]
        x, y, c, _chips = _place()
        cps = []
        for w in range(n):
            cp = _remote(ins[w], outs[w], send, recv, w, (x, y, 1 - c))
            cp.start()
            cps.append(cp)
        for cp in cps:
            cp.wait()

    return _comm_call(body, name, pieces, [jax.ShapeDtypeStruct(p.shape, p.dtype) for p in pieces], n, 0)


def add_own_half(name, g, ra, c, wire, b0):
    nblk, h, cols = ra.shape
    tr = _row_tile(h, cols, mult=16)
    nt = h // tr

    def body(c_ref, g_ref, r_ref, o_ref):
        o_ref[...] = (g_ref[...] + r_ref[...]).astype(o_ref.dtype)

    blk = (None, tr, cols)
    return pl.pallas_call(
        body, name=name,
        grid_spec=pltpu.PrefetchScalarGridSpec(
            num_scalar_prefetch=1, grid=(nblk, nt),
            in_specs=[pl.BlockSpec(blk, lambda s, i, c_ref: (b0 + s, c_ref[0] * nt + i, 0)), pl.BlockSpec(blk, lambda s, i, c_ref: (s, i, 0))],
            out_specs=pl.BlockSpec(blk, lambda s, i, c_ref: (s, i, 0))),
        out_shape=jax.ShapeDtypeStruct(ra.shape, wire),
    )(c.reshape(1), g, ra)


def add_chips(name, hb, rb, me):
    npc, _, h, cols = hb.shape
    tr = _row_tile(h, cols, mult=16)

    def body(me_ref, h_ref, r0, r1, r2, o_ref):
        f = lambda r: r[...].astype(F32)
        o_ref[...] = ((f(h_ref) + f(r0)) + f(r1)) + f(r2)

    rspec = lambda j: pl.BlockSpec((None, None, tr, cols), functools.partial(lambda p, i, me_ref, j: (j, p, i, 0), j=j))
    return pl.pallas_call(
        body, name=name,
        grid_spec=pltpu.PrefetchScalarGridSpec(
            num_scalar_prefetch=1, grid=(npc, h // tr),
            in_specs=[pl.BlockSpec((None, None, tr, cols), lambda p, i, me_ref: (p, me_ref[0], i, 0)), rspec(0), rspec(1), rspec(2)],
            out_specs=pl.BlockSpec((None, tr, cols), lambda p, i, me_ref: (p, i, 0))),
        out_shape=jax.ShapeDtypeStruct((npc, h, cols), F32),
    )(me.reshape(1), hb, rb, rb, rb)


def adamw_halves(name, w, m, v, own, other, c):
    npc, rows, cols = w.shape
    h = rows // 2
    tr = _row_tile(h, cols, budget=1024 * 1024)
    nt = h // tr
    c1 = 1.0 - ADAM_B1 ** ADAM_STEP
    c2 = 1.0 - ADAM_B2 ** ADAM_STEP

    def body(c_ref, w_ref, m_ref, v_ref, own_ref, oth_ref, g_ref, d_ref, nm_ref, nv_ref):
        g = jnp.where(pl.program_id(1) == c_ref[0], own_ref[...], oth_ref[...])
        nm = ADAM_B1 * m_ref[...] + (1.0 - ADAM_B1) * g
        nv = ADAM_B2 * v_ref[...] + (1.0 - ADAM_B2) * jnp.square(g)
        g_ref[...] = g
        d_ref[...] = -ADAM_LR * ((nm / c1) / (jnp.sqrt(nv / c2) + ADAM_EPS) + ADAM_WD * w_ref[...])
        nm_ref[...] = nm
        nv_ref[...] = nv

    full = pl.BlockSpec((None, tr, cols), lambda p, hh, i, c_ref: (p, hh * nt + i, 0))
    half = pl.BlockSpec((None, tr, cols), lambda p, hh, i, c_ref: (p, i, 0))
    return pl.pallas_call(
        body, name=name,
        grid_spec=pltpu.PrefetchScalarGridSpec(
            num_scalar_prefetch=1, grid=(npc, 2, nt),
            in_specs=[full, full, full, half, half], out_specs=[full] * 4),
        out_shape=[jax.ShapeDtypeStruct(w.shape, F32)] * 4,
    )(c.reshape(1), w, m, v, own, other)


WEIGHTS = ("norm_w", "final_norm_w", "ffn_gate", "ffn_up", "ffn_down", "w_in", "branch_proj", "w_out", "s5_lambda_re",
           "s5_lambda_im", "s5_log_dt", "s5_b_re", "s5_b_im", "s5_c_re", "s5_c_im", "s5_d", "s5_glu_w", "s5_glu_b",
           "hg_lb_logits", "hg_norm_w", "rg_conv_w", "rg_conv_b", "rg_wa", "rg_ba", "rg_wx", "rg_bx", "rg_lambda")
BIG = ("ffn_gate", "ffn_up", "ffn_down", "w_in", "branch_proj", "w_out", "s5_glu_w")
SHARDED_SMALL = ("norm_w", "rg_conv_w")
SMALL = SMALL_RAW + SHARDED_SMALL


def _view2d(shape):
    return (1, shape[0]) if len(shape) == 1 else (math.prod(shape[:-1]), shape[-1])


def _small_layout(shapes, row_multiple):
    layout, at = [], 0
    for shape in shapes:
        r, c = _view2d(shape)
        rp = -(-r // 8) * 8
        layout.append((at, r, c, rp))
        at += rp * max(1, c // LANE)
    return layout, -(-at // row_multiple) * row_multiple


def pack_small(name, arrays, row_multiple):
    layout, rows = _small_layout([a.shape for a in arrays], row_multiple)

    def body(*refs):
        out = refs[-1]
        out[...] = jnp.zeros_like(out)
        for ref, (r0, r, c, rp) in zip(refs[:-1], layout):
            if c <= LANE:
                out[r0:r0 + r, 0:c] = ref[...]
            else:
                for q in range(c // LANE):
                    out[r0 + q * rp:r0 + q * rp + r, :] = ref[:, q * LANE:(q + 1) * LANE]

    return pl.pallas_call(
        body, name=name, out_shape=jax.ShapeDtypeStruct((rows, LANE), F32),
        compiler_params=pltpu.CompilerParams(vmem_limit_bytes=VMEM_LIMIT),
    )(*[a.reshape(_view2d(a.shape)) for a in arrays])


def unpack_small(name, packed, shapes):
    layout, _ = _small_layout(shapes, 8)

    def body(p_ref, *outs):
        for ref, (r0, r, c, rp) in zip(outs, layout):
            if c <= LANE:
                ref[...] = p_ref[r0:r0 + r, 0:c]
            else:
                for q in range(c // LANE):
                    ref[:, q * LANE:(q + 1) * LANE] = p_ref[r0 + q * rp:r0 + q * rp + r, :]

    res = pl.pallas_call(
        body, name=name, out_shape=[jax.ShapeDtypeStruct(_view2d(s), F32) for s in shapes],
        compiler_params=pltpu.CompilerParams(vmem_limit_bytes=VMEM_LIMIT),
    )(packed)
    return [a.reshape(s) for a, s in zip(res, shapes)]


HBM = pl.BlockSpec(memory_space=pltpu.HBM)
SEM = pl.BlockSpec(memory_space=pltpu.SEMAPHORE)
EFFECT = pltpu.SideEffectType.DATAFLOW_SIDE_EFFECTING


def split_start(name, srcs, land_shapes, plan, n_send, n_recv):
    ns, nl = len(srcs), len(land_shapes)

    def body(*refs):
        ins, lands = refs[:ns], refs[ns:ns + nl]
        send, recv = refs[ns + nl], refs[ns + nl + 1]
        for src, dst, ks, kr, dev in plan(ins, lands):
            pltpu.make_async_remote_copy(src_ref=src, dst_ref=dst, send_sem=send.at[ks], recv_sem=recv.at[kr],
                                         device_id=dev, device_id_type=MESH).start()
        refs[-1][...] = jnp.zeros_like(refs[-1])

    hbm = lambda a: pltpu.with_memory_space_constraint(a, pltpu.HBM)
    lands = [lax.empty(s.shape, s.dtype) for s in land_shapes]
    out = pl.pallas_call(
        body, name=name,
        out_shape=(pltpu.SemaphoreType.DMA((n_send,)), pltpu.SemaphoreType.DMA((n_recv,)),
                   *[pltpu.HBM(a.shape, a.dtype) for a in srcs], *[pltpu.HBM(s.shape, s.dtype) for s in land_shapes],
                   jax.ShapeDtypeStruct((8, LANE), F32)),
        in_specs=[HBM] * (ns + nl), out_specs=(SEM, SEM, *[HBM] * (ns + nl), pl.BlockSpec(memory_space=pltpu.VMEM)),
        input_output_aliases={k: 2 + k for k in range(ns + nl)},
        compiler_params=pltpu.CompilerParams(has_side_effects=EFFECT),
    )(*[hbm(a) for a in srcs], *[hbm(a) for a in lands])
    return out[:-1], out[-1]


def split_wait(name, handles, n_src, waits, after):
    send, recv, *bufs = handles
    nb = len(bufs)

    def body(*refs):
        ins, lands = refs[:n_src], refs[n_src:nb]
        send_sem, recv_sem = refs[nb], refs[nb + 1]
        x, y, c, _chips = _place()
        sends, recvs = waits(ins, lands)
        for src, k in sends:
            pltpu.make_async_remote_copy(src_ref=src, dst_ref=src, send_sem=send_sem.at[k], recv_sem=recv_sem.at[0],
                                         device_id=(x, y, 1 - c), device_id_type=MESH).wait_send()
        for dst, k in recvs:
            pltpu.make_async_remote_copy(src_ref=dst, dst_ref=dst, send_sem=send_sem.at[0], recv_sem=recv_sem.at[k],
                                         device_id=(x, y, 1 - c), device_id_type=MESH).wait_recv()

    out = pl.pallas_call(
        body, name=name, out_shape=tuple(pltpu.HBM(a.shape, a.dtype) for a in bufs),
        in_specs=[HBM] * nb + [SEM, SEM, ANY], out_specs=tuple([HBM] * nb),
        input_output_aliases={k: k for k in range(nb)},
        compiler_params=pltpu.CompilerParams(has_side_effects=EFFECT),
    )(*bufs, send, recv, after)
    return list(out[:n_src]), list(out[n_src:])


def gather_plan(n):
    def plan(ins, lands):
        x, y, c, chips = _place()
        me = 2 * x + y
        copies = []
        for w in range(n):
            for j, (cx, cy) in enumerate(chips):
                for t in range(2):
                    copies.append((ins[w].at[c], lands[w].at[c, me], 8 * w + 2 * j + t, 8 * w + 2 * j + c, (cx, cy, t)))
            for half in range(2):
                copies.append((ins[w].at[half], lands[w].at[half, me], 8 * w + 6 + half, 8 * w + 6 + half, (x, y, 1 - c)))
        return copies

    def waits(ins, lands):
        x, y, c, chips = _place()
        me = 2 * x + y
        sends, recvs = [], []
        for w in range(n):
            for j, (cx, cy) in enumerate(chips):
                for t in range(2):
                    sends.append((ins[w].at[c], 8 * w + 2 * j + t))
                    recvs.append((lands[w].at[t, 2 * cx + cy], 8 * w + 2 * j + t))
            for half in range(2):
                sends.append((ins[w].at[half], 8 * w + 6 + half))
                recvs.append((lands[w].at[half, me], 8 * w + 6 + half))
        return sends, recvs

    return plan, waits


def scatter_plan(n):
    def plan(ins, lands):
        x, y, c, chips = _place()
        return [(ins[w].at[:, 2 * cx + cy], lands[w].at[j], 3 * w + j, 3 * w + j, (cx, cy, c))
                for w in range(n) for j, (cx, cy) in enumerate(chips)]

    def waits(ins, lands):
        x, y, c, chips = _place()
        sends = [(ins[w].at[:, 2 * cx + cy], 3 * w + j) for w in range(n) for j, (cx, cy) in enumerate(chips)]
        recvs = [(lands[w].at[j], 3 * w + j) for w in range(n) for j in range(3)]
        return sends, recvs

    return plan, waits


def _layer_shards(w, l):
    return [w["ffn_gate"][l].astype(MMT), w["ffn_up"][l].astype(MMT), w["ffn_down"][l].astype(MMT),
            w["w_in"][l].reshape(2, D_MODEL // 2, -1).astype(MMT),
            w["branch_proj"][l].reshape(2, 3 * BW // 2, -1).astype(MMT),
            w["w_out"][l].reshape(2, -1, D_MODEL).astype(MMT),
            w["s5_glu_w"][l].reshape(2, -1, BW).astype(MMT)]


def _layer_weights(g):
    rows = lambda a: a.transpose(1, 0, 2, 3).reshape(NSH, -1, a.shape[-1])
    p = rows(g[4]).reshape(NSH, 3, BW, -1).transpose(1, 2, 0, 3).reshape(3, BW, D_MODEL)
    return dict(wg=g[0], wu=g[1], wd=g[2], win=rows(g[3]), pfull=p,
                woutfull=rows(g[5]).reshape(D_MODEL, D_MODEL), gluw=rows(g[6]).reshape(BW, BW))


GROUPS = {"ffn1": ("ffn_gate", "ffn_up", "ffn_down"), "merge": ("branch_proj", "w_out"), "mid": ("s5_glu_w",),
          "pre": ("w_in",), "ffn0": ("ffn_gate", "ffn_up", "ffn_down")}


def _grad_views(big, l, group):
    views = []
    for name in GROUPS[group]:
        if name == "branch_proj":
            dq = D_MODEL // NSH
            a = big[(name, l)].reshape(3, BW, NSH, dq).transpose(2, 0, 1, 3).reshape(1, NSH, 3 * BW, dq)
        elif name.startswith("ffn"):
            a = big[(name, l, 1 if group == "ffn1" else 0)]
        else:
            a = big[(name, l)]
            a = a.reshape(1, NSH, -1, a.shape[-1])
        views.append((name, a, 0))
    return views


def halves_plan(n):
    def src(ref, c):
        h = ref.shape[2] // 2
        return ref.at[:, :, pl.ds((1 - c) * h, h)]

    def plan(ins, lands):
        x, y, c, _chips = _place()
        return [(src(ins[w], c), lands[w], w, w, (x, y, 1 - c)) for w in range(n)]

    def waits(ins, lands):
        x, y, c, _chips = _place()
        return [(src(ins[w], c), w) for w in range(n)], [(lands[w], w) for w in range(n)]

    return plan, waits


def _reduce_to_halves(tag, views, c, wire):
    from_sibling = exchange_halves(f"reduce_cores_{tag}", [a for _, a, _ in views], [(p0, 1) for _, _, p0 in views])
    merge = lambda a: a.reshape((-1,) + a.shape[2:])
    return [add_own_half(f"sum_cores_{tag}_{i}", merge(a), merge(r), c, wire[i], NSH * p0).reshape(r.shape)
            for i, ((_, a, p0), r) in enumerate(zip(views, from_sibling))]


def _step(x, target, w, m, v):
    mx, my, mc = lax.axis_index("x"), lax.axis_index("y"), lax.axis_index("c")
    me = (2 * mx + my).astype(jnp.int32)
    mc = mc.astype(jnp.int32)

    W = dict(L=[None] * DEPTH)
    state = {"pending": []}
    n_big = len(BIG)
    g_plan, g_waits = gather_plan(n_big)

    def layer_weights(l, h):
        if l == 0:
            got = gather_shards("gather_weights_0", _layer_shards(w, 0) + [w[n] for n in SHARDED_SMALL])
            nxt = _layer_shards(w, 1)
            got, nxt = lax.optimization_barrier((got, nxt))
            shapes = [jax.ShapeDtypeStruct((2, NSH) + a.shape[1:], a.dtype) for a in nxt]
            state["gather"], token = split_start("gather_weights_1_start", nxt, shapes, g_plan, 8 * n_big, 8 * n_big)
            W["nw"] = got[n_big].transpose(0, 2, 1, 3).reshape(DEPTH, 3, 1, D_MODEL) + token[0, 0]
            W["convw"] = got[n_big + 1].transpose(0, 2, 1, 3).reshape(DEPTH, CONV_W, BW)
            return _layer_weights(got[:n_big]), h
        return _layer_weights(split_wait("gather_weights_1_wait", state["gather"], n_big, g_waits, h)[1]), h

    def to_chips(after):
        if "cores" not in state:
            return
        tag, names, l, group, handles, waits = state.pop("cores")
        sent, landed = split_wait(f"reduce_cores_{tag}_wait", handles, len(names), waits, after)
        merge = lambda a: a.reshape((-1,) + a.shape[2:])
        halves = [add_own_half(f"sum_cores_{tag}_{i}", merge(a), merge(r), mc, jnp.bfloat16, 0).reshape(r.shape)
                  for i, (a, r) in enumerate(zip(sent, landed))]
        shapes = [jax.ShapeDtypeStruct((3, a.shape[0]) + a.shape[2:], a.dtype) for a in halves]
        plan, waits = scatter_plan(len(halves))
        handles, token = split_start(f"reduce_chips_{tag}_start", halves, shapes, plan, 3 * len(halves), 3 * len(halves))
        W["nw"] = W["nw"] + token[0, 0]
        state["pending"].append((tag, names, l, group, handles, waits))

    def layer_grads(l, group, big):
        views = _grad_views(big, l, group)
        to_chips(views[0][1])
        if (l, group) == (0, "ffn0"):
            return
        tag = f"{l}_{group}"
        if (l, group) == (0, "pre"):
            halves = _reduce_to_halves(tag, views, mc, [jnp.bfloat16] * len(views))
            shapes = [jax.ShapeDtypeStruct((3, a.shape[0]) + a.shape[2:], a.dtype) for a in halves]
            plan, waits = scatter_plan(len(halves))
            handles, token = split_start(f"reduce_chips_{tag}_start", halves, shapes, plan, 3 * len(halves), 3 * len(halves))
            W["nw"] = W["nw"] + token[0, 0]
            state["pending"].append((tag, [name for name, _, _ in views], l, group, handles, waits))
            return
        arrays = [a for _, a, _ in views]
        shapes = [jax.ShapeDtypeStruct((1, NSH, a.shape[2] // 2, a.shape[3]), a.dtype) for a in arrays]
        plan, waits = halves_plan(len(arrays))
        handles, token = split_start(f"reduce_cores_{tag}_start", arrays, shapes, plan, len(arrays), len(arrays))
        W["nw"] = W["nw"] + token[0, 0]
        state["cores"] = (tag, [name for name, _, _ in views], l, group, handles, waits)

    loss, dx, big, small = local_step(x[0], target[0], W, {k: w[k] for k in SMALL_RAW}, layer_weights, layer_grads)

    pieces = {n: {} for n in BIG}
    block_of = lambda name, l, group: (2 * l + (group == "ffn1")) if name.startswith("ffn") else l
    views = _grad_views(big, 0, "ffn0")
    small_packed = pack_small("pack_small_grads", [small[n] for n in SMALL], NSH * 32)
    halves = _reduce_to_halves("0_ffn0", views + [("small", small_packed.reshape(1, NSH, -1, LANE), 0)], mc,
                               [jnp.bfloat16] * len(views) + [F32])
    shapes = [jax.ShapeDtypeStruct((3, a.shape[0]) + a.shape[2:], a.dtype) for a in halves]
    plan, waits = scatter_plan(len(halves))
    last_handles, token = split_start("reduce_chips_0_ffn0_start", halves, shapes, plan, 3 * len(halves), 3 * len(halves))
    mc = mc + token[0, 0].astype(jnp.int32)
    for tag, names, l, group, handles, waits_k in state["pending"]:
        sent, landed = split_wait(f"reduce_chips_{tag}_wait", handles, len(names), waits_k, dx)
        for i, (name, h, r) in enumerate(zip(names, sent, landed)):
            pieces[name][block_of(name, l, group)] = add_chips(f"sum_chips_{tag}_{i}", h, r, me)

    g, delta, new_m, new_v = {}, {}, {}, {}

    def update(tag, names, extra):
        own = [jnp.concatenate([pieces[n][b] for b in sorted(pieces[n])], axis=0) for n in names] + extra
        other = share_halves(f"reduce_share_{tag}", own)
        for i, n in enumerate(names):
            view = lambda a: a.reshape(own[i].shape[0], -1, own[i].shape[2])
            res = adamw_halves(f"adamw_{n}", view(w[n]), view(m[n]), view(v[n]), own[i], other[i], mc)
            g[n], delta[n], new_m[n], new_v[n] = [a.reshape(w[n].shape) for a in res]
        return own, other

    early = [n for n in BIG if not n.startswith("ffn")]
    update("early", early, [])
    sent, landed = split_wait("reduce_chips_0_ffn0_wait", last_handles, len(halves), waits, new_v[early[0]])
    last = [add_chips(f"sum_chips_0_ffn0_{i}", h, r, me) for i, (h, r) in enumerate(zip(sent, landed))]
    for (name, _, _), piece in zip(views, last):
        pieces[name][block_of(name, 0, "ffn0")] = piece
    own, other = update("last", [n for n in BIG if n.startswith("ffn")], [last[-1]])

    piece = jnp.stack([jnp.where(mc == 0, own[-1][0], other[-1][0]), jnp.where(mc == 0, other[-1][0], own[-1][0])])
    (all_small,) = gather_shards("gather_small", [piece])
    full_small = unpack_small("unpack_small_grads", all_small.transpose(1, 0, 2, 3).reshape(-1, LANE),
                              [small[n].shape for n in SMALL])
    g.update(zip(SMALL, full_small))
    g["norm_w"] = lax.dynamic_slice_in_dim(g["norm_w"], me * (D_MODEL // NSH), D_MODEL // NSH, axis=2)
    g["rg_conv_w"] = lax.dynamic_slice_in_dim(g["rg_conv_w"], me * (BW // NSH), BW // NSH, axis=2)

    packed = [pack_small(f"pack_small_{tag}", [src[n] for n in SMALL], 8)
              for tag, src in (("w", w), ("g", g), ("m", m), ("v", v))]
    for tag, dst, flat in zip(("delta", "m", "v"), (delta, new_m, new_v), adamw(*packed)):
        dst.update(zip(SMALL, unpack_small(f"unpack_small_{tag}", flat, [w[n].shape for n in SMALL])))

    total = lax.psum(loss[0, 0], ("x", "y", "c"))
    return (total, dx[None], *[g[n] for n in WEIGHTS], *[delta[n] for n in WEIGHTS],
            *[new_m[n] for n in WEIGHTS], *[new_v[n] for n in WEIGHTS])


def kernel(x, norm_w, final_norm_w, ffn_gate, ffn_up, ffn_down, w_in, branch_proj, w_out, s5_lambda_re, s5_lambda_im, s5_log_dt, s5_b_re, s5_b_im, s5_c_re, s5_c_im, s5_d, s5_glu_w, s5_glu_b, hg_lb_logits, hg_norm_w, rg_conv_w, rg_conv_b, rg_wa, rg_ba, rg_wx, rg_bx, rg_lambda, loss_target, m_norm_w, m_final_norm_w, m_ffn_gate, m_ffn_up, m_ffn_down, m_w_in, m_branch_proj, m_w_out, m_s5_lambda_re, m_s5_lambda_im, m_s5_log_dt, m_s5_b_re, m_s5_b_im, m_s5_c_re, m_s5_c_im, m_s5_d, m_s5_glu_w, m_s5_glu_b, m_hg_lb_logits, m_hg_norm_w, m_rg_conv_w, m_rg_conv_b, m_rg_wa, m_rg_ba, m_rg_wx, m_rg_bx, m_rg_lambda, v_norm_w, v_final_norm_w, v_ffn_gate, v_ffn_up, v_ffn_down, v_w_in, v_branch_proj, v_w_out, v_s5_lambda_re, v_s5_lambda_im, v_s5_log_dt, v_s5_b_re, v_s5_b_im, v_s5_c_re, v_s5_c_im, v_s5_d, v_s5_glu_w, v_s5_glu_b, v_hg_lb_logits, v_hg_norm_w, v_rg_conv_w, v_rg_conv_b, v_rg_wa, v_rg_ba, v_rg_wx, v_rg_bx, v_rg_lambda):
    ws = (norm_w, final_norm_w, ffn_gate, ffn_up, ffn_down, w_in, branch_proj, w_out, s5_lambda_re, s5_lambda_im, s5_log_dt, s5_b_re, s5_b_im, s5_c_re, s5_c_im, s5_d, s5_glu_w, s5_glu_b, hg_lb_logits, hg_norm_w, rg_conv_w, rg_conv_b, rg_wa, rg_ba, rg_wx, rg_bx, rg_lambda)
    ms = (m_norm_w, m_final_norm_w, m_ffn_gate, m_ffn_up, m_ffn_down, m_w_in, m_branch_proj, m_w_out, m_s5_lambda_re, m_s5_lambda_im, m_s5_log_dt, m_s5_b_re, m_s5_b_im, m_s5_c_re, m_s5_c_im, m_s5_d, m_s5_glu_w, m_s5_glu_b, m_hg_lb_logits, m_hg_norm_w, m_rg_conv_w, m_rg_conv_b, m_rg_wa, m_rg_ba, m_rg_wx, m_rg_bx, m_rg_lambda)
    vs = (v_norm_w, v_final_norm_w, v_ffn_gate, v_ffn_up, v_ffn_down, v_w_in, v_branch_proj, v_w_out, v_s5_lambda_re, v_s5_lambda_im, v_s5_log_dt, v_s5_b_re, v_s5_b_im, v_s5_c_re, v_s5_c_im, v_s5_d, v_s5_glu_w, v_s5_glu_b, v_hg_lb_logits, v_hg_norm_w, v_rg_conv_w, v_rg_conv_b, v_rg_wa, v_rg_ba, v_rg_wx, v_rg_bx, v_rg_lambda)
    return _step(x, loss_target, dict(zip(WEIGHTS, ws)), dict(zip(WEIGHTS, ms)), dict(zip(WEIGHTS, vs)))
```

```python
import functools
import math
from typing import NamedTuple

import jax
import jax.numpy as jnp
from jax import lax
from jax.experimental import pallas as pl
from jax.experimental.pallas import tpu as pltpu

F32 = jnp.float32
MMT = jnp.bfloat16

D_MODEL = 1024
BW = 512
S5_GROUP, S5_GROUPS, S5_STATE = 16, 32, 64
S5_N = S5_GROUPS * S5_STATE
HG_HEADS, HG_D = 4, 128
HG_CHUNK = 128
RG_BLOCKS, RG_BLOCK = 8, 64
RG_C = 8.0
CONV_W = 4
D_FF = 2816
EPS = 1e-6
IN_TOTAL = 6656
NSH = 4
NSEG = 8
LANE = 128
VMEM_LIMIT = 56 * 1024 * 1024
TM_FWD = 512
TM_WGRAD = 512

ADAM_LR, ADAM_B1, ADAM_B2, ADAM_EPS, ADAM_WD, ADAM_STEP = 0.001, 0.9, 0.999, 1e-08, 0.01, 10

MESH = pl.DeviceIdType.MESH


class WP(NamedTuple):
    w: jax.Array
    p: jax.Array


def _dg(a, b, ca, cb):
    return lax.dot_general(a, b, (((ca,), (cb,)), ((), ())), preferred_element_type=F32)


@jax.custom_vjp
def _mmw(a, w, p):
    return _dg(a.astype(MMT), w, 1, 0)


def _mmw_fwd(a, w, p):
    return _mmw(a, w, p), (a, w)


def _mmw_bwd(res, g):
    a, w = res
    gb = g.astype(MMT)
    return _dg(gb, w, 1, 1), jnp.zeros_like(w), _dg(a.astype(MMT), gb, 0, 0)


_mmw.defvjp(_mmw_fwd, _mmw_bwd)


def mm(a, w):
    if isinstance(w, WP):
        return _mmw(a, w.w, w.p)
    return _dg(a.astype(MMT), w, 1, 0)


@jax.custom_vjp
def mma_nn(a, b):
    return _dg(a.astype(MMT), b.astype(MMT), 1, 0)


def _nn_f(a, b):
    return mma_nn(a, b), (a, b)


def _nn_b(res, g):
    a, b = res
    gb = g.astype(MMT)
    return _dg(gb, b.astype(MMT), 1, 1), _dg(a.astype(MMT), gb, 0, 0)


mma_nn.defvjp(_nn_f, _nn_b)


@jax.custom_vjp
def mma_nt(a, b):
    return _dg(a.astype(MMT), b.astype(MMT), 1, 1)


def _nt_f(a, b):
    return mma_nt(a, b), (a, b)


def _nt_b(res, g):
    a, b = res
    gb = g.astype(MMT)
    return _dg(gb, b.astype(MMT), 1, 0), _dg(gb, a.astype(MMT), 0, 0)


mma_nt.defvjp(_nt_f, _nt_b)


@jax.custom_vjp
def mma_tn(a, b):
    return _dg(a.astype(MMT), b.astype(MMT), 0, 0)


def _tn_f(a, b):
    return mma_tn(a, b), (a, b)


def _tn_b(res, g):
    a, b = res
    gb = g.astype(MMT)
    return _dg(b.astype(MMT), gb, 1, 1), _dg(a.astype(MMT), gb, 1, 0)


mma_tn.defvjp(_tn_f, _tn_b)


def _split3(x):
    hi = x.astype(MMT)
    r = x - hi.astype(F32)
    mid = r.astype(MMT)
    return hi, mid, (r - mid.astype(F32)).astype(MMT)


@jax.custom_vjp
def mm_exact(m, x):
    mb = m.astype(MMT)
    hi, mid, lo = _split3(x)
    return (_dg(mb, hi, 1, 0) + _dg(mb, mid, 1, 0)) + _dg(mb, lo, 1, 0)


def _mm_exact_fwd(m, x):
    return mm_exact(m, x), m


def _mm_exact_bwd(m, g):
    mb = m.astype(MMT)
    hi, mid, lo = _split3(g)
    return jnp.zeros_like(m), (_dg(mb, hi, 0, 0) + _dg(mb, mid, 0, 0)) + _dg(mb, lo, 0, 0)


mm_exact.defvjp(_mm_exact_fwd, _mm_exact_bwd)


@jax.custom_vjp
def mm_exact_r(x, m):
    mb = m.astype(MMT)
    hi, mid, lo = _split3(x)
    return (_dg(hi, mb, 1, 0) + _dg(mid, mb, 1, 0)) + _dg(lo, mb, 1, 0)


def _mm_exact_r_fwd(x, m):
    return mm_exact_r(x, m), m


def _mm_exact_r_bwd(m, g):
    mb = m.astype(MMT)
    hi, mid, lo = _split3(g)
    return (_dg(hi, mb, 1, 1) + _dg(mid, mb, 1, 1)) + _dg(lo, mb, 1, 1), jnp.zeros_like(m)


mm_exact_r.defvjp(_mm_exact_r_fwd, _mm_exact_r_bwd)


def _rms(x, w):
    return x * lax.rsqrt(jnp.mean(x * x, axis=-1, keepdims=True) + EPS) * w


def _expm1(x):
    series = x * (1.0 + x * (1.0 / 2) * (1.0 + x * (1.0 / 3) * (1.0 + x * (1.0 / 4) * (1.0 + x * (1.0 / 5) * (1.0 + x * (1.0 / 6))))))
    return jnp.where(jnp.abs(x) < 0.1, series, jnp.exp(x) - 1.0)


def _bspec(block, fn, order):
    if order == "is":
        return pl.BlockSpec(block, lambda i, s: fn(s, i))
    return pl.BlockSpec(block, lambda s, i: fn(s, i))


def tile_fwd(fn, name, n_i, n_s, ins, outs, s_outer=False):
    n_in = len(ins)
    order = "si" if s_outer else "is"
    assert not (s_outer and any(o[4] for o in outs))

    def body(*refs):
        s = pl.program_id(0 if s_outer else 1)
        res = fn(*[r[...] for r in refs[:n_in]], s)
        for o_ref, val, spec in zip(refs[n_in:], res, outs):
            if spec[4] and n_s > 1:
                @pl.when(s == 0)
                def _(o_ref=o_ref, val=val):
                    o_ref[...] = val.astype(o_ref.dtype)

                @pl.when(s != 0)
                def _(o_ref=o_ref, val=val):
                    o_ref[...] += val.astype(o_ref.dtype)
            else:
                o_ref[...] = val.astype(o_ref.dtype)

    return pl.pallas_call(
        body, grid=(n_s, n_i) if s_outer else (n_i, n_s), name=name,
        in_specs=[_bspec(b, f, order) for _, b, f in ins],
        out_specs=[_bspec(b, f, order) for _, _, b, f, _ in outs],
        out_shape=[jax.ShapeDtypeStruct(sh, dt) for sh, dt, _, _, _ in outs],
        compiler_params=pltpu.CompilerParams(vmem_limit_bytes=VMEM_LIMIT,
                                             dimension_semantics=("arbitrary", "arbitrary")),
    )(*[a for a, _, _ in ins])


def tile_bwd(fn, name, n_i, n_s, ins, cts, gouts):
    groups = [c if isinstance(c, list) else [c] for c in cts]
    cts = [blk for grp in groups for blk in grp]
    n_in, n_ct = len(ins), len(cts)
    kinds = [k for _, _, _, k in ins]
    d_pos = [j for j, k in enumerate(kinds) if k != "c"]
    shared = [(gi, spec[4]) for gi, spec in enumerate(gouts) if len(spec) == 5 and spec[4] is not None]
    n_sh = len(shared)

    def body(*refs):
        s, i = pl.program_id(0), pl.program_id(1)
        vals = [r[...] for r in refs[:n_in]]
        ct_refs, ctv = list(refs[n_in:n_in + n_ct]), []
        for grp in groups:
            parts = [ct_refs.pop(0)[...] for _ in grp]
            ctv.append(parts[0] if len(parts) == 1 else jnp.concatenate(parts, axis=1))
        ctv = tuple(ctv)
        g_refs = refs[n_in + n_ct + n_sh:]

        def g(*dv):
            args = list(vals)
            for j, v in zip(d_pos, dv):
                args[j] = WP(vals[j], v) if kinds[j] == "w" else v
            return tuple(fn(*args))

        dv0 = [jnp.zeros(vals[j].shape, F32) if kinds[j] == "w" else vals[j] for j in d_pos]
        _, vjp = jax.vjp(g, *dv0)
        grads = vjp(ctv)
        for g_ref, gv, spec in zip(g_refs, grads, gouts):
            mode = spec[3]
            if mode == "write":
                g_ref[...] = gv.astype(g_ref.dtype)
            else:
                first = (i == 0) if mode == "acc_i" else jnp.logical_and(i == 0, s == 0)

                @pl.when(first)
                def _(g_ref=g_ref, gv=gv):
                    g_ref[...] = gv.astype(g_ref.dtype)

                @pl.when(jnp.logical_not(first))
                def _(g_ref=g_ref, gv=gv):
                    g_ref[...] += gv.astype(g_ref.dtype)

    return pl.pallas_call(
        body, grid=(n_s, n_i), name=name,
        in_specs=([_bspec(b, f, "si") for _, b, f, _ in ins] + [_bspec(b, f, "si") for _, b, f in cts]
                  + [pl.BlockSpec(memory_space=pl.ANY)] * n_sh),
        out_specs=[_bspec(spec[1], spec[2], "si") for spec in gouts],
        out_shape=[jax.ShapeDtypeStruct(spec[0], F32) for spec in gouts],
        input_output_aliases={n_in + n_ct + k: gi for k, (gi, _) in enumerate(shared)},
        compiler_params=pltpu.CompilerParams(vmem_limit_bytes=VMEM_LIMIT,
                                             dimension_semantics=("arbitrary", "arbitrary")),
    )(*[a for a, _, _, _ in ins], *[a for a, _, _ in cts], *[buf for _, buf in shared])


def _row_tile(rows, width, itemsize=4, budget=2 * 1024 * 1024, mult=8):
    best = mult
    for t in range(mult, rows + 1, mult):
        if rows % t == 0 and t * width * itemsize <= budget:
            best = t
    return best


def add_n(name, terms, shape):
    rows, cols = shape
    tr = _row_tile(rows, cols)

    def body(*refs):
        acc = refs[0][...]
        for r in refs[1:-1]:
            acc = acc + r[...]
        refs[-1][...] = acc

    specs = []
    for _, lead in terms:
        specs.append(pl.BlockSpec((None,) * len(lead) + (tr, cols), functools.partial(lambda i, lead: (*lead, i, 0), lead=lead)))
    return pl.pallas_call(
        body, grid=(rows // tr,), name=name, in_specs=specs,
        out_specs=pl.BlockSpec((tr, cols), lambda i: (i, 0)),
        out_shape=jax.ShapeDtypeStruct((rows, cols), F32),
    )(*[a for a, _ in terms])


def pre_core(x, nw, win):
    return (mm(_rms(x, nw), win),)


def _split_lanes(y):
    return jnp.stack([y[:, k * LANE:(k + 1) * LANE] for k in range(y.shape[1] // LANE)], axis=0)


def _join_lanes(y3):
    return jnp.concatenate([y3[k] for k in range(y3.shape[0])], axis=1)


def s5_pre_core(u, b_re0, b_re1, b_im0, b_im1):
    u0, u1 = u[:, :BW // 2], u[:, BW // 2:]
    re = jnp.concatenate([mm(u0, b_re0), mm(u1, b_re1)], axis=1)
    im = jnp.concatenate([mm(u0, b_im0), mm(u1, b_im1)], axis=1)
    return _split_lanes(re), _split_lanes(im)


def mid_core(xr, xi, u, o, g, hs, gc, hmat, c0, c1, d, gluw, glub, hgw):
    half = xr.shape[0] // 2
    xs0 = jnp.concatenate([_join_lanes(xr[:half]), _join_lanes(xi[:half])], axis=1)
    xs1 = jnp.concatenate([_join_lanes(xr[half:]), _join_lanes(xi[half:])], axis=1)
    y = jnp.concatenate([mm(xs0, c0), mm(xs1, c1)], axis=1) + d * u
    z = jax.nn.gelu(y)
    ya = z * jax.nn.sigmoid(mm(z, gluw) + glub)
    ms = mm_exact_r(o * o, hmat)
    yb = o * lax.rsqrt(ms + EPS) * hgw * jax.nn.silu(g)
    yc = hs * jax.nn.gelu(gc)
    return ya, yb, yc


def _sub(w, n):
    return WP(w.w[n], w.p[n]) if isinstance(w, WP) else w[n]


def merge_core(ya, yb, yc, g0, g1, g2, g3, g4, g5, p, wout):
    gate = lambda a, b: jax.nn.sigmoid(jnp.concatenate([a, b], axis=1))
    m = gate(g0, g1) * mm(ya, _sub(p, 0)) + gate(g2, g3) * mm(yb, _sub(p, 1)) + gate(g4, g5) * mm(yc, _sub(p, 2))
    return (mm(m, wout),)


def gates_core(xc, wa, ba, wx, bx, lam):
    r = jax.nn.sigmoid(mm(xc, wa) + ba)
    i = jax.nn.sigmoid(mm(xc, wx) + bx)
    log_a = -RG_C * jax.nn.softplus(-lam) * r
    a = jnp.exp(log_a)
    b = jnp.sqrt(-_expm1(2.0 * log_a)) * (i * xc)
    return a, b


def _seg_rows(ref, k, j, n):
    rows = pl.ds(pl.multiple_of(j * NSEG, NSEG), NSEG)
    if k is None:
        return ref[rows, :]
    return ref[k, rows, :]


def _seg_store(ref, k, j, n, val):
    rows = pl.ds(pl.multiple_of(j * NSEG, NSEG), NSEG)
    if k is None:
        ref[rows, :] = val
    else:
        ref[k, rows, :] = val


def _seg_carries(er, ei, pr, pi, reverse):
    rows = lax.broadcasted_iota(jnp.int32, er.shape, 0)
    cr = jnp.zeros_like(er)
    ci = None if ei is None else jnp.zeros_like(er)
    order = range(NSEG - 2, -1, -1) if reverse else range(1, NSEG)
    shift = NSEG - 1 if reverse else 1
    for s in order:
        if ei is None:
            tr = er + pr * cr
            cr = jnp.where(rows == s, pltpu.roll(tr, shift, 0), cr)
        else:
            tr = er + pr * cr - pi * ci
            ti = ei + pr * ci + pi * cr
            cr = jnp.where(rows == s, pltpu.roll(tr, shift, 0), cr)
            ci = jnp.where(rows == s, pltpu.roll(ti, shift, 0), ci)
    return cr, ci


def _cpow(ar, ai, n):
    out = None
    while n:
        if n & 1:
            out = (ar, ai) if out is None else (out[0] * ar - out[1] * ai, out[0] * ai + out[1] * ar)
        ar, ai = ar * ar - ai * ai, 2.0 * ar * ai
        n >>= 1
    return out


S5_K = 2


def s5_scan_fwd(bur, bui, ar, ai, L):
    n = L // NSEG
    nb = S5_N // LANE
    K = S5_K

    def body(br_ref, bi_ref, ar_ref, ai_ref, xr_ref, xi_ref):
        zero = jnp.zeros((NSEG, LANE), F32)
        A = [(jnp.broadcast_to(ar_ref[k], (NSEG, LANE)), jnp.broadcast_to(ai_ref[k], (NSEG, LANE))) for k in range(K)]

        def p1(j, st):
            new = []
            for k in range(K):
                sr, si = st[k]
                a_r, a_i = A[k]
                nr = a_r * sr - a_i * si + _seg_rows(br_ref, k, j, n)
                ni = a_r * si + a_i * sr + _seg_rows(bi_ref, k, j, n)
                _seg_store(xr_ref, k, j, n, nr)
                _seg_store(xi_ref, k, j, n, ni)
                new.append((nr, ni))
            return tuple(new)

        st = lax.fori_loop(0, n, p1, tuple((zero, zero) for _ in range(K)))
        C = [_seg_carries(st[k][0], st[k][1], *_cpow(*A[k], n), False) for k in range(K)]

        def p2(j, st):
            new = []
            for k in range(K):
                pr, pi = st[k]
                a_r, a_i = A[k]
                pr, pi = a_r * pr - a_i * pi, a_r * pi + a_i * pr
                cr, ci = C[k]
                _seg_store(xr_ref, k, j, n, _seg_rows(xr_ref, k, j, n) + pr * cr - pi * ci)
                _seg_store(xi_ref, k, j, n, _seg_rows(xi_ref, k, j, n) + pr * ci + pi * cr)
                new.append((pr, pi))
            return tuple(new)

        lax.fori_loop(0, n, p2, tuple((zero + 1.0, zero) for _ in range(K)))

    blk = pl.BlockSpec((K, L, LANE), lambda g: (g, 0, 0))
    ablk = pl.BlockSpec((K, 1, LANE), lambda g: (g, 0, 0))
    return pl.pallas_call(
        body, grid=(nb // K,), name="s5_scan_fwd",
        in_specs=[blk, blk, ablk, ablk], out_specs=[blk, blk],
        out_shape=[jax.ShapeDtypeStruct((nb, L, LANE), F32)] * 2,
        compiler_params=pltpu.CompilerParams(vmem_limit_bytes=VMEM_LIMIT),
    )(bur, bui, ar, ai)


def s5_scan_bwd(dxr, dxi, xr, xi, ar, ai, L):
    n = L // NSEG
    nb = S5_N // LANE
    K = S5_K

    def body(dr_ref, di_ref, xr_ref, xi_ref, ar_ref, ai_ref, gr_ref, gi_ref, dar_ref, dai_ref):
        zero = jnp.zeros((NSEG, LANE), F32)
        rows = lax.broadcasted_iota(jnp.int32, (NSEG, LANE), 0)
        A = [(jnp.broadcast_to(ar_ref[k], (NSEG, LANE)), -jnp.broadcast_to(ai_ref[k], (NSEG, LANE))) for k in range(K)]

        def p1(jj, st):
            j = n - 1 - jj
            new = []
            for k in range(K):
                sr, si = st[k]
                a_r, a_i = A[k]
                nr = a_r * sr - a_i * si + _seg_rows(dr_ref, k, j, n)
                ni = a_r * si + a_i * sr + _seg_rows(di_ref, k, j, n)
                _seg_store(gr_ref, k, j, n, nr)
                _seg_store(gi_ref, k, j, n, ni)
                new.append((nr, ni))
            return tuple(new)

        st = lax.fori_loop(0, n, p1, tuple((zero, zero) for _ in range(K)))
        C = [_seg_carries(st[k][0], st[k][1], *_cpow(*A[k], n), True) for k in range(K)]
        xb = [(jnp.where(rows == 0, 0.0, pltpu.roll(_seg_rows(xr_ref, k, n - 1, n), 1, 0)),
               jnp.where(rows == 0, 0.0, pltpu.roll(_seg_rows(xi_ref, k, n - 1, n), 1, 0))) for k in range(K)]

        def p2(jj, st):
            j = n - 1 - jj
            jp = jnp.maximum(j - 1, 0)
            new = []
            for k in range(K):
                pr, pi, acr, aci = st[k]
                a_r, a_i = A[k]
                pr, pi = a_r * pr - a_i * pi, a_r * pi + a_i * pr
                cr, ci = C[k]
                g_r = _seg_rows(gr_ref, k, j, n) + pr * cr - pi * ci
                g_i = _seg_rows(gi_ref, k, j, n) + pr * ci + pi * cr
                _seg_store(gr_ref, k, j, n, g_r)
                _seg_store(gi_ref, k, j, n, g_i)
                xpr = jnp.where(j == 0, xb[k][0], _seg_rows(xr_ref, k, jp, n))
                xpi = jnp.where(j == 0, xb[k][1], _seg_rows(xi_ref, k, jp, n))
                new.append((pr, pi, acr + g_r * xpr + g_i * xpi, aci + g_i * xpr - g_r * xpi))
            return tuple(new)

        st = lax.fori_loop(0, n, p2, tuple((zero + 1.0, zero, zero, zero) for _ in range(K)))
        for k in range(K):
            dar_ref[k] = jnp.sum(st[k][2], axis=0, keepdims=True)
            dai_ref[k] = jnp.sum(st[k][3], axis=0, keepdims=True)

    blk = pl.BlockSpec((K, L, LANE), lambda g: (g, 0, 0))
    ablk = pl.BlockSpec((K, 1, LANE), lambda g: (g, 0, 0))
    return pl.pallas_call(
        body, grid=(nb // K,), name="s5_scan_bwd",
        in_specs=[blk, blk, blk, blk, ablk, ablk], out_specs=[blk, blk, ablk, ablk],
        out_shape=[jax.ShapeDtypeStruct((nb, L, LANE), F32)] * 2 + [jax.ShapeDtypeStruct((nb, 1, LANE), F32)] * 2,
        compiler_params=pltpu.CompilerParams(vmem_limit_bytes=VMEM_LIMIT),
    )(dxr, dxi, xr, xi, ar, ai)


def rg_scan_fwd(a, b, L):
    n = L // NSEG

    def body(a_ref, b_ref, h_ref):
        zero = jnp.zeros((NSEG, LANE), F32)

        def p1(j, st):
            h, p = st
            aj = _seg_rows(a_ref, None, j, n)
            h = aj * h + _seg_rows(b_ref, None, j, n)
            _seg_store(h_ref, None, j, n, h)
            return h, aj * p

        e, pe = lax.fori_loop(0, n, p1, (zero, zero + 1.0))
        c, _ = _seg_carries(e, None, pe, None, False)

        def p2(j, p):
            p = _seg_rows(a_ref, None, j, n) * p
            _seg_store(h_ref, None, j, n, _seg_rows(h_ref, None, j, n) + p * c)
            return p

        lax.fori_loop(0, n, p2, zero + 1.0)

    blk = pl.BlockSpec((L, LANE), lambda g: (0, g))
    return pl.pallas_call(
        body, grid=(BW // LANE,), name="rg_scan_fwd", in_specs=[blk, blk], out_specs=blk,
        out_shape=jax.ShapeDtypeStruct((L, BW), F32),
        compiler_params=pltpu.CompilerParams(vmem_limit_bytes=VMEM_LIMIT),
    )(a, b)


def rg_scan_bwd(a, h, dh, L):
    n = L // NSEG

    def body(a_ref, h_ref, dh_ref, da_ref, db_ref):
        zero = jnp.zeros((NSEG, LANE), F32)
        rows = lax.broadcasted_iota(jnp.int32, (NSEG, LANE), 0)
        a_edge = jnp.where(rows == NSEG - 1, 0.0, pltpu.roll(_seg_rows(a_ref, None, 0, n), NSEG - 1, 0))
        h_edge = jnp.where(rows == 0, 0.0, pltpu.roll(_seg_rows(h_ref, None, n - 1, n), 1, 0))

        def mult(j):
            return jnp.where(j == n - 1, a_edge, _seg_rows(a_ref, None, jnp.minimum(j + 1, n - 1), n))

        def p1(jj, st):
            j = n - 1 - jj
            g, p = st
            m = mult(j)
            g = m * g + _seg_rows(dh_ref, None, j, n)
            _seg_store(db_ref, None, j, n, g)
            return g, m * p

        e, pe = lax.fori_loop(0, n, p1, (zero, zero + 1.0))
        c, _ = _seg_carries(e, None, pe, None, True)

        def p2(jj, p):
            j = n - 1 - jj
            p = mult(j) * p
            g = _seg_rows(db_ref, None, j, n) + p * c
            _seg_store(db_ref, None, j, n, g)
            hp = jnp.where(j == 0, h_edge, _seg_rows(h_ref, None, jnp.maximum(j - 1, 0), n))
            _seg_store(da_ref, None, j, n, g * hp)
            return p

        lax.fori_loop(0, n, p2, zero + 1.0)

    blk = pl.BlockSpec((L, LANE), lambda g: (0, g))
    return pl.pallas_call(
        body, grid=(BW // LANE,), name="rg_scan_bwd", in_specs=[blk, blk, blk], out_specs=[blk, blk],
        out_shape=[jax.ShapeDtypeStruct((L, BW), F32)] * 2,
        compiler_params=pltpu.CompilerParams(vmem_limit_bytes=VMEM_LIMIT),
    )(a, h, dh)


def _hg_consts(C):
    t = lax.broadcasted_iota(jnp.int32, (C, C), 0)
    s = lax.broadcasted_iota(jnp.int32, (C, C), 1)
    tril = (s <= t).astype(F32)
    diag = (s == t).astype(F32)
    levels = []
    k = 1
    while (1 << k) <= C:
        m = 1 << (k - 1)
        same = (t >> k) == (s >> k)
        t_right = ((t >> (k - 1)) & 1) == 1
        s_left = ((s >> (k - 1)) & 1) == 0
        mask = jnp.logical_and(same, jnp.logical_and(t_right, s_left)).astype(F32)
        bnd = ((t >> k) << k) + (m - 1)
        levels.append((mask, (s <= bnd).astype(F32)))
        k += 1
    return tril, diag, levels


def hg_chunk(st, q, z, v, lb):
    C = q.shape[0]
    tril, diag, levels = _hg_consts(C)
    sig = jax.nn.sigmoid(z)
    lf = jnp.log(lb + (1.0 - lb) * sig)
    k = (1.0 - lb) * jax.nn.sigmoid(-z)
    qh = jax.nn.silu(q)
    b = mm_exact(tril, lf)
    blast = jnp.sum(lf, axis=0, keepdims=True)
    qe = qh * jnp.exp(b)
    kd = k * jnp.exp(blast - b)
    scaled = []
    for level, (_, sel) in enumerate(levels):
        size = 2 << level
        if size >= NSEG:
            b3 = b.reshape(C // size, size, b.shape[1])
            ref = jnp.broadcast_to(b3[:, size // 2 - 1:size // 2, :], b3.shape).reshape(b.shape)
        else:
            ref = mm_exact(sel, lf)
        scaled.append((qh * jnp.exp(jnp.minimum(b - ref, 0.0)), k * jnp.exp(jnp.minimum(ref - b, 0.0))))
    outs, news = [], []
    for h in range(HG_HEADS):
        sl = slice(h * HG_D, (h + 1) * HG_D)
        st_h = st[h * HG_D:(h + 1) * HG_D, :]
        sc = diag * mma_nt(qh[:, sl], k[:, sl])
        for (mask, _), (qt, kt) in zip(levels, scaled):
            sc = sc + mask * mma_nt(qt[:, sl], kt[:, sl])
        outs.append(mma_nt(qe[:, sl], st_h) + mma_nn(sc, v[:, sl]))
        news.append(st_h * jnp.exp(blast[:, sl]) + mma_tn(v[:, sl], kd[:, sl]))
    return jnp.concatenate(news, axis=0), jnp.concatenate(outs, axis=1)


def hg_fwd(qzv, lb, L):
    C = HG_CHUNK
    nc = L // C

    def body(q_ref, z_ref, v_ref, lb_ref, o_ref, sst_ref, st_ref):
        @pl.when(pl.program_id(0) == 0)
        def _():
            st_ref[...] = jnp.zeros_like(st_ref)

        st = st_ref[...]
        sst_ref[...] = st
        new, o = hg_chunk(st, q_ref[...], z_ref[...], v_ref[...], lb_ref[...])
        st_ref[...] = new
        o_ref[...] = o

    col = lambda cb: pl.BlockSpec((C, BW), functools.partial(lambda c, cb: (c, cb), cb=cb))
    return pl.pallas_call(
        body, grid=(nc,), name="hg_fwd",
        in_specs=[col(0), col(1), col(2), pl.BlockSpec((1, BW), lambda c: (0, 0))],
        out_specs=[pl.BlockSpec((C, BW), lambda c: (c, 0)), pl.BlockSpec((None, BW, HG_D), lambda c: (c, 0, 0))],
        out_shape=[jax.ShapeDtypeStruct((L, BW), F32), jax.ShapeDtypeStruct((nc, BW, HG_D), F32)],
        scratch_shapes=[pltpu.VMEM((BW, HG_D), F32)],
        compiler_params=pltpu.CompilerParams(vmem_limit_bytes=VMEM_LIMIT, dimension_semantics=("arbitrary",)),
    )(qzv, qzv, qzv, lb)


def hg_bwd(qzv, lb, sst, do, L):
    C = HG_CHUNK
    nc = L // C

    def body(q_ref, z_ref, v_ref, lb_ref, sst_ref, do_ref, dq_ref, dz_ref, dv_ref, dlb_ref, dst_ref):
        @pl.when(pl.program_id(0) == 0)
        def _():
            dst_ref[...] = jnp.zeros_like(dst_ref)
            dlb_ref[...] = jnp.zeros_like(dlb_ref)

        _, vjp = jax.vjp(hg_chunk, sst_ref[...], q_ref[...], z_ref[...], v_ref[...], lb_ref[...])
        dst, dq, dz, dv, dlb = vjp((dst_ref[...], do_ref[...]))
        dst_ref[...] = dst
        dq_ref[...] = dq
        dz_ref[...] = dz
        dv_ref[...] = dv
        dlb_ref[...] += dlb

    col = lambda cb: pl.BlockSpec((C, BW), functools.partial(lambda c, cb: (nc - 1 - c, cb), cb=cb))
    rev = pl.BlockSpec((C, BW), lambda c: (nc - 1 - c, 0))
    return pl.pallas_call(
        body, grid=(nc,), name="hg_bwd",
        in_specs=[col(0), col(1), col(2), pl.BlockSpec((1, BW), lambda c: (0, 0)),
                  pl.BlockSpec((None, BW, HG_D), lambda c: (nc - 1 - c, 0, 0)), rev],
        out_specs=[rev, rev, rev, pl.BlockSpec((1, BW), lambda c: (0, 0))],
        out_shape=[jax.ShapeDtypeStruct((L, BW), F32)] * 3 + [jax.ShapeDtypeStruct((1, BW), F32)],
        scratch_shapes=[pltpu.VMEM((BW, HG_D), F32)],
        compiler_params=pltpu.CompilerParams(vmem_limit_bytes=VMEM_LIMIT, dimension_semantics=("arbitrary",)),
    )(qzv, qzv, qzv, lb, sst, do)


def _shift_down(x, d, rows, L):
    if d == 0:
        return x
    wrapped = jnp.where((rows & (NSEG - 1)) == 0, 0.0, pltpu.roll(x, NSEG * d + 1, 0))
    return jnp.where(rows < NSEG * d, wrapped, pltpu.roll(x, NSEG * d, 0))


def _shift_up(x, d, rows, L):
    if d == 0:
        return x
    wrapped = jnp.where((rows & (NSEG - 1)) == NSEG - 1, 0.0, pltpu.roll(x, L - (NSEG * d + 1), 0))
    return jnp.where(rows >= L - NSEG * d, wrapped, pltpu.roll(x, L - NSEG * d, 0))


def conv_fwd(proj, w, b, L):
    def body(x_ref, w_ref, b_ref, o_ref):
        x = x_ref[...]
        rows = lax.broadcasted_iota(jnp.int32, x.shape, 0)
        acc = jnp.broadcast_to(b_ref[...], x.shape)
        for k in range(CONV_W):
            acc = acc + w_ref[pl.ds(k, 1), :] * _shift_down(x, CONV_W - 1 - k, rows, L)
        o_ref[...] = acc

    nl = BW // LANE
    return pl.pallas_call(
        body, grid=(nl,), name="conv_fwd",
        in_specs=[pl.BlockSpec((L, LANE), lambda g: (0, 5 * nl + g)), pl.BlockSpec((CONV_W, LANE), lambda g: (0, g)),
                  pl.BlockSpec((1, LANE), lambda g: (0, g))],
        out_specs=pl.BlockSpec((L, LANE), lambda g: (0, g)),
        out_shape=jax.ShapeDtypeStruct((L, BW), F32),
        compiler_params=pltpu.CompilerParams(vmem_limit_bytes=VMEM_LIMIT),
    )(proj, w, b)


def conv_bwd(proj, w, dxc, L):
    def body(x_ref, w_ref, d_ref, dx_ref, dw_ref, db_ref):
        x, d = x_ref[...], d_ref[...]
        rows = lax.broadcasted_iota(jnp.int32, x.shape, 0)
        acc = jnp.zeros_like(x)
        for k in range(CONV_W):
            acc = acc + w_ref[pl.ds(k, 1), :] * _shift_up(d, CONV_W - 1 - k, rows, L)
            dw_ref[pl.ds(k, 1), :] = jnp.sum(d * _shift_down(x, CONV_W - 1 - k, rows, L), axis=0, keepdims=True)
        dx_ref[...] = acc
        db_ref[...] = jnp.sum(d, axis=0, keepdims=True)

    nl = BW // LANE
    blk = pl.BlockSpec((L, LANE), lambda g: (0, g))
    return pl.pallas_call(
        body, grid=(nl,), name="conv_bwd",
        in_specs=[pl.BlockSpec((L, LANE), lambda g: (0, 5 * nl + g)), pl.BlockSpec((CONV_W, LANE), lambda g: (0, g)), blk],
        out_specs=[blk, pl.BlockSpec((CONV_W, LANE), lambda g: (0, g)), pl.BlockSpec((1, LANE), lambda g: (0, g))],
        out_shape=[jax.ShapeDtypeStruct((L, BW), F32), jax.ShapeDtypeStruct((CONV_W, BW), F32),
                   jax.ShapeDtypeStruct((1, BW), F32)],
        compiler_params=pltpu.CompilerParams(vmem_limit_bytes=VMEM_LIMIT),
    )(proj, w, dxc)


def loss_fwd_bwd(x, fw, target, L, tm):
    def fn(x, fw, t):
        err = jnp.square(_rms(x, fw) - t)
        return jnp.sum(0.5 * jnp.mean(err, axis=-1, keepdims=True), axis=0, keepdims=True)

    def body(x_ref, fw_ref, t_ref, l_ref, dx_ref, dfw_ref):
        i = pl.program_id(0)
        t = t_ref[...]
        val, vjp = jax.vjp(lambda x, fw: fn(x, fw, t), x_ref[...], fw_ref[...])
        dx, dfw = vjp(jnp.ones((1, 1), F32))
        dx_ref[...] = dx

        @pl.when(i == 0)
        def _():
            l_ref[...] = jnp.zeros_like(l_ref)
            dfw_ref[...] = jnp.zeros_like(dfw_ref)

        l_ref[...] += jnp.broadcast_to(val, l_ref.shape)
        dfw_ref[...] += dfw

    row = pl.BlockSpec((tm, D_MODEL), lambda i: (i, 0))
    vec = pl.BlockSpec((1, D_MODEL), lambda i: (0, 0))
    return pl.pallas_call(
        body, grid=(L // tm,), name="loss_fwd_bwd", in_specs=[row, vec, row],
        out_specs=[pl.BlockSpec((1, LANE), lambda i: (0, 0)), row, vec],
        out_shape=[jax.ShapeDtypeStruct((1, LANE), F32), jax.ShapeDtypeStruct((L, D_MODEL), F32),
                   jax.ShapeDtypeStruct((1, D_MODEL), F32)],
        compiler_params=pltpu.CompilerParams(vmem_limit_bytes=VMEM_LIMIT, dimension_semantics=("arbitrary",)),
    )(x, fw, target)


def adamw(w, g, m, v):
    rows, cols = w.shape
    tr = _row_tile(rows, cols, budget=1024 * 1024)
    c1 = 1.0 - ADAM_B1 ** ADAM_STEP
    c2 = 1.0 - ADAM_B2 ** ADAM_STEP

    def body(w_ref, g_ref, m_ref, v_ref, d_ref, nm_ref, nv_ref):
        g = g_ref[...]
        nm = ADAM_B1 * m_ref[...] + (1.0 - ADAM_B1) * g
        nv = ADAM_B2 * v_ref[...] + (1.0 - ADAM_B2) * jnp.square(g)
        d_ref[...] = -ADAM_LR * ((nm / c1) / (jnp.sqrt(nv / c2) + ADAM_EPS) + ADAM_WD * w_ref[...])
        nm_ref[...] = nm
        nv_ref[...] = nv

    blk = pl.BlockSpec((tr, cols), lambda i: (i, 0))
    return pl.pallas_call(
        body, grid=(rows // tr,), name="adamw", in_specs=[blk] * 4, out_specs=[blk] * 3,
        out_shape=[jax.ShapeDtypeStruct((rows, cols), F32)] * 3,
    )(w, g, m, v)


def s5_prep(lam_re, lam_im, log_dt, b_re, b_im, c_re, c_im):
    lr = jnp.minimum(lam_re, -1e-4)
    li = lam_im
    dt = jnp.exp(log_dt)[:, None]
    mag = jnp.exp(lr * dt)
    ar = mag * jnp.cos(li * dt)
    ai = mag * jnp.sin(li * dt)
    den = lr * lr + li * li
    fr = ((ar - 1.0) * lr + ai * li) / den
    fi = (ai * lr - (ar - 1.0) * li) / den
    bbr = fr[..., None] * b_re - fi[..., None] * b_im
    bbi = fr[..., None] * b_im + fi[..., None] * b_re
    hg = S5_GROUPS // 2
    emb_b = lambda bb: _block_diag(bb.transpose(0, 2, 1).reshape(hg * S5_GROUP, S5_STATE), hg)
    emb_c = lambda cc: _block_diag(cc.transpose(0, 2, 1).reshape(hg * S5_STATE, S5_GROUP), hg)
    bsub = jnp.stack([emb_b(bbr[:hg]), emb_b(bbr[hg:]), emb_b(bbi[:hg]), emb_b(bbi[hg:])])
    csub = jnp.stack([jnp.concatenate([emb_c(c_re[:hg]), -emb_c(c_im[:hg])], axis=0),
                      jnp.concatenate([emb_c(c_re[hg:]), -emb_c(c_im[hg:])], axis=0)])
    nb = S5_N // LANE
    return ar.reshape(nb, 1, LANE), ai.reshape(nb, 1, LANE), bsub, csub


def _block_diag(stacked, groups):
    rows, c = stacked.shape
    r = rows // groups
    row_g = jnp.arange(rows)[:, None] // r
    col_g = jnp.arange(groups * c)[None, :] // c
    return jnp.where(row_g == col_g, jnp.tile(stacked, (1, groups)), 0.0)


def rg_prep(w):
    return _block_diag(w.reshape(BW, RG_BLOCK), RG_BLOCKS)


def hg_prep(logits):
    p = jax.nn.softmax(logits, axis=0)
    return jnp.cumsum(p, axis=0) - p[0]


def _head_mean_matrix():
    r = jnp.arange(BW) // HG_D
    return (r[:, None] == r[None, :]).astype(F32) / HG_D


def _to_segment_order(a):
    L = a.shape[0]
    return a.reshape(NSEG, L // NSEG, -1).transpose(1, 0, 2).reshape(a.shape)


def _to_time_order(a):
    L = a.shape[0]
    return a.reshape(L // NSEG, NSEG, -1).transpose(1, 0, 2).reshape(a.shape)


def _const(*idx):
    return lambda s, i: idx


def _rows(cb=0):
    return lambda s, i: (i, cb)


def _sum_parts(name, first, parts, shape):
    return add_n(name, [(first, ())] + [(parts, (s,)) for s in range(NSH)], shape)


def _ffn_weight_specs(l, j):
    F = D_FF // NSH
    one = pl.Buffered(1)
    return [pl.BlockSpec((None, NSH, D_MODEL, F), lambda i: (j, 0, 0, 0), pipeline_mode=one),
            pl.BlockSpec((None, NSH, D_MODEL, F), lambda i: (j, 0, 0, 0), pipeline_mode=one),
            pl.BlockSpec((None, NSH, F, D_MODEL), lambda i: (j, 0, 0, 0), pipeline_mode=one)]


def ffn_fwd(name, x, W, l, j, k, L, tm):
    D, F = D_MODEL, D_FF // NSH

    def body(x_ref, nw_ref, wg_ref, wu_ref, wd_ref, y_ref, g_ref, u_ref):
        x = x_ref[...]
        h = _rms(x, nw_ref[...]).astype(MMT)
        y = x
        for s in range(NSH):
            g = _dg(h, wg_ref[s], 1, 0)
            u = _dg(h, wu_ref[s], 1, 0)
            g_ref[s] = g.astype(g_ref.dtype)
            u_ref[s] = u.astype(u_ref.dtype)
            y = y + 0.5 * _dg((jax.nn.silu(g) * u).astype(MMT), wd_ref[s], 1, 0)
        y_ref[...] = y

    row = pl.BlockSpec((tm, D), lambda i: (i, 0))
    act = pl.BlockSpec((NSH, tm, F), lambda i: (0, i, 0))
    return pl.pallas_call(
        body, grid=(L // tm,), name=name,
        in_specs=[row, pl.BlockSpec((None, None, 1, D), lambda i: (l, k, 0, 0))] + _ffn_weight_specs(l, j),
        out_specs=[row, act, act],
        out_shape=[jax.ShapeDtypeStruct((L, D), F32), jax.ShapeDtypeStruct((NSH, L, F), MMT),
                   jax.ShapeDtypeStruct((NSH, L, F), MMT)],
        compiler_params=pltpu.CompilerParams(vmem_limit_bytes=VMEM_LIMIT, dimension_semantics=("arbitrary",)),
    )(x, W["nw"], W["L"][l]["wg"], W["L"][l]["wu"], W["L"][l]["wd"])


def ffn_bwd(name, x, g, u, dy, W, bufs, l, j, k, L, tm):
    D, F = D_MODEL, D_FF // NSH
    tm = min(TM_WGRAD, L)

    def body(x_ref, nw_ref, dy_ref, g_ref, u_ref, wg_ref, wu_ref, wd_ref, *rest):
        part_ref, dnw_ref, dwg_ref, dwu_ref, dwd_ref = rest[-5:]
        s, i = pl.program_id(0), pl.program_id(1)
        x, nw = x_ref[...], nw_ref[...]
        r = lax.rsqrt(jnp.mean(x * x, axis=-1, keepdims=True) + EPS)
        xhat = x * r
        h = (xhat * nw).astype(MMT)
        half_dy = (0.5 * dy_ref[...]).astype(MMT)
        gs, us = g_ref[...].astype(F32), u_ref[...].astype(F32)
        sig = jax.nn.sigmoid(gs)
        act = gs * sig
        da = _dg(half_dy, wd_ref[...], 1, 1)
        du = (da * act).astype(MMT)
        dg = (da * us * (sig * (1.0 + gs * (1.0 - sig)))).astype(MMT)
        dh = _dg(dg, wg_ref[...], 1, 1) + _dg(du, wu_ref[...], 1, 1)
        dxh = dh * nw
        part_ref[...] = r * (dxh - xhat * jnp.mean(dxh * xhat, axis=-1, keepdims=True))
        grads = (_dg(h, dg, 0, 0), _dg(h, du, 0, 0), _dg((act * us).astype(MMT), half_dy, 0, 0))
        dnw = jnp.sum(dh * xhat, axis=0, keepdims=True)
        first = jnp.logical_and(s == 0, i == 0)
        for ref, val, start in zip((dwg_ref, dwu_ref, dwd_ref, dnw_ref), grads + (dnw,), (i == 0, i == 0, i == 0, first)):
            @pl.when(start)
            def _(ref=ref, val=val):
                ref[...] = val

            @pl.when(jnp.logical_not(start))
            def _(ref=ref, val=val):
                ref[...] += val

    row = pl.BlockSpec((tm, D), lambda s, i: (i, 0))
    act = pl.BlockSpec((None, tm, F), lambda s, i: (s, i, 0))
    wsp = lambda r, c: pl.BlockSpec((None, None, r, c), lambda s, i: (j, s, 0, 0))
    gsp = lambda r, c: pl.BlockSpec((None, None, r, c), lambda s, i: (0, s, 0, 0))
    part, dnw, bufs[("ffn_gate", l, j)], bufs[("ffn_up", l, j)], bufs[("ffn_down", l, j)] = pl.pallas_call(
        body, grid=(NSH, L // tm), name=name,
        in_specs=[row, pl.BlockSpec((None, None, 1, D), lambda s, i: (l, k, 0, 0)), row, act, act,
                  wsp(D, F), wsp(D, F), wsp(F, D)],
        out_specs=[pl.BlockSpec((None, tm, D), lambda s, i: (s, i, 0)), pl.BlockSpec((1, D), lambda s, i: (0, 0)),
                   gsp(D, F), gsp(D, F), gsp(F, D)],
        out_shape=[jax.ShapeDtypeStruct((NSH, L, D), F32), jax.ShapeDtypeStruct((1, D), F32)]
        + [jax.ShapeDtypeStruct((1, NSH, D, F), F32)] * 2 + [jax.ShapeDtypeStruct((1, NSH, F, D), F32)],
        compiler_params=pltpu.CompilerParams(vmem_limit_bytes=VMEM_LIMIT, dimension_semantics=("arbitrary", "arbitrary")),
    )(x, W["nw"], dy, g, u, W["L"][l]["wg"], W["L"][l]["wu"], W["L"][l]["wd"])
    return _sum_parts(name + "_dx", dy, part, (L, D)), dnw


def layer_fwd(l, x0, W, P, L, tm):
    D = D_MODEL
    tmm = tm
    tm = min(TM_FWD, L)
    n_i = L // tm
    x1, g0, u0 = ffn_fwd(f"ffn_fwd_{l}0", x0, W, l, 0, 0, L, tm)
    proj = tile_fwd(
        lambda x, nw, win, s: pre_core(x, nw, win), f"pre_fwd_{l}", n_i, NSH,
        [(x1, (tm, D), _rows()), (W["nw"], (None, None, 1, D), _const(l, 1, 0, 0)),
         (W["L"][l]["win"], (None, D, IN_TOTAL // NSH), lambda s, i: (s, 0, 0))],
        [((L, IN_TOTAL), F32, (tm, IN_TOTAL // NSH), lambda s, i: (i, s), False)], s_outer=True)[0]
    nb = S5_N // LANE
    blk3 = lambda s, i: (0, i, 0)
    bur, bui = tile_fwd(
        lambda *a: s5_pre_core(*a[:-1]), f"s5pre_fwd_{l}", n_i, 1,
        [(proj, (tm, BW), _rows(0))] + [(P["bsub"], (None, None, BW // 2, S5_N // 2), _const(l, q, 0, 0)) for q in range(4)],
        [((nb, L, LANE), F32, (nb, tm, LANE), blk3, False)] * 2)
    xr, xi = s5_scan_fwd(bur, bui, P["ar"][l], P["ai"][l], L)
    qzv = _to_time_order(proj[:, BW:4 * BW])
    o_t, sst = hg_fwd(qzv, P["lb"][l], L)
    o = _to_segment_order(o_t)
    xc = conv_fwd(proj, W["convw"][l], P["convb"][l], L)
    vec = (None, 1, BW)
    a, b = tile_fwd(
        lambda xc, wa, ba, wx, bx, lam, s: gates_core(xc, wa, ba, wx, bx, lam), f"gates_fwd_{l}", n_i, 1,
        [(xc, (tm, BW), _rows()), (P["wa"], (None, BW, BW), _const(l, 0, 0)), (P["ba"], vec, _const(l, 0, 0)),
         (P["wx"], (None, BW, BW), _const(l, 0, 0)), (P["bx"], vec, _const(l, 0, 0)), (P["lam"], vec, _const(l, 0, 0))],
        [((L, BW), F32, (tm, BW), _rows(), False)] * 2)
    hs = rg_scan_fwd(a, b, L)
    ya, yb, yc = tile_fwd(
        lambda *a: mid_core(*a[:-1]), f"mid_fwd_{l}", L // tmm, 1,
        [(xr, (nb, tmm, LANE), blk3), (xi, (nb, tmm, LANE), blk3), (proj, (tmm, BW), _rows(0)), (o, (tmm, BW), _rows()),
         (proj, (tmm, BW), _rows(4)), (hs, (tmm, BW), _rows()), (proj, (tmm, BW), _rows(6)),
         (P["hmat"], (BW, BW), _const(0, 0)), (P["csub"], (None, None, S5_N, BW // 2), _const(l, 0, 0, 0)),
         (P["csub"], (None, None, S5_N, BW // 2), _const(l, 1, 0, 0)), (P["d"], vec, _const(l, 0, 0)),
         (W["L"][l]["gluw"], (BW, BW), _const(0, 0)), (P["glub"], vec, _const(l, 0, 0)), (P["hgw"], vec, _const(l, 0, 0))],
        [((L, BW), F32, (tmm, BW), _rows(), False)] * 3)
    x2 = tile_fwd(
        lambda x, *rest: (x + merge_core(*rest[:-1])[0],), f"merge_fwd_{l}", n_i, 1,
        [(x1, (tm, D), _rows()), (ya, (tm, BW), _rows()), (yb, (tm, BW), _rows()), (yc, (tm, BW), _rows())]
        + [(proj, (tm, BW), _rows(7 + k)) for k in range(6)]
        + [(W["L"][l]["pfull"], (3, BW, D), _const(0, 0, 0)), (W["L"][l]["woutfull"], (D, D), _const(0, 0))],
        [((L, D), F32, (tm, D), _rows(), False)])[0]
    x3, g1, u1 = ffn_fwd(f"ffn_fwd_{l}1", x2, W, l, 1, 2, L, tm)
    saved = dict(x0=x0, x1=x1, x2=x2, proj=proj, xr=xr, xi=xi, o=o, sst=sst, xc=xc, a=a, hs=hs, ya=ya, yb=yb, yc=yc,
                 qzv=qzv, g0=g0, u0=u0, g1=g1, u1=u1)
    return x3, saved


def layer_bwd(l, dx3, sv, W, P, bufs, L, tm, ready=lambda l, group: None):
    D = D_MODEL
    n_i = L // tm
    nb = S5_N // LANE
    dq = D // NSH
    vec = (None, 1, BW)
    vout = ((1, BW), (1, BW), _const(0, 0), "acc_all")
    blk3 = lambda s, i: (0, i, 0)
    small = {}
    proj = sv["proj"]

    dx2, dnw2 = ffn_bwd(f"ffn_bwd_{l}1", sv["x2"], sv["g1"], sv["u1"], dx3, W, bufs, l, 1, 2, L, tm)
    ready(l, "ffn1")

    rw256 = ((L, BW), (tm, BW), _rows(), "write")
    res = tile_bwd(
        merge_core, f"merge_bwd_{l}", n_i, 1,
        [(sv["ya"], (tm, BW), _rows(), "r"), (sv["yb"], (tm, BW), _rows(), "r"), (sv["yc"], (tm, BW), _rows(), "r")]
        + [(proj, (tm, BW), _rows(7 + k), "r") for k in range(6)]
        + [(W["L"][l]["pfull"], (3, BW, D), _const(0, 0, 0), "w"), (W["L"][l]["woutfull"], (D, D), _const(0, 0), "w")],
        [(dx2, (tm, D), _rows())],
        [rw256] * 9
        + [((3, BW, D), (3, BW, D), _const(0, 0, 0), "acc_all"), ((D, D), (D, D), _const(0, 0), "acc_all")])
    dya, dyb, dyc = res[:3]
    dgm = res[3:9]
    bufs[("branch_proj", l)], bufs[("w_out", l)] = res[9:]
    ready(l, "merge")

    tmm = tm
    rw = ((L, BW), (tmm, BW), _rows(), "write")
    xw = ((nb, L, LANE), (nb, tmm, LANE), blk3, "write")
    res = tile_bwd(
        mid_core, f"mid_bwd_{l}", L // tmm, 1,
        [(sv["xr"], (nb, tmm, LANE), blk3, "r"), (sv["xi"], (nb, tmm, LANE), blk3, "r"), (proj, (tmm, BW), _rows(0), "r"),
         (sv["o"], (tmm, BW), _rows(), "r"), (proj, (tmm, BW), _rows(4), "r"), (sv["hs"], (tmm, BW), _rows(), "r"),
         (proj, (tmm, BW), _rows(6), "r"), (P["hmat"], (BW, BW), _const(0, 0), "c"),
         (P["csub"], (None, None, S5_N, BW // 2), _const(l, 0, 0, 0), "w"),
         (P["csub"], (None, None, S5_N, BW // 2), _const(l, 1, 0, 0), "w"), (P["d"], vec, _const(l, 0, 0), "p"),
         (W["L"][l]["gluw"], (BW, BW), _const(0, 0), "w"), (P["glub"], vec, _const(l, 0, 0), "p"),
         (P["hgw"], vec, _const(l, 0, 0), "p")],
        [(dya, (tmm, BW), _rows()), (dyb, (tmm, BW), _rows()), (dyc, (tmm, BW), _rows())],
        [xw, xw, rw, rw, rw, rw, rw,
         ((DEPTH, S5_N, BW // 2), (None, S5_N, BW // 2), _const(l, 0, 0), "acc_all", bufs.get("csub0")),
         ((DEPTH, S5_N, BW // 2), (None, S5_N, BW // 2), _const(l, 0, 0), "acc_all", bufs.get("csub1")), vout,
         ((BW, BW), (BW, BW), _const(0, 0), "acc_all"), vout, vout])
    dxr, dxi, du_skip, do, dg_b, dhs, dgate_c, bufs["csub0"], bufs["csub1"], dd, bufs[("s5_glu_w", l)], dglub, dhgw = res
    small["s5_d"], small["s5_glu_b"], small["hg_norm_w"] = dd[0], dglub[0], dhgw[0]
    ready(l, "mid")

    da, db = rg_scan_bwd(sv["a"], sv["hs"], dhs, L)
    wmat = lambda key: ((DEPTH, BW, BW), (None, BW, BW), _const(l, 0, 0), "acc_all", bufs.get(key))
    res = tile_bwd(
        gates_core, f"gates_bwd_{l}", n_i, 1,
        [(sv["xc"], (tm, BW), _rows(), "r"), (P["wa"], (None, BW, BW), _const(l, 0, 0), "w"), (P["ba"], vec, _const(l, 0, 0), "p"),
         (P["wx"], (None, BW, BW), _const(l, 0, 0), "w"), (P["bx"], vec, _const(l, 0, 0), "p"), (P["lam"], vec, _const(l, 0, 0), "p")],
        [(da, (tm, BW), _rows()), (db, (tm, BW), _rows())],
        [((L, BW), (tm, BW), _rows(), "write"), wmat("wa"), vout, wmat("wx"), vout, vout])
    dxc, bufs["wa"], dba, bufs["wx"], dbx, dlam = res
    small["rg_ba"], small["rg_bx"], small["rg_lambda"] = dba[0], dbx[0], dlam[0]
    dx_c, dconvw, dconvb = conv_bwd(proj, W["convw"][l], dxc, L)
    small["rg_conv_w"], small["rg_conv_b"] = dconvw, dconvb[0]

    dq_b, dz_b, dv_b, dlb = hg_bwd(sv["qzv"], P["lb"][l], sv["sst"], _to_time_order(do), L)
    dq_b, dz_b, dv_b = [_to_segment_order(a) for a in (dq_b, dz_b, dv_b)]

    gr, gi, dar, dai = s5_scan_bwd(dxr, dxi, sv["xr"], sv["xi"], P["ar"][l], P["ai"][l], L)
    bblk = (None, None, BW // 2, S5_N // 2)
    res = tile_bwd(
        s5_pre_core, f"s5pre_bwd_{l}", n_i, 1,
        [(proj, (tm, BW), _rows(0), "r")] + [(P["bsub"], bblk, _const(l, q, 0, 0), "w") for q in range(4)],
        [(gr, (nb, tm, LANE), blk3), (gi, (nb, tm, LANE), blk3)],
        [((L, BW), (tm, BW), _rows(), "write")]
        + [((DEPTH, BW // 2, S5_N // 2), bblk[1:], _const(l, 0, 0), "acc_all", bufs.get(f"bsub{q}")) for q in range(4)])
    du_pre = res[0]
    for q in range(4):
        bufs[f"bsub{q}"] = res[1 + q]
    du_a = add_n(f"du_a_{l}", [(du_skip, ()), (du_pre, ())], (L, BW))
    prep_ct = dict(dar=dar, dai=dai, dlb=dlb)

    pieces = [du_a, dq_b, dz_b, dv_b, dg_b, dx_c, dgate_c, *dgm]
    per_piece, per_shard = BW // LANE, IN_TOTAL // NSH // LANE
    dx1, dnw1 = dx2, []
    tmw = min(TM_WGRAD, L)
    pre_and_x = lambda x, nw, win: (pre_core(x, nw, win)[0], x)
    for s in range(NSH):
        groups = [(pieces[g // per_piece], (tmw, LANE), _rows(g % per_piece))
                  for g in range(s * per_shard, (s + 1) * per_shard)]
        dx1, dnw_s, bufs[("w_in", l)] = tile_bwd(
            pre_and_x, f"pre_bwd_{l}{s}", L // tmw, 1,
            [(sv["x1"], (tmw, D), _rows(), "r"), (W["nw"], (None, None, 1, D), _const(l, 1, 0, 0), "p"),
             (W["L"][l]["win"], (None, D, IN_TOTAL // NSH), _const(s, 0, 0), "w")],
            [groups, (dx1, (tmw, D), _rows())],
            [((L, D), (tmw, D), _rows(), "write"),
             ((1, D), (1, D), _const(0, 0), "acc_all"),
             ((1, NSH, D, IN_TOTAL // NSH), (None, None, D, IN_TOTAL // NSH), _const(0, s, 0, 0), "acc_all",
              bufs.get(("w_in", l)))])
        dnw1.append(dnw_s)
    dnw1 = (dnw1[0] + dnw1[1]) + (dnw1[2] + dnw1[3])
    ready(l, "pre")

    dx0, dnw0 = ffn_bwd(f"ffn_bwd_{l}0", sv["x0"], sv["g0"], sv["u0"], dx1, W, bufs, l, 0, 0, L, tm)
    ready(l, "ffn0")
    small["norm_w"] = jnp.concatenate([dnw0, dnw1, dnw2], axis=0)
    return dx0, small, prep_ct


SMALL_RAW = ("s5_lambda_re", "s5_lambda_im", "s5_log_dt", "s5_b_re", "s5_b_im", "s5_c_re", "s5_c_im", "s5_d", "s5_glu_b",
             "hg_lb_logits", "hg_norm_w", "rg_conv_b", "rg_wa", "rg_ba", "rg_wx", "rg_bx", "rg_lambda", "final_norm_w")
DEPTH = 2


def local_step(x, target, W, raw, layer_weights=None, layer_grads=None):
    L = x.shape[0]
    tm = min(256, L)
    col = lambda v: v.reshape(DEPTH, 1, BW)
    (ar, ai, bsub, csub), s5_vjp = jax.vjp(jax.vmap(s5_prep), *[raw[k] for k in SMALL_RAW[:7]])
    (wa, wx), rg_vjp = jax.vjp(lambda a, b: (jax.vmap(rg_prep)(a), jax.vmap(rg_prep)(b)), raw["rg_wa"], raw["rg_wx"])
    lb, hg_vjp = jax.vjp(hg_prep, raw["hg_lb_logits"])
    P = dict(
        ar=[ar[l] for l in range(DEPTH)], ai=[ai[l] for l in range(DEPTH)],
        bsub=bsub.astype(MMT), csub=csub.astype(MMT), wa=wa.astype(MMT), wx=wx.astype(MMT),
        lb=[lb[l].reshape(1, BW) for l in range(DEPTH)], convb=[raw["rg_conv_b"][l].reshape(1, BW) for l in range(DEPTH)],
        ba=col(raw["rg_ba"]), bx=col(raw["rg_bx"]), lam=col(raw["rg_lambda"]), d=col(raw["s5_d"]),
        glub=col(raw["s5_glu_b"]), hgw=col(raw["hg_norm_w"]), hmat=_head_mean_matrix())

    saved = []
    h = _to_segment_order(x)
    for l in range(DEPTH):
        if layer_weights is not None:
            W["L"][l], h = layer_weights(l, h)
        h, sv = layer_fwd(l, h, W, P, L, tm)
        saved.append(sv)
    loss, dh, dfw = loss_fwd_bwd(h, raw["final_norm_w"].reshape(1, D_MODEL), _to_segment_order(target), L, tm)

    big, per_layer, prep_cts = {}, [None] * DEPTH, [None] * DEPTH
    ready = (lambda l, group: None) if layer_grads is None else (lambda l, group: layer_grads(l, group, big))
    for l in reversed(range(DEPTH)):
        dh, sm, pc = layer_bwd(l, dh, saved[l], W, P, big, L, tm, ready)
        per_layer[l], prep_cts[l] = sm, pc
    dh = _to_time_order(dh)

    small = {k: jnp.stack([per_layer[l][k] for l in range(DEPTH)]) for k in per_layer[0]}
    both = lambda k: jnp.stack([prep_cts[l][k] for l in range(DEPTH)])
    dbsub = jnp.stack([big.pop(f"bsub{q}") for q in range(4)], axis=1)
    dcsub = jnp.stack([big.pop("csub0"), big.pop("csub1")], axis=1)
    s5_g = s5_vjp((both("dar"), both("dai"), dbsub, dcsub))
    small.update(zip(SMALL_RAW[:7], s5_g))
    small["rg_wa"], small["rg_wx"] = rg_vjp((big.pop("wa"), big.pop("wx")))
    (small["hg_lb_logits"],) = hg_vjp(jnp.concatenate([prep_cts[l]["dlb"] for l in range(DEPTH)], axis=0))
    small["final_norm_w"] = dfw[0]
    return loss, dh, big, small


ANY = pl.BlockSpec(memory_space=pl.ANY)


def _place():
    x, y, c = lax.axis_index("x"), lax.axis_index("y"), lax.axis_index("c")
    chips = [(1 - x, y), (x, 1 - y), (1 - x, 1 - y)]
    return x, y, c, chips


def _remote(src, dst, send, recv, k, to):
    return pltpu.make_async_remote_copy(src_ref=src, dst_ref=dst, send_sem=send.at[k], recv_sem=recv.at[k],
                                        device_id=to, device_id_type=MESH)


def _comm_call(body, name, ins, out_shapes, n_sem, n_loc):
    return pl.pallas_call(
        body, name=name, in_specs=[ANY] * len(ins), out_specs=[ANY] * len(out_shapes), out_shape=out_shapes,
        scratch_shapes=[pltpu.SemaphoreType.DMA((n_sem,)), pltpu.SemaphoreType.DMA((n_sem,)),
                        pltpu.SemaphoreType.DMA((max(n_loc, 1),))],
    )(*ins)


def gather_shards(name, shards):
    n = len(shards)
    per = 8

    def body(*refs):
        ins, outs = refs[:n], refs[n:2 * n]
        send, recv, _ = refs[2 * n:]
        x, y, c, chips = _place()
        me = 2 * x + y
        sib = (x, y, 1 - c)
        sends = []
        for w in range(n):
            for j, (cx, cy) in enumerate(chips):
                cp = _remote(ins[w].at[c], outs[w].at[c, me], send, recv, per * w + j, (cx, cy, c))
                cp.start()
                sends.append(cp)
        for w in range(n):
            for l in range(2):
                cp = _remote(ins[w].at[l], outs[w].at[l, me], send, recv, per * w + 6 + l, sib)
                cp.start()
                sends.append(cp)
        for w in range(n):
            for j, (cx, cy) in enumerate(chips):
                theirs = outs[w].at[c, 2 * cx + cy]
                _remote(ins[w].at[c], theirs, send, recv, per * w + j, (cx, cy, c)).wait_recv()
                cp = _remote(theirs, theirs, send, recv, per * w + 3 + j, sib)
                cp.start()
                sends.append(cp)
        for w in range(n):
            for j, (cx, cy) in enumerate(chips):
                dst = outs[w].at[1 - c, 2 * cx + cy]
                _remote(dst, dst, send, recv, per * w + 3 + j, sib).wait_recv()
            for l in range(2):
                dst = outs[w].at[l, me]
                _remote(dst, dst, send, recv, per * w + 6 + l, sib).wait_recv()
        for cp in sends:
            cp.wait_send()

    shapes = [jax.ShapeDtypeStruct((2, NSH) + s.shape[1:], s.dtype) for s in shards]
    return _comm_call(body, name, shards, shapes, per * n, 0)


def exchange_halves(name, grads, ranges):
    n = len(grads)

    def body(*refs):
        ins, outs = refs[:n], refs[n:2 * n]
        send, recv, _ = refs[2 * n:]
        x, y, c, _chips = _place()
        cps = []
        for w in range(n):
            h = grads[w].shape[2] // 2
            p0, np_ = ranges[w]
            cp = _remote(ins[w].at[pl.ds(p0, np_), :, pl.ds((1 - c) * h, h)], outs[w], send, recv, w, (x, y, 1 - c))
            cp.start()
            cps.append(cp)
        for cp in cps:
            cp.wait()

    shapes = [jax.ShapeDtypeStruct((r[1], NSH, g.shape[2] // 2, g.shape[3]), g.dtype) for g, r in zip(grads, ranges)]
    return _comm_call(body, name, grads, shapes, n, 0)


def share_halves(name, pieces):
    n = len(pieces)

    def body(*refs):
        ins, outs = refs[:n], refs[n:2 * n]
        send, recv, _ = refs[2 * n:]
        x, y, c, _chips = _place()
        cps = []
        for w in range(n):
            cp = _remote(ins[w], outs[w], send, recv, w, (x, y, 1 - c))
            cp.start()
            cps.append(cp)
        for cp in cps:
            cp.wait()

    return _comm_call(body, name, pieces, [jax.ShapeDtypeStruct(p.shape, p.dtype) for p in pieces], n, 0)


def add_own_half(name, g, ra, c, wire, b0):
    nblk, h, cols = ra.shape
    tr = _row_tile(h, cols, mult=16)
    nt = h // tr

    def body(c_ref, g_ref, r_ref, o_ref):
        o_ref[...] = (g_ref[...] + r_ref[...]).astype(o_ref.dtype)

    blk = (None, tr, cols)
    return pl.pallas_call(
        body, name=name,
        grid_spec=pltpu.PrefetchScalarGridSpec(
            num_scalar_prefetch=1, grid=(nblk, nt),
            in_specs=[pl.BlockSpec(blk, lambda s, i, c_ref: (b0 + s, c_ref[0] * nt + i, 0)), pl.BlockSpec(blk, lambda s, i, c_ref: (s, i, 0))],
            out_specs=pl.BlockSpec(blk, lambda s, i, c_ref: (s, i, 0))),
        out_shape=jax.ShapeDtypeStruct(ra.shape, wire),
    )(c.reshape(1), g, ra)


def add_chips(name, hb, rb, me):
    npc, _, h, cols = hb.shape
    tr = _row_tile(h, cols, mult=16)

    def body(me_ref, h_ref, r0, r1, r2, o_ref):
        f = lambda r: r[...].astype(F32)
        o_ref[...] = ((f(h_ref) + f(r0)) + f(r1)) + f(r2)

    rspec = lambda j: pl.BlockSpec((None, None, tr, cols), functools.partial(lambda p, i, me_ref, j: (j, p, i, 0), j=j))
    return pl.pallas_call(
        body, name=name,
        grid_spec=pltpu.PrefetchScalarGridSpec(
            num_scalar_prefetch=1, grid=(npc, h // tr),
            in_specs=[pl.BlockSpec((None, None, tr, cols), lambda p, i, me_ref: (p, me_ref[0], i, 0)), rspec(0), rspec(1), rspec(2)],
            out_specs=pl.BlockSpec((None, tr, cols), lambda p, i, me_ref: (p, i, 0))),
        out_shape=jax.ShapeDtypeStruct((npc, h, cols), F32),
    )(me.reshape(1), hb, rb, rb, rb)


def adamw_halves(name, w, m, v, own, other, c):
    npc, rows, cols = w.shape
    h = rows // 2
    tr = _row_tile(h, cols, budget=1024 * 1024)
    nt = h // tr
    c1 = 1.0 - ADAM_B1 ** ADAM_STEP
    c2 = 1.0 - ADAM_B2 ** ADAM_STEP

    def body(c_ref, w_ref, m_ref, v_ref, own_ref, oth_ref, g_ref, d_ref, nm_ref, nv_ref):
        g = jnp.where(pl.program_id(1) == c_ref[0], own_ref[...], oth_ref[...])
        nm = ADAM_B1 * m_ref[...] + (1.0 - ADAM_B1) * g
        nv = ADAM_B2 * v_ref[...] + (1.0 - ADAM_B2) * jnp.square(g)
        g_ref[...] = g
        d_ref[...] = -ADAM_LR * ((nm / c1) / (jnp.sqrt(nv / c2) + ADAM_EPS) + ADAM_WD * w_ref[...])
        nm_ref[...] = nm
        nv_ref[...] = nv

    full = pl.BlockSpec((None, tr, cols), lambda p, hh, i, c_ref: (p, hh * nt + i, 0))
    half = pl.BlockSpec((None, tr, cols), lambda p, hh, i, c_ref: (p, i, 0))
    return pl.pallas_call(
        body, name=name,
        grid_spec=pltpu.PrefetchScalarGridSpec(
            num_scalar_prefetch=1, grid=(npc, 2, nt),
            in_specs=[full, full, full, half, half], out_specs=[full] * 4),
        out_shape=[jax.ShapeDtypeStruct(w.shape, F32)] * 4,
    )(c.reshape(1), w, m, v, own, other)


WEIGHTS = ("norm_w", "final_norm_w", "ffn_gate", "ffn_up", "ffn_down", "w_in", "branch_proj", "w_out", "s5_lambda_re",
           "s5_lambda_im", "s5_log_dt", "s5_b_re", "s5_b_im", "s5_c_re", "s5_c_im", "s5_d", "s5_glu_w", "s5_glu_b",
           "hg_lb_logits", "hg_norm_w", "rg_conv_w", "rg_conv_b", "rg_wa", "rg_ba", "rg_wx", "rg_bx", "rg_lambda")
BIG = ("ffn_gate", "ffn_up", "ffn_down", "w_in", "branch_proj", "w_out", "s5_glu_w")
SHARDED_SMALL = ("norm_w", "rg_conv_w")
SMALL = SMALL_RAW + SHARDED_SMALL


def _view2d(shape):
    return (1, shape[0]) if len(shape) == 1 else (math.prod(shape[:-1]), shape[-1])


def _small_layout(shapes, row_multiple):
    layout, at = [], 0
    for shape in shapes:
        r, c = _view2d(shape)
        rp = -(-r // 8) * 8
        layout.append((at, r, c, rp))
        at += rp * max(1, c // LANE)
    return layout, -(-at // row_multiple) * row_multiple


def pack_small(name, arrays, row_multiple):
    layout, rows = _small_layout([a.shape for a in arrays], row_multiple)

    def body(*refs):
        out = refs[-1]
        out[...] = jnp.zeros_like(out)
        for ref, (r0, r, c, rp) in zip(refs[:-1], layout):
            if c <= LANE:
                out[r0:r0 + r, 0:c] = ref[...]
            else:
                for q in range(c // LANE):
                    out[r0 + q * rp:r0 + q * rp + r, :] = ref[:, q * LANE:(q + 1) * LANE]

    return pl.pallas_call(
        body, name=name, out_shape=jax.ShapeDtypeStruct((rows, LANE), F32),
        compiler_params=pltpu.CompilerParams(vmem_limit_bytes=VMEM_LIMIT),
    )(*[a.reshape(_view2d(a.shape)) for a in arrays])


def unpack_small(name, packed, shapes):
    layout, _ = _small_layout(shapes, 8)

    def body(p_ref, *outs):
        for ref, (r0, r, c, rp) in zip(outs, layout):
            if c <= LANE:
                ref[...] = p_ref[r0:r0 + r, 0:c]
            else:
                for q in range(c // LANE):
                    ref[:, q * LANE:(q + 1) * LANE] = p_ref[r0 + q * rp:r0 + q * rp + r, :]

    res = pl.pallas_call(
        body, name=name, out_shape=[jax.ShapeDtypeStruct(_view2d(s), F32) for s in shapes],
        compiler_params=pltpu.CompilerParams(vmem_limit_bytes=VMEM_LIMIT),
    )(packed)
    return [a.reshape(s) for a, s in zip(res, shapes)]


HBM = pl.BlockSpec(memory_space=pltpu.HBM)
SEM = pl.BlockSpec(memory_space=pltpu.SEMAPHORE)
EFFECT = pltpu.SideEffectType.DATAFLOW_SIDE_EFFECTING


def split_start(name, srcs, land_shapes, plan, n_send, n_recv):
    ns, nl = len(srcs), len(land_shapes)

    def body(*refs):
        ins, lands = refs[:ns], refs[ns:ns + nl]
        send, recv = refs[ns + nl], refs[ns + nl + 1]
        for src, dst, ks, kr, dev in plan(ins, lands):
            pltpu.make_async_remote_copy(src_ref=src, dst_ref=dst, send_sem=send.at[ks], recv_sem=recv.at[kr],
                                         device_id=dev, device_id_type=MESH).start()
        refs[-1][...] = jnp.zeros_like(refs[-1])

    hbm = lambda a: pltpu.with_memory_space_constraint(a, pltpu.HBM)
    lands = [lax.empty(s.shape, s.dtype) for s in land_shapes]
    out = pl.pallas_call(
        body, name=name,
        out_shape=(pltpu.SemaphoreType.DMA((n_send,)), pltpu.SemaphoreType.DMA((n_recv,)),
                   *[pltpu.HBM(a.shape, a.dtype) for a in srcs], *[pltpu.HBM(s.shape, s.dtype) for s in land_shapes],
                   jax.ShapeDtypeStruct((8, LANE), F32)),
        in_specs=[HBM] * (ns + nl), out_specs=(SEM, SEM, *[HBM] * (ns + nl), pl.BlockSpec(memory_space=pltpu.VMEM)),
        input_output_aliases={k: 2 + k for k in range(ns + nl)},
        compiler_params=pltpu.CompilerParams(has_side_effects=EFFECT),
    )(*[hbm(a) for a in srcs], *[hbm(a) for a in lands])
    return out[:-1], out[-1]


def split_wait(name, handles, n_src, waits, after):
    send, recv, *bufs = handles
    nb = len(bufs)

    def body(*refs):
        ins, lands = refs[:n_src], refs[n_src:nb]
        send_sem, recv_sem = refs[nb], refs[nb + 1]
        x, y, c, _chips = _place()
        sends, recvs = waits(ins, lands)
        for src, k in sends:
            pltpu.make_async_remote_copy(src_ref=src, dst_ref=src, send_sem=send_sem.at[k], recv_sem=recv_sem.at[0],
                                         device_id=(x, y, 1 - c), device_id_type=MESH).wait_send()
        for dst, k in recvs:
            pltpu.make_async_remote_copy(src_ref=dst, dst_ref=dst, send_sem=send_sem.at[0], recv_sem=recv_sem.at[k],
                                         device_id=(x, y, 1 - c), device_id_type=MESH).wait_recv()

    out = pl.pallas_call(
        body, name=name, out_shape=tuple(pltpu.HBM(a.shape, a.dtype) for a in bufs),
        in_specs=[HBM] * nb + [SEM, SEM, ANY], out_specs=tuple([HBM] * nb),
        input_output_aliases={k: k for k in range(nb)},
        compiler_params=pltpu.CompilerParams(has_side_effects=EFFECT),
    )(*bufs, send, recv, after)
    return list(out[:n_src]), list(out[n_src:])


def gather_plan(n):
    def plan(ins, lands):
        x, y, c, chips = _place()
        me = 2 * x + y
        copies = []
        for w in range(n):
            for j, (cx, cy) in enumerate(chips):
                for t in range(2):
                    copies.append((ins[w].at[c], lands[w].at[c, me], 8 * w + 2 * j + t, 8 * w + 2 * j + c, (cx, cy, t)))
            for half in range(2):
                copies.append((ins[w].at[half], lands[w].at[half, me], 8 * w + 6 + half, 8 * w + 6 + half, (x, y, 1 - c)))
        return copies

    def waits(ins, lands):
        x, y, c, chips = _place()
        me = 2 * x + y
        sends, recvs = [], []
        for w in range(n):
            for j, (cx, cy) in enumerate(chips):
                for t in range(2):
                    sends.append((ins[w].at[c], 8 * w + 2 * j + t))
                    recvs.append((lands[w].at[t, 2 * cx + cy], 8 * w + 2 * j + t))
            for half in range(2):
                sends.append((ins[w].at[half], 8 * w + 6 + half))
                recvs.append((lands[w].at[half, me], 8 * w + 6 + half))
        return sends, recvs

    return plan, waits


def scatter_plan(n):
    def plan(ins, lands):
        x, y, c, chips = _place()
        return [(ins[w].at[:, 2 * cx + cy], lands[w].at[j], 3 * w + j, 3 * w + j, (cx, cy, c))
                for w in range(n) for j, (cx, cy) in enumerate(chips)]

    def waits(ins, lands):
        x, y, c, chips = _place()
        sends = [(ins[w].at[:, 2 * cx + cy], 3 * w + j) for w in range(n) for j, (cx, cy) in enumerate(chips)]
        recvs = [(lands[w].at[j], 3 * w + j) for w in range(n) for j in range(3)]
        return sends, recvs

    return plan, waits


def _layer_shards(w, l):
    return [w["ffn_gate"][l].astype(MMT), w["ffn_up"][l].astype(MMT), w["ffn_down"][l].astype(MMT),
            w["w_in"][l].reshape(2, D_MODEL // 2, -1).astype(MMT),
            w["branch_proj"][l].reshape(2, 3 * BW // 2, -1).astype(MMT),
            w["w_out"][l].reshape(2, -1, D_MODEL).astype(MMT),
            w["s5_glu_w"][l].reshape(2, -1, BW).astype(MMT)]


def _layer_weights(g):
    rows = lambda a: a.transpose(1, 0, 2, 3).reshape(NSH, -1, a.shape[-1])
    p = rows(g[4]).reshape(NSH, 3, BW, -1).transpose(1, 2, 0, 3).reshape(3, BW, D_MODEL)
    return dict(wg=g[0], wu=g[1], wd=g[2], win=rows(g[3]), pfull=p,
                woutfull=rows(g[5]).reshape(D_MODEL, D_MODEL), gluw=rows(g[6]).reshape(BW, BW))


GROUPS = {"ffn1": ("ffn_gate", "ffn_up", "ffn_down"), "merge": ("branch_proj", "w_out"), "mid": ("s5_glu_w",),
          "pre": ("w_in",), "ffn0": ("ffn_gate", "ffn_up", "ffn_down")}


def _grad_views(big, l, group):
    views = []
    for name in GROUPS[group]:
        if name == "branch_proj":
            dq = D_MODEL // NSH
            a = big[(name, l)].reshape(3, BW, NSH, dq).transpose(2, 0, 1, 3).reshape(1, NSH, 3 * BW, dq)
        elif name.startswith("ffn"):
            a = big[(name, l, 1 if group == "ffn1" else 0)]
        else:
            a = big[(name, l)]
            a = a.reshape(1, NSH, -1, a.shape[-1])
        views.append((name, a, 0))
    return views


def halves_plan(n):
    def src(ref, c):
        h = ref.shape[2] // 2
        return ref.at[:, :, pl.ds((1 - c) * h, h)]

    def plan(ins, lands):
        x, y, c, _chips = _place()
        return [(src(ins[w], c), lands[w], w, w, (x, y, 1 - c)) for w in range(n)]

    def waits(ins, lands):
        x, y, c, _chips = _place()
        return [(src(ins[w], c), w) for w in range(n)], [(lands[w], w) for w in range(n)]

    return plan, waits


def _reduce_to_halves(tag, views, c, wire):
    from_sibling = exchange_halves(f"reduce_cores_{tag}", [a for _, a, _ in views], [(p0, 1) for _, _, p0 in views])
    merge = lambda a: a.reshape((-1,) + a.shape[2:])
    return [add_own_half(f"sum_cores_{tag}_{i}", merge(a), merge(r), c, wire[i], NSH * p0).reshape(r.shape)
            for i, ((_, a, p0), r) in enumerate(zip(views, from_sibling))]


def _step(x, target, w, m, v):
    mx, my, mc = lax.axis_index("x"), lax.axis_index("y"), lax.axis_index("c")
    me = (2 * mx + my).astype(jnp.int32)
    mc = mc.astype(jnp.int32)

    W = dict(L=[None] * DEPTH)
    state = {"pending": []}
    n_big = len(BIG)
    g_plan, g_waits = gather_plan(n_big)

    def layer_weights(l, h):
        if l == 0:
            got = gather_shards("gather_weights_0", _layer_shards(w, 0) + [w[n] for n in SHARDED_SMALL])
            nxt = _layer_shards(w, 1)
            got, nxt = lax.optimization_barrier((got, nxt))
            shapes = [jax.ShapeDtypeStruct((2, NSH) + a.shape[1:], a.dtype) for a in nxt]
            state["gather"], token = split_start("gather_weights_1_start", nxt, shapes, g_plan, 8 * n_big, 8 * n_big)
            W["nw"] = got[n_big].transpose(0, 2, 1, 3).reshape(DEPTH, 3, 1, D_MODEL) + token[0, 0]
            W["convw"] = got[n_big + 1].transpose(0, 2, 1, 3).reshape(DEPTH, CONV_W, BW)
            return _layer_weights(got[:n_big]), h
        return _layer_weights(split_wait("gather_weights_1_wait", state["gather"], n_big, g_waits, h)[1]), h

    def to_chips(after):
        if "cores" not in state:
            return
        tag, names, l, group, handles, waits = state.pop("cores")
        sent, landed = split_wait(f"reduce_cores_{tag}_wait", handles, len(names), waits, after)
        merge = lambda a: a.reshape((-1,) + a.shape[2:])
        halves = [add_own_half(f"sum_cores_{tag}_{i}", merge(a), merge(r), mc, jnp.bfloat16, 0).reshape(r.shape)
                  for i, (a, r) in enumerate(zip(sent, landed))]
        shapes = [jax.ShapeDtypeStruct((3, a.shape[0]) + a.shape[2:], a.dtype) for a in halves]
        plan, waits = scatter_plan(len(halves))
        handles, token = split_start(f"reduce_chips_{tag}_start", halves, shapes, plan, 3 * len(halves), 3 * len(halves))
        W["nw"] = W["nw"] + token[0, 0]
        state["pending"].append((tag, names, l, group, handles, waits))

    def layer_grads(l, group, big):
        views = _grad_views(big, l, group)
        to_chips(views[0][1])
        if (l, group) == (0, "ffn0"):
            return
        tag = f"{l}_{group}"
        if (l, group) == (0, "pre"):
            halves = _reduce_to_halves(tag, views, mc, [jnp.bfloat16] * len(views))
            shapes = [jax.ShapeDtypeStruct((3, a.shape[0]) + a.shape[2:], a.dtype) for a in halves]
            plan, waits = scatter_plan(len(halves))
            handles, token = split_start(f"reduce_chips_{tag}_start", halves, shapes, plan, 3 * len(halves), 3 * len(halves))
            W["nw"] = W["nw"] + token[0, 0]
            state["pending"].append((tag, [name for name, _, _ in views], l, group, handles, waits))
            return
        arrays = [a for _, a, _ in views]
        shapes = [jax.ShapeDtypeStruct((1, NSH, a.shape[2] // 2, a.shape[3]), a.dtype) for a in arrays]
        plan, waits = halves_plan(len(arrays))
        handles, token = split_start(f"reduce_cores_{tag}_start", arrays, shapes, plan, len(arrays), len(arrays))
        W["nw"] = W["nw"] + token[0, 0]
        state["cores"] = (tag, [name for name, _, _ in views], l, group, handles, waits)

    loss, dx, big, small = local_step(x[0], target[0], W, {k: w[k] for k in SMALL_RAW}, layer_weights, layer_grads)

    pieces = {n: {} for n in BIG}
    block_of = lambda name, l, group: (2 * l + (group == "ffn1")) if name.startswith("ffn") else l
    views = _grad_views(big, 0, "ffn0")
    small_packed = pack_small("pack_small_grads", [small[n] for n in SMALL], NSH * 32)
    halves = _reduce_to_halves("0_ffn0", views + [("small", small_packed.reshape(1, NSH, -1, LANE), 0)], mc,
                               [jnp.bfloat16] * len(views) + [F32])
    shapes = [jax.ShapeDtypeStruct((3, a.shape[0]) + a.shape[2:], a.dtype) for a in halves]
    plan, waits = scatter_plan(len(halves))
    last_handles, token = split_start("reduce_chips_0_ffn0_start", halves, shapes, plan, 3 * len(halves), 3 * len(halves))
    mc = mc + token[0, 0].astype(jnp.int32)
    for tag, names, l, group, handles, waits_k in state["pending"]:
        sent, landed = split_wait(f"reduce_chips_{tag}_wait", handles, len(names), waits_k, dx)
        for i, (name, h, r) in enumerate(zip(names, sent, landed)):
            pieces[name][block_of(name, l, group)] = add_chips(f"sum_chips_{tag}_{i}", h, r, me)

    g, delta, new_m, new_v = {}, {}, {}, {}

    def update(tag, names, extra):
        own = [jnp.concatenate([pieces[n][b] for b in sorted(pieces[n])], axis=0) for n in names] + extra
        other = share_halves(f"reduce_share_{tag}", own)
        for i, n in enumerate(names):
            view = lambda a: a.reshape(own[i].shape[0], -1, own[i].shape[2])
            res = adamw_halves(f"adamw_{n}", view(w[n]), view(m[n]), view(v[n]), own[i], other[i], mc)
            g[n], delta[n], new_m[n], new_v[n] = [a.reshape(w[n].shape) for a in res]
        return own, other

    early = [n for n in BIG if not n.startswith("ffn")]
    update("early", early, [])
    sent, landed = split_wait("reduce_chips_0_ffn0_wait", last_handles, len(halves), waits, new_v[early[0]])
    last = [add_chips(f"sum_chips_0_ffn0_{i}", h, r, me) for i, (h, r) in enumerate(zip(sent, landed))]
    for (name, _, _), piece in zip(views, last):
        pieces[name][block_of(name, 0, "ffn0")] = piece
    own, other = update("last", [n for n in BIG if n.startswith("ffn")], [last[-1]])

    piece = jnp.stack([jnp.where(mc == 0, own[-1][0], other[-1][0]), jnp.where(mc == 0, other[-1][0], own[-1][0])])
    (all_small,) = gather_shards("gather_small", [piece])
    full_small = unpack_small("unpack_small_grads", all_small.transpose(1, 0, 2, 3).reshape(-1, LANE),
                              [small[n].shape for n in SMALL])
    g.update(zip(SMALL, full_small))
    g["norm_w"] = lax.dynamic_slice_in_dim(g["norm_w"], me * (D_MODEL // NSH), D_MODEL // NSH, axis=2)
    g["rg_conv_w"] = lax.dynamic_slice_in_dim(g["rg_conv_w"], me * (BW // NSH), BW // NSH, axis=2)

    packed = [pack_small(f"pack_small_{tag}", [src[n] for n in SMALL], 8)
              for tag, src in (("w", w), ("g", g), ("m", m), ("v", v))]
    for tag, dst, flat in zip(("delta", "m", "v"), (delta, new_m, new_v), adamw(*packed)):
        dst.update(zip(SMALL, unpack_small(f"unpack_small_{tag}", flat, [w[n].shape for n in SMALL])))

    total = lax.psum(loss[0, 0], ("x", "y", "c"))
    return (total, dx[None], *[g[n] for n in WEIGHTS], *[delta[n] for n in WEIGHTS],
            *[new_m[n] for n in WEIGHTS], *[new_v[n] for n in WEIGHTS])


def kernel(x, norm_w, final_norm_w, ffn_gate, ffn_up, ffn_down, w_in, branch_proj, w_out, s5_lambda_re, s5_lambda_im, s5_log_dt, s5_b_re, s5_b_im, s5_c_re, s5_c_im, s5_d, s5_glu_w, s5_glu_b, hg_lb_logits, hg_norm_w, rg_conv_w, rg_conv_b, rg_wa, rg_ba, rg_wx, rg_bx, rg_lambda, loss_target, m_norm_w, m_final_norm_w, m_ffn_gate, m_ffn_up, m_ffn_down, m_w_in, m_branch_proj, m_w_out, m_s5_lambda_re, m_s5_lambda_im, m_s5_log_dt, m_s5_b_re, m_s5_b_im, m_s5_c_re, m_s5_c_im, m_s5_d, m_s5_glu_w, m_s5_glu_b, m_hg_lb_logits, m_hg_norm_w, m_rg_conv_w, m_rg_conv_b, m_rg_wa, m_rg_ba, m_rg_wx, m_rg_bx, m_rg_lambda, v_norm_w, v_final_norm_w, v_ffn_gate, v_ffn_up, v_ffn_down, v_w_in, v_branch_proj, v_w_out, v_s5_lambda_re, v_s5_lambda_im, v_s5_log_dt, v_s5_b_re, v_s5_b_im, v_s5_c_re, v_s5_c_im, v_s5_d, v_s5_glu_w, v_s5_glu_b, v_hg_lb_logits, v_hg_norm_w, v_rg_conv_w, v_rg_conv_b, v_rg_wa, v_rg_ba, v_rg_wx, v_rg_bx, v_rg_lambda):
    ws = (norm_w, final_norm_w, ffn_gate, ffn_up, ffn_down, w_in, branch_proj, w_out, s5_lambda_re, s5_lambda_im, s5_log_dt, s5_b_re, s5_b_im, s5_c_re, s5_c_im, s5_d, s5_glu_w, s5_glu_b, hg_lb_logits, hg_norm_w, rg_conv_w, rg_conv_b, rg_wa, rg_ba, rg_wx, rg_bx, rg_lambda)
    ms = (m_norm_w, m_final_norm_w, m_ffn_gate, m_ffn_up, m_ffn_down, m_w_in, m_branch_proj, m_w_out, m_s5_lambda_re, m_s5_lambda_im, m_s5_log_dt, m_s5_b_re, m_s5_b_im, m_s5_c_re, m_s5_c_im, m_s5_d, m_s5_glu_w, m_s5_glu_b, m_hg_lb_logits, m_hg_norm_w, m_rg_conv_w, m_rg_conv_b, m_rg_wa, m_rg_ba, m_rg_wx, m_rg_bx, m_rg_lambda)
    vs = (v_norm_w, v_final_norm_w, v_ffn_gate, v_ffn_up, v_ffn_down, v_w_in, v_branch_proj, v_w_out, v_s5_lambda_re, v_s5_lambda_im, v_s5_log_dt, v_s5_b_re, v_s5_b_im, v_s5_c_re, v_s5_c_im, v_s5_d, v_s5_glu_w, v_s5_glu_b, v_hg_lb_logits, v_hg_norm_w, v_rg_conv_w, v_rg_conv_b, v_rg_wa, v_rg_ba, v_rg_wx, v_rg_bx, v_rg_lambda)
    return _step(x, loss_target, dict(zip(WEIGHTS, ws)), dict(zip(WEIGHTS, ms)), dict(zip(WEIGHTS, vs)))
```

```python
import functools
import math
from typing import NamedTuple

import jax
import jax.numpy as jnp
from jax import lax
from jax.experimental import pallas as pl
from jax.experimental.pallas import tpu as pltpu

F32 = jnp.float32
MMT = jnp.bfloat16

D_MODEL = 1024
BW = 512
S5_GROUP, S5_GROUPS, S5_STATE = 16, 32, 64
S5_N = S5_GROUPS * S5_STATE
HG_HEADS, HG_D = 4, 128
HG_CHUNK = 128
RG_BLOCKS, RG_BLOCK = 8, 64
RG_C = 8.0
CONV_W = 4
D_FF = 2816
EPS = 1e-6
IN_TOTAL = 6656
NSH = 4
NSEG = 8
LANE = 128
VMEM_LIMIT = 56 * 1024 * 1024
TM_FWD = 512
TM_WGRAD = 512

ADAM_LR, ADAM_B1, ADAM_B2, ADAM_EPS, ADAM_WD, ADAM_STEP = 0.001, 0.9, 0.999, 1e-08, 0.01, 10

MESH = pl.DeviceIdType.MESH


class WP(NamedTuple):
    w: jax.Array
    p: jax.Array


def _dg(a, b, ca, cb):
    return lax.dot_general(a, b, (((ca,), (cb,)), ((), ())), preferred_element_type=F32)


@jax.custom_vjp
def _mmw(a, w, p):
    return _dg(a.astype(MMT), w, 1, 0)


def _mmw_fwd(a, w, p):
    return _mmw(a, w, p), (a, w)


def _mmw_bwd(res, g):
    a, w = res
    gb = g.astype(MMT)
    return _dg(gb, w, 1, 1), jnp.zeros_like(w), _dg(a.astype(MMT), gb, 0, 0)


_mmw.defvjp(_mmw_fwd, _mmw_bwd)


def mm(a, w):
    if isinstance(w, WP):
        return _mmw(a, w.w, w.p)
    return _dg(a.astype(MMT), w, 1, 0)


@jax.custom_vjp
def mma_nn(a, b):
    return _dg(a.astype(MMT), b.astype(MMT), 1, 0)


def _nn_f(a, b):
    return mma_nn(a, b), (a, b)


def _nn_b(res, g):
    a, b = res
    gb = g.astype(MMT)
    return _dg(gb, b.astype(MMT), 1, 1), _dg(a.astype(MMT), gb, 0, 0)


mma_nn.defvjp(_nn_f, _nn_b)


@jax.custom_vjp
def mma_nt(a, b):
    return _dg(a.astype(MMT), b.astype(MMT), 1, 1)


def _nt_f(a, b):
    return mma_nt(a, b), (a, b)


def _nt_b(res, g):
    a, b = res
    gb = g.astype(MMT)
    return _dg(gb, b.astype(MMT), 1, 0), _dg(gb, a.astype(MMT), 0, 0)


mma_nt.defvjp(_nt_f, _nt_b)


@jax.custom_vjp
def mma_tn(a, b):
    return _dg(a.astype(MMT), b.astype(MMT), 0, 0)


def _tn_f(a, b):
    return mma_tn(a, b), (a, b)


def _tn_b(res, g):
    a, b = res
    gb = g.astype(MMT)
    return _dg(b.astype(MMT), gb, 1, 1), _dg(a.astype(MMT), gb, 1, 0)


mma_tn.defvjp(_tn_f, _tn_b)


def _split3(x):
    hi = x.astype(MMT)
    r = x - hi.astype(F32)
    mid = r.astype(MMT)
    return hi, mid, (r - mid.astype(F32)).astype(MMT)


@jax.custom_vjp
def mm_exact(m, x):
    mb = m.astype(MMT)
    hi, mid, lo = _split3(x)
    return (_dg(mb, hi, 1, 0) + _dg(mb, mid, 1, 0)) + _dg(mb, lo, 1, 0)


def _mm_exact_fwd(m, x):
    return mm_exact(m, x), m


def _mm_exact_bwd(m, g):
    mb = m.astype(MMT)
    hi, mid, lo = _split3(g)
    return jnp.zeros_like(m), (_dg(mb, hi, 0, 0) + _dg(mb, mid, 0, 0)) + _dg(mb, lo, 0, 0)


mm_exact.defvjp(_mm_exact_fwd, _mm_exact_bwd)


@jax.custom_vjp
def mm_exact_r(x, m):
    mb = m.astype(MMT)
    hi, mid, lo = _split3(x)
    return (_dg(hi, mb, 1, 0) + _dg(mid, mb, 1, 0)) + _dg(lo, mb, 1, 0)


def _mm_exact_r_fwd(x, m):
    return mm_exact_r(x, m), m


def _mm_exact_r_bwd(m, g):
    mb = m.astype(MMT)
    hi, mid, lo = _split3(g)
    return (_dg(hi, mb, 1, 1) + _dg(mid, mb, 1, 1)) + _dg(lo, mb, 1, 1), jnp.zeros_like(m)


mm_exact_r.defvjp(_mm_exact_r_fwd, _mm_exact_r_bwd)


def _rms(x, w):
    return x * lax.rsqrt(jnp.mean(x * x, axis=-1, keepdims=True) + EPS) * w


def _expm1(x):
    series = x * (1.0 + x * (1.0 / 2) * (1.0 + x * (1.0 / 3) * (1.0 + x * (1.0 / 4) * (1.0 + x * (1.0 / 5) * (1.0 + x * (1.0 / 6))))))
    return jnp.where(jnp.abs(x) < 0.1, series, jnp.exp(x) - 1.0)


def _bspec(block, fn, order):
    if order == "is":
        return pl.BlockSpec(block, lambda i, s: fn(s, i))
    return pl.BlockSpec(block, lambda s, i: fn(s, i))


def tile_fwd(fn, name, n_i, n_s, ins, outs, s_outer=False):
    n_in = len(ins)
    order = "si" if s_outer else "is"
    assert not (s_outer and any(o[4] for o in outs))

    def body(*refs):
        s = pl.program_id(0 if s_outer else 1)
        res = fn(*[r[...] for r in refs[:n_in]], s)
        for o_ref, val, spec in zip(refs[n_in:], res, outs):
            if spec[4] and n_s > 1:
                @pl.when(s == 0)
                def _(o_ref=o_ref, val=val):
                    o_ref[...] = val.astype(o_ref.dtype)

                @pl.when(s != 0)
                def _(o_ref=o_ref, val=val):
                    o_ref[...] += val.astype(o_ref.dtype)
            else:
                o_ref[...] = val.astype(o_ref.dtype)

    return pl.pallas_call(
        body, grid=(n_s, n_i) if s_outer else (n_i, n_s), name=name,
        in_specs=[_bspec(b, f, order) for _, b, f in ins],
        out_specs=[_bspec(b, f, order) for _, _, b, f, _ in outs],
        out_shape=[jax.ShapeDtypeStruct(sh, dt) for sh, dt, _, _, _ in outs],
        compiler_params=pltpu.CompilerParams(vmem_limit_bytes=VMEM_LIMIT,
                                             dimension_semantics=("arbitrary", "arbitrary")),
    )(*[a for a, _, _ in ins])


def tile_bwd(fn, name, n_i, n_s, ins, cts, gouts):
    groups = [c if isinstance(c, list) else [c] for c in cts]
    cts = [blk for grp in groups for blk in grp]
    n_in, n_ct = len(ins), len(cts)
    kinds = [k for _, _, _, k in ins]
    d_pos = [j for j, k in enumerate(kinds) if k != "c"]
    shared = [(gi, spec[4]) for gi, spec in enumerate(gouts) if len(spec) == 5 and spec[4] is not None]
    n_sh = len(shared)

    def body(*refs):
        s, i = pl.program_id(0), pl.program_id(1)
        vals = [r[...] for r in refs[:n_in]]
        ct_refs, ctv = list(refs[n_in:n_in + n_ct]), []
        for grp in groups:
            parts = [ct_refs.pop(0)[...] for _ in grp]
            ctv.append(parts[0] if len(parts) == 1 else jnp.concatenate(parts, axis=1))
        ctv = tuple(ctv)
        g_refs = refs[n_in + n_ct + n_sh:]

        def g(*dv):
            args = list(vals)
            for j, v in zip(d_pos, dv):
                args[j] = WP(vals[j], v) if kinds[j] == "w" else v
            return tuple(fn(*args))

        dv0 = [jnp.zeros(vals[j].shape, F32) if kinds[j] == "w" else vals[j] for j in d_pos]
        _, vjp = jax.vjp(g, *dv0)
        grads = vjp(ctv)
        for g_ref, gv, spec in zip(g_refs, grads, gouts):
            mode = spec[3]
            if mode == "write":
                g_ref[...] = gv.astype(g_ref.dtype)
            else:
                first = (i == 0) if mode == "acc_i" else jnp.logical_and(i == 0, s == 0)

                @pl.when(first)
                def _(g_ref=g_ref, gv=gv):
                    g_ref[...] = gv.astype(g_ref.dtype)

                @pl.when(jnp.logical_not(first))
                def _(g_ref=g_ref, gv=gv):
                    g_ref[...] += gv.astype(g_ref.dtype)

    return pl.pallas_call(
        body, grid=(n_s, n_i), name=name,
        in_specs=([_bspec(b, f, "si") for _, b, f, _ in ins] + [_bspec(b, f, "si") for _, b, f in cts]
                  + [pl.BlockSpec(memory_space=pl.ANY)] * n_sh),
        out_specs=[_bspec(spec[1], spec[2], "si") for spec in gouts],
        out_shape=[jax.ShapeDtypeStruct(spec[0], F32) for spec in gouts],
        input_output_aliases={n_in + n_ct + k: gi for k, (gi, _) in enumerate(shared)},
        compiler_params=pltpu.CompilerParams(vmem_limit_bytes=VMEM_LIMIT,
                                             dimension_semantics=("arbitrary", "arbitrary")),
    )(*[a for a, _, _, _ in ins], *[a for a, _, _ in cts], *[buf for _, buf in shared])


def _row_tile(rows, width, itemsize=4, budget=2 * 1024 * 1024, mult=8):
    best = mult
    for t in range(mult, rows + 1, mult):
        if rows % t == 0 and t * width * itemsize <= budget:
            best = t
    return best


def add_n(name, terms, shape):
    rows, cols = shape
    tr = _row_tile(rows, cols)

    def body(*refs):
        acc = refs[0][...]
        for r in refs[1:-1]:
            acc = acc + r[...]
        refs[-1][...] = acc

    specs = []
    for _, lead in terms:
        specs.append(pl.BlockSpec((None,) * len(lead) + (tr, cols), functools.partial(lambda i, lead: (*lead, i, 0), lead=lead)))
    return pl.pallas_call(
        body, grid=(rows // tr,), name=name, in_specs=specs,
        out_specs=pl.BlockSpec((tr, cols), lambda i: (i, 0)),
        out_shape=jax.ShapeDtypeStruct((rows, cols), F32),
    )(*[a for a, _ in terms])


def pre_core(x, nw, win):
    return (mm(_rms(x, nw), win),)


def _split_lanes(y):
    return jnp.stack([y[:, k * LANE:(k + 1) * LANE] for k in range(y.shape[1] // LANE)], axis=0)


def _join_lanes(y3):
    return jnp.concatenate([y3[k] for k in range(y3.shape[0])], axis=1)


def s5_pre_core(u, b_re0, b_re1, b_im0, b_im1):
    u0, u1 = u[:, :BW // 2], u[:, BW // 2:]
    re = jnp.concatenate([mm(u0, b_re0), mm(u1, b_re1)], axis=1)
    im = jnp.concatenate([mm(u0, b_im0), mm(u1, b_im1)], axis=1)
    return _split_lanes(re), _split_lanes(im)


def mid_core(xr, xi, u, o, g, hs, gc, hmat, c0, c1, d, gluw, glub, hgw):
    half = xr.shape[0] // 2
    xs0 = jnp.concatenate([_join_lanes(xr[:half]), _join_lanes(xi[:half])], axis=1)
    xs1 = jnp.concatenate([_join_lanes(xr[half:]), _join_lanes(xi[half:])], axis=1)
    y = jnp.concatenate([mm(xs0, c0), mm(xs1, c1)], axis=1) + d * u
    z = jax.nn.gelu(y)
    ya = z * jax.nn.sigmoid(mm(z, gluw) + glub)
    ms = mm_exact_r(o * o, hmat)
    yb = o * lax.rsqrt(ms + EPS) * hgw * jax.nn.silu(g)
    yc = hs * jax.nn.gelu(gc)
    return ya, yb, yc


def _sub(w, n):
    return WP(w.w[n], w.p[n]) if isinstance(w, WP) else w[n]


def merge_core(ya, yb, yc, g0, g1, g2, g3, g4, g5, p, wout):
    gate = lambda a, b: jax.nn.sigmoid(jnp.concatenate([a, b], axis=1))
    m = gate(g0, g1) * mm(ya, _sub(p, 0)) + gate(g2, g3) * mm(yb, _sub(p, 1)) + gate(g4, g5) * mm(yc, _sub(p, 2))
    return (mm(m, wout),)


def gates_core(xc, wa, ba, wx, bx, lam):
    r = jax.nn.sigmoid(mm(xc, wa) + ba)
    i = jax.nn.sigmoid(mm(xc, wx) + bx)
    log_a = -RG_C * jax.nn.softplus(-lam) * r
    a = jnp.exp(log_a)
    b = jnp.sqrt(-_expm1(2.0 * log_a)) * (i * xc)
    return a, b


def _seg_rows(ref, k, j, n):
    rows = pl.ds(pl.multiple_of(j * NSEG, NSEG), NSEG)
    if k is None:
        return ref[rows, :]
    return ref[k, rows, :]


def _seg_store(ref, k, j, n, val):
    rows = pl.ds(pl.multiple_of(j * NSEG, NSEG), NSEG)
    if k is None:
        ref[rows, :] = val
    else:
        ref[k, rows, :] = val


def _seg_carries(er, ei, pr, pi, reverse):
    rows = lax.broadcasted_iota(jnp.int32, er.shape, 0)
    cr = jnp.zeros_like(er)
    ci = None if ei is None else jnp.zeros_like(er)
    order = range(NSEG - 2, -1, -1) if reverse else range(1, NSEG)
    shift = NSEG - 1 if reverse else 1
    for s in order:
        if ei is None:
            tr = er + pr * cr
            cr = jnp.where(rows == s, pltpu.roll(tr, shift, 0), cr)
        else:
            tr = er + pr * cr - pi * ci
            ti = ei + pr * ci + pi * cr
            cr = jnp.where(rows == s, pltpu.roll(tr, shift, 0), cr)
            ci = jnp.where(rows == s, pltpu.roll(ti, shift, 0), ci)
    return cr, ci


def _cpow(ar, ai, n):
    out = None
    while n:
        if n & 1:
            out = (ar, ai) if out is None else (out[0] * ar - out[1] * ai, out[0] * ai + out[1] * ar)
        ar, ai = ar * ar - ai * ai, 2.0 * ar * ai
        n >>= 1
    return out


S5_K = 2


def s5_scan_fwd(bur, bui, ar, ai, L):
    n = L // NSEG
    nb = S5_N // LANE
    K = S5_K

    def body(br_ref, bi_ref, ar_ref, ai_ref, xr_ref, xi_ref):
        zero = jnp.zeros((NSEG, LANE), F32)
        A = [(jnp.broadcast_to(ar_ref[k], (NSEG, LANE)), jnp.broadcast_to(ai_ref[k], (NSEG, LANE))) for k in range(K)]

        def p1(j, st):
            new = []
            for k in range(K):
                sr, si = st[k]
                a_r, a_i = A[k]
                nr = a_r * sr - a_i * si + _seg_rows(br_ref, k, j, n)
                ni = a_r * si + a_i * sr + _seg_rows(bi_ref, k, j, n)
                _seg_store(xr_ref, k, j, n, nr)
                _seg_store(xi_ref, k, j, n, ni)
                new.append((nr, ni))
            return tuple(new)

        st = lax.fori_loop(0, n, p1, tuple((zero, zero) for _ in range(K)))
        C = [_seg_carries(st[k][0], st[k][1], *_cpow(*A[k], n), False) for k in range(K)]

        def p2(j, st):
            new = []
            for k in range(K):
                pr, pi = st[k]
                a_r, a_i = A[k]
                pr, pi = a_r * pr - a_i * pi, a_r * pi + a_i * pr
                cr, ci = C[k]
                _seg_store(xr_ref, k, j, n, _seg_rows(xr_ref, k, j, n) + pr * cr - pi * ci)
                _seg_store(xi_ref, k, j, n, _seg_rows(xi_ref, k, j, n) + pr * ci + pi * cr)
                new.append((pr, pi))
            return tuple(new)

        lax.fori_loop(0, n, p2, tuple((zero + 1.0, zero) for _ in range(K)))

    blk = pl.BlockSpec((K, L, LANE), lambda g: (g, 0, 0))
    ablk = pl.BlockSpec((K, 1, LANE), lambda g: (g, 0, 0))
    return pl.pallas_call(
        body, grid=(nb // K,), name="s5_scan_fwd",
        in_specs=[blk, blk, ablk, ablk], out_specs=[blk, blk],
        out_shape=[jax.ShapeDtypeStruct((nb, L, LANE), F32)] * 2,
        compiler_params=pltpu.CompilerParams(vmem_limit_bytes=VMEM_LIMIT),
    )(bur, bui, ar, ai)


def s5_scan_bwd(dxr, dxi, xr, xi, ar, ai, L):
    n = L // NSEG
    nb = S5_N // LANE
    K = S5_K

    def body(dr_ref, di_ref, xr_ref, xi_ref, ar_ref, ai_ref, gr_ref, gi_ref, dar_ref, dai_ref):
        zero = jnp.zeros((NSEG, LANE), F32)
        rows = lax.broadcasted_iota(jnp.int32, (NSEG, LANE), 0)
        A = [(jnp.broadcast_to(ar_ref[k], (NSEG, LANE)), -jnp.broadcast_to(ai_ref[k], (NSEG, LANE))) for k in range(K)]

        def p1(jj, st):
            j = n - 1 - jj
            new = []
            for k in range(K):
                sr, si = st[k]
                a_r, a_i = A[k]
                nr = a_r * sr - a_i * si + _seg_rows(dr_ref, k, j, n)
                ni = a_r * si + a_i * sr + _seg_rows(di_ref, k, j, n)
                _seg_store(gr_ref, k, j, n, nr)
                _seg_store(gi_ref, k, j, n, ni)
                new.append((nr, ni))
            return tuple(new)

        st = lax.fori_loop(0, n, p1, tuple((zero, zero) for _ in range(K)))
        C = [_seg_carries(st[k][0], st[k][1], *_cpow(*A[k], n), True) for k in range(K)]
        xb = [(jnp.where(rows == 0, 0.0, pltpu.roll(_seg_rows(xr_ref, k, n - 1, n), 1, 0)),
               jnp.where(rows == 0, 0.0, pltpu.roll(_seg_rows(xi_ref, k, n - 1, n), 1, 0))) for k in range(K)]

        def p2(jj, st):
            j = n - 1 - jj
            jp = jnp.maximum(j - 1, 0)
            new = []
            for k in range(K):
                pr, pi, acr, aci = st[k]
                a_r, a_i = A[k]
                pr, pi = a_r * pr - a_i * pi, a_r * pi + a_i * pr
                cr, ci = C[k]
                g_r = _seg_rows(gr_ref, k, j, n) + pr * cr - pi * ci
                g_i = _seg_rows(gi_ref, k, j, n) + pr * ci + pi * cr
                _seg_store(gr_ref, k, j, n, g_r)
                _seg_store(gi_ref, k, j, n, g_i)
                xpr = jnp.where(j == 0, xb[k][0], _seg_rows(xr_ref, k, jp, n))
                xpi = jnp.where(j == 0, xb[k][1], _seg_rows(xi_ref, k, jp, n))
                new.append((pr, pi, acr + g_r * xpr + g_i * xpi, aci + g_i * xpr - g_r * xpi))
            return tuple(new)

        st = lax.fori_loop(0, n, p2, tuple((zero + 1.0, zero, zero, zero) for _ in range(K)))
        for k in range(K):
            dar_ref[k] = jnp.sum(st[k][2], axis=0, keepdims=True)
            dai_ref[k] = jnp.sum(st[k][3], axis=0, keepdims=True)

    blk = pl.BlockSpec((K, L, LANE), lambda g: (g, 0, 0))
    ablk = pl.BlockSpec((K, 1, LANE), lambda g: (g, 0, 0))
    return pl.pallas_call(
        body, grid=(nb // K,), name="s5_scan_bwd",
        in_specs=[blk, blk, blk, blk, ablk, ablk], out_specs=[blk, blk, ablk, ablk],
        out_shape=[jax.ShapeDtypeStruct((nb, L, LANE), F32)] * 2 + [jax.ShapeDtypeStruct((nb, 1, LANE), F32)] * 2,
        compiler_params=pltpu.CompilerParams(vmem_limit_bytes=VMEM_LIMIT),
    )(dxr, dxi, xr, xi, ar, ai)


def rg_scan_fwd(a, b, L):
    n = L // NSEG

    def body(a_ref, b_ref, h_ref):
        zero = jnp.zeros((NSEG, LANE), F32)

        def p1(j, st):
            h, p = st
            aj = _seg_rows(a_ref, None, j, n)
            h = aj * h + _seg_rows(b_ref, None, j, n)
            _seg_store(h_ref, None, j, n, h)
            return h, aj * p

        e, pe = lax.fori_loop(0, n, p1, (zero, zero + 1.0))
        c, _ = _seg_carries(e, None, pe, None, False)

        def p2(j, p):
            p = _seg_rows(a_ref, None, j, n) * p
            _seg_store(h_ref, None, j, n, _seg_rows(h_ref, None, j, n) + p * c)
            return p

        lax.fori_loop(0, n, p2, zero + 1.0)

    blk = pl.BlockSpec((L, LANE), lambda g: (0, g))
    return pl.pallas_call(
        body, grid=(BW // LANE,), name="rg_scan_fwd", in_specs=[blk, blk], out_specs=blk,
        out_shape=jax.ShapeDtypeStruct((L, BW), F32),
        compiler_params=pltpu.CompilerParams(vmem_limit_bytes=VMEM_LIMIT),
    )(a, b)


def rg_scan_bwd(a, h, dh, L):
    n = L // NSEG

    def body(a_ref, h_ref, dh_ref, da_ref, db_ref):
        zero = jnp.zeros((NSEG, LANE), F32)
        rows = lax.broadcasted_iota(jnp.int32, (NSEG, LANE), 0)
        a_edge = jnp.where(rows == NSEG - 1, 0.0, pltpu.roll(_seg_rows(a_ref, None, 0, n), NSEG - 1, 0))
        h_edge = jnp.where(rows == 0, 0.0, pltpu.roll(_seg_rows(h_ref, None, n - 1, n), 1, 0))

        def mult(j):
            return jnp.where(j == n - 1, a_edge, _seg_rows(a_ref, None, jnp.minimum(j + 1, n - 1), n))

        def p1(jj, st):
            j = n - 1 - jj
            g, p = st
            m = mult(j)
            g = m * g + _seg_rows(dh_ref, None, j, n)
            _seg_store(db_ref, None, j, n, g)
            return g, m * p

        e, pe = lax.fori_loop(0, n, p1, (zero, zero + 1.0))
        c, _ = _seg_carries(e, None, pe, None, True)

        def p2(jj, p):
            j = n - 1 - jj
            p = mult(j) * p
            g = _seg_rows(db_ref, None, j, n) + p * c
            _seg_store(db_ref, None, j, n, g)
            hp = jnp.where(j == 0, h_edge, _seg_rows(h_ref, None, jnp.maximum(j - 1, 0), n))
            _seg_store(da_ref, None, j, n, g * hp)
            return p

        lax.fori_loop(0, n, p2, zero + 1.0)

    blk = pl.BlockSpec((L, LANE), lambda g: (0, g))
    return pl.pallas_call(
        body, grid=(BW // LANE,), name="rg_scan_bwd", in_specs=[blk, blk, blk], out_specs=[blk, blk],
        out_shape=[jax.ShapeDtypeStruct((L, BW), F32)] * 2,
        compiler_params=pltpu.CompilerParams(vmem_limit_bytes=VMEM_LIMIT),
    )(a, h, dh)


def _hg_consts(C):
    t = lax.broadcasted_iota(jnp.int32, (C, C), 0)
    s = lax.broadcasted_iota(jnp.int32, (C, C), 1)
    tril = (s <= t).astype(F32)
    diag = (s == t).astype(F32)
    levels = []
    k = 1
    while (1 << k) <= C:
        m = 1 << (k - 1)
        same = (t >> k) == (s >> k)
        t_right = ((t >> (k - 1)) & 1) == 1
        s_left = ((s >> (k - 1)) & 1) == 0
        mask = jnp.logical_and(same, jnp.logical_and(t_right, s_left)).astype(F32)
        bnd = ((t >> k) << k) + (m - 1)
        levels.append((mask, (s <= bnd).astype(F32)))
        k += 1
    return tril, diag, levels


def hg_chunk(st, q, z, v, lb):
    C = q.shape[0]
    tril, diag, levels = _hg_consts(C)
    sig = jax.nn.sigmoid(z)
    lf = jnp.log(lb + (1.0 - lb) * sig)
    k = (1.0 - lb) * jax.nn.sigmoid(-z)
    qh = jax.nn.silu(q)
    b = mm_exact(tril, lf)
    blast = jnp.sum(lf, axis=0, keepdims=True)
    qe = qh * jnp.exp(b)
    kd = k * jnp.exp(blast - b)
    scaled = []
    for level, (_, sel) in enumerate(levels):
        size = 2 << level
        if size >= NSEG:
            b3 = b.reshape(C // size, size, b.shape[1])
            ref = jnp.broadcast_to(b3[:, size // 2 - 1:size // 2, :], b3.shape).reshape(b.shape)
        else:
            ref = mm_exact(sel, lf)
        scaled.append((qh * jnp.exp(jnp.minimum(b - ref, 0.0)), k * jnp.exp(jnp.minimum(ref - b, 0.0))))
    outs, news = [], []
    for h in range(HG_HEADS):
        sl = slice(h * HG_D, (h + 1) * HG_D)
        st_h = st[h * HG_D:(h + 1) * HG_D, :]
        sc = diag * mma_nt(qh[:, sl], k[:, sl])
        for (mask, _), (qt, kt) in zip(levels, scaled):
            sc = sc + mask * mma_nt(qt[:, sl], kt[:, sl])
        outs.append(mma_nt(qe[:, sl], st_h) + mma_nn(sc, v[:, sl]))
        news.append(st_h * jnp.exp(blast[:, sl]) + mma_tn(v[:, sl], kd[:, sl]))
    return jnp.concatenate(news, axis=0), jnp.concatenate(outs, axis=1)


def hg_fwd(qzv, lb, L):
    C = HG_CHUNK
    nc = L // C

    def body(q_ref, z_ref, v_ref, lb_ref, o_ref, sst_ref, st_ref):
        @pl.when(pl.program_id(0) == 0)
        def _():
            st_ref[...] = jnp.zeros_like(st_ref)

        st = st_ref[...]
        sst_ref[...] = st
        new, o = hg_chunk(st, q_ref[...], z_ref[...], v_ref[...], lb_ref[...])
        st_ref[...] = new
        o_ref[...] = o

    col = lambda cb: pl.BlockSpec((C, BW), functools.partial(lambda c, cb: (c, cb), cb=cb))
    return pl.pallas_call(
        body, grid=(nc,), name="hg_fwd",
        in_specs=[col(0), col(1), col(2), pl.BlockSpec((1, BW), lambda c: (0, 0))],
        out_specs=[pl.BlockSpec((C, BW), lambda c: (c, 0)), pl.BlockSpec((None, BW, HG_D), lambda c: (c, 0, 0))],
        out_shape=[jax.ShapeDtypeStruct((L, BW), F32), jax.ShapeDtypeStruct((nc, BW, HG_D), F32)],
        scratch_shapes=[pltpu.VMEM((BW, HG_D), F32)],
        compiler_params=pltpu.CompilerParams(vmem_limit_bytes=VMEM_LIMIT, dimension_semantics=("arbitrary",)),
    )(qzv, qzv, qzv, lb)


def hg_bwd(qzv, lb, sst, do, L):
    C = HG_CHUNK
    nc = L // C

    def body(q_ref, z_ref, v_ref, lb_ref, sst_ref, do_ref, dq_ref, dz_ref, dv_ref, dlb_ref, dst_ref):
        @pl.when(pl.program_id(0) == 0)
        def _():
            dst_ref[...] = jnp.zeros_like(dst_ref)
            dlb_ref[...] = jnp.zeros_like(dlb_ref)

        _, vjp = jax.vjp(hg_chunk, sst_ref[...], q_ref[...], z_ref[...], v_ref[...], lb_ref[...])
        dst, dq, dz, dv, dlb = vjp((dst_ref[...], do_ref[...]))
        dst_ref[...] = dst
        dq_ref[...] = dq
        dz_ref[...] = dz
        dv_ref[...] = dv
        dlb_ref[...] += dlb

    col = lambda cb: pl.BlockSpec((C, BW), functools.partial(lambda c, cb: (nc - 1 - c, cb), cb=cb))
    rev = pl.BlockSpec((C, BW), lambda c: (nc - 1 - c, 0))
    return pl.pallas_call(
        body, grid=(nc,), name="hg_bwd",
        in_specs=[col(0), col(1), col(2), pl.BlockSpec((1, BW), lambda c: (0, 0)),
                  pl.BlockSpec((None, BW, HG_D), lambda c: (nc - 1 - c, 0, 0)), rev],
        out_specs=[rev, rev, rev, pl.BlockSpec((1, BW), lambda c: (0, 0))],
        out_shape=[jax.ShapeDtypeStruct((L, BW), F32)] * 3 + [jax.ShapeDtypeStruct((1, BW), F32)],
        scratch_shapes=[pltpu.VMEM((BW, HG_D), F32)],
        compiler_params=pltpu.CompilerParams(vmem_limit_bytes=VMEM_LIMIT, dimension_semantics=("arbitrary",)),
    )(qzv, qzv, qzv, lb, sst, do)


def _shift_down(x, d, rows, L):
    if d == 0:
        return x
    wrapped = jnp.where((rows & (NSEG - 1)) == 0, 0.0, pltpu.roll(x, NSEG * d + 1, 0))
    return jnp.where(rows < NSEG * d, wrapped, pltpu.roll(x, NSEG * d, 0))


def _shift_up(x, d, rows, L):
    if d == 0:
        return x
    wrapped = jnp.where((rows & (NSEG - 1)) == NSEG - 1, 0.0, pltpu.roll(x, L - (NSEG * d + 1), 0))
    return jnp.where(rows >= L - NSEG * d, wrapped, pltpu.roll(x, L - NSEG * d, 0))


def conv_fwd(proj, w, b, L):
    def body(x_ref, w_ref, b_ref, o_ref):
        x = x_ref[...]
        rows = lax.broadcasted_iota(jnp.int32, x.shape, 0)
        acc = jnp.broadcast_to(b_ref[...], x.shape)
        for k in range(CONV_W):
            acc = acc + w_ref[pl.ds(k, 1), :] * _shift_down(x, CONV_W - 1 - k, rows, L)
        o_ref[...] = acc

    nl = BW // LANE
    return pl.pallas_call(
        body, grid=(nl,), name="conv_fwd",
        in_specs=[pl.BlockSpec((L, LANE), lambda g: (0, 5 * nl + g)), pl.BlockSpec((CONV_W, LANE), lambda g: (0, g)),
                  pl.BlockSpec((1, LANE), lambda g: (0, g))],
        out_specs=pl.BlockSpec((L, LANE), lambda g: (0, g)),
        out_shape=jax.ShapeDtypeStruct((L, BW), F32),
        compiler_params=pltpu.CompilerParams(vmem_limit_bytes=VMEM_LIMIT),
    )(proj, w, b)


def conv_bwd(proj, w, dxc, L):
    def body(x_ref, w_ref, d_ref, dx_ref, dw_ref, db_ref):
        x, d = x_ref[...], d_ref[...]
        rows = lax.broadcasted_iota(jnp.int32, x.shape, 0)
        acc = jnp.zeros_like(x)
        for k in range(CONV_W):
            acc = acc + w_ref[pl.ds(k, 1), :] * _shift_up(d, CONV_W - 1 - k, rows, L)
            dw_ref[pl.ds(k, 1), :] = jnp.sum(d * _shift_down(x, CONV_W - 1 - k, rows, L), axis=0, keepdims=True)
        dx_ref[...] = acc
        db_ref[...] = jnp.sum(d, axis=0, keepdims=True)

    nl = BW // LANE
    blk = pl.BlockSpec((L, LANE), lambda g: (0, g))
    return pl.pallas_call(
        body, grid=(nl,), name="conv_bwd",
        in_specs=[pl.BlockSpec((L, LANE), lambda g: (0, 5 * nl + g)), pl.BlockSpec((CONV_W, LANE), lambda g: (0, g)), blk],
        out_specs=[blk, pl.BlockSpec((CONV_W, LANE), lambda g: (0, g)), pl.BlockSpec((1, LANE), lambda g: (0, g))],
        out_shape=[jax.ShapeDtypeStruct((L, BW), F32), jax.ShapeDtypeStruct((CONV_W, BW), F32),
                   jax.ShapeDtypeStruct((1, BW), F32)],
        compiler_params=pltpu.CompilerParams(vmem_limit_bytes=VMEM_LIMIT),
    )(proj, w, dxc)


def loss_fwd_bwd(x, fw, target, L, tm):
    def fn(x, fw, t):
        err = jnp.square(_rms(x, fw) - t)
        return jnp.sum(0.5 * jnp.mean(err, axis=-1, keepdims=True), axis=0, keepdims=True)

    def body(x_ref, fw_ref, t_ref, l_ref, dx_ref, dfw_ref):
        i = pl.program_id(0)
        t = t_ref[...]
        val, vjp = jax.vjp(lambda x, fw: fn(x, fw, t), x_ref[...], fw_ref[...])
        dx, dfw = vjp(jnp.ones((1, 1), F32))
        dx_ref[...] = dx

        @pl.when(i == 0)
        def _():
            l_ref[...] = jnp.zeros_like(l_ref)
            dfw_ref[...] = jnp.zeros_like(dfw_ref)

        l_ref[...] += jnp.broadcast_to(val, l_ref.shape)
        dfw_ref[...] += dfw

    row = pl.BlockSpec((tm, D_MODEL), lambda i: (i, 0))
    vec = pl.BlockSpec((1, D_MODEL), lambda i: (0, 0))
    return pl.pallas_call(
        body, grid=(L // tm,), name="loss_fwd_bwd", in_specs=[row, vec, row],
        out_specs=[pl.BlockSpec((1, LANE), lambda i: (0, 0)), row, vec],
        out_shape=[jax.ShapeDtypeStruct((1, LANE), F32), jax.ShapeDtypeStruct((L, D_MODEL), F32),
                   jax.ShapeDtypeStruct((1, D_MODEL), F32)],
        compiler_params=pltpu.CompilerParams(vmem_limit_bytes=VMEM_LIMIT, dimension_semantics=("arbitrary",)),
    )(x, fw, target)


def adamw(w, g, m, v):
    rows, cols = w.shape
    tr = _row_tile(rows, cols, budget=1024 * 1024)
    c1 = 1.0 - ADAM_B1 ** ADAM_STEP
    c2 = 1.0 - ADAM_B2 ** ADAM_STEP

    def body(w_ref, g_ref, m_ref, v_ref, d_ref, nm_ref, nv_ref):
        g = g_ref[...]
        nm = ADAM_B1 * m_ref[...] + (1.0 - ADAM_B1) * g
        nv = ADAM_B2 * v_ref[...] + (1.0 - ADAM_B2) * jnp.square(g)
        d_ref[...] = -ADAM_LR * ((nm / c1) / (jnp.sqrt(nv / c2) + ADAM_EPS) + ADAM_WD * w_ref[...])
        nm_ref[...] = nm
        nv_ref[...] = nv

    blk = pl.BlockSpec((tr, cols), lambda i: (i, 0))
    return pl.pallas_call(
        body, grid=(rows // tr,), name="adamw", in_specs=[blk] * 4, out_specs=[blk] * 3,
        out_shape=[jax.ShapeDtypeStruct((rows, cols), F32)] * 3,
    )(w, g, m, v)


def s5_prep(lam_re, lam_im, log_dt, b_re, b_im, c_re, c_im):
    lr = jnp.minimum(lam_re, -1e-4)
    li = lam_im
    dt = jnp.exp(log_dt)[:, None]
    mag = jnp.exp(lr * dt)
    ar = mag * jnp.cos(li * dt)
    ai = mag * jnp.sin(li * dt)
    den = lr * lr + li * li
    fr = ((ar - 1.0) * lr + ai * li) / den
    fi = (ai * lr - (ar - 1.0) * li) / den
    bbr = fr[..., None] * b_re - fi[..., None] * b_im
    bbi = fr[..., None] * b_im + fi[..., None] * b_re
    hg = S5_GROUPS // 2
    emb_b = lambda bb: _block_diag(bb.transpose(0, 2, 1).reshape(hg * S5_GROUP, S5_STATE), hg)
    emb_c = lambda cc: _block_diag(cc.transpose(0, 2, 1).reshape(hg * S5_STATE, S5_GROUP), hg)
    bsub = jnp.stack([emb_b(bbr[:hg]), emb_b(bbr[hg:]), emb_b(bbi[:hg]), emb_b(bbi[hg:])])
    csub = jnp.stack([jnp.concatenate([emb_c(c_re[:hg]), -emb_c(c_im[:hg])], axis=0),
                      jnp.concatenate([emb_c(c_re[hg:]), -emb_c(c_im[hg:])], axis=0)])
    nb = S5_N // LANE
    return ar.reshape(nb, 1, LANE), ai.reshape(nb, 1, LANE), bsub, csub


def _block_diag(stacked, groups):
    rows, c = stacked.shape
    r = rows // groups
    row_g = jnp.arange(rows)[:, None] // r
    col_g = jnp.arange(groups * c)[None, :] // c
    return jnp.where(row_g == col_g, jnp.tile(stacked, (1, groups)), 0.0)


def rg_prep(w):
    return _block_diag(w.reshape(BW, RG_BLOCK), RG_BLOCKS)


def hg_prep(logits):
    p = jax.nn.softmax(logits, axis=0)
    return jnp.cumsum(p, axis=0) - p[0]


def _head_mean_matrix():
    r = jnp.arange(BW) // HG_D
    return (r[:, None] == r[None, :]).astype(F32) / HG_D


def _to_segment_order(a):
    L = a.shape[0]
    return a.reshape(NSEG, L // NSEG, -1).transpose(1, 0, 2).reshape(a.shape)


def _to_time_order(a):
    L = a.shape[0]
    return a.reshape(L // NSEG, NSEG, -1).transpose(1, 0, 2).reshape(a.shape)


def _const(*idx):
    return lambda s, i: idx


def _rows(cb=0):
    return lambda s, i: (i, cb)


def _sum_parts(name, first, parts, shape):
    return add_n(name, [(first, ())] + [(parts, (s,)) for s in range(NSH)], shape)


def _ffn_weight_specs(l, j):
    F = D_FF // NSH
    one = pl.Buffered(1)
    return [pl.BlockSpec((None, NSH, D_MODEL, F), lambda i: (j, 0, 0, 0), pipeline_mode=one),
            pl.BlockSpec((None, NSH, D_MODEL, F), lambda i: (j, 0, 0, 0), pipeline_mode=one),
            pl.BlockSpec((None, NSH, F, D_MODEL), lambda i: (j, 0, 0, 0), pipeline_mode=one)]


def ffn_fwd(name, x, W, l, j, k, L, tm):
    D, F = D_MODEL, D_FF // NSH

    def body(x_ref, nw_ref, wg_ref, wu_ref, wd_ref, y_ref, g_ref, u_ref):
        x = x_ref[...]
        h = _rms(x, nw_ref[...]).astype(MMT)
        y = x
        for s in range(NSH):
            g = _dg(h, wg_ref[s], 1, 0)
            u = _dg(h, wu_ref[s], 1, 0)
            g_ref[s] = g.astype(g_ref.dtype)
            u_ref[s] = u.astype(u_ref.dtype)
            y = y + 0.5 * _dg((jax.nn.silu(g) * u).astype(MMT), wd_ref[s], 1, 0)
        y_ref[...] = y

    row = pl.BlockSpec((tm, D), lambda i: (i, 0))
    act = pl.BlockSpec((NSH, tm, F), lambda i: (0, i, 0))
    return pl.pallas_call(
        body, grid=(L // tm,), name=name,
        in_specs=[row, pl.BlockSpec((None, None, 1, D), lambda i: (l, k, 0, 0))] + _ffn_weight_specs(l, j),
        out_specs=[row, act, act],
        out_shape=[jax.ShapeDtypeStruct((L, D), F32), jax.ShapeDtypeStruct((NSH, L, F), MMT),
                   jax.ShapeDtypeStruct((NSH, L, F), MMT)],
        compiler_params=pltpu.CompilerParams(vmem_limit_bytes=VMEM_LIMIT, dimension_semantics=("arbitrary",)),
    )(x, W["nw"], W["L"][l]["wg"], W["L"][l]["wu"], W["L"][l]["wd"])


def ffn_bwd(name, x, g, u, dy, W, bufs, l, j, k, L, tm):
    D, F = D_MODEL, D_FF // NSH
    tm = min(TM_WGRAD, L)

    def body(x_ref, nw_ref, dy_ref, g_ref, u_ref, wg_ref, wu_ref, wd_ref, *rest):
        part_ref, dnw_ref, dwg_ref, dwu_ref, dwd_ref = rest[-5:]
        s, i = pl.program_id(0), pl.program_id(1)
        x, nw = x_ref[...], nw_ref[...]
        r = lax.rsqrt(jnp.mean(x * x, axis=-1, keepdims=True) + EPS)
        xhat = x * r
        h = (xhat * nw).astype(MMT)
        half_dy = (0.5 * dy_ref[...]).astype(MMT)
        gs, us = g_ref[...].astype(F32), u_ref[...].astype(F32)
        sig = jax.nn.sigmoid(gs)
        act = gs * sig
        da = _dg(half_dy, wd_ref[...], 1, 1)
        du = (da * act).astype(MMT)
        dg = (da * us * (sig * (1.0 + gs * (1.0 - sig)))).astype(MMT)
        dh = _dg(dg, wg_ref[...], 1, 1) + _dg(du, wu_ref[...], 1, 1)
        dxh = dh * nw
        part_ref[...] = r * (dxh - xhat * jnp.mean(dxh * xhat, axis=-1, keepdims=True))
        grads = (_dg(h, dg, 0, 0), _dg(h, du, 0, 0), _dg((act * us).astype(MMT), half_dy, 0, 0))
        dnw = jnp.sum(dh * xhat, axis=0, keepdims=True)
        first = jnp.logical_and(s == 0, i == 0)
        for ref, val, start in zip((dwg_ref, dwu_ref, dwd_ref, dnw_ref), grads + (dnw,), (i == 0, i == 0, i == 0, first)):
            @pl.when(start)
            def _(ref=ref, val=val):
                ref[...] = val

            @pl.when(jnp.logical_not(start))
            def _(ref=ref, val=val):
                ref[...] += val

    row = pl.BlockSpec((tm, D), lambda s, i: (i, 0))
    act = pl.BlockSpec((None, tm, F), lambda s, i: (s, i, 0))
    wsp = lambda r, c: pl.BlockSpec((None, None, r, c), lambda s, i: (j, s, 0, 0))
    gsp = lambda r, c: pl.BlockSpec((None, None, r, c), lambda s, i: (0, s, 0, 0))
    part, dnw, bufs[("ffn_gate", l, j)], bufs[("ffn_up", l, j)], bufs[("ffn_down", l, j)] = pl.pallas_call(
        body, grid=(NSH, L // tm), name=name,
        in_specs=[row, pl.BlockSpec((None, None, 1, D), lambda s, i: (l, k, 0, 0)), row, act, act,
                  wsp(D, F), wsp(D, F), wsp(F, D)],
        out_specs=[pl.BlockSpec((None, tm, D), lambda s, i: (s, i, 0)), pl.BlockSpec((1, D), lambda s, i: (0, 0)),
                   gsp(D, F), gsp(D, F), gsp(F, D)],
        out_shape=[jax.ShapeDtypeStruct((NSH, L, D), F32), jax.ShapeDtypeStruct((1, D), F32)]
        + [jax.ShapeDtypeStruct((1, NSH, D, F), F32)] * 2 + [jax.ShapeDtypeStruct((1, NSH, F, D), F32)],
        compiler_params=pltpu.CompilerParams(vmem_limit_bytes=VMEM_LIMIT, dimension_semantics=("arbitrary", "arbitrary")),
    )(x, W["nw"], dy, g, u, W["L"][l]["wg"], W["L"][l]["wu"], W["L"][l]["wd"])
    return _sum_parts(name + "_dx", dy, part, (L, D)), dnw


def layer_fwd(l, x0, W, P, L, tm):
    D = D_MODEL
    tmm = tm
    tm = min(TM_FWD, L)
    n_i = L // tm
    x1, g0, u0 = ffn_fwd(f"ffn_fwd_{l}0", x0, W, l, 0, 0, L, tm)
    proj = tile_fwd(
        lambda x, nw, win, s: pre_core(x, nw, win), f"pre_fwd_{l}", n_i, NSH,
        [(x1, (tm, D), _rows()), (W["nw"], (None, None, 1, D), _const(l, 1, 0, 0)),
         (W["L"][l]["win"], (None, D, IN_TOTAL // NSH), lambda s, i: (s, 0, 0))],
        [((L, IN_TOTAL), F32, (tm, IN_TOTAL // NSH), lambda s, i: (i, s), False)], s_outer=True)[0]
    nb = S5_N // LANE
    blk3 = lambda s, i: (0, i, 0)
    bur, bui = tile_fwd(
        lambda *a: s5_pre_core(*a[:-1]), f"s5pre_fwd_{l}", n_i, 1,
        [(proj, (tm, BW), _rows(0))] + [(P["bsub"], (None, None, BW // 2, S5_N // 2), _const(l, q, 0, 0)) for q in range(4)],
        [((nb, L, LANE), F32, (nb, tm, LANE), blk3, False)] * 2)
    xr, xi = s5_scan_fwd(bur, bui, P["ar"][l], P["ai"][l], L)
    qzv = _to_time_order(proj[:, BW:4 * BW])
    o_t, sst = hg_fwd(qzv, P["lb"][l], L)
    o = _to_segment_order(o_t)
    xc = conv_fwd(proj, W["convw"][l], P["convb"][l], L)
    vec = (None, 1, BW)
    a, b = tile_fwd(
        lambda xc, wa, ba, wx, bx, lam, s: gates_core(xc, wa, ba, wx, bx, lam), f"gates_fwd_{l}", n_i, 1,
        [(xc, (tm, BW), _rows()), (P["wa"], (None, BW, BW), _const(l, 0, 0)), (P["ba"], vec, _const(l, 0, 0)),
         (P["wx"], (None, BW, BW), _const(l, 0, 0)), (P["bx"], vec, _const(l, 0, 0)), (P["lam"], vec, _const(l, 0, 0))],
        [((L, BW), F32, (tm, BW), _rows(), False)] * 2)
    hs = rg_scan_fwd(a, b, L)
    ya, yb, yc = tile_fwd(
        lambda *a: mid_core(*a[:-1]), f"mid_fwd_{l}", L // tmm, 1,
        [(xr, (nb, tmm, LANE), blk3), (xi, (nb, tmm, LANE), blk3), (proj, (tmm, BW), _rows(0)), (o, (tmm, BW), _rows()),
         (proj, (tmm, BW), _rows(4)), (hs, (tmm, BW), _rows()), (proj, (tmm, BW), _rows(6)),
         (P["hmat"], (BW, BW), _const(0, 0)), (P["csub"], (None, None, S5_N, BW // 2), _const(l, 0, 0, 0)),
         (P["csub"], (None, None, S5_N, BW // 2), _const(l, 1, 0, 0)), (P["d"], vec, _const(l, 0, 0)),
         (W["L"][l]["gluw"], (BW, BW), _const(0, 0)), (P["glub"], vec, _const(l, 0, 0)), (P["hgw"], vec, _const(l, 0, 0))],
        [((L, BW), F32, (tmm, BW), _rows(), False)] * 3)
    x2 = tile_fwd(
        lambda x, *rest: (x + merge_core(*rest[:-1])[0],), f"merge_fwd_{l}", n_i, 1,
        [(x1, (tm, D), _rows()), (ya, (tm, BW), _rows()), (yb, (tm, BW), _rows()), (yc, (tm, BW), _rows())]
        + [(proj, (tm, BW), _rows(7 + k)) for k in range(6)]
        + [(W["L"][l]["pfull"], (3, BW, D), _const(0, 0, 0)), (W["L"][l]["woutfull"], (D, D), _const(0, 0))],
        [((L, D), F32, (tm, D), _rows(), False)])[0]
    x3, g1, u1 = ffn_fwd(f"ffn_fwd_{l}1", x2, W, l, 1, 2, L, tm)
    saved = dict(x0=x0, x1=x1, x2=x2, proj=proj, xr=xr, xi=xi, o=o, sst=sst, xc=xc, a=a, hs=hs, ya=ya, yb=yb, yc=yc,
                 qzv=qzv, g0=g0, u0=u0, g1=g1, u1=u1)
    return x3, saved


def layer_bwd(l, dx3, sv, W, P, bufs, L, tm, ready=lambda l, group: None):
    D = D_MODEL
    n_i = L // tm
    nb = S5_N // LANE
    dq = D // NSH
    vec = (None, 1, BW)
    vout = ((1, BW), (1, BW), _const(0, 0), "acc_all")
    blk3 = lambda s, i: (0, i, 0)
    small = {}
    proj = sv["proj"]

    dx2, dnw2 = ffn_bwd(f"ffn_bwd_{l}1", sv["x2"], sv["g1"], sv["u1"], dx3, W, bufs, l, 1, 2, L, tm)
    ready(l, "ffn1")

    rw256 = ((L, BW), (tm, BW), _rows(), "write")
    res = tile_bwd(
        merge_core, f"merge_bwd_{l}", n_i, 1,
        [(sv["ya"], (tm, BW), _rows(), "r"), (sv["yb"], (tm, BW), _rows(), "r"), (sv["yc"], (tm, BW), _rows(), "r")]
        + [(proj, (tm, BW), _rows(7 + k), "r") for k in range(6)]
        + [(W["L"][l]["pfull"], (3, BW, D), _const(0, 0, 0), "w"), (W["L"][l]["woutfull"], (D, D), _const(0, 0), "w")],
        [(dx2, (tm, D), _rows())],
        [rw256] * 9
        + [((3, BW, D), (3, BW, D), _const(0, 0, 0), "acc_all"), ((D, D), (D, D), _const(0, 0), "acc_all")])
    dya, dyb, dyc = res[:3]
    dgm = res[3:9]
    bufs[("branch_proj", l)], bufs[("w_out", l)] = res[9:]
    ready(l, "merge")

    tmm = tm
    rw = ((L, BW), (tmm, BW), _rows(), "write")
    xw = ((nb, L, LANE), (nb, tmm, LANE), blk3, "write")
    res = tile_bwd(
        mid_core, f"mid_bwd_{l}", L // tmm, 1,
        [(sv["xr"], (nb, tmm, LANE), blk3, "r"), (sv["xi"], (nb, tmm, LANE), blk3, "r"), (proj, (tmm, BW), _rows(0), "r"),
         (sv["o"], (tmm, BW), _rows(), "r"), (proj, (tmm, BW), _rows(4), "r"), (sv["hs"], (tmm, BW), _rows(), "r"),
         (proj, (tmm, BW), _rows(6), "r"), (P["hmat"], (BW, BW), _const(0, 0), "c"),
         (P["csub"], (None, None, S5_N, BW // 2), _const(l, 0, 0, 0), "w"),
         (P["csub"], (None, None, S5_N, BW // 2), _const(l, 1, 0, 0), "w"), (P["d"], vec, _const(l, 0, 0), "p"),
         (W["L"][l]["gluw"], (BW, BW), _const(0, 0), "w"), (P["glub"], vec, _const(l, 0, 0), "p"),
         (P["hgw"], vec, _const(l, 0, 0), "p")],
        [(dya, (tmm, BW), _rows()), (dyb, (tmm, BW), _rows()), (dyc, (tmm, BW), _rows())],
        [xw, xw, rw, rw, rw, rw, rw,
         ((DEPTH, S5_N, BW // 2), (None, S5_N, BW // 2), _const(l, 0, 0), "acc_all", bufs.get("csub0")),
         ((DEPTH, S5_N, BW // 2), (None, S5_N, BW // 2), _const(l, 0, 0), "acc_all", bufs.get("csub1")), vout,
         ((BW, BW), (BW, BW), _const(0, 0), "acc_all"), vout, vout])
    dxr, dxi, du_skip, do, dg_b, dhs, dgate_c, bufs["csub0"], bufs["csub1"], dd, bufs[("s5_glu_w", l)], dglub, dhgw = res
    small["s5_d"], small["s5_glu_b"], small["hg_norm_w"] = dd[0], dglub[0], dhgw[0]
    ready(l, "mid")

    da, db = rg_scan_bwd(sv["a"], sv["hs"], dhs, L)
    wmat = lambda key: ((DEPTH, BW, BW), (None, BW, BW), _const(l, 0, 0), "acc_all", bufs.get(key))
    res = tile_bwd(
        gates_core, f"gates_bwd_{l}", n_i, 1,
        [(sv["xc"], (tm, BW), _rows(), "r"), (P["wa"], (None, BW, BW), _const(l, 0, 0), "w"), (P["ba"], vec, _const(l, 0, 0), "p"),
         (P["wx"], (None, BW, BW), _const(l, 0, 0), "w"), (P["bx"], vec, _const(l, 0, 0), "p"), (P["lam"], vec, _const(l, 0, 0), "p")],
        [(da, (tm, BW), _rows()), (db, (tm, BW), _rows())],
        [((L, BW), (tm, BW), _rows(), "write"), wmat("wa"), vout, wmat("wx"), vout, vout])
    dxc, bufs["wa"], dba, bufs["wx"], dbx, dlam = res
    small["rg_ba"], small["rg_bx"], small["rg_lambda"] = dba[0], dbx[0], dlam[0]
    dx_c, dconvw, dconvb = conv_bwd(proj, W["convw"][l], dxc, L)
    small["rg_conv_w"], small["rg_conv_b"] = dconvw, dconvb[0]

    dq_b, dz_b, dv_b, dlb = hg_bwd(sv["qzv"], P["lb"][l], sv["sst"], _to_time_order(do), L)
    dq_b, dz_b, dv_b = [_to_segment_order(a) for a in (dq_b, dz_b, dv_b)]

    gr, gi, dar, dai = s5_scan_bwd(dxr, dxi, sv["xr"], sv["xi"], P["ar"][l], P["ai"][l], L)
    bblk = (None, None, BW // 2, S5_N // 2)
    res = tile_bwd(
        s5_pre_core, f"s5pre_bwd_{l}", n_i, 1,
        [(proj, (tm, BW), _rows(0), "r")] + [(P["bsub"], bblk, _const(l, q, 0, 0), "w") for q in range(4)],
        [(gr, (nb, tm, LANE), blk3), (gi, (nb, tm, LANE), blk3)],
        [((L, BW), (tm, BW), _rows(), "write")]
        + [((DEPTH, BW // 2, S5_N // 2), bblk[1:], _const(l, 0, 0), "acc_all", bufs.get(f"bsub{q}")) for q in range(4)])
    du_pre = res[0]
    for q in range(4):
        bufs[f"bsub{q}"] = res[1 + q]
    du_a = add_n(f"du_a_{l}", [(du_skip, ()), (du_pre, ())], (L, BW))
    prep_ct = dict(dar=dar, dai=dai, dlb=dlb)

    pieces = [du_a, dq_b, dz_b, dv_b, dg_b, dx_c, dgate_c, *dgm]
    per_piece, per_shard = BW // LANE, IN_TOTAL // NSH // LANE
    dx1, dnw1 = dx2, []
    tmw = min(TM_WGRAD, L)
    pre_and_x = lambda x, nw, win: (pre_core(x, nw, win)[0], x)
    for s in range(NSH):
        groups = [(pieces[g // per_piece], (tmw, LANE), _rows(g % per_piece))
                  for g in range(s * per_shard, (s + 1) * per_shard)]
        dx1, dnw_s, bufs[("w_in", l)] = tile_bwd(
            pre_and_x, f"pre_bwd_{l}{s}", L // tmw, 1,
            [(sv["x1"], (tmw, D), _rows(), "r"), (W["nw"], (None, None, 1, D), _const(l, 1, 0, 0), "p"),
             (W["L"][l]["win"], (None, D, IN_TOTAL // NSH), _const(s, 0, 0), "w")],
            [groups, (dx1, (tmw, D), _rows())],
            [((L, D), (tmw, D), _rows(), "write"),
             ((1, D), (1, D), _const(0, 0), "acc_all"),
             ((1, NSH, D, IN_TOTAL // NSH), (None, None, D, IN_TOTAL // NSH), _const(0, s, 0, 0), "acc_all",
              bufs.get(("w_in", l)))])
        dnw1.append(dnw_s)
    dnw1 = (dnw1[0] + dnw1[1]) + (dnw1[2] + dnw1[3])
    ready(l, "pre")

    dx0, dnw0 = ffn_bwd(f"ffn_bwd_{l}0", sv["x0"], sv["g0"], sv["u0"], dx1, W, bufs, l, 0, 0, L, tm)
    ready(l, "ffn0")
    small["norm_w"] = jnp.concatenate([dnw0, dnw1, dnw2], axis=0)
    return dx0, small, prep_ct


SMALL_RAW = ("s5_lambda_re", "s5_lambda_im", "s5_log_dt", "s5_b_re", "s5_b_im", "s5_c_re", "s5_c_im", "s5_d", "s5_glu_b",
             "hg_lb_logits", "hg_norm_w", "rg_conv_b", "rg_wa", "rg_ba", "rg_wx", "rg_bx", "rg_lambda", "final_norm_w")
DEPTH = 2


def local_step(x, target, W, raw, layer_weights=None, layer_grads=None):
    L = x.shape[0]
    tm = min(256, L)
    col = lambda v: v.reshape(DEPTH, 1, BW)
    (ar, ai, bsub, csub), s5_vjp = jax.vjp(jax.vmap(s5_prep), *[raw[k] for k in SMALL_RAW[:7]])
    (wa, wx), rg_vjp = jax.vjp(lambda a, b: (jax.vmap(rg_prep)(a), jax.vmap(rg_prep)(b)), raw["rg_wa"], raw["rg_wx"])
    lb, hg_vjp = jax.vjp(hg_prep, raw["hg_lb_logits"])
    P = dict(
        ar=[ar[l] for l in range(DEPTH)], ai=[ai[l] for l in range(DEPTH)],
        bsub=bsub.astype(MMT), csub=csub.astype(MMT), wa=wa.astype(MMT), wx=wx.astype(MMT),
        lb=[lb[l].reshape(1, BW) for l in range(DEPTH)], convb=[raw["rg_conv_b"][l].reshape(1, BW) for l in range(DEPTH)],
        ba=col(raw["rg_ba"]), bx=col(raw["rg_bx"]), lam=col(raw["rg_lambda"]), d=col(raw["s5_d"]),
        glub=col(raw["s5_glu_b"]), hgw=col(raw["hg_norm_w"]), hmat=_head_mean_matrix())

    saved = []
    h = _to_segment_order(x)
    for l in range(DEPTH):
        if layer_weights is not None:
            W["L"][l], h = layer_weights(l, h)
        h, sv = layer_fwd(l, h, W, P, L, tm)
        saved.append(sv)
    loss, dh, dfw = loss_fwd_bwd(h, raw["final_norm_w"].reshape(1, D_MODEL), _to_segment_order(target), L, tm)

    big, per_layer, prep_cts = {}, [None] * DEPTH, [None] * DEPTH
    ready = (lambda l, group: None) if layer_grads is None else (lambda l, group: layer_grads(l, group, big))
    for l in reversed(range(DEPTH)):
        dh, sm, pc = layer_bwd(l, dh, saved[l], W, P, big, L, tm, ready)
        per_layer[l], prep_cts[l] = sm, pc
    dh = _to_time_order(dh)

    small = {k: jnp.stack([per_layer[l][k] for l in range(DEPTH)]) for k in per_layer[0]}
    both = lambda k: jnp.stack([prep_cts[l][k] for l in range(DEPTH)])
    dbsub = jnp.stack([big.pop(f"bsub{q}") for q in range(4)], axis=1)
    dcsub = jnp.stack([big.pop("csub0"), big.pop("csub1")], axis=1)
    s5_g = s5_vjp((both("dar"), both("dai"), dbsub, dcsub))
    small.update(zip(SMALL_RAW[:7], s5_g))
    small["rg_wa"], small["rg_wx"] = rg_vjp((big.pop("wa"), big.pop("wx")))
    (small["hg_lb_logits"],) = hg_vjp(jnp.concatenate([prep_cts[l]["dlb"] for l in range(DEPTH)], axis=0))
    small["final_norm_w"] = dfw[0]
    return loss, dh, big, small


ANY = pl.BlockSpec(memory_space=pl.ANY)


def _place():
    x, y, c = lax.axis_index("x"), lax.axis_index("y"), lax.axis_index("c")
    chips = [(1 - x, y), (x, 1 - y), (1 - x, 1 - y)]
    return x, y, c, chips


def _remote(src, dst, send, recv, k, to):
    return pltpu.make_async_remote_copy(src_ref=src, dst_ref=dst, send_sem=send.at[k], recv_sem=recv.at[k],
                                        device_id=to, device_id_type=MESH)


def _comm_call(body, name, ins, out_shapes, n_sem, n_loc):
    return pl.pallas_call(
        body, name=name, in_specs=[ANY] * len(ins), out_specs=[ANY] * len(out_shapes), out_shape=out_shapes,
        scratch_shapes=[pltpu.SemaphoreType.DMA((n_sem,)), pltpu.SemaphoreType.DMA((n_sem,)),
                        pltpu.SemaphoreType.DMA((max(n_loc, 1),))],
    )(*ins)


def gather_shards(name, shards):
    n = len(shards)
    per = 8

    def body(*refs):
        ins, outs = refs[:n], refs[n:2 * n]
        send, recv, _ = refs[2 * n:]
        x, y, c, chips = _place()
        me = 2 * x + y
        sib = (x, y, 1 - c)
        sends = []
        for w in range(n):
            for j, (cx, cy) in enumerate(chips):
                cp = _remote(ins[w].at[c], outs[w].at[c, me], send, recv, per * w + j, (cx, cy, c))
                cp.start()
                sends.append(cp)
        for w in range(n):
            for l in range(2):
                cp = _remote(ins[w].at[l], outs[w].at[l, me], send, recv, per * w + 6 + l, sib)
                cp.start()
                sends.append(cp)
        for w in range(n):
            for j, (cx, cy) in enumerate(chips):
                theirs = outs[w].at[c, 2 * cx + cy]
                _remote(ins[w].at[c], theirs, send, recv, per * w + j, (cx, cy, c)).wait_recv()
                cp = _remote(theirs, theirs, send, recv, per * w + 3 + j, sib)
                cp.start()
                sends.append(cp)
        for w in range(n):
            for j, (cx, cy) in enumerate(chips):
                dst = outs[w].at[1 - c, 2 * cx + cy]
                _remote(dst, dst, send, recv, per * w + 3 + j, sib).wait_recv()
            for l in range(2):
                dst = outs[w].at[l, me]
                _remote(dst, dst, send, recv, per * w + 6 + l, sib).wait_recv()
        for cp in sends:
            cp.wait_send()

    shapes = [jax.ShapeDtypeStruct((2, NSH) + s.shape[1:], s.dtype) for s in shards]
    return _comm_call(body, name, shards, shapes, per * n, 0)


def exchange_halves(name, grads, ranges):
    n = len(grads)

    def body(*refs):
        ins, outs = refs[:n], refs[n:2 * n]
        send, recv, _ = refs[2 * n:]
        x, y, c, _chips = _place()
        cps = []
        for w in range(n):
            h = grads[w].shape[2] // 2
            p0, np_ = ranges[w]
            cp = _remote(ins[w].at[pl.ds(p0, np_), :, pl.ds((1 - c) * h, h)], outs[w], send, recv, w, (x, y, 1 - c))
            cp.start()
            cps.append(cp)
        for cp in cps:
            cp.wait()

    shapes = [jax.ShapeDtypeStruct((r[1], NSH, g.shape[2] // 2, g.shape[3]), g.dtype) for g, r in zip(grads, ranges)]
    return _comm_call(body, name, grads, shapes, n, 0)


def share_halves(name, pieces):
    n = len(pieces)

    def body(*refs):
        ins, outs = refs[:n], refs[n:2 * n]
        send, recv, _ = refs[2 * n:]
        x, y, c, _chips = _place()
        cps = []
        for w in range(n):
            cp = _remote(ins[w], outs[w], send, recv, w, (x, y, 1 - c))
            cp.start()
            cps.append(cp)
        for cp in cps:
            cp.wait()

    return _comm_call(body, name, pieces, [jax.ShapeDtypeStruct(p.shape, p.dtype) for p in pieces], n, 0)


def add_own_half(name, g, ra, c, wire, b0):
    nblk, h, cols = ra.shape
    tr = _row_tile(h, cols, mult=16)
    nt = h // tr

    def body(c_ref, g_ref, r_ref, o_ref):
        o_ref[...] = (g_ref[...] + r_ref[...]).astype(o_ref.dtype)

    blk = (None, tr, cols)
    return pl.pallas_call(
        body, name=name,
        grid_spec=pltpu.PrefetchScalarGridSpec(
            num_scalar_prefetch=1, grid=(nblk, nt),
            in_specs=[pl.BlockSpec(blk, lambda s, i, c_ref: (b0 + s, c_ref[0] * nt + i, 0)), pl.BlockSpec(blk, lambda s, i, c_ref: (s, i, 0))],
            out_specs=pl.BlockSpec(blk, lambda s, i, c_ref: (s, i, 0))),
        out_shape=jax.ShapeDtypeStruct(ra.shape, wire),
    )(c.reshape(1), g, ra)


def add_chips(name, hb, rb, me):
    npc, _, h, cols = hb.shape
    tr = _row_tile(h, cols, mult=16)

    def body(me_ref, h_ref, r0, r1, r2, o_ref):
        f = lambda r: r[...].astype(F32)
        o_ref[...] = ((f(h_ref) + f(r0)) + f(r1)) + f(r2)

    rspec = lambda j: pl.BlockSpec((None, None, tr, cols), functools.partial(lambda p, i, me_ref, j: (j, p, i, 0), j=j))
    return pl.pallas_call(
        body, name=name,
        grid_spec=pltpu.PrefetchScalarGridSpec(
            num_scalar_prefetch=1, grid=(npc, h // tr),
            in_specs=[pl.BlockSpec((None, None, tr, cols), lambda p, i, me_ref: (p, me_ref[0], i, 0)), rspec(0), rspec(1), rspec(2)],
            out_specs=pl.BlockSpec((None, tr, cols), lambda p, i, me_ref: (p, i, 0))),
        out_shape=jax.ShapeDtypeStruct((npc, h, cols), F32),
    )(me.reshape(1), hb, rb, rb, rb)


def adamw_halves(name, w, m, v, own, other, c):
    npc, rows, cols = w.shape
    h = rows // 2
    tr = _row_tile(h, cols, budget=1024 * 1024)
    nt = h // tr
    c1 = 1.0 - ADAM_B1 ** ADAM_STEP
    c2 = 1.0 - ADAM_B2 ** ADAM_STEP

    def body(c_ref, w_ref, m_ref, v_ref, own_ref, oth_ref, g_ref, d_ref, nm_ref, nv_ref):
        g = jnp.where(pl.program_id(1) == c_ref[0], own_ref[...], oth_ref[...])
        nm = ADAM_B1 * m_ref[...] + (1.0 - ADAM_B1) * g
        nv = ADAM_B2 * v_ref[...] + (1.0 - ADAM_B2) * jnp.square(g)
        g_ref[...] = g
        d_ref[...] = -ADAM_LR * ((nm / c1) / (jnp.sqrt(nv / c2) + ADAM_EPS) + ADAM_WD * w_ref[...])
        nm_ref[...] = nm
        nv_ref[...] = nv

    full = pl.BlockSpec((None, tr, cols), lambda p, hh, i, c_ref: (p, hh * nt + i, 0))
    half = pl.BlockSpec((None, tr, cols), lambda p, hh, i, c_ref: (p, i, 0))
    return pl.pallas_call(
        body, name=name,
        grid_spec=pltpu.PrefetchScalarGridSpec(
            num_scalar_prefetch=1, grid=(npc, 2, nt),
            in_specs=[full, full, full, half, half], out_specs=[full] * 4),
        out_shape=[jax.ShapeDtypeStruct(w.shape, F32)] * 4,
    )(c.reshape(1), w, m, v, own, other)


WEIGHTS = ("norm_w", "final_norm_w", "ffn_gate", "ffn_up", "ffn_down", "w_in", "branch_proj", "w_out", "s5_lambda_re",
           "s5_lambda_im", "s5_log_dt", "s5_b_re", "s5_b_im", "s5_c_re", "s5_c_im", "s5_d", "s5_glu_w", "s5_glu_b",
           "hg_lb_logits", "hg_norm_w", "rg_conv_w", "rg_conv_b", "rg_wa", "rg_ba", "rg_wx", "rg_bx", "rg_lambda")
BIG = ("ffn_gate", "ffn_up", "ffn_down", "w_in", "branch_proj", "w_out", "s5_glu_w")
SHARDED_SMALL = ("norm_w", "rg_conv_w")
SMALL = SMALL_RAW + SHARDED_SMALL


def _view2d(shape):
    return (1, shape[0]) if len(shape) == 1 else (math.prod(shape[:-1]), shape[-1])


def _small_layout(shapes, row_multiple):
    layout, at = [], 0
    for shape in shapes:
        r, c = _view2d(shape)
        rp = -(-r // 8) * 8
        layout.append((at, r, c, rp))
        at += rp * max(1, c // LANE)
    return layout, -(-at // row_multiple) * row_multiple


def pack_small(name, arrays, row_multiple):
    layout, rows = _small_layout([a.shape for a in arrays], row_multiple)

    def body(*refs):
        out = refs[-1]
        out[...] = jnp.zeros_like(out)
        for ref, (r0, r, c, rp) in zip(refs[:-1], layout):
            if c <= LANE:
                out[r0:r0 + r, 0:c] = ref[...]
            else:
                for q in range(c // LANE):
                    out[r0 + q * rp:r0 + q * rp + r, :] = ref[:, q * LANE:(q + 1) * LANE]

    return pl.pallas_call(
        body, name=name, out_shape=jax.ShapeDtypeStruct((rows, LANE), F32),
        compiler_params=pltpu.CompilerParams(vmem_limit_bytes=VMEM_LIMIT),
    )(*[a.reshape(_view2d(a.shape)) for a in arrays])


def unpack_small(name, packed, shapes):
    layout, _ = _small_layout(shapes, 8)

    def body(p_ref, *outs):
        for ref, (r0, r, c, rp) in zip(outs, layout):
            if c <= LANE:
                ref[...] = p_ref[r0:r0 + r, 0:c]
            else:
                for q in range(c // LANE):
                    ref[:, q * LANE:(q + 1) * LANE] = p_ref[r0 + q * rp:r0 + q * rp + r, :]

    res = pl.pallas_call(
        body, name=name, out_shape=[jax.ShapeDtypeStruct(_view2d(s), F32) for s in shapes],
        compiler_params=pltpu.CompilerParams(vmem_limit_bytes=VMEM_LIMIT),
    )(packed)
    return [a.reshape(s) for a, s in zip(res, shapes)]


HBM = pl.BlockSpec(memory_space=pltpu.HBM)
SEM = pl.BlockSpec(memory_space=pltpu.SEMAPHORE)
EFFECT = pltpu.SideEffectType.DATAFLOW_SIDE_EFFECTING


def split_start(name, srcs, land_shapes, plan, n_send, n_recv):
    ns, nl = len(srcs), len(land_shapes)

    def body(*refs):
        ins, lands = refs[:ns], refs[ns:ns + nl]
        send, recv = refs[ns + nl], refs[ns + nl + 1]
        for src, dst, ks, kr, dev in plan(ins, lands):
            pltpu.make_async_remote_copy(src_ref=src, dst_ref=dst, send_sem=send.at[ks], recv_sem=recv.at[kr],
                                         device_id=dev, device_id_type=MESH).start()
        refs[-1][...] = jnp.zeros_like(refs[-1])

    hbm = lambda a: pltpu.with_memory_space_constraint(a, pltpu.HBM)
    lands = [lax.empty(s.shape, s.dtype) for s in land_shapes]
    out = pl.pallas_call(
        body, name=name,
        out_shape=(pltpu.SemaphoreType.DMA((n_send,)), pltpu.SemaphoreType.DMA((n_recv,)),
                   *[pltpu.HBM(a.shape, a.dtype) for a in srcs], *[pltpu.HBM(s.shape, s.dtype) for s in land_shapes],
                   jax.ShapeDtypeStruct((8, LANE), F32)),
        in_specs=[HBM] * (ns + nl), out_specs=(SEM, SEM, *[HBM] * (ns + nl), pl.BlockSpec(memory_space=pltpu.VMEM)),
        input_output_aliases={k: 2 + k for k in range(ns + nl)},
        compiler_params=pltpu.CompilerParams(has_side_effects=EFFECT),
    )(*[hbm(a) for a in srcs], *[hbm(a) for a in lands])
    return out[:-1], out[-1]


def split_wait(name, handles, n_src, waits, after):
    send, recv, *bufs = handles
    nb = len(bufs)

    def body(*refs):
        ins, lands = refs[:n_src], refs[n_src:nb]
        send_sem, recv_sem = refs[nb], refs[nb + 1]
        x, y, c, _chips = _place()
        sends, recvs = waits(ins, lands)
        for src, k in sends:
            pltpu.make_async_remote_copy(src_ref=src, dst_ref=src, send_sem=send_sem.at[k], recv_sem=recv_sem.at[0],
                                         device_id=(x, y, 1 - c), device_id_type=MESH).wait_send()
        for dst, k in recvs:
            pltpu.make_async_remote_copy(src_ref=dst, dst_ref=dst, send_sem=send_sem.at[0], recv_sem=recv_sem.at[k],
                                         device_id=(x, y, 1 - c), device_id_type=MESH).wait_recv()

    out = pl.pallas_call(
        body, name=name, out_shape=tuple(pltpu.HBM(a.shape, a.dtype) for a in bufs),
        in_specs=[HBM] * nb + [SEM, SEM, ANY], out_specs=tuple([HBM] * nb),
        input_output_aliases={k: k for k in range(nb)},
        compiler_params=pltpu.CompilerParams(has_side_effects=EFFECT),
    )(*bufs, send, recv, after)
    return list(out[:n_src]), list(out[n_src:])


def gather_plan(n):
    def plan(ins, lands):
        x, y, c, chips = _place()
        me = 2 * x + y
        copies = []
        for w in range(n):
            for j, (cx, cy) in enumerate(chips):
                for t in range(2):
                    copies.append((ins[w].at[c], lands[w].at[c, me], 8 * w + 2 * j + t, 8 * w + 2 * j + c, (cx, cy, t)))
            for half in range(2):
                copies.append((ins[w].at[half], lands[w].at[half, me], 8 * w + 6 + half, 8 * w + 6 + half, (x, y, 1 - c)))
        return copies

    def waits(ins, lands):
        x, y, c, chips = _place()
        me = 2 * x + y
        sends, recvs = [], []
        for w in range(n):
            for j, (cx, cy) in enumerate(chips):
                for t in range(2):
                    sends.append((ins[w].at[c], 8 * w + 2 * j + t))
                    recvs.append((lands[w].at[t, 2 * cx + cy], 8 * w + 2 * j + t))
            for half in range(2):
                sends.append((ins[w].at[half], 8 * w + 6 + half))
                recvs.append((lands[w].at[half, me], 8 * w + 6 + half))
        return sends, recvs

    return plan, waits


def scatter_plan(n):
    def plan(ins, lands):
        x, y, c, chips = _place()
        return [(ins[w].at[:, 2 * cx + cy], lands[w].at[j], 3 * w + j, 3 * w + j, (cx, cy, c))
                for w in range(n) for j, (cx, cy) in enumerate(chips)]

    def waits(ins, lands):
        x, y, c, chips = _place()
        sends = [(ins[w].at[:, 2 * cx + cy], 3 * w + j) for w in range(n) for j, (cx, cy) in enumerate(chips)]
        recvs = [(lands[w].at[j], 3 * w + j) for w in range(n) for j in range(3)]
        return sends, recvs

    return plan, waits


def _layer_shards(w, l):
    return [w["ffn_gate"][l].astype(MMT), w["ffn_up"][l].astype(MMT), w["ffn_down"][l].astype(MMT),
            w["w_in"][l].reshape(2, D_MODEL // 2, -1).astype(MMT),
            w["branch_proj"][l].reshape(2, 3 * BW // 2, -1).astype(MMT),
            w["w_out"][l].reshape(2, -1, D_MODEL).astype(MMT),
            w["s5_glu_w"][l].reshape(2, -1, BW).astype(MMT)]


def _layer_weights(g):
    rows = lambda a: a.transpose(1, 0, 2, 3).reshape(NSH, -1, a.shape[-1])
    p = rows(g[4]).reshape(NSH, 3, BW, -1).transpose(1, 2, 0, 3).reshape(3, BW, D_MODEL)
    return dict(wg=g[0], wu=g[1], wd=g[2], win=rows(g[3]), pfull=p,
                woutfull=rows(g[5]).reshape(D_MODEL, D_MODEL), gluw=rows(g[6]).reshape(BW, BW))


GROUPS = {"ffn1": ("ffn_gate", "ffn_up", "ffn_down"), "merge": ("branch_proj", "w_out"), "mid": ("s5_glu_w",),
          "pre": ("w_in",), "ffn0": ("ffn_gate", "ffn_up", "ffn_down")}


def _grad_views(big, l, group):
    views = []
    for name in GROUPS[group]:
        if name == "branch_proj":
            dq = D_MODEL // NSH
            a = big[(name, l)].reshape(3, BW, NSH, dq).transpose(2, 0, 1, 3).reshape(1, NSH, 3 * BW, dq)
        elif name.startswith("ffn"):
            a = big[(name, l, 1 if group == "ffn1" else 0)]
        else:
            a = big[(name, l)]
            a = a.reshape(1, NSH, -1, a.shape[-1])
        views.append((name, a, 0))
    return views


def halves_plan(n):
    def src(ref, c):
        h = ref.shape[2] // 2
        return ref.at[:, :, pl.ds((1 - c) * h, h)]

    def plan(ins, lands):
        x, y, c, _chips = _place()
        return [(src(ins[w], c), lands[w], w, w, (x, y, 1 - c)) for w in range(n)]

    def waits(ins, lands):
        x, y, c, _chips = _place()
        return [(src(ins[w], c), w) for w in range(n)], [(lands[w], w) for w in range(n)]

    return plan, waits


def _reduce_to_halves(tag, views, c, wire):
    from_sibling = exchange_halves(f"reduce_cores_{tag}", [a for _, a, _ in views], [(p0, 1) for _, _, p0 in views])
    merge = lambda a: a.reshape((-1,) + a.shape[2:])
    return [add_own_half(f"sum_cores_{tag}_{i}", merge(a), merge(r), c, wire[i], NSH * p0).reshape(r.shape)
            for i, ((_, a, p0), r) in enumerate(zip(views, from_sibling))]


def _step(x, target, w, m, v):
    mx, my, mc = lax.axis_index("x"), lax.axis_index("y"), lax.axis_index("c")
    me = (2 * mx + my).astype(jnp.int32)
    mc = mc.astype(jnp.int32)

    W = dict(L=[None] * DEPTH)
    state = {"pending": []}
    n_big = len(BIG)
    g_plan, g_waits = gather_plan(n_big)

    def layer_weights(l, h):
        if l == 0:
            got = gather_shards("gather_weights_0", _layer_shards(w, 0) + [w[n] for n in SHARDED_SMALL])
            nxt = _layer_shards(w, 1)
            got, nxt = lax.optimization_barrier((got, nxt))
            shapes = [jax.ShapeDtypeStruct((2, NSH) + a.shape[1:], a.dtype) for a in nxt]
            state["gather"], token = split_start("gather_weights_1_start", nxt, shapes, g_plan, 8 * n_big, 8 * n_big)
            W["nw"] = got[n_big].transpose(0, 2, 1, 3).reshape(DEPTH, 3, 1, D_MODEL) + token[0, 0]
            W["convw"] = got[n_big + 1].transpose(0, 2, 1, 3).reshape(DEPTH, CONV_W, BW)
            return _layer_weights(got[:n_big]), h
        return _layer_weights(split_wait("gather_weights_1_wait", state["gather"], n_big, g_waits, h)[1]), h

    def to_chips(after):
        if "cores" not in state:
            return
        tag, names, l, group, handles, waits = state.pop("cores")
        sent, landed = split_wait(f"reduce_cores_{tag}_wait", handles, len(names), waits, after)
        merge = lambda a: a.reshape((-1,) + a.shape[2:])
        halves = [add_own_half(f"sum_cores_{tag}_{i}", merge(a), merge(r), mc, jnp.bfloat16, 0).reshape(r.shape)
                  for i, (a, r) in enumerate(zip(sent, landed))]
        shapes = [jax.ShapeDtypeStruct((3, a.shape[0]) + a.shape[2:], a.dtype) for a in halves]
        plan, waits = scatter_plan(len(halves))
        handles, _ = split_start(f"reduce_chips_{tag}_start", halves, shapes, plan, 3 * len(halves), 3 * len(halves))
        state["pending"].append((tag, names, l, group, handles, waits))

    def layer_grads(l, group, big):
        views = _grad_views(big, l, group)
        to_chips(views[0][1])
        if (l, group) == (0, "ffn0"):
            return
        tag = f"{l}_{group}"
        if (l, group) == (0, "pre"):
            halves = _reduce_to_halves(tag, views, mc, [jnp.bfloat16] * len(views))
            shapes = [jax.ShapeDtypeStruct((3, a.shape[0]) + a.shape[2:], a.dtype) for a in halves]
            plan, waits = scatter_plan(len(halves))
            handles, token = split_start(f"reduce_chips_{tag}_start", halves, shapes, plan, 3 * len(halves), 3 * len(halves))
            W["nw"] = W["nw"] + token[0, 0]
            state["pending"].append((tag, [name for name, _, _ in views], l, group, handles, waits))
            return
        arrays = [a for _, a, _ in views]
        shapes = [jax.ShapeDtypeStruct((1, NSH, a.shape[2] // 2, a.shape[3]), a.dtype) for a in arrays]
        plan, waits = halves_plan(len(arrays))
        handles, token = split_start(f"reduce_cores_{tag}_start", arrays, shapes, plan, len(arrays), len(arrays))
        W["nw"] = W["nw"] + token[0, 0]
        state["cores"] = (tag, [name for name, _, _ in views], l, group, handles, waits)

    loss, dx, big, small = local_step(x[0], target[0], W, {k: w[k] for k in SMALL_RAW}, layer_weights, layer_grads)

    pieces = {n: {} for n in BIG}
    block_of = lambda name, l, group: (2 * l + (group == "ffn1")) if name.startswith("ffn") else l
    views = _grad_views(big, 0, "ffn0")
    small_packed = pack_small("pack_small_grads", [small[n] for n in SMALL], NSH * 32)
    halves = _reduce_to_halves("0_ffn0", views + [("small", small_packed.reshape(1, NSH, -1, LANE), 0)], mc,
                               [jnp.bfloat16] * len(views) + [F32])
    shapes = [jax.ShapeDtypeStruct((3, a.shape[0]) + a.shape[2:], a.dtype) for a in halves]
    plan, waits = scatter_plan(len(halves))
    last_handles, token = split_start("reduce_chips_0_ffn0_start", halves, shapes, plan, 3 * len(halves), 3 * len(halves))
    mc = mc + token[0, 0].astype(jnp.int32)
    for tag, names, l, group, handles, waits_k in state["pending"]:
        sent, landed = split_wait(f"reduce_chips_{tag}_wait", handles, len(names), waits_k, dx)
        for i, (name, h, r) in enumerate(zip(names, sent, landed)):
            pieces[name][block_of(name, l, group)] = add_chips(f"sum_chips_{tag}_{i}", h, r, me)

    g, delta, new_m, new_v = {}, {}, {}, {}

    def update(tag, names, extra):
        own = [jnp.concatenate([pieces[n][b] for b in sorted(pieces[n])], axis=0) for n in names] + extra
        other = share_halves(f"reduce_share_{tag}", own)
        for i, n in enumerate(names):
            view = lambda a: a.reshape(own[i].shape[0], -1, own[i].shape[2])
            res = adamw_halves(f"adamw_{n}", view(w[n]), view(m[n]), view(v[n]), own[i], other[i], mc)
            g[n], delta[n], new_m[n], new_v[n] = [a.reshape(w[n].shape) for a in res]
        return own, other

    early = [n for n in BIG if not n.startswith("ffn")]
    update("early", early, [])
    sent, landed = split_wait("reduce_chips_0_ffn0_wait", last_handles, len(halves), waits, new_v[early[0]])
    last = [add_chips(f"sum_chips_0_ffn0_{i}", h, r, me) for i, (h, r) in enumerate(zip(sent, landed))]
    for (name, _, _), piece in zip(views, last):
        pieces[name][block_of(name, 0, "ffn0")] = piece
    own, other = update("last", [n for n in BIG if n.startswith("ffn")], [last[-1]])

    piece = jnp.stack([jnp.where(mc == 0, own[-1][0], other[-1][0]), jnp.where(mc == 0, other[-1][0], own[-1][0])])
    (all_small,) = gather_shards("gather_small", [piece])
    full_small = unpack_small("unpack_small_grads", all_small.transpose(1, 0, 2, 3).reshape(-1, LANE),
                              [small[n].shape for n in SMALL])
    g.update(zip(SMALL, full_small))
    g["norm_w"] = lax.dynamic_slice_in_dim(g["norm_w"], me * (D_MODEL // NSH), D_MODEL // NSH, axis=2)
    g["rg_conv_w"] = lax.dynamic_slice_in_dim(g["rg_conv_w"], me * (BW // NSH), BW // NSH, axis=2)

    packed = [pack_small(f"pack_small_{tag}", [src[n] for n in SMALL], 8)
              for tag, src in (("w", w), ("g", g), ("m", m), ("v", v))]
    for tag, dst, flat in zip(("delta", "m", "v"), (delta, new_m, new_v), adamw(*packed)):
        dst.update(zip(SMALL, unpack_small(f"unpack_small_{tag}", flat, [w[n].shape for n in SMALL])))

    total = lax.psum(loss[0, 0], ("x", "y", "c"))
    return (total, dx[None], *[g[n] for n in WEIGHTS], *[delta[n] for n in WEIGHTS],
            *[new_m[n] for n in WEIGHTS], *[new_v[n] for n in WEIGHTS])


def kernel(x, norm_w, final_norm_w, ffn_gate, ffn_up, ffn_down, w_in, branch_proj, w_out, s5_lambda_re, s5_lambda_im, s5_log_dt, s5_b_re, s5_b_im, s5_c_re, s5_c_im, s5_d, s5_glu_w, s5_glu_b, hg_lb_logits, hg_norm_w, rg_conv_w, rg_conv_b, rg_wa, rg_ba, rg_wx, rg_bx, rg_lambda, loss_target, m_norm_w, m_final_norm_w, m_ffn_gate, m_ffn_up, m_ffn_down, m_w_in, m_branch_proj, m_w_out, m_s5_lambda_re, m_s5_lambda_im, m_s5_log_dt, m_s5_b_re, m_s5_b_im, m_s5_c_re, m_s5_c_im, m_s5_d, m_s5_glu_w, m_s5_glu_b, m_hg_lb_logits, m_hg_norm_w, m_rg_conv_w, m_rg_conv_b, m_rg_wa, m_rg_ba, m_rg_wx, m_rg_bx, m_rg_lambda, v_norm_w, v_final_norm_w, v_ffn_gate, v_ffn_up, v_ffn_down, v_w_in, v_branch_proj, v_w_out, v_s5_lambda_re, v_s5_lambda_im, v_s5_log_dt, v_s5_b_re, v_s5_b_im, v_s5_c_re, v_s5_c_im, v_s5_d, v_s5_glu_w, v_s5_glu_b, v_hg_lb_logits, v_hg_norm_w, v_rg_conv_w, v_rg_conv_b, v_rg_wa, v_rg_ba, v_rg_wx, v_rg_bx, v_rg_lambda):
    ws = (norm_w, final_norm_w, ffn_gate, ffn_up, ffn_down, w_in, branch_proj, w_out, s5_lambda_re, s5_lambda_im, s5_log_dt, s5_b_re, s5_b_im, s5_c_re, s5_c_im, s5_d, s5_glu_w, s5_glu_b, hg_lb_logits, hg_norm_w, rg_conv_w, rg_conv_b, rg_wa, rg_ba, rg_wx, rg_bx, rg_lambda)
    ms = (m_norm_w, m_final_norm_w, m_ffn_gate, m_ffn_up, m_ffn_down, m_w_in, m_branch_proj, m_w_out, m_s5_lambda_re, m_s5_lambda_im, m_s5_log_dt, m_s5_b_re, m_s5_b_im, m_s5_c_re, m_s5_c_im, m_s5_d, m_s5_glu_w, m_s5_glu_b, m_hg_lb_logits, m_hg_norm_w, m_rg_conv_w, m_rg_conv_b, m_rg_wa, m_rg_ba, m_rg_wx, m_rg_bx, m_rg_lambda)
    vs = (v_norm_w, v_final_norm_w, v_ffn_gate, v_ffn_up, v_ffn_down, v_w_in, v_branch_proj, v_w_out, v_s5_lambda_re, v_s5_lambda_im, v_s5_log_dt, v_s5_b_re, v_s5_b_im, v_s5_c_re, v_s5_c_im, v_s5_d, v_s5_glu_w, v_s5_glu_b, v_hg_lb_logits, v_hg_norm_w, v_rg_conv_w, v_rg_conv_b, v_rg_wa, v_rg_ba, v_rg_wx, v_rg_bx, v_rg_lambda)
    return _step(x, loss_target, dict(zip(WEIGHTS, ws)), dict(zip(WEIGHTS, ms)), dict(zip(WEIGHTS, vs)))
```

```python
import functools
import math
from typing import NamedTuple

import jax
import jax.numpy as jnp
from jax import lax
from jax.experimental import pallas as pl
from jax.experimental.pallas import tpu as pltpu

F32 = jnp.float32
MMT = jnp.bfloat16

D_MODEL = 1024
BW = 512
S5_GROUP, S5_GROUPS, S5_STATE = 16, 32, 64
S5_N = S5_GROUPS * S5_STATE
HG_HEADS, HG_D = 4, 128
HG_CHUNK = 256
RG_BLOCKS, RG_BLOCK = 8, 64
RG_C = 8.0
CONV_W = 4
D_FF = 2816
EPS = 1e-6
IN_TOTAL = 6656
NSH = 4
NSEG = 8
LANE = 128
VMEM_LIMIT = 56 * 1024 * 1024
TM_FWD = 512
TM_WGRAD = 512

ADAM_LR, ADAM_B1, ADAM_B2, ADAM_EPS, ADAM_WD, ADAM_STEP = 0.001, 0.9, 0.999, 1e-08, 0.01, 10

MESH = pl.DeviceIdType.MESH


class WP(NamedTuple):
    w: jax.Array
    p: jax.Array


def _dg(a, b, ca, cb):
    return lax.dot_general(a, b, (((ca,), (cb,)), ((), ())), preferred_element_type=F32)


@jax.custom_vjp
def _mmw(a, w, p):
    return _dg(a.astype(MMT), w, 1, 0)


def _mmw_fwd(a, w, p):
    return _mmw(a, w, p), (a, w)


def _mmw_bwd(res, g):
    a, w = res
    gb = g.astype(MMT)
    return _dg(gb, w, 1, 1), jnp.zeros_like(w), _dg(a.astype(MMT), gb, 0, 0)


_mmw.defvjp(_mmw_fwd, _mmw_bwd)


def mm(a, w):
    if isinstance(w, WP):
        return _mmw(a, w.w, w.p)
    return _dg(a.astype(MMT), w, 1, 0)


@jax.custom_vjp
def mma_nn(a, b):
    return _dg(a.astype(MMT), b.astype(MMT), 1, 0)


def _nn_f(a, b):
    return mma_nn(a, b), (a, b)


def _nn_b(res, g):
    a, b = res
    gb = g.astype(MMT)
    return _dg(gb, b.astype(MMT), 1, 1), _dg(a.astype(MMT), gb, 0, 0)


mma_nn.defvjp(_nn_f, _nn_b)


@jax.custom_vjp
def mma_nt(a, b):
    return _dg(a.astype(MMT), b.astype(MMT), 1, 1)


def _nt_f(a, b):
    return mma_nt(a, b), (a, b)


def _nt_b(res, g):
    a, b = res
    gb = g.astype(MMT)
    return _dg(gb, b.astype(MMT), 1, 0), _dg(gb, a.astype(MMT), 0, 0)


mma_nt.defvjp(_nt_f, _nt_b)


@jax.custom_vjp
def mma_tn(a, b):
    return _dg(a.astype(MMT), b.astype(MMT), 0, 0)


def _tn_f(a, b):
    return mma_tn(a, b), (a, b)


def _tn_b(res, g):
    a, b = res
    gb = g.astype(MMT)
    return _dg(b.astype(MMT), gb, 1, 1), _dg(a.astype(MMT), gb, 1, 0)


mma_tn.defvjp(_tn_f, _tn_b)


def _split3(x):
    hi = x.astype(MMT)
    r = x - hi.astype(F32)
    mid = r.astype(MMT)
    return hi, mid, (r - mid.astype(F32)).astype(MMT)


@jax.custom_vjp
def mm_exact(m, x):
    mb = m.astype(MMT)
    hi, mid, lo = _split3(x)
    return (_dg(mb, hi, 1, 0) + _dg(mb, mid, 1, 0)) + _dg(mb, lo, 1, 0)


def _mm_exact_fwd(m, x):
    return mm_exact(m, x), m


def _mm_exact_bwd(m, g):
    mb = m.astype(MMT)
    hi, mid, lo = _split3(g)
    return jnp.zeros_like(m), (_dg(mb, hi, 0, 0) + _dg(mb, mid, 0, 0)) + _dg(mb, lo, 0, 0)


mm_exact.defvjp(_mm_exact_fwd, _mm_exact_bwd)


@jax.custom_vjp
def mm_exact_r(x, m):
    mb = m.astype(MMT)
    hi, mid, lo = _split3(x)
    return (_dg(hi, mb, 1, 0) + _dg(mid, mb, 1, 0)) + _dg(lo, mb, 1, 0)


def _mm_exact_r_fwd(x, m):
    return mm_exact_r(x, m), m


def _mm_exact_r_bwd(m, g):
    mb = m.astype(MMT)
    hi, mid, lo = _split3(g)
    return (_dg(hi, mb, 1, 1) + _dg(mid, mb, 1, 1)) + _dg(lo, mb, 1, 1), jnp.zeros_like(m)


mm_exact_r.defvjp(_mm_exact_r_fwd, _mm_exact_r_bwd)


def _rms(x, w):
    return x * lax.rsqrt(jnp.mean(x * x, axis=-1, keepdims=True) + EPS) * w


def _expm1(x):
    series = x * (1.0 + x * (1.0 / 2) * (1.0 + x * (1.0 / 3) * (1.0 + x * (1.0 / 4) * (1.0 + x * (1.0 / 5) * (1.0 + x * (1.0 / 6))))))
    return jnp.where(jnp.abs(x) < 0.1, series, jnp.exp(x) - 1.0)


def _bspec(block, fn, order):
    if order == "is":
        return pl.BlockSpec(block, lambda i, s: fn(s, i))
    return pl.BlockSpec(block, lambda s, i: fn(s, i))


def tile_fwd(fn, name, n_i, n_s, ins, outs, s_outer=False):
    n_in = len(ins)
    order = "si" if s_outer else "is"
    assert not (s_outer and any(o[4] for o in outs))

    def body(*refs):
        s = pl.program_id(0 if s_outer else 1)
        res = fn(*[r[...] for r in refs[:n_in]], s)
        for o_ref, val, spec in zip(refs[n_in:], res, outs):
            if spec[4] and n_s > 1:
                @pl.when(s == 0)
                def _(o_ref=o_ref, val=val):
                    o_ref[...] = val.astype(o_ref.dtype)

                @pl.when(s != 0)
                def _(o_ref=o_ref, val=val):
                    o_ref[...] += val.astype(o_ref.dtype)
            else:
                o_ref[...] = val.astype(o_ref.dtype)

    return pl.pallas_call(
        body, grid=(n_s, n_i) if s_outer else (n_i, n_s), name=name,
        in_specs=[_bspec(b, f, order) for _, b, f in ins],
        out_specs=[_bspec(b, f, order) for _, _, b, f, _ in outs],
        out_shape=[jax.ShapeDtypeStruct(sh, dt) for sh, dt, _, _, _ in outs],
        compiler_params=pltpu.CompilerParams(vmem_limit_bytes=VMEM_LIMIT,
                                             dimension_semantics=("arbitrary", "arbitrary")),
    )(*[a for a, _, _ in ins])


def tile_bwd(fn, name, n_i, n_s, ins, cts, gouts):
    groups = [c if isinstance(c, list) else [c] for c in cts]
    cts = [blk for grp in groups for blk in grp]
    n_in, n_ct = len(ins), len(cts)
    kinds = [k for _, _, _, k in ins]
    d_pos = [j for j, k in enumerate(kinds) if k != "c"]
    shared = [(gi, spec[4]) for gi, spec in enumerate(gouts) if len(spec) == 5 and spec[4] is not None]
    n_sh = len(shared)

    def body(*refs):
        s, i = pl.program_id(0), pl.program_id(1)
        vals = [r[...] for r in refs[:n_in]]
        ct_refs, ctv = list(refs[n_in:n_in + n_ct]), []
        for grp in groups:
            parts = [ct_refs.pop(0)[...] for _ in grp]
            ctv.append(parts[0] if len(parts) == 1 else jnp.concatenate(parts, axis=1))
        ctv = tuple(ctv)
        g_refs = refs[n_in + n_ct + n_sh:]

        def g(*dv):
            args = list(vals)
            for j, v in zip(d_pos, dv):
                args[j] = WP(vals[j], v) if kinds[j] == "w" else v
            return tuple(fn(*args))

        dv0 = [jnp.zeros(vals[j].shape, F32) if kinds[j] == "w" else vals[j] for j in d_pos]
        _, vjp = jax.vjp(g, *dv0)
        grads = vjp(ctv)
        for g_ref, gv, spec in zip(g_refs, grads, gouts):
            mode = spec[3]
            if mode == "write":
                g_ref[...] = gv.astype(g_ref.dtype)
            else:
                first = (i == 0) if mode == "acc_i" else jnp.logical_and(i == 0, s == 0)

                @pl.when(first)
                def _(g_ref=g_ref, gv=gv):
                    g_ref[...] = gv.astype(g_ref.dtype)

                @pl.when(jnp.logical_not(first))
                def _(g_ref=g_ref, gv=gv):
                    g_ref[...] += gv.astype(g_ref.dtype)

    return pl.pallas_call(
        body, grid=(n_s, n_i), name=name,
        in_specs=([_bspec(b, f, "si") for _, b, f, _ in ins] + [_bspec(b, f, "si") for _, b, f in cts]
                  + [pl.BlockSpec(memory_space=pl.ANY)] * n_sh),
        out_specs=[_bspec(spec[1], spec[2], "si") for spec in gouts],
        out_shape=[jax.ShapeDtypeStruct(spec[0], F32) for spec in gouts],
        input_output_aliases={n_in + n_ct + k: gi for k, (gi, _) in enumerate(shared)},
        compiler_params=pltpu.CompilerParams(vmem_limit_bytes=VMEM_LIMIT,
                                             dimension_semantics=("arbitrary", "arbitrary")),
    )(*[a for a, _, _, _ in ins], *[a for a, _, _ in cts], *[buf for _, buf in shared])


def _row_tile(rows, width, itemsize=4, budget=2 * 1024 * 1024, mult=8):
    best = mult
    for t in range(mult, rows + 1, mult):
        if rows % t == 0 and t * width * itemsize <= budget:
            best = t
    return best


def add_n(name, terms, shape):
    rows, cols = shape
    tr = _row_tile(rows, cols)

    def body(*refs):
        acc = refs[0][...]
        for r in refs[1:-1]:
            acc = acc + r[...]
        refs[-1][...] = acc

    specs = []
    for _, lead in terms:
        specs.append(pl.BlockSpec((None,) * len(lead) + (tr, cols), functools.partial(lambda i, lead: (*lead, i, 0), lead=lead)))
    return pl.pallas_call(
        body, grid=(rows // tr,), name=name, in_specs=specs,
        out_specs=pl.BlockSpec((tr, cols), lambda i: (i, 0)),
        out_shape=jax.ShapeDtypeStruct((rows, cols), F32),
    )(*[a for a, _ in terms])


def pre_core(x, nw, win):
    return (mm(_rms(x, nw), win),)


def _split_lanes(y):
    return jnp.stack([y[:, k * LANE:(k + 1) * LANE] for k in range(y.shape[1] // LANE)], axis=0)


def _join_lanes(y3):
    return jnp.concatenate([y3[k] for k in range(y3.shape[0])], axis=1)


def s5_pre_core(u, b_re0, b_re1, b_im0, b_im1):
    u0, u1 = u[:, :BW // 2], u[:, BW // 2:]
    re = jnp.concatenate([mm(u0, b_re0), mm(u1, b_re1)], axis=1)
    im = jnp.concatenate([mm(u0, b_im0), mm(u1, b_im1)], axis=1)
    return _split_lanes(re), _split_lanes(im)


def mid_core(xr, xi, u, o, g, hs, gc, hmat, c0, c1, d, gluw, glub, hgw):
    half = xr.shape[0] // 2
    xs0 = jnp.concatenate([_join_lanes(xr[:half]), _join_lanes(xi[:half])], axis=1)
    xs1 = jnp.concatenate([_join_lanes(xr[half:]), _join_lanes(xi[half:])], axis=1)
    y = jnp.concatenate([mm(xs0, c0), mm(xs1, c1)], axis=1) + d * u
    z = jax.nn.gelu(y)
    ya = z * jax.nn.sigmoid(mm(z, gluw) + glub)
    ms = mm_exact_r(o * o, hmat)
    yb = o * lax.rsqrt(ms + EPS) * hgw * jax.nn.silu(g)
    yc = hs * jax.nn.gelu(gc)
    return ya, yb, yc


def _sub(w, n):
    return WP(w.w[n], w.p[n]) if isinstance(w, WP) else w[n]


def merge_core(ya, yb, yc, g0, g1, g2, g3, g4, g5, p, wout):
    gate = lambda a, b: jax.nn.sigmoid(jnp.concatenate([a, b], axis=1))
    m = gate(g0, g1) * mm(ya, _sub(p, 0)) + gate(g2, g3) * mm(yb, _sub(p, 1)) + gate(g4, g5) * mm(yc, _sub(p, 2))
    return (mm(m, wout),)


def gates_core(xc, wa, ba, wx, bx, lam):
    r = jax.nn.sigmoid(mm(xc, wa) + ba)
    i = jax.nn.sigmoid(mm(xc, wx) + bx)
    log_a = -RG_C * jax.nn.softplus(-lam) * r
    a = jnp.exp(log_a)
    b = jnp.sqrt(-_expm1(2.0 * log_a)) * (i * xc)
    return a, b


def _seg_rows(ref, k, j, n):
    rows = pl.ds(pl.multiple_of(j * NSEG, NSEG), NSEG)
    if k is None:
        return ref[rows, :]
    return ref[k, rows, :]


def _seg_store(ref, k, j, n, val):
    rows = pl.ds(pl.multiple_of(j * NSEG, NSEG), NSEG)
    if k is None:
        ref[rows, :] = val
    else:
        ref[k, rows, :] = val


def _seg_carries(er, ei, pr, pi, reverse):
    rows = lax.broadcasted_iota(jnp.int32, er.shape, 0)
    cr = jnp.zeros_like(er)
    ci = None if ei is None else jnp.zeros_like(er)
    order = range(NSEG - 2, -1, -1) if reverse else range(1, NSEG)
    shift = NSEG - 1 if reverse else 1
    for s in order:
        if ei is None:
            tr = er + pr * cr
            cr = jnp.where(rows == s, pltpu.roll(tr, shift, 0), cr)
        else:
            tr = er + pr * cr - pi * ci
            ti = ei + pr * ci + pi * cr
            cr = jnp.where(rows == s, pltpu.roll(tr, shift, 0), cr)
            ci = jnp.where(rows == s, pltpu.roll(ti, shift, 0), ci)
    return cr, ci


def _cpow(ar, ai, n):
    out = None
    while n:
        if n & 1:
            out = (ar, ai) if out is None else (out[0] * ar - out[1] * ai, out[0] * ai + out[1] * ar)
        ar, ai = ar * ar - ai * ai, 2.0 * ar * ai
        n >>= 1
    return out


S5_K = 2


def s5_scan_fwd(bur, bui, ar, ai, L):
    n = L // NSEG
    nb = S5_N // LANE
    K = S5_K

    def body(br_ref, bi_ref, ar_ref, ai_ref, xr_ref, xi_ref):
        zero = jnp.zeros((NSEG, LANE), F32)
        A = [(jnp.broadcast_to(ar_ref[k], (NSEG, LANE)), jnp.broadcast_to(ai_ref[k], (NSEG, LANE))) for k in range(K)]

        def p1(j, st):
            new = []
            for k in range(K):
                sr, si = st[k]
                a_r, a_i = A[k]
                nr = a_r * sr - a_i * si + _seg_rows(br_ref, k, j, n)
                ni = a_r * si + a_i * sr + _seg_rows(bi_ref, k, j, n)
                _seg_store(xr_ref, k, j, n, nr)
                _seg_store(xi_ref, k, j, n, ni)
                new.append((nr, ni))
            return tuple(new)

        st = lax.fori_loop(0, n, p1, tuple((zero, zero) for _ in range(K)))
        C = [_seg_carries(st[k][0], st[k][1], *_cpow(*A[k], n), False) for k in range(K)]

        def p2(j, st):
            new = []
            for k in range(K):
                pr, pi = st[k]
                a_r, a_i = A[k]
                pr, pi = a_r * pr - a_i * pi, a_r * pi + a_i * pr
                cr, ci = C[k]
                _seg_store(xr_ref, k, j, n, _seg_rows(xr_ref, k, j, n) + pr * cr - pi * ci)
                _seg_store(xi_ref, k, j, n, _seg_rows(xi_ref, k, j, n) + pr * ci + pi * cr)
                new.append((pr, pi))
            return tuple(new)

        lax.fori_loop(0, n, p2, tuple((zero + 1.0, zero) for _ in range(K)))

    blk = pl.BlockSpec((K, L, LANE), lambda g: (g, 0, 0))
    ablk = pl.BlockSpec((K, 1, LANE), lambda g: (g, 0, 0))
    return pl.pallas_call(
        body, grid=(nb // K,), name="s5_scan_fwd",
        in_specs=[blk, blk, ablk, ablk], out_specs=[blk, blk],
        out_shape=[jax.ShapeDtypeStruct((nb, L, LANE), F32)] * 2,
        compiler_params=pltpu.CompilerParams(vmem_limit_bytes=VMEM_LIMIT),
    )(bur, bui, ar, ai)


def s5_scan_bwd(dxr, dxi, xr, xi, ar, ai, L):
    n = L // NSEG
    nb = S5_N // LANE
    K = S5_K

    def body(dr_ref, di_ref, xr_ref, xi_ref, ar_ref, ai_ref, gr_ref, gi_ref, dar_ref, dai_ref):
        zero = jnp.zeros((NSEG, LANE), F32)
        rows = lax.broadcasted_iota(jnp.int32, (NSEG, LANE), 0)
        A = [(jnp.broadcast_to(ar_ref[k], (NSEG, LANE)), -jnp.broadcast_to(ai_ref[k], (NSEG, LANE))) for k in range(K)]

        def p1(jj, st):
            j = n - 1 - jj
            new = []
            for k in range(K):
                sr, si = st[k]
                a_r, a_i = A[k]
                nr = a_r * sr - a_i * si + _seg_rows(dr_ref, k, j, n)
                ni = a_r * si + a_i * sr + _seg_rows(di_ref, k, j, n)
                _seg_store(gr_ref, k, j, n, nr)
                _seg_store(gi_ref, k, j, n, ni)
                new.append((nr, ni))
            return tuple(new)

        st = lax.fori_loop(0, n, p1, tuple((zero, zero) for _ in range(K)))
        C = [_seg_carries(st[k][0], st[k][1], *_cpow(*A[k], n), True) for k in range(K)]
        xb = [(jnp.where(rows == 0, 0.0, pltpu.roll(_seg_rows(xr_ref, k, n - 1, n), 1, 0)),
               jnp.where(rows == 0, 0.0, pltpu.roll(_seg_rows(xi_ref, k, n - 1, n), 1, 0))) for k in range(K)]

        def p2(jj, st):
            j = n - 1 - jj
            jp = jnp.maximum(j - 1, 0)
            new = []
            for k in range(K):
                pr, pi, acr, aci = st[k]
                a_r, a_i = A[k]
                pr, pi = a_r * pr - a_i * pi, a_r * pi + a_i * pr
                cr, ci = C[k]
                g_r = _seg_rows(gr_ref, k, j, n) + pr * cr - pi * ci
                g_i = _seg_rows(gi_ref, k, j, n) + pr * ci + pi * cr
                _seg_store(gr_ref, k, j, n, g_r)
                _seg_store(gi_ref, k, j, n, g_i)
                xpr = jnp.where(j == 0, xb[k][0], _seg_rows(xr_ref, k, jp, n))
                xpi = jnp.where(j == 0, xb[k][1], _seg_rows(xi_ref, k, jp, n))
                new.append((pr, pi, acr + g_r * xpr + g_i * xpi, aci + g_i * xpr - g_r * xpi))
            return tuple(new)

        st = lax.fori_loop(0, n, p2, tuple((zero + 1.0, zero, zero, zero) for _ in range(K)))
        for k in range(K):
            dar_ref[k] = jnp.sum(st[k][2], axis=0, keepdims=True)
            dai_ref[k] = jnp.sum(st[k][3], axis=0, keepdims=True)

    blk = pl.BlockSpec((K, L, LANE), lambda g: (g, 0, 0))
    ablk = pl.BlockSpec((K, 1, LANE), lambda g: (g, 0, 0))
    return pl.pallas_call(
        body, grid=(nb // K,), name="s5_scan_bwd",
        in_specs=[blk, blk, blk, blk, ablk, ablk], out_specs=[blk, blk, ablk, ablk],
        out_shape=[jax.ShapeDtypeStruct((nb, L, LANE), F32)] * 2 + [jax.ShapeDtypeStruct((nb, 1, LANE), F32)] * 2,
        compiler_params=pltpu.CompilerParams(vmem_limit_bytes=VMEM_LIMIT),
    )(dxr, dxi, xr, xi, ar, ai)


def rg_scan_fwd(a, b, L):
    n = L // NSEG

    def body(a_ref, b_ref, h_ref):
        zero = jnp.zeros((NSEG, LANE), F32)

        def p1(j, st):
            h, p = st
            aj = _seg_rows(a_ref, None, j, n)
            h = aj * h + _seg_rows(b_ref, None, j, n)
            _seg_store(h_ref, None, j, n, h)
            return h, aj * p

        e, pe = lax.fori_loop(0, n, p1, (zero, zero + 1.0))
        c, _ = _seg_carries(e, None, pe, None, False)

        def p2(j, p):
            p = _seg_rows(a_ref, None, j, n) * p
            _seg_store(h_ref, None, j, n, _seg_rows(h_ref, None, j, n) + p * c)
            return p

        lax.fori_loop(0, n, p2, zero + 1.0)

    blk = pl.BlockSpec((L, LANE), lambda g: (0, g))
    return pl.pallas_call(
        body, grid=(BW // LANE,), name="rg_scan_fwd", in_specs=[blk, blk], out_specs=blk,
        out_shape=jax.ShapeDtypeStruct((L, BW), F32),
        compiler_params=pltpu.CompilerParams(vmem_limit_bytes=VMEM_LIMIT),
    )(a, b)


def rg_scan_bwd(a, h, dh, L):
    n = L // NSEG

    def body(a_ref, h_ref, dh_ref, da_ref, db_ref):
        zero = jnp.zeros((NSEG, LANE), F32)
        rows = lax.broadcasted_iota(jnp.int32, (NSEG, LANE), 0)
        a_edge = jnp.where(rows == NSEG - 1, 0.0, pltpu.roll(_seg_rows(a_ref, None, 0, n), NSEG - 1, 0))
        h_edge = jnp.where(rows == 0, 0.0, pltpu.roll(_seg_rows(h_ref, None, n - 1, n), 1, 0))

        def mult(j):
            return jnp.where(j == n - 1, a_edge, _seg_rows(a_ref, None, jnp.minimum(j + 1, n - 1), n))

        def p1(jj, st):
            j = n - 1 - jj
            g, p = st
            m = mult(j)
            g = m * g + _seg_rows(dh_ref, None, j, n)
            _seg_store(db_ref, None, j, n, g)
            return g, m * p

        e, pe = lax.fori_loop(0, n, p1, (zero, zero + 1.0))
        c, _ = _seg_carries(e, None, pe, None, True)

        def p2(jj, p):
            j = n - 1 - jj
            p = mult(j) * p
            g = _seg_rows(db_ref, None, j, n) + p * c
            _seg_store(db_ref, None, j, n, g)
            hp = jnp.where(j == 0, h_edge, _seg_rows(h_ref, None, jnp.maximum(j - 1, 0), n))
            _seg_store(da_ref, None, j, n, g * hp)
            return p

        lax.fori_loop(0, n, p2, zero + 1.0)

    blk = pl.BlockSpec((L, LANE), lambda g: (0, g))
    return pl.pallas_call(
        body, grid=(BW // LANE,), name="rg_scan_bwd", in_specs=[blk, blk, blk], out_specs=[blk, blk],
        out_shape=[jax.ShapeDtypeStruct((L, BW), F32)] * 2,
        compiler_params=pltpu.CompilerParams(vmem_limit_bytes=VMEM_LIMIT),
    )(a, h, dh)


def _hg_consts(C):
    t = lax.broadcasted_iota(jnp.int32, (C, C), 0)
    s = lax.broadcasted_iota(jnp.int32, (C, C), 1)
    tril = (s <= t).astype(F32)
    diag = (s == t).astype(F32)
    levels = []
    k = 1
    while (1 << k) <= C:
        m = 1 << (k - 1)
        same = (t >> k) == (s >> k)
        t_right = ((t >> (k - 1)) & 1) == 1
        s_left = ((s >> (k - 1)) & 1) == 0
        mask = jnp.logical_and(same, jnp.logical_and(t_right, s_left)).astype(F32)
        bnd = ((t >> k) << k) + (m - 1)
        levels.append((mask, (s <= bnd).astype(F32)))
        k += 1
    return tril, diag, levels


def hg_chunk(st, q, z, v, lb):
    C = q.shape[0]
    tril, diag, levels = _hg_consts(C)
    sig = jax.nn.sigmoid(z)
    lf = jnp.log(lb + (1.0 - lb) * sig)
    k = (1.0 - lb) * jax.nn.sigmoid(-z)
    qh = jax.nn.silu(q)
    b = mm_exact(tril, lf)
    blast = jnp.sum(lf, axis=0, keepdims=True)
    qe = qh * jnp.exp(b)
    kd = k * jnp.exp(blast - b)
    scaled = []
    for level, (_, sel) in enumerate(levels):
        size = 2 << level
        if size >= NSEG:
            b3 = b.reshape(C // size, size, b.shape[1])
            ref = jnp.broadcast_to(b3[:, size // 2 - 1:size // 2, :], b3.shape).reshape(b.shape)
        else:
            ref = mm_exact(sel, lf)
        scaled.append((qh * jnp.exp(jnp.minimum(b - ref, 0.0)), k * jnp.exp(jnp.minimum(ref - b, 0.0))))
    outs, news = [], []
    for h in range(HG_HEADS):
        sl = slice(h * HG_D, (h + 1) * HG_D)
        st_h = st[h * HG_D:(h + 1) * HG_D, :]
        sc = diag * mma_nt(qh[:, sl], k[:, sl])
        for (mask, _), (qt, kt) in zip(levels, scaled):
            sc = sc + mask * mma_nt(qt[:, sl], kt[:, sl])
        outs.append(mma_nt(qe[:, sl], st_h) + mma_nn(sc, v[:, sl]))
        news.append(st_h * jnp.exp(blast[:, sl]) + mma_tn(v[:, sl], kd[:, sl]))
    return jnp.concatenate(news, axis=0), jnp.concatenate(outs, axis=1)


def hg_fwd(qzv, lb, L):
    C = HG_CHUNK
    nc = L // C

    def body(q_ref, z_ref, v_ref, lb_ref, o_ref, sst_ref, st_ref):
        @pl.when(pl.program_id(0) == 0)
        def _():
            st_ref[...] = jnp.zeros_like(st_ref)

        st = st_ref[...]
        sst_ref[...] = st
        new, o = hg_chunk(st, q_ref[...], z_ref[...], v_ref[...], lb_ref[...])
        st_ref[...] = new
        o_ref[...] = o

    col = lambda cb: pl.BlockSpec((C, BW), functools.partial(lambda c, cb: (c, cb), cb=cb))
    return pl.pallas_call(
        body, grid=(nc,), name="hg_fwd",
        in_specs=[col(0), col(1), col(2), pl.BlockSpec((1, BW), lambda c: (0, 0))],
        out_specs=[pl.BlockSpec((C, BW), lambda c: (c, 0)), pl.BlockSpec((None, BW, HG_D), lambda c: (c, 0, 0))],
        out_shape=[jax.ShapeDtypeStruct((L, BW), F32), jax.ShapeDtypeStruct((nc, BW, HG_D), F32)],
        scratch_shapes=[pltpu.VMEM((BW, HG_D), F32)],
        compiler_params=pltpu.CompilerParams(vmem_limit_bytes=VMEM_LIMIT, dimension_semantics=("arbitrary",)),
    )(qzv, qzv, qzv, lb)


def hg_bwd(qzv, lb, sst, do, L):
    C = HG_CHUNK
    nc = L // C

    def body(q_ref, z_ref, v_ref, lb_ref, sst_ref, do_ref, dq_ref, dz_ref, dv_ref, dlb_ref, dst_ref):
        @pl.when(pl.program_id(0) == 0)
        def _():
            dst_ref[...] = jnp.zeros_like(dst_ref)
            dlb_ref[...] = jnp.zeros_like(dlb_ref)

        _, vjp = jax.vjp(hg_chunk, sst_ref[...], q_ref[...], z_ref[...], v_ref[...], lb_ref[...])
        dst, dq, dz, dv, dlb = vjp((dst_ref[...], do_ref[...]))
        dst_ref[...] = dst
        dq_ref[...] = dq
        dz_ref[...] = dz
        dv_ref[...] = dv
        dlb_ref[...] += dlb

    col = lambda cb: pl.BlockSpec((C, BW), functools.partial(lambda c, cb: (nc - 1 - c, cb), cb=cb))
    rev = pl.BlockSpec((C, BW), lambda c: (nc - 1 - c, 0))
    return pl.pallas_call(
        body, grid=(nc,), name="hg_bwd",
        in_specs=[col(0), col(1), col(2), pl.BlockSpec((1, BW), lambda c: (0, 0)),
                  pl.BlockSpec((None, BW, HG_D), lambda c: (nc - 1 - c, 0, 0)), rev],
        out_specs=[rev, rev, rev, pl.BlockSpec((1, BW), lambda c: (0, 0))],
        out_shape=[jax.ShapeDtypeStruct((L, BW), F32)] * 3 + [jax.ShapeDtypeStruct((1, BW), F32)],
        scratch_shapes=[pltpu.VMEM((BW, HG_D), F32)],
        compiler_params=pltpu.CompilerParams(vmem_limit_bytes=VMEM_LIMIT, dimension_semantics=("arbitrary",)),
    )(qzv, qzv, qzv, lb, sst, do)


def _shift_down(x, d, rows, L):
    if d == 0:
        return x
    wrapped = jnp.where((rows & (NSEG - 1)) == 0, 0.0, pltpu.roll(x, NSEG * d + 1, 0))
    return jnp.where(rows < NSEG * d, wrapped, pltpu.roll(x, NSEG * d, 0))


def _shift_up(x, d, rows, L):
    if d == 0:
        return x
    wrapped = jnp.where((rows & (NSEG - 1)) == NSEG - 1, 0.0, pltpu.roll(x, L - (NSEG * d + 1), 0))
    return jnp.where(rows >= L - NSEG * d, wrapped, pltpu.roll(x, L - NSEG * d, 0))


def conv_fwd(proj, w, b, L):
    def body(x_ref, w_ref, b_ref, o_ref):
        x = x_ref[...]
        rows = lax.broadcasted_iota(jnp.int32, x.shape, 0)
        acc = jnp.broadcast_to(b_ref[...], x.shape)
        for k in range(CONV_W):
            acc = acc + w_ref[pl.ds(k, 1), :] * _shift_down(x, CONV_W - 1 - k, rows, L)
        o_ref[...] = acc

    nl = BW // LANE
    return pl.pallas_call(
        body, grid=(nl,), name="conv_fwd",
        in_specs=[pl.BlockSpec((L, LANE), lambda g: (0, 5 * nl + g)), pl.BlockSpec((CONV_W, LANE), lambda g: (0, g)),
                  pl.BlockSpec((1, LANE), lambda g: (0, g))],
        out_specs=pl.BlockSpec((L, LANE), lambda g: (0, g)),
        out_shape=jax.ShapeDtypeStruct((L, BW), F32),
        compiler_params=pltpu.CompilerParams(vmem_limit_bytes=VMEM_LIMIT),
    )(proj, w, b)


def conv_bwd(proj, w, dxc, L):
    def body(x_ref, w_ref, d_ref, dx_ref, dw_ref, db_ref):
        x, d = x_ref[...], d_ref[...]
        rows = lax.broadcasted_iota(jnp.int32, x.shape, 0)
        acc = jnp.zeros_like(x)
        for k in range(CONV_W):
            acc = acc + w_ref[pl.ds(k, 1), :] * _shift_up(d, CONV_W - 1 - k, rows, L)
            dw_ref[pl.ds(k, 1), :] = jnp.sum(d * _shift_down(x, CONV_W - 1 - k, rows, L), axis=0, keepdims=True)
        dx_ref[...] = acc
        db_ref[...] = jnp.sum(d, axis=0, keepdims=True)

    nl = BW // LANE
    blk = pl.BlockSpec((L, LANE), lambda g: (0, g))
    return pl.pallas_call(
        body, grid=(nl,), name="conv_bwd",
        in_specs=[pl.BlockSpec((L, LANE), lambda g: (0, 5 * nl + g)), pl.BlockSpec((CONV_W, LANE), lambda g: (0, g)), blk],
        out_specs=[blk, pl.BlockSpec((CONV_W, LANE), lambda g: (0, g)), pl.BlockSpec((1, LANE), lambda g: (0, g))],
        out_shape=[jax.ShapeDtypeStruct((L, BW), F32), jax.ShapeDtypeStruct((CONV_W, BW), F32),
                   jax.ShapeDtypeStruct((1, BW), F32)],
        compiler_params=pltpu.CompilerParams(vmem_limit_bytes=VMEM_LIMIT),
    )(proj, w, dxc)


def loss_fwd_bwd(x, fw, target, L, tm):
    def fn(x, fw, t):
        err = jnp.square(_rms(x, fw) - t)
        return jnp.sum(0.5 * jnp.mean(err, axis=-1, keepdims=True), axis=0, keepdims=True)

    def body(x_ref, fw_ref, t_ref, l_ref, dx_ref, dfw_ref):
        i = pl.program_id(0)
        t = t_ref[...]
        val, vjp = jax.vjp(lambda x, fw: fn(x, fw, t), x_ref[...], fw_ref[...])
        dx, dfw = vjp(jnp.ones((1, 1), F32))
        dx_ref[...] = dx

        @pl.when(i == 0)
        def _():
            l_ref[...] = jnp.zeros_like(l_ref)
            dfw_ref[...] = jnp.zeros_like(dfw_ref)

        l_ref[...] += jnp.broadcast_to(val, l_ref.shape)
        dfw_ref[...] += dfw

    row = pl.BlockSpec((tm, D_MODEL), lambda i: (i, 0))
    vec = pl.BlockSpec((1, D_MODEL), lambda i: (0, 0))
    return pl.pallas_call(
        body, grid=(L // tm,), name="loss_fwd_bwd", in_specs=[row, vec, row],
        out_specs=[pl.BlockSpec((1, LANE), lambda i: (0, 0)), row, vec],
        out_shape=[jax.ShapeDtypeStruct((1, LANE), F32), jax.ShapeDtypeStruct((L, D_MODEL), F32),
                   jax.ShapeDtypeStruct((1, D_MODEL), F32)],
        compiler_params=pltpu.CompilerParams(vmem_limit_bytes=VMEM_LIMIT, dimension_semantics=("arbitrary",)),
    )(x, fw, target)


def adamw(w, g, m, v):
    rows, cols = w.shape
    tr = _row_tile(rows, cols, budget=1024 * 1024)
    c1 = 1.0 - ADAM_B1 ** ADAM_STEP
    c2 = 1.0 - ADAM_B2 ** ADAM_STEP

    def body(w_ref, g_ref, m_ref, v_ref, d_ref, nm_ref, nv_ref):
        g = g_ref[...]
        nm = ADAM_B1 * m_ref[...] + (1.0 - ADAM_B1) * g
        nv = ADAM_B2 * v_ref[...] + (1.0 - ADAM_B2) * jnp.square(g)
        d_ref[...] = -ADAM_LR * ((nm / c1) / (jnp.sqrt(nv / c2) + ADAM_EPS) + ADAM_WD * w_ref[...])
        nm_ref[...] = nm
        nv_ref[...] = nv

    blk = pl.BlockSpec((tr, cols), lambda i: (i, 0))
    return pl.pallas_call(
        body, grid=(rows // tr,), name="adamw", in_specs=[blk] * 4, out_specs=[blk] * 3,
        out_shape=[jax.ShapeDtypeStruct((rows, cols), F32)] * 3,
    )(w, g, m, v)


def s5_prep(lam_re, lam_im, log_dt, b_re, b_im, c_re, c_im):
    lr = jnp.minimum(lam_re, -1e-4)
    li = lam_im
    dt = jnp.exp(log_dt)[:, None]
    mag = jnp.exp(lr * dt)
    ar = mag * jnp.cos(li * dt)
    ai = mag * jnp.sin(li * dt)
    den = lr * lr + li * li
    fr = ((ar - 1.0) * lr + ai * li) / den
    fi = (ai * lr - (ar - 1.0) * li) / den
    bbr = fr[..., None] * b_re - fi[..., None] * b_im
    bbi = fr[..., None] * b_im + fi[..., None] * b_re
    hg = S5_GROUPS // 2
    emb_b = lambda bb: _block_diag(bb.transpose(0, 2, 1).reshape(hg * S5_GROUP, S5_STATE), hg)
    emb_c = lambda cc: _block_diag(cc.transpose(0, 2, 1).reshape(hg * S5_STATE, S5_GROUP), hg)
    bsub = jnp.stack([emb_b(bbr[:hg]), emb_b(bbr[hg:]), emb_b(bbi[:hg]), emb_b(bbi[hg:])])
    csub = jnp.stack([jnp.concatenate([emb_c(c_re[:hg]), -emb_c(c_im[:hg])], axis=0),
                      jnp.concatenate([emb_c(c_re[hg:]), -emb_c(c_im[hg:])], axis=0)])
    nb = S5_N // LANE
    return ar.reshape(nb, 1, LANE), ai.reshape(nb, 1, LANE), bsub, csub


def _block_diag(stacked, groups):
    rows, c = stacked.shape
    r = rows // groups
    row_g = jnp.arange(rows)[:, None] // r
    col_g = jnp.arange(groups * c)[None, :] // c
    return jnp.where(row_g == col_g, jnp.tile(stacked, (1, groups)), 0.0)


def rg_prep(w):
    return _block_diag(w.reshape(BW, RG_BLOCK), RG_BLOCKS)


def hg_prep(logits):
    p = jax.nn.softmax(logits, axis=0)
    return jnp.cumsum(p, axis=0) - p[0]


def _head_mean_matrix():
    r = jnp.arange(BW) // HG_D
    return (r[:, None] == r[None, :]).astype(F32) / HG_D


def _to_segment_order(a):
    L = a.shape[0]
    return a.reshape(NSEG, L // NSEG, -1).transpose(1, 0, 2).reshape(a.shape)


def _to_time_order(a):
    L = a.shape[0]
    return a.reshape(L // NSEG, NSEG, -1).transpose(1, 0, 2).reshape(a.shape)


def _const(*idx):
    return lambda s, i: idx


def _rows(cb=0):
    return lambda s, i: (i, cb)


def _sum_parts(name, first, parts, shape):
    return add_n(name, [(first, ())] + [(parts, (s,)) for s in range(NSH)], shape)


def _ffn_weight_specs(l, j):
    F = D_FF // NSH
    one = pl.Buffered(1)
    return [pl.BlockSpec((None, NSH, D_MODEL, F), lambda i: (j, 0, 0, 0), pipeline_mode=one),
            pl.BlockSpec((None, NSH, D_MODEL, F), lambda i: (j, 0, 0, 0), pipeline_mode=one),
            pl.BlockSpec((None, NSH, F, D_MODEL), lambda i: (j, 0, 0, 0), pipeline_mode=one)]


def ffn_fwd(name, x, W, l, j, k, L, tm):
    D, F = D_MODEL, D_FF // NSH

    def body(x_ref, nw_ref, wg_ref, wu_ref, wd_ref, y_ref, g_ref, u_ref):
        x = x_ref[...]
        h = _rms(x, nw_ref[...]).astype(MMT)
        y = x
        for s in range(NSH):
            g = _dg(h, wg_ref[s], 1, 0)
            u = _dg(h, wu_ref[s], 1, 0)
            g_ref[s] = g.astype(g_ref.dtype)
            u_ref[s] = u.astype(u_ref.dtype)
            y = y + 0.5 * _dg((jax.nn.silu(g) * u).astype(MMT), wd_ref[s], 1, 0)
        y_ref[...] = y

    row = pl.BlockSpec((tm, D), lambda i: (i, 0))
    act = pl.BlockSpec((NSH, tm, F), lambda i: (0, i, 0))
    return pl.pallas_call(
        body, grid=(L // tm,), name=name,
        in_specs=[row, pl.BlockSpec((None, None, 1, D), lambda i: (l, k, 0, 0))] + _ffn_weight_specs(l, j),
        out_specs=[row, act, act],
        out_shape=[jax.ShapeDtypeStruct((L, D), F32), jax.ShapeDtypeStruct((NSH, L, F), MMT),
                   jax.ShapeDtypeStruct((NSH, L, F), MMT)],
        compiler_params=pltpu.CompilerParams(vmem_limit_bytes=VMEM_LIMIT, dimension_semantics=("arbitrary",)),
    )(x, W["nw"], W["L"][l]["wg"], W["L"][l]["wu"], W["L"][l]["wd"])


def ffn_bwd(name, x, g, u, dy, W, bufs, l, j, k, L, tm):
    D, F = D_MODEL, D_FF // NSH
    tm = min(TM_WGRAD, L)

    def body(x_ref, nw_ref, dy_ref, g_ref, u_ref, wg_ref, wu_ref, wd_ref, *rest):
        part_ref, dnw_ref, dwg_ref, dwu_ref, dwd_ref = rest[-5:]
        s, i = pl.program_id(0), pl.program_id(1)
        x, nw = x_ref[...], nw_ref[...]
        r = lax.rsqrt(jnp.mean(x * x, axis=-1, keepdims=True) + EPS)
        xhat = x * r
        h = (xhat * nw).astype(MMT)
        half_dy = (0.5 * dy_ref[...]).astype(MMT)
        gs, us = g_ref[...].astype(F32), u_ref[...].astype(F32)
        sig = jax.nn.sigmoid(gs)
        act = gs * sig
        da = _dg(half_dy, wd_ref[...], 1, 1)
        du = (da * act).astype(MMT)
        dg = (da * us * (sig * (1.0 + gs * (1.0 - sig)))).astype(MMT)
        dh = _dg(dg, wg_ref[...], 1, 1) + _dg(du, wu_ref[...], 1, 1)
        dxh = dh * nw
        part_ref[...] = r * (dxh - xhat * jnp.mean(dxh * xhat, axis=-1, keepdims=True))
        grads = (_dg(h, dg, 0, 0), _dg(h, du, 0, 0), _dg((act * us).astype(MMT), half_dy, 0, 0))
        dnw = jnp.sum(dh * xhat, axis=0, keepdims=True)
        first = jnp.logical_and(s == 0, i == 0)
        for ref, val, start in zip((dwg_ref, dwu_ref, dwd_ref, dnw_ref), grads + (dnw,), (i == 0, i == 0, i == 0, first)):
            @pl.when(start)
            def _(ref=ref, val=val):
                ref[...] = val

            @pl.when(jnp.logical_not(start))
            def _(ref=ref, val=val):
                ref[...] += val

    row = pl.BlockSpec((tm, D), lambda s, i: (i, 0))
    act = pl.BlockSpec((None, tm, F), lambda s, i: (s, i, 0))
    wsp = lambda r, c: pl.BlockSpec((None, None, r, c), lambda s, i: (j, s, 0, 0))
    gsp = lambda r, c: pl.BlockSpec((None, None, r, c), lambda s, i: (0, s, 0, 0))
    part, dnw, bufs[("ffn_gate", l, j)], bufs[("ffn_up", l, j)], bufs[("ffn_down", l, j)] = pl.pallas_call(
        body, grid=(NSH, L // tm), name=name,
        in_specs=[row, pl.BlockSpec((None, None, 1, D), lambda s, i: (l, k, 0, 0)), row, act, act,
                  wsp(D, F), wsp(D, F), wsp(F, D)],
        out_specs=[pl.BlockSpec((None, tm, D), lambda s, i: (s, i, 0)), pl.BlockSpec((1, D), lambda s, i: (0, 0)),
                   gsp(D, F), gsp(D, F), gsp(F, D)],
        out_shape=[jax.ShapeDtypeStruct((NSH, L, D), F32), jax.ShapeDtypeStruct((1, D), F32)]
        + [jax.ShapeDtypeStruct((1, NSH, D, F), F32)] * 2 + [jax.ShapeDtypeStruct((1, NSH, F, D), F32)],
        compiler_params=pltpu.CompilerParams(vmem_limit_bytes=VMEM_LIMIT, dimension_semantics=("arbitrary", "arbitrary")),
    )(x, W["nw"], dy, g, u, W["L"][l]["wg"], W["L"][l]["wu"], W["L"][l]["wd"])
    return _sum_parts(name + "_dx", dy, part, (L, D)), dnw


def layer_fwd(l, x0, W, P, L, tm):
    D = D_MODEL
    tmm = tm
    tm = min(TM_FWD, L)
    n_i = L // tm
    x1, g0, u0 = ffn_fwd(f"ffn_fwd_{l}0", x0, W, l, 0, 0, L, tm)
    proj = tile_fwd(
        lambda x, nw, win, s: pre_core(x, nw, win), f"pre_fwd_{l}", n_i, NSH,
        [(x1, (tm, D), _rows()), (W["nw"], (None, None, 1, D), _const(l, 1, 0, 0)),
         (W["L"][l]["win"], (None, D, IN_TOTAL // NSH), lambda s, i: (s, 0, 0))],
        [((L, IN_TOTAL), F32, (tm, IN_TOTAL // NSH), lambda s, i: (i, s), False)], s_outer=True)[0]
    nb = S5_N // LANE
    blk3 = lambda s, i: (0, i, 0)
    bur, bui = tile_fwd(
        lambda *a: s5_pre_core(*a[:-1]), f"s5pre_fwd_{l}", n_i, 1,
        [(proj, (tm, BW), _rows(0))] + [(P["bsub"], (None, None, BW // 2, S5_N // 2), _const(l, q, 0, 0)) for q in range(4)],
        [((nb, L, LANE), F32, (nb, tm, LANE), blk3, False)] * 2)
    xr, xi = s5_scan_fwd(bur, bui, P["ar"][l], P["ai"][l], L)
    qzv = _to_time_order(proj[:, BW:4 * BW])
    o_t, sst = hg_fwd(qzv, P["lb"][l], L)
    o = _to_segment_order(o_t)
    xc = conv_fwd(proj, W["convw"][l], P["convb"][l], L)
    vec = (None, 1, BW)
    a, b = tile_fwd(
        lambda xc, wa, ba, wx, bx, lam, s: gates_core(xc, wa, ba, wx, bx, lam), f"gates_fwd_{l}", n_i, 1,
        [(xc, (tm, BW), _rows()), (P["wa"], (None, BW, BW), _const(l, 0, 0)), (P["ba"], vec, _const(l, 0, 0)),
         (P["wx"], (None, BW, BW), _const(l, 0, 0)), (P["bx"], vec, _const(l, 0, 0)), (P["lam"], vec, _const(l, 0, 0))],
        [((L, BW), F32, (tm, BW), _rows(), False)] * 2)
    hs = rg_scan_fwd(a, b, L)
    ya, yb, yc = tile_fwd(
        lambda *a: mid_core(*a[:-1]), f"mid_fwd_{l}", L // tmm, 1,
        [(xr, (nb, tmm, LANE), blk3), (xi, (nb, tmm, LANE), blk3), (proj, (tmm, BW), _rows(0)), (o, (tmm, BW), _rows()),
         (proj, (tmm, BW), _rows(4)), (hs, (tmm, BW), _rows()), (proj, (tmm, BW), _rows(6)),
         (P["hmat"], (BW, BW), _const(0, 0)), (P["csub"], (None, None, S5_N, BW // 2), _const(l, 0, 0, 0)),
         (P["csub"], (None, None, S5_N, BW // 2), _const(l, 1, 0, 0)), (P["d"], vec, _const(l, 0, 0)),
         (W["L"][l]["gluw"], (BW, BW), _const(0, 0)), (P["glub"], vec, _const(l, 0, 0)), (P["hgw"], vec, _const(l, 0, 0))],
        [((L, BW), F32, (tmm, BW), _rows(), False)] * 3)
    x2 = tile_fwd(
        lambda x, *rest: (x + merge_core(*rest[:-1])[0],), f"merge_fwd_{l}", n_i, 1,
        [(x1, (tm, D), _rows()), (ya, (tm, BW), _rows()), (yb, (tm, BW), _rows()), (yc, (tm, BW), _rows())]
        + [(proj, (tm, BW), _rows(7 + k)) for k in range(6)]
        + [(W["L"][l]["pfull"], (3, BW, D), _const(0, 0, 0)), (W["L"][l]["woutfull"], (D, D), _const(0, 0))],
        [((L, D), F32, (tm, D), _rows(), False)])[0]
    x3, g1, u1 = ffn_fwd(f"ffn_fwd_{l}1", x2, W, l, 1, 2, L, tm)
    saved = dict(x0=x0, x1=x1, x2=x2, proj=proj, xr=xr, xi=xi, o=o, sst=sst, xc=xc, a=a, hs=hs, ya=ya, yb=yb, yc=yc,
                 qzv=qzv, g0=g0, u0=u0, g1=g1, u1=u1)
    return x3, saved


def layer_bwd(l, dx3, sv, W, P, bufs, L, tm, ready=lambda l, group: None):
    D = D_MODEL
    n_i = L // tm
    nb = S5_N // LANE
    dq = D // NSH
    vec = (None, 1, BW)
    vout = ((1, BW), (1, BW), _const(0, 0), "acc_all")
    blk3 = lambda s, i: (0, i, 0)
    small = {}
    proj = sv["proj"]

    dx2, dnw2 = ffn_bwd(f"ffn_bwd_{l}1", sv["x2"], sv["g1"], sv["u1"], dx3, W, bufs, l, 1, 2, L, tm)
    ready(l, "ffn1")

    rw256 = ((L, BW), (tm, BW), _rows(), "write")
    res = tile_bwd(
        merge_core, f"merge_bwd_{l}", n_i, 1,
        [(sv["ya"], (tm, BW), _rows(), "r"), (sv["yb"], (tm, BW), _rows(), "r"), (sv["yc"], (tm, BW), _rows(), "r")]
        + [(proj, (tm, BW), _rows(7 + k), "r") for k in range(6)]
        + [(W["L"][l]["pfull"], (3, BW, D), _const(0, 0, 0), "w"), (W["L"][l]["woutfull"], (D, D), _const(0, 0), "w")],
        [(dx2, (tm, D), _rows())],
        [rw256] * 9
        + [((3, BW, D), (3, BW, D), _const(0, 0, 0), "acc_all"), ((D, D), (D, D), _const(0, 0), "acc_all")])
    dya, dyb, dyc = res[:3]
    dgm = res[3:9]
    bufs[("branch_proj", l)], bufs[("w_out", l)] = res[9:]
    ready(l, "merge")

    tmm = tm
    rw = ((L, BW), (tmm, BW), _rows(), "write")
    xw = ((nb, L, LANE), (nb, tmm, LANE), blk3, "write")
    res = tile_bwd(
        mid_core, f"mid_bwd_{l}", L // tmm, 1,
        [(sv["xr"], (nb, tmm, LANE), blk3, "r"), (sv["xi"], (nb, tmm, LANE), blk3, "r"), (proj, (tmm, BW), _rows(0), "r"),
         (sv["o"], (tmm, BW), _rows(), "r"), (proj, (tmm, BW), _rows(4), "r"), (sv["hs"], (tmm, BW), _rows(), "r"),
         (proj, (tmm, BW), _rows(6), "r"), (P["hmat"], (BW, BW), _const(0, 0), "c"),
         (P["csub"], (None, None, S5_N, BW // 2), _const(l, 0, 0, 0), "w"),
         (P["csub"], (None, None, S5_N, BW // 2), _const(l, 1, 0, 0), "w"), (P["d"], vec, _const(l, 0, 0), "p"),
         (W["L"][l]["gluw"], (BW, BW), _const(0, 0), "w"), (P["glub"], vec, _const(l, 0, 0), "p"),
         (P["hgw"], vec, _const(l, 0, 0), "p")],
        [(dya, (tmm, BW), _rows()), (dyb, (tmm, BW), _rows()), (dyc, (tmm, BW), _rows())],
        [xw, xw, rw, rw, rw, rw, rw,
         ((DEPTH, S5_N, BW // 2), (None, S5_N, BW // 2), _const(l, 0, 0), "acc_all", bufs.get("csub0")),
         ((DEPTH, S5_N, BW // 2), (None, S5_N, BW // 2), _const(l, 0, 0), "acc_all", bufs.get("csub1")), vout,
         ((BW, BW), (BW, BW), _const(0, 0), "acc_all"), vout, vout])
    dxr, dxi, du_skip, do, dg_b, dhs, dgate_c, bufs["csub0"], bufs["csub1"], dd, bufs[("s5_glu_w", l)], dglub, dhgw = res
    small["s5_d"], small["s5_glu_b"], small["hg_norm_w"] = dd[0], dglub[0], dhgw[0]
    ready(l, "mid")

    da, db = rg_scan_bwd(sv["a"], sv["hs"], dhs, L)
    wmat = lambda key: ((DEPTH, BW, BW), (None, BW, BW), _const(l, 0, 0), "acc_all", bufs.get(key))
    res = tile_bwd(
        gates_core, f"gates_bwd_{l}", n_i, 1,
        [(sv["xc"], (tm, BW), _rows(), "r"), (P["wa"], (None, BW, BW), _const(l, 0, 0), "w"), (P["ba"], vec, _const(l, 0, 0), "p"),
         (P["wx"], (None, BW, BW), _const(l, 0, 0), "w"), (P["bx"], vec, _const(l, 0, 0), "p"), (P["lam"], vec, _const(l, 0, 0), "p")],
        [(da, (tm, BW), _rows()), (db, (tm, BW), _rows())],
        [((L, BW), (tm, BW), _rows(), "write"), wmat("wa"), vout, wmat("wx"), vout, vout])
    dxc, bufs["wa"], dba, bufs["wx"], dbx, dlam = res
    small["rg_ba"], small["rg_bx"], small["rg_lambda"] = dba[0], dbx[0], dlam[0]
    dx_c, dconvw, dconvb = conv_bwd(proj, W["convw"][l], dxc, L)
    small["rg_conv_w"], small["rg_conv_b"] = dconvw, dconvb[0]

    dq_b, dz_b, dv_b, dlb = hg_bwd(sv["qzv"], P["lb"][l], sv["sst"], _to_time_order(do), L)
    dq_b, dz_b, dv_b = [_to_segment_order(a) for a in (dq_b, dz_b, dv_b)]

    gr, gi, dar, dai = s5_scan_bwd(dxr, dxi, sv["xr"], sv["xi"], P["ar"][l], P["ai"][l], L)
    bblk = (None, None, BW // 2, S5_N // 2)
    res = tile_bwd(
        s5_pre_core, f"s5pre_bwd_{l}", n_i, 1,
        [(proj, (tm, BW), _rows(0), "r")] + [(P["bsub"], bblk, _const(l, q, 0, 0), "w") for q in range(4)],
        [(gr, (nb, tm, LANE), blk3), (gi, (nb, tm, LANE), blk3)],
        [((L, BW), (tm, BW), _rows(), "write")]
        + [((DEPTH, BW // 2, S5_N // 2), bblk[1:], _const(l, 0, 0), "acc_all", bufs.get(f"bsub{q}")) for q in range(4)])
    du_pre = res[0]
    for q in range(4):
        bufs[f"bsub{q}"] = res[1 + q]
    du_a = add_n(f"du_a_{l}", [(du_skip, ()), (du_pre, ())], (L, BW))
    prep_ct = dict(dar=dar, dai=dai, dlb=dlb)

    pieces = [du_a, dq_b, dz_b, dv_b, dg_b, dx_c, dgate_c, *dgm]
    per_piece, per_shard = BW // LANE, IN_TOTAL // NSH // LANE
    dx1, dnw1 = dx2, []
    tmw = min(TM_WGRAD, L)
    pre_and_x = lambda x, nw, win: (pre_core(x, nw, win)[0], x)
    for s in range(NSH):
        groups = [(pieces[g // per_piece], (tmw, LANE), _rows(g % per_piece))
                  for g in range(s * per_shard, (s + 1) * per_shard)]
        dx1, dnw_s, bufs[("w_in", l)] = tile_bwd(
            pre_and_x, f"pre_bwd_{l}{s}", L // tmw, 1,
            [(sv["x1"], (tmw, D), _rows(), "r"), (W["nw"], (None, None, 1, D), _const(l, 1, 0, 0), "p"),
             (W["L"][l]["win"], (None, D, IN_TOTAL // NSH), _const(s, 0, 0), "w")],
            [groups, (dx1, (tmw, D), _rows())],
            [((L, D), (tmw, D), _rows(), "write"),
             ((1, D), (1, D), _const(0, 0), "acc_all"),
             ((1, NSH, D, IN_TOTAL // NSH), (None, None, D, IN_TOTAL // NSH), _const(0, s, 0, 0), "acc_all",
              bufs.get(("w_in", l)))])
        dnw1.append(dnw_s)
    dnw1 = (dnw1[0] + dnw1[1]) + (dnw1[2] + dnw1[3])
    ready(l, "pre")

    dx0, dnw0 = ffn_bwd(f"ffn_bwd_{l}0", sv["x0"], sv["g0"], sv["u0"], dx1, W, bufs, l, 0, 0, L, tm)
    ready(l, "ffn0")
    small["norm_w"] = jnp.concatenate([dnw0, dnw1, dnw2], axis=0)
    return dx0, small, prep_ct


SMALL_RAW = ("s5_lambda_re", "s5_lambda_im", "s5_log_dt", "s5_b_re", "s5_b_im", "s5_c_re", "s5_c_im", "s5_d", "s5_glu_b",
             "hg_lb_logits", "hg_norm_w", "rg_conv_b", "rg_wa", "rg_ba", "rg_wx", "rg_bx", "rg_lambda", "final_norm_w")
DEPTH = 2


def local_step(x, target, W, raw, layer_weights=None, layer_grads=None):
    L = x.shape[0]
    tm = min(256, L)
    col = lambda v: v.reshape(DEPTH, 1, BW)
    (ar, ai, bsub, csub), s5_vjp = jax.vjp(jax.vmap(s5_prep), *[raw[k] for k in SMALL_RAW[:7]])
    (wa, wx), rg_vjp = jax.vjp(lambda a, b: (jax.vmap(rg_prep)(a), jax.vmap(rg_prep)(b)), raw["rg_wa"], raw["rg_wx"])
    lb, hg_vjp = jax.vjp(hg_prep, raw["hg_lb_logits"])
    P = dict(
        ar=[ar[l] for l in range(DEPTH)], ai=[ai[l] for l in range(DEPTH)],
        bsub=bsub.astype(MMT), csub=csub.astype(MMT), wa=wa.astype(MMT), wx=wx.astype(MMT),
        lb=[lb[l].reshape(1, BW) for l in range(DEPTH)], convb=[raw["rg_conv_b"][l].reshape(1, BW) for l in range(DEPTH)],
        ba=col(raw["rg_ba"]), bx=col(raw["rg_bx"]), lam=col(raw["rg_lambda"]), d=col(raw["s5_d"]),
        glub=col(raw["s5_glu_b"]), hgw=col(raw["hg_norm_w"]), hmat=_head_mean_matrix())

    saved = []
    h = _to_segment_order(x)
    for l in range(DEPTH):
        if layer_weights is not None:
            W["L"][l], h = layer_weights(l, h)
        h, sv = layer_fwd(l, h, W, P, L, tm)
        saved.append(sv)
    loss, dh, dfw = loss_fwd_bwd(h, raw["final_norm_w"].reshape(1, D_MODEL), _to_segment_order(target), L, tm)

    big, per_layer, prep_cts = {}, [None] * DEPTH, [None] * DEPTH
    ready = (lambda l, group: None) if layer_grads is None else (lambda l, group: layer_grads(l, group, big))
    for l in reversed(range(DEPTH)):
        dh, sm, pc = layer_bwd(l, dh, saved[l], W, P, big, L, tm, ready)
        per_layer[l], prep_cts[l] = sm, pc
    dh = _to_time_order(dh)

    small = {k: jnp.stack([per_layer[l][k] for l in range(DEPTH)]) for k in per_layer[0]}
    both = lambda k: jnp.stack([prep_cts[l][k] for l in range(DEPTH)])
    dbsub = jnp.stack([big.pop(f"bsub{q}") for q in range(4)], axis=1)
    dcsub = jnp.stack([big.pop("csub0"), big.pop("csub1")], axis=1)
    s5_g = s5_vjp((both("dar"), both("dai"), dbsub, dcsub))
    small.update(zip(SMALL_RAW[:7], s5_g))
    small["rg_wa"], small["rg_wx"] = rg_vjp((big.pop("wa"), big.pop("wx")))
    (small["hg_lb_logits"],) = hg_vjp(jnp.concatenate([prep_cts[l]["dlb"] for l in range(DEPTH)], axis=0))
    small["final_norm_w"] = dfw[0]
    return loss, dh, big, small


ANY = pl.BlockSpec(memory_space=pl.ANY)


def _place():
    x, y, c = lax.axis_index("x"), lax.axis_index("y"), lax.axis_index("c")
    chips = [(1 - x, y), (x, 1 - y), (1 - x, 1 - y)]
    return x, y, c, chips


def _remote(src, dst, send, recv, k, to):
    return pltpu.make_async_remote_copy(src_ref=src, dst_ref=dst, send_sem=send.at[k], recv_sem=recv.at[k],
                                        device_id=to, device_id_type=MESH)


def _comm_call(body, name, ins, out_shapes, n_sem, n_loc):
    return pl.pallas_call(
        body, name=name, in_specs=[ANY] * len(ins), out_specs=[ANY] * len(out_shapes), out_shape=out_shapes,
        scratch_shapes=[pltpu.SemaphoreType.DMA((n_sem,)), pltpu.SemaphoreType.DMA((n_sem,)),
                        pltpu.SemaphoreType.DMA((max(n_loc, 1),))],
    )(*ins)


def gather_shards(name, shards):
    n = len(shards)
    per = 8

    def body(*refs):
        ins, outs = refs[:n], refs[n:2 * n]
        send, recv, _ = refs[2 * n:]
        x, y, c, chips = _place()
        me = 2 * x + y
        sib = (x, y, 1 - c)
        sends = []
        for w in range(n):
            for j, (cx, cy) in enumerate(chips):
                cp = _remote(ins[w].at[c], outs[w].at[c, me], send, recv, per * w + j, (cx, cy, c))
                cp.start()
                sends.append(cp)
        for w in range(n):
            for l in range(2):
                cp = _remote(ins[w].at[l], outs[w].at[l, me], send, recv, per * w + 6 + l, sib)
                cp.start()
                sends.append(cp)
        for w in range(n):
            for j, (cx, cy) in enumerate(chips):
                theirs = outs[w].at[c, 2 * cx + cy]
                _remote(ins[w].at[c], theirs, send, recv, per * w + j, (cx, cy, c)).wait_recv()
                cp = _remote(theirs, theirs, send, recv, per * w + 3 + j, sib)
                cp.start()
                sends.append(cp)
        for w in range(n):
            for j, (cx, cy) in enumerate(chips):
                dst = outs[w].at[1 - c, 2 * cx + cy]
                _remote(dst, dst, send, recv, per * w + 3 + j, sib).wait_recv()
            for l in range(2):
                dst = outs[w].at[l, me]
                _remote(dst, dst, send, recv, per * w + 6 + l, sib).wait_recv()
        for cp in sends:
            cp.wait_send()

    shapes = [jax.ShapeDtypeStruct((2, NSH) + s.shape[1:], s.dtype) for s in shards]
    return _comm_call(body, name, shards, shapes, per * n, 0)


def exchange_halves(name, grads, ranges):
    n = len(grads)

    def body(*refs):
        ins, outs = refs[:n], refs[n:2 * n]
        send, recv, _ = refs[2 * n:]
        x, y, c, _chips = _place()
        cps = []
        for w in range(n):
            h = grads[w].shape[2] // 2
            p0, np_ = ranges[w]
            cp = _remote(ins[w].at[pl.ds(p0, np_), :, pl.ds((1 - c) * h, h)], outs[w], send, recv, w, (x, y, 1 - c))
            cp.start()
            cps.append(cp)
        for cp in cps:
            cp.wait()

    shapes = [jax.ShapeDtypeStruct((r[1], NSH, g.shape[2] // 2, g.shape[3]), g.dtype) for g, r in zip(grads, ranges)]
    return _comm_call(body, name, grads, shapes, n, 0)


def share_halves(name, pieces):
    n = len(pieces)

    def body(*refs):
        ins, outs = refs[:n], refs[n:2 * n]
        send, recv, _ = refs[2 * n:]
        x, y, c, _chips = _place()
        cps = []
        for w in range(n):
            cp = _remote(ins[w], outs[w], send, recv, w, (x, y, 1 - c))
            cp.start()
            cps.append(cp)
        for cp in cps:
            cp.wait()

    return _comm_call(body, name, pieces, [jax.ShapeDtypeStruct(p.shape, p.dtype) for p in pieces], n, 0)


def add_own_half(name, g, ra, c, wire, b0):
    nblk, h, cols = ra.shape
    tr = _row_tile(h, cols, mult=16)
    nt = h // tr

    def body(c_ref, g_ref, r_ref, o_ref):
        o_ref[...] = (g_ref[...] + r_ref[...]).astype(o_ref.dtype)

    blk = (None, tr, cols)
    return pl.pallas_call(
        body, name=name,
        grid_spec=pltpu.PrefetchScalarGridSpec(
            num_scalar_prefetch=1, grid=(nblk, nt),
            in_specs=[pl.BlockSpec(blk, lambda s, i, c_ref: (b0 + s, c_ref[0] * nt + i, 0)), pl.BlockSpec(blk, lambda s, i, c_ref: (s, i, 0))],
            out_specs=pl.BlockSpec(blk, lambda s, i, c_ref: (s, i, 0))),
        out_shape=jax.ShapeDtypeStruct(ra.shape, wire),
    )(c.reshape(1), g, ra)


def add_chips(name, hb, rb, me):
    npc, _, h, cols = hb.shape
    tr = _row_tile(h, cols, mult=16)

    def body(me_ref, h_ref, r0, r1, r2, o_ref):
        f = lambda r: r[...].astype(F32)
        o_ref[...] = ((f(h_ref) + f(r0)) + f(r1)) + f(r2)

    rspec = lambda j: pl.BlockSpec((None, None, tr, cols), functools.partial(lambda p, i, me_ref, j: (j, p, i, 0), j=j))
    return pl.pallas_call(
        body, name=name,
        grid_spec=pltpu.PrefetchScalarGridSpec(
            num_scalar_prefetch=1, grid=(npc, h // tr),
            in_specs=[pl.BlockSpec((None, None, tr, cols), lambda p, i, me_ref: (p, me_ref[0], i, 0)), rspec(0), rspec(1), rspec(2)],
            out_specs=pl.BlockSpec((None, tr, cols), lambda p, i, me_ref: (p, i, 0))),
        out_shape=jax.ShapeDtypeStruct((npc, h, cols), F32),
    )(me.reshape(1), hb, rb, rb, rb)


def adamw_halves(name, w, m, v, own, other, c):
    npc, rows, cols = w.shape
    h = rows // 2
    tr = _row_tile(h, cols, budget=1024 * 1024)
    nt = h // tr
    c1 = 1.0 - ADAM_B1 ** ADAM_STEP
    c2 = 1.0 - ADAM_B2 ** ADAM_STEP

    def body(c_ref, w_ref, m_ref, v_ref, own_ref, oth_ref, g_ref, d_ref, nm_ref, nv_ref):
        g = jnp.where(pl.program_id(1) == c_ref[0], own_ref[...], oth_ref[...])
        nm = ADAM_B1 * m_ref[...] + (1.0 - ADAM_B1) * g
        nv = ADAM_B2 * v_ref[...] + (1.0 - ADAM_B2) * jnp.square(g)
        g_ref[...] = g
        d_ref[...] = -ADAM_LR * ((nm / c1) / (jnp.sqrt(nv / c2) + ADAM_EPS) + ADAM_WD * w_ref[...])
        nm_ref[...] = nm
        nv_ref[...] = nv

    full = pl.BlockSpec((None, tr, cols), lambda p, hh, i, c_ref: (p, hh * nt + i, 0))
    half = pl.BlockSpec((None, tr, cols), lambda p, hh, i, c_ref: (p, i, 0))
    return pl.pallas_call(
        body, name=name,
        grid_spec=pltpu.PrefetchScalarGridSpec(
            num_scalar_prefetch=1, grid=(npc, 2, nt),
            in_specs=[full, full, full, half, half], out_specs=[full] * 4),
        out_shape=[jax.ShapeDtypeStruct(w.shape, F32)] * 4,
    )(c.reshape(1), w, m, v, own, other)


WEIGHTS = ("norm_w", "final_norm_w", "ffn_gate", "ffn_up", "ffn_down", "w_in", "branch_proj", "w_out", "s5_lambda_re",
           "s5_lambda_im", "s5_log_dt", "s5_b_re", "s5_b_im", "s5_c_re", "s5_c_im", "s5_d", "s5_glu_w", "s5_glu_b",
           "hg_lb_logits", "hg_norm_w", "rg_conv_w", "rg_conv_b", "rg_wa", "rg_ba", "rg_wx", "rg_bx", "rg_lambda")
BIG = ("ffn_gate", "ffn_up", "ffn_down", "w_in", "branch_proj", "w_out", "s5_glu_w")
SHARDED_SMALL = ("norm_w", "rg_conv_w")
SMALL = SMALL_RAW + SHARDED_SMALL


def _view2d(shape):
    return (1, shape[0]) if len(shape) == 1 else (math.prod(shape[:-1]), shape[-1])


def _small_layout(shapes, row_multiple):
    layout, at = [], 0
    for shape in shapes:
        r, c = _view2d(shape)
        rp = -(-r // 8) * 8
        layout.append((at, r, c, rp))
        at += rp * max(1, c // LANE)
    return layout, -(-at // row_multiple) * row_multiple


def pack_small(name, arrays, row_multiple):
    layout, rows = _small_layout([a.shape for a in arrays], row_multiple)

    def body(*refs):
        out = refs[-1]
        out[...] = jnp.zeros_like(out)
        for ref, (r0, r, c, rp) in zip(refs[:-1], layout):
            if c <= LANE:
                out[r0:r0 + r, 0:c] = ref[...]
            else:
                for q in range(c // LANE):
                    out[r0 + q * rp:r0 + q * rp + r, :] = ref[:, q * LANE:(q + 1) * LANE]

    return pl.pallas_call(
        body, name=name, out_shape=jax.ShapeDtypeStruct((rows, LANE), F32),
        compiler_params=pltpu.CompilerParams(vmem_limit_bytes=VMEM_LIMIT),
    )(*[a.reshape(_view2d(a.shape)) for a in arrays])


def unpack_small(name, packed, shapes):
    layout, _ = _small_layout(shapes, 8)

    def body(p_ref, *outs):
        for ref, (r0, r, c, rp) in zip(outs, layout):
            if c <= LANE:
                ref[...] = p_ref[r0:r0 + r, 0:c]
            else:
                for q in range(c // LANE):
                    ref[:, q * LANE:(q + 1) * LANE] = p_ref[r0 + q * rp:r0 + q * rp + r, :]

    res = pl.pallas_call(
        body, name=name, out_shape=[jax.ShapeDtypeStruct(_view2d(s), F32) for s in shapes],
        compiler_params=pltpu.CompilerParams(vmem_limit_bytes=VMEM_LIMIT),
    )(packed)
    return [a.reshape(s) for a, s in zip(res, shapes)]


HBM = pl.BlockSpec(memory_space=pltpu.HBM)
SEM = pl.BlockSpec(memory_space=pltpu.SEMAPHORE)
EFFECT = pltpu.SideEffectType.DATAFLOW_SIDE_EFFECTING


def split_start(name, srcs, land_shapes, plan, n_send, n_recv):
    ns, nl = len(srcs), len(land_shapes)

    def body(*refs):
        ins, lands = refs[:ns], refs[ns:ns + nl]
        send, recv = refs[ns + nl], refs[ns + nl + 1]
        for src, dst, ks, kr, dev in plan(ins, lands):
            pltpu.make_async_remote_copy(src_ref=src, dst_ref=dst, send_sem=send.at[ks], recv_sem=recv.at[kr],
                                         device_id=dev, device_id_type=MESH).start()
        refs[-1][...] = jnp.zeros_like(refs[-1])

    hbm = lambda a: pltpu.with_memory_space_constraint(a, pltpu.HBM)
    lands = [lax.empty(s.shape, s.dtype) for s in land_shapes]
    out = pl.pallas_call(
        body, name=name,
        out_shape=(pltpu.SemaphoreType.DMA((n_send,)), pltpu.SemaphoreType.DMA((n_recv,)),
                   *[pltpu.HBM(a.shape, a.dtype) for a in srcs], *[pltpu.HBM(s.shape, s.dtype) for s in land_shapes],
                   jax.ShapeDtypeStruct((8, LANE), F32)),
        in_specs=[HBM] * (ns + nl), out_specs=(SEM, SEM, *[HBM] * (ns + nl), pl.BlockSpec(memory_space=pltpu.VMEM)),
        input_output_aliases={k: 2 + k for k in range(ns + nl)},
        compiler_params=pltpu.CompilerParams(has_side_effects=EFFECT),
    )(*[hbm(a) for a in srcs], *[hbm(a) for a in lands])
    return out[:-1], out[-1]


def split_wait(name, handles, n_src, waits, after):
    send, recv, *bufs = handles
    nb = len(bufs)

    def body(*refs):
        ins, lands = refs[:n_src], refs[n_src:nb]
        send_sem, recv_sem = refs[nb], refs[nb + 1]
        x, y, c, _chips = _place()
        sends, recvs = waits(ins, lands)
        for src, k in sends:
            pltpu.make_async_remote_copy(src_ref=src, dst_ref=src, send_sem=send_sem.at[k], recv_sem=recv_sem.at[0],
                                         device_id=(x, y, 1 - c), device_id_type=MESH).wait_send()
        for dst, k in recvs:
            pltpu.make_async_remote_copy(src_ref=dst, dst_ref=dst, send_sem=send_sem.at[0], recv_sem=recv_sem.at[k],
                                         device_id=(x, y, 1 - c), device_id_type=MESH).wait_recv()

    out = pl.pallas_call(
        body, name=name, out_shape=tuple(pltpu.HBM(a.shape, a.dtype) for a in bufs),
        in_specs=[HBM] * nb + [SEM, SEM, ANY], out_specs=tuple([HBM] * nb),
        input_output_aliases={k: k for k in range(nb)},
        compiler_params=pltpu.CompilerParams(has_side_effects=EFFECT),
    )(*bufs, send, recv, after)
    return list(out[:n_src]), list(out[n_src:])


def gather_plan(n):
    def plan(ins, lands):
        x, y, c, chips = _place()
        me = 2 * x + y
        copies = []
        for w in range(n):
            for j, (cx, cy) in enumerate(chips):
                for t in range(2):
                    copies.append((ins[w].at[c], lands[w].at[c, me], 8 * w + 2 * j + t, 8 * w + 2 * j + c, (cx, cy, t)))
            for half in range(2):
                copies.append((ins[w].at[half], lands[w].at[half, me], 8 * w + 6 + half, 8 * w + 6 + half, (x, y, 1 - c)))
        return copies

    def waits(ins, lands):
        x, y, c, chips = _place()
        me = 2 * x + y
        sends, recvs = [], []
        for w in range(n):
            for j, (cx, cy) in enumerate(chips):
                for t in range(2):
                    sends.append((ins[w].at[c], 8 * w + 2 * j + t))
                    recvs.append((lands[w].at[t, 2 * cx + cy], 8 * w + 2 * j + t))
            for half in range(2):
                sends.append((ins[w].at[half], 8 * w + 6 + half))
                recvs.append((lands[w].at[half, me], 8 * w + 6 + half))
        return sends, recvs

    return plan, waits


def scatter_plan(n):
    def plan(ins, lands):
        x, y, c, chips = _place()
        return [(ins[w].at[:, 2 * cx + cy], lands[w].at[j], 3 * w + j, 3 * w + j, (cx, cy, c))
                for w in range(n) for j, (cx, cy) in enumerate(chips)]

    def waits(ins, lands):
        x, y, c, chips = _place()
        sends = [(ins[w].at[:, 2 * cx + cy], 3 * w + j) for w in range(n) for j, (cx, cy) in enumerate(chips)]
        recvs = [(lands[w].at[j], 3 * w + j) for w in range(n) for j in range(3)]
        return sends, recvs

    return plan, waits


def _layer_shards(w, l):
    return [w["ffn_gate"][l].astype(MMT), w["ffn_up"][l].astype(MMT), w["ffn_down"][l].astype(MMT),
            w["w_in"][l].reshape(2, D_MODEL // 2, -1).astype(MMT),
            w["branch_proj"][l].reshape(2, 3 * BW // 2, -1).astype(MMT),
            w["w_out"][l].reshape(2, -1, D_MODEL).astype(MMT),
            w["s5_glu_w"][l].reshape(2, -1, BW).astype(MMT)]


def _layer_weights(g):
    rows = lambda a: a.transpose(1, 0, 2, 3).reshape(NSH, -1, a.shape[-1])
    p = rows(g[4]).reshape(NSH, 3, BW, -1).transpose(1, 2, 0, 3).reshape(3, BW, D_MODEL)
    return dict(wg=g[0], wu=g[1], wd=g[2], win=rows(g[3]), pfull=p,
                woutfull=rows(g[5]).reshape(D_MODEL, D_MODEL), gluw=rows(g[6]).reshape(BW, BW))


GROUPS = {"ffn1": ("ffn_gate", "ffn_up", "ffn_down"), "merge": ("branch_proj", "w_out"), "mid": ("s5_glu_w",),
          "pre": ("w_in",), "ffn0": ("ffn_gate", "ffn_up", "ffn_down")}


def _grad_views(big, l, group):
    views = []
    for name in GROUPS[group]:
        if name == "branch_proj":
            dq = D_MODEL // NSH
            a = big[(name, l)].reshape(3, BW, NSH, dq).transpose(2, 0, 1, 3).reshape(1, NSH, 3 * BW, dq)
        elif name.startswith("ffn"):
            a = big[(name, l, 1 if group == "ffn1" else 0)]
        else:
            a = big[(name, l)]
            a = a.reshape(1, NSH, -1, a.shape[-1])
        views.append((name, a, 0))
    return views


def halves_plan(n):
    def src(ref, c):
        h = ref.shape[2] // 2
        return ref.at[:, :, pl.ds((1 - c) * h, h)]

    def plan(ins, lands):
        x, y, c, _chips = _place()
        return [(src(ins[w], c), lands[w], w, w, (x, y, 1 - c)) for w in range(n)]

    def waits(ins, lands):
        x, y, c, _chips = _place()
        return [(src(ins[w], c), w) for w in range(n)], [(lands[w], w) for w in range(n)]

    return plan, waits


def _reduce_to_halves(tag, views, c, wire):
    from_sibling = exchange_halves(f"reduce_cores_{tag}", [a for _, a, _ in views], [(p0, 1) for _, _, p0 in views])
    merge = lambda a: a.reshape((-1,) + a.shape[2:])
    return [add_own_half(f"sum_cores_{tag}_{i}", merge(a), merge(r), c, wire[i], NSH * p0).reshape(r.shape)
            for i, ((_, a, p0), r) in enumerate(zip(views, from_sibling))]


def _step(x, target, w, m, v):
    mx, my, mc = lax.axis_index("x"), lax.axis_index("y"), lax.axis_index("c")
    me = (2 * mx + my).astype(jnp.int32)
    mc = mc.astype(jnp.int32)

    W = dict(L=[None] * DEPTH)
    state = {"pending": []}
    n_big = len(BIG)
    g_plan, g_waits = gather_plan(n_big)

    def layer_weights(l, h):
        if l == 0:
            got = gather_shards("gather_weights_0", _layer_shards(w, 0) + [w[n] for n in SHARDED_SMALL])
            nxt = _layer_shards(w, 1)
            got, nxt = lax.optimization_barrier((got, nxt))
            shapes = [jax.ShapeDtypeStruct((2, NSH) + a.shape[1:], a.dtype) for a in nxt]
            state["gather"], token = split_start("gather_weights_1_start", nxt, shapes, g_plan, 8 * n_big, 8 * n_big)
            W["nw"] = got[n_big].transpose(0, 2, 1, 3).reshape(DEPTH, 3, 1, D_MODEL) + token[0, 0]
            W["convw"] = got[n_big + 1].transpose(0, 2, 1, 3).reshape(DEPTH, CONV_W, BW)
            return _layer_weights(got[:n_big]), h
        return _layer_weights(split_wait("gather_weights_1_wait", state["gather"], n_big, g_waits, h)[1]), h

    def to_chips(after):
        if "cores" not in state:
            return
        tag, names, l, group, handles, waits = state.pop("cores")
        sent, landed = split_wait(f"reduce_cores_{tag}_wait", handles, len(names), waits, after)
        merge = lambda a: a.reshape((-1,) + a.shape[2:])
        halves = [add_own_half(f"sum_cores_{tag}_{i}", merge(a), merge(r), mc, jnp.bfloat16, 0).reshape(r.shape)
                  for i, (a, r) in enumerate(zip(sent, landed))]
        shapes = [jax.ShapeDtypeStruct((3, a.shape[0]) + a.shape[2:], a.dtype) for a in halves]
        plan, waits = scatter_plan(len(halves))
        handles, token = split_start(f"reduce_chips_{tag}_start", halves, shapes, plan, 3 * len(halves), 3 * len(halves))
        W["nw"] = W["nw"] + token[0, 0]
        state["pending"].append((tag, names, l, group, handles, waits))

    def layer_grads(l, group, big):
        views = _grad_views(big, l, group)
        to_chips(views[0][1])
        if (l, group) == (0, "ffn0"):
            return
        tag = f"{l}_{group}"
        if (l, group) == (0, "pre"):
            halves = _reduce_to_halves(tag, views, mc, [jnp.bfloat16] * len(views))
            shapes = [jax.ShapeDtypeStruct((3, a.shape[0]) + a.shape[2:], a.dtype) for a in halves]
            plan, waits = scatter_plan(len(halves))
            handles, token = split_start(f"reduce_chips_{tag}_start", halves, shapes, plan, 3 * len(halves), 3 * len(halves))
            W["nw"] = W["nw"] + token[0, 0]
            state["pending"].append((tag, [name for name, _, _ in views], l, group, handles, waits))
            return
        arrays = [a for _, a, _ in views]
        shapes = [jax.ShapeDtypeStruct((1, NSH, a.shape[2] // 2, a.shape[3]), a.dtype) for a in arrays]
        plan, waits = halves_plan(len(arrays))
        handles, token = split_start(f"reduce_cores_{tag}_start", arrays, shapes, plan, len(arrays), len(arrays))
        W["nw"] = W["nw"] + token[0, 0]
        state["cores"] = (tag, [name for name, _, _ in views], l, group, handles, waits)

    loss, dx, big, small = local_step(x[0], target[0], W, {k: w[k] for k in SMALL_RAW}, layer_weights, layer_grads)

    pieces = {n: {} for n in BIG}
    block_of = lambda name, l, group: (2 * l + (group == "ffn1")) if name.startswith("ffn") else l
    views = _grad_views(big, 0, "ffn0")
    small_packed = pack_small("pack_small_grads", [small[n] for n in SMALL], NSH * 32)
    halves = _reduce_to_halves("0_ffn0", views + [("small", small_packed.reshape(1, NSH, -1, LANE), 0)], mc,
                               [jnp.bfloat16] * len(views) + [F32])
    shapes = [jax.ShapeDtypeStruct((3, a.shape[0]) + a.shape[2:], a.dtype) for a in halves]
    plan, waits = scatter_plan(len(halves))
    last_handles, token = split_start("reduce_chips_0_ffn0_start", halves, shapes, plan, 3 * len(halves), 3 * len(halves))
    mc = mc + token[0, 0].astype(jnp.int32)
    for tag, names, l, group, handles, waits_k in state["pending"]:
        sent, landed = split_wait(f"reduce_chips_{tag}_wait", handles, len(names), waits_k, dx)
        for i, (name, h, r) in enumerate(zip(names, sent, landed)):
            pieces[name][block_of(name, l, group)] = add_chips(f"sum_chips_{tag}_{i}", h, r, me)

    g, delta, new_m, new_v = {}, {}, {}, {}

    def update(tag, names, extra):
        own = [jnp.concatenate([pieces[n][b] for b in sorted(pieces[n])], axis=0) for n in names] + extra
        other = share_halves(f"reduce_share_{tag}", own)
        for i, n in enumerate(names):
            view = lambda a: a.reshape(own[i].shape[0], -1, own[i].shape[2])
            res = adamw_halves(f"adamw_{n}", view(w[n]), view(m[n]), view(v[n]), own[i], other[i], mc)
            g[n], delta[n], new_m[n], new_v[n] = [a.reshape(w[n].shape) for a in res]
        return own, other

    early = [n for n in BIG if not n.startswith("ffn")]
    update("early", early, [])
    sent, landed = split_wait("reduce_chips_0_ffn0_wait", last_handles, len(halves), waits, new_v[early[0]])
    last = [add_chips(f"sum_chips_0_ffn0_{i}", h, r, me) for i, (h, r) in enumerate(zip(sent, landed))]
    for (name, _, _), piece in zip(views, last):
        pieces[name][block_of(name, 0, "ffn0")] = piece
    own, other = update("last", [n for n in BIG if n.startswith("ffn")], [last[-1]])

    piece = jnp.stack([jnp.where(mc == 0, own[-1][0], other[-1][0]), jnp.where(mc == 0, other[-1][0], own[-1][0])])
    (all_small,) = gather_shards("gather_small", [piece])
    full_small = unpack_small("unpack_small_grads", all_small.transpose(1, 0, 2, 3).reshape(-1, LANE),
                              [small[n].shape for n in SMALL])
    g.update(zip(SMALL, full_small))
    g["norm_w"] = lax.dynamic_slice_in_dim(g["norm_w"], me * (D_MODEL // NSH), D_MODEL // NSH, axis=2)
    g["rg_conv_w"] = lax.dynamic_slice_in_dim(g["rg_conv_w"], me * (BW // NSH), BW // NSH, axis=2)

    packed = [pack_small(f"pack_small_{tag}", [src[n] for n in SMALL], 8)
              for tag, src in (("w", w), ("g", g), ("m", m), ("v", v))]
    for tag, dst, flat in zip(("delta", "m", "v"), (delta, new_m, new_v), adamw(*packed)):
        dst.update(zip(SMALL, unpack_small(f"unpack_small_{tag}", flat, [w[n].shape for n in SMALL])))

    total = lax.psum(loss[0, 0], ("x", "y", "c"))
    return (total, dx[None], *[g[n] for n in WEIGHTS], *[delta[n] for n in WEIGHTS],
            *[new_m[n] for n in WEIGHTS], *[new_v[n] for n in WEIGHTS])


def kernel(x, norm_w, final_norm_w, ffn_gate, ffn_up, ffn_down, w_in, branch_proj, w_out, s5_lambda_re, s5_lambda_im, s5_log_dt, s5_b_re, s5_b_im, s5_c_re, s5_c_im, s5_d, s5_glu_w, s5_glu_b, hg_lb_logits, hg_norm_w, rg_conv_w, rg_conv_b, rg_wa, rg_ba, rg_wx, rg_bx, rg_lambda, loss_target, m_norm_w, m_final_norm_w, m_ffn_gate, m_ffn_up, m_ffn_down, m_w_in, m_branch_proj, m_w_out, m_s5_lambda_re, m_s5_lambda_im, m_s5_log_dt, m_s5_b_re, m_s5_b_im, m_s5_c_re, m_s5_c_im, m_s5_d, m_s5_glu_w, m_s5_glu_b, m_hg_lb_logits, m_hg_norm_w, m_rg_conv_w, m_rg_conv_b, m_rg_wa, m_rg_ba, m_rg_wx, m_rg_bx, m_rg_lambda, v_norm_w, v_final_norm_w, v_ffn_gate, v_ffn_up, v_ffn_down, v_w_in, v_branch_proj, v_w_out, v_s5_lambda_re, v_s5_lambda_im, v_s5_log_dt, v_s5_b_re, v_s5_b_im, v_s5_c_re, v_s5_c_im, v_s5_d, v_s5_glu_w, v_s5_glu_b, v_hg_lb_logits, v_hg_norm_w, v_rg_conv_w, v_rg_conv_b, v_rg_wa, v_rg_ba, v_rg_wx, v_rg_bx, v_rg_lambda):
    ws = (norm_w, final_norm_w, ffn_gate, ffn_up, ffn_down, w_in, branch_proj, w_out, s5_lambda_re, s5_lambda_im, s5_log_dt, s5_b_re, s5_b_im, s5_c_re, s5_c_im, s5_d, s5_glu_w, s5_glu_b, hg_lb_logits, hg_norm_w, rg_conv_w, rg_conv_b, rg_wa, rg_ba, rg_wx, rg_bx, rg_lambda)
    ms = (m_norm_w, m_final_norm_w, m_ffn_gate, m_ffn_up, m_ffn_down, m_w_in, m_branch_proj, m_w_out, m_s5_lambda_re, m_s5_lambda_im, m_s5_log_dt, m_s5_b_re, m_s5_b_im, m_s5_c_re, m_s5_c_im, m_s5_d, m_s5_glu_w, m_s5_glu_b, m_hg_lb_logits, m_hg_norm_w, m_rg_conv_w, m_rg_conv_b, m_rg_wa, m_rg_ba, m_rg_wx, m_rg_bx, m_rg_lambda)
    vs = (v_norm_w, v_final_norm_w, v_ffn_gate, v_ffn_up, v_ffn_down, v_w_in, v_branch_proj, v_w_out, v_s5_lambda_re, v_s5_lambda_im, v_s5_log_dt, v_s5_b_re, v_s5_b_im, v_s5_c_re, v_s5_c_im, v_s5_d, v_s5_glu_w, v_s5_glu_b, v_hg_lb_logits, v_hg_norm_w, v_rg_conv_w, v_rg_conv_b, v_rg_wa, v_rg_ba, v_rg_wx, v_rg_bx, v_rg_lambda)
    return _step(x, loss_target, dict(zip(WEIGHTS, ws)), dict(zip(WEIGHTS, ms)), dict(zip(WEIGHTS, vs)))
```

```python
import functools
import math
from typing import NamedTuple

import jax
import jax.numpy as jnp
from jax import lax
from jax.experimental import pallas as pl
from jax.experimental.pallas import tpu as pltpu

F32 = jnp.float32
MMT = jnp.bfloat16

D_MODEL = 1024
BW = 512
S5_GROUP, S5_GROUPS, S5_STATE = 16, 32, 64
S5_N = S5_GROUPS * S5_STATE
HG_HEADS, HG_D = 4, 128
HG_CHUNK = 128
RG_BLOCKS, RG_BLOCK = 8, 64
RG_C = 8.0
CONV_W = 4
D_FF = 2816
EPS = 1e-6
IN_TOTAL = 6656
NSH = 4
NSEG = 8
LANE = 128
VMEM_LIMIT = 56 * 1024 * 1024
TM_FWD = 512
TM_WGRAD = 512

ADAM_LR, ADAM_B1, ADAM_B2, ADAM_EPS, ADAM_WD, ADAM_STEP = 0.001, 0.9, 0.999, 1e-08, 0.01, 10

MESH = pl.DeviceIdType.MESH


class WP(NamedTuple):
    w: jax.Array
    p: jax.Array


def _dg(a, b, ca, cb):
    return lax.dot_general(a, b, (((ca,), (cb,)), ((), ())), preferred_element_type=F32)


@jax.custom_vjp
def _mmw(a, w, p):
    return _dg(a.astype(MMT), w, 1, 0)


def _mmw_fwd(a, w, p):
    return _mmw(a, w, p), (a, w)


def _mmw_bwd(res, g):
    a, w = res
    gb = g.astype(MMT)
    return _dg(gb, w, 1, 1), jnp.zeros_like(w), _dg(a.astype(MMT), gb, 0, 0)


_mmw.defvjp(_mmw_fwd, _mmw_bwd)


def mm(a, w):
    if isinstance(w, WP):
        return _mmw(a, w.w, w.p)
    return _dg(a.astype(MMT), w, 1, 0)


@jax.custom_vjp
def mma_nn(a, b):
    return _dg(a.astype(MMT), b.astype(MMT), 1, 0)


def _nn_f(a, b):
    return mma_nn(a, b), (a, b)


def _nn_b(res, g):
    a, b = res
    gb = g.astype(MMT)
    return _dg(gb, b.astype(MMT), 1, 1), _dg(a.astype(MMT), gb, 0, 0)


mma_nn.defvjp(_nn_f, _nn_b)


@jax.custom_vjp
def mma_nt(a, b):
    return _dg(a.astype(MMT), b.astype(MMT), 1, 1)


def _nt_f(a, b):
    return mma_nt(a, b), (a, b)


def _nt_b(res, g):
    a, b = res
    gb = g.astype(MMT)
    return _dg(gb, b.astype(MMT), 1, 0), _dg(gb, a.astype(MMT), 0, 0)


mma_nt.defvjp(_nt_f, _nt_b)


@jax.custom_vjp
def mma_tn(a, b):
    return _dg(a.astype(MMT), b.astype(MMT), 0, 0)


def _tn_f(a, b):
    return mma_tn(a, b), (a, b)


def _tn_b(res, g):
    a, b = res
    gb = g.astype(MMT)
    return _dg(b.astype(MMT), gb, 1, 1), _dg(a.astype(MMT), gb, 1, 0)


mma_tn.defvjp(_tn_f, _tn_b)


def _split3(x):
    hi = x.astype(MMT)
    r = x - hi.astype(F32)
    mid = r.astype(MMT)
    return hi, mid, (r - mid.astype(F32)).astype(MMT)


@jax.custom_vjp
def mm_exact(m, x):
    mb = m.astype(MMT)
    hi, mid, lo = _split3(x)
    return (_dg(mb, hi, 1, 0) + _dg(mb, mid, 1, 0)) + _dg(mb, lo, 1, 0)


def _mm_exact_fwd(m, x):
    return mm_exact(m, x), m


def _mm_exact_bwd(m, g):
    mb = m.astype(MMT)
    hi, mid, lo = _split3(g)
    return jnp.zeros_like(m), (_dg(mb, hi, 0, 0) + _dg(mb, mid, 0, 0)) + _dg(mb, lo, 0, 0)


mm_exact.defvjp(_mm_exact_fwd, _mm_exact_bwd)


@jax.custom_vjp
def mm_exact_r(x, m):
    mb = m.astype(MMT)
    hi, mid, lo = _split3(x)
    return (_dg(hi, mb, 1, 0) + _dg(mid, mb, 1, 0)) + _dg(lo, mb, 1, 0)


def _mm_exact_r_fwd(x, m):
    return mm_exact_r(x, m), m


def _mm_exact_r_bwd(m, g):
    mb = m.astype(MMT)
    hi, mid, lo = _split3(g)
    return (_dg(hi, mb, 1, 1) + _dg(mid, mb, 1, 1)) + _dg(lo, mb, 1, 1), jnp.zeros_like(m)


mm_exact_r.defvjp(_mm_exact_r_fwd, _mm_exact_r_bwd)


def _rms(x, w):
    return x * lax.rsqrt(jnp.mean(x * x, axis=-1, keepdims=True) + EPS) * w


def _expm1(x):
    series = x * (1.0 + x * (1.0 / 2) * (1.0 + x * (1.0 / 3) * (1.0 + x * (1.0 / 4) * (1.0 + x * (1.0 / 5) * (1.0 + x * (1.0 / 6))))))
    return jnp.where(jnp.abs(x) < 0.1, series, jnp.exp(x) - 1.0)


def _bspec(block, fn, order):
    if order == "is":
        return pl.BlockSpec(block, lambda i, s: fn(s, i))
    return pl.BlockSpec(block, lambda s, i: fn(s, i))


def tile_fwd(fn, name, n_i, n_s, ins, outs, s_outer=False):
    n_in = len(ins)
    order = "si" if s_outer else "is"
    assert not (s_outer and any(o[4] for o in outs))

    def body(*refs):
        s = pl.program_id(0 if s_outer else 1)
        res = fn(*[r[...] for r in refs[:n_in]], s)
        for o_ref, val, spec in zip(refs[n_in:], res, outs):
            if spec[4] and n_s > 1:
                @pl.when(s == 0)
                def _(o_ref=o_ref, val=val):
                    o_ref[...] = val.astype(o_ref.dtype)

                @pl.when(s != 0)
                def _(o_ref=o_ref, val=val):
                    o_ref[...] += val.astype(o_ref.dtype)
            else:
                o_ref[...] = val.astype(o_ref.dtype)

    return pl.pallas_call(
        body, grid=(n_s, n_i) if s_outer else (n_i, n_s), name=name,
        in_specs=[_bspec(b, f, order) for _, b, f in ins],
        out_specs=[_bspec(b, f, order) for _, _, b, f, _ in outs],
        out_shape=[jax.ShapeDtypeStruct(sh, dt) for sh, dt, _, _, _ in outs],
        compiler_params=pltpu.CompilerParams(vmem_limit_bytes=VMEM_LIMIT,
                                             dimension_semantics=("arbitrary", "arbitrary")),
    )(*[a for a, _, _ in ins])


def tile_bwd(fn, name, n_i, n_s, ins, cts, gouts):
    groups = [c if isinstance(c, list) else [c] for c in cts]
    cts = [blk for grp in groups for blk in grp]
    n_in, n_ct = len(ins), len(cts)
    kinds = [k for _, _, _, k in ins]
    d_pos = [j for j, k in enumerate(kinds) if k != "c"]
    shared = [(gi, spec[4]) for gi, spec in enumerate(gouts) if len(spec) == 5 and spec[4] is not None]
    n_sh = len(shared)

    def body(*refs):
        s, i = pl.program_id(0), pl.program_id(1)
        vals = [r[...] for r in refs[:n_in]]
        ct_refs, ctv = list(refs[n_in:n_in + n_ct]), []
        for grp in groups:
            parts = [ct_refs.pop(0)[...] for _ in grp]
            ctv.append(parts[0] if len(parts) == 1 else jnp.concatenate(parts, axis=1))
        ctv = tuple(ctv)
        g_refs = refs[n_in + n_ct + n_sh:]

        def g(*dv):
            args = list(vals)
            for j, v in zip(d_pos, dv):
                args[j] = WP(vals[j], v) if kinds[j] == "w" else v
            return tuple(fn(*args))

        dv0 = [jnp.zeros(vals[j].shape, F32) if kinds[j] == "w" else vals[j] for j in d_pos]
        _, vjp = jax.vjp(g, *dv0)
        grads = vjp(ctv)
        for g_ref, gv, spec in zip(g_refs, grads, gouts):
            mode = spec[3]
            if mode == "write":
                g_ref[...] = gv.astype(g_ref.dtype)
            else:
                first = (i == 0) if mode == "acc_i" else jnp.logical_and(i == 0, s == 0)

                @pl.when(first)
                def _(g_ref=g_ref, gv=gv):
                    g_ref[...] = gv.astype(g_ref.dtype)

                @pl.when(jnp.logical_not(first))
                def _(g_ref=g_ref, gv=gv):
                    g_ref[...] += gv.astype(g_ref.dtype)

    return pl.pallas_call(
        body, grid=(n_s, n_i), name=name,
        in_specs=([_bspec(b, f, "si") for _, b, f, _ in ins] + [_bspec(b, f, "si") for _, b, f in cts]
                  + [pl.BlockSpec(memory_space=pl.ANY)] * n_sh),
        out_specs=[_bspec(spec[1], spec[2], "si") for spec in gouts],
        out_shape=[jax.ShapeDtypeStruct(spec[0], F32) for spec in gouts],
        input_output_aliases={n_in + n_ct + k: gi for k, (gi, _) in enumerate(shared)},
        compiler_params=pltpu.CompilerParams(vmem_limit_bytes=VMEM_LIMIT,
                                             dimension_semantics=("arbitrary", "arbitrary")),
    )(*[a for a, _, _, _ in ins], *[a for a, _, _ in cts], *[buf for _, buf in shared])


def _row_tile(rows, width, itemsize=4, budget=2 * 1024 * 1024, mult=8):
    best = mult
    for t in range(mult, rows + 1, mult):
        if rows % t == 0 and t * width * itemsize <= budget:
            best = t
    return best


def add_n(name, terms, shape):
    rows, cols = shape
    tr = _row_tile(rows, cols)

    def body(*refs):
        acc = refs[0][...]
        for r in refs[1:-1]:
            acc = acc + r[...]
        refs[-1][...] = acc

    specs = []
    for _, lead in terms:
        specs.append(pl.BlockSpec((None,) * len(lead) + (tr, cols), functools.partial(lambda i, lead: (*lead, i, 0), lead=lead)))
    return pl.pallas_call(
        body, grid=(rows // tr,), name=name, in_specs=specs,
        out_specs=pl.BlockSpec((tr, cols), lambda i: (i, 0)),
        out_shape=jax.ShapeDtypeStruct((rows, cols), F32),
    )(*[a for a, _ in terms])


def pre_core(x, nw, win):
    return (mm(_rms(x, nw), win),)


def _split_lanes(y):
    return jnp.stack([y[:, k * LANE:(k + 1) * LANE] for k in range(y.shape[1] // LANE)], axis=0)


def _join_lanes(y3):
    return jnp.concatenate([y3[k] for k in range(y3.shape[0])], axis=1)


def s5_pre_core(u, b_re0, b_re1, b_im0, b_im1):
    u0, u1 = u[:, :BW // 2], u[:, BW // 2:]
    re = jnp.concatenate([mm(u0, b_re0), mm(u1, b_re1)], axis=1)
    im = jnp.concatenate([mm(u0, b_im0), mm(u1, b_im1)], axis=1)
    return _split_lanes(re), _split_lanes(im)


def mid_core(xr, xi, u, o, g, hs, gc, hmat, c0, c1, d, gluw, glub, hgw):
    half = xr.shape[0] // 2
    xs0 = jnp.concatenate([_join_lanes(xr[:half]), _join_lanes(xi[:half])], axis=1)
    xs1 = jnp.concatenate([_join_lanes(xr[half:]), _join_lanes(xi[half:])], axis=1)
    y = jnp.concatenate([mm(xs0, c0), mm(xs1, c1)], axis=1) + d * u
    z = jax.nn.gelu(y)
    ya = z * jax.nn.sigmoid(mm(z, gluw) + glub)
    ms = mm_exact_r(o * o, hmat)
    yb = o * lax.rsqrt(ms + EPS) * hgw * jax.nn.silu(g)
    yc = hs * jax.nn.gelu(gc)
    return ya, yb, yc


def _sub(w, n):
    return WP(w.w[n], w.p[n]) if isinstance(w, WP) else w[n]


def merge_core(ya, yb, yc, g0, g1, g2, g3, g4, g5, p, wout):
    gate = lambda a, b: jax.nn.sigmoid(jnp.concatenate([a, b], axis=1))
    m = gate(g0, g1) * mm(ya, _sub(p, 0)) + gate(g2, g3) * mm(yb, _sub(p, 1)) + gate(g4, g5) * mm(yc, _sub(p, 2))
    return (mm(m, wout),)


def gates_core(xc, wa, ba, wx, bx, lam):
    r = jax.nn.sigmoid(mm(xc, wa) + ba)
    i = jax.nn.sigmoid(mm(xc, wx) + bx)
    log_a = -RG_C * jax.nn.softplus(-lam) * r
    a = jnp.exp(log_a)
    b = jnp.sqrt(-_expm1(2.0 * log_a)) * (i * xc)
    return a, b


def _seg_rows(ref, k, j, n):
    rows = pl.ds(pl.multiple_of(j * NSEG, NSEG), NSEG)
    if k is None:
        return ref[rows, :]
    return ref[k, rows, :]


def _seg_store(ref, k, j, n, val):
    rows = pl.ds(pl.multiple_of(j * NSEG, NSEG), NSEG)
    if k is None:
        ref[rows, :] = val
    else:
        ref[k, rows, :] = val


def _seg_carries(er, ei, pr, pi, reverse):
    rows = lax.broadcasted_iota(jnp.int32, er.shape, 0)
    cr = jnp.zeros_like(er)
    ci = None if ei is None else jnp.zeros_like(er)
    order = range(NSEG - 2, -1, -1) if reverse else range(1, NSEG)
    shift = NSEG - 1 if reverse else 1
    for s in order:
        if ei is None:
            tr = er + pr * cr
            cr = jnp.where(rows == s, pltpu.roll(tr, shift, 0), cr)
        else:
            tr = er + pr * cr - pi * ci
            ti = ei + pr * ci + pi * cr
            cr = jnp.where(rows == s, pltpu.roll(tr, shift, 0), cr)
            ci = jnp.where(rows == s, pltpu.roll(ti, shift, 0), ci)
    return cr, ci


def _cpow(ar, ai, n):
    out = None
    while n:
        if n & 1:
            out = (ar, ai) if out is None else (out[0] * ar - out[1] * ai, out[0] * ai + out[1] * ar)
        ar, ai = ar * ar - ai * ai, 2.0 * ar * ai
        n >>= 1
    return out


S5_K = 2


def s5_scan_fwd(bur, bui, ar, ai, L):
    n = L // NSEG
    nb = S5_N // LANE
    K = S5_K

    def body(br_ref, bi_ref, ar_ref, ai_ref, xr_ref, xi_ref):
        zero = jnp.zeros((NSEG, LANE), F32)
        A = [(jnp.broadcast_to(ar_ref[k], (NSEG, LANE)), jnp.broadcast_to(ai_ref[k], (NSEG, LANE))) for k in range(K)]

        def p1(j, st):
            new = []
            for k in range(K):
                sr, si = st[k]
                a_r, a_i = A[k]
                nr = a_r * sr - a_i * si + _seg_rows(br_ref, k, j, n)
                ni = a_r * si + a_i * sr + _seg_rows(bi_ref, k, j, n)
                _seg_store(xr_ref, k, j, n, nr)
                _seg_store(xi_ref, k, j, n, ni)
                new.append((nr, ni))
            return tuple(new)

        st = lax.fori_loop(0, n, p1, tuple((zero, zero) for _ in range(K)))
        C = [_seg_carries(st[k][0], st[k][1], *_cpow(*A[k], n), False) for k in range(K)]

        def p2(j, st):
            new = []
            for k in range(K):
                pr, pi = st[k]
                a_r, a_i = A[k]
                pr, pi = a_r * pr - a_i * pi, a_r * pi + a_i * pr
                cr, ci = C[k]
                _seg_store(xr_ref, k, j, n, _seg_rows(xr_ref, k, j, n) + pr * cr - pi * ci)
                _seg_store(xi_ref, k, j, n, _seg_rows(xi_ref, k, j, n) + pr * ci + pi * cr)
                new.append((pr, pi))
            return tuple(new)

        lax.fori_loop(0, n, p2, tuple((zero + 1.0, zero) for _ in range(K)))

    blk = pl.BlockSpec((K, L, LANE), lambda g: (g, 0, 0))
    ablk = pl.BlockSpec((K, 1, LANE), lambda g: (g, 0, 0))
    return pl.pallas_call(
        body, grid=(nb // K,), name="s5_scan_fwd",
        in_specs=[blk, blk, ablk, ablk], out_specs=[blk, blk],
        out_shape=[jax.ShapeDtypeStruct((nb, L, LANE), F32)] * 2,
        compiler_params=pltpu.CompilerParams(vmem_limit_bytes=VMEM_LIMIT),
    )(bur, bui, ar, ai)


def s5_scan_bwd(dxr, dxi, xr, xi, ar, ai, L):
    n = L // NSEG
    nb = S5_N // LANE
    K = S5_K

    def body(dr_ref, di_ref, xr_ref, xi_ref, ar_ref, ai_ref, gr_ref, gi_ref, dar_ref, dai_ref):
        zero = jnp.zeros((NSEG, LANE), F32)
        rows = lax.broadcasted_iota(jnp.int32, (NSEG, LANE), 0)
        A = [(jnp.broadcast_to(ar_ref[k], (NSEG, LANE)), -jnp.broadcast_to(ai_ref[k], (NSEG, LANE))) for k in range(K)]

        def p1(jj, st):
            j = n - 1 - jj
            new = []
            for k in range(K):
                sr, si = st[k]
                a_r, a_i = A[k]
                nr = a_r * sr - a_i * si + _seg_rows(dr_ref, k, j, n)
                ni = a_r * si + a_i * sr + _seg_rows(di_ref, k, j, n)
                _seg_store(gr_ref, k, j, n, nr)
                _seg_store(gi_ref, k, j, n, ni)
                new.append((nr, ni))
            return tuple(new)

        st = lax.fori_loop(0, n, p1, tuple((zero, zero) for _ in range(K)))
        C = [_seg_carries(st[k][0], st[k][1], *_cpow(*A[k], n), True) for k in range(K)]
        xb = [(jnp.where(rows == 0, 0.0, pltpu.roll(_seg_rows(xr_ref, k, n - 1, n), 1, 0)),
               jnp.where(rows == 0, 0.0, pltpu.roll(_seg_rows(xi_ref, k, n - 1, n), 1, 0))) for k in range(K)]

        def p2(jj, st):
            j = n - 1 - jj
            jp = jnp.maximum(j - 1, 0)
            new = []
            for k in range(K):
                pr, pi, acr, aci = st[k]
                a_r, a_i = A[k]
                pr, pi = a_r * pr - a_i * pi, a_r * pi + a_i * pr
                cr, ci = C[k]
                g_r = _seg_rows(gr_ref, k, j, n) + pr * cr - pi * ci
                g_i = _seg_rows(gi_ref, k, j, n) + pr * ci + pi * cr
                _seg_store(gr_ref, k, j, n, g_r)
                _seg_store(gi_ref, k, j, n, g_i)
                xpr = jnp.where(j == 0, xb[k][0], _seg_rows(xr_ref, k, jp, n))
                xpi = jnp.where(j == 0, xb[k][1], _seg_rows(xi_ref, k, jp, n))
                new.append((pr, pi, acr + g_r * xpr + g_i * xpi, aci + g_i * xpr - g_r * xpi))
            return tuple(new)

        st = lax.fori_loop(0, n, p2, tuple((zero + 1.0, zero, zero, zero) for _ in range(K)))
        for k in range(K):
            dar_ref[k] = jnp.sum(st[k][2], axis=0, keepdims=True)
            dai_ref[k] = jnp.sum(st[k][3], axis=0, keepdims=True)

    blk = pl.BlockSpec((K, L, LANE), lambda g: (g, 0, 0))
    ablk = pl.BlockSpec((K, 1, LANE), lambda g: (g, 0, 0))
    return pl.pallas_call(
        body, grid=(nb // K,), name="s5_scan_bwd",
        in_specs=[blk, blk, blk, blk, ablk, ablk], out_specs=[blk, blk, ablk, ablk],
        out_shape=[jax.ShapeDtypeStruct((nb, L, LANE), F32)] * 2 + [jax.ShapeDtypeStruct((nb, 1, LANE), F32)] * 2,
        compiler_params=pltpu.CompilerParams(vmem_limit_bytes=VMEM_LIMIT),
    )(dxr, dxi, xr, xi, ar, ai)


def rg_scan_fwd(a, b, L):
    n = L // NSEG

    def body(a_ref, b_ref, h_ref):
        zero = jnp.zeros((NSEG, LANE), F32)

        def p1(j, st):
            h, p = st
            aj = _seg_rows(a_ref, None, j, n)
            h = aj * h + _seg_rows(b_ref, None, j, n)
            _seg_store(h_ref, None, j, n, h)
            return h, aj * p

        e, pe = lax.fori_loop(0, n, p1, (zero, zero + 1.0))
        c, _ = _seg_carries(e, None, pe, None, False)

        def p2(j, p):
            p = _seg_rows(a_ref, None, j, n) * p
            _seg_store(h_ref, None, j, n, _seg_rows(h_ref, None, j, n) + p * c)
            return p

        lax.fori_loop(0, n, p2, zero + 1.0)

    blk = pl.BlockSpec((L, LANE), lambda g: (0, g))
    return pl.pallas_call(
        body, grid=(BW // LANE,), name="rg_scan_fwd", in_specs=[blk, blk], out_specs=blk,
        out_shape=jax.ShapeDtypeStruct((L, BW), F32),
        compiler_params=pltpu.CompilerParams(vmem_limit_bytes=VMEM_LIMIT),
    )(a, b)


def rg_scan_bwd(a, h, dh, L):
    n = L // NSEG

    def body(a_ref, h_ref, dh_ref, da_ref, db_ref):
        zero = jnp.zeros((NSEG, LANE), F32)
        rows = lax.broadcasted_iota(jnp.int32, (NSEG, LANE), 0)
        a_edge = jnp.where(rows == NSEG - 1, 0.0, pltpu.roll(_seg_rows(a_ref, None, 0, n), NSEG - 1, 0))
        h_edge = jnp.where(rows == 0, 0.0, pltpu.roll(_seg_rows(h_ref, None, n - 1, n), 1, 0))

        def mult(j):
            return jnp.where(j == n - 1, a_edge, _seg_rows(a_ref, None, jnp.minimum(j + 1, n - 1), n))

        def p1(jj, st):
            j = n - 1 - jj
            g, p = st
            m = mult(j)
            g = m * g + _seg_rows(dh_ref, None, j, n)
            _seg_store(db_ref, None, j, n, g)
            return g, m * p

        e, pe = lax.fori_loop(0, n, p1, (zero, zero + 1.0))
        c, _ = _seg_carries(e, None, pe, None, True)

        def p2(jj, p):
            j = n - 1 - jj
            p = mult(j) * p
            g = _seg_rows(db_ref, None, j, n) + p * c
            _seg_store(db_ref, None, j, n, g)
            hp = jnp.where(j == 0, h_edge, _seg_rows(h_ref, None, jnp.maximum(j - 1, 0), n))
            _seg_store(da_ref, None, j, n, g * hp)
            return p

        lax.fori_loop(0, n, p2, zero + 1.0)

    blk = pl.BlockSpec((L, LANE), lambda g: (0, g))
    return pl.pallas_call(
        body, grid=(BW // LANE,), name="rg_scan_bwd", in_specs=[blk, blk, blk], out_specs=[blk, blk],
        out_shape=[jax.ShapeDtypeStruct((L, BW), F32)] * 2,
        compiler_params=pltpu.CompilerParams(vmem_limit_bytes=VMEM_LIMIT),
    )(a, h, dh)


def _hg_consts(C):
    t = lax.broadcasted_iota(jnp.int32, (C, C), 0)
    s = lax.broadcasted_iota(jnp.int32, (C, C), 1)
    tril = (s <= t).astype(F32)
    diag = (s == t).astype(F32)
    levels = []
    k = 1
    while (1 << k) <= C:
        m = 1 << (k - 1)
        same = (t >> k) == (s >> k)
        t_right = ((t >> (k - 1)) & 1) == 1
        s_left = ((s >> (k - 1)) & 1) == 0
        mask = jnp.logical_and(same, jnp.logical_and(t_right, s_left)).astype(F32)
        bnd = ((t >> k) << k) + (m - 1)
        levels.append((mask, (s <= bnd).astype(F32)))
        k += 1
    return tril, diag, levels


def hg_chunk(st, q, z, v, lb):
    C = q.shape[0]
    tril, diag, levels = _hg_consts(C)
    sig = jax.nn.sigmoid(z)
    lf = jnp.log(lb + (1.0 - lb) * sig)
    k = (1.0 - lb) * jax.nn.sigmoid(-z)
    qh = jax.nn.silu(q)
    b = mm_exact(tril, lf)
    blast = jnp.sum(lf, axis=0, keepdims=True)
    qe = qh * jnp.exp(b)
    kd = k * jnp.exp(blast - b)
    scaled = []
    for level, (_, sel) in enumerate(levels):
        size = 2 << level
        if size >= NSEG:
            b3 = b.reshape(C // size, size, b.shape[1])
            ref = jnp.broadcast_to(b3[:, size // 2 - 1:size // 2, :], b3.shape).reshape(b.shape)
        else:
            ref = mm_exact(sel, lf)
        scaled.append((qh * jnp.exp(jnp.minimum(b - ref, 0.0)), k * jnp.exp(jnp.minimum(ref - b, 0.0))))
    outs, news = [], []
    for h in range(HG_HEADS):
        sl = slice(h * HG_D, (h + 1) * HG_D)
        st_h = st[h * HG_D:(h + 1) * HG_D, :]
        sc = diag * mma_nt(qh[:, sl], k[:, sl])
        for (mask, _), (qt, kt) in zip(levels, scaled):
            sc = sc + mask * mma_nt(qt[:, sl], kt[:, sl])
        outs.append(mma_nt(qe[:, sl], st_h) + mma_nn(sc, v[:, sl]))
        news.append(st_h * jnp.exp(blast[:, sl]) + mma_tn(v[:, sl], kd[:, sl]))
    return jnp.concatenate(news, axis=0), jnp.concatenate(outs, axis=1)


def hg_fwd(qzv, lb, L):
    C = HG_CHUNK
    nc = L // C

    def body(q_ref, z_ref, v_ref, lb_ref, o_ref, sst_ref, st_ref):
        @pl.when(pl.program_id(0) == 0)
        def _():
            st_ref[...] = jnp.zeros_like(st_ref)

        st = st_ref[...]
        sst_ref[...] = st
        new, o = hg_chunk(st, q_ref[...], z_ref[...], v_ref[...], lb_ref[...])
        st_ref[...] = new
        o_ref[...] = o

    col = lambda cb: pl.BlockSpec((C, BW), functools.partial(lambda c, cb: (c, cb), cb=cb))
    return pl.pallas_call(
        body, grid=(nc,), name="hg_fwd",
        in_specs=[col(0), col(1), col(2), pl.BlockSpec((1, BW), lambda c: (0, 0))],
        out_specs=[pl.BlockSpec((C, BW), lambda c: (c, 0)), pl.BlockSpec((None, BW, HG_D), lambda c: (c, 0, 0))],
        out_shape=[jax.ShapeDtypeStruct((L, BW), F32), jax.ShapeDtypeStruct((nc, BW, HG_D), F32)],
        scratch_shapes=[pltpu.VMEM((BW, HG_D), F32)],
        compiler_params=pltpu.CompilerParams(vmem_limit_bytes=VMEM_LIMIT, dimension_semantics=("arbitrary",)),
    )(qzv, qzv, qzv, lb)


def hg_bwd(qzv, lb, sst, do, L):
    C = HG_CHUNK
    nc = L // C

    def body(q_ref, z_ref, v_ref, lb_ref, sst_ref, do_ref, dq_ref, dz_ref, dv_ref, dlb_ref, dst_ref):
        @pl.when(pl.program_id(0) == 0)
        def _():
            dst_ref[...] = jnp.zeros_like(dst_ref)
            dlb_ref[...] = jnp.zeros_like(dlb_ref)

        _, vjp = jax.vjp(hg_chunk, sst_ref[...], q_ref[...], z_ref[...], v_ref[...], lb_ref[...])
        dst, dq, dz, dv, dlb = vjp((dst_ref[...], do_ref[...]))
        dst_ref[...] = dst
        dq_ref[...] = dq
        dz_ref[...] = dz
        dv_ref[...] = dv
        dlb_ref[...] += dlb

    col = lambda cb: pl.BlockSpec((C, BW), functools.partial(lambda c, cb: (nc - 1 - c, cb), cb=cb))
    rev = pl.BlockSpec((C, BW), lambda c: (nc - 1 - c, 0))
    return pl.pallas_call(
        body, grid=(nc,), name="hg_bwd",
        in_specs=[col(0), col(1), col(2), pl.BlockSpec((1, BW), lambda c: (0, 0)),
                  pl.BlockSpec((None, BW, HG_D), lambda c: (nc - 1 - c, 0, 0)), rev],
        out_specs=[rev, rev, rev, pl.BlockSpec((1, BW), lambda c: (0, 0))],
        out_shape=[jax.ShapeDtypeStruct((L, BW), F32)] * 3 + [jax.ShapeDtypeStruct((1, BW), F32)],
        scratch_shapes=[pltpu.VMEM((BW, HG_D), F32)],
        compiler_params=pltpu.CompilerParams(vmem_limit_bytes=VMEM_LIMIT, dimension_semantics=("arbitrary",)),
    )(qzv, qzv, qzv, lb, sst, do)


def _shift_down(x, d, rows, L):
    if d == 0:
        return x
    wrapped = jnp.where((rows & (NSEG - 1)) == 0, 0.0, pltpu.roll(x, NSEG * d + 1, 0))
    return jnp.where(rows < NSEG * d, wrapped, pltpu.roll(x, NSEG * d, 0))


def _shift_up(x, d, rows, L):
    if d == 0:
        return x
    wrapped = jnp.where((rows & (NSEG - 1)) == NSEG - 1, 0.0, pltpu.roll(x, L - (NSEG * d + 1), 0))
    return jnp.where(rows >= L - NSEG * d, wrapped, pltpu.roll(x, L - NSEG * d, 0))


def conv_fwd(proj, w, b, L):
    def body(x_ref, w_ref, b_ref, o_ref):
        x = x_ref[...]
        rows = lax.broadcasted_iota(jnp.int32, x.shape, 0)
        acc = jnp.broadcast_to(b_ref[...], x.shape)
        for k in range(CONV_W):
            acc = acc + w_ref[pl.ds(k, 1), :] * _shift_down(x, CONV_W - 1 - k, rows, L)
        o_ref[...] = acc

    nl = BW // LANE
    return pl.pallas_call(
        body, grid=(nl,), name="conv_fwd",
        in_specs=[pl.BlockSpec((L, LANE), lambda g: (0, 5 * nl + g)), pl.BlockSpec((CONV_W, LANE), lambda g: (0, g)),
                  pl.BlockSpec((1, LANE), lambda g: (0, g))],
        out_specs=pl.BlockSpec((L, LANE), lambda g: (0, g)),
        out_shape=jax.ShapeDtypeStruct((L, BW), F32),
        compiler_params=pltpu.CompilerParams(vmem_limit_bytes=VMEM_LIMIT),
    )(proj, w, b)


def conv_bwd(proj, w, dxc, L):
    def body(x_ref, w_ref, d_ref, dx_ref, dw_ref, db_ref):
        x, d = x_ref[...], d_ref[...]
        rows = lax.broadcasted_iota(jnp.int32, x.shape, 0)
        acc = jnp.zeros_like(x)
        for k in range(CONV_W):
            acc = acc + w_ref[pl.ds(k, 1), :] * _shift_up(d, CONV_W - 1 - k, rows, L)
            dw_ref[pl.ds(k, 1), :] = jnp.sum(d * _shift_down(x, CONV_W - 1 - k, rows, L), axis=0, keepdims=True)
        dx_ref[...] = acc
        db_ref[...] = jnp.sum(d, axis=0, keepdims=True)

    nl = BW // LANE
    blk = pl.BlockSpec((L, LANE), lambda g: (0, g))
    return pl.pallas_call(
        body, grid=(nl,), name="conv_bwd",
        in_specs=[pl.BlockSpec((L, LANE), lambda g: (0, 5 * nl + g)), pl.BlockSpec((CONV_W, LANE), lambda g: (0, g)), blk],
        out_specs=[blk, pl.BlockSpec((CONV_W, LANE), lambda g: (0, g)), pl.BlockSpec((1, LANE), lambda g: (0, g))],
        out_shape=[jax.ShapeDtypeStruct((L, BW), F32), jax.ShapeDtypeStruct((CONV_W, BW), F32),
                   jax.ShapeDtypeStruct((1, BW), F32)],
        compiler_params=pltpu.CompilerParams(vmem_limit_bytes=VMEM_LIMIT),
    )(proj, w, dxc)


def loss_fwd_bwd(x, fw, target, L, tm):
    def fn(x, fw, t):
        err = jnp.square(_rms(x, fw) - t)
        return jnp.sum(0.5 * jnp.mean(err, axis=-1, keepdims=True), axis=0, keepdims=True)

    def body(x_ref, fw_ref, t_ref, l_ref, dx_ref, dfw_ref):
        i = pl.program_id(0)
        t = t_ref[...]
        val, vjp = jax.vjp(lambda x, fw: fn(x, fw, t), x_ref[...], fw_ref[...])
        dx, dfw = vjp(jnp.ones((1, 1), F32))
        dx_ref[...] = dx

        @pl.when(i == 0)
        def _():
            l_ref[...] = jnp.zeros_like(l_ref)
            dfw_ref[...] = jnp.zeros_like(dfw_ref)

        l_ref[...] += jnp.broadcast_to(val, l_ref.shape)
        dfw_ref[...] += dfw

    row = pl.BlockSpec((tm, D_MODEL), lambda i: (i, 0))
    vec = pl.BlockSpec((1, D_MODEL), lambda i: (0, 0))
    return pl.pallas_call(
        body, grid=(L // tm,), name="loss_fwd_bwd", in_specs=[row, vec, row],
        out_specs=[pl.BlockSpec((1, LANE), lambda i: (0, 0)), row, vec],
        out_shape=[jax.ShapeDtypeStruct((1, LANE), F32), jax.ShapeDtypeStruct((L, D_MODEL), F32),
                   jax.ShapeDtypeStruct((1, D_MODEL), F32)],
        compiler_params=pltpu.CompilerParams(vmem_limit_bytes=VMEM_LIMIT, dimension_semantics=("arbitrary",)),
    )(x, fw, target)


def adamw(w, g, m, v):
    rows, cols = w.shape
    tr = _row_tile(rows, cols, budget=1024 * 1024)
    c1 = 1.0 - ADAM_B1 ** ADAM_STEP
    c2 = 1.0 - ADAM_B2 ** ADAM_STEP

    def body(w_ref, g_ref, m_ref, v_ref, d_ref, nm_ref, nv_ref):
        g = g_ref[...]
        nm = ADAM_B1 * m_ref[...] + (1.0 - ADAM_B1) * g
        nv = ADAM_B2 * v_ref[...] + (1.0 - ADAM_B2) * jnp.square(g)
        d_ref[...] = -ADAM_LR * ((nm / c1) / (jnp.sqrt(nv / c2) + ADAM_EPS) + ADAM_WD * w_ref[...])
        nm_ref[...] = nm
        nv_ref[...] = nv

    blk = pl.BlockSpec((tr, cols), lambda i: (i, 0))
    return pl.pallas_call(
        body, grid=(rows // tr,), name="adamw", in_specs=[blk] * 4, out_specs=[blk] * 3,
        out_shape=[jax.ShapeDtypeStruct((rows, cols), F32)] * 3,
    )(w, g, m, v)


def s5_prep(lam_re, lam_im, log_dt, b_re, b_im, c_re, c_im):
    lr = jnp.minimum(lam_re, -1e-4)
    li = lam_im
    dt = jnp.exp(log_dt)[:, None]
    mag = jnp.exp(lr * dt)
    ar = mag * jnp.cos(li * dt)
    ai = mag * jnp.sin(li * dt)
    den = lr * lr + li * li
    fr = ((ar - 1.0) * lr + ai * li) / den
    fi = (ai * lr - (ar - 1.0) * li) / den
    bbr = fr[..., None] * b_re - fi[..., None] * b_im
    bbi = fr[..., None] * b_im + fi[..., None] * b_re
    hg = S5_GROUPS // 2
    emb_b = lambda bb: _block_diag(bb.transpose(0, 2, 1).reshape(hg * S5_GROUP, S5_STATE), hg)
    emb_c = lambda cc: _block_diag(cc.transpose(0, 2, 1).reshape(hg * S5_STATE, S5_GROUP), hg)
    bsub = jnp.stack([emb_b(bbr[:hg]), emb_b(bbr[hg:]), emb_b(bbi[:hg]), emb_b(bbi[hg:])])
    csub = jnp.stack([jnp.concatenate([emb_c(c_re[:hg]), -emb_c(c_im[:hg])], axis=0),
                      jnp.concatenate([emb_c(c_re[hg:]), -emb_c(c_im[hg:])], axis=0)])
    nb = S5_N // LANE
    return ar.reshape(nb, 1, LANE), ai.reshape(nb, 1, LANE), bsub, csub


def _block_diag(stacked, groups):
    rows, c = stacked.shape
    r = rows // groups
    row_g = jnp.arange(rows)[:, None] // r
    col_g = jnp.arange(groups * c)[None, :] // c
    return jnp.where(row_g == col_g, jnp.tile(stacked, (1, groups)), 0.0)


def rg_prep(w):
    return _block_diag(w.reshape(BW, RG_BLOCK), RG_BLOCKS)


def hg_prep(logits):
    p = jax.nn.softmax(logits, axis=0)
    return jnp.cumsum(p, axis=0) - p[0]


def _head_mean_matrix():
    r = jnp.arange(BW) // HG_D
    return (r[:, None] == r[None, :]).astype(F32) / HG_D


def _to_segment_order(a):
    L = a.shape[0]
    return a.reshape(NSEG, L // NSEG, -1).transpose(1, 0, 2).reshape(a.shape)


def _to_time_order(a):
    L = a.shape[0]
    return a.reshape(L // NSEG, NSEG, -1).transpose(1, 0, 2).reshape(a.shape)


def _const(*idx):
    return lambda s, i: idx


def _rows(cb=0):
    return lambda s, i: (i, cb)


def _sum_parts(name, first, parts, shape):
    return add_n(name, [(first, ())] + [(parts, (s,)) for s in range(NSH)], shape)


def _ffn_weight_specs(l, j):
    F = D_FF // NSH
    one = pl.Buffered(1)
    return [pl.BlockSpec((None, NSH, D_MODEL, F), lambda i: (j, 0, 0, 0), pipeline_mode=one),
            pl.BlockSpec((None, NSH, D_MODEL, F), lambda i: (j, 0, 0, 0), pipeline_mode=one),
            pl.BlockSpec((None, NSH, F, D_MODEL), lambda i: (j, 0, 0, 0), pipeline_mode=one)]


def ffn_fwd(name, x, W, l, j, k, L, tm):
    D, F = D_MODEL, D_FF // NSH

    def body(x_ref, nw_ref, wg_ref, wu_ref, wd_ref, y_ref, g_ref, u_ref):
        x = x_ref[...]
        h = _rms(x, nw_ref[...]).astype(MMT)
        y = x
        for s in range(NSH):
            g = _dg(h, wg_ref[s], 1, 0)
            u = _dg(h, wu_ref[s], 1, 0)
            g_ref[s] = g.astype(g_ref.dtype)
            u_ref[s] = u.astype(u_ref.dtype)
            y = y + 0.5 * _dg((jax.nn.silu(g) * u).astype(MMT), wd_ref[s], 1, 0)
        y_ref[...] = y

    row = pl.BlockSpec((tm, D), lambda i: (i, 0))
    act = pl.BlockSpec((NSH, tm, F), lambda i: (0, i, 0))
    return pl.pallas_call(
        body, grid=(L // tm,), name=name,
        in_specs=[row, pl.BlockSpec((None, None, 1, D), lambda i: (l, k, 0, 0))] + _ffn_weight_specs(l, j),
        out_specs=[row, act, act],
        out_shape=[jax.ShapeDtypeStruct((L, D), F32), jax.ShapeDtypeStruct((NSH, L, F), MMT),
                   jax.ShapeDtypeStruct((NSH, L, F), MMT)],
        compiler_params=pltpu.CompilerParams(vmem_limit_bytes=VMEM_LIMIT, dimension_semantics=("arbitrary",)),
    )(x, W["nw"], W["L"][l]["wg"], W["L"][l]["wu"], W["L"][l]["wd"])


def ffn_bwd(name, x, g, u, dy, W, bufs, l, j, k, L, tm):
    D, F = D_MODEL, D_FF // NSH
    tm = min(TM_WGRAD, L)

    def body(x_ref, nw_ref, dy_ref, g_ref, u_ref, wg_ref, wu_ref, wd_ref, *rest):
        part_ref, dnw_ref, dwg_ref, dwu_ref, dwd_ref = rest[-5:]
        s, i = pl.program_id(0), pl.program_id(1)
        x, nw = x_ref[...], nw_ref[...]
        r = lax.rsqrt(jnp.mean(x * x, axis=-1, keepdims=True) + EPS)
        xhat = x * r
        h = (xhat * nw).astype(MMT)
        half_dy = (0.5 * dy_ref[...]).astype(MMT)
        gs, us = g_ref[...].astype(F32), u_ref[...].astype(F32)
        sig = jax.nn.sigmoid(gs)
        act = gs * sig
        da = _dg(half_dy, wd_ref[...], 1, 1)
        du = (da * act).astype(MMT)
        dg = (da * us * (sig * (1.0 + gs * (1.0 - sig)))).astype(MMT)
        dh = _dg(dg, wg_ref[...], 1, 1) + _dg(du, wu_ref[...], 1, 1)
        dxh = dh * nw
        part_ref[...] = r * (dxh - xhat * jnp.mean(dxh * xhat, axis=-1, keepdims=True))
        grads = (_dg(h, dg, 0, 0), _dg(h, du, 0, 0), _dg((act * us).astype(MMT), half_dy, 0, 0))
        dnw = jnp.sum(dh * xhat, axis=0, keepdims=True)
        first = jnp.logical_and(s == 0, i == 0)
        for ref, val, start in zip((dwg_ref, dwu_ref, dwd_ref, dnw_ref), grads + (dnw,), (i == 0, i == 0, i == 0, first)):
            @pl.when(start)
            def _(ref=ref, val=val):
                ref[...] = val

            @pl.when(jnp.logical_not(start))
            def _(ref=ref, val=val):
                ref[...] += val

    row = pl.BlockSpec((tm, D), lambda s, i: (i, 0))
    act = pl.BlockSpec((None, tm, F), lambda s, i: (s, i, 0))
    wsp = lambda r, c: pl.BlockSpec((None, None, r, c), lambda s, i: (j, s, 0, 0))
    gsp = lambda r, c: pl.BlockSpec((None, None, r, c), lambda s, i: (0, s, 0, 0))
    part, dnw, bufs[("ffn_gate", l, j)], bufs[("ffn_up", l, j)], bufs[("ffn_down", l, j)] = pl.pallas_call(
        body, grid=(NSH, L // tm), name=name,
        in_specs=[row, pl.BlockSpec((None, None, 1, D), lambda s, i: (l, k, 0, 0)), row, act, act,
                  wsp(D, F), wsp(D, F), wsp(F, D)],
        out_specs=[pl.BlockSpec((None, tm, D), lambda s, i: (s, i, 0)), pl.BlockSpec((1, D), lambda s, i: (0, 0)),
                   gsp(D, F), gsp(D, F), gsp(F, D)],
        out_shape=[jax.ShapeDtypeStruct((NSH, L, D), F32), jax.ShapeDtypeStruct((1, D), F32)]
        + [jax.ShapeDtypeStruct((1, NSH, D, F), F32)] * 2 + [jax.ShapeDtypeStruct((1, NSH, F, D), F32)],
        compiler_params=pltpu.CompilerParams(vmem_limit_bytes=VMEM_LIMIT, dimension_semantics=("arbitrary", "arbitrary")),
    )(x, W["nw"], dy, g, u, W["L"][l]["wg"], W["L"][l]["wu"], W["L"][l]["wd"])
    return _sum_parts(name + "_dx", dy, part, (L, D)), dnw


def layer_fwd(l, x0, W, P, L, tm):
    D = D_MODEL
    tmm = tm
    tm = min(TM_FWD, L)
    n_i = L // tm
    x1, g0, u0 = ffn_fwd(f"ffn_fwd_{l}0", x0, W, l, 0, 0, L, tm)
    proj = tile_fwd(
        lambda x, nw, win, s: pre_core(x, nw, win), f"pre_fwd_{l}", n_i, NSH,
        [(x1, (tm, D), _rows()), (W["nw"], (None, None, 1, D), _const(l, 1, 0, 0)),
         (W["L"][l]["win"], (None, D, IN_TOTAL // NSH), lambda s, i: (s, 0, 0))],
        [((L, IN_TOTAL), F32, (tm, IN_TOTAL // NSH), lambda s, i: (i, s), False)], s_outer=True)[0]
    nb = S5_N // LANE
    blk3 = lambda s, i: (0, i, 0)
    bur, bui = tile_fwd(
        lambda *a: s5_pre_core(*a[:-1]), f"s5pre_fwd_{l}", n_i, 1,
        [(proj, (tm, BW), _rows(0))] + [(P["bsub"], (None, None, BW // 2, S5_N // 2), _const(l, q, 0, 0)) for q in range(4)],
        [((nb, L, LANE), F32, (nb, tm, LANE), blk3, False)] * 2)
    xr, xi = s5_scan_fwd(bur, bui, P["ar"][l], P["ai"][l], L)
    qzv = _to_time_order(proj[:, BW:4 * BW])
    o_t, sst = hg_fwd(qzv, P["lb"][l], L)
    o = _to_segment_order(o_t)
    xc = conv_fwd(proj, W["convw"][l], P["convb"][l], L)
    vec = (None, 1, BW)
    a, b = tile_fwd(
        lambda xc, wa, ba, wx, bx, lam, s: gates_core(xc, wa, ba, wx, bx, lam), f"gates_fwd_{l}", n_i, 1,
        [(xc, (tm, BW), _rows()), (P["wa"], (None, BW, BW), _const(l, 0, 0)), (P["ba"], vec, _const(l, 0, 0)),
         (P["wx"], (None, BW, BW), _const(l, 0, 0)), (P["bx"], vec, _const(l, 0, 0)), (P["lam"], vec, _const(l, 0, 0))],
        [((L, BW), F32, (tm, BW), _rows(), False)] * 2)
    hs = rg_scan_fwd(a, b, L)
    ya, yb, yc = tile_fwd(
        lambda *a: mid_core(*a[:-1]), f"mid_fwd_{l}", L // tmm, 1,
        [(xr, (nb, tmm, LANE), blk3), (xi, (nb, tmm, LANE), blk3), (proj, (tmm, BW), _rows(0)), (o, (tmm, BW), _rows()),
         (proj, (tmm, BW), _rows(4)), (hs, (tmm, BW), _rows()), (proj, (tmm, BW), _rows(6)),
         (P["hmat"], (BW, BW), _const(0, 0)), (P["csub"], (None, None, S5_N, BW // 2), _const(l, 0, 0, 0)),
         (P["csub"], (None, None, S5_N, BW // 2), _const(l, 1, 0, 0)), (P["d"], vec, _const(l, 0, 0)),
         (W["L"][l]["gluw"], (BW, BW), _const(0, 0)), (P["glub"], vec, _const(l, 0, 0)), (P["hgw"], vec, _const(l, 0, 0))],
        [((L, BW), F32, (tmm, BW), _rows(), False)] * 3)
    x2 = tile_fwd(
        lambda x, *rest: (x + merge_core(*rest[:-1])[0],), f"merge_fwd_{l}", n_i, 1,
        [(x1, (tm, D), _rows()), (ya, (tm, BW), _rows()), (yb, (tm, BW), _rows()), (yc, (tm, BW), _rows())]
        + [(proj, (tm, BW), _rows(7 + k)) for k in range(6)]
        + [(W["L"][l]["pfull"], (3, BW, D), _const(0, 0, 0)), (W["L"][l]["woutfull"], (D, D), _const(0, 0))],
        [((L, D), F32, (tm, D), _rows(), False)])[0]
    x3, g1, u1 = ffn_fwd(f"ffn_fwd_{l}1", x2, W, l, 1, 2, L, tm)
    saved = dict(x0=x0, x1=x1, x2=x2, proj=proj, xr=xr, xi=xi, o=o, sst=sst, xc=xc, a=a, hs=hs, ya=ya, yb=yb, yc=yc,
                 qzv=qzv, g0=g0, u0=u0, g1=g1, u1=u1)
    return x3, saved


def layer_bwd(l, dx3, sv, W, P, bufs, L, tm, ready=lambda l, group: None):
    D = D_MODEL
    n_i = L // tm
    nb = S5_N // LANE
    dq = D // NSH
    vec = (None, 1, BW)
    vout = ((1, BW), (1, BW), _const(0, 0), "acc_all")
    blk3 = lambda s, i: (0, i, 0)
    small = {}
    proj = sv["proj"]

    dx2, dnw2 = ffn_bwd(f"ffn_bwd_{l}1", sv["x2"], sv["g1"], sv["u1"], dx3, W, bufs, l, 1, 2, L, tm)
    ready(l, "ffn1")

    rw256 = ((L, BW), (tm, BW), _rows(), "write")
    res = tile_bwd(
        merge_core, f"merge_bwd_{l}", n_i, 1,
        [(sv["ya"], (tm, BW), _rows(), "r"), (sv["yb"], (tm, BW), _rows(), "r"), (sv["yc"], (tm, BW), _rows(), "r")]
        + [(proj, (tm, BW), _rows(7 + k), "r") for k in range(6)]
        + [(W["L"][l]["pfull"], (3, BW, D), _const(0, 0, 0), "w"), (W["L"][l]["woutfull"], (D, D), _const(0, 0), "w")],
        [(dx2, (tm, D), _rows())],
        [rw256] * 9
        + [((3, BW, D), (3, BW, D), _const(0, 0, 0), "acc_all"), ((D, D), (D, D), _const(0, 0), "acc_all")])
    dya, dyb, dyc = res[:3]
    dgm = res[3:9]
    bufs[("branch_proj", l)], bufs[("w_out", l)] = res[9:]
    ready(l, "merge")

    tmm = tm
    rw = ((L, BW), (tmm, BW), _rows(), "write")
    xw = ((nb, L, LANE), (nb, tmm, LANE), blk3, "write")
    res = tile_bwd(
        mid_core, f"mid_bwd_{l}", L // tmm, 1,
        [(sv["xr"], (nb, tmm, LANE), blk3, "r"), (sv["xi"], (nb, tmm, LANE), blk3, "r"), (proj, (tmm, BW), _rows(0), "r"),
         (sv["o"], (tmm, BW), _rows(), "r"), (proj, (tmm, BW), _rows(4), "r"), (sv["hs"], (tmm, BW), _rows(), "r"),
         (proj, (tmm, BW), _rows(6), "r"), (P["hmat"], (BW, BW), _const(0, 0), "c"),
         (P["csub"], (None, None, S5_N, BW // 2), _const(l, 0, 0, 0), "w"),
         (P["csub"], (None, None, S5_N, BW // 2), _const(l, 1, 0, 0), "w"), (P["d"], vec, _const(l, 0, 0), "p"),
         (W["L"][l]["gluw"], (BW, BW), _const(0, 0), "w"), (P["glub"], vec, _const(l, 0, 0), "p"),
         (P["hgw"], vec, _const(l, 0, 0), "p")],
        [(dya, (tmm, BW), _rows()), (dyb, (tmm, BW), _rows()), (dyc, (tmm, BW), _rows())],
        [xw, xw, rw, rw, rw, rw, rw,
         ((DEPTH, S5_N, BW // 2), (None, S5_N, BW // 2), _const(l, 0, 0), "acc_all", bufs.get("csub0")),
         ((DEPTH, S5_N, BW // 2), (None, S5_N, BW // 2), _const(l, 0, 0), "acc_all", bufs.get("csub1")), vout,
         ((BW, BW), (BW, BW), _const(0, 0), "acc_all"), vout, vout])
    dxr, dxi, du_skip, do, dg_b, dhs, dgate_c, bufs["csub0"], bufs["csub1"], dd, bufs[("s5_glu_w", l)], dglub, dhgw = res
    small["s5_d"], small["s5_glu_b"], small["hg_norm_w"] = dd[0], dglub[0], dhgw[0]
    ready(l, "mid")

    da, db = rg_scan_bwd(sv["a"], sv["hs"], dhs, L)
    wmat = lambda key: ((DEPTH, BW, BW), (None, BW, BW), _const(l, 0, 0), "acc_all", bufs.get(key))
    res = tile_bwd(
        gates_core, f"gates_bwd_{l}", n_i, 1,
        [(sv["xc"], (tm, BW), _rows(), "r"), (P["wa"], (None, BW, BW), _const(l, 0, 0), "w"), (P["ba"], vec, _const(l, 0, 0), "p"),
         (P["wx"], (None, BW, BW), _const(l, 0, 0), "w"), (P["bx"], vec, _const(l, 0, 0), "p"), (P["lam"], vec, _const(l, 0, 0), "p")],
        [(da, (tm, BW), _rows()), (db, (tm, BW), _rows())],
        [((L, BW), (tm, BW), _rows(), "write"), wmat("wa"), vout, wmat("wx"), vout, vout])
    dxc, bufs["wa"], dba, bufs["wx"], dbx, dlam = res
    small["rg_ba"], small["rg_bx"], small["rg_lambda"] = dba[0], dbx[0], dlam[0]
    dx_c, dconvw, dconvb = conv_bwd(proj, W["convw"][l], dxc, L)
    small["rg_conv_w"], small["rg_conv_b"] = dconvw, dconvb[0]

    dq_b, dz_b, dv_b, dlb = hg_bwd(sv["qzv"], P["lb"][l], sv["sst"], _to_time_order(do), L)
    dq_b, dz_b, dv_b = [_to_segment_order(a) for a in (dq_b, dz_b, dv_b)]

    gr, gi, dar, dai = s5_scan_bwd(dxr, dxi, sv["xr"], sv["xi"], P["ar"][l], P["ai"][l], L)
    bblk = (None, None, BW // 2, S5_N // 2)
    res = tile_bwd(
        s5_pre_core, f"s5pre_bwd_{l}", n_i, 1,
        [(proj, (tm, BW), _rows(0), "r")] + [(P["bsub"], bblk, _const(l, q, 0, 0), "w") for q in range(4)],
        [(gr, (nb, tm, LANE), blk3), (gi, (nb, tm, LANE), blk3)],
        [((L, BW), (tm, BW), _rows(), "write")]
        + [((DEPTH, BW // 2, S5_N // 2), bblk[1:], _const(l, 0, 0), "acc_all", bufs.get(f"bsub{q}")) for q in range(4)])
    du_pre = res[0]
    for q in range(4):
        bufs[f"bsub{q}"] = res[1 + q]
    du_a = add_n(f"du_a_{l}", [(du_skip, ()), (du_pre, ())], (L, BW))
    prep_ct = dict(dar=dar, dai=dai, dlb=dlb)

    pieces = [du_a, dq_b, dz_b, dv_b, dg_b, dx_c, dgate_c, *dgm]
    per_piece, per_shard = BW // LANE, IN_TOTAL // NSH // LANE
    dx1, dnw1 = dx2, []
    tmw = min(TM_WGRAD, L)
    pre_and_x = lambda x, nw, win: (pre_core(x, nw, win)[0], x)
    for s in range(NSH):
        groups = [(pieces[g // per_piece], (tmw, LANE), _rows(g % per_piece))
                  for g in range(s * per_shard, (s + 1) * per_shard)]
        dx1, dnw_s, bufs[("w_in", l)] = tile_bwd(
            pre_and_x, f"pre_bwd_{l}{s}", L // tmw, 1,
            [(sv["x1"], (tmw, D), _rows(), "r"), (W["nw"], (None, None, 1, D), _const(l, 1, 0, 0), "p"),
             (W["L"][l]["win"], (None, D, IN_TOTAL // NSH), _const(s, 0, 0), "w")],
            [groups, (dx1, (tmw, D), _rows())],
            [((L, D), (tmw, D), _rows(), "write"),
             ((1, D), (1, D), _const(0, 0), "acc_all"),
             ((1, NSH, D, IN_TOTAL // NSH), (None, None, D, IN_TOTAL // NSH), _const(0, s, 0, 0), "acc_all",
              bufs.get(("w_in", l)))])
        dnw1.append(dnw_s)
    dnw1 = (dnw1[0] + dnw1[1]) + (dnw1[2] + dnw1[3])
    ready(l, "pre")

    dx0, dnw0 = ffn_bwd(f"ffn_bwd_{l}0", sv["x0"], sv["g0"], sv["u0"], dx1, W, bufs, l, 0, 0, L, tm)
    ready(l, "ffn0")
    small["norm_w"] = jnp.concatenate([dnw0, dnw1, dnw2], axis=0)
    return dx0, small, prep_ct


SMALL_RAW = ("s5_lambda_re", "s5_lambda_im", "s5_log_dt", "s5_b_re", "s5_b_im", "s5_c_re", "s5_c_im", "s5_d", "s5_glu_b",
             "hg_lb_logits", "hg_norm_w", "rg_conv_b", "rg_wa", "rg_ba", "rg_wx", "rg_bx", "rg_lambda", "final_norm_w")
DEPTH = 2


def local_step(x, target, W, raw, layer_weights=None, layer_grads=None):
    L = x.shape[0]
    tm = min(256, L)
    col = lambda v: v.reshape(DEPTH, 1, BW)
    (ar, ai, bsub, csub), s5_vjp = jax.vjp(jax.vmap(s5_prep), *[raw[k] for k in SMALL_RAW[:7]])
    (wa, wx), rg_vjp = jax.vjp(lambda a, b: (jax.vmap(rg_prep)(a), jax.vmap(rg_prep)(b)), raw["rg_wa"], raw["rg_wx"])
    lb, hg_vjp = jax.vjp(hg_prep, raw["hg_lb_logits"])
    P = dict(
        ar=[ar[l] for l in range(DEPTH)], ai=[ai[l] for l in range(DEPTH)],
        bsub=bsub.astype(MMT), csub=csub.astype(MMT), wa=wa.astype(MMT), wx=wx.astype(MMT),
        lb=[lb[l].reshape(1, BW) for l in range(DEPTH)], convb=[raw["rg_conv_b"][l].reshape(1, BW) for l in range(DEPTH)],
        ba=col(raw["rg_ba"]), bx=col(raw["rg_bx"]), lam=col(raw["rg_lambda"]), d=col(raw["s5_d"]),
        glub=col(raw["s5_glu_b"]), hgw=col(raw["hg_norm_w"]), hmat=_head_mean_matrix())

    saved = []
    h = _to_segment_order(x)
    for l in range(DEPTH):
        if layer_weights is not None:
            W["L"][l], h = layer_weights(l, h)
        h, sv = layer_fwd(l, h, W, P, L, tm)
        saved.append(sv)
    loss, dh, dfw = loss_fwd_bwd(h, raw["final_norm_w"].reshape(1, D_MODEL), _to_segment_order(target), L, tm)

    big, per_layer, prep_cts = {}, [None] * DEPTH, [None] * DEPTH
    ready = (lambda l, group: None) if layer_grads is None else (lambda l, group: layer_grads(l, group, big))
    for l in reversed(range(DEPTH)):
        dh, sm, pc = layer_bwd(l, dh, saved[l], W, P, big, L, tm, ready)
        per_layer[l], prep_cts[l] = sm, pc
    dh = _to_time_order(dh)

    small = {k: jnp.stack([per_layer[l][k] for l in range(DEPTH)]) for k in per_layer[0]}
    both = lambda k: jnp.stack([prep_cts[l][k] for l in range(DEPTH)])
    dbsub = jnp.stack([big.pop(f"bsub{q}") for q in range(4)], axis=1)
    dcsub = jnp.stack([big.pop("csub0"), big.pop("csub1")], axis=1)
    s5_g = s5_vjp((both("dar"), both("dai"), dbsub, dcsub))
    small.update(zip(SMALL_RAW[:7], s5_g))
    small["rg_wa"], small["rg_wx"] = rg_vjp((big.pop("wa"), big.pop("wx")))
    (small["hg_lb_logits"],) = hg_vjp(jnp.concatenate([prep_cts[l]["dlb"] for l in range(DEPTH)], axis=0))
    small["final_norm_w"] = dfw[0]
    return loss, dh, big, small


ANY = pl.BlockSpec(memory_space=pl.ANY)


def _place():
    x, y, c = lax.axis_index("x"), lax.axis_index("y"), lax.axis_index("c")
    chips = [(1 - x, y), (x, 1 - y), (1 - x, 1 - y)]
    return x, y, c, chips


def _remote(src, dst, send, recv, k, to):
    return pltpu.make_async_remote_copy(src_ref=src, dst_ref=dst, send_sem=send.at[k], recv_sem=recv.at[k],
                                        device_id=to, device_id_type=MESH)


def _comm_call(body, name, ins, out_shapes, n_sem, n_loc):
    return pl.pallas_call(
        body, name=name, in_specs=[ANY] * len(ins), out_specs=[ANY] * len(out_shapes), out_shape=out_shapes,
        scratch_shapes=[pltpu.SemaphoreType.DMA((n_sem,)), pltpu.SemaphoreType.DMA((n_sem,)),
                        pltpu.SemaphoreType.DMA((max(n_loc, 1),))],
    )(*ins)


def gather_shards(name, shards):
    n = len(shards)
    per = 8

    def body(*refs):
        ins, outs = refs[:n], refs[n:2 * n]
        send, recv, _ = refs[2 * n:]
        x, y, c, chips = _place()
        me = 2 * x + y
        sib = (x, y, 1 - c)
        sends = []
        for w in range(n):
            for j, (cx, cy) in enumerate(chips):
                cp = _remote(ins[w].at[c], outs[w].at[c, me], send, recv, per * w + j, (cx, cy, c))
                cp.start()
                sends.append(cp)
        for w in range(n):
            for l in range(2):
                cp = _remote(ins[w].at[l], outs[w].at[l, me], send, recv, per * w + 6 + l, sib)
                cp.start()
                sends.append(cp)
        for w in range(n):
            for j, (cx, cy) in enumerate(chips):
                theirs = outs[w].at[c, 2 * cx + cy]
                _remote(ins[w].at[c], theirs, send, recv, per * w + j, (cx, cy, c)).wait_recv()
                cp = _remote(theirs, theirs, send, recv, per * w + 3 + j, sib)
                cp.start()
                sends.append(cp)
        for w in range(n):
            for j, (cx, cy) in enumerate(chips):
                dst = outs[w].at[1 - c, 2 * cx + cy]
                _remote(dst, dst, send, recv, per * w + 3 + j, sib).wait_recv()
            for l in range(2):
                dst = outs[w].at[l, me]
                _remote(dst, dst, send, recv, per * w + 6 + l, sib).wait_recv()
        for cp in sends:
            cp.wait_send()

    shapes = [jax.ShapeDtypeStruct((2, NSH) + s.shape[1:], s.dtype) for s in shards]
    return _comm_call(body, name, shards, shapes, per * n, 0)


def exchange_halves(name, grads, ranges):
    n = len(grads)

    def body(*refs):
        ins, outs = refs[:n], refs[n:2 * n]
        send, recv, _ = refs[2 * n:]
        x, y, c, _chips = _place()
        cps = []
        for w in range(n):
            h = grads[w].shape[2] // 2
            p0, np_ = ranges[w]
            cp = _remote(ins[w].at[pl.ds(p0, np_), :, pl.ds((1 - c) * h, h)], outs[w], send, recv, w, (x, y, 1 - c))
            cp.start()
            cps.append(cp)
        for cp in cps:
            cp.wait()

    shapes = [jax.ShapeDtypeStruct((r[1], NSH, g.shape[2] // 2, g.shape[3]), g.dtype) for g, r in zip(grads, ranges)]
    return _comm_call(body, name, grads, shapes, n, 0)


def share_halves(name, pieces):
    n = len(pieces)

    def body(*refs):
        ins, outs = refs[:n], refs[n:2 * n]
        send, recv, _ = refs[2 * n:]
        x, y, c, _chips = _place()
        cps = []
        for w in range(n):
            cp = _remote(ins[w], outs[w], send, recv, w, (x, y, 1 - c))
            cp.start()
            cps.append(cp)
        for cp in cps:
            cp.wait()

    return _comm_call(body, name, pieces, [jax.ShapeDtypeStruct(p.shape, p.dtype) for p in pieces], n, 0)


def add_own_half(name, g, ra, c, wire, b0):
    nblk, h, cols = ra.shape
    tr = _row_tile(h, cols, mult=16)
    nt = h // tr

    def body(c_ref, g_ref, r_ref, o_ref):
        o_ref[...] = (g_ref[...] + r_ref[...]).astype(o_ref.dtype)

    blk = (None, tr, cols)
    return pl.pallas_call(
        body, name=name,
        grid_spec=pltpu.PrefetchScalarGridSpec(
            num_scalar_prefetch=1, grid=(nblk, nt),
            in_specs=[pl.BlockSpec(blk, lambda s, i, c_ref: (b0 + s, c_ref[0] * nt + i, 0)), pl.BlockSpec(blk, lambda s, i, c_ref: (s, i, 0))],
            out_specs=pl.BlockSpec(blk, lambda s, i, c_ref: (s, i, 0))),
        out_shape=jax.ShapeDtypeStruct(ra.shape, wire),
    )(c.reshape(1), g, ra)


def add_chips(name, hb, rb, me):
    npc, _, h, cols = hb.shape
    tr = _row_tile(h, cols, mult=16)

    def body(me_ref, h_ref, r0, r1, r2, o_ref):
        f = lambda r: r[...].astype(F32)
        o_ref[...] = ((f(h_ref) + f(r0)) + f(r1)) + f(r2)

    rspec = lambda j: pl.BlockSpec((None, None, tr, cols), functools.partial(lambda p, i, me_ref, j: (j, p, i, 0), j=j))
    return pl.pallas_call(
        body, name=name,
        grid_spec=pltpu.PrefetchScalarGridSpec(
            num_scalar_prefetch=1, grid=(npc, h // tr),
            in_specs=[pl.BlockSpec((None, None, tr, cols), lambda p, i, me_ref: (p, me_ref[0], i, 0)), rspec(0), rspec(1), rspec(2)],
            out_specs=pl.BlockSpec((None, tr, cols), lambda p, i, me_ref: (p, i, 0))),
        out_shape=jax.ShapeDtypeStruct((npc, h, cols), F32),
    )(me.reshape(1), hb, rb, rb, rb)


def adamw_halves(name, w, m, v, own, other, c):
    npc, rows, cols = w.shape
    h = rows // 2
    tr = _row_tile(h, cols, budget=1024 * 1024)
    nt = h // tr
    c1 = 1.0 - ADAM_B1 ** ADAM_STEP
    c2 = 1.0 - ADAM_B2 ** ADAM_STEP

    def body(c_ref, w_ref, m_ref, v_ref, own_ref, oth_ref, g_ref, d_ref, nm_ref, nv_ref):
        g = jnp.where(pl.program_id(1) == c_ref[0], own_ref[...], oth_ref[...])
        nm = ADAM_B1 * m_ref[...] + (1.0 - ADAM_B1) * g
        nv = ADAM_B2 * v_ref[...] + (1.0 - ADAM_B2) * jnp.square(g)
        g_ref[...] = g
        d_ref[...] = -ADAM_LR * ((nm / c1) / (jnp.sqrt(nv / c2) + ADAM_EPS) + ADAM_WD * w_ref[...])
        nm_ref[...] = nm
        nv_ref[...] = nv

    full = pl.BlockSpec((None, tr, cols), lambda p, hh, i, c_ref: (p, hh * nt + i, 0))
    own_spec = pl.BlockSpec((None, tr, cols), lambda p, hh, i, c_ref: (
        p, jnp.where(hh == c_ref[0], i, jnp.where(c_ref[0] == 1, 0, nt - 1)), 0))
    other_spec = pl.BlockSpec((None, tr, cols), lambda p, hh, i, c_ref: (
        p, jnp.where(hh != c_ref[0], i, jnp.where(c_ref[0] == 0, 0, nt - 1)), 0))
    return pl.pallas_call(
        body, name=name,
        grid_spec=pltpu.PrefetchScalarGridSpec(
            num_scalar_prefetch=1, grid=(npc, 2, nt),
            in_specs=[full, full, full, own_spec, other_spec], out_specs=[full] * 4),
        out_shape=[jax.ShapeDtypeStruct(w.shape, F32)] * 4,
    )(c.reshape(1), w, m, v, own, other)


WEIGHTS = ("norm_w", "final_norm_w", "ffn_gate", "ffn_up", "ffn_down", "w_in", "branch_proj", "w_out", "s5_lambda_re",
           "s5_lambda_im", "s5_log_dt", "s5_b_re", "s5_b_im", "s5_c_re", "s5_c_im", "s5_d", "s5_glu_w", "s5_glu_b",
           "hg_lb_logits", "hg_norm_w", "rg_conv_w", "rg_conv_b", "rg_wa", "rg_ba", "rg_wx", "rg_bx", "rg_lambda")
BIG = ("ffn_gate", "ffn_up", "ffn_down", "w_in", "branch_proj", "w_out", "s5_glu_w")
SHARDED_SMALL = ("norm_w", "rg_conv_w")
SMALL = SMALL_RAW + SHARDED_SMALL


def _view2d(shape):
    return (1, shape[0]) if len(shape) == 1 else (math.prod(shape[:-1]), shape[-1])


def _small_layout(shapes, row_multiple):
    layout, at = [], 0
    for shape in shapes:
        r, c = _view2d(shape)
        rp = -(-r // 8) * 8
        layout.append((at, r, c, rp))
        at += rp * max(1, c // LANE)
    return layout, -(-at // row_multiple) * row_multiple


def pack_small(name, arrays, row_multiple):
    layout, rows = _small_layout([a.shape for a in arrays], row_multiple)

    def body(*refs):
        out = refs[-1]
        out[...] = jnp.zeros_like(out)
        for ref, (r0, r, c, rp) in zip(refs[:-1], layout):
            if c <= LANE:
                out[r0:r0 + r, 0:c] = ref[...]
            else:
                for q in range(c // LANE):
                    out[r0 + q * rp:r0 + q * rp + r, :] = ref[:, q * LANE:(q + 1) * LANE]

    return pl.pallas_call(
        body, name=name, out_shape=jax.ShapeDtypeStruct((rows, LANE), F32),
        compiler_params=pltpu.CompilerParams(vmem_limit_bytes=VMEM_LIMIT),
    )(*[a.reshape(_view2d(a.shape)) for a in arrays])


def unpack_small(name, packed, shapes):
    layout, _ = _small_layout(shapes, 8)

    def body(p_ref, *outs):
        for ref, (r0, r, c, rp) in zip(outs, layout):
            if c <= LANE:
                ref[...] = p_ref[r0:r0 + r, 0:c]
            else:
                for q in range(c // LANE):
                    ref[:, q * LANE:(q + 1) * LANE] = p_ref[r0 + q * rp:r0 + q * rp + r, :]

    res = pl.pallas_call(
        body, name=name, out_shape=[jax.ShapeDtypeStruct(_view2d(s), F32) for s in shapes],
        compiler_params=pltpu.CompilerParams(vmem_limit_bytes=VMEM_LIMIT),
    )(packed)
    return [a.reshape(s) for a, s in zip(res, shapes)]


HBM = pl.BlockSpec(memory_space=pltpu.HBM)
SEM = pl.BlockSpec(memory_space=pltpu.SEMAPHORE)
EFFECT = pltpu.SideEffectType.DATAFLOW_SIDE_EFFECTING


def split_start(name, srcs, land_shapes, plan, n_send, n_recv):
    ns, nl = len(srcs), len(land_shapes)

    def body(*refs):
        ins, lands = refs[:ns], refs[ns:ns + nl]
        send, recv = refs[ns + nl], refs[ns + nl + 1]
        for src, dst, ks, kr, dev in plan(ins, lands):
            pltpu.make_async_remote_copy(src_ref=src, dst_ref=dst, send_sem=send.at[ks], recv_sem=recv.at[kr],
                                         device_id=dev, device_id_type=MESH).start()
        refs[-1][...] = jnp.zeros_like(refs[-1])

    hbm = lambda a: pltpu.with_memory_space_constraint(a, pltpu.HBM)
    lands = [lax.empty(s.shape, s.dtype) for s in land_shapes]
    out = pl.pallas_call(
        body, name=name,
        out_shape=(pltpu.SemaphoreType.DMA((n_send,)), pltpu.SemaphoreType.DMA((n_recv,)),
                   *[pltpu.HBM(a.shape, a.dtype) for a in srcs], *[pltpu.HBM(s.shape, s.dtype) for s in land_shapes],
                   jax.ShapeDtypeStruct((8, LANE), F32)),
        in_specs=[HBM] * (ns + nl), out_specs=(SEM, SEM, *[HBM] * (ns + nl), pl.BlockSpec(memory_space=pltpu.VMEM)),
        input_output_aliases={k: 2 + k for k in range(ns + nl)},
        compiler_params=pltpu.CompilerParams(has_side_effects=EFFECT),
    )(*[hbm(a) for a in srcs], *[hbm(a) for a in lands])
    return out[:-1], out[-1]


def split_wait(name, handles, n_src, waits, after):
    send, recv, *bufs = handles
    nb = len(bufs)

    def body(*refs):
        ins, lands = refs[:n_src], refs[n_src:nb]
        send_sem, recv_sem = refs[nb], refs[nb + 1]
        x, y, c, _chips = _place()
        sends, recvs = waits(ins, lands)
        for src, k in sends:
            pltpu.make_async_remote_copy(src_ref=src, dst_ref=src, send_sem=send_sem.at[k], recv_sem=recv_sem.at[0],
                                         device_id=(x, y, 1 - c), device_id_type=MESH).wait_send()
        for dst, k in recvs:
            pltpu.make_async_remote_copy(src_ref=dst, dst_ref=dst, send_sem=send_sem.at[0], recv_sem=recv_sem.at[k],
                                         device_id=(x, y, 1 - c), device_id_type=MESH).wait_recv()

    out = pl.pallas_call(
        body, name=name, out_shape=tuple(pltpu.HBM(a.shape, a.dtype) for a in bufs),
        in_specs=[HBM] * nb + [SEM, SEM, ANY], out_specs=tuple([HBM] * nb),
        input_output_aliases={k: k for k in range(nb)},
        compiler_params=pltpu.CompilerParams(has_side_effects=EFFECT),
    )(*bufs, send, recv, after)
    return list(out[:n_src]), list(out[n_src:])


def gather_plan(n):
    def plan(ins, lands):
        x, y, c, chips = _place()
        me = 2 * x + y
        copies = []
        for w in range(n):
            for j, (cx, cy) in enumerate(chips):
                for t in range(2):
                    copies.append((ins[w].at[c], lands[w].at[c, me], 8 * w + 2 * j + t, 8 * w + 2 * j + c, (cx, cy, t)))
            for half in range(2):
                copies.append((ins[w].at[half], lands[w].at[half, me], 8 * w + 6 + half, 8 * w + 6 + half, (x, y, 1 - c)))
        return copies

    def waits(ins, lands):
        x, y, c, chips = _place()
        me = 2 * x + y
        sends, recvs = [], []
        for w in range(n):
            for j, (cx, cy) in enumerate(chips):
                for t in range(2):
                    sends.append((ins[w].at[c], 8 * w + 2 * j + t))
                    recvs.append((lands[w].at[t, 2 * cx + cy], 8 * w + 2 * j + t))
            for half in range(2):
                sends.append((ins[w].at[half], 8 * w + 6 + half))
                recvs.append((lands[w].at[half, me], 8 * w + 6 + half))
        return sends, recvs

    return plan, waits


def scatter_plan(n):
    def plan(ins, lands):
        x, y, c, chips = _place()
        return [(ins[w].at[:, 2 * cx + cy], lands[w].at[j], 3 * w + j, 3 * w + j, (cx, cy, c))
                for w in range(n) for j, (cx, cy) in enumerate(chips)]

    def waits(ins, lands):
        x, y, c, chips = _place()
        sends = [(ins[w].at[:, 2 * cx + cy], 3 * w + j) for w in range(n) for j, (cx, cy) in enumerate(chips)]
        recvs = [(lands[w].at[j], 3 * w + j) for w in range(n) for j in range(3)]
        return sends, recvs

    return plan, waits


def _layer_shards(w, l):
    return [w["ffn_gate"][l].astype(MMT), w["ffn_up"][l].astype(MMT), w["ffn_down"][l].astype(MMT),
            w["w_in"][l].reshape(2, D_MODEL // 2, -1).astype(MMT),
            w["branch_proj"][l].reshape(2, 3 * BW // 2, -1).astype(MMT),
            w["w_out"][l].reshape(2, -1, D_MODEL).astype(MMT),
            w["s5_glu_w"][l].reshape(2, -1, BW).astype(MMT)]


def _layer_weights(g):
    rows = lambda a: a.transpose(1, 0, 2, 3).reshape(NSH, -1, a.shape[-1])
    p = rows(g[4]).reshape(NSH, 3, BW, -1).transpose(1, 2, 0, 3).reshape(3, BW, D_MODEL)
    return dict(wg=g[0], wu=g[1], wd=g[2], win=rows(g[3]), pfull=p,
                woutfull=rows(g[5]).reshape(D_MODEL, D_MODEL), gluw=rows(g[6]).reshape(BW, BW))


GROUPS = {"ffn1": ("ffn_gate", "ffn_up", "ffn_down"), "merge": ("branch_proj", "w_out"), "mid": ("s5_glu_w",),
          "pre": ("w_in",), "ffn0": ("ffn_gate", "ffn_up", "ffn_down")}


def _grad_views(big, l, group):
    views = []
    for name in GROUPS[group]:
        if name == "branch_proj":
            dq = D_MODEL // NSH
            a = big[(name, l)].reshape(3, BW, NSH, dq).transpose(2, 0, 1, 3).reshape(1, NSH, 3 * BW, dq)
        elif name.startswith("ffn"):
            a = big[(name, l, 1 if group == "ffn1" else 0)]
        else:
            a = big[(name, l)]
            a = a.reshape(1, NSH, -1, a.shape[-1])
        views.append((name, a, 0))
    return views


def halves_plan(n):
    def src(ref, c):
        h = ref.shape[2] // 2
        return ref.at[:, :, pl.ds((1 - c) * h, h)]

    def plan(ins, lands):
        x, y, c, _chips = _place()
        return [(src(ins[w], c), lands[w], w, w, (x, y, 1 - c)) for w in range(n)]

    def waits(ins, lands):
        x, y, c, _chips = _place()
        return [(src(ins[w], c), w) for w in range(n)], [(lands[w], w) for w in range(n)]

    return plan, waits


def _reduce_to_halves(tag, views, c, wire):
    from_sibling = exchange_halves(f"reduce_cores_{tag}", [a for _, a, _ in views], [(p0, 1) for _, _, p0 in views])
    merge = lambda a: a.reshape((-1,) + a.shape[2:])
    return [add_own_half(f"sum_cores_{tag}_{i}", merge(a), merge(r), c, wire[i], NSH * p0).reshape(r.shape)
            for i, ((_, a, p0), r) in enumerate(zip(views, from_sibling))]


def _step(x, target, w, m, v):
    mx, my, mc = lax.axis_index("x"), lax.axis_index("y"), lax.axis_index("c")
    me = (2 * mx + my).astype(jnp.int32)
    mc = mc.astype(jnp.int32)

    W = dict(L=[None] * DEPTH)
    state = {"pending": []}
    n_big = len(BIG)
    g_plan, g_waits = gather_plan(n_big)

    def layer_weights(l, h):
        if l == 0:
            got = gather_shards("gather_weights_0", _layer_shards(w, 0) + [w[n] for n in SHARDED_SMALL])
            nxt = _layer_shards(w, 1)
            got, nxt = lax.optimization_barrier((got, nxt))
            shapes = [jax.ShapeDtypeStruct((2, NSH) + a.shape[1:], a.dtype) for a in nxt]
            state["gather"], token = split_start("gather_weights_1_start", nxt, shapes, g_plan, 8 * n_big, 8 * n_big)
            W["nw"] = got[n_big].transpose(0, 2, 1, 3).reshape(DEPTH, 3, 1, D_MODEL) + token[0, 0]
            W["convw"] = got[n_big + 1].transpose(0, 2, 1, 3).reshape(DEPTH, CONV_W, BW)
            return _layer_weights(got[:n_big]), h
        return _layer_weights(split_wait("gather_weights_1_wait", state["gather"], n_big, g_waits, h)[1]), h

    def to_chips(after):
        if "cores" not in state:
            return
        tag, names, l, group, handles, waits = state.pop("cores")
        sent, landed = split_wait(f"reduce_cores_{tag}_wait", handles, len(names), waits, after)
        merge = lambda a: a.reshape((-1,) + a.shape[2:])
        halves = [add_own_half(f"sum_cores_{tag}_{i}", merge(a), merge(r), mc, jnp.bfloat16, 0).reshape(r.shape)
                  for i, (a, r) in enumerate(zip(sent, landed))]
        shapes = [jax.ShapeDtypeStruct((3, a.shape[0]) + a.shape[2:], a.dtype) for a in halves]
        plan, waits = scatter_plan(len(halves))
        handles, token = split_start(f"reduce_chips_{tag}_start", halves, shapes, plan, 3 * len(halves), 3 * len(halves))
        W["nw"] = W["nw"] + token[0, 0]
        state["pending"].append((tag, names, l, group, handles, waits))

    def layer_grads(l, group, big):
        views = _grad_views(big, l, group)
        to_chips(views[0][1])
        if (l, group) == (0, "ffn0"):
            return
        tag = f"{l}_{group}"
        if (l, group) == (0, "pre"):
            halves = _reduce_to_halves(tag, views, mc, [jnp.bfloat16] * len(views))
            shapes = [jax.ShapeDtypeStruct((3, a.shape[0]) + a.shape[2:], a.dtype) for a in halves]
            plan, waits = scatter_plan(len(halves))
            handles, token = split_start(f"reduce_chips_{tag}_start", halves, shapes, plan, 3 * len(halves), 3 * len(halves))
            W["nw"] = W["nw"] + token[0, 0]
            state["pending"].append((tag, [name for name, _, _ in views], l, group, handles, waits))
            return
        arrays = [a for _, a, _ in views]
        shapes = [jax.ShapeDtypeStruct((1, NSH, a.shape[2] // 2, a.shape[3]), a.dtype) for a in arrays]
        plan, waits = halves_plan(len(arrays))
        handles, token = split_start(f"reduce_cores_{tag}_start", arrays, shapes, plan, len(arrays), len(arrays))
        W["nw"] = W["nw"] + token[0, 0]
        state["cores"] = (tag, [name for name, _, _ in views], l, group, handles, waits)

    loss, dx, big, small = local_step(x[0], target[0], W, {k: w[k] for k in SMALL_RAW}, layer_weights, layer_grads)

    pieces = {n: {} for n in BIG}
    block_of = lambda name, l, group: (2 * l + (group == "ffn1")) if name.startswith("ffn") else l
    views = _grad_views(big, 0, "ffn0")
    small_packed = pack_small("pack_small_grads", [small[n] for n in SMALL], NSH * 32)
    halves = _reduce_to_halves("0_ffn0", views + [("small", small_packed.reshape(1, NSH, -1, LANE), 0)], mc,
                               [jnp.bfloat16] * len(views) + [F32])
    shapes = [jax.ShapeDtypeStruct((3, a.shape[0]) + a.shape[2:], a.dtype) for a in halves]
    plan, waits = scatter_plan(len(halves))
    last_handles, token = split_start("reduce_chips_0_ffn0_start", halves, shapes, plan, 3 * len(halves), 3 * len(halves))
    mc = mc + token[0, 0].astype(jnp.int32)
    for tag, names, l, group, handles, waits_k in state["pending"]:
        sent, landed = split_wait(f"reduce_chips_{tag}_wait", handles, len(names), waits_k, dx)
        for i, (name, h, r) in enumerate(zip(names, sent, landed)):
            pieces[name][block_of(name, l, group)] = add_chips(f"sum_chips_{tag}_{i}", h, r, me)

    g, delta, new_m, new_v = {}, {}, {}, {}

    def update(tag, names, extra):
        own = [jnp.concatenate([pieces[n][b] for b in sorted(pieces[n])], axis=0) for n in names] + extra
        other = share_halves(f"reduce_share_{tag}", own)
        for i, n in enumerate(names):
            view = lambda a: a.reshape(own[i].shape[0], -1, own[i].shape[2])
            res = adamw_halves(f"adamw_{n}", view(w[n]), view(m[n]), view(v[n]), own[i], other[i], mc)
            g[n], delta[n], new_m[n], new_v[n] = [a.reshape(w[n].shape) for a in res]
        return own, other

    early = [n for n in BIG if not n.startswith("ffn")]
    update("early", early, [])
    sent, landed = split_wait("reduce_chips_0_ffn0_wait", last_handles, len(halves), waits, new_v[early[0]])
    last = [add_chips(f"sum_chips_0_ffn0_{i}", h, r, me) for i, (h, r) in enumerate(zip(sent, landed))]
    for (name, _, _), piece in zip(views, last):
        pieces[name][block_of(name, 0, "ffn0")] = piece
    own, other = update("last", [n for n in BIG if n.startswith("ffn")], [last[-1]])

    piece = jnp.stack([jnp.where(mc == 0, own[-1][0], other[-1][0]), jnp.where(mc == 0, other[-1][0], own[-1][0])])
    (all_small,) = gather_shards("gather_small", [piece])
    full_small = unpack_small("unpack_small_grads", all_small.transpose(1, 0, 2, 3).reshape(-1, LANE),
                              [small[n].shape for n in SMALL])
    g.update(zip(SMALL, full_small))
    g["norm_w"] = lax.dynamic_slice_in_dim(g["norm_w"], me * (D_MODEL // NSH), D_MODEL // NSH, axis=2)
    g["rg_conv_w"] = lax.dynamic_slice_in_dim(g["rg_conv_w"], me * (BW // NSH), BW // NSH, axis=2)

    packed = [pack_small(f"pack_small_{tag}", [src[n] for n in SMALL], 8)
              for tag, src in (("w", w), ("g", g), ("m", m), ("v", v))]
    for tag, dst, flat in zip(("delta", "m", "v"), (delta, new_m, new_v), adamw(*packed)):
        dst.update(zip(SMALL, unpack_small(f"unpack_small_{tag}", flat, [w[n].shape for n in SMALL])))

    total = lax.psum(loss[0, 0], ("x", "y", "c"))
    return (total, dx[None], *[g[n] for n in WEIGHTS], *[delta[n] for n in WEIGHTS],
            *[new_m[n] for n in WEIGHTS], *[new_v[n] for n in WEIGHTS])


def kernel(x, norm_w, final_norm_w, ffn_gate, ffn_up, ffn_down, w_in, branch_proj, w_out, s5_lambda_re, s5_lambda_im, s5_log_dt, s5_b_re, s5_b_im, s5_c_re, s5_c_im, s5_d, s5_glu_w, s5_glu_b, hg_lb_logits, hg_norm_w, rg_conv_w, rg_conv_b, rg_wa, rg_ba, rg_wx, rg_bx, rg_lambda, loss_target, m_norm_w, m_final_norm_w, m_ffn_gate, m_ffn_up, m_ffn_down, m_w_in, m_branch_proj, m_w_out, m_s5_lambda_re, m_s5_lambda_im, m_s5_log_dt, m_s5_b_re, m_s5_b_im, m_s5_c_re, m_s5_c_im, m_s5_d, m_s5_glu_w, m_s5_glu_b, m_hg_lb_logits, m_hg_norm_w, m_rg_conv_w, m_rg_conv_b, m_rg_wa, m_rg_ba, m_rg_wx, m_rg_bx, m_rg_lambda, v_norm_w, v_final_norm_w, v_ffn_gate, v_ffn_up, v_ffn_down, v_w_in, v_branch_proj, v_w_out, v_s5_lambda_re, v_s5_lambda_im, v_s5_log_dt, v_s5_b_re, v_s5_b_im, v_s5_c_re, v_s5_c_im, v_s5_d, v_s5_glu_w, v_s5_glu_b, v_hg_lb_logits, v_hg_norm_w, v_rg_conv_w, v_rg_conv_b, v_rg_wa, v_rg_ba, v_rg_wx, v_rg_bx, v_rg_lambda):
    ws = (norm_w, final_norm_w, ffn_gate, ffn_up, ffn_down, w_in, branch_proj, w_out, s5_lambda_re, s5_lambda_im, s5_log_dt, s5_b_re, s5_b_im, s5_c_re, s5_c_im, s5_d, s5_glu_w, s5_glu_b, hg_lb_logits, hg_norm_w, rg_conv_w, rg_conv_b, rg_wa, rg_ba, rg_wx, rg_bx, rg_lambda)
    ms = (m_norm_w, m_final_norm_w, m_ffn_gate, m_ffn_up, m_ffn_down, m_w_in, m_branch_proj, m_w_out, m_s5_lambda_re, m_s5_lambda_im, m_s5_log_dt, m_s5_b_re, m_s5_b_im, m_s5_c_re, m_s5_c_im, m_s5_d, m_s5_glu_w, m_s5_glu_b, m_hg_lb_logits, m_hg_norm_w, m_rg_conv_w, m_rg_conv_b, m_rg_wa, m_rg_ba, m_rg_wx, m_rg_bx, m_rg_lambda)
    vs = (v_norm_w, v_final_norm_w, v_ffn_gate, v_ffn_up, v_ffn_down, v_w_in, v_branch_proj, v_w_out, v_s5_lambda_re, v_s5_lambda_im, v_s5_log_dt, v_s5_b_re, v_s5_b_im, v_s5_c_re, v_s5_c_im, v_s5_d, v_s5_glu_w, v_s5_glu_b, v_hg_lb_logits, v_hg_norm_w, v_rg_conv_w, v_rg_conv_b, v_rg_wa, v_rg_ba, v_rg_wx, v_rg_bx, v_rg_lambda)
    return _step(x, loss_target, dict(zip(WEIGHTS, ws)), dict(zip(WEIGHTS, ms)), dict(zip(WEIGHTS, vs)))
```

```python
import functools
import math
from typing import NamedTuple

import jax
import jax.numpy as jnp
from jax import lax
from jax.experimental import pallas as pl
from jax.experimental.pallas import tpu as pltpu

F32 = jnp.float32
MMT = jnp.bfloat16

D_MODEL = 1024
BW = 512
S5_GROUP, S5_GROUPS, S5_STATE = 16, 32, 64
S5_N = S5_GROUPS * S5_STATE
HG_HEADS, HG_D = 4, 128
HG_CHUNK = 128
RG_BLOCKS, RG_BLOCK = 8, 64
RG_C = 8.0
CONV_W = 4
D_FF = 2816
EPS = 1e-6
IN_TOTAL = 6656
NSH = 4
NSEG = 8
LANE = 128
VMEM_LIMIT = 56 * 1024 * 1024
TM_FWD = 512
TM_WGRAD = 512

ADAM_LR, ADAM_B1, ADAM_B2, ADAM_EPS, ADAM_WD, ADAM_STEP = 0.001, 0.9, 0.999, 1e-08, 0.01, 10

MESH = pl.DeviceIdType.MESH


class WP(NamedTuple):
    w: jax.Array
    p: jax.Array


def _dg(a, b, ca, cb):
    return lax.dot_general(a, b, (((ca,), (cb,)), ((), ())), preferred_element_type=F32)


@jax.custom_vjp
def _mmw(a, w, p):
    return _dg(a.astype(MMT), w, 1, 0)


def _mmw_fwd(a, w, p):
    return _mmw(a, w, p), (a, w)


def _mmw_bwd(res, g):
    a, w = res
    gb = g.astype(MMT)
    return _dg(gb, w, 1, 1), jnp.zeros_like(w), _dg(a.astype(MMT), gb, 0, 0)


_mmw.defvjp(_mmw_fwd, _mmw_bwd)


def mm(a, w):
    if isinstance(w, WP):
        return _mmw(a, w.w, w.p)
    return _dg(a.astype(MMT), w, 1, 0)


@jax.custom_vjp
def mma_nn(a, b):
    return _dg(a.astype(MMT), b.astype(MMT), 1, 0)


def _nn_f(a, b):
    return mma_nn(a, b), (a, b)


def _nn_b(res, g):
    a, b = res
    gb = g.astype(MMT)
    return _dg(gb, b.astype(MMT), 1, 1), _dg(a.astype(MMT), gb, 0, 0)


mma_nn.defvjp(_nn_f, _nn_b)


@jax.custom_vjp
def mma_nt(a, b):
    return _dg(a.astype(MMT), b.astype(MMT), 1, 1)


def _nt_f(a, b):
    return mma_nt(a, b), (a, b)


def _nt_b(res, g):
    a, b = res
    gb = g.astype(MMT)
    return _dg(gb, b.astype(MMT), 1, 0), _dg(gb, a.astype(MMT), 0, 0)


mma_nt.defvjp(_nt_f, _nt_b)


@jax.custom_vjp
def mma_tn(a, b):
    return _dg(a.astype(MMT), b.astype(MMT), 0, 0)


def _tn_f(a, b):
    return mma_tn(a, b), (a, b)


def _tn_b(res, g):
    a, b = res
    gb = g.astype(MMT)
    return _dg(b.astype(MMT), gb, 1, 1), _dg(a.astype(MMT), gb, 1, 0)


mma_tn.defvjp(_tn_f, _tn_b)


def _split3(x):
    hi = x.astype(MMT)
    r = x - hi.astype(F32)
    mid = r.astype(MMT)
    return hi, mid, (r - mid.astype(F32)).astype(MMT)


@jax.custom_vjp
def mm_exact(m, x):
    mb = m.astype(MMT)
    hi, mid, lo = _split3(x)
    return (_dg(mb, hi, 1, 0) + _dg(mb, mid, 1, 0)) + _dg(mb, lo, 1, 0)


def _mm_exact_fwd(m, x):
    return mm_exact(m, x), m


def _mm_exact_bwd(m, g):
    mb = m.astype(MMT)
    hi, mid, lo = _split3(g)
    return jnp.zeros_like(m), (_dg(mb, hi, 0, 0) + _dg(mb, mid, 0, 0)) + _dg(mb, lo, 0, 0)


mm_exact.defvjp(_mm_exact_fwd, _mm_exact_bwd)


@jax.custom_vjp
def mm_exact_r(x, m):
    mb = m.astype(MMT)
    hi, mid, lo = _split3(x)
    return (_dg(hi, mb, 1, 0) + _dg(mid, mb, 1, 0)) + _dg(lo, mb, 1, 0)


def _mm_exact_r_fwd(x, m):
    return mm_exact_r(x, m), m


def _mm_exact_r_bwd(m, g):
    mb = m.astype(MMT)
    hi, mid, lo = _split3(g)
    return (_dg(hi, mb, 1, 1) + _dg(mid, mb, 1, 1)) + _dg(lo, mb, 1, 1), jnp.zeros_like(m)


mm_exact_r.defvjp(_mm_exact_r_fwd, _mm_exact_r_bwd)


def _rms(x, w):
    return x * lax.rsqrt(jnp.mean(x * x, axis=-1, keepdims=True) + EPS) * w


def _expm1(x):
    series = x * (1.0 + x * (1.0 / 2) * (1.0 + x * (1.0 / 3) * (1.0 + x * (1.0 / 4) * (1.0 + x * (1.0 / 5) * (1.0 + x * (1.0 / 6))))))
    return jnp.where(jnp.abs(x) < 0.1, series, jnp.exp(x) - 1.0)


def _bspec(block, fn, order):
    if order == "is":
        return pl.BlockSpec(block, lambda i, s: fn(s, i))
    return pl.BlockSpec(block, lambda s, i: fn(s, i))


def tile_fwd(fn, name, n_i, n_s, ins, outs, s_outer=False):
    n_in = len(ins)
    order = "si" if s_outer else "is"
    assert not (s_outer and any(o[4] for o in outs))

    def body(*refs):
        s = pl.program_id(0 if s_outer else 1)
        res = fn(*[r[...] for r in refs[:n_in]], s)
        for o_ref, val, spec in zip(refs[n_in:], res, outs):
            if spec[4] and n_s > 1:
                @pl.when(s == 0)
                def _(o_ref=o_ref, val=val):
                    o_ref[...] = val.astype(o_ref.dtype)

                @pl.when(s != 0)
                def _(o_ref=o_ref, val=val):
                    o_ref[...] += val.astype(o_ref.dtype)
            else:
                o_ref[...] = val.astype(o_ref.dtype)

    return pl.pallas_call(
        body, grid=(n_s, n_i) if s_outer else (n_i, n_s), name=name,
        in_specs=[_bspec(b, f, order) for _, b, f in ins],
        out_specs=[_bspec(b, f, order) for _, _, b, f, _ in outs],
        out_shape=[jax.ShapeDtypeStruct(sh, dt) for sh, dt, _, _, _ in outs],
        compiler_params=pltpu.CompilerParams(vmem_limit_bytes=VMEM_LIMIT,
                                             dimension_semantics=("arbitrary", "arbitrary")),
    )(*[a for a, _, _ in ins])


def tile_bwd(fn, name, n_i, n_s, ins, cts, gouts):
    groups = [c if isinstance(c, list) else [c] for c in cts]
    cts = [blk for grp in groups for blk in grp]
    n_in, n_ct = len(ins), len(cts)
    kinds = [k for _, _, _, k in ins]
    d_pos = [j for j, k in enumerate(kinds) if k != "c"]
    shared = [(gi, spec[4]) for gi, spec in enumerate(gouts) if len(spec) == 5 and spec[4] is not None]
    n_sh = len(shared)

    def body(*refs):
        s, i = pl.program_id(0), pl.program_id(1)
        vals = [r[...] for r in refs[:n_in]]
        ct_refs, ctv = list(refs[n_in:n_in + n_ct]), []
        for grp in groups:
            parts = [ct_refs.pop(0)[...] for _ in grp]
            ctv.append(parts[0] if len(parts) == 1 else jnp.concatenate(parts, axis=1))
        ctv = tuple(ctv)
        g_refs = refs[n_in + n_ct + n_sh:]

        def g(*dv):
            args = list(vals)
            for j, v in zip(d_pos, dv):
                args[j] = WP(vals[j], v) if kinds[j] == "w" else v
            return tuple(fn(*args))

        dv0 = [jnp.zeros(vals[j].shape, F32) if kinds[j] == "w" else vals[j] for j in d_pos]
        _, vjp = jax.vjp(g, *dv0)
        grads = vjp(ctv)
        for g_ref, gv, spec in zip(g_refs, grads, gouts):
            mode = spec[3]
            if mode == "write":
                g_ref[...] = gv.astype(g_ref.dtype)
            else:
                first = (i == 0) if mode == "acc_i" else jnp.logical_and(i == 0, s == 0)

                @pl.when(first)
                def _(g_ref=g_ref, gv=gv):
                    g_ref[...] = gv.astype(g_ref.dtype)

                @pl.when(jnp.logical_not(first))
                def _(g_ref=g_ref, gv=gv):
                    g_ref[...] += gv.astype(g_ref.dtype)

    return pl.pallas_call(
        body, grid=(n_s, n_i), name=name,
        in_specs=([_bspec(b, f, "si") for _, b, f, _ in ins] + [_bspec(b, f, "si") for _, b, f in cts]
                  + [pl.BlockSpec(memory_space=pl.ANY)] * n_sh),
        out_specs=[_bspec(spec[1], spec[2], "si") for spec in gouts],
        out_shape=[jax.ShapeDtypeStruct(spec[0], F32) for spec in gouts],
        input_output_aliases={n_in + n_ct + k: gi for k, (gi, _) in enumerate(shared)},
        compiler_params=pltpu.CompilerParams(vmem_limit_bytes=VMEM_LIMIT,
                                             dimension_semantics=("arbitrary", "arbitrary")),
    )(*[a for a, _, _, _ in ins], *[a for a, _, _ in cts], *[buf for _, buf in shared])


def _row_tile(rows, width, itemsize=4, budget=2 * 1024 * 1024, mult=8):
    best = mult
    for t in range(mult, rows + 1, mult):
        if rows % t == 0 and t * width * itemsize <= budget:
            best = t
    return best


def add_n(name, terms, shape):
    rows, cols = shape
    tr = _row_tile(rows, cols)

    def body(*refs):
        acc = refs[0][...]
        for r in refs[1:-1]:
            acc = acc + r[...]
        refs[-1][...] = acc

    specs = []
    for _, lead in terms:
        specs.append(pl.BlockSpec((None,) * len(lead) + (tr, cols), functools.partial(lambda i, lead: (*lead, i, 0), lead=lead)))
    return pl.pallas_call(
        body, grid=(rows // tr,), name=name, in_specs=specs,
        out_specs=pl.BlockSpec((tr, cols), lambda i: (i, 0)),
        out_shape=jax.ShapeDtypeStruct((rows, cols), F32),
    )(*[a for a, _ in terms])


def pre_core(x, nw, win):
    return (mm(_rms(x, nw), win),)


def _split_lanes(y):
    return jnp.stack([y[:, k * LANE:(k + 1) * LANE] for k in range(y.shape[1] // LANE)], axis=0)


def _join_lanes(y3):
    return jnp.concatenate([y3[k] for k in range(y3.shape[0])], axis=1)


def s5_pre_core(u, b_re0, b_re1, b_im0, b_im1):
    u0, u1 = u[:, :BW // 2], u[:, BW // 2:]
    re = jnp.concatenate([mm(u0, b_re0), mm(u1, b_re1)], axis=1)
    im = jnp.concatenate([mm(u0, b_im0), mm(u1, b_im1)], axis=1)
    return _split_lanes(re), _split_lanes(im)


def mid_core(xr, xi, u, o, g, hs, gc, hmat, c0, c1, d, gluw, glub, hgw):
    half = xr.shape[0] // 2
    xs0 = jnp.concatenate([_join_lanes(xr[:half]), _join_lanes(xi[:half])], axis=1)
    xs1 = jnp.concatenate([_join_lanes(xr[half:]), _join_lanes(xi[half:])], axis=1)
    y = jnp.concatenate([mm(xs0, c0), mm(xs1, c1)], axis=1) + d * u
    z = jax.nn.gelu(y)
    ya = z * jax.nn.sigmoid(mm(z, gluw) + glub)
    ms = mm_exact_r(o * o, hmat)
    yb = o * lax.rsqrt(ms + EPS) * hgw * jax.nn.silu(g)
    yc = hs * jax.nn.gelu(gc)
    return ya, yb, yc


def _sub(w, n):
    return WP(w.w[n], w.p[n]) if isinstance(w, WP) else w[n]


def merge_core(ya, yb, yc, g0, g1, g2, g3, g4, g5, p, wout):
    gate = lambda a, b: jax.nn.sigmoid(jnp.concatenate([a, b], axis=1))
    m = gate(g0, g1) * mm(ya, _sub(p, 0)) + gate(g2, g3) * mm(yb, _sub(p, 1)) + gate(g4, g5) * mm(yc, _sub(p, 2))
    return (mm(m, wout),)


def gates_core(xc, wa, ba, wx, bx, lam):
    r = jax.nn.sigmoid(mm(xc, wa) + ba)
    i = jax.nn.sigmoid(mm(xc, wx) + bx)
    log_a = -RG_C * jax.nn.softplus(-lam) * r
    a = jnp.exp(log_a)
    b = jnp.sqrt(-_expm1(2.0 * log_a)) * (i * xc)
    return a, b


def _seg_rows(ref, k, j, n):
    rows = pl.ds(pl.multiple_of(j * NSEG, NSEG), NSEG)
    if k is None:
        return ref[rows, :]
    return ref[k, rows, :]


def _seg_store(ref, k, j, n, val):
    rows = pl.ds(pl.multiple_of(j * NSEG, NSEG), NSEG)
    if k is None:
        ref[rows, :] = val
    else:
        ref[k, rows, :] = val


def _seg_carries(er, ei, pr, pi, reverse):
    rows = lax.broadcasted_iota(jnp.int32, er.shape, 0)
    cr = jnp.zeros_like(er)
    ci = None if ei is None else jnp.zeros_like(er)
    order = range(NSEG - 2, -1, -1) if reverse else range(1, NSEG)
    shift = NSEG - 1 if reverse else 1
    for s in order:
        if ei is None:
            tr = er + pr * cr
            cr = jnp.where(rows == s, pltpu.roll(tr, shift, 0), cr)
        else:
            tr = er + pr * cr - pi * ci
            ti = ei + pr * ci + pi * cr
            cr = jnp.where(rows == s, pltpu.roll(tr, shift, 0), cr)
            ci = jnp.where(rows == s, pltpu.roll(ti, shift, 0), ci)
    return cr, ci


def _cpow(ar, ai, n):
    out = None
    while n:
        if n & 1:
            out = (ar, ai) if out is None else (out[0] * ar - out[1] * ai, out[0] * ai + out[1] * ar)
        ar, ai = ar * ar - ai * ai, 2.0 * ar * ai
        n >>= 1
    return out


S5_K = 2


def s5_scan_fwd(bur, bui, ar, ai, L):
    n = L // NSEG
    nb = S5_N // LANE
    K = S5_K

    def body(br_ref, bi_ref, ar_ref, ai_ref, xr_ref, xi_ref):
        zero = jnp.zeros((NSEG, LANE), F32)
        A = [(jnp.broadcast_to(ar_ref[k], (NSEG, LANE)), jnp.broadcast_to(ai_ref[k], (NSEG, LANE))) for k in range(K)]

        def p1(j, st):
            new = []
            for k in range(K):
                sr, si = st[k]
                a_r, a_i = A[k]
                nr = a_r * sr - a_i * si + _seg_rows(br_ref, k, j, n)
                ni = a_r * si + a_i * sr + _seg_rows(bi_ref, k, j, n)
                _seg_store(xr_ref, k, j, n, nr)
                _seg_store(xi_ref, k, j, n, ni)
                new.append((nr, ni))
            return tuple(new)

        st = lax.fori_loop(0, n, p1, tuple((zero, zero) for _ in range(K)))
        C = [_seg_carries(st[k][0], st[k][1], *_cpow(*A[k], n), False) for k in range(K)]

        def p2(j, st):
            new = []
            for k in range(K):
                pr, pi = st[k]
                a_r, a_i = A[k]
                pr, pi = a_r * pr - a_i * pi, a_r * pi + a_i * pr
                cr, ci = C[k]
                _seg_store(xr_ref, k, j, n, _seg_rows(xr_ref, k, j, n) + pr * cr - pi * ci)
                _seg_store(xi_ref, k, j, n, _seg_rows(xi_ref, k, j, n) + pr * ci + pi * cr)
                new.append((pr, pi))
            return tuple(new)

        lax.fori_loop(0, n, p2, tuple((zero + 1.0, zero) for _ in range(K)))

    blk = pl.BlockSpec((K, L, LANE), lambda g: (g, 0, 0))
    ablk = pl.BlockSpec((K, 1, LANE), lambda g: (g, 0, 0))
    return pl.pallas_call(
        body, grid=(nb // K,), name="s5_scan_fwd",
        in_specs=[blk, blk, ablk, ablk], out_specs=[blk, blk],
        out_shape=[jax.ShapeDtypeStruct((nb, L, LANE), F32)] * 2,
        compiler_params=pltpu.CompilerParams(vmem_limit_bytes=VMEM_LIMIT),
    )(bur, bui, ar, ai)


def s5_scan_bwd(dxr, dxi, xr, xi, ar, ai, L):
    n = L // NSEG
    nb = S5_N // LANE
    K = S5_K

    def body(dr_ref, di_ref, xr_ref, xi_ref, ar_ref, ai_ref, gr_ref, gi_ref, dar_ref, dai_ref):
        zero = jnp.zeros((NSEG, LANE), F32)
        rows = lax.broadcasted_iota(jnp.int32, (NSEG, LANE), 0)
        A = [(jnp.broadcast_to(ar_ref[k], (NSEG, LANE)), -jnp.broadcast_to(ai_ref[k], (NSEG, LANE))) for k in range(K)]

        def p1(jj, st):
            j = n - 1 - jj
            new = []
            for k in range(K):
                sr, si = st[k]
                a_r, a_i = A[k]
                nr = a_r * sr - a_i * si + _seg_rows(dr_ref, k, j, n)
                ni = a_r * si + a_i * sr + _seg_rows(di_ref, k, j, n)
                _seg_store(gr_ref, k, j, n, nr)
                _seg_store(gi_ref, k, j, n, ni)
                new.append((nr, ni))
            return tuple(new)

        st = lax.fori_loop(0, n, p1, tuple((zero, zero) for _ in range(K)))
        C = [_seg_carries(st[k][0], st[k][1], *_cpow(*A[k], n), True) for k in range(K)]
        xb = [(jnp.where(rows == 0, 0.0, pltpu.roll(_seg_rows(xr_ref, k, n - 1, n), 1, 0)),
               jnp.where(rows == 0, 0.0, pltpu.roll(_seg_rows(xi_ref, k, n - 1, n), 1, 0))) for k in range(K)]

        def p2(jj, st):
            j = n - 1 - jj
            jp = jnp.maximum(j - 1, 0)
            new = []
            for k in range(K):
                pr, pi, acr, aci = st[k]
                a_r, a_i = A[k]
                pr, pi = a_r * pr - a_i * pi, a_r * pi + a_i * pr
                cr, ci = C[k]
                g_r = _seg_rows(gr_ref, k, j, n) + pr * cr - pi * ci
                g_i = _seg_rows(gi_ref, k, j, n) + pr * ci + pi * cr
                _seg_store(gr_ref, k, j, n, g_r)
                _seg_store(gi_ref, k, j, n, g_i)
                xpr = jnp.where(j == 0, xb[k][0], _seg_rows(xr_ref, k, jp, n))
                xpi = jnp.where(j == 0, xb[k][1], _seg_rows(xi_ref, k, jp, n))
                new.append((pr, pi, acr + g_r * xpr + g_i * xpi, aci + g_i * xpr - g_r * xpi))
            return tuple(new)

        st = lax.fori_loop(0, n, p2, tuple((zero + 1.0, zero, zero, zero) for _ in range(K)))
        for k in range(K):
            dar_ref[k] = jnp.sum(st[k][2], axis=0, keepdims=True)
            dai_ref[k] = jnp.sum(st[k][3], axis=0, keepdims=True)

    blk = pl.BlockSpec((K, L, LANE), lambda g: (g, 0, 0))
    ablk = pl.BlockSpec((K, 1, LANE), lambda g: (g, 0, 0))
    return pl.pallas_call(
        body, grid=(nb // K,), name="s5_scan_bwd",
        in_specs=[blk, blk, blk, blk, ablk, ablk], out_specs=[blk, blk, ablk, ablk],
        out_shape=[jax.ShapeDtypeStruct((nb, L, LANE), F32)] * 2 + [jax.ShapeDtypeStruct((nb, 1, LANE), F32)] * 2,
        compiler_params=pltpu.CompilerParams(vmem_limit_bytes=VMEM_LIMIT),
    )(dxr, dxi, xr, xi, ar, ai)


def rg_scan_fwd(a, b, L):
    n = L // NSEG

    def body(a_ref, b_ref, h_ref):
        zero = jnp.zeros((NSEG, LANE), F32)

        def p1(j, st):
            h, p = st
            aj = _seg_rows(a_ref, None, j, n)
            h = aj * h + _seg_rows(b_ref, None, j, n)
            _seg_store(h_ref, None, j, n, h)
            return h, aj * p

        e, pe = lax.fori_loop(0, n, p1, (zero, zero + 1.0))
        c, _ = _seg_carries(e, None, pe, None, False)

        def p2(j, p):
            p = _seg_rows(a_ref, None, j, n) * p
            _seg_store(h_ref, None, j, n, _seg_rows(h_ref, None, j, n) + p * c)
            return p

        lax.fori_loop(0, n, p2, zero + 1.0)

    blk = pl.BlockSpec((L, LANE), lambda g: (0, g))
    return pl.pallas_call(
        body, grid=(BW // LANE,), name="rg_scan_fwd", in_specs=[blk, blk], out_specs=blk,
        out_shape=jax.ShapeDtypeStruct((L, BW), F32),
        compiler_params=pltpu.CompilerParams(vmem_limit_bytes=VMEM_LIMIT),
    )(a, b)


def rg_scan_bwd(a, h, dh, L):
    n = L // NSEG

    def body(a_ref, h_ref, dh_ref, da_ref, db_ref):
        zero = jnp.zeros((NSEG, LANE), F32)
        rows = lax.broadcasted_iota(jnp.int32, (NSEG, LANE), 0)
        a_edge = jnp.where(rows == NSEG - 1, 0.0, pltpu.roll(_seg_rows(a_ref, None, 0, n), NSEG - 1, 0))
        h_edge = jnp.where(rows == 0, 0.0, pltpu.roll(_seg_rows(h_ref, None, n - 1, n), 1, 0))

        def mult(j):
            return jnp.where(j == n - 1, a_edge, _seg_rows(a_ref, None, jnp.minimum(j + 1, n - 1), n))

        def p1(jj, st):
            j = n - 1 - jj
            g, p = st
            m = mult(j)
            g = m * g + _seg_rows(dh_ref, None, j, n)
            _seg_store(db_ref, None, j, n, g)
            return g, m * p

        e, pe = lax.fori_loop(0, n, p1, (zero, zero + 1.0))
        c, _ = _seg_carries(e, None, pe, None, True)

        def p2(jj, p):
            j = n - 1 - jj
            p = mult(j) * p
            g = _seg_rows(db_ref, None, j, n) + p * c
            _seg_store(db_ref, None, j, n, g)
            hp = jnp.where(j == 0, h_edge, _seg_rows(h_ref, None, jnp.maximum(j - 1, 0), n))
            _seg_store(da_ref, None, j, n, g * hp)
            return p

        lax.fori_loop(0, n, p2, zero + 1.0)

    blk = pl.BlockSpec((L, LANE), lambda g: (0, g))
    return pl.pallas_call(
        body, grid=(BW // LANE,), name="rg_scan_bwd", in_specs=[blk, blk, blk], out_specs=[blk, blk],
        out_shape=[jax.ShapeDtypeStruct((L, BW), F32)] * 2,
        compiler_params=pltpu.CompilerParams(vmem_limit_bytes=VMEM_LIMIT),
    )(a, h, dh)


def _hg_consts(C):
    t = lax.broadcasted_iota(jnp.int32, (C, C), 0)
    s = lax.broadcasted_iota(jnp.int32, (C, C), 1)
    tril = (s <= t).astype(F32)
    diag = (s == t).astype(F32)
    levels = []
    k = 1
    while (1 << k) <= C:
        m = 1 << (k - 1)
        same = (t >> k) == (s >> k)
        t_right = ((t >> (k - 1)) & 1) == 1
        s_left = ((s >> (k - 1)) & 1) == 0
        mask = jnp.logical_and(same, jnp.logical_and(t_right, s_left)).astype(F32)
        bnd = ((t >> k) << k) + (m - 1)
        levels.append((mask, (s <= bnd).astype(F32)))
        k += 1
    return tril, diag, levels


def hg_chunk(st, q, z, v, lb):
    C = q.shape[0]
    tril, diag, levels = _hg_consts(C)
    sig = jax.nn.sigmoid(z)
    lf = jnp.log(lb + (1.0 - lb) * sig)
    k = (1.0 - lb) * jax.nn.sigmoid(-z)
    qh = jax.nn.silu(q)
    b = mm_exact(tril, lf)
    blast = jnp.sum(lf, axis=0, keepdims=True)
    qe = qh * jnp.exp(b)
    kd = k * jnp.exp(blast - b)
    scaled = []
    for level, (_, sel) in enumerate(levels):
        size = 2 << level
        if size >= NSEG:
            b3 = b.reshape(C // size, size, b.shape[1])
            ref = jnp.broadcast_to(b3[:, size // 2 - 1:size // 2, :], b3.shape).reshape(b.shape)
        else:
            ref = mm_exact(sel, lf)
        scaled.append((qh * jnp.exp(jnp.minimum(b - ref, 0.0)), k * jnp.exp(jnp.minimum(ref - b, 0.0))))
    outs, news = [], []
    for h in range(HG_HEADS):
        sl = slice(h * HG_D, (h + 1) * HG_D)
        st_h = st[h * HG_D:(h + 1) * HG_D, :]
        sc = diag * mma_nt(qh[:, sl], k[:, sl])
        for (mask, _), (qt, kt) in zip(levels, scaled):
            sc = sc + mask * mma_nt(qt[:, sl], kt[:, sl])
        outs.append(mma_nt(qe[:, sl], st_h) + mma_nn(sc, v[:, sl]))
        news.append(st_h * jnp.exp(blast[:, sl]) + mma_tn(v[:, sl], kd[:, sl]))
    return jnp.concatenate(news, axis=0), jnp.concatenate(outs, axis=1)


def hg_fwd(qzv, lb, L):
    C = HG_CHUNK
    nc = L // C

    def body(q_ref, z_ref, v_ref, lb_ref, o_ref, sst_ref, st_ref):
        @pl.when(pl.program_id(0) == 0)
        def _():
            st_ref[...] = jnp.zeros_like(st_ref)

        st = st_ref[...]
        sst_ref[...] = st
        new, o = hg_chunk(st, q_ref[...], z_ref[...], v_ref[...], lb_ref[...])
        st_ref[...] = new
        o_ref[...] = o

    col = lambda cb: pl.BlockSpec((C, BW), functools.partial(lambda c, cb: (c, cb), cb=cb))
    return pl.pallas_call(
        body, grid=(nc,), name="hg_fwd",
        in_specs=[col(0), col(1), col(2), pl.BlockSpec((1, BW), lambda c: (0, 0))],
        out_specs=[pl.BlockSpec((C, BW), lambda c: (c, 0)), pl.BlockSpec((None, BW, HG_D), lambda c: (c, 0, 0))],
        out_shape=[jax.ShapeDtypeStruct((L, BW), F32), jax.ShapeDtypeStruct((nc, BW, HG_D), F32)],
        scratch_shapes=[pltpu.VMEM((BW, HG_D), F32)],
        compiler_params=pltpu.CompilerParams(vmem_limit_bytes=VMEM_LIMIT, dimension_semantics=("arbitrary",)),
    )(qzv, qzv, qzv, lb)


def hg_bwd(qzv, lb, sst, do, L):
    C = HG_CHUNK
    nc = L // C

    def body(q_ref, z_ref, v_ref, lb_ref, sst_ref, do_ref, dq_ref, dz_ref, dv_ref, dlb_ref, dst_ref):
        @pl.when(pl.program_id(0) == 0)
        def _():
            dst_ref[...] = jnp.zeros_like(dst_ref)
            dlb_ref[...] = jnp.zeros_like(dlb_ref)

        _, vjp = jax.vjp(hg_chunk, sst_ref[...], q_ref[...], z_ref[...], v_ref[...], lb_ref[...])
        dst, dq, dz, dv, dlb = vjp((dst_ref[...], do_ref[...]))
        dst_ref[...] = dst
        dq_ref[...] = dq
        dz_ref[...] = dz
        dv_ref[...] = dv
        dlb_ref[...] += dlb

    col = lambda cb: pl.BlockSpec((C, BW), functools.partial(lambda c, cb: (nc - 1 - c, cb), cb=cb))
    rev = pl.BlockSpec((C, BW), lambda c: (nc - 1 - c, 0))
    return pl.pallas_call(
        body, grid=(nc,), name="hg_bwd",
        in_specs=[col(0), col(1), col(2), pl.BlockSpec((1, BW), lambda c: (0, 0)),
                  pl.BlockSpec((None, BW, HG_D), lambda c: (nc - 1 - c, 0, 0)), rev],
        out_specs=[rev, rev, rev, pl.BlockSpec((1, BW), lambda c: (0, 0))],
        out_shape=[jax.ShapeDtypeStruct((L, BW), F32)] * 3 + [jax.ShapeDtypeStruct((1, BW), F32)],
        scratch_shapes=[pltpu.VMEM((BW, HG_D), F32)],
        compiler_params=pltpu.CompilerParams(vmem_limit_bytes=VMEM_LIMIT, dimension_semantics=("arbitrary",)),
    )(qzv, qzv, qzv, lb, sst, do)


def _shift_down(x, d, rows, L):
    if d == 0:
        return x
    wrapped = jnp.where((rows & (NSEG - 1)) == 0, 0.0, pltpu.roll(x, NSEG * d + 1, 0))
    return jnp.where(rows < NSEG * d, wrapped, pltpu.roll(x, NSEG * d, 0))


def _shift_up(x, d, rows, L):
    if d == 0:
        return x
    wrapped = jnp.where((rows & (NSEG - 1)) == NSEG - 1, 0.0, pltpu.roll(x, L - (NSEG * d + 1), 0))
    return jnp.where(rows >= L - NSEG * d, wrapped, pltpu.roll(x, L - NSEG * d, 0))


def conv_fwd(proj, w, b, L):
    def body(x_ref, w_ref, b_ref, o_ref):
        x = x_ref[...]
        rows = lax.broadcasted_iota(jnp.int32, x.shape, 0)
        acc = jnp.broadcast_to(b_ref[...], x.shape)
        for k in range(CONV_W):
            acc = acc + w_ref[pl.ds(k, 1), :] * _shift_down(x, CONV_W - 1 - k, rows, L)
        o_ref[...] = acc

    nl = BW // LANE
    return pl.pallas_call(
        body, grid=(nl,), name="conv_fwd",
        in_specs=[pl.BlockSpec((L, LANE), lambda g: (0, 5 * nl + g)), pl.BlockSpec((CONV_W, LANE), lambda g: (0, g)),
                  pl.BlockSpec((1, LANE), lambda g: (0, g))],
        out_specs=pl.BlockSpec((L, LANE), lambda g: (0, g)),
        out_shape=jax.ShapeDtypeStruct((L, BW), F32),
        compiler_params=pltpu.CompilerParams(vmem_limit_bytes=VMEM_LIMIT),
    )(proj, w, b)


def conv_bwd(proj, w, dxc, L):
    def body(x_ref, w_ref, d_ref, dx_ref, dw_ref, db_ref):
        x, d = x_ref[...], d_ref[...]
        rows = lax.broadcasted_iota(jnp.int32, x.shape, 0)
        acc = jnp.zeros_like(x)
        for k in range(CONV_W):
            acc = acc + w_ref[pl.ds(k, 1), :] * _shift_up(d, CONV_W - 1 - k, rows, L)
            dw_ref[pl.ds(k, 1), :] = jnp.sum(d * _shift_down(x, CONV_W - 1 - k, rows, L), axis=0, keepdims=True)
        dx_ref[...] = acc
        db_ref[...] = jnp.sum(d, axis=0, keepdims=True)

    nl = BW // LANE
    blk = pl.BlockSpec((L, LANE), lambda g: (0, g))
    return pl.pallas_call(
        body, grid=(nl,), name="conv_bwd",
        in_specs=[pl.BlockSpec((L, LANE), lambda g: (0, 5 * nl + g)), pl.BlockSpec((CONV_W, LANE), lambda g: (0, g)), blk],
        out_specs=[blk, pl.BlockSpec((CONV_W, LANE), lambda g: (0, g)), pl.BlockSpec((1, LANE), lambda g: (0, g))],
        out_shape=[jax.ShapeDtypeStruct((L, BW), F32), jax.ShapeDtypeStruct((CONV_W, BW), F32),
                   jax.ShapeDtypeStruct((1, BW), F32)],
        compiler_params=pltpu.CompilerParams(vmem_limit_bytes=VMEM_LIMIT),
    )(proj, w, dxc)


def loss_fwd_bwd(x, fw, target, L, tm):
    def fn(x, fw, t):
        err = jnp.square(_rms(x, fw) - t)
        return jnp.sum(0.5 * jnp.mean(err, axis=-1, keepdims=True), axis=0, keepdims=True)

    def body(x_ref, fw_ref, t_ref, l_ref, dx_ref, dfw_ref):
        i = pl.program_id(0)
        t = t_ref[...]
        val, vjp = jax.vjp(lambda x, fw: fn(x, fw, t), x_ref[...], fw_ref[...])
        dx, dfw = vjp(jnp.ones((1, 1), F32))
        dx_ref[...] = dx

        @pl.when(i == 0)
        def _():
            l_ref[...] = jnp.zeros_like(l_ref)
            dfw_ref[...] = jnp.zeros_like(dfw_ref)

        l_ref[...] += jnp.broadcast_to(val, l_ref.shape)
        dfw_ref[...] += dfw

    row = pl.BlockSpec((tm, D_MODEL), lambda i: (i, 0))
    vec = pl.BlockSpec((1, D_MODEL), lambda i: (0, 0))
    return pl.pallas_call(
        body, grid=(L // tm,), name="loss_fwd_bwd", in_specs=[row, vec, row],
        out_specs=[pl.BlockSpec((1, LANE), lambda i: (0, 0)), row, vec],
        out_shape=[jax.ShapeDtypeStruct((1, LANE), F32), jax.ShapeDtypeStruct((L, D_MODEL), F32),
                   jax.ShapeDtypeStruct((1, D_MODEL), F32)],
        compiler_params=pltpu.CompilerParams(vmem_limit_bytes=VMEM_LIMIT, dimension_semantics=("arbitrary",)),
    )(x, fw, target)


def adamw(w, g, m, v):
    rows, cols = w.shape
    tr = _row_tile(rows, cols, budget=1024 * 1024)
    c1 = 1.0 - ADAM_B1 ** ADAM_STEP
    c2 = 1.0 - ADAM_B2 ** ADAM_STEP

    def body(w_ref, g_ref, m_ref, v_ref, d_ref, nm_ref, nv_ref):
        g = g_ref[...]
        nm = ADAM_B1 * m_ref[...] + (1.0 - ADAM_B1) * g
        nv = ADAM_B2 * v_ref[...] + (1.0 - ADAM_B2) * jnp.square(g)
        d_ref[...] = -ADAM_LR * ((nm / c1) / (jnp.sqrt(nv / c2) + ADAM_EPS) + ADAM_WD * w_ref[...])
        nm_ref[...] = nm
        nv_ref[...] = nv

    blk = pl.BlockSpec((tr, cols), lambda i: (i, 0))
    return pl.pallas_call(
        body, grid=(rows // tr,), name="adamw", in_specs=[blk] * 4, out_specs=[blk] * 3,
        out_shape=[jax.ShapeDtypeStruct((rows, cols), F32)] * 3,
    )(w, g, m, v)


def s5_prep(lam_re, lam_im, log_dt, b_re, b_im, c_re, c_im):
    lr = jnp.minimum(lam_re, -1e-4)
    li = lam_im
    dt = jnp.exp(log_dt)[:, None]
    mag = jnp.exp(lr * dt)
    ar = mag * jnp.cos(li * dt)
    ai = mag * jnp.sin(li * dt)
    den = lr * lr + li * li
    fr = ((ar - 1.0) * lr + ai * li) / den
    fi = (ai * lr - (ar - 1.0) * li) / den
    bbr = fr[..., None] * b_re - fi[..., None] * b_im
    bbi = fr[..., None] * b_im + fi[..., None] * b_re
    hg = S5_GROUPS // 2
    emb_b = lambda bb: _block_diag(bb.transpose(0, 2, 1).reshape(hg * S5_GROUP, S5_STATE), hg)
    emb_c = lambda cc: _block_diag(cc.transpose(0, 2, 1).reshape(hg * S5_STATE, S5_GROUP), hg)
    bsub = jnp.stack([emb_b(bbr[:hg]), emb_b(bbr[hg:]), emb_b(bbi[:hg]), emb_b(bbi[hg:])])
    csub = jnp.stack([jnp.concatenate([emb_c(c_re[:hg]), -emb_c(c_im[:hg])], axis=0),
                      jnp.concatenate([emb_c(c_re[hg:]), -emb_c(c_im[hg:])], axis=0)])
    nb = S5_N // LANE
    return ar.reshape(nb, 1, LANE), ai.reshape(nb, 1, LANE), bsub, csub


def _block_diag(stacked, groups):
    rows, c = stacked.shape
    r = rows // groups
    row_g = jnp.arange(rows)[:, None] // r
    col_g = jnp.arange(groups * c)[None, :] // c
    return jnp.where(row_g == col_g, jnp.tile(stacked, (1, groups)), 0.0)


def rg_prep(w):
    return _block_diag(w.reshape(BW, RG_BLOCK), RG_BLOCKS)


def hg_prep(logits):
    p = jax.nn.softmax(logits, axis=0)
    return jnp.cumsum(p, axis=0) - p[0]


def _head_mean_matrix():
    r = jnp.arange(BW) // HG_D
    return (r[:, None] == r[None, :]).astype(F32) / HG_D


def _to_segment_order(a):
    L = a.shape[0]
    return a.reshape(NSEG, L // NSEG, -1).transpose(1, 0, 2).reshape(a.shape)


def _to_time_order(a):
    L = a.shape[0]
    return a.reshape(L // NSEG, NSEG, -1).transpose(1, 0, 2).reshape(a.shape)


def _const(*idx):
    return lambda s, i: idx


def _rows(cb=0):
    return lambda s, i: (i, cb)


def _sum_parts(name, first, parts, shape):
    return add_n(name, [(first, ())] + [(parts, (s,)) for s in range(NSH)], shape)


def _ffn_weight_specs(l, j):
    F = D_FF // NSH
    one = pl.Buffered(1)
    return [pl.BlockSpec((None, NSH, D_MODEL, F), lambda i: (j, 0, 0, 0), pipeline_mode=one),
            pl.BlockSpec((None, NSH, D_MODEL, F), lambda i: (j, 0, 0, 0), pipeline_mode=one),
            pl.BlockSpec((None, NSH, F, D_MODEL), lambda i: (j, 0, 0, 0), pipeline_mode=one)]


def ffn_fwd(name, x, W, l, j, k, L, tm):
    D, F = D_MODEL, D_FF // NSH

    def body(x_ref, nw_ref, wg_ref, wu_ref, wd_ref, y_ref, g_ref, u_ref):
        x = x_ref[...]
        h = _rms(x, nw_ref[...]).astype(MMT)
        y = x
        for s in range(NSH):
            g = _dg(h, wg_ref[s], 1, 0)
            u = _dg(h, wu_ref[s], 1, 0)
            g_ref[s] = g.astype(g_ref.dtype)
            u_ref[s] = u.astype(u_ref.dtype)
            y = y + 0.5 * _dg((jax.nn.silu(g) * u).astype(MMT), wd_ref[s], 1, 0)
        y_ref[...] = y

    row = pl.BlockSpec((tm, D), lambda i: (i, 0))
    act = pl.BlockSpec((NSH, tm, F), lambda i: (0, i, 0))
    return pl.pallas_call(
        body, grid=(L // tm,), name=name,
        in_specs=[row, pl.BlockSpec((None, None, 1, D), lambda i: (l, k, 0, 0))] + _ffn_weight_specs(l, j),
        out_specs=[row, act, act],
        out_shape=[jax.ShapeDtypeStruct((L, D), F32), jax.ShapeDtypeStruct((NSH, L, F), MMT),
                   jax.ShapeDtypeStruct((NSH, L, F), MMT)],
        compiler_params=pltpu.CompilerParams(vmem_limit_bytes=VMEM_LIMIT, dimension_semantics=("arbitrary",)),
    )(x, W["nw"], W["L"][l]["wg"], W["L"][l]["wu"], W["L"][l]["wd"])


def ffn_bwd(name, x, g, u, dy, W, bufs, l, j, k, L, tm):
    D, F = D_MODEL, D_FF // NSH
    tm = min(TM_WGRAD, L)

    def body(x_ref, nw_ref, dy_ref, g_ref, u_ref, wg_ref, wu_ref, wd_ref, *rest):
        part_ref, dnw_ref, dwg_ref, dwu_ref, dwd_ref = rest[-5:]
        s, i = pl.program_id(0), pl.program_id(1)
        x, nw = x_ref[...], nw_ref[...]
        r = lax.rsqrt(jnp.mean(x * x, axis=-1, keepdims=True) + EPS)
        xhat = x * r
        h = (xhat * nw).astype(MMT)
        half_dy = (0.5 * dy_ref[...]).astype(MMT)
        gs, us = g_ref[...].astype(F32), u_ref[...].astype(F32)
        sig = jax.nn.sigmoid(gs)
        act = gs * sig
        da = _dg(half_dy, wd_ref[...], 1, 1)
        du = (da * act).astype(MMT)
        dg = (da * us * (sig * (1.0 + gs * (1.0 - sig)))).astype(MMT)
        dh = _dg(dg, wg_ref[...], 1, 1) + _dg(du, wu_ref[...], 1, 1)
        dxh = dh * nw
        part_ref[...] = (r * (dxh - xhat * jnp.mean(dxh * xhat, axis=-1, keepdims=True))).astype(part_ref.dtype)
        grads = (_dg(h, dg, 0, 0), _dg(h, du, 0, 0), _dg((act * us).astype(MMT), half_dy, 0, 0))
        dnw = jnp.sum(dh * xhat, axis=0, keepdims=True)
        first = jnp.logical_and(s == 0, i == 0)
        for ref, val, start in zip((dwg_ref, dwu_ref, dwd_ref, dnw_ref), grads + (dnw,), (i == 0, i == 0, i == 0, first)):
            @pl.when(start)
            def _(ref=ref, val=val):
                ref[...] = val

            @pl.when(jnp.logical_not(start))
            def _(ref=ref, val=val):
                ref[...] += val

    row = pl.BlockSpec((tm, D), lambda s, i: (i, 0))
    act = pl.BlockSpec((None, tm, F), lambda s, i: (s, i, 0))
    wsp = lambda r, c: pl.BlockSpec((None, None, r, c), lambda s, i: (j, s, 0, 0))
    gsp = lambda r, c: pl.BlockSpec((None, None, r, c), lambda s, i: (0, s, 0, 0))
    part, dnw, bufs[("ffn_gate", l, j)], bufs[("ffn_up", l, j)], bufs[("ffn_down", l, j)] = pl.pallas_call(
        body, grid=(NSH, L // tm), name=name,
        in_specs=[row, pl.BlockSpec((None, None, 1, D), lambda s, i: (l, k, 0, 0)), row, act, act,
                  wsp(D, F), wsp(D, F), wsp(F, D)],
        out_specs=[pl.BlockSpec((None, tm, D), lambda s, i: (s, i, 0)), pl.BlockSpec((1, D), lambda s, i: (0, 0)),
                   gsp(D, F), gsp(D, F), gsp(F, D)],
        out_shape=[jax.ShapeDtypeStruct((NSH, L, D), MMT), jax.ShapeDtypeStruct((1, D), F32)]
        + [jax.ShapeDtypeStruct((1, NSH, D, F), F32)] * 2 + [jax.ShapeDtypeStruct((1, NSH, F, D), F32)],
        compiler_params=pltpu.CompilerParams(vmem_limit_bytes=VMEM_LIMIT, dimension_semantics=("arbitrary", "arbitrary")),
    )(x, W["nw"], dy, g, u, W["L"][l]["wg"], W["L"][l]["wu"], W["L"][l]["wd"])
    return _sum_parts(name + "_dx", dy, part, (L, D)), dnw


def layer_fwd(l, x0, W, P, L, tm):
    D = D_MODEL
    tmm = tm
    tm = min(TM_FWD, L)
    n_i = L // tm
    x1, g0, u0 = ffn_fwd(f"ffn_fwd_{l}0", x0, W, l, 0, 0, L, tm)
    proj = tile_fwd(
        lambda x, nw, win, s: pre_core(x, nw, win), f"pre_fwd_{l}", n_i, NSH,
        [(x1, (tm, D), _rows()), (W["nw"], (None, None, 1, D), _const(l, 1, 0, 0)),
         (W["L"][l]["win"], (None, D, IN_TOTAL // NSH), lambda s, i: (s, 0, 0))],
        [((L, IN_TOTAL), F32, (tm, IN_TOTAL // NSH), lambda s, i: (i, s), False)], s_outer=True)[0]
    nb = S5_N // LANE
    blk3 = lambda s, i: (0, i, 0)
    bur, bui = tile_fwd(
        lambda *a: s5_pre_core(*a[:-1]), f"s5pre_fwd_{l}", n_i, 1,
        [(proj, (tm, BW), _rows(0))] + [(P["bsub"], (None, None, BW // 2, S5_N // 2), _const(l, q, 0, 0)) for q in range(4)],
        [((nb, L, LANE), F32, (nb, tm, LANE), blk3, False)] * 2)
    xr, xi = s5_scan_fwd(bur, bui, P["ar"][l], P["ai"][l], L)
    qzv = _to_time_order(proj[:, BW:4 * BW])
    o_t, sst = hg_fwd(qzv, P["lb"][l], L)
    o = _to_segment_order(o_t)
    xc = conv_fwd(proj, W["convw"][l], P["convb"][l], L)
    vec = (None, 1, BW)
    a, b = tile_fwd(
        lambda xc, wa, ba, wx, bx, lam, s: gates_core(xc, wa, ba, wx, bx, lam), f"gates_fwd_{l}", n_i, 1,
        [(xc, (tm, BW), _rows()), (P["wa"], (None, BW, BW), _const(l, 0, 0)), (P["ba"], vec, _const(l, 0, 0)),
         (P["wx"], (None, BW, BW), _const(l, 0, 0)), (P["bx"], vec, _const(l, 0, 0)), (P["lam"], vec, _const(l, 0, 0))],
        [((L, BW), F32, (tm, BW), _rows(), False)] * 2)
    hs = rg_scan_fwd(a, b, L)
    ya, yb, yc = tile_fwd(
        lambda *a: mid_core(*a[:-1]), f"mid_fwd_{l}", L // tmm, 1,
        [(xr, (nb, tmm, LANE), blk3), (xi, (nb, tmm, LANE), blk3), (proj, (tmm, BW), _rows(0)), (o, (tmm, BW), _rows()),
         (proj, (tmm, BW), _rows(4)), (hs, (tmm, BW), _rows()), (proj, (tmm, BW), _rows(6)),
         (P["hmat"], (BW, BW), _const(0, 0)), (P["csub"], (None, None, S5_N, BW // 2), _const(l, 0, 0, 0)),
         (P["csub"], (None, None, S5_N, BW // 2), _const(l, 1, 0, 0)), (P["d"], vec, _const(l, 0, 0)),
         (W["L"][l]["gluw"], (BW, BW), _const(0, 0)), (P["glub"], vec, _const(l, 0, 0)), (P["hgw"], vec, _const(l, 0, 0))],
        [((L, BW), F32, (tmm, BW), _rows(), False)] * 3)
    x2 = tile_fwd(
        lambda x, *rest: (x + merge_core(*rest[:-1])[0],), f"merge_fwd_{l}", n_i, 1,
        [(x1, (tm, D), _rows()), (ya, (tm, BW), _rows()), (yb, (tm, BW), _rows()), (yc, (tm, BW), _rows())]
        + [(proj, (tm, BW), _rows(7 + k)) for k in range(6)]
        + [(W["L"][l]["pfull"], (3, BW, D), _const(0, 0, 0)), (W["L"][l]["woutfull"], (D, D), _const(0, 0))],
        [((L, D), F32, (tm, D), _rows(), False)])[0]
    x3, g1, u1 = ffn_fwd(f"ffn_fwd_{l}1", x2, W, l, 1, 2, L, tm)
    saved = dict(x0=x0, x1=x1, x2=x2, proj=proj, xr=xr, xi=xi, o=o, sst=sst, xc=xc, a=a, hs=hs, ya=ya, yb=yb, yc=yc,
                 qzv=qzv, g0=g0, u0=u0, g1=g1, u1=u1)
    return x3, saved


def layer_bwd(l, dx3, sv, W, P, bufs, L, tm, ready=lambda l, group: None):
    D = D_MODEL
    n_i = L // tm
    nb = S5_N // LANE
    dq = D // NSH
    vec = (None, 1, BW)
    vout = ((1, BW), (1, BW), _const(0, 0), "acc_all")
    blk3 = lambda s, i: (0, i, 0)
    small = {}
    proj = sv["proj"]

    dx2, dnw2 = ffn_bwd(f"ffn_bwd_{l}1", sv["x2"], sv["g1"], sv["u1"], dx3, W, bufs, l, 1, 2, L, tm)
    ready(l, "ffn1")

    rw256 = ((L, BW), (tm, BW), _rows(), "write")
    res = tile_bwd(
        merge_core, f"merge_bwd_{l}", n_i, 1,
        [(sv["ya"], (tm, BW), _rows(), "r"), (sv["yb"], (tm, BW), _rows(), "r"), (sv["yc"], (tm, BW), _rows(), "r")]
        + [(proj, (tm, BW), _rows(7 + k), "r") for k in range(6)]
        + [(W["L"][l]["pfull"], (3, BW, D), _const(0, 0, 0), "w"), (W["L"][l]["woutfull"], (D, D), _const(0, 0), "w")],
        [(dx2, (tm, D), _rows())],
        [rw256] * 9
        + [((3, BW, D), (3, BW, D), _const(0, 0, 0), "acc_all"), ((D, D), (D, D), _const(0, 0), "acc_all")])
    dya, dyb, dyc = res[:3]
    dgm = res[3:9]
    bufs[("branch_proj", l)], bufs[("w_out", l)] = res[9:]
    ready(l, "merge")

    tmm = tm
    rw = ((L, BW), (tmm, BW), _rows(), "write")
    xw = ((nb, L, LANE), (nb, tmm, LANE), blk3, "write")
    res = tile_bwd(
        mid_core, f"mid_bwd_{l}", L // tmm, 1,
        [(sv["xr"], (nb, tmm, LANE), blk3, "r"), (sv["xi"], (nb, tmm, LANE), blk3, "r"), (proj, (tmm, BW), _rows(0), "r"),
         (sv["o"], (tmm, BW), _rows(), "r"), (proj, (tmm, BW), _rows(4), "r"), (sv["hs"], (tmm, BW), _rows(), "r"),
         (proj, (tmm, BW), _rows(6), "r"), (P["hmat"], (BW, BW), _const(0, 0), "c"),
         (P["csub"], (None, None, S5_N, BW // 2), _const(l, 0, 0, 0), "w"),
         (P["csub"], (None, None, S5_N, BW // 2), _const(l, 1, 0, 0), "w"), (P["d"], vec, _const(l, 0, 0), "p"),
         (W["L"][l]["gluw"], (BW, BW), _const(0, 0), "w"), (P["glub"], vec, _const(l, 0, 0), "p"),
         (P["hgw"], vec, _const(l, 0, 0), "p")],
        [(dya, (tmm, BW), _rows()), (dyb, (tmm, BW), _rows()), (dyc, (tmm, BW), _rows())],
        [xw, xw, rw, rw, rw, rw, rw,
         ((DEPTH, S5_N, BW // 2), (None, S5_N, BW // 2), _const(l, 0, 0), "acc_all", bufs.get("csub0")),
         ((DEPTH, S5_N, BW // 2), (None, S5_N, BW // 2), _const(l, 0, 0), "acc_all", bufs.get("csub1")), vout,
         ((BW, BW), (BW, BW), _const(0, 0), "acc_all"), vout, vout])
    dxr, dxi, du_skip, do, dg_b, dhs, dgate_c, bufs["csub0"], bufs["csub1"], dd, bufs[("s5_glu_w", l)], dglub, dhgw = res
    small["s5_d"], small["s5_glu_b"], small["hg_norm_w"] = dd[0], dglub[0], dhgw[0]
    ready(l, "mid")

    da, db = rg_scan_bwd(sv["a"], sv["hs"], dhs, L)
    wmat = lambda key: ((DEPTH, BW, BW), (None, BW, BW), _const(l, 0, 0), "acc_all", bufs.get(key))
    res = tile_bwd(
        gates_core, f"gates_bwd_{l}", n_i, 1,
        [(sv["xc"], (tm, BW), _rows(), "r"), (P["wa"], (None, BW, BW), _const(l, 0, 0), "w"), (P["ba"], vec, _const(l, 0, 0), "p"),
         (P["wx"], (None, BW, BW), _const(l, 0, 0), "w"), (P["bx"], vec, _const(l, 0, 0), "p"), (P["lam"], vec, _const(l, 0, 0), "p")],
        [(da, (tm, BW), _rows()), (db, (tm, BW), _rows())],
        [((L, BW), (tm, BW), _rows(), "write"), wmat("wa"), vout, wmat("wx"), vout, vout])
    dxc, bufs["wa"], dba, bufs["wx"], dbx, dlam = res
    small["rg_ba"], small["rg_bx"], small["rg_lambda"] = dba[0], dbx[0], dlam[0]
    dx_c, dconvw, dconvb = conv_bwd(proj, W["convw"][l], dxc, L)
    small["rg_conv_w"], small["rg_conv_b"] = dconvw, dconvb[0]

    dq_b, dz_b, dv_b, dlb = hg_bwd(sv["qzv"], P["lb"][l], sv["sst"], _to_time_order(do), L)
    dq_b, dz_b, dv_b = [_to_segment_order(a) for a in (dq_b, dz_b, dv_b)]

    gr, gi, dar, dai = s5_scan_bwd(dxr, dxi, sv["xr"], sv["xi"], P["ar"][l], P["ai"][l], L)
    bblk = (None, None, BW // 2, S5_N // 2)
    res = tile_bwd(
        s5_pre_core, f"s5pre_bwd_{l}", n_i, 1,
        [(proj, (tm, BW), _rows(0), "r")] + [(P["bsub"], bblk, _const(l, q, 0, 0), "w") for q in range(4)],
        [(gr, (nb, tm, LANE), blk3), (gi, (nb, tm, LANE), blk3)],
        [((L, BW), (tm, BW), _rows(), "write")]
        + [((DEPTH, BW // 2, S5_N // 2), bblk[1:], _const(l, 0, 0), "acc_all", bufs.get(f"bsub{q}")) for q in range(4)])
    du_pre = res[0]
    for q in range(4):
        bufs[f"bsub{q}"] = res[1 + q]
    du_a = add_n(f"du_a_{l}", [(du_skip, ()), (du_pre, ())], (L, BW))
    prep_ct = dict(dar=dar, dai=dai, dlb=dlb)

    pieces = [du_a, dq_b, dz_b, dv_b, dg_b, dx_c, dgate_c, *dgm]
    per_piece, per_shard = BW // LANE, IN_TOTAL // NSH // LANE
    dx1, dnw1 = dx2, []
    tmw = min(TM_WGRAD, L)
    pre_and_x = lambda x, nw, win: (pre_core(x, nw, win)[0], x)
    for s in range(NSH):
        groups = [(pieces[g // per_piece], (tmw, LANE), _rows(g % per_piece))
                  for g in range(s * per_shard, (s + 1) * per_shard)]
        dx1, dnw_s, bufs[("w_in", l)] = tile_bwd(
            pre_and_x, f"pre_bwd_{l}{s}", L // tmw, 1,
            [(sv["x1"], (tmw, D), _rows(), "r"), (W["nw"], (None, None, 1, D), _const(l, 1, 0, 0), "p"),
             (W["L"][l]["win"], (None, D, IN_TOTAL // NSH), _const(s, 0, 0), "w")],
            [groups, (dx1, (tmw, D), _rows())],
            [((L, D), (tmw, D), _rows(), "write"),
             ((1, D), (1, D), _const(0, 0), "acc_all"),
             ((1, NSH, D, IN_TOTAL // NSH), (None, None, D, IN_TOTAL // NSH), _const(0, s, 0, 0), "acc_all",
              bufs.get(("w_in", l)))])
        dnw1.append(dnw_s)
    dnw1 = (dnw1[0] + dnw1[1]) + (dnw1[2] + dnw1[3])
    ready(l, "pre")

    dx0, dnw0 = ffn_bwd(f"ffn_bwd_{l}0", sv["x0"], sv["g0"], sv["u0"], dx1, W, bufs, l, 0, 0, L, tm)
    ready(l, "ffn0")
    small["norm_w"] = jnp.concatenate([dnw0, dnw1, dnw2], axis=0)
    return dx0, small, prep_ct


SMALL_RAW = ("s5_lambda_re", "s5_lambda_im", "s5_log_dt", "s5_b_re", "s5_b_im", "s5_c_re", "s5_c_im", "s5_d", "s5_glu_b",
             "hg_lb_logits", "hg_norm_w", "rg_conv_b", "rg_wa", "rg_ba", "rg_wx", "rg_bx", "rg_lambda", "final_norm_w")
DEPTH = 2


def local_step(x, target, W, raw, layer_weights=None, layer_grads=None):
    L = x.shape[0]
    tm = min(256, L)
    col = lambda v: v.reshape(DEPTH, 1, BW)
    (ar, ai, bsub, csub), s5_vjp = jax.vjp(jax.vmap(s5_prep), *[raw[k] for k in SMALL_RAW[:7]])
    (wa, wx), rg_vjp = jax.vjp(lambda a, b: (jax.vmap(rg_prep)(a), jax.vmap(rg_prep)(b)), raw["rg_wa"], raw["rg_wx"])
    lb, hg_vjp = jax.vjp(hg_prep, raw["hg_lb_logits"])
    P = dict(
        ar=[ar[l] for l in range(DEPTH)], ai=[ai[l] for l in range(DEPTH)],
        bsub=bsub.astype(MMT), csub=csub.astype(MMT), wa=wa.astype(MMT), wx=wx.astype(MMT),
        lb=[lb[l].reshape(1, BW) for l in range(DEPTH)], convb=[raw["rg_conv_b"][l].reshape(1, BW) for l in range(DEPTH)],
        ba=col(raw["rg_ba"]), bx=col(raw["rg_bx"]), lam=col(raw["rg_lambda"]), d=col(raw["s5_d"]),
        glub=col(raw["s5_glu_b"]), hgw=col(raw["hg_norm_w"]), hmat=_head_mean_matrix())

    saved = []
    h = _to_segment_order(x)
    for l in range(DEPTH):
        if layer_weights is not None:
            W["L"][l], h = layer_weights(l, h)
        h, sv = layer_fwd(l, h, W, P, L, tm)
        saved.append(sv)
    loss, dh, dfw = loss_fwd_bwd(h, raw["final_norm_w"].reshape(1, D_MODEL), _to_segment_order(target), L, tm)

    big, per_layer, prep_cts = {}, [None] * DEPTH, [None] * DEPTH
    ready = (lambda l, group: None) if layer_grads is None else (lambda l, group: layer_grads(l, group, big))
    for l in reversed(range(DEPTH)):
        dh, sm, pc = layer_bwd(l, dh, saved[l], W, P, big, L, tm, ready)
        per_layer[l], prep_cts[l] = sm, pc
    dh = _to_time_order(dh)

    small = {k: jnp.stack([per_layer[l][k] for l in range(DEPTH)]) for k in per_layer[0]}
    both = lambda k: jnp.stack([prep_cts[l][k] for l in range(DEPTH)])
    dbsub = jnp.stack([big.pop(f"bsub{q}") for q in range(4)], axis=1)
    dcsub = jnp.stack([big.pop("csub0"), big.pop("csub1")], axis=1)
    s5_g = s5_vjp((both("dar"), both("dai"), dbsub, dcsub))
    small.update(zip(SMALL_RAW[:7], s5_g))
    small["rg_wa"], small["rg_wx"] = rg_vjp((big.pop("wa"), big.pop("wx")))
    (small["hg_lb_logits"],) = hg_vjp(jnp.concatenate([prep_cts[l]["dlb"] for l in range(DEPTH)], axis=0))
    small["final_norm_w"] = dfw[0]
    return loss, dh, big, small


ANY = pl.BlockSpec(memory_space=pl.ANY)


def _place():
    x, y, c = lax.axis_index("x"), lax.axis_index("y"), lax.axis_index("c")
    chips = [(1 - x, y), (x, 1 - y), (1 - x, 1 - y)]
    return x, y, c, chips


def _remote(src, dst, send, recv, k, to):
    return pltpu.make_async_remote_copy(src_ref=src, dst_ref=dst, send_sem=send.at[k], recv_sem=recv.at[k],
                                        device_id=to, device_id_type=MESH)


def _comm_call(body, name, ins, out_shapes, n_sem, n_loc):
    return pl.pallas_call(
        body, name=name, in_specs=[ANY] * len(ins), out_specs=[ANY] * len(out_shapes), out_shape=out_shapes,
        scratch_shapes=[pltpu.SemaphoreType.DMA((n_sem,)), pltpu.SemaphoreType.DMA((n_sem,)),
                        pltpu.SemaphoreType.DMA((max(n_loc, 1),))],
    )(*ins)


def gather_shards(name, shards):
    n = len(shards)
    per = 8

    def body(*refs):
        ins, outs = refs[:n], refs[n:2 * n]
        send, recv, _ = refs[2 * n:]
        x, y, c, chips = _place()
        me = 2 * x + y
        sib = (x, y, 1 - c)
        sends = []
        for w in range(n):
            for j, (cx, cy) in enumerate(chips):
                cp = _remote(ins[w].at[c], outs[w].at[c, me], send, recv, per * w + j, (cx, cy, c))
                cp.start()
                sends.append(cp)
        for w in range(n):
            for l in range(2):
                cp = _remote(ins[w].at[l], outs[w].at[l, me], send, recv, per * w + 6 + l, sib)
                cp.start()
                sends.append(cp)
        for w in range(n):
            for j, (cx, cy) in enumerate(chips):
                theirs = outs[w].at[c, 2 * cx + cy]
                _remote(ins[w].at[c], theirs, send, recv, per * w + j, (cx, cy, c)).wait_recv()
                cp = _remote(theirs, theirs, send, recv, per * w + 3 + j, sib)
                cp.start()
                sends.append(cp)
        for w in range(n):
            for j, (cx, cy) in enumerate(chips):
                dst = outs[w].at[1 - c, 2 * cx + cy]
                _remote(dst, dst, send, recv, per * w + 3 + j, sib).wait_recv()
            for l in range(2):
                dst = outs[w].at[l, me]
                _remote(dst, dst, send, recv, per * w + 6 + l, sib).wait_recv()
        for cp in sends:
            cp.wait_send()

    shapes = [jax.ShapeDtypeStruct((2, NSH) + s.shape[1:], s.dtype) for s in shards]
    return _comm_call(body, name, shards, shapes, per * n, 0)


def exchange_halves(name, grads, ranges):
    n = len(grads)

    def body(*refs):
        ins, outs = refs[:n], refs[n:2 * n]
        send, recv, _ = refs[2 * n:]
        x, y, c, _chips = _place()
        cps = []
        for w in range(n):
            h = grads[w].shape[2] // 2
            p0, np_ = ranges[w]
            cp = _remote(ins[w].at[pl.ds(p0, np_), :, pl.ds((1 - c) * h, h)], outs[w], send, recv, w, (x, y, 1 - c))
            cp.start()
            cps.append(cp)
        for cp in cps:
            cp.wait()

    shapes = [jax.ShapeDtypeStruct((r[1], NSH, g.shape[2] // 2, g.shape[3]), g.dtype) for g, r in zip(grads, ranges)]
    return _comm_call(body, name, grads, shapes, n, 0)


def share_halves(name, pieces):
    n = len(pieces)

    def body(*refs):
        ins, outs = refs[:n], refs[n:2 * n]
        send, recv, _ = refs[2 * n:]
        x, y, c, _chips = _place()
        cps = []
        for w in range(n):
            cp = _remote(ins[w], outs[w], send, recv, w, (x, y, 1 - c))
            cp.start()
            cps.append(cp)
        for cp in cps:
            cp.wait()

    return _comm_call(body, name, pieces, [jax.ShapeDtypeStruct(p.shape, p.dtype) for p in pieces], n, 0)


def add_own_half(name, g, ra, c, wire, b0):
    nblk, h, cols = ra.shape
    tr = _row_tile(h, cols, mult=16)
    nt = h // tr

    def body(c_ref, g_ref, r_ref, o_ref):
        o_ref[...] = (g_ref[...] + r_ref[...]).astype(o_ref.dtype)

    blk = (None, tr, cols)
    return pl.pallas_call(
        body, name=name,
        grid_spec=pltpu.PrefetchScalarGridSpec(
            num_scalar_prefetch=1, grid=(nblk, nt),
            in_specs=[pl.BlockSpec(blk, lambda s, i, c_ref: (b0 + s, c_ref[0] * nt + i, 0)), pl.BlockSpec(blk, lambda s, i, c_ref: (s, i, 0))],
            out_specs=pl.BlockSpec(blk, lambda s, i, c_ref: (s, i, 0))),
        out_shape=jax.ShapeDtypeStruct(ra.shape, wire),
    )(c.reshape(1), g, ra)


def add_chips(name, hb, rb, me):
    npc, _, h, cols = hb.shape
    tr = _row_tile(h, cols, mult=16)

    def body(me_ref, h_ref, r0, r1, r2, o_ref):
        f = lambda r: r[...].astype(F32)
        o_ref[...] = ((f(h_ref) + f(r0)) + f(r1)) + f(r2)

    rspec = lambda j: pl.BlockSpec((None, None, tr, cols), functools.partial(lambda p, i, me_ref, j: (j, p, i, 0), j=j))
    return pl.pallas_call(
        body, name=name,
        grid_spec=pltpu.PrefetchScalarGridSpec(
            num_scalar_prefetch=1, grid=(npc, h // tr),
            in_specs=[pl.BlockSpec((None, None, tr, cols), lambda p, i, me_ref: (p, me_ref[0], i, 0)), rspec(0), rspec(1), rspec(2)],
            out_specs=pl.BlockSpec((None, tr, cols), lambda p, i, me_ref: (p, i, 0))),
        out_shape=jax.ShapeDtypeStruct((npc, h, cols), F32),
    )(me.reshape(1), hb, rb, rb, rb)


def adamw_halves(name, w, m, v, own, other, c):
    npc, rows, cols = w.shape
    h = rows // 2
    tr = _row_tile(h, cols, budget=1024 * 1024)
    nt = h // tr
    c1 = 1.0 - ADAM_B1 ** ADAM_STEP
    c2 = 1.0 - ADAM_B2 ** ADAM_STEP

    def body(c_ref, w_ref, m_ref, v_ref, own_ref, oth_ref, g_ref, d_ref, nm_ref, nv_ref):
        g = jnp.where(pl.program_id(1) == c_ref[0], own_ref[...], oth_ref[...])
        nm = ADAM_B1 * m_ref[...] + (1.0 - ADAM_B1) * g
        nv = ADAM_B2 * v_ref[...] + (1.0 - ADAM_B2) * jnp.square(g)
        g_ref[...] = g
        d_ref[...] = -ADAM_LR * ((nm / c1) / (jnp.sqrt(nv / c2) + ADAM_EPS) + ADAM_WD * w_ref[...])
        nm_ref[...] = nm
        nv_ref[...] = nv

    full = pl.BlockSpec((None, tr, cols), lambda p, hh, i, c_ref: (p, hh * nt + i, 0))
    half = pl.BlockSpec((None, tr, cols), lambda p, hh, i, c_ref: (p, i, 0))
    return pl.pallas_call(
        body, name=name,
        grid_spec=pltpu.PrefetchScalarGridSpec(
            num_scalar_prefetch=1, grid=(npc, 2, nt),
            in_specs=[full, full, full, half, half], out_specs=[full] * 4),
        out_shape=[jax.ShapeDtypeStruct(w.shape, F32)] * 4,
    )(c.reshape(1), w, m, v, own, other)


WEIGHTS = ("norm_w", "final_norm_w", "ffn_gate", "ffn_up", "ffn_down", "w_in", "branch_proj", "w_out", "s5_lambda_re",
           "s5_lambda_im", "s5_log_dt", "s5_b_re", "s5_b_im", "s5_c_re", "s5_c_im", "s5_d", "s5_glu_w", "s5_glu_b",
           "hg_lb_logits", "hg_norm_w", "rg_conv_w", "rg_conv_b", "rg_wa", "rg_ba", "rg_wx", "rg_bx", "rg_lambda")
BIG = ("ffn_gate", "ffn_up", "ffn_down", "w_in", "branch_proj", "w_out", "s5_glu_w")
SHARDED_SMALL = ("norm_w", "rg_conv_w")
SMALL = SMALL_RAW + SHARDED_SMALL


def _view2d(shape):
    return (1, shape[0]) if len(shape) == 1 else (math.prod(shape[:-1]), shape[-1])


def _small_layout(shapes, row_multiple):
    layout, at = [], 0
    for shape in shapes:
        r, c = _view2d(shape)
        rp = -(-r // 8) * 8
        layout.append((at, r, c, rp))
        at += rp * max(1, c // LANE)
    return layout, -(-at // row_multiple) * row_multiple


def pack_small(name, arrays, row_multiple):
    layout, rows = _small_layout([a.shape for a in arrays], row_multiple)

    def body(*refs):
        out = refs[-1]
        out[...] = jnp.zeros_like(out)
        for ref, (r0, r, c, rp) in zip(refs[:-1], layout):
            if c <= LANE:
                out[r0:r0 + r, 0:c] = ref[...]
            else:
                for q in range(c // LANE):
                    out[r0 + q * rp:r0 + q * rp + r, :] = ref[:, q * LANE:(q + 1) * LANE]

    return pl.pallas_call(
        body, name=name, out_shape=jax.ShapeDtypeStruct((rows, LANE), F32),
        compiler_params=pltpu.CompilerParams(vmem_limit_bytes=VMEM_LIMIT),
    )(*[a.reshape(_view2d(a.shape)) for a in arrays])


def unpack_small(name, packed, shapes):
    layout, _ = _small_layout(shapes, 8)

    def body(p_ref, *outs):
        for ref, (r0, r, c, rp) in zip(outs, layout):
            if c <= LANE:
                ref[...] = p_ref[r0:r0 + r, 0:c]
            else:
                for q in range(c // LANE):
                    ref[:, q * LANE:(q + 1) * LANE] = p_ref[r0 + q * rp:r0 + q * rp + r, :]

    res = pl.pallas_call(
        body, name=name, out_shape=[jax.ShapeDtypeStruct(_view2d(s), F32) for s in shapes],
        compiler_params=pltpu.CompilerParams(vmem_limit_bytes=VMEM_LIMIT),
    )(packed)
    return [a.reshape(s) for a, s in zip(res, shapes)]


HBM = pl.BlockSpec(memory_space=pltpu.HBM)
SEM = pl.BlockSpec(memory_space=pltpu.SEMAPHORE)
EFFECT = pltpu.SideEffectType.DATAFLOW_SIDE_EFFECTING


def split_start(name, srcs, land_shapes, plan, n_send, n_recv):
    ns, nl = len(srcs), len(land_shapes)

    def body(*refs):
        ins, lands = refs[:ns], refs[ns:ns + nl]
        send, recv = refs[ns + nl], refs[ns + nl + 1]
        for src, dst, ks, kr, dev in plan(ins, lands):
            pltpu.make_async_remote_copy(src_ref=src, dst_ref=dst, send_sem=send.at[ks], recv_sem=recv.at[kr],
                                         device_id=dev, device_id_type=MESH).start()
        refs[-1][...] = jnp.zeros_like(refs[-1])

    hbm = lambda a: pltpu.with_memory_space_constraint(a, pltpu.HBM)
    lands = [lax.empty(s.shape, s.dtype) for s in land_shapes]
    out = pl.pallas_call(
        body, name=name,
        out_shape=(pltpu.SemaphoreType.DMA((n_send,)), pltpu.SemaphoreType.DMA((n_recv,)),
                   *[pltpu.HBM(a.shape, a.dtype) for a in srcs], *[pltpu.HBM(s.shape, s.dtype) for s in land_shapes],
                   jax.ShapeDtypeStruct((8, LANE), F32)),
        in_specs=[HBM] * (ns + nl), out_specs=(SEM, SEM, *[HBM] * (ns + nl), pl.BlockSpec(memory_space=pltpu.VMEM)),
        input_output_aliases={k: 2 + k for k in range(ns + nl)},
        compiler_params=pltpu.CompilerParams(has_side_effects=EFFECT),
    )(*[hbm(a) for a in srcs], *[hbm(a) for a in lands])
    return out[:-1], out[-1]


def split_wait(name, handles, n_src, waits, after):
    send, recv, *bufs = handles
    nb = len(bufs)

    def body(*refs):
        ins, lands = refs[:n_src], refs[n_src:nb]
        send_sem, recv_sem = refs[nb], refs[nb + 1]
        x, y, c, _chips = _place()
        sends, recvs = waits(ins, lands)
        for src, k in sends:
            pltpu.make_async_remote_copy(src_ref=src, dst_ref=src, send_sem=send_sem.at[k], recv_sem=recv_sem.at[0],
                                         device_id=(x, y, 1 - c), device_id_type=MESH).wait_send()
        for dst, k in recvs:
            pltpu.make_async_remote_copy(src_ref=dst, dst_ref=dst, send_sem=send_sem.at[0], recv_sem=recv_sem.at[k],
                                         device_id=(x, y, 1 - c), device_id_type=MESH).wait_recv()

    out = pl.pallas_call(
        body, name=name, out_shape=tuple(pltpu.HBM(a.shape, a.dtype) for a in bufs),
        in_specs=[HBM] * nb + [SEM, SEM, ANY], out_specs=tuple([HBM] * nb),
        input_output_aliases={k: k for k in range(nb)},
        compiler_params=pltpu.CompilerParams(has_side_effects=EFFECT),
    )(*bufs, send, recv, after)
    return list(out[:n_src]), list(out[n_src:])


def gather_plan(n):
    def plan(ins, lands):
        x, y, c, chips = _place()
        me = 2 * x + y
        copies = []
        for w in range(n):
            for j, (cx, cy) in enumerate(chips):
                for t in range(2):
                    copies.append((ins[w].at[c], lands[w].at[c, me], 8 * w + 2 * j + t, 8 * w + 2 * j + c, (cx, cy, t)))
            for half in range(2):
                copies.append((ins[w].at[half], lands[w].at[half, me], 8 * w + 6 + half, 8 * w + 6 + half, (x, y, 1 - c)))
        return copies

    def waits(ins, lands):
        x, y, c, chips = _place()
        me = 2 * x + y
        sends, recvs = [], []
        for w in range(n):
            for j, (cx, cy) in enumerate(chips):
                for t in range(2):
                    sends.append((ins[w].at[c], 8 * w + 2 * j + t))
                    recvs.append((lands[w].at[t, 2 * cx + cy], 8 * w + 2 * j + t))
            for half in range(2):
                sends.append((ins[w].at[half], 8 * w + 6 + half))
                recvs.append((lands[w].at[half, me], 8 * w + 6 + half))
        return sends, recvs

    return plan, waits


def scatter_plan(n):
    def plan(ins, lands):
        x, y, c, chips = _place()
        return [(ins[w].at[:, 2 * cx + cy], lands[w].at[j], 3 * w + j, 3 * w + j, (cx, cy, c))
                for w in range(n) for j, (cx, cy) in enumerate(chips)]

    def waits(ins, lands):
        x, y, c, chips = _place()
        sends = [(ins[w].at[:, 2 * cx + cy], 3 * w + j) for w in range(n) for j, (cx, cy) in enumerate(chips)]
        recvs = [(lands[w].at[j], 3 * w + j) for w in range(n) for j in range(3)]
        return sends, recvs

    return plan, waits


def _layer_shards(w, l):
    return [w["ffn_gate"][l].astype(MMT), w["ffn_up"][l].astype(MMT), w["ffn_down"][l].astype(MMT),
            w["w_in"][l].reshape(2, D_MODEL // 2, -1).astype(MMT),
            w["branch_proj"][l].reshape(2, 3 * BW // 2, -1).astype(MMT),
            w["w_out"][l].reshape(2, -1, D_MODEL).astype(MMT),
            w["s5_glu_w"][l].reshape(2, -1, BW).astype(MMT)]


def _layer_weights(g):
    rows = lambda a: a.transpose(1, 0, 2, 3).reshape(NSH, -1, a.shape[-1])
    p = rows(g[4]).reshape(NSH, 3, BW, -1).transpose(1, 2, 0, 3).reshape(3, BW, D_MODEL)
    return dict(wg=g[0], wu=g[1], wd=g[2], win=rows(g[3]), pfull=p,
                woutfull=rows(g[5]).reshape(D_MODEL, D_MODEL), gluw=rows(g[6]).reshape(BW, BW))


GROUPS = {"ffn1": ("ffn_gate", "ffn_up", "ffn_down"), "merge": ("branch_proj", "w_out"), "mid": ("s5_glu_w",),
          "pre": ("w_in",), "ffn0": ("ffn_gate", "ffn_up", "ffn_down")}


def _grad_views(big, l, group):
    views = []
    for name in GROUPS[group]:
        if name == "branch_proj":
            dq = D_MODEL // NSH
            a = big[(name, l)].reshape(3, BW, NSH, dq).transpose(2, 0, 1, 3).reshape(1, NSH, 3 * BW, dq)
        elif name.startswith("ffn"):
            a = big[(name, l, 1 if group == "ffn1" else 0)]
        else:
            a = big[(name, l)]
            a = a.reshape(1, NSH, -1, a.shape[-1])
        views.append((name, a, 0))
    return views


def halves_plan(n):
    def src(ref, c):
        h = ref.shape[2] // 2
        return ref.at[:, :, pl.ds((1 - c) * h, h)]

    def plan(ins, lands):
        x, y, c, _chips = _place()
        return [(src(ins[w], c), lands[w], w, w, (x, y, 1 - c)) for w in range(n)]

    def waits(ins, lands):
        x, y, c, _chips = _place()
        return [(src(ins[w], c), w) for w in range(n)], [(lands[w], w) for w in range(n)]

    return plan, waits


def _reduce_to_halves(tag, views, c, wire):
    from_sibling = exchange_halves(f"reduce_cores_{tag}", [a for _, a, _ in views], [(p0, 1) for _, _, p0 in views])
    merge = lambda a: a.reshape((-1,) + a.shape[2:])
    return [add_own_half(f"sum_cores_{tag}_{i}", merge(a), merge(r), c, wire[i], NSH * p0).reshape(r.shape)
            for i, ((_, a, p0), r) in enumerate(zip(views, from_sibling))]


def _step(x, target, w, m, v):
    mx, my, mc = lax.axis_index("x"), lax.axis_index("y"), lax.axis_index("c")
    me = (2 * mx + my).astype(jnp.int32)
    mc = mc.astype(jnp.int32)

    W = dict(L=[None] * DEPTH)
    state = {"pending": []}
    n_big = len(BIG)
    g_plan, g_waits = gather_plan(n_big)

    def layer_weights(l, h):
        if l == 0:
            got = gather_shards("gather_weights_0", _layer_shards(w, 0) + [w[n] for n in SHARDED_SMALL])
            nxt = _layer_shards(w, 1)
            got, nxt = lax.optimization_barrier((got, nxt))
            shapes = [jax.ShapeDtypeStruct((2, NSH) + a.shape[1:], a.dtype) for a in nxt]
            state["gather"], token = split_start("gather_weights_1_start", nxt, shapes, g_plan, 8 * n_big, 8 * n_big)
            W["nw"] = got[n_big].transpose(0, 2, 1, 3).reshape(DEPTH, 3, 1, D_MODEL) + token[0, 0]
            W["convw"] = got[n_big + 1].transpose(0, 2, 1, 3).reshape(DEPTH, CONV_W, BW)
            return _layer_weights(got[:n_big]), h
        return _layer_weights(split_wait("gather_weights_1_wait", state["gather"], n_big, g_waits, h)[1]), h

    def to_chips(after):
        if "cores" not in state:
            return
        tag, names, l, group, handles, waits = state.pop("cores")
        sent, landed = split_wait(f"reduce_cores_{tag}_wait", handles, len(names), waits, after)
        merge = lambda a: a.reshape((-1,) + a.shape[2:])
        halves = [add_own_half(f"sum_cores_{tag}_{i}", merge(a), merge(r), mc, jnp.bfloat16, 0).reshape(r.shape)
                  for i, (a, r) in enumerate(zip(sent, landed))]
        shapes = [jax.ShapeDtypeStruct((3, a.shape[0]) + a.shape[2:], a.dtype) for a in halves]
        plan, waits = scatter_plan(len(halves))
        handles, token = split_start(f"reduce_chips_{tag}_start", halves, shapes, plan, 3 * len(halves), 3 * len(halves))
        W["nw"] = W["nw"] + token[0, 0]
        state["pending"].append((tag, names, l, group, handles, waits))

    def layer_grads(l, group, big):
        views = _grad_views(big, l, group)
        to_chips(views[0][1])
        if (l, group) == (0, "ffn0"):
            return
        tag = f"{l}_{group}"
        if (l, group) == (0, "pre"):
            halves = _reduce_to_halves(tag, views, mc, [jnp.bfloat16] * len(views))
            shapes = [jax.ShapeDtypeStruct((3, a.shape[0]) + a.shape[2:], a.dtype) for a in halves]
            plan, waits = scatter_plan(len(halves))
            handles, token = split_start(f"reduce_chips_{tag}_start", halves, shapes, plan, 3 * len(halves), 3 * len(halves))
            W["nw"] = W["nw"] + token[0, 0]
            state["pending"].append((tag, [name for name, _, _ in views], l, group, handles, waits))
            return
        arrays = [a for _, a, _ in views]
        shapes = [jax.ShapeDtypeStruct((1, NSH, a.shape[2] // 2, a.shape[3]), a.dtype) for a in arrays]
        plan, waits = halves_plan(len(arrays))
        handles, token = split_start(f"reduce_cores_{tag}_start", arrays, shapes, plan, len(arrays), len(arrays))
        W["nw"] = W["nw"] + token[0, 0]
        state["cores"] = (tag, [name for name, _, _ in views], l, group, handles, waits)

    loss, dx, big, small = local_step(x[0], target[0], W, {k: w[k] for k in SMALL_RAW}, layer_weights, layer_grads)

    pieces = {n: {} for n in BIG}
    block_of = lambda name, l, group: (2 * l + (group == "ffn1")) if name.startswith("ffn") else l
    views = _grad_views(big, 0, "ffn0")
    small_packed = pack_small("pack_small_grads", [small[n] for n in SMALL], NSH * 32)
    halves = _reduce_to_halves("0_ffn0", views + [("small", small_packed.reshape(1, NSH, -1, LANE), 0)], mc,
                               [jnp.bfloat16] * len(views) + [F32])
    shapes = [jax.ShapeDtypeStruct((3, a.shape[0]) + a.shape[2:], a.dtype) for a in halves]
    plan, waits = scatter_plan(len(halves))
    last_handles, token = split_start("reduce_chips_0_ffn0_start", halves, shapes, plan, 3 * len(halves), 3 * len(halves))
    mc = mc + token[0, 0].astype(jnp.int32)
    for tag, names, l, group, handles, waits_k in state["pending"]:
        sent, landed = split_wait(f"reduce_chips_{tag}_wait", handles, len(names), waits_k, dx)
        for i, (name, h, r) in enumerate(zip(names, sent, landed)):
            pieces[name][block_of(name, l, group)] = add_chips(f"sum_chips_{tag}_{i}", h, r, me)

    g, delta, new_m, new_v = {}, {}, {}, {}

    def update(tag, names, extra):
        own = [jnp.concatenate([pieces[n][b] for b in sorted(pieces[n])], axis=0) for n in names] + extra
        other = share_halves(f"reduce_share_{tag}", own)
        for i, n in enumerate(names):
            view = lambda a: a.reshape(own[i].shape[0], -1, own[i].shape[2])
            res = adamw_halves(f"adamw_{n}", view(w[n]), view(m[n]), view(v[n]), own[i], other[i], mc)
            g[n], delta[n], new_m[n], new_v[n] = [a.reshape(w[n].shape) for a in res]
        return own, other

    early = [n for n in BIG if not n.startswith("ffn")]
    update("early", early, [])
    sent, landed = split_wait("reduce_chips_0_ffn0_wait", last_handles, len(halves), waits, new_v[early[0]])
    last = [add_chips(f"sum_chips_0_ffn0_{i}", h, r, me) for i, (h, r) in enumerate(zip(sent, landed))]
    for (name, _, _), piece in zip(views, last):
        pieces[name][block_of(name, 0, "ffn0")] = piece
    own, other = update("last", [n for n in BIG if n.startswith("ffn")], [last[-1]])

    piece = jnp.stack([jnp.where(mc == 0, own[-1][0], other[-1][0]), jnp.where(mc == 0, other[-1][0], own[-1][0])])
    (all_small,) = gather_shards("gather_small", [piece])
    full_small = unpack_small("unpack_small_grads", all_small.transpose(1, 0, 2, 3).reshape(-1, LANE),
                              [small[n].shape for n in SMALL])
    g.update(zip(SMALL, full_small))
    g["norm_w"] = lax.dynamic_slice_in_dim(g["norm_w"], me * (D_MODEL // NSH), D_MODEL // NSH, axis=2)
    g["rg_conv_w"] = lax.dynamic_slice_in_dim(g["rg_conv_w"], me * (BW // NSH), BW // NSH, axis=2)

    packed = [pack_small(f"pack_small_{tag}", [src[n] for n in SMALL], 8)
              for tag, src in (("w", w), ("g", g), ("m", m), ("v", v))]
    for tag, dst, flat in zip(("delta", "m", "v"), (delta, new_m, new_v), adamw(*packed)):
        dst.update(zip(SMALL, unpack_small(f"unpack_small_{tag}", flat, [w[n].shape for n in SMALL])))

    total = lax.psum(loss[0, 0], ("x", "y", "c"))
    return (total, dx[None], *[g[n] for n in WEIGHTS], *[delta[n] for n in WEIGHTS],
            *[new_m[n] for n in WEIGHTS], *[new_v[n] for n in WEIGHTS])


def kernel(x, norm_w, final_norm_w, ffn_gate, ffn_up, ffn_down, w_in, branch_proj, w_out, s5_lambda_re, s5_lambda_im, s5_log_dt, s5_b_re, s5_b_im, s5_c_re, s5_c_im, s5_d, s5_glu_w, s5_glu_b, hg_lb_logits, hg_norm_w, rg_conv_w, rg_conv_b, rg_wa, rg_ba, rg_wx, rg_bx, rg_lambda, loss_target, m_norm_w, m_final_norm_w, m_ffn_gate, m_ffn_up, m_ffn_down, m_w_in, m_branch_proj, m_w_out, m_s5_lambda_re, m_s5_lambda_im, m_s5_log_dt, m_s5_b_re, m_s5_b_im, m_s5_c_re, m_s5_c_im, m_s5_d, m_s5_glu_w, m_s5_glu_b, m_hg_lb_logits, m_hg_norm_w, m_rg_conv_w, m_rg_conv_b, m_rg_wa, m_rg_ba, m_rg_wx, m_rg_bx, m_rg_lambda, v_norm_w, v_final_norm_w, v_ffn_gate, v_ffn_up, v_ffn_down, v_w_in, v_branch_proj, v_w_out, v_s5_lambda_re, v_s5_lambda_im, v_s5_log_dt, v_s5_b_re, v_s5_b_im, v_s5_c_re, v_s5_c_im, v_s5_d, v_s5_glu_w, v_s5_glu_b, v_hg_lb_logits, v_hg_norm_w, v_rg_conv_w, v_rg_conv_b, v_rg_wa, v_rg_ba, v_rg_wx, v_rg_bx, v_rg_lambda):
    ws = (norm_w, final_norm_w, ffn_gate, ffn_up, ffn_down, w_in, branch_proj, w_out, s5_lambda_re, s5_lambda_im, s5_log_dt, s5_b_re, s5_b_im, s5_c_re, s5_c_im, s5_d, s5_glu_w, s5_glu_b, hg_lb_logits, hg_norm_w, rg_conv_w, rg_conv_b, rg_wa, rg_ba, rg_wx, rg_bx, rg_lambda)
    ms = (m_norm_w, m_final_norm_w, m_ffn_gate, m_ffn_up, m_ffn_down, m_w_in, m_branch_proj, m_w_out, m_s5_lambda_re, m_s5_lambda_im, m_s5_log_dt, m_s5_b_re, m_s5_b_im, m_s5_c_re, m_s5_c_im, m_s5_d, m_s5_glu_w, m_s5_glu_b, m_hg_lb_logits, m_hg_norm_w, m_rg_conv_w, m_rg_conv_b, m_rg_wa, m_rg_ba, m_rg_wx, m_rg_bx, m_rg_lambda)
    vs = (v_norm_w, v_final_norm_w, v_ffn_gate, v_ffn_up, v_ffn_down, v_w_in, v_branch_proj, v_w_out, v_s5_lambda_re, v_s5_lambda_im, v_s5_log_dt, v_s5_b_re, v_s5_b_im, v_s5_c_re, v_s5_c_im, v_s5_d, v_s5_glu_w, v_s5_glu_b, v_hg_lb_logits, v_hg_norm_w, v_rg_conv_w, v_rg_conv_b, v_rg_wa, v_rg_ba, v_rg_wx, v_rg_bx, v_rg_lambda)
    return _step(x, loss_target, dict(zip(WEIGHTS, ws)), dict(zip(WEIGHTS, ms)), dict(zip(WEIGHTS, vs)))
```
